```python
import math
import jax, jax.numpy as jnp
from jax import lax
import numpy as np

D_MODEL = 1024
BATCH = 8
SEQ = 8192
DEPTH = 2

N_A_LAYERS = DEPTH // 2
N_B_LAYERS = DEPTH - N_A_LAYERS

CONV_WIDTH = D_MODEL
CONV_K = 3

HEAD_DIM = 64
N_Q_HEADS = D_MODEL // HEAD_DIM
N_KV_HEADS = max(1, N_Q_HEADS // 8)
GROUP = N_Q_HEADS // N_KV_HEADS
ATTN_WIDTH = N_Q_HEADS * HEAD_DIM
KV_WIDTH = N_KV_HEADS * HEAD_DIM
WINDOW = 128
BLOCK = 128

N_BUCKETS = 32
MAX_DISTANCE = 128

EPS = 1e-6
NEG_INF = -1e30

kernel_name = "yoco_shortconv_swa_sink_hybrid"


def rmsnorm(x, g):
    xf = x.astype(jnp.float32)
    y = xf * lax.rsqrt(jnp.mean(xf * xf, axis=-1, keepdims=True) + EPS) * g.astype(jnp.float32)
    return y.astype(x.dtype)


def t5_causal_bucket(dist):
    max_exact = N_BUCKETS // 2
    is_small = dist < max_exact
    d = jnp.maximum(dist, 1).astype(jnp.float32)
    large = max_exact + (jnp.log(d / max_exact) / math.log(MAX_DISTANCE / max_exact)
                         * (N_BUCKETS - max_exact)).astype(jnp.int32)
    large = jnp.minimum(large, N_BUCKETS - 1)
    return jnp.where(is_small, dist, large)


def short_conv_mixer(h, w_in, conv_w, w_out):
    proj = h @ w_in
    b_gate, c_gate, u, z = jnp.split(proj, 4, axis=-1)
    v = c_gate * u
    conv = lax.conv_general_dilated(
        v, conv_w[:, None, :].astype(v.dtype),
        window_strides=(1,), padding=[(CONV_K - 1, 0)],
        dimension_numbers=("NWC", "WIO", "NWC"),
        feature_group_count=CONV_WIDTH)
    y = b_gate * conv * jax.nn.silu(z)
    return y @ w_out


def shared_kv(h, kv_norm, w_kv):
    bsz, seq, _ = h.shape
    nb = seq // BLOCK
    kv = rmsnorm(h, kv_norm) @ w_kv
    k, v = jnp.split(kv, 2, axis=-1)
    k = k.reshape(bsz, nb, BLOCK, N_KV_HEADS, HEAD_DIM)
    v = v.reshape(bsz, nb, BLOCK, N_KV_HEADS, HEAD_DIM)

    def band(t):
        prev = jnp.concatenate([jnp.zeros_like(t[:, :1]), t[:, :-1]], axis=1)
        return jnp.concatenate([prev, t], axis=2)

    return band(k), band(v)


def banded_bias_and_mask(nb, rel_bias):
    q_loc = jnp.arange(BLOCK, dtype=jnp.int32)[:, None]
    s_loc = jnp.arange(2 * BLOCK, dtype=jnp.int32)[None, :]
    dist = q_loc + BLOCK - s_loc
    in_window = (dist >= 0) & (dist < WINDOW)
    bucket = t5_causal_bucket(jnp.maximum(dist, 0))
    bias = rel_bias.astype(jnp.float32)[bucket]
    bias = jnp.transpose(bias, (2, 0, 1)).reshape(N_KV_HEADS, GROUP, BLOCK, 2 * BLOCK)
    blk = jnp.arange(nb, dtype=jnp.int32)[:, None, None]
    exists = (blk > 0) | (s_loc >= BLOCK)[None]
    mask = in_window[None] & exists
    return bias, mask


def swa_sink_attention(q, keys, vals, sinks, bias, mask):
    bsz, seq, _ = q.shape
    nb = seq // BLOCK
    qb = q.reshape(bsz, nb, BLOCK, N_KV_HEADS, GROUP, HEAD_DIM)
    scores = jnp.einsum("bnqkgd,bnskd->bnkgqs", qb, keys).astype(jnp.float32)
    logits = scores * (HEAD_DIM ** -0.5) + bias[None, None]
    logits = jnp.where(mask[None, :, None, None], logits, NEG_INF)
    sink = sinks.astype(jnp.float32).reshape(1, 1, N_KV_HEADS, GROUP, 1, 1)
    m = jnp.maximum(jnp.max(logits, axis=-1, keepdims=True), sink)
    p = jnp.exp(logits - m)
    p = p / (jnp.sum(p, axis=-1, keepdims=True) + jnp.exp(sink - m))
    out = jnp.einsum("bnkgqs,bnskd->bnqkgd", p.astype(vals.dtype), vals)
    return out.reshape(bsz, seq, ATTN_WIDTH)


def _fwd_setup_inputs(seed: int = 0) -> dict:
    key = jax.random.key(seed)
    ks = jax.random.split(key, 16)
    f32 = jnp.float32
    nrm = lambda k, shape, s: jax.random.normal(k, shape, f32) * s
    return {
        "x": nrm(ks[0], (BATCH, SEQ, D_MODEL), 1.0),
        "a_pre_norm": 1.0 + nrm(ks[1], (N_A_LAYERS, D_MODEL), 0.05),
        "a_w_in": nrm(ks[2], (N_A_LAYERS, D_MODEL, 4 * CONV_WIDTH), D_MODEL ** -0.5),
        "a_conv_w": nrm(ks[3], (N_A_LAYERS, CONV_K, CONV_WIDTH), CONV_K ** -0.5),
        "a_w_out": nrm(ks[4], (N_A_LAYERS, CONV_WIDTH, D_MODEL), CONV_WIDTH ** -0.5),
        "a_post_norm": 1.0 + nrm(ks[5], (N_A_LAYERS, D_MODEL), 0.05),
        "kv_norm": 1.0 + nrm(ks[6], (D_MODEL,), 0.05),
        "w_kv": nrm(ks[7], (D_MODEL, 2 * KV_WIDTH), D_MODEL ** -0.5),
        "rel_bias": nrm(ks[8], (N_BUCKETS, N_Q_HEADS), 0.1),
        "b_pre_norm": 1.0 + nrm(ks[9], (N_B_LAYERS, D_MODEL), 0.05),
        "b_w_in": nrm(ks[10], (N_B_LAYERS, D_MODEL, 2 * ATTN_WIDTH), D_MODEL ** -0.5),
        "b_sinks": nrm(ks[11], (N_B_LAYERS, N_Q_HEADS), 0.5),
        "b_w_out": nrm(ks[12], (N_B_LAYERS, ATTN_WIDTH, D_MODEL), ATTN_WIDTH ** -0.5),
        "b_post_norm": 1.0 + nrm(ks[13], (N_B_LAYERS, D_MODEL), 0.05),
    }


def _fwd_reference(x, a_pre_norm, a_w_in, a_conv_w, a_w_out, a_post_norm,
              kv_norm, w_kv, rel_bias,
              b_pre_norm, b_w_in, b_sinks, b_w_out, b_post_norm):
    h = x
    nb = x.shape[1] // BLOCK
    bias, mask = banded_bias_and_mask(nb, rel_bias)
    keys = vals = None
    for layer in range(DEPTH):
        if layer < N_A_LAYERS:
            i = layer
            y = short_conv_mixer(rmsnorm(h, a_pre_norm[i]), a_w_in[i], a_conv_w[i], a_w_out[i])
            h = h + rmsnorm(y, a_post_norm[i])
            if layer == N_A_LAYERS - 1:
                keys, vals = shared_kv(h, kv_norm, w_kv)
        else:
            j = layer - N_A_LAYERS
            qz = rmsnorm(h, b_pre_norm[j]) @ b_w_in[j]
            q, z = jnp.split(qz, 2, axis=-1)
            o = swa_sink_attention(q, keys, vals, b_sinks[j], bias, mask) * jax.nn.silu(z)
            y = o @ b_w_out[j]
            h = h + rmsnorm(y, b_post_norm[j])
    return h


import jax as _jax
import jax.numpy as _jnp

TWIN_FORMAT = 'train_step'
FWD_PARAMS = ['x', 'a_pre_norm', 'a_w_in', 'a_conv_w', 'a_w_out', 'a_post_norm', 'kv_norm', 'w_kv', 'rel_bias', 'b_pre_norm', 'b_w_in', 'b_sinks', 'b_w_out', 'b_post_norm']
TWIN_WEIGHTS = ['a_pre_norm', 'a_w_in', 'a_conv_w', 'a_w_out', 'a_post_norm', 'kv_norm', 'w_kv', 'rel_bias', 'b_pre_norm', 'b_w_in', 'b_sinks', 'b_w_out', 'b_post_norm']
TWIN_DIFF_INPUT = 'x'
TWIN_INPUTS = ['x', 'a_pre_norm', 'a_w_in', 'a_conv_w', 'a_w_out', 'a_post_norm', 'kv_norm', 'w_kv', 'rel_bias', 'b_pre_norm', 'b_w_in', 'b_sinks', 'b_w_out', 'b_post_norm', 'loss_target', 'm_a_pre_norm', 'm_a_w_in', 'm_a_conv_w', 'm_a_w_out', 'm_a_post_norm', 'm_kv_norm', 'm_w_kv', 'm_rel_bias', 'm_b_pre_norm', 'm_b_w_in', 'm_b_sinks', 'm_b_w_out', 'm_b_post_norm', 'v_a_pre_norm', 'v_a_w_in', 'v_a_conv_w', 'v_a_w_out', 'v_a_post_norm', 'v_kv_norm', 'v_w_kv', 'v_rel_bias', 'v_b_pre_norm', 'v_b_w_in', 'v_b_sinks', 'v_b_w_out', 'v_b_post_norm']
TWIN_OUTPUTS = ['loss', 'grad_x', 'grad_a_pre_norm', 'grad_a_w_in', 'grad_a_conv_w', 'grad_a_w_out', 'grad_a_post_norm', 'grad_kv_norm', 'grad_w_kv', 'grad_rel_bias', 'grad_b_pre_norm', 'grad_b_w_in', 'grad_b_sinks', 'grad_b_w_out', 'grad_b_post_norm', 'delta_a_pre_norm', 'delta_a_w_in', 'delta_a_conv_w', 'delta_a_w_out', 'delta_a_post_norm', 'delta_kv_norm', 'delta_w_kv', 'delta_rel_bias', 'delta_b_pre_norm', 'delta_b_w_in', 'delta_b_sinks', 'delta_b_w_out', 'delta_b_post_norm', 'new_m_a_pre_norm', 'new_m_a_w_in', 'new_m_a_conv_w', 'new_m_a_w_out', 'new_m_a_post_norm', 'new_m_kv_norm', 'new_m_w_kv', 'new_m_rel_bias', 'new_m_b_pre_norm', 'new_m_b_w_in', 'new_m_b_sinks', 'new_m_b_w_out', 'new_m_b_post_norm', 'new_v_a_pre_norm', 'new_v_a_w_in', 'new_v_a_conv_w', 'new_v_a_w_out', 'new_v_a_post_norm', 'new_v_kv_norm', 'new_v_w_kv', 'new_v_rel_bias', 'new_v_b_pre_norm', 'new_v_b_w_in', 'new_v_b_sinks', 'new_v_b_w_out', 'new_v_b_post_norm']
TWIN_LEAF_KINDS = {'loss': 'loss', 'grad_x': 'grad_x', 'grad_a_pre_norm': 'grad_w', 'grad_a_w_in': 'grad_w', 'grad_a_conv_w': 'grad_w', 'grad_a_w_out': 'grad_w', 'grad_a_post_norm': 'grad_w', 'grad_kv_norm': 'grad_w', 'grad_w_kv': 'grad_w', 'grad_rel_bias': 'grad_w', 'grad_b_pre_norm': 'grad_w', 'grad_b_w_in': 'grad_w', 'grad_b_sinks': 'grad_w', 'grad_b_w_out': 'grad_w', 'grad_b_post_norm': 'grad_w', 'delta_a_pre_norm': 'delta_w', 'delta_a_w_in': 'delta_w', 'delta_a_conv_w': 'delta_w', 'delta_a_w_out': 'delta_w', 'delta_a_post_norm': 'delta_w', 'delta_kv_norm': 'delta_w', 'delta_w_kv': 'delta_w', 'delta_rel_bias': 'delta_w', 'delta_b_pre_norm': 'delta_w', 'delta_b_w_in': 'delta_w', 'delta_b_sinks': 'delta_w', 'delta_b_w_out': 'delta_w', 'delta_b_post_norm': 'delta_w', 'new_m_a_pre_norm': 'new_m', 'new_m_a_w_in': 'new_m', 'new_m_a_conv_w': 'new_m', 'new_m_a_w_out': 'new_m', 'new_m_a_post_norm': 'new_m', 'new_m_kv_norm': 'new_m', 'new_m_w_kv': 'new_m', 'new_m_rel_bias': 'new_m', 'new_m_b_pre_norm': 'new_m', 'new_m_b_w_in': 'new_m', 'new_m_b_sinks': 'new_m', 'new_m_b_w_out': 'new_m', 'new_m_b_post_norm': 'new_m', 'new_v_a_pre_norm': 'new_v', 'new_v_a_w_in': 'new_v', 'new_v_a_conv_w': 'new_v', 'new_v_a_w_out': 'new_v', 'new_v_a_post_norm': 'new_v', 'new_v_kv_norm': 'new_v', 'new_v_w_kv': 'new_v', 'new_v_rel_bias': 'new_v', 'new_v_b_pre_norm': 'new_v', 'new_v_b_w_in': 'new_v', 'new_v_b_sinks': 'new_v', 'new_v_b_w_out': 'new_v', 'new_v_b_post_norm': 'new_v'}


def _forward(args):
    return _fwd_reference(*[args[k] for k in FWD_PARAMS])


def _output_shape():
    def fwd():
        inp = _fwd_setup_inputs(0)
        return _fwd_reference(*[inp[k] for k in FWD_PARAMS])
    out = _jax.eval_shape(fwd)
    return out.shape, out.dtype

N_MICROBATCH = 1
ADAM_LR = 0.001
ADAM_B1 = 0.9
ADAM_B2 = 0.999
ADAM_EPS = 1e-08
ADAM_WD = 0.01
ADAM_STEP = 10
PER_EXAMPLE_BATCH_AXIS = {'x': 0, 'loss_target': 0}
SHARED_INPUTS = []
_WEIGHT_DTYPES = {'a_pre_norm': _jnp.float32, 'a_w_in': _jnp.float32, 'a_conv_w': _jnp.float32, 'a_w_out': _jnp.float32, 'a_post_norm': _jnp.float32, 'kv_norm': _jnp.float32, 'w_kv': _jnp.float32, 'rel_bias': _jnp.float32, 'b_pre_norm': _jnp.float32, 'b_w_in': _jnp.float32, 'b_sinks': _jnp.float32, 'b_w_out': _jnp.float32, 'b_post_norm': _jnp.float32}
MOMENT_SCALE = {'a_pre_norm': 1.513578e+00, 'a_w_in': 7.742440e-01, 'a_conv_w': 8.354737e-01, 'a_w_out': 7.900105e-01, 'a_post_norm': 6.407892e+01, 'kv_norm': 6.797358e-01, 'w_kv': 1.311703e+00, 'rel_bias': 5.672953e-01, 'b_pre_norm': 6.509593e-01, 'b_w_in': 4.578384e-01, 'b_sinks': 1.987962e-01, 'b_w_out': 4.827393e-01, 'b_post_norm': 6.406388e+01}


def _to_microbatches(a, axis):
    t = _jnp.moveaxis(a, axis, 0)
    t = t.reshape((N_MICROBATCH, t.shape[0] // N_MICROBATCH) + t.shape[1:])
    return _jnp.moveaxis(t, 1, axis + 1)


def setup_inputs(seed: int = 0) -> dict:
    inp = _fwd_setup_inputs(seed)
    key = _jax.random.fold_in(_jax.random.key(seed), 7919)
    shape, _ = _output_shape()
    out = dict(inp)
    out["loss_target"] = _jax.random.normal(_jax.random.fold_in(key, 0), shape, _jnp.float32)
    for i, name in enumerate(TWIN_WEIGHTS):
        w = inp[name].astype(_jnp.float32)
        if MOMENT_SCALE is None:
            s = _jnp.sqrt(_jnp.mean(_jnp.square(w)) + 1e-30)
        else:
            s = MOMENT_SCALE[name]
        km, kv = _jax.random.split(_jax.random.fold_in(key, i + 1))
        out[name] = w
        out["m_" + name] = s * _jax.random.normal(km, w.shape, _jnp.float32)
        out["v_" + name] = (s * s) * _jax.random.uniform(kv, w.shape, _jnp.float32, 0.5, 1.5)
    if N_MICROBATCH > 1:
        for name, axis in PER_EXAMPLE_BATCH_AXIS.items():
            out[name] = _to_microbatches(out[name], axis)
    return {'x': out['x'], 'a_pre_norm': out['a_pre_norm'], 'a_w_in': out['a_w_in'], 'a_conv_w': out['a_conv_w'], 'a_w_out': out['a_w_out'], 'a_post_norm': out['a_post_norm'], 'kv_norm': out['kv_norm'], 'w_kv': out['w_kv'], 'rel_bias': out['rel_bias'], 'b_pre_norm': out['b_pre_norm'], 'b_w_in': out['b_w_in'], 'b_sinks': out['b_sinks'], 'b_w_out': out['b_w_out'], 'b_post_norm': out['b_post_norm'], 'loss_target': out['loss_target'], 'm_a_pre_norm': out['m_a_pre_norm'], 'm_a_w_in': out['m_a_w_in'], 'm_a_conv_w': out['m_a_conv_w'], 'm_a_w_out': out['m_a_w_out'], 'm_a_post_norm': out['m_a_post_norm'], 'm_kv_norm': out['m_kv_norm'], 'm_w_kv': out['m_w_kv'], 'm_rel_bias': out['m_rel_bias'], 'm_b_pre_norm': out['m_b_pre_norm'], 'm_b_w_in': out['m_b_w_in'], 'm_b_sinks': out['m_b_sinks'], 'm_b_w_out': out['m_b_w_out'], 'm_b_post_norm': out['m_b_post_norm'], 'v_a_pre_norm': out['v_a_pre_norm'], 'v_a_w_in': out['v_a_w_in'], 'v_a_conv_w': out['v_a_conv_w'], 'v_a_w_out': out['v_a_w_out'], 'v_a_post_norm': out['v_a_post_norm'], 'v_kv_norm': out['v_kv_norm'], 'v_w_kv': out['v_w_kv'], 'v_rel_bias': out['v_rel_bias'], 'v_b_pre_norm': out['v_b_pre_norm'], 'v_b_w_in': out['v_b_w_in'], 'v_b_sinks': out['v_b_sinks'], 'v_b_w_out': out['v_b_w_out'], 'v_b_post_norm': out['v_b_post_norm']}


def _loss(weights, diff, rest, loss_target):
    with _jax.named_scope("forward"):
        args = {**rest, TWIN_DIFF_INPUT: diff, **{k: w.astype(_WEIGHT_DTYPES[k]) for k, w in weights.items()}}
        y = _forward(args)
    with _jax.named_scope("loss_head"):
        err = _jnp.square(y.astype(_jnp.float32) - loss_target)
        return 0.5 * _jnp.sum(_jnp.mean(err, axis=-1)) if err.ndim else 0.5 * err


def _adamw(w, g, m, v):
    m = ADAM_B1 * m + (1.0 - ADAM_B1) * g
    v = ADAM_B2 * v + (1.0 - ADAM_B2) * _jnp.square(g)
    m_hat = m / (1.0 - ADAM_B1 ** ADAM_STEP)
    v_hat = v / (1.0 - ADAM_B2 ** ADAM_STEP)
    delta = -ADAM_LR * (m_hat / (_jnp.sqrt(v_hat) + ADAM_EPS) + ADAM_WD * w)
    return delta, m, v


def reference(x, a_pre_norm, a_w_in, a_conv_w, a_w_out, a_post_norm, kv_norm, w_kv, rel_bias, b_pre_norm, b_w_in, b_sinks, b_w_out, b_post_norm, loss_target, m_a_pre_norm, m_a_w_in, m_a_conv_w, m_a_w_out, m_a_post_norm, m_kv_norm, m_w_kv, m_rel_bias, m_b_pre_norm, m_b_w_in, m_b_sinks, m_b_w_out, m_b_post_norm, v_a_pre_norm, v_a_w_in, v_a_conv_w, v_a_w_out, v_a_post_norm, v_kv_norm, v_w_kv, v_rel_bias, v_b_pre_norm, v_b_w_in, v_b_sinks, v_b_w_out, v_b_post_norm):
    given = dict(x=x, a_pre_norm=a_pre_norm, a_w_in=a_w_in, a_conv_w=a_conv_w, a_w_out=a_w_out, a_post_norm=a_post_norm, kv_norm=kv_norm, w_kv=w_kv, rel_bias=rel_bias, b_pre_norm=b_pre_norm, b_w_in=b_w_in, b_sinks=b_sinks, b_w_out=b_w_out, b_post_norm=b_post_norm, loss_target=loss_target, m_a_pre_norm=m_a_pre_norm, m_a_w_in=m_a_w_in, m_a_conv_w=m_a_conv_w, m_a_w_out=m_a_w_out, m_a_post_norm=m_a_post_norm, m_kv_norm=m_kv_norm, m_w_kv=m_w_kv, m_rel_bias=m_rel_bias, m_b_pre_norm=m_b_pre_norm, m_b_w_in=m_b_w_in, m_b_sinks=m_b_sinks, m_b_w_out=m_b_w_out, m_b_post_norm=m_b_post_norm, v_a_pre_norm=v_a_pre_norm, v_a_w_in=v_a_w_in, v_a_conv_w=v_a_conv_w, v_a_w_out=v_a_w_out, v_a_post_norm=v_a_post_norm, v_kv_norm=v_kv_norm, v_w_kv=v_w_kv, v_rel_bias=v_rel_bias, v_b_pre_norm=v_b_pre_norm, v_b_w_in=v_b_w_in, v_b_sinks=v_b_sinks, v_b_w_out=v_b_w_out, v_b_post_norm=v_b_post_norm)
    weights = {n: given[n] for n in TWIN_WEIGHTS}
    shared = {n: given[n] for n in SHARED_INPUTS}
    per_example = {n: given[n] for n in ['x']}
    grad_fn = _jax.value_and_grad(_loss, argnums=(0, 1))

    def one_microbatch(ex, loss_target):
        ex = dict(ex)
        diff = ex.pop(TWIN_DIFF_INPUT)
        return grad_fn(weights, diff, {**shared, **ex}, loss_target)

    if N_MICROBATCH == 1:
        loss, (grad_w, grad_x) = one_microbatch(per_example, given["loss_target"])
    else:
        def body(carry, xs):
            loss_sum, grad_sum = carry
            l_k, (gw_k, gx_k) = one_microbatch(xs[0], xs[1])
            with _jax.named_scope("update"):
                return (loss_sum + l_k, _jax.tree.map(_jnp.add, grad_sum, gw_k)), gx_k

        init = (_jnp.zeros((), _jnp.float32), _jax.tree.map(_jnp.zeros_like, weights))
        (loss, grad_w), grad_x = _jax.lax.scan(body, init, (per_example, given["loss_target"]))
    with _jax.named_scope("update"):
        delta_w, new_m, new_v = {}, {}, {}
        for n in TWIN_WEIGHTS:
            delta_w[n], new_m[n], new_v[n] = _adamw(weights[n], grad_w[n], given["m_" + n], given["v_" + n])
    return (loss, grad_x, *[grad_w[n] for n in TWIN_WEIGHTS], *[delta_w[n] for n in TWIN_WEIGHTS],
            *[new_m[n] for n in TWIN_WEIGHTS], *[new_v[n] for n in TWIN_WEIGHTS])
```

```python
import functools
import math

import jax
import jax.numpy as jnp
from jax import lax
from jax.experimental import pallas as pl
from jax.experimental.pallas import tpu as pltpu

HEAD_DIM = 64
N_Q_HEADS = 16
N_KV_HEADS = 2
GROUP = N_Q_HEADS // N_KV_HEADS
BLOCK = 128
N_BUCKETS = 32
MAX_DISTANCE = 128
EPS = 1e-6
NEG_INF = -1e30
SCALE = HEAD_DIM ** -0.5

ADAM_LR = 0.001
ADAM_B1 = 0.9
ADAM_B2 = 0.999
ADAM_EPS = 1e-08
ADAM_WD = 0.01
ADAM_STEP = 10

N_DEV = 8
LANES = 128
F32 = jnp.float32
BF16 = jnp.bfloat16
MESH = pl.DeviceIdType.MESH
MIB = 1024 * 1024


def _params(semantics=None, vmem_mib=48):
    return pltpu.CompilerParams(dimension_semantics=semantics, vmem_limit_bytes=vmem_mib * MIB)


def _full(shape):
    zeros = (0,) * len(shape)
    return pl.BlockSpec(shape, lambda *_: zeros)


def _rows(ts, cols):
    return pl.BlockSpec((ts, cols), lambda i: (i, 0))


def _dot(a, b):
    return jnp.dot(a, b, preferred_element_type=F32)


def _dot_nt(a, b):
    return lax.dot_general(a, b, (((1,), (1,)), ((), ())), preferred_element_type=F32)


def _dot_tn(a, b):
    return lax.dot_general(a, b, (((0,), (0,)), ((), ())), preferred_element_type=F32)


def _rms(xf):
    r = lax.rsqrt(jnp.mean(xf * xf, axis=-1, keepdims=True) + EPS)
    return xf * r, r


def _rms_bwd(dn, xn, r):
    return r * (dn - xn * jnp.mean(dn * xn, axis=-1, keepdims=True))


def _silu(z):
    s = jax.nn.sigmoid(z)
    return z * s, s * (1.0 + z * (1.0 - s))


def _my_index():
    return 4 * lax.axis_index("x") + 2 * lax.axis_index("y") + lax.axis_index("c")


def _all_gather(shards, out_dtypes):
    n = len(shards)

    def body(*refs):
        ins, outs = refs[:n], refs[n:2 * n]
        send_sems, recv_sems = refs[2 * n], refs[2 * n + 1]
        x, y, c = lax.axis_index("x"), lax.axis_index("y"), lax.axis_index("c")
        me, sibling = (x, y, c), (x, y, 1 - c)
        chips = [(1 - x, y), (x, 1 - y), (1 - x, 1 - y)]

        def copy(t, k, block, to):
            rows = outs[t].at[4 * block[0] + 2 * block[1] + block[2]]
            return pltpu.make_async_remote_copy(
                src_ref=rows, dst_ref=rows, send_sem=send_sems.at[t, k], recv_sem=recv_sems.at[t, k],
                device_id=to, device_id_type=MESH)

        for t in range(n):
            outs[t][pl.ds(_my_index(), 1)] = ins[t][...].astype(outs[t].dtype)[None]
        first = []
        for t in range(n):
            first.append(copy(t, 0, me, sibling))
            first += [copy(t, 1 + j, me, (*chip, c)) for j, chip in enumerate(chips)]
        for cp in first:
            cp.start()
        passed = []
        for j, chip in enumerate(chips):
            for t in range(n):
                copy(t, 1 + j, (*chip, c), me).wait_recv()
                fwd = copy(t, 4 + j, (*chip, c), sibling)
                fwd.start()
                passed.append(fwd)
        for t in range(n):
            copy(t, 0, sibling, me).wait_recv()
        for j, chip in enumerate(chips):
            for t in range(n):
                copy(t, 4 + j, (*chip, 1 - c), me).wait_recv()
        for cp in first + passed:
            cp.wait_send()

    vmem = pl.BlockSpec(memory_space=pltpu.VMEM)
    return pl.pallas_call(
        body,
        name="gather_weights",
        out_shape=[jax.ShapeDtypeStruct((N_DEV,) + s.shape, dt) for s, dt in zip(shards, out_dtypes)],
        in_specs=[vmem] * n,
        out_specs=[vmem] * n,
        scratch_shapes=[pltpu.SemaphoreType.DMA((n, 7)), pltpu.SemaphoreType.DMA((n, 7))],
        compiler_params=_params(vmem_mib=48),
    )(*shards)


def _reduce_exchange(parts, smalls):
    ns, ng = len(parts), len(smalls)
    nt = ns + ng

    def body(*refs):
        p_in, s_in = refs[:ns], refs[ns:nt]
        p_out, s_out = refs[nt:nt + ns], refs[nt + ns:2 * nt]
        p_recv, s_recv = refs[2 * nt:2 * nt + ns], refs[2 * nt + ns:3 * nt]
        send_sems, recv_sems = refs[3 * nt], refs[3 * nt + 1]
        x, y, c = lax.axis_index("x"), lax.axis_index("y"), lax.axis_index("c")
        me = 4 * x + 2 * y + c

        def peer_of(k):
            px = 1 - x if k & 4 else x
            py = 1 - y if k & 2 else y
            pc = 1 - c if k & 1 else c
            return (px, py, pc), 4 * px + 2 * py + pc

        def copy(t, k):
            peer, pidx = peer_of(k)
            if t < ns:
                src, dst = p_in[t].at[pidx], p_recv[t].at[me]
            else:
                src, dst = s_in[t - ns], s_recv[t - ns].at[me]
            return pltpu.make_async_remote_copy(
                src_ref=src, dst_ref=dst, send_sem=send_sems.at[t, k - 1], recv_sem=recv_sems.at[t, k - 1],
                device_id=peer, device_id_type=MESH)

        def landed(t, k):
            _, pidx = peer_of(k)
            buf = p_recv[t] if t < ns else s_recv[t - ns]
            return pltpu.make_async_remote_copy(
                src_ref=buf.at[pidx], dst_ref=buf.at[pidx], send_sem=send_sems.at[t, k - 1],
                recv_sem=recv_sems.at[t, k - 1], device_id=peer_of(k)[0], device_id_type=MESH)

        sent = [copy(t, k) for k in range(1, N_DEV) for t in range(nt)]
        for cp in sent:
            cp.start()
        for t in range(ns):
            p_recv[t][pl.ds(me, 1)] = p_in[t][pl.ds(me, 1)]
        for t in range(ng):
            s_recv[t][pl.ds(me, 1)] = s_in[t][...][None]
        for k in range(1, N_DEV):
            for t in range(nt):
                landed(t, k).wait_recv()
        for cp in sent:
            cp.wait_send()

        for t in range(ns):
            rows = p_out[t].shape[0]
            chunk = min(rows, 128)

            def add(i, carry, t=t, chunk=chunk):
                r0 = pl.multiple_of(i * chunk, chunk)
                acc = p_recv[t][0, pl.ds(r0, chunk), :].astype(F32)
                for d in range(1, N_DEV):
                    acc = acc + p_recv[t][d, pl.ds(r0, chunk), :].astype(F32)
                p_out[t][pl.ds(r0, chunk), :] = acc
                return carry

            lax.fori_loop(0, rows // chunk, add, 0)
        for t in range(ng):
            acc = s_recv[t][0]
            for d in range(1, N_DEV):
                acc = acc + s_recv[t][d]
            s_out[t][...] = acc

    vmem = pl.BlockSpec(memory_space=pltpu.VMEM)
    outs = pl.pallas_call(
        body,
        name="reduce_grads",
        out_shape=[jax.ShapeDtypeStruct(p.shape[1:], F32) for p in parts]
        + [jax.ShapeDtypeStruct(s.shape, F32) for s in smalls],
        in_specs=[vmem] * nt,
        out_specs=[vmem] * nt,
        scratch_shapes=[pltpu.VMEM(p.shape, p.dtype) for p in parts]
        + [pltpu.VMEM((N_DEV,) + s.shape, F32) for s in smalls]
        + [pltpu.SemaphoreType.DMA((nt, 7)), pltpu.SemaphoreType.DMA((nt, 7))],
        compiler_params=_params(vmem_mib=56),
    )(*parts, *smalls)
    return outs[:ns], outs[ns:]


def _layer_a_fwd(x2, sm, win_g, wout, ts):
    seq, d = x2.shape
    width = wout.shape[0]
    half = win_g.shape[2]
    n_half = width // half

    def body(x_ref, sm_ref, win_ref, wout_ref, h1_ref, n1_ref, proj_ref, conv_ref, y_ref, ya_ref, vprev_ref):
        @pl.when(pl.program_id(0) == 0)
        def _():
            vprev_ref[...] = jnp.zeros_like(vprev_ref)

        xf = x_ref[...]
        xn, _ = _rms(xf)
        n1 = (xn * sm_ref[0:1, :]).astype(BF16)
        n1_ref[...] = n1
        row = lax.broadcasted_iota(jnp.int32, (ts, half), 0)
        ya = jnp.zeros((ts, d), F32)
        for hh in range(n_half):
            cols = slice(hh * half, (hh + 1) * half)
            parts = []
            for part in range(4):
                j = part * n_half + hh
                pj = _dot(n1, win_ref[j])
                proj_ref[:, j * half:(j + 1) * half] = pj.astype(BF16)
                parts.append(pj)
            b, c, u, z = parts
            v = c * u
            last1, last2 = vprev_ref[7:8, cols], vprev_ref[6:7, cols]
            v1 = jnp.where(row == 0, last1, pltpu.roll(v, 1, 0))
            v2 = jnp.where(row == 0, last2, jnp.where(row == 1, last1, pltpu.roll(v, 2, 0)))
            vprev_ref[:, cols] = v[ts - 8:ts, :]
            conv = sm_ref[1:2, cols] * v2 + sm_ref[2:3, cols] * v1 + sm_ref[3:4, cols] * v
            conv_ref[:, cols] = conv.astype(BF16)
            yh = (b * conv * _silu(z)[0]).astype(BF16)
            y_ref[:, cols] = yh
            ya = ya + _dot(yh, wout_ref[cols, :])
        ya_ref[...] = ya
        h1_ref[...] = xf + _rms(ya)[0] * sm_ref[4:5, :]

    return pl.pallas_call(
        body,
        name="layer_a_fwd",
        grid=(seq // ts,),
        in_specs=[_rows(ts, d), _full(sm.shape), _full(win_g.shape), _full(wout.shape)],
        out_specs=[_rows(ts, d), _rows(ts, d), _rows(ts, 4 * width), _rows(ts, width), _rows(ts, width), _rows(ts, d)],
        out_shape=[
            jax.ShapeDtypeStruct((seq, d), F32),
            jax.ShapeDtypeStruct((seq, d), BF16),
            jax.ShapeDtypeStruct((seq, 4 * width), BF16),
            jax.ShapeDtypeStruct((seq, width), BF16),
            jax.ShapeDtypeStruct((seq, width), BF16),
            jax.ShapeDtypeStruct((seq, d), F32),
        ],
        scratch_shapes=[pltpu.VMEM((8, width), F32)],
        compiler_params=_params(("arbitrary",), 56),
    )(x2, sm, win_g, wout)


def _layer_b_in(h1, kvn, bpre, wkv, wbin_g, ts):
    seq, d = h1.shape
    kvw = wkv.shape[1]
    cw = wbin_g.shape[2]
    aw = N_Q_HEADS * HEAD_DIM
    per = aw // cw

    def body(h1_ref, kvn_ref, bpre_ref, wkv_ref, wbin_ref, n3_ref, n4_ref, kv_ref, q_ref, z2_ref):
        hn, _ = _rms(h1_ref[...])
        n3 = (hn * kvn_ref[...]).astype(BF16)
        n4 = (hn * bpre_ref[...]).astype(BF16)
        n3_ref[...] = n3
        n4_ref[...] = n4
        kv_ref[...] = _dot(n3, wkv_ref[...]).astype(BF16)
        for j in range(N_DEV):
            pj = _dot(n4, wbin_ref[j])
            if j < per:
                q_ref[:, j * cw:(j + 1) * cw] = pj.astype(BF16)
            else:
                z2_ref[:, (j - per) * cw:(j - per + 1) * cw] = pj

    return pl.pallas_call(
        body,
        name="layer_b_in",
        grid=(seq // ts,),
        in_specs=[_rows(ts, d), _full(kvn.shape), _full(bpre.shape), _full(wkv.shape), _full(wbin_g.shape)],
        out_specs=[_rows(ts, d), _rows(ts, d), _rows(ts, kvw), _rows(ts, aw), _rows(ts, aw)],
        out_shape=[
            jax.ShapeDtypeStruct((seq, d), BF16),
            jax.ShapeDtypeStruct((seq, d), BF16),
            jax.ShapeDtypeStruct((seq, kvw), BF16),
            jax.ShapeDtypeStruct((seq, aw), BF16),
            jax.ShapeDtypeStruct((seq, aw), F32),
        ],
        compiler_params=_params(("parallel",), 48),
    )(h1, kvn, bpre, wkv, wbin_g)


def _bias_table(rel_bias, bucket, in_window):
    def body(rb_ref, bucket_ref, win_ref, out_ref):
        bk = bucket_ref[...]
        inside = win_ref[...] != 0
        for h in range(N_Q_HEADS):
            acc = jnp.full(bk.shape, NEG_INF, F32)
            for b in range(N_BUCKETS):
                acc = jnp.where(jnp.logical_and(bk == b, inside), rb_ref[b, h], acc)
            out_ref[h] = acc

    vmem = pl.BlockSpec(memory_space=pltpu.VMEM)
    return pl.pallas_call(
        body,
        name="bias_table",
        in_specs=[pl.BlockSpec(memory_space=pltpu.SMEM), vmem, vmem],
        out_specs=vmem,
        out_shape=jax.ShapeDtypeStruct((N_Q_HEADS,) + bucket.shape, F32),
    )(rel_bias, bucket, in_window)


def _banded_kv(kvp_ref, kvc_ref):
    kvp = kvp_ref[...].astype(F32)
    kvc = kvc_ref[...].astype(F32)
    kw = N_KV_HEADS * HEAD_DIM
    kfull = jnp.concatenate([kvp[:, :kw], kvc[:, :kw]], axis=0)
    vfull = jnp.concatenate([kvp[:, kw:], kvc[:, kw:]], axis=0)
    lo = lax.broadcasted_iota(jnp.int32, kfull.shape, 1) < HEAD_DIM
    kroll = pltpu.roll(kfull, HEAD_DIM, 1)
    vroll = pltpu.roll(vfull, HEAD_DIM, 1)
    k2 = [jnp.where(lo, kfull, kroll).astype(BF16), jnp.where(lo, kroll, kfull).astype(BF16)]
    v2 = [jnp.where(lo, vfull, vroll).astype(BF16), jnp.where(lo, vroll, vfull).astype(BF16)]
    return k2, v2


def _probs(qq, k2, bias, kill, sink):
    logits = _dot_nt(qq, k2) * SCALE + bias
    logits = jnp.where(kill, NEG_INF, logits)
    mx = jnp.maximum(jnp.max(logits, axis=-1, keepdims=True), sink)
    p = jnp.exp(logits - mx)
    sink_p = jnp.exp(sink - mx)
    den = jnp.sum(p, axis=-1, keepdims=True) + sink_p
    return p / den, sink_p / den


def _head_pairs(ref, m, lo):
    both = ref[:, m * LANES:(m + 1) * LANES].astype(F32)
    zero = jnp.zeros_like(both)
    return jnp.where(lo, both, zero).astype(BF16), jnp.where(lo, zero, both).astype(BF16)


def _attn_fwd(q, kv, z2, biasm, sinks):
    seq, aw = q.shape
    kvw = kv.shape[1]
    nb = seq // BLOCK

    def body(sink_ref, q_ref, kvc_ref, kvp_ref, z2_ref, bias_ref, attn_ref, o_ref, acc_ref):
        i = pl.program_id(0)
        lo = lax.broadcasted_iota(jnp.int32, (BLOCK, LANES), 1) < HEAD_DIM
        kcol = lax.broadcasted_iota(jnp.int32, (BLOCK, 2 * BLOCK), 1)
        kill = jnp.logical_and(i == 0, kcol < BLOCK)
        k2, v2 = _banded_kv(kvp_ref, kvc_ref)
        for m in range(N_Q_HEADS // 2):
            kh = (2 * m) // GROUP
            outs = []
            for e, qq in enumerate(_head_pairs(q_ref, m, lo)):
                h = 2 * m + e
                pn, _ = _probs(qq, k2[kh], bias_ref[h], kill, sink_ref[0, h])
                outs.append(_dot(pn.astype(BF16), v2[kh]))
            acc_ref[:, m * LANES:(m + 1) * LANES] = jnp.where(lo, outs[0], outs[1])
        attn = acc_ref[...]
        attn_ref[...] = attn.astype(BF16)
        o_ref[...] = (attn * _silu(z2_ref[...])[0]).astype(BF16)

    blk = lambda w: pl.BlockSpec((BLOCK, w), lambda i: (i, 0))
    return pl.pallas_call(
        body,
        name="attn_fwd",
        grid=(nb,),
        in_specs=[
            pl.BlockSpec(memory_space=pltpu.SMEM),
            blk(aw),
            blk(kvw),
            pl.BlockSpec((BLOCK, kvw), lambda i: (jnp.maximum(i - 1, 0), 0)),
            blk(aw),
            _full(biasm.shape),
        ],
        out_specs=[blk(aw), blk(aw)],
        out_shape=[jax.ShapeDtypeStruct((seq, aw), BF16), jax.ShapeDtypeStruct((seq, aw), BF16)],
        scratch_shapes=[pltpu.VMEM((BLOCK, aw), F32)],
        compiler_params=_params(("parallel",), 32),
    )(sinks, q, kv, kv, z2, biasm)


def _layer_b_out(o, attn, z2, h1, target, wbout, bpost, ts):
    seq, d = h1.shape
    aw = o.shape[1]

    def body(o_ref, attn_ref, z2_ref, h1_ref, tgt_ref, w_ref, g_ref, dh2_ref, dyb_ref, dattn_ref, dz2_ref, acc_ref):
        @pl.when(pl.program_id(0) == 0)
        def _():
            acc_ref[...] = jnp.zeros_like(acc_ref)

        w = w_ref[...]
        yb = _dot(o_ref[...], w)
        ybn, r = _rms(yb)
        g = g_ref[...]
        diff = h1_ref[...] + ybn * g - tgt_ref[...]
        dh2 = diff * (1.0 / d)
        dh2_ref[...] = dh2
        acc_ref[0:1, :] += jnp.sum(dh2 * ybn, axis=0, keepdims=True)
        tok = jnp.mean(diff * diff, axis=-1, keepdims=True)
        acc_ref[1:2, :] += 0.5 * jnp.sum(tok, axis=0, keepdims=True)
        dyb = _rms_bwd(dh2 * g, ybn, r).astype(BF16)
        dyb_ref[...] = dyb
        do = _dot_nt(dyb, w)
        sz, dsz = _silu(z2_ref[...])
        dattn_ref[...] = (do * sz).astype(BF16)
        dz2_ref[...] = (do * attn_ref[...].astype(F32) * dsz).astype(BF16)

    return pl.pallas_call(
        body,
        name="layer_b_out",
        grid=(seq // ts,),
        in_specs=[_rows(ts, aw), _rows(ts, aw), _rows(ts, aw), _rows(ts, d), _rows(ts, d), _full(wbout.shape), _full(bpost.shape)],
        out_specs=[_rows(ts, d), _rows(ts, d), _rows(ts, aw), _rows(ts, aw), _full((8, d))],
        out_shape=[
            jax.ShapeDtypeStruct((seq, d), F32),
            jax.ShapeDtypeStruct((seq, d), BF16),
            jax.ShapeDtypeStruct((seq, aw), BF16),
            jax.ShapeDtypeStruct((seq, aw), BF16),
            jax.ShapeDtypeStruct((8, d), F32),
        ],
        compiler_params=_params(("arbitrary",), 48),
    )(o, attn, z2, h1, target, wbout, bpost)


def _attn_bwd(q, kv, dattn, biasm, sinks):
    seq, aw = q.shape
    kvw = kv.shape[1]
    kw = N_KV_HEADS * HEAD_DIM
    nb = seq // BLOCK

    def body(sink_ref, q_ref, kvc_ref, kvp_ref, da_ref, bias_ref, dq_ref, dkv_ref, dssum_ref, dsink_ref,
             carry_ref, acck_ref, accv_ref):
        i = pl.program_id(0)

        @pl.when(i == 0)
        def _():
            dssum_ref[...] = jnp.zeros_like(dssum_ref)
            dsink_ref[...] = jnp.zeros_like(dsink_ref)
            carry_ref[...] = jnp.zeros_like(carry_ref)

        @pl.when(i < nb)
        def _():
            lo = lax.broadcasted_iota(jnp.int32, (BLOCK, LANES), 1) < HEAD_DIM
            lo2 = lax.broadcasted_iota(jnp.int32, (2 * BLOCK, LANES), 1) < HEAD_DIM
            kcol = lax.broadcasted_iota(jnp.int32, (BLOCK, 2 * BLOCK), 1)
            head_lane = lax.broadcasted_iota(jnp.int32, (1, LANES), 1)
            kill = jnp.logical_and(i == 0, kcol < BLOCK)
            k2, v2 = _banded_kv(kvp_ref, kvc_ref)
            acck_ref[...] = jnp.zeros_like(acck_ref)
            accv_ref[...] = jnp.zeros_like(accv_ref)
            dsink = jnp.zeros((1, LANES), F32)
            for m in range(N_Q_HEADS // 2):
                kh = (2 * m) // GROUP
                dqs = []
                for e, (qq, dd) in enumerate(zip(_head_pairs(q_ref, m, lo), _head_pairs(da_ref, m, lo))):
                    h = 2 * m + e
                    pn, sink_p = _probs(qq, k2[kh], bias_ref[h], kill, sink_ref[0, h])
                    dp = _dot_nt(dd, v2[kh])
                    delta = jnp.sum(pn * dp, axis=-1, keepdims=True)
                    ds = pn * (dp - delta)
                    dssum_ref[h] += ds
                    dsink = dsink + jnp.where(head_lane == h, -jnp.sum(sink_p * delta, axis=0, keepdims=True), 0.0)
                    dsc = (ds * SCALE).astype(BF16)
                    dqs.append(_dot(dsc, k2[kh]))
                    acck_ref[kh] += _dot_tn(dsc, qq)
                    accv_ref[kh] += _dot_tn(pn.astype(BF16), dd)
                dq_ref[:, m * LANES:(m + 1) * LANES] = jnp.where(lo, dqs[0], dqs[1]).astype(BF16)
            dsink_ref[0:1, :] += dsink
            folded = []
            for acc_ref in (acck_ref, accv_ref):
                per_head = [acc_ref[kh] + pltpu.roll(acc_ref[kh], HEAD_DIM, 1) for kh in range(N_KV_HEADS)]
                folded.append(jnp.where(lo2, per_head[0], per_head[1]))
            dk, dv = folded

            @pl.when(i > 0)
            def _():
                dkv_ref[:, :kw] = (carry_ref[:, :kw] + dk[:BLOCK]).astype(BF16)
                dkv_ref[:, kw:] = (carry_ref[:, kw:] + dv[:BLOCK]).astype(BF16)

            carry_ref[:, :kw] = dk[BLOCK:]
            carry_ref[:, kw:] = dv[BLOCK:]

        @pl.when(i == nb)
        def _():
            dkv_ref[...] = carry_ref[...].astype(BF16)

    last = nb - 1
    blk = lambda w: pl.BlockSpec((BLOCK, w), lambda i: (jnp.minimum(i, last), 0))
    return pl.pallas_call(
        body,
        name="attn_bwd",
        grid=(nb + 1,),
        in_specs=[
            pl.BlockSpec(memory_space=pltpu.SMEM),
            blk(aw),
            blk(kvw),
            pl.BlockSpec((BLOCK, kvw), lambda i: (jnp.clip(i - 1, 0, last), 0)),
            blk(aw),
            _full(biasm.shape),
        ],
        out_specs=[
            blk(aw),
            pl.BlockSpec((BLOCK, kvw), lambda i: (jnp.maximum(i - 1, 0), 0)),
            _full(biasm.shape),
            _full((8, LANES)),
        ],
        out_shape=[
            jax.ShapeDtypeStruct((seq, aw), BF16),
            jax.ShapeDtypeStruct((seq, kvw), BF16),
            jax.ShapeDtypeStruct(biasm.shape, F32),
            jax.ShapeDtypeStruct((8, LANES), F32),
        ],
        scratch_shapes=[
            pltpu.VMEM((BLOCK, kvw), F32),
            pltpu.VMEM((N_KV_HEADS, 2 * BLOCK, LANES), F32),
            pltpu.VMEM((N_KV_HEADS, 2 * BLOCK, LANES), F32),
        ],
        compiler_params=_params(("arbitrary",), 40),
    )(sinks, q, kv, kv, dattn, biasm)


def _relbias_grad(dssum2, onehot, chunk):
    heads, n = dssum2.shape

    def body(a_ref, oh_ref, out_ref):
        @pl.when(pl.program_id(0) == 0)
        def _():
            out_ref[...] = jnp.zeros_like(out_ref)

        a = a_ref[...]
        hi = a.astype(BF16)
        lo = (a - hi.astype(F32)).astype(BF16)
        out_ref[...] += _dot(hi, oh_ref[...]) + _dot(lo, oh_ref[...])

    return pl.pallas_call(
        body,
        name="relbias_grad",
        grid=(n // chunk,),
        in_specs=[pl.BlockSpec((heads, chunk), lambda i: (0, i)), pl.BlockSpec((chunk, LANES), lambda i: (i, 0))],
        out_specs=_full((heads, LANES)),
        out_shape=jax.ShapeDtypeStruct((heads, LANES), F32),
        compiler_params=_params(("arbitrary",), 32),
    )(dssum2, onehot)


def _layer_b_in_bwd(dh2, dq, dz2, dkv, h1, ya, wbin_g, wkv, kvn, bpre, sm, ts):
    seq, d = h1.shape
    aw = dq.shape[1]
    kvw = dkv.shape[1]
    cw = wbin_g.shape[2]
    per = aw // cw

    def body(dh2_ref, dq_ref, dz2_ref, dkv_ref, h1_ref, ya_ref, wbin_ref, wkv_ref, kvn_ref, bpre_ref, sm_ref,
             dh1_ref, dya_ref, acc_ref):
        @pl.when(pl.program_id(0) == 0)
        def _():
            acc_ref[...] = jnp.zeros_like(acc_ref)

        dn4 = jnp.zeros((ts, d), F32)
        for j in range(N_DEV):
            src = dq_ref if j < per else dz2_ref
            jj = j % per
            dn4 = dn4 + _dot_nt(src[:, jj * cw:(jj + 1) * cw], wbin_ref[j])
        dn3 = _dot_nt(dkv_ref[...], wkv_ref[...])
        hn, r = _rms(h1_ref[...])
        acc_ref[0:1, :] += jnp.sum(dn4 * hn, axis=0, keepdims=True)
        acc_ref[1:2, :] += jnp.sum(dn3 * hn, axis=0, keepdims=True)
        dh1 = dh2_ref[...] + _rms_bwd(dn4 * bpre_ref[...] + dn3 * kvn_ref[...], hn, r)
        dh1_ref[...] = dh1
        yan, r2 = _rms(ya_ref[...])
        acc_ref[2:3, :] += jnp.sum(dh1 * yan, axis=0, keepdims=True)
        dya_ref[...] = _rms_bwd(dh1 * sm_ref[4:5, :], yan, r2).astype(BF16)

    return pl.pallas_call(
        body,
        name="layer_b_in_bwd",
        grid=(seq // ts,),
        in_specs=[_rows(ts, d), _rows(ts, aw), _rows(ts, aw), _rows(ts, kvw), _rows(ts, d), _rows(ts, d),
                  _full(wbin_g.shape), _full(wkv.shape), _full(kvn.shape), _full(bpre.shape), _full(sm.shape)],
        out_specs=[_rows(ts, d), _rows(ts, d), _full((8, d))],
        out_shape=[jax.ShapeDtypeStruct((seq, d), F32), jax.ShapeDtypeStruct((seq, d), BF16),
                   jax.ShapeDtypeStruct((8, d), F32)],
        compiler_params=_params(("arbitrary",), 48),
    )(dh2, dq, dz2, dkv, h1, ya, wbin_g, wkv, kvn, bpre, sm)


def _layer_a_bwd(dya, proj, conv, dh1, x2, wout, win_g, sm, ts):
    seq, d = x2.shape
    width = wout.shape[0]
    half = win_g.shape[2]
    n_half = width // half
    nt = seq // ts

    def body(dya_ref, proj_ref, conv_ref, dh1_ref, x_ref, wout_ref, win_ref, sm_ref, dproj_ref, gx_ref, acc_ref,
             dnext_ref):
        @pl.when(pl.program_id(0) == 0)
        def _():
            acc_ref[...] = jnp.zeros_like(acc_ref)
            dnext_ref[...] = jnp.zeros_like(dnext_ref)

        dy = _dot_nt(dya_ref[...], wout_ref[...])
        row = lax.broadcasted_iota(jnp.int32, (ts, half), 0)
        dn1 = jnp.zeros((ts, d), F32)
        for hh in range(n_half):
            cols = slice(hh * half, (hh + 1) * half)
            b, c, u, z = [proj_ref[:, (part * n_half + hh) * half:(part * n_half + hh + 1) * half].astype(F32)
                          for part in range(4)]
            cv = conv_ref[:, cols].astype(F32)
            dyh = dy[:, cols]
            sz, dsz = _silu(z)
            dconv = dyh * b * sz
            grads = [dyh * cv * sz, None, None, dyh * b * cv * dsz]
            next0, next1 = dnext_ref[0:1, cols], dnext_ref[1:2, cols]
            dc1 = jnp.where(row == ts - 1, next0, pltpu.roll(dconv, ts - 1, 0))
            dc2 = jnp.where(row == ts - 1, next1, jnp.where(row == ts - 2, next0, pltpu.roll(dconv, ts - 2, 0)))
            dnext_ref[:, cols] = dconv[0:8, :]
            v = c * u
            acc_ref[1:2, cols] += jnp.sum(dc2 * v, axis=0, keepdims=True)
            acc_ref[2:3, cols] += jnp.sum(dc1 * v, axis=0, keepdims=True)
            acc_ref[3:4, cols] += jnp.sum(dconv * v, axis=0, keepdims=True)
            dv = sm_ref[3:4, cols] * dconv + sm_ref[2:3, cols] * dc1 + sm_ref[1:2, cols] * dc2
            grads[1] = dv * u
            grads[2] = dv * c
            for part in range(4):
                j = part * n_half + hh
                gj = grads[part].astype(BF16)
                dproj_ref[:, j * half:(j + 1) * half] = gj
                dn1 = dn1 + _dot_nt(gj, win_ref[j])
        xn, r = _rms(x_ref[...])
        acc_ref[0:1, :] += jnp.sum(dn1 * xn, axis=0, keepdims=True)
        gx_ref[...] = dh1_ref[...] + _rms_bwd(dn1 * sm_ref[0:1, :], xn, r)

    rev = lambda w: pl.BlockSpec((ts, w), lambda i: (nt - 1 - i, 0))
    return pl.pallas_call(
        body,
        name="layer_a_bwd",
        grid=(nt,),
        in_specs=[rev(d), rev(4 * width), rev(width), rev(d), rev(d), _full(wout.shape), _full(win_g.shape), _full(sm.shape)],
        out_specs=[rev(4 * width), rev(d), _full((8, d))],
        out_shape=[jax.ShapeDtypeStruct((seq, 4 * width), BF16), jax.ShapeDtypeStruct((seq, d), F32),
                   jax.ShapeDtypeStruct((8, d), F32)],
        scratch_shapes=[pltpu.VMEM((8, width), F32)],
        compiler_params=_params(("arbitrary",), 56),
    )(dya, proj, conv, dh1, x2, wout, win_g, sm)


def _wgrad(a, bs, n_slots, ts, name):
    seq, k = a.shape
    nb_in = len(bs)
    n_each = bs[0].shape[1]
    n = nb_in * n_each
    bn = min(n_each, 1024)
    per_in = n_each // bn
    n_blocks = nb_in * per_in
    ns = seq // ts

    def b_spec(idx):
        def index(j, s):
            mine = j // per_in == idx
            row = jnp.where(mine, s, jnp.where(j // per_in > idx, ns - 1, 0))
            return (row, jnp.where(mine, j % per_in, jnp.where(j // per_in > idx, per_in - 1, 0)))
        return pl.BlockSpec((ts, bn), index)

    if n_slots:
        sw = n // n_slots
        spb = bn // sw
        out_shape = jax.ShapeDtypeStruct((n_slots, k, sw), BF16)
        out_spec = pl.BlockSpec((spb, k, sw), lambda j, s: (j, 0, 0))
    else:
        out_shape = jax.ShapeDtypeStruct((k, n), BF16)
        out_spec = pl.BlockSpec((k, bn), lambda j, s: (0, j))

    def body(a_ref, *refs):
        b_refs, o_ref, acc_ref = refs[:nb_in], refs[nb_in], refs[nb_in + 1]
        j, s = pl.program_id(0), pl.program_id(1)

        @pl.when(s == 0)
        def _():
            acc_ref[...] = jnp.zeros_like(acc_ref)

        for idx in range(nb_in):
            @pl.when(j // per_in == idx)
            def _(idx=idx):
                acc_ref[...] += _dot_tn(a_ref[...], b_refs[idx][...])

        @pl.when(s == ns - 1)
        def _():
            if n_slots:
                for e in range(spb):
                    o_ref[e] = acc_ref[:, e * sw:(e + 1) * sw].astype(BF16)
            else:
                o_ref[...] = acc_ref[...].astype(BF16)

    return pl.pallas_call(
        body,
        name=name,
        grid=(n_blocks, ns),
        in_specs=[pl.BlockSpec((ts, k), lambda j, s: (s, 0))] + [b_spec(idx) for idx in range(nb_in)],
        out_specs=out_spec,
        out_shape=out_shape,
        scratch_shapes=[pltpu.VMEM((k, bn), F32)],
        compiler_params=_params(("arbitrary", "arbitrary"), 48),
    )(a, *bs)


def _adamw(ws, gs, ms, vs):
    n = len(ws)

    def step(w, g, m, v):
        m = ADAM_B1 * m + (1.0 - ADAM_B1) * g
        v = ADAM_B2 * v + (1.0 - ADAM_B2) * jnp.square(g)
        m_hat = m / (1.0 - ADAM_B1 ** ADAM_STEP)
        v_hat = v / (1.0 - ADAM_B2 ** ADAM_STEP)
        return -ADAM_LR * (m_hat / (jnp.sqrt(v_hat) + ADAM_EPS) + ADAM_WD * w), m, v

    def body(*refs):
        w_refs, g_refs, m_refs, v_refs = (refs[k * n:(k + 1) * n] for k in range(4))
        d_refs, nm_refs, nv_refs = (refs[(4 + k) * n:(5 + k) * n] for k in range(3))
        for t in range(n):
            rows = w_refs[t].shape[0]
            if rows <= 128:
                d_refs[t][...], nm_refs[t][...], nv_refs[t][...] = step(
                    w_refs[t][...], g_refs[t][...], m_refs[t][...], v_refs[t][...])
                continue
            chunk = 128

            def one(i, carry, t=t):
                r = pl.ds(pl.multiple_of(i * chunk, chunk), chunk)
                d_refs[t][r, :], nm_refs[t][r, :], nv_refs[t][r, :] = step(
                    w_refs[t][r, :], g_refs[t][r, :], m_refs[t][r, :], v_refs[t][r, :])
                return carry

            lax.fori_loop(0, rows // chunk, one, 0)

    vmem = pl.BlockSpec(memory_space=pltpu.VMEM)
    outs = pl.pallas_call(
        body,
        name="adamw",
        in_specs=[vmem] * (4 * n),
        out_specs=[vmem] * (3 * n),
        out_shape=[jax.ShapeDtypeStruct(w.shape, F32) for w in ws] * 3,
        compiler_params=_params(vmem_mib=56),
    )(*ws, *gs, *ms, *vs)
    return outs[:n], outs[n:2 * n], outs[2 * n:]


def _band_structure():
    q_loc = jnp.arange(BLOCK, dtype=jnp.int32)[:, None]
    s_loc = jnp.arange(2 * BLOCK, dtype=jnp.int32)[None, :]
    dist = q_loc + BLOCK - s_loc
    in_window = (dist >= 0) & (dist < BLOCK)
    dd = jnp.maximum(dist, 0)
    max_exact = N_BUCKETS // 2
    large = max_exact + (jnp.log(jnp.maximum(dd, 1).astype(F32) / max_exact) / math.log(MAX_DISTANCE / max_exact)
                         * (N_BUCKETS - max_exact)).astype(jnp.int32)
    bucket = jnp.where(dd < max_exact, dd, jnp.minimum(large, N_BUCKETS - 1))
    onehot = (bucket.reshape(-1, 1) == jnp.arange(LANES, dtype=jnp.int32)[None, :]).astype(BF16)
    return bucket, in_window.astype(jnp.int32), onehot


def _place_rows(a, row, rows=8):
    return jnp.pad(a, ((row, rows - row - a.shape[0]), (0, 0)))


def kernel(x, a_pre_norm, a_w_in, a_conv_w, a_w_out, a_post_norm, kv_norm, w_kv, rel_bias, b_pre_norm, b_w_in, b_sinks, b_w_out, b_post_norm, loss_target, m_a_pre_norm, m_a_w_in, m_a_conv_w, m_a_w_out, m_a_post_norm, m_kv_norm, m_w_kv, m_rel_bias, m_b_pre_norm, m_b_w_in, m_b_sinks, m_b_w_out, m_b_post_norm, v_a_pre_norm, v_a_w_in, v_a_conv_w, v_a_w_out, v_a_post_norm, v_kv_norm, v_w_kv, v_rel_bias, v_b_pre_norm, v_b_w_in, v_b_sinks, v_b_w_out, v_b_post_norm):
    seq, d = x.shape[1], x.shape[2]
    x2 = x.reshape(seq, d)
    target = loss_target.reshape(seq, d)
    shard = a_pre_norm.shape[1]
    me = _my_index()
    ts_a = min(seq, 256)
    ts = min(seq, 512)

    small = _place_rows(a_pre_norm, 0) + _place_rows(a_conv_w[0], 1) + _place_rows(a_post_norm, 4)
    win_g, wout_g, wkv_g, wbin_g, wbout_g, small_g = _all_gather(
        [a_w_in[0], a_w_out[0], w_kv, b_w_in[0], b_w_out[0], small], [BF16] * 5 + [F32])
    wout = wout_g.reshape(-1, wout_g.shape[2])
    wkv = wkv_g.reshape(-1, wkv_g.shape[2])
    wbout = wbout_g.reshape(-1, wbout_g.shape[2])
    sm = small_g.transpose(1, 0, 2).reshape(8, N_DEV * shard)
    kvn = kv_norm.reshape(1, d)

    h1, n1, proj, conv, y, ya = _layer_a_fwd(x2, sm, win_g, wout, ts_a)
    n3, n4, kv, q, z2 = _layer_b_in(h1, kvn, b_pre_norm, wkv, wbin_g, ts)
    bucket, in_window, onehot = _band_structure()
    biasm = _bias_table(rel_bias, bucket, in_window)
    attn, o = _attn_fwd(q, kv, z2, biasm, b_sinks)
    dh2, dyb, dattn, dz2, acc_c = _layer_b_out(o, attn, z2, h1, target, wbout, b_post_norm, ts)

    dq, dkv, dssum, dsink = _attn_bwd(q, kv, dattn, biasm, b_sinks)
    relb = _relbias_grad(dssum.reshape(N_Q_HEADS, -1), onehot, 4096)
    dh1, dya, acc_b = _layer_b_in_bwd(dh2, dq, dz2, dkv, h1, ya, wbin_g, wkv, kvn, b_pre_norm, sm, ts)
    dproj, gx, acc_a = _layer_a_bwd(dya, proj, conv, dh1, x2, wout, win_g, sm, ts_a)
    g_win = _wgrad(n1, [dproj], N_DEV, ts, "wgrad_a_in")
    g_wout = _wgrad(y, [dya], 0, ts, "wgrad_a_out")
    g_wkv = _wgrad(n3, [dkv], 0, ts, "wgrad_kv")
    g_wbin = _wgrad(n4, [dq, dz2], N_DEV, ts, "wgrad_b_in")
    g_wbout = _wgrad(o, [dyb], 0, ts, "wgrad_b_out")

    (r_win, r_wout, r_wkv, r_wbin, r_wbout), (s_a, s_b, s_c, s_relb, s_sink) = _reduce_exchange(
        [g_win, g_wout.reshape(wout_g.shape), g_wkv.reshape(wkv_g.shape), g_wbin, g_wbout.reshape(wbout_g.shape)],
        [acc_a, acc_b, acc_c, relb, dsink])
    mine = lambda rows: lax.dynamic_slice_in_dim(rows, me * shard, shard, axis=1)
    loss = s_c[1, 0]
    weights = [a_pre_norm, a_w_in[0], a_conv_w[0], a_w_out[0], a_post_norm, kvn, w_kv, rel_bias, b_pre_norm,
               b_w_in[0], b_sinks, b_w_out[0], b_post_norm]
    grads = [mine(s_a[0:1]), r_win, mine(s_a[1:4]), r_wout, mine(s_b[2:3]), s_b[1:2], r_wkv,
             s_relb[:, :N_BUCKETS].T, s_b[0:1], r_wbin, s_sink[0:1, :N_Q_HEADS], r_wbout, s_c[0:1]]
    first = [m_a_pre_norm, m_a_w_in[0], m_a_conv_w[0], m_a_w_out[0], m_a_post_norm, m_kv_norm.reshape(1, d), m_w_kv,
             m_rel_bias, m_b_pre_norm, m_b_w_in[0], m_b_sinks, m_b_w_out[0], m_b_post_norm]
    second = [v_a_pre_norm, v_a_w_in[0], v_a_conv_w[0], v_a_w_out[0], v_a_post_norm, v_kv_norm.reshape(1, d), v_w_kv,
              v_rel_bias, v_b_pre_norm, v_b_w_in[0], v_b_sinks, v_b_w_out[0], v_b_post_norm]
    deltas, new_m, new_v = _adamw(weights, grads, first, second)

    shapes = [a_pre_norm.shape, a_w_in.shape, a_conv_w.shape, a_w_out.shape, a_post_norm.shape, kv_norm.shape,
              w_kv.shape, rel_bias.shape, b_pre_norm.shape, b_w_in.shape, b_sinks.shape, b_w_out.shape, b_post_norm.shape]
    shaped = lambda arrays: [a.reshape(s) for a, s in zip(arrays, shapes)]
    return (loss, gx.reshape(x.shape), *shaped(grads), *shaped(deltas), *shaped(new_m), *shaped(new_v))
```

```python
import functools
import math

import jax
import jax.numpy as jnp
from jax import lax
from jax.experimental import pallas as pl
from jax.experimental.pallas import tpu as pltpu

HEAD_DIM = 64
N_Q_HEADS = 16
N_KV_HEADS = 2
GROUP = N_Q_HEADS // N_KV_HEADS
BLOCK = 128
N_BUCKETS = 32
MAX_DISTANCE = 128
EPS = 1e-6
NEG_INF = -1e30
SCALE = HEAD_DIM ** -0.5

ADAM_LR = 0.001
ADAM_B1 = 0.9
ADAM_B2 = 0.999
ADAM_EPS = 1e-08
ADAM_WD = 0.01
ADAM_STEP = 10

N_DEV = 8
LANES = 128
F32 = jnp.float32
BF16 = jnp.bfloat16
MESH = pl.DeviceIdType.MESH
MIB = 1024 * 1024


def _params(semantics=None, vmem_mib=48):
    return pltpu.CompilerParams(dimension_semantics=semantics, vmem_limit_bytes=vmem_mib * MIB)


def _full(shape):
    zeros = (0,) * len(shape)
    return pl.BlockSpec(shape, lambda *_: zeros)


def _rows(ts, cols):
    return pl.BlockSpec((ts, cols), lambda i: (i, 0))


def _dot(a, b):
    return jnp.dot(a, b, preferred_element_type=F32)


def _dot_nt(a, b):
    return lax.dot_general(a, b, (((1,), (1,)), ((), ())), preferred_element_type=F32)


def _dot_tn(a, b):
    return lax.dot_general(a, b, (((0,), (0,)), ((), ())), preferred_element_type=F32)


def _rms(xf):
    r = lax.rsqrt(jnp.mean(xf * xf, axis=-1, keepdims=True) + EPS)
    return xf * r, r


def _rms_bwd(dn, xn, r):
    return r * (dn - xn * jnp.mean(dn * xn, axis=-1, keepdims=True))


def _silu(z):
    s = jax.nn.sigmoid(z)
    return z * s, s * (1.0 + z * (1.0 - s))


def _my_index():
    return 4 * lax.axis_index("x") + 2 * lax.axis_index("y") + lax.axis_index("c")


def _all_gather(shards, out_dtypes):
    n = len(shards)

    def body(*refs):
        ins, outs = refs[:n], refs[n:2 * n]
        send_sems, recv_sems = refs[2 * n], refs[2 * n + 1]
        x, y, c = lax.axis_index("x"), lax.axis_index("y"), lax.axis_index("c")
        me, sibling = (x, y, c), (x, y, 1 - c)
        chips = [(1 - x, y), (x, 1 - y), (1 - x, 1 - y)]

        def copy(t, k, block, to):
            rows = outs[t].at[4 * block[0] + 2 * block[1] + block[2]]
            return pltpu.make_async_remote_copy(
                src_ref=rows, dst_ref=rows, send_sem=send_sems.at[t, k], recv_sem=recv_sems.at[t, k],
                device_id=to, device_id_type=MESH)

        for t in range(n):
            outs[t][pl.ds(_my_index(), 1)] = ins[t][...].astype(outs[t].dtype)[None]
        first = []
        for t in range(n):
            first.append(copy(t, 0, me, sibling))
            first += [copy(t, 1 + j, me, (*chip, c)) for j, chip in enumerate(chips)]
        for cp in first:
            cp.start()
        passed = []
        for j, chip in enumerate(chips):
            for t in range(n):
                copy(t, 1 + j, (*chip, c), me).wait_recv()
                fwd = copy(t, 4 + j, (*chip, c), sibling)
                fwd.start()
                passed.append(fwd)
        for t in range(n):
            copy(t, 0, sibling, me).wait_recv()
        for j, chip in enumerate(chips):
            for t in range(n):
                copy(t, 4 + j, (*chip, 1 - c), me).wait_recv()
        for cp in first + passed:
            cp.wait_send()

    vmem = pl.BlockSpec(memory_space=pltpu.VMEM)
    return pl.pallas_call(
        body,
        name="gather_weights",
        out_shape=[jax.ShapeDtypeStruct((N_DEV,) + s.shape, dt) for s, dt in zip(shards, out_dtypes)],
        in_specs=[vmem] * n,
        out_specs=[vmem] * n,
        scratch_shapes=[pltpu.SemaphoreType.DMA((n, 7)), pltpu.SemaphoreType.DMA((n, 7))],
        compiler_params=_params(vmem_mib=48),
    )(*shards)


def _reduce_exchange(parts, smalls):
    ns, ng = len(parts), len(smalls)
    nt = ns + ng

    def body(*refs):
        p_in, s_in = refs[:ns], refs[ns:nt]
        p_out, s_out = refs[nt:nt + ns], refs[nt + ns:2 * nt]
        p_recv, s_recv = refs[2 * nt:2 * nt + ns], refs[2 * nt + ns:3 * nt]
        send_sems, recv_sems = refs[3 * nt], refs[3 * nt + 1]
        x, y, c = lax.axis_index("x"), lax.axis_index("y"), lax.axis_index("c")
        me = 4 * x + 2 * y + c

        def peer_of(k):
            px = 1 - x if k & 4 else x
            py = 1 - y if k & 2 else y
            pc = 1 - c if k & 1 else c
            return (px, py, pc), 4 * px + 2 * py + pc

        def copy(t, k):
            peer, pidx = peer_of(k)
            if t < ns:
                src, dst = p_in[t].at[pidx], p_recv[t].at[me]
            else:
                src, dst = s_in[t - ns], s_recv[t - ns].at[me]
            return pltpu.make_async_remote_copy(
                src_ref=src, dst_ref=dst, send_sem=send_sems.at[t, k - 1], recv_sem=recv_sems.at[t, k - 1],
                device_id=peer, device_id_type=MESH)

        def landed(t, k):
            _, pidx = peer_of(k)
            buf = p_recv[t] if t < ns else s_recv[t - ns]
            return pltpu.make_async_remote_copy(
                src_ref=buf.at[pidx], dst_ref=buf.at[pidx], send_sem=send_sems.at[t, k - 1],
                recv_sem=recv_sems.at[t, k - 1], device_id=peer_of(k)[0], device_id_type=MESH)

        sent = [copy(t, k) for k in range(1, N_DEV) for t in range(nt)]
        for cp in sent:
            cp.start()
        for t in range(ns):
            p_recv[t][pl.ds(me, 1)] = p_in[t][pl.ds(me, 1)]
        for t in range(ng):
            s_recv[t][pl.ds(me, 1)] = s_in[t][...][None]
        for k in range(1, N_DEV):
            for t in range(nt):
                landed(t, k).wait_recv()
        for cp in sent:
            cp.wait_send()

        for t in range(ns):
            rows = p_out[t].shape[0]
            chunk = min(rows, 128)

            def add(i, carry, t=t, chunk=chunk):
                r0 = pl.multiple_of(i * chunk, chunk)
                acc = p_recv[t][0, pl.ds(r0, chunk), :].astype(F32)
                for d in range(1, N_DEV):
                    acc = acc + p_recv[t][d, pl.ds(r0, chunk), :].astype(F32)
                p_out[t][pl.ds(r0, chunk), :] = acc
                return carry

            lax.fori_loop(0, rows // chunk, add, 0)
        for t in range(ng):
            acc = s_recv[t][0]
            for d in range(1, N_DEV):
                acc = acc + s_recv[t][d]
            s_out[t][...] = acc

    vmem = pl.BlockSpec(memory_space=pltpu.VMEM)
    outs = pl.pallas_call(
        body,
        name="reduce_grads",
        out_shape=[jax.ShapeDtypeStruct(p.shape[1:], F32) for p in parts]
        + [jax.ShapeDtypeStruct(s.shape, F32) for s in smalls],
        in_specs=[vmem] * nt,
        out_specs=[vmem] * nt,
        scratch_shapes=[pltpu.VMEM(p.shape, p.dtype) for p in parts]
        + [pltpu.VMEM((N_DEV,) + s.shape, F32) for s in smalls]
        + [pltpu.SemaphoreType.DMA((nt, 7)), pltpu.SemaphoreType.DMA((nt, 7))],
        compiler_params=_params(vmem_mib=56),
    )(*parts, *smalls)
    return outs[:ns], outs[ns:]


def _layer_a_fwd(x2, sm, win_g, wout, ts):
    seq, d = x2.shape
    width = wout.shape[0]
    half = win_g.shape[2]
    n_half = width // half

    def body(x_ref, sm_ref, win_ref, wout_ref, h1_ref, n1_ref, proj_ref, conv_ref, y_ref, ya_ref, vprev_ref):
        @pl.when(pl.program_id(0) == 0)
        def _():
            vprev_ref[...] = jnp.zeros_like(vprev_ref)

        xf = x_ref[...]
        xn, _ = _rms(xf)
        n1 = (xn * sm_ref[0:1, :]).astype(BF16)
        n1_ref[...] = n1
        row = lax.broadcasted_iota(jnp.int32, (ts, half), 0)
        ya = jnp.zeros((ts, d), F32)
        for hh in range(n_half):
            cols = slice(hh * half, (hh + 1) * half)
            parts = []
            for part in range(4):
                j = part * n_half + hh
                pj = _dot(n1, win_ref[j])
                proj_ref[:, j * half:(j + 1) * half] = pj.astype(BF16)
                parts.append(pj)
            b, c, u, z = parts
            v = c * u
            last1, last2 = vprev_ref[7:8, cols], vprev_ref[6:7, cols]
            v1 = jnp.where(row == 0, last1, pltpu.roll(v, 1, 0))
            v2 = jnp.where(row == 0, last2, jnp.where(row == 1, last1, pltpu.roll(v, 2, 0)))
            vprev_ref[:, cols] = v[ts - 8:ts, :]
            conv = sm_ref[1:2, cols] * v2 + sm_ref[2:3, cols] * v1 + sm_ref[3:4, cols] * v
            conv_ref[:, cols] = conv.astype(BF16)
            yh = (b * conv * _silu(z)[0]).astype(BF16)
            y_ref[:, cols] = yh
            ya = ya + _dot(yh, wout_ref[cols, :])
        ya_ref[...] = ya
        h1_ref[...] = xf + _rms(ya)[0] * sm_ref[4:5, :]

    return pl.pallas_call(
        body,
        name="layer_a_fwd",
        grid=(seq // ts,),
        in_specs=[_rows(ts, d), _full(sm.shape), _full(win_g.shape), _full(wout.shape)],
        out_specs=[_rows(ts, d), _rows(ts, d), _rows(ts, 4 * width), _rows(ts, width), _rows(ts, width), _rows(ts, d)],
        out_shape=[
            jax.ShapeDtypeStruct((seq, d), F32),
            jax.ShapeDtypeStruct((seq, d), BF16),
            jax.ShapeDtypeStruct((seq, 4 * width), BF16),
            jax.ShapeDtypeStruct((seq, width), BF16),
            jax.ShapeDtypeStruct((seq, width), BF16),
            jax.ShapeDtypeStruct((seq, d), F32),
        ],
        scratch_shapes=[pltpu.VMEM((8, width), F32)],
        compiler_params=_params(("arbitrary",), 56),
    )(x2, sm, win_g, wout)


def _layer_b_in(h1, kvn, bpre, wkv, wbin_g, ts):
    seq, d = h1.shape
    kvw = wkv.shape[1]
    cw = wbin_g.shape[2]
    aw = N_Q_HEADS * HEAD_DIM
    per = aw // cw

    def body(h1_ref, kvn_ref, bpre_ref, wkv_ref, wbin_ref, n3_ref, n4_ref, kv_ref, q_ref, z2_ref):
        hn, _ = _rms(h1_ref[...])
        n3 = (hn * kvn_ref[...]).astype(BF16)
        n4 = (hn * bpre_ref[...]).astype(BF16)
        n3_ref[...] = n3
        n4_ref[...] = n4
        kv_ref[...] = _dot(n3, wkv_ref[...]).astype(BF16)
        for j in range(N_DEV):
            pj = _dot(n4, wbin_ref[j])
            if j < per:
                q_ref[:, j * cw:(j + 1) * cw] = pj.astype(BF16)
            else:
                z2_ref[:, (j - per) * cw:(j - per + 1) * cw] = pj

    return pl.pallas_call(
        body,
        name="layer_b_in",
        grid=(seq // ts,),
        in_specs=[_rows(ts, d), _full(kvn.shape), _full(bpre.shape), _full(wkv.shape), _full(wbin_g.shape)],
        out_specs=[_rows(ts, d), _rows(ts, d), _rows(ts, kvw), _rows(ts, aw), _rows(ts, aw)],
        out_shape=[
            jax.ShapeDtypeStruct((seq, d), BF16),
            jax.ShapeDtypeStruct((seq, d), BF16),
            jax.ShapeDtypeStruct((seq, kvw), BF16),
            jax.ShapeDtypeStruct((seq, aw), BF16),
            jax.ShapeDtypeStruct((seq, aw), F32),
        ],
        compiler_params=_params(("parallel",), 48),
    )(h1, kvn, bpre, wkv, wbin_g)


N_PAIRS = N_Q_HEADS // 2
BAND = 2 * BLOCK


def _bias_table(rel_bias, bucket_t, in_window_t):
    def body(rb_ref, bucket_ref, win_ref, out_ref):
        bk = bucket_ref[...]
        inside = win_ref[...] != 0
        has_prev = lax.broadcasted_iota(jnp.int32, bk.shape, 0) >= BLOCK
        for h in range(N_Q_HEADS):
            acc = jnp.full(bk.shape, NEG_INF, F32)
            for b in range(N_BUCKETS):
                acc = jnp.where(jnp.logical_and(bk == b, inside), rb_ref[b, h], acc)
            cols = slice((h % 2) * BLOCK, (h % 2 + 1) * BLOCK)
            out_ref[1, h // 2, :, cols] = acc
            out_ref[0, h // 2, :, cols] = jnp.where(has_prev, acc, NEG_INF)

    vmem = pl.BlockSpec(memory_space=pltpu.VMEM)
    return pl.pallas_call(
        body,
        name="bias_table",
        in_specs=[pl.BlockSpec(memory_space=pltpu.SMEM), vmem, vmem],
        out_specs=vmem,
        out_shape=jax.ShapeDtypeStruct((2, N_PAIRS, BAND, 2 * BLOCK), F32),
    )(rel_bias, bucket_t, in_window_t)


def _bias_spec(biasm):
    return pl.BlockSpec((None,) + biasm.shape[1:], lambda i: (jnp.minimum(i, 1), 0, 0, 0))


def _banded_kv(kvp_ref, kvc_ref):
    kvp = kvp_ref[...].astype(F32)
    kvc = kvc_ref[...].astype(F32)
    kw = N_KV_HEADS * HEAD_DIM
    out = []
    for full in (jnp.concatenate([kvp[:, :kw], kvc[:, :kw]], axis=0), jnp.concatenate([kvp[:, kw:], kvc[:, kw:]], axis=0)):
        lo = lax.broadcasted_iota(jnp.int32, full.shape, 1) < HEAD_DIM
        rolled = pltpu.roll(full, HEAD_DIM, 1)
        x2 = [jnp.where(lo, full, rolled).astype(BF16), jnp.where(lo, rolled, full).astype(BF16)]
        ft = full.T
        x2t = [jnp.concatenate([ft[kh * HEAD_DIM:(kh + 1) * HEAD_DIM]] * 2, axis=0).astype(BF16) for kh in range(N_KV_HEADS)]
        out += [x2, x2t]
    return out


def _pair_rows(ref, m, scale=None):
    both = ref[:, m * LANES:(m + 1) * LANES].astype(F32)
    if scale is not None:
        both = both * scale
    lo = lax.broadcasted_iota(jnp.int32, both.shape, 1) < HEAD_DIM
    zero = jnp.zeros_like(both)
    return jnp.concatenate([jnp.where(lo, both, zero), jnp.where(lo, zero, both)], axis=0).astype(BF16)


def _pair_cols(res_t):
    top = lax.broadcasted_iota(jnp.int32, (LANES, BLOCK), 0) < HEAD_DIM
    return jnp.where(top, res_t[:, :BLOCK], res_t[:, BLOCK:]).T


def _sink_row(sink_ref, m):
    first = lax.broadcasted_iota(jnp.int32, (1, 2 * BLOCK), 1) < BLOCK
    return jnp.where(first, sink_ref[0, 2 * m], sink_ref[0, 2 * m + 1])


def _probs_t(k2, qpair, bias, sink):
    return _softmax_t(_dot_nt(k2, qpair) + bias, sink)


def _softmax_t(logits, sink):
    mx = jnp.maximum(jnp.max(logits, axis=0, keepdims=True), sink)
    p = jnp.exp(logits - mx)
    sink_p = jnp.exp(sink - mx)
    inv = 1.0 / (jnp.sum(p, axis=0, keepdims=True) + sink_p)
    return p * inv, sink_p * inv


def _attn_fwd(q, kv, z2, biasm, sinks):
    seq, aw = q.shape
    kvw = kv.shape[1]
    nb = seq // BLOCK

    def body(sink_ref, q_ref, kvc_ref, kvp_ref, z2_ref, bias_ref, attn_ref, o_ref, acc_ref):
        k2, _, _, v2t = _banded_kv(kvp_ref, kvc_ref)
        kv_of = lambda m: (2 * m) // GROUP
        logits, probs = {}, {}
        for step in range(N_PAIRS + 2):
            if step < N_PAIRS:
                logits[step] = _dot_nt(k2[kv_of(step)], _pair_rows(q_ref, step, SCALE)) + bias_ref[step]
            m = step - 1
            if 0 <= m < N_PAIRS:
                probs[m] = _softmax_t(logits.pop(m), _sink_row(sink_ref, m))[0].astype(BF16)
            m = step - 2
            if 0 <= m < N_PAIRS:
                acc_ref[:, m * LANES:(m + 1) * LANES] = _pair_cols(_dot(v2t[kv_of(m)], probs.pop(m)))
        attn = acc_ref[...]
        attn_ref[...] = attn.astype(BF16)
        o_ref[...] = (attn * _silu(z2_ref[...])[0]).astype(BF16)

    blk = lambda w: pl.BlockSpec((BLOCK, w), lambda i: (i, 0))
    return pl.pallas_call(
        body,
        name="attn_fwd",
        grid=(nb,),
        in_specs=[
            pl.BlockSpec(memory_space=pltpu.SMEM),
            blk(aw),
            blk(kvw),
            pl.BlockSpec((BLOCK, kvw), lambda i: (jnp.maximum(i - 1, 0), 0)),
            blk(aw),
            _bias_spec(biasm),
        ],
        out_specs=[blk(aw), blk(aw)],
        out_shape=[jax.ShapeDtypeStruct((seq, aw), BF16), jax.ShapeDtypeStruct((seq, aw), BF16)],
        scratch_shapes=[pltpu.VMEM((BLOCK, aw), F32)],
        compiler_params=_params(("arbitrary",), 32),
    )(sinks, q, kv, kv, z2, biasm)


def _layer_b_out(o, attn, z2, h1, target, wbout, bpost, ts):
    seq, d = h1.shape
    aw = o.shape[1]

    def body(o_ref, attn_ref, z2_ref, h1_ref, tgt_ref, w_ref, g_ref, dh2_ref, dyb_ref, dattn_ref, dz2_ref, acc_ref):
        @pl.when(pl.program_id(0) == 0)
        def _():
            acc_ref[...] = jnp.zeros_like(acc_ref)

        w = w_ref[...]
        yb = _dot(o_ref[...], w)
        ybn, r = _rms(yb)
        g = g_ref[...]
        diff = h1_ref[...] + ybn * g - tgt_ref[...]
        dh2 = diff * (1.0 / d)
        dh2_ref[...] = dh2
        acc_ref[0:1, :] += jnp.sum(dh2 * ybn, axis=0, keepdims=True)
        tok = jnp.mean(diff * diff, axis=-1, keepdims=True)
        acc_ref[1:2, :] += 0.5 * jnp.sum(tok, axis=0, keepdims=True)
        dyb = _rms_bwd(dh2 * g, ybn, r).astype(BF16)
        dyb_ref[...] = dyb
        do = _dot_nt(dyb, w)
        sz, dsz = _silu(z2_ref[...])
        dattn_ref[...] = (do * sz).astype(BF16)
        dz2_ref[...] = (do * attn_ref[...].astype(F32) * dsz).astype(BF16)

    return pl.pallas_call(
        body,
        name="layer_b_out",
        grid=(seq // ts,),
        in_specs=[_rows(ts, aw), _rows(ts, aw), _rows(ts, aw), _rows(ts, d), _rows(ts, d), _full(wbout.shape), _full(bpost.shape)],
        out_specs=[_rows(ts, d), _rows(ts, d), _rows(ts, aw), _rows(ts, aw), _full((8, d))],
        out_shape=[
            jax.ShapeDtypeStruct((seq, d), F32),
            jax.ShapeDtypeStruct((seq, d), BF16),
            jax.ShapeDtypeStruct((seq, aw), BF16),
            jax.ShapeDtypeStruct((seq, aw), BF16),
            jax.ShapeDtypeStruct((8, d), F32),
        ],
        compiler_params=_params(("arbitrary",), 48),
    )(o, attn, z2, h1, target, wbout, bpost)


def _attn_bwd(q, kv, dattn, biasm, sinks):
    seq, aw = q.shape
    kvw = kv.shape[1]
    kw = N_KV_HEADS * HEAD_DIM
    nb = seq // BLOCK
    pairs_per_kv = N_PAIRS // N_KV_HEADS

    def body(sink_ref, q_ref, kvc_ref, kvp_ref, da_ref, bias_ref, dq_ref, dkv_ref, dssum_ref, dsink_ref,
             carry_ref, qs_ref, dos_ref, dst_ref, pt_ref):
        i = pl.program_id(0)

        @pl.when(i == 0)
        def _():
            dssum_ref[...] = jnp.zeros_like(dssum_ref)
            dsink_ref[...] = jnp.zeros_like(dsink_ref)
            carry_ref[...] = jnp.zeros_like(carry_ref)

        @pl.when(i < nb)
        def _():
            lo = lax.broadcasted_iota(jnp.int32, (BAND, LANES), 1) < HEAD_DIM
            head_lane = lax.broadcasted_iota(jnp.int32, (1, LANES), 1)
            k2, k2t, v2, _ = _banded_kv(kvp_ref, kvc_ref)
            dsink = jnp.zeros((1, LANES), F32)
            folded = []
            logits, dps, dsbs = {}, {}, {}
            for step in range(N_PAIRS + 2):
                if step < N_PAIRS:
                    kh, rows = step // pairs_per_kv, slice((step % pairs_per_kv) * BAND, (step % pairs_per_kv + 1) * BAND)
                    qpair = _pair_rows(q_ref, step, SCALE)
                    dopair = _pair_rows(da_ref, step)
                    qs_ref[kh, rows, :] = qpair
                    dos_ref[kh, rows, :] = dopair
                    logits[step] = _dot_nt(k2[kh], qpair) + bias_ref[step]
                    dps[step] = _dot_nt(v2[kh], dopair)
                m = step - 1
                if 0 <= m < N_PAIRS:
                    kh, rows = m // pairs_per_kv, slice((m % pairs_per_kv) * BAND, (m % pairs_per_kv + 1) * BAND)
                    pn, sink_p = _softmax_t(logits.pop(m), _sink_row(sink_ref, m))
                    dp = dps.pop(m)
                    delta = jnp.sum(pn * dp, axis=0, keepdims=True)
                    ds = pn * (dp - delta)
                    dssum_ref[m] += ds
                    sink_term = sink_p * delta
                    for e in range(2):
                        total = jnp.sum(sink_term[:, e * BLOCK:(e + 1) * BLOCK], axis=1, keepdims=True)
                        dsink = dsink - jnp.where(head_lane == 2 * m + e, total, 0.0)
                    dsbs[m] = ds.astype(BF16)
                    dst_ref[kh, :, rows] = dsbs[m]
                    pt_ref[kh, :, rows] = pn.astype(BF16)
                m = step - 2
                if 0 <= m < N_PAIRS:
                    kh = m // pairs_per_kv
                    dq_ref[:, m * LANES:(m + 1) * LANES] = (_pair_cols(_dot(k2t[kh], dsbs.pop(m))) * SCALE).astype(BF16)
                    if m % pairs_per_kv == pairs_per_kv - 1:
                        for lhs_ref, rhs_ref in ((dst_ref, qs_ref), (pt_ref, dos_ref)):
                            acc = _dot(lhs_ref[kh], rhs_ref[kh])
                            folded.append(acc + pltpu.roll(acc, HEAD_DIM, 1))
            dsink_ref[0:1, :] += dsink
            dk = jnp.where(lo, folded[0], folded[2])
            dv = jnp.where(lo, folded[1], folded[3])

            @pl.when(i > 0)
            def _():
                dkv_ref[:, :kw] = (carry_ref[:, :kw] + dk[:BLOCK]).astype(BF16)
                dkv_ref[:, kw:] = (carry_ref[:, kw:] + dv[:BLOCK]).astype(BF16)

            carry_ref[:, :kw] = dk[BLOCK:]
            carry_ref[:, kw:] = dv[BLOCK:]

        @pl.when(i == nb)
        def _():
            dkv_ref[...] = carry_ref[...].astype(BF16)

    last = nb - 1
    blk = lambda w: pl.BlockSpec((BLOCK, w), lambda i: (jnp.minimum(i, last), 0))
    return pl.pallas_call(
        body,
        name="attn_bwd",
        grid=(nb + 1,),
        in_specs=[
            pl.BlockSpec(memory_space=pltpu.SMEM),
            blk(aw),
            blk(kvw),
            pl.BlockSpec((BLOCK, kvw), lambda i: (jnp.clip(i - 1, 0, last), 0)),
            blk(aw),
            _bias_spec(biasm),
        ],
        out_specs=[
            blk(aw),
            pl.BlockSpec((BLOCK, kvw), lambda i: (jnp.maximum(i - 1, 0), 0)),
            _full(biasm.shape[1:]),
            _full((8, LANES)),
        ],
        out_shape=[
            jax.ShapeDtypeStruct((seq, aw), BF16),
            jax.ShapeDtypeStruct((seq, kvw), BF16),
            jax.ShapeDtypeStruct(biasm.shape[1:], F32),
            jax.ShapeDtypeStruct((8, LANES), F32),
        ],
        scratch_shapes=[
            pltpu.VMEM((BLOCK, kvw), F32),
            pltpu.VMEM((N_KV_HEADS, pairs_per_kv * BAND, LANES), BF16),
            pltpu.VMEM((N_KV_HEADS, pairs_per_kv * BAND, LANES), BF16),
            pltpu.VMEM((N_KV_HEADS, BAND, pairs_per_kv * BAND), BF16),
            pltpu.VMEM((N_KV_HEADS, BAND, pairs_per_kv * BAND), BF16),
        ],
        compiler_params=_params(("arbitrary",), 40),
    )(sinks, q, kv, kv, dattn, biasm)


def _relbias_grad(dssum2, onehot, chunk):
    heads, n = dssum2.shape

    def body(a_ref, oh_ref, out_ref):
        @pl.when(pl.program_id(0) == 0)
        def _():
            out_ref[...] = jnp.zeros_like(out_ref)

        a = a_ref[...]
        hi = a.astype(BF16)
        lo = (a - hi.astype(F32)).astype(BF16)
        out_ref[...] += _dot(hi, oh_ref[...]) + _dot(lo, oh_ref[...])

    return pl.pallas_call(
        body,
        name="relbias_grad",
        grid=(n // chunk,),
        in_specs=[pl.BlockSpec((heads, chunk), lambda i: (0, i)), pl.BlockSpec((chunk, LANES), lambda i: (i, 0))],
        out_specs=_full((heads, LANES)),
        out_shape=jax.ShapeDtypeStruct((heads, LANES), F32),
        compiler_params=_params(("arbitrary",), 32),
    )(dssum2, onehot)


def _layer_b_in_bwd(dh2, dq, dz2, dkv, h1, ya, wbin_g, wkv, kvn, bpre, sm, ts):
    seq, d = h1.shape
    aw = dq.shape[1]
    kvw = dkv.shape[1]
    cw = wbin_g.shape[2]
    per = aw // cw

    def body(dh2_ref, dq_ref, dz2_ref, dkv_ref, h1_ref, ya_ref, wbin_ref, wkv_ref, kvn_ref, bpre_ref, sm_ref,
             dh1_ref, dya_ref, acc_ref):
        @pl.when(pl.program_id(0) == 0)
        def _():
            acc_ref[...] = jnp.zeros_like(acc_ref)

        dn4 = jnp.zeros((ts, d), F32)
        for j in range(N_DEV):
            src = dq_ref if j < per else dz2_ref
            jj = j % per
            dn4 = dn4 + _dot_nt(src[:, jj * cw:(jj + 1) * cw], wbin_ref[j])
        dn3 = _dot_nt(dkv_ref[...], wkv_ref[...])
        hn, r = _rms(h1_ref[...])
        acc_ref[0:1, :] += jnp.sum(dn4 * hn, axis=0, keepdims=True)
        acc_ref[1:2, :] += jnp.sum(dn3 * hn, axis=0, keepdims=True)
        dh1 = dh2_ref[...] + _rms_bwd(dn4 * bpre_ref[...] + dn3 * kvn_ref[...], hn, r)
        dh1_ref[...] = dh1
        yan, r2 = _rms(ya_ref[...])
        acc_ref[2:3, :] += jnp.sum(dh1 * yan, axis=0, keepdims=True)
        dya_ref[...] = _rms_bwd(dh1 * sm_ref[4:5, :], yan, r2).astype(BF16)

    return pl.pallas_call(
        body,
        name="layer_b_in_bwd",
        grid=(seq // ts,),
        in_specs=[_rows(ts, d), _rows(ts, aw), _rows(ts, aw), _rows(ts, kvw), _rows(ts, d), _rows(ts, d),
                  _full(wbin_g.shape), _full(wkv.shape), _full(kvn.shape), _full(bpre.shape), _full(sm.shape)],
        out_specs=[_rows(ts, d), _rows(ts, d), _full((8, d))],
        out_shape=[jax.ShapeDtypeStruct((seq, d), F32), jax.ShapeDtypeStruct((seq, d), BF16),
                   jax.ShapeDtypeStruct((8, d), F32)],
        compiler_params=_params(("arbitrary",), 48),
    )(dh2, dq, dz2, dkv, h1, ya, wbin_g, wkv, kvn, bpre, sm)


def _layer_a_bwd(dya, proj, conv, dh1, x2, wout, win_g, sm, ts):
    seq, d = x2.shape
    width = wout.shape[0]
    half = win_g.shape[2]
    n_half = width // half
    nt = seq // ts

    def body(dya_ref, proj_ref, conv_ref, dh1_ref, x_ref, wout_ref, win_ref, sm_ref, dproj_ref, gx_ref, acc_ref,
             dnext_ref):
        @pl.when(pl.program_id(0) == 0)
        def _():
            acc_ref[...] = jnp.zeros_like(acc_ref)
            dnext_ref[...] = jnp.zeros_like(dnext_ref)

        dy = _dot_nt(dya_ref[...], wout_ref[...])
        row = lax.broadcasted_iota(jnp.int32, (ts, half), 0)
        dn1 = jnp.zeros((ts, d), F32)
        for hh in range(n_half):
            cols = slice(hh * half, (hh + 1) * half)
            b, c, u, z = [proj_ref[:, (part * n_half + hh) * half:(part * n_half + hh + 1) * half].astype(F32)
                          for part in range(4)]
            cv = conv_ref[:, cols].astype(F32)
            dyh = dy[:, cols]
            sz, dsz = _silu(z)
            dconv = dyh * b * sz
            grads = [dyh * cv * sz, None, None, dyh * b * cv * dsz]
            next0, next1 = dnext_ref[0:1, cols], dnext_ref[1:2, cols]
            dc1 = jnp.where(row == ts - 1, next0, pltpu.roll(dconv, ts - 1, 0))
            dc2 = jnp.where(row == ts - 1, next1, jnp.where(row == ts - 2, next0, pltpu.roll(dconv, ts - 2, 0)))
            dnext_ref[:, cols] = dconv[0:8, :]
            v = c * u
            acc_ref[1:2, cols] += jnp.sum(dc2 * v, axis=0, keepdims=True)
            acc_ref[2:3, cols] += jnp.sum(dc1 * v, axis=0, keepdims=True)
            acc_ref[3:4, cols] += jnp.sum(dconv * v, axis=0, keepdims=True)
            dv = sm_ref[3:4, cols] * dconv + sm_ref[2:3, cols] * dc1 + sm_ref[1:2, cols] * dc2
            grads[1] = dv * u
            grads[2] = dv * c
            for part in range(4):
                j = part * n_half + hh
                gj = grads[part].astype(BF16)
                dproj_ref[:, j * half:(j + 1) * half] = gj
                dn1 = dn1 + _dot_nt(gj, win_ref[j])
        xn, r = _rms(x_ref[...])
        acc_ref[0:1, :] += jnp.sum(dn1 * xn, axis=0, keepdims=True)
        gx_ref[...] = dh1_ref[...] + _rms_bwd(dn1 * sm_ref[0:1, :], xn, r)

    rev = lambda w: pl.BlockSpec((ts, w), lambda i: (nt - 1 - i, 0))
    return pl.pallas_call(
        body,
        name="layer_a_bwd",
        grid=(nt,),
        in_specs=[rev(d), rev(4 * width), rev(width), rev(d), rev(d), _full(wout.shape), _full(win_g.shape), _full(sm.shape)],
        out_specs=[rev(4 * width), rev(d), _full((8, d))],
        out_shape=[jax.ShapeDtypeStruct((seq, 4 * width), BF16), jax.ShapeDtypeStruct((seq, d), F32),
                   jax.ShapeDtypeStruct((8, d), F32)],
        scratch_shapes=[pltpu.VMEM((8, width), F32)],
        compiler_params=_params(("arbitrary",), 56),
    )(dya, proj, conv, dh1, x2, wout, win_g, sm)


def _wgrad(a, bs, n_slots, ts, name):
    seq, k = a.shape
    nb_in = len(bs)
    n_each = bs[0].shape[1]
    n = nb_in * n_each
    bn = min(n_each, 1024)
    per_in = n_each // bn
    n_blocks = nb_in * per_in
    ns = seq // ts

    def b_spec(idx):
        def index(j, s):
            mine = j // per_in == idx
            row = jnp.where(mine, s, jnp.where(j // per_in > idx, ns - 1, 0))
            return (row, jnp.where(mine, j % per_in, jnp.where(j // per_in > idx, per_in - 1, 0)))
        return pl.BlockSpec((ts, bn), index)

    if n_slots:
        sw = n // n_slots
        spb = bn // sw
        out_shape = jax.ShapeDtypeStruct((n_slots, k, sw), BF16)
        out_spec = pl.BlockSpec((spb, k, sw), lambda j, s: (j, 0, 0))
    else:
        out_shape = jax.ShapeDtypeStruct((k, n), BF16)
        out_spec = pl.BlockSpec((k, bn), lambda j, s: (0, j))

    def body(a_ref, *refs):
        b_refs, o_ref, acc_ref = refs[:nb_in], refs[nb_in], refs[nb_in + 1]
        j, s = pl.program_id(0), pl.program_id(1)

        @pl.when(s == 0)
        def _():
            acc_ref[...] = jnp.zeros_like(acc_ref)

        for idx in range(nb_in):
            @pl.when(j // per_in == idx)
            def _(idx=idx):
                acc_ref[...] += _dot_tn(a_ref[...], b_refs[idx][...])

        @pl.when(s == ns - 1)
        def _():
            if n_slots:
                for e in range(spb):
                    o_ref[e] = acc_ref[:, e * sw:(e + 1) * sw].astype(BF16)
            else:
                o_ref[...] = acc_ref[...].astype(BF16)

    return pl.pallas_call(
        body,
        name=name,
        grid=(n_blocks, ns),
        in_specs=[pl.BlockSpec((ts, k), lambda j, s: (s, 0))] + [b_spec(idx) for idx in range(nb_in)],
        out_specs=out_spec,
        out_shape=out_shape,
        scratch_shapes=[pltpu.VMEM((k, bn), F32)],
        compiler_params=_params(("arbitrary", "arbitrary"), 48),
    )(a, *bs)


def _adamw(ws, gs, ms, vs):
    n = len(ws)

    def step(w, g, m, v):
        m = ADAM_B1 * m + (1.0 - ADAM_B1) * g
        v = ADAM_B2 * v + (1.0 - ADAM_B2) * jnp.square(g)
        m_hat = m / (1.0 - ADAM_B1 ** ADAM_STEP)
        v_hat = v / (1.0 - ADAM_B2 ** ADAM_STEP)
        return -ADAM_LR * (m_hat / (jnp.sqrt(v_hat) + ADAM_EPS) + ADAM_WD * w), m, v

    def body(*refs):
        w_refs, g_refs, m_refs, v_refs = (refs[k * n:(k + 1) * n] for k in range(4))
        d_refs, nm_refs, nv_refs = (refs[(4 + k) * n:(5 + k) * n] for k in range(3))
        for t in range(n):
            rows = w_refs[t].shape[0]
            if rows <= 128:
                d_refs[t][...], nm_refs[t][...], nv_refs[t][...] = step(
                    w_refs[t][...], g_refs[t][...], m_refs[t][...], v_refs[t][...])
                continue
            chunk = 128

            def one(i, carry, t=t):
                r = pl.ds(pl.multiple_of(i * chunk, chunk), chunk)
                d_refs[t][r, :], nm_refs[t][r, :], nv_refs[t][r, :] = step(
                    w_refs[t][r, :], g_refs[t][r, :], m_refs[t][r, :], v_refs[t][r, :])
                return carry

            lax.fori_loop(0, rows // chunk, one, 0)

    vmem = pl.BlockSpec(memory_space=pltpu.VMEM)
    outs = pl.pallas_call(
        body,
        name="adamw",
        in_specs=[vmem] * (4 * n),
        out_specs=[vmem] * (3 * n),
        out_shape=[jax.ShapeDtypeStruct(w.shape, F32) for w in ws] * 3,
        compiler_params=_params(vmem_mib=56),
    )(*ws, *gs, *ms, *vs)
    return outs[:n], outs[n:2 * n], outs[2 * n:]


def _band_structure():
    q_loc = jnp.arange(BLOCK, dtype=jnp.int32)[:, None]
    s_loc = jnp.arange(2 * BLOCK, dtype=jnp.int32)[None, :]
    dist = q_loc + BLOCK - s_loc
    in_window = (dist >= 0) & (dist < BLOCK)
    dd = jnp.maximum(dist, 0)
    max_exact = N_BUCKETS // 2
    large = max_exact + (jnp.log(jnp.maximum(dd, 1).astype(F32) / max_exact) / math.log(MAX_DISTANCE / max_exact)
                         * (N_BUCKETS - max_exact)).astype(jnp.int32)
    bucket = jnp.where(dd < max_exact, dd, jnp.minimum(large, N_BUCKETS - 1))
    onehot = (bucket.reshape(-1, 1) == jnp.arange(LANES, dtype=jnp.int32)[None, :]).astype(BF16)
    return bucket, in_window.astype(jnp.int32), onehot


def _place_rows(a, row, rows=8):
    return jnp.pad(a, ((row, rows - row - a.shape[0]), (0, 0)))


def kernel(x, a_pre_norm, a_w_in, a_conv_w, a_w_out, a_post_norm, kv_norm, w_kv, rel_bias, b_pre_norm, b_w_in, b_sinks, b_w_out, b_post_norm, loss_target, m_a_pre_norm, m_a_w_in, m_a_conv_w, m_a_w_out, m_a_post_norm, m_kv_norm, m_w_kv, m_rel_bias, m_b_pre_norm, m_b_w_in, m_b_sinks, m_b_w_out, m_b_post_norm, v_a_pre_norm, v_a_w_in, v_a_conv_w, v_a_w_out, v_a_post_norm, v_kv_norm, v_w_kv, v_rel_bias, v_b_pre_norm, v_b_w_in, v_b_sinks, v_b_w_out, v_b_post_norm):
    seq, d = x.shape[1], x.shape[2]
    x2 = x.reshape(seq, d)
    target = loss_target.reshape(seq, d)
    shard = a_pre_norm.shape[1]
    me = _my_index()
    ts_a = min(seq, 256)
    ts = min(seq, 512)

    small = _place_rows(a_pre_norm, 0) + _place_rows(a_conv_w[0], 1) + _place_rows(a_post_norm, 4)
    win_g, wout_g, wkv_g, wbin_g, wbout_g, small_g = _all_gather(
        [a_w_in[0], a_w_out[0], w_kv, b_w_in[0], b_w_out[0], small], [BF16] * 5 + [F32])
    wout = wout_g.reshape(-1, wout_g.shape[2])
    wkv = wkv_g.reshape(-1, wkv_g.shape[2])
    wbout = wbout_g.reshape(-1, wbout_g.shape[2])
    sm = small_g.transpose(1, 0, 2).reshape(8, N_DEV * shard)
    kvn = kv_norm.reshape(1, d)

    h1, n1, proj, conv, y, ya = _layer_a_fwd(x2, sm, win_g, wout, ts_a)
    n3, n4, kv, q, z2 = _layer_b_in(h1, kvn, b_pre_norm, wkv, wbin_g, ts)
    bucket, in_window, onehot = _band_structure()
    biasm = _bias_table(rel_bias, bucket.T, in_window.T)
    attn, o = _attn_fwd(q, kv, z2, biasm, b_sinks)
    dh2, dyb, dattn, dz2, acc_c = _layer_b_out(o, attn, z2, h1, target, wbout, b_post_norm, ts)

    dq, dkv, dssum, dsink = _attn_bwd(q, kv, dattn, biasm, b_sinks)
    by_head = dssum.reshape(N_PAIRS, BAND, 2, BLOCK).transpose(0, 2, 3, 1)
    relb = _relbias_grad(by_head.reshape(N_Q_HEADS, -1), onehot, 4096)
    dh1, dya, acc_b = _layer_b_in_bwd(dh2, dq, dz2, dkv, h1, ya, wbin_g, wkv, kvn, b_pre_norm, sm, ts)
    dproj, gx, acc_a = _layer_a_bwd(dya, proj, conv, dh1, x2, wout, win_g, sm, ts_a)
    g_win = _wgrad(n1, [dproj], N_DEV, ts, "wgrad_a_in")
    g_wout = _wgrad(y, [dya], 0, ts, "wgrad_a_out")
    g_wkv = _wgrad(n3, [dkv], 0, ts, "wgrad_kv")
    g_wbin = _wgrad(n4, [dq, dz2], N_DEV, ts, "wgrad_b_in")
    g_wbout = _wgrad(o, [dyb], 0, ts, "wgrad_b_out")

    (r_win, r_wout, r_wkv, r_wbin, r_wbout), (s_a, s_b, s_c, s_relb, s_sink) = _reduce_exchange(
        [g_win, g_wout.reshape(wout_g.shape), g_wkv.reshape(wkv_g.shape), g_wbin, g_wbout.reshape(wbout_g.shape)],
        [acc_a, acc_b, acc_c, relb, dsink])
    mine = lambda rows: lax.dynamic_slice_in_dim(rows, me * shard, shard, axis=1)
    loss = s_c[1, 0]
    weights = [a_pre_norm, a_w_in[0], a_conv_w[0], a_w_out[0], a_post_norm, kvn, w_kv, rel_bias, b_pre_norm,
               b_w_in[0], b_sinks, b_w_out[0], b_post_norm]
    grads = [mine(s_a[0:1]), r_win, mine(s_a[1:4]), r_wout, mine(s_b[2:3]), s_b[1:2], r_wkv,
             s_relb[:, :N_BUCKETS].T, s_b[0:1], r_wbin, s_sink[0:1, :N_Q_HEADS], r_wbout, s_c[0:1]]
    first = [m_a_pre_norm, m_a_w_in[0], m_a_conv_w[0], m_a_w_out[0], m_a_post_norm, m_kv_norm.reshape(1, d), m_w_kv,
             m_rel_bias, m_b_pre_norm, m_b_w_in[0], m_b_sinks, m_b_w_out[0], m_b_post_norm]
    second = [v_a_pre_norm, v_a_w_in[0], v_a_conv_w[0], v_a_w_out[0], v_a_post_norm, v_kv_norm.reshape(1, d), v_w_kv,
              v_rel_bias, v_b_pre_norm, v_b_w_in[0], v_b_sinks, v_b_w_out[0], v_b_post_norm]
    deltas, new_m, new_v = _adamw(weights, grads, first, second)

    shapes = [a_pre_norm.shape, a_w_in.shape, a_conv_w.shape, a_w_out.shape, a_post_norm.shape, kv_norm.shape,
              w_kv.shape, rel_bias.shape, b_pre_norm.shape, b_w_in.shape, b_sinks.shape, b_w_out.shape, b_post_norm.shape]
    shaped = lambda arrays: [a.reshape(s) for a, s in zip(arrays, shapes)]
    return (loss, gx.reshape(x.shape), *shaped(grads), *shaped(deltas), *shaped(new_m), *shaped(new_v))
```

```python
import functools
import math

import jax
import jax.numpy as jnp
from jax import lax
from jax.experimental import pallas as pl
from jax.experimental.pallas import tpu as pltpu

HEAD_DIM = 64
N_Q_HEADS = 16
N_KV_HEADS = 2
GROUP = N_Q_HEADS // N_KV_HEADS
BLOCK = 128
N_BUCKETS = 32
MAX_DISTANCE = 128
EPS = 1e-6
NEG_INF = -1e30
SCALE = HEAD_DIM ** -0.5

ADAM_LR = 0.001
ADAM_B1 = 0.9
ADAM_B2 = 0.999
ADAM_EPS = 1e-08
ADAM_WD = 0.01
ADAM_STEP = 10

N_DEV = 8
LANES = 128
F32 = jnp.float32
BF16 = jnp.bfloat16
MESH = pl.DeviceIdType.MESH
MIB = 1024 * 1024


def _params(semantics=None, vmem_mib=48):
    return pltpu.CompilerParams(dimension_semantics=semantics, vmem_limit_bytes=vmem_mib * MIB)


def _full(shape):
    zeros = (0,) * len(shape)
    return pl.BlockSpec(shape, lambda *_: zeros, pipeline_mode=pl.Buffered(1))


def _resident(shape):
    zeros = (0,) * len(shape)
    return pl.BlockSpec(shape, lambda *_: zeros)


def _rows(ts, cols):
    return pl.BlockSpec((ts, cols), lambda i: (i, 0))


def _dot(a, b):
    return jnp.dot(a, b, preferred_element_type=F32)


def _dot_nt(a, b):
    return lax.dot_general(a, b, (((1,), (1,)), ((), ())), preferred_element_type=F32)


def _dot_tn(a, b):
    return lax.dot_general(a, b, (((0,), (0,)), ((), ())), preferred_element_type=F32)


def _rms(xf):
    r = lax.rsqrt(jnp.mean(xf * xf, axis=-1, keepdims=True) + EPS)
    return xf * r, r


def _rms_bwd(dn, xn, r):
    return r * (dn - xn * jnp.mean(dn * xn, axis=-1, keepdims=True))


def _silu(z):
    s = jax.nn.sigmoid(z)
    return z * s, s * (1.0 + z * (1.0 - s))


def _my_index():
    return 4 * lax.axis_index("x") + 2 * lax.axis_index("y") + lax.axis_index("c")


def _all_gather(shards, out_dtypes):
    n = len(shards)

    def body(*refs):
        ins, outs = refs[:n], refs[n:2 * n]
        send_sems, recv_sems = refs[2 * n], refs[2 * n + 1]
        x, y, c = lax.axis_index("x"), lax.axis_index("y"), lax.axis_index("c")
        me, sibling = (x, y, c), (x, y, 1 - c)
        chips = [(1 - x, y), (x, 1 - y), (1 - x, 1 - y)]

        def copy(t, k, block, to):
            rows = outs[t].at[4 * block[0] + 2 * block[1] + block[2]]
            return pltpu.make_async_remote_copy(
                src_ref=rows, dst_ref=rows, send_sem=send_sems.at[t, k], recv_sem=recv_sems.at[t, k],
                device_id=to, device_id_type=MESH)

        for t in range(n):
            outs[t][pl.ds(_my_index(), 1)] = ins[t][...].astype(outs[t].dtype)[None]
        first = []
        for t in range(n):
            first.append(copy(t, 0, me, sibling))
            first += [copy(t, 1 + j, me, (*chip, c)) for j, chip in enumerate(chips)]
        for cp in first:
            cp.start()
        passed = []
        for j, chip in enumerate(chips):
            for t in range(n):
                copy(t, 1 + j, (*chip, c), me).wait_recv()
                fwd = copy(t, 4 + j, (*chip, c), sibling)
                fwd.start()
                passed.append(fwd)
        for t in range(n):
            copy(t, 0, sibling, me).wait_recv()
        for j, chip in enumerate(chips):
            for t in range(n):
                copy(t, 4 + j, (*chip, 1 - c), me).wait_recv()
        for cp in first + passed:
            cp.wait_send()

    vmem = pl.BlockSpec(memory_space=pltpu.VMEM)
    return pl.pallas_call(
        body,
        name="gather_weights",
        out_shape=[jax.ShapeDtypeStruct((N_DEV,) + s.shape, dt) for s, dt in zip(shards, out_dtypes)],
        in_specs=[vmem] * n,
        out_specs=[vmem] * n,
        scratch_shapes=[pltpu.SemaphoreType.DMA((n, 7)), pltpu.SemaphoreType.DMA((n, 7))],
        compiler_params=_params(vmem_mib=48),
    )(*shards)


def _reduce_exchange(parts, smalls):
    ns, ng = len(parts), len(smalls)
    nt = ns + ng

    def body(*refs):
        p_in, s_in = refs[:ns], refs[ns:nt]
        p_out, s_out = refs[nt:nt + ns], refs[nt + ns:2 * nt]
        p_recv, s_recv = refs[2 * nt:2 * nt + ns], refs[2 * nt + ns:3 * nt]
        send_sems, recv_sems = refs[3 * nt], refs[3 * nt + 1]
        x, y, c = lax.axis_index("x"), lax.axis_index("y"), lax.axis_index("c")
        me = 4 * x + 2 * y + c

        def peer_of(k):
            px = 1 - x if k & 4 else x
            py = 1 - y if k & 2 else y
            pc = 1 - c if k & 1 else c
            return (px, py, pc), 4 * px + 2 * py + pc

        def copy(t, k):
            peer, pidx = peer_of(k)
            if t < ns:
                src, dst = p_in[t].at[pidx], p_recv[t].at[me]
            else:
                src, dst = s_in[t - ns], s_recv[t - ns].at[me]
            return pltpu.make_async_remote_copy(
                src_ref=src, dst_ref=dst, send_sem=send_sems.at[t, k - 1], recv_sem=recv_sems.at[t, k - 1],
                device_id=peer, device_id_type=MESH)

        def landed(t, k):
            _, pidx = peer_of(k)
            buf = p_recv[t] if t < ns else s_recv[t - ns]
            return pltpu.make_async_remote_copy(
                src_ref=buf.at[pidx], dst_ref=buf.at[pidx], send_sem=send_sems.at[t, k - 1],
                recv_sem=recv_sems.at[t, k - 1], device_id=peer_of(k)[0], device_id_type=MESH)

        sent = [copy(t, k) for k in range(1, N_DEV) for t in range(nt)]
        for cp in sent:
            cp.start()
        for t in range(ns):
            p_recv[t][pl.ds(me, 1)] = p_in[t][pl.ds(me, 1)]
        for t in range(ng):
            s_recv[t][pl.ds(me, 1)] = s_in[t][...][None]
        for k in range(1, N_DEV):
            for t in range(nt):
                landed(t, k).wait_recv()
        for cp in sent:
            cp.wait_send()

        for t in range(ns):
            rows = p_out[t].shape[0]
            chunk = min(rows, 128)

            def add(i, carry, t=t, chunk=chunk):
                r0 = pl.multiple_of(i * chunk, chunk)
                acc = p_recv[t][0, pl.ds(r0, chunk), :].astype(F32)
                for d in range(1, N_DEV):
                    acc = acc + p_recv[t][d, pl.ds(r0, chunk), :].astype(F32)
                p_out[t][pl.ds(r0, chunk), :] = acc
                return carry

            lax.fori_loop(0, rows // chunk, add, 0)
        for t in range(ng):
            acc = s_recv[t][0]
            for d in range(1, N_DEV):
                acc = acc + s_recv[t][d]
            s_out[t][...] = acc

    vmem = pl.BlockSpec(memory_space=pltpu.VMEM)
    outs = pl.pallas_call(
        body,
        name="reduce_grads",
        out_shape=[jax.ShapeDtypeStruct(p.shape[1:], F32) for p in parts]
        + [jax.ShapeDtypeStruct(s.shape, F32) for s in smalls],
        in_specs=[vmem] * nt,
        out_specs=[vmem] * nt,
        scratch_shapes=[pltpu.VMEM(p.shape, p.dtype) for p in parts]
        + [pltpu.VMEM((N_DEV,) + s.shape, F32) for s in smalls]
        + [pltpu.SemaphoreType.DMA((nt, 7)), pltpu.SemaphoreType.DMA((nt, 7))],
        compiler_params=_params(vmem_mib=56),
    )(*parts, *smalls)
    return outs[:ns], outs[ns:]


def _layer_a_fwd(x2, sm, win_g, wout, ts):
    seq, d = x2.shape
    width = wout.shape[0]
    half = win_g.shape[2]
    n_half = width // half

    def body(x_ref, sm_ref, win_ref, wout_ref, h1_ref, n1_ref, proj_ref, conv_ref, y_ref, ya_ref, vprev_ref):
        @pl.when(pl.program_id(0) == 0)
        def _():
            vprev_ref[...] = jnp.zeros_like(vprev_ref)

        xf = x_ref[...]
        xn, _ = _rms(xf)
        n1 = (xn * sm_ref[0:1, :]).astype(BF16)
        n1_ref[...] = n1
        row = lax.broadcasted_iota(jnp.int32, (ts, half), 0)
        ya = jnp.zeros((ts, d), F32)
        for hh in range(n_half):
            cols = slice(hh * half, (hh + 1) * half)
            parts = []
            for part in range(4):
                j = part * n_half + hh
                pj = _dot(n1, win_ref[j])
                proj_ref[:, j * half:(j + 1) * half] = pj.astype(BF16)
                parts.append(pj)
            b, c, u, z = parts
            v = c * u
            last1, last2 = vprev_ref[7:8, cols], vprev_ref[6:7, cols]
            v1 = jnp.where(row == 0, last1, pltpu.roll(v, 1, 0))
            v2 = jnp.where(row == 0, last2, jnp.where(row == 1, last1, pltpu.roll(v, 2, 0)))
            vprev_ref[:, cols] = v[ts - 8:ts, :]
            conv = sm_ref[1:2, cols] * v2 + sm_ref[2:3, cols] * v1 + sm_ref[3:4, cols] * v
            conv_ref[:, cols] = conv.astype(BF16)
            yh = (b * conv * _silu(z)[0]).astype(BF16)
            y_ref[:, cols] = yh
            ya = ya + _dot(yh, wout_ref[cols, :])
        ya_ref[...] = ya
        h1_ref[...] = xf + _rms(ya)[0] * sm_ref[4:5, :]

    return pl.pallas_call(
        body,
        name="layer_a_fwd",
        grid=(seq // ts,),
        in_specs=[_rows(ts, d), _full(sm.shape), _full(win_g.shape), _full(wout.shape)],
        out_specs=[_rows(ts, d), _rows(ts, d), _rows(ts, 4 * width), _rows(ts, width), _rows(ts, width), _rows(ts, d)],
        out_shape=[
            jax.ShapeDtypeStruct((seq, d), F32),
            jax.ShapeDtypeStruct((seq, d), BF16),
            jax.ShapeDtypeStruct((seq, 4 * width), BF16),
            jax.ShapeDtypeStruct((seq, width), BF16),
            jax.ShapeDtypeStruct((seq, width), BF16),
            jax.ShapeDtypeStruct((seq, d), F32),
        ],
        scratch_shapes=[pltpu.VMEM((8, width), F32)],
        compiler_params=_params(("arbitrary",), 56),
    )(x2, sm, win_g, wout)


def _layer_b_in(h1, kvn, bpre, wkv, wbin_g, ts):
    seq, d = h1.shape
    kvw = wkv.shape[1]
    cw = wbin_g.shape[2]
    aw = N_Q_HEADS * HEAD_DIM
    per = aw // cw

    def body(h1_ref, kvn_ref, bpre_ref, wkv_ref, wbin_ref, n3_ref, n4_ref, kv_ref, q_ref, z2_ref):
        hn, _ = _rms(h1_ref[...])
        n3 = (hn * kvn_ref[...]).astype(BF16)
        n4 = (hn * bpre_ref[...]).astype(BF16)
        n3_ref[...] = n3
        n4_ref[...] = n4
        kv_ref[...] = _dot(n3, wkv_ref[...]).astype(BF16)
        for j in range(N_DEV):
            pj = _dot(n4, wbin_ref[j])
            if j < per:
                q_ref[:, j * cw:(j + 1) * cw] = pj.astype(BF16)
            else:
                z2_ref[:, (j - per) * cw:(j - per + 1) * cw] = pj

    return pl.pallas_call(
        body,
        name="layer_b_in",
        grid=(seq // ts,),
        in_specs=[_rows(ts, d), _full(kvn.shape), _full(bpre.shape), _full(wkv.shape), _full(wbin_g.shape)],
        out_specs=[_rows(ts, d), _rows(ts, d), _rows(ts, kvw), _rows(ts, aw), _rows(ts, aw)],
        out_shape=[
            jax.ShapeDtypeStruct((seq, d), BF16),
            jax.ShapeDtypeStruct((seq, d), BF16),
            jax.ShapeDtypeStruct((seq, kvw), BF16),
            jax.ShapeDtypeStruct((seq, aw), BF16),
            jax.ShapeDtypeStruct((seq, aw), F32),
        ],
        compiler_params=_params(("parallel",), 48),
    )(h1, kvn, bpre, wkv, wbin_g)


N_PAIRS = N_Q_HEADS // 2
BAND = 2 * BLOCK


def _bias_table(rel_bias, bucket_t, in_window_t):
    def body(rb_ref, bucket_ref, win_ref, out_ref):
        bk = bucket_ref[...]
        inside = win_ref[...] != 0
        has_prev = lax.broadcasted_iota(jnp.int32, bk.shape, 0) >= BLOCK
        for h in range(N_Q_HEADS):
            acc = jnp.full(bk.shape, NEG_INF, F32)
            for b in range(N_BUCKETS):
                acc = jnp.where(jnp.logical_and(bk == b, inside), rb_ref[b, h], acc)
            cols = slice((h % 2) * BLOCK, (h % 2 + 1) * BLOCK)
            out_ref[1, h // 2, :, cols] = acc
            out_ref[0, h // 2, :, cols] = jnp.where(has_prev, acc, NEG_INF)

    vmem = pl.BlockSpec(memory_space=pltpu.VMEM)
    return pl.pallas_call(
        body,
        name="bias_table",
        in_specs=[pl.BlockSpec(memory_space=pltpu.SMEM), vmem, vmem],
        out_specs=vmem,
        out_shape=jax.ShapeDtypeStruct((2, N_PAIRS, BAND, 2 * BLOCK), F32),
    )(rel_bias, bucket_t, in_window_t)


def _bias_spec(biasm):
    return pl.BlockSpec((None,) + biasm.shape[1:], lambda i: (jnp.minimum(i, 1), 0, 0, 0))


def _banded_kv(kvp_ref, kvc_ref):
    kvp = kvp_ref[...].astype(F32)
    kvc = kvc_ref[...].astype(F32)
    kw = N_KV_HEADS * HEAD_DIM
    out = []
    for full in (jnp.concatenate([kvp[:, :kw], kvc[:, :kw]], axis=0), jnp.concatenate([kvp[:, kw:], kvc[:, kw:]], axis=0)):
        lo = lax.broadcasted_iota(jnp.int32, full.shape, 1) < HEAD_DIM
        rolled = pltpu.roll(full, HEAD_DIM, 1)
        x2 = [jnp.where(lo, full, rolled).astype(BF16), jnp.where(lo, rolled, full).astype(BF16)]
        ft = full.T
        x2t = [jnp.concatenate([ft[kh * HEAD_DIM:(kh + 1) * HEAD_DIM]] * 2, axis=0).astype(BF16) for kh in range(N_KV_HEADS)]
        out += [x2, x2t]
    return out


def _pair_rows(ref, m, scale=None):
    both = ref[:, m * LANES:(m + 1) * LANES].astype(F32)
    if scale is not None:
        both = both * scale
    lo = lax.broadcasted_iota(jnp.int32, both.shape, 1) < HEAD_DIM
    zero = jnp.zeros_like(both)
    return jnp.concatenate([jnp.where(lo, both, zero), jnp.where(lo, zero, both)], axis=0).astype(BF16)


def _pair_cols(res_t):
    top = lax.broadcasted_iota(jnp.int32, (LANES, BLOCK), 0) < HEAD_DIM
    return jnp.where(top, res_t[:, :BLOCK], res_t[:, BLOCK:]).T


def _sink_row(sink_ref, m):
    first = lax.broadcasted_iota(jnp.int32, (1, 2 * BLOCK), 1) < BLOCK
    return jnp.where(first, sink_ref[0, 2 * m], sink_ref[0, 2 * m + 1])


def _probs_t(k2, qpair, bias, sink):
    return _softmax_t(_dot_nt(k2, qpair) + bias, sink)


def _softmax_t(logits, sink):
    mx = jnp.maximum(jnp.max(logits, axis=0, keepdims=True), sink)
    p = jnp.exp(logits - mx)
    sink_p = jnp.exp(sink - mx)
    inv = 1.0 / (jnp.sum(p, axis=0, keepdims=True) + sink_p)
    return p * inv, sink_p * inv


def _attn_fwd(q, kv, z2, biasm, sinks):
    seq, aw = q.shape
    kvw = kv.shape[1]
    nb = seq // BLOCK

    def body(sink_ref, q_ref, kvc_ref, kvp_ref, z2_ref, bias_ref, attn_ref, o_ref, acc_ref):
        k2, _, _, v2t = _banded_kv(kvp_ref, kvc_ref)
        kv_of = lambda m: (2 * m) // GROUP
        logits, probs = {}, {}
        for step in range(N_PAIRS + 2):
            if step < N_PAIRS:
                logits[step] = _dot_nt(k2[kv_of(step)], _pair_rows(q_ref, step, SCALE)) + bias_ref[step]
            m = step - 1
            if 0 <= m < N_PAIRS:
                probs[m] = _softmax_t(logits.pop(m), _sink_row(sink_ref, m))[0].astype(BF16)
            m = step - 2
            if 0 <= m < N_PAIRS:
                acc_ref[:, m * LANES:(m + 1) * LANES] = _pair_cols(_dot(v2t[kv_of(m)], probs.pop(m)))
        attn = acc_ref[...]
        attn_ref[...] = attn.astype(BF16)
        o_ref[...] = (attn * _silu(z2_ref[...])[0]).astype(BF16)

    blk = lambda w: pl.BlockSpec((BLOCK, w), lambda i: (i, 0))
    return pl.pallas_call(
        body,
        name="attn_fwd",
        grid=(nb,),
        in_specs=[
            pl.BlockSpec(memory_space=pltpu.SMEM),
            blk(aw),
            blk(kvw),
            pl.BlockSpec((BLOCK, kvw), lambda i: (jnp.maximum(i - 1, 0), 0)),
            blk(aw),
            _bias_spec(biasm),
        ],
        out_specs=[blk(aw), blk(aw)],
        out_shape=[jax.ShapeDtypeStruct((seq, aw), BF16), jax.ShapeDtypeStruct((seq, aw), BF16)],
        scratch_shapes=[pltpu.VMEM((BLOCK, aw), F32)],
        compiler_params=_params(("arbitrary",), 32),
    )(sinks, q, kv, kv, z2, biasm)


def _layer_b_out(o, attn, z2, h1, target, wbout, bpost, ts):
    seq, d = h1.shape
    aw = o.shape[1]

    def body(o_ref, attn_ref, z2_ref, h1_ref, tgt_ref, w_ref, g_ref, dh2_ref, dyb_ref, dattn_ref, dz2_ref, acc_ref):
        @pl.when(pl.program_id(0) == 0)
        def _():
            acc_ref[...] = jnp.zeros_like(acc_ref)

        w = w_ref[...]
        yb = _dot(o_ref[...], w)
        ybn, r = _rms(yb)
        g = g_ref[...]
        diff = h1_ref[...] + ybn * g - tgt_ref[...]
        dh2 = diff * (1.0 / d)
        dh2_ref[...] = dh2
        acc_ref[0:1, :] += jnp.sum(dh2 * ybn, axis=0, keepdims=True)
        tok = jnp.mean(diff * diff, axis=-1, keepdims=True)
        acc_ref[1:2, :] += 0.5 * jnp.sum(tok, axis=0, keepdims=True)
        dyb = _rms_bwd(dh2 * g, ybn, r).astype(BF16)
        dyb_ref[...] = dyb
        do = _dot_nt(dyb, w)
        sz, dsz = _silu(z2_ref[...])
        dattn_ref[...] = (do * sz).astype(BF16)
        dz2_ref[...] = (do * attn_ref[...].astype(F32) * dsz).astype(BF16)

    return pl.pallas_call(
        body,
        name="layer_b_out",
        grid=(seq // ts,),
        in_specs=[_rows(ts, aw), _rows(ts, aw), _rows(ts, aw), _rows(ts, d), _rows(ts, d), _full(wbout.shape), _full(bpost.shape)],
        out_specs=[_rows(ts, d), _rows(ts, d), _rows(ts, aw), _rows(ts, aw), _resident((8, d))],
        out_shape=[
            jax.ShapeDtypeStruct((seq, d), F32),
            jax.ShapeDtypeStruct((seq, d), BF16),
            jax.ShapeDtypeStruct((seq, aw), BF16),
            jax.ShapeDtypeStruct((seq, aw), BF16),
            jax.ShapeDtypeStruct((8, d), F32),
        ],
        compiler_params=_params(("arbitrary",), 48),
    )(o, attn, z2, h1, target, wbout, bpost)


def _attn_bwd(q, kv, dattn, biasm, sinks):
    seq, aw = q.shape
    kvw = kv.shape[1]
    kw = N_KV_HEADS * HEAD_DIM
    nb = seq // BLOCK
    pairs_per_kv = N_PAIRS // N_KV_HEADS

    def body(sink_ref, q_ref, kvc_ref, kvp_ref, da_ref, bias_ref, dq_ref, dkv_ref, dssum_ref, dsink_ref,
             carry_ref, qs_ref, dos_ref, dst_ref, pt_ref):
        i = pl.program_id(0)

        @pl.when(i == 0)
        def _():
            dssum_ref[...] = jnp.zeros_like(dssum_ref)
            dsink_ref[...] = jnp.zeros_like(dsink_ref)
            carry_ref[...] = jnp.zeros_like(carry_ref)

        @pl.when(i < nb)
        def _():
            lo = lax.broadcasted_iota(jnp.int32, (BAND, LANES), 1) < HEAD_DIM
            head_lane = lax.broadcasted_iota(jnp.int32, (1, LANES), 1)
            k2, k2t, v2, _ = _banded_kv(kvp_ref, kvc_ref)
            dsink = jnp.zeros((1, LANES), F32)
            folded = []
            logits, dps, dsbs = {}, {}, {}
            for step in range(N_PAIRS + 2):
                if step < N_PAIRS:
                    kh, rows = step // pairs_per_kv, slice((step % pairs_per_kv) * BAND, (step % pairs_per_kv + 1) * BAND)
                    qpair = _pair_rows(q_ref, step, SCALE)
                    dopair = _pair_rows(da_ref, step)
                    qs_ref[kh, rows, :] = qpair
                    dos_ref[kh, rows, :] = dopair
                    logits[step] = _dot_nt(k2[kh], qpair) + bias_ref[step]
                    dps[step] = _dot_nt(v2[kh], dopair)
                m = step - 1
                if 0 <= m < N_PAIRS:
                    kh, rows = m // pairs_per_kv, slice((m % pairs_per_kv) * BAND, (m % pairs_per_kv + 1) * BAND)
                    pn, sink_p = _softmax_t(logits.pop(m), _sink_row(sink_ref, m))
                    dp = dps.pop(m)
                    delta = jnp.sum(pn * dp, axis=0, keepdims=True)
                    ds = pn * (dp - delta)
                    dssum_ref[m] += ds
                    sink_term = sink_p * delta
                    for e in range(2):
                        total = jnp.sum(sink_term[:, e * BLOCK:(e + 1) * BLOCK], axis=1, keepdims=True)
                        dsink = dsink - jnp.where(head_lane == 2 * m + e, total, 0.0)
                    dsbs[m] = ds.astype(BF16)
                    dst_ref[kh, :, rows] = dsbs[m]
                    pt_ref[kh, :, rows] = pn.astype(BF16)
                m = step - 2
                if 0 <= m < N_PAIRS:
                    kh = m // pairs_per_kv
                    dq_ref[:, m * LANES:(m + 1) * LANES] = (_pair_cols(_dot(k2t[kh], dsbs.pop(m))) * SCALE).astype(BF16)
                    if m % pairs_per_kv == pairs_per_kv - 1:
                        for lhs_ref, rhs_ref in ((dst_ref, qs_ref), (pt_ref, dos_ref)):
                            acc = _dot(lhs_ref[kh], rhs_ref[kh])
                            folded.append(acc + pltpu.roll(acc, HEAD_DIM, 1))
            dsink_ref[0:1, :] += dsink
            dk = jnp.where(lo, folded[0], folded[2])
            dv = jnp.where(lo, folded[1], folded[3])

            @pl.when(i > 0)
            def _():
                dkv_ref[:, :kw] = (carry_ref[:, :kw] + dk[:BLOCK]).astype(BF16)
                dkv_ref[:, kw:] = (carry_ref[:, kw:] + dv[:BLOCK]).astype(BF16)

            carry_ref[:, :kw] = dk[BLOCK:]
            carry_ref[:, kw:] = dv[BLOCK:]

        @pl.when(i == nb)
        def _():
            dkv_ref[...] = carry_ref[...].astype(BF16)

    last = nb - 1
    blk = lambda w: pl.BlockSpec((BLOCK, w), lambda i: (jnp.minimum(i, last), 0))
    return pl.pallas_call(
        body,
        name="attn_bwd",
        grid=(nb + 1,),
        in_specs=[
            pl.BlockSpec(memory_space=pltpu.SMEM),
            blk(aw),
            blk(kvw),
            pl.BlockSpec((BLOCK, kvw), lambda i: (jnp.clip(i - 1, 0, last), 0)),
            blk(aw),
            _bias_spec(biasm),
        ],
        out_specs=[
            blk(aw),
            pl.BlockSpec((BLOCK, kvw), lambda i: (jnp.maximum(i - 1, 0), 0)),
            _resident(biasm.shape[1:]),
            _resident((8, LANES)),
        ],
        out_shape=[
            jax.ShapeDtypeStruct((seq, aw), BF16),
            jax.ShapeDtypeStruct((seq, kvw), BF16),
            jax.ShapeDtypeStruct(biasm.shape[1:], F32),
            jax.ShapeDtypeStruct((8, LANES), F32),
        ],
        scratch_shapes=[
            pltpu.VMEM((BLOCK, kvw), F32),
            pltpu.VMEM((N_KV_HEADS, pairs_per_kv * BAND, LANES), BF16),
            pltpu.VMEM((N_KV_HEADS, pairs_per_kv * BAND, LANES), BF16),
            pltpu.VMEM((N_KV_HEADS, BAND, pairs_per_kv * BAND), BF16),
            pltpu.VMEM((N_KV_HEADS, BAND, pairs_per_kv * BAND), BF16),
        ],
        compiler_params=_params(("arbitrary",), 40),
    )(sinks, q, kv, kv, dattn, biasm)


def _relbias_grad(dssum2, onehot, chunk):
    heads, n = dssum2.shape

    def body(a_ref, oh_ref, out_ref):
        @pl.when(pl.program_id(0) == 0)
        def _():
            out_ref[...] = jnp.zeros_like(out_ref)

        a = a_ref[...]
        hi = a.astype(BF16)
        lo = (a - hi.astype(F32)).astype(BF16)
        out_ref[...] += _dot(hi, oh_ref[...]) + _dot(lo, oh_ref[...])

    return pl.pallas_call(
        body,
        name="relbias_grad",
        grid=(n // chunk,),
        in_specs=[pl.BlockSpec((heads, chunk), lambda i: (0, i)), pl.BlockSpec((chunk, LANES), lambda i: (i, 0))],
        out_specs=_resident((heads, LANES)),
        out_shape=jax.ShapeDtypeStruct((heads, LANES), F32),
        compiler_params=_params(("arbitrary",), 32),
    )(dssum2, onehot)


def _layer_b_in_bwd(dh2, dq, dz2, dkv, h1, ya, wbin_g, wkv, kvn, bpre, sm, ts):
    seq, d = h1.shape
    aw = dq.shape[1]
    kvw = dkv.shape[1]
    cw = wbin_g.shape[2]
    per = aw // cw

    def body(dh2_ref, dq_ref, dz2_ref, dkv_ref, h1_ref, ya_ref, wbin_ref, wkv_ref, kvn_ref, bpre_ref, sm_ref,
             dh1_ref, dya_ref, acc_ref):
        @pl.when(pl.program_id(0) == 0)
        def _():
            acc_ref[...] = jnp.zeros_like(acc_ref)

        dn4 = jnp.zeros((ts, d), F32)
        for j in range(N_DEV):
            src = dq_ref if j < per else dz2_ref
            jj = j % per
            dn4 = dn4 + _dot_nt(src[:, jj * cw:(jj + 1) * cw], wbin_ref[j])
        dn3 = _dot_nt(dkv_ref[...], wkv_ref[...])
        hn, r = _rms(h1_ref[...])
        acc_ref[0:1, :] += jnp.sum(dn4 * hn, axis=0, keepdims=True)
        acc_ref[1:2, :] += jnp.sum(dn3 * hn, axis=0, keepdims=True)
        dh1 = dh2_ref[...] + _rms_bwd(dn4 * bpre_ref[...] + dn3 * kvn_ref[...], hn, r)
        dh1_ref[...] = dh1
        yan, r2 = _rms(ya_ref[...])
        acc_ref[2:3, :] += jnp.sum(dh1 * yan, axis=0, keepdims=True)
        dya_ref[...] = _rms_bwd(dh1 * sm_ref[4:5, :], yan, r2).astype(BF16)

    return pl.pallas_call(
        body,
        name="layer_b_in_bwd",
        grid=(seq // ts,),
        in_specs=[_rows(ts, d), _rows(ts, aw), _rows(ts, aw), _rows(ts, kvw), _rows(ts, d), _rows(ts, d),
                  _full(wbin_g.shape), _full(wkv.shape), _full(kvn.shape), _full(bpre.shape), _full(sm.shape)],
        out_specs=[_rows(ts, d), _rows(ts, d), _resident((8, d))],
        out_shape=[jax.ShapeDtypeStruct((seq, d), F32), jax.ShapeDtypeStruct((seq, d), BF16),
                   jax.ShapeDtypeStruct((8, d), F32)],
        compiler_params=_params(("arbitrary",), 48),
    )(dh2, dq, dz2, dkv, h1, ya, wbin_g, wkv, kvn, bpre, sm)


def _layer_a_bwd(dya, proj, conv, dh1, x2, wout, win_g, sm, ts):
    seq, d = x2.shape
    width = wout.shape[0]
    half = win_g.shape[2]
    n_half = width // half
    nt = seq // ts

    def body(dya_ref, proj_ref, conv_ref, dh1_ref, x_ref, wout_ref, win_ref, sm_ref, dproj_ref, gx_ref, acc_ref,
             dnext_ref):
        @pl.when(pl.program_id(0) == 0)
        def _():
            acc_ref[...] = jnp.zeros_like(acc_ref)
            dnext_ref[...] = jnp.zeros_like(dnext_ref)

        dy = _dot_nt(dya_ref[...], wout_ref[...])
        row = lax.broadcasted_iota(jnp.int32, (ts, half), 0)
        dn1 = jnp.zeros((ts, d), F32)
        for hh in range(n_half):
            cols = slice(hh * half, (hh + 1) * half)
            b, c, u, z = [proj_ref[:, (part * n_half + hh) * half:(part * n_half + hh + 1) * half].astype(F32)
                          for part in range(4)]
            cv = conv_ref[:, cols].astype(F32)
            dyh = dy[:, cols]
            sz, dsz = _silu(z)
            dconv = dyh * b * sz
            grads = [dyh * cv * sz, None, None, dyh * b * cv * dsz]
            next0, next1 = dnext_ref[0:1, cols], dnext_ref[1:2, cols]
            dc1 = jnp.where(row == ts - 1, next0, pltpu.roll(dconv, ts - 1, 0))
            dc2 = jnp.where(row == ts - 1, next1, jnp.where(row == ts - 2, next0, pltpu.roll(dconv, ts - 2, 0)))
            dnext_ref[:, cols] = dconv[0:8, :]
            v = c * u
            acc_ref[1:2, cols] += jnp.sum(dc2 * v, axis=0, keepdims=True)
            acc_ref[2:3, cols] += jnp.sum(dc1 * v, axis=0, keepdims=True)
            acc_ref[3:4, cols] += jnp.sum(dconv * v, axis=0, keepdims=True)
            dv = sm_ref[3:4, cols] * dconv + sm_ref[2:3, cols] * dc1 + sm_ref[1:2, cols] * dc2
            grads[1] = dv * u
            grads[2] = dv * c
            for part in range(4):
                j = part * n_half + hh
                gj = grads[part].astype(BF16)
                dproj_ref[:, j * half:(j + 1) * half] = gj
                dn1 = dn1 + _dot_nt(gj, win_ref[j])
        xn, r = _rms(x_ref[...])
        acc_ref[0:1, :] += jnp.sum(dn1 * xn, axis=0, keepdims=True)
        gx_ref[...] = dh1_ref[...] + _rms_bwd(dn1 * sm_ref[0:1, :], xn, r)

    rev = lambda w: pl.BlockSpec((ts, w), lambda i: (nt - 1 - i, 0))
    return pl.pallas_call(
        body,
        name="layer_a_bwd",
        grid=(nt,),
        in_specs=[rev(d), rev(4 * width), rev(width), rev(d), rev(d), _full(wout.shape), _full(win_g.shape), _full(sm.shape)],
        out_specs=[rev(4 * width), rev(d), _resident((8, d))],
        out_shape=[jax.ShapeDtypeStruct((seq, 4 * width), BF16), jax.ShapeDtypeStruct((seq, d), F32),
                   jax.ShapeDtypeStruct((8, d), F32)],
        scratch_shapes=[pltpu.VMEM((8, width), F32)],
        compiler_params=_params(("arbitrary",), 56),
    )(dya, proj, conv, dh1, x2, wout, win_g, sm)


def _wgrad(a, bs, n_slots, ts, name):
    seq, k = a.shape
    nb_in = len(bs)
    n_each = bs[0].shape[1]
    n = nb_in * n_each
    bn = min(n_each, 1024)
    per_in = n_each // bn
    n_blocks = nb_in * per_in
    ns = seq // ts

    def b_spec(idx):
        def index(j, s):
            mine = j // per_in == idx
            row = jnp.where(mine, s, jnp.where(j // per_in > idx, ns - 1, 0))
            return (row, jnp.where(mine, j % per_in, jnp.where(j // per_in > idx, per_in - 1, 0)))
        return pl.BlockSpec((ts, bn), index)

    if n_slots:
        sw = n // n_slots
        spb = bn // sw
        out_shape = jax.ShapeDtypeStruct((n_slots, k, sw), BF16)
        out_spec = pl.BlockSpec((spb, k, sw), lambda j, s: (j, 0, 0))
    else:
        out_shape = jax.ShapeDtypeStruct((k, n), BF16)
        out_spec = pl.BlockSpec((k, bn), lambda j, s: (0, j))

    def body(a_ref, *refs):
        b_refs, o_ref, acc_ref = refs[:nb_in], refs[nb_in], refs[nb_in + 1]
        j, s = pl.program_id(0), pl.program_id(1)

        @pl.when(s == 0)
        def _():
            acc_ref[...] = jnp.zeros_like(acc_ref)

        for idx in range(nb_in):
            @pl.when(j // per_in == idx)
            def _(idx=idx):
                acc_ref[...] += _dot_tn(a_ref[...], b_refs[idx][...])

        @pl.when(s == ns - 1)
        def _():
            if n_slots:
                for e in range(spb):
                    o_ref[e] = acc_ref[:, e * sw:(e + 1) * sw].astype(BF16)
            else:
                o_ref[...] = acc_ref[...].astype(BF16)

    return pl.pallas_call(
        body,
        name=name,
        grid=(n_blocks, ns),
        in_specs=[pl.BlockSpec((ts, k), lambda j, s: (s, 0))] + [b_spec(idx) for idx in range(nb_in)],
        out_specs=out_spec,
        out_shape=out_shape,
        scratch_shapes=[pltpu.VMEM((k, bn), F32)],
        compiler_params=_params(("arbitrary", "arbitrary"), 48),
    )(a, *bs)


def _adamw(ws, gs, ms, vs):
    n = len(ws)

    def step(w, g, m, v):
        m = ADAM_B1 * m + (1.0 - ADAM_B1) * g
        v = ADAM_B2 * v + (1.0 - ADAM_B2) * jnp.square(g)
        m_hat = m / (1.0 - ADAM_B1 ** ADAM_STEP)
        v_hat = v / (1.0 - ADAM_B2 ** ADAM_STEP)
        return -ADAM_LR * (m_hat / (jnp.sqrt(v_hat) + ADAM_EPS) + ADAM_WD * w), m, v

    def body(*refs):
        w_refs, g_refs, m_refs, v_refs = (refs[k * n:(k + 1) * n] for k in range(4))
        d_refs, nm_refs, nv_refs = (refs[(4 + k) * n:(5 + k) * n] for k in range(3))
        for t in range(n):
            rows = w_refs[t].shape[0]
            if rows <= 128:
                d_refs[t][...], nm_refs[t][...], nv_refs[t][...] = step(
                    w_refs[t][...], g_refs[t][...], m_refs[t][...], v_refs[t][...])
                continue
            chunk = 128

            def one(i, carry, t=t):
                r = pl.ds(pl.multiple_of(i * chunk, chunk), chunk)
                d_refs[t][r, :], nm_refs[t][r, :], nv_refs[t][r, :] = step(
                    w_refs[t][r, :], g_refs[t][r, :], m_refs[t][r, :], v_refs[t][r, :])
                return carry

            lax.fori_loop(0, rows // chunk, one, 0)

    vmem = pl.BlockSpec(memory_space=pltpu.VMEM)
    outs = pl.pallas_call(
        body,
        name="adamw",
        in_specs=[vmem] * (4 * n),
        out_specs=[vmem] * (3 * n),
        out_shape=[jax.ShapeDtypeStruct(w.shape, F32) for w in ws] * 3,
        compiler_params=_params(vmem_mib=56),
    )(*ws, *gs, *ms, *vs)
    return outs[:n], outs[n:2 * n], outs[2 * n:]


def _band_structure():
    q_loc = jnp.arange(BLOCK, dtype=jnp.int32)[:, None]
    s_loc = jnp.arange(2 * BLOCK, dtype=jnp.int32)[None, :]
    dist = q_loc + BLOCK - s_loc
    in_window = (dist >= 0) & (dist < BLOCK)
    dd = jnp.maximum(dist, 0)
    max_exact = N_BUCKETS // 2
    large = max_exact + (jnp.log(jnp.maximum(dd, 1).astype(F32) / max_exact) / math.log(MAX_DISTANCE / max_exact)
                         * (N_BUCKETS - max_exact)).astype(jnp.int32)
    bucket = jnp.where(dd < max_exact, dd, jnp.minimum(large, N_BUCKETS - 1))
    onehot = (bucket.reshape(-1, 1) == jnp.arange(LANES, dtype=jnp.int32)[None, :]).astype(BF16)
    return bucket, in_window.astype(jnp.int32), onehot


def _place_rows(a, row, rows=8):
    return jnp.pad(a, ((row, rows - row - a.shape[0]), (0, 0)))


def kernel(x, a_pre_norm, a_w_in, a_conv_w, a_w_out, a_post_norm, kv_norm, w_kv, rel_bias, b_pre_norm, b_w_in, b_sinks, b_w_out, b_post_norm, loss_target, m_a_pre_norm, m_a_w_in, m_a_conv_w, m_a_w_out, m_a_post_norm, m_kv_norm, m_w_kv, m_rel_bias, m_b_pre_norm, m_b_w_in, m_b_sinks, m_b_w_out, m_b_post_norm, v_a_pre_norm, v_a_w_in, v_a_conv_w, v_a_w_out, v_a_post_norm, v_kv_norm, v_w_kv, v_rel_bias, v_b_pre_norm, v_b_w_in, v_b_sinks, v_b_w_out, v_b_post_norm):
    seq, d = x.shape[1], x.shape[2]
    x2 = x.reshape(seq, d)
    target = loss_target.reshape(seq, d)
    shard = a_pre_norm.shape[1]
    me = _my_index()
    ts_a = min(seq, 512)
    ts = min(seq, 512)
    ts_w = min(seq, 2048)

    small = _place_rows(a_pre_norm, 0) + _place_rows(a_conv_w[0], 1) + _place_rows(a_post_norm, 4)
    win_g, wout_g, wkv_g, wbin_g, wbout_g, small_g = _all_gather(
        [a_w_in[0], a_w_out[0], w_kv, b_w_in[0], b_w_out[0], small], [BF16] * 5 + [F32])
    wout = wout_g.reshape(-1, wout_g.shape[2])
    wkv = wkv_g.reshape(-1, wkv_g.shape[2])
    wbout = wbout_g.reshape(-1, wbout_g.shape[2])
    sm = small_g.transpose(1, 0, 2).reshape(8, N_DEV * shard)
    kvn = kv_norm.reshape(1, d)

    h1, n1, proj, conv, y, ya = _layer_a_fwd(x2, sm, win_g, wout, ts_a)
    n3, n4, kv, q, z2 = _layer_b_in(h1, kvn, b_pre_norm, wkv, wbin_g, ts)
    bucket, in_window, onehot = _band_structure()
    biasm = _bias_table(rel_bias, bucket.T, in_window.T)
    attn, o = _attn_fwd(q, kv, z2, biasm, b_sinks)
    dh2, dyb, dattn, dz2, acc_c = _layer_b_out(o, attn, z2, h1, target, wbout, b_post_norm, ts)

    dq, dkv, dssum, dsink = _attn_bwd(q, kv, dattn, biasm, b_sinks)
    by_head = dssum.reshape(N_PAIRS, BAND, 2, BLOCK).transpose(0, 2, 3, 1)
    relb = _relbias_grad(by_head.reshape(N_Q_HEADS, -1), onehot, 4096)
    dh1, dya, acc_b = _layer_b_in_bwd(dh2, dq, dz2, dkv, h1, ya, wbin_g, wkv, kvn, b_pre_norm, sm, ts)
    dproj, gx, acc_a = _layer_a_bwd(dya, proj, conv, dh1, x2, wout, win_g, sm, ts_a)
    g_win = _wgrad(n1, [dproj], N_DEV, ts_w, "wgrad_a_in")
    g_wout = _wgrad(y, [dya], 0, ts_w, "wgrad_a_out")
    g_wkv = _wgrad(n3, [dkv], 0, ts_w, "wgrad_kv")
    g_wbin = _wgrad(n4, [dq, dz2], N_DEV, ts_w, "wgrad_b_in")
    g_wbout = _wgrad(o, [dyb], 0, ts_w, "wgrad_b_out")

    (r_win, r_wout, r_wkv, r_wbin, r_wbout), (s_a, s_b, s_c, s_relb, s_sink) = _reduce_exchange(
        [g_win, g_wout.reshape(wout_g.shape), g_wkv.reshape(wkv_g.shape), g_wbin, g_wbout.reshape(wbout_g.shape)],
        [acc_a, acc_b, acc_c, relb, dsink])
    mine = lambda rows: lax.dynamic_slice_in_dim(rows, me * shard, shard, axis=1)
    loss = s_c[1, 0]
    weights = [a_pre_norm, a_w_in[0], a_conv_w[0], a_w_out[0], a_post_norm, kvn, w_kv, rel_bias, b_pre_norm,
               b_w_in[0], b_sinks, b_w_out[0], b_post_norm]
    grads = [mine(s_a[0:1]), r_win, mine(s_a[1:4]), r_wout, mine(s_b[2:3]), s_b[1:2], r_wkv,
             s_relb[:, :N_BUCKETS].T, s_b[0:1], r_wbin, s_sink[0:1, :N_Q_HEADS], r_wbout, s_c[0:1]]
    first = [m_a_pre_norm, m_a_w_in[0], m_a_conv_w[0], m_a_w_out[0], m_a_post_norm, m_kv_norm.reshape(1, d), m_w_kv,
             m_rel_bias, m_b_pre_norm, m_b_w_in[0], m_b_sinks, m_b_w_out[0], m_b_post_norm]
    second = [v_a_pre_norm, v_a_w_in[0], v_a_conv_w[0], v_a_w_out[0], v_a_post_norm, v_kv_norm.reshape(1, d), v_w_kv,
              v_rel_bias, v_b_pre_norm, v_b_w_in[0], v_b_sinks, v_b_w_out[0], v_b_post_norm]
    deltas, new_m, new_v = _adamw(weights, grads, first, second)

    shapes = [a_pre_norm.shape, a_w_in.shape, a_conv_w.shape, a_w_out.shape, a_post_norm.shape, kv_norm.shape,
              w_kv.shape, rel_bias.shape, b_pre_norm.shape, b_w_in.shape, b_sinks.shape, b_w_out.shape, b_post_norm.shape]
    shaped = lambda arrays: [a.reshape(s) for a, s in zip(arrays, shapes)]
    return (loss, gx.reshape(x.shape), *shaped(grads), *shaped(deltas), *shaped(new_m), *shaped(new_v))
```

```python
import functools
import math

import jax
import jax.numpy as jnp
from jax import lax
from jax.experimental import pallas as pl
from jax.experimental.pallas import tpu as pltpu

HEAD_DIM = 64
N_Q_HEADS = 16
N_KV_HEADS = 2
GROUP = N_Q_HEADS // N_KV_HEADS
BLOCK = 128
N_BUCKETS = 32
MAX_DISTANCE = 128
EPS = 1e-6
NEG_INF = -1e30
SCALE = HEAD_DIM ** -0.5

ADAM_LR = 0.001
ADAM_B1 = 0.9
ADAM_B2 = 0.999
ADAM_EPS = 1e-08
ADAM_WD = 0.01
ADAM_STEP = 10

N_DEV = 8
LANES = 128
F32 = jnp.float32
BF16 = jnp.bfloat16
MESH = pl.DeviceIdType.MESH
MIB = 1024 * 1024


def _params(semantics=None, vmem_mib=48):
    return pltpu.CompilerParams(dimension_semantics=semantics, vmem_limit_bytes=vmem_mib * MIB)


def _full(shape):
    zeros = (0,) * len(shape)
    return pl.BlockSpec(shape, lambda *_: zeros, pipeline_mode=pl.Buffered(1))


def _resident(shape):
    zeros = (0,) * len(shape)
    return pl.BlockSpec(shape, lambda *_: zeros)


def _rows(ts, cols):
    return pl.BlockSpec((ts, cols), lambda i: (i, 0))


def _dot(a, b):
    return jnp.dot(a, b, preferred_element_type=F32)


def _dot_nt(a, b):
    return lax.dot_general(a, b, (((1,), (1,)), ((), ())), preferred_element_type=F32)


def _dot_tn(a, b):
    return lax.dot_general(a, b, (((0,), (0,)), ((), ())), preferred_element_type=F32)


def _rms(xf):
    r = lax.rsqrt(jnp.mean(xf * xf, axis=-1, keepdims=True) + EPS)
    return xf * r, r


def _rms_bwd(dn, xn, r):
    return r * (dn - xn * jnp.mean(dn * xn, axis=-1, keepdims=True))


def _silu(z):
    s = jax.nn.sigmoid(z)
    return z * s, s * (1.0 + z * (1.0 - s))


def _my_index():
    return 4 * lax.axis_index("x") + 2 * lax.axis_index("y") + lax.axis_index("c")


def _all_gather(shards, out_dtypes):
    n = len(shards)

    def body(*refs):
        ins, outs = refs[:n], refs[n:2 * n]
        send_sems, recv_sems = refs[2 * n], refs[2 * n + 1]
        x, y, c = lax.axis_index("x"), lax.axis_index("y"), lax.axis_index("c")
        me, sibling = (x, y, c), (x, y, 1 - c)
        chips = [(1 - x, y), (x, 1 - y), (1 - x, 1 - y)]

        def copy(t, k, block, to):
            rows = outs[t].at[4 * block[0] + 2 * block[1] + block[2]]
            return pltpu.make_async_remote_copy(
                src_ref=rows, dst_ref=rows, send_sem=send_sems.at[t, k], recv_sem=recv_sems.at[t, k],
                device_id=to, device_id_type=MESH)

        for t in range(n):
            outs[t][pl.ds(_my_index(), 1)] = ins[t][...].astype(outs[t].dtype)[None]
        first = []
        for t in range(n):
            first.append(copy(t, 0, me, sibling))
            first += [copy(t, 1 + j, me, (*chip, c)) for j, chip in enumerate(chips)]
        for cp in first:
            cp.start()
        passed = []
        for j, chip in enumerate(chips):
            for t in range(n):
                copy(t, 1 + j, (*chip, c), me).wait_recv()
                fwd = copy(t, 4 + j, (*chip, c), sibling)
                fwd.start()
                passed.append(fwd)
        for t in range(n):
            copy(t, 0, sibling, me).wait_recv()
        for j, chip in enumerate(chips):
            for t in range(n):
                copy(t, 4 + j, (*chip, 1 - c), me).wait_recv()
        for cp in first + passed:
            cp.wait_send()

    vmem = pl.BlockSpec(memory_space=pltpu.VMEM)
    return pl.pallas_call(
        body,
        name="gather_weights",
        out_shape=[jax.ShapeDtypeStruct((N_DEV,) + s.shape, dt) for s, dt in zip(shards, out_dtypes)],
        in_specs=[vmem] * n,
        out_specs=[vmem] * n,
        scratch_shapes=[pltpu.SemaphoreType.DMA((n, 7)), pltpu.SemaphoreType.DMA((n, 7))],
        compiler_params=_params(vmem_mib=48),
    )(*shards)


def _peer(k):
    x, y, c = lax.axis_index("x"), lax.axis_index("y"), lax.axis_index("c")
    px = 1 - x if k & 4 else x
    py = 1 - y if k & 2 else y
    pc = 1 - c if k & 1 else c
    return (px, py, pc), 4 * px + 2 * py + pc


def _exchange(srcs, dsts, send_sems, recv_sems, local_sems, scatter):
    me = _my_index()
    sends, arrivals = [], []
    for k in range(1, N_DEV):
        peer, pidx = _peer(k)
        for t, (src, dst) in enumerate(zip(srcs, dsts)):
            mine = src.at[pidx] if scatter else src
            sems = dict(send_sem=send_sems.at[t, k - 1], recv_sem=recv_sems.at[t, k - 1], device_id=peer, device_id_type=MESH)
            sends.append(pltpu.make_async_remote_copy(src_ref=mine, dst_ref=dst.at[me], **sems))
            arrivals.append(pltpu.make_async_remote_copy(src_ref=mine, dst_ref=dst.at[pidx], **sems))
    local = [pltpu.make_async_copy(src.at[me] if scatter else src, dst.at[me], local_sems.at[t])
             for t, (src, dst) in enumerate(zip(srcs, dsts))]
    return sends, arrivals, local


def _exchange_start(*args):
    sends, _, local = _exchange(*args)
    for cp in sends + local:
        cp.start()


def _exchange_wait(*args):
    sends, arrivals, local = _exchange(*args)
    for cp in arrivals:
        cp.wait_recv()
    for cp in sends:
        cp.wait_send()
    for cp in local:
        cp.wait()


def _exchange_sems(n):
    return [pltpu.SemaphoreType.DMA((n, N_DEV - 1)), pltpu.SemaphoreType.DMA((n, N_DEV - 1)), pltpu.SemaphoreType.DMA((n,))]


HBM_SPEC = pl.BlockSpec(memory_space=pl.ANY)


def _sum_slots(recv_ref, out_ref):
    rows = out_ref.shape[0]
    chunk = min(rows, 128)

    def add(i, carry):
        r0 = pl.multiple_of(i * chunk, chunk)
        acc = recv_ref[0, pl.ds(r0, chunk), :].astype(F32)
        for dev in range(1, N_DEV):
            acc = acc + recv_ref[dev, pl.ds(r0, chunk), :].astype(F32)
        out_ref[pl.ds(r0, chunk), :] = acc
        return carry

    lax.fori_loop(0, rows // chunk, add, 0)


def _reduce_exchange(parts, landed, smalls):
    ns, nl, ng = len(parts), len(landed), len(smalls)
    nx = ns + ng
    n_out = ns + nl + ng

    def body(*refs):
        p_in, l_in, s_in = refs[:ns], refs[ns:ns + nl], refs[ns + nl:n_out]
        outs = refs[n_out:2 * n_out]
        p_out, l_out, s_out = outs[:ns], outs[ns:ns + nl], outs[ns + nl:]
        p_recv, s_recv = refs[2 * n_out:2 * n_out + ns], refs[2 * n_out + ns:2 * n_out + nx]
        sems = refs[2 * n_out + nx:]
        _exchange_start(p_in, p_recv, *sems[:3], True)
        _exchange_start(s_in, s_recv, *sems[3:], False)
        for t in range(nl):
            _sum_slots(l_in[t], l_out[t])
        _exchange_wait(p_in, p_recv, *sems[:3], True)
        _exchange_wait(s_in, s_recv, *sems[3:], False)
        for t in range(ns):
            _sum_slots(p_recv[t], p_out[t])
        for t in range(ng):
            acc = s_recv[t][0]
            for dev in range(1, N_DEV):
                acc = acc + s_recv[t][dev]
            s_out[t][...] = acc

    vmem = pl.BlockSpec(memory_space=pltpu.VMEM)
    outs = pl.pallas_call(
        body,
        name="reduce_grads",
        out_shape=[jax.ShapeDtypeStruct(p.shape[1:], F32) for p in parts + landed]
        + [jax.ShapeDtypeStruct(s.shape, F32) for s in smalls],
        in_specs=[vmem] * n_out,
        out_specs=[vmem] * n_out,
        scratch_shapes=[pltpu.VMEM(p.shape, p.dtype) for p in parts]
        + [pltpu.VMEM((N_DEV,) + s.shape, F32) for s in smalls]
        + _exchange_sems(ns) + _exchange_sems(ng),
        compiler_params=_params(vmem_mib=56),
    )(*parts, *landed, *smalls)
    return outs[:ns], outs[ns:ns + nl], outs[ns + nl:]


def _layer_a_fwd(x2, sm, win_g, wout, later, ts):
    seq, d = x2.shape
    width = wout.shape[0]
    half = win_g.shape[2]
    n_half = width // half
    nl = len(later)
    nt = seq // ts

    def body(x_ref, sm_ref, win_ref, wout_ref, *refs):
        shard_refs, refs = refs[:nl], refs[nl:]
        h1_ref, n1_ref, proj_ref, conv_ref, y_ref, ya_ref = refs[:6]
        gathered_refs, (vprev_ref, *sems) = refs[6:6 + nl], refs[6 + nl:]

        @pl.when(pl.program_id(0) == 0)
        def _():
            vprev_ref[...] = jnp.zeros_like(vprev_ref)
            _exchange_start(shard_refs, gathered_refs, *sems, False)

        @pl.when(pl.program_id(0) == nt - 1)
        def _():
            _exchange_wait(shard_refs, gathered_refs, *sems, False)

        xf = x_ref[...]
        xn, _ = _rms(xf)
        n1 = (xn * sm_ref[0:1, :]).astype(BF16)
        n1_ref[...] = n1
        row = lax.broadcasted_iota(jnp.int32, (ts, half), 0)
        ya = jnp.zeros((ts, d), F32)
        for hh in range(n_half):
            cols = slice(hh * half, (hh + 1) * half)
            parts = []
            for part in range(4):
                j = part * n_half + hh
                pj = _dot(n1, win_ref[j])
                proj_ref[:, j * half:(j + 1) * half] = pj.astype(BF16)
                parts.append(pj)
            b, c, u, z = parts
            v = c * u
            last1, last2 = vprev_ref[7:8, cols], vprev_ref[6:7, cols]
            v1 = jnp.where(row == 0, last1, pltpu.roll(v, 1, 0))
            v2 = jnp.where(row == 0, last2, jnp.where(row == 1, last1, pltpu.roll(v, 2, 0)))
            vprev_ref[:, cols] = v[ts - 8:ts, :]
            conv = sm_ref[1:2, cols] * v2 + sm_ref[2:3, cols] * v1 + sm_ref[3:4, cols] * v
            conv_ref[:, cols] = conv.astype(BF16)
            yh = (b * conv * _silu(z)[0]).astype(BF16)
            y_ref[:, cols] = yh
            ya = ya + _dot(yh, wout_ref[cols, :])
        ya_ref[...] = ya
        h1_ref[...] = xf + _rms(ya)[0] * sm_ref[4:5, :]

    outs = pl.pallas_call(
        body,
        name="layer_a_fwd",
        grid=(nt,),
        in_specs=[_rows(ts, d), _full(sm.shape), _full(win_g.shape), _full(wout.shape)] + [HBM_SPEC] * nl,
        out_specs=[_rows(ts, d), _rows(ts, d), _rows(ts, 4 * width), _rows(ts, width), _rows(ts, width), _rows(ts, d)]
        + [HBM_SPEC] * nl,
        out_shape=[
            jax.ShapeDtypeStruct((seq, d), F32),
            jax.ShapeDtypeStruct((seq, d), BF16),
            jax.ShapeDtypeStruct((seq, 4 * width), BF16),
            jax.ShapeDtypeStruct((seq, width), BF16),
            jax.ShapeDtypeStruct((seq, width), BF16),
            jax.ShapeDtypeStruct((seq, d), F32),
        ] + [jax.ShapeDtypeStruct((N_DEV,) + s.shape, s.dtype) for s in later],
        scratch_shapes=[pltpu.VMEM((8, width), F32)] + _exchange_sems(nl),
        compiler_params=_params(("arbitrary",), 56),
    )(x2, sm, win_g, wout, *later)
    return outs[:6], outs[6:]


def _layer_b_in(h1, kvn, bpre, wkv, wbin_g, ts):
    seq, d = h1.shape
    kvw = wkv.shape[1]
    cw = wbin_g.shape[2]
    aw = N_Q_HEADS * HEAD_DIM
    per = aw // cw

    def body(h1_ref, kvn_ref, bpre_ref, wkv_ref, wbin_ref, n3_ref, n4_ref, kv_ref, q_ref, z2_ref):
        hn, _ = _rms(h1_ref[...])
        n3 = (hn * kvn_ref[...]).astype(BF16)
        n4 = (hn * bpre_ref[...]).astype(BF16)
        n3_ref[...] = n3
        n4_ref[...] = n4
        kv_ref[...] = _dot(n3, wkv_ref[...]).astype(BF16)
        for j in range(N_DEV):
            pj = _dot(n4, wbin_ref[j])
            if j < per:
                q_ref[:, j * cw:(j + 1) * cw] = pj.astype(BF16)
            else:
                z2_ref[:, (j - per) * cw:(j - per + 1) * cw] = pj

    return pl.pallas_call(
        body,
        name="layer_b_in",
        grid=(seq // ts,),
        in_specs=[_rows(ts, d), _full(kvn.shape), _full(bpre.shape), _full(wkv.shape), _full(wbin_g.shape)],
        out_specs=[_rows(ts, d), _rows(ts, d), _rows(ts, kvw), _rows(ts, aw), _rows(ts, aw)],
        out_shape=[
            jax.ShapeDtypeStruct((seq, d), BF16),
            jax.ShapeDtypeStruct((seq, d), BF16),
            jax.ShapeDtypeStruct((seq, kvw), BF16),
            jax.ShapeDtypeStruct((seq, aw), BF16),
            jax.ShapeDtypeStruct((seq, aw), F32),
        ],
        compiler_params=_params(("parallel",), 48),
    )(h1, kvn, bpre, wkv, wbin_g)


N_PAIRS = N_Q_HEADS // 2
BAND = 2 * BLOCK


def _bias_table(rel_bias, bucket_t, in_window_t):
    def body(rb_ref, bucket_ref, win_ref, out_ref):
        bk = bucket_ref[...]
        inside = win_ref[...] != 0
        has_prev = lax.broadcasted_iota(jnp.int32, bk.shape, 0) >= BLOCK
        for h in range(N_Q_HEADS):
            acc = jnp.full(bk.shape, NEG_INF, F32)
            for b in range(N_BUCKETS):
                acc = jnp.where(jnp.logical_and(bk == b, inside), rb_ref[b, h], acc)
            cols = slice((h % 2) * BLOCK, (h % 2 + 1) * BLOCK)
            out_ref[1, h // 2, :, cols] = acc
            out_ref[0, h // 2, :, cols] = jnp.where(has_prev, acc, NEG_INF)

    vmem = pl.BlockSpec(memory_space=pltpu.VMEM)
    return pl.pallas_call(
        body,
        name="bias_table",
        in_specs=[pl.BlockSpec(memory_space=pltpu.SMEM), vmem, vmem],
        out_specs=vmem,
        out_shape=jax.ShapeDtypeStruct((2, N_PAIRS, BAND, 2 * BLOCK), F32),
    )(rel_bias, bucket_t, in_window_t)


def _bias_spec(biasm):
    return pl.BlockSpec((None,) + biasm.shape[1:], lambda i: (jnp.minimum(i, 1), 0, 0, 0))


def _banded_kv(kvp_ref, kvc_ref):
    kvp = kvp_ref[...].astype(F32)
    kvc = kvc_ref[...].astype(F32)
    kw = N_KV_HEADS * HEAD_DIM
    out = []
    for full in (jnp.concatenate([kvp[:, :kw], kvc[:, :kw]], axis=0), jnp.concatenate([kvp[:, kw:], kvc[:, kw:]], axis=0)):
        lo = lax.broadcasted_iota(jnp.int32, full.shape, 1) < HEAD_DIM
        rolled = pltpu.roll(full, HEAD_DIM, 1)
        x2 = [jnp.where(lo, full, rolled).astype(BF16), jnp.where(lo, rolled, full).astype(BF16)]
        ft = full.T
        x2t = [jnp.concatenate([ft[kh * HEAD_DIM:(kh + 1) * HEAD_DIM]] * 2, axis=0).astype(BF16) for kh in range(N_KV_HEADS)]
        out += [x2, x2t]
    return out


def _pair_rows(ref, m, scale=None):
    both = ref[:, m * LANES:(m + 1) * LANES].astype(F32)
    if scale is not None:
        both = both * scale
    lo = lax.broadcasted_iota(jnp.int32, both.shape, 1) < HEAD_DIM
    zero = jnp.zeros_like(both)
    return jnp.concatenate([jnp.where(lo, both, zero), jnp.where(lo, zero, both)], axis=0).astype(BF16)


def _pair_cols(res_t):
    top = lax.broadcasted_iota(jnp.int32, (LANES, BLOCK), 0) < HEAD_DIM
    return jnp.where(top, res_t[:, :BLOCK], res_t[:, BLOCK:]).T


def _sink_row(sink_ref, m):
    first = lax.broadcasted_iota(jnp.int32, (1, 2 * BLOCK), 1) < BLOCK
    return jnp.where(first, sink_ref[0, 2 * m], sink_ref[0, 2 * m + 1])


def _probs_t(k2, qpair, bias, sink):
    return _softmax_t(_dot_nt(k2, qpair) + bias, sink)


def _softmax_t(logits, sink):
    mx = jnp.maximum(jnp.max(logits, axis=0, keepdims=True), sink)
    p = jnp.exp(logits - mx)
    sink_p = jnp.exp(sink - mx)
    inv = 1.0 / (jnp.sum(p, axis=0, keepdims=True) + sink_p)
    return p * inv, sink_p * inv


def _attn_fwd(q, kv, z2, biasm, sinks):
    seq, aw = q.shape
    kvw = kv.shape[1]
    nb = seq // BLOCK

    def body(sink_ref, q_ref, kvc_ref, kvp_ref, z2_ref, bias_ref, attn_ref, o_ref, acc_ref):
        k2, _, _, v2t = _banded_kv(kvp_ref, kvc_ref)
        kv_of = lambda m: (2 * m) // GROUP
        logits, probs = {}, {}
        for step in range(N_PAIRS + 2):
            if step < N_PAIRS:
                logits[step] = _dot_nt(k2[kv_of(step)], _pair_rows(q_ref, step, SCALE)) + bias_ref[step]
            m = step - 1
            if 0 <= m < N_PAIRS:
                probs[m] = _softmax_t(logits.pop(m), _sink_row(sink_ref, m))[0].astype(BF16)
            m = step - 2
            if 0 <= m < N_PAIRS:
                acc_ref[:, m * LANES:(m + 1) * LANES] = _pair_cols(_dot(v2t[kv_of(m)], probs.pop(m)))
        attn = acc_ref[...]
        attn_ref[...] = attn.astype(BF16)
        o_ref[...] = (attn * _silu(z2_ref[...])[0]).astype(BF16)

    blk = lambda w: pl.BlockSpec((BLOCK, w), lambda i: (i, 0))
    return pl.pallas_call(
        body,
        name="attn_fwd",
        grid=(nb,),
        in_specs=[
            pl.BlockSpec(memory_space=pltpu.SMEM),
            blk(aw),
            blk(kvw),
            pl.BlockSpec((BLOCK, kvw), lambda i: (jnp.maximum(i - 1, 0), 0)),
            blk(aw),
            _bias_spec(biasm),
        ],
        out_specs=[blk(aw), blk(aw)],
        out_shape=[jax.ShapeDtypeStruct((seq, aw), BF16), jax.ShapeDtypeStruct((seq, aw), BF16)],
        scratch_shapes=[pltpu.VMEM((BLOCK, aw), F32)],
        compiler_params=_params(("arbitrary",), 32),
    )(sinks, q, kv, kv, z2, biasm)


def _layer_b_out(o, attn, z2, h1, target, wbout, bpost, ts):
    seq, d = h1.shape
    aw = o.shape[1]

    def body(o_ref, attn_ref, z2_ref, h1_ref, tgt_ref, w_ref, g_ref, dh2_ref, dyb_ref, dattn_ref, dz2_ref, acc_ref):
        @pl.when(pl.program_id(0) == 0)
        def _():
            acc_ref[...] = jnp.zeros_like(acc_ref)

        w = w_ref[...]
        yb = _dot(o_ref[...], w)
        ybn, r = _rms(yb)
        g = g_ref[...]
        diff = h1_ref[...] + ybn * g - tgt_ref[...]
        dh2 = diff * (1.0 / d)
        dh2_ref[...] = dh2
        acc_ref[0:1, :] += jnp.sum(dh2 * ybn, axis=0, keepdims=True)
        tok = jnp.mean(diff * diff, axis=-1, keepdims=True)
        acc_ref[1:2, :] += 0.5 * jnp.sum(tok, axis=0, keepdims=True)
        dyb = _rms_bwd(dh2 * g, ybn, r).astype(BF16)
        dyb_ref[...] = dyb
        do = _dot_nt(dyb, w)
        sz, dsz = _silu(z2_ref[...])
        dattn_ref[...] = (do * sz).astype(BF16)
        dz2_ref[...] = (do * attn_ref[...].astype(F32) * dsz).astype(BF16)

    return pl.pallas_call(
        body,
        name="layer_b_out",
        grid=(seq // ts,),
        in_specs=[_rows(ts, aw), _rows(ts, aw), _rows(ts, aw), _rows(ts, d), _rows(ts, d), _full(wbout.shape), _full(bpost.shape)],
        out_specs=[_rows(ts, d), _rows(ts, d), _rows(ts, aw), _rows(ts, aw), _resident((8, d))],
        out_shape=[
            jax.ShapeDtypeStruct((seq, d), F32),
            jax.ShapeDtypeStruct((seq, d), BF16),
            jax.ShapeDtypeStruct((seq, aw), BF16),
            jax.ShapeDtypeStruct((seq, aw), BF16),
            jax.ShapeDtypeStruct((8, d), F32),
        ],
        compiler_params=_params(("arbitrary",), 48),
    )(o, attn, z2, h1, target, wbout, bpost)


def _attn_bwd(q, kv, dattn, biasm, sinks):
    seq, aw = q.shape
    kvw = kv.shape[1]
    kw = N_KV_HEADS * HEAD_DIM
    nb = seq // BLOCK
    pairs_per_kv = N_PAIRS // N_KV_HEADS

    def body(sink_ref, q_ref, kvc_ref, kvp_ref, da_ref, bias_ref, dq_ref, dkv_ref, dssum_ref, dsink_ref,
             carry_ref, qs_ref, dos_ref, dst_ref, pt_ref):
        i = pl.program_id(0)

        @pl.when(i == 0)
        def _():
            dssum_ref[...] = jnp.zeros_like(dssum_ref)
            dsink_ref[...] = jnp.zeros_like(dsink_ref)
            carry_ref[...] = jnp.zeros_like(carry_ref)

        @pl.when(i < nb)
        def _():
            lo = lax.broadcasted_iota(jnp.int32, (BAND, LANES), 1) < HEAD_DIM
            head_lane = lax.broadcasted_iota(jnp.int32, (1, LANES), 1)
            k2, k2t, v2, _ = _banded_kv(kvp_ref, kvc_ref)
            dsink = jnp.zeros((1, LANES), F32)
            folded = []
            logits, dps, dsbs = {}, {}, {}
            for step in range(N_PAIRS + 2):
                if step < N_PAIRS:
                    kh, rows = step // pairs_per_kv, slice((step % pairs_per_kv) * BAND, (step % pairs_per_kv + 1) * BAND)
                    qpair = _pair_rows(q_ref, step, SCALE)
                    dopair = _pair_rows(da_ref, step)
                    qs_ref[kh, rows, :] = qpair
                    dos_ref[kh, rows, :] = dopair
                    logits[step] = _dot_nt(k2[kh], qpair) + bias_ref[step]
                    dps[step] = _dot_nt(v2[kh], dopair)
                m = step - 1
                if 0 <= m < N_PAIRS:
                    kh, rows = m // pairs_per_kv, slice((m % pairs_per_kv) * BAND, (m % pairs_per_kv + 1) * BAND)
                    pn, sink_p = _softmax_t(logits.pop(m), _sink_row(sink_ref, m))
                    dp = dps.pop(m)
                    delta = jnp.sum(pn * dp, axis=0, keepdims=True)
                    ds = pn * (dp - delta)
                    dssum_ref[m] += ds
                    sink_term = sink_p * delta
                    for e in range(2):
                        total = jnp.sum(sink_term[:, e * BLOCK:(e + 1) * BLOCK], axis=1, keepdims=True)
                        dsink = dsink - jnp.where(head_lane == 2 * m + e, total, 0.0)
                    dsbs[m] = ds.astype(BF16)
                    dst_ref[kh, :, rows] = dsbs[m]
                    pt_ref[kh, :, rows] = pn.astype(BF16)
                m = step - 2
                if 0 <= m < N_PAIRS:
                    kh = m // pairs_per_kv
                    dq_ref[:, m * LANES:(m + 1) * LANES] = (_pair_cols(_dot(k2t[kh], dsbs.pop(m))) * SCALE).astype(BF16)
                    if m % pairs_per_kv == pairs_per_kv - 1:
                        for lhs_ref, rhs_ref in ((dst_ref, qs_ref), (pt_ref, dos_ref)):
                            acc = _dot(lhs_ref[kh], rhs_ref[kh])
                            folded.append(acc + pltpu.roll(acc, HEAD_DIM, 1))
            dsink_ref[0:1, :] += dsink
            dk = jnp.where(lo, folded[0], folded[2])
            dv = jnp.where(lo, folded[1], folded[3])

            @pl.when(i > 0)
            def _():
                dkv_ref[:, :kw] = (carry_ref[:, :kw] + dk[:BLOCK]).astype(BF16)
                dkv_ref[:, kw:] = (carry_ref[:, kw:] + dv[:BLOCK]).astype(BF16)

            carry_ref[:, :kw] = dk[BLOCK:]
            carry_ref[:, kw:] = dv[BLOCK:]

        @pl.when(i == nb)
        def _():
            dkv_ref[...] = carry_ref[...].astype(BF16)

    last = nb - 1
    blk = lambda w: pl.BlockSpec((BLOCK, w), lambda i: (jnp.minimum(i, last), 0))
    return pl.pallas_call(
        body,
        name="attn_bwd",
        grid=(nb + 1,),
        in_specs=[
            pl.BlockSpec(memory_space=pltpu.SMEM),
            blk(aw),
            blk(kvw),
            pl.BlockSpec((BLOCK, kvw), lambda i: (jnp.clip(i - 1, 0, last), 0)),
            blk(aw),
            _bias_spec(biasm),
        ],
        out_specs=[
            blk(aw),
            pl.BlockSpec((BLOCK, kvw), lambda i: (jnp.maximum(i - 1, 0), 0)),
            _resident(biasm.shape[1:]),
            _resident((8, LANES)),
        ],
        out_shape=[
            jax.ShapeDtypeStruct((seq, aw), BF16),
            jax.ShapeDtypeStruct((seq, kvw), BF16),
            jax.ShapeDtypeStruct(biasm.shape[1:], F32),
            jax.ShapeDtypeStruct((8, LANES), F32),
        ],
        scratch_shapes=[
            pltpu.VMEM((BLOCK, kvw), F32),
            pltpu.VMEM((N_KV_HEADS, pairs_per_kv * BAND, LANES), BF16),
            pltpu.VMEM((N_KV_HEADS, pairs_per_kv * BAND, LANES), BF16),
            pltpu.VMEM((N_KV_HEADS, BAND, pairs_per_kv * BAND), BF16),
            pltpu.VMEM((N_KV_HEADS, BAND, pairs_per_kv * BAND), BF16),
        ],
        compiler_params=_params(("arbitrary",), 40),
    )(sinks, q, kv, kv, dattn, biasm)


def _relbias_grad(dssum2, onehot, chunk):
    heads, n = dssum2.shape

    def body(a_ref, oh_ref, out_ref):
        @pl.when(pl.program_id(0) == 0)
        def _():
            out_ref[...] = jnp.zeros_like(out_ref)

        a = a_ref[...]
        hi = a.astype(BF16)
        lo = (a - hi.astype(F32)).astype(BF16)
        out_ref[...] += _dot(hi, oh_ref[...]) + _dot(lo, oh_ref[...])

    return pl.pallas_call(
        body,
        name="relbias_grad",
        grid=(n // chunk,),
        in_specs=[pl.BlockSpec((heads, chunk), lambda i: (0, i)), pl.BlockSpec((chunk, LANES), lambda i: (i, 0))],
        out_specs=_resident((heads, LANES)),
        out_shape=jax.ShapeDtypeStruct((heads, LANES), F32),
        compiler_params=_params(("arbitrary",), 32),
    )(dssum2, onehot)


def _layer_b_in_bwd(dh2, dq, dz2, dkv, h1, ya, wbin_g, wkv, kvn, bpre, sm, ts):
    seq, d = h1.shape
    aw = dq.shape[1]
    kvw = dkv.shape[1]
    cw = wbin_g.shape[2]
    per = aw // cw

    def body(dh2_ref, dq_ref, dz2_ref, dkv_ref, h1_ref, ya_ref, wbin_ref, wkv_ref, kvn_ref, bpre_ref, sm_ref,
             dh1_ref, dya_ref, acc_ref):
        @pl.when(pl.program_id(0) == 0)
        def _():
            acc_ref[...] = jnp.zeros_like(acc_ref)

        dn4 = jnp.zeros((ts, d), F32)
        for j in range(N_DEV):
            src = dq_ref if j < per else dz2_ref
            jj = j % per
            dn4 = dn4 + _dot_nt(src[:, jj * cw:(jj + 1) * cw], wbin_ref[j])
        dn3 = _dot_nt(dkv_ref[...], wkv_ref[...])
        hn, r = _rms(h1_ref[...])
        acc_ref[0:1, :] += jnp.sum(dn4 * hn, axis=0, keepdims=True)
        acc_ref[1:2, :] += jnp.sum(dn3 * hn, axis=0, keepdims=True)
        dh1 = dh2_ref[...] + _rms_bwd(dn4 * bpre_ref[...] + dn3 * kvn_ref[...], hn, r)
        dh1_ref[...] = dh1
        yan, r2 = _rms(ya_ref[...])
        acc_ref[2:3, :] += jnp.sum(dh1 * yan, axis=0, keepdims=True)
        dya_ref[...] = _rms_bwd(dh1 * sm_ref[4:5, :], yan, r2).astype(BF16)

    return pl.pallas_call(
        body,
        name="layer_b_in_bwd",
        grid=(seq // ts,),
        in_specs=[_rows(ts, d), _rows(ts, aw), _rows(ts, aw), _rows(ts, kvw), _rows(ts, d), _rows(ts, d),
                  _full(wbin_g.shape), _full(wkv.shape), _full(kvn.shape), _full(bpre.shape), _full(sm.shape)],
        out_specs=[_rows(ts, d), _rows(ts, d), _resident((8, d))],
        out_shape=[jax.ShapeDtypeStruct((seq, d), F32), jax.ShapeDtypeStruct((seq, d), BF16),
                   jax.ShapeDtypeStruct((8, d), F32)],
        compiler_params=_params(("arbitrary",), 48),
    )(dh2, dq, dz2, dkv, h1, ya, wbin_g, wkv, kvn, bpre, sm)


def _layer_a_bwd(dya, proj, conv, dh1, x2, wout, win_g, sm, ready, ts):
    seq, d = x2.shape
    width = wout.shape[0]
    half = win_g.shape[2]
    n_half = width // half
    nt = seq // ts
    nr = len(ready)

    def body(dya_ref, proj_ref, conv_ref, dh1_ref, x_ref, wout_ref, win_ref, sm_ref, *refs):
        ready_refs, (dproj_ref, gx_ref, acc_ref) = refs[:nr], refs[nr:nr + 3]
        landed_refs, (dnext_ref, *sems) = refs[nr + 3:2 * nr + 3], refs[2 * nr + 3:]

        @pl.when(pl.program_id(0) == 0)
        def _():
            acc_ref[...] = jnp.zeros_like(acc_ref)
            dnext_ref[...] = jnp.zeros_like(dnext_ref)
            _exchange_start(ready_refs, landed_refs, *sems, True)

        @pl.when(pl.program_id(0) == nt - 1)
        def _():
            _exchange_wait(ready_refs, landed_refs, *sems, True)

        dy = _dot_nt(dya_ref[...], wout_ref[...])
        row = lax.broadcasted_iota(jnp.int32, (ts, half), 0)
        dn1 = jnp.zeros((ts, d), F32)
        for hh in range(n_half):
            cols = slice(hh * half, (hh + 1) * half)
            b, c, u, z = [proj_ref[:, (part * n_half + hh) * half:(part * n_half + hh + 1) * half].astype(F32)
                          for part in range(4)]
            cv = conv_ref[:, cols].astype(F32)
            dyh = dy[:, cols]
            sz, dsz = _silu(z)
            dconv = dyh * b * sz
            grads = [dyh * cv * sz, None, None, dyh * b * cv * dsz]
            next0, next1 = dnext_ref[0:1, cols], dnext_ref[1:2, cols]
            dc1 = jnp.where(row == ts - 1, next0, pltpu.roll(dconv, ts - 1, 0))
            dc2 = jnp.where(row == ts - 1, next1, jnp.where(row == ts - 2, next0, pltpu.roll(dconv, ts - 2, 0)))
            dnext_ref[:, cols] = dconv[0:8, :]
            v = c * u
            acc_ref[1:2, cols] += jnp.sum(dc2 * v, axis=0, keepdims=True)
            acc_ref[2:3, cols] += jnp.sum(dc1 * v, axis=0, keepdims=True)
            acc_ref[3:4, cols] += jnp.sum(dconv * v, axis=0, keepdims=True)
            dv = sm_ref[3:4, cols] * dconv + sm_ref[2:3, cols] * dc1 + sm_ref[1:2, cols] * dc2
            grads[1] = dv * u
            grads[2] = dv * c
            for part in range(4):
                j = part * n_half + hh
                gj = grads[part].astype(BF16)
                dproj_ref[:, j * half:(j + 1) * half] = gj
                dn1 = dn1 + _dot_nt(gj, win_ref[j])
        xn, r = _rms(x_ref[...])
        acc_ref[0:1, :] += jnp.sum(dn1 * xn, axis=0, keepdims=True)
        gx_ref[...] = dh1_ref[...] + _rms_bwd(dn1 * sm_ref[0:1, :], xn, r)

    rev = lambda w: pl.BlockSpec((ts, w), lambda i: (nt - 1 - i, 0))
    outs = pl.pallas_call(
        body,
        name="layer_a_bwd",
        grid=(nt,),
        in_specs=[rev(d), rev(4 * width), rev(width), rev(d), rev(d), _full(wout.shape), _full(win_g.shape), _full(sm.shape)]
        + [HBM_SPEC] * nr,
        out_specs=[rev(4 * width), rev(d), _resident((8, d))] + [HBM_SPEC] * nr,
        out_shape=[jax.ShapeDtypeStruct((seq, 4 * width), BF16), jax.ShapeDtypeStruct((seq, d), F32),
                   jax.ShapeDtypeStruct((8, d), F32)] + [jax.ShapeDtypeStruct(g.shape, g.dtype) for g in ready],
        scratch_shapes=[pltpu.VMEM((8, width), F32)] + _exchange_sems(nr),
        compiler_params=_params(("arbitrary",), 56),
    )(dya, proj, conv, dh1, x2, wout, win_g, sm, *ready)
    return outs[:3], outs[3:]


def _wgrad(a, bs, n_slots, ts, name, ready=()):
    nr = len(ready)
    seq, k = a.shape
    nb_in = len(bs)
    n_each = bs[0].shape[1]
    n = nb_in * n_each
    bn = min(n_each, 1024)
    per_in = n_each // bn
    n_blocks = nb_in * per_in
    ns = seq // ts

    def b_spec(idx):
        def index(j, s):
            mine = j // per_in == idx
            row = jnp.where(mine, s, jnp.where(j // per_in > idx, ns - 1, 0))
            return (row, jnp.where(mine, j % per_in, jnp.where(j // per_in > idx, per_in - 1, 0)))
        return pl.BlockSpec((ts, bn), index)

    if n_slots:
        sw = n // n_slots
        spb = bn // sw
        out_shape = jax.ShapeDtypeStruct((n_slots, k, sw), BF16)
        out_spec = pl.BlockSpec((spb, k, sw), lambda j, s: (j, 0, 0))
    else:
        out_shape = jax.ShapeDtypeStruct((k, n), BF16)
        out_spec = pl.BlockSpec((k, bn), lambda j, s: (0, j))

    def body(a_ref, *refs):
        b_refs, ready_refs, o_ref = refs[:nb_in], refs[nb_in:nb_in + nr], refs[nb_in + nr]
        landed_refs, (acc_ref, *sems) = refs[nb_in + nr + 1:nb_in + 2 * nr + 1], refs[nb_in + 2 * nr + 1:]
        j, s = pl.program_id(0), pl.program_id(1)

        if nr:
            @pl.when(jnp.logical_and(j == 0, s == 0))
            def _():
                _exchange_start(ready_refs, landed_refs, *sems, True)

            @pl.when(jnp.logical_and(j == n_blocks - 1, s == ns - 1))
            def _():
                _exchange_wait(ready_refs, landed_refs, *sems, True)

        @pl.when(s == 0)
        def _():
            acc_ref[...] = jnp.zeros_like(acc_ref)

        for idx in range(nb_in):
            @pl.when(j // per_in == idx)
            def _(idx=idx):
                acc_ref[...] += _dot_tn(a_ref[...], b_refs[idx][...])

        @pl.when(s == ns - 1)
        def _():
            if n_slots:
                for e in range(spb):
                    o_ref[e] = acc_ref[:, e * sw:(e + 1) * sw].astype(BF16)
            else:
                o_ref[...] = acc_ref[...].astype(BF16)

    outs = pl.pallas_call(
        body,
        name=name,
        grid=(n_blocks, ns),
        in_specs=[pl.BlockSpec((ts, k), lambda j, s: (s, 0))] + [b_spec(idx) for idx in range(nb_in)] + [HBM_SPEC] * nr,
        out_specs=[out_spec] + [HBM_SPEC] * nr,
        out_shape=[out_shape] + [jax.ShapeDtypeStruct(g.shape, g.dtype) for g in ready],
        scratch_shapes=[pltpu.VMEM((k, bn), F32)] + (_exchange_sems(nr) if nr else []),
        compiler_params=_params(("arbitrary", "arbitrary"), 48),
    )(a, *bs, *ready)
    return (outs[0], outs[1:]) if nr else outs[0]


def _adamw(ws, gs, ms, vs):
    n = len(ws)

    def step(w, g, m, v):
        m = ADAM_B1 * m + (1.0 - ADAM_B1) * g
        v = ADAM_B2 * v + (1.0 - ADAM_B2) * jnp.square(g)
        m_hat = m / (1.0 - ADAM_B1 ** ADAM_STEP)
        v_hat = v / (1.0 - ADAM_B2 ** ADAM_STEP)
        return -ADAM_LR * (m_hat / (jnp.sqrt(v_hat) + ADAM_EPS) + ADAM_WD * w), m, v

    def body(*refs):
        w_refs, g_refs, m_refs, v_refs = (refs[k * n:(k + 1) * n] for k in range(4))
        d_refs, nm_refs, nv_refs = (refs[(4 + k) * n:(5 + k) * n] for k in range(3))
        for t in range(n):
            rows = w_refs[t].shape[0]
            if rows <= 128:
                d_refs[t][...], nm_refs[t][...], nv_refs[t][...] = step(
                    w_refs[t][...], g_refs[t][...], m_refs[t][...], v_refs[t][...])
                continue
            chunk = 128

            def one(i, carry, t=t):
                r = pl.ds(pl.multiple_of(i * chunk, chunk), chunk)
                d_refs[t][r, :], nm_refs[t][r, :], nv_refs[t][r, :] = step(
                    w_refs[t][r, :], g_refs[t][r, :], m_refs[t][r, :], v_refs[t][r, :])
                return carry

            lax.fori_loop(0, rows // chunk, one, 0)

    vmem = pl.BlockSpec(memory_space=pltpu.VMEM)
    outs = pl.pallas_call(
        body,
        name="adamw",
        in_specs=[vmem] * (4 * n),
        out_specs=[vmem] * (3 * n),
        out_shape=[jax.ShapeDtypeStruct(w.shape, F32) for w in ws] * 3,
        compiler_params=_params(vmem_mib=56),
    )(*ws, *gs, *ms, *vs)
    return outs[:n], outs[n:2 * n], outs[2 * n:]


def _band_structure():
    q_loc = jnp.arange(BLOCK, dtype=jnp.int32)[:, None]
    s_loc = jnp.arange(2 * BLOCK, dtype=jnp.int32)[None, :]
    dist = q_loc + BLOCK - s_loc
    in_window = (dist >= 0) & (dist < BLOCK)
    dd = jnp.maximum(dist, 0)
    max_exact = N_BUCKETS // 2
    large = max_exact + (jnp.log(jnp.maximum(dd, 1).astype(F32) / max_exact) / math.log(MAX_DISTANCE / max_exact)
                         * (N_BUCKETS - max_exact)).astype(jnp.int32)
    bucket = jnp.where(dd < max_exact, dd, jnp.minimum(large, N_BUCKETS - 1))
    onehot = (bucket.reshape(-1, 1) == jnp.arange(LANES, dtype=jnp.int32)[None, :]).astype(BF16)
    return bucket, in_window.astype(jnp.int32), onehot


def _place_rows(a, row, rows=8):
    return jnp.pad(a, ((row, rows - row - a.shape[0]), (0, 0)))


def kernel(x, a_pre_norm, a_w_in, a_conv_w, a_w_out, a_post_norm, kv_norm, w_kv, rel_bias, b_pre_norm, b_w_in, b_sinks, b_w_out, b_post_norm, loss_target, m_a_pre_norm, m_a_w_in, m_a_conv_w, m_a_w_out, m_a_post_norm, m_kv_norm, m_w_kv, m_rel_bias, m_b_pre_norm, m_b_w_in, m_b_sinks, m_b_w_out, m_b_post_norm, v_a_pre_norm, v_a_w_in, v_a_conv_w, v_a_w_out, v_a_post_norm, v_kv_norm, v_w_kv, v_rel_bias, v_b_pre_norm, v_b_w_in, v_b_sinks, v_b_w_out, v_b_post_norm):
    seq, d = x.shape[1], x.shape[2]
    x2 = x.reshape(seq, d)
    target = loss_target.reshape(seq, d)
    shard = a_pre_norm.shape[1]
    me = _my_index()
    ts_a = min(seq, 512)
    ts = min(seq, 512)
    ts_w = min(seq, 2048)

    small = _place_rows(a_pre_norm, 0) + _place_rows(a_conv_w[0], 1) + _place_rows(a_post_norm, 4)
    win_g, wout_g, small_g = _all_gather([a_w_in[0], a_w_out[0], small], [BF16, BF16, F32])
    wout = wout_g.reshape(-1, wout_g.shape[2])
    sm = small_g.transpose(1, 0, 2).reshape(8, N_DEV * shard)
    kvn = kv_norm.reshape(1, d)

    (h1, n1, proj, conv, y, ya), (wkv_g, wbin_g, wbout_g) = _layer_a_fwd(
        x2, sm, win_g, wout, [w_kv.astype(BF16), b_w_in[0].astype(BF16), b_w_out[0].astype(BF16)], ts_a)
    wkv = wkv_g.reshape(-1, wkv_g.shape[2])
    wbout = wbout_g.reshape(-1, wbout_g.shape[2])
    n3, n4, kv, q, z2 = _layer_b_in(h1, kvn, b_pre_norm, wkv, wbin_g, ts)
    bucket, in_window, onehot = _band_structure()
    biasm = _bias_table(rel_bias, bucket.T, in_window.T)
    attn, o = _attn_fwd(q, kv, z2, biasm, b_sinks)
    dh2, dyb, dattn, dz2, acc_c = _layer_b_out(o, attn, z2, h1, target, wbout, b_post_norm, ts)

    g_wbout = _wgrad(o, [dyb], 0, ts_w, "wgrad_b_out").reshape(wbout_g.shape)
    dq, dkv, dssum, dsink = _attn_bwd(q, kv, dattn, biasm, b_sinks)
    by_head = dssum.reshape(N_PAIRS, BAND, 2, BLOCK).transpose(0, 2, 3, 1)
    relb = _relbias_grad(by_head.reshape(N_Q_HEADS, -1), onehot, 4096)
    g_wkv = _wgrad(n3, [dkv], 0, ts_w, "wgrad_kv").reshape(wkv_g.shape)
    g_wbin = _wgrad(n4, [dq, dz2], N_DEV, ts_w, "wgrad_b_in")
    dh1, dya, acc_b = _layer_b_in_bwd(dh2, dq, dz2, dkv, h1, ya, wbin_g, wkv, kvn, b_pre_norm, sm, ts)
    (dproj, gx, acc_a), (l_wkv, l_wbin, l_wbout) = _layer_a_bwd(
        dya, proj, conv, dh1, x2, wout, win_g, sm, [g_wkv, g_wbin, g_wbout], ts_a)
    g_wout = _wgrad(y, [dya], 0, ts_w, "wgrad_a_out").reshape(wout_g.shape)
    g_win, (l_wout,) = _wgrad(n1, [dproj], N_DEV, ts_w, "wgrad_a_in", [g_wout])

    (r_win,), (r_wout, r_wkv, r_wbin, r_wbout), (s_a, s_b, s_c, s_relb, s_sink) = _reduce_exchange(
        [g_win], [l_wout, l_wkv, l_wbin, l_wbout], [acc_a, acc_b, acc_c, relb, dsink])
    mine = lambda rows: lax.dynamic_slice_in_dim(rows, me * shard, shard, axis=1)
    loss = s_c[1, 0]
    weights = [a_pre_norm, a_w_in[0], a_conv_w[0], a_w_out[0], a_post_norm, kvn, w_kv, rel_bias, b_pre_norm,
               b_w_in[0], b_sinks, b_w_out[0], b_post_norm]
    grads = [mine(s_a[0:1]), r_win, mine(s_a[1:4]), r_wout, mine(s_b[2:3]), s_b[1:2], r_wkv,
             s_relb[:, :N_BUCKETS].T, s_b[0:1], r_wbin, s_sink[0:1, :N_Q_HEADS], r_wbout, s_c[0:1]]
    first = [m_a_pre_norm, m_a_w_in[0], m_a_conv_w[0], m_a_w_out[0], m_a_post_norm, m_kv_norm.reshape(1, d), m_w_kv,
             m_rel_bias, m_b_pre_norm, m_b_w_in[0], m_b_sinks, m_b_w_out[0], m_b_post_norm]
    second = [v_a_pre_norm, v_a_w_in[0], v_a_conv_w[0], v_a_w_out[0], v_a_post_norm, v_kv_norm.reshape(1, d), v_w_kv,
              v_rel_bias, v_b_pre_norm, v_b_w_in[0], v_b_sinks, v_b_w_out[0], v_b_post_norm]
    deltas, new_m, new_v = _adamw(weights, grads, first, second)

    shapes = [a_pre_norm.shape, a_w_in.shape, a_conv_w.shape, a_w_out.shape, a_post_norm.shape, kv_norm.shape,
              w_kv.shape, rel_bias.shape, b_pre_norm.shape, b_w_in.shape, b_sinks.shape, b_w_out.shape, b_post_norm.shape]
    shaped = lambda arrays: [a.reshape(s) for a, s in zip(arrays, shapes)]
    return (loss, gx.reshape(x.shape), *shaped(grads), *shaped(deltas), *shaped(new_m), *shaped(new_v))
```

```python
import functools
import math

import jax
import jax.numpy as jnp
from jax import lax
from jax.experimental import pallas as pl
from jax.experimental.pallas import tpu as pltpu

HEAD_DIM = 64
N_Q_HEADS = 16
N_KV_HEADS = 2
GROUP = N_Q_HEADS // N_KV_HEADS
BLOCK = 128
N_BUCKETS = 32
MAX_DISTANCE = 128
EPS = 1e-6
NEG_INF = -1e30
SCALE = HEAD_DIM ** -0.5

ADAM_LR = 0.001
ADAM_B1 = 0.9
ADAM_B2 = 0.999
ADAM_EPS = 1e-08
ADAM_WD = 0.01
ADAM_STEP = 10

N_DEV = 8
LANES = 128
F32 = jnp.float32
BF16 = jnp.bfloat16
MESH = pl.DeviceIdType.MESH
MIB = 1024 * 1024


def _params(semantics=None, vmem_mib=48):
    return pltpu.CompilerParams(dimension_semantics=semantics, vmem_limit_bytes=vmem_mib * MIB)


def _full(shape):
    zeros = (0,) * len(shape)
    return pl.BlockSpec(shape, lambda *_: zeros, pipeline_mode=pl.Buffered(1))


def _resident(shape):
    zeros = (0,) * len(shape)
    return pl.BlockSpec(shape, lambda *_: zeros)


def _rows(ts, cols):
    return pl.BlockSpec((ts, cols), lambda i: (i, 0))


def _dot(a, b):
    return jnp.dot(a, b, preferred_element_type=F32)


def _dot_nt(a, b):
    return lax.dot_general(a, b, (((1,), (1,)), ((), ())), preferred_element_type=F32)


def _dot_tn(a, b):
    return lax.dot_general(a, b, (((0,), (0,)), ((), ())), preferred_element_type=F32)


def _rms(xf):
    r = lax.rsqrt(jnp.mean(xf * xf, axis=-1, keepdims=True) + EPS)
    return xf * r, r


def _rms_bwd(dn, xn, r):
    return r * (dn - xn * jnp.mean(dn * xn, axis=-1, keepdims=True))


def _silu(z):
    s = jax.nn.sigmoid(z)
    return z * s, s * (1.0 + z * (1.0 - s))


def _my_index():
    return 4 * lax.axis_index("x") + 2 * lax.axis_index("y") + lax.axis_index("c")


def _all_gather(shards, out_dtypes):
    n = len(shards)

    def body(*refs):
        ins, outs = refs[:n], refs[n:2 * n]
        send_sems, recv_sems = refs[2 * n], refs[2 * n + 1]
        x, y, c = lax.axis_index("x"), lax.axis_index("y"), lax.axis_index("c")
        me, sibling = (x, y, c), (x, y, 1 - c)
        chips = [(1 - x, y), (x, 1 - y), (1 - x, 1 - y)]

        def copy(t, k, block, to):
            rows = outs[t].at[4 * block[0] + 2 * block[1] + block[2]]
            return pltpu.make_async_remote_copy(
                src_ref=rows, dst_ref=rows, send_sem=send_sems.at[t, k], recv_sem=recv_sems.at[t, k],
                device_id=to, device_id_type=MESH)

        for t in range(n):
            outs[t][pl.ds(_my_index(), 1)] = ins[t][...].astype(outs[t].dtype)[None]
        first = []
        for t in range(n):
            first.append(copy(t, 0, me, sibling))
            first += [copy(t, 1 + j, me, (*chip, c)) for j, chip in enumerate(chips)]
        for cp in first:
            cp.start()
        passed = []
        for j, chip in enumerate(chips):
            for t in range(n):
                copy(t, 1 + j, (*chip, c), me).wait_recv()
                fwd = copy(t, 4 + j, (*chip, c), sibling)
                fwd.start()
                passed.append(fwd)
        for t in range(n):
            copy(t, 0, sibling, me).wait_recv()
        for j, chip in enumerate(chips):
            for t in range(n):
                copy(t, 4 + j, (*chip, 1 - c), me).wait_recv()
        for cp in first + passed:
            cp.wait_send()

    vmem = pl.BlockSpec(memory_space=pltpu.VMEM)
    return pl.pallas_call(
        body,
        name="gather_weights",
        out_shape=[jax.ShapeDtypeStruct((N_DEV,) + s.shape, dt) for s, dt in zip(shards, out_dtypes)],
        in_specs=[vmem] * n,
        out_specs=[vmem] * n,
        scratch_shapes=[pltpu.SemaphoreType.DMA((n, 7)), pltpu.SemaphoreType.DMA((n, 7))],
        compiler_params=_params(vmem_mib=48),
    )(*shards)


def _peer(k):
    x, y, c = lax.axis_index("x"), lax.axis_index("y"), lax.axis_index("c")
    px = 1 - x if k & 4 else x
    py = 1 - y if k & 2 else y
    pc = 1 - c if k & 1 else c
    return (px, py, pc), 4 * px + 2 * py + pc


def _exchange(srcs, dsts, send_sems, recv_sems, local_sems, scatter):
    me = _my_index()
    sends, arrivals = [], []
    for k in range(1, N_DEV):
        peer, pidx = _peer(k)
        for t, (src, dst) in enumerate(zip(srcs, dsts)):
            mine = src.at[pidx] if scatter else src
            sems = dict(send_sem=send_sems.at[t, k - 1], recv_sem=recv_sems.at[t, k - 1], device_id=peer, device_id_type=MESH)
            sends.append(pltpu.make_async_remote_copy(src_ref=mine, dst_ref=dst.at[me], **sems))
            arrivals.append(pltpu.make_async_remote_copy(src_ref=mine, dst_ref=dst.at[pidx], **sems))
    local = [pltpu.make_async_copy(src.at[me] if scatter else src, dst.at[me], local_sems.at[t])
             for t, (src, dst) in enumerate(zip(srcs, dsts))]
    return sends, arrivals, local


def _exchange_start(*args):
    sends, _, local = _exchange(*args)
    for cp in sends + local:
        cp.start()


def _exchange_wait(*args):
    sends, arrivals, local = _exchange(*args)
    for cp in arrivals:
        cp.wait_recv()
    for cp in sends:
        cp.wait_send()
    for cp in local:
        cp.wait()


def _exchange_sems(n):
    return [pltpu.SemaphoreType.DMA((n, N_DEV - 1)), pltpu.SemaphoreType.DMA((n, N_DEV - 1)), pltpu.SemaphoreType.DMA((n,))]


HBM_SPEC = pl.BlockSpec(memory_space=pl.ANY)


def _sum_slots(recv_ref, out_ref):
    rows = out_ref.shape[0]
    chunk = min(rows, 128)

    def add(i, carry):
        r0 = pl.multiple_of(i * chunk, chunk)
        acc = recv_ref[0, pl.ds(r0, chunk), :].astype(F32)
        for dev in range(1, N_DEV):
            acc = acc + recv_ref[dev, pl.ds(r0, chunk), :].astype(F32)
        out_ref[pl.ds(r0, chunk), :] = acc
        return carry

    lax.fori_loop(0, rows // chunk, add, 0)


N_CHIPS = N_DEV // 2


def _rows_loop(rows, fn):
    chunk = min(rows, 128)

    def step(i, carry):
        fn(pl.ds(pl.multiple_of(i * chunk, chunk), chunk))
        return carry

    lax.fori_loop(0, rows // chunk, step, 0)


def _chip_reduce(g_ref, out_ref, sib_ref, chip_ref, send_ref, sems, between):
    sib_send, sib_recv, chip_send, chip_recv = sems
    x, y, c = lax.axis_index("x"), lax.axis_index("y"), lax.axis_index("c")
    my_chip = 2 * x + y
    rows = out_ref.shape[0]

    def chip_of(k):
        cx = 1 - x if k & 2 else x
        cy = 1 - y if k & 1 else y
        return (cx, cy), 2 * cx + cy

    def to_sibling(t):
        return pltpu.make_async_remote_copy(
            src_ref=g_ref.at[2 * t + 1 - c], dst_ref=sib_ref.at[t], send_sem=sib_send.at[t], recv_sem=sib_recv.at[t],
            device_id=(x, y, 1 - c), device_id_type=MESH)

    def to_chip(k):
        (cx, cy), t = chip_of(k)
        return t, pltpu.make_async_remote_copy(
            src_ref=send_ref.at[k - 1], dst_ref=chip_ref.at[my_chip], send_sem=chip_send.at[k - 1],
            recv_sem=chip_recv.at[k - 1], device_id=(cx, cy, c), device_id_type=MESH)

    def from_chip(k):
        _, t = chip_of(k)
        return pltpu.make_async_remote_copy(
            src_ref=send_ref.at[k - 1], dst_ref=chip_ref.at[t], send_sem=chip_send.at[k - 1],
            recv_sem=chip_recv.at[k - 1], device_id=(x, y, c), device_id_type=MESH)

    for t in range(N_CHIPS):
        to_sibling(t).start()
    between()
    for t in range(N_CHIPS):
        to_sibling(t).wait_recv()

    def pair_sum(t, r):
        return g_ref[2 * t + c, r, :].astype(F32) + sib_ref[t, r, :].astype(F32)

    for k in range(1, N_CHIPS):
        t, cp = to_chip(k)

        def fill(r, t=t, k=k):
            send_ref[k - 1, r, :] = pair_sum(t, r).astype(BF16)

        _rows_loop(rows, fill)
        cp.start()

    def own(r):
        chip_ref[my_chip, r, :] = pair_sum(my_chip, r).astype(BF16)

    _rows_loop(rows, own)
    for k in range(1, N_CHIPS):
        from_chip(k).wait_recv()

    def total(r):
        acc = chip_ref[0, r, :].astype(F32)
        for t in range(1, N_CHIPS):
            acc = acc + chip_ref[t, r, :].astype(F32)
        out_ref[r, :] = acc

    _rows_loop(rows, total)
    for t in range(N_CHIPS):
        to_sibling(t).wait_send()
    for k in range(1, N_CHIPS):
        to_chip(k)[1].wait_send()


def _reduce_exchange(part, landed, smalls):
    nl, ng = len(landed), len(smalls)
    n_out = 1 + nl + ng

    def body(*refs):
        p_in, l_in, s_in = refs[0], refs[1:1 + nl], refs[1 + nl:n_out]
        p_out, l_out, s_out = refs[n_out], refs[n_out + 1:n_out + 1 + nl], refs[n_out + 1 + nl:2 * n_out]
        scratch = refs[2 * n_out:]
        s_recv, (sib_ref, chip_ref, send_ref), sems = scratch[:ng], scratch[ng:ng + 3], scratch[ng + 3:]

        def between():
            _exchange_start(s_in, s_recv, *sems[4:], False)
            for t in range(nl):
                _sum_slots(l_in[t], l_out[t])

        _chip_reduce(p_in, p_out, sib_ref, chip_ref, send_ref, sems[:4], between)
        _exchange_wait(s_in, s_recv, *sems[4:], False)
        for t in range(ng):
            acc = s_recv[t][0]
            for dev in range(1, N_DEV):
                acc = acc + s_recv[t][dev]
            s_out[t][...] = acc

    vmem = pl.BlockSpec(memory_space=pltpu.VMEM)
    slot = part.shape[1:]
    outs = pl.pallas_call(
        body,
        name="reduce_grads",
        out_shape=[jax.ShapeDtypeStruct(p.shape[1:], F32) for p in [part] + landed]
        + [jax.ShapeDtypeStruct(s.shape, F32) for s in smalls],
        in_specs=[vmem] * n_out,
        out_specs=[vmem] * n_out,
        scratch_shapes=[pltpu.VMEM((N_DEV,) + s.shape, F32) for s in smalls]
        + [pltpu.VMEM((N_CHIPS,) + slot, BF16), pltpu.VMEM((N_CHIPS,) + slot, BF16), pltpu.VMEM((N_CHIPS - 1,) + slot, BF16)]
        + [pltpu.SemaphoreType.DMA((N_CHIPS,)), pltpu.SemaphoreType.DMA((N_CHIPS,)),
           pltpu.SemaphoreType.DMA((N_CHIPS - 1,)), pltpu.SemaphoreType.DMA((N_CHIPS - 1,))]
        + _exchange_sems(ng),
        compiler_params=_params(vmem_mib=56),
    )(part, *landed, *smalls)
    return outs[0], outs[1:1 + nl], outs[1 + nl:]


def _layer_a_fwd(x2, sm, win_g, wout, later, ts):
    seq, d = x2.shape
    width = wout.shape[0]
    half = win_g.shape[2]
    n_half = width // half
    nl = len(later)
    nt = seq // ts

    def body(x_ref, sm_ref, win_ref, wout_ref, *refs):
        shard_refs, refs = refs[:nl], refs[nl:]
        h1_ref, n1_ref, proj_ref, conv_ref, y_ref, ya_ref = refs[:6]
        gathered_refs, (vprev_ref, *sems) = refs[6:6 + nl], refs[6 + nl:]

        @pl.when(pl.program_id(0) == 0)
        def _():
            vprev_ref[...] = jnp.zeros_like(vprev_ref)
            _exchange_start(shard_refs, gathered_refs, *sems, False)

        @pl.when(pl.program_id(0) == nt - 1)
        def _():
            _exchange_wait(shard_refs, gathered_refs, *sems, False)

        xf = x_ref[...]
        xn, _ = _rms(xf)
        n1 = (xn * sm_ref[0:1, :]).astype(BF16)
        n1_ref[...] = n1
        row = lax.broadcasted_iota(jnp.int32, (ts, half), 0)
        ya = jnp.zeros((ts, d), F32)
        for hh in range(n_half):
            cols = slice(hh * half, (hh + 1) * half)
            parts = []
            for part in range(4):
                j = part * n_half + hh
                pj = _dot(n1, win_ref[j])
                proj_ref[:, j * half:(j + 1) * half] = pj.astype(BF16)
                parts.append(pj)
            b, c, u, z = parts
            v = c * u
            last1, last2 = vprev_ref[7:8, cols], vprev_ref[6:7, cols]
            v1 = jnp.where(row == 0, last1, pltpu.roll(v, 1, 0))
            v2 = jnp.where(row == 0, last2, jnp.where(row == 1, last1, pltpu.roll(v, 2, 0)))
            vprev_ref[:, cols] = v[ts - 8:ts, :]
            conv = sm_ref[1:2, cols] * v2 + sm_ref[2:3, cols] * v1 + sm_ref[3:4, cols] * v
            conv_ref[:, cols] = conv.astype(BF16)
            yh = (b * conv * _silu(z)[0]).astype(BF16)
            y_ref[:, cols] = yh
            ya = ya + _dot(yh, wout_ref[cols, :])
        ya_ref[...] = ya
        h1_ref[...] = xf + _rms(ya)[0] * sm_ref[4:5, :]

    outs = pl.pallas_call(
        body,
        name="layer_a_fwd",
        grid=(nt,),
        in_specs=[_rows(ts, d), _full(sm.shape), _full(win_g.shape), _full(wout.shape)] + [HBM_SPEC] * nl,
        out_specs=[_rows(ts, d), _rows(ts, d), _rows(ts, 4 * width), _rows(ts, width), _rows(ts, width), _rows(ts, d)]
        + [HBM_SPEC] * nl,
        out_shape=[
            jax.ShapeDtypeStruct((seq, d), F32),
            jax.ShapeDtypeStruct((seq, d), BF16),
            jax.ShapeDtypeStruct((seq, 4 * width), BF16),
            jax.ShapeDtypeStruct((seq, width), BF16),
            jax.ShapeDtypeStruct((seq, width), BF16),
            jax.ShapeDtypeStruct((seq, d), F32),
        ] + [jax.ShapeDtypeStruct((N_DEV,) + s.shape, s.dtype) for s in later],
        scratch_shapes=[pltpu.VMEM((8, width), F32)] + _exchange_sems(nl),
        compiler_params=_params(("arbitrary",), 56),
    )(x2, sm, win_g, wout, *later)
    return outs[:6], outs[6:]


def _layer_b_in(h1, kvn, bpre, wkv, wbin_g, ts):
    seq, d = h1.shape
    kvw = wkv.shape[1]
    cw = wbin_g.shape[2]
    aw = N_Q_HEADS * HEAD_DIM
    per = aw // cw

    def body(h1_ref, kvn_ref, bpre_ref, wkv_ref, wbin_ref, n3_ref, n4_ref, kv_ref, q_ref, z2_ref):
        hn, _ = _rms(h1_ref[...])
        n3 = (hn * kvn_ref[...]).astype(BF16)
        n4 = (hn * bpre_ref[...]).astype(BF16)
        n3_ref[...] = n3
        n4_ref[...] = n4
        kv_ref[...] = _dot(n3, wkv_ref[...]).astype(BF16)
        for j in range(N_DEV):
            pj = _dot(n4, wbin_ref[j])
            if j < per:
                q_ref[:, j * cw:(j + 1) * cw] = pj.astype(BF16)
            else:
                z2_ref[:, (j - per) * cw:(j - per + 1) * cw] = pj

    return pl.pallas_call(
        body,
        name="layer_b_in",
        grid=(seq // ts,),
        in_specs=[_rows(ts, d), _full(kvn.shape), _full(bpre.shape), _full(wkv.shape), _full(wbin_g.shape)],
        out_specs=[_rows(ts, d), _rows(ts, d), _rows(ts, kvw), _rows(ts, aw), _rows(ts, aw)],
        out_shape=[
            jax.ShapeDtypeStruct((seq, d), BF16),
            jax.ShapeDtypeStruct((seq, d), BF16),
            jax.ShapeDtypeStruct((seq, kvw), BF16),
            jax.ShapeDtypeStruct((seq, aw), BF16),
            jax.ShapeDtypeStruct((seq, aw), F32),
        ],
        compiler_params=_params(("parallel",), 48),
    )(h1, kvn, bpre, wkv, wbin_g)


N_PAIRS = N_Q_HEADS // 2
BAND = 2 * BLOCK


def _bias_table(rel_bias, bucket_t, in_window_t):
    def body(rb_ref, bucket_ref, win_ref, out_ref):
        bk = bucket_ref[...]
        inside = win_ref[...] != 0
        has_prev = lax.broadcasted_iota(jnp.int32, bk.shape, 0) >= BLOCK
        for h in range(N_Q_HEADS):
            acc = jnp.full(bk.shape, NEG_INF, F32)
            for b in range(N_BUCKETS):
                acc = jnp.where(jnp.logical_and(bk == b, inside), rb_ref[b, h], acc)
            cols = slice((h % 2) * BLOCK, (h % 2 + 1) * BLOCK)
            out_ref[1, h // 2, :, cols] = acc
            out_ref[0, h // 2, :, cols] = jnp.where(has_prev, acc, NEG_INF)

    vmem = pl.BlockSpec(memory_space=pltpu.VMEM)
    return pl.pallas_call(
        body,
        name="bias_table",
        in_specs=[pl.BlockSpec(memory_space=pltpu.SMEM), vmem, vmem],
        out_specs=vmem,
        out_shape=jax.ShapeDtypeStruct((2, N_PAIRS, BAND, 2 * BLOCK), F32),
    )(rel_bias, bucket_t, in_window_t)


def _bias_spec(biasm):
    return pl.BlockSpec((None,) + biasm.shape[1:], lambda i: (jnp.minimum(i, 1), 0, 0, 0))


def _banded_kv(kvp_ref, kvc_ref):
    kvp = kvp_ref[...].astype(F32)
    kvc = kvc_ref[...].astype(F32)
    kw = N_KV_HEADS * HEAD_DIM
    out = []
    for full in (jnp.concatenate([kvp[:, :kw], kvc[:, :kw]], axis=0), jnp.concatenate([kvp[:, kw:], kvc[:, kw:]], axis=0)):
        lo = lax.broadcasted_iota(jnp.int32, full.shape, 1) < HEAD_DIM
        rolled = pltpu.roll(full, HEAD_DIM, 1)
        x2 = [jnp.where(lo, full, rolled).astype(BF16), jnp.where(lo, rolled, full).astype(BF16)]
        ft = full.T
        x2t = [jnp.concatenate([ft[kh * HEAD_DIM:(kh + 1) * HEAD_DIM]] * 2, axis=0).astype(BF16) for kh in range(N_KV_HEADS)]
        out += [x2, x2t]
    return out


def _pair_rows(ref, m, scale=None):
    both = ref[:, m * LANES:(m + 1) * LANES].astype(F32)
    if scale is not None:
        both = both * scale
    lo = lax.broadcasted_iota(jnp.int32, both.shape, 1) < HEAD_DIM
    zero = jnp.zeros_like(both)
    return jnp.concatenate([jnp.where(lo, both, zero), jnp.where(lo, zero, both)], axis=0).astype(BF16)


def _pair_cols(res_t):
    top = lax.broadcasted_iota(jnp.int32, (LANES, BLOCK), 0) < HEAD_DIM
    return jnp.where(top, res_t[:, :BLOCK], res_t[:, BLOCK:]).T


def _sink_row(sink_ref, m):
    first = lax.broadcasted_iota(jnp.int32, (1, 2 * BLOCK), 1) < BLOCK
    return jnp.where(first, sink_ref[0, 2 * m], sink_ref[0, 2 * m + 1])


def _probs_t(k2, qpair, bias, sink):
    return _softmax_t(_dot_nt(k2, qpair) + bias, sink)


def _softmax_t(logits, sink):
    mx = jnp.maximum(jnp.max(logits, axis=0, keepdims=True), sink)
    p = jnp.exp(logits - mx)
    sink_p = jnp.exp(sink - mx)
    inv = 1.0 / (jnp.sum(p, axis=0, keepdims=True) + sink_p)
    return p * inv, sink_p * inv


def _attn_fwd(q, kv, z2, biasm, sinks):
    seq, aw = q.shape
    kvw = kv.shape[1]
    nb = seq // BLOCK

    def body(sink_ref, q_ref, kvc_ref, kvp_ref, z2_ref, bias_ref, attn_ref, o_ref, acc_ref):
        k2, _, _, v2t = _banded_kv(kvp_ref, kvc_ref)
        kv_of = lambda m: (2 * m) // GROUP
        logits, probs = {}, {}
        for step in range(N_PAIRS + 2):
            if step < N_PAIRS:
                logits[step] = _dot_nt(k2[kv_of(step)], _pair_rows(q_ref, step, SCALE)) + bias_ref[step]
            m = step - 1
            if 0 <= m < N_PAIRS:
                probs[m] = _softmax_t(logits.pop(m), _sink_row(sink_ref, m))[0].astype(BF16)
            m = step - 2
            if 0 <= m < N_PAIRS:
                acc_ref[:, m * LANES:(m + 1) * LANES] = _pair_cols(_dot(v2t[kv_of(m)], probs.pop(m)))
        attn = acc_ref[...]
        attn_ref[...] = attn.astype(BF16)
        o_ref[...] = (attn * _silu(z2_ref[...])[0]).astype(BF16)

    blk = lambda w: pl.BlockSpec((BLOCK, w), lambda i: (i, 0))
    return pl.pallas_call(
        body,
        name="attn_fwd",
        grid=(nb,),
        in_specs=[
            pl.BlockSpec(memory_space=pltpu.SMEM),
            blk(aw),
            blk(kvw),
            pl.BlockSpec((BLOCK, kvw), lambda i: (jnp.maximum(i - 1, 0), 0)),
            blk(aw),
            _bias_spec(biasm),
        ],
        out_specs=[blk(aw), blk(aw)],
        out_shape=[jax.ShapeDtypeStruct((seq, aw), BF16), jax.ShapeDtypeStruct((seq, aw), BF16)],
        scratch_shapes=[pltpu.VMEM((BLOCK, aw), F32)],
        compiler_params=_params(("arbitrary",), 32),
    )(sinks, q, kv, kv, z2, biasm)


def _layer_b_out(o, attn, z2, h1, target, wbout, bpost, ts):
    seq, d = h1.shape
    aw = o.shape[1]

    def body(o_ref, attn_ref, z2_ref, h1_ref, tgt_ref, w_ref, g_ref, dh2_ref, dyb_ref, dattn_ref, dz2_ref, acc_ref):
        @pl.when(pl.program_id(0) == 0)
        def _():
            acc_ref[...] = jnp.zeros_like(acc_ref)

        w = w_ref[...]
        yb = _dot(o_ref[...], w)
        ybn, r = _rms(yb)
        g = g_ref[...]
        diff = h1_ref[...] + ybn * g - tgt_ref[...]
        dh2 = diff * (1.0 / d)
        dh2_ref[...] = dh2
        acc_ref[0:1, :] += jnp.sum(dh2 * ybn, axis=0, keepdims=True)
        tok = jnp.mean(diff * diff, axis=-1, keepdims=True)
        acc_ref[1:2, :] += 0.5 * jnp.sum(tok, axis=0, keepdims=True)
        dyb = _rms_bwd(dh2 * g, ybn, r).astype(BF16)
        dyb_ref[...] = dyb
        do = _dot_nt(dyb, w)
        sz, dsz = _silu(z2_ref[...])
        dattn_ref[...] = (do * sz).astype(BF16)
        dz2_ref[...] = (do * attn_ref[...].astype(F32) * dsz).astype(BF16)

    return pl.pallas_call(
        body,
        name="layer_b_out",
        grid=(seq // ts,),
        in_specs=[_rows(ts, aw), _rows(ts, aw), _rows(ts, aw), _rows(ts, d), _rows(ts, d), _full(wbout.shape), _full(bpost.shape)],
        out_specs=[_rows(ts, d), _rows(ts, d), _rows(ts, aw), _rows(ts, aw), _resident((8, d))],
        out_shape=[
            jax.ShapeDtypeStruct((seq, d), F32),
            jax.ShapeDtypeStruct((seq, d), BF16),
            jax.ShapeDtypeStruct((seq, aw), BF16),
            jax.ShapeDtypeStruct((seq, aw), BF16),
            jax.ShapeDtypeStruct((8, d), F32),
        ],
        compiler_params=_params(("arbitrary",), 48),
    )(o, attn, z2, h1, target, wbout, bpost)


def _attn_bwd(q, kv, dattn, biasm, sinks, ready):
    seq, aw = q.shape
    kvw = kv.shape[1]
    kw = N_KV_HEADS * HEAD_DIM
    nb = seq // BLOCK
    pairs_per_kv = N_PAIRS // N_KV_HEADS
    nr = len(ready)

    def body(sink_ref, q_ref, kvc_ref, kvp_ref, da_ref, bias_ref, *refs):
        ready_refs, (dq_ref, dkv_ref, dssum_ref, dsink_ref) = refs[:nr], refs[nr:nr + 4]
        landed_refs, (carry_ref, qs_ref, dos_ref, dst_ref, pt_ref, *sems) = refs[nr + 4:2 * nr + 4], refs[2 * nr + 4:]
        i = pl.program_id(0)

        @pl.when(i == 0)
        def _():
            dssum_ref[...] = jnp.zeros_like(dssum_ref)
            dsink_ref[...] = jnp.zeros_like(dsink_ref)
            carry_ref[...] = jnp.zeros_like(carry_ref)
            _exchange_start(ready_refs, landed_refs, *sems, True)

        @pl.when(i == nb)
        def _():
            _exchange_wait(ready_refs, landed_refs, *sems, True)

        @pl.when(i < nb)
        def _():
            lo = lax.broadcasted_iota(jnp.int32, (BAND, LANES), 1) < HEAD_DIM
            head_lane = lax.broadcasted_iota(jnp.int32, (1, LANES), 1)
            k2, k2t, v2, _ = _banded_kv(kvp_ref, kvc_ref)
            dsink = jnp.zeros((1, LANES), F32)
            folded = []
            logits, dps, dsbs = {}, {}, {}
            for step in range(N_PAIRS + 2):
                if step < N_PAIRS:
                    kh, rows = step // pairs_per_kv, slice((step % pairs_per_kv) * BAND, (step % pairs_per_kv + 1) * BAND)
                    qpair = _pair_rows(q_ref, step, SCALE)
                    dopair = _pair_rows(da_ref, step)
                    qs_ref[kh, rows, :] = qpair
                    dos_ref[kh, rows, :] = dopair
                    logits[step] = _dot_nt(k2[kh], qpair) + bias_ref[step]
                    dps[step] = _dot_nt(v2[kh], dopair)
                m = step - 1
                if 0 <= m < N_PAIRS:
                    kh, rows = m // pairs_per_kv, slice((m % pairs_per_kv) * BAND, (m % pairs_per_kv + 1) * BAND)
                    pn, sink_p = _softmax_t(logits.pop(m), _sink_row(sink_ref, m))
                    dp = dps.pop(m)
                    delta = jnp.sum(pn * dp, axis=0, keepdims=True)
                    ds = pn * (dp - delta)
                    dssum_ref[m] += ds
                    sink_term = sink_p * delta
                    for e in range(2):
                        total = jnp.sum(sink_term[:, e * BLOCK:(e + 1) * BLOCK], axis=1, keepdims=True)
                        dsink = dsink - jnp.where(head_lane == 2 * m + e, total, 0.0)
                    dsbs[m] = ds.astype(BF16)
                    dst_ref[kh, :, rows] = dsbs[m]
                    pt_ref[kh, :, rows] = pn.astype(BF16)
                m = step - 2
                if 0 <= m < N_PAIRS:
                    kh = m // pairs_per_kv
                    dq_ref[:, m * LANES:(m + 1) * LANES] = (_pair_cols(_dot(k2t[kh], dsbs.pop(m))) * SCALE).astype(BF16)
                    if m % pairs_per_kv == pairs_per_kv - 1:
                        for lhs_ref, rhs_ref in ((dst_ref, qs_ref), (pt_ref, dos_ref)):
                            acc = _dot(lhs_ref[kh], rhs_ref[kh])
                            folded.append(acc + pltpu.roll(acc, HEAD_DIM, 1))
            dsink_ref[0:1, :] += dsink
            dk = jnp.where(lo, folded[0], folded[2])
            dv = jnp.where(lo, folded[1], folded[3])

            @pl.when(i > 0)
            def _():
                dkv_ref[:, :kw] = (carry_ref[:, :kw] + dk[:BLOCK]).astype(BF16)
                dkv_ref[:, kw:] = (carry_ref[:, kw:] + dv[:BLOCK]).astype(BF16)

            carry_ref[:, :kw] = dk[BLOCK:]
            carry_ref[:, kw:] = dv[BLOCK:]

        @pl.when(i == nb)
        def _():
            dkv_ref[...] = carry_ref[...].astype(BF16)

    last = nb - 1
    blk = lambda w: pl.BlockSpec((BLOCK, w), lambda i: (jnp.minimum(i, last), 0))
    outs = pl.pallas_call(
        body,
        name="attn_bwd",
        grid=(nb + 1,),
        in_specs=[
            pl.BlockSpec(memory_space=pltpu.SMEM),
            blk(aw),
            blk(kvw),
            pl.BlockSpec((BLOCK, kvw), lambda i: (jnp.clip(i - 1, 0, last), 0)),
            blk(aw),
            _bias_spec(biasm),
        ] + [HBM_SPEC] * nr,
        out_specs=[
            blk(aw),
            pl.BlockSpec((BLOCK, kvw), lambda i: (jnp.maximum(i - 1, 0), 0)),
            _resident(biasm.shape[1:]),
            _resident((8, LANES)),
        ] + [HBM_SPEC] * nr,
        out_shape=[
            jax.ShapeDtypeStruct((seq, aw), BF16),
            jax.ShapeDtypeStruct((seq, kvw), BF16),
            jax.ShapeDtypeStruct(biasm.shape[1:], F32),
            jax.ShapeDtypeStruct((8, LANES), F32),
        ] + [jax.ShapeDtypeStruct(g.shape, g.dtype) for g in ready],
        scratch_shapes=[
            pltpu.VMEM((BLOCK, kvw), F32),
            pltpu.VMEM((N_KV_HEADS, pairs_per_kv * BAND, LANES), BF16),
            pltpu.VMEM((N_KV_HEADS, pairs_per_kv * BAND, LANES), BF16),
            pltpu.VMEM((N_KV_HEADS, BAND, pairs_per_kv * BAND), BF16),
            pltpu.VMEM((N_KV_HEADS, BAND, pairs_per_kv * BAND), BF16),
        ] + _exchange_sems(nr),
        compiler_params=_params(("arbitrary",), 40),
    )(sinks, q, kv, kv, dattn, biasm, *ready)
    return outs[:4], outs[4:]


def _relbias_grad(dssum2, onehot, chunk):
    heads, n = dssum2.shape

    def body(a_ref, oh_ref, out_ref):
        @pl.when(pl.program_id(0) == 0)
        def _():
            out_ref[...] = jnp.zeros_like(out_ref)

        a = a_ref[...]
        hi = a.astype(BF16)
        lo = (a - hi.astype(F32)).astype(BF16)
        out_ref[...] += _dot(hi, oh_ref[...]) + _dot(lo, oh_ref[...])

    return pl.pallas_call(
        body,
        name="relbias_grad",
        grid=(n // chunk,),
        in_specs=[pl.BlockSpec((heads, chunk), lambda i: (0, i)), pl.BlockSpec((chunk, LANES), lambda i: (i, 0))],
        out_specs=_resident((heads, LANES)),
        out_shape=jax.ShapeDtypeStruct((heads, LANES), F32),
        compiler_params=_params(("arbitrary",), 32),
    )(dssum2, onehot)


def _layer_b_in_bwd(dh2, dq, dz2, dkv, h1, ya, wbin_g, wkv, kvn, bpre, sm, ready, ts):
    seq, d = h1.shape
    aw = dq.shape[1]
    kvw = dkv.shape[1]
    cw = wbin_g.shape[2]
    per = aw // cw

    nr = len(ready)
    nt = seq // ts

    def body(dh2_ref, dq_ref, dz2_ref, dkv_ref, h1_ref, ya_ref, wbin_ref, wkv_ref, kvn_ref, bpre_ref, sm_ref, *refs):
        ready_refs, (dh1_ref, dya_ref, acc_ref) = refs[:nr], refs[nr:nr + 3]
        landed_refs, sems = refs[nr + 3:2 * nr + 3], refs[2 * nr + 3:]

        @pl.when(pl.program_id(0) == 0)
        def _():
            acc_ref[...] = jnp.zeros_like(acc_ref)
            _exchange_start(ready_refs, landed_refs, *sems, True)

        @pl.when(pl.program_id(0) == nt - 1)
        def _():
            _exchange_wait(ready_refs, landed_refs, *sems, True)

        dn4 = jnp.zeros((ts, d), F32)
        for j in range(N_DEV):
            src = dq_ref if j < per else dz2_ref
            jj = j % per
            dn4 = dn4 + _dot_nt(src[:, jj * cw:(jj + 1) * cw], wbin_ref[j])
        dn3 = _dot_nt(dkv_ref[...], wkv_ref[...])
        hn, r = _rms(h1_ref[...])
        acc_ref[0:1, :] += jnp.sum(dn4 * hn, axis=0, keepdims=True)
        acc_ref[1:2, :] += jnp.sum(dn3 * hn, axis=0, keepdims=True)
        dh1 = dh2_ref[...] + _rms_bwd(dn4 * bpre_ref[...] + dn3 * kvn_ref[...], hn, r)
        dh1_ref[...] = dh1
        yan, r2 = _rms(ya_ref[...])
        acc_ref[2:3, :] += jnp.sum(dh1 * yan, axis=0, keepdims=True)
        dya_ref[...] = _rms_bwd(dh1 * sm_ref[4:5, :], yan, r2).astype(BF16)

    outs = pl.pallas_call(
        body,
        name="layer_b_in_bwd",
        grid=(nt,),
        in_specs=[_rows(ts, d), _rows(ts, aw), _rows(ts, aw), _rows(ts, kvw), _rows(ts, d), _rows(ts, d),
                  _full(wbin_g.shape), _full(wkv.shape), _full(kvn.shape), _full(bpre.shape), _full(sm.shape)]
        + [HBM_SPEC] * nr,
        out_specs=[_rows(ts, d), _rows(ts, d), _resident((8, d))] + [HBM_SPEC] * nr,
        out_shape=[jax.ShapeDtypeStruct((seq, d), F32), jax.ShapeDtypeStruct((seq, d), BF16),
                   jax.ShapeDtypeStruct((8, d), F32)] + [jax.ShapeDtypeStruct(g.shape, g.dtype) for g in ready],
        scratch_shapes=_exchange_sems(nr),
        compiler_params=_params(("arbitrary",), 48),
    )(dh2, dq, dz2, dkv, h1, ya, wbin_g, wkv, kvn, bpre, sm, *ready)
    return outs[:3], outs[3:]


def _layer_a_bwd(dya, proj, conv, dh1, x2, wout, win_g, sm, ts):
    seq, d = x2.shape
    width = wout.shape[0]
    half = win_g.shape[2]
    n_half = width // half
    nt = seq // ts

    def body(dya_ref, proj_ref, conv_ref, dh1_ref, x_ref, wout_ref, win_ref, sm_ref, dproj_ref, gx_ref, acc_ref,
             dnext_ref):
        @pl.when(pl.program_id(0) == 0)
        def _():
            acc_ref[...] = jnp.zeros_like(acc_ref)
            dnext_ref[...] = jnp.zeros_like(dnext_ref)

        dy = _dot_nt(dya_ref[...], wout_ref[...])
        row = lax.broadcasted_iota(jnp.int32, (ts, half), 0)
        dn1 = jnp.zeros((ts, d), F32)
        for hh in range(n_half):
            cols = slice(hh * half, (hh + 1) * half)
            b, c, u, z = [proj_ref[:, (part * n_half + hh) * half:(part * n_half + hh + 1) * half].astype(F32)
                          for part in range(4)]
            cv = conv_ref[:, cols].astype(F32)
            dyh = dy[:, cols]
            sz, dsz = _silu(z)
            dconv = dyh * b * sz
            grads = [dyh * cv * sz, None, None, dyh * b * cv * dsz]
            next0, next1 = dnext_ref[0:1, cols], dnext_ref[1:2, cols]
            dc1 = jnp.where(row == ts - 1, next0, pltpu.roll(dconv, ts - 1, 0))
            dc2 = jnp.where(row == ts - 1, next1, jnp.where(row == ts - 2, next0, pltpu.roll(dconv, ts - 2, 0)))
            dnext_ref[:, cols] = dconv[0:8, :]
            v = c * u
            acc_ref[1:2, cols] += jnp.sum(dc2 * v, axis=0, keepdims=True)
            acc_ref[2:3, cols] += jnp.sum(dc1 * v, axis=0, keepdims=True)
            acc_ref[3:4, cols] += jnp.sum(dconv * v, axis=0, keepdims=True)
            dv = sm_ref[3:4, cols] * dconv + sm_ref[2:3, cols] * dc1 + sm_ref[1:2, cols] * dc2
            grads[1] = dv * u
            grads[2] = dv * c
            for part in range(4):
                j = part * n_half + hh
                gj = grads[part].astype(BF16)
                dproj_ref[:, j * half:(j + 1) * half] = gj
                dn1 = dn1 + _dot_nt(gj, win_ref[j])
        xn, r = _rms(x_ref[...])
        acc_ref[0:1, :] += jnp.sum(dn1 * xn, axis=0, keepdims=True)
        gx_ref[...] = dh1_ref[...] + _rms_bwd(dn1 * sm_ref[0:1, :], xn, r)

    rev = lambda w: pl.BlockSpec((ts, w), lambda i: (nt - 1 - i, 0))
    return pl.pallas_call(
        body,
        name="layer_a_bwd",
        grid=(nt,),
        in_specs=[rev(d), rev(4 * width), rev(width), rev(d), rev(d), _full(wout.shape), _full(win_g.shape), _full(sm.shape)],
        out_specs=[rev(4 * width), rev(d), _resident((8, d))],
        out_shape=[jax.ShapeDtypeStruct((seq, 4 * width), BF16), jax.ShapeDtypeStruct((seq, d), F32),
                   jax.ShapeDtypeStruct((8, d), F32)],
        scratch_shapes=[pltpu.VMEM((8, width), F32)],
        compiler_params=_params(("arbitrary",), 56),
    )(dya, proj, conv, dh1, x2, wout, win_g, sm)


def _wgrad(a, bs, n_slots, ts, name, ready=()):
    nr = len(ready)
    seq, k = a.shape
    nb_in = len(bs)
    n_each = bs[0].shape[1]
    n = nb_in * n_each
    bn = min(n_each, 1024)
    per_in = n_each // bn
    n_blocks = nb_in * per_in
    ns = seq // ts

    def b_spec(idx):
        def index(j, s):
            mine = j // per_in == idx
            row = jnp.where(mine, s, jnp.where(j // per_in > idx, ns - 1, 0))
            return (row, jnp.where(mine, j % per_in, jnp.where(j // per_in > idx, per_in - 1, 0)))
        return pl.BlockSpec((ts, bn), index)

    if n_slots:
        sw = n // n_slots
        spb = bn // sw
        out_shape = jax.ShapeDtypeStruct((n_slots, k, sw), BF16)
        out_spec = pl.BlockSpec((spb, k, sw), lambda j, s: (j, 0, 0))
    else:
        out_shape = jax.ShapeDtypeStruct((k, n), BF16)
        out_spec = pl.BlockSpec((k, bn), lambda j, s: (0, j))

    def body(a_ref, *refs):
        b_refs, ready_refs, o_ref = refs[:nb_in], refs[nb_in:nb_in + nr], refs[nb_in + nr]
        landed_refs, (acc_ref, *sems) = refs[nb_in + nr + 1:nb_in + 2 * nr + 1], refs[nb_in + 2 * nr + 1:]
        j, s = pl.program_id(0), pl.program_id(1)

        if nr:
            @pl.when(jnp.logical_and(j == 0, s == 0))
            def _():
                _exchange_start(ready_refs, landed_refs, *sems, True)

            @pl.when(jnp.logical_and(j == n_blocks - 1, s == ns - 1))
            def _():
                _exchange_wait(ready_refs, landed_refs, *sems, True)

        @pl.when(s == 0)
        def _():
            acc_ref[...] = jnp.zeros_like(acc_ref)

        for idx in range(nb_in):
            @pl.when(j // per_in == idx)
            def _(idx=idx):
                acc_ref[...] += _dot_tn(a_ref[...], b_refs[idx][...])

        @pl.when(s == ns - 1)
        def _():
            if n_slots:
                for e in range(spb):
                    o_ref[e] = acc_ref[:, e * sw:(e + 1) * sw].astype(BF16)
            else:
                o_ref[...] = acc_ref[...].astype(BF16)

    outs = pl.pallas_call(
        body,
        name=name,
        grid=(n_blocks, ns),
        in_specs=[pl.BlockSpec((ts, k), lambda j, s: (s, 0))] + [b_spec(idx) for idx in range(nb_in)] + [HBM_SPEC] * nr,
        out_specs=[out_spec] + [HBM_SPEC] * nr,
        out_shape=[out_shape] + [jax.ShapeDtypeStruct(g.shape, g.dtype) for g in ready],
        scratch_shapes=[pltpu.VMEM((k, bn), F32)] + (_exchange_sems(nr) if nr else []),
        compiler_params=_params(("arbitrary", "arbitrary"), 48),
    )(a, *bs, *ready)
    return (outs[0], outs[1:]) if nr else outs[0]


def _adamw(ws, gs, ms, vs):
    n = len(ws)

    def step(w, g, m, v):
        m = ADAM_B1 * m + (1.0 - ADAM_B1) * g
        v = ADAM_B2 * v + (1.0 - ADAM_B2) * jnp.square(g)
        m_hat = m / (1.0 - ADAM_B1 ** ADAM_STEP)
        v_hat = v / (1.0 - ADAM_B2 ** ADAM_STEP)
        return -ADAM_LR * (m_hat / (jnp.sqrt(v_hat) + ADAM_EPS) + ADAM_WD * w), m, v

    def body(*refs):
        w_refs, g_refs, m_refs, v_refs = (refs[k * n:(k + 1) * n] for k in range(4))
        d_refs, nm_refs, nv_refs = (refs[(4 + k) * n:(5 + k) * n] for k in range(3))
        for t in range(n):
            rows = w_refs[t].shape[0]
            if rows <= 128:
                d_refs[t][...], nm_refs[t][...], nv_refs[t][...] = step(
                    w_refs[t][...], g_refs[t][...], m_refs[t][...], v_refs[t][...])
                continue
            chunk = 128

            def one(i, carry, t=t):
                r = pl.ds(pl.multiple_of(i * chunk, chunk), chunk)
                d_refs[t][r, :], nm_refs[t][r, :], nv_refs[t][r, :] = step(
                    w_refs[t][r, :], g_refs[t][r, :], m_refs[t][r, :], v_refs[t][r, :])
                return carry

            lax.fori_loop(0, rows // chunk, one, 0)

    vmem = pl.BlockSpec(memory_space=pltpu.VMEM)
    outs = pl.pallas_call(
        body,
        name="adamw",
        in_specs=[vmem] * (4 * n),
        out_specs=[vmem] * (3 * n),
        out_shape=[jax.ShapeDtypeStruct(w.shape, F32) for w in ws] * 3,
        compiler_params=_params(vmem_mib=56),
    )(*ws, *gs, *ms, *vs)
    return outs[:n], outs[n:2 * n], outs[2 * n:]


def _band_structure():
    q_loc = jnp.arange(BLOCK, dtype=jnp.int32)[:, None]
    s_loc = jnp.arange(2 * BLOCK, dtype=jnp.int32)[None, :]
    dist = q_loc + BLOCK - s_loc
    in_window = (dist >= 0) & (dist < BLOCK)
    dd = jnp.maximum(dist, 0)
    max_exact = N_BUCKETS // 2
    large = max_exact + (jnp.log(jnp.maximum(dd, 1).astype(F32) / max_exact) / math.log(MAX_DISTANCE / max_exact)
                         * (N_BUCKETS - max_exact)).astype(jnp.int32)
    bucket = jnp.where(dd < max_exact, dd, jnp.minimum(large, N_BUCKETS - 1))
    onehot = (bucket.reshape(-1, 1) == jnp.arange(LANES, dtype=jnp.int32)[None, :]).astype(BF16)
    return bucket, in_window.astype(jnp.int32), onehot


def _place_rows(a, row, rows=8):
    return jnp.pad(a, ((row, rows - row - a.shape[0]), (0, 0)))


def kernel(x, a_pre_norm, a_w_in, a_conv_w, a_w_out, a_post_norm, kv_norm, w_kv, rel_bias, b_pre_norm, b_w_in, b_sinks, b_w_out, b_post_norm, loss_target, m_a_pre_norm, m_a_w_in, m_a_conv_w, m_a_w_out, m_a_post_norm, m_kv_norm, m_w_kv, m_rel_bias, m_b_pre_norm, m_b_w_in, m_b_sinks, m_b_w_out, m_b_post_norm, v_a_pre_norm, v_a_w_in, v_a_conv_w, v_a_w_out, v_a_post_norm, v_kv_norm, v_w_kv, v_rel_bias, v_b_pre_norm, v_b_w_in, v_b_sinks, v_b_w_out, v_b_post_norm):
    seq, d = x.shape[1], x.shape[2]
    x2 = x.reshape(seq, d)
    target = loss_target.reshape(seq, d)
    shard = a_pre_norm.shape[1]
    me = _my_index()
    ts_a = min(seq, 512)
    ts = min(seq, 512)
    ts_w = min(seq, 2048)

    small = _place_rows(a_pre_norm, 0) + _place_rows(a_conv_w[0], 1) + _place_rows(a_post_norm, 4)
    win_g, wout_g, small_g = _all_gather([a_w_in[0], a_w_out[0], small], [BF16, BF16, F32])
    wout = wout_g.reshape(-1, wout_g.shape[2])
    sm = small_g.transpose(1, 0, 2).reshape(8, N_DEV * shard)
    kvn = kv_norm.reshape(1, d)

    (h1, n1, proj, conv, y, ya), (wkv_g, wbin_g, wbout_g) = _layer_a_fwd(
        x2, sm, win_g, wout, [w_kv.astype(BF16), b_w_in[0].astype(BF16), b_w_out[0].astype(BF16)], ts_a)
    wkv = wkv_g.reshape(-1, wkv_g.shape[2])
    wbout = wbout_g.reshape(-1, wbout_g.shape[2])
    n3, n4, kv, q, z2 = _layer_b_in(h1, kvn, b_pre_norm, wkv, wbin_g, ts)
    bucket, in_window, onehot = _band_structure()
    biasm = _bias_table(rel_bias, bucket.T, in_window.T)
    attn, o = _attn_fwd(q, kv, z2, biasm, b_sinks)
    dh2, dyb, dattn, dz2, acc_c = _layer_b_out(o, attn, z2, h1, target, wbout, b_post_norm, ts)

    g_wbout = _wgrad(o, [dyb], 0, ts_w, "wgrad_b_out").reshape(wbout_g.shape)
    (dq, dkv, dssum, dsink), (l_wbout,) = _attn_bwd(q, kv, dattn, biasm, b_sinks, [g_wbout])
    by_head = dssum.reshape(N_PAIRS, BAND, 2, BLOCK).transpose(0, 2, 3, 1)
    relb = _relbias_grad(by_head.reshape(N_Q_HEADS, -1), onehot, 4096)
    g_wkv = _wgrad(n3, [dkv], 0, ts_w, "wgrad_kv").reshape(wkv_g.shape)
    g_wbin = _wgrad(n4, [dq, dz2], N_DEV, ts_w, "wgrad_b_in")
    (dh1, dya, acc_b), (l_wkv, l_wbin) = _layer_b_in_bwd(
        dh2, dq, dz2, dkv, h1, ya, wbin_g, wkv, kvn, b_pre_norm, sm, [g_wkv, g_wbin], ts)
    dproj, gx, acc_a = _layer_a_bwd(dya, proj, conv, dh1, x2, wout, win_g, sm, ts_a)
    g_wout = _wgrad(y, [dya], 0, ts_w, "wgrad_a_out").reshape(wout_g.shape)
    g_win, (l_wout,) = _wgrad(n1, [dproj], N_DEV, ts_w, "wgrad_a_in", [g_wout])

    r_win, (r_wout, r_wkv, r_wbin, r_wbout), (s_a, s_b, s_c, s_relb, s_sink) = _reduce_exchange(
        g_win, [l_wout, l_wkv, l_wbin, l_wbout], [acc_a, acc_b, acc_c, relb, dsink])
    mine = lambda rows: lax.dynamic_slice_in_dim(rows, me * shard, shard, axis=1)
    loss = s_c[1, 0]
    weights = [a_pre_norm, a_w_in[0], a_conv_w[0], a_w_out[0], a_post_norm, kvn, w_kv, rel_bias, b_pre_norm,
               b_w_in[0], b_sinks, b_w_out[0], b_post_norm]
    grads = [mine(s_a[0:1]), r_win, mine(s_a[1:4]), r_wout, mine(s_b[2:3]), s_b[1:2], r_wkv,
             s_relb[:, :N_BUCKETS].T, s_b[0:1], r_wbin, s_sink[0:1, :N_Q_HEADS], r_wbout, s_c[0:1]]
    first = [m_a_pre_norm, m_a_w_in[0], m_a_conv_w[0], m_a_w_out[0], m_a_post_norm, m_kv_norm.reshape(1, d), m_w_kv,
             m_rel_bias, m_b_pre_norm, m_b_w_in[0], m_b_sinks, m_b_w_out[0], m_b_post_norm]
    second = [v_a_pre_norm, v_a_w_in[0], v_a_conv_w[0], v_a_w_out[0], v_a_post_norm, v_kv_norm.reshape(1, d), v_w_kv,
              v_rel_bias, v_b_pre_norm, v_b_w_in[0], v_b_sinks, v_b_w_out[0], v_b_post_norm]
    deltas, new_m, new_v = _adamw(weights, grads, first, second)

    shapes = [a_pre_norm.shape, a_w_in.shape, a_conv_w.shape, a_w_out.shape, a_post_norm.shape, kv_norm.shape,
              w_kv.shape, rel_bias.shape, b_pre_norm.shape, b_w_in.shape, b_sinks.shape, b_w_out.shape, b_post_norm.shape]
    shaped = lambda arrays: [a.reshape(s) for a, s in zip(arrays, shapes)]
    return (loss, gx.reshape(x.shape), *shaped(grads), *shaped(deltas), *shaped(new_m), *shaped(new_v))
```

```python
import functools
import math

import jax
import jax.numpy as jnp
from jax import lax
from jax.experimental import pallas as pl
from jax.experimental.pallas import tpu as pltpu

HEAD_DIM = 64
N_Q_HEADS = 16
N_KV_HEADS = 2
GROUP = N_Q_HEADS // N_KV_HEADS
BLOCK = 128
N_BUCKETS = 32
MAX_DISTANCE = 128
EPS = 1e-6
NEG_INF = -1e30
SCALE = HEAD_DIM ** -0.5

ADAM_LR = 0.001
ADAM_B1 = 0.9
ADAM_B2 = 0.999
ADAM_EPS = 1e-08
ADAM_WD = 0.01
ADAM_STEP = 10

N_DEV = 8
LANES = 128
F32 = jnp.float32
BF16 = jnp.bfloat16
MESH = pl.DeviceIdType.MESH
MIB = 1024 * 1024


def _params(semantics=None, vmem_mib=48):
    return pltpu.CompilerParams(dimension_semantics=semantics, vmem_limit_bytes=vmem_mib * MIB)


def _full(shape):
    zeros = (0,) * len(shape)
    return pl.BlockSpec(shape, lambda *_: zeros, pipeline_mode=pl.Buffered(1))


def _resident(shape):
    zeros = (0,) * len(shape)
    return pl.BlockSpec(shape, lambda *_: zeros)


def _rows(ts, cols):
    return pl.BlockSpec((ts, cols), lambda i: (i, 0))


def _dot(a, b):
    return jnp.dot(a, b, preferred_element_type=F32)


def _dot_nt(a, b):
    return lax.dot_general(a, b, (((1,), (1,)), ((), ())), preferred_element_type=F32)


def _dot_tn(a, b):
    return lax.dot_general(a, b, (((0,), (0,)), ((), ())), preferred_element_type=F32)


def _rms(xf):
    r = lax.rsqrt(jnp.mean(xf * xf, axis=-1, keepdims=True) + EPS)
    return xf * r, r


def _rms_bwd(dn, xn, r):
    return r * (dn - xn * jnp.mean(dn * xn, axis=-1, keepdims=True))


def _silu(z):
    s = jax.nn.sigmoid(z)
    return z * s, s * (1.0 + z * (1.0 - s))


SUB_TILES = 2


def _skewed(ts, stages):
    sub = ts // SUB_TILES
    carry = {}
    for step in range(SUB_TILES + len(stages) - 1):
        for k in reversed(range(SUB_TILES)):
            stage = step - k
            if 0 <= stage < len(stages):
                rows = slice(k * sub, (k + 1) * sub)
                carry[k] = stages[stage](rows) if stage == 0 else stages[stage](rows, carry[k])


def _my_index():
    return 4 * lax.axis_index("x") + 2 * lax.axis_index("y") + lax.axis_index("c")


def _all_gather(shards, out_dtypes):
    n = len(shards)

    def body(*refs):
        ins, outs = refs[:n], refs[n:2 * n]
        send_sems, recv_sems = refs[2 * n], refs[2 * n + 1]
        x, y, c = lax.axis_index("x"), lax.axis_index("y"), lax.axis_index("c")
        me, sibling = (x, y, c), (x, y, 1 - c)
        chips = [(1 - x, y), (x, 1 - y), (1 - x, 1 - y)]

        def copy(t, k, block, to):
            rows = outs[t].at[4 * block[0] + 2 * block[1] + block[2]]
            return pltpu.make_async_remote_copy(
                src_ref=rows, dst_ref=rows, send_sem=send_sems.at[t, k], recv_sem=recv_sems.at[t, k],
                device_id=to, device_id_type=MESH)

        for t in range(n):
            outs[t][pl.ds(_my_index(), 1)] = ins[t][...].astype(outs[t].dtype)[None]
        first = []
        for t in range(n):
            first.append(copy(t, 0, me, sibling))
            first += [copy(t, 1 + j, me, (*chip, c)) for j, chip in enumerate(chips)]
        for cp in first:
            cp.start()
        passed = []
        for j, chip in enumerate(chips):
            for t in range(n):
                copy(t, 1 + j, (*chip, c), me).wait_recv()
                fwd = copy(t, 4 + j, (*chip, c), sibling)
                fwd.start()
                passed.append(fwd)
        for t in range(n):
            copy(t, 0, sibling, me).wait_recv()
        for j, chip in enumerate(chips):
            for t in range(n):
                copy(t, 4 + j, (*chip, 1 - c), me).wait_recv()
        for cp in first + passed:
            cp.wait_send()

    vmem = pl.BlockSpec(memory_space=pltpu.VMEM)
    return pl.pallas_call(
        body,
        name="gather_weights",
        out_shape=[jax.ShapeDtypeStruct((N_DEV,) + s.shape, dt) for s, dt in zip(shards, out_dtypes)],
        in_specs=[vmem] * n,
        out_specs=[vmem] * n,
        scratch_shapes=[pltpu.SemaphoreType.DMA((n, 7)), pltpu.SemaphoreType.DMA((n, 7))],
        compiler_params=_params(vmem_mib=48),
    )(*shards)


def _peer(k):
    x, y, c = lax.axis_index("x"), lax.axis_index("y"), lax.axis_index("c")
    px = 1 - x if k & 4 else x
    py = 1 - y if k & 2 else y
    pc = 1 - c if k & 1 else c
    return (px, py, pc), 4 * px + 2 * py + pc


def _exchange(srcs, dsts, send_sems, recv_sems, local_sems, scatter):
    me = _my_index()
    sends, arrivals = [], []
    for k in range(1, N_DEV):
        peer, pidx = _peer(k)
        for t, (src, dst) in enumerate(zip(srcs, dsts)):
            mine = src.at[pidx] if scatter else src
            sems = dict(send_sem=send_sems.at[t, k - 1], recv_sem=recv_sems.at[t, k - 1], device_id=peer, device_id_type=MESH)
            sends.append(pltpu.make_async_remote_copy(src_ref=mine, dst_ref=dst.at[me], **sems))
            arrivals.append(pltpu.make_async_remote_copy(src_ref=mine, dst_ref=dst.at[pidx], **sems))
    local = [pltpu.make_async_copy(src.at[me] if scatter else src, dst.at[me], local_sems.at[t])
             for t, (src, dst) in enumerate(zip(srcs, dsts))]
    return sends, arrivals, local


def _exchange_start(*args):
    sends, _, local = _exchange(*args)
    for cp in sends + local:
        cp.start()


def _exchange_wait(*args):
    sends, arrivals, local = _exchange(*args)
    for cp in arrivals:
        cp.wait_recv()
    for cp in sends:
        cp.wait_send()
    for cp in local:
        cp.wait()


def _exchange_sems(n):
    return [pltpu.SemaphoreType.DMA((n, N_DEV - 1)), pltpu.SemaphoreType.DMA((n, N_DEV - 1)), pltpu.SemaphoreType.DMA((n,))]


HBM_SPEC = pl.BlockSpec(memory_space=pl.ANY)


def _sum_slots(recv_ref, out_ref):
    rows = out_ref.shape[0]
    chunk = min(rows, 128)

    def add(i, carry):
        r0 = pl.multiple_of(i * chunk, chunk)
        acc = recv_ref[0, pl.ds(r0, chunk), :].astype(F32)
        for dev in range(1, N_DEV):
            acc = acc + recv_ref[dev, pl.ds(r0, chunk), :].astype(F32)
        out_ref[pl.ds(r0, chunk), :] = acc
        return carry

    lax.fori_loop(0, rows // chunk, add, 0)


N_CHIPS = N_DEV // 2


def _rows_loop(rows, fn):
    chunk = min(rows, 128)

    def step(i, carry):
        fn(pl.ds(pl.multiple_of(i * chunk, chunk), chunk))
        return carry

    lax.fori_loop(0, rows // chunk, step, 0)


def _chip_reduce(g_ref, out_ref, sib_ref, chip_ref, send_ref, sems, between):
    sib_send, sib_recv, chip_send, chip_recv = sems
    x, y, c = lax.axis_index("x"), lax.axis_index("y"), lax.axis_index("c")
    my_chip = 2 * x + y
    rows = out_ref.shape[0]

    def chip_of(k):
        cx = 1 - x if k & 2 else x
        cy = 1 - y if k & 1 else y
        return (cx, cy), 2 * cx + cy

    def to_sibling(t):
        return pltpu.make_async_remote_copy(
            src_ref=g_ref.at[2 * t + 1 - c], dst_ref=sib_ref.at[t], send_sem=sib_send.at[t], recv_sem=sib_recv.at[t],
            device_id=(x, y, 1 - c), device_id_type=MESH)

    def to_chip(k):
        (cx, cy), t = chip_of(k)
        return t, pltpu.make_async_remote_copy(
            src_ref=send_ref.at[k - 1], dst_ref=chip_ref.at[my_chip], send_sem=chip_send.at[k - 1],
            recv_sem=chip_recv.at[k - 1], device_id=(cx, cy, c), device_id_type=MESH)

    def from_chip(k):
        _, t = chip_of(k)
        return pltpu.make_async_remote_copy(
            src_ref=send_ref.at[k - 1], dst_ref=chip_ref.at[t], send_sem=chip_send.at[k - 1],
            recv_sem=chip_recv.at[k - 1], device_id=(x, y, c), device_id_type=MESH)

    for t in range(N_CHIPS):
        to_sibling(t).start()
    between()
    for t in range(N_CHIPS):
        to_sibling(t).wait_recv()

    def pair_sum(t, r):
        return g_ref[2 * t + c, r, :].astype(F32) + sib_ref[t, r, :].astype(F32)

    for k in range(1, N_CHIPS):
        t, cp = to_chip(k)

        def fill(r, t=t, k=k):
            send_ref[k - 1, r, :] = pair_sum(t, r).astype(BF16)

        _rows_loop(rows, fill)
        cp.start()

    def own(r):
        chip_ref[my_chip, r, :] = pair_sum(my_chip, r).astype(BF16)

    _rows_loop(rows, own)
    for k in range(1, N_CHIPS):
        from_chip(k).wait_recv()

    def total(r):
        acc = chip_ref[0, r, :].astype(F32)
        for t in range(1, N_CHIPS):
            acc = acc + chip_ref[t, r, :].astype(F32)
        out_ref[r, :] = acc

    _rows_loop(rows, total)
    for t in range(N_CHIPS):
        to_sibling(t).wait_send()
    for k in range(1, N_CHIPS):
        to_chip(k)[1].wait_send()


def _reduce_exchange(part, landed, smalls):
    nl, ng = len(landed), len(smalls)
    n_out = 1 + nl + ng

    def body(*refs):
        p_in, l_in, s_in = refs[0], refs[1:1 + nl], refs[1 + nl:n_out]
        p_out, l_out, s_out = refs[n_out], refs[n_out + 1:n_out + 1 + nl], refs[n_out + 1 + nl:2 * n_out]
        scratch = refs[2 * n_out:]
        s_recv, (sib_ref, chip_ref, send_ref), sems = scratch[:ng], scratch[ng:ng + 3], scratch[ng + 3:]

        def between():
            _exchange_start(s_in, s_recv, *sems[4:], False)
            for t in range(nl):
                _sum_slots(l_in[t], l_out[t])

        _chip_reduce(p_in, p_out, sib_ref, chip_ref, send_ref, sems[:4], between)
        _exchange_wait(s_in, s_recv, *sems[4:], False)
        for t in range(ng):
            acc = s_recv[t][0]
            for dev in range(1, N_DEV):
                acc = acc + s_recv[t][dev]
            s_out[t][...] = acc

    vmem = pl.BlockSpec(memory_space=pltpu.VMEM)
    slot = part.shape[1:]
    outs = pl.pallas_call(
        body,
        name="reduce_grads",
        out_shape=[jax.ShapeDtypeStruct(p.shape[1:], F32) for p in [part] + landed]
        + [jax.ShapeDtypeStruct(s.shape, F32) for s in smalls],
        in_specs=[vmem] * n_out,
        out_specs=[vmem] * n_out,
        scratch_shapes=[pltpu.VMEM((N_DEV,) + s.shape, F32) for s in smalls]
        + [pltpu.VMEM((N_CHIPS,) + slot, BF16), pltpu.VMEM((N_CHIPS,) + slot, BF16), pltpu.VMEM((N_CHIPS - 1,) + slot, BF16)]
        + [pltpu.SemaphoreType.DMA((N_CHIPS,)), pltpu.SemaphoreType.DMA((N_CHIPS,)),
           pltpu.SemaphoreType.DMA((N_CHIPS - 1,)), pltpu.SemaphoreType.DMA((N_CHIPS - 1,))]
        + _exchange_sems(ng),
        compiler_params=_params(vmem_mib=56),
    )(part, *landed, *smalls)
    return outs[0], outs[1:1 + nl], outs[1 + nl:]


def _layer_a_fwd(x2, sm, win_g, wout, later, ts):
    seq, d = x2.shape
    width = wout.shape[0]
    half = win_g.shape[2]
    n_half = width // half
    nl = len(later)
    nt = seq // ts

    def body(x_ref, sm_ref, win_ref, wout_ref, *refs):
        shard_refs, refs = refs[:nl], refs[nl:]
        h1_ref, n1_ref, proj_ref, conv_ref, y_ref, ya_ref = refs[:6]
        gathered_refs, (vprev_ref, *sems) = refs[6:6 + nl], refs[6 + nl:]

        @pl.when(pl.program_id(0) == 0)
        def _():
            vprev_ref[...] = jnp.zeros_like(vprev_ref)
            _exchange_start(shard_refs, gathered_refs, *sems, False)

        @pl.when(pl.program_id(0) == nt - 1)
        def _():
            _exchange_wait(shard_refs, gathered_refs, *sems, False)

        xf = x_ref[...]
        xn, _ = _rms(xf)
        n1 = (xn * sm_ref[0:1, :]).astype(BF16)
        n1_ref[...] = n1
        row = lax.broadcasted_iota(jnp.int32, (ts, half), 0)
        ya = jnp.zeros((ts, d), F32)
        for hh in range(n_half):
            cols = slice(hh * half, (hh + 1) * half)
            parts = []
            for part in range(4):
                j = part * n_half + hh
                pj = _dot(n1, win_ref[j])
                proj_ref[:, j * half:(j + 1) * half] = pj.astype(BF16)
                parts.append(pj)
            b, c, u, z = parts
            v = c * u
            last1, last2 = vprev_ref[7:8, cols], vprev_ref[6:7, cols]
            v1 = jnp.where(row == 0, last1, pltpu.roll(v, 1, 0))
            v2 = jnp.where(row == 0, last2, jnp.where(row == 1, last1, pltpu.roll(v, 2, 0)))
            vprev_ref[:, cols] = v[ts - 8:ts, :]
            conv = sm_ref[1:2, cols] * v2 + sm_ref[2:3, cols] * v1 + sm_ref[3:4, cols] * v
            conv_ref[:, cols] = conv.astype(BF16)
            yh = (b * conv * _silu(z)[0]).astype(BF16)
            y_ref[:, cols] = yh
            ya = ya + _dot(yh, wout_ref[cols, :])
        ya_ref[...] = ya
        h1_ref[...] = xf + _rms(ya)[0] * sm_ref[4:5, :]

    outs = pl.pallas_call(
        body,
        name="layer_a_fwd",
        grid=(nt,),
        in_specs=[_rows(ts, d), _full(sm.shape), _full(win_g.shape), _full(wout.shape)] + [HBM_SPEC] * nl,
        out_specs=[_rows(ts, d), _rows(ts, d), _rows(ts, 4 * width), _rows(ts, width), _rows(ts, width), _rows(ts, d)]
        + [HBM_SPEC] * nl,
        out_shape=[
            jax.ShapeDtypeStruct((seq, d), F32),
            jax.ShapeDtypeStruct((seq, d), BF16),
            jax.ShapeDtypeStruct((seq, 4 * width), BF16),
            jax.ShapeDtypeStruct((seq, width), BF16),
            jax.ShapeDtypeStruct((seq, width), BF16),
            jax.ShapeDtypeStruct((seq, d), F32),
        ] + [jax.ShapeDtypeStruct((N_DEV,) + s.shape, s.dtype) for s in later],
        scratch_shapes=[pltpu.VMEM((8, width), F32)] + _exchange_sems(nl),
        compiler_params=_params(("arbitrary",), 56),
    )(x2, sm, win_g, wout, *later)
    return outs[:6], outs[6:]


def _layer_b_in(h1, kvn, bpre, wkv, wbin_g, ts):
    seq, d = h1.shape
    kvw = wkv.shape[1]
    cw = wbin_g.shape[2]
    aw = N_Q_HEADS * HEAD_DIM
    per = aw // cw

    def body(h1_ref, kvn_ref, bpre_ref, wkv_ref, wbin_ref, n3_ref, n4_ref, kv_ref, q_ref, z2_ref):
        hn, _ = _rms(h1_ref[...])
        n3 = (hn * kvn_ref[...]).astype(BF16)
        n4 = (hn * bpre_ref[...]).astype(BF16)
        n3_ref[...] = n3
        n4_ref[...] = n4
        kv_ref[...] = _dot(n3, wkv_ref[...]).astype(BF16)
        for j in range(N_DEV):
            pj = _dot(n4, wbin_ref[j])
            if j < per:
                q_ref[:, j * cw:(j + 1) * cw] = pj.astype(BF16)
            else:
                z2_ref[:, (j - per) * cw:(j - per + 1) * cw] = pj

    return pl.pallas_call(
        body,
        name="layer_b_in",
        grid=(seq // ts,),
        in_specs=[_rows(ts, d), _full(kvn.shape), _full(bpre.shape), _full(wkv.shape), _full(wbin_g.shape)],
        out_specs=[_rows(ts, d), _rows(ts, d), _rows(ts, kvw), _rows(ts, aw), _rows(ts, aw)],
        out_shape=[
            jax.ShapeDtypeStruct((seq, d), BF16),
            jax.ShapeDtypeStruct((seq, d), BF16),
            jax.ShapeDtypeStruct((seq, kvw), BF16),
            jax.ShapeDtypeStruct((seq, aw), BF16),
            jax.ShapeDtypeStruct((seq, aw), F32),
        ],
        compiler_params=_params(("parallel",), 48),
    )(h1, kvn, bpre, wkv, wbin_g)


N_PAIRS = N_Q_HEADS // 2
BAND = 2 * BLOCK


def _bias_table(rel_bias, bucket_t, in_window_t):
    def body(rb_ref, bucket_ref, win_ref, out_ref):
        bk = bucket_ref[...]
        inside = win_ref[...] != 0
        has_prev = lax.broadcasted_iota(jnp.int32, bk.shape, 0) >= BLOCK
        for h in range(N_Q_HEADS):
            acc = jnp.full(bk.shape, NEG_INF, F32)
            for b in range(N_BUCKETS):
                acc = jnp.where(jnp.logical_and(bk == b, inside), rb_ref[b, h], acc)
            cols = slice((h % 2) * BLOCK, (h % 2 + 1) * BLOCK)
            out_ref[1, h // 2, :, cols] = acc
            out_ref[0, h // 2, :, cols] = jnp.where(has_prev, acc, NEG_INF)

    vmem = pl.BlockSpec(memory_space=pltpu.VMEM)
    return pl.pallas_call(
        body,
        name="bias_table",
        in_specs=[pl.BlockSpec(memory_space=pltpu.SMEM), vmem, vmem],
        out_specs=vmem,
        out_shape=jax.ShapeDtypeStruct((2, N_PAIRS, BAND, 2 * BLOCK), F32),
    )(rel_bias, bucket_t, in_window_t)


Q_BLOCKS = 2


def _banded_tiles(kvp_ref, kvc_ref):
    tile = kvc_ref[...].astype(F32)
    blocks = [kvp_ref[...].astype(F32)] + [tile[u * BLOCK:(u + 1) * BLOCK] for u in range(Q_BLOCKS)]
    return [_banded_kv(blocks[u], blocks[u + 1]) for u in range(Q_BLOCKS)]


def _bias_of(bias_ref, i, u, m):
    return bias_ref[jnp.minimum(i, 1) if u == 0 else 1, m]


def _banded_kv(kvp, kvc):
    kw = N_KV_HEADS * HEAD_DIM
    out = []
    for full in (jnp.concatenate([kvp[:, :kw], kvc[:, :kw]], axis=0), jnp.concatenate([kvp[:, kw:], kvc[:, kw:]], axis=0)):
        lo = lax.broadcasted_iota(jnp.int32, full.shape, 1) < HEAD_DIM
        rolled = pltpu.roll(full, HEAD_DIM, 1)
        x2 = [jnp.where(lo, full, rolled).astype(BF16), jnp.where(lo, rolled, full).astype(BF16)]
        ft = full.T
        x2t = [jnp.concatenate([ft[kh * HEAD_DIM:(kh + 1) * HEAD_DIM]] * 2, axis=0).astype(BF16) for kh in range(N_KV_HEADS)]
        out += [x2, x2t]
    return out


def _pair_rows(ref, rows, m, scale=None):
    both = ref[rows, m * LANES:(m + 1) * LANES].astype(F32)
    if scale is not None:
        both = both * scale
    lo = lax.broadcasted_iota(jnp.int32, both.shape, 1) < HEAD_DIM
    zero = jnp.zeros_like(both)
    return jnp.concatenate([jnp.where(lo, both, zero), jnp.where(lo, zero, both)], axis=0).astype(BF16)


def _pair_cols(res_t):
    top = lax.broadcasted_iota(jnp.int32, (LANES, BLOCK), 0) < HEAD_DIM
    return jnp.where(top, res_t[:, :BLOCK], res_t[:, BLOCK:]).T


def _sink_row(sink_ref, m):
    first = lax.broadcasted_iota(jnp.int32, (1, 2 * BLOCK), 1) < BLOCK
    return jnp.where(first, sink_ref[0, 2 * m], sink_ref[0, 2 * m + 1])


def _probs_t(k2, qpair, bias, sink):
    return _softmax_t(_dot_nt(k2, qpair) + bias, sink)


def _softmax_t(logits, sink):
    mx = jnp.maximum(jnp.max(logits, axis=0, keepdims=True), sink)
    p = jnp.exp(logits - mx)
    sink_p = jnp.exp(sink - mx)
    inv = 1.0 / (jnp.sum(p, axis=0, keepdims=True) + sink_p)
    return p * inv, sink_p * inv


def _attn_fwd(q, kv, z2, biasm, sinks):
    seq, aw = q.shape
    kvw = kv.shape[1]
    nb = seq // BLOCK

    tile = Q_BLOCKS * BLOCK

    def body(sink_ref, q_ref, kvc_ref, kvp_ref, z2_ref, bias_ref, attn_ref, o_ref, acc_ref):
        i = pl.program_id(0)
        banded = _banded_tiles(kvp_ref, kvc_ref)
        units = [(u, m) for u in range(Q_BLOCKS) for m in range(N_PAIRS)]
        kv_of = lambda m: (2 * m) // GROUP
        logits, probs = {}, {}
        for step in range(len(units) + 2):
            if step < len(units):
                u, m = units[step]
                qpair = _pair_rows(q_ref, slice(u * BLOCK, (u + 1) * BLOCK), m, SCALE)
                logits[step] = _dot_nt(banded[u][0][kv_of(m)], qpair) + _bias_of(bias_ref, i, u, m)
            if 0 <= step - 1 < len(units):
                u, m = units[step - 1]
                probs[step - 1] = _softmax_t(logits.pop(step - 1), _sink_row(sink_ref, m))[0].astype(BF16)
            if 0 <= step - 2 < len(units):
                u, m = units[step - 2]
                out_t = _dot(banded[u][3][kv_of(m)], probs.pop(step - 2))
                acc_ref[u * BLOCK:(u + 1) * BLOCK, m * LANES:(m + 1) * LANES] = _pair_cols(out_t)
        attn = acc_ref[...]
        attn_ref[...] = attn.astype(BF16)
        o_ref[...] = (attn * _silu(z2_ref[...])[0]).astype(BF16)

    blk = lambda w: pl.BlockSpec((tile, w), lambda i: (i, 0))
    return pl.pallas_call(
        body,
        name="attn_fwd",
        grid=(seq // tile,),
        in_specs=[
            pl.BlockSpec(memory_space=pltpu.SMEM),
            blk(aw),
            blk(kvw),
            pl.BlockSpec((BLOCK, kvw), lambda i: (jnp.maximum(Q_BLOCKS * i - 1, 0), 0)),
            blk(aw),
            _full(biasm.shape),
        ],
        out_specs=[blk(aw), blk(aw)],
        out_shape=[jax.ShapeDtypeStruct((seq, aw), BF16), jax.ShapeDtypeStruct((seq, aw), BF16)],
        scratch_shapes=[pltpu.VMEM((tile, aw), F32)],
        compiler_params=_params(("arbitrary",), 40),
    )(sinks, q, kv, kv, z2, biasm)


def _layer_b_out(o, attn, z2, h1, target, wbout, bpost, ts):
    seq, d = h1.shape
    aw = o.shape[1]

    def body(o_ref, attn_ref, z2_ref, h1_ref, tgt_ref, w_ref, g_ref, dh2_ref, dyb_ref, dattn_ref, dz2_ref, acc_ref):
        @pl.when(pl.program_id(0) == 0)
        def _():
            acc_ref[...] = jnp.zeros_like(acc_ref)

        g = g_ref[...]

        def project(rows):
            return _dot(o_ref[rows, :], w_ref[...])

        def loss_and_norm_bwd(rows, yb):
            ybn, r = _rms(yb)
            diff = h1_ref[rows, :] + ybn * g - tgt_ref[rows, :]
            dh2 = diff * (1.0 / d)
            dh2_ref[rows, :] = dh2
            acc_ref[0:1, :] += jnp.sum(dh2 * ybn, axis=0, keepdims=True)
            tok = jnp.mean(diff * diff, axis=-1, keepdims=True)
            acc_ref[1:2, :] += 0.5 * jnp.sum(tok, axis=0, keepdims=True)
            dyb = _rms_bwd(dh2 * g, ybn, r).astype(BF16)
            dyb_ref[rows, :] = dyb
            return _dot_nt(dyb, w_ref[...])

        def gate_bwd(rows, do):
            sz, dsz = _silu(z2_ref[rows, :])
            dattn_ref[rows, :] = (do * sz).astype(BF16)
            dz2_ref[rows, :] = (do * attn_ref[rows, :].astype(F32) * dsz).astype(BF16)

        _skewed(ts, [project, loss_and_norm_bwd, gate_bwd])

    return pl.pallas_call(
        body,
        name="layer_b_out",
        grid=(seq // ts,),
        in_specs=[_rows(ts, aw), _rows(ts, aw), _rows(ts, aw), _rows(ts, d), _rows(ts, d), _full(wbout.shape), _full(bpost.shape)],
        out_specs=[_rows(ts, d), _rows(ts, d), _rows(ts, aw), _rows(ts, aw), _resident((8, d))],
        out_shape=[
            jax.ShapeDtypeStruct((seq, d), F32),
            jax.ShapeDtypeStruct((seq, d), BF16),
            jax.ShapeDtypeStruct((seq, aw), BF16),
            jax.ShapeDtypeStruct((seq, aw), BF16),
            jax.ShapeDtypeStruct((8, d), F32),
        ],
        compiler_params=_params(("arbitrary",), 48),
    )(o, attn, z2, h1, target, wbout, bpost)


def _attn_bwd(q, kv, dattn, biasm, sinks, ready):
    seq, aw = q.shape
    kvw = kv.shape[1]
    kw = N_KV_HEADS * HEAD_DIM
    nb = seq // BLOCK
    pairs_per_kv = N_PAIRS // N_KV_HEADS
    nr = len(ready)

    tile = Q_BLOCKS * BLOCK
    nsteps = seq // tile
    held = (Q_BLOCKS - 1) * BLOCK

    def body(sink_ref, q_ref, kvc_ref, kvp_ref, da_ref, bias_ref, *refs):
        ready_refs, (dq_ref, dkv_ref, dssum_ref, dsink_ref) = refs[:nr], refs[nr:nr + 4]
        landed_refs, scratch = refs[nr + 4:2 * nr + 4], refs[2 * nr + 4:]
        carry_ref, done_ref, qs_ref, dos_ref, dst_ref, pt_ref, *sems = scratch
        i = pl.program_id(0)

        @pl.when(i == 0)
        def _():
            dssum_ref[...] = jnp.zeros_like(dssum_ref)
            dsink_ref[...] = jnp.zeros_like(dsink_ref)
            carry_ref[...] = jnp.zeros_like(carry_ref)
            done_ref[...] = jnp.zeros_like(done_ref)
            _exchange_start(ready_refs, landed_refs, *sems, True)

        @pl.when(i == nsteps)
        def _():
            _exchange_wait(ready_refs, landed_refs, *sems, True)

        @pl.when(i < nsteps)
        def _():
            lo = lax.broadcasted_iota(jnp.int32, (BAND, LANES), 1) < HEAD_DIM
            head_lane = lax.broadcasted_iota(jnp.int32, (1, LANES), 1)
            banded = _banded_tiles(kvp_ref, kvc_ref)
            units = [(u, m) for u in range(Q_BLOCKS) for m in range(N_PAIRS)]
            dsink = jnp.zeros((1, LANES), F32)
            folded = {}
            logits, dps, dsbs = {}, {}, {}
            for step in range(len(units) + 2):
                if step < len(units):
                    u, m = units[step]
                    kh, rows = m // pairs_per_kv, slice((m % pairs_per_kv) * BAND, (m % pairs_per_kv + 1) * BAND)
                    qrows = slice(u * BLOCK, (u + 1) * BLOCK)
                    qpair = _pair_rows(q_ref, qrows, m, SCALE)
                    dopair = _pair_rows(da_ref, qrows, m)
                    qs_ref[u, kh, rows, :] = qpair
                    dos_ref[u, kh, rows, :] = dopair
                    logits[step] = _dot_nt(banded[u][0][kh], qpair) + _bias_of(bias_ref, i, u, m)
                    dps[step] = _dot_nt(banded[u][2][kh], dopair)
                if 0 <= step - 1 < len(units):
                    u, m = units[step - 1]
                    kh, rows = m // pairs_per_kv, slice((m % pairs_per_kv) * BAND, (m % pairs_per_kv + 1) * BAND)
                    pn, sink_p = _softmax_t(logits.pop(step - 1), _sink_row(sink_ref, m))
                    dp = dps.pop(step - 1)
                    delta = jnp.sum(pn * dp, axis=0, keepdims=True)
                    ds = pn * (dp - delta)
                    dssum_ref[m] += ds
                    sink_term = sink_p * delta
                    for e in range(2):
                        total = jnp.sum(sink_term[:, e * BLOCK:(e + 1) * BLOCK], axis=1, keepdims=True)
                        dsink = dsink - jnp.where(head_lane == 2 * m + e, total, 0.0)
                    dsbs[step - 1] = ds.astype(BF16)
                    dst_ref[u, kh, :, rows] = dsbs[step - 1]
                    pt_ref[u, kh, :, rows] = pn.astype(BF16)
                if 0 <= step - 2 < len(units):
                    u, m = units[step - 2]
                    kh = m // pairs_per_kv
                    dq_t = _dot(banded[u][1][kh], dsbs.pop(step - 2))
                    dq_ref[u * BLOCK:(u + 1) * BLOCK, m * LANES:(m + 1) * LANES] = (_pair_cols(dq_t) * SCALE).astype(BF16)
                    if m % pairs_per_kv == pairs_per_kv - 1:
                        for name, lhs_ref, rhs_ref in (("k", dst_ref, qs_ref), ("v", pt_ref, dos_ref)):
                            acc = _dot(lhs_ref[u, kh], rhs_ref[u, kh])
                            folded[u, kh, name] = acc + pltpu.roll(acc, HEAD_DIM, 1)
            dsink_ref[0:1, :] += dsink
            dkv = [jnp.concatenate([jnp.where(lo, folded[u, 0, n], folded[u, 1, n]) for n in ("k", "v")], axis=1)
                   for u in range(Q_BLOCKS)]

            @pl.when(i > 0)
            def _():
                if held:
                    dkv_ref[:held, :] = done_ref[...].astype(BF16)
                dkv_ref[held:, :] = (carry_ref[...] + dkv[0][:BLOCK]).astype(BF16)

            for u in range(Q_BLOCKS - 1):
                done_ref[u * BLOCK:(u + 1) * BLOCK, :] = dkv[u][BLOCK:] + dkv[u + 1][:BLOCK]
            carry_ref[...] = dkv[Q_BLOCKS - 1][BLOCK:]

        @pl.when(i == nsteps)
        def _():
            if held:
                dkv_ref[:held, :] = done_ref[...].astype(BF16)
            dkv_ref[held:, :] = carry_ref[...].astype(BF16)

    last = nsteps - 1
    blk = lambda w: pl.BlockSpec((tile, w), lambda i: (jnp.minimum(i, last), 0))
    outs = pl.pallas_call(
        body,
        name="attn_bwd",
        grid=(nsteps + 1,),
        in_specs=[
            pl.BlockSpec(memory_space=pltpu.SMEM),
            blk(aw),
            blk(kvw),
            pl.BlockSpec((BLOCK, kvw), lambda i: (jnp.clip(Q_BLOCKS * i - 1, 0, nb - 1), 0)),
            blk(aw),
            _full(biasm.shape),
        ] + [HBM_SPEC] * nr,
        out_specs=[
            blk(aw),
            pl.BlockSpec((tile, kvw), lambda i: (jnp.maximum(i - 1, 0), 0)),
            _resident(biasm.shape[1:]),
            _resident((8, LANES)),
        ] + [HBM_SPEC] * nr,
        out_shape=[
            jax.ShapeDtypeStruct((seq, aw), BF16),
            jax.ShapeDtypeStruct((seq, kvw), BF16),
            jax.ShapeDtypeStruct(biasm.shape[1:], F32),
            jax.ShapeDtypeStruct((8, LANES), F32),
        ] + [jax.ShapeDtypeStruct(g.shape, g.dtype) for g in ready],
        scratch_shapes=[
            pltpu.VMEM((BLOCK, kvw), F32),
            pltpu.VMEM((max(held, 8), kvw), F32),
            pltpu.VMEM((Q_BLOCKS, N_KV_HEADS, pairs_per_kv * BAND, LANES), BF16),
            pltpu.VMEM((Q_BLOCKS, N_KV_HEADS, pairs_per_kv * BAND, LANES), BF16),
            pltpu.VMEM((Q_BLOCKS, N_KV_HEADS, BAND, pairs_per_kv * BAND), BF16),
            pltpu.VMEM((Q_BLOCKS, N_KV_HEADS, BAND, pairs_per_kv * BAND), BF16),
        ] + _exchange_sems(nr),
        compiler_params=_params(("arbitrary",), 48),
    )(sinks, q, kv, kv, dattn, biasm, *ready)
    return outs[:4], outs[4:]


def _relbias_grad(dssum2, onehot, chunk):
    heads, n = dssum2.shape

    def body(a_ref, oh_ref, out_ref):
        @pl.when(pl.program_id(0) == 0)
        def _():
            out_ref[...] = jnp.zeros_like(out_ref)

        a = a_ref[...]
        hi = a.astype(BF16)
        lo = (a - hi.astype(F32)).astype(BF16)
        out_ref[...] += _dot(hi, oh_ref[...]) + _dot(lo, oh_ref[...])

    return pl.pallas_call(
        body,
        name="relbias_grad",
        grid=(n // chunk,),
        in_specs=[pl.BlockSpec((heads, chunk), lambda i: (0, i)), pl.BlockSpec((chunk, LANES), lambda i: (i, 0))],
        out_specs=_resident((heads, LANES)),
        out_shape=jax.ShapeDtypeStruct((heads, LANES), F32),
        compiler_params=_params(("arbitrary",), 32),
    )(dssum2, onehot)


def _layer_b_in_bwd(dh2, dq, dz2, dkv, h1, ya, wbin_g, wkv, kvn, bpre, sm, ready, ts):
    seq, d = h1.shape
    aw = dq.shape[1]
    kvw = dkv.shape[1]
    cw = wbin_g.shape[2]
    per = aw // cw

    nr = len(ready)
    nt = seq // ts

    def body(dh2_ref, dq_ref, dz2_ref, dkv_ref, h1_ref, ya_ref, wbin_ref, wkv_ref, kvn_ref, bpre_ref, sm_ref, *refs):
        ready_refs, (dh1_ref, dya_ref, acc_ref) = refs[:nr], refs[nr:nr + 3]
        landed_refs, sems = refs[nr + 3:2 * nr + 3], refs[2 * nr + 3:]

        @pl.when(pl.program_id(0) == 0)
        def _():
            acc_ref[...] = jnp.zeros_like(acc_ref)
            _exchange_start(ready_refs, landed_refs, *sems, True)

        @pl.when(pl.program_id(0) == nt - 1)
        def _():
            _exchange_wait(ready_refs, landed_refs, *sems, True)

        dn4 = jnp.zeros((ts, d), F32)
        for j in range(N_DEV):
            src = dq_ref if j < per else dz2_ref
            jj = j % per
            dn4 = dn4 + _dot_nt(src[:, jj * cw:(jj + 1) * cw], wbin_ref[j])
        dn3 = _dot_nt(dkv_ref[...], wkv_ref[...])
        hn, r = _rms(h1_ref[...])
        acc_ref[0:1, :] += jnp.sum(dn4 * hn, axis=0, keepdims=True)
        acc_ref[1:2, :] += jnp.sum(dn3 * hn, axis=0, keepdims=True)
        dh1 = dh2_ref[...] + _rms_bwd(dn4 * bpre_ref[...] + dn3 * kvn_ref[...], hn, r)
        dh1_ref[...] = dh1
        yan, r2 = _rms(ya_ref[...])
        acc_ref[2:3, :] += jnp.sum(dh1 * yan, axis=0, keepdims=True)
        dya_ref[...] = _rms_bwd(dh1 * sm_ref[4:5, :], yan, r2).astype(BF16)

    outs = pl.pallas_call(
        body,
        name="layer_b_in_bwd",
        grid=(nt,),
        in_specs=[_rows(ts, d), _rows(ts, aw), _rows(ts, aw), _rows(ts, kvw), _rows(ts, d), _rows(ts, d),
                  _full(wbin_g.shape), _full(wkv.shape), _full(kvn.shape), _full(bpre.shape), _full(sm.shape)]
        + [HBM_SPEC] * nr,
        out_specs=[_rows(ts, d), _rows(ts, d), _resident((8, d))] + [HBM_SPEC] * nr,
        out_shape=[jax.ShapeDtypeStruct((seq, d), F32), jax.ShapeDtypeStruct((seq, d), BF16),
                   jax.ShapeDtypeStruct((8, d), F32)] + [jax.ShapeDtypeStruct(g.shape, g.dtype) for g in ready],
        scratch_shapes=_exchange_sems(nr),
        compiler_params=_params(("arbitrary",), 48),
    )(dh2, dq, dz2, dkv, h1, ya, wbin_g, wkv, kvn, bpre, sm, *ready)
    return outs[:3], outs[3:]


def _layer_a_bwd(dya, proj, conv, dh1, x2, wout, win_g, sm, ts):
    seq, d = x2.shape
    width = wout.shape[0]
    half = win_g.shape[2]
    n_half = width // half
    nt = seq // ts

    def body(dya_ref, proj_ref, conv_ref, dh1_ref, x_ref, wout_ref, win_ref, sm_ref, dproj_ref, gx_ref, acc_ref,
             dnext_ref):
        @pl.when(pl.program_id(0) == 0)
        def _():
            acc_ref[...] = jnp.zeros_like(acc_ref)
            dnext_ref[...] = jnp.zeros_like(dnext_ref)

        dy = _dot_nt(dya_ref[...], wout_ref[...])
        row = lax.broadcasted_iota(jnp.int32, (ts, half), 0)
        dn1 = jnp.zeros((ts, d), F32)
        for hh in range(n_half):
            cols = slice(hh * half, (hh + 1) * half)
            b, c, u, z = [proj_ref[:, (part * n_half + hh) * half:(part * n_half + hh + 1) * half].astype(F32)
                          for part in range(4)]
            cv = conv_ref[:, cols].astype(F32)
            dyh = dy[:, cols]
            sz, dsz = _silu(z)
            dconv = dyh * b * sz
            grads = [dyh * cv * sz, None, None, dyh * b * cv * dsz]
            next0, next1 = dnext_ref[0:1, cols], dnext_ref[1:2, cols]
            dc1 = jnp.where(row == ts - 1, next0, pltpu.roll(dconv, ts - 1, 0))
            dc2 = jnp.where(row == ts - 1, next1, jnp.where(row == ts - 2, next0, pltpu.roll(dconv, ts - 2, 0)))
            dnext_ref[:, cols] = dconv[0:8, :]
            v = c * u
            acc_ref[1:2, cols] += jnp.sum(dc2 * v, axis=0, keepdims=True)
            acc_ref[2:3, cols] += jnp.sum(dc1 * v, axis=0, keepdims=True)
            acc_ref[3:4, cols] += jnp.sum(dconv * v, axis=0, keepdims=True)
            dv = sm_ref[3:4, cols] * dconv + sm_ref[2:3, cols] * dc1 + sm_ref[1:2, cols] * dc2
            grads[1] = dv * u
            grads[2] = dv * c
            for part in range(4):
                j = part * n_half + hh
                gj = grads[part].astype(BF16)
                dproj_ref[:, j * half:(j + 1) * half] = gj
                dn1 = dn1 + _dot_nt(gj, win_ref[j])
        xn, r = _rms(x_ref[...])
        acc_ref[0:1, :] += jnp.sum(dn1 * xn, axis=0, keepdims=True)
        gx_ref[...] = dh1_ref[...] + _rms_bwd(dn1 * sm_ref[0:1, :], xn, r)

    rev = lambda w: pl.BlockSpec((ts, w), lambda i: (nt - 1 - i, 0))
    return pl.pallas_call(
        body,
        name="layer_a_bwd",
        grid=(nt,),
        in_specs=[rev(d), rev(4 * width), rev(width), rev(d), rev(d), _full(wout.shape), _full(win_g.shape), _full(sm.shape)],
        out_specs=[rev(4 * width), rev(d), _resident((8, d))],
        out_shape=[jax.ShapeDtypeStruct((seq, 4 * width), BF16), jax.ShapeDtypeStruct((seq, d), F32),
                   jax.ShapeDtypeStruct((8, d), F32)],
        scratch_shapes=[pltpu.VMEM((8, width), F32)],
        compiler_params=_params(("arbitrary",), 56),
    )(dya, proj, conv, dh1, x2, wout, win_g, sm)


def _wgrad(a, bs, n_slots, ts, name, ready=()):
    nr = len(ready)
    seq, k = a.shape
    nb_in = len(bs)
    n_each = bs[0].shape[1]
    n = nb_in * n_each
    bn = min(n_each, 1024)
    per_in = n_each // bn
    n_blocks = nb_in * per_in
    ns = seq // ts

    def b_spec(idx):
        def index(j, s):
            mine = j // per_in == idx
            row = jnp.where(mine, s, jnp.where(j // per_in > idx, ns - 1, 0))
            return (row, jnp.where(mine, j % per_in, jnp.where(j // per_in > idx, per_in - 1, 0)))
        return pl.BlockSpec((ts, bn), index)

    if n_slots:
        sw = n // n_slots
        spb = bn // sw
        out_shape = jax.ShapeDtypeStruct((n_slots, k, sw), BF16)
        out_spec = pl.BlockSpec((spb, k, sw), lambda j, s: (j, 0, 0))
    else:
        out_shape = jax.ShapeDtypeStruct((k, n), BF16)
        out_spec = pl.BlockSpec((k, bn), lambda j, s: (0, j))

    def body(a_ref, *refs):
        b_refs, ready_refs, o_ref = refs[:nb_in], refs[nb_in:nb_in + nr], refs[nb_in + nr]
        landed_refs, (acc_ref, *sems) = refs[nb_in + nr + 1:nb_in + 2 * nr + 1], refs[nb_in + 2 * nr + 1:]
        j, s = pl.program_id(0), pl.program_id(1)

        if nr:
            @pl.when(jnp.logical_and(j == 0, s == 0))
            def _():
                _exchange_start(ready_refs, landed_refs, *sems, True)

            @pl.when(jnp.logical_and(j == n_blocks - 1, s == ns - 1))
            def _():
                _exchange_wait(ready_refs, landed_refs, *sems, True)

        @pl.when(s == 0)
        def _():
            acc_ref[...] = jnp.zeros_like(acc_ref)

        for idx in range(nb_in):
            @pl.when(j // per_in == idx)
            def _(idx=idx):
                acc_ref[...] += _dot_tn(a_ref[...], b_refs[idx][...])

        @pl.when(s == ns - 1)
        def _():
            if n_slots:
                for e in range(spb):
                    o_ref[e] = acc_ref[:, e * sw:(e + 1) * sw].astype(BF16)
            else:
                o_ref[...] = acc_ref[...].astype(BF16)

    outs = pl.pallas_call(
        body,
        name=name,
        grid=(n_blocks, ns),
        in_specs=[pl.BlockSpec((ts, k), lambda j, s: (s, 0))] + [b_spec(idx) for idx in range(nb_in)] + [HBM_SPEC] * nr,
        out_specs=[out_spec] + [HBM_SPEC] * nr,
        out_shape=[out_shape] + [jax.ShapeDtypeStruct(g.shape, g.dtype) for g in ready],
        scratch_shapes=[pltpu.VMEM((k, bn), F32)] + (_exchange_sems(nr) if nr else []),
        compiler_params=_params(("arbitrary", "arbitrary"), 48),
    )(a, *bs, *ready)
    return (outs[0], outs[1:]) if nr else outs[0]


def _adamw(ws, gs, ms, vs):
    n = len(ws)

    def step(w, g, m, v):
        m = ADAM_B1 * m + (1.0 - ADAM_B1) * g
        v = ADAM_B2 * v + (1.0 - ADAM_B2) * jnp.square(g)
        m_hat = m / (1.0 - ADAM_B1 ** ADAM_STEP)
        v_hat = v / (1.0 - ADAM_B2 ** ADAM_STEP)
        return -ADAM_LR * (m_hat / (jnp.sqrt(v_hat) + ADAM_EPS) + ADAM_WD * w), m, v

    def body(*refs):
        w_refs, g_refs, m_refs, v_refs = (refs[k * n:(k + 1) * n] for k in range(4))
        d_refs, nm_refs, nv_refs = (refs[(4 + k) * n:(5 + k) * n] for k in range(3))
        for t in range(n):
            rows = w_refs[t].shape[0]
            if rows <= 128:
                d_refs[t][...], nm_refs[t][...], nv_refs[t][...] = step(
                    w_refs[t][...], g_refs[t][...], m_refs[t][...], v_refs[t][...])
                continue
            chunk = 128

            def one(i, carry, t=t):
                r = pl.ds(pl.multiple_of(i * chunk, chunk), chunk)
                d_refs[t][r, :], nm_refs[t][r, :], nv_refs[t][r, :] = step(
                    w_refs[t][r, :], g_refs[t][r, :], m_refs[t][r, :], v_refs[t][r, :])
                return carry

            lax.fori_loop(0, rows // chunk, one, 0)

    vmem = pl.BlockSpec(memory_space=pltpu.VMEM)
    outs = pl.pallas_call(
        body,
        name="adamw",
        in_specs=[vmem] * (4 * n),
        out_specs=[vmem] * (3 * n),
        out_shape=[jax.ShapeDtypeStruct(w.shape, F32) for w in ws] * 3,
        compiler_params=_params(vmem_mib=56),
    )(*ws, *gs, *ms, *vs)
    return outs[:n], outs[n:2 * n], outs[2 * n:]


def _band_structure():
    q_loc = jnp.arange(BLOCK, dtype=jnp.int32)[:, None]
    s_loc = jnp.arange(2 * BLOCK, dtype=jnp.int32)[None, :]
    dist = q_loc + BLOCK - s_loc
    in_window = (dist >= 0) & (dist < BLOCK)
    dd = jnp.maximum(dist, 0)
    max_exact = N_BUCKETS // 2
    large = max_exact + (jnp.log(jnp.maximum(dd, 1).astype(F32) / max_exact) / math.log(MAX_DISTANCE / max_exact)
                         * (N_BUCKETS - max_exact)).astype(jnp.int32)
    bucket = jnp.where(dd < max_exact, dd, jnp.minimum(large, N_BUCKETS - 1))
    onehot = (bucket.reshape(-1, 1) == jnp.arange(LANES, dtype=jnp.int32)[None, :]).astype(BF16)
    return bucket, in_window.astype(jnp.int32), onehot


def _place_rows(a, row, rows=8):
    return jnp.pad(a, ((row, rows - row - a.shape[0]), (0, 0)))


def kernel(x, a_pre_norm, a_w_in, a_conv_w, a_w_out, a_post_norm, kv_norm, w_kv, rel_bias, b_pre_norm, b_w_in, b_sinks, b_w_out, b_post_norm, loss_target, m_a_pre_norm, m_a_w_in, m_a_conv_w, m_a_w_out, m_a_post_norm, m_kv_norm, m_w_kv, m_rel_bias, m_b_pre_norm, m_b_w_in, m_b_sinks, m_b_w_out, m_b_post_norm, v_a_pre_norm, v_a_w_in, v_a_conv_w, v_a_w_out, v_a_post_norm, v_kv_norm, v_w_kv, v_rel_bias, v_b_pre_norm, v_b_w_in, v_b_sinks, v_b_w_out, v_b_post_norm):
    seq, d = x.shape[1], x.shape[2]
    x2 = x.reshape(seq, d)
    target = loss_target.reshape(seq, d)
    shard = a_pre_norm.shape[1]
    me = _my_index()
    ts_a = min(seq, 512)
    ts = min(seq, 512)
    ts_w = min(seq, 2048)

    small = _place_rows(a_pre_norm, 0) + _place_rows(a_conv_w[0], 1) + _place_rows(a_post_norm, 4)
    win_g, wout_g, small_g = _all_gather([a_w_in[0], a_w_out[0], small], [BF16, BF16, F32])
    wout = wout_g.reshape(-1, wout_g.shape[2])
    sm = small_g.transpose(1, 0, 2).reshape(8, N_DEV * shard)
    kvn = kv_norm.reshape(1, d)

    (h1, n1, proj, conv, y, ya), (wkv_g, wbin_g, wbout_g) = _layer_a_fwd(
        x2, sm, win_g, wout, [w_kv.astype(BF16), b_w_in[0].astype(BF16), b_w_out[0].astype(BF16)], ts_a)
    wkv = wkv_g.reshape(-1, wkv_g.shape[2])
    wbout = wbout_g.reshape(-1, wbout_g.shape[2])
    n3, n4, kv, q, z2 = _layer_b_in(h1, kvn, b_pre_norm, wkv, wbin_g, ts)
    bucket, in_window, onehot = _band_structure()
    biasm = _bias_table(rel_bias, bucket.T, in_window.T)
    attn, o = _attn_fwd(q, kv, z2, biasm, b_sinks)
    dh2, dyb, dattn, dz2, acc_c = _layer_b_out(o, attn, z2, h1, target, wbout, b_post_norm, ts)

    g_wbout = _wgrad(o, [dyb], 0, ts_w, "wgrad_b_out").reshape(wbout_g.shape)
    (dq, dkv, dssum, dsink), (l_wbout,) = _attn_bwd(q, kv, dattn, biasm, b_sinks, [g_wbout])
    by_head = dssum.reshape(N_PAIRS, BAND, 2, BLOCK).transpose(0, 2, 3, 1)
    relb = _relbias_grad(by_head.reshape(N_Q_HEADS, -1), onehot, 4096)
    g_wkv = _wgrad(n3, [dkv], 0, ts_w, "wgrad_kv").reshape(wkv_g.shape)
    g_wbin = _wgrad(n4, [dq, dz2], N_DEV, ts_w, "wgrad_b_in")
    (dh1, dya, acc_b), (l_wkv, l_wbin) = _layer_b_in_bwd(
        dh2, dq, dz2, dkv, h1, ya, wbin_g, wkv, kvn, b_pre_norm, sm, [g_wkv, g_wbin], ts)
    dproj, gx, acc_a = _layer_a_bwd(dya, proj, conv, dh1, x2, wout, win_g, sm, ts_a)
    g_wout = _wgrad(y, [dya], 0, ts_w, "wgrad_a_out").reshape(wout_g.shape)
    g_win, (l_wout,) = _wgrad(n1, [dproj], N_DEV, ts_w, "wgrad_a_in", [g_wout])

    r_win, (r_wout, r_wkv, r_wbin, r_wbout), (s_a, s_b, s_c, s_relb, s_sink) = _reduce_exchange(
        g_win, [l_wout, l_wkv, l_wbin, l_wbout], [acc_a, acc_b, acc_c, relb, dsink])
    mine = lambda rows: lax.dynamic_slice_in_dim(rows, me * shard, shard, axis=1)
    loss = s_c[1, 0]
    weights = [a_pre_norm, a_w_in[0], a_conv_w[0], a_w_out[0], a_post_norm, kvn, w_kv, rel_bias, b_pre_norm,
               b_w_in[0], b_sinks, b_w_out[0], b_post_norm]
    grads = [mine(s_a[0:1]), r_win, mine(s_a[1:4]), r_wout, mine(s_b[2:3]), s_b[1:2], r_wkv,
             s_relb[:, :N_BUCKETS].T, s_b[0:1], r_wbin, s_sink[0:1, :N_Q_HEADS], r_wbout, s_c[0:1]]
    first = [m_a_pre_norm, m_a_w_in[0], m_a_conv_w[0], m_a_w_out[0], m_a_post_norm, m_kv_norm.reshape(1, d), m_w_kv,
             m_rel_bias, m_b_pre_norm, m_b_w_in[0], m_b_sinks, m_b_w_out[0], m_b_post_norm]
    second = [v_a_pre_norm, v_a_w_in[0], v_a_conv_w[0], v_a_w_out[0], v_a_post_norm, v_kv_norm.reshape(1, d), v_w_kv,
              v_rel_bias, v_b_pre_norm, v_b_w_in[0], v_b_sinks, v_b_w_out[0], v_b_post_norm]
    deltas, new_m, new_v = _adamw(weights, grads, first, second)

    shapes = [a_pre_norm.shape, a_w_in.shape, a_conv_w.shape, a_w_out.shape, a_post_norm.shape, kv_norm.shape,
              w_kv.shape, rel_bias.shape, b_pre_norm.shape, b_w_in.shape, b_sinks.shape, b_w_out.shape, b_post_norm.shape]
    shaped = lambda arrays: [a.reshape(s) for a, s in zip(arrays, shapes)]
    return (loss, gx.reshape(x.shape), *shaped(grads), *shaped(deltas), *shaped(new_m), *shaped(new_v))
```

```python
import functools
import math

import jax
import jax.numpy as jnp
from jax import lax
from jax.experimental import pallas as pl
from jax.experimental.pallas import tpu as pltpu

HEAD_DIM = 64
N_Q_HEADS = 16
N_KV_HEADS = 2
GROUP = N_Q_HEADS // N_KV_HEADS
BLOCK = 128
N_BUCKETS = 32
MAX_DISTANCE = 128
EPS = 1e-6
NEG_INF = -1e30
SCALE = HEAD_DIM ** -0.5

ADAM_LR = 0.001
ADAM_B1 = 0.9
ADAM_B2 = 0.999
ADAM_EPS = 1e-08
ADAM_WD = 0.01
ADAM_STEP = 10

N_DEV = 8
LANES = 128
F32 = jnp.float32
BF16 = jnp.bfloat16
MESH = pl.DeviceIdType.MESH
MIB = 1024 * 1024


def _params(semantics=None, vmem_mib=48):
    return pltpu.CompilerParams(dimension_semantics=semantics, vmem_limit_bytes=vmem_mib * MIB)


def _full(shape):
    zeros = (0,) * len(shape)
    return pl.BlockSpec(shape, lambda *_: zeros, pipeline_mode=pl.Buffered(1))


def _resident(shape):
    zeros = (0,) * len(shape)
    return pl.BlockSpec(shape, lambda *_: zeros)


def _rows(ts, cols):
    return pl.BlockSpec((ts, cols), lambda i: (i, 0))


def _dot(a, b):
    return jnp.dot(a, b, preferred_element_type=F32)


def _dot_nt(a, b):
    return lax.dot_general(a, b, (((1,), (1,)), ((), ())), preferred_element_type=F32)


def _dot_tn(a, b):
    return lax.dot_general(a, b, (((0,), (0,)), ((), ())), preferred_element_type=F32)


def _rms(xf):
    r = lax.rsqrt(jnp.mean(xf * xf, axis=-1, keepdims=True) + EPS)
    return xf * r, r


def _rms_bwd(dn, xn, r):
    return r * (dn - xn * jnp.mean(dn * xn, axis=-1, keepdims=True))


def _silu(z):
    s = jax.nn.sigmoid(z)
    return z * s, s * (1.0 + z * (1.0 - s))


def _my_index():
    return 4 * lax.axis_index("x") + 2 * lax.axis_index("y") + lax.axis_index("c")


def _all_gather(shards, out_dtypes):
    n = len(shards)

    def body(*refs):
        ins, outs = refs[:n], refs[n:2 * n]
        send_sems, recv_sems = refs[2 * n], refs[2 * n + 1]
        x, y, c = lax.axis_index("x"), lax.axis_index("y"), lax.axis_index("c")
        me, sibling = (x, y, c), (x, y, 1 - c)
        x_nbr, y_nbr, diagonal = (1 - x, y), (x, 1 - y), (1 - x, 1 - y)
        south = c == 0
        relayed = (jnp.where(south, 1 - x, x), jnp.where(south, y, 1 - y))
        relay_to = (jnp.where(south, x, 1 - x), jnp.where(south, 1 - y, y))

        def copy(t, k, block, to):
            rows = outs[t].at[4 * block[0] + 2 * block[1] + block[2]]
            return pltpu.make_async_remote_copy(
                src_ref=rows, dst_ref=rows, send_sem=send_sems.at[t, k], recv_sem=recv_sems.at[t, k],
                device_id=to, device_id_type=MESH)

        for t in range(n):
            outs[t][pl.ds(_my_index(), 1)] = ins[t][...].astype(outs[t].dtype)[None]
        started = []

        def start(cp):
            cp.start()
            started.append(cp)

        for t in range(n):
            start(copy(t, 0, me, sibling))
            start(copy(t, 1, me, (*x_nbr, c)))
            start(copy(t, 2, me, (*y_nbr, c)))
        for k, chip in ((1, x_nbr), (2, y_nbr)):
            for t in range(n):
                copy(t, k, (*chip, c), me).wait_recv()
                start(copy(t, 3 + k, (*chip, c), sibling))
        for t in range(n):
            start(copy(t, 3, (*relayed, c), (*relay_to, c)))
        for t in range(n):
            copy(t, 3, (*diagonal, c), me).wait_recv()
            start(copy(t, 6, (*diagonal, c), sibling))
        for t in range(n):
            copy(t, 0, sibling, me).wait_recv()
        for k, chip in ((4, x_nbr), (5, y_nbr), (6, diagonal)):
            for t in range(n):
                copy(t, k, (*chip, 1 - c), me).wait_recv()
        for cp in started:
            cp.wait_send()

    vmem = pl.BlockSpec(memory_space=pltpu.VMEM)
    return pl.pallas_call(
        body,
        name="gather_weights",
        out_shape=[jax.ShapeDtypeStruct((N_DEV,) + s.shape, dt) for s, dt in zip(shards, out_dtypes)],
        in_specs=[vmem] * n,
        out_specs=[vmem] * n,
        scratch_shapes=[pltpu.SemaphoreType.DMA((n, 7)), pltpu.SemaphoreType.DMA((n, 7))],
        compiler_params=_params(vmem_mib=48),
    )(*shards)


def _peer(k):
    x, y, c = lax.axis_index("x"), lax.axis_index("y"), lax.axis_index("c")
    px = 1 - x if k & 4 else x
    py = 1 - y if k & 2 else y
    pc = 1 - c if k & 1 else c
    return (px, py, pc), 4 * px + 2 * py + pc


def _exchange(srcs, dsts, send_sems, recv_sems, local_sems, scatter):
    me = _my_index()
    sends, arrivals = [], []
    for k in range(1, N_DEV):
        peer, pidx = _peer(k)
        for t, (src, dst) in enumerate(zip(srcs, dsts)):
            mine = src.at[pidx] if scatter else src
            sems = dict(send_sem=send_sems.at[t, k - 1], recv_sem=recv_sems.at[t, k - 1], device_id=peer, device_id_type=MESH)
            sends.append(pltpu.make_async_remote_copy(src_ref=mine, dst_ref=dst.at[me], **sems))
            arrivals.append(pltpu.make_async_remote_copy(src_ref=mine, dst_ref=dst.at[pidx], **sems))
    local = [pltpu.make_async_copy(src.at[me] if scatter else src, dst.at[me], local_sems.at[t])
             for t, (src, dst) in enumerate(zip(srcs, dsts))]
    return sends, arrivals, local


def _exchange_start(*args):
    sends, _, local = _exchange(*args)
    for cp in sends + local:
        cp.start()


def _exchange_wait(*args):
    sends, arrivals, local = _exchange(*args)
    for cp in arrivals:
        cp.wait_recv()
    for cp in sends:
        cp.wait_send()
    for cp in local:
        cp.wait()


def _exchange_sems(n):
    return [pltpu.SemaphoreType.DMA((n, N_DEV - 1)), pltpu.SemaphoreType.DMA((n, N_DEV - 1)), pltpu.SemaphoreType.DMA((n,))]


HBM_SPEC = pl.BlockSpec(memory_space=pl.ANY)


def _sum_slots(recv_ref, out_ref):
    rows = out_ref.shape[0]
    chunk = min(rows, 128)

    def add(i, carry):
        r0 = pl.multiple_of(i * chunk, chunk)
        acc = recv_ref[0, pl.ds(r0, chunk), :].astype(F32)
        for dev in range(1, N_DEV):
            acc = acc + recv_ref[dev, pl.ds(r0, chunk), :].astype(F32)
        out_ref[pl.ds(r0, chunk), :] = acc
        return carry

    lax.fori_loop(0, rows // chunk, add, 0)


N_CHIPS = N_DEV // 2


def _rows_loop(rows, fn):
    chunk = min(rows, 128)

    def step(i, carry):
        fn(pl.ds(pl.multiple_of(i * chunk, chunk), chunk))
        return carry

    lax.fori_loop(0, rows // chunk, step, 0)


def _chip_reduce(g_ref, out_ref, sib_ref, chip_ref, send_ref, sems, between):
    sib_send, sib_recv, chip_send, chip_recv = sems
    x, y, c = lax.axis_index("x"), lax.axis_index("y"), lax.axis_index("c")
    my_chip = 2 * x + y
    rows = out_ref.shape[0]

    def chip_of(k):
        cx = 1 - x if k & 2 else x
        cy = 1 - y if k & 1 else y
        return (cx, cy), 2 * cx + cy

    def to_sibling(t):
        return pltpu.make_async_remote_copy(
            src_ref=g_ref.at[2 * t + 1 - c], dst_ref=sib_ref.at[t], send_sem=sib_send.at[t], recv_sem=sib_recv.at[t],
            device_id=(x, y, 1 - c), device_id_type=MESH)

    def to_chip(k):
        (cx, cy), t = chip_of(k)
        return t, pltpu.make_async_remote_copy(
            src_ref=send_ref.at[k - 1], dst_ref=chip_ref.at[my_chip], send_sem=chip_send.at[k - 1],
            recv_sem=chip_recv.at[k - 1], device_id=(cx, cy, c), device_id_type=MESH)

    def from_chip(k):
        _, t = chip_of(k)
        return pltpu.make_async_remote_copy(
            src_ref=send_ref.at[k - 1], dst_ref=chip_ref.at[t], send_sem=chip_send.at[k - 1],
            recv_sem=chip_recv.at[k - 1], device_id=(x, y, c), device_id_type=MESH)

    for t in range(N_CHIPS):
        to_sibling(t).start()
    between()
    for t in range(N_CHIPS):
        to_sibling(t).wait_recv()

    def pair_sum(t, r):
        return g_ref[2 * t + c, r, :].astype(F32) + sib_ref[t, r, :].astype(F32)

    for k in range(1, N_CHIPS):
        t, cp = to_chip(k)

        def fill(r, t=t, k=k):
            send_ref[k - 1, r, :] = pair_sum(t, r).astype(BF16)

        _rows_loop(rows, fill)
        cp.start()

    def own(r):
        chip_ref[my_chip, r, :] = pair_sum(my_chip, r).astype(BF16)

    _rows_loop(rows, own)
    for k in range(1, N_CHIPS):
        from_chip(k).wait_recv()

    def total(r):
        acc = chip_ref[0, r, :].astype(F32)
        for t in range(1, N_CHIPS):
            acc = acc + chip_ref[t, r, :].astype(F32)
        out_ref[r, :] = acc

    _rows_loop(rows, total)
    for t in range(N_CHIPS):
        to_sibling(t).wait_send()
    for k in range(1, N_CHIPS):
        to_chip(k)[1].wait_send()


def _reduce_exchange(part, landed, smalls):
    nl, ng = len(landed), len(smalls)
    n_out = 1 + nl + ng

    def body(*refs):
        p_in, l_in, s_in = refs[0], refs[1:1 + nl], refs[1 + nl:n_out]
        p_out, l_out, s_out = refs[n_out], refs[n_out + 1:n_out + 1 + nl], refs[n_out + 1 + nl:2 * n_out]
        scratch = refs[2 * n_out:]
        s_recv, (sib_ref, chip_ref, send_ref), sems = scratch[:ng], scratch[ng:ng + 3], scratch[ng + 3:]

        def between():
            _exchange_start(s_in, s_recv, *sems[4:], False)
            for t in range(nl):
                _sum_slots(l_in[t], l_out[t])

        _chip_reduce(p_in, p_out, sib_ref, chip_ref, send_ref, sems[:4], between)
        _exchange_wait(s_in, s_recv, *sems[4:], False)
        for t in range(ng):
            acc = s_recv[t][0]
            for dev in range(1, N_DEV):
                acc = acc + s_recv[t][dev]
            s_out[t][...] = acc

    vmem = pl.BlockSpec(memory_space=pltpu.VMEM)
    slot = part.shape[1:]
    outs = pl.pallas_call(
        body,
        name="reduce_grads",
        out_shape=[jax.ShapeDtypeStruct(p.shape[1:], F32) for p in [part] + landed]
        + [jax.ShapeDtypeStruct(s.shape, F32) for s in smalls],
        in_specs=[vmem] * n_out,
        out_specs=[vmem] * n_out,
        scratch_shapes=[pltpu.VMEM((N_DEV,) + s.shape, F32) for s in smalls]
        + [pltpu.VMEM((N_CHIPS,) + slot, BF16), pltpu.VMEM((N_CHIPS,) + slot, BF16), pltpu.VMEM((N_CHIPS - 1,) + slot, BF16)]
        + [pltpu.SemaphoreType.DMA((N_CHIPS,)), pltpu.SemaphoreType.DMA((N_CHIPS,)),
           pltpu.SemaphoreType.DMA((N_CHIPS - 1,)), pltpu.SemaphoreType.DMA((N_CHIPS - 1,))]
        + _exchange_sems(ng),
        compiler_params=_params(vmem_mib=56),
    )(part, *landed, *smalls)
    return outs[0], outs[1:1 + nl], outs[1 + nl:]


def _layer_a_fwd(x2, sm, win_g, wout, later, ts):
    seq, d = x2.shape
    width = wout.shape[0]
    half = win_g.shape[2]
    n_half = width // half
    nl = len(later)
    nt = seq // ts

    def body(x_ref, sm_ref, win_ref, wout_ref, *refs):
        shard_refs, refs = refs[:nl], refs[nl:]
        h1_ref, n1_ref, proj_ref, conv_ref, y_ref, ya_ref = refs[:6]
        gathered_refs, (vprev_ref, *sems) = refs[6:6 + nl], refs[6 + nl:]

        @pl.when(pl.program_id(0) == 0)
        def _():
            vprev_ref[...] = jnp.zeros_like(vprev_ref)
            _exchange_start(shard_refs, gathered_refs, *sems, False)

        @pl.when(pl.program_id(0) == nt - 1)
        def _():
            _exchange_wait(shard_refs, gathered_refs, *sems, False)

        xf = x_ref[...]
        xn, _ = _rms(xf)
        n1 = (xn * sm_ref[0:1, :]).astype(BF16)
        n1_ref[...] = n1
        row = lax.broadcasted_iota(jnp.int32, (ts, half), 0)
        ya = jnp.zeros((ts, d), F32)
        for hh in range(n_half):
            cols = slice(hh * half, (hh + 1) * half)
            parts = []
            for part in range(4):
                j = part * n_half + hh
                pj = _dot(n1, win_ref[j])
                proj_ref[:, j * half:(j + 1) * half] = pj.astype(BF16)
                parts.append(pj)
            b, c, u, z = parts
            v = c * u
            last1, last2 = vprev_ref[7:8, cols], vprev_ref[6:7, cols]
            v1 = jnp.where(row == 0, last1, pltpu.roll(v, 1, 0))
            v2 = jnp.where(row == 0, last2, jnp.where(row == 1, last1, pltpu.roll(v, 2, 0)))
            vprev_ref[:, cols] = v[ts - 8:ts, :]
            conv = sm_ref[1:2, cols] * v2 + sm_ref[2:3, cols] * v1 + sm_ref[3:4, cols] * v
            conv_ref[:, cols] = conv.astype(BF16)
            yh = (b * conv * _silu(z)[0]).astype(BF16)
            y_ref[:, cols] = yh
            ya = ya + _dot(yh, wout_ref[cols, :])
        ya_ref[...] = ya
        h1_ref[...] = xf + _rms(ya)[0] * sm_ref[4:5, :]

    outs = pl.pallas_call(
        body,
        name="layer_a_fwd",
        grid=(nt,),
        in_specs=[_rows(ts, d), _full(sm.shape), _full(win_g.shape), _full(wout.shape)] + [HBM_SPEC] * nl,
        out_specs=[_rows(ts, d), _rows(ts, d), _rows(ts, 4 * width), _rows(ts, width), _rows(ts, width), _rows(ts, d)]
        + [HBM_SPEC] * nl,
        out_shape=[
            jax.ShapeDtypeStruct((seq, d), F32),
            jax.ShapeDtypeStruct((seq, d), BF16),
            jax.ShapeDtypeStruct((seq, 4 * width), BF16),
            jax.ShapeDtypeStruct((seq, width), BF16),
            jax.ShapeDtypeStruct((seq, width), BF16),
            jax.ShapeDtypeStruct((seq, d), F32),
        ] + [jax.ShapeDtypeStruct((N_DEV,) + s.shape, s.dtype) for s in later],
        scratch_shapes=[pltpu.VMEM((8, width), F32)] + _exchange_sems(nl),
        compiler_params=_params(("arbitrary",), 56),
    )(x2, sm, win_g, wout, *later)
    return outs[:6], outs[6:]


def _layer_b_in(h1, kvn, bpre, wkv, wbin_g, ts):
    seq, d = h1.shape
    kvw = wkv.shape[1]
    cw = wbin_g.shape[2]
    aw = N_Q_HEADS * HEAD_DIM
    per = aw // cw

    def body(h1_ref, kvn_ref, bpre_ref, wkv_ref, wbin_ref, n3_ref, n4_ref, kv_ref, q_ref, z2_ref):
        hn, _ = _rms(h1_ref[...])
        n3 = (hn * kvn_ref[...]).astype(BF16)
        n4 = (hn * bpre_ref[...]).astype(BF16)
        n3_ref[...] = n3
        n4_ref[...] = n4
        kv_ref[...] = _dot(n3, wkv_ref[...]).astype(BF16)
        for j in range(N_DEV):
            pj = _dot(n4, wbin_ref[j])
            if j < per:
                q_ref[:, j * cw:(j + 1) * cw] = pj.astype(BF16)
            else:
                z2_ref[:, (j - per) * cw:(j - per + 1) * cw] = pj

    return pl.pallas_call(
        body,
        name="layer_b_in",
        grid=(seq // ts,),
        in_specs=[_rows(ts, d), _full(kvn.shape), _full(bpre.shape), _full(wkv.shape), _full(wbin_g.shape)],
        out_specs=[_rows(ts, d), _rows(ts, d), _rows(ts, kvw), _rows(ts, aw), _rows(ts, aw)],
        out_shape=[
            jax.ShapeDtypeStruct((seq, d), BF16),
            jax.ShapeDtypeStruct((seq, d), BF16),
            jax.ShapeDtypeStruct((seq, kvw), BF16),
            jax.ShapeDtypeStruct((seq, aw), BF16),
            jax.ShapeDtypeStruct((seq, aw), F32),
        ],
        compiler_params=_params(("parallel",), 48),
    )(h1, kvn, bpre, wkv, wbin_g)


N_PAIRS = N_Q_HEADS // 2
BAND = 2 * BLOCK


def _bias_table(rel_bias, bucket_t, in_window_t):
    def body(rb_ref, bucket_ref, win_ref, out_ref):
        bk = bucket_ref[...]
        inside = win_ref[...] != 0
        has_prev = lax.broadcasted_iota(jnp.int32, bk.shape, 0) >= BLOCK
        for h in range(N_Q_HEADS):
            acc = jnp.full(bk.shape, NEG_INF, F32)
            for b in range(N_BUCKETS):
                acc = jnp.where(jnp.logical_and(bk == b, inside), rb_ref[b, h], acc)
            cols = slice((h % 2) * BLOCK, (h % 2 + 1) * BLOCK)
            out_ref[1, h // 2, :, cols] = acc
            out_ref[0, h // 2, :, cols] = jnp.where(has_prev, acc, NEG_INF)

    vmem = pl.BlockSpec(memory_space=pltpu.VMEM)
    return pl.pallas_call(
        body,
        name="bias_table",
        in_specs=[pl.BlockSpec(memory_space=pltpu.SMEM), vmem, vmem],
        out_specs=vmem,
        out_shape=jax.ShapeDtypeStruct((2, N_PAIRS, BAND, 2 * BLOCK), F32),
    )(rel_bias, bucket_t, in_window_t)


Q_BLOCKS = 2


def _banded_tiles(kvp_ref, kvc_ref):
    tile = kvc_ref[...].astype(F32)
    blocks = [kvp_ref[...].astype(F32)] + [tile[u * BLOCK:(u + 1) * BLOCK] for u in range(Q_BLOCKS)]
    return [_banded_kv(blocks[u], blocks[u + 1]) for u in range(Q_BLOCKS)]


def _bias_of(bias_ref, i, u, m):
    return bias_ref[jnp.minimum(i, 1) if u == 0 else 1, m]


def _banded_kv(kvp, kvc):
    kw = N_KV_HEADS * HEAD_DIM
    out = []
    for full in (jnp.concatenate([kvp[:, :kw], kvc[:, :kw]], axis=0), jnp.concatenate([kvp[:, kw:], kvc[:, kw:]], axis=0)):
        lo = lax.broadcasted_iota(jnp.int32, full.shape, 1) < HEAD_DIM
        rolled = pltpu.roll(full, HEAD_DIM, 1)
        x2 = [jnp.where(lo, full, rolled).astype(BF16), jnp.where(lo, rolled, full).astype(BF16)]
        ft = full.T
        x2t = [jnp.concatenate([ft[kh * HEAD_DIM:(kh + 1) * HEAD_DIM]] * 2, axis=0).astype(BF16) for kh in range(N_KV_HEADS)]
        out += [x2, x2t]
    return out


def _pair_rows(ref, rows, m, scale=None):
    both = ref[rows, m * LANES:(m + 1) * LANES].astype(F32)
    if scale is not None:
        both = both * scale
    lo = lax.broadcasted_iota(jnp.int32, both.shape, 1) < HEAD_DIM
    zero = jnp.zeros_like(both)
    return jnp.concatenate([jnp.where(lo, both, zero), jnp.where(lo, zero, both)], axis=0).astype(BF16)


def _pair_cols(res_t):
    top = lax.broadcasted_iota(jnp.int32, (LANES, BLOCK), 0) < HEAD_DIM
    return jnp.where(top, res_t[:, :BLOCK], res_t[:, BLOCK:]).T


def _sink_row(sink_ref, m):
    first = lax.broadcasted_iota(jnp.int32, (1, 2 * BLOCK), 1) < BLOCK
    return jnp.where(first, sink_ref[0, 2 * m], sink_ref[0, 2 * m + 1])


def _probs_t(k2, qpair, bias, sink):
    return _softmax_t(_dot_nt(k2, qpair) + bias, sink)


def _softmax_t(logits, sink):
    mx = jnp.maximum(jnp.max(logits, axis=0, keepdims=True), sink)
    p = jnp.exp(logits - mx)
    sink_p = jnp.exp(sink - mx)
    inv = 1.0 / (jnp.sum(p, axis=0, keepdims=True) + sink_p)
    return p * inv, sink_p * inv


def _attn_fwd(q, kv, z2, biasm, sinks):
    seq, aw = q.shape
    kvw = kv.shape[1]
    nb = seq // BLOCK

    tile = Q_BLOCKS * BLOCK

    def body(sink_ref, q_ref, kvc_ref, kvp_ref, z2_ref, bias_ref, attn_ref, o_ref, acc_ref):
        i = pl.program_id(0)
        banded = _banded_tiles(kvp_ref, kvc_ref)
        units = [(u, m) for u in range(Q_BLOCKS) for m in range(N_PAIRS)]
        kv_of = lambda m: (2 * m) // GROUP
        logits, probs = {}, {}
        for step in range(len(units) + 2):
            if step < len(units):
                u, m = units[step]
                qpair = _pair_rows(q_ref, slice(u * BLOCK, (u + 1) * BLOCK), m, SCALE)
                logits[step] = _dot_nt(banded[u][0][kv_of(m)], qpair) + _bias_of(bias_ref, i, u, m)
            if 0 <= step - 1 < len(units):
                u, m = units[step - 1]
                probs[step - 1] = _softmax_t(logits.pop(step - 1), _sink_row(sink_ref, m))[0].astype(BF16)
            if 0 <= step - 2 < len(units):
                u, m = units[step - 2]
                out_t = _dot(banded[u][3][kv_of(m)], probs.pop(step - 2))
                acc_ref[u * BLOCK:(u + 1) * BLOCK, m * LANES:(m + 1) * LANES] = _pair_cols(out_t)
        attn = acc_ref[...]
        attn_ref[...] = attn.astype(BF16)
        o_ref[...] = (attn * _silu(z2_ref[...])[0]).astype(BF16)

    blk = lambda w: pl.BlockSpec((tile, w), lambda i: (i, 0))
    return pl.pallas_call(
        body,
        name="attn_fwd",
        grid=(seq // tile,),
        in_specs=[
            pl.BlockSpec(memory_space=pltpu.SMEM),
            blk(aw),
            blk(kvw),
            pl.BlockSpec((BLOCK, kvw), lambda i: (jnp.maximum(Q_BLOCKS * i - 1, 0), 0)),
            blk(aw),
            _full(biasm.shape),
        ],
        out_specs=[blk(aw), blk(aw)],
        out_shape=[jax.ShapeDtypeStruct((seq, aw), BF16), jax.ShapeDtypeStruct((seq, aw), BF16)],
        scratch_shapes=[pltpu.VMEM((tile, aw), F32)],
        compiler_params=_params(("arbitrary",), 40),
    )(sinks, q, kv, kv, z2, biasm)


def _layer_b_out(o, attn, z2, h1, target, wbout, bpost, ts):
    seq, d = h1.shape
    aw = o.shape[1]

    def body(o_ref, attn_ref, z2_ref, h1_ref, tgt_ref, w_ref, g_ref, dh2_ref, dyb_ref, dattn_ref, dz2_ref, acc_ref):
        @pl.when(pl.program_id(0) == 0)
        def _():
            acc_ref[...] = jnp.zeros_like(acc_ref)

        w = w_ref[...]
        yb = _dot(o_ref[...], w)
        ybn, r = _rms(yb)
        g = g_ref[...]
        diff = h1_ref[...] + ybn * g - tgt_ref[...]
        dh2 = diff * (1.0 / d)
        dh2_ref[...] = dh2
        acc_ref[0:1, :] += jnp.sum(dh2 * ybn, axis=0, keepdims=True)
        tok = jnp.mean(diff * diff, axis=-1, keepdims=True)
        acc_ref[1:2, :] += 0.5 * jnp.sum(tok, axis=0, keepdims=True)
        dyb = _rms_bwd(dh2 * g, ybn, r).astype(BF16)
        dyb_ref[...] = dyb
        do = _dot_nt(dyb, w)
        sz, dsz = _silu(z2_ref[...])
        dattn_ref[...] = (do * sz).astype(BF16)
        dz2_ref[...] = (do * attn_ref[...].astype(F32) * dsz).astype(BF16)

    return pl.pallas_call(
        body,
        name="layer_b_out",
        grid=(seq // ts,),
        in_specs=[_rows(ts, aw), _rows(ts, aw), _rows(ts, aw), _rows(ts, d), _rows(ts, d), _full(wbout.shape), _full(bpost.shape)],
        out_specs=[_rows(ts, d), _rows(ts, d), _rows(ts, aw), _rows(ts, aw), _resident((8, d))],
        out_shape=[
            jax.ShapeDtypeStruct((seq, d), F32),
            jax.ShapeDtypeStruct((seq, d), BF16),
            jax.ShapeDtypeStruct((seq, aw), BF16),
            jax.ShapeDtypeStruct((seq, aw), BF16),
            jax.ShapeDtypeStruct((8, d), F32),
        ],
        compiler_params=_params(("arbitrary",), 48),
    )(o, attn, z2, h1, target, wbout, bpost)


def _attn_bwd(q, kv, dattn, biasm, sinks, ready):
    seq, aw = q.shape
    kvw = kv.shape[1]
    kw = N_KV_HEADS * HEAD_DIM
    nb = seq // BLOCK
    pairs_per_kv = N_PAIRS // N_KV_HEADS
    nr = len(ready)

    tile = Q_BLOCKS * BLOCK
    nsteps = seq // tile
    held = (Q_BLOCKS - 1) * BLOCK

    def body(sink_ref, q_ref, kvc_ref, kvp_ref, da_ref, bias_ref, *refs):
        ready_refs, (dq_ref, dkv_ref, dssum_ref, dsink_ref) = refs[:nr], refs[nr:nr + 4]
        landed_refs, scratch = refs[nr + 4:2 * nr + 4], refs[2 * nr + 4:]
        carry_ref, done_ref, qs_ref, dos_ref, dst_ref, pt_ref, *sems = scratch
        i = pl.program_id(0)

        @pl.when(i == 0)
        def _():
            dssum_ref[...] = jnp.zeros_like(dssum_ref)
            dsink_ref[...] = jnp.zeros_like(dsink_ref)
            carry_ref[...] = jnp.zeros_like(carry_ref)
            done_ref[...] = jnp.zeros_like(done_ref)
            _exchange_start(ready_refs, landed_refs, *sems, True)

        @pl.when(i == nsteps)
        def _():
            _exchange_wait(ready_refs, landed_refs, *sems, True)

        @pl.when(i < nsteps)
        def _():
            lo = lax.broadcasted_iota(jnp.int32, (BAND, LANES), 1) < HEAD_DIM
            head_lane = lax.broadcasted_iota(jnp.int32, (1, LANES), 1)
            banded = _banded_tiles(kvp_ref, kvc_ref)
            units = [(u, m) for u in range(Q_BLOCKS) for m in range(N_PAIRS)]
            dsink = jnp.zeros((1, LANES), F32)
            folded = {}
            logits, dps, dsbs = {}, {}, {}
            for step in range(len(units) + 2):
                if step < len(units):
                    u, m = units[step]
                    kh, rows = m // pairs_per_kv, slice((m % pairs_per_kv) * BAND, (m % pairs_per_kv + 1) * BAND)
                    qrows = slice(u * BLOCK, (u + 1) * BLOCK)
                    qpair = _pair_rows(q_ref, qrows, m, SCALE)
                    dopair = _pair_rows(da_ref, qrows, m)
                    qs_ref[u, kh, rows, :] = qpair
                    dos_ref[u, kh, rows, :] = dopair
                    logits[step] = _dot_nt(banded[u][0][kh], qpair) + _bias_of(bias_ref, i, u, m)
                    dps[step] = _dot_nt(banded[u][2][kh], dopair)
                if 0 <= step - 1 < len(units):
                    u, m = units[step - 1]
                    kh, rows = m // pairs_per_kv, slice((m % pairs_per_kv) * BAND, (m % pairs_per_kv + 1) * BAND)
                    pn, sink_p = _softmax_t(logits.pop(step - 1), _sink_row(sink_ref, m))
                    dp = dps.pop(step - 1)
                    delta = jnp.sum(pn * dp, axis=0, keepdims=True)
                    ds = pn * (dp - delta)
                    dssum_ref[m] += ds
                    sink_term = sink_p * delta
                    for e in range(2):
                        total = jnp.sum(sink_term[:, e * BLOCK:(e + 1) * BLOCK], axis=1, keepdims=True)
                        dsink = dsink - jnp.where(head_lane == 2 * m + e, total, 0.0)
                    dsbs[step - 1] = ds.astype(BF16)
                    dst_ref[u, kh, :, rows] = dsbs[step - 1]
                    pt_ref[u, kh, :, rows] = pn.astype(BF16)
                if 0 <= step - 2 < len(units):
                    u, m = units[step - 2]
                    kh = m // pairs_per_kv
                    dq_t = _dot(banded[u][1][kh], dsbs.pop(step - 2))
                    dq_ref[u * BLOCK:(u + 1) * BLOCK, m * LANES:(m + 1) * LANES] = (_pair_cols(dq_t) * SCALE).astype(BF16)
                    if m % pairs_per_kv == pairs_per_kv - 1:
                        for name, lhs_ref, rhs_ref in (("k", dst_ref, qs_ref), ("v", pt_ref, dos_ref)):
                            acc = _dot(lhs_ref[u, kh], rhs_ref[u, kh])
                            folded[u, kh, name] = acc + pltpu.roll(acc, HEAD_DIM, 1)
            dsink_ref[0:1, :] += dsink
            dkv = [jnp.concatenate([jnp.where(lo, folded[u, 0, n], folded[u, 1, n]) for n in ("k", "v")], axis=1)
                   for u in range(Q_BLOCKS)]

            @pl.when(i > 0)
            def _():
                if held:
                    dkv_ref[:held, :] = done_ref[...].astype(BF16)
                dkv_ref[held:, :] = (carry_ref[...] + dkv[0][:BLOCK]).astype(BF16)

            for u in range(Q_BLOCKS - 1):
                done_ref[u * BLOCK:(u + 1) * BLOCK, :] = dkv[u][BLOCK:] + dkv[u + 1][:BLOCK]
            carry_ref[...] = dkv[Q_BLOCKS - 1][BLOCK:]

        @pl.when(i == nsteps)
        def _():
            if held:
                dkv_ref[:held, :] = done_ref[...].astype(BF16)
            dkv_ref[held:, :] = carry_ref[...].astype(BF16)

    last = nsteps - 1
    blk = lambda w: pl.BlockSpec((tile, w), lambda i: (jnp.minimum(i, last), 0))
    outs = pl.pallas_call(
        body,
        name="attn_bwd",
        grid=(nsteps + 1,),
        in_specs=[
            pl.BlockSpec(memory_space=pltpu.SMEM),
            blk(aw),
            blk(kvw),
            pl.BlockSpec((BLOCK, kvw), lambda i: (jnp.clip(Q_BLOCKS * i - 1, 0, nb - 1), 0)),
            blk(aw),
            _full(biasm.shape),
        ] + [HBM_SPEC] * nr,
        out_specs=[
            blk(aw),
            pl.BlockSpec((tile, kvw), lambda i: (jnp.maximum(i - 1, 0), 0)),
            _resident(biasm.shape[1:]),
            _resident((8, LANES)),
        ] + [HBM_SPEC] * nr,
        out_shape=[
            jax.ShapeDtypeStruct((seq, aw), BF16),
            jax.ShapeDtypeStruct((seq, kvw), BF16),
            jax.ShapeDtypeStruct(biasm.shape[1:], F32),
            jax.ShapeDtypeStruct((8, LANES), F32),
        ] + [jax.ShapeDtypeStruct(g.shape, g.dtype) for g in ready],
        scratch_shapes=[
            pltpu.VMEM((BLOCK, kvw), F32),
            pltpu.VMEM((max(held, 8), kvw), F32),
            pltpu.VMEM((Q_BLOCKS, N_KV_HEADS, pairs_per_kv * BAND, LANES), BF16),
            pltpu.VMEM((Q_BLOCKS, N_KV_HEADS, pairs_per_kv * BAND, LANES), BF16),
            pltpu.VMEM((Q_BLOCKS, N_KV_HEADS, BAND, pairs_per_kv * BAND), BF16),
            pltpu.VMEM((Q_BLOCKS, N_KV_HEADS, BAND, pairs_per_kv * BAND), BF16),
        ] + _exchange_sems(nr),
        compiler_params=_params(("arbitrary",), 48),
    )(sinks, q, kv, kv, dattn, biasm, *ready)
    return outs[:4], outs[4:]


def _relbias_grad(dssum2, bucket_row, chunk):
    heads, n = dssum2.shape

    def body(a_ref, bucket_ref, out_ref):
        @pl.when(pl.program_id(0) == 0)
        def _():
            out_ref[...] = jnp.zeros_like(out_ref)

        a = a_ref[...]
        hi = a.astype(BF16)
        lo = (a - hi.astype(F32)).astype(BF16)
        onehot_t = (lax.broadcasted_iota(jnp.int32, (LANES, chunk), 0) == bucket_ref[...]).astype(F32).astype(BF16)
        out_ref[...] += _dot_nt(hi, onehot_t) + _dot_nt(lo, onehot_t)

    return pl.pallas_call(
        body,
        name="relbias_grad",
        grid=(n // chunk,),
        in_specs=[pl.BlockSpec((heads, chunk), lambda i: (0, i)), pl.BlockSpec((1, chunk), lambda i: (0, i))],
        out_specs=_resident((heads, LANES)),
        out_shape=jax.ShapeDtypeStruct((heads, LANES), F32),
        compiler_params=_params(("arbitrary",), 32),
    )(dssum2, bucket_row)


def _layer_b_in_bwd(dh2, dq, dz2, dkv, h1, ya, wbin_g, wkv, kvn, bpre, sm, ready, ts):
    seq, d = h1.shape
    aw = dq.shape[1]
    kvw = dkv.shape[1]
    cw = wbin_g.shape[2]
    per = aw // cw

    nr = len(ready)
    nt = seq // ts

    def body(dh2_ref, dq_ref, dz2_ref, dkv_ref, h1_ref, ya_ref, wbin_ref, wkv_ref, kvn_ref, bpre_ref, sm_ref, *refs):
        ready_refs, (dh1_ref, dya_ref, acc_ref) = refs[:nr], refs[nr:nr + 3]
        landed_refs, sems = refs[nr + 3:2 * nr + 3], refs[2 * nr + 3:]

        @pl.when(pl.program_id(0) == 0)
        def _():
            acc_ref[...] = jnp.zeros_like(acc_ref)
            _exchange_start(ready_refs, landed_refs, *sems, True)

        @pl.when(pl.program_id(0) == nt - 1)
        def _():
            _exchange_wait(ready_refs, landed_refs, *sems, True)

        dn4 = jnp.zeros((ts, d), F32)
        for j in range(N_DEV):
            src = dq_ref if j < per else dz2_ref
            jj = j % per
            dn4 = dn4 + _dot_nt(src[:, jj * cw:(jj + 1) * cw], wbin_ref[j])
        dn3 = _dot_nt(dkv_ref[...], wkv_ref[...])
        hn, r = _rms(h1_ref[...])
        acc_ref[0:1, :] += jnp.sum(dn4 * hn, axis=0, keepdims=True)
        acc_ref[1:2, :] += jnp.sum(dn3 * hn, axis=0, keepdims=True)
        dh1 = dh2_ref[...] + _rms_bwd(dn4 * bpre_ref[...] + dn3 * kvn_ref[...], hn, r)
        dh1_ref[...] = dh1
        yan, r2 = _rms(ya_ref[...])
        acc_ref[2:3, :] += jnp.sum(dh1 * yan, axis=0, keepdims=True)
        dya_ref[...] = _rms_bwd(dh1 * sm_ref[4:5, :], yan, r2).astype(BF16)

    outs = pl.pallas_call(
        body,
        name="layer_b_in_bwd",
        grid=(nt,),
        in_specs=[_rows(ts, d), _rows(ts, aw), _rows(ts, aw), _rows(ts, kvw), _rows(ts, d), _rows(ts, d),
                  _full(wbin_g.shape), _full(wkv.shape), _full(kvn.shape), _full(bpre.shape), _full(sm.shape)]
        + [HBM_SPEC] * nr,
        out_specs=[_rows(ts, d), _rows(ts, d), _resident((8, d))] + [HBM_SPEC] * nr,
        out_shape=[jax.ShapeDtypeStruct((seq, d), F32), jax.ShapeDtypeStruct((seq, d), BF16),
                   jax.ShapeDtypeStruct((8, d), F32)] + [jax.ShapeDtypeStruct(g.shape, g.dtype) for g in ready],
        scratch_shapes=_exchange_sems(nr),
        compiler_params=_params(("arbitrary",), 48),
    )(dh2, dq, dz2, dkv, h1, ya, wbin_g, wkv, kvn, bpre, sm, *ready)
    return outs[:3], outs[3:]


def _layer_a_bwd(dya, proj, conv, dh1, x2, wout, win_g, sm, ts):
    seq, d = x2.shape
    width = wout.shape[0]
    half = win_g.shape[2]
    n_half = width // half
    nt = seq // ts

    def body(dya_ref, proj_ref, conv_ref, dh1_ref, x_ref, wout_ref, win_ref, sm_ref, dproj_ref, gx_ref, acc_ref,
             dnext_ref):
        @pl.when(pl.program_id(0) == 0)
        def _():
            acc_ref[...] = jnp.zeros_like(acc_ref)
            dnext_ref[...] = jnp.zeros_like(dnext_ref)

        dy = _dot_nt(dya_ref[...], wout_ref[...])
        row = lax.broadcasted_iota(jnp.int32, (ts, half), 0)
        dn1 = jnp.zeros((ts, d), F32)
        for hh in range(n_half):
            cols = slice(hh * half, (hh + 1) * half)
            b, c, u, z = [proj_ref[:, (part * n_half + hh) * half:(part * n_half + hh + 1) * half].astype(F32)
                          for part in range(4)]
            cv = conv_ref[:, cols].astype(F32)
            dyh = dy[:, cols]
            sz, dsz = _silu(z)
            dconv = dyh * b * sz
            grads = [dyh * cv * sz, None, None, dyh * b * cv * dsz]
            next0, next1 = dnext_ref[0:1, cols], dnext_ref[1:2, cols]
            dc1 = jnp.where(row == ts - 1, next0, pltpu.roll(dconv, ts - 1, 0))
            dc2 = jnp.where(row == ts - 1, next1, jnp.where(row == ts - 2, next0, pltpu.roll(dconv, ts - 2, 0)))
            dnext_ref[:, cols] = dconv[0:8, :]
            v = c * u
            acc_ref[1:2, cols] += jnp.sum(dc2 * v, axis=0, keepdims=True)
            acc_ref[2:3, cols] += jnp.sum(dc1 * v, axis=0, keepdims=True)
            acc_ref[3:4, cols] += jnp.sum(dconv * v, axis=0, keepdims=True)
            dv = sm_ref[3:4, cols] * dconv + sm_ref[2:3, cols] * dc1 + sm_ref[1:2, cols] * dc2
            grads[1] = dv * u
            grads[2] = dv * c
            for part in range(4):
                j = part * n_half + hh
                gj = grads[part].astype(BF16)
                dproj_ref[:, j * half:(j + 1) * half] = gj
                dn1 = dn1 + _dot_nt(gj, win_ref[j])
        xn, r = _rms(x_ref[...])
        acc_ref[0:1, :] += jnp.sum(dn1 * xn, axis=0, keepdims=True)
        gx_ref[...] = dh1_ref[...] + _rms_bwd(dn1 * sm_ref[0:1, :], xn, r)

    rev = lambda w: pl.BlockSpec((ts, w), lambda i: (nt - 1 - i, 0))
    return pl.pallas_call(
        body,
        name="layer_a_bwd",
        grid=(nt,),
        in_specs=[rev(d), rev(4 * width), rev(width), rev(d), rev(d), _full(wout.shape), _full(win_g.shape), _full(sm.shape)],
        out_specs=[rev(4 * width), rev(d), _resident((8, d))],
        out_shape=[jax.ShapeDtypeStruct((seq, 4 * width), BF16), jax.ShapeDtypeStruct((seq, d), F32),
                   jax.ShapeDtypeStruct((8, d), F32)],
        scratch_shapes=[pltpu.VMEM((8, width), F32)],
        compiler_params=_params(("arbitrary",), 56),
    )(dya, proj, conv, dh1, x2, wout, win_g, sm)


def _wgrad(a, bs, n_slots, ts, name, ready=()):
    nr = len(ready)
    seq, k = a.shape
    nb_in = len(bs)
    n_each = bs[0].shape[1]
    n = nb_in * n_each
    bn = min(n_each, 1024)
    per_in = n_each // bn
    n_blocks = nb_in * per_in
    ns = seq // ts

    def b_spec(idx):
        def index(j, s):
            mine = j // per_in == idx
            row = jnp.where(mine, s, jnp.where(j // per_in > idx, ns - 1, 0))
            return (row, jnp.where(mine, j % per_in, jnp.where(j // per_in > idx, per_in - 1, 0)))
        return pl.BlockSpec((ts, bn), index)

    if n_slots:
        sw = n // n_slots
        spb = bn // sw
        out_shape = jax.ShapeDtypeStruct((n_slots, k, sw), BF16)
        out_spec = pl.BlockSpec((spb, k, sw), lambda j, s: (j, 0, 0))
    else:
        out_shape = jax.ShapeDtypeStruct((k, n), BF16)
        out_spec = pl.BlockSpec((k, bn), lambda j, s: (0, j))

    def body(a_ref, *refs):
        b_refs, ready_refs, o_ref = refs[:nb_in], refs[nb_in:nb_in + nr], refs[nb_in + nr]
        landed_refs, (acc_ref, *sems) = refs[nb_in + nr + 1:nb_in + 2 * nr + 1], refs[nb_in + 2 * nr + 1:]
        j, s = pl.program_id(0), pl.program_id(1)

        if nr:
            @pl.when(jnp.logical_and(j == 0, s == 0))
            def _():
                _exchange_start(ready_refs, landed_refs, *sems, True)

            @pl.when(jnp.logical_and(j == n_blocks - 1, s == ns - 1))
            def _():
                _exchange_wait(ready_refs, landed_refs, *sems, True)

        @pl.when(s == 0)
        def _():
            acc_ref[...] = jnp.zeros_like(acc_ref)

        for idx in range(nb_in):
            @pl.when(j // per_in == idx)
            def _(idx=idx):
                acc_ref[...] += _dot_tn(a_ref[...], b_refs[idx][...])

        @pl.when(s == ns - 1)
        def _():
            if n_slots:
                for e in range(spb):
                    o_ref[e] = acc_ref[:, e * sw:(e + 1) * sw].astype(BF16)
            else:
                o_ref[...] = acc_ref[...].astype(BF16)

    outs = pl.pallas_call(
        body,
        name=name,
        grid=(n_blocks, ns),
        in_specs=[pl.BlockSpec((ts, k), lambda j, s: (s, 0))] + [b_spec(idx) for idx in range(nb_in)] + [HBM_SPEC] * nr,
        out_specs=[out_spec] + [HBM_SPEC] * nr,
        out_shape=[out_shape] + [jax.ShapeDtypeStruct(g.shape, g.dtype) for g in ready],
        scratch_shapes=[pltpu.VMEM((k, bn), F32)] + (_exchange_sems(nr) if nr else []),
        compiler_params=_params(("arbitrary", "arbitrary"), 48),
    )(a, *bs, *ready)
    return (outs[0], outs[1:]) if nr else outs[0]


def _adamw(ws, gs, ms, vs):
    n = len(ws)

    def step(w, g, m, v):
        m = ADAM_B1 * m + (1.0 - ADAM_B1) * g
        v = ADAM_B2 * v + (1.0 - ADAM_B2) * jnp.square(g)
        m_hat = m / (1.0 - ADAM_B1 ** ADAM_STEP)
        v_hat = v / (1.0 - ADAM_B2 ** ADAM_STEP)
        return -ADAM_LR * (m_hat / (jnp.sqrt(v_hat) + ADAM_EPS) + ADAM_WD * w), m, v

    def body(*refs):
        w_refs, g_refs, m_refs, v_refs = (refs[k * n:(k + 1) * n] for k in range(4))
        d_refs, nm_refs, nv_refs = (refs[(4 + k) * n:(5 + k) * n] for k in range(3))
        for t in range(n):
            rows = w_refs[t].shape[0]
            if rows <= 128:
                d_refs[t][...], nm_refs[t][...], nv_refs[t][...] = step(
                    w_refs[t][...], g_refs[t][...], m_refs[t][...], v_refs[t][...])
                continue
            chunk = 128

            def one(i, carry, t=t):
                r = pl.ds(pl.multiple_of(i * chunk, chunk), chunk)
                d_refs[t][r, :], nm_refs[t][r, :], nv_refs[t][r, :] = step(
                    w_refs[t][r, :], g_refs[t][r, :], m_refs[t][r, :], v_refs[t][r, :])
                return carry

            lax.fori_loop(0, rows // chunk, one, 0)

    vmem = pl.BlockSpec(memory_space=pltpu.VMEM)
    outs = pl.pallas_call(
        body,
        name="adamw",
        in_specs=[vmem] * (4 * n),
        out_specs=[vmem] * (3 * n),
        out_shape=[jax.ShapeDtypeStruct(w.shape, F32) for w in ws] * 3,
        compiler_params=_params(vmem_mib=56),
    )(*ws, *gs, *ms, *vs)
    return outs[:n], outs[n:2 * n], outs[2 * n:]


def _band_structure():
    q_loc = jnp.arange(BLOCK, dtype=jnp.int32)[:, None]
    s_loc = jnp.arange(2 * BLOCK, dtype=jnp.int32)[None, :]
    dist = q_loc + BLOCK - s_loc
    in_window = (dist >= 0) & (dist < BLOCK)
    dd = jnp.maximum(dist, 0)
    max_exact = N_BUCKETS // 2
    large = max_exact + (jnp.log(jnp.maximum(dd, 1).astype(F32) / max_exact) / math.log(MAX_DISTANCE / max_exact)
                         * (N_BUCKETS - max_exact)).astype(jnp.int32)
    bucket = jnp.where(dd < max_exact, dd, jnp.minimum(large, N_BUCKETS - 1))
    return bucket, in_window.astype(jnp.int32)


def _place_rows(a, row, rows=8):
    return jnp.pad(a, ((row, rows - row - a.shape[0]), (0, 0)))


def kernel(x, a_pre_norm, a_w_in, a_conv_w, a_w_out, a_post_norm, kv_norm, w_kv, rel_bias, b_pre_norm, b_w_in, b_sinks, b_w_out, b_post_norm, loss_target, m_a_pre_norm, m_a_w_in, m_a_conv_w, m_a_w_out, m_a_post_norm, m_kv_norm, m_w_kv, m_rel_bias, m_b_pre_norm, m_b_w_in, m_b_sinks, m_b_w_out, m_b_post_norm, v_a_pre_norm, v_a_w_in, v_a_conv_w, v_a_w_out, v_a_post_norm, v_kv_norm, v_w_kv, v_rel_bias, v_b_pre_norm, v_b_w_in, v_b_sinks, v_b_w_out, v_b_post_norm):
    seq, d = x.shape[1], x.shape[2]
    x2 = x.reshape(seq, d)
    target = loss_target.reshape(seq, d)
    shard = a_pre_norm.shape[1]
    me = _my_index()
    ts_a = min(seq, 512)
    ts = min(seq, 512)
    ts_w = min(seq, 2048)

    small = _place_rows(a_pre_norm, 0) + _place_rows(a_conv_w[0], 1) + _place_rows(a_post_norm, 4)
    win_g, wout_g, small_g = _all_gather([a_w_in[0], a_w_out[0], small], [BF16, BF16, F32])
    wout = wout_g.reshape(-1, wout_g.shape[2])
    sm = small_g.transpose(1, 0, 2).reshape(8, N_DEV * shard)
    kvn = kv_norm.reshape(1, d)

    (h1, n1, proj, conv, y, ya), (wkv_g, wbin_g, wbout_g) = _layer_a_fwd(
        x2, sm, win_g, wout, [w_kv.astype(BF16), b_w_in[0].astype(BF16), b_w_out[0].astype(BF16)], ts_a)
    wkv = wkv_g.reshape(-1, wkv_g.shape[2])
    wbout = wbout_g.reshape(-1, wbout_g.shape[2])
    n3, n4, kv, q, z2 = _layer_b_in(h1, kvn, b_pre_norm, wkv, wbin_g, min(seq, 1024))
    bucket, in_window = _band_structure()
    biasm = _bias_table(rel_bias, bucket.T, in_window.T)
    attn, o = _attn_fwd(q, kv, z2, biasm, b_sinks)
    dh2, dyb, dattn, dz2, acc_c = _layer_b_out(o, attn, z2, h1, target, wbout, b_post_norm, ts)

    g_wbout = _wgrad(o, [dyb], 0, ts_w, "wgrad_b_out").reshape(wbout_g.shape)
    (dq, dkv, dssum, dsink), (l_wbout,) = _attn_bwd(q, kv, dattn, biasm, b_sinks, [g_wbout])
    by_head = dssum.reshape(N_PAIRS, BAND, 2, BLOCK).transpose(0, 2, 3, 1)
    relb = _relbias_grad(by_head.reshape(N_Q_HEADS, -1), bucket.reshape(1, -1), 4096)
    g_wkv = _wgrad(n3, [dkv], 0, ts_w, "wgrad_kv").reshape(wkv_g.shape)
    g_wbin = _wgrad(n4, [dq, dz2], N_DEV, ts_w, "wgrad_b_in")
    (dh1, dya, acc_b), (l_wkv, l_wbin) = _layer_b_in_bwd(
        dh2, dq, dz2, dkv, h1, ya, wbin_g, wkv, kvn, b_pre_norm, sm, [g_wkv, g_wbin], ts)
    dproj, gx, acc_a = _layer_a_bwd(dya, proj, conv, dh1, x2, wout, win_g, sm, ts_a)
    g_wout = _wgrad(y, [dya], 0, ts_w, "wgrad_a_out").reshape(wout_g.shape)
    g_win, (l_wout,) = _wgrad(n1, [dproj], N_DEV, ts_w, "wgrad_a_in", [g_wout])

    r_win, (r_wout, r_wkv, r_wbin, r_wbout), (s_a, s_b, s_c, s_relb, s_sink) = _reduce_exchange(
        g_win, [l_wout, l_wkv, l_wbin, l_wbout], [acc_a, acc_b, acc_c, relb, dsink])
    mine = lambda rows: lax.dynamic_slice_in_dim(rows, me * shard, shard, axis=1)
    loss = s_c[1, 0]
    weights = [a_pre_norm, a_w_in[0], a_conv_w[0], a_w_out[0], a_post_norm, kvn, w_kv, rel_bias, b_pre_norm,
               b_w_in[0], b_sinks, b_w_out[0], b_post_norm]
    grads = [mine(s_a[0:1]), r_win, mine(s_a[1:4]), r_wout, mine(s_b[2:3]), s_b[1:2], r_wkv,
             s_relb[:, :N_BUCKETS].T, s_b[0:1], r_wbin, s_sink[0:1, :N_Q_HEADS], r_wbout, s_c[0:1]]
    first = [m_a_pre_norm, m_a_w_in[0], m_a_conv_w[0], m_a_w_out[0], m_a_post_norm, m_kv_norm.reshape(1, d), m_w_kv,
             m_rel_bias, m_b_pre_norm, m_b_w_in[0], m_b_sinks, m_b_w_out[0], m_b_post_norm]
    second = [v_a_pre_norm, v_a_w_in[0], v_a_conv_w[0], v_a_w_out[0], v_a_post_norm, v_kv_norm.reshape(1, d), v_w_kv,
              v_rel_bias, v_b_pre_norm, v_b_w_in[0], v_b_sinks, v_b_w_out[0], v_b_post_norm]
    deltas, new_m, new_v = _adamw(weights, grads, first, second)

    shapes = [a_pre_norm.shape, a_w_in.shape, a_conv_w.shape, a_w_out.shape, a_post_norm.shape, kv_norm.shape,
              w_kv.shape, rel_bias.shape, b_pre_norm.shape, b_w_in.shape, b_sinks.shape, b_w_out.shape, b_post_norm.shape]
    shaped = lambda arrays: [a.reshape(s) for a, s in zip(arrays, shapes)]
    return (loss, gx.reshape(x.shape), *shaped(grads), *shaped(deltas), *shaped(new_m), *shaped(new_v))
```

```python
import functools
import math

import jax
import jax.numpy as jnp
from jax import lax
from jax.experimental import pallas as pl
from jax.experimental.pallas import tpu as pltpu

HEAD_DIM = 64
N_Q_HEADS = 16
N_KV_HEADS = 2
GROUP = N_Q_HEADS // N_KV_HEADS
BLOCK = 128
N_BUCKETS = 32
MAX_DISTANCE = 128
EPS = 1e-6
NEG_INF = -1e30
SCALE = HEAD_DIM ** -0.5

ADAM_LR = 0.001
ADAM_B1 = 0.9
ADAM_B2 = 0.999
ADAM_EPS = 1e-08
ADAM_WD = 0.01
ADAM_STEP = 10

N_DEV = 8
LANES = 128
F32 = jnp.float32
BF16 = jnp.bfloat16
MESH = pl.DeviceIdType.MESH
MIB = 1024 * 1024


def _params(semantics=None, vmem_mib=48):
    return pltpu.CompilerParams(dimension_semantics=semantics, vmem_limit_bytes=vmem_mib * MIB)


def _full(shape):
    zeros = (0,) * len(shape)
    return pl.BlockSpec(shape, lambda *_: zeros, pipeline_mode=pl.Buffered(1))


def _resident(shape):
    zeros = (0,) * len(shape)
    return pl.BlockSpec(shape, lambda *_: zeros)


def _rows(ts, cols):
    return pl.BlockSpec((ts, cols), lambda i: (i, 0))


def _dot(a, b):
    return jnp.dot(a, b, preferred_element_type=F32)


def _dot_nt(a, b):
    return lax.dot_general(a, b, (((1,), (1,)), ((), ())), preferred_element_type=F32)


def _dot_tn(a, b):
    return lax.dot_general(a, b, (((0,), (0,)), ((), ())), preferred_element_type=F32)


def _rms(xf):
    r = lax.rsqrt(jnp.mean(xf * xf, axis=-1, keepdims=True) + EPS)
    return xf * r, r


def _rms_bwd(dn, xn, r):
    return r * (dn - xn * jnp.mean(dn * xn, axis=-1, keepdims=True))


def _silu(z):
    s = jax.nn.sigmoid(z)
    return z * s, s * (1.0 + z * (1.0 - s))


def _my_index():
    return 4 * lax.axis_index("x") + 2 * lax.axis_index("y") + lax.axis_index("c")


def _all_gather(shards, out_dtypes):
    n = len(shards)

    def body(*refs):
        ins, outs = refs[:n], refs[n:2 * n]
        send_sems, recv_sems = refs[2 * n], refs[2 * n + 1]
        x, y, c = lax.axis_index("x"), lax.axis_index("y"), lax.axis_index("c")
        me, sibling = (x, y, c), (x, y, 1 - c)
        x_nbr, y_nbr, diagonal = (1 - x, y), (x, 1 - y), (1 - x, 1 - y)
        south = c == 0
        relayed = (jnp.where(south, 1 - x, x), jnp.where(south, y, 1 - y))
        relay_to = (jnp.where(south, x, 1 - x), jnp.where(south, 1 - y, y))

        def copy(t, k, block, to):
            rows = outs[t].at[4 * block[0] + 2 * block[1] + block[2]]
            return pltpu.make_async_remote_copy(
                src_ref=rows, dst_ref=rows, send_sem=send_sems.at[t, k], recv_sem=recv_sems.at[t, k],
                device_id=to, device_id_type=MESH)

        for t in range(n):
            outs[t][pl.ds(_my_index(), 1)] = ins[t][...].astype(outs[t].dtype)[None]
        started = []

        def start(cp):
            cp.start()
            started.append(cp)

        for t in range(n):
            start(copy(t, 0, me, sibling))
            start(copy(t, 1, me, (*x_nbr, c)))
            start(copy(t, 2, me, (*y_nbr, c)))
        for k, chip in ((1, x_nbr), (2, y_nbr)):
            for t in range(n):
                copy(t, k, (*chip, c), me).wait_recv()
                start(copy(t, 3 + k, (*chip, c), sibling))
        for t in range(n):
            start(copy(t, 3, (*relayed, c), (*relay_to, c)))
        for t in range(n):
            copy(t, 3, (*diagonal, c), me).wait_recv()
            start(copy(t, 6, (*diagonal, c), sibling))
        for t in range(n):
            copy(t, 0, sibling, me).wait_recv()
        for k, chip in ((4, x_nbr), (5, y_nbr), (6, diagonal)):
            for t in range(n):
                copy(t, k, (*chip, 1 - c), me).wait_recv()
        for cp in started:
            cp.wait_send()

    vmem = pl.BlockSpec(memory_space=pltpu.VMEM)
    return pl.pallas_call(
        body,
        name="gather_weights",
        out_shape=[jax.ShapeDtypeStruct((N_DEV,) + s.shape, dt) for s, dt in zip(shards, out_dtypes)],
        in_specs=[vmem] * n,
        out_specs=[vmem] * n,
        scratch_shapes=[pltpu.SemaphoreType.DMA((n, 7)), pltpu.SemaphoreType.DMA((n, 7))],
        compiler_params=_params(vmem_mib=48),
    )(*shards)


def _peer(k):
    x, y, c = lax.axis_index("x"), lax.axis_index("y"), lax.axis_index("c")
    px = 1 - x if k & 4 else x
    py = 1 - y if k & 2 else y
    pc = 1 - c if k & 1 else c
    return (px, py, pc), 4 * px + 2 * py + pc


def _exchange(srcs, dsts, send_sems, recv_sems, local_sems, scatter):
    me = _my_index()
    sends, arrivals = [], []
    for k in range(1, N_DEV):
        peer, pidx = _peer(k)
        for t, (src, dst) in enumerate(zip(srcs, dsts)):
            mine = src.at[pidx] if scatter else src
            sems = dict(send_sem=send_sems.at[t, k - 1], recv_sem=recv_sems.at[t, k - 1], device_id=peer, device_id_type=MESH)
            sends.append(pltpu.make_async_remote_copy(src_ref=mine, dst_ref=dst.at[me], **sems))
            arrivals.append(pltpu.make_async_remote_copy(src_ref=mine, dst_ref=dst.at[pidx], **sems))
    local = [pltpu.make_async_copy(src.at[me] if scatter else src, dst.at[me], local_sems.at[t])
             for t, (src, dst) in enumerate(zip(srcs, dsts))]
    return sends, arrivals, local


def _exchange_start(*args):
    sends, _, local = _exchange(*args)
    for cp in sends + local:
        cp.start()


def _exchange_wait(*args):
    sends, arrivals, local = _exchange(*args)
    for cp in arrivals:
        cp.wait_recv()
    for cp in sends:
        cp.wait_send()
    for cp in local:
        cp.wait()


def _exchange_sems(n):
    return [pltpu.SemaphoreType.DMA((n, N_DEV - 1)), pltpu.SemaphoreType.DMA((n, N_DEV - 1)), pltpu.SemaphoreType.DMA((n,))]


HBM_SPEC = pl.BlockSpec(memory_space=pl.ANY)


def _sum_slots(recv_ref, out_ref):
    rows = out_ref.shape[0]
    chunk = min(rows, 128)

    def add(i, carry):
        r0 = pl.multiple_of(i * chunk, chunk)
        acc = recv_ref[0, pl.ds(r0, chunk), :].astype(F32)
        for dev in range(1, N_DEV):
            acc = acc + recv_ref[dev, pl.ds(r0, chunk), :].astype(F32)
        out_ref[pl.ds(r0, chunk), :] = acc
        return carry

    lax.fori_loop(0, rows // chunk, add, 0)


N_CHIPS = N_DEV // 2


def _rows_loop(rows, fn):
    chunk = min(rows, 128)

    def step(i, carry):
        fn(pl.ds(pl.multiple_of(i * chunk, chunk), chunk))
        return carry

    lax.fori_loop(0, rows // chunk, step, 0)


def _chip_reduce(g_ref, out_ref, sib_ref, chip_ref, send_ref, sems, between):
    sib_send, sib_recv, chip_send, chip_recv = sems
    x, y, c = lax.axis_index("x"), lax.axis_index("y"), lax.axis_index("c")
    my_chip = 2 * x + y
    rows = out_ref.shape[0]

    def chip_of(k):
        cx = 1 - x if k & 2 else x
        cy = 1 - y if k & 1 else y
        return (cx, cy), 2 * cx + cy

    def to_sibling(t):
        return pltpu.make_async_remote_copy(
            src_ref=g_ref.at[2 * t + 1 - c], dst_ref=sib_ref.at[t], send_sem=sib_send.at[t], recv_sem=sib_recv.at[t],
            device_id=(x, y, 1 - c), device_id_type=MESH)

    def to_chip(k):
        (cx, cy), t = chip_of(k)
        return t, pltpu.make_async_remote_copy(
            src_ref=send_ref.at[k - 1], dst_ref=chip_ref.at[my_chip], send_sem=chip_send.at[k - 1],
            recv_sem=chip_recv.at[k - 1], device_id=(cx, cy, c), device_id_type=MESH)

    def from_chip(k):
        _, t = chip_of(k)
        return pltpu.make_async_remote_copy(
            src_ref=send_ref.at[k - 1], dst_ref=chip_ref.at[t], send_sem=chip_send.at[k - 1],
            recv_sem=chip_recv.at[k - 1], device_id=(x, y, c), device_id_type=MESH)

    for t in range(N_CHIPS):
        to_sibling(t).start()
    for t in range(N_CHIPS):
        to_sibling(t).wait_recv()

    def pair_sum(t, r):
        return g_ref[2 * t + c, r, :].astype(F32) + sib_ref[t, r, :].astype(F32)

    for k in (3, 1, 2):
        t, cp = to_chip(k)

        def fill(r, t=t, k=k):
            send_ref[k - 1, r, :] = pair_sum(t, r).astype(BF16)

        _rows_loop(rows, fill)
        cp.start()
    between()

    def own(r):
        chip_ref[my_chip, r, :] = pair_sum(my_chip, r).astype(BF16)

    _rows_loop(rows, own)
    for k in range(1, N_CHIPS):
        from_chip(k).wait_recv()

    def total(r):
        acc = chip_ref[0, r, :].astype(F32)
        for t in range(1, N_CHIPS):
            acc = acc + chip_ref[t, r, :].astype(F32)
        out_ref[r, :] = acc

    _rows_loop(rows, total)
    for t in range(N_CHIPS):
        to_sibling(t).wait_send()
    for k in range(1, N_CHIPS):
        to_chip(k)[1].wait_send()


def _reduce_exchange(part, landed, smalls):
    nl, ng = len(landed), len(smalls)
    n_out = 1 + nl + ng

    def body(*refs):
        p_in, l_in, s_in = refs[0], refs[1:1 + nl], refs[1 + nl:n_out]
        p_out, l_out, s_out = refs[n_out], refs[n_out + 1:n_out + 1 + nl], refs[n_out + 1 + nl:2 * n_out]
        scratch = refs[2 * n_out:]
        s_recv, (sib_ref, chip_ref, send_ref), sems = scratch[:ng], scratch[ng:ng + 3], scratch[ng + 3:]

        def between():
            for t in range(nl):
                _sum_slots(l_in[t], l_out[t])

        _exchange_start(s_in, s_recv, *sems[4:], False)
        _chip_reduce(p_in, p_out, sib_ref, chip_ref, send_ref, sems[:4], between)
        _exchange_wait(s_in, s_recv, *sems[4:], False)
        for t in range(ng):
            acc = s_recv[t][0]
            for dev in range(1, N_DEV):
                acc = acc + s_recv[t][dev]
            s_out[t][...] = acc

    vmem = pl.BlockSpec(memory_space=pltpu.VMEM)
    slot = part.shape[1:]
    outs = pl.pallas_call(
        body,
        name="reduce_grads",
        out_shape=[jax.ShapeDtypeStruct(p.shape[1:], F32) for p in [part] + landed]
        + [jax.ShapeDtypeStruct(s.shape, F32) for s in smalls],
        in_specs=[vmem] * n_out,
        out_specs=[vmem] * n_out,
        scratch_shapes=[pltpu.VMEM((N_DEV,) + s.shape, F32) for s in smalls]
        + [pltpu.VMEM((N_CHIPS,) + slot, BF16), pltpu.VMEM((N_CHIPS,) + slot, BF16), pltpu.VMEM((N_CHIPS - 1,) + slot, BF16)]
        + [pltpu.SemaphoreType.DMA((N_CHIPS,)), pltpu.SemaphoreType.DMA((N_CHIPS,)),
           pltpu.SemaphoreType.DMA((N_CHIPS - 1,)), pltpu.SemaphoreType.DMA((N_CHIPS - 1,))]
        + _exchange_sems(ng),
        compiler_params=_params(vmem_mib=56),
    )(part, *landed, *smalls)
    return outs[0], outs[1:1 + nl], outs[1 + nl:]


def _layer_a_fwd(x2, sm, win_g, wout, later, ts):
    seq, d = x2.shape
    width = wout.shape[0]
    half = win_g.shape[2]
    n_half = width // half
    nl = len(later)
    nt = seq // ts

    def body(x_ref, sm_ref, win_ref, wout_ref, *refs):
        shard_refs, refs = refs[:nl], refs[nl:]
        h1_ref, n1_ref, proj_ref, conv_ref, y_ref, ya_ref = refs[:6]
        gathered_refs, (vprev_ref, *sems) = refs[6:6 + nl], refs[6 + nl:]

        @pl.when(pl.program_id(0) == 0)
        def _():
            vprev_ref[...] = jnp.zeros_like(vprev_ref)
            _exchange_start(shard_refs, gathered_refs, *sems, False)

        @pl.when(pl.program_id(0) == nt - 1)
        def _():
            _exchange_wait(shard_refs, gathered_refs, *sems, False)

        xf = x_ref[...]
        xn, _ = _rms(xf)
        n1 = (xn * sm_ref[0:1, :]).astype(BF16)
        n1_ref[...] = n1
        row = lax.broadcasted_iota(jnp.int32, (ts, half), 0)
        ya = jnp.zeros((ts, d), F32)
        for hh in range(n_half):
            cols = slice(hh * half, (hh + 1) * half)
            parts = []
            for part in range(4):
                j = part * n_half + hh
                pj = _dot(n1, win_ref[j])
                proj_ref[:, j * half:(j + 1) * half] = pj.astype(BF16)
                parts.append(pj)
            b, c, u, z = parts
            v = c * u
            last1, last2 = vprev_ref[7:8, cols], vprev_ref[6:7, cols]
            v1 = jnp.where(row == 0, last1, pltpu.roll(v, 1, 0))
            v2 = jnp.where(row == 0, last2, jnp.where(row == 1, last1, pltpu.roll(v, 2, 0)))
            vprev_ref[:, cols] = v[ts - 8:ts, :]
            conv = sm_ref[1:2, cols] * v2 + sm_ref[2:3, cols] * v1 + sm_ref[3:4, cols] * v
            conv_ref[:, cols] = conv.astype(BF16)
            yh = (b * conv * _silu(z)[0]).astype(BF16)
            y_ref[:, cols] = yh
            ya = ya + _dot(yh, wout_ref[cols, :])
        ya_ref[...] = ya
        h1_ref[...] = xf + _rms(ya)[0] * sm_ref[4:5, :]

    outs = pl.pallas_call(
        body,
        name="layer_a_fwd",
        grid=(nt,),
        in_specs=[_rows(ts, d), _full(sm.shape), _full(win_g.shape), _full(wout.shape)] + [HBM_SPEC] * nl,
        out_specs=[_rows(ts, d), _rows(ts, d), _rows(ts, 4 * width), _rows(ts, width), _rows(ts, width), _rows(ts, d)]
        + [HBM_SPEC] * nl,
        out_shape=[
            jax.ShapeDtypeStruct((seq, d), F32),
            jax.ShapeDtypeStruct((seq, d), BF16),
            jax.ShapeDtypeStruct((seq, 4 * width), BF16),
            jax.ShapeDtypeStruct((seq, width), BF16),
            jax.ShapeDtypeStruct((seq, width), BF16),
            jax.ShapeDtypeStruct((seq, d), F32),
        ] + [jax.ShapeDtypeStruct((N_DEV,) + s.shape, s.dtype) for s in later],
        scratch_shapes=[pltpu.VMEM((8, width), F32)] + _exchange_sems(nl),
        compiler_params=_params(("arbitrary",), 56),
    )(x2, sm, win_g, wout, *later)
    return outs[:6], outs[6:]


def _layer_b_in(h1, kvn, bpre, wkv, wbin_g, ts):
    seq, d = h1.shape
    kvw = wkv.shape[1]
    cw = wbin_g.shape[2]
    aw = N_Q_HEADS * HEAD_DIM
    per = aw // cw

    def body(h1_ref, kvn_ref, bpre_ref, wkv_ref, wbin_ref, n3_ref, n4_ref, kv_ref, q_ref, z2_ref):
        hn, _ = _rms(h1_ref[...])
        n3 = (hn * kvn_ref[...]).astype(BF16)
        n4 = (hn * bpre_ref[...]).astype(BF16)
        n3_ref[...] = n3
        n4_ref[...] = n4
        kv_ref[...] = _dot(n3, wkv_ref[...]).astype(BF16)
        for j in range(N_DEV):
            pj = _dot(n4, wbin_ref[j])
            if j < per:
                q_ref[:, j * cw:(j + 1) * cw] = pj.astype(BF16)
            else:
                z2_ref[:, (j - per) * cw:(j - per + 1) * cw] = pj

    return pl.pallas_call(
        body,
        name="layer_b_in",
        grid=(seq // ts,),
        in_specs=[_rows(ts, d), _full(kvn.shape), _full(bpre.shape), _full(wkv.shape), _full(wbin_g.shape)],
        out_specs=[_rows(ts, d), _rows(ts, d), _rows(ts, kvw), _rows(ts, aw), _rows(ts, aw)],
        out_shape=[
            jax.ShapeDtypeStruct((seq, d), BF16),
            jax.ShapeDtypeStruct((seq, d), BF16),
            jax.ShapeDtypeStruct((seq, kvw), BF16),
            jax.ShapeDtypeStruct((seq, aw), BF16),
            jax.ShapeDtypeStruct((seq, aw), F32),
        ],
        compiler_params=_params(("parallel",), 48),
    )(h1, kvn, bpre, wkv, wbin_g)


N_PAIRS = N_Q_HEADS // 2
BAND = 2 * BLOCK


def _bias_table(rel_bias, bucket_t, in_window_t):
    def body(rb_ref, bucket_ref, win_ref, out_ref):
        bk = bucket_ref[...]
        inside = win_ref[...] != 0
        has_prev = lax.broadcasted_iota(jnp.int32, bk.shape, 0) >= BLOCK
        for h in range(N_Q_HEADS):
            acc = jnp.full(bk.shape, NEG_INF, F32)
            for b in range(N_BUCKETS):
                acc = jnp.where(jnp.logical_and(bk == b, inside), rb_ref[b, h], acc)
            cols = slice((h % 2) * BLOCK, (h % 2 + 1) * BLOCK)
            out_ref[1, h // 2, :, cols] = acc
            out_ref[0, h // 2, :, cols] = jnp.where(has_prev, acc, NEG_INF)

    vmem = pl.BlockSpec(memory_space=pltpu.VMEM)
    return pl.pallas_call(
        body,
        name="bias_table",
        in_specs=[pl.BlockSpec(memory_space=pltpu.SMEM), vmem, vmem],
        out_specs=vmem,
        out_shape=jax.ShapeDtypeStruct((2, N_PAIRS, BAND, 2 * BLOCK), F32),
    )(rel_bias, bucket_t, in_window_t)


Q_BLOCKS = 4


def _banded_tiles(kvp_ref, kvc_ref):
    tile = kvc_ref[...].astype(F32)
    blocks = [kvp_ref[...].astype(F32)] + [tile[u * BLOCK:(u + 1) * BLOCK] for u in range(Q_BLOCKS)]
    return [_banded_kv(blocks[u], blocks[u + 1]) for u in range(Q_BLOCKS)]


def _bias_of(bias_ref, i, u, m):
    return bias_ref[jnp.minimum(i, 1) if u == 0 else 1, m]


def _banded_kv(kvp, kvc):
    kw = N_KV_HEADS * HEAD_DIM
    out = []
    for full in (jnp.concatenate([kvp[:, :kw], kvc[:, :kw]], axis=0), jnp.concatenate([kvp[:, kw:], kvc[:, kw:]], axis=0)):
        lo = lax.broadcasted_iota(jnp.int32, full.shape, 1) < HEAD_DIM
        rolled = pltpu.roll(full, HEAD_DIM, 1)
        x2 = [jnp.where(lo, full, rolled).astype(BF16), jnp.where(lo, rolled, full).astype(BF16)]
        ft = full.T
        x2t = [jnp.concatenate([ft[kh * HEAD_DIM:(kh + 1) * HEAD_DIM]] * 2, axis=0).astype(BF16) for kh in range(N_KV_HEADS)]
        out += [x2, x2t]
    return out


def _pair_rows(ref, rows, m, scale=None):
    both = ref[rows, m * LANES:(m + 1) * LANES].astype(F32)
    if scale is not None:
        both = both * scale
    lo = lax.broadcasted_iota(jnp.int32, both.shape, 1) < HEAD_DIM
    zero = jnp.zeros_like(both)
    return jnp.concatenate([jnp.where(lo, both, zero), jnp.where(lo, zero, both)], axis=0).astype(BF16)


def _pair_cols(res_t):
    top = lax.broadcasted_iota(jnp.int32, (LANES, BLOCK), 0) < HEAD_DIM
    return jnp.where(top, res_t[:, :BLOCK], res_t[:, BLOCK:]).T


def _sink_row(sink_ref, m):
    first = lax.broadcasted_iota(jnp.int32, (1, 2 * BLOCK), 1) < BLOCK
    return jnp.where(first, sink_ref[0, 2 * m], sink_ref[0, 2 * m + 1])


def _probs_t(k2, qpair, bias, sink):
    return _softmax_t(_dot_nt(k2, qpair) + bias, sink)


def _softmax_t(logits, sink):
    mx = jnp.maximum(jnp.max(logits, axis=0, keepdims=True), sink)
    p = jnp.exp(logits - mx)
    sink_p = jnp.exp(sink - mx)
    inv = 1.0 / (jnp.sum(p, axis=0, keepdims=True) + sink_p)
    return p * inv, sink_p * inv


def _attn_fwd(q, kv, z2, biasm, sinks):
    seq, aw = q.shape
    kvw = kv.shape[1]
    nb = seq // BLOCK

    tile = Q_BLOCKS * BLOCK

    def body(sink_ref, q_ref, kvc_ref, kvp_ref, z2_ref, bias_ref, attn_ref, o_ref, acc_ref):
        i = pl.program_id(0)
        banded = _banded_tiles(kvp_ref, kvc_ref)
        units = [(u, m) for u in range(Q_BLOCKS) for m in range(N_PAIRS)]
        kv_of = lambda m: (2 * m) // GROUP
        logits, probs = {}, {}
        for step in range(len(units) + 2):
            if step < len(units):
                u, m = units[step]
                qpair = _pair_rows(q_ref, slice(u * BLOCK, (u + 1) * BLOCK), m, SCALE)
                logits[step] = _dot_nt(banded[u][0][kv_of(m)], qpair) + _bias_of(bias_ref, i, u, m)
            if 0 <= step - 1 < len(units):
                u, m = units[step - 1]
                probs[step - 1] = _softmax_t(logits.pop(step - 1), _sink_row(sink_ref, m))[0].astype(BF16)
            if 0 <= step - 2 < len(units):
                u, m = units[step - 2]
                out_t = _dot(banded[u][3][kv_of(m)], probs.pop(step - 2))
                acc_ref[u * BLOCK:(u + 1) * BLOCK, m * LANES:(m + 1) * LANES] = _pair_cols(out_t)
        attn = acc_ref[...]
        attn_ref[...] = attn.astype(BF16)
        o_ref[...] = (attn * _silu(z2_ref[...])[0]).astype(BF16)

    blk = lambda w: pl.BlockSpec((tile, w), lambda i: (i, 0))
    return pl.pallas_call(
        body,
        name="attn_fwd",
        grid=(seq // tile,),
        in_specs=[
            pl.BlockSpec(memory_space=pltpu.SMEM),
            blk(aw),
            blk(kvw),
            pl.BlockSpec((BLOCK, kvw), lambda i: (jnp.maximum(Q_BLOCKS * i - 1, 0), 0)),
            blk(aw),
            _full(biasm.shape),
        ],
        out_specs=[blk(aw), blk(aw)],
        out_shape=[jax.ShapeDtypeStruct((seq, aw), BF16), jax.ShapeDtypeStruct((seq, aw), BF16)],
        scratch_shapes=[pltpu.VMEM((tile, aw), F32)],
        compiler_params=_params(("arbitrary",), 40),
    )(sinks, q, kv, kv, z2, biasm)


def _layer_b_out(o, attn, z2, h1, target, wbout, bpost, ts):
    seq, d = h1.shape
    aw = o.shape[1]

    def body(o_ref, attn_ref, z2_ref, h1_ref, tgt_ref, w_ref, g_ref, dh2_ref, dyb_ref, dattn_ref, dz2_ref, acc_ref):
        @pl.when(pl.program_id(0) == 0)
        def _():
            acc_ref[...] = jnp.zeros_like(acc_ref)

        w = w_ref[...]
        yb = _dot(o_ref[...], w)
        ybn, r = _rms(yb)
        g = g_ref[...]
        diff = h1_ref[...] + ybn * g - tgt_ref[...]
        dh2 = diff * (1.0 / d)
        dh2_ref[...] = dh2
        acc_ref[0:1, :] += jnp.sum(dh2 * ybn, axis=0, keepdims=True)
        tok = jnp.mean(diff * diff, axis=-1, keepdims=True)
        acc_ref[1:2, :] += 0.5 * jnp.sum(tok, axis=0, keepdims=True)
        dyb = _rms_bwd(dh2 * g, ybn, r).astype(BF16)
        dyb_ref[...] = dyb
        do = _dot_nt(dyb, w)
        sz, dsz = _silu(z2_ref[...])
        dattn_ref[...] = (do * sz).astype(BF16)
        dz2_ref[...] = (do * attn_ref[...].astype(F32) * dsz).astype(BF16)

    return pl.pallas_call(
        body,
        name="layer_b_out",
        grid=(seq // ts,),
        in_specs=[_rows(ts, aw), _rows(ts, aw), _rows(ts, aw), _rows(ts, d), _rows(ts, d), _full(wbout.shape), _full(bpost.shape)],
        out_specs=[_rows(ts, d), _rows(ts, d), _rows(ts, aw), _rows(ts, aw), _resident((8, d))],
        out_shape=[
            jax.ShapeDtypeStruct((seq, d), F32),
            jax.ShapeDtypeStruct((seq, d), BF16),
            jax.ShapeDtypeStruct((seq, aw), BF16),
            jax.ShapeDtypeStruct((seq, aw), BF16),
            jax.ShapeDtypeStruct((8, d), F32),
        ],
        compiler_params=_params(("arbitrary",), 48),
    )(o, attn, z2, h1, target, wbout, bpost)


def _attn_bwd(q, kv, dattn, biasm, sinks, ready):
    seq, aw = q.shape
    kvw = kv.shape[1]
    kw = N_KV_HEADS * HEAD_DIM
    nb = seq // BLOCK
    pairs_per_kv = N_PAIRS // N_KV_HEADS
    nr = len(ready)

    tile = Q_BLOCKS * BLOCK
    nsteps = seq // tile
    held = (Q_BLOCKS - 1) * BLOCK

    def body(sink_ref, q_ref, kvc_ref, kvp_ref, da_ref, bias_ref, *refs):
        ready_refs, (dq_ref, dkv_ref, dssum_ref, dsink_ref) = refs[:nr], refs[nr:nr + 4]
        landed_refs, scratch = refs[nr + 4:2 * nr + 4], refs[2 * nr + 4:]
        carry_ref, done_ref, qs_ref, dos_ref, dst_ref, pt_ref, *sems = scratch
        i = pl.program_id(0)

        @pl.when(i == 0)
        def _():
            dssum_ref[...] = jnp.zeros_like(dssum_ref)
            dsink_ref[...] = jnp.zeros_like(dsink_ref)
            carry_ref[...] = jnp.zeros_like(carry_ref)
            done_ref[...] = jnp.zeros_like(done_ref)
            _exchange_start(ready_refs, landed_refs, *sems, True)

        @pl.when(i == nsteps)
        def _():
            _exchange_wait(ready_refs, landed_refs, *sems, True)

        @pl.when(i < nsteps)
        def _():
            lo = lax.broadcasted_iota(jnp.int32, (BAND, LANES), 1) < HEAD_DIM
            head_lane = lax.broadcasted_iota(jnp.int32, (1, LANES), 1)
            banded = _banded_tiles(kvp_ref, kvc_ref)
            units = [(u, m) for u in range(Q_BLOCKS) for m in range(N_PAIRS)]
            dsink = jnp.zeros((1, LANES), F32)
            folded = {}
            logits, dps, dsbs = {}, {}, {}
            for step in range(len(units) + 2):
                if step < len(units):
                    u, m = units[step]
                    kh, rows = m // pairs_per_kv, slice((m % pairs_per_kv) * BAND, (m % pairs_per_kv + 1) * BAND)
                    qrows = slice(u * BLOCK, (u + 1) * BLOCK)
                    qpair = _pair_rows(q_ref, qrows, m, SCALE)
                    dopair = _pair_rows(da_ref, qrows, m)
                    qs_ref[u, kh, rows, :] = qpair
                    dos_ref[u, kh, rows, :] = dopair
                    logits[step] = _dot_nt(banded[u][0][kh], qpair) + _bias_of(bias_ref, i, u, m)
                    dps[step] = _dot_nt(banded[u][2][kh], dopair)
                if 0 <= step - 1 < len(units):
                    u, m = units[step - 1]
                    kh, rows = m // pairs_per_kv, slice((m % pairs_per_kv) * BAND, (m % pairs_per_kv + 1) * BAND)
                    pn, sink_p = _softmax_t(logits.pop(step - 1), _sink_row(sink_ref, m))
                    dp = dps.pop(step - 1)
                    delta = jnp.sum(pn * dp, axis=0, keepdims=True)
                    ds = pn * (dp - delta)
                    dssum_ref[m] += ds
                    sink_term = sink_p * delta
                    for e in range(2):
                        total = jnp.sum(sink_term[:, e * BLOCK:(e + 1) * BLOCK], axis=1, keepdims=True)
                        dsink = dsink - jnp.where(head_lane == 2 * m + e, total, 0.0)
                    dsbs[step - 1] = ds.astype(BF16)
                    dst_ref[u, kh, :, rows] = dsbs[step - 1]
                    pt_ref[u, kh, :, rows] = pn.astype(BF16)
                if 0 <= step - 2 < len(units):
                    u, m = units[step - 2]
                    kh = m // pairs_per_kv
                    dq_t = _dot(banded[u][1][kh], dsbs.pop(step - 2))
                    dq_ref[u * BLOCK:(u + 1) * BLOCK, m * LANES:(m + 1) * LANES] = (_pair_cols(dq_t) * SCALE).astype(BF16)
                    if m % pairs_per_kv == pairs_per_kv - 1:
                        for name, lhs_ref, rhs_ref in (("k", dst_ref, qs_ref), ("v", pt_ref, dos_ref)):
                            acc = _dot(lhs_ref[u, kh], rhs_ref[u, kh])
                            folded[u, kh, name] = acc + pltpu.roll(acc, HEAD_DIM, 1)
            dsink_ref[0:1, :] += dsink
            dkv = [jnp.concatenate([jnp.where(lo, folded[u, 0, n], folded[u, 1, n]) for n in ("k", "v")], axis=1)
                   for u in range(Q_BLOCKS)]

            @pl.when(i > 0)
            def _():
                if held:
                    dkv_ref[:held, :] = done_ref[...].astype(BF16)
                dkv_ref[held:, :] = (carry_ref[...] + dkv[0][:BLOCK]).astype(BF16)

            for u in range(Q_BLOCKS - 1):
                done_ref[u * BLOCK:(u + 1) * BLOCK, :] = dkv[u][BLOCK:] + dkv[u + 1][:BLOCK]
            carry_ref[...] = dkv[Q_BLOCKS - 1][BLOCK:]

        @pl.when(i == nsteps)
        def _():
            if held:
                dkv_ref[:held, :] = done_ref[...].astype(BF16)
            dkv_ref[held:, :] = carry_ref[...].astype(BF16)

    last = nsteps - 1
    blk = lambda w: pl.BlockSpec((tile, w), lambda i: (jnp.minimum(i, last), 0))
    outs = pl.pallas_call(
        body,
        name="attn_bwd",
        grid=(nsteps + 1,),
        in_specs=[
            pl.BlockSpec(memory_space=pltpu.SMEM),
            blk(aw),
            blk(kvw),
            pl.BlockSpec((BLOCK, kvw), lambda i: (jnp.clip(Q_BLOCKS * i - 1, 0, nb - 1), 0)),
            blk(aw),
            _full(biasm.shape),
        ] + [HBM_SPEC] * nr,
        out_specs=[
            blk(aw),
            pl.BlockSpec((tile, kvw), lambda i: (jnp.maximum(i - 1, 0), 0)),
            _resident(biasm.shape[1:]),
            _resident((8, LANES)),
        ] + [HBM_SPEC] * nr,
        out_shape=[
            jax.ShapeDtypeStruct((seq, aw), BF16),
            jax.ShapeDtypeStruct((seq, kvw), BF16),
            jax.ShapeDtypeStruct(biasm.shape[1:], F32),
            jax.ShapeDtypeStruct((8, LANES), F32),
        ] + [jax.ShapeDtypeStruct(g.shape, g.dtype) for g in ready],
        scratch_shapes=[
            pltpu.VMEM((BLOCK, kvw), F32),
            pltpu.VMEM((max(held, 8), kvw), F32),
            pltpu.VMEM((Q_BLOCKS, N_KV_HEADS, pairs_per_kv * BAND, LANES), BF16),
            pltpu.VMEM((Q_BLOCKS, N_KV_HEADS, pairs_per_kv * BAND, LANES), BF16),
            pltpu.VMEM((Q_BLOCKS, N_KV_HEADS, BAND, pairs_per_kv * BAND), BF16),
            pltpu.VMEM((Q_BLOCKS, N_KV_HEADS, BAND, pairs_per_kv * BAND), BF16),
        ] + _exchange_sems(nr),
        compiler_params=_params(("arbitrary",), 48),
    )(sinks, q, kv, kv, dattn, biasm, *ready)
    return outs[:4], outs[4:]


def _relbias_grad(dssum2, bucket_row, chunk):
    heads, n = dssum2.shape

    def body(a_ref, bucket_ref, out_ref):
        @pl.when(pl.program_id(0) == 0)
        def _():
            out_ref[...] = jnp.zeros_like(out_ref)

        a = a_ref[...]
        hi = a.astype(BF16)
        lo = (a - hi.astype(F32)).astype(BF16)
        onehot_t = (lax.broadcasted_iota(jnp.int32, (LANES, chunk), 0) == bucket_ref[...]).astype(F32).astype(BF16)
        out_ref[...] += _dot_nt(hi, onehot_t) + _dot_nt(lo, onehot_t)

    return pl.pallas_call(
        body,
        name="relbias_grad",
        grid=(n // chunk,),
        in_specs=[pl.BlockSpec((heads, chunk), lambda i: (0, i)), pl.BlockSpec((1, chunk), lambda i: (0, i))],
        out_specs=_resident((heads, LANES)),
        out_shape=jax.ShapeDtypeStruct((heads, LANES), F32),
        compiler_params=_params(("arbitrary",), 32),
    )(dssum2, bucket_row)


def _layer_b_in_bwd(dh2, dq, dz2, dkv, h1, ya, wbin_g, wkv, kvn, bpre, sm, ready, ts):
    seq, d = h1.shape
    aw = dq.shape[1]
    kvw = dkv.shape[1]
    cw = wbin_g.shape[2]
    per = aw // cw

    nr = len(ready)
    nt = seq // ts

    def body(dh2_ref, dq_ref, dz2_ref, dkv_ref, h1_ref, ya_ref, wbin_ref, wkv_ref, kvn_ref, bpre_ref, sm_ref, *refs):
        ready_refs, (dh1_ref, dya_ref, acc_ref) = refs[:nr], refs[nr:nr + 3]
        landed_refs, sems = refs[nr + 3:2 * nr + 3], refs[2 * nr + 3:]

        @pl.when(pl.program_id(0) == 0)
        def _():
            acc_ref[...] = jnp.zeros_like(acc_ref)
            _exchange_start(ready_refs, landed_refs, *sems, True)

        @pl.when(pl.program_id(0) == nt - 1)
        def _():
            _exchange_wait(ready_refs, landed_refs, *sems, True)

        dn4 = jnp.zeros((ts, d), F32)
        for j in range(N_DEV):
            src = dq_ref if j < per else dz2_ref
            jj = j % per
            dn4 = dn4 + _dot_nt(src[:, jj * cw:(jj + 1) * cw], wbin_ref[j])
        dn3 = _dot_nt(dkv_ref[...], wkv_ref[...])
        hn, r = _rms(h1_ref[...])
        acc_ref[0:1, :] += jnp.sum(dn4 * hn, axis=0, keepdims=True)
        acc_ref[1:2, :] += jnp.sum(dn3 * hn, axis=0, keepdims=True)
        dh1 = dh2_ref[...] + _rms_bwd(dn4 * bpre_ref[...] + dn3 * kvn_ref[...], hn, r)
        dh1_ref[...] = dh1
        yan, r2 = _rms(ya_ref[...])
        acc_ref[2:3, :] += jnp.sum(dh1 * yan, axis=0, keepdims=True)
        dya_ref[...] = _rms_bwd(dh1 * sm_ref[4:5, :], yan, r2).astype(BF16)

    outs = pl.pallas_call(
        body,
        name="layer_b_in_bwd",
        grid=(nt,),
        in_specs=[_rows(ts, d), _rows(ts, aw), _rows(ts, aw), _rows(ts, kvw), _rows(ts, d), _rows(ts, d),
                  _full(wbin_g.shape), _full(wkv.shape), _full(kvn.shape), _full(bpre.shape), _full(sm.shape)]
        + [HBM_SPEC] * nr,
        out_specs=[_rows(ts, d), _rows(ts, d), _resident((8, d))] + [HBM_SPEC] * nr,
        out_shape=[jax.ShapeDtypeStruct((seq, d), F32), jax.ShapeDtypeStruct((seq, d), BF16),
                   jax.ShapeDtypeStruct((8, d), F32)] + [jax.ShapeDtypeStruct(g.shape, g.dtype) for g in ready],
        scratch_shapes=_exchange_sems(nr),
        compiler_params=_params(("arbitrary",), 48),
    )(dh2, dq, dz2, dkv, h1, ya, wbin_g, wkv, kvn, bpre, sm, *ready)
    return outs[:3], outs[3:]


def _layer_a_bwd(dya, proj, conv, dh1, x2, wout, win_g, sm, ts):
    seq, d = x2.shape
    width = wout.shape[0]
    half = win_g.shape[2]
    n_half = width // half
    nt = seq // ts

    def body(dya_ref, proj_ref, conv_ref, dh1_ref, x_ref, wout_ref, win_ref, sm_ref, dproj_ref, gx_ref, acc_ref,
             dnext_ref):
        @pl.when(pl.program_id(0) == 0)
        def _():
            acc_ref[...] = jnp.zeros_like(acc_ref)
            dnext_ref[...] = jnp.zeros_like(dnext_ref)

        dy = _dot_nt(dya_ref[...], wout_ref[...])
        row = lax.broadcasted_iota(jnp.int32, (ts, half), 0)
        dn1 = jnp.zeros((ts, d), F32)
        for hh in range(n_half):
            cols = slice(hh * half, (hh + 1) * half)
            b, c, u, z = [proj_ref[:, (part * n_half + hh) * half:(part * n_half + hh + 1) * half].astype(F32)
                          for part in range(4)]
            cv = conv_ref[:, cols].astype(F32)
            dyh = dy[:, cols]
            sz, dsz = _silu(z)
            dconv = dyh * b * sz
            grads = [dyh * cv * sz, None, None, dyh * b * cv * dsz]
            next0, next1 = dnext_ref[0:1, cols], dnext_ref[1:2, cols]
            dc1 = jnp.where(row == ts - 1, next0, pltpu.roll(dconv, ts - 1, 0))
            dc2 = jnp.where(row == ts - 1, next1, jnp.where(row == ts - 2, next0, pltpu.roll(dconv, ts - 2, 0)))
            dnext_ref[:, cols] = dconv[0:8, :]
            v = c * u
            acc_ref[1:2, cols] += jnp.sum(dc2 * v, axis=0, keepdims=True)
            acc_ref[2:3, cols] += jnp.sum(dc1 * v, axis=0, keepdims=True)
            acc_ref[3:4, cols] += jnp.sum(dconv * v, axis=0, keepdims=True)
            dv = sm_ref[3:4, cols] * dconv + sm_ref[2:3, cols] * dc1 + sm_ref[1:2, cols] * dc2
            grads[1] = dv * u
            grads[2] = dv * c
            for part in range(4):
                j = part * n_half + hh
                gj = grads[part].astype(BF16)
                dproj_ref[:, j * half:(j + 1) * half] = gj
                dn1 = dn1 + _dot_nt(gj, win_ref[j])
        xn, r = _rms(x_ref[...])
        acc_ref[0:1, :] += jnp.sum(dn1 * xn, axis=0, keepdims=True)
        gx_ref[...] = dh1_ref[...] + _rms_bwd(dn1 * sm_ref[0:1, :], xn, r)

    rev = lambda w: pl.BlockSpec((ts, w), lambda i: (nt - 1 - i, 0))
    return pl.pallas_call(
        body,
        name="layer_a_bwd",
        grid=(nt,),
        in_specs=[rev(d), rev(4 * width), rev(width), rev(d), rev(d), _full(wout.shape), _full(win_g.shape), _full(sm.shape)],
        out_specs=[rev(4 * width), rev(d), _resident((8, d))],
        out_shape=[jax.ShapeDtypeStruct((seq, 4 * width), BF16), jax.ShapeDtypeStruct((seq, d), F32),
                   jax.ShapeDtypeStruct((8, d), F32)],
        scratch_shapes=[pltpu.VMEM((8, width), F32)],
        compiler_params=_params(("arbitrary",), 56),
    )(dya, proj, conv, dh1, x2, wout, win_g, sm)


def _wgrad(a, bs, n_slots, ts, name, ready=()):
    nr = len(ready)
    seq, k = a.shape
    nb_in = len(bs)
    n_each = bs[0].shape[1]
    n = nb_in * n_each
    bn = min(n_each, 1024)
    per_in = n_each // bn
    n_blocks = nb_in * per_in
    ns = seq // ts

    def b_spec(idx):
        def index(j, s):
            mine = j // per_in == idx
            row = jnp.where(mine, s, jnp.where(j // per_in > idx, ns - 1, 0))
            return (row, jnp.where(mine, j % per_in, jnp.where(j // per_in > idx, per_in - 1, 0)))
        return pl.BlockSpec((ts, bn), index)

    if n_slots:
        sw = n // n_slots
        spb = bn // sw
        out_shape = jax.ShapeDtypeStruct((n_slots, k, sw), BF16)
        out_spec = pl.BlockSpec((spb, k, sw), lambda j, s: (j, 0, 0))
    else:
        out_shape = jax.ShapeDtypeStruct((k, n), BF16)
        out_spec = pl.BlockSpec((k, bn), lambda j, s: (0, j))

    def body(a_ref, *refs):
        b_refs, ready_refs, o_ref = refs[:nb_in], refs[nb_in:nb_in + nr], refs[nb_in + nr]
        landed_refs, (acc_ref, *sems) = refs[nb_in + nr + 1:nb_in + 2 * nr + 1], refs[nb_in + 2 * nr + 1:]
        j, s = pl.program_id(0), pl.program_id(1)

        if nr:
            @pl.when(jnp.logical_and(j == 0, s == 0))
            def _():
                _exchange_start(ready_refs, landed_refs, *sems, True)

            @pl.when(jnp.logical_and(j == n_blocks - 1, s == ns - 1))
            def _():
                _exchange_wait(ready_refs, landed_refs, *sems, True)

        @pl.when(s == 0)
        def _():
            acc_ref[...] = jnp.zeros_like(acc_ref)

        for idx in range(nb_in):
            @pl.when(j // per_in == idx)
            def _(idx=idx):
                acc_ref[...] += _dot_tn(a_ref[...], b_refs[idx][...])

        @pl.when(s == ns - 1)
        def _():
            if n_slots:
                for e in range(spb):
                    o_ref[e] = acc_ref[:, e * sw:(e + 1) * sw].astype(BF16)
            else:
                o_ref[...] = acc_ref[...].astype(BF16)

    outs = pl.pallas_call(
        body,
        name=name,
        grid=(n_blocks, ns),
        in_specs=[pl.BlockSpec((ts, k), lambda j, s: (s, 0))] + [b_spec(idx) for idx in range(nb_in)] + [HBM_SPEC] * nr,
        out_specs=[out_spec] + [HBM_SPEC] * nr,
        out_shape=[out_shape] + [jax.ShapeDtypeStruct(g.shape, g.dtype) for g in ready],
        scratch_shapes=[pltpu.VMEM((k, bn), F32)] + (_exchange_sems(nr) if nr else []),
        compiler_params=_params(("arbitrary", "arbitrary"), 48),
    )(a, *bs, *ready)
    return (outs[0], outs[1:]) if nr else outs[0]


def _adamw(ws, gs, ms, vs):
    n = len(ws)

    def step(w, g, m, v):
        m = ADAM_B1 * m + (1.0 - ADAM_B1) * g
        v = ADAM_B2 * v + (1.0 - ADAM_B2) * jnp.square(g)
        m_hat = m / (1.0 - ADAM_B1 ** ADAM_STEP)
        v_hat = v / (1.0 - ADAM_B2 ** ADAM_STEP)
        return -ADAM_LR * (m_hat / (jnp.sqrt(v_hat) + ADAM_EPS) + ADAM_WD * w), m, v

    def body(*refs):
        w_refs, g_refs, m_refs, v_refs = (refs[k * n:(k + 1) * n] for k in range(4))
        d_refs, nm_refs, nv_refs = (refs[(4 + k) * n:(5 + k) * n] for k in range(3))
        for t in range(n):
            rows = w_refs[t].shape[0]
            if rows <= 128:
                d_refs[t][...], nm_refs[t][...], nv_refs[t][...] = step(
                    w_refs[t][...], g_refs[t][...], m_refs[t][...], v_refs[t][...])
                continue
            chunk = 128

            def one(i, carry, t=t):
                r = pl.ds(pl.multiple_of(i * chunk, chunk), chunk)
                d_refs[t][r, :], nm_refs[t][r, :], nv_refs[t][r, :] = step(
                    w_refs[t][r, :], g_refs[t][r, :], m_refs[t][r, :], v_refs[t][r, :])
                return carry

            lax.fori_loop(0, rows // chunk, one, 0)

    vmem = pl.BlockSpec(memory_space=pltpu.VMEM)
    outs = pl.pallas_call(
        body,
        name="adamw",
        in_specs=[vmem] * (4 * n),
        out_specs=[vmem] * (3 * n),
        out_shape=[jax.ShapeDtypeStruct(w.shape, F32) for w in ws] * 3,
        compiler_params=_params(vmem_mib=56),
    )(*ws, *gs, *ms, *vs)
    return outs[:n], outs[n:2 * n], outs[2 * n:]


def _band_structure():
    q_loc = jnp.arange(BLOCK, dtype=jnp.int32)[:, None]
    s_loc = jnp.arange(2 * BLOCK, dtype=jnp.int32)[None, :]
    dist = q_loc + BLOCK - s_loc
    in_window = (dist >= 0) & (dist < BLOCK)
    dd = jnp.maximum(dist, 0)
    max_exact = N_BUCKETS // 2
    large = max_exact + (jnp.log(jnp.maximum(dd, 1).astype(F32) / max_exact) / math.log(MAX_DISTANCE / max_exact)
                         * (N_BUCKETS - max_exact)).astype(jnp.int32)
    bucket = jnp.where(dd < max_exact, dd, jnp.minimum(large, N_BUCKETS - 1))
    return bucket, in_window.astype(jnp.int32)


def _place_rows(a, row, rows=8):
    return jnp.pad(a, ((row, rows - row - a.shape[0]), (0, 0)))


def kernel(x, a_pre_norm, a_w_in, a_conv_w, a_w_out, a_post_norm, kv_norm, w_kv, rel_bias, b_pre_norm, b_w_in, b_sinks, b_w_out, b_post_norm, loss_target, m_a_pre_norm, m_a_w_in, m_a_conv_w, m_a_w_out, m_a_post_norm, m_kv_norm, m_w_kv, m_rel_bias, m_b_pre_norm, m_b_w_in, m_b_sinks, m_b_w_out, m_b_post_norm, v_a_pre_norm, v_a_w_in, v_a_conv_w, v_a_w_out, v_a_post_norm, v_kv_norm, v_w_kv, v_rel_bias, v_b_pre_norm, v_b_w_in, v_b_sinks, v_b_w_out, v_b_post_norm):
    seq, d = x.shape[1], x.shape[2]
    x2 = x.reshape(seq, d)
    target = loss_target.reshape(seq, d)
    shard = a_pre_norm.shape[1]
    me = _my_index()
    ts_a = min(seq, 512)
    ts = min(seq, 512)
    ts_w = min(seq, 2048)

    small = _place_rows(a_pre_norm, 0) + _place_rows(a_conv_w[0], 1) + _place_rows(a_post_norm, 4)
    win_g, wout_g, small_g = _all_gather([a_w_in[0], a_w_out[0], small], [BF16, BF16, F32])
    wout = wout_g.reshape(-1, wout_g.shape[2])
    sm = small_g.transpose(1, 0, 2).reshape(8, N_DEV * shard)
    kvn = kv_norm.reshape(1, d)

    (h1, n1, proj, conv, y, ya), (wkv_g, wbin_g, wbout_g) = _layer_a_fwd(
        x2, sm, win_g, wout, [w_kv.astype(BF16), b_w_in[0].astype(BF16), b_w_out[0].astype(BF16)], ts_a)
    wkv = wkv_g.reshape(-1, wkv_g.shape[2])
    wbout = wbout_g.reshape(-1, wbout_g.shape[2])
    n3, n4, kv, q, z2 = _layer_b_in(h1, kvn, b_pre_norm, wkv, wbin_g, min(seq, 1024))
    bucket, in_window = _band_structure()
    biasm = _bias_table(rel_bias, bucket.T, in_window.T)
    attn, o = _attn_fwd(q, kv, z2, biasm, b_sinks)
    dh2, dyb, dattn, dz2, acc_c = _layer_b_out(o, attn, z2, h1, target, wbout, b_post_norm, ts)

    g_wbout = _wgrad(o, [dyb], 0, ts_w, "wgrad_b_out").reshape(wbout_g.shape)
    (dq, dkv, dssum, dsink), (l_wbout,) = _attn_bwd(q, kv, dattn, biasm, b_sinks, [g_wbout])
    by_head = dssum.reshape(N_PAIRS, BAND, 2, BLOCK).transpose(0, 2, 3, 1)
    relb = _relbias_grad(by_head.reshape(N_Q_HEADS, -1), bucket.reshape(1, -1), 4096)
    g_wkv = _wgrad(n3, [dkv], 0, ts_w, "wgrad_kv").reshape(wkv_g.shape)
    g_wbin = _wgrad(n4, [dq, dz2], N_DEV, ts_w, "wgrad_b_in")
    (dh1, dya, acc_b), (l_wkv, l_wbin) = _layer_b_in_bwd(
        dh2, dq, dz2, dkv, h1, ya, wbin_g, wkv, kvn, b_pre_norm, sm, [g_wkv, g_wbin], ts)
    dproj, gx, acc_a = _layer_a_bwd(dya, proj, conv, dh1, x2, wout, win_g, sm, ts_a)
    g_wout = _wgrad(y, [dya], 0, ts_w, "wgrad_a_out").reshape(wout_g.shape)
    g_win, (l_wout,) = _wgrad(n1, [dproj], N_DEV, ts_w, "wgrad_a_in", [g_wout])

    r_win, (r_wout, r_wkv, r_wbin, r_wbout), (s_a, s_b, s_c, s_relb, s_sink) = _reduce_exchange(
        g_win, [l_wout, l_wkv, l_wbin, l_wbout], [acc_a, acc_b, acc_c, relb, dsink])
    mine = lambda rows: lax.dynamic_slice_in_dim(rows, me * shard, shard, axis=1)
    loss = s_c[1, 0]
    weights = [a_pre_norm, a_w_in[0], a_conv_w[0], a_w_out[0], a_post_norm, kvn, w_kv, rel_bias, b_pre_norm,
               b_w_in[0], b_sinks, b_w_out[0], b_post_norm]
    grads = [mine(s_a[0:1]), r_win, mine(s_a[1:4]), r_wout, mine(s_b[2:3]), s_b[1:2], r_wkv,
             s_relb[:, :N_BUCKETS].T, s_b[0:1], r_wbin, s_sink[0:1, :N_Q_HEADS], r_wbout, s_c[0:1]]
    first = [m_a_pre_norm, m_a_w_in[0], m_a_conv_w[0], m_a_w_out[0], m_a_post_norm, m_kv_norm.reshape(1, d), m_w_kv,
             m_rel_bias, m_b_pre_norm, m_b_w_in[0], m_b_sinks, m_b_w_out[0], m_b_post_norm]
    second = [v_a_pre_norm, v_a_w_in[0], v_a_conv_w[0], v_a_w_out[0], v_a_post_norm, v_kv_norm.reshape(1, d), v_w_kv,
              v_rel_bias, v_b_pre_norm, v_b_w_in[0], v_b_sinks, v_b_w_out[0], v_b_post_norm]
    deltas, new_m, new_v = _adamw(weights, grads, first, second)

    shapes = [a_pre_norm.shape, a_w_in.shape, a_conv_w.shape, a_w_out.shape, a_post_norm.shape, kv_norm.shape,
              w_kv.shape, rel_bias.shape, b_pre_norm.shape, b_w_in.shape, b_sinks.shape, b_w_out.shape, b_post_norm.shape]
    shaped = lambda arrays: [a.reshape(s) for a, s in zip(arrays, shapes)]
    return (loss, gx.reshape(x.shape), *shaped(grads), *shaped(deltas), *shaped(new_m), *shaped(new_v))
```

```python
import functools
import math

import jax
import jax.numpy as jnp
from jax import lax
from jax.experimental import pallas as pl
from jax.experimental.pallas import tpu as pltpu

HEAD_DIM = 64
N_Q_HEADS = 16
N_KV_HEADS = 2
GROUP = N_Q_HEADS // N_KV_HEADS
BLOCK = 128
N_BUCKETS = 32
MAX_DISTANCE = 128
EPS = 1e-6
NEG_INF = -1e30
SCALE = HEAD_DIM ** -0.5

ADAM_LR = 0.001
ADAM_B1 = 0.9
ADAM_B2 = 0.999
ADAM_EPS = 1e-08
ADAM_WD = 0.01
ADAM_STEP = 10

N_DEV = 8
LANES = 128
F32 = jnp.float32
BF16 = jnp.bfloat16
MESH = pl.DeviceIdType.MESH
MIB = 1024 * 1024


def _params(semantics=None, vmem_mib=48):
    return pltpu.CompilerParams(dimension_semantics=semantics, vmem_limit_bytes=vmem_mib * MIB)


def _full(shape):
    zeros = (0,) * len(shape)
    return pl.BlockSpec(shape, lambda *_: zeros, pipeline_mode=pl.Buffered(1))


def _resident(shape):
    zeros = (0,) * len(shape)
    return pl.BlockSpec(shape, lambda *_: zeros)


def _rows(ts, cols):
    return pl.BlockSpec((ts, cols), lambda i: (i, 0))


def _dot(a, b):
    return jnp.dot(a, b, preferred_element_type=F32)


def _dot_nt(a, b):
    return lax.dot_general(a, b, (((1,), (1,)), ((), ())), preferred_element_type=F32)


def _dot_tn(a, b):
    return lax.dot_general(a, b, (((0,), (0,)), ((), ())), preferred_element_type=F32)


def _rms(xf):
    r = lax.rsqrt(jnp.mean(xf * xf, axis=-1, keepdims=True) + EPS)
    return xf * r, r


def _rms_bwd(dn, xn, r):
    return r * (dn - xn * jnp.mean(dn * xn, axis=-1, keepdims=True))


def _silu(z):
    s = jax.nn.sigmoid(z)
    return z * s, s * (1.0 + z * (1.0 - s))


def _my_index():
    return 4 * lax.axis_index("x") + 2 * lax.axis_index("y") + lax.axis_index("c")


def _all_gather(shards, out_dtypes):
    n = len(shards)

    def body(*refs):
        ins, outs = refs[:n], refs[n:2 * n]
        send_sems, recv_sems = refs[2 * n], refs[2 * n + 1]
        x, y, c = lax.axis_index("x"), lax.axis_index("y"), lax.axis_index("c")
        me, sibling = (x, y, c), (x, y, 1 - c)
        x_nbr, y_nbr, diagonal = (1 - x, y), (x, 1 - y), (1 - x, 1 - y)
        south = c == 0
        relayed = (jnp.where(south, 1 - x, x), jnp.where(south, y, 1 - y))
        relay_to = (jnp.where(south, x, 1 - x), jnp.where(south, 1 - y, y))

        def copy(t, k, block, to):
            rows = outs[t].at[4 * block[0] + 2 * block[1] + block[2]]
            return pltpu.make_async_remote_copy(
                src_ref=rows, dst_ref=rows, send_sem=send_sems.at[t, k], recv_sem=recv_sems.at[t, k],
                device_id=to, device_id_type=MESH)

        for t in range(n):
            outs[t][pl.ds(_my_index(), 1)] = ins[t][...].astype(outs[t].dtype)[None]
        started = []

        def start(cp):
            cp.start()
            started.append(cp)

        for t in range(n):
            start(copy(t, 0, me, sibling))
            start(copy(t, 1, me, (*x_nbr, c)))
            start(copy(t, 2, me, (*y_nbr, c)))
        for k, chip in ((1, x_nbr), (2, y_nbr)):
            for t in range(n):
                copy(t, k, (*chip, c), me).wait_recv()
                start(copy(t, 3 + k, (*chip, c), sibling))
        for t in range(n):
            start(copy(t, 3, (*relayed, c), (*relay_to, c)))
        for t in range(n):
            copy(t, 3, (*diagonal, c), me).wait_recv()
            start(copy(t, 6, (*diagonal, c), sibling))
        for t in range(n):
            copy(t, 0, sibling, me).wait_recv()
        for k, chip in ((4, x_nbr), (5, y_nbr), (6, diagonal)):
            for t in range(n):
                copy(t, k, (*chip, 1 - c), me).wait_recv()
        for cp in started:
            cp.wait_send()

    vmem = pl.BlockSpec(memory_space=pltpu.VMEM)
    return pl.pallas_call(
        body,
        name="gather_weights",
        out_shape=[jax.ShapeDtypeStruct((N_DEV,) + s.shape, dt) for s, dt in zip(shards, out_dtypes)],
        in_specs=[vmem] * n,
        out_specs=[vmem] * n,
        scratch_shapes=[pltpu.SemaphoreType.DMA((n, 7)), pltpu.SemaphoreType.DMA((n, 7))],
        compiler_params=_params(vmem_mib=48),
    )(*shards)


def _peer(k):
    x, y, c = lax.axis_index("x"), lax.axis_index("y"), lax.axis_index("c")
    px = 1 - x if k & 4 else x
    py = 1 - y if k & 2 else y
    pc = 1 - c if k & 1 else c
    return (px, py, pc), 4 * px + 2 * py + pc


def _exchange(srcs, dsts, send_sems, recv_sems, local_sems, scatter):
    me = _my_index()
    sends, arrivals = [], []
    for k in range(1, N_DEV):
        peer, pidx = _peer(k)
        for t, (src, dst) in enumerate(zip(srcs, dsts)):
            mine = src.at[pidx] if scatter else src
            sems = dict(send_sem=send_sems.at[t, k - 1], recv_sem=recv_sems.at[t, k - 1], device_id=peer, device_id_type=MESH)
            sends.append(pltpu.make_async_remote_copy(src_ref=mine, dst_ref=dst.at[me], **sems))
            arrivals.append(pltpu.make_async_remote_copy(src_ref=mine, dst_ref=dst.at[pidx], **sems))
    local = [pltpu.make_async_copy(src.at[me] if scatter else src, dst.at[me], local_sems.at[t])
             for t, (src, dst) in enumerate(zip(srcs, dsts))]
    return sends, arrivals, local


def _exchange_start(*args):
    sends, _, local = _exchange(*args)
    for cp in sends + local:
        cp.start()


def _exchange_wait(*args):
    sends, arrivals, local = _exchange(*args)
    for cp in arrivals:
        cp.wait_recv()
    for cp in sends:
        cp.wait_send()
    for cp in local:
        cp.wait()


def _exchange_sems(n):
    return [pltpu.SemaphoreType.DMA((n, N_DEV - 1)), pltpu.SemaphoreType.DMA((n, N_DEV - 1)), pltpu.SemaphoreType.DMA((n,))]


HBM_SPEC = pl.BlockSpec(memory_space=pl.ANY)


def _sum_slots(recv_ref, out_ref):
    rows = out_ref.shape[0]
    chunk = min(rows, 128)

    def add(i, carry):
        r0 = pl.multiple_of(i * chunk, chunk)
        acc = recv_ref[0, pl.ds(r0, chunk), :].astype(F32)
        for dev in range(1, N_DEV):
            acc = acc + recv_ref[dev, pl.ds(r0, chunk), :].astype(F32)
        out_ref[pl.ds(r0, chunk), :] = acc
        return carry

    lax.fori_loop(0, rows // chunk, add, 0)


N_CHIPS = N_DEV // 2


def _rows_loop(rows, fn):
    chunk = min(rows, 128)

    def step(i, carry):
        fn(pl.ds(pl.multiple_of(i * chunk, chunk), chunk))
        return carry

    lax.fori_loop(0, rows // chunk, step, 0)


def _chip_reduce(g_ref, out_ref, sib_ref, chip_ref, send_ref, sems, between):
    sib_send, sib_recv, chip_send, chip_recv = sems
    x, y, c = lax.axis_index("x"), lax.axis_index("y"), lax.axis_index("c")
    my_chip = 2 * x + y
    rows = out_ref.shape[0]

    def chip_of(k):
        cx = 1 - x if k & 2 else x
        cy = 1 - y if k & 1 else y
        return (cx, cy), 2 * cx + cy

    def to_sibling(t):
        return pltpu.make_async_remote_copy(
            src_ref=g_ref.at[2 * t + 1 - c], dst_ref=sib_ref.at[t], send_sem=sib_send.at[t], recv_sem=sib_recv.at[t],
            device_id=(x, y, 1 - c), device_id_type=MESH)

    def to_chip(k):
        (cx, cy), t = chip_of(k)
        return t, pltpu.make_async_remote_copy(
            src_ref=send_ref.at[k - 1], dst_ref=chip_ref.at[my_chip], send_sem=chip_send.at[k - 1],
            recv_sem=chip_recv.at[k - 1], device_id=(cx, cy, c), device_id_type=MESH)

    def from_chip(k):
        _, t = chip_of(k)
        return pltpu.make_async_remote_copy(
            src_ref=send_ref.at[k - 1], dst_ref=chip_ref.at[t], send_sem=chip_send.at[k - 1],
            recv_sem=chip_recv.at[k - 1], device_id=(x, y, c), device_id_type=MESH)

    for t in range(N_CHIPS):
        to_sibling(t).start()
    for t in range(N_CHIPS):
        to_sibling(t).wait_recv()

    def pair_sum(t, r):
        return g_ref[2 * t + c, r, :].astype(F32) + sib_ref[t, r, :].astype(F32)

    for k in (3, 1, 2):
        t, cp = to_chip(k)

        def fill(r, t=t, k=k):
            send_ref[k - 1, r, :] = pair_sum(t, r).astype(BF16)

        _rows_loop(rows, fill)
        cp.start()
    between()

    def own(r):
        chip_ref[my_chip, r, :] = pair_sum(my_chip, r).astype(BF16)

    _rows_loop(rows, own)
    for k in range(1, N_CHIPS):
        from_chip(k).wait_recv()

    def total(r):
        acc = chip_ref[0, r, :].astype(F32)
        for t in range(1, N_CHIPS):
            acc = acc + chip_ref[t, r, :].astype(F32)
        out_ref[r, :] = acc

    _rows_loop(rows, total)
    for t in range(N_CHIPS):
        to_sibling(t).wait_send()
    for k in range(1, N_CHIPS):
        to_chip(k)[1].wait_send()


def _reduce_exchange(part, landed, smalls):
    nl, ng = len(landed), len(smalls)
    n_out = 1 + nl + ng

    def body(*refs):
        p_in, l_in, s_in = refs[0], refs[1:1 + nl], refs[1 + nl:n_out]
        p_out, l_out, s_out = refs[n_out], refs[n_out + 1:n_out + 1 + nl], refs[n_out + 1 + nl:2 * n_out]
        scratch = refs[2 * n_out:]
        s_recv, (sib_ref, chip_ref, send_ref), sems = scratch[:ng], scratch[ng:ng + 3], scratch[ng + 3:]

        def between():
            for t in range(nl):
                _sum_slots(l_in[t], l_out[t])

        _exchange_start(s_in, s_recv, *sems[4:], False)
        _chip_reduce(p_in, p_out, sib_ref, chip_ref, send_ref, sems[:4], between)
        _exchange_wait(s_in, s_recv, *sems[4:], False)
        for t in range(ng):
            acc = s_recv[t][0]
            for dev in range(1, N_DEV):
                acc = acc + s_recv[t][dev]
            s_out[t][...] = acc

    vmem = pl.BlockSpec(memory_space=pltpu.VMEM)
    slot = part.shape[1:]
    outs = pl.pallas_call(
        body,
        name="reduce_grads",
        out_shape=[jax.ShapeDtypeStruct(p.shape[1:], F32) for p in [part] + landed]
        + [jax.ShapeDtypeStruct(s.shape, F32) for s in smalls],
        in_specs=[vmem] * n_out,
        out_specs=[vmem] * n_out,
        scratch_shapes=[pltpu.VMEM((N_DEV,) + s.shape, F32) for s in smalls]
        + [pltpu.VMEM((N_CHIPS,) + slot, BF16), pltpu.VMEM((N_CHIPS,) + slot, BF16), pltpu.VMEM((N_CHIPS - 1,) + slot, BF16)]
        + [pltpu.SemaphoreType.DMA((N_CHIPS,)), pltpu.SemaphoreType.DMA((N_CHIPS,)),
           pltpu.SemaphoreType.DMA((N_CHIPS - 1,)), pltpu.SemaphoreType.DMA((N_CHIPS - 1,))]
        + _exchange_sems(ng),
        compiler_params=_params(vmem_mib=56),
    )(part, *landed, *smalls)
    return outs[0], outs[1:1 + nl], outs[1 + nl:]


def _layer_a_fwd(x2, sm, win_g, wout, later, ts):
    seq, d = x2.shape
    width = wout.shape[0]
    half = win_g.shape[2]
    n_half = width // half
    nl = len(later)
    nt = seq // ts

    def body(x_ref, sm_ref, win_ref, wout_ref, *refs):
        shard_refs, refs = refs[:nl], refs[nl:]
        h1_ref, n1_ref, proj_ref, conv_ref, y_ref, ya_ref = refs[:6]
        gathered_refs, (vprev_ref, *sems) = refs[6:6 + nl], refs[6 + nl:]

        @pl.when(pl.program_id(0) == 0)
        def _():
            vprev_ref[...] = jnp.zeros_like(vprev_ref)
            _exchange_start(shard_refs, gathered_refs, *sems, False)

        @pl.when(pl.program_id(0) == nt - 1)
        def _():
            _exchange_wait(shard_refs, gathered_refs, *sems, False)

        xf = x_ref[...]
        xn, _ = _rms(xf)
        n1 = (xn * sm_ref[0:1, :]).astype(BF16)
        n1_ref[...] = n1
        row = lax.broadcasted_iota(jnp.int32, (ts, half), 0)
        ya = jnp.zeros((ts, d), F32)
        for hh in range(n_half):
            cols = slice(hh * half, (hh + 1) * half)
            parts = []
            for part in range(4):
                j = part * n_half + hh
                pj = _dot(n1, win_ref[j])
                proj_ref[:, j * half:(j + 1) * half] = pj.astype(BF16)
                parts.append(pj)
            b, c, u, z = parts
            v = c * u
            last1, last2 = vprev_ref[7:8, cols], vprev_ref[6:7, cols]
            v1 = jnp.where(row == 0, last1, pltpu.roll(v, 1, 0))
            v2 = jnp.where(row == 0, last2, jnp.where(row == 1, last1, pltpu.roll(v, 2, 0)))
            vprev_ref[:, cols] = v[ts - 8:ts, :]
            conv = sm_ref[1:2, cols] * v2 + sm_ref[2:3, cols] * v1 + sm_ref[3:4, cols] * v
            conv_ref[:, cols] = conv.astype(BF16)
            yh = (b * conv * _silu(z)[0]).astype(BF16)
            y_ref[:, cols] = yh
            ya = ya + _dot(yh, wout_ref[cols, :])
        ya_ref[...] = ya
        h1_ref[...] = xf + _rms(ya)[0] * sm_ref[4:5, :]

    outs = pl.pallas_call(
        body,
        name="layer_a_fwd",
        grid=(nt,),
        in_specs=[_rows(ts, d), _full(sm.shape), _full(win_g.shape), _full(wout.shape)] + [HBM_SPEC] * nl,
        out_specs=[_rows(ts, d), _rows(ts, d), _rows(ts, 4 * width), _rows(ts, width), _rows(ts, width), _rows(ts, d)]
        + [HBM_SPEC] * nl,
        out_shape=[
            jax.ShapeDtypeStruct((seq, d), F32),
            jax.ShapeDtypeStruct((seq, d), BF16),
            jax.ShapeDtypeStruct((seq, 4 * width), BF16),
            jax.ShapeDtypeStruct((seq, width), BF16),
            jax.ShapeDtypeStruct((seq, width), BF16),
            jax.ShapeDtypeStruct((seq, d), F32),
        ] + [jax.ShapeDtypeStruct((N_DEV,) + s.shape, s.dtype) for s in later],
        scratch_shapes=[pltpu.VMEM((8, width), F32)] + _exchange_sems(nl),
        compiler_params=_params(("arbitrary",), 56),
    )(x2, sm, win_g, wout, *later)
    return outs[:6], outs[6:]


def _layer_b_in(h1, kvn, bpre, wkv, wbin_g, ts):
    seq, d = h1.shape
    kvw = wkv.shape[1]
    cw = wbin_g.shape[2]
    aw = N_Q_HEADS * HEAD_DIM
    per = aw // cw

    def body(h1_ref, kvn_ref, bpre_ref, wkv_ref, wbin_ref, n3_ref, n4_ref, kv_ref, q_ref, z2_ref):
        hn, _ = _rms(h1_ref[...])
        n3 = (hn * kvn_ref[...]).astype(BF16)
        n4 = (hn * bpre_ref[...]).astype(BF16)
        n3_ref[...] = n3
        n4_ref[...] = n4
        kv_ref[...] = _dot(n3, wkv_ref[...]).astype(BF16)
        for j in range(N_DEV):
            pj = _dot(n4, wbin_ref[j])
            if j < per:
                q_ref[:, j * cw:(j + 1) * cw] = pj.astype(BF16)
            else:
                z2_ref[:, (j - per) * cw:(j - per + 1) * cw] = pj

    return pl.pallas_call(
        body,
        name="layer_b_in",
        grid=(seq // ts,),
        in_specs=[_rows(ts, d), _full(kvn.shape), _full(bpre.shape), _full(wkv.shape), _full(wbin_g.shape)],
        out_specs=[_rows(ts, d), _rows(ts, d), _rows(ts, kvw), _rows(ts, aw), _rows(ts, aw)],
        out_shape=[
            jax.ShapeDtypeStruct((seq, d), BF16),
            jax.ShapeDtypeStruct((seq, d), BF16),
            jax.ShapeDtypeStruct((seq, kvw), BF16),
            jax.ShapeDtypeStruct((seq, aw), BF16),
            jax.ShapeDtypeStruct((seq, aw), F32),
        ],
        compiler_params=_params(("parallel",), 48),
    )(h1, kvn, bpre, wkv, wbin_g)


N_PAIRS = N_Q_HEADS // 2
BAND = 2 * BLOCK


def _bias_table(rel_bias, bucket_t, in_window_t):
    def body(rb_ref, bucket_ref, win_ref, out_ref):
        bk = bucket_ref[...]
        inside = win_ref[...] != 0
        has_prev = lax.broadcasted_iota(jnp.int32, bk.shape, 0) >= BLOCK
        for h in range(N_Q_HEADS):
            acc = jnp.full(bk.shape, NEG_INF, F32)
            for b in range(N_BUCKETS):
                acc = jnp.where(jnp.logical_and(bk == b, inside), rb_ref[b, h], acc)
            cols = slice((h % 2) * BLOCK, (h % 2 + 1) * BLOCK)
            out_ref[1, h // 2, :, cols] = acc
            out_ref[0, h // 2, :, cols] = jnp.where(has_prev, acc, NEG_INF)

    vmem = pl.BlockSpec(memory_space=pltpu.VMEM)
    return pl.pallas_call(
        body,
        name="bias_table",
        in_specs=[pl.BlockSpec(memory_space=pltpu.SMEM), vmem, vmem],
        out_specs=vmem,
        out_shape=jax.ShapeDtypeStruct((2, N_PAIRS, BAND, 2 * BLOCK), F32),
    )(rel_bias, bucket_t, in_window_t)


Q_BLOCKS = 4


def _banded_tiles(kvp_ref, kvc_ref):
    tile = kvc_ref[...].astype(F32)
    blocks = [kvp_ref[...].astype(F32)] + [tile[u * BLOCK:(u + 1) * BLOCK] for u in range(Q_BLOCKS)]
    return [_banded_kv(blocks[u], blocks[u + 1]) for u in range(Q_BLOCKS)]


def _bias_of(bias_ref, i, u, m):
    return bias_ref[jnp.minimum(i, 1) if u == 0 else 1, m]


def _banded_kv(kvp, kvc):
    kw = N_KV_HEADS * HEAD_DIM
    out = []
    for full in (jnp.concatenate([kvp[:, :kw], kvc[:, :kw]], axis=0), jnp.concatenate([kvp[:, kw:], kvc[:, kw:]], axis=0)):
        lo = lax.broadcasted_iota(jnp.int32, full.shape, 1) < HEAD_DIM
        rolled = pltpu.roll(full, HEAD_DIM, 1)
        x2 = [jnp.where(lo, full, rolled).astype(BF16), jnp.where(lo, rolled, full).astype(BF16)]
        ft = full.T
        x2t = [jnp.concatenate([ft[kh * HEAD_DIM:(kh + 1) * HEAD_DIM]] * 2, axis=0).astype(BF16) for kh in range(N_KV_HEADS)]
        out += [x2, x2t]
    return out


def _pair_rows(ref, rows, m, scale=None):
    both = ref[rows, m * LANES:(m + 1) * LANES].astype(F32)
    if scale is not None:
        both = both * scale
    lo = lax.broadcasted_iota(jnp.int32, both.shape, 1) < HEAD_DIM
    zero = jnp.zeros_like(both)
    return jnp.concatenate([jnp.where(lo, both, zero), jnp.where(lo, zero, both)], axis=0).astype(BF16)


def _pair_cols(res_t):
    top = lax.broadcasted_iota(jnp.int32, (LANES, BLOCK), 0) < HEAD_DIM
    return jnp.where(top, res_t[:, :BLOCK], res_t[:, BLOCK:]).T


def _sink_row(sink_ref, m):
    first = lax.broadcasted_iota(jnp.int32, (1, 2 * BLOCK), 1) < BLOCK
    return jnp.where(first, sink_ref[0, 2 * m], sink_ref[0, 2 * m + 1])


def _probs_t(k2, qpair, bias, sink):
    return _softmax_t(_dot_nt(k2, qpair) + bias, sink)


def _softmax_t(logits, sink):
    mx = jnp.maximum(jnp.max(logits, axis=0, keepdims=True), sink)
    p = jnp.exp(logits - mx)
    sink_p = jnp.exp(sink - mx)
    inv = 1.0 / (jnp.sum(p, axis=0, keepdims=True) + sink_p)
    return p * inv, sink_p * inv


def _attn_and_out(q, kv, z2, biasm, sinks, h1, target, wbout, bpost):
    seq, aw = q.shape
    kvw = kv.shape[1]
    d = h1.shape[1]
    tile = Q_BLOCKS * BLOCK

    def body(sink_ref, q_ref, kvc_ref, kvp_ref, z2_ref, bias_ref, h1_ref, tgt_ref, w_ref, g_ref,
             o_ref, dh2_ref, dyb_ref, dattn_ref, dz2_ref, acc_ref, attn_ref):
        i = pl.program_id(0)

        @pl.when(i == 0)
        def _():
            acc_ref[...] = jnp.zeros_like(acc_ref)

        banded = _banded_tiles(kvp_ref, kvc_ref)
        units = [(u, m) for u in range(Q_BLOCKS) for m in range(N_PAIRS)]
        kv_of = lambda m: (2 * m) // GROUP
        logits, probs = {}, {}
        for step in range(len(units) + 2):
            if step < len(units):
                u, m = units[step]
                qpair = _pair_rows(q_ref, slice(u * BLOCK, (u + 1) * BLOCK), m, SCALE)
                logits[step] = _dot_nt(banded[u][0][kv_of(m)], qpair) + _bias_of(bias_ref, i, u, m)
            if 0 <= step - 1 < len(units):
                u, m = units[step - 1]
                probs[step - 1] = _softmax_t(logits.pop(step - 1), _sink_row(sink_ref, m))[0].astype(BF16)
            if 0 <= step - 2 < len(units):
                u, m = units[step - 2]
                out_t = _dot(banded[u][3][kv_of(m)], probs.pop(step - 2))
                attn_ref[u * BLOCK:(u + 1) * BLOCK, m * LANES:(m + 1) * LANES] = _pair_cols(out_t)
        attn = attn_ref[...]
        sz, dsz = _silu(z2_ref[...])
        o = (attn * sz).astype(BF16)
        o_ref[...] = o

        w = w_ref[...]
        yb = _dot(o, w)
        ybn, r = _rms(yb)
        g = g_ref[...]
        diff = h1_ref[...] + ybn * g - tgt_ref[...]
        dh2 = diff * (1.0 / d)
        dh2_ref[...] = dh2
        acc_ref[0:1, :] += jnp.sum(dh2 * ybn, axis=0, keepdims=True)
        tok = jnp.mean(diff * diff, axis=-1, keepdims=True)
        acc_ref[1:2, :] += 0.5 * jnp.sum(tok, axis=0, keepdims=True)
        dyb = _rms_bwd(dh2 * g, ybn, r).astype(BF16)
        dyb_ref[...] = dyb
        do = _dot_nt(dyb, w)
        dattn_ref[...] = (do * sz).astype(BF16)
        dz2_ref[...] = (do * attn * dsz).astype(BF16)

    blk = lambda w: pl.BlockSpec((tile, w), lambda i: (i, 0))
    return pl.pallas_call(
        body,
        name="attn_and_out",
        grid=(seq // tile,),
        in_specs=[
            pl.BlockSpec(memory_space=pltpu.SMEM),
            blk(aw),
            blk(kvw),
            pl.BlockSpec((BLOCK, kvw), lambda i: (jnp.maximum(Q_BLOCKS * i - 1, 0), 0)),
            blk(aw),
            _full(biasm.shape),
            blk(d),
            blk(d),
            _full(wbout.shape),
            _full(bpost.shape),
        ],
        out_specs=[blk(aw), blk(d), blk(d), blk(aw), blk(aw), _resident((8, d))],
        out_shape=[
            jax.ShapeDtypeStruct((seq, aw), BF16),
            jax.ShapeDtypeStruct((seq, d), F32),
            jax.ShapeDtypeStruct((seq, d), BF16),
            jax.ShapeDtypeStruct((seq, aw), BF16),
            jax.ShapeDtypeStruct((seq, aw), BF16),
            jax.ShapeDtypeStruct((8, d), F32),
        ],
        scratch_shapes=[pltpu.VMEM((tile, aw), F32)],
        compiler_params=_params(("arbitrary",), 56),
    )(sinks, q, kv, kv, z2, biasm, h1, target, wbout, bpost)


def _attn_bwd(q, kv, dattn, biasm, sinks, ready):
    seq, aw = q.shape
    kvw = kv.shape[1]
    kw = N_KV_HEADS * HEAD_DIM
    nb = seq // BLOCK
    pairs_per_kv = N_PAIRS // N_KV_HEADS
    nr = len(ready)

    tile = Q_BLOCKS * BLOCK
    nsteps = seq // tile
    held = (Q_BLOCKS - 1) * BLOCK

    def body(sink_ref, q_ref, kvc_ref, kvp_ref, da_ref, bias_ref, *refs):
        ready_refs, (dq_ref, dkv_ref, dssum_ref, dsink_ref) = refs[:nr], refs[nr:nr + 4]
        landed_refs, scratch = refs[nr + 4:2 * nr + 4], refs[2 * nr + 4:]
        carry_ref, done_ref, qs_ref, dos_ref, dst_ref, pt_ref, *sems = scratch
        i = pl.program_id(0)

        @pl.when(i == 0)
        def _():
            dssum_ref[...] = jnp.zeros_like(dssum_ref)
            dsink_ref[...] = jnp.zeros_like(dsink_ref)
            carry_ref[...] = jnp.zeros_like(carry_ref)
            done_ref[...] = jnp.zeros_like(done_ref)
            _exchange_start(ready_refs, landed_refs, *sems, True)

        @pl.when(i == nsteps)
        def _():
            _exchange_wait(ready_refs, landed_refs, *sems, True)

        @pl.when(i < nsteps)
        def _():
            lo = lax.broadcasted_iota(jnp.int32, (BAND, LANES), 1) < HEAD_DIM
            head_lane = lax.broadcasted_iota(jnp.int32, (1, LANES), 1)
            banded = _banded_tiles(kvp_ref, kvc_ref)
            units = [(u, m) for u in range(Q_BLOCKS) for m in range(N_PAIRS)]
            dsink = jnp.zeros((1, LANES), F32)
            folded = {}
            logits, dps, dsbs = {}, {}, {}
            for step in range(len(units) + 2):
                if step < len(units):
                    u, m = units[step]
                    kh, rows = m // pairs_per_kv, slice((m % pairs_per_kv) * BAND, (m % pairs_per_kv + 1) * BAND)
                    qrows = slice(u * BLOCK, (u + 1) * BLOCK)
                    qpair = _pair_rows(q_ref, qrows, m, SCALE)
                    dopair = _pair_rows(da_ref, qrows, m)
                    qs_ref[u, kh, rows, :] = qpair
                    dos_ref[u, kh, rows, :] = dopair
                    logits[step] = _dot_nt(banded[u][0][kh], qpair) + _bias_of(bias_ref, i, u, m)
                    dps[step] = _dot_nt(banded[u][2][kh], dopair)
                if 0 <= step - 1 < len(units):
                    u, m = units[step - 1]
                    kh, rows = m // pairs_per_kv, slice((m % pairs_per_kv) * BAND, (m % pairs_per_kv + 1) * BAND)
                    pn, sink_p = _softmax_t(logits.pop(step - 1), _sink_row(sink_ref, m))
                    dp = dps.pop(step - 1)
                    delta = jnp.sum(pn * dp, axis=0, keepdims=True)
                    ds = pn * (dp - delta)
                    dssum_ref[m] += ds
                    sink_term = sink_p * delta
                    for e in range(2):
                        total = jnp.sum(sink_term[:, e * BLOCK:(e + 1) * BLOCK], axis=1, keepdims=True)
                        dsink = dsink - jnp.where(head_lane == 2 * m + e, total, 0.0)
                    dsbs[step - 1] = ds.astype(BF16)
                    dst_ref[u, kh, :, rows] = dsbs[step - 1]
                    pt_ref[u, kh, :, rows] = pn.astype(BF16)
                if 0 <= step - 2 < len(units):
                    u, m = units[step - 2]
                    kh = m // pairs_per_kv
                    dq_t = _dot(banded[u][1][kh], dsbs.pop(step - 2))
                    dq_ref[u * BLOCK:(u + 1) * BLOCK, m * LANES:(m + 1) * LANES] = (_pair_cols(dq_t) * SCALE).astype(BF16)
                    if m % pairs_per_kv == pairs_per_kv - 1:
                        for name, lhs_ref, rhs_ref in (("k", dst_ref, qs_ref), ("v", pt_ref, dos_ref)):
                            acc = _dot(lhs_ref[u, kh], rhs_ref[u, kh])
                            folded[u, kh, name] = acc + pltpu.roll(acc, HEAD_DIM, 1)
            dsink_ref[0:1, :] += dsink
            dkv = [jnp.concatenate([jnp.where(lo, folded[u, 0, n], folded[u, 1, n]) for n in ("k", "v")], axis=1)
                   for u in range(Q_BLOCKS)]

            @pl.when(i > 0)
            def _():
                if held:
                    dkv_ref[:held, :] = done_ref[...].astype(BF16)
                dkv_ref[held:, :] = (carry_ref[...] + dkv[0][:BLOCK]).astype(BF16)

            for u in range(Q_BLOCKS - 1):
                done_ref[u * BLOCK:(u + 1) * BLOCK, :] = dkv[u][BLOCK:] + dkv[u + 1][:BLOCK]
            carry_ref[...] = dkv[Q_BLOCKS - 1][BLOCK:]

        @pl.when(i == nsteps)
        def _():
            if held:
                dkv_ref[:held, :] = done_ref[...].astype(BF16)
            dkv_ref[held:, :] = carry_ref[...].astype(BF16)

    last = nsteps - 1
    blk = lambda w: pl.BlockSpec((tile, w), lambda i: (jnp.minimum(i, last), 0))
    outs = pl.pallas_call(
        body,
        name="attn_bwd",
        grid=(nsteps + 1,),
        in_specs=[
            pl.BlockSpec(memory_space=pltpu.SMEM),
            blk(aw),
            blk(kvw),
            pl.BlockSpec((BLOCK, kvw), lambda i: (jnp.clip(Q_BLOCKS * i - 1, 0, nb - 1), 0)),
            blk(aw),
            _full(biasm.shape),
        ] + [HBM_SPEC] * nr,
        out_specs=[
            blk(aw),
            pl.BlockSpec((tile, kvw), lambda i: (jnp.maximum(i - 1, 0), 0)),
            _resident(biasm.shape[1:]),
            _resident((8, LANES)),
        ] + [HBM_SPEC] * nr,
        out_shape=[
            jax.ShapeDtypeStruct((seq, aw), BF16),
            jax.ShapeDtypeStruct((seq, kvw), BF16),
            jax.ShapeDtypeStruct(biasm.shape[1:], F32),
            jax.ShapeDtypeStruct((8, LANES), F32),
        ] + [jax.ShapeDtypeStruct(g.shape, g.dtype) for g in ready],
        scratch_shapes=[
            pltpu.VMEM((BLOCK, kvw), F32),
            pltpu.VMEM((max(held, 8), kvw), F32),
            pltpu.VMEM((Q_BLOCKS, N_KV_HEADS, pairs_per_kv * BAND, LANES), BF16),
            pltpu.VMEM((Q_BLOCKS, N_KV_HEADS, pairs_per_kv * BAND, LANES), BF16),
            pltpu.VMEM((Q_BLOCKS, N_KV_HEADS, BAND, pairs_per_kv * BAND), BF16),
            pltpu.VMEM((Q_BLOCKS, N_KV_HEADS, BAND, pairs_per_kv * BAND), BF16),
        ] + _exchange_sems(nr),
        compiler_params=_params(("arbitrary",), 48),
    )(sinks, q, kv, kv, dattn, biasm, *ready)
    return outs[:4], outs[4:]


def _relbias_grad(dssum2, bucket_row, chunk):
    heads, n = dssum2.shape

    def body(a_ref, bucket_ref, out_ref):
        @pl.when(pl.program_id(0) == 0)
        def _():
            out_ref[...] = jnp.zeros_like(out_ref)

        a = a_ref[...]
        hi = a.astype(BF16)
        lo = (a - hi.astype(F32)).astype(BF16)
        onehot_t = (lax.broadcasted_iota(jnp.int32, (LANES, chunk), 0) == bucket_ref[...]).astype(F32).astype(BF16)
        out_ref[...] += _dot_nt(hi, onehot_t) + _dot_nt(lo, onehot_t)

    return pl.pallas_call(
        body,
        name="relbias_grad",
        grid=(n // chunk,),
        in_specs=[pl.BlockSpec((heads, chunk), lambda i: (0, i)), pl.BlockSpec((1, chunk), lambda i: (0, i))],
        out_specs=_resident((heads, LANES)),
        out_shape=jax.ShapeDtypeStruct((heads, LANES), F32),
        compiler_params=_params(("arbitrary",), 32),
    )(dssum2, bucket_row)


def _layer_b_in_bwd(dh2, dq, dz2, dkv, h1, ya, wbin_g, wkv, kvn, bpre, sm, ready, ts):
    seq, d = h1.shape
    aw = dq.shape[1]
    kvw = dkv.shape[1]
    cw = wbin_g.shape[2]
    per = aw // cw

    nr = len(ready)
    nt = seq // ts

    def body(dh2_ref, dq_ref, dz2_ref, dkv_ref, h1_ref, ya_ref, wbin_ref, wkv_ref, kvn_ref, bpre_ref, sm_ref, *refs):
        ready_refs, (dh1_ref, dya_ref, acc_ref) = refs[:nr], refs[nr:nr + 3]
        landed_refs, sems = refs[nr + 3:2 * nr + 3], refs[2 * nr + 3:]

        @pl.when(pl.program_id(0) == 0)
        def _():
            acc_ref[...] = jnp.zeros_like(acc_ref)
            _exchange_start(ready_refs, landed_refs, *sems, True)

        @pl.when(pl.program_id(0) == nt - 1)
        def _():
            _exchange_wait(ready_refs, landed_refs, *sems, True)

        dn4 = jnp.zeros((ts, d), F32)
        for j in range(N_DEV):
            src = dq_ref if j < per else dz2_ref
            jj = j % per
            dn4 = dn4 + _dot_nt(src[:, jj * cw:(jj + 1) * cw], wbin_ref[j])
        dn3 = _dot_nt(dkv_ref[...], wkv_ref[...])
        hn, r = _rms(h1_ref[...])
        acc_ref[0:1, :] += jnp.sum(dn4 * hn, axis=0, keepdims=True)
        acc_ref[1:2, :] += jnp.sum(dn3 * hn, axis=0, keepdims=True)
        dh1 = dh2_ref[...] + _rms_bwd(dn4 * bpre_ref[...] + dn3 * kvn_ref[...], hn, r)
        dh1_ref[...] = dh1
        yan, r2 = _rms(ya_ref[...])
        acc_ref[2:3, :] += jnp.sum(dh1 * yan, axis=0, keepdims=True)
        dya_ref[...] = _rms_bwd(dh1 * sm_ref[4:5, :], yan, r2).astype(BF16)

    outs = pl.pallas_call(
        body,
        name="layer_b_in_bwd",
        grid=(nt,),
        in_specs=[_rows(ts, d), _rows(ts, aw), _rows(ts, aw), _rows(ts, kvw), _rows(ts, d), _rows(ts, d),
                  _full(wbin_g.shape), _full(wkv.shape), _full(kvn.shape), _full(bpre.shape), _full(sm.shape)]
        + [HBM_SPEC] * nr,
        out_specs=[_rows(ts, d), _rows(ts, d), _resident((8, d))] + [HBM_SPEC] * nr,
        out_shape=[jax.ShapeDtypeStruct((seq, d), F32), jax.ShapeDtypeStruct((seq, d), BF16),
                   jax.ShapeDtypeStruct((8, d), F32)] + [jax.ShapeDtypeStruct(g.shape, g.dtype) for g in ready],
        scratch_shapes=_exchange_sems(nr),
        compiler_params=_params(("arbitrary",), 48),
    )(dh2, dq, dz2, dkv, h1, ya, wbin_g, wkv, kvn, bpre, sm, *ready)
    return outs[:3], outs[3:]


def _layer_a_bwd(dya, proj, conv, dh1, x2, wout, win_g, sm, ts):
    seq, d = x2.shape
    width = wout.shape[0]
    half = win_g.shape[2]
    n_half = width // half
    nt = seq // ts

    def body(dya_ref, proj_ref, conv_ref, dh1_ref, x_ref, wout_ref, win_ref, sm_ref, dproj_ref, gx_ref, acc_ref,
             dnext_ref):
        @pl.when(pl.program_id(0) == 0)
        def _():
            acc_ref[...] = jnp.zeros_like(acc_ref)
            dnext_ref[...] = jnp.zeros_like(dnext_ref)

        dy = _dot_nt(dya_ref[...], wout_ref[...])
        row = lax.broadcasted_iota(jnp.int32, (ts, half), 0)
        dn1 = jnp.zeros((ts, d), F32)
        for hh in range(n_half):
            cols = slice(hh * half, (hh + 1) * half)
            b, c, u, z = [proj_ref[:, (part * n_half + hh) * half:(part * n_half + hh + 1) * half].astype(F32)
                          for part in range(4)]
            cv = conv_ref[:, cols].astype(F32)
            dyh = dy[:, cols]
            sz, dsz = _silu(z)
            dconv = dyh * b * sz
            grads = [dyh * cv * sz, None, None, dyh * b * cv * dsz]
            next0, next1 = dnext_ref[0:1, cols], dnext_ref[1:2, cols]
            dc1 = jnp.where(row == ts - 1, next0, pltpu.roll(dconv, ts - 1, 0))
            dc2 = jnp.where(row == ts - 1, next1, jnp.where(row == ts - 2, next0, pltpu.roll(dconv, ts - 2, 0)))
            dnext_ref[:, cols] = dconv[0:8, :]
            v = c * u
            acc_ref[1:2, cols] += jnp.sum(dc2 * v, axis=0, keepdims=True)
            acc_ref[2:3, cols] += jnp.sum(dc1 * v, axis=0, keepdims=True)
            acc_ref[3:4, cols] += jnp.sum(dconv * v, axis=0, keepdims=True)
            dv = sm_ref[3:4, cols] * dconv + sm_ref[2:3, cols] * dc1 + sm_ref[1:2, cols] * dc2
            grads[1] = dv * u
            grads[2] = dv * c
            for part in range(4):
                j = part * n_half + hh
                gj = grads[part].astype(BF16)
                dproj_ref[:, j * half:(j + 1) * half] = gj
                dn1 = dn1 + _dot_nt(gj, win_ref[j])
        xn, r = _rms(x_ref[...])
        acc_ref[0:1, :] += jnp.sum(dn1 * xn, axis=0, keepdims=True)
        gx_ref[...] = dh1_ref[...] + _rms_bwd(dn1 * sm_ref[0:1, :], xn, r)

    rev = lambda w: pl.BlockSpec((ts, w), lambda i: (nt - 1 - i, 0))
    return pl.pallas_call(
        body,
        name="layer_a_bwd",
        grid=(nt,),
        in_specs=[rev(d), rev(4 * width), rev(width), rev(d), rev(d), _full(wout.shape), _full(win_g.shape), _full(sm.shape)],
        out_specs=[rev(4 * width), rev(d), _resident((8, d))],
        out_shape=[jax.ShapeDtypeStruct((seq, 4 * width), BF16), jax.ShapeDtypeStruct((seq, d), F32),
                   jax.ShapeDtypeStruct((8, d), F32)],
        scratch_shapes=[pltpu.VMEM((8, width), F32)],
        compiler_params=_params(("arbitrary",), 56),
    )(dya, proj, conv, dh1, x2, wout, win_g, sm)


def _wgrad(a, bs, n_slots, ts, name, ready=()):
    nr = len(ready)
    seq, k = a.shape
    nb_in = len(bs)
    n_each = bs[0].shape[1]
    n = nb_in * n_each
    bn = min(n_each, 1024)
    per_in = n_each // bn
    n_blocks = nb_in * per_in
    ns = seq // ts

    def b_spec(idx):
        def index(j, s):
            mine = j // per_in == idx
            row = jnp.where(mine, s, jnp.where(j // per_in > idx, ns - 1, 0))
            return (row, jnp.where(mine, j % per_in, jnp.where(j // per_in > idx, per_in - 1, 0)))
        return pl.BlockSpec((ts, bn), index)

    if n_slots:
        sw = n // n_slots
        spb = bn // sw
        out_shape = jax.ShapeDtypeStruct((n_slots, k, sw), BF16)
        out_spec = pl.BlockSpec((spb, k, sw), lambda j, s: (j, 0, 0))
    else:
        out_shape = jax.ShapeDtypeStruct((k, n), BF16)
        out_spec = pl.BlockSpec((k, bn), lambda j, s: (0, j))

    def body(a_ref, *refs):
        b_refs, ready_refs, o_ref = refs[:nb_in], refs[nb_in:nb_in + nr], refs[nb_in + nr]
        landed_refs, (acc_ref, *sems) = refs[nb_in + nr + 1:nb_in + 2 * nr + 1], refs[nb_in + 2 * nr + 1:]
        j, s = pl.program_id(0), pl.program_id(1)

        if nr:
            @pl.when(jnp.logical_and(j == 0, s == 0))
            def _():
                _exchange_start(ready_refs, landed_refs, *sems, True)

            @pl.when(jnp.logical_and(j == n_blocks - 1, s == ns - 1))
            def _():
                _exchange_wait(ready_refs, landed_refs, *sems, True)

        @pl.when(s == 0)
        def _():
            acc_ref[...] = jnp.zeros_like(acc_ref)

        for idx in range(nb_in):
            @pl.when(j // per_in == idx)
            def _(idx=idx):
                acc_ref[...] += _dot_tn(a_ref[...], b_refs[idx][...])

        @pl.when(s == ns - 1)
        def _():
            if n_slots:
                for e in range(spb):
                    o_ref[e] = acc_ref[:, e * sw:(e + 1) * sw].astype(BF16)
            else:
                o_ref[...] = acc_ref[...].astype(BF16)

    outs = pl.pallas_call(
        body,
        name=name,
        grid=(n_blocks, ns),
        in_specs=[pl.BlockSpec((ts, k), lambda j, s: (s, 0))] + [b_spec(idx) for idx in range(nb_in)] + [HBM_SPEC] * nr,
        out_specs=[out_spec] + [HBM_SPEC] * nr,
        out_shape=[out_shape] + [jax.ShapeDtypeStruct(g.shape, g.dtype) for g in ready],
        scratch_shapes=[pltpu.VMEM((k, bn), F32)] + (_exchange_sems(nr) if nr else []),
        compiler_params=_params(("arbitrary", "arbitrary"), 48),
    )(a, *bs, *ready)
    return (outs[0], outs[1:]) if nr else outs[0]


def _adamw(ws, gs, ms, vs):
    n = len(ws)

    def step(w, g, m, v):
        m = ADAM_B1 * m + (1.0 - ADAM_B1) * g
        v = ADAM_B2 * v + (1.0 - ADAM_B2) * jnp.square(g)
        m_hat = m / (1.0 - ADAM_B1 ** ADAM_STEP)
        v_hat = v / (1.0 - ADAM_B2 ** ADAM_STEP)
        return -ADAM_LR * (m_hat / (jnp.sqrt(v_hat) + ADAM_EPS) + ADAM_WD * w), m, v

    def body(*refs):
        w_refs, g_refs, m_refs, v_refs = (refs[k * n:(k + 1) * n] for k in range(4))
        d_refs, nm_refs, nv_refs = (refs[(4 + k) * n:(5 + k) * n] for k in range(3))
        for t in range(n):
            rows = w_refs[t].shape[0]
            if rows <= 128:
                d_refs[t][...], nm_refs[t][...], nv_refs[t][...] = step(
                    w_refs[t][...], g_refs[t][...], m_refs[t][...], v_refs[t][...])
                continue
            chunk = 128

            def one(i, carry, t=t):
                r = pl.ds(pl.multiple_of(i * chunk, chunk), chunk)
                d_refs[t][r, :], nm_refs[t][r, :], nv_refs[t][r, :] = step(
                    w_refs[t][r, :], g_refs[t][r, :], m_refs[t][r, :], v_refs[t][r, :])
                return carry

            lax.fori_loop(0, rows // chunk, one, 0)

    vmem = pl.BlockSpec(memory_space=pltpu.VMEM)
    outs = pl.pallas_call(
        body,
        name="adamw",
        in_specs=[vmem] * (4 * n),
        out_specs=[vmem] * (3 * n),
        out_shape=[jax.ShapeDtypeStruct(w.shape, F32) for w in ws] * 3,
        compiler_params=_params(vmem_mib=56),
    )(*ws, *gs, *ms, *vs)
    return outs[:n], outs[n:2 * n], outs[2 * n:]


def _band_structure():
    q_loc = jnp.arange(BLOCK, dtype=jnp.int32)[:, None]
    s_loc = jnp.arange(2 * BLOCK, dtype=jnp.int32)[None, :]
    dist = q_loc + BLOCK - s_loc
    in_window = (dist >= 0) & (dist < BLOCK)
    dd = jnp.maximum(dist, 0)
    max_exact = N_BUCKETS // 2
    large = max_exact + (jnp.log(jnp.maximum(dd, 1).astype(F32) / max_exact) / math.log(MAX_DISTANCE / max_exact)
                         * (N_BUCKETS - max_exact)).astype(jnp.int32)
    bucket = jnp.where(dd < max_exact, dd, jnp.minimum(large, N_BUCKETS - 1))
    return bucket, in_window.astype(jnp.int32)


def _place_rows(a, row, rows=8):
    return jnp.pad(a, ((row, rows - row - a.shape[0]), (0, 0)))


def kernel(x, a_pre_norm, a_w_in, a_conv_w, a_w_out, a_post_norm, kv_norm, w_kv, rel_bias, b_pre_norm, b_w_in, b_sinks, b_w_out, b_post_norm, loss_target, m_a_pre_norm, m_a_w_in, m_a_conv_w, m_a_w_out, m_a_post_norm, m_kv_norm, m_w_kv, m_rel_bias, m_b_pre_norm, m_b_w_in, m_b_sinks, m_b_w_out, m_b_post_norm, v_a_pre_norm, v_a_w_in, v_a_conv_w, v_a_w_out, v_a_post_norm, v_kv_norm, v_w_kv, v_rel_bias, v_b_pre_norm, v_b_w_in, v_b_sinks, v_b_w_out, v_b_post_norm):
    seq, d = x.shape[1], x.shape[2]
    x2 = x.reshape(seq, d)
    target = loss_target.reshape(seq, d)
    shard = a_pre_norm.shape[1]
    me = _my_index()
    ts_a = min(seq, 512)
    ts = min(seq, 512)
    ts_w = min(seq, 2048)

    small = _place_rows(a_pre_norm, 0) + _place_rows(a_conv_w[0], 1) + _place_rows(a_post_norm, 4)
    win_g, wout_g, small_g = _all_gather([a_w_in[0], a_w_out[0], small], [BF16, BF16, F32])
    wout = wout_g.reshape(-1, wout_g.shape[2])
    sm = small_g.transpose(1, 0, 2).reshape(8, N_DEV * shard)
    kvn = kv_norm.reshape(1, d)

    (h1, n1, proj, conv, y, ya), (wkv_g, wbin_g, wbout_g) = _layer_a_fwd(
        x2, sm, win_g, wout, [w_kv.astype(BF16), b_w_in[0].astype(BF16), b_w_out[0].astype(BF16)], ts_a)
    wkv = wkv_g.reshape(-1, wkv_g.shape[2])
    wbout = wbout_g.reshape(-1, wbout_g.shape[2])
    n3, n4, kv, q, z2 = _layer_b_in(h1, kvn, b_pre_norm, wkv, wbin_g, min(seq, 1024))
    bucket, in_window = _band_structure()
    biasm = _bias_table(rel_bias, bucket.T, in_window.T)
    o, dh2, dyb, dattn, dz2, acc_c = _attn_and_out(q, kv, z2, biasm, b_sinks, h1, target, wbout, b_post_norm)

    g_wbout = _wgrad(o, [dyb], 0, ts_w, "wgrad_b_out").reshape(wbout_g.shape)
    (dq, dkv, dssum, dsink), (l_wbout,) = _attn_bwd(q, kv, dattn, biasm, b_sinks, [g_wbout])
    by_head = dssum.reshape(N_PAIRS, BAND, 2, BLOCK).transpose(0, 2, 3, 1)
    relb = _relbias_grad(by_head.reshape(N_Q_HEADS, -1), bucket.reshape(1, -1), 4096)
    g_wkv = _wgrad(n3, [dkv], 0, ts_w, "wgrad_kv").reshape(wkv_g.shape)
    g_wbin = _wgrad(n4, [dq, dz2], N_DEV, ts_w, "wgrad_b_in")
    (dh1, dya, acc_b), (l_wkv, l_wbin) = _layer_b_in_bwd(
        dh2, dq, dz2, dkv, h1, ya, wbin_g, wkv, kvn, b_pre_norm, sm, [g_wkv, g_wbin], ts)
    dproj, gx, acc_a = _layer_a_bwd(dya, proj, conv, dh1, x2, wout, win_g, sm, ts_a)
    g_wout = _wgrad(y, [dya], 0, ts_w, "wgrad_a_out").reshape(wout_g.shape)
    g_win, (l_wout,) = _wgrad(n1, [dproj], N_DEV, ts_w, "wgrad_a_in", [g_wout])

    r_win, (r_wout, r_wkv, r_wbin, r_wbout), (s_a, s_b, s_c, s_relb, s_sink) = _reduce_exchange(
        g_win, [l_wout, l_wkv, l_wbin, l_wbout], [acc_a, acc_b, acc_c, relb, dsink])
    mine = lambda rows: lax.dynamic_slice_in_dim(rows, me * shard, shard, axis=1)
    loss = s_c[1, 0]
    weights = [a_pre_norm, a_w_in[0], a_conv_w[0], a_w_out[0], a_post_norm, kvn, w_kv, rel_bias, b_pre_norm,
               b_w_in[0], b_sinks, b_w_out[0], b_post_norm]
    grads = [mine(s_a[0:1]), r_win, mine(s_a[1:4]), r_wout, mine(s_b[2:3]), s_b[1:2], r_wkv,
             s_relb[:, :N_BUCKETS].T, s_b[0:1], r_wbin, s_sink[0:1, :N_Q_HEADS], r_wbout, s_c[0:1]]
    first = [m_a_pre_norm, m_a_w_in[0], m_a_conv_w[0], m_a_w_out[0], m_a_post_norm, m_kv_norm.reshape(1, d), m_w_kv,
             m_rel_bias, m_b_pre_norm, m_b_w_in[0], m_b_sinks, m_b_w_out[0], m_b_post_norm]
    second = [v_a_pre_norm, v_a_w_in[0], v_a_conv_w[0], v_a_w_out[0], v_a_post_norm, v_kv_norm.reshape(1, d), v_w_kv,
              v_rel_bias, v_b_pre_norm, v_b_w_in[0], v_b_sinks, v_b_w_out[0], v_b_post_norm]
    deltas, new_m, new_v = _adamw(weights, grads, first, second)

    shapes = [a_pre_norm.shape, a_w_in.shape, a_conv_w.shape, a_w_out.shape, a_post_norm.shape, kv_norm.shape,
              w_kv.shape, rel_bias.shape, b_pre_norm.shape, b_w_in.shape, b_sinks.shape, b_w_out.shape, b_post_norm.shape]
    shaped = lambda arrays: [a.reshape(s) for a, s in zip(arrays, shapes)]
    return (loss, gx.reshape(x.shape), *shaped(grads), *shaped(deltas), *shaped(new_m), *shaped(new_v))
```

```python
import functools
import math

import jax
import jax.numpy as jnp
from jax import lax
from jax.experimental import pallas as pl
from jax.experimental.pallas import tpu as pltpu

HEAD_DIM = 64
N_Q_HEADS = 16
N_KV_HEADS = 2
GROUP = N_Q_HEADS // N_KV_HEADS
BLOCK = 128
N_BUCKETS = 32
MAX_DISTANCE = 128
EPS = 1e-6
NEG_INF = -1e30
SCALE = HEAD_DIM ** -0.5

ADAM_LR = 0.001
ADAM_B1 = 0.9
ADAM_B2 = 0.999
ADAM_EPS = 1e-08
ADAM_WD = 0.01
ADAM_STEP = 10

N_DEV = 8
LANES = 128
F32 = jnp.float32
BF16 = jnp.bfloat16
MESH = pl.DeviceIdType.MESH
MIB = 1024 * 1024


def _params(semantics=None, vmem_mib=48):
    return pltpu.CompilerParams(dimension_semantics=semantics, vmem_limit_bytes=vmem_mib * MIB)


def _full(shape):
    zeros = (0,) * len(shape)
    return pl.BlockSpec(shape, lambda *_: zeros, pipeline_mode=pl.Buffered(1))


def _resident(shape):
    zeros = (0,) * len(shape)
    return pl.BlockSpec(shape, lambda *_: zeros)


def _rows(ts, cols):
    return pl.BlockSpec((ts, cols), lambda i: (i, 0))


def _dot(a, b):
    return jnp.dot(a, b, preferred_element_type=F32)


def _dot_nt(a, b):
    return lax.dot_general(a, b, (((1,), (1,)), ((), ())), preferred_element_type=F32)


def _dot_tn(a, b):
    return lax.dot_general(a, b, (((0,), (0,)), ((), ())), preferred_element_type=F32)


def _rms(xf):
    r = lax.rsqrt(jnp.mean(xf * xf, axis=-1, keepdims=True) + EPS)
    return xf * r, r


def _rms_bwd(dn, xn, r):
    return r * (dn - xn * jnp.mean(dn * xn, axis=-1, keepdims=True))


def _silu(z):
    s = jax.nn.sigmoid(z)
    return z * s, s * (1.0 + z * (1.0 - s))


def _my_index():
    return 4 * lax.axis_index("x") + 2 * lax.axis_index("y") + lax.axis_index("c")


def _all_gather(shards, out_dtypes):
    n = len(shards)

    def body(*refs):
        ins, outs = refs[:n], refs[n:2 * n]
        send_sems, recv_sems = refs[2 * n], refs[2 * n + 1]
        x, y, c = lax.axis_index("x"), lax.axis_index("y"), lax.axis_index("c")
        me, sibling = (x, y, c), (x, y, 1 - c)
        x_nbr, y_nbr, diagonal = (1 - x, y), (x, 1 - y), (1 - x, 1 - y)
        south = c == 0
        relayed = (jnp.where(south, 1 - x, x), jnp.where(south, y, 1 - y))
        relay_to = (jnp.where(south, x, 1 - x), jnp.where(south, 1 - y, y))

        def copy(t, k, block, to):
            rows = outs[t].at[4 * block[0] + 2 * block[1] + block[2]]
            return pltpu.make_async_remote_copy(
                src_ref=rows, dst_ref=rows, send_sem=send_sems.at[t, k], recv_sem=recv_sems.at[t, k],
                device_id=to, device_id_type=MESH)

        for t in range(n):
            outs[t][pl.ds(_my_index(), 1)] = ins[t][...].astype(outs[t].dtype)[None]
        started = []

        def start(cp):
            cp.start()
            started.append(cp)

        for t in range(n):
            start(copy(t, 0, me, sibling))
            start(copy(t, 1, me, (*x_nbr, c)))
            start(copy(t, 2, me, (*y_nbr, c)))
        for k, chip in ((1, x_nbr), (2, y_nbr)):
            for t in range(n):
                copy(t, k, (*chip, c), me).wait_recv()
                start(copy(t, 3 + k, (*chip, c), sibling))
        for t in range(n):
            start(copy(t, 3, (*relayed, c), (*relay_to, c)))
        for t in range(n):
            copy(t, 3, (*diagonal, c), me).wait_recv()
            start(copy(t, 6, (*diagonal, c), sibling))
        for t in range(n):
            copy(t, 0, sibling, me).wait_recv()
        for k, chip in ((4, x_nbr), (5, y_nbr), (6, diagonal)):
            for t in range(n):
                copy(t, k, (*chip, 1 - c), me).wait_recv()
        for cp in started:
            cp.wait_send()

    vmem = pl.BlockSpec(memory_space=pltpu.VMEM)
    return pl.pallas_call(
        body,
        name="gather_weights",
        out_shape=[jax.ShapeDtypeStruct((N_DEV,) + s.shape, dt) for s, dt in zip(shards, out_dtypes)],
        in_specs=[vmem] * n,
        out_specs=[vmem] * n,
        scratch_shapes=[pltpu.SemaphoreType.DMA((n, 7)), pltpu.SemaphoreType.DMA((n, 7))],
        compiler_params=_params(vmem_mib=48),
    )(*shards)


def _peer(k):
    x, y, c = lax.axis_index("x"), lax.axis_index("y"), lax.axis_index("c")
    px = 1 - x if k & 4 else x
    py = 1 - y if k & 2 else y
    pc = 1 - c if k & 1 else c
    return (px, py, pc), 4 * px + 2 * py + pc


def _exchange(srcs, dsts, send_sems, recv_sems, local_sems, scatter):
    me = _my_index()
    sends, arrivals = [], []
    for k in range(1, N_DEV):
        peer, pidx = _peer(k)
        for t, (src, dst) in enumerate(zip(srcs, dsts)):
            mine = src.at[pidx] if scatter else src
            sems = dict(send_sem=send_sems.at[t, k - 1], recv_sem=recv_sems.at[t, k - 1], device_id=peer, device_id_type=MESH)
            sends.append(pltpu.make_async_remote_copy(src_ref=mine, dst_ref=dst.at[me], **sems))
            arrivals.append(pltpu.make_async_remote_copy(src_ref=mine, dst_ref=dst.at[pidx], **sems))
    local = [pltpu.make_async_copy(src.at[me] if scatter else src, dst.at[me], local_sems.at[t])
             for t, (src, dst) in enumerate(zip(srcs, dsts))]
    return sends, arrivals, local


def _exchange_start(*args):
    sends, _, local = _exchange(*args)
    for cp in sends + local:
        cp.start()


def _exchange_wait(*args):
    sends, arrivals, local = _exchange(*args)
    for cp in arrivals:
        cp.wait_recv()
    for cp in sends:
        cp.wait_send()
    for cp in local:
        cp.wait()


def _exchange_sems(n):
    return [pltpu.SemaphoreType.DMA((n, N_DEV - 1)), pltpu.SemaphoreType.DMA((n, N_DEV - 1)), pltpu.SemaphoreType.DMA((n,))]


HBM_SPEC = pl.BlockSpec(memory_space=pl.ANY)


def _sum_slots(recv_ref, out_ref):
    rows = out_ref.shape[0]
    chunk = min(rows, 128)

    def add(i, carry):
        r0 = pl.multiple_of(i * chunk, chunk)
        acc = recv_ref[0, pl.ds(r0, chunk), :].astype(F32)
        for dev in range(1, N_DEV):
            acc = acc + recv_ref[dev, pl.ds(r0, chunk), :].astype(F32)
        out_ref[pl.ds(r0, chunk), :] = acc
        return carry

    lax.fori_loop(0, rows // chunk, add, 0)


N_CHIPS = N_DEV // 2


def _rows_loop(rows, fn):
    chunk = min(rows, 128)

    def step(i, carry):
        fn(pl.ds(pl.multiple_of(i * chunk, chunk), chunk))
        return carry

    lax.fori_loop(0, rows // chunk, step, 0)


def _chip_reduce(g_ref, out_ref, sib_ref, chip_ref, send_ref, sems, between):
    sib_send, sib_recv, chip_send, chip_recv = sems
    x, y, c = lax.axis_index("x"), lax.axis_index("y"), lax.axis_index("c")
    my_chip = 2 * x + y
    rows = out_ref.shape[0]

    def chip_of(k):
        cx = 1 - x if k & 2 else x
        cy = 1 - y if k & 1 else y
        return (cx, cy), 2 * cx + cy

    def to_sibling(t):
        return pltpu.make_async_remote_copy(
            src_ref=g_ref.at[2 * t + 1 - c], dst_ref=sib_ref.at[t], send_sem=sib_send.at[t], recv_sem=sib_recv.at[t],
            device_id=(x, y, 1 - c), device_id_type=MESH)

    def to_chip(k):
        (cx, cy), t = chip_of(k)
        return t, pltpu.make_async_remote_copy(
            src_ref=send_ref.at[k - 1], dst_ref=chip_ref.at[my_chip], send_sem=chip_send.at[k - 1],
            recv_sem=chip_recv.at[k - 1], device_id=(cx, cy, c), device_id_type=MESH)

    def from_chip(k):
        _, t = chip_of(k)
        return pltpu.make_async_remote_copy(
            src_ref=send_ref.at[k - 1], dst_ref=chip_ref.at[t], send_sem=chip_send.at[k - 1],
            recv_sem=chip_recv.at[k - 1], device_id=(x, y, c), device_id_type=MESH)

    for t in range(N_CHIPS):
        to_sibling(t).start()
    for t in range(N_CHIPS):
        to_sibling(t).wait_recv()

    def pair_sum(t, r):
        return g_ref[2 * t + c, r, :].astype(F32) + sib_ref[t, r, :].astype(F32)

    for k in (3, 1, 2):
        t, cp = to_chip(k)

        def fill(r, t=t, k=k):
            send_ref[k - 1, r, :] = pair_sum(t, r).astype(BF16)

        _rows_loop(rows, fill)
        cp.start()
    between()

    def own(r):
        chip_ref[my_chip, r, :] = pair_sum(my_chip, r).astype(BF16)

    _rows_loop(rows, own)
    for k in range(1, N_CHIPS):
        from_chip(k).wait_recv()

    def total(r):
        acc = chip_ref[0, r, :].astype(F32)
        for t in range(1, N_CHIPS):
            acc = acc + chip_ref[t, r, :].astype(F32)
        out_ref[r, :] = acc

    _rows_loop(rows, total)
    for t in range(N_CHIPS):
        to_sibling(t).wait_send()
    for k in range(1, N_CHIPS):
        to_chip(k)[1].wait_send()


def _reduce_exchange(part, landed, smalls):
    nl, ng = len(landed), len(smalls)
    n_out = 1 + nl + ng

    def body(*refs):
        p_in, l_in, s_in = refs[0], refs[1:1 + nl], refs[1 + nl:n_out]
        p_out, l_out, s_out = refs[n_out], refs[n_out + 1:n_out + 1 + nl], refs[n_out + 1 + nl:2 * n_out]
        scratch = refs[2 * n_out:]
        s_recv, (sib_ref, chip_ref, send_ref), sems = scratch[:ng], scratch[ng:ng + 3], scratch[ng + 3:]

        def between():
            for t in range(nl):
                _sum_slots(l_in[t], l_out[t])

        _exchange_start(s_in, s_recv, *sems[4:], False)
        _chip_reduce(p_in, p_out, sib_ref, chip_ref, send_ref, sems[:4], between)
        _exchange_wait(s_in, s_recv, *sems[4:], False)
        for t in range(ng):
            acc = s_recv[t][0]
            for dev in range(1, N_DEV):
                acc = acc + s_recv[t][dev]
            s_out[t][...] = acc

    vmem = pl.BlockSpec(memory_space=pltpu.VMEM)
    slot = part.shape[1:]
    outs = pl.pallas_call(
        body,
        name="reduce_grads",
        out_shape=[jax.ShapeDtypeStruct(p.shape[1:], F32) for p in [part] + landed]
        + [jax.ShapeDtypeStruct(s.shape, F32) for s in smalls],
        in_specs=[vmem] * n_out,
        out_specs=[vmem] * n_out,
        scratch_shapes=[pltpu.VMEM((N_DEV,) + s.shape, F32) for s in smalls]
        + [pltpu.VMEM((N_CHIPS,) + slot, BF16), pltpu.VMEM((N_CHIPS,) + slot, BF16), pltpu.VMEM((N_CHIPS - 1,) + slot, BF16)]
        + [pltpu.SemaphoreType.DMA((N_CHIPS,)), pltpu.SemaphoreType.DMA((N_CHIPS,)),
           pltpu.SemaphoreType.DMA((N_CHIPS - 1,)), pltpu.SemaphoreType.DMA((N_CHIPS - 1,))]
        + _exchange_sems(ng),
        compiler_params=_params(vmem_mib=56),
    )(part, *landed, *smalls)
    return outs[0], outs[1:1 + nl], outs[1 + nl:]


def _layer_a_fwd(x2, sm, win_g, wout, later, ts):
    seq, d = x2.shape
    width = wout.shape[0]
    half = win_g.shape[2]
    n_half = width // half
    nl = len(later)
    nt = seq // ts

    def body(x_ref, sm_ref, win_ref, wout_ref, *refs):
        shard_refs, refs = refs[:nl], refs[nl:]
        h1_ref, n1_ref, proj_ref, conv_ref, y_ref, ya_ref = refs[:6]
        gathered_refs, (vprev_ref, *sems) = refs[6:6 + nl], refs[6 + nl:]

        @pl.when(pl.program_id(0) == 0)
        def _():
            vprev_ref[...] = jnp.zeros_like(vprev_ref)
            _exchange_start(shard_refs, gathered_refs, *sems, False)

        @pl.when(pl.program_id(0) == nt - 1)
        def _():
            _exchange_wait(shard_refs, gathered_refs, *sems, False)

        xf = x_ref[...]
        xn, _ = _rms(xf)
        n1 = (xn * sm_ref[0:1, :]).astype(BF16)
        n1_ref[...] = n1
        row = lax.broadcasted_iota(jnp.int32, (ts, half), 0)
        ya = jnp.zeros((ts, d), F32)
        for hh in range(n_half):
            cols = slice(hh * half, (hh + 1) * half)
            parts = []
            for part in range(4):
                j = part * n_half + hh
                pj = _dot(n1, win_ref[j])
                proj_ref[:, j * half:(j + 1) * half] = pj.astype(BF16)
                parts.append(pj)
            b, c, u, z = parts
            v = c * u
            last1, last2 = vprev_ref[7:8, cols], vprev_ref[6:7, cols]
            v1 = jnp.where(row == 0, last1, pltpu.roll(v, 1, 0))
            v2 = jnp.where(row == 0, last2, jnp.where(row == 1, last1, pltpu.roll(v, 2, 0)))
            vprev_ref[:, cols] = v[ts - 8:ts, :]
            conv = sm_ref[1:2, cols] * v2 + sm_ref[2:3, cols] * v1 + sm_ref[3:4, cols] * v
            conv_ref[:, cols] = conv.astype(BF16)
            yh = (b * conv * _silu(z)[0]).astype(BF16)
            y_ref[:, cols] = yh
            ya = ya + _dot(yh, wout_ref[cols, :])
        ya_ref[...] = ya
        h1_ref[...] = xf + _rms(ya)[0] * sm_ref[4:5, :]

    outs = pl.pallas_call(
        body,
        name="layer_a_fwd",
        grid=(nt,),
        in_specs=[_rows(ts, d), _full(sm.shape), _full(win_g.shape), _full(wout.shape)] + [HBM_SPEC] * nl,
        out_specs=[_rows(ts, d), _rows(ts, d), _rows(ts, 4 * width), _rows(ts, width), _rows(ts, width), _rows(ts, d)]
        + [HBM_SPEC] * nl,
        out_shape=[
            jax.ShapeDtypeStruct((seq, d), F32),
            jax.ShapeDtypeStruct((seq, d), BF16),
            jax.ShapeDtypeStruct((seq, 4 * width), BF16),
            jax.ShapeDtypeStruct((seq, width), BF16),
            jax.ShapeDtypeStruct((seq, width), BF16),
            jax.ShapeDtypeStruct((seq, d), F32),
        ] + [jax.ShapeDtypeStruct((N_DEV,) + s.shape, s.dtype) for s in later],
        scratch_shapes=[pltpu.VMEM((8, width), F32)] + _exchange_sems(nl),
        compiler_params=_params(("arbitrary",), 56),
    )(x2, sm, win_g, wout, *later)
    return outs[:6], outs[6:]


N_PAIRS = N_Q_HEADS // 2
BAND = 2 * BLOCK


def _bias_table(rel_bias, bucket_t, in_window_t):
    def body(rb_ref, bucket_ref, win_ref, out_ref):
        bk = bucket_ref[...]
        inside = win_ref[...] != 0
        has_prev = lax.broadcasted_iota(jnp.int32, bk.shape, 0) >= BLOCK
        for h in range(N_Q_HEADS):
            acc = jnp.full(bk.shape, NEG_INF, F32)
            for b in range(N_BUCKETS):
                acc = jnp.where(jnp.logical_and(bk == b, inside), rb_ref[b, h], acc)
            cols = slice((h % 2) * BLOCK, (h % 2 + 1) * BLOCK)
            out_ref[1, h // 2, :, cols] = acc
            out_ref[0, h // 2, :, cols] = jnp.where(has_prev, acc, NEG_INF)

    vmem = pl.BlockSpec(memory_space=pltpu.VMEM)
    return pl.pallas_call(
        body,
        name="bias_table",
        in_specs=[pl.BlockSpec(memory_space=pltpu.SMEM), vmem, vmem],
        out_specs=vmem,
        out_shape=jax.ShapeDtypeStruct((2, N_PAIRS, BAND, 2 * BLOCK), F32),
    )(rel_bias, bucket_t, in_window_t)


Q_BLOCKS = 4


def _banded_tiles(kvp_ref, kvc_ref):
    tile = kvc_ref[...].astype(F32)
    blocks = [kvp_ref[...].astype(F32)] + [tile[u * BLOCK:(u + 1) * BLOCK] for u in range(Q_BLOCKS)]
    return [_banded_kv(blocks[u], blocks[u + 1]) for u in range(Q_BLOCKS)]


def _bias_of(bias_ref, i, u, m):
    return bias_ref[jnp.minimum(i, 1) if u == 0 else 1, m]


def _banded_kv(kvp, kvc):
    kw = N_KV_HEADS * HEAD_DIM
    out = []
    for full in (jnp.concatenate([kvp[:, :kw], kvc[:, :kw]], axis=0), jnp.concatenate([kvp[:, kw:], kvc[:, kw:]], axis=0)):
        lo = lax.broadcasted_iota(jnp.int32, full.shape, 1) < HEAD_DIM
        rolled = pltpu.roll(full, HEAD_DIM, 1)
        x2 = [jnp.where(lo, full, rolled).astype(BF16), jnp.where(lo, rolled, full).astype(BF16)]
        ft = full.T
        x2t = [jnp.concatenate([ft[kh * HEAD_DIM:(kh + 1) * HEAD_DIM]] * 2, axis=0).astype(BF16) for kh in range(N_KV_HEADS)]
        out += [x2, x2t]
    return out


def _pair_rows(ref, rows, m, scale=None):
    both = ref[rows, m * LANES:(m + 1) * LANES].astype(F32)
    if scale is not None:
        both = both * scale
    lo = lax.broadcasted_iota(jnp.int32, both.shape, 1) < HEAD_DIM
    zero = jnp.zeros_like(both)
    return jnp.concatenate([jnp.where(lo, both, zero), jnp.where(lo, zero, both)], axis=0).astype(BF16)


def _pair_cols(res_t):
    top = lax.broadcasted_iota(jnp.int32, (LANES, BLOCK), 0) < HEAD_DIM
    return jnp.where(top, res_t[:, :BLOCK], res_t[:, BLOCK:]).T


def _sink_row(sink_ref, m):
    first = lax.broadcasted_iota(jnp.int32, (1, 2 * BLOCK), 1) < BLOCK
    return jnp.where(first, sink_ref[0, 2 * m], sink_ref[0, 2 * m + 1])


def _probs_t(k2, qpair, bias, sink):
    return _softmax_t(_dot_nt(k2, qpair) + bias, sink)


def _softmax_t(logits, sink):
    mx = jnp.maximum(jnp.max(logits, axis=0, keepdims=True), sink)
    p = jnp.exp(logits - mx)
    sink_p = jnp.exp(sink - mx)
    inv = 1.0 / (jnp.sum(p, axis=0, keepdims=True) + sink_p)
    return p * inv, sink_p * inv


def _layer_b_fwd(h1, target, kvn, bpre, wkv, wbin_g, biasm, sinks, wbout, bpost):
    seq, d = h1.shape
    kvw = wkv.shape[1]
    cw = wbin_g.shape[2]
    aw = N_Q_HEADS * HEAD_DIM
    per = aw // cw
    tile = Q_BLOCKS * BLOCK

    def body(sink_ref, h1_ref, tgt_ref, kvn_ref, bpre_ref, wkv_ref, wbin_ref, bias_ref, w_ref, g_ref,
             n3_ref, n4_ref, kvc_ref, q_ref, o_ref, dh2_ref, dyb_ref, dattn_ref, dz2_ref, acc_ref,
             attn_ref, z2_ref, kvp_ref):
        i = pl.program_id(0)

        @pl.when(i == 0)
        def _():
            acc_ref[...] = jnp.zeros_like(acc_ref)
            kvp_ref[...] = jnp.zeros_like(kvp_ref)

        hn, _ = _rms(h1_ref[...])
        n3 = (hn * kvn_ref[...]).astype(BF16)
        n4 = (hn * bpre_ref[...]).astype(BF16)
        n3_ref[...] = n3
        n4_ref[...] = n4
        kvc_ref[...] = _dot(n3, wkv_ref[...]).astype(BF16)
        for j in range(N_DEV):
            pj = _dot(n4, wbin_ref[j])
            if j < per:
                q_ref[:, j * cw:(j + 1) * cw] = pj.astype(BF16)
            else:
                z2_ref[:, (j - per) * cw:(j - per + 1) * cw] = pj

        banded = _banded_tiles(kvp_ref, kvc_ref)
        kvp_ref[...] = kvc_ref[tile - BLOCK:tile, :]
        units = [(u, m) for u in range(Q_BLOCKS) for m in range(N_PAIRS)]
        kv_of = lambda m: (2 * m) // GROUP
        logits, probs = {}, {}
        for step in range(len(units) + 2):
            if step < len(units):
                u, m = units[step]
                qpair = _pair_rows(q_ref, slice(u * BLOCK, (u + 1) * BLOCK), m, SCALE)
                logits[step] = _dot_nt(banded[u][0][kv_of(m)], qpair) + _bias_of(bias_ref, i, u, m)
            if 0 <= step - 1 < len(units):
                u, m = units[step - 1]
                probs[step - 1] = _softmax_t(logits.pop(step - 1), _sink_row(sink_ref, m))[0].astype(BF16)
            if 0 <= step - 2 < len(units):
                u, m = units[step - 2]
                out_t = _dot(banded[u][3][kv_of(m)], probs.pop(step - 2))
                attn_ref[u * BLOCK:(u + 1) * BLOCK, m * LANES:(m + 1) * LANES] = _pair_cols(out_t)
        attn = attn_ref[...]
        sz, dsz = _silu(z2_ref[...])
        o = (attn * sz).astype(BF16)
        o_ref[...] = o

        w = w_ref[...]
        yb = _dot(o, w)
        ybn, r = _rms(yb)
        g = g_ref[...]
        diff = h1_ref[...] + ybn * g - tgt_ref[...]
        dh2 = diff * (1.0 / d)
        dh2_ref[...] = dh2
        acc_ref[0:1, :] += jnp.sum(dh2 * ybn, axis=0, keepdims=True)
        tok = jnp.mean(diff * diff, axis=-1, keepdims=True)
        acc_ref[1:2, :] += 0.5 * jnp.sum(tok, axis=0, keepdims=True)
        dyb = _rms_bwd(dh2 * g, ybn, r).astype(BF16)
        dyb_ref[...] = dyb
        do = _dot_nt(dyb, w)
        dattn_ref[...] = (do * sz).astype(BF16)
        dz2_ref[...] = (do * attn * dsz).astype(BF16)

    blk = lambda w: pl.BlockSpec((tile, w), lambda i: (i, 0))
    return pl.pallas_call(
        body,
        name="layer_b_fwd",
        grid=(seq // tile,),
        in_specs=[
            pl.BlockSpec(memory_space=pltpu.SMEM),
            blk(d),
            blk(d),
            _full(kvn.shape),
            _full(bpre.shape),
            _full(wkv.shape),
            _full(wbin_g.shape),
            _full(biasm.shape),
            _full(wbout.shape),
            _full(bpost.shape),
        ],
        out_specs=[blk(d), blk(d), blk(kvw), blk(aw), blk(aw), blk(d), blk(d), blk(aw), blk(aw), _resident((8, d))],
        out_shape=[
            jax.ShapeDtypeStruct((seq, d), BF16),
            jax.ShapeDtypeStruct((seq, d), BF16),
            jax.ShapeDtypeStruct((seq, kvw), BF16),
            jax.ShapeDtypeStruct((seq, aw), BF16),
            jax.ShapeDtypeStruct((seq, aw), BF16),
            jax.ShapeDtypeStruct((seq, d), F32),
            jax.ShapeDtypeStruct((seq, d), BF16),
            jax.ShapeDtypeStruct((seq, aw), BF16),
            jax.ShapeDtypeStruct((seq, aw), BF16),
            jax.ShapeDtypeStruct((8, d), F32),
        ],
        scratch_shapes=[pltpu.VMEM((tile, aw), F32), pltpu.VMEM((tile, aw), F32), pltpu.VMEM((BLOCK, kvw), BF16)],
        compiler_params=_params(("arbitrary",), 56),
    )(sinks, h1, target, kvn, bpre, wkv, wbin_g, biasm, wbout, bpost)


def _attn_bwd(q, kv, dattn, biasm, sinks, ready):
    seq, aw = q.shape
    kvw = kv.shape[1]
    kw = N_KV_HEADS * HEAD_DIM
    nb = seq // BLOCK
    pairs_per_kv = N_PAIRS // N_KV_HEADS
    nr = len(ready)

    tile = Q_BLOCKS * BLOCK
    nsteps = seq // tile
    held = (Q_BLOCKS - 1) * BLOCK

    def body(sink_ref, q_ref, kvc_ref, kvp_ref, da_ref, bias_ref, *refs):
        ready_refs, (dq_ref, dkv_ref, dssum_ref, dsink_ref) = refs[:nr], refs[nr:nr + 4]
        landed_refs, scratch = refs[nr + 4:2 * nr + 4], refs[2 * nr + 4:]
        carry_ref, done_ref, qs_ref, dos_ref, dst_ref, pt_ref, *sems = scratch
        i = pl.program_id(0)

        @pl.when(i == 0)
        def _():
            dssum_ref[...] = jnp.zeros_like(dssum_ref)
            dsink_ref[...] = jnp.zeros_like(dsink_ref)
            carry_ref[...] = jnp.zeros_like(carry_ref)
            done_ref[...] = jnp.zeros_like(done_ref)
            _exchange_start(ready_refs, landed_refs, *sems, True)

        @pl.when(i == nsteps)
        def _():
            _exchange_wait(ready_refs, landed_refs, *sems, True)

        @pl.when(i < nsteps)
        def _():
            lo = lax.broadcasted_iota(jnp.int32, (BAND, LANES), 1) < HEAD_DIM
            head_lane = lax.broadcasted_iota(jnp.int32, (1, LANES), 1)
            banded = _banded_tiles(kvp_ref, kvc_ref)
            units = [(u, m) for u in range(Q_BLOCKS) for m in range(N_PAIRS)]
            dsink = jnp.zeros((1, LANES), F32)
            folded = {}
            logits, dps, dsbs = {}, {}, {}
            for step in range(len(units) + 2):
                if step < len(units):
                    u, m = units[step]
                    kh, rows = m // pairs_per_kv, slice((m % pairs_per_kv) * BAND, (m % pairs_per_kv + 1) * BAND)
                    qrows = slice(u * BLOCK, (u + 1) * BLOCK)
                    qpair = _pair_rows(q_ref, qrows, m, SCALE)
                    dopair = _pair_rows(da_ref, qrows, m)
                    qs_ref[u, kh, rows, :] = qpair
                    dos_ref[u, kh, rows, :] = dopair
                    logits[step] = _dot_nt(banded[u][0][kh], qpair) + _bias_of(bias_ref, i, u, m)
                    dps[step] = _dot_nt(banded[u][2][kh], dopair)
                if 0 <= step - 1 < len(units):
                    u, m = units[step - 1]
                    kh, rows = m // pairs_per_kv, slice((m % pairs_per_kv) * BAND, (m % pairs_per_kv + 1) * BAND)
                    pn, sink_p = _softmax_t(logits.pop(step - 1), _sink_row(sink_ref, m))
                    dp = dps.pop(step - 1)
                    delta = jnp.sum(pn * dp, axis=0, keepdims=True)
                    ds = pn * (dp - delta)
                    dssum_ref[m] += ds
                    sink_term = sink_p * delta
                    for e in range(2):
                        total = jnp.sum(sink_term[:, e * BLOCK:(e + 1) * BLOCK], axis=1, keepdims=True)
                        dsink = dsink - jnp.where(head_lane == 2 * m + e, total, 0.0)
                    dsbs[step - 1] = ds.astype(BF16)
                    dst_ref[u, kh, :, rows] = dsbs[step - 1]
                    pt_ref[u, kh, :, rows] = pn.astype(BF16)
                if 0 <= step - 2 < len(units):
                    u, m = units[step - 2]
                    kh = m // pairs_per_kv
                    dq_t = _dot(banded[u][1][kh], dsbs.pop(step - 2))
                    dq_ref[u * BLOCK:(u + 1) * BLOCK, m * LANES:(m + 1) * LANES] = (_pair_cols(dq_t) * SCALE).astype(BF16)
                    if m % pairs_per_kv == pairs_per_kv - 1:
                        for name, lhs_ref, rhs_ref in (("k", dst_ref, qs_ref), ("v", pt_ref, dos_ref)):
                            acc = _dot(lhs_ref[u, kh], rhs_ref[u, kh])
                            folded[u, kh, name] = acc + pltpu.roll(acc, HEAD_DIM, 1)
            dsink_ref[0:1, :] += dsink
            dkv = [jnp.concatenate([jnp.where(lo, folded[u, 0, n], folded[u, 1, n]) for n in ("k", "v")], axis=1)
                   for u in range(Q_BLOCKS)]

            @pl.when(i > 0)
            def _():
                if held:
                    dkv_ref[:held, :] = done_ref[...].astype(BF16)
                dkv_ref[held:, :] = (carry_ref[...] + dkv[0][:BLOCK]).astype(BF16)

            for u in range(Q_BLOCKS - 1):
                done_ref[u * BLOCK:(u + 1) * BLOCK, :] = dkv[u][BLOCK:] + dkv[u + 1][:BLOCK]
            carry_ref[...] = dkv[Q_BLOCKS - 1][BLOCK:]

        @pl.when(i == nsteps)
        def _():
            if held:
                dkv_ref[:held, :] = done_ref[...].astype(BF16)
            dkv_ref[held:, :] = carry_ref[...].astype(BF16)

    last = nsteps - 1
    blk = lambda w: pl.BlockSpec((tile, w), lambda i: (jnp.minimum(i, last), 0))
    outs = pl.pallas_call(
        body,
        name="attn_bwd",
        grid=(nsteps + 1,),
        in_specs=[
            pl.BlockSpec(memory_space=pltpu.SMEM),
            blk(aw),
            blk(kvw),
            pl.BlockSpec((BLOCK, kvw), lambda i: (jnp.clip(Q_BLOCKS * i - 1, 0, nb - 1), 0)),
            blk(aw),
            _full(biasm.shape),
        ] + [HBM_SPEC] * nr,
        out_specs=[
            blk(aw),
            pl.BlockSpec((tile, kvw), lambda i: (jnp.maximum(i - 1, 0), 0)),
            _resident(biasm.shape[1:]),
            _resident((8, LANES)),
        ] + [HBM_SPEC] * nr,
        out_shape=[
            jax.ShapeDtypeStruct((seq, aw), BF16),
            jax.ShapeDtypeStruct((seq, kvw), BF16),
            jax.ShapeDtypeStruct(biasm.shape[1:], F32),
            jax.ShapeDtypeStruct((8, LANES), F32),
        ] + [jax.ShapeDtypeStruct(g.shape, g.dtype) for g in ready],
        scratch_shapes=[
            pltpu.VMEM((BLOCK, kvw), F32),
            pltpu.VMEM((max(held, 8), kvw), F32),
            pltpu.VMEM((Q_BLOCKS, N_KV_HEADS, pairs_per_kv * BAND, LANES), BF16),
            pltpu.VMEM((Q_BLOCKS, N_KV_HEADS, pairs_per_kv * BAND, LANES), BF16),
            pltpu.VMEM((Q_BLOCKS, N_KV_HEADS, BAND, pairs_per_kv * BAND), BF16),
            pltpu.VMEM((Q_BLOCKS, N_KV_HEADS, BAND, pairs_per_kv * BAND), BF16),
        ] + _exchange_sems(nr),
        compiler_params=_params(("arbitrary",), 48),
    )(sinks, q, kv, kv, dattn, biasm, *ready)
    return outs[:4], outs[4:]


def _relbias_grad(dssum2, bucket_row, chunk):
    heads, n = dssum2.shape

    def body(a_ref, bucket_ref, out_ref):
        @pl.when(pl.program_id(0) == 0)
        def _():
            out_ref[...] = jnp.zeros_like(out_ref)

        a = a_ref[...]
        hi = a.astype(BF16)
        lo = (a - hi.astype(F32)).astype(BF16)
        onehot_t = (lax.broadcasted_iota(jnp.int32, (LANES, chunk), 0) == bucket_ref[...]).astype(F32).astype(BF16)
        out_ref[...] += _dot_nt(hi, onehot_t) + _dot_nt(lo, onehot_t)

    return pl.pallas_call(
        body,
        name="relbias_grad",
        grid=(n // chunk,),
        in_specs=[pl.BlockSpec((heads, chunk), lambda i: (0, i)), pl.BlockSpec((1, chunk), lambda i: (0, i))],
        out_specs=_resident((heads, LANES)),
        out_shape=jax.ShapeDtypeStruct((heads, LANES), F32),
        compiler_params=_params(("arbitrary",), 32),
    )(dssum2, bucket_row)


def _layer_b_in_bwd(dh2, dq, dz2, dkv, h1, ya, wbin_g, wkv, kvn, bpre, sm, ready, ts):
    seq, d = h1.shape
    aw = dq.shape[1]
    kvw = dkv.shape[1]
    cw = wbin_g.shape[2]
    per = aw // cw

    nr = len(ready)
    nt = seq // ts

    def body(dh2_ref, dq_ref, dz2_ref, dkv_ref, h1_ref, ya_ref, wbin_ref, wkv_ref, kvn_ref, bpre_ref, sm_ref, *refs):
        ready_refs, (dh1_ref, dya_ref, acc_ref) = refs[:nr], refs[nr:nr + 3]
        landed_refs, sems = refs[nr + 3:2 * nr + 3], refs[2 * nr + 3:]

        @pl.when(pl.program_id(0) == 0)
        def _():
            acc_ref[...] = jnp.zeros_like(acc_ref)
            _exchange_start(ready_refs, landed_refs, *sems, True)

        @pl.when(pl.program_id(0) == nt - 1)
        def _():
            _exchange_wait(ready_refs, landed_refs, *sems, True)

        dn4 = jnp.zeros((ts, d), F32)
        for j in range(N_DEV):
            src = dq_ref if j < per else dz2_ref
            jj = j % per
            dn4 = dn4 + _dot_nt(src[:, jj * cw:(jj + 1) * cw], wbin_ref[j])
        dn3 = _dot_nt(dkv_ref[...], wkv_ref[...])
        hn, r = _rms(h1_ref[...])
        acc_ref[0:1, :] += jnp.sum(dn4 * hn, axis=0, keepdims=True)
        acc_ref[1:2, :] += jnp.sum(dn3 * hn, axis=0, keepdims=True)
        dh1 = dh2_ref[...] + _rms_bwd(dn4 * bpre_ref[...] + dn3 * kvn_ref[...], hn, r)
        dh1_ref[...] = dh1
        yan, r2 = _rms(ya_ref[...])
        acc_ref[2:3, :] += jnp.sum(dh1 * yan, axis=0, keepdims=True)
        dya_ref[...] = _rms_bwd(dh1 * sm_ref[4:5, :], yan, r2).astype(BF16)

    outs = pl.pallas_call(
        body,
        name="layer_b_in_bwd",
        grid=(nt,),
        in_specs=[_rows(ts, d), _rows(ts, aw), _rows(ts, aw), _rows(ts, kvw), _rows(ts, d), _rows(ts, d),
                  _full(wbin_g.shape), _full(wkv.shape), _full(kvn.shape), _full(bpre.shape), _full(sm.shape)]
        + [HBM_SPEC] * nr,
        out_specs=[_rows(ts, d), _rows(ts, d), _resident((8, d))] + [HBM_SPEC] * nr,
        out_shape=[jax.ShapeDtypeStruct((seq, d), F32), jax.ShapeDtypeStruct((seq, d), BF16),
                   jax.ShapeDtypeStruct((8, d), F32)] + [jax.ShapeDtypeStruct(g.shape, g.dtype) for g in ready],
        scratch_shapes=_exchange_sems(nr),
        compiler_params=_params(("arbitrary",), 48),
    )(dh2, dq, dz2, dkv, h1, ya, wbin_g, wkv, kvn, bpre, sm, *ready)
    return outs[:3], outs[3:]


def _layer_a_bwd(dya, proj, conv, dh1, x2, wout, win_g, sm, ts):
    seq, d = x2.shape
    width = wout.shape[0]
    half = win_g.shape[2]
    n_half = width // half
    nt = seq // ts

    def body(dya_ref, proj_ref, conv_ref, dh1_ref, x_ref, wout_ref, win_ref, sm_ref, dproj_ref, gx_ref, acc_ref,
             dnext_ref):
        @pl.when(pl.program_id(0) == 0)
        def _():
            acc_ref[...] = jnp.zeros_like(acc_ref)
            dnext_ref[...] = jnp.zeros_like(dnext_ref)

        dy = _dot_nt(dya_ref[...], wout_ref[...])
        row = lax.broadcasted_iota(jnp.int32, (ts, half), 0)
        dn1 = jnp.zeros((ts, d), F32)
        for hh in range(n_half):
            cols = slice(hh * half, (hh + 1) * half)
            b, c, u, z = [proj_ref[:, (part * n_half + hh) * half:(part * n_half + hh + 1) * half].astype(F32)
                          for part in range(4)]
            cv = conv_ref[:, cols].astype(F32)
            dyh = dy[:, cols]
            sz, dsz = _silu(z)
            dconv = dyh * b * sz
            grads = [dyh * cv * sz, None, None, dyh * b * cv * dsz]
            next0, next1 = dnext_ref[0:1, cols], dnext_ref[1:2, cols]
            dc1 = jnp.where(row == ts - 1, next0, pltpu.roll(dconv, ts - 1, 0))
            dc2 = jnp.where(row == ts - 1, next1, jnp.where(row == ts - 2, next0, pltpu.roll(dconv, ts - 2, 0)))
            dnext_ref[:, cols] = dconv[0:8, :]
            v = c * u
            acc_ref[1:2, cols] += jnp.sum(dc2 * v, axis=0, keepdims=True)
            acc_ref[2:3, cols] += jnp.sum(dc1 * v, axis=0, keepdims=True)
            acc_ref[3:4, cols] += jnp.sum(dconv * v, axis=0, keepdims=True)
            dv = sm_ref[3:4, cols] * dconv + sm_ref[2:3, cols] * dc1 + sm_ref[1:2, cols] * dc2
            grads[1] = dv * u
            grads[2] = dv * c
            for part in range(4):
                j = part * n_half + hh
                gj = grads[part].astype(BF16)
                dproj_ref[:, j * half:(j + 1) * half] = gj
                dn1 = dn1 + _dot_nt(gj, win_ref[j])
        xn, r = _rms(x_ref[...])
        acc_ref[0:1, :] += jnp.sum(dn1 * xn, axis=0, keepdims=True)
        gx_ref[...] = dh1_ref[...] + _rms_bwd(dn1 * sm_ref[0:1, :], xn, r)

    rev = lambda w: pl.BlockSpec((ts, w), lambda i: (nt - 1 - i, 0))
    return pl.pallas_call(
        body,
        name="layer_a_bwd",
        grid=(nt,),
        in_specs=[rev(d), rev(4 * width), rev(width), rev(d), rev(d), _full(wout.shape), _full(win_g.shape), _full(sm.shape)],
        out_specs=[rev(4 * width), rev(d), _resident((8, d))],
        out_shape=[jax.ShapeDtypeStruct((seq, 4 * width), BF16), jax.ShapeDtypeStruct((seq, d), F32),
                   jax.ShapeDtypeStruct((8, d), F32)],
        scratch_shapes=[pltpu.VMEM((8, width), F32)],
        compiler_params=_params(("arbitrary",), 56),
    )(dya, proj, conv, dh1, x2, wout, win_g, sm)


def _wgrad(a, bs, n_slots, ts, name, ready=()):
    nr = len(ready)
    seq, k = a.shape
    nb_in = len(bs)
    n_each = bs[0].shape[1]
    n = nb_in * n_each
    bn = min(n_each, 1024)
    per_in = n_each // bn
    n_blocks = nb_in * per_in
    ns = seq // ts

    def b_spec(idx):
        def index(j, s):
            mine = j // per_in == idx
            row = jnp.where(mine, s, jnp.where(j // per_in > idx, ns - 1, 0))
            return (row, jnp.where(mine, j % per_in, jnp.where(j // per_in > idx, per_in - 1, 0)))
        return pl.BlockSpec((ts, bn), index)

    if n_slots:
        sw = n // n_slots
        spb = bn // sw
        out_shape = jax.ShapeDtypeStruct((n_slots, k, sw), BF16)
        out_spec = pl.BlockSpec((spb, k, sw), lambda j, s: (j, 0, 0))
    else:
        out_shape = jax.ShapeDtypeStruct((k, n), BF16)
        out_spec = pl.BlockSpec((k, bn), lambda j, s: (0, j))

    def body(a_ref, *refs):
        b_refs, ready_refs, o_ref = refs[:nb_in], refs[nb_in:nb_in + nr], refs[nb_in + nr]
        landed_refs, (acc_ref, *sems) = refs[nb_in + nr + 1:nb_in + 2 * nr + 1], refs[nb_in + 2 * nr + 1:]
        j, s = pl.program_id(0), pl.program_id(1)

        if nr:
            @pl.when(jnp.logical_and(j == 0, s == 0))
            def _():
                _exchange_start(ready_refs, landed_refs, *sems, True)

            @pl.when(jnp.logical_and(j == n_blocks - 1, s == ns - 1))
            def _():
                _exchange_wait(ready_refs, landed_refs, *sems, True)

        @pl.when(s == 0)
        def _():
            acc_ref[...] = jnp.zeros_like(acc_ref)

        for idx in range(nb_in):
            @pl.when(j // per_in == idx)
            def _(idx=idx):
                acc_ref[...] += _dot_tn(a_ref[...], b_refs[idx][...])

        @pl.when(s == ns - 1)
        def _():
            if n_slots:
                for e in range(spb):
                    o_ref[e] = acc_ref[:, e * sw:(e + 1) * sw].astype(BF16)
            else:
                o_ref[...] = acc_ref[...].astype(BF16)

    outs = pl.pallas_call(
        body,
        name=name,
        grid=(n_blocks, ns),
        in_specs=[pl.BlockSpec((ts, k), lambda j, s: (s, 0))] + [b_spec(idx) for idx in range(nb_in)] + [HBM_SPEC] * nr,
        out_specs=[out_spec] + [HBM_SPEC] * nr,
        out_shape=[out_shape] + [jax.ShapeDtypeStruct(g.shape, g.dtype) for g in ready],
        scratch_shapes=[pltpu.VMEM((k, bn), F32)] + (_exchange_sems(nr) if nr else []),
        compiler_params=_params(("arbitrary", "arbitrary"), 48),
    )(a, *bs, *ready)
    return (outs[0], outs[1:]) if nr else outs[0]


def _adamw(ws, gs, ms, vs):
    n = len(ws)

    def step(w, g, m, v):
        m = ADAM_B1 * m + (1.0 - ADAM_B1) * g
        v = ADAM_B2 * v + (1.0 - ADAM_B2) * jnp.square(g)
        m_hat = m / (1.0 - ADAM_B1 ** ADAM_STEP)
        v_hat = v / (1.0 - ADAM_B2 ** ADAM_STEP)
        return -ADAM_LR * (m_hat / (jnp.sqrt(v_hat) + ADAM_EPS) + ADAM_WD * w), m, v

    def body(*refs):
        w_refs, g_refs, m_refs, v_refs = (refs[k * n:(k + 1) * n] for k in range(4))
        d_refs, nm_refs, nv_refs = (refs[(4 + k) * n:(5 + k) * n] for k in range(3))
        for t in range(n):
            rows = w_refs[t].shape[0]
            if rows <= 128:
                d_refs[t][...], nm_refs[t][...], nv_refs[t][...] = step(
                    w_refs[t][...], g_refs[t][...], m_refs[t][...], v_refs[t][...])
                continue
            chunk = 128

            def one(i, carry, t=t):
                r = pl.ds(pl.multiple_of(i * chunk, chunk), chunk)
                d_refs[t][r, :], nm_refs[t][r, :], nv_refs[t][r, :] = step(
                    w_refs[t][r, :], g_refs[t][r, :], m_refs[t][r, :], v_refs[t][r, :])
                return carry

            lax.fori_loop(0, rows // chunk, one, 0)

    vmem = pl.BlockSpec(memory_space=pltpu.VMEM)
    outs = pl.pallas_call(
        body,
        name="adamw",
        in_specs=[vmem] * (4 * n),
        out_specs=[vmem] * (3 * n),
        out_shape=[jax.ShapeDtypeStruct(w.shape, F32) for w in ws] * 3,
        compiler_params=_params(vmem_mib=56),
    )(*ws, *gs, *ms, *vs)
    return outs[:n], outs[n:2 * n], outs[2 * n:]


def _band_structure():
    q_loc = jnp.arange(BLOCK, dtype=jnp.int32)[:, None]
    s_loc = jnp.arange(2 * BLOCK, dtype=jnp.int32)[None, :]
    dist = q_loc + BLOCK - s_loc
    in_window = (dist >= 0) & (dist < BLOCK)
    dd = jnp.maximum(dist, 0)
    max_exact = N_BUCKETS // 2
    large = max_exact + (jnp.log(jnp.maximum(dd, 1).astype(F32) / max_exact) / math.log(MAX_DISTANCE / max_exact)
                         * (N_BUCKETS - max_exact)).astype(jnp.int32)
    bucket = jnp.where(dd < max_exact, dd, jnp.minimum(large, N_BUCKETS - 1))
    return bucket, in_window.astype(jnp.int32)


def _place_rows(a, row, rows=8):
    return jnp.pad(a, ((row, rows - row - a.shape[0]), (0, 0)))


def kernel(x, a_pre_norm, a_w_in, a_conv_w, a_w_out, a_post_norm, kv_norm, w_kv, rel_bias, b_pre_norm, b_w_in, b_sinks, b_w_out, b_post_norm, loss_target, m_a_pre_norm, m_a_w_in, m_a_conv_w, m_a_w_out, m_a_post_norm, m_kv_norm, m_w_kv, m_rel_bias, m_b_pre_norm, m_b_w_in, m_b_sinks, m_b_w_out, m_b_post_norm, v_a_pre_norm, v_a_w_in, v_a_conv_w, v_a_w_out, v_a_post_norm, v_kv_norm, v_w_kv, v_rel_bias, v_b_pre_norm, v_b_w_in, v_b_sinks, v_b_w_out, v_b_post_norm):
    seq, d = x.shape[1], x.shape[2]
    x2 = x.reshape(seq, d)
    target = loss_target.reshape(seq, d)
    shard = a_pre_norm.shape[1]
    me = _my_index()
    ts_a = min(seq, 512)
    ts = min(seq, 512)
    ts_w = min(seq, 2048)

    small = _place_rows(a_pre_norm, 0) + _place_rows(a_conv_w[0], 1) + _place_rows(a_post_norm, 4)
    win_g, wout_g, small_g = _all_gather([a_w_in[0], a_w_out[0], small], [BF16, BF16, F32])
    wout = wout_g.reshape(-1, wout_g.shape[2])
    sm = small_g.transpose(1, 0, 2).reshape(8, N_DEV * shard)
    kvn = kv_norm.reshape(1, d)

    (h1, n1, proj, conv, y, ya), (wkv_g, wbin_g, wbout_g) = _layer_a_fwd(
        x2, sm, win_g, wout, [w_kv.astype(BF16), b_w_in[0].astype(BF16), b_w_out[0].astype(BF16)], ts_a)
    wkv = wkv_g.reshape(-1, wkv_g.shape[2])
    wbout = wbout_g.reshape(-1, wbout_g.shape[2])
    bucket, in_window = _band_structure()
    biasm = _bias_table(rel_bias, bucket.T, in_window.T)
    n3, n4, kv, q, o, dh2, dyb, dattn, dz2, acc_c = _layer_b_fwd(
        h1, target, kvn, b_pre_norm, wkv, wbin_g, biasm, b_sinks, wbout, b_post_norm)

    g_wbout = _wgrad(o, [dyb], 0, ts_w, "wgrad_b_out").reshape(wbout_g.shape)
    (dq, dkv, dssum, dsink), (l_wbout,) = _attn_bwd(q, kv, dattn, biasm, b_sinks, [g_wbout])
    by_head = dssum.reshape(N_PAIRS, BAND, 2, BLOCK).transpose(0, 2, 3, 1)
    relb = _relbias_grad(by_head.reshape(N_Q_HEADS, -1), bucket.reshape(1, -1), 4096)
    g_wkv = _wgrad(n3, [dkv], 0, ts_w, "wgrad_kv").reshape(wkv_g.shape)
    g_wbin = _wgrad(n4, [dq, dz2], N_DEV, ts_w, "wgrad_b_in")
    (dh1, dya, acc_b), (l_wkv, l_wbin) = _layer_b_in_bwd(
        dh2, dq, dz2, dkv, h1, ya, wbin_g, wkv, kvn, b_pre_norm, sm, [g_wkv, g_wbin], ts)
    dproj, gx, acc_a = _layer_a_bwd(dya, proj, conv, dh1, x2, wout, win_g, sm, ts_a)
    g_wout = _wgrad(y, [dya], 0, ts_w, "wgrad_a_out").reshape(wout_g.shape)
    g_win, (l_wout,) = _wgrad(n1, [dproj], N_DEV, ts_w, "wgrad_a_in", [g_wout])

    r_win, (r_wout, r_wkv, r_wbin, r_wbout), (s_a, s_b, s_c, s_relb, s_sink) = _reduce_exchange(
        g_win, [l_wout, l_wkv, l_wbin, l_wbout], [acc_a, acc_b, acc_c, relb, dsink])
    mine = lambda rows: lax.dynamic_slice_in_dim(rows, me * shard, shard, axis=1)
    loss = s_c[1, 0]
    weights = [a_pre_norm, a_w_in[0], a_conv_w[0], a_w_out[0], a_post_norm, kvn, w_kv, rel_bias, b_pre_norm,
               b_w_in[0], b_sinks, b_w_out[0], b_post_norm]
    grads = [mine(s_a[0:1]), r_win, mine(s_a[1:4]), r_wout, mine(s_b[2:3]), s_b[1:2], r_wkv,
             s_relb[:, :N_BUCKETS].T, s_b[0:1], r_wbin, s_sink[0:1, :N_Q_HEADS], r_wbout, s_c[0:1]]
    first = [m_a_pre_norm, m_a_w_in[0], m_a_conv_w[0], m_a_w_out[0], m_a_post_norm, m_kv_norm.reshape(1, d), m_w_kv,
             m_rel_bias, m_b_pre_norm, m_b_w_in[0], m_b_sinks, m_b_w_out[0], m_b_post_norm]
    second = [v_a_pre_norm, v_a_w_in[0], v_a_conv_w[0], v_a_w_out[0], v_a_post_norm, v_kv_norm.reshape(1, d), v_w_kv,
              v_rel_bias, v_b_pre_norm, v_b_w_in[0], v_b_sinks, v_b_w_out[0], v_b_post_norm]
    deltas, new_m, new_v = _adamw(weights, grads, first, second)

    shapes = [a_pre_norm.shape, a_w_in.shape, a_conv_w.shape, a_w_out.shape, a_post_norm.shape, kv_norm.shape,
              w_kv.shape, rel_bias.shape, b_pre_norm.shape, b_w_in.shape, b_sinks.shape, b_w_out.shape, b_post_norm.shape]
    shaped = lambda arrays: [a.reshape(s) for a, s in zip(arrays, shapes)]
    return (loss, gx.reshape(x.shape), *shaped(grads), *shaped(deltas), *shaped(new_m), *shaped(new_v))
```

```python
import functools
import math

import jax
import jax.numpy as jnp
from jax import lax
from jax.experimental import pallas as pl
from jax.experimental.pallas import tpu as pltpu

HEAD_DIM = 64
N_Q_HEADS = 16
N_KV_HEADS = 2
GROUP = N_Q_HEADS // N_KV_HEADS
BLOCK = 128
N_BUCKETS = 32
MAX_DISTANCE = 128
EPS = 1e-6
NEG_INF = -1e30
SCALE = HEAD_DIM ** -0.5

ADAM_LR = 0.001
ADAM_B1 = 0.9
ADAM_B2 = 0.999
ADAM_EPS = 1e-08
ADAM_WD = 0.01
ADAM_STEP = 10

N_DEV = 8
LANES = 128
F32 = jnp.float32
BF16 = jnp.bfloat16
MESH = pl.DeviceIdType.MESH
MIB = 1024 * 1024


def _params(semantics=None, vmem_mib=48):
    return pltpu.CompilerParams(dimension_semantics=semantics, vmem_limit_bytes=vmem_mib * MIB)


def _full(shape):
    zeros = (0,) * len(shape)
    return pl.BlockSpec(shape, lambda *_: zeros, pipeline_mode=pl.Buffered(1))


def _resident(shape):
    zeros = (0,) * len(shape)
    return pl.BlockSpec(shape, lambda *_: zeros)


def _rows(ts, cols):
    return pl.BlockSpec((ts, cols), lambda i: (i, 0))


def _dot(a, b):
    return jnp.dot(a, b, preferred_element_type=F32)


def _dot_nt(a, b):
    return lax.dot_general(a, b, (((1,), (1,)), ((), ())), preferred_element_type=F32)


def _dot_tn(a, b):
    return lax.dot_general(a, b, (((0,), (0,)), ((), ())), preferred_element_type=F32)


def _rms(xf):
    r = lax.rsqrt(jnp.mean(xf * xf, axis=-1, keepdims=True) + EPS)
    return xf * r, r


def _rms_bwd(dn, xn, r):
    return r * (dn - xn * jnp.mean(dn * xn, axis=-1, keepdims=True))


def _silu(z):
    s = jax.nn.sigmoid(z)
    return z * s, s * (1.0 + z * (1.0 - s))


def _my_index():
    return 4 * lax.axis_index("x") + 2 * lax.axis_index("y") + lax.axis_index("c")


def _all_gather(shards, out_dtypes):
    n = len(shards)

    def body(*refs):
        ins, outs = refs[:n], refs[n:2 * n]
        send_sems, recv_sems = refs[2 * n], refs[2 * n + 1]
        x, y, c = lax.axis_index("x"), lax.axis_index("y"), lax.axis_index("c")
        me, sibling = (x, y, c), (x, y, 1 - c)
        x_nbr, y_nbr, diagonal = (1 - x, y), (x, 1 - y), (1 - x, 1 - y)
        south = c == 0
        relayed = (jnp.where(south, 1 - x, x), jnp.where(south, y, 1 - y))
        relay_to = (jnp.where(south, x, 1 - x), jnp.where(south, 1 - y, y))

        def copy(t, k, block, to):
            rows = outs[t].at[4 * block[0] + 2 * block[1] + block[2]]
            return pltpu.make_async_remote_copy(
                src_ref=rows, dst_ref=rows, send_sem=send_sems.at[t, k], recv_sem=recv_sems.at[t, k],
                device_id=to, device_id_type=MESH)

        for t in range(n):
            outs[t][pl.ds(_my_index(), 1)] = ins[t][...].astype(outs[t].dtype)[None]
        started = []

        def start(cp):
            cp.start()
            started.append(cp)

        for t in range(n):
            start(copy(t, 0, me, sibling))
            start(copy(t, 1, me, (*x_nbr, c)))
            start(copy(t, 2, me, (*y_nbr, c)))
        for k, chip in ((1, x_nbr), (2, y_nbr)):
            for t in range(n):
                copy(t, k, (*chip, c), me).wait_recv()
                start(copy(t, 3 + k, (*chip, c), sibling))
        for t in range(n):
            start(copy(t, 3, (*relayed, c), (*relay_to, c)))
        for t in range(n):
            copy(t, 3, (*diagonal, c), me).wait_recv()
            start(copy(t, 6, (*diagonal, c), sibling))
        for t in range(n):
            copy(t, 0, sibling, me).wait_recv()
        for k, chip in ((4, x_nbr), (5, y_nbr), (6, diagonal)):
            for t in range(n):
                copy(t, k, (*chip, 1 - c), me).wait_recv()
        for cp in started:
            cp.wait_send()

    vmem = pl.BlockSpec(memory_space=pltpu.VMEM)
    return pl.pallas_call(
        body,
        name="gather_weights",
        out_shape=[jax.ShapeDtypeStruct((N_DEV,) + s.shape, dt) for s, dt in zip(shards, out_dtypes)],
        in_specs=[vmem] * n,
        out_specs=[vmem] * n,
        scratch_shapes=[pltpu.SemaphoreType.DMA((n, 7)), pltpu.SemaphoreType.DMA((n, 7))],
        compiler_params=_params(vmem_mib=48),
    )(*shards)


def _peer(k):
    x, y, c = lax.axis_index("x"), lax.axis_index("y"), lax.axis_index("c")
    px = 1 - x if k & 4 else x
    py = 1 - y if k & 2 else y
    pc = 1 - c if k & 1 else c
    return (px, py, pc), 4 * px + 2 * py + pc


def _exchange(srcs, dsts, send_sems, recv_sems, local_sems, scatter):
    me = _my_index()
    sends, arrivals = [], []
    for k in range(1, N_DEV):
        peer, pidx = _peer(k)
        for t, (src, dst) in enumerate(zip(srcs, dsts)):
            mine = src.at[pidx] if scatter else src
            sems = dict(send_sem=send_sems.at[t, k - 1], recv_sem=recv_sems.at[t, k - 1], device_id=peer, device_id_type=MESH)
            sends.append(pltpu.make_async_remote_copy(src_ref=mine, dst_ref=dst.at[me], **sems))
            arrivals.append(pltpu.make_async_remote_copy(src_ref=mine, dst_ref=dst.at[pidx], **sems))
    local = [pltpu.make_async_copy(src.at[me] if scatter else src, dst.at[me], local_sems.at[t])
             for t, (src, dst) in enumerate(zip(srcs, dsts))]
    return sends, arrivals, local


def _exchange_start(*args):
    sends, _, local = _exchange(*args)
    for cp in sends + local:
        cp.start()


def _exchange_wait(*args):
    sends, arrivals, local = _exchange(*args)
    for cp in arrivals:
        cp.wait_recv()
    for cp in sends:
        cp.wait_send()
    for cp in local:
        cp.wait()


def _exchange_sems(n):
    return [pltpu.SemaphoreType.DMA((n, N_DEV - 1)), pltpu.SemaphoreType.DMA((n, N_DEV - 1)), pltpu.SemaphoreType.DMA((n,))]


HBM_SPEC = pl.BlockSpec(memory_space=pl.ANY)


def _sum_slots(recv_ref, out_ref):
    rows = out_ref.shape[0]
    chunk = min(rows, 128)

    def add(i, carry):
        r0 = pl.multiple_of(i * chunk, chunk)
        acc = recv_ref[0, pl.ds(r0, chunk), :].astype(F32)
        for dev in range(1, N_DEV):
            acc = acc + recv_ref[dev, pl.ds(r0, chunk), :].astype(F32)
        out_ref[pl.ds(r0, chunk), :] = acc
        return carry

    lax.fori_loop(0, rows // chunk, add, 0)


N_CHIPS = N_DEV // 2


def _rows_loop(rows, fn):
    chunk = min(rows, 128)

    def step(i, carry):
        fn(pl.ds(pl.multiple_of(i * chunk, chunk), chunk))
        return carry

    lax.fori_loop(0, rows // chunk, step, 0)


def _chip_reduce(g_ref, out_ref, sib_ref, chip_ref, send_ref, sems, between):
    sib_send, sib_recv, chip_send, chip_recv = sems
    x, y, c = lax.axis_index("x"), lax.axis_index("y"), lax.axis_index("c")
    my_chip = 2 * x + y
    rows = out_ref.shape[0]

    def chip_of(k):
        cx = 1 - x if k & 2 else x
        cy = 1 - y if k & 1 else y
        return (cx, cy), 2 * cx + cy

    def to_sibling(t):
        return pltpu.make_async_remote_copy(
            src_ref=g_ref.at[2 * t + 1 - c], dst_ref=sib_ref.at[t], send_sem=sib_send.at[t], recv_sem=sib_recv.at[t],
            device_id=(x, y, 1 - c), device_id_type=MESH)

    def to_chip(k):
        (cx, cy), t = chip_of(k)
        return t, pltpu.make_async_remote_copy(
            src_ref=send_ref.at[k - 1], dst_ref=chip_ref.at[my_chip], send_sem=chip_send.at[k - 1],
            recv_sem=chip_recv.at[k - 1], device_id=(cx, cy, c), device_id_type=MESH)

    def from_chip(k):
        _, t = chip_of(k)
        return pltpu.make_async_remote_copy(
            src_ref=send_ref.at[k - 1], dst_ref=chip_ref.at[t], send_sem=chip_send.at[k - 1],
            recv_sem=chip_recv.at[k - 1], device_id=(x, y, c), device_id_type=MESH)

    for t in range(N_CHIPS):
        to_sibling(t).start()
    for t in range(N_CHIPS):
        to_sibling(t).wait_recv()

    def pair_sum(t, r):
        return g_ref[2 * t + c, r, :].astype(F32) + sib_ref[t, r, :].astype(F32)

    for k in (3, 1, 2):
        t, cp = to_chip(k)

        def fill(r, t=t, k=k):
            send_ref[k - 1, r, :] = pair_sum(t, r).astype(BF16)

        _rows_loop(rows, fill)
        cp.start()
    between()

    def own(r):
        chip_ref[my_chip, r, :] = pair_sum(my_chip, r).astype(BF16)

    _rows_loop(rows, own)
    for k in range(1, N_CHIPS):
        from_chip(k).wait_recv()

    def total(r):
        acc = chip_ref[0, r, :].astype(F32)
        for t in range(1, N_CHIPS):
            acc = acc + chip_ref[t, r, :].astype(F32)
        out_ref[r, :] = acc

    _rows_loop(rows, total)
    for t in range(N_CHIPS):
        to_sibling(t).wait_send()
    for k in range(1, N_CHIPS):
        to_chip(k)[1].wait_send()


def _reduce_exchange(part, landed, smalls):
    nl, ng = len(landed), len(smalls)
    n_out = 1 + nl + ng

    def body(*refs):
        p_in, l_in, s_in = refs[0], refs[1:1 + nl], refs[1 + nl:n_out]
        p_out, l_out, s_out = refs[n_out], refs[n_out + 1:n_out + 1 + nl], refs[n_out + 1 + nl:2 * n_out]
        scratch = refs[2 * n_out:]
        s_recv, (sib_ref, chip_ref, send_ref), sems = scratch[:ng], scratch[ng:ng + 3], scratch[ng + 3:]

        def between():
            for t in range(nl):
                _sum_slots(l_in[t], l_out[t])

        _exchange_start(s_in, s_recv, *sems[4:], False)
        _chip_reduce(p_in, p_out, sib_ref, chip_ref, send_ref, sems[:4], between)
        _exchange_wait(s_in, s_recv, *sems[4:], False)
        for t in range(ng):
            acc = s_recv[t][0]
            for dev in range(1, N_DEV):
                acc = acc + s_recv[t][dev]
            s_out[t][...] = acc

    vmem = pl.BlockSpec(memory_space=pltpu.VMEM)
    slot = part.shape[1:]
    outs = pl.pallas_call(
        body,
        name="reduce_grads",
        out_shape=[jax.ShapeDtypeStruct(p.shape[1:], F32) for p in [part] + landed]
        + [jax.ShapeDtypeStruct(s.shape, F32) for s in smalls],
        in_specs=[vmem] * n_out,
        out_specs=[vmem] * n_out,
        scratch_shapes=[pltpu.VMEM((N_DEV,) + s.shape, F32) for s in smalls]
        + [pltpu.VMEM((N_CHIPS,) + slot, BF16), pltpu.VMEM((N_CHIPS,) + slot, BF16), pltpu.VMEM((N_CHIPS - 1,) + slot, BF16)]
        + [pltpu.SemaphoreType.DMA((N_CHIPS,)), pltpu.SemaphoreType.DMA((N_CHIPS,)),
           pltpu.SemaphoreType.DMA((N_CHIPS - 1,)), pltpu.SemaphoreType.DMA((N_CHIPS - 1,))]
        + _exchange_sems(ng),
        compiler_params=_params(vmem_mib=56),
    )(part, *landed, *smalls)
    return outs[0], outs[1:1 + nl], outs[1 + nl:]


def _layer_a_fwd(x2, sm, win_g, wout, later, ts):
    seq, d = x2.shape
    width = wout.shape[0]
    half = win_g.shape[2]
    n_half = width // half
    nl = len(later)
    nt = seq // ts

    def body(x_ref, sm_ref, win_ref, wout_ref, *refs):
        shard_refs, refs = refs[:nl], refs[nl:]
        h1_ref, n1_ref, proj_ref, conv_ref, y_ref, ya_ref = refs[:6]
        gathered_refs, (vprev_ref, *sems) = refs[6:6 + nl], refs[6 + nl:]

        @pl.when(pl.program_id(0) == 0)
        def _():
            vprev_ref[...] = jnp.zeros_like(vprev_ref)
            _exchange_start(shard_refs, gathered_refs, *sems, False)

        @pl.when(pl.program_id(0) == nt - 1)
        def _():
            _exchange_wait(shard_refs, gathered_refs, *sems, False)

        xf = x_ref[...]
        xn, _ = _rms(xf)
        n1 = (xn * sm_ref[0:1, :]).astype(BF16)
        n1_ref[...] = n1
        row = lax.broadcasted_iota(jnp.int32, (ts, half), 0)
        ya = jnp.zeros((ts, d), F32)
        for hh in range(n_half):
            cols = slice(hh * half, (hh + 1) * half)
            parts = []
            for part in range(4):
                j = part * n_half + hh
                pj = _dot(n1, win_ref[j])
                proj_ref[:, j * half:(j + 1) * half] = pj.astype(BF16)
                parts.append(pj)
            b, c, u, z = parts
            v = c * u
            last1, last2 = vprev_ref[7:8, cols], vprev_ref[6:7, cols]
            v1 = jnp.where(row == 0, last1, pltpu.roll(v, 1, 0))
            v2 = jnp.where(row == 0, last2, jnp.where(row == 1, last1, pltpu.roll(v, 2, 0)))
            vprev_ref[:, cols] = v[ts - 8:ts, :]
            conv = sm_ref[1:2, cols] * v2 + sm_ref[2:3, cols] * v1 + sm_ref[3:4, cols] * v
            conv_ref[:, cols] = conv.astype(BF16)
            yh = (b * conv * _silu(z)[0]).astype(BF16)
            y_ref[:, cols] = yh
            ya = ya + _dot(yh, wout_ref[cols, :])
        ya_ref[...] = ya
        h1_ref[...] = xf + _rms(ya)[0] * sm_ref[4:5, :]

    outs = pl.pallas_call(
        body,
        name="layer_a_fwd",
        grid=(nt,),
        in_specs=[_rows(ts, d), _full(sm.shape), _full(win_g.shape), _full(wout.shape)] + [HBM_SPEC] * nl,
        out_specs=[_rows(ts, d), _rows(ts, d), _rows(ts, 4 * width), _rows(ts, width), _rows(ts, width), _rows(ts, d)]
        + [HBM_SPEC] * nl,
        out_shape=[
            jax.ShapeDtypeStruct((seq, d), F32),
            jax.ShapeDtypeStruct((seq, d), BF16),
            jax.ShapeDtypeStruct((seq, 4 * width), BF16),
            jax.ShapeDtypeStruct((seq, width), BF16),
            jax.ShapeDtypeStruct((seq, width), BF16),
            jax.ShapeDtypeStruct((seq, d), F32),
        ] + [jax.ShapeDtypeStruct((N_DEV,) + s.shape, s.dtype) for s in later],
        scratch_shapes=[pltpu.VMEM((8, width), F32)] + _exchange_sems(nl),
        compiler_params=_params(("arbitrary",), 56),
    )(x2, sm, win_g, wout, *later)
    return outs[:6], outs[6:]


N_PAIRS = N_Q_HEADS // 2
BAND = 2 * BLOCK


def _bias_table(rel_bias_t, bucket_t, in_window_t):
    def body(rb_ref, bucket_ref, win_ref, out_ref):
        bk = jnp.where(win_ref[...] != 0, bucket_ref[...], -1)
        has_prev = lax.broadcasted_iota(jnp.int32, bk.shape, 0) >= BLOCK
        for h in range(N_Q_HEADS):
            acc = jnp.full(bk.shape, NEG_INF, F32)
            for b in range(N_BUCKETS):
                acc = jnp.where(bk == b, rb_ref[h, b], acc)
            cols = slice((h % 2) * BLOCK, (h % 2 + 1) * BLOCK)
            out_ref[1, h // 2, :, cols] = acc
            out_ref[0, h // 2, :, cols] = jnp.where(has_prev, acc, NEG_INF)

    vmem = pl.BlockSpec(memory_space=pltpu.VMEM)
    return pl.pallas_call(
        body,
        name="bias_table",
        in_specs=[pl.BlockSpec(memory_space=pltpu.SMEM), vmem, vmem],
        out_specs=vmem,
        out_shape=jax.ShapeDtypeStruct((2, N_PAIRS, BAND, 2 * BLOCK), F32),
    )(rel_bias_t, bucket_t, in_window_t)


Q_BLOCKS = 4


def _banded_tiles(kvp_ref, kvc_ref):
    tile = kvc_ref[...].astype(F32)
    blocks = [kvp_ref[...].astype(F32)] + [tile[u * BLOCK:(u + 1) * BLOCK] for u in range(Q_BLOCKS)]
    return [_banded_kv(blocks[u], blocks[u + 1]) for u in range(Q_BLOCKS)]


def _bias_of(bias_ref, i, u, m):
    return bias_ref[jnp.minimum(i, 1) if u == 0 else 1, m]


def _banded_kv(kvp, kvc):
    kw = N_KV_HEADS * HEAD_DIM
    out = []
    for full in (jnp.concatenate([kvp[:, :kw], kvc[:, :kw]], axis=0), jnp.concatenate([kvp[:, kw:], kvc[:, kw:]], axis=0)):
        lo = lax.broadcasted_iota(jnp.int32, full.shape, 1) < HEAD_DIM
        rolled = pltpu.roll(full, HEAD_DIM, 1)
        x2 = [jnp.where(lo, full, rolled).astype(BF16), jnp.where(lo, rolled, full).astype(BF16)]
        ft = full.T
        x2t = [jnp.concatenate([ft[kh * HEAD_DIM:(kh + 1) * HEAD_DIM]] * 2, axis=0).astype(BF16) for kh in range(N_KV_HEADS)]
        out += [x2, x2t]
    return out


def _pair_rows(ref, rows, m, scale=None):
    both = ref[rows, m * LANES:(m + 1) * LANES].astype(F32)
    if scale is not None:
        both = both * scale
    lo = lax.broadcasted_iota(jnp.int32, both.shape, 1) < HEAD_DIM
    zero = jnp.zeros_like(both)
    return jnp.concatenate([jnp.where(lo, both, zero), jnp.where(lo, zero, both)], axis=0).astype(BF16)


def _pair_cols(res_t):
    top = lax.broadcasted_iota(jnp.int32, (LANES, BLOCK), 0) < HEAD_DIM
    return jnp.where(top, res_t[:, :BLOCK], res_t[:, BLOCK:]).T


def _sink_row(sink_ref, m):
    first = lax.broadcasted_iota(jnp.int32, (1, 2 * BLOCK), 1) < BLOCK
    return jnp.where(first, sink_ref[0, 2 * m], sink_ref[0, 2 * m + 1])


def _probs_t(k2, qpair, bias, sink):
    return _softmax_t(_dot_nt(k2, qpair) + bias, sink)


def _softmax_t(logits, sink):
    mx = jnp.maximum(jnp.max(logits, axis=0, keepdims=True), sink)
    p = jnp.exp(logits - mx)
    sink_p = jnp.exp(sink - mx)
    inv = 1.0 / (jnp.sum(p, axis=0, keepdims=True) + sink_p)
    return p * inv, sink_p * inv


def _layer_b_fwd(h1, target, kvn, bpre, wkv, wbin_g, biasm, sinks, wbout, bpost):
    seq, d = h1.shape
    kvw = wkv.shape[1]
    cw = wbin_g.shape[2]
    aw = N_Q_HEADS * HEAD_DIM
    per = aw // cw
    tile = Q_BLOCKS * BLOCK

    def body(sink_ref, h1_ref, tgt_ref, kvn_ref, bpre_ref, wkv_ref, wbin_ref, bias_ref, w_ref, g_ref,
             n3_ref, n4_ref, kvc_ref, q_ref, o_ref, dh2_ref, dyb_ref, dattn_ref, dz2_ref, acc_ref,
             attn_ref, z2_ref, kvp_ref):
        i = pl.program_id(0)

        @pl.when(i == 0)
        def _():
            acc_ref[...] = jnp.zeros_like(acc_ref)
            kvp_ref[...] = jnp.zeros_like(kvp_ref)

        hn, _ = _rms(h1_ref[...])
        n3 = (hn * kvn_ref[...]).astype(BF16)
        n4 = (hn * bpre_ref[...]).astype(BF16)
        n3_ref[...] = n3
        n4_ref[...] = n4
        kvc_ref[...] = _dot(n3, wkv_ref[...]).astype(BF16)
        for j in range(N_DEV):
            pj = _dot(n4, wbin_ref[j])
            if j < per:
                q_ref[:, j * cw:(j + 1) * cw] = pj.astype(BF16)
            else:
                z2_ref[:, (j - per) * cw:(j - per + 1) * cw] = pj

        banded = _banded_tiles(kvp_ref, kvc_ref)
        kvp_ref[...] = kvc_ref[tile - BLOCK:tile, :]
        units = [(u, m) for u in range(Q_BLOCKS) for m in range(N_PAIRS)]
        kv_of = lambda m: (2 * m) // GROUP
        logits, probs = {}, {}
        for step in range(len(units) + 2):
            if step < len(units):
                u, m = units[step]
                qpair = _pair_rows(q_ref, slice(u * BLOCK, (u + 1) * BLOCK), m, SCALE)
                logits[step] = _dot_nt(banded[u][0][kv_of(m)], qpair) + _bias_of(bias_ref, i, u, m)
            if 0 <= step - 1 < len(units):
                u, m = units[step - 1]
                probs[step - 1] = _softmax_t(logits.pop(step - 1), _sink_row(sink_ref, m))[0].astype(BF16)
            if 0 <= step - 2 < len(units):
                u, m = units[step - 2]
                out_t = _dot(banded[u][3][kv_of(m)], probs.pop(step - 2))
                attn_ref[u * BLOCK:(u + 1) * BLOCK, m * LANES:(m + 1) * LANES] = _pair_cols(out_t)
        attn = attn_ref[...]
        sz, dsz = _silu(z2_ref[...])
        o = (attn * sz).astype(BF16)
        o_ref[...] = o

        w = w_ref[...]
        yb = _dot(o, w)
        ybn, r = _rms(yb)
        g = g_ref[...]
        diff = h1_ref[...] + ybn * g - tgt_ref[...]
        dh2 = diff * (1.0 / d)
        dh2_ref[...] = dh2
        acc_ref[0:1, :] += jnp.sum(dh2 * ybn, axis=0, keepdims=True)
        tok = jnp.mean(diff * diff, axis=-1, keepdims=True)
        acc_ref[1:2, :] += 0.5 * jnp.sum(tok, axis=0, keepdims=True)
        dyb = _rms_bwd(dh2 * g, ybn, r).astype(BF16)
        dyb_ref[...] = dyb
        do = _dot_nt(dyb, w)
        dattn_ref[...] = (do * sz).astype(BF16)
        dz2_ref[...] = (do * attn * dsz).astype(BF16)

    blk = lambda w: pl.BlockSpec((tile, w), lambda i: (i, 0))
    return pl.pallas_call(
        body,
        name="layer_b_fwd",
        grid=(seq // tile,),
        in_specs=[
            pl.BlockSpec(memory_space=pltpu.SMEM),
            blk(d),
            blk(d),
            _full(kvn.shape),
            _full(bpre.shape),
            _full(wkv.shape),
            _full(wbin_g.shape),
            _full(biasm.shape),
            _full(wbout.shape),
            _full(bpost.shape),
        ],
        out_specs=[blk(d), blk(d), blk(kvw), blk(aw), blk(aw), blk(d), blk(d), blk(aw), blk(aw), _resident((8, d))],
        out_shape=[
            jax.ShapeDtypeStruct((seq, d), BF16),
            jax.ShapeDtypeStruct((seq, d), BF16),
            jax.ShapeDtypeStruct((seq, kvw), BF16),
            jax.ShapeDtypeStruct((seq, aw), BF16),
            jax.ShapeDtypeStruct((seq, aw), BF16),
            jax.ShapeDtypeStruct((seq, d), F32),
            jax.ShapeDtypeStruct((seq, d), BF16),
            jax.ShapeDtypeStruct((seq, aw), BF16),
            jax.ShapeDtypeStruct((seq, aw), BF16),
            jax.ShapeDtypeStruct((8, d), F32),
        ],
        scratch_shapes=[pltpu.VMEM((tile, aw), F32), pltpu.VMEM((tile, aw), F32), pltpu.VMEM((BLOCK, kvw), BF16)],
        compiler_params=_params(("arbitrary",), 56),
    )(sinks, h1, target, kvn, bpre, wkv, wbin_g, biasm, wbout, bpost)


def _attn_bwd(q, kv, dattn, biasm, sinks, ready):
    seq, aw = q.shape
    kvw = kv.shape[1]
    kw = N_KV_HEADS * HEAD_DIM
    nb = seq // BLOCK
    pairs_per_kv = N_PAIRS // N_KV_HEADS
    nr = len(ready)

    tile = Q_BLOCKS * BLOCK
    nsteps = seq // tile
    held = (Q_BLOCKS - 1) * BLOCK

    def body(sink_ref, q_ref, kvc_ref, kvp_ref, da_ref, bias_ref, *refs):
        ready_refs, (dq_ref, dkv_ref, dssum_ref, dsink_ref) = refs[:nr], refs[nr:nr + 4]
        landed_refs, scratch = refs[nr + 4:2 * nr + 4], refs[2 * nr + 4:]
        carry_ref, done_ref, qs_ref, dos_ref, dst_ref, pt_ref, *sems = scratch
        i = pl.program_id(0)

        @pl.when(i == 0)
        def _():
            dssum_ref[...] = jnp.zeros_like(dssum_ref)
            dsink_ref[...] = jnp.zeros_like(dsink_ref)
            carry_ref[...] = jnp.zeros_like(carry_ref)
            done_ref[...] = jnp.zeros_like(done_ref)
            _exchange_start(ready_refs, landed_refs, *sems, True)

        @pl.when(i == nsteps)
        def _():
            _exchange_wait(ready_refs, landed_refs, *sems, True)

        @pl.when(i < nsteps)
        def _():
            lo = lax.broadcasted_iota(jnp.int32, (BAND, LANES), 1) < HEAD_DIM
            head_lane = lax.broadcasted_iota(jnp.int32, (1, LANES), 1)
            banded = _banded_tiles(kvp_ref, kvc_ref)
            units = [(u, m) for u in range(Q_BLOCKS) for m in range(N_PAIRS)]
            dsink = jnp.zeros((1, LANES), F32)
            folded = {}
            logits, dps, dsbs = {}, {}, {}
            for step in range(len(units) + 2):
                if step < len(units):
                    u, m = units[step]
                    kh, rows = m // pairs_per_kv, slice((m % pairs_per_kv) * BAND, (m % pairs_per_kv + 1) * BAND)
                    qrows = slice(u * BLOCK, (u + 1) * BLOCK)
                    qpair = _pair_rows(q_ref, qrows, m, SCALE)
                    dopair = _pair_rows(da_ref, qrows, m)
                    qs_ref[u, kh, rows, :] = qpair
                    dos_ref[u, kh, rows, :] = dopair
                    logits[step] = _dot_nt(banded[u][0][kh], qpair) + _bias_of(bias_ref, i, u, m)
                    dps[step] = _dot_nt(banded[u][2][kh], dopair)
                if 0 <= step - 1 < len(units):
                    u, m = units[step - 1]
                    kh, rows = m // pairs_per_kv, slice((m % pairs_per_kv) * BAND, (m % pairs_per_kv + 1) * BAND)
                    pn, sink_p = _softmax_t(logits.pop(step - 1), _sink_row(sink_ref, m))
                    dp = dps.pop(step - 1)
                    delta = jnp.sum(pn * dp, axis=0, keepdims=True)
                    ds = pn * (dp - delta)
                    dssum_ref[m] += ds
                    sink_term = sink_p * delta
                    for e in range(2):
                        total = jnp.sum(sink_term[:, e * BLOCK:(e + 1) * BLOCK], axis=1, keepdims=True)
                        dsink = dsink - jnp.where(head_lane == 2 * m + e, total, 0.0)
                    dsbs[step - 1] = ds.astype(BF16)
                    dst_ref[u, kh, :, rows] = dsbs[step - 1]
                    pt_ref[u, kh, :, rows] = pn.astype(BF16)
                if 0 <= step - 2 < len(units):
                    u, m = units[step - 2]
                    kh = m // pairs_per_kv
                    dq_t = _dot(banded[u][1][kh], dsbs.pop(step - 2))
                    dq_ref[u * BLOCK:(u + 1) * BLOCK, m * LANES:(m + 1) * LANES] = (_pair_cols(dq_t) * SCALE).astype(BF16)
                    if m % pairs_per_kv == pairs_per_kv - 1:
                        for name, lhs_ref, rhs_ref in (("k", dst_ref, qs_ref), ("v", pt_ref, dos_ref)):
                            acc = _dot(lhs_ref[u, kh], rhs_ref[u, kh])
                            folded[u, kh, name] = acc + pltpu.roll(acc, HEAD_DIM, 1)
            dsink_ref[0:1, :] += dsink
            dkv = [jnp.concatenate([jnp.where(lo, folded[u, 0, n], folded[u, 1, n]) for n in ("k", "v")], axis=1)
                   for u in range(Q_BLOCKS)]

            @pl.when(i > 0)
            def _():
                if held:
                    dkv_ref[:held, :] = done_ref[...].astype(BF16)
                dkv_ref[held:, :] = (carry_ref[...] + dkv[0][:BLOCK]).astype(BF16)

            for u in range(Q_BLOCKS - 1):
                done_ref[u * BLOCK:(u + 1) * BLOCK, :] = dkv[u][BLOCK:] + dkv[u + 1][:BLOCK]
            carry_ref[...] = dkv[Q_BLOCKS - 1][BLOCK:]

        @pl.when(i == nsteps)
        def _():
            if held:
                dkv_ref[:held, :] = done_ref[...].astype(BF16)
            dkv_ref[held:, :] = carry_ref[...].astype(BF16)

    last = nsteps - 1
    blk = lambda w: pl.BlockSpec((tile, w), lambda i: (jnp.minimum(i, last), 0))
    outs = pl.pallas_call(
        body,
        name="attn_bwd",
        grid=(nsteps + 1,),
        in_specs=[
            pl.BlockSpec(memory_space=pltpu.SMEM),
            blk(aw),
            blk(kvw),
            pl.BlockSpec((BLOCK, kvw), lambda i: (jnp.clip(Q_BLOCKS * i - 1, 0, nb - 1), 0)),
            blk(aw),
            _full(biasm.shape),
        ] + [HBM_SPEC] * nr,
        out_specs=[
            blk(aw),
            pl.BlockSpec((tile, kvw), lambda i: (jnp.maximum(i - 1, 0), 0)),
            _resident(biasm.shape[1:]),
            _resident((8, LANES)),
        ] + [HBM_SPEC] * nr,
        out_shape=[
            jax.ShapeDtypeStruct((seq, aw), BF16),
            jax.ShapeDtypeStruct((seq, kvw), BF16),
            jax.ShapeDtypeStruct(biasm.shape[1:], F32),
            jax.ShapeDtypeStruct((8, LANES), F32),
        ] + [jax.ShapeDtypeStruct(g.shape, g.dtype) for g in ready],
        scratch_shapes=[
            pltpu.VMEM((BLOCK, kvw), F32),
            pltpu.VMEM((max(held, 8), kvw), F32),
            pltpu.VMEM((Q_BLOCKS, N_KV_HEADS, pairs_per_kv * BAND, LANES), BF16),
            pltpu.VMEM((Q_BLOCKS, N_KV_HEADS, pairs_per_kv * BAND, LANES), BF16),
            pltpu.VMEM((Q_BLOCKS, N_KV_HEADS, BAND, pairs_per_kv * BAND), BF16),
            pltpu.VMEM((Q_BLOCKS, N_KV_HEADS, BAND, pairs_per_kv * BAND), BF16),
        ] + _exchange_sems(nr),
        compiler_params=_params(("arbitrary",), 48),
    )(sinks, q, kv, kv, dattn, biasm, *ready)
    return outs[:4], outs[4:]


def _relbias_grad(dssum2, bucket_row, chunk):
    heads, n = dssum2.shape

    def body(a_ref, bucket_ref, out_ref):
        @pl.when(pl.program_id(0) == 0)
        def _():
            out_ref[...] = jnp.zeros_like(out_ref)

        a = a_ref[...]
        hi = a.astype(BF16)
        lo = (a - hi.astype(F32)).astype(BF16)
        onehot_t = (lax.broadcasted_iota(jnp.int32, (LANES, chunk), 0) == bucket_ref[...]).astype(F32).astype(BF16)
        out_ref[...] += _dot_nt(hi, onehot_t) + _dot_nt(lo, onehot_t)

    return pl.pallas_call(
        body,
        name="relbias_grad",
        grid=(n // chunk,),
        in_specs=[pl.BlockSpec((heads, chunk), lambda i: (0, i)), pl.BlockSpec((1, chunk), lambda i: (0, i))],
        out_specs=_resident((heads, LANES)),
        out_shape=jax.ShapeDtypeStruct((heads, LANES), F32),
        compiler_params=_params(("arbitrary",), 32),
    )(dssum2, bucket_row)


def _layer_b_in_bwd(dh2, dq, dz2, dkv, h1, ya, wbin_g, wkv, kvn, bpre, sm, ready, ts):
    seq, d = h1.shape
    aw = dq.shape[1]
    kvw = dkv.shape[1]
    cw = wbin_g.shape[2]
    per = aw // cw

    nr = len(ready)
    nt = seq // ts

    def body(dh2_ref, dq_ref, dz2_ref, dkv_ref, h1_ref, ya_ref, wbin_ref, wkv_ref, kvn_ref, bpre_ref, sm_ref, *refs):
        ready_refs, (dh1_ref, dya_ref, acc_ref) = refs[:nr], refs[nr:nr + 3]
        landed_refs, sems = refs[nr + 3:2 * nr + 3], refs[2 * nr + 3:]

        @pl.when(pl.program_id(0) == 0)
        def _():
            acc_ref[...] = jnp.zeros_like(acc_ref)
            _exchange_start(ready_refs, landed_refs, *sems, True)

        @pl.when(pl.program_id(0) == nt - 1)
        def _():
            _exchange_wait(ready_refs, landed_refs, *sems, True)

        dn4 = jnp.zeros((ts, d), F32)
        for j in range(N_DEV):
            src = dq_ref if j < per else dz2_ref
            jj = j % per
            dn4 = dn4 + _dot_nt(src[:, jj * cw:(jj + 1) * cw], wbin_ref[j])
        dn3 = _dot_nt(dkv_ref[...], wkv_ref[...])
        hn, r = _rms(h1_ref[...])
        acc_ref[0:1, :] += jnp.sum(dn4 * hn, axis=0, keepdims=True)
        acc_ref[1:2, :] += jnp.sum(dn3 * hn, axis=0, keepdims=True)
        dh1 = dh2_ref[...] + _rms_bwd(dn4 * bpre_ref[...] + dn3 * kvn_ref[...], hn, r)
        dh1_ref[...] = dh1
        yan, r2 = _rms(ya_ref[...])
        acc_ref[2:3, :] += jnp.sum(dh1 * yan, axis=0, keepdims=True)
        dya_ref[...] = _rms_bwd(dh1 * sm_ref[4:5, :], yan, r2).astype(BF16)

    outs = pl.pallas_call(
        body,
        name="layer_b_in_bwd",
        grid=(nt,),
        in_specs=[_rows(ts, d), _rows(ts, aw), _rows(ts, aw), _rows(ts, kvw), _rows(ts, d), _rows(ts, d),
                  _full(wbin_g.shape), _full(wkv.shape), _full(kvn.shape), _full(bpre.shape), _full(sm.shape)]
        + [HBM_SPEC] * nr,
        out_specs=[_rows(ts, d), _rows(ts, d), _resident((8, d))] + [HBM_SPEC] * nr,
        out_shape=[jax.ShapeDtypeStruct((seq, d), F32), jax.ShapeDtypeStruct((seq, d), BF16),
                   jax.ShapeDtypeStruct((8, d), F32)] + [jax.ShapeDtypeStruct(g.shape, g.dtype) for g in ready],
        scratch_shapes=_exchange_sems(nr),
        compiler_params=_params(("arbitrary",), 48),
    )(dh2, dq, dz2, dkv, h1, ya, wbin_g, wkv, kvn, bpre, sm, *ready)
    return outs[:3], outs[3:]


def _layer_a_bwd(dya, proj, conv, dh1, x2, wout, win_g, sm, ts):
    seq, d = x2.shape
    width = wout.shape[0]
    half = win_g.shape[2]
    n_half = width // half
    nt = seq // ts

    def body(dya_ref, proj_ref, conv_ref, dh1_ref, x_ref, wout_ref, win_ref, sm_ref, dproj_ref, gx_ref, acc_ref,
             dnext_ref):
        @pl.when(pl.program_id(0) == 0)
        def _():
            acc_ref[...] = jnp.zeros_like(acc_ref)
            dnext_ref[...] = jnp.zeros_like(dnext_ref)

        dy = _dot_nt(dya_ref[...], wout_ref[...])
        row = lax.broadcasted_iota(jnp.int32, (ts, half), 0)
        dn1 = jnp.zeros((ts, d), F32)
        for hh in range(n_half):
            cols = slice(hh * half, (hh + 1) * half)
            b, c, u, z = [proj_ref[:, (part * n_half + hh) * half:(part * n_half + hh + 1) * half].astype(F32)
                          for part in range(4)]
            cv = conv_ref[:, cols].astype(F32)
            dyh = dy[:, cols]
            sz, dsz = _silu(z)
            dconv = dyh * b * sz
            grads = [dyh * cv * sz, None, None, dyh * b * cv * dsz]
            next0, next1 = dnext_ref[0:1, cols], dnext_ref[1:2, cols]
            dc1 = jnp.where(row == ts - 1, next0, pltpu.roll(dconv, ts - 1, 0))
            dc2 = jnp.where(row == ts - 1, next1, jnp.where(row == ts - 2, next0, pltpu.roll(dconv, ts - 2, 0)))
            dnext_ref[:, cols] = dconv[0:8, :]
            v = c * u
            acc_ref[1:2, cols] += jnp.sum(dc2 * v, axis=0, keepdims=True)
            acc_ref[2:3, cols] += jnp.sum(dc1 * v, axis=0, keepdims=True)
            acc_ref[3:4, cols] += jnp.sum(dconv * v, axis=0, keepdims=True)
            dv = sm_ref[3:4, cols] * dconv + sm_ref[2:3, cols] * dc1 + sm_ref[1:2, cols] * dc2
            grads[1] = dv * u
            grads[2] = dv * c
            for part in range(4):
                j = part * n_half + hh
                gj = grads[part].astype(BF16)
                dproj_ref[:, j * half:(j + 1) * half] = gj
                dn1 = dn1 + _dot_nt(gj, win_ref[j])
        xn, r = _rms(x_ref[...])
        acc_ref[0:1, :] += jnp.sum(dn1 * xn, axis=0, keepdims=True)
        gx_ref[...] = dh1_ref[...] + _rms_bwd(dn1 * sm_ref[0:1, :], xn, r)

    rev = lambda w: pl.BlockSpec((ts, w), lambda i: (nt - 1 - i, 0))
    return pl.pallas_call(
        body,
        name="layer_a_bwd",
        grid=(nt,),
        in_specs=[rev(d), rev(4 * width), rev(width), rev(d), rev(d), _full(wout.shape), _full(win_g.shape), _full(sm.shape)],
        out_specs=[rev(4 * width), rev(d), _resident((8, d))],
        out_shape=[jax.ShapeDtypeStruct((seq, 4 * width), BF16), jax.ShapeDtypeStruct((seq, d), F32),
                   jax.ShapeDtypeStruct((8, d), F32)],
        scratch_shapes=[pltpu.VMEM((8, width), F32)],
        compiler_params=_params(("arbitrary",), 56),
    )(dya, proj, conv, dh1, x2, wout, win_g, sm)


def _wgrad(a, bs, n_slots, ts, name, ready=()):
    nr = len(ready)
    seq, k = a.shape
    nb_in = len(bs)
    n_each = bs[0].shape[1]
    n = nb_in * n_each
    bn = min(n_each, 1024)
    per_in = n_each // bn
    n_blocks = nb_in * per_in
    ns = seq // ts

    def b_spec(idx):
        def index(j, s):
            mine = j // per_in == idx
            row = jnp.where(mine, s, jnp.where(j // per_in > idx, ns - 1, 0))
            return (row, jnp.where(mine, j % per_in, jnp.where(j // per_in > idx, per_in - 1, 0)))
        return pl.BlockSpec((ts, bn), index)

    if n_slots:
        sw = n // n_slots
        spb = bn // sw
        out_shape = jax.ShapeDtypeStruct((n_slots, k, sw), BF16)
        out_spec = pl.BlockSpec((spb, k, sw), lambda j, s: (j, 0, 0))
    else:
        out_shape = jax.ShapeDtypeStruct((k, n), BF16)
        out_spec = pl.BlockSpec((k, bn), lambda j, s: (0, j))

    def body(a_ref, *refs):
        b_refs, ready_refs, o_ref = refs[:nb_in], refs[nb_in:nb_in + nr], refs[nb_in + nr]
        landed_refs, (acc_ref, *sems) = refs[nb_in + nr + 1:nb_in + 2 * nr + 1], refs[nb_in + 2 * nr + 1:]
        j, s = pl.program_id(0), pl.program_id(1)

        if nr:
            @pl.when(jnp.logical_and(j == 0, s == 0))
            def _():
                _exchange_start(ready_refs, landed_refs, *sems, True)

            @pl.when(jnp.logical_and(j == n_blocks - 1, s == ns - 1))
            def _():
                _exchange_wait(ready_refs, landed_refs, *sems, True)

        @pl.when(s == 0)
        def _():
            acc_ref[...] = jnp.zeros_like(acc_ref)

        for idx in range(nb_in):
            @pl.when(j // per_in == idx)
            def _(idx=idx):
                acc_ref[...] += _dot_tn(a_ref[...], b_refs[idx][...])

        @pl.when(s == ns - 1)
        def _():
            if n_slots:
                for e in range(spb):
                    o_ref[e] = acc_ref[:, e * sw:(e + 1) * sw].astype(BF16)
            else:
                o_ref[...] = acc_ref[...].astype(BF16)

    outs = pl.pallas_call(
        body,
        name=name,
        grid=(n_blocks, ns),
        in_specs=[pl.BlockSpec((ts, k), lambda j, s: (s, 0))] + [b_spec(idx) for idx in range(nb_in)] + [HBM_SPEC] * nr,
        out_specs=[out_spec] + [HBM_SPEC] * nr,
        out_shape=[out_shape] + [jax.ShapeDtypeStruct(g.shape, g.dtype) for g in ready],
        scratch_shapes=[pltpu.VMEM((k, bn), F32)] + (_exchange_sems(nr) if nr else []),
        compiler_params=_params(("arbitrary", "arbitrary"), 48),
    )(a, *bs, *ready)
    return (outs[0], outs[1:]) if nr else outs[0]


def _adamw(ws, gs, ms, vs):
    n = len(ws)

    def step(w, g, m, v):
        m = ADAM_B1 * m + (1.0 - ADAM_B1) * g
        v = ADAM_B2 * v + (1.0 - ADAM_B2) * jnp.square(g)
        m_hat = m / (1.0 - ADAM_B1 ** ADAM_STEP)
        v_hat = v / (1.0 - ADAM_B2 ** ADAM_STEP)
        return g, -ADAM_LR * (m_hat / (jnp.sqrt(v_hat) + ADAM_EPS) + ADAM_WD * w), m, v

    def body(*refs):
        w_refs, g_refs, m_refs, v_refs = (refs[k * n:(k + 1) * n] for k in range(4))
        go_refs, d_refs, nm_refs, nv_refs = (refs[(4 + k) * n:(5 + k) * n] for k in range(4))
        for t in range(n):
            rows = w_refs[t].shape[0]
            if rows <= 128:
                go_refs[t][...], d_refs[t][...], nm_refs[t][...], nv_refs[t][...] = step(
                    w_refs[t][...], g_refs[t][...], m_refs[t][...], v_refs[t][...])
                continue
            chunk = 128

            def one(i, carry, t=t):
                r = pl.ds(pl.multiple_of(i * chunk, chunk), chunk)
                go_refs[t][r, :], d_refs[t][r, :], nm_refs[t][r, :], nv_refs[t][r, :] = step(
                    w_refs[t][r, :], g_refs[t][r, :], m_refs[t][r, :], v_refs[t][r, :])
                return carry

            lax.fori_loop(0, rows // chunk, one, 0)

    vmem = pl.BlockSpec(memory_space=pltpu.VMEM)
    outs = pl.pallas_call(
        body,
        name="adamw",
        in_specs=[vmem] * (4 * n),
        out_specs=[vmem] * (4 * n),
        out_shape=[jax.ShapeDtypeStruct(w.shape, F32) for w in ws] * 4,
        compiler_params=_params(vmem_mib=56),
    )(*ws, *gs, *ms, *vs)
    return outs[:n], outs[n:2 * n], outs[2 * n:3 * n], outs[3 * n:]


def _band_structure():
    q_loc = jnp.arange(BLOCK, dtype=jnp.int32)[:, None]
    s_loc = jnp.arange(2 * BLOCK, dtype=jnp.int32)[None, :]
    dist = q_loc + BLOCK - s_loc
    in_window = (dist >= 0) & (dist < BLOCK)
    dd = jnp.maximum(dist, 0)
    max_exact = N_BUCKETS // 2
    large = max_exact + (jnp.log(jnp.maximum(dd, 1).astype(F32) / max_exact) / math.log(MAX_DISTANCE / max_exact)
                         * (N_BUCKETS - max_exact)).astype(jnp.int32)
    bucket = jnp.where(dd < max_exact, dd, jnp.minimum(large, N_BUCKETS - 1))
    return bucket, in_window.astype(jnp.int32)


def _place_rows(a, row, rows=8):
    return jnp.pad(a, ((row, rows - row - a.shape[0]), (0, 0)))


def kernel(x, a_pre_norm, a_w_in, a_conv_w, a_w_out, a_post_norm, kv_norm, w_kv, rel_bias, b_pre_norm, b_w_in, b_sinks, b_w_out, b_post_norm, loss_target, m_a_pre_norm, m_a_w_in, m_a_conv_w, m_a_w_out, m_a_post_norm, m_kv_norm, m_w_kv, m_rel_bias, m_b_pre_norm, m_b_w_in, m_b_sinks, m_b_w_out, m_b_post_norm, v_a_pre_norm, v_a_w_in, v_a_conv_w, v_a_w_out, v_a_post_norm, v_kv_norm, v_w_kv, v_rel_bias, v_b_pre_norm, v_b_w_in, v_b_sinks, v_b_w_out, v_b_post_norm):
    seq, d = x.shape[1], x.shape[2]
    x2 = x.reshape(seq, d)
    target = loss_target.reshape(seq, d)
    shard = a_pre_norm.shape[1]
    me = _my_index()
    ts_a = min(seq, 512)
    ts = min(seq, 512)
    ts_w = min(seq, 2048)

    small = _place_rows(a_pre_norm, 0) + _place_rows(a_conv_w[0], 1) + _place_rows(a_post_norm, 4)
    win_g, wout_g, small_g = _all_gather([a_w_in[0], a_w_out[0], small], [BF16, BF16, F32])
    wout = wout_g.reshape(-1, wout_g.shape[2])
    sm = small_g.transpose(1, 0, 2).reshape(8, N_DEV * shard)
    kvn = kv_norm.reshape(1, d)

    (h1, n1, proj, conv, y, ya), (wkv_g, wbin_g, wbout_g) = _layer_a_fwd(
        x2, sm, win_g, wout, [w_kv.astype(BF16), b_w_in[0].astype(BF16), b_w_out[0].astype(BF16)], ts_a)
    wkv = wkv_g.reshape(-1, wkv_g.shape[2])
    wbout = wbout_g.reshape(-1, wbout_g.shape[2])
    bucket, in_window = _band_structure()
    biasm = _bias_table(rel_bias.T, bucket.T, in_window.T)
    n3, n4, kv, q, o, dh2, dyb, dattn, dz2, acc_c = _layer_b_fwd(
        h1, target, kvn, b_pre_norm, wkv, wbin_g, biasm, b_sinks, wbout, b_post_norm)

    g_wbout = _wgrad(o, [dyb], 0, ts_w, "wgrad_b_out").reshape(wbout_g.shape)
    (dq, dkv, dssum, dsink), (l_wbout,) = _attn_bwd(q, kv, dattn, biasm, b_sinks, [g_wbout])
    by_head = dssum.reshape(N_PAIRS, BAND, 2, BLOCK).transpose(0, 2, 3, 1)
    relb = _relbias_grad(by_head.reshape(N_Q_HEADS, -1), bucket.reshape(1, -1), 4096)
    g_wkv = _wgrad(n3, [dkv], 0, ts_w, "wgrad_kv").reshape(wkv_g.shape)
    g_wbin = _wgrad(n4, [dq, dz2], N_DEV, ts_w, "wgrad_b_in")
    (dh1, dya, acc_b), (l_wkv, l_wbin) = _layer_b_in_bwd(
        dh2, dq, dz2, dkv, h1, ya, wbin_g, wkv, kvn, b_pre_norm, sm, [g_wkv, g_wbin], ts)
    dproj, gx, acc_a = _layer_a_bwd(dya, proj, conv, dh1, x2, wout, win_g, sm, ts_a)
    g_wout = _wgrad(y, [dya], 0, ts_w, "wgrad_a_out").reshape(wout_g.shape)
    g_win, (l_wout,) = _wgrad(n1, [dproj], N_DEV, ts_w, "wgrad_a_in", [g_wout])

    r_win, (r_wout, r_wkv, r_wbin, r_wbout), (s_a, s_b, s_c, s_relb, s_sink) = _reduce_exchange(
        g_win, [l_wout, l_wkv, l_wbin, l_wbout], [acc_a, acc_b, acc_c, relb, dsink])
    mine = lambda rows: lax.dynamic_slice_in_dim(rows, me * shard, shard, axis=1)
    loss = s_c[1, 0]
    weights = [a_pre_norm, a_w_in[0], a_conv_w[0], a_w_out[0], a_post_norm, kvn, w_kv, rel_bias.T, b_pre_norm,
               b_w_in[0], b_sinks, b_w_out[0], b_post_norm]
    grads = [mine(s_a[0:1]), r_win, mine(s_a[1:4]), r_wout, mine(s_b[2:3]), s_b[1:2], r_wkv,
             s_relb[:, :N_BUCKETS], s_b[0:1], r_wbin, s_sink[0:1, :N_Q_HEADS], r_wbout, s_c[0:1]]
    first = [m_a_pre_norm, m_a_w_in[0], m_a_conv_w[0], m_a_w_out[0], m_a_post_norm, m_kv_norm.reshape(1, d), m_w_kv,
             m_rel_bias.T, m_b_pre_norm, m_b_w_in[0], m_b_sinks, m_b_w_out[0], m_b_post_norm]
    second = [v_a_pre_norm, v_a_w_in[0], v_a_conv_w[0], v_a_w_out[0], v_a_post_norm, v_kv_norm.reshape(1, d), v_w_kv,
              v_rel_bias.T, v_b_pre_norm, v_b_w_in[0], v_b_sinks, v_b_w_out[0], v_b_post_norm]
    grads, deltas, new_m, new_v = _adamw(weights, grads, first, second)

    shapes = [a_pre_norm.shape, a_w_in.shape, a_conv_w.shape, a_w_out.shape, a_post_norm.shape, kv_norm.shape,
              w_kv.shape, None, b_pre_norm.shape, b_w_in.shape, b_sinks.shape, b_w_out.shape, b_post_norm.shape]
    shaped = lambda arrays: [a.T if s is None else a.reshape(s) for a, s in zip(arrays, shapes)]
    return (loss, gx.reshape(x.shape), *shaped(grads), *shaped(deltas), *shaped(new_m), *shaped(new_v))
```

```python
import functools
import math

import jax
import jax.numpy as jnp
from jax import lax
from jax.experimental import pallas as pl
from jax.experimental.pallas import tpu as pltpu

HEAD_DIM = 64
N_Q_HEADS = 16
N_KV_HEADS = 2
GROUP = N_Q_HEADS // N_KV_HEADS
BLOCK = 128
N_BUCKETS = 32
MAX_DISTANCE = 128
EPS = 1e-6
NEG_INF = -1e30
SCALE = HEAD_DIM ** -0.5

ADAM_LR = 0.001
ADAM_B1 = 0.9
ADAM_B2 = 0.999
ADAM_EPS = 1e-08
ADAM_WD = 0.01
ADAM_STEP = 10

N_DEV = 8
LANES = 128
F32 = jnp.float32
BF16 = jnp.bfloat16
MESH = pl.DeviceIdType.MESH
MIB = 1024 * 1024


def _params(semantics=None, vmem_mib=48):
    return pltpu.CompilerParams(dimension_semantics=semantics, vmem_limit_bytes=vmem_mib * MIB)


def _full(shape):
    zeros = (0,) * len(shape)
    return pl.BlockSpec(shape, lambda *_: zeros, pipeline_mode=pl.Buffered(1))


def _resident(shape):
    zeros = (0,) * len(shape)
    return pl.BlockSpec(shape, lambda *_: zeros)


def _rows(ts, cols):
    return pl.BlockSpec((ts, cols), lambda i: (i, 0))


def _dot(a, b):
    return jnp.dot(a, b, preferred_element_type=F32)


def _dot_nt(a, b):
    return lax.dot_general(a, b, (((1,), (1,)), ((), ())), preferred_element_type=F32)


def _dot_tn(a, b):
    return lax.dot_general(a, b, (((0,), (0,)), ((), ())), preferred_element_type=F32)


def _rms(xf):
    r = lax.rsqrt(jnp.mean(xf * xf, axis=-1, keepdims=True) + EPS)
    return xf * r, r


def _rms_bwd(dn, xn, r):
    return r * (dn - xn * jnp.mean(dn * xn, axis=-1, keepdims=True))


def _silu(z):
    s = jax.nn.sigmoid(z)
    return z * s, s * (1.0 + z * (1.0 - s))


def _my_index():
    return 4 * lax.axis_index("x") + 2 * lax.axis_index("y") + lax.axis_index("c")


def _all_gather(shards, out_dtypes):
    n = len(shards)

    def body(*refs):
        ins, outs = refs[:n], refs[n:2 * n]
        send_sems, recv_sems = refs[2 * n], refs[2 * n + 1]
        x, y, c = lax.axis_index("x"), lax.axis_index("y"), lax.axis_index("c")
        me, sibling = (x, y, c), (x, y, 1 - c)
        x_nbr, y_nbr, diagonal = (1 - x, y), (x, 1 - y), (1 - x, 1 - y)
        south = c == 0
        relayed = (jnp.where(south, 1 - x, x), jnp.where(south, y, 1 - y))
        relay_to = (jnp.where(south, x, 1 - x), jnp.where(south, 1 - y, y))

        def copy(t, k, block, to):
            rows = outs[t].at[4 * block[0] + 2 * block[1] + block[2]]
            return pltpu.make_async_remote_copy(
                src_ref=rows, dst_ref=rows, send_sem=send_sems.at[t, k], recv_sem=recv_sems.at[t, k],
                device_id=to, device_id_type=MESH)

        for t in range(n):
            outs[t][pl.ds(_my_index(), 1)] = ins[t][...].astype(outs[t].dtype)[None]
        started = []

        def start(cp):
            cp.start()
            started.append(cp)

        for t in range(n):
            start(copy(t, 0, me, sibling))
            start(copy(t, 1, me, (*x_nbr, c)))
            start(copy(t, 2, me, (*y_nbr, c)))
        for k, chip in ((1, x_nbr), (2, y_nbr)):
            for t in range(n):
                copy(t, k, (*chip, c), me).wait_recv()
                start(copy(t, 3 + k, (*chip, c), sibling))
        for t in range(n):
            start(copy(t, 3, (*relayed, c), (*relay_to, c)))
        for t in range(n):
            copy(t, 3, (*diagonal, c), me).wait_recv()
            start(copy(t, 6, (*diagonal, c), sibling))
        for t in range(n):
            copy(t, 0, sibling, me).wait_recv()
        for k, chip in ((4, x_nbr), (5, y_nbr), (6, diagonal)):
            for t in range(n):
                copy(t, k, (*chip, 1 - c), me).wait_recv()
        for cp in started:
            cp.wait_send()

    vmem = pl.BlockSpec(memory_space=pltpu.VMEM)
    return pl.pallas_call(
        body,
        name="gather_weights",
        out_shape=[jax.ShapeDtypeStruct((N_DEV,) + s.shape, dt) for s, dt in zip(shards, out_dtypes)],
        in_specs=[vmem] * n,
        out_specs=[vmem] * n,
        scratch_shapes=[pltpu.SemaphoreType.DMA((n, 7)), pltpu.SemaphoreType.DMA((n, 7))],
        compiler_params=_params(vmem_mib=48),
    )(*shards)


def _peer(k):
    x, y, c = lax.axis_index("x"), lax.axis_index("y"), lax.axis_index("c")
    px = 1 - x if k & 4 else x
    py = 1 - y if k & 2 else y
    pc = 1 - c if k & 1 else c
    return (px, py, pc), 4 * px + 2 * py + pc


def _exchange(srcs, dsts, send_sems, recv_sems, local_sems, scatter):
    me = _my_index()
    sends, arrivals = [], []
    for k in range(1, N_DEV):
        peer, pidx = _peer(k)
        for t, (src, dst) in enumerate(zip(srcs, dsts)):
            mine = src.at[pidx] if scatter else src
            sems = dict(send_sem=send_sems.at[t, k - 1], recv_sem=recv_sems.at[t, k - 1], device_id=peer, device_id_type=MESH)
            sends.append(pltpu.make_async_remote_copy(src_ref=mine, dst_ref=dst.at[me], **sems))
            arrivals.append(pltpu.make_async_remote_copy(src_ref=mine, dst_ref=dst.at[pidx], **sems))
    local = [pltpu.make_async_copy(src.at[me] if scatter else src, dst.at[me], local_sems.at[t])
             for t, (src, dst) in enumerate(zip(srcs, dsts))]
    return sends, arrivals, local


def _exchange_start(*args):
    sends, _, local = _exchange(*args)
    for cp in sends + local:
        cp.start()


def _exchange_wait(*args):
    sends, arrivals, local = _exchange(*args)
    for cp in arrivals:
        cp.wait_recv()
    for cp in sends:
        cp.wait_send()
    for cp in local:
        cp.wait()


def _exchange_sems(n):
    if not n:
        return []
    return [pltpu.SemaphoreType.DMA((n, N_DEV - 1)), pltpu.SemaphoreType.DMA((n, N_DEV - 1)), pltpu.SemaphoreType.DMA((n,))]


HBM_SPEC = pl.BlockSpec(memory_space=pl.ANY)


def _sum_slots(recv_ref, out_ref):
    rows = out_ref.shape[0]
    chunk = min(rows, 128)

    def add(i, carry):
        r0 = pl.multiple_of(i * chunk, chunk)
        acc = recv_ref[0, pl.ds(r0, chunk), :].astype(F32)
        for dev in range(1, N_DEV):
            acc = acc + recv_ref[dev, pl.ds(r0, chunk), :].astype(F32)
        out_ref[pl.ds(r0, chunk), :] = acc
        return carry

    lax.fori_loop(0, rows // chunk, add, 0)


N_CHIPS = N_DEV // 2


def _rows_loop(rows, fn):
    chunk = min(rows, 128)

    def step(i, carry):
        fn(pl.ds(pl.multiple_of(i * chunk, chunk), chunk))
        return carry

    lax.fori_loop(0, rows // chunk, step, 0)


def _chip_reduce(g_ref, out_ref, sib_ref, chip_ref, send_ref, sems):
    sib_send, sib_recv, chip_send, chip_recv = sems
    x, y, c = lax.axis_index("x"), lax.axis_index("y"), lax.axis_index("c")
    my_chip = 2 * x + y
    rows = out_ref.shape[0]

    def chip_of(k):
        cx = 1 - x if k & 2 else x
        cy = 1 - y if k & 1 else y
        return (cx, cy), 2 * cx + cy

    def to_sibling(t):
        return pltpu.make_async_remote_copy(
            src_ref=g_ref.at[2 * t + 1 - c], dst_ref=sib_ref.at[t], send_sem=sib_send.at[t], recv_sem=sib_recv.at[t],
            device_id=(x, y, 1 - c), device_id_type=MESH)

    def to_chip(k):
        (cx, cy), t = chip_of(k)
        return t, pltpu.make_async_remote_copy(
            src_ref=send_ref.at[k - 1], dst_ref=chip_ref.at[my_chip], send_sem=chip_send.at[k - 1],
            recv_sem=chip_recv.at[k - 1], device_id=(cx, cy, c), device_id_type=MESH)

    def from_chip(k):
        _, t = chip_of(k)
        return pltpu.make_async_remote_copy(
            src_ref=send_ref.at[k - 1], dst_ref=chip_ref.at[t], send_sem=chip_send.at[k - 1],
            recv_sem=chip_recv.at[k - 1], device_id=(x, y, c), device_id_type=MESH)

    def pair_sum(t, r):
        return g_ref[2 * t + c, r, :].astype(F32) + sib_ref[t, r, :].astype(F32)

    def swap():
        for t in range(N_CHIPS):
            to_sibling(t).start()

    def send():
        for t in range(N_CHIPS):
            to_sibling(t).wait_recv()
        for k in (3, 1, 2):
            t, cp = to_chip(k)

            def fill(r, t=t, k=k):
                send_ref[k - 1, r, :] = pair_sum(t, r).astype(BF16)

            _rows_loop(rows, fill)
            cp.start()

        def own(r):
            chip_ref[my_chip, r, :] = pair_sum(my_chip, r).astype(BF16)

        _rows_loop(rows, own)

    def finish():
        for k in range(1, N_CHIPS):
            from_chip(k).wait_recv()

        def total(r):
            acc = chip_ref[0, r, :].astype(F32)
            for t in range(1, N_CHIPS):
                acc = acc + chip_ref[t, r, :].astype(F32)
            out_ref[r, :] = acc

        _rows_loop(rows, total)
        for t in range(N_CHIPS):
            to_sibling(t).wait_send()
        for k in range(1, N_CHIPS):
            to_chip(k)[1].wait_send()

    return swap, send, finish


def _chip_reduce_scratch(slot):
    return [pltpu.VMEM((N_CHIPS,) + slot, BF16), pltpu.VMEM((N_CHIPS,) + slot, BF16),
            pltpu.VMEM((N_CHIPS - 1,) + slot, BF16),
            pltpu.SemaphoreType.DMA((N_CHIPS,)), pltpu.SemaphoreType.DMA((N_CHIPS,)),
            pltpu.SemaphoreType.DMA((N_CHIPS - 1,)), pltpu.SemaphoreType.DMA((N_CHIPS - 1,))]


def _reduce_exchange(part, landed, smalls):
    nl, ng = len(landed), len(smalls)
    n_out = 1 + nl + ng

    def body(*refs):
        p_in, l_in, s_in = refs[0], refs[1:1 + nl], refs[1 + nl:n_out]
        p_out, l_out, s_out = refs[n_out], refs[n_out + 1:n_out + 1 + nl], refs[n_out + 1 + nl:2 * n_out]
        scratch = refs[2 * n_out:]
        s_recv, (sib_ref, chip_ref, send_ref), sems = scratch[:ng], scratch[ng:ng + 3], scratch[ng + 3:]
        swap, send, finish = _chip_reduce(p_in, p_out, sib_ref, chip_ref, send_ref, sems[:4])
        swap()
        _exchange_start(s_in, s_recv, *sems[4:], False)
        send()
        for t in range(nl):
            _sum_slots(l_in[t], l_out[t])
        finish()
        _exchange_wait(s_in, s_recv, *sems[4:], False)
        for t in range(ng):
            acc = s_recv[t][0]
            for dev in range(1, N_DEV):
                acc = acc + s_recv[t][dev]
            s_out[t][...] = acc

    vmem = pl.BlockSpec(memory_space=pltpu.VMEM)
    slot = part.shape[1:]
    outs = pl.pallas_call(
        body,
        name="reduce_grads",
        out_shape=[jax.ShapeDtypeStruct(p.shape[1:], F32) for p in [part] + landed]
        + [jax.ShapeDtypeStruct(s.shape, F32) for s in smalls],
        in_specs=[vmem] * n_out,
        out_specs=[vmem] * n_out,
        scratch_shapes=[pltpu.VMEM((N_DEV,) + s.shape, F32) for s in smalls] + _chip_reduce_scratch(slot)
        + _exchange_sems(ng),
        compiler_params=_params(vmem_mib=56),
    )(part, *landed, *smalls)
    return outs[0], outs[1:1 + nl], outs[1 + nl:]


def _layer_a_fwd(x2, sm, win_g, wout, later, ts):
    seq, d = x2.shape
    width = wout.shape[0]
    half = win_g.shape[2]
    n_half = width // half
    nl = len(later)
    nt = seq // ts

    def body(x_ref, sm_ref, win_ref, wout_ref, *refs):
        shard_refs, refs = refs[:nl], refs[nl:]
        h1_ref, n1_ref, proj_ref, conv_ref, y_ref, ya_ref = refs[:6]
        gathered_refs, (vprev_ref, *sems) = refs[6:6 + nl], refs[6 + nl:]

        @pl.when(pl.program_id(0) == 0)
        def _():
            vprev_ref[...] = jnp.zeros_like(vprev_ref)
            _exchange_start(shard_refs, gathered_refs, *sems, False)

        @pl.when(pl.program_id(0) == nt - 1)
        def _():
            _exchange_wait(shard_refs, gathered_refs, *sems, False)

        xf = x_ref[...]
        xn, _ = _rms(xf)
        n1 = (xn * sm_ref[0:1, :]).astype(BF16)
        n1_ref[...] = n1
        row = lax.broadcasted_iota(jnp.int32, (ts, half), 0)
        ya = jnp.zeros((ts, d), F32)
        for hh in range(n_half):
            cols = slice(hh * half, (hh + 1) * half)
            parts = []
            for part in range(4):
                j = part * n_half + hh
                pj = _dot(n1, win_ref[j])
                proj_ref[:, j * half:(j + 1) * half] = pj.astype(BF16)
                parts.append(pj)
            b, c, u, z = parts
            v = c * u
            last1, last2 = vprev_ref[7:8, cols], vprev_ref[6:7, cols]
            v1 = jnp.where(row == 0, last1, pltpu.roll(v, 1, 0))
            v2 = jnp.where(row == 0, last2, jnp.where(row == 1, last1, pltpu.roll(v, 2, 0)))
            vprev_ref[:, cols] = v[ts - 8:ts, :]
            conv = sm_ref[1:2, cols] * v2 + sm_ref[2:3, cols] * v1 + sm_ref[3:4, cols] * v
            conv_ref[:, cols] = conv.astype(BF16)
            yh = (b * conv * _silu(z)[0]).astype(BF16)
            y_ref[:, cols] = yh
            ya = ya + _dot(yh, wout_ref[cols, :])
        ya_ref[...] = ya
        h1_ref[...] = xf + _rms(ya)[0] * sm_ref[4:5, :]

    outs = pl.pallas_call(
        body,
        name="layer_a_fwd",
        grid=(nt,),
        in_specs=[_rows(ts, d), _full(sm.shape), _full(win_g.shape), _full(wout.shape)] + [HBM_SPEC] * nl,
        out_specs=[_rows(ts, d), _rows(ts, d), _rows(ts, 4 * width), _rows(ts, width), _rows(ts, width), _rows(ts, d)]
        + [HBM_SPEC] * nl,
        out_shape=[
            jax.ShapeDtypeStruct((seq, d), F32),
            jax.ShapeDtypeStruct((seq, d), BF16),
            jax.ShapeDtypeStruct((seq, 4 * width), BF16),
            jax.ShapeDtypeStruct((seq, width), BF16),
            jax.ShapeDtypeStruct((seq, width), BF16),
            jax.ShapeDtypeStruct((seq, d), F32),
        ] + [jax.ShapeDtypeStruct((N_DEV,) + s.shape, s.dtype) for s in later],
        scratch_shapes=[pltpu.VMEM((8, width), F32)] + _exchange_sems(nl),
        compiler_params=_params(("arbitrary",), 56),
    )(x2, sm, win_g, wout, *later)
    return outs[:6], outs[6:]


N_PAIRS = N_Q_HEADS // 2
BAND = 2 * BLOCK


def _bias_table(rel_bias_t, bucket_t, in_window_t):
    def body(rb_ref, bucket_ref, win_ref, out_ref):
        bk = jnp.where(win_ref[...] != 0, bucket_ref[...], -1)
        has_prev = lax.broadcasted_iota(jnp.int32, bk.shape, 0) >= BLOCK
        for h in range(N_Q_HEADS):
            acc = jnp.full(bk.shape, NEG_INF, F32)
            for b in range(N_BUCKETS):
                acc = jnp.where(bk == b, rb_ref[h, b], acc)
            cols = slice((h % 2) * BLOCK, (h % 2 + 1) * BLOCK)
            out_ref[1, h // 2, :, cols] = acc
            out_ref[0, h // 2, :, cols] = jnp.where(has_prev, acc, NEG_INF)

    vmem = pl.BlockSpec(memory_space=pltpu.VMEM)
    return pl.pallas_call(
        body,
        name="bias_table",
        in_specs=[pl.BlockSpec(memory_space=pltpu.SMEM), vmem, vmem],
        out_specs=vmem,
        out_shape=jax.ShapeDtypeStruct((2, N_PAIRS, BAND, 2 * BLOCK), F32),
    )(rel_bias_t, bucket_t, in_window_t)


Q_BLOCKS = 4


def _banded_tiles(kvp_ref, kvc_ref):
    tile = kvc_ref[...].astype(F32)
    blocks = [kvp_ref[...].astype(F32)] + [tile[u * BLOCK:(u + 1) * BLOCK] for u in range(Q_BLOCKS)]
    return [_banded_kv(blocks[u], blocks[u + 1]) for u in range(Q_BLOCKS)]


def _bias_of(bias_ref, i, u, m):
    return bias_ref[jnp.minimum(i, 1) if u == 0 else 1, m]


def _banded_kv(kvp, kvc):
    kw = N_KV_HEADS * HEAD_DIM
    out = []
    for full in (jnp.concatenate([kvp[:, :kw], kvc[:, :kw]], axis=0), jnp.concatenate([kvp[:, kw:], kvc[:, kw:]], axis=0)):
        lo = lax.broadcasted_iota(jnp.int32, full.shape, 1) < HEAD_DIM
        rolled = pltpu.roll(full, HEAD_DIM, 1)
        x2 = [jnp.where(lo, full, rolled).astype(BF16), jnp.where(lo, rolled, full).astype(BF16)]
        ft = full.T
        x2t = [jnp.concatenate([ft[kh * HEAD_DIM:(kh + 1) * HEAD_DIM]] * 2, axis=0).astype(BF16) for kh in range(N_KV_HEADS)]
        out += [x2, x2t]
    return out


def _pair_rows(ref, rows, m, scale=None):
    both = ref[rows, m * LANES:(m + 1) * LANES].astype(F32)
    if scale is not None:
        both = both * scale
    lo = lax.broadcasted_iota(jnp.int32, both.shape, 1) < HEAD_DIM
    zero = jnp.zeros_like(both)
    return jnp.concatenate([jnp.where(lo, both, zero), jnp.where(lo, zero, both)], axis=0).astype(BF16)


def _pair_cols(res_t):
    top = lax.broadcasted_iota(jnp.int32, (LANES, BLOCK), 0) < HEAD_DIM
    return jnp.where(top, res_t[:, :BLOCK], res_t[:, BLOCK:]).T


def _sink_row(sink_ref, m):
    first = lax.broadcasted_iota(jnp.int32, (1, 2 * BLOCK), 1) < BLOCK
    return jnp.where(first, sink_ref[0, 2 * m], sink_ref[0, 2 * m + 1])


def _probs_t(k2, qpair, bias, sink):
    return _softmax_t(_dot_nt(k2, qpair) + bias, sink)


def _softmax_t(logits, sink):
    mx = jnp.maximum(jnp.max(logits, axis=0, keepdims=True), sink)
    p = jnp.exp(logits - mx)
    sink_p = jnp.exp(sink - mx)
    inv = 1.0 / (jnp.sum(p, axis=0, keepdims=True) + sink_p)
    return p * inv, sink_p * inv


def _layer_b_fwd(h1, target, kvn, bpre, wkv, wbin_g, biasm, sinks, wbout, bpost):
    seq, d = h1.shape
    kvw = wkv.shape[1]
    cw = wbin_g.shape[2]
    aw = N_Q_HEADS * HEAD_DIM
    per = aw // cw
    tile = Q_BLOCKS * BLOCK

    def body(sink_ref, h1_ref, tgt_ref, kvn_ref, bpre_ref, wkv_ref, wbin_ref, bias_ref, w_ref, g_ref,
             n3_ref, n4_ref, kvc_ref, q_ref, o_ref, dh2_ref, dyb_ref, dattn_ref, dz2_ref, acc_ref,
             attn_ref, z2_ref, kvp_ref):
        i = pl.program_id(0)

        @pl.when(i == 0)
        def _():
            acc_ref[...] = jnp.zeros_like(acc_ref)
            kvp_ref[...] = jnp.zeros_like(kvp_ref)

        hn, _ = _rms(h1_ref[...])
        n3 = (hn * kvn_ref[...]).astype(BF16)
        n4 = (hn * bpre_ref[...]).astype(BF16)
        n3_ref[...] = n3
        n4_ref[...] = n4
        kvc_ref[...] = _dot(n3, wkv_ref[...]).astype(BF16)
        for j in range(N_DEV):
            pj = _dot(n4, wbin_ref[j])
            if j < per:
                q_ref[:, j * cw:(j + 1) * cw] = pj.astype(BF16)
            else:
                z2_ref[:, (j - per) * cw:(j - per + 1) * cw] = pj

        banded = _banded_tiles(kvp_ref, kvc_ref)
        kvp_ref[...] = kvc_ref[tile - BLOCK:tile, :]
        units = [(u, m) for u in range(Q_BLOCKS) for m in range(N_PAIRS)]
        kv_of = lambda m: (2 * m) // GROUP
        logits, probs = {}, {}
        for step in range(len(units) + 2):
            if step < len(units):
                u, m = units[step]
                qpair = _pair_rows(q_ref, slice(u * BLOCK, (u + 1) * BLOCK), m, SCALE)
                logits[step] = _dot_nt(banded[u][0][kv_of(m)], qpair) + _bias_of(bias_ref, i, u, m)
            if 0 <= step - 1 < len(units):
                u, m = units[step - 1]
                probs[step - 1] = _softmax_t(logits.pop(step - 1), _sink_row(sink_ref, m))[0].astype(BF16)
            if 0 <= step - 2 < len(units):
                u, m = units[step - 2]
                out_t = _dot(banded[u][3][kv_of(m)], probs.pop(step - 2))
                attn_ref[u * BLOCK:(u + 1) * BLOCK, m * LANES:(m + 1) * LANES] = _pair_cols(out_t)
        attn = attn_ref[...]
        sz, dsz = _silu(z2_ref[...])
        o = (attn * sz).astype(BF16)
        o_ref[...] = o

        w = w_ref[...]
        yb = _dot(o, w)
        ybn, r = _rms(yb)
        g = g_ref[...]
        diff = h1_ref[...] + ybn * g - tgt_ref[...]
        dh2 = diff * (1.0 / d)
        dh2_ref[...] = dh2
        acc_ref[0:1, :] += jnp.sum(dh2 * ybn, axis=0, keepdims=True)
        tok = jnp.mean(diff * diff, axis=-1, keepdims=True)
        acc_ref[1:2, :] += 0.5 * jnp.sum(tok, axis=0, keepdims=True)
        dyb = _rms_bwd(dh2 * g, ybn, r).astype(BF16)
        dyb_ref[...] = dyb
        do = _dot_nt(dyb, w)
        dattn_ref[...] = (do * sz).astype(BF16)
        dz2_ref[...] = (do * attn * dsz).astype(BF16)

    blk = lambda w: pl.BlockSpec((tile, w), lambda i: (i, 0))
    return pl.pallas_call(
        body,
        name="layer_b_fwd",
        grid=(seq // tile,),
        in_specs=[
            pl.BlockSpec(memory_space=pltpu.SMEM),
            blk(d),
            blk(d),
            _full(kvn.shape),
            _full(bpre.shape),
            _full(wkv.shape),
            _full(wbin_g.shape),
            _full(biasm.shape),
            _full(wbout.shape),
            _full(bpost.shape),
        ],
        out_specs=[blk(d), blk(d), blk(kvw), blk(aw), blk(aw), blk(d), blk(d), blk(aw), blk(aw), _resident((8, d))],
        out_shape=[
            jax.ShapeDtypeStruct((seq, d), BF16),
            jax.ShapeDtypeStruct((seq, d), BF16),
            jax.ShapeDtypeStruct((seq, kvw), BF16),
            jax.ShapeDtypeStruct((seq, aw), BF16),
            jax.ShapeDtypeStruct((seq, aw), BF16),
            jax.ShapeDtypeStruct((seq, d), F32),
            jax.ShapeDtypeStruct((seq, d), BF16),
            jax.ShapeDtypeStruct((seq, aw), BF16),
            jax.ShapeDtypeStruct((seq, aw), BF16),
            jax.ShapeDtypeStruct((8, d), F32),
        ],
        scratch_shapes=[pltpu.VMEM((tile, aw), F32), pltpu.VMEM((tile, aw), F32), pltpu.VMEM((BLOCK, kvw), BF16)],
        compiler_params=_params(("arbitrary",), 56),
    )(sinks, h1, target, kvn, bpre, wkv, wbin_g, biasm, wbout, bpost)


def _attn_bwd(q, kv, dattn, biasm, sinks, ready):
    seq, aw = q.shape
    kvw = kv.shape[1]
    kw = N_KV_HEADS * HEAD_DIM
    nb = seq // BLOCK
    pairs_per_kv = N_PAIRS // N_KV_HEADS
    nr = len(ready)

    tile = Q_BLOCKS * BLOCK
    nsteps = seq // tile
    held = (Q_BLOCKS - 1) * BLOCK

    def body(sink_ref, q_ref, kvc_ref, kvp_ref, da_ref, bias_ref, *refs):
        ready_refs, (dq_ref, dkv_ref, dssum_ref, dsink_ref) = refs[:nr], refs[nr:nr + 4]
        landed_refs, scratch = refs[nr + 4:2 * nr + 4], refs[2 * nr + 4:]
        carry_ref, done_ref, qs_ref, dos_ref, dst_ref, pt_ref, *sems = scratch
        i = pl.program_id(0)

        @pl.when(i == 0)
        def _():
            dssum_ref[...] = jnp.zeros_like(dssum_ref)
            dsink_ref[...] = jnp.zeros_like(dsink_ref)
            carry_ref[...] = jnp.zeros_like(carry_ref)
            done_ref[...] = jnp.zeros_like(done_ref)
            if nr:
                _exchange_start(ready_refs, landed_refs, *sems, True)

        if nr:
            @pl.when(i == nsteps)
            def _():
                _exchange_wait(ready_refs, landed_refs, *sems, True)

        @pl.when(i < nsteps)
        def _():
            lo = lax.broadcasted_iota(jnp.int32, (BAND, LANES), 1) < HEAD_DIM
            head_lane = lax.broadcasted_iota(jnp.int32, (1, LANES), 1)
            banded = _banded_tiles(kvp_ref, kvc_ref)
            units = [(u, m) for u in range(Q_BLOCKS) for m in range(N_PAIRS)]
            dsink = jnp.zeros((1, LANES), F32)
            folded = {}
            logits, dps, dsbs = {}, {}, {}
            for step in range(len(units) + 2):
                if step < len(units):
                    u, m = units[step]
                    kh, rows = m // pairs_per_kv, slice((m % pairs_per_kv) * BAND, (m % pairs_per_kv + 1) * BAND)
                    qrows = slice(u * BLOCK, (u + 1) * BLOCK)
                    qpair = _pair_rows(q_ref, qrows, m, SCALE)
                    dopair = _pair_rows(da_ref, qrows, m)
                    qs_ref[u, kh, rows, :] = qpair
                    dos_ref[u, kh, rows, :] = dopair
                    logits[step] = _dot_nt(banded[u][0][kh], qpair) + _bias_of(bias_ref, i, u, m)
                    dps[step] = _dot_nt(banded[u][2][kh], dopair)
                if 0 <= step - 1 < len(units):
                    u, m = units[step - 1]
                    kh, rows = m // pairs_per_kv, slice((m % pairs_per_kv) * BAND, (m % pairs_per_kv + 1) * BAND)
                    pn, sink_p = _softmax_t(logits.pop(step - 1), _sink_row(sink_ref, m))
                    dp = dps.pop(step - 1)
                    delta = jnp.sum(pn * dp, axis=0, keepdims=True)
                    ds = pn * (dp - delta)
                    dssum_ref[m] += ds
                    sink_term = sink_p * delta
                    for e in range(2):
                        total = jnp.sum(sink_term[:, e * BLOCK:(e + 1) * BLOCK], axis=1, keepdims=True)
                        dsink = dsink - jnp.where(head_lane == 2 * m + e, total, 0.0)
                    dsbs[step - 1] = ds.astype(BF16)
                    dst_ref[u, kh, :, rows] = dsbs[step - 1]
                    pt_ref[u, kh, :, rows] = pn.astype(BF16)
                if 0 <= step - 2 < len(units):
                    u, m = units[step - 2]
                    kh = m // pairs_per_kv
                    dq_t = _dot(banded[u][1][kh], dsbs.pop(step - 2))
                    dq_ref[u * BLOCK:(u + 1) * BLOCK, m * LANES:(m + 1) * LANES] = (_pair_cols(dq_t) * SCALE).astype(BF16)
                    if m % pairs_per_kv == pairs_per_kv - 1:
                        for name, lhs_ref, rhs_ref in (("k", dst_ref, qs_ref), ("v", pt_ref, dos_ref)):
                            acc = _dot(lhs_ref[u, kh], rhs_ref[u, kh])
                            folded[u, kh, name] = acc + pltpu.roll(acc, HEAD_DIM, 1)
            dsink_ref[0:1, :] += dsink
            dkv = [jnp.concatenate([jnp.where(lo, folded[u, 0, n], folded[u, 1, n]) for n in ("k", "v")], axis=1)
                   for u in range(Q_BLOCKS)]

            @pl.when(i > 0)
            def _():
                if held:
                    dkv_ref[:held, :] = done_ref[...].astype(BF16)
                dkv_ref[held:, :] = (carry_ref[...] + dkv[0][:BLOCK]).astype(BF16)

            for u in range(Q_BLOCKS - 1):
                done_ref[u * BLOCK:(u + 1) * BLOCK, :] = dkv[u][BLOCK:] + dkv[u + 1][:BLOCK]
            carry_ref[...] = dkv[Q_BLOCKS - 1][BLOCK:]

        @pl.when(i == nsteps)
        def _():
            if held:
                dkv_ref[:held, :] = done_ref[...].astype(BF16)
            dkv_ref[held:, :] = carry_ref[...].astype(BF16)

    last = nsteps - 1
    blk = lambda w: pl.BlockSpec((tile, w), lambda i: (jnp.minimum(i, last), 0))
    outs = pl.pallas_call(
        body,
        name="attn_bwd",
        grid=(nsteps + 1,),
        in_specs=[
            pl.BlockSpec(memory_space=pltpu.SMEM),
            blk(aw),
            blk(kvw),
            pl.BlockSpec((BLOCK, kvw), lambda i: (jnp.clip(Q_BLOCKS * i - 1, 0, nb - 1), 0)),
            blk(aw),
            _full(biasm.shape),
        ] + [HBM_SPEC] * nr,
        out_specs=[
            blk(aw),
            pl.BlockSpec((tile, kvw), lambda i: (jnp.maximum(i - 1, 0), 0)),
            _resident(biasm.shape[1:]),
            _resident((8, LANES)),
        ] + [HBM_SPEC] * nr,
        out_shape=[
            jax.ShapeDtypeStruct((seq, aw), BF16),
            jax.ShapeDtypeStruct((seq, kvw), BF16),
            jax.ShapeDtypeStruct(biasm.shape[1:], F32),
            jax.ShapeDtypeStruct((8, LANES), F32),
        ] + [jax.ShapeDtypeStruct(g.shape, g.dtype) for g in ready],
        scratch_shapes=[
            pltpu.VMEM((BLOCK, kvw), F32),
            pltpu.VMEM((max(held, 8), kvw), F32),
            pltpu.VMEM((Q_BLOCKS, N_KV_HEADS, pairs_per_kv * BAND, LANES), BF16),
            pltpu.VMEM((Q_BLOCKS, N_KV_HEADS, pairs_per_kv * BAND, LANES), BF16),
            pltpu.VMEM((Q_BLOCKS, N_KV_HEADS, BAND, pairs_per_kv * BAND), BF16),
            pltpu.VMEM((Q_BLOCKS, N_KV_HEADS, BAND, pairs_per_kv * BAND), BF16),
        ] + _exchange_sems(nr),
        compiler_params=_params(("arbitrary",), 48),
    )(sinks, q, kv, kv, dattn, biasm, *ready)
    return outs[:4], outs[4:]


def _relbias_grad(dssum2, bucket_row, chunk):
    heads, n = dssum2.shape

    def body(a_ref, bucket_ref, out_ref):
        @pl.when(pl.program_id(0) == 0)
        def _():
            out_ref[...] = jnp.zeros_like(out_ref)

        a = a_ref[...]
        hi = a.astype(BF16)
        lo = (a - hi.astype(F32)).astype(BF16)
        onehot_t = (lax.broadcasted_iota(jnp.int32, (LANES, chunk), 0) == bucket_ref[...]).astype(F32).astype(BF16)
        out_ref[...] += _dot_nt(hi, onehot_t) + _dot_nt(lo, onehot_t)

    return pl.pallas_call(
        body,
        name="relbias_grad",
        grid=(n // chunk,),
        in_specs=[pl.BlockSpec((heads, chunk), lambda i: (0, i)), pl.BlockSpec((1, chunk), lambda i: (0, i))],
        out_specs=_resident((heads, LANES)),
        out_shape=jax.ShapeDtypeStruct((heads, LANES), F32),
        compiler_params=_params(("arbitrary",), 32),
    )(dssum2, bucket_row)


def _layer_b_in_bwd(dh2, dq, dz2, dkv, h1, ya, wbin_g, wkv, kvn, bpre, sm, ready, ts):
    seq, d = h1.shape
    aw = dq.shape[1]
    kvw = dkv.shape[1]
    cw = wbin_g.shape[2]
    per = aw // cw

    nr = len(ready)
    nt = seq // ts

    def body(dh2_ref, dq_ref, dz2_ref, dkv_ref, h1_ref, ya_ref, wbin_ref, wkv_ref, kvn_ref, bpre_ref, sm_ref, *refs):
        ready_refs, (dh1_ref, dya_ref, acc_ref) = refs[:nr], refs[nr:nr + 3]
        landed_refs, sems = refs[nr + 3:2 * nr + 3], refs[2 * nr + 3:]

        @pl.when(pl.program_id(0) == 0)
        def _():
            acc_ref[...] = jnp.zeros_like(acc_ref)
            _exchange_start(ready_refs, landed_refs, *sems, True)

        @pl.when(pl.program_id(0) == nt - 1)
        def _():
            _exchange_wait(ready_refs, landed_refs, *sems, True)

        dn4 = jnp.zeros((ts, d), F32)
        for j in range(N_DEV):
            src = dq_ref if j < per else dz2_ref
            jj = j % per
            dn4 = dn4 + _dot_nt(src[:, jj * cw:(jj + 1) * cw], wbin_ref[j])
        dn3 = _dot_nt(dkv_ref[...], wkv_ref[...])
        hn, r = _rms(h1_ref[...])
        acc_ref[0:1, :] += jnp.sum(dn4 * hn, axis=0, keepdims=True)
        acc_ref[1:2, :] += jnp.sum(dn3 * hn, axis=0, keepdims=True)
        dh1 = dh2_ref[...] + _rms_bwd(dn4 * bpre_ref[...] + dn3 * kvn_ref[...], hn, r)
        dh1_ref[...] = dh1
        yan, r2 = _rms(ya_ref[...])
        acc_ref[2:3, :] += jnp.sum(dh1 * yan, axis=0, keepdims=True)
        dya_ref[...] = _rms_bwd(dh1 * sm_ref[4:5, :], yan, r2).astype(BF16)

    outs = pl.pallas_call(
        body,
        name="layer_b_in_bwd",
        grid=(nt,),
        in_specs=[_rows(ts, d), _rows(ts, aw), _rows(ts, aw), _rows(ts, kvw), _rows(ts, d), _rows(ts, d),
                  _full(wbin_g.shape), _full(wkv.shape), _full(kvn.shape), _full(bpre.shape), _full(sm.shape)]
        + [HBM_SPEC] * nr,
        out_specs=[_rows(ts, d), _rows(ts, d), _resident((8, d))] + [HBM_SPEC] * nr,
        out_shape=[jax.ShapeDtypeStruct((seq, d), F32), jax.ShapeDtypeStruct((seq, d), BF16),
                   jax.ShapeDtypeStruct((8, d), F32)] + [jax.ShapeDtypeStruct(g.shape, g.dtype) for g in ready],
        scratch_shapes=_exchange_sems(nr),
        compiler_params=_params(("arbitrary",), 48),
    )(dh2, dq, dz2, dkv, h1, ya, wbin_g, wkv, kvn, bpre, sm, *ready)
    return outs[:3], outs[3:]


def _layer_a_bwd(dya, proj, conv, dh1, x2, wout, win_g, sm, ts):
    seq, d = x2.shape
    width = wout.shape[0]
    half = win_g.shape[2]
    n_half = width // half
    nt = seq // ts

    def body(dya_ref, proj_ref, conv_ref, dh1_ref, x_ref, wout_ref, win_ref, sm_ref, dproj_ref, gx_ref, acc_ref,
             dnext_ref):
        @pl.when(pl.program_id(0) == 0)
        def _():
            acc_ref[...] = jnp.zeros_like(acc_ref)
            dnext_ref[...] = jnp.zeros_like(dnext_ref)

        dy = _dot_nt(dya_ref[...], wout_ref[...])
        row = lax.broadcasted_iota(jnp.int32, (ts, half), 0)
        dn1 = jnp.zeros((ts, d), F32)
        for hh in range(n_half):
            cols = slice(hh * half, (hh + 1) * half)
            b, c, u, z = [proj_ref[:, (part * n_half + hh) * half:(part * n_half + hh + 1) * half].astype(F32)
                          for part in range(4)]
            cv = conv_ref[:, cols].astype(F32)
            dyh = dy[:, cols]
            sz, dsz = _silu(z)
            dconv = dyh * b * sz
            grads = [dyh * cv * sz, None, None, dyh * b * cv * dsz]
            next0, next1 = dnext_ref[0:1, cols], dnext_ref[1:2, cols]
            dc1 = jnp.where(row == ts - 1, next0, pltpu.roll(dconv, ts - 1, 0))
            dc2 = jnp.where(row == ts - 1, next1, jnp.where(row == ts - 2, next0, pltpu.roll(dconv, ts - 2, 0)))
            dnext_ref[:, cols] = dconv[0:8, :]
            v = c * u
            acc_ref[1:2, cols] += jnp.sum(dc2 * v, axis=0, keepdims=True)
            acc_ref[2:3, cols] += jnp.sum(dc1 * v, axis=0, keepdims=True)
            acc_ref[3:4, cols] += jnp.sum(dconv * v, axis=0, keepdims=True)
            dv = sm_ref[3:4, cols] * dconv + sm_ref[2:3, cols] * dc1 + sm_ref[1:2, cols] * dc2
            grads[1] = dv * u
            grads[2] = dv * c
            for part in range(4):
                j = part * n_half + hh
                gj = grads[part].astype(BF16)
                dproj_ref[:, j * half:(j + 1) * half] = gj
                dn1 = dn1 + _dot_nt(gj, win_ref[j])
        xn, r = _rms(x_ref[...])
        acc_ref[0:1, :] += jnp.sum(dn1 * xn, axis=0, keepdims=True)
        gx_ref[...] = dh1_ref[...] + _rms_bwd(dn1 * sm_ref[0:1, :], xn, r)

    rev = lambda w: pl.BlockSpec((ts, w), lambda i: (nt - 1 - i, 0))
    return pl.pallas_call(
        body,
        name="layer_a_bwd",
        grid=(nt,),
        in_specs=[rev(d), rev(4 * width), rev(width), rev(d), rev(d), _full(wout.shape), _full(win_g.shape), _full(sm.shape)],
        out_specs=[rev(4 * width), rev(d), _resident((8, d))],
        out_shape=[jax.ShapeDtypeStruct((seq, 4 * width), BF16), jax.ShapeDtypeStruct((seq, d), F32),
                   jax.ShapeDtypeStruct((8, d), F32)],
        scratch_shapes=[pltpu.VMEM((8, width), F32)],
        compiler_params=_params(("arbitrary",), 56),
    )(dya, proj, conv, dh1, x2, wout, win_g, sm)


def _wgrad(a, bs, n_slots, ts, name, ready=()):
    nr = len(ready)
    seq, k = a.shape
    nb_in = len(bs)
    n_each = bs[0].shape[1]
    n = nb_in * n_each
    bn = min(n_each, 1024)
    per_in = n_each // bn
    n_blocks = nb_in * per_in
    ns = seq // ts

    def b_spec(idx):
        def index(j, s):
            mine = j // per_in == idx
            row = jnp.where(mine, s, jnp.where(j // per_in > idx, ns - 1, 0))
            return (row, jnp.where(mine, j % per_in, jnp.where(j // per_in > idx, per_in - 1, 0)))
        return pl.BlockSpec((ts, bn), index)

    if n_slots:
        sw = n // n_slots
        spb = bn // sw
        out_shape = jax.ShapeDtypeStruct((n_slots, k, sw), BF16)
        out_spec = pl.BlockSpec((spb, k, sw), lambda j, s: (j, 0, 0))
    else:
        out_shape = jax.ShapeDtypeStruct((k, n), BF16)
        out_spec = pl.BlockSpec((k, bn), lambda j, s: (0, j))

    def body(a_ref, *refs):
        b_refs, ready_refs, o_ref = refs[:nb_in], refs[nb_in:nb_in + nr], refs[nb_in + nr]
        landed_refs, (acc_ref, *sems) = refs[nb_in + nr + 1:nb_in + 2 * nr + 1], refs[nb_in + 2 * nr + 1:]
        j, s = pl.program_id(0), pl.program_id(1)

        if nr:
            @pl.when(jnp.logical_and(j == 0, s == 0))
            def _():
                _exchange_start(ready_refs, landed_refs, *sems, True)

            @pl.when(jnp.logical_and(j == n_blocks - 1, s == ns - 1))
            def _():
                _exchange_wait(ready_refs, landed_refs, *sems, True)

        @pl.when(s == 0)
        def _():
            acc_ref[...] = jnp.zeros_like(acc_ref)

        for idx in range(nb_in):
            @pl.when(j // per_in == idx)
            def _(idx=idx):
                acc_ref[...] += _dot_tn(a_ref[...], b_refs[idx][...])

        @pl.when(s == ns - 1)
        def _():
            if n_slots:
                for e in range(spb):
                    o_ref[e] = acc_ref[:, e * sw:(e + 1) * sw].astype(BF16)
            else:
                o_ref[...] = acc_ref[...].astype(BF16)

    outs = pl.pallas_call(
        body,
        name=name,
        grid=(n_blocks, ns),
        in_specs=[pl.BlockSpec((ts, k), lambda j, s: (s, 0))] + [b_spec(idx) for idx in range(nb_in)] + [HBM_SPEC] * nr,
        out_specs=[out_spec] + [HBM_SPEC] * nr,
        out_shape=[out_shape] + [jax.ShapeDtypeStruct(g.shape, g.dtype) for g in ready],
        scratch_shapes=[pltpu.VMEM((k, bn), F32)] + (_exchange_sems(nr) if nr else []),
        compiler_params=_params(("arbitrary", "arbitrary"), 48),
    )(a, *bs, *ready)
    return (outs[0], outs[1:]) if nr else outs[0]


def _wgrad_tail(pairs, part, ts):
    n_tasks = len(pairs)
    seq, k = pairs[0][0].shape
    n = pairs[0][1].shape[1]
    ns = seq // ts
    total = n_tasks * ns
    per = k // N_DEV

    def spec(t, width):
        return pl.BlockSpec((ts, width), lambda j, s: (jnp.where(j == t, s, jnp.where(j > t, ns - 1, 0)), 0))

    def body(*refs):
        ab_refs, part_ref = refs[:2 * n_tasks], refs[2 * n_tasks]
        o_ref, red_ref, acc_ref, sib_ref, chip_ref, send_ref, *sems = refs[2 * n_tasks + 1:]
        j, s = pl.program_id(0), pl.program_id(1)
        flat = j * ns + s
        swap, send, finish = _chip_reduce(part_ref, red_ref, sib_ref, chip_ref, send_ref, sems)

        @pl.when(flat == 0)
        def _():
            swap()

        @pl.when(flat == min(1, total - 1))
        def _():
            send()

        @pl.when(s == 0)
        def _():
            acc_ref[...] = jnp.zeros_like(acc_ref)

        for t in range(n_tasks):
            @pl.when(j == t)
            def _(t=t):
                acc_ref[...] += _dot_tn(ab_refs[2 * t][...], ab_refs[2 * t + 1][...])

        @pl.when(s == ns - 1)
        def _():
            for dev in range(N_DEV):
                o_ref[dev] = acc_ref[dev * per:(dev + 1) * per, :].astype(BF16)

        @pl.when(flat == total - 1)
        def _():
            finish()

    slot = part.shape[1:]
    return pl.pallas_call(
        body,
        name="wgrad_tail",
        grid=(n_tasks, ns),
        in_specs=[spec(t, w) for t in range(n_tasks) for w in (k, n)] + [_full(part.shape)],
        out_specs=[pl.BlockSpec((N_DEV, per, n), lambda j, s: (0, j, 0)), _resident(slot)],
        out_shape=[jax.ShapeDtypeStruct((N_DEV, n_tasks * per, n), BF16), jax.ShapeDtypeStruct(slot, F32)],
        scratch_shapes=[pltpu.VMEM((k, n), F32)] + _chip_reduce_scratch(slot),
        compiler_params=_params(("arbitrary", "arbitrary"), 56),
    )(*[op for pair in pairs for op in pair], part)


def _adamw(ws, gs, ms, vs):
    n = len(ws)

    def step(w, g, m, v):
        m = ADAM_B1 * m + (1.0 - ADAM_B1) * g
        v = ADAM_B2 * v + (1.0 - ADAM_B2) * jnp.square(g)
        m_hat = m / (1.0 - ADAM_B1 ** ADAM_STEP)
        v_hat = v / (1.0 - ADAM_B2 ** ADAM_STEP)
        return g, -ADAM_LR * (m_hat / (jnp.sqrt(v_hat) + ADAM_EPS) + ADAM_WD * w), m, v

    def body(*refs):
        w_refs, g_refs, m_refs, v_refs = (refs[k * n:(k + 1) * n] for k in range(4))
        go_refs, d_refs, nm_refs, nv_refs = (refs[(4 + k) * n:(5 + k) * n] for k in range(4))
        for t in range(n):
            rows = w_refs[t].shape[0]
            if rows <= 128:
                go_refs[t][...], d_refs[t][...], nm_refs[t][...], nv_refs[t][...] = step(
                    w_refs[t][...], g_refs[t][...], m_refs[t][...], v_refs[t][...])
                continue
            chunk = 128

            def one(i, carry, t=t):
                r = pl.ds(pl.multiple_of(i * chunk, chunk), chunk)
                go_refs[t][r, :], d_refs[t][r, :], nm_refs[t][r, :], nv_refs[t][r, :] = step(
                    w_refs[t][r, :], g_refs[t][r, :], m_refs[t][r, :], v_refs[t][r, :])
                return carry

            lax.fori_loop(0, rows // chunk, one, 0)

    vmem = pl.BlockSpec(memory_space=pltpu.VMEM)
    outs = pl.pallas_call(
        body,
        name="adamw",
        in_specs=[vmem] * (4 * n),
        out_specs=[vmem] * (4 * n),
        out_shape=[jax.ShapeDtypeStruct(w.shape, F32) for w in ws] * 4,
        compiler_params=_params(vmem_mib=56),
    )(*ws, *gs, *ms, *vs)
    return outs[:n], outs[n:2 * n], outs[2 * n:3 * n], outs[3 * n:]


def _band_structure():
    q_loc = jnp.arange(BLOCK, dtype=jnp.int32)[:, None]
    s_loc = jnp.arange(2 * BLOCK, dtype=jnp.int32)[None, :]
    dist = q_loc + BLOCK - s_loc
    in_window = (dist >= 0) & (dist < BLOCK)
    dd = jnp.maximum(dist, 0)
    max_exact = N_BUCKETS // 2
    large = max_exact + (jnp.log(jnp.maximum(dd, 1).astype(F32) / max_exact) / math.log(MAX_DISTANCE / max_exact)
                         * (N_BUCKETS - max_exact)).astype(jnp.int32)
    bucket = jnp.where(dd < max_exact, dd, jnp.minimum(large, N_BUCKETS - 1))
    return bucket, in_window.astype(jnp.int32)


def _place_rows(a, row, rows=8):
    return jnp.pad(a, ((row, rows - row - a.shape[0]), (0, 0)))


def kernel(x, a_pre_norm, a_w_in, a_conv_w, a_w_out, a_post_norm, kv_norm, w_kv, rel_bias, b_pre_norm, b_w_in, b_sinks, b_w_out, b_post_norm, loss_target, m_a_pre_norm, m_a_w_in, m_a_conv_w, m_a_w_out, m_a_post_norm, m_kv_norm, m_w_kv, m_rel_bias, m_b_pre_norm, m_b_w_in, m_b_sinks, m_b_w_out, m_b_post_norm, v_a_pre_norm, v_a_w_in, v_a_conv_w, v_a_w_out, v_a_post_norm, v_kv_norm, v_w_kv, v_rel_bias, v_b_pre_norm, v_b_w_in, v_b_sinks, v_b_w_out, v_b_post_norm):
    seq, d = x.shape[1], x.shape[2]
    x2 = x.reshape(seq, d)
    target = loss_target.reshape(seq, d)
    shard = a_pre_norm.shape[1]
    me = _my_index()
    ts_a = min(seq, 512)
    ts = min(seq, 512)
    ts_w = min(seq, 2048)

    small = _place_rows(a_pre_norm, 0) + _place_rows(a_conv_w[0], 1) + _place_rows(a_post_norm, 4)
    win_g, wout_g, small_g = _all_gather([a_w_in[0], a_w_out[0], small], [BF16, BF16, F32])
    wout = wout_g.reshape(-1, wout_g.shape[2])
    sm = small_g.transpose(1, 0, 2).reshape(8, N_DEV * shard)
    kvn = kv_norm.reshape(1, d)

    (h1, n1, proj, conv, y, ya), (wkv_g, wbin_g, wbout_g) = _layer_a_fwd(
        x2, sm, win_g, wout, [w_kv.astype(BF16), b_w_in[0].astype(BF16), b_w_out[0].astype(BF16)], ts_a)
    wkv = wkv_g.reshape(-1, wkv_g.shape[2])
    wbout = wbout_g.reshape(-1, wbout_g.shape[2])
    bucket, in_window = _band_structure()
    biasm = _bias_table(rel_bias.T, bucket.T, in_window.T)
    n3, n4, kv, q, o, dh2, dyb, dattn, dz2, acc_c = _layer_b_fwd(
        h1, target, kvn, b_pre_norm, wkv, wbin_g, biasm, b_sinks, wbout, b_post_norm)

    (dq, dkv, dssum, dsink), _ = _attn_bwd(q, kv, dattn, biasm, b_sinks, [])
    by_head = dssum.reshape(N_PAIRS, BAND, 2, BLOCK).transpose(0, 2, 3, 1)
    relb = _relbias_grad(by_head.reshape(N_Q_HEADS, -1), bucket.reshape(1, -1), 4096)
    g_wkv = _wgrad(n3, [dkv], 0, ts_w, "wgrad_kv").reshape(wkv_g.shape)
    g_wbin = _wgrad(n4, [dq, dz2], N_DEV, ts_w, "wgrad_b_in")
    (dh1, dya, acc_b), (l_wkv, l_wbin) = _layer_b_in_bwd(
        dh2, dq, dz2, dkv, h1, ya, wbin_g, wkv, kvn, b_pre_norm, sm, [g_wkv, g_wbin], ts)
    dproj, gx, acc_a = _layer_a_bwd(dya, proj, conv, dh1, x2, wout, win_g, sm, ts_a)
    g_win = _wgrad(n1, [dproj], N_DEV, ts_w, "wgrad_a_in")
    g_outs, r_win = _wgrad_tail([(y, dya), (o, dyb)], g_win, min(seq, 1024))

    r_outs, (r_wkv, r_wbin), (s_a, s_b, s_c, s_relb, s_sink) = _reduce_exchange(
        g_outs, [l_wkv, l_wbin], [acc_a, acc_b, acc_c, relb, dsink])
    rows_out = wout_g.shape[1]
    r_wout, r_wbout = r_outs[:rows_out], r_outs[rows_out:]
    mine = lambda rows: lax.dynamic_slice_in_dim(rows, me * shard, shard, axis=1)
    loss = s_c[1, 0]
    weights = [a_pre_norm, a_w_in[0], a_conv_w[0], a_w_out[0], a_post_norm, kvn, w_kv, rel_bias.T, b_pre_norm,
               b_w_in[0], b_sinks, b_w_out[0], b_post_norm]
    grads = [mine(s_a[0:1]), r_win, mine(s_a[1:4]), r_wout, mine(s_b[2:3]), s_b[1:2], r_wkv,
             s_relb[:, :N_BUCKETS], s_b[0:1], r_wbin, s_sink[0:1, :N_Q_HEADS], r_wbout, s_c[0:1]]
    first = [m_a_pre_norm, m_a_w_in[0], m_a_conv_w[0], m_a_w_out[0], m_a_post_norm, m_kv_norm.reshape(1, d), m_w_kv,
             m_rel_bias.T, m_b_pre_norm, m_b_w_in[0], m_b_sinks, m_b_w_out[0], m_b_post_norm]
    second = [v_a_pre_norm, v_a_w_in[0], v_a_conv_w[0], v_a_w_out[0], v_a_post_norm, v_kv_norm.reshape(1, d), v_w_kv,
              v_rel_bias.T, v_b_pre_norm, v_b_w_in[0], v_b_sinks, v_b_w_out[0], v_b_post_norm]
    grads, deltas, new_m, new_v = _adamw(weights, grads, first, second)

    shapes = [a_pre_norm.shape, a_w_in.shape, a_conv_w.shape, a_w_out.shape, a_post_norm.shape, kv_norm.shape,
              w_kv.shape, None, b_pre_norm.shape, b_w_in.shape, b_sinks.shape, b_w_out.shape, b_post_norm.shape]
    shaped = lambda arrays: [a.T if s is None else a.reshape(s) for a, s in zip(arrays, shapes)]
    return (loss, gx.reshape(x.shape), *shaped(grads), *shaped(deltas), *shaped(new_m), *shaped(new_v))
```

```python
import functools
import math

import jax
import jax.numpy as jnp
from jax import lax
from jax.experimental import pallas as pl
from jax.experimental.pallas import tpu as pltpu

HEAD_DIM = 64
N_Q_HEADS = 16
N_KV_HEADS = 2
GROUP = N_Q_HEADS // N_KV_HEADS
BLOCK = 128
N_BUCKETS = 32
MAX_DISTANCE = 128
EPS = 1e-6
NEG_INF = -1e30
SCALE = HEAD_DIM ** -0.5

ADAM_LR = 0.001
ADAM_B1 = 0.9
ADAM_B2 = 0.999
ADAM_EPS = 1e-08
ADAM_WD = 0.01
ADAM_STEP = 10

N_DEV = 8
LANES = 128
F32 = jnp.float32
BF16 = jnp.bfloat16
MESH = pl.DeviceIdType.MESH
MIB = 1024 * 1024


def _params(semantics=None, vmem_mib=48):
    return pltpu.CompilerParams(dimension_semantics=semantics, vmem_limit_bytes=vmem_mib * MIB)


def _full(shape):
    zeros = (0,) * len(shape)
    return pl.BlockSpec(shape, lambda *_: zeros, pipeline_mode=pl.Buffered(1))


def _resident(shape):
    zeros = (0,) * len(shape)
    return pl.BlockSpec(shape, lambda *_: zeros)


def _rows(ts, cols):
    return pl.BlockSpec((ts, cols), lambda i: (i, 0))


def _dot(a, b):
    return jnp.dot(a, b, preferred_element_type=F32)


def _dot_nt(a, b):
    return lax.dot_general(a, b, (((1,), (1,)), ((), ())), preferred_element_type=F32)


def _dot_tn(a, b):
    return lax.dot_general(a, b, (((0,), (0,)), ((), ())), preferred_element_type=F32)


def _rms(xf):
    r = lax.rsqrt(jnp.mean(xf * xf, axis=-1, keepdims=True) + EPS)
    return xf * r, r


def _rms_bwd(dn, xn, r):
    return r * (dn - xn * jnp.mean(dn * xn, axis=-1, keepdims=True))


def _silu(z):
    s = jax.nn.sigmoid(z)
    return z * s, s * (1.0 + z * (1.0 - s))


def _my_index():
    return 4 * lax.axis_index("x") + 2 * lax.axis_index("y") + lax.axis_index("c")


def _all_gather(shards, out_dtypes):
    n = len(shards)

    def body(*refs):
        ins, outs = refs[:n], refs[n:2 * n]
        send_sems, recv_sems = refs[2 * n], refs[2 * n + 1]
        x, y, c = lax.axis_index("x"), lax.axis_index("y"), lax.axis_index("c")
        me, sibling = (x, y, c), (x, y, 1 - c)
        x_nbr, y_nbr, diagonal = (1 - x, y), (x, 1 - y), (1 - x, 1 - y)
        south = c == 0
        relayed = (jnp.where(south, 1 - x, x), jnp.where(south, y, 1 - y))
        relay_to = (jnp.where(south, x, 1 - x), jnp.where(south, 1 - y, y))

        def copy(t, k, block, to):
            rows = outs[t].at[4 * block[0] + 2 * block[1] + block[2]]
            return pltpu.make_async_remote_copy(
                src_ref=rows, dst_ref=rows, send_sem=send_sems.at[t, k], recv_sem=recv_sems.at[t, k],
                device_id=to, device_id_type=MESH)

        for t in range(n):
            outs[t][pl.ds(_my_index(), 1)] = ins[t][...].astype(outs[t].dtype)[None]
        started = []

        def start(cp):
            cp.start()
            started.append(cp)

        for t in range(n):
            start(copy(t, 0, me, sibling))
            start(copy(t, 1, me, (*x_nbr, c)))
            start(copy(t, 2, me, (*y_nbr, c)))
        for k, chip in ((1, x_nbr), (2, y_nbr)):
            for t in range(n):
                copy(t, k, (*chip, c), me).wait_recv()
                start(copy(t, 3 + k, (*chip, c), sibling))
        for t in range(n):
            start(copy(t, 3, (*relayed, c), (*relay_to, c)))
        for t in range(n):
            copy(t, 3, (*diagonal, c), me).wait_recv()
            start(copy(t, 6, (*diagonal, c), sibling))
        for t in range(n):
            copy(t, 0, sibling, me).wait_recv()
        for k, chip in ((4, x_nbr), (5, y_nbr), (6, diagonal)):
            for t in range(n):
                copy(t, k, (*chip, 1 - c), me).wait_recv()
        for cp in started:
            cp.wait_send()

    vmem = pl.BlockSpec(memory_space=pltpu.VMEM)
    return pl.pallas_call(
        body,
        name="gather_weights",
        out_shape=[jax.ShapeDtypeStruct((N_DEV,) + s.shape, dt) for s, dt in zip(shards, out_dtypes)],
        in_specs=[vmem] * n,
        out_specs=[vmem] * n,
        scratch_shapes=[pltpu.SemaphoreType.DMA((n, 7)), pltpu.SemaphoreType.DMA((n, 7))],
        compiler_params=_params(vmem_mib=48),
    )(*shards)


def _peer(k):
    x, y, c = lax.axis_index("x"), lax.axis_index("y"), lax.axis_index("c")
    px = 1 - x if k & 4 else x
    py = 1 - y if k & 2 else y
    pc = 1 - c if k & 1 else c
    return (px, py, pc), 4 * px + 2 * py + pc


def _exchange(srcs, dsts, send_sems, recv_sems, local_sems, scatter):
    me = _my_index()
    sends, arrivals = [], []
    for k in range(1, N_DEV):
        peer, pidx = _peer(k)
        for t, (src, dst) in enumerate(zip(srcs, dsts)):
            mine = src.at[pidx] if scatter else src
            sems = dict(send_sem=send_sems.at[t, k - 1], recv_sem=recv_sems.at[t, k - 1], device_id=peer, device_id_type=MESH)
            sends.append(pltpu.make_async_remote_copy(src_ref=mine, dst_ref=dst.at[me], **sems))
            arrivals.append(pltpu.make_async_remote_copy(src_ref=mine, dst_ref=dst.at[pidx], **sems))
    local = [pltpu.make_async_copy(src.at[me] if scatter else src, dst.at[me], local_sems.at[t])
             for t, (src, dst) in enumerate(zip(srcs, dsts))]
    return sends, arrivals, local


def _exchange_start(*args):
    sends, _, local = _exchange(*args)
    for cp in sends + local:
        cp.start()


def _exchange_wait(*args):
    sends, arrivals, local = _exchange(*args)
    for cp in arrivals:
        cp.wait_recv()
    for cp in sends:
        cp.wait_send()
    for cp in local:
        cp.wait()


def _exchange_sems(n):
    if not n:
        return []
    return [pltpu.SemaphoreType.DMA((n, N_DEV - 1)), pltpu.SemaphoreType.DMA((n, N_DEV - 1)), pltpu.SemaphoreType.DMA((n,))]


HBM_SPEC = pl.BlockSpec(memory_space=pl.ANY)


def _sum_slots(recv_ref, out_ref):
    rows = out_ref.shape[0]
    chunk = min(rows, 128)

    def add(i, carry):
        r0 = pl.multiple_of(i * chunk, chunk)
        acc = recv_ref[0, pl.ds(r0, chunk), :].astype(F32)
        for dev in range(1, N_DEV):
            acc = acc + recv_ref[dev, pl.ds(r0, chunk), :].astype(F32)
        out_ref[pl.ds(r0, chunk), :] = acc
        return carry

    lax.fori_loop(0, rows // chunk, add, 0)


N_CHIPS = N_DEV // 2


def _rows_loop(rows, fn):
    chunk = min(rows, 128)

    def step(i, carry):
        fn(pl.ds(pl.multiple_of(i * chunk, chunk), chunk))
        return carry

    lax.fori_loop(0, rows // chunk, step, 0)


def _chip_reduce(g_ref, out_ref, sib_ref, land_ref, send_ref, sems):
    sib_send, sib_recv, ici_send, ici_recv = sems
    x, y, c = lax.axis_index("x"), lax.axis_index("y"), lax.axis_index("c")
    south = c == 0
    near =(jnp.where(south, 1 - x, x), jnp.where(south, y, 1 - y))
    far = (jnp.where(south, x, 1 - x), jnp.where(south, 1 - y, y))
    diagonal = (1 - x, 1 - y)
    rows = out_ref.shape[0]
    direct, fold, folded = 0, 1, 2

    def to_sibling(t):
        return pltpu.make_async_remote_copy(
            src_ref=g_ref.at[2 * t + 1 - c], dst_ref=sib_ref.at[t], send_sem=sib_send.at[t], recv_sem=sib_recv.at[t],
            device_id=(x, y, 1 - c), device_id_type=MESH)

    def ici(role, chip):
        return pltpu.make_async_remote_copy(
            src_ref=send_ref.at[role], dst_ref=land_ref.at[role], send_sem=ici_send.at[role],
            recv_sem=ici_recv.at[role], device_id=(*chip, c), device_id_type=MESH)

    def pair_sum(chip, r):
        t = 2 * chip[0] + chip[1]
        return g_ref[2 * t + c, r, :].astype(F32) + sib_ref[t, r, :].astype(F32)

    def swap():
        for t in range(N_CHIPS):
            to_sibling(t).start()

    def send():
        for t in range(N_CHIPS):
            to_sibling(t).wait_recv()
        for role, chip in ((fold, diagonal), (direct, near)):
            def fill(r, role=role, chip=chip):
                send_ref[role, r, :] = pair_sum(chip, r).astype(BF16)

            _rows_loop(rows, fill)
            ici(role, near).start()

    def forward():
        ici(fold, near).wait_recv()

        def fill(r):
            send_ref[folded, r, :] = (pair_sum(far, r) + land_ref[fold, r, :].astype(F32)).astype(BF16)

        _rows_loop(rows, fill)
        ici(folded, far).start()

    def finish():
        ici(direct, near).wait_recv()
        ici(folded, far).wait_recv()

        def total(r):
            mine = pair_sum((x, y), r)
            out_ref[r, :] = mine + land_ref[direct, r, :].astype(F32) + land_ref[folded, r, :].astype(F32)

        _rows_loop(rows, total)
        for t in range(N_CHIPS):
            to_sibling(t).wait_send()
        for role, chip in ((direct, near), (fold, near), (folded, far)):
            ici(role, chip).wait_send()

    return swap, send, forward, finish


def _chip_reduce_scratch(slot):
    return [pltpu.VMEM((N_CHIPS,) + slot, BF16), pltpu.VMEM((3,) + slot, BF16), pltpu.VMEM((3,) + slot, BF16),
            pltpu.SemaphoreType.DMA((N_CHIPS,)), pltpu.SemaphoreType.DMA((N_CHIPS,)),
            pltpu.SemaphoreType.DMA((3,)), pltpu.SemaphoreType.DMA((3,))]


def _reduce_exchange(part, landed, smalls):
    nl, ng = len(landed), len(smalls)
    n_out = 1 + nl + ng

    def body(*refs):
        p_in, l_in, s_in = refs[0], refs[1:1 + nl], refs[1 + nl:n_out]
        p_out, l_out, s_out = refs[n_out], refs[n_out + 1:n_out + 1 + nl], refs[n_out + 1 + nl:2 * n_out]
        scratch = refs[2 * n_out:]
        s_recv, (sib_ref, chip_ref, send_ref), sems = scratch[:ng], scratch[ng:ng + 3], scratch[ng + 3:]
        swap, send, forward, finish = _chip_reduce(p_in, p_out, sib_ref, chip_ref, send_ref, sems[:4])
        swap()
        _exchange_start(s_in, s_recv, *sems[4:], False)
        send()
        for t in range(nl):
            _sum_slots(l_in[t], l_out[t])
        forward()
        finish()
        _exchange_wait(s_in, s_recv, *sems[4:], False)
        for t in range(ng):
            acc = s_recv[t][0]
            for dev in range(1, N_DEV):
                acc = acc + s_recv[t][dev]
            s_out[t][...] = acc

    vmem = pl.BlockSpec(memory_space=pltpu.VMEM)
    slot = part.shape[1:]
    outs = pl.pallas_call(
        body,
        name="reduce_grads",
        out_shape=[jax.ShapeDtypeStruct(p.shape[1:], F32) for p in [part] + landed]
        + [jax.ShapeDtypeStruct(s.shape, F32) for s in smalls],
        in_specs=[vmem] * n_out,
        out_specs=[vmem] * n_out,
        scratch_shapes=[pltpu.VMEM((N_DEV,) + s.shape, F32) for s in smalls] + _chip_reduce_scratch(slot)
        + _exchange_sems(ng),
        compiler_params=_params(vmem_mib=56),
    )(part, *landed, *smalls)
    return outs[0], outs[1:1 + nl], outs[1 + nl:]


def _layer_a_fwd(x2, sm, win_g, wout, later, ts):
    seq, d = x2.shape
    width = wout.shape[0]
    half = win_g.shape[2]
    n_half = width // half
    nl = len(later)
    nt = seq // ts

    def body(x_ref, sm_ref, win_ref, wout_ref, *refs):
        shard_refs, refs = refs[:nl], refs[nl:]
        h1_ref, n1_ref, proj_ref, conv_ref, y_ref, ya_ref = refs[:6]
        gathered_refs, (vprev_ref, *sems) = refs[6:6 + nl], refs[6 + nl:]

        @pl.when(pl.program_id(0) == 0)
        def _():
            vprev_ref[...] = jnp.zeros_like(vprev_ref)
            _exchange_start(shard_refs, gathered_refs, *sems, False)

        @pl.when(pl.program_id(0) == nt - 1)
        def _():
            _exchange_wait(shard_refs, gathered_refs, *sems, False)

        xf = x_ref[...]
        xn, _ = _rms(xf)
        n1 = (xn * sm_ref[0:1, :]).astype(BF16)
        n1_ref[...] = n1
        row = lax.broadcasted_iota(jnp.int32, (ts, half), 0)
        ya = jnp.zeros((ts, d), F32)
        for hh in range(n_half):
            cols = slice(hh * half, (hh + 1) * half)
            parts = []
            for part in range(4):
                j = part * n_half + hh
                pj = _dot(n1, win_ref[j])
                proj_ref[:, j * half:(j + 1) * half] = pj.astype(BF16)
                parts.append(pj)
            b, c, u, z = parts
            v = c * u
            last1, last2 = vprev_ref[7:8, cols], vprev_ref[6:7, cols]
            v1 = jnp.where(row == 0, last1, pltpu.roll(v, 1, 0))
            v2 = jnp.where(row == 0, last2, jnp.where(row == 1, last1, pltpu.roll(v, 2, 0)))
            vprev_ref[:, cols] = v[ts - 8:ts, :]
            conv = sm_ref[1:2, cols] * v2 + sm_ref[2:3, cols] * v1 + sm_ref[3:4, cols] * v
            conv_ref[:, cols] = conv.astype(BF16)
            yh = (b * conv * _silu(z)[0]).astype(BF16)
            y_ref[:, cols] = yh
            ya = ya + _dot(yh, wout_ref[cols, :])
        ya_ref[...] = ya
        h1_ref[...] = xf + _rms(ya)[0] * sm_ref[4:5, :]

    outs = pl.pallas_call(
        body,
        name="layer_a_fwd",
        grid=(nt,),
        in_specs=[_rows(ts, d), _full(sm.shape), _full(win_g.shape), _full(wout.shape)] + [HBM_SPEC] * nl,
        out_specs=[_rows(ts, d), _rows(ts, d), _rows(ts, 4 * width), _rows(ts, width), _rows(ts, width), _rows(ts, d)]
        + [HBM_SPEC] * nl,
        out_shape=[
            jax.ShapeDtypeStruct((seq, d), F32),
            jax.ShapeDtypeStruct((seq, d), BF16),
            jax.ShapeDtypeStruct((seq, 4 * width), BF16),
            jax.ShapeDtypeStruct((seq, width), BF16),
            jax.ShapeDtypeStruct((seq, width), BF16),
            jax.ShapeDtypeStruct((seq, d), F32),
        ] + [jax.ShapeDtypeStruct((N_DEV,) + s.shape, s.dtype) for s in later],
        scratch_shapes=[pltpu.VMEM((8, width), F32)] + _exchange_sems(nl),
        compiler_params=_params(("arbitrary",), 56),
    )(x2, sm, win_g, wout, *later)
    return outs[:6], outs[6:]


N_PAIRS = N_Q_HEADS // 2
BAND = 2 * BLOCK


def _bias_table(rel_bias_t, bucket_t, in_window_t):
    def body(rb_ref, bucket_ref, win_ref, out_ref):
        bk = jnp.where(win_ref[...] != 0, bucket_ref[...], -1)
        has_prev = lax.broadcasted_iota(jnp.int32, bk.shape, 0) >= BLOCK
        for h in range(N_Q_HEADS):
            acc = jnp.full(bk.shape, NEG_INF, F32)
            for b in range(N_BUCKETS):
                acc = jnp.where(bk == b, rb_ref[h, b], acc)
            cols = slice((h % 2) * BLOCK, (h % 2 + 1) * BLOCK)
            out_ref[1, h // 2, :, cols] = acc
            out_ref[0, h // 2, :, cols] = jnp.where(has_prev, acc, NEG_INF)

    vmem = pl.BlockSpec(memory_space=pltpu.VMEM)
    return pl.pallas_call(
        body,
        name="bias_table",
        in_specs=[pl.BlockSpec(memory_space=pltpu.SMEM), vmem, vmem],
        out_specs=vmem,
        out_shape=jax.ShapeDtypeStruct((2, N_PAIRS, BAND, 2 * BLOCK), F32),
    )(rel_bias_t, bucket_t, in_window_t)


Q_BLOCKS = 4


def _banded_tiles(kvp_ref, kvc_ref):
    tile = kvc_ref[...].astype(F32)
    blocks = [kvp_ref[...].astype(F32)] + [tile[u * BLOCK:(u + 1) * BLOCK] for u in range(Q_BLOCKS)]
    return [_banded_kv(blocks[u], blocks[u + 1]) for u in range(Q_BLOCKS)]


def _bias_of(bias_ref, i, u, m):
    return bias_ref[jnp.minimum(i, 1) if u == 0 else 1, m]


def _banded_kv(kvp, kvc):
    kw = N_KV_HEADS * HEAD_DIM
    out = []
    for full in (jnp.concatenate([kvp[:, :kw], kvc[:, :kw]], axis=0), jnp.concatenate([kvp[:, kw:], kvc[:, kw:]], axis=0)):
        lo = lax.broadcasted_iota(jnp.int32, full.shape, 1) < HEAD_DIM
        rolled = pltpu.roll(full, HEAD_DIM, 1)
        x2 = [jnp.where(lo, full, rolled).astype(BF16), jnp.where(lo, rolled, full).astype(BF16)]
        ft = full.T
        x2t = [jnp.concatenate([ft[kh * HEAD_DIM:(kh + 1) * HEAD_DIM]] * 2, axis=0).astype(BF16) for kh in range(N_KV_HEADS)]
        out += [x2, x2t]
    return out


def _pair_rows(ref, rows, m, scale=None):
    both = ref[rows, m * LANES:(m + 1) * LANES].astype(F32)
    if scale is not None:
        both = both * scale
    lo = lax.broadcasted_iota(jnp.int32, both.shape, 1) < HEAD_DIM
    zero = jnp.zeros_like(both)
    return jnp.concatenate([jnp.where(lo, both, zero), jnp.where(lo, zero, both)], axis=0).astype(BF16)


def _pair_cols(res_t):
    top = lax.broadcasted_iota(jnp.int32, (LANES, BLOCK), 0) < HEAD_DIM
    return jnp.where(top, res_t[:, :BLOCK], res_t[:, BLOCK:]).T


def _sink_row(sink_ref, m):
    first = lax.broadcasted_iota(jnp.int32, (1, 2 * BLOCK), 1) < BLOCK
    return jnp.where(first, sink_ref[0, 2 * m], sink_ref[0, 2 * m + 1])


def _probs_t(k2, qpair, bias, sink):
    return _softmax_t(_dot_nt(k2, qpair) + bias, sink)


def _softmax_t(logits, sink):
    mx = jnp.maximum(jnp.max(logits, axis=0, keepdims=True), sink)
    p = jnp.exp(logits - mx)
    sink_p = jnp.exp(sink - mx)
    inv = 1.0 / (jnp.sum(p, axis=0, keepdims=True) + sink_p)
    return p * inv, sink_p * inv


def _layer_b_fwd(h1, target, kvn, bpre, wkv, wbin_g, biasm, sinks, wbout, bpost):
    seq, d = h1.shape
    kvw = wkv.shape[1]
    cw = wbin_g.shape[2]
    aw = N_Q_HEADS * HEAD_DIM
    per = aw // cw
    tile = Q_BLOCKS * BLOCK

    def body(sink_ref, h1_ref, tgt_ref, kvn_ref, bpre_ref, wkv_ref, wbin_ref, bias_ref, w_ref, g_ref,
             n3_ref, n4_ref, kvc_ref, q_ref, o_ref, dh2_ref, dyb_ref, dattn_ref, dz2_ref, acc_ref,
             attn_ref, z2_ref, kvp_ref):
        i = pl.program_id(0)

        @pl.when(i == 0)
        def _():
            acc_ref[...] = jnp.zeros_like(acc_ref)
            kvp_ref[...] = jnp.zeros_like(kvp_ref)

        hn, _ = _rms(h1_ref[...])
        n3 = (hn * kvn_ref[...]).astype(BF16)
        n4 = (hn * bpre_ref[...]).astype(BF16)
        n3_ref[...] = n3
        n4_ref[...] = n4
        kvc_ref[...] = _dot(n3, wkv_ref[...]).astype(BF16)
        for j in range(N_DEV):
            pj = _dot(n4, wbin_ref[j])
            if j < per:
                q_ref[:, j * cw:(j + 1) * cw] = pj.astype(BF16)
            else:
                z2_ref[:, (j - per) * cw:(j - per + 1) * cw] = pj

        banded = _banded_tiles(kvp_ref, kvc_ref)
        kvp_ref[...] = kvc_ref[tile - BLOCK:tile, :]
        units = [(u, m) for u in range(Q_BLOCKS) for m in range(N_PAIRS)]
        kv_of = lambda m: (2 * m) // GROUP
        logits, probs = {}, {}
        for step in range(len(units) + 2):
            if step < len(units):
                u, m = units[step]
                qpair = _pair_rows(q_ref, slice(u * BLOCK, (u + 1) * BLOCK), m, SCALE)
                logits[step] = _dot_nt(banded[u][0][kv_of(m)], qpair) + _bias_of(bias_ref, i, u, m)
            if 0 <= step - 1 < len(units):
                u, m = units[step - 1]
                probs[step - 1] = _softmax_t(logits.pop(step - 1), _sink_row(sink_ref, m))[0].astype(BF16)
            if 0 <= step - 2 < len(units):
                u, m = units[step - 2]
                out_t = _dot(banded[u][3][kv_of(m)], probs.pop(step - 2))
                attn_ref[u * BLOCK:(u + 1) * BLOCK, m * LANES:(m + 1) * LANES] = _pair_cols(out_t)
        attn = attn_ref[...]
        sz, dsz = _silu(z2_ref[...])
        o = (attn * sz).astype(BF16)
        o_ref[...] = o

        w = w_ref[...]
        yb = _dot(o, w)
        ybn, r = _rms(yb)
        g = g_ref[...]
        diff = h1_ref[...] + ybn * g - tgt_ref[...]
        dh2 = diff * (1.0 / d)
        dh2_ref[...] = dh2
        acc_ref[0:1, :] += jnp.sum(dh2 * ybn, axis=0, keepdims=True)
        tok = jnp.mean(diff * diff, axis=-1, keepdims=True)
        acc_ref[1:2, :] += 0.5 * jnp.sum(tok, axis=0, keepdims=True)
        dyb = _rms_bwd(dh2 * g, ybn, r).astype(BF16)
        dyb_ref[...] = dyb
        do = _dot_nt(dyb, w)
        dattn_ref[...] = (do * sz).astype(BF16)
        dz2_ref[...] = (do * attn * dsz).astype(BF16)

    blk = lambda w: pl.BlockSpec((tile, w), lambda i: (i, 0))
    return pl.pallas_call(
        body,
        name="layer_b_fwd",
        grid=(seq // tile,),
        in_specs=[
            pl.BlockSpec(memory_space=pltpu.SMEM),
            blk(d),
            blk(d),
            _full(kvn.shape),
            _full(bpre.shape),
            _full(wkv.shape),
            _full(wbin_g.shape),
            _full(biasm.shape),
            _full(wbout.shape),
            _full(bpost.shape),
        ],
        out_specs=[blk(d), blk(d), blk(kvw), blk(aw), blk(aw), blk(d), blk(d), blk(aw), blk(aw), _resident((8, d))],
        out_shape=[
            jax.ShapeDtypeStruct((seq, d), BF16),
            jax.ShapeDtypeStruct((seq, d), BF16),
            jax.ShapeDtypeStruct((seq, kvw), BF16),
            jax.ShapeDtypeStruct((seq, aw), BF16),
            jax.ShapeDtypeStruct((seq, aw), BF16),
            jax.ShapeDtypeStruct((seq, d), F32),
            jax.ShapeDtypeStruct((seq, d), BF16),
            jax.ShapeDtypeStruct((seq, aw), BF16),
            jax.ShapeDtypeStruct((seq, aw), BF16),
            jax.ShapeDtypeStruct((8, d), F32),
        ],
        scratch_shapes=[pltpu.VMEM((tile, aw), F32), pltpu.VMEM((tile, aw), F32), pltpu.VMEM((BLOCK, kvw), BF16)],
        compiler_params=_params(("arbitrary",), 56),
    )(sinks, h1, target, kvn, bpre, wkv, wbin_g, biasm, wbout, bpost)


def _attn_bwd(q, kv, dattn, biasm, sinks, ready):
    seq, aw = q.shape
    kvw = kv.shape[1]
    kw = N_KV_HEADS * HEAD_DIM
    nb = seq // BLOCK
    pairs_per_kv = N_PAIRS // N_KV_HEADS
    nr = len(ready)

    tile = Q_BLOCKS * BLOCK
    nsteps = seq // tile
    held = (Q_BLOCKS - 1) * BLOCK

    def body(sink_ref, q_ref, kvc_ref, kvp_ref, da_ref, bias_ref, *refs):
        ready_refs, (dq_ref, dkv_ref, dssum_ref, dsink_ref) = refs[:nr], refs[nr:nr + 4]
        landed_refs, scratch = refs[nr + 4:2 * nr + 4], refs[2 * nr + 4:]
        carry_ref, done_ref, qs_ref, dos_ref, dst_ref, pt_ref, *sems = scratch
        i = pl.program_id(0)

        @pl.when(i == 0)
        def _():
            dssum_ref[...] = jnp.zeros_like(dssum_ref)
            dsink_ref[...] = jnp.zeros_like(dsink_ref)
            carry_ref[...] = jnp.zeros_like(carry_ref)
            done_ref[...] = jnp.zeros_like(done_ref)
            if nr:
                _exchange_start(ready_refs, landed_refs, *sems, True)

        if nr:
            @pl.when(i == nsteps)
            def _():
                _exchange_wait(ready_refs, landed_refs, *sems, True)

        @pl.when(i < nsteps)
        def _():
            lo = lax.broadcasted_iota(jnp.int32, (BAND, LANES), 1) < HEAD_DIM
            head_lane = lax.broadcasted_iota(jnp.int32, (1, LANES), 1)
            banded = _banded_tiles(kvp_ref, kvc_ref)
            units = [(u, m) for u in range(Q_BLOCKS) for m in range(N_PAIRS)]
            dsink = jnp.zeros((1, LANES), F32)
            folded = {}
            logits, dps, dsbs = {}, {}, {}
            for step in range(len(units) + 2):
                if step < len(units):
                    u, m = units[step]
                    kh, rows = m // pairs_per_kv, slice((m % pairs_per_kv) * BAND, (m % pairs_per_kv + 1) * BAND)
                    qrows = slice(u * BLOCK, (u + 1) * BLOCK)
                    qpair = _pair_rows(q_ref, qrows, m, SCALE)
                    dopair = _pair_rows(da_ref, qrows, m)
                    qs_ref[u, kh, rows, :] = qpair
                    dos_ref[u, kh, rows, :] = dopair
                    logits[step] = _dot_nt(banded[u][0][kh], qpair) + _bias_of(bias_ref, i, u, m)
                    dps[step] = _dot_nt(banded[u][2][kh], dopair)
                if 0 <= step - 1 < len(units):
                    u, m = units[step - 1]
                    kh, rows = m // pairs_per_kv, slice((m % pairs_per_kv) * BAND, (m % pairs_per_kv + 1) * BAND)
                    pn, sink_p = _softmax_t(logits.pop(step - 1), _sink_row(sink_ref, m))
                    dp = dps.pop(step - 1)
                    delta = jnp.sum(pn * dp, axis=0, keepdims=True)
                    ds = pn * (dp - delta)
                    dssum_ref[m] += ds
                    sink_term = sink_p * delta
                    for e in range(2):
                        total = jnp.sum(sink_term[:, e * BLOCK:(e + 1) * BLOCK], axis=1, keepdims=True)
                        dsink = dsink - jnp.where(head_lane == 2 * m + e, total, 0.0)
                    dsbs[step - 1] = ds.astype(BF16)
                    dst_ref[u, kh, :, rows] = dsbs[step - 1]
                    pt_ref[u, kh, :, rows] = pn.astype(BF16)
                if 0 <= step - 2 < len(units):
                    u, m = units[step - 2]
                    kh = m // pairs_per_kv
                    dq_t = _dot(banded[u][1][kh], dsbs.pop(step - 2))
                    dq_ref[u * BLOCK:(u + 1) * BLOCK, m * LANES:(m + 1) * LANES] = (_pair_cols(dq_t) * SCALE).astype(BF16)
                    if m % pairs_per_kv == pairs_per_kv - 1:
                        for name, lhs_ref, rhs_ref in (("k", dst_ref, qs_ref), ("v", pt_ref, dos_ref)):
                            acc = _dot(lhs_ref[u, kh], rhs_ref[u, kh])
                            folded[u, kh, name] = acc + pltpu.roll(acc, HEAD_DIM, 1)
            dsink_ref[0:1, :] += dsink
            dkv = [jnp.concatenate([jnp.where(lo, folded[u, 0, n], folded[u, 1, n]) for n in ("k", "v")], axis=1)
                   for u in range(Q_BLOCKS)]

            @pl.when(i > 0)
            def _():
                if held:
                    dkv_ref[:held, :] = done_ref[...].astype(BF16)
                dkv_ref[held:, :] = (carry_ref[...] + dkv[0][:BLOCK]).astype(BF16)

            for u in range(Q_BLOCKS - 1):
                done_ref[u * BLOCK:(u + 1) * BLOCK, :] = dkv[u][BLOCK:] + dkv[u + 1][:BLOCK]
            carry_ref[...] = dkv[Q_BLOCKS - 1][BLOCK:]

        @pl.when(i == nsteps)
        def _():
            if held:
                dkv_ref[:held, :] = done_ref[...].astype(BF16)
            dkv_ref[held:, :] = carry_ref[...].astype(BF16)

    last = nsteps - 1
    blk = lambda w: pl.BlockSpec((tile, w), lambda i: (jnp.minimum(i, last), 0))
    outs = pl.pallas_call(
        body,
        name="attn_bwd",
        grid=(nsteps + 1,),
        in_specs=[
            pl.BlockSpec(memory_space=pltpu.SMEM),
            blk(aw),
            blk(kvw),
            pl.BlockSpec((BLOCK, kvw), lambda i: (jnp.clip(Q_BLOCKS * i - 1, 0, nb - 1), 0)),
            blk(aw),
            _full(biasm.shape),
        ] + [HBM_SPEC] * nr,
        out_specs=[
            blk(aw),
            pl.BlockSpec((tile, kvw), lambda i: (jnp.maximum(i - 1, 0), 0)),
            _resident(biasm.shape[1:]),
            _resident((8, LANES)),
        ] + [HBM_SPEC] * nr,
        out_shape=[
            jax.ShapeDtypeStruct((seq, aw), BF16),
            jax.ShapeDtypeStruct((seq, kvw), BF16),
            jax.ShapeDtypeStruct(biasm.shape[1:], F32),
            jax.ShapeDtypeStruct((8, LANES), F32),
        ] + [jax.ShapeDtypeStruct(g.shape, g.dtype) for g in ready],
        scratch_shapes=[
            pltpu.VMEM((BLOCK, kvw), F32),
            pltpu.VMEM((max(held, 8), kvw), F32),
            pltpu.VMEM((Q_BLOCKS, N_KV_HEADS, pairs_per_kv * BAND, LANES), BF16),
            pltpu.VMEM((Q_BLOCKS, N_KV_HEADS, pairs_per_kv * BAND, LANES), BF16),
            pltpu.VMEM((Q_BLOCKS, N_KV_HEADS, BAND, pairs_per_kv * BAND), BF16),
            pltpu.VMEM((Q_BLOCKS, N_KV_HEADS, BAND, pairs_per_kv * BAND), BF16),
        ] + _exchange_sems(nr),
        compiler_params=_params(("arbitrary",), 48),
    )(sinks, q, kv, kv, dattn, biasm, *ready)
    return outs[:4], outs[4:]


def _relbias_grad(dssum2, bucket_row, chunk):
    heads, n = dssum2.shape

    def body(a_ref, bucket_ref, out_ref):
        @pl.when(pl.program_id(0) == 0)
        def _():
            out_ref[...] = jnp.zeros_like(out_ref)

        a = a_ref[...]
        hi = a.astype(BF16)
        lo = (a - hi.astype(F32)).astype(BF16)
        onehot_t = (lax.broadcasted_iota(jnp.int32, (LANES, chunk), 0) == bucket_ref[...]).astype(F32).astype(BF16)
        out_ref[...] += _dot_nt(hi, onehot_t) + _dot_nt(lo, onehot_t)

    return pl.pallas_call(
        body,
        name="relbias_grad",
        grid=(n // chunk,),
        in_specs=[pl.BlockSpec((heads, chunk), lambda i: (0, i)), pl.BlockSpec((1, chunk), lambda i: (0, i))],
        out_specs=_resident((heads, LANES)),
        out_shape=jax.ShapeDtypeStruct((heads, LANES), F32),
        compiler_params=_params(("arbitrary",), 32),
    )(dssum2, bucket_row)


def _layer_b_in_bwd(dh2, dq, dz2, dkv, h1, ya, wbin_g, wkv, kvn, bpre, sm, ready, ts):
    seq, d = h1.shape
    aw = dq.shape[1]
    kvw = dkv.shape[1]
    cw = wbin_g.shape[2]
    per = aw // cw

    nr = len(ready)
    nt = seq // ts

    def body(dh2_ref, dq_ref, dz2_ref, dkv_ref, h1_ref, ya_ref, wbin_ref, wkv_ref, kvn_ref, bpre_ref, sm_ref, *refs):
        ready_refs, (dh1_ref, dya_ref, acc_ref) = refs[:nr], refs[nr:nr + 3]
        landed_refs, sems = refs[nr + 3:2 * nr + 3], refs[2 * nr + 3:]

        @pl.when(pl.program_id(0) == 0)
        def _():
            acc_ref[...] = jnp.zeros_like(acc_ref)
            _exchange_start(ready_refs, landed_refs, *sems, True)

        @pl.when(pl.program_id(0) == nt - 1)
        def _():
            _exchange_wait(ready_refs, landed_refs, *sems, True)

        dn4 = jnp.zeros((ts, d), F32)
        for j in range(N_DEV):
            src = dq_ref if j < per else dz2_ref
            jj = j % per
            dn4 = dn4 + _dot_nt(src[:, jj * cw:(jj + 1) * cw], wbin_ref[j])
        dn3 = _dot_nt(dkv_ref[...], wkv_ref[...])
        hn, r = _rms(h1_ref[...])
        acc_ref[0:1, :] += jnp.sum(dn4 * hn, axis=0, keepdims=True)
        acc_ref[1:2, :] += jnp.sum(dn3 * hn, axis=0, keepdims=True)
        dh1 = dh2_ref[...] + _rms_bwd(dn4 * bpre_ref[...] + dn3 * kvn_ref[...], hn, r)
        dh1_ref[...] = dh1
        yan, r2 = _rms(ya_ref[...])
        acc_ref[2:3, :] += jnp.sum(dh1 * yan, axis=0, keepdims=True)
        dya_ref[...] = _rms_bwd(dh1 * sm_ref[4:5, :], yan, r2).astype(BF16)

    outs = pl.pallas_call(
        body,
        name="layer_b_in_bwd",
        grid=(nt,),
        in_specs=[_rows(ts, d), _rows(ts, aw), _rows(ts, aw), _rows(ts, kvw), _rows(ts, d), _rows(ts, d),
                  _full(wbin_g.shape), _full(wkv.shape), _full(kvn.shape), _full(bpre.shape), _full(sm.shape)]
        + [HBM_SPEC] * nr,
        out_specs=[_rows(ts, d), _rows(ts, d), _resident((8, d))] + [HBM_SPEC] * nr,
        out_shape=[jax.ShapeDtypeStruct((seq, d), F32), jax.ShapeDtypeStruct((seq, d), BF16),
                   jax.ShapeDtypeStruct((8, d), F32)] + [jax.ShapeDtypeStruct(g.shape, g.dtype) for g in ready],
        scratch_shapes=_exchange_sems(nr),
        compiler_params=_params(("arbitrary",), 48),
    )(dh2, dq, dz2, dkv, h1, ya, wbin_g, wkv, kvn, bpre, sm, *ready)
    return outs[:3], outs[3:]


def _layer_a_bwd(dya, proj, conv, dh1, x2, wout, win_g, sm, ts):
    seq, d = x2.shape
    width = wout.shape[0]
    half = win_g.shape[2]
    n_half = width // half
    nt = seq // ts

    def body(dya_ref, proj_ref, conv_ref, dh1_ref, x_ref, wout_ref, win_ref, sm_ref, dproj_ref, gx_ref, acc_ref,
             dnext_ref):
        @pl.when(pl.program_id(0) == 0)
        def _():
            acc_ref[...] = jnp.zeros_like(acc_ref)
            dnext_ref[...] = jnp.zeros_like(dnext_ref)

        dy = _dot_nt(dya_ref[...], wout_ref[...])
        row = lax.broadcasted_iota(jnp.int32, (ts, half), 0)
        dn1 = jnp.zeros((ts, d), F32)
        for hh in range(n_half):
            cols = slice(hh * half, (hh + 1) * half)
            b, c, u, z = [proj_ref[:, (part * n_half + hh) * half:(part * n_half + hh + 1) * half].astype(F32)
                          for part in range(4)]
            cv = conv_ref[:, cols].astype(F32)
            dyh = dy[:, cols]
            sz, dsz = _silu(z)
            dconv = dyh * b * sz
            grads = [dyh * cv * sz, None, None, dyh * b * cv * dsz]
            next0, next1 = dnext_ref[0:1, cols], dnext_ref[1:2, cols]
            dc1 = jnp.where(row == ts - 1, next0, pltpu.roll(dconv, ts - 1, 0))
            dc2 = jnp.where(row == ts - 1, next1, jnp.where(row == ts - 2, next0, pltpu.roll(dconv, ts - 2, 0)))
            dnext_ref[:, cols] = dconv[0:8, :]
            v = c * u
            acc_ref[1:2, cols] += jnp.sum(dc2 * v, axis=0, keepdims=True)
            acc_ref[2:3, cols] += jnp.sum(dc1 * v, axis=0, keepdims=True)
            acc_ref[3:4, cols] += jnp.sum(dconv * v, axis=0, keepdims=True)
            dv = sm_ref[3:4, cols] * dconv + sm_ref[2:3, cols] * dc1 + sm_ref[1:2, cols] * dc2
            grads[1] = dv * u
            grads[2] = dv * c
            for part in range(4):
                j = part * n_half + hh
                gj = grads[part].astype(BF16)
                dproj_ref[:, j * half:(j + 1) * half] = gj
                dn1 = dn1 + _dot_nt(gj, win_ref[j])
        xn, r = _rms(x_ref[...])
        acc_ref[0:1, :] += jnp.sum(dn1 * xn, axis=0, keepdims=True)
        gx_ref[...] = dh1_ref[...] + _rms_bwd(dn1 * sm_ref[0:1, :], xn, r)

    rev = lambda w: pl.BlockSpec((ts, w), lambda i: (nt - 1 - i, 0))
    return pl.pallas_call(
        body,
        name="layer_a_bwd",
        grid=(nt,),
        in_specs=[rev(d), rev(4 * width), rev(width), rev(d), rev(d), _full(wout.shape), _full(win_g.shape), _full(sm.shape)],
        out_specs=[rev(4 * width), rev(d), _resident((8, d))],
        out_shape=[jax.ShapeDtypeStruct((seq, 4 * width), BF16), jax.ShapeDtypeStruct((seq, d), F32),
                   jax.ShapeDtypeStruct((8, d), F32)],
        scratch_shapes=[pltpu.VMEM((8, width), F32)],
        compiler_params=_params(("arbitrary",), 56),
    )(dya, proj, conv, dh1, x2, wout, win_g, sm)


def _wgrad(a, bs, n_slots, ts, name, ready=()):
    nr = len(ready)
    seq, k = a.shape
    nb_in = len(bs)
    n_each = bs[0].shape[1]
    n = nb_in * n_each
    bn = min(n_each, 1024)
    per_in = n_each // bn
    n_blocks = nb_in * per_in
    ns = seq // ts

    def b_spec(idx):
        def index(j, s):
            mine = j // per_in == idx
            row = jnp.where(mine, s, jnp.where(j // per_in > idx, ns - 1, 0))
            return (row, jnp.where(mine, j % per_in, jnp.where(j // per_in > idx, per_in - 1, 0)))
        return pl.BlockSpec((ts, bn), index)

    if n_slots:
        sw = n // n_slots
        spb = bn // sw
        out_shape = jax.ShapeDtypeStruct((n_slots, k, sw), BF16)
        out_spec = pl.BlockSpec((spb, k, sw), lambda j, s: (j, 0, 0))
    else:
        out_shape = jax.ShapeDtypeStruct((k, n), BF16)
        out_spec = pl.BlockSpec((k, bn), lambda j, s: (0, j))

    def body(a_ref, *refs):
        b_refs, ready_refs, o_ref = refs[:nb_in], refs[nb_in:nb_in + nr], refs[nb_in + nr]
        landed_refs, (acc_ref, *sems) = refs[nb_in + nr + 1:nb_in + 2 * nr + 1], refs[nb_in + 2 * nr + 1:]
        j, s = pl.program_id(0), pl.program_id(1)

        if nr:
            @pl.when(jnp.logical_and(j == 0, s == 0))
            def _():
                _exchange_start(ready_refs, landed_refs, *sems, True)

            @pl.when(jnp.logical_and(j == n_blocks - 1, s == ns - 1))
            def _():
                _exchange_wait(ready_refs, landed_refs, *sems, True)

        @pl.when(s == 0)
        def _():
            acc_ref[...] = jnp.zeros_like(acc_ref)

        for idx in range(nb_in):
            @pl.when(j // per_in == idx)
            def _(idx=idx):
                acc_ref[...] += _dot_tn(a_ref[...], b_refs[idx][...])

        @pl.when(s == ns - 1)
        def _():
            if n_slots:
                for e in range(spb):
                    o_ref[e] = acc_ref[:, e * sw:(e + 1) * sw].astype(BF16)
            else:
                o_ref[...] = acc_ref[...].astype(BF16)

    outs = pl.pallas_call(
        body,
        name=name,
        grid=(n_blocks, ns),
        in_specs=[pl.BlockSpec((ts, k), lambda j, s: (s, 0))] + [b_spec(idx) for idx in range(nb_in)] + [HBM_SPEC] * nr,
        out_specs=[out_spec] + [HBM_SPEC] * nr,
        out_shape=[out_shape] + [jax.ShapeDtypeStruct(g.shape, g.dtype) for g in ready],
        scratch_shapes=[pltpu.VMEM((k, bn), F32)] + (_exchange_sems(nr) if nr else []),
        compiler_params=_params(("arbitrary", "arbitrary"), 48),
    )(a, *bs, *ready)
    return (outs[0], outs[1:]) if nr else outs[0]


def _wgrad_tail(pairs, part, ts):
    n_tasks = len(pairs)
    seq, k = pairs[0][0].shape
    n = pairs[0][1].shape[1]
    ns = seq // ts
    total = n_tasks * ns
    per = k // N_DEV

    def spec(t, width):
        return pl.BlockSpec((ts, width), lambda j, s: (jnp.where(j == t, s, jnp.where(j > t, ns - 1, 0)), 0))

    def body(*refs):
        ab_refs, part_ref = refs[:2 * n_tasks], refs[2 * n_tasks]
        o_ref, red_ref, acc_ref, sib_ref, chip_ref, send_ref, *sems = refs[2 * n_tasks + 1:]
        j, s = pl.program_id(0), pl.program_id(1)
        flat = j * ns + s
        swap, send, forward, finish = _chip_reduce(part_ref, red_ref, sib_ref, chip_ref, send_ref, sems)

        @pl.when(flat == 0)
        def _():
            swap()

        @pl.when(flat == min(1, total - 1))
        def _():
            send()

        @pl.when(flat == min(total // 2 + 1, total - 1))
        def _():
            forward()

        @pl.when(s == 0)
        def _():
            acc_ref[...] = jnp.zeros_like(acc_ref)

        for t in range(n_tasks):
            @pl.when(j == t)
            def _(t=t):
                acc_ref[...] += _dot_tn(ab_refs[2 * t][...], ab_refs[2 * t + 1][...])

        @pl.when(s == ns - 1)
        def _():
            for dev in range(N_DEV):
                o_ref[dev] = acc_ref[dev * per:(dev + 1) * per, :].astype(BF16)

        @pl.when(flat == total - 1)
        def _():
            finish()

    slot = part.shape[1:]
    return pl.pallas_call(
        body,
        name="wgrad_tail",
        grid=(n_tasks, ns),
        in_specs=[spec(t, w) for t in range(n_tasks) for w in (k, n)] + [_full(part.shape)],
        out_specs=[pl.BlockSpec((N_DEV, per, n), lambda j, s: (0, j, 0)), _resident(slot)],
        out_shape=[jax.ShapeDtypeStruct((N_DEV, n_tasks * per, n), BF16), jax.ShapeDtypeStruct(slot, F32)],
        scratch_shapes=[pltpu.VMEM((k, n), F32)] + _chip_reduce_scratch(slot),
        compiler_params=_params(("arbitrary", "arbitrary"), 56),
    )(*[op for pair in pairs for op in pair], part)


def _adamw(ws, gs, ms, vs):
    n = len(ws)

    def step(w, g, m, v):
        m = ADAM_B1 * m + (1.0 - ADAM_B1) * g
        v = ADAM_B2 * v + (1.0 - ADAM_B2) * jnp.square(g)
        m_hat = m / (1.0 - ADAM_B1 ** ADAM_STEP)
        v_hat = v / (1.0 - ADAM_B2 ** ADAM_STEP)
        return g, -ADAM_LR * (m_hat / (jnp.sqrt(v_hat) + ADAM_EPS) + ADAM_WD * w), m, v

    def body(*refs):
        w_refs, g_refs, m_refs, v_refs = (refs[k * n:(k + 1) * n] for k in range(4))
        go_refs, d_refs, nm_refs, nv_refs = (refs[(4 + k) * n:(5 + k) * n] for k in range(4))
        for t in range(n):
            rows = w_refs[t].shape[0]
            if rows <= 128:
                go_refs[t][...], d_refs[t][...], nm_refs[t][...], nv_refs[t][...] = step(
                    w_refs[t][...], g_refs[t][...], m_refs[t][...], v_refs[t][...])
                continue
            chunk = 128

            def one(i, carry, t=t):
                r = pl.ds(pl.multiple_of(i * chunk, chunk), chunk)
                go_refs[t][r, :], d_refs[t][r, :], nm_refs[t][r, :], nv_refs[t][r, :] = step(
                    w_refs[t][r, :], g_refs[t][r, :], m_refs[t][r, :], v_refs[t][r, :])
                return carry

            lax.fori_loop(0, rows // chunk, one, 0)

    vmem = pl.BlockSpec(memory_space=pltpu.VMEM)
    outs = pl.pallas_call(
        body,
        name="adamw",
        in_specs=[vmem] * (4 * n),
        out_specs=[vmem] * (4 * n),
        out_shape=[jax.ShapeDtypeStruct(w.shape, F32) for w in ws] * 4,
        compiler_params=_params(vmem_mib=56),
    )(*ws, *gs, *ms, *vs)
    return outs[:n], outs[n:2 * n], outs[2 * n:3 * n], outs[3 * n:]


def _band_structure():
    q_loc = jnp.arange(BLOCK, dtype=jnp.int32)[:, None]
    s_loc = jnp.arange(2 * BLOCK, dtype=jnp.int32)[None, :]
    dist = q_loc + BLOCK - s_loc
    in_window = (dist >= 0) & (dist < BLOCK)
    dd = jnp.maximum(dist, 0)
    max_exact = N_BUCKETS // 2
    large = max_exact + (jnp.log(jnp.maximum(dd, 1).astype(F32) / max_exact) / math.log(MAX_DISTANCE / max_exact)
                         * (N_BUCKETS - max_exact)).astype(jnp.int32)
    bucket = jnp.where(dd < max_exact, dd, jnp.minimum(large, N_BUCKETS - 1))
    return bucket, in_window.astype(jnp.int32)


def _place_rows(a, row, rows=8):
    return jnp.pad(a, ((row, rows - row - a.shape[0]), (0, 0)))


def kernel(x, a_pre_norm, a_w_in, a_conv_w, a_w_out, a_post_norm, kv_norm, w_kv, rel_bias, b_pre_norm, b_w_in, b_sinks, b_w_out, b_post_norm, loss_target, m_a_pre_norm, m_a_w_in, m_a_conv_w, m_a_w_out, m_a_post_norm, m_kv_norm, m_w_kv, m_rel_bias, m_b_pre_norm, m_b_w_in, m_b_sinks, m_b_w_out, m_b_post_norm, v_a_pre_norm, v_a_w_in, v_a_conv_w, v_a_w_out, v_a_post_norm, v_kv_norm, v_w_kv, v_rel_bias, v_b_pre_norm, v_b_w_in, v_b_sinks, v_b_w_out, v_b_post_norm):
    seq, d = x.shape[1], x.shape[2]
    x2 = x.reshape(seq, d)
    target = loss_target.reshape(seq, d)
    shard = a_pre_norm.shape[1]
    me = _my_index()
    ts_a = min(seq, 512)
    ts = min(seq, 512)
    ts_w = min(seq, 2048)

    small = _place_rows(a_pre_norm, 0) + _place_rows(a_conv_w[0], 1) + _place_rows(a_post_norm, 4)
    win_g, wout_g, small_g = _all_gather([a_w_in[0], a_w_out[0], small], [BF16, BF16, F32])
    wout = wout_g.reshape(-1, wout_g.shape[2])
    sm = small_g.transpose(1, 0, 2).reshape(8, N_DEV * shard)
    kvn = kv_norm.reshape(1, d)

    (h1, n1, proj, conv, y, ya), (wkv_g, wbin_g, wbout_g) = _layer_a_fwd(
        x2, sm, win_g, wout, [w_kv.astype(BF16), b_w_in[0].astype(BF16), b_w_out[0].astype(BF16)], ts_a)
    wkv = wkv_g.reshape(-1, wkv_g.shape[2])
    wbout = wbout_g.reshape(-1, wbout_g.shape[2])
    bucket, in_window = _band_structure()
    biasm = _bias_table(rel_bias.T, bucket.T, in_window.T)
    n3, n4, kv, q, o, dh2, dyb, dattn, dz2, acc_c = _layer_b_fwd(
        h1, target, kvn, b_pre_norm, wkv, wbin_g, biasm, b_sinks, wbout, b_post_norm)

    (dq, dkv, dssum, dsink), _ = _attn_bwd(q, kv, dattn, biasm, b_sinks, [])
    by_head = dssum.reshape(N_PAIRS, BAND, 2, BLOCK).transpose(0, 2, 3, 1)
    relb = _relbias_grad(by_head.reshape(N_Q_HEADS, -1), bucket.reshape(1, -1), 4096)
    g_wkv = _wgrad(n3, [dkv], 0, ts_w, "wgrad_kv").reshape(wkv_g.shape)
    g_wbin = _wgrad(n4, [dq, dz2], N_DEV, ts_w, "wgrad_b_in")
    (dh1, dya, acc_b), (l_wkv, l_wbin) = _layer_b_in_bwd(
        dh2, dq, dz2, dkv, h1, ya, wbin_g, wkv, kvn, b_pre_norm, sm, [g_wkv, g_wbin], ts)
    dproj, gx, acc_a = _layer_a_bwd(dya, proj, conv, dh1, x2, wout, win_g, sm, ts_a)
    g_win = _wgrad(n1, [dproj], N_DEV, ts_w, "wgrad_a_in")
    g_outs, r_win = _wgrad_tail([(y, dya), (o, dyb)], g_win, min(seq, 1024))

    r_outs, (r_wkv, r_wbin), (s_a, s_b, s_c, s_relb, s_sink) = _reduce_exchange(
        g_outs, [l_wkv, l_wbin], [acc_a, acc_b, acc_c, relb, dsink])
    rows_out = wout_g.shape[1]
    r_wout, r_wbout = r_outs[:rows_out], r_outs[rows_out:]
    mine = lambda rows: lax.dynamic_slice_in_dim(rows, me * shard, shard, axis=1)
    loss = s_c[1, 0]
    weights = [a_pre_norm, a_w_in[0], a_conv_w[0], a_w_out[0], a_post_norm, kvn, w_kv, rel_bias.T, b_pre_norm,
               b_w_in[0], b_sinks, b_w_out[0], b_post_norm]
    grads = [mine(s_a[0:1]), r_win, mine(s_a[1:4]), r_wout, mine(s_b[2:3]), s_b[1:2], r_wkv,
             s_relb[:, :N_BUCKETS], s_b[0:1], r_wbin, s_sink[0:1, :N_Q_HEADS], r_wbout, s_c[0:1]]
    first = [m_a_pre_norm, m_a_w_in[0], m_a_conv_w[0], m_a_w_out[0], m_a_post_norm, m_kv_norm.reshape(1, d), m_w_kv,
             m_rel_bias.T, m_b_pre_norm, m_b_w_in[0], m_b_sinks, m_b_w_out[0], m_b_post_norm]
    second = [v_a_pre_norm, v_a_w_in[0], v_a_conv_w[0], v_a_w_out[0], v_a_post_norm, v_kv_norm.reshape(1, d), v_w_kv,
              v_rel_bias.T, v_b_pre_norm, v_b_w_in[0], v_b_sinks, v_b_w_out[0], v_b_post_norm]
    grads, deltas, new_m, new_v = _adamw(weights, grads, first, second)

    shapes = [a_pre_norm.shape, a_w_in.shape, a_conv_w.shape, a_w_out.shape, a_post_norm.shape, kv_norm.shape,
              w_kv.shape, None, b_pre_norm.shape, b_w_in.shape, b_sinks.shape, b_w_out.shape, b_post_norm.shape]
    shaped = lambda arrays: [a.T if s is None else a.reshape(s) for a, s in zip(arrays, shapes)]
    return (loss, gx.reshape(x.shape), *shaped(grads), *shaped(deltas), *shaped(new_m), *shaped(new_v))
```

```python
import math

import jax
import jax.numpy as jnp
from jax import lax
from jax.experimental import pallas as pl
from jax.experimental.pallas import tpu as pltpu

HEAD_DIM = 64
N_Q_HEADS = 16
N_KV_HEADS = 2
GROUP = N_Q_HEADS // N_KV_HEADS
BLOCK = 128
N_BUCKETS = 32
MAX_DISTANCE = 128
EPS = 1e-6
NEG_INF = -1e30
SCALE = HEAD_DIM ** -0.5

ADAM_LR = 0.001
ADAM_B1 = 0.9
ADAM_B2 = 0.999
ADAM_EPS = 1e-08
ADAM_WD = 0.01
ADAM_STEP = 10

N_PAIRS = N_Q_HEADS // 2
BAND = 2 * BLOCK

N_DEV = 8
LANES = 128
F32 = jnp.float32
BF16 = jnp.bfloat16
MESH = pl.DeviceIdType.MESH
MIB = 1024 * 1024


def _params(semantics=None, vmem_mib=48):
    return pltpu.CompilerParams(dimension_semantics=semantics, vmem_limit_bytes=vmem_mib * MIB)


def _full(shape):
    zeros = (0,) * len(shape)
    return pl.BlockSpec(shape, lambda *_: zeros, pipeline_mode=pl.Buffered(1))


def _resident(shape):
    zeros = (0,) * len(shape)
    return pl.BlockSpec(shape, lambda *_: zeros)


def _rows(ts, cols):
    return pl.BlockSpec((ts, cols), lambda i: (i, 0))


def _dot(a, b):
    return jnp.dot(a, b, preferred_element_type=F32)


def _dot_nt(a, b):
    return lax.dot_general(a, b, (((1,), (1,)), ((), ())), preferred_element_type=F32)


def _dot_tn(a, b):
    return lax.dot_general(a, b, (((0,), (0,)), ((), ())), preferred_element_type=F32)


def _rms(xf):
    r = lax.rsqrt(jnp.mean(xf * xf, axis=-1, keepdims=True) + EPS)
    return xf * r, r


def _rms_bwd(dn, xn, r):
    return r * (dn - xn * jnp.mean(dn * xn, axis=-1, keepdims=True))


def _silu(z):
    s = jax.nn.sigmoid(z)
    return z * s, s * (1.0 + z * (1.0 - s))


def _my_index():
    return 4 * lax.axis_index("x") + 2 * lax.axis_index("y") + lax.axis_index("c")


def _bias_table(rb_ref, bucket_ref, win_ref, out_ref):
    bk = jnp.where(win_ref[...] != 0, bucket_ref[...], -1)
    has_prev = lax.broadcasted_iota(jnp.int32, bk.shape, 0) >= BLOCK
    for h in range(N_Q_HEADS):
        acc = jnp.full(bk.shape, NEG_INF, F32)
        for b in range(N_BUCKETS):
            acc = jnp.where(bk == b, rb_ref[h, b], acc)
        cols = slice((h % 2) * BLOCK, (h % 2 + 1) * BLOCK)
        out_ref[1, h // 2, :, cols] = acc
        out_ref[0, h // 2, :, cols] = jnp.where(has_prev, acc, NEG_INF)


def _all_gather(shards, out_dtypes, rel_bias_t, bucket_t, in_window_t):
    n = len(shards)

    def body(*refs):
        ins, (rb_ref, bucket_ref, win_ref) = refs[:n], refs[n:n + 3]
        outs, bias_ref = refs[n + 3:2 * n + 3], refs[2 * n + 3]
        send_sems, recv_sems = refs[2 * n + 4], refs[2 * n + 5]
        x, y, c = lax.axis_index("x"), lax.axis_index("y"), lax.axis_index("c")
        me, sibling = (x, y, c), (x, y, 1 - c)
        x_nbr, y_nbr, diagonal = (1 - x, y), (x, 1 - y), (1 - x, 1 - y)
        south = c == 0
        relayed = (jnp.where(south, 1 - x, x), jnp.where(south, y, 1 - y))
        relay_to = (jnp.where(south, x, 1 - x), jnp.where(south, 1 - y, y))

        def copy(t, k, block, to):
            rows = outs[t].at[4 * block[0] + 2 * block[1] + block[2]]
            return pltpu.make_async_remote_copy(
                src_ref=rows, dst_ref=rows, send_sem=send_sems.at[t, k], recv_sem=recv_sems.at[t, k],
                device_id=to, device_id_type=MESH)

        for t in range(n):
            outs[t][pl.ds(_my_index(), 1)] = ins[t][...].astype(outs[t].dtype)[None]
        started = []

        def start(cp):
            cp.start()
            started.append(cp)

        for t in range(n):
            start(copy(t, 0, me, sibling))
            start(copy(t, 1, me, (*x_nbr, c)))
            start(copy(t, 2, me, (*y_nbr, c)))
        _bias_table(rb_ref, bucket_ref, win_ref, bias_ref)
        for k, chip in ((1, x_nbr), (2, y_nbr)):
            for t in range(n):
                copy(t, k, (*chip, c), me).wait_recv()
                start(copy(t, 3 + k, (*chip, c), sibling))
        for t in range(n):
            start(copy(t, 3, (*relayed, c), (*relay_to, c)))
        for t in range(n):
            copy(t, 3, (*diagonal, c), me).wait_recv()
            start(copy(t, 6, (*diagonal, c), sibling))
        for t in range(n):
            copy(t, 0, sibling, me).wait_recv()
        for k, chip in ((4, x_nbr), (5, y_nbr), (6, diagonal)):
            for t in range(n):
                copy(t, k, (*chip, 1 - c), me).wait_recv()
        for cp in started:
            cp.wait_send()

    vmem = pl.BlockSpec(memory_space=pltpu.VMEM)
    return pl.pallas_call(
        body,
        name="gather_weights",
        out_shape=[jax.ShapeDtypeStruct((N_DEV,) + s.shape, dt) for s, dt in zip(shards, out_dtypes)]
        + [jax.ShapeDtypeStruct((2, N_PAIRS, BAND, 2 * BLOCK), F32)],
        in_specs=[vmem] * n + [pl.BlockSpec(memory_space=pltpu.SMEM), vmem, vmem],
        out_specs=[vmem] * (n + 1),
        scratch_shapes=[pltpu.SemaphoreType.DMA((n, 7)), pltpu.SemaphoreType.DMA((n, 7))],
        compiler_params=_params(vmem_mib=48),
    )(*shards, rel_bias_t, bucket_t, in_window_t)


def _peer(k):
    x, y, c = lax.axis_index("x"), lax.axis_index("y"), lax.axis_index("c")
    px = 1 - x if k & 4 else x
    py = 1 - y if k & 2 else y
    pc = 1 - c if k & 1 else c
    return (px, py, pc), 4 * px + 2 * py + pc


def _exchange(srcs, dsts, send_sems, recv_sems, local_sems, scatter):
    me = _my_index()
    sends, arrivals = [], []
    for k in range(1, N_DEV):
        peer, pidx = _peer(k)
        for t, (src, dst) in enumerate(zip(srcs, dsts)):
            mine = src.at[pidx] if scatter else src
            sems = dict(send_sem=send_sems.at[t, k - 1], recv_sem=recv_sems.at[t, k - 1], device_id=peer, device_id_type=MESH)
            sends.append(pltpu.make_async_remote_copy(src_ref=mine, dst_ref=dst.at[me], **sems))
            arrivals.append(pltpu.make_async_remote_copy(src_ref=mine, dst_ref=dst.at[pidx], **sems))
    local = [pltpu.make_async_copy(src.at[me] if scatter else src, dst.at[me], local_sems.at[t])
             for t, (src, dst) in enumerate(zip(srcs, dsts))]
    return sends, arrivals, local


def _exchange_start(*args):
    sends, _, local = _exchange(*args)
    for cp in sends + local:
        cp.start()


def _exchange_wait(*args):
    sends, arrivals, local = _exchange(*args)
    for cp in arrivals:
        cp.wait_recv()
    for cp in sends:
        cp.wait_send()
    for cp in local:
        cp.wait()


def _exchange_sems(n):
    if not n:
        return []
    return [pltpu.SemaphoreType.DMA((n, N_DEV - 1)), pltpu.SemaphoreType.DMA((n, N_DEV - 1)), pltpu.SemaphoreType.DMA((n,))]


HBM_SPEC = pl.BlockSpec(memory_space=pl.ANY)


def _sum_slots(recv_ref, out_ref):
    rows = out_ref.shape[0]
    chunk = min(rows, 128)

    def add(i, carry):
        r0 = pl.multiple_of(i * chunk, chunk)
        acc = recv_ref[0, pl.ds(r0, chunk), :].astype(F32)
        for dev in range(1, N_DEV):
            acc = acc + recv_ref[dev, pl.ds(r0, chunk), :].astype(F32)
        out_ref[pl.ds(r0, chunk), :] = acc
        return carry

    lax.fori_loop(0, rows // chunk, add, 0)


N_CHIPS = N_DEV // 2


def _rows_loop(rows, fn):
    chunk = min(rows, 128)

    def step(i, carry):
        fn(pl.ds(pl.multiple_of(i * chunk, chunk), chunk))
        return carry

    lax.fori_loop(0, rows // chunk, step, 0)


def _chip_reduce(g_ref, out_ref, sib_ref, land_ref, send_ref, sems):
    sib_send, sib_recv, ici_send, ici_recv = sems
    x, y, c = lax.axis_index("x"), lax.axis_index("y"), lax.axis_index("c")
    south = c == 0
    near =(jnp.where(south, 1 - x, x), jnp.where(south, y, 1 - y))
    far = (jnp.where(south, x, 1 - x), jnp.where(south, 1 - y, y))
    diagonal = (1 - x, 1 - y)
    rows = out_ref.shape[0]
    direct, fold, folded = 0, 1, 2

    def to_sibling(t):
        return pltpu.make_async_remote_copy(
            src_ref=g_ref.at[2 * t + 1 - c], dst_ref=sib_ref.at[t], send_sem=sib_send.at[t], recv_sem=sib_recv.at[t],
            device_id=(x, y, 1 - c), device_id_type=MESH)

    def ici(role, chip):
        return pltpu.make_async_remote_copy(
            src_ref=send_ref.at[role], dst_ref=land_ref.at[role], send_sem=ici_send.at[role],
            recv_sem=ici_recv.at[role], device_id=(*chip, c), device_id_type=MESH)

    def pair_sum(chip, r):
        t = 2 * chip[0] + chip[1]
        return g_ref[2 * t + c, r, :].astype(F32) + sib_ref[t, r, :].astype(F32)

    def swap():
        for t in range(N_CHIPS):
            to_sibling(t).start()

    def send():
        for t in range(N_CHIPS):
            to_sibling(t).wait_recv()
        for role, chip in ((fold, diagonal), (direct, near)):
            def fill(r, role=role, chip=chip):
                send_ref[role, r, :] = pair_sum(chip, r).astype(BF16)

            _rows_loop(rows, fill)
            ici(role, near).start()

    def forward():
        ici(fold, near).wait_recv()

        def fill(r):
            send_ref[folded, r, :] = (pair_sum(far, r) + land_ref[fold, r, :].astype(F32)).astype(BF16)

        _rows_loop(rows, fill)
        ici(folded, far).start()

    def finish():
        ici(direct, near).wait_recv()
        ici(folded, far).wait_recv()

        def total(r):
            mine = pair_sum((x, y), r)
            out_ref[r, :] = mine + land_ref[direct, r, :].astype(F32) + land_ref[folded, r, :].astype(F32)

        _rows_loop(rows, total)
        for t in range(N_CHIPS):
            to_sibling(t).wait_send()
        for role, chip in ((direct, near), (fold, near), (folded, far)):
            ici(role, chip).wait_send()

    return swap, send, forward, finish


def _chip_reduce_scratch(slot):
    return [pltpu.VMEM((N_CHIPS,) + slot, BF16), pltpu.VMEM((3,) + slot, BF16), pltpu.VMEM((3,) + slot, BF16),
            pltpu.SemaphoreType.DMA((N_CHIPS,)), pltpu.SemaphoreType.DMA((N_CHIPS,)),
            pltpu.SemaphoreType.DMA((3,)), pltpu.SemaphoreType.DMA((3,))]


def _reduce_exchange(part, landed, smalls):
    nl, ng = len(landed), len(smalls)
    n_out = 1 + nl + ng

    def body(*refs):
        p_in, l_in, s_in = refs[0], refs[1:1 + nl], refs[1 + nl:n_out]
        p_out, l_out, s_out = refs[n_out], refs[n_out + 1:n_out + 1 + nl], refs[n_out + 1 + nl:2 * n_out]
        scratch = refs[2 * n_out:]
        s_recv, (sib_ref, chip_ref, send_ref), sems = scratch[:ng], scratch[ng:ng + 3], scratch[ng + 3:]
        swap, send, forward, finish = _chip_reduce(p_in, p_out, sib_ref, chip_ref, send_ref, sems[:4])
        swap()
        _exchange_start(s_in, s_recv, *sems[4:], False)
        send()
        for t in range(nl):
            _sum_slots(l_in[t], l_out[t])
        forward()
        finish()
        _exchange_wait(s_in, s_recv, *sems[4:], False)
        for t in range(ng):
            acc = s_recv[t][0]
            for dev in range(1, N_DEV):
                acc = acc + s_recv[t][dev]
            s_out[t][...] = acc

    vmem = pl.BlockSpec(memory_space=pltpu.VMEM)
    slot = part.shape[1:]
    outs = pl.pallas_call(
        body,
        name="reduce_grads",
        out_shape=[jax.ShapeDtypeStruct(p.shape[1:], F32) for p in [part] + landed]
        + [jax.ShapeDtypeStruct(s.shape, F32) for s in smalls],
        in_specs=[vmem] * n_out,
        out_specs=[vmem] * n_out,
        scratch_shapes=[pltpu.VMEM((N_DEV,) + s.shape, F32) for s in smalls] + _chip_reduce_scratch(slot)
        + _exchange_sems(ng),
        compiler_params=_params(vmem_mib=56),
    )(part, *landed, *smalls)
    return outs[0], outs[1:1 + nl], outs[1 + nl:]


def _layer_a_fwd(x2, sm, win_g, wout, later, ts):
    seq, d = x2.shape
    width = wout.shape[0]
    half = win_g.shape[2]
    n_half = width // half
    nl = len(later)
    nt = seq // ts

    def body(x_ref, sm_ref, win_ref, wout_ref, *refs):
        shard_refs, refs = refs[:nl], refs[nl:]
        h1_ref, n1_ref, proj_ref, conv_ref, y_ref, ya_ref = refs[:6]
        gathered_refs, (vprev_ref, *sems) = refs[6:6 + nl], refs[6 + nl:]

        @pl.when(pl.program_id(0) == 0)
        def _():
            vprev_ref[...] = jnp.zeros_like(vprev_ref)
            _exchange_start(shard_refs, gathered_refs, *sems, False)

        @pl.when(pl.program_id(0) == nt - 1)
        def _():
            _exchange_wait(shard_refs, gathered_refs, *sems, False)

        xf = x_ref[...]
        xn, _ = _rms(xf)
        n1 = (xn * sm_ref[0:1, :]).astype(BF16)
        n1_ref[...] = n1
        row = lax.broadcasted_iota(jnp.int32, (ts, half), 0)
        ya = jnp.zeros((ts, d), F32)
        for hh in range(n_half):
            cols = slice(hh * half, (hh + 1) * half)
            parts = []
            for part in range(4):
                j = part * n_half + hh
                pj = _dot(n1, win_ref[j])
                proj_ref[:, j * half:(j + 1) * half] = pj.astype(BF16)
                parts.append(pj)
            b, c, u, z = parts
            v = c * u
            last1, last2 = vprev_ref[7:8, cols], vprev_ref[6:7, cols]
            v1 = jnp.where(row == 0, last1, pltpu.roll(v, 1, 0))
            v2 = jnp.where(row == 0, last2, jnp.where(row == 1, last1, pltpu.roll(v, 2, 0)))
            vprev_ref[:, cols] = v[ts - 8:ts, :]
            conv = sm_ref[1:2, cols] * v2 + sm_ref[2:3, cols] * v1 + sm_ref[3:4, cols] * v
            conv_ref[:, cols] = conv.astype(BF16)
            yh = (b * conv * _silu(z)[0]).astype(BF16)
            y_ref[:, cols] = yh
            ya = ya + _dot(yh, wout_ref[cols, :])
        ya_ref[...] = ya
        h1_ref[...] = xf + _rms(ya)[0] * sm_ref[4:5, :]

    outs = pl.pallas_call(
        body,
        name="layer_a_fwd",
        grid=(nt,),
        in_specs=[_rows(ts, d), _full(sm.shape), _full(win_g.shape), _full(wout.shape)] + [HBM_SPEC] * nl,
        out_specs=[_rows(ts, d), _rows(ts, d), _rows(ts, 4 * width), _rows(ts, width), _rows(ts, width), _rows(ts, d)]
        + [HBM_SPEC] * nl,
        out_shape=[
            jax.ShapeDtypeStruct((seq, d), F32),
            jax.ShapeDtypeStruct((seq, d), BF16),
            jax.ShapeDtypeStruct((seq, 4 * width), BF16),
            jax.ShapeDtypeStruct((seq, width), BF16),
            jax.ShapeDtypeStruct((seq, width), BF16),
            jax.ShapeDtypeStruct((seq, d), F32),
        ] + [jax.ShapeDtypeStruct((N_DEV,) + s.shape, s.dtype) for s in later],
        scratch_shapes=[pltpu.VMEM((8, width), F32)] + _exchange_sems(nl),
        compiler_params=_params(("arbitrary",), 56),
    )(x2, sm, win_g, wout, *later)
    return outs[:6], outs[6:]


Q_BLOCKS = 4


def _banded_tiles(kvp_ref, kvc_ref):
    tile = kvc_ref[...].astype(F32)
    blocks = [kvp_ref[...].astype(F32)] + [tile[u * BLOCK:(u + 1) * BLOCK] for u in range(Q_BLOCKS)]
    return [_banded_kv(blocks[u], blocks[u + 1]) for u in range(Q_BLOCKS)]


def _bias_of(bias_ref, i, u, m):
    return bias_ref[jnp.minimum(i, 1) if u == 0 else 1, m]


def _banded_kv(kvp, kvc):
    kw = N_KV_HEADS * HEAD_DIM
    out = []
    for full in (jnp.concatenate([kvp[:, :kw], kvc[:, :kw]], axis=0), jnp.concatenate([kvp[:, kw:], kvc[:, kw:]], axis=0)):
        lo = lax.broadcasted_iota(jnp.int32, full.shape, 1) < HEAD_DIM
        rolled = pltpu.roll(full, HEAD_DIM, 1)
        x2 = [jnp.where(lo, full, rolled).astype(BF16), jnp.where(lo, rolled, full).astype(BF16)]
        ft = full.T
        x2t = [jnp.concatenate([ft[kh * HEAD_DIM:(kh + 1) * HEAD_DIM]] * 2, axis=0).astype(BF16) for kh in range(N_KV_HEADS)]
        out += [x2, x2t]
    return out


def _pair_rows(ref, rows, m, scale=None):
    both = ref[rows, m * LANES:(m + 1) * LANES].astype(F32)
    if scale is not None:
        both = both * scale
    lo = lax.broadcasted_iota(jnp.int32, both.shape, 1) < HEAD_DIM
    zero = jnp.zeros_like(both)
    return jnp.concatenate([jnp.where(lo, both, zero), jnp.where(lo, zero, both)], axis=0).astype(BF16)


def _pair_cols(res_t):
    top = lax.broadcasted_iota(jnp.int32, (LANES, BLOCK), 0) < HEAD_DIM
    return jnp.where(top, res_t[:, :BLOCK], res_t[:, BLOCK:]).T


def _sink_row(sink_ref, m):
    first = lax.broadcasted_iota(jnp.int32, (1, 2 * BLOCK), 1) < BLOCK
    return jnp.where(first, sink_ref[0, 2 * m], sink_ref[0, 2 * m + 1])


def _softmax_t(logits, sink):
    mx =jnp.maximum(jnp.max(logits, axis=0, keepdims=True), sink)
    p = jnp.exp(logits - mx)
    sink_p = jnp.exp(sink - mx)
    inv = 1.0 / (jnp.sum(p, axis=0, keepdims=True) + sink_p)
    return p * inv, sink_p * inv


def _layer_b_fwd(h1, target, kvn, bpre, wkv, wbin_g, biasm, sinks, wbout, bpost):
    seq, d = h1.shape
    kvw = wkv.shape[1]
    cw = wbin_g.shape[2]
    aw = N_Q_HEADS * HEAD_DIM
    per = aw // cw
    tile = Q_BLOCKS * BLOCK

    def body(sink_ref, h1_ref, tgt_ref, kvn_ref, bpre_ref, wkv_ref, wbin_ref, bias_ref, w_ref, g_ref,
             n3_ref, n4_ref, kvc_ref, q_ref, o_ref, dh2_ref, dyb_ref, dattn_ref, dz2_ref, acc_ref,
             attn_ref, z2_ref, kvp_ref):
        i = pl.program_id(0)

        @pl.when(i == 0)
        def _():
            acc_ref[...] = jnp.zeros_like(acc_ref)
            kvp_ref[...] = jnp.zeros_like(kvp_ref)

        hn, _ = _rms(h1_ref[...])
        n3 = (hn * kvn_ref[...]).astype(BF16)
        n4 = (hn * bpre_ref[...]).astype(BF16)
        n3_ref[...] = n3
        n4_ref[...] = n4
        kvc_ref[...] = _dot(n3, wkv_ref[...]).astype(BF16)
        for j in range(N_DEV):
            pj = _dot(n4, wbin_ref[j])
            if j < per:
                q_ref[:, j * cw:(j + 1) * cw] = pj.astype(BF16)
            else:
                z2_ref[:, (j - per) * cw:(j - per + 1) * cw] = pj

        banded = _banded_tiles(kvp_ref, kvc_ref)
        kvp_ref[...] = kvc_ref[tile - BLOCK:tile, :]
        units = [(u, m) for u in range(Q_BLOCKS) for m in range(N_PAIRS)]
        kv_of = lambda m: (2 * m) // GROUP
        logits, probs = {}, {}
        for step in range(len(units) + 2):
            if step < len(units):
                u, m = units[step]
                qpair = _pair_rows(q_ref, slice(u * BLOCK, (u + 1) * BLOCK), m, SCALE)
                logits[step] = _dot_nt(banded[u][0][kv_of(m)], qpair) + _bias_of(bias_ref, i, u, m)
            if 0 <= step - 1 < len(units):
                u, m = units[step - 1]
                probs[step - 1] = _softmax_t(logits.pop(step - 1), _sink_row(sink_ref, m))[0].astype(BF16)
            if 0 <= step - 2 < len(units):
                u, m = units[step - 2]
                out_t = _dot(banded[u][3][kv_of(m)], probs.pop(step - 2))
                attn_ref[u * BLOCK:(u + 1) * BLOCK, m * LANES:(m + 1) * LANES] = _pair_cols(out_t)
        attn = attn_ref[...]
        sz, dsz = _silu(z2_ref[...])
        o = (attn * sz).astype(BF16)
        o_ref[...] = o

        w = w_ref[...]
        yb = _dot(o, w)
        ybn, r = _rms(yb)
        g = g_ref[...]
        diff = h1_ref[...] + ybn * g - tgt_ref[...]
        dh2 = diff * (1.0 / d)
        dh2_ref[...] = dh2
        acc_ref[0:1, :] += jnp.sum(dh2 * ybn, axis=0, keepdims=True)
        tok = jnp.mean(diff * diff, axis=-1, keepdims=True)
        acc_ref[1:2, :] += 0.5 * jnp.sum(tok, axis=0, keepdims=True)
        dyb = _rms_bwd(dh2 * g, ybn, r).astype(BF16)
        dyb_ref[...] = dyb
        do = _dot_nt(dyb, w)
        dattn_ref[...] = (do * sz).astype(BF16)
        dz2_ref[...] = (do * attn * dsz).astype(BF16)

    blk = lambda w: pl.BlockSpec((tile, w), lambda i: (i, 0))
    return pl.pallas_call(
        body,
        name="layer_b_fwd",
        grid=(seq // tile,),
        in_specs=[
            pl.BlockSpec(memory_space=pltpu.SMEM),
            blk(d),
            blk(d),
            _full(kvn.shape),
            _full(bpre.shape),
            _full(wkv.shape),
            _full(wbin_g.shape),
            _full(biasm.shape),
            _full(wbout.shape),
            _full(bpost.shape),
        ],
        out_specs=[blk(d), blk(d), blk(kvw), blk(aw), blk(aw), blk(d), blk(d), blk(aw), blk(aw), _resident((8, d))],
        out_shape=[
            jax.ShapeDtypeStruct((seq, d), BF16),
            jax.ShapeDtypeStruct((seq, d), BF16),
            jax.ShapeDtypeStruct((seq, kvw), BF16),
            jax.ShapeDtypeStruct((seq, aw), BF16),
            jax.ShapeDtypeStruct((seq, aw), BF16),
            jax.ShapeDtypeStruct((seq, d), F32),
            jax.ShapeDtypeStruct((seq, d), BF16),
            jax.ShapeDtypeStruct((seq, aw), BF16),
            jax.ShapeDtypeStruct((seq, aw), BF16),
            jax.ShapeDtypeStruct((8, d), F32),
        ],
        scratch_shapes=[pltpu.VMEM((tile, aw), F32), pltpu.VMEM((tile, aw), F32), pltpu.VMEM((BLOCK, kvw), BF16)],
        compiler_params=_params(("arbitrary",), 56),
    )(sinks, h1, target, kvn, bpre, wkv, wbin_g, biasm, wbout, bpost)


def _attn_bwd(q, kv, dattn, biasm, sinks, ready):
    seq, aw = q.shape
    kvw = kv.shape[1]
    kw = N_KV_HEADS * HEAD_DIM
    nb = seq // BLOCK
    pairs_per_kv = N_PAIRS // N_KV_HEADS
    nr = len(ready)

    tile = Q_BLOCKS * BLOCK
    nsteps = seq // tile
    held = (Q_BLOCKS - 1) * BLOCK

    def body(sink_ref, q_ref, kvc_ref, kvp_ref, da_ref, bias_ref, *refs):
        ready_refs, (dq_ref, dkv_ref, dssum_ref, dsink_ref) = refs[:nr], refs[nr:nr + 4]
        landed_refs, scratch = refs[nr + 4:2 * nr + 4], refs[2 * nr + 4:]
        carry_ref, done_ref, qs_ref, dos_ref, dst_ref, pt_ref, *sems = scratch
        i = pl.program_id(0)

        @pl.when(i == 0)
        def _():
            dssum_ref[...] = jnp.zeros_like(dssum_ref)
            dsink_ref[...] = jnp.zeros_like(dsink_ref)
            carry_ref[...] = jnp.zeros_like(carry_ref)
            done_ref[...] = jnp.zeros_like(done_ref)
            if nr:
                _exchange_start(ready_refs, landed_refs, *sems, True)

        if nr:
            @pl.when(i == nsteps)
            def _():
                _exchange_wait(ready_refs, landed_refs, *sems, True)

        @pl.when(i < nsteps)
        def _():
            lo = lax.broadcasted_iota(jnp.int32, (BAND, LANES), 1) < HEAD_DIM
            head_lane = lax.broadcasted_iota(jnp.int32, (1, LANES), 1)
            banded = _banded_tiles(kvp_ref, kvc_ref)
            units = [(u, m) for u in range(Q_BLOCKS) for m in range(N_PAIRS)]
            dsink = jnp.zeros((1, LANES), F32)
            folded = {}
            logits, dps, dsbs = {}, {}, {}
            for step in range(len(units) + 2):
                if step < len(units):
                    u, m = units[step]
                    kh, rows = m // pairs_per_kv, slice((m % pairs_per_kv) * BAND, (m % pairs_per_kv + 1) * BAND)
                    qrows = slice(u * BLOCK, (u + 1) * BLOCK)
                    qpair = _pair_rows(q_ref, qrows, m, SCALE)
                    dopair = _pair_rows(da_ref, qrows, m)
                    qs_ref[u, kh, rows, :] = qpair
                    dos_ref[u, kh, rows, :] = dopair
                    logits[step] = _dot_nt(banded[u][0][kh], qpair) + _bias_of(bias_ref, i, u, m)
                    dps[step] = _dot_nt(banded[u][2][kh], dopair)
                if 0 <= step - 1 < len(units):
                    u, m = units[step - 1]
                    kh, rows = m // pairs_per_kv, slice((m % pairs_per_kv) * BAND, (m % pairs_per_kv + 1) * BAND)
                    pn, sink_p = _softmax_t(logits.pop(step - 1), _sink_row(sink_ref, m))
                    dp = dps.pop(step - 1)
                    delta = jnp.sum(pn * dp, axis=0, keepdims=True)
                    ds = pn * (dp - delta)
                    dssum_ref[m] += ds
                    sink_term = sink_p * delta
                    for e in range(2):
                        total = jnp.sum(sink_term[:, e * BLOCK:(e + 1) * BLOCK], axis=1, keepdims=True)
                        dsink = dsink - jnp.where(head_lane == 2 * m + e, total, 0.0)
                    dsbs[step - 1] = ds.astype(BF16)
                    dst_ref[u, kh, :, rows] = dsbs[step - 1]
                    pt_ref[u, kh, :, rows] = pn.astype(BF16)
                if 0 <= step - 2 < len(units):
                    u, m = units[step - 2]
                    kh = m // pairs_per_kv
                    dq_t = _dot(banded[u][1][kh], dsbs.pop(step - 2))
                    dq_ref[u * BLOCK:(u + 1) * BLOCK, m * LANES:(m + 1) * LANES] = (_pair_cols(dq_t) * SCALE).astype(BF16)
                    if m % pairs_per_kv == pairs_per_kv - 1:
                        for name, lhs_ref, rhs_ref in (("k", dst_ref, qs_ref), ("v", pt_ref, dos_ref)):
                            acc = _dot(lhs_ref[u, kh], rhs_ref[u, kh])
                            folded[u, kh, name] = acc + pltpu.roll(acc, HEAD_DIM, 1)
            dsink_ref[0:1, :] += dsink
            dkv = [jnp.concatenate([jnp.where(lo, folded[u, 0, n], folded[u, 1, n]) for n in ("k", "v")], axis=1)
                   for u in range(Q_BLOCKS)]

            @pl.when(i > 0)
            def _():
                if held:
                    dkv_ref[:held, :] = done_ref[...].astype(BF16)
                dkv_ref[held:, :] = (carry_ref[...] + dkv[0][:BLOCK]).astype(BF16)

            for u in range(Q_BLOCKS - 1):
                done_ref[u * BLOCK:(u + 1) * BLOCK, :] = dkv[u][BLOCK:] + dkv[u + 1][:BLOCK]
            carry_ref[...] = dkv[Q_BLOCKS - 1][BLOCK:]

        @pl.when(i == nsteps)
        def _():
            if held:
                dkv_ref[:held, :] = done_ref[...].astype(BF16)
            dkv_ref[held:, :] = carry_ref[...].astype(BF16)

    last = nsteps - 1
    blk = lambda w: pl.BlockSpec((tile, w), lambda i: (jnp.minimum(i, last), 0))
    outs = pl.pallas_call(
        body,
        name="attn_bwd",
        grid=(nsteps + 1,),
        in_specs=[
            pl.BlockSpec(memory_space=pltpu.SMEM),
            blk(aw),
            blk(kvw),
            pl.BlockSpec((BLOCK, kvw), lambda i: (jnp.clip(Q_BLOCKS * i - 1, 0, nb - 1), 0)),
            blk(aw),
            _full(biasm.shape),
        ] + [HBM_SPEC] * nr,
        out_specs=[
            blk(aw),
            pl.BlockSpec((tile, kvw), lambda i: (jnp.maximum(i - 1, 0), 0)),
            _resident(biasm.shape[1:]),
            _resident((8, LANES)),
        ] + [HBM_SPEC] * nr,
        out_shape=[
            jax.ShapeDtypeStruct((seq, aw), BF16),
            jax.ShapeDtypeStruct((seq, kvw), BF16),
            jax.ShapeDtypeStruct(biasm.shape[1:], F32),
            jax.ShapeDtypeStruct((8, LANES), F32),
        ] + [jax.ShapeDtypeStruct(g.shape, g.dtype) for g in ready],
        scratch_shapes=[
            pltpu.VMEM((BLOCK, kvw), F32),
            pltpu.VMEM((max(held, 8), kvw), F32),
            pltpu.VMEM((Q_BLOCKS, N_KV_HEADS, pairs_per_kv * BAND, LANES), BF16),
            pltpu.VMEM((Q_BLOCKS, N_KV_HEADS, pairs_per_kv * BAND, LANES), BF16),
            pltpu.VMEM((Q_BLOCKS, N_KV_HEADS, BAND, pairs_per_kv * BAND), BF16),
            pltpu.VMEM((Q_BLOCKS, N_KV_HEADS, BAND, pairs_per_kv * BAND), BF16),
        ] + _exchange_sems(nr),
        compiler_params=_params(("arbitrary",), 48),
    )(sinks, q, kv, kv, dattn, biasm, *ready)
    return outs[:4], outs[4:]


def _relbias_grad(dssum2, bucket_row, chunk):
    heads, n = dssum2.shape

    def body(a_ref, bucket_ref, out_ref):
        @pl.when(pl.program_id(0) == 0)
        def _():
            out_ref[...] = jnp.zeros_like(out_ref)

        a = a_ref[...]
        hi = a.astype(BF16)
        lo = (a - hi.astype(F32)).astype(BF16)
        onehot_t = (lax.broadcasted_iota(jnp.int32, (LANES, chunk), 0) == bucket_ref[...]).astype(F32).astype(BF16)
        out_ref[...] += _dot_nt(hi, onehot_t) + _dot_nt(lo, onehot_t)

    return pl.pallas_call(
        body,
        name="relbias_grad",
        grid=(n // chunk,),
        in_specs=[pl.BlockSpec((heads, chunk), lambda i: (0, i)), pl.BlockSpec((1, chunk), lambda i: (0, i))],
        out_specs=_resident((heads, LANES)),
        out_shape=jax.ShapeDtypeStruct((heads, LANES), F32),
        compiler_params=_params(("arbitrary",), 32),
    )(dssum2, bucket_row)


def _layer_b_in_bwd(dh2, dq, dz2, dkv, h1, ya, wbin_g, wkv, kvn, bpre, sm, ready, ts):
    seq, d = h1.shape
    aw = dq.shape[1]
    kvw = dkv.shape[1]
    cw = wbin_g.shape[2]
    per = aw // cw

    nr = len(ready)
    nt = seq // ts

    def body(dh2_ref, dq_ref, dz2_ref, dkv_ref, h1_ref, ya_ref, wbin_ref, wkv_ref, kvn_ref, bpre_ref, sm_ref, *refs):
        ready_refs, (dh1_ref, dya_ref, acc_ref) = refs[:nr], refs[nr:nr + 3]
        landed_refs, sems = refs[nr + 3:2 * nr + 3], refs[2 * nr + 3:]

        @pl.when(pl.program_id(0) == 0)
        def _():
            acc_ref[...] = jnp.zeros_like(acc_ref)
            _exchange_start(ready_refs, landed_refs, *sems, True)

        @pl.when(pl.program_id(0) == nt - 1)
        def _():
            _exchange_wait(ready_refs, landed_refs, *sems, True)

        dn4 = jnp.zeros((ts, d), F32)
        for j in range(N_DEV):
            src = dq_ref if j < per else dz2_ref
            jj = j % per
            dn4 = dn4 + _dot_nt(src[:, jj * cw:(jj + 1) * cw], wbin_ref[j])
        dn3 = _dot_nt(dkv_ref[...], wkv_ref[...])
        hn, r = _rms(h1_ref[...])
        acc_ref[0:1, :] += jnp.sum(dn4 * hn, axis=0, keepdims=True)
        acc_ref[1:2, :] += jnp.sum(dn3 * hn, axis=0, keepdims=True)
        dh1 = dh2_ref[...] + _rms_bwd(dn4 * bpre_ref[...] + dn3 * kvn_ref[...], hn, r)
        dh1_ref[...] = dh1
        yan, r2 = _rms(ya_ref[...])
        acc_ref[2:3, :] += jnp.sum(dh1 * yan, axis=0, keepdims=True)
        dya_ref[...] = _rms_bwd(dh1 * sm_ref[4:5, :], yan, r2).astype(BF16)

    outs = pl.pallas_call(
        body,
        name="layer_b_in_bwd",
        grid=(nt,),
        in_specs=[_rows(ts, d), _rows(ts, aw), _rows(ts, aw), _rows(ts, kvw), _rows(ts, d), _rows(ts, d),
                  _full(wbin_g.shape), _full(wkv.shape), _full(kvn.shape), _full(bpre.shape), _full(sm.shape)]
        + [HBM_SPEC] * nr,
        out_specs=[_rows(ts, d), _rows(ts, d), _resident((8, d))] + [HBM_SPEC] * nr,
        out_shape=[jax.ShapeDtypeStruct((seq, d), F32), jax.ShapeDtypeStruct((seq, d), BF16),
                   jax.ShapeDtypeStruct((8, d), F32)] + [jax.ShapeDtypeStruct(g.shape, g.dtype) for g in ready],
        scratch_shapes=_exchange_sems(nr),
        compiler_params=_params(("arbitrary",), 48),
    )(dh2, dq, dz2, dkv, h1, ya, wbin_g, wkv, kvn, bpre, sm, *ready)
    return outs[:3], outs[3:]


def _layer_a_bwd(dya, proj, conv, dh1, x2, wout, win_g, sm, ts):
    seq, d = x2.shape
    width = wout.shape[0]
    half = win_g.shape[2]
    n_half = width // half
    nt = seq // ts

    def body(dya_ref, proj_ref, conv_ref, dh1_ref, x_ref, wout_ref, win_ref, sm_ref, dproj_ref, gx_ref, acc_ref,
             dnext_ref):
        @pl.when(pl.program_id(0) == 0)
        def _():
            acc_ref[...] = jnp.zeros_like(acc_ref)
            dnext_ref[...] = jnp.zeros_like(dnext_ref)

        dy = _dot_nt(dya_ref[...], wout_ref[...])
        row = lax.broadcasted_iota(jnp.int32, (ts, half), 0)
        dn1 = jnp.zeros((ts, d), F32)
        for hh in range(n_half):
            cols = slice(hh * half, (hh + 1) * half)
            b, c, u, z = [proj_ref[:, (part * n_half + hh) * half:(part * n_half + hh + 1) * half].astype(F32)
                          for part in range(4)]
            cv = conv_ref[:, cols].astype(F32)
            dyh = dy[:, cols]
            sz, dsz = _silu(z)
            dconv = dyh * b * sz
            grads = [dyh * cv * sz, None, None, dyh * b * cv * dsz]
            next0, next1 = dnext_ref[0:1, cols], dnext_ref[1:2, cols]
            dc1 = jnp.where(row == ts - 1, next0, pltpu.roll(dconv, ts - 1, 0))
            dc2 = jnp.where(row == ts - 1, next1, jnp.where(row == ts - 2, next0, pltpu.roll(dconv, ts - 2, 0)))
            dnext_ref[:, cols] = dconv[0:8, :]
            v = c * u
            acc_ref[1:2, cols] += jnp.sum(dc2 * v, axis=0, keepdims=True)
            acc_ref[2:3, cols] += jnp.sum(dc1 * v, axis=0, keepdims=True)
            acc_ref[3:4, cols] += jnp.sum(dconv * v, axis=0, keepdims=True)
            dv = sm_ref[3:4, cols] * dconv + sm_ref[2:3, cols] * dc1 + sm_ref[1:2, cols] * dc2
            grads[1] = dv * u
            grads[2] = dv * c
            for part in range(4):
                j = part * n_half + hh
                gj = grads[part].astype(BF16)
                dproj_ref[:, j * half:(j + 1) * half] = gj
                dn1 = dn1 + _dot_nt(gj, win_ref[j])
        xn, r = _rms(x_ref[...])
        acc_ref[0:1, :] += jnp.sum(dn1 * xn, axis=0, keepdims=True)
        gx_ref[...] = dh1_ref[...] + _rms_bwd(dn1 * sm_ref[0:1, :], xn, r)

    rev = lambda w: pl.BlockSpec((ts, w), lambda i: (nt - 1 - i, 0))
    return pl.pallas_call(
        body,
        name="layer_a_bwd",
        grid=(nt,),
        in_specs=[rev(d), rev(4 * width), rev(width), rev(d), rev(d), _full(wout.shape), _full(win_g.shape), _full(sm.shape)],
        out_specs=[rev(4 * width), rev(d), _resident((8, d))],
        out_shape=[jax.ShapeDtypeStruct((seq, 4 * width), BF16), jax.ShapeDtypeStruct((seq, d), F32),
                   jax.ShapeDtypeStruct((8, d), F32)],
        scratch_shapes=[pltpu.VMEM((8, width), F32)],
        compiler_params=_params(("arbitrary",), 56),
    )(dya, proj, conv, dh1, x2, wout, win_g, sm)


def _wgrad(a, bs, n_slots, ts, name, ready=(), block_cols=1024):
    nr = len(ready)
    seq, k = a.shape
    nb_in = len(bs)
    n_each = bs[0].shape[1]
    n = nb_in * n_each
    bn = min(n_each, block_cols)
    per_in = n_each // bn
    n_blocks = nb_in * per_in
    ns = seq // ts

    def b_spec(idx):
        def index(j, s):
            mine = j // per_in == idx
            row = jnp.where(mine, s, jnp.where(j // per_in > idx, ns - 1, 0))
            return (row, jnp.where(mine, j % per_in, jnp.where(j // per_in > idx, per_in - 1, 0)))
        return pl.BlockSpec((ts, bn), index)

    if n_slots:
        sw = n // n_slots
        spb = bn // sw
        out_shape = jax.ShapeDtypeStruct((n_slots, k, sw), BF16)
        out_spec = pl.BlockSpec((spb, k, sw), lambda j, s: (j, 0, 0))
    else:
        out_shape = jax.ShapeDtypeStruct((k, n), BF16)
        out_spec = pl.BlockSpec((k, bn), lambda j, s: (0, j))

    def body(a_ref, *refs):
        b_refs, ready_refs, o_ref = refs[:nb_in], refs[nb_in:nb_in + nr], refs[nb_in + nr]
        landed_refs, (acc_ref, *sems) = refs[nb_in + nr + 1:nb_in + 2 * nr + 1], refs[nb_in + 2 * nr + 1:]
        j, s = pl.program_id(0), pl.program_id(1)

        if nr:
            @pl.when(jnp.logical_and(j == 0, s == 0))
            def _():
                _exchange_start(ready_refs, landed_refs, *sems, True)

            @pl.when(jnp.logical_and(j == n_blocks - 1, s == ns - 1))
            def _():
                _exchange_wait(ready_refs, landed_refs, *sems, True)

        @pl.when(s == 0)
        def _():
            acc_ref[...] = jnp.zeros_like(acc_ref)

        for idx in range(nb_in):
            @pl.when(j // per_in == idx)
            def _(idx=idx):
                acc_ref[...] += _dot_tn(a_ref[...], b_refs[idx][...])

        @pl.when(s == ns - 1)
        def _():
            if n_slots:
                for e in range(spb):
                    o_ref[e] = acc_ref[:, e * sw:(e + 1) * sw].astype(BF16)
            else:
                o_ref[...] = acc_ref[...].astype(BF16)

    outs = pl.pallas_call(
        body,
        name=name,
        grid=(n_blocks, ns),
        in_specs=[pl.BlockSpec((ts, k), lambda j, s: (s, 0))] + [b_spec(idx) for idx in range(nb_in)] + [HBM_SPEC] * nr,
        out_specs=[out_spec] + [HBM_SPEC] * nr,
        out_shape=[out_shape] + [jax.ShapeDtypeStruct(g.shape, g.dtype) for g in ready],
        scratch_shapes=[pltpu.VMEM((k, bn), F32)] + (_exchange_sems(nr) if nr else []),
        compiler_params=_params(("arbitrary", "arbitrary"), 48),
    )(a, *bs, *ready)
    return (outs[0], outs[1:]) if nr else outs[0]


def _wgrad_tail(pairs, part, ts):
    n_tasks = len(pairs)
    seq, k = pairs[0][0].shape
    n = pairs[0][1].shape[1]
    ns = seq // ts
    total = n_tasks * ns
    per = k // N_DEV

    def spec(t, width):
        return pl.BlockSpec((ts, width), lambda j, s: (jnp.where(j == t, s, jnp.where(j > t, ns - 1, 0)), 0))

    def body(*refs):
        ab_refs, part_ref = refs[:2 * n_tasks], refs[2 * n_tasks]
        o_ref, red_ref, acc_ref, sib_ref, chip_ref, send_ref, *sems = refs[2 * n_tasks + 1:]
        j, s = pl.program_id(0), pl.program_id(1)
        flat = j * ns + s
        swap, send, forward, finish = _chip_reduce(part_ref, red_ref, sib_ref, chip_ref, send_ref, sems)

        @pl.when(flat == 0)
        def _():
            swap()

        @pl.when(flat == min(1, total - 1))
        def _():
            send()

        @pl.when(flat == min(total // 2 + 1, total - 1))
        def _():
            forward()

        @pl.when(s == 0)
        def _():
            acc_ref[...] = jnp.zeros_like(acc_ref)

        for t in range(n_tasks):
            @pl.when(j == t)
            def _(t=t):
                acc_ref[...] += _dot_tn(ab_refs[2 * t][...], ab_refs[2 * t + 1][...])

        @pl.when(s == ns - 1)
        def _():
            for dev in range(N_DEV):
                o_ref[dev] = acc_ref[dev * per:(dev + 1) * per, :].astype(BF16)

        @pl.when(flat == total - 1)
        def _():
            finish()

    slot = part.shape[1:]
    return pl.pallas_call(
        body,
        name="wgrad_tail",
        grid=(n_tasks, ns),
        in_specs=[spec(t, w) for t in range(n_tasks) for w in (k, n)] + [_full(part.shape)],
        out_specs=[pl.BlockSpec((N_DEV, per, n), lambda j, s: (0, j, 0)), _resident(slot)],
        out_shape=[jax.ShapeDtypeStruct((N_DEV, n_tasks * per, n), BF16), jax.ShapeDtypeStruct(slot, F32)],
        scratch_shapes=[pltpu.VMEM((k, n), F32)] + _chip_reduce_scratch(slot),
        compiler_params=_params(("arbitrary", "arbitrary"), 56),
    )(*[op for pair in pairs for op in pair], part)


def _adamw(ws, gs, ms, vs):
    n = len(ws)

    def step(w, g, m, v):
        m = ADAM_B1 * m + (1.0 - ADAM_B1) * g
        v = ADAM_B2 * v + (1.0 - ADAM_B2) * jnp.square(g)
        m_hat = m / (1.0 - ADAM_B1 ** ADAM_STEP)
        v_hat = v / (1.0 - ADAM_B2 ** ADAM_STEP)
        return g, -ADAM_LR * (m_hat / (jnp.sqrt(v_hat) + ADAM_EPS) + ADAM_WD * w), m, v

    def body(*refs):
        w_refs, g_refs, m_refs, v_refs = (refs[k * n:(k + 1) * n] for k in range(4))
        go_refs, d_refs, nm_refs, nv_refs = (refs[(4 + k) * n:(5 + k) * n] for k in range(4))
        for t in range(n):
            rows = w_refs[t].shape[0]
            if rows <= 128:
                go_refs[t][...], d_refs[t][...], nm_refs[t][...], nv_refs[t][...] = step(
                    w_refs[t][...], g_refs[t][...], m_refs[t][...], v_refs[t][...])
                continue
            chunk = 128

            def one(i, carry, t=t):
                r = pl.ds(pl.multiple_of(i * chunk, chunk), chunk)
                go_refs[t][r, :], d_refs[t][r, :], nm_refs[t][r, :], nv_refs[t][r, :] = step(
                    w_refs[t][r, :], g_refs[t][r, :], m_refs[t][r, :], v_refs[t][r, :])
                return carry

            lax.fori_loop(0, rows // chunk, one, 0)

    vmem = pl.BlockSpec(memory_space=pltpu.VMEM)
    outs = pl.pallas_call(
        body,
        name="adamw",
        in_specs=[vmem] * (4 * n),
        out_specs=[vmem] * (4 * n),
        out_shape=[jax.ShapeDtypeStruct(w.shape, F32) for w in ws] * 4,
        compiler_params=_params(vmem_mib=56),
    )(*ws, *gs, *ms, *vs)
    return outs[:n], outs[n:2 * n], outs[2 * n:3 * n], outs[3 * n:]


def _band_structure():
    q_loc = jnp.arange(BLOCK, dtype=jnp.int32)[:, None]
    s_loc = jnp.arange(2 * BLOCK, dtype=jnp.int32)[None, :]
    dist = q_loc + BLOCK - s_loc
    in_window = (dist >= 0) & (dist < BLOCK)
    dd = jnp.maximum(dist, 0)
    max_exact = N_BUCKETS // 2
    large = max_exact + (jnp.log(jnp.maximum(dd, 1).astype(F32) / max_exact) / math.log(MAX_DISTANCE / max_exact)
                         * (N_BUCKETS - max_exact)).astype(jnp.int32)
    bucket = jnp.where(dd < max_exact, dd, jnp.minimum(large, N_BUCKETS - 1))
    return bucket, in_window.astype(jnp.int32)


def _place_rows(a, row, rows=8):
    return jnp.pad(a, ((row, rows - row - a.shape[0]), (0, 0)))


def kernel(x, a_pre_norm, a_w_in, a_conv_w, a_w_out, a_post_norm, kv_norm, w_kv, rel_bias, b_pre_norm, b_w_in, b_sinks, b_w_out, b_post_norm, loss_target, m_a_pre_norm, m_a_w_in, m_a_conv_w, m_a_w_out, m_a_post_norm, m_kv_norm, m_w_kv, m_rel_bias, m_b_pre_norm, m_b_w_in, m_b_sinks, m_b_w_out, m_b_post_norm, v_a_pre_norm, v_a_w_in, v_a_conv_w, v_a_w_out, v_a_post_norm, v_kv_norm, v_w_kv, v_rel_bias, v_b_pre_norm, v_b_w_in, v_b_sinks, v_b_w_out, v_b_post_norm):
    seq, d = x.shape[1], x.shape[2]
    x2 = x.reshape(seq, d)
    target = loss_target.reshape(seq, d)
    shard = a_pre_norm.shape[1]
    me = _my_index()
    ts_a = min(seq, 512)
    ts = min(seq, 512)
    ts_w = min(seq, 2048)

    small = _place_rows(a_pre_norm, 0) + _place_rows(a_conv_w[0], 1) + _place_rows(a_post_norm, 4)
    bucket, in_window = _band_structure()
    win_g, wout_g, small_g, biasm = _all_gather(
        [a_w_in[0], a_w_out[0], small], [BF16, BF16, F32], rel_bias.T, bucket.T, in_window.T)
    wout = wout_g.reshape(-1, wout_g.shape[2])
    sm = small_g.transpose(1, 0, 2).reshape(8, N_DEV * shard)
    kvn = kv_norm.reshape(1, d)

    (h1, n1, proj, conv, y, ya), (wkv_g, wbin_g, wbout_g) = _layer_a_fwd(
        x2, sm, win_g, wout, [w_kv.astype(BF16), b_w_in[0].astype(BF16), b_w_out[0].astype(BF16)], ts_a)
    wkv = wkv_g.reshape(-1, wkv_g.shape[2])
    wbout = wbout_g.reshape(-1, wbout_g.shape[2])
    n3, n4, kv, q, o, dh2, dyb, dattn, dz2, acc_c = _layer_b_fwd(
        h1, target, kvn, b_pre_norm, wkv, wbin_g, biasm, b_sinks, wbout, b_post_norm)

    (dq, dkv, dssum, dsink), _ = _attn_bwd(q, kv, dattn, biasm, b_sinks, [])
    by_head = dssum.reshape(N_PAIRS, BAND, 2, BLOCK).transpose(0, 2, 3, 1)
    relb = _relbias_grad(by_head.reshape(N_Q_HEADS, -1), bucket.reshape(1, -1), 4096)
    g_wkv = _wgrad(n3, [dkv], 0, ts_w, "wgrad_kv").reshape(wkv_g.shape)
    g_wbin = _wgrad(n4, [dq, dz2], N_DEV, ts_w, "wgrad_b_in")
    (dh1, dya, acc_b), (l_wkv, l_wbin) = _layer_b_in_bwd(
        dh2, dq, dz2, dkv, h1, ya, wbin_g, wkv, kvn, b_pre_norm, sm, [g_wkv, g_wbin], ts)
    dproj, gx, acc_a = _layer_a_bwd(dya, proj, conv, dh1, x2, wout, win_g, sm, ts_a)
    g_win = _wgrad(n1, [dproj], N_DEV, ts_w, "wgrad_a_in", block_cols=2048)
    g_outs, r_win = _wgrad_tail([(y, dya), (o, dyb)], g_win, min(seq, 1024))

    r_outs, (r_wkv, r_wbin), (s_a, s_b, s_c, s_relb, s_sink) = _reduce_exchange(
        g_outs, [l_wkv, l_wbin], [acc_a, acc_b, acc_c, relb, dsink])
    rows_out = wout_g.shape[1]
    r_wout, r_wbout = r_outs[:rows_out], r_outs[rows_out:]
    mine = lambda rows: lax.dynamic_slice_in_dim(rows, me * shard, shard, axis=1)
    loss = s_c[1, 0]
    weights = [a_pre_norm, a_w_in[0], a_conv_w[0], a_w_out[0], a_post_norm, kvn, w_kv, rel_bias.T, b_pre_norm,
               b_w_in[0], b_sinks, b_w_out[0], b_post_norm]
    grads = [mine(s_a[0:1]), r_win, mine(s_a[1:4]), r_wout, mine(s_b[2:3]), s_b[1:2], r_wkv,
             s_relb[:, :N_BUCKETS], s_b[0:1], r_wbin, s_sink[0:1, :N_Q_HEADS], r_wbout, s_c[0:1]]
    first = [m_a_pre_norm, m_a_w_in[0], m_a_conv_w[0], m_a_w_out[0], m_a_post_norm, m_kv_norm.reshape(1, d), m_w_kv,
             m_rel_bias.T, m_b_pre_norm, m_b_w_in[0], m_b_sinks, m_b_w_out[0], m_b_post_norm]
    second = [v_a_pre_norm, v_a_w_in[0], v_a_conv_w[0], v_a_w_out[0], v_a_post_norm, v_kv_norm.reshape(1, d), v_w_kv,
              v_rel_bias.T, v_b_pre_norm, v_b_w_in[0], v_b_sinks, v_b_w_out[0], v_b_post_norm]
    grads, deltas, new_m, new_v = _adamw(weights, grads, first, second)

    shapes = [a_pre_norm.shape, a_w_in.shape, a_conv_w.shape, a_w_out.shape, a_post_norm.shape, kv_norm.shape,
              w_kv.shape, None, b_pre_norm.shape, b_w_in.shape, b_sinks.shape, b_w_out.shape, b_post_norm.shape]
    shaped = lambda arrays: [a.T if s is None else a.reshape(s) for a, s in zip(arrays, shapes)]
    return (loss, gx.reshape(x.shape), *shaped(grads), *shaped(deltas), *shaped(new_m), *shaped(new_v))
```

```python
import math

import jax
import jax.numpy as jnp
from jax import lax
from jax.experimental import pallas as pl
from jax.experimental.pallas import tpu as pltpu

HEAD_DIM = 64
N_Q_HEADS = 16
N_KV_HEADS = 2
GROUP = N_Q_HEADS // N_KV_HEADS
BLOCK = 128
N_BUCKETS = 32
MAX_DISTANCE = 128
EPS = 1e-6
NEG_INF = -1e30
SCALE = HEAD_DIM ** -0.5

ADAM_LR = 0.001
ADAM_B1 = 0.9
ADAM_B2 = 0.999
ADAM_EPS = 1e-08
ADAM_WD = 0.01
ADAM_STEP = 10

N_PAIRS = N_Q_HEADS // 2
BAND = 2 * BLOCK

N_DEV = 8
GATHER_PIECE_ROWS = 256
LANES = 128
F32 = jnp.float32
BF16 = jnp.bfloat16
MESH = pl.DeviceIdType.MESH
MIB = 1024 * 1024


def _params(semantics=None, vmem_mib=48):
    return pltpu.CompilerParams(dimension_semantics=semantics, vmem_limit_bytes=vmem_mib * MIB)


def _full(shape):
    zeros = (0,) * len(shape)
    return pl.BlockSpec(shape, lambda *_: zeros, pipeline_mode=pl.Buffered(1))


def _resident(shape):
    zeros = (0,) * len(shape)
    return pl.BlockSpec(shape, lambda *_: zeros)


def _rows(ts, cols):
    return pl.BlockSpec((ts, cols), lambda i: (i, 0))


def _dot(a, b):
    return jnp.dot(a, b, preferred_element_type=F32)


def _dot_nt(a, b):
    return lax.dot_general(a, b, (((1,), (1,)), ((), ())), preferred_element_type=F32)


def _dot_tn(a, b):
    return lax.dot_general(a, b, (((0,), (0,)), ((), ())), preferred_element_type=F32)


def _rms(xf):
    r = lax.rsqrt(jnp.mean(xf * xf, axis=-1, keepdims=True) + EPS)
    return xf * r, r


def _rms_bwd(dn, xn, r):
    return r * (dn - xn * jnp.mean(dn * xn, axis=-1, keepdims=True))


def _silu(z):
    s = jax.nn.sigmoid(z)
    return z * s, s * (1.0 + z * (1.0 - s))


def _my_index():
    return 4 * lax.axis_index("x") + 2 * lax.axis_index("y") + lax.axis_index("c")


def _bias_table(rb_ref, bucket_ref, win_ref, out_ref):
    bk = jnp.where(win_ref[...] != 0, bucket_ref[...], -1)
    has_prev = lax.broadcasted_iota(jnp.int32, bk.shape, 0) >= BLOCK
    for h in range(N_Q_HEADS):
        acc = jnp.full(bk.shape, NEG_INF, F32)
        for b in range(N_BUCKETS):
            acc = jnp.where(bk == b, rb_ref[h, b], acc)
        cols = slice((h % 2) * BLOCK, (h % 2 + 1) * BLOCK)
        out_ref[1, h // 2, :, cols] = acc
        out_ref[0, h // 2, :, cols] = jnp.where(has_prev, acc, NEG_INF)


def _all_gather(shards, out_dtypes, rel_bias_t, bucket_t, in_window_t):
    n = len(shards)
    pieces = [(t, r0, min(GATHER_PIECE_ROWS, s.shape[0] - r0))
              for t, s in enumerate(shards) for r0 in range(0, s.shape[0], GATHER_PIECE_ROWS)]

    def body(*refs):
        ins, (rb_ref, bucket_ref, win_ref) = refs[:n], refs[n:n + 3]
        outs, bias_ref = refs[n + 3:2 * n + 3], refs[2 * n + 3]
        send_sems, recv_sems = refs[2 * n + 4], refs[2 * n + 5]
        x, y, c = lax.axis_index("x"), lax.axis_index("y"), lax.axis_index("c")
        me, sibling = (x, y, c), (x, y, 1 - c)
        x_nbr, y_nbr, diagonal = (1 - x, y), (x, 1 - y), (1 - x, 1 - y)
        south = c == 0
        relayed = (jnp.where(south, 1 - x, x), jnp.where(south, y, 1 - y))
        relay_to = (jnp.where(south, x, 1 - x), jnp.where(south, 1 - y, y))

        def copy(u, k, block, to):
            t, r0, nrows = pieces[u]
            rows = outs[t].at[4 * block[0] + 2 * block[1] + block[2], pl.ds(r0, nrows)]
            return pltpu.make_async_remote_copy(
                src_ref=rows, dst_ref=rows, send_sem=send_sems.at[u, k], recv_sem=recv_sems.at[u, k],
                device_id=to, device_id_type=MESH)

        for t in range(n):
            outs[t][pl.ds(_my_index(), 1)] = ins[t][...].astype(outs[t].dtype)[None]
        started = []

        def start(cp):
            cp.start()
            started.append(cp)

        units = range(len(pieces))
        for u in units:
            start(copy(u, 0, me, sibling))
            start(copy(u, 1, me, (*x_nbr, c)))
            start(copy(u, 2, me, (*y_nbr, c)))
        _bias_table(rb_ref, bucket_ref, win_ref, bias_ref)
        for u in units:
            for k, chip in ((1, x_nbr), (2, y_nbr)):
                copy(u, k, (*chip, c), me).wait_recv()
                start(copy(u, 3 + k, (*chip, c), sibling))
            start(copy(u, 3, (*relayed, c), (*relay_to, c)))
        for u in units:
            copy(u, 3, (*diagonal, c), me).wait_recv()
            start(copy(u, 6, (*diagonal, c), sibling))
        for u in units:
            copy(u, 0, sibling, me).wait_recv()
        for k, chip in ((4, x_nbr), (5, y_nbr), (6, diagonal)):
            for u in units:
                copy(u, k, (*chip, 1 - c), me).wait_recv()
        for cp in started:
            cp.wait_send()

    vmem = pl.BlockSpec(memory_space=pltpu.VMEM)
    return pl.pallas_call(
        body,
        name="gather_weights",
        out_shape=[jax.ShapeDtypeStruct((N_DEV,) + s.shape, dt) for s, dt in zip(shards, out_dtypes)]
        + [jax.ShapeDtypeStruct((2, N_PAIRS, BAND, 2 * BLOCK), F32)],
        in_specs=[vmem] * n + [pl.BlockSpec(memory_space=pltpu.SMEM), vmem, vmem],
        out_specs=[vmem] * (n + 1),
        scratch_shapes=[pltpu.SemaphoreType.DMA((len(pieces), 7)), pltpu.SemaphoreType.DMA((len(pieces), 7))],
        compiler_params=_params(vmem_mib=48),
    )(*shards, rel_bias_t, bucket_t, in_window_t)


def _peer(k):
    x, y, c = lax.axis_index("x"), lax.axis_index("y"), lax.axis_index("c")
    px = 1 - x if k & 4 else x
    py = 1 - y if k & 2 else y
    pc = 1 - c if k & 1 else c
    return (px, py, pc), 4 * px + 2 * py + pc


def _exchange(srcs, dsts, send_sems, recv_sems, local_sems, scatter):
    me = _my_index()
    sends, arrivals = [], []
    for k in range(1, N_DEV):
        peer, pidx = _peer(k)
        for t, (src, dst) in enumerate(zip(srcs, dsts)):
            mine = src.at[pidx] if scatter else src
            sems = dict(send_sem=send_sems.at[t, k - 1], recv_sem=recv_sems.at[t, k - 1], device_id=peer, device_id_type=MESH)
            sends.append(pltpu.make_async_remote_copy(src_ref=mine, dst_ref=dst.at[me], **sems))
            arrivals.append(pltpu.make_async_remote_copy(src_ref=mine, dst_ref=dst.at[pidx], **sems))
    local = [pltpu.make_async_copy(src.at[me] if scatter else src, dst.at[me], local_sems.at[t])
             for t, (src, dst) in enumerate(zip(srcs, dsts))]
    return sends, arrivals, local


def _exchange_start(*args):
    sends, _, local = _exchange(*args)
    for cp in sends + local:
        cp.start()


def _exchange_wait(*args):
    sends, arrivals, local = _exchange(*args)
    for cp in arrivals:
        cp.wait_recv()
    for cp in sends:
        cp.wait_send()
    for cp in local:
        cp.wait()


def _exchange_sems(n):
    if not n:
        return []
    return [pltpu.SemaphoreType.DMA((n, N_DEV - 1)), pltpu.SemaphoreType.DMA((n, N_DEV - 1)), pltpu.SemaphoreType.DMA((n,))]


HBM_SPEC = pl.BlockSpec(memory_space=pl.ANY)


def _sum_slots(recv_ref, out_ref):
    rows = out_ref.shape[0]
    chunk = min(rows, 128)

    def add(i, carry):
        r0 = pl.multiple_of(i * chunk, chunk)
        acc = recv_ref[0, pl.ds(r0, chunk), :].astype(F32)
        for dev in range(1, N_DEV):
            acc = acc + recv_ref[dev, pl.ds(r0, chunk), :].astype(F32)
        out_ref[pl.ds(r0, chunk), :] = acc
        return carry

    lax.fori_loop(0, rows // chunk, add, 0)


N_CHIPS = N_DEV // 2


def _rows_loop(rows, fn):
    chunk = min(rows, 128)

    def step(i, carry):
        fn(pl.ds(pl.multiple_of(i * chunk, chunk), chunk))
        return carry

    lax.fori_loop(0, rows // chunk, step, 0)


def _chip_reduce(g_ref, out_ref, sib_ref, land_ref, send_ref, sems):
    sib_send, sib_recv, ici_send, ici_recv = sems
    x, y, c = lax.axis_index("x"), lax.axis_index("y"), lax.axis_index("c")
    south = c == 0
    near =(jnp.where(south, 1 - x, x), jnp.where(south, y, 1 - y))
    far = (jnp.where(south, x, 1 - x), jnp.where(south, 1 - y, y))
    diagonal = (1 - x, 1 - y)
    rows = out_ref.shape[0]
    direct, fold, folded = 0, 1, 2

    def to_sibling(t):
        return pltpu.make_async_remote_copy(
            src_ref=g_ref.at[2 * t + 1 - c], dst_ref=sib_ref.at[t], send_sem=sib_send.at[t], recv_sem=sib_recv.at[t],
            device_id=(x, y, 1 - c), device_id_type=MESH)

    def ici(role, chip):
        return pltpu.make_async_remote_copy(
            src_ref=send_ref.at[role], dst_ref=land_ref.at[role], send_sem=ici_send.at[role],
            recv_sem=ici_recv.at[role], device_id=(*chip, c), device_id_type=MESH)

    def pair_sum(chip, r):
        t = 2 * chip[0] + chip[1]
        return g_ref[2 * t + c, r, :].astype(F32) + sib_ref[t, r, :].astype(F32)

    def swap():
        for t in range(N_CHIPS):
            to_sibling(t).start()

    def send():
        for t in range(N_CHIPS):
            to_sibling(t).wait_recv()
        for role, chip in ((fold, diagonal), (direct, near)):
            def fill(r, role=role, chip=chip):
                send_ref[role, r, :] = pair_sum(chip, r).astype(BF16)

            _rows_loop(rows, fill)
            ici(role, near).start()

    def forward():
        ici(fold, near).wait_recv()

        def fill(r):
            send_ref[folded, r, :] = (pair_sum(far, r) + land_ref[fold, r, :].astype(F32)).astype(BF16)

        _rows_loop(rows, fill)
        ici(folded, far).start()

    def finish():
        ici(direct, near).wait_recv()
        ici(folded, far).wait_recv()

        def total(r):
            mine = pair_sum((x, y), r)
            out_ref[r, :] = mine + land_ref[direct, r, :].astype(F32) + land_ref[folded, r, :].astype(F32)

        _rows_loop(rows, total)
        for t in range(N_CHIPS):
            to_sibling(t).wait_send()
        for role, chip in ((direct, near), (fold, near), (folded, far)):
            ici(role, chip).wait_send()

    return swap, send, forward, finish


def _chip_reduce_scratch(slot):
    return [pltpu.VMEM((N_CHIPS,) + slot, BF16), pltpu.VMEM((3,) + slot, BF16), pltpu.VMEM((3,) + slot, BF16),
            pltpu.SemaphoreType.DMA((N_CHIPS,)), pltpu.SemaphoreType.DMA((N_CHIPS,)),
            pltpu.SemaphoreType.DMA((3,)), pltpu.SemaphoreType.DMA((3,))]


def _reduce_exchange(part, landed, smalls):
    nl, ng = len(landed), len(smalls)
    n_out = 1 + nl + ng

    def body(*refs):
        p_in, l_in, s_in = refs[0], refs[1:1 + nl], refs[1 + nl:n_out]
        p_out, l_out, s_out = refs[n_out], refs[n_out + 1:n_out + 1 + nl], refs[n_out + 1 + nl:2 * n_out]
        scratch = refs[2 * n_out:]
        s_recv, (sib_ref, chip_ref, send_ref), sems = scratch[:ng], scratch[ng:ng + 3], scratch[ng + 3:]
        swap, send, forward, finish = _chip_reduce(p_in, p_out, sib_ref, chip_ref, send_ref, sems[:4])
        swap()
        _exchange_start(s_in, s_recv, *sems[4:], False)
        send()
        for t in range(nl):
            _sum_slots(l_in[t], l_out[t])
        forward()
        finish()
        _exchange_wait(s_in, s_recv, *sems[4:], False)
        for t in range(ng):
            acc = s_recv[t][0]
            for dev in range(1, N_DEV):
                acc = acc + s_recv[t][dev]
            s_out[t][...] = acc

    vmem = pl.BlockSpec(memory_space=pltpu.VMEM)
    slot = part.shape[1:]
    outs = pl.pallas_call(
        body,
        name="reduce_grads",
        out_shape=[jax.ShapeDtypeStruct(p.shape[1:], F32) for p in [part] + landed]
        + [jax.ShapeDtypeStruct(s.shape, F32) for s in smalls],
        in_specs=[vmem] * n_out,
        out_specs=[vmem] * n_out,
        scratch_shapes=[pltpu.VMEM((N_DEV,) + s.shape, F32) for s in smalls] + _chip_reduce_scratch(slot)
        + _exchange_sems(ng),
        compiler_params=_params(vmem_mib=56),
    )(part, *landed, *smalls)
    return outs[0], outs[1:1 + nl], outs[1 + nl:]


def _layer_a_fwd(x2, sm, win_g, wout, later, ts):
    seq, d = x2.shape
    width = wout.shape[0]
    half = win_g.shape[2]
    n_half = width // half
    nl = len(later)
    nt = seq // ts

    def body(x_ref, sm_ref, win_ref, wout_ref, *refs):
        shard_refs, refs = refs[:nl], refs[nl:]
        h1_ref, n1_ref, proj_ref, conv_ref, y_ref, ya_ref = refs[:6]
        gathered_refs, (vprev_ref, *sems) = refs[6:6 + nl], refs[6 + nl:]

        @pl.when(pl.program_id(0) == 0)
        def _():
            vprev_ref[...] = jnp.zeros_like(vprev_ref)
            _exchange_start(shard_refs, gathered_refs, *sems, False)

        @pl.when(pl.program_id(0) == nt - 1)
        def _():
            _exchange_wait(shard_refs, gathered_refs, *sems, False)

        xf = x_ref[...]
        xn, _ = _rms(xf)
        n1 = (xn * sm_ref[0:1, :]).astype(BF16)
        n1_ref[...] = n1
        row = lax.broadcasted_iota(jnp.int32, (ts, half), 0)
        ya = jnp.zeros((ts, d), F32)
        for hh in range(n_half):
            cols = slice(hh * half, (hh + 1) * half)
            parts = []
            for part in range(4):
                j = part * n_half + hh
                pj = _dot(n1, win_ref[j])
                proj_ref[:, j * half:(j + 1) * half] = pj.astype(BF16)
                parts.append(pj)
            b, c, u, z = parts
            v = c * u
            last1, last2 = vprev_ref[7:8, cols], vprev_ref[6:7, cols]
            v1 = jnp.where(row == 0, last1, pltpu.roll(v, 1, 0))
            v2 = jnp.where(row == 0, last2, jnp.where(row == 1, last1, pltpu.roll(v, 2, 0)))
            vprev_ref[:, cols] = v[ts - 8:ts, :]
            conv = sm_ref[1:2, cols] * v2 + sm_ref[2:3, cols] * v1 + sm_ref[3:4, cols] * v
            conv_ref[:, cols] = conv.astype(BF16)
            yh = (b * conv * _silu(z)[0]).astype(BF16)
            y_ref[:, cols] = yh
            ya = ya + _dot(yh, wout_ref[cols, :])
        ya_ref[...] = ya
        h1_ref[...] = xf + _rms(ya)[0] * sm_ref[4:5, :]

    outs = pl.pallas_call(
        body,
        name="layer_a_fwd",
        grid=(nt,),
        in_specs=[_rows(ts, d), _full(sm.shape), _full(win_g.shape), _full(wout.shape)] + [HBM_SPEC] * nl,
        out_specs=[_rows(ts, d), _rows(ts, d), _rows(ts, 4 * width), _rows(ts, width), _rows(ts, width), _rows(ts, d)]
        + [HBM_SPEC] * nl,
        out_shape=[
            jax.ShapeDtypeStruct((seq, d), F32),
            jax.ShapeDtypeStruct((seq, d), BF16),
            jax.ShapeDtypeStruct((seq, 4 * width), BF16),
            jax.ShapeDtypeStruct((seq, width), BF16),
            jax.ShapeDtypeStruct((seq, width), BF16),
            jax.ShapeDtypeStruct((seq, d), F32),
        ] + [jax.ShapeDtypeStruct((N_DEV,) + s.shape, s.dtype) for s in later],
        scratch_shapes=[pltpu.VMEM((8, width), F32)] + _exchange_sems(nl),
        compiler_params=_params(("arbitrary",), 56),
    )(x2, sm, win_g, wout, *later)
    return outs[:6], outs[6:]


Q_BLOCKS = 4


def _banded_tiles(kvp_ref, kvc_ref):
    tile = kvc_ref[...].astype(F32)
    blocks = [kvp_ref[...].astype(F32)] + [tile[u * BLOCK:(u + 1) * BLOCK] for u in range(Q_BLOCKS)]
    return [_banded_kv(blocks[u], blocks[u + 1]) for u in range(Q_BLOCKS)]


def _bias_of(bias_ref, i, u, m):
    return bias_ref[jnp.minimum(i, 1) if u == 0 else 1, m]


def _banded_kv(kvp, kvc):
    kw = N_KV_HEADS * HEAD_DIM
    out = []
    for full in (jnp.concatenate([kvp[:, :kw], kvc[:, :kw]], axis=0), jnp.concatenate([kvp[:, kw:], kvc[:, kw:]], axis=0)):
        lo = lax.broadcasted_iota(jnp.int32, full.shape, 1) < HEAD_DIM
        rolled = pltpu.roll(full, HEAD_DIM, 1)
        x2 = [jnp.where(lo, full, rolled).astype(BF16), jnp.where(lo, rolled, full).astype(BF16)]
        ft = full.T
        x2t = [jnp.concatenate([ft[kh * HEAD_DIM:(kh + 1) * HEAD_DIM]] * 2, axis=0).astype(BF16) for kh in range(N_KV_HEADS)]
        out += [x2, x2t]
    return out


def _pair_rows(ref, rows, m, scale=None):
    both = ref[rows, m * LANES:(m + 1) * LANES].astype(F32)
    if scale is not None:
        both = both * scale
    lo = lax.broadcasted_iota(jnp.int32, both.shape, 1) < HEAD_DIM
    zero = jnp.zeros_like(both)
    return jnp.concatenate([jnp.where(lo, both, zero), jnp.where(lo, zero, both)], axis=0).astype(BF16)


def _pair_cols(res_t):
    top = lax.broadcasted_iota(jnp.int32, (LANES, BLOCK), 0) < HEAD_DIM
    return jnp.where(top, res_t[:, :BLOCK], res_t[:, BLOCK:]).T


def _sink_row(sink_ref, m):
    first = lax.broadcasted_iota(jnp.int32, (1, 2 * BLOCK), 1) < BLOCK
    return jnp.where(first, sink_ref[0, 2 * m], sink_ref[0, 2 * m + 1])


def _softmax_t(logits, sink):
    mx =jnp.maximum(jnp.max(logits, axis=0, keepdims=True), sink)
    p = jnp.exp(logits - mx)
    sink_p = jnp.exp(sink - mx)
    inv = 1.0 / (jnp.sum(p, axis=0, keepdims=True) + sink_p)
    return p * inv, sink_p * inv


def _layer_b_fwd(h1, target, kvn, bpre, wkv, wbin_g, biasm, sinks, wbout, bpost):
    seq, d = h1.shape
    kvw = wkv.shape[1]
    cw = wbin_g.shape[2]
    aw = N_Q_HEADS * HEAD_DIM
    per = aw // cw
    tile = Q_BLOCKS * BLOCK

    def body(sink_ref, h1_ref, tgt_ref, kvn_ref, bpre_ref, wkv_ref, wbin_ref, bias_ref, w_ref, g_ref,
             n3_ref, n4_ref, kvc_ref, q_ref, o_ref, dh2_ref, dyb_ref, dattn_ref, dz2_ref, acc_ref,
             attn_ref, z2_ref, kvp_ref):
        i = pl.program_id(0)

        @pl.when(i == 0)
        def _():
            acc_ref[...] = jnp.zeros_like(acc_ref)
            kvp_ref[...] = jnp.zeros_like(kvp_ref)

        hn, _ = _rms(h1_ref[...])
        n3 = (hn * kvn_ref[...]).astype(BF16)
        n4 = (hn * bpre_ref[...]).astype(BF16)
        n3_ref[...] = n3
        n4_ref[...] = n4
        kvc_ref[...] = _dot(n3, wkv_ref[...]).astype(BF16)
        for j in range(N_DEV):
            pj = _dot(n4, wbin_ref[j])
            if j < per:
                q_ref[:, j * cw:(j + 1) * cw] = pj.astype(BF16)
            else:
                z2_ref[:, (j - per) * cw:(j - per + 1) * cw] = pj

        banded = _banded_tiles(kvp_ref, kvc_ref)
        kvp_ref[...] = kvc_ref[tile - BLOCK:tile, :]
        units = [(u, m) for u in range(Q_BLOCKS) for m in range(N_PAIRS)]
        kv_of = lambda m: (2 * m) // GROUP
        logits, probs = {}, {}
        for step in range(len(units) + 2):
            if step < len(units):
                u, m = units[step]
                qpair = _pair_rows(q_ref, slice(u * BLOCK, (u + 1) * BLOCK), m, SCALE)
                logits[step] = _dot_nt(banded[u][0][kv_of(m)], qpair) + _bias_of(bias_ref, i, u, m)
            if 0 <= step - 1 < len(units):
                u, m = units[step - 1]
                probs[step - 1] = _softmax_t(logits.pop(step - 1), _sink_row(sink_ref, m))[0].astype(BF16)
            if 0 <= step - 2 < len(units):
                u, m = units[step - 2]
                out_t = _dot(banded[u][3][kv_of(m)], probs.pop(step - 2))
                attn_ref[u * BLOCK:(u + 1) * BLOCK, m * LANES:(m + 1) * LANES] = _pair_cols(out_t)
        attn = attn_ref[...]
        sz, dsz = _silu(z2_ref[...])
        o = (attn * sz).astype(BF16)
        o_ref[...] = o

        w = w_ref[...]
        yb = _dot(o, w)
        ybn, r = _rms(yb)
        g = g_ref[...]
        diff = h1_ref[...] + ybn * g - tgt_ref[...]
        dh2 = diff * (1.0 / d)
        dh2_ref[...] = dh2
        acc_ref[0:1, :] += jnp.sum(dh2 * ybn, axis=0, keepdims=True)
        tok = jnp.mean(diff * diff, axis=-1, keepdims=True)
        acc_ref[1:2, :] += 0.5 * jnp.sum(tok, axis=0, keepdims=True)
        dyb = _rms_bwd(dh2 * g, ybn, r).astype(BF16)
        dyb_ref[...] = dyb
        do = _dot_nt(dyb, w)
        dattn_ref[...] = (do * sz).astype(BF16)
        dz2_ref[...] = (do * attn * dsz).astype(BF16)

    blk = lambda w: pl.BlockSpec((tile, w), lambda i: (i, 0))
    return pl.pallas_call(
        body,
        name="layer_b_fwd",
        grid=(seq // tile,),
        in_specs=[
            pl.BlockSpec(memory_space=pltpu.SMEM),
            blk(d),
            blk(d),
            _full(kvn.shape),
            _full(bpre.shape),
            _full(wkv.shape),
            _full(wbin_g.shape),
            _full(biasm.shape),
            _full(wbout.shape),
            _full(bpost.shape),
        ],
        out_specs=[blk(d), blk(d), blk(kvw), blk(aw), blk(aw), blk(d), blk(d), blk(aw), blk(aw), _resident((8, d))],
        out_shape=[
            jax.ShapeDtypeStruct((seq, d), BF16),
            jax.ShapeDtypeStruct((seq, d), BF16),
            jax.ShapeDtypeStruct((seq, kvw), BF16),
            jax.ShapeDtypeStruct((seq, aw), BF16),
            jax.ShapeDtypeStruct((seq, aw), BF16),
            jax.ShapeDtypeStruct((seq, d), F32),
            jax.ShapeDtypeStruct((seq, d), BF16),
            jax.ShapeDtypeStruct((seq, aw), BF16),
            jax.ShapeDtypeStruct((seq, aw), BF16),
            jax.ShapeDtypeStruct((8, d), F32),
        ],
        scratch_shapes=[pltpu.VMEM((tile, aw), F32), pltpu.VMEM((tile, aw), F32), pltpu.VMEM((BLOCK, kvw), BF16)],
        compiler_params=_params(("arbitrary",), 56),
    )(sinks, h1, target, kvn, bpre, wkv, wbin_g, biasm, wbout, bpost)


def _attn_bwd(q, kv, dattn, biasm, sinks, ready):
    seq, aw = q.shape
    kvw = kv.shape[1]
    kw = N_KV_HEADS * HEAD_DIM
    nb = seq // BLOCK
    pairs_per_kv = N_PAIRS // N_KV_HEADS
    nr = len(ready)

    tile = Q_BLOCKS * BLOCK
    nsteps = seq // tile
    held = (Q_BLOCKS - 1) * BLOCK

    def body(sink_ref, q_ref, kvc_ref, kvp_ref, da_ref, bias_ref, *refs):
        ready_refs, (dq_ref, dkv_ref, dssum_ref, dsink_ref) = refs[:nr], refs[nr:nr + 4]
        landed_refs, scratch = refs[nr + 4:2 * nr + 4], refs[2 * nr + 4:]
        carry_ref, done_ref, qs_ref, dos_ref, dst_ref, pt_ref, *sems = scratch
        i = pl.program_id(0)

        @pl.when(i == 0)
        def _():
            dssum_ref[...] = jnp.zeros_like(dssum_ref)
            dsink_ref[...] = jnp.zeros_like(dsink_ref)
            carry_ref[...] = jnp.zeros_like(carry_ref)
            done_ref[...] = jnp.zeros_like(done_ref)
            if nr:
                _exchange_start(ready_refs, landed_refs, *sems, True)

        if nr:
            @pl.when(i == nsteps)
            def _():
                _exchange_wait(ready_refs, landed_refs, *sems, True)

        @pl.when(i < nsteps)
        def _():
            lo = lax.broadcasted_iota(jnp.int32, (BAND, LANES), 1) < HEAD_DIM
            head_lane = lax.broadcasted_iota(jnp.int32, (1, LANES), 1)
            banded = _banded_tiles(kvp_ref, kvc_ref)
            units = [(u, m) for u in range(Q_BLOCKS) for m in range(N_PAIRS)]
            dsink = jnp.zeros((1, LANES), F32)
            folded = {}
            logits, dps, dsbs = {}, {}, {}
            for step in range(len(units) + 2):
                if step < len(units):
                    u, m = units[step]
                    kh, rows = m // pairs_per_kv, slice((m % pairs_per_kv) * BAND, (m % pairs_per_kv + 1) * BAND)
                    qrows = slice(u * BLOCK, (u + 1) * BLOCK)
                    qpair = _pair_rows(q_ref, qrows, m, SCALE)
                    dopair = _pair_rows(da_ref, qrows, m)
                    qs_ref[u, kh, rows, :] = qpair
                    dos_ref[u, kh, rows, :] = dopair
                    logits[step] = _dot_nt(banded[u][0][kh], qpair) + _bias_of(bias_ref, i, u, m)
                    dps[step] = _dot_nt(banded[u][2][kh], dopair)
                if 0 <= step - 1 < len(units):
                    u, m = units[step - 1]
                    kh, rows = m // pairs_per_kv, slice((m % pairs_per_kv) * BAND, (m % pairs_per_kv + 1) * BAND)
                    pn, sink_p = _softmax_t(logits.pop(step - 1), _sink_row(sink_ref, m))
                    dp = dps.pop(step - 1)
                    delta = jnp.sum(pn * dp, axis=0, keepdims=True)
                    ds = pn * (dp - delta)
                    dssum_ref[m] += ds
                    sink_term = sink_p * delta
                    for e in range(2):
                        total = jnp.sum(sink_term[:, e * BLOCK:(e + 1) * BLOCK], axis=1, keepdims=True)
                        dsink = dsink - jnp.where(head_lane == 2 * m + e, total, 0.0)
                    dsbs[step - 1] = ds.astype(BF16)
                    dst_ref[u, kh, :, rows] = dsbs[step - 1]
                    pt_ref[u, kh, :, rows] = pn.astype(BF16)
                if 0 <= step - 2 < len(units):
                    u, m = units[step - 2]
                    kh = m // pairs_per_kv
                    dq_t = _dot(banded[u][1][kh], dsbs.pop(step - 2))
                    dq_ref[u * BLOCK:(u + 1) * BLOCK, m * LANES:(m + 1) * LANES] = (_pair_cols(dq_t) * SCALE).astype(BF16)
                    if m % pairs_per_kv == pairs_per_kv - 1:
                        for name, lhs_ref, rhs_ref in (("k", dst_ref, qs_ref), ("v", pt_ref, dos_ref)):
                            acc = _dot(lhs_ref[u, kh], rhs_ref[u, kh])
                            folded[u, kh, name] = acc + pltpu.roll(acc, HEAD_DIM, 1)
            dsink_ref[0:1, :] += dsink
            dkv = [jnp.concatenate([jnp.where(lo, folded[u, 0, n], folded[u, 1, n]) for n in ("k", "v")], axis=1)
                   for u in range(Q_BLOCKS)]

            @pl.when(i > 0)
            def _():
                if held:
                    dkv_ref[:held, :] = done_ref[...].astype(BF16)
                dkv_ref[held:, :] = (carry_ref[...] + dkv[0][:BLOCK]).astype(BF16)

            for u in range(Q_BLOCKS - 1):
                done_ref[u * BLOCK:(u + 1) * BLOCK, :] = dkv[u][BLOCK:] + dkv[u + 1][:BLOCK]
            carry_ref[...] = dkv[Q_BLOCKS - 1][BLOCK:]

        @pl.when(i == nsteps)
        def _():
            if held:
                dkv_ref[:held, :] = done_ref[...].astype(BF16)
            dkv_ref[held:, :] = carry_ref[...].astype(BF16)

    last = nsteps - 1
    blk = lambda w: pl.BlockSpec((tile, w), lambda i: (jnp.minimum(i, last), 0))
    outs = pl.pallas_call(
        body,
        name="attn_bwd",
        grid=(nsteps + 1,),
        in_specs=[
            pl.BlockSpec(memory_space=pltpu.SMEM),
            blk(aw),
            blk(kvw),
            pl.BlockSpec((BLOCK, kvw), lambda i: (jnp.clip(Q_BLOCKS * i - 1, 0, nb - 1), 0)),
            blk(aw),
            _full(biasm.shape),
        ] + [HBM_SPEC] * nr,
        out_specs=[
            blk(aw),
            pl.BlockSpec((tile, kvw), lambda i: (jnp.maximum(i - 1, 0), 0)),
            _resident(biasm.shape[1:]),
            _resident((8, LANES)),
        ] + [HBM_SPEC] * nr,
        out_shape=[
            jax.ShapeDtypeStruct((seq, aw), BF16),
            jax.ShapeDtypeStruct((seq, kvw), BF16),
            jax.ShapeDtypeStruct(biasm.shape[1:], F32),
            jax.ShapeDtypeStruct((8, LANES), F32),
        ] + [jax.ShapeDtypeStruct(g.shape, g.dtype) for g in ready],
        scratch_shapes=[
            pltpu.VMEM((BLOCK, kvw), F32),
            pltpu.VMEM((max(held, 8), kvw), F32),
            pltpu.VMEM((Q_BLOCKS, N_KV_HEADS, pairs_per_kv * BAND, LANES), BF16),
            pltpu.VMEM((Q_BLOCKS, N_KV_HEADS, pairs_per_kv * BAND, LANES), BF16),
            pltpu.VMEM((Q_BLOCKS, N_KV_HEADS, BAND, pairs_per_kv * BAND), BF16),
            pltpu.VMEM((Q_BLOCKS, N_KV_HEADS, BAND, pairs_per_kv * BAND), BF16),
        ] + _exchange_sems(nr),
        compiler_params=_params(("arbitrary",), 48),
    )(sinks, q, kv, kv, dattn, biasm, *ready)
    return outs[:4], outs[4:]


def _relbias_grad(dssum2, bucket_row, chunk):
    heads, n = dssum2.shape

    def body(a_ref, bucket_ref, out_ref):
        @pl.when(pl.program_id(0) == 0)
        def _():
            out_ref[...] = jnp.zeros_like(out_ref)

        a = a_ref[...]
        hi = a.astype(BF16)
        lo = (a - hi.astype(F32)).astype(BF16)
        onehot_t = (lax.broadcasted_iota(jnp.int32, (LANES, chunk), 0) == bucket_ref[...]).astype(F32).astype(BF16)
        out_ref[...] += _dot_nt(hi, onehot_t) + _dot_nt(lo, onehot_t)

    return pl.pallas_call(
        body,
        name="relbias_grad",
        grid=(n // chunk,),
        in_specs=[pl.BlockSpec((heads, chunk), lambda i: (0, i)), pl.BlockSpec((1, chunk), lambda i: (0, i))],
        out_specs=_resident((heads, LANES)),
        out_shape=jax.ShapeDtypeStruct((heads, LANES), F32),
        compiler_params=_params(("arbitrary",), 32),
    )(dssum2, bucket_row)


def _layer_b_in_bwd(dh2, dq, dz2, dkv, h1, ya, wbin_g, wkv, kvn, bpre, sm, ready, ts):
    seq, d = h1.shape
    aw = dq.shape[1]
    kvw = dkv.shape[1]
    cw = wbin_g.shape[2]
    per = aw // cw

    nr = len(ready)
    nt = seq // ts

    def body(dh2_ref, dq_ref, dz2_ref, dkv_ref, h1_ref, ya_ref, wbin_ref, wkv_ref, kvn_ref, bpre_ref, sm_ref, *refs):
        ready_refs, (dh1_ref, dya_ref, acc_ref) = refs[:nr], refs[nr:nr + 3]
        landed_refs, sems = refs[nr + 3:2 * nr + 3], refs[2 * nr + 3:]

        @pl.when(pl.program_id(0) == 0)
        def _():
            acc_ref[...] = jnp.zeros_like(acc_ref)
            _exchange_start(ready_refs, landed_refs, *sems, True)

        @pl.when(pl.program_id(0) == nt - 1)
        def _():
            _exchange_wait(ready_refs, landed_refs, *sems, True)

        dn4 = jnp.zeros((ts, d), F32)
        for j in range(N_DEV):
            src = dq_ref if j < per else dz2_ref
            jj = j % per
            dn4 = dn4 + _dot_nt(src[:, jj * cw:(jj + 1) * cw], wbin_ref[j])
        dn3 = _dot_nt(dkv_ref[...], wkv_ref[...])
        hn, r = _rms(h1_ref[...])
        acc_ref[0:1, :] += jnp.sum(dn4 * hn, axis=0, keepdims=True)
        acc_ref[1:2, :] += jnp.sum(dn3 * hn, axis=0, keepdims=True)
        dh1 = dh2_ref[...] + _rms_bwd(dn4 * bpre_ref[...] + dn3 * kvn_ref[...], hn, r)
        dh1_ref[...] = dh1
        yan, r2 = _rms(ya_ref[...])
        acc_ref[2:3, :] += jnp.sum(dh1 * yan, axis=0, keepdims=True)
        dya_ref[...] = _rms_bwd(dh1 * sm_ref[4:5, :], yan, r2).astype(BF16)

    outs = pl.pallas_call(
        body,
        name="layer_b_in_bwd",
        grid=(nt,),
        in_specs=[_rows(ts, d), _rows(ts, aw), _rows(ts, aw), _rows(ts, kvw), _rows(ts, d), _rows(ts, d),
                  _full(wbin_g.shape), _full(wkv.shape), _full(kvn.shape), _full(bpre.shape), _full(sm.shape)]
        + [HBM_SPEC] * nr,
        out_specs=[_rows(ts, d), _rows(ts, d), _resident((8, d))] + [HBM_SPEC] * nr,
        out_shape=[jax.ShapeDtypeStruct((seq, d), F32), jax.ShapeDtypeStruct((seq, d), BF16),
                   jax.ShapeDtypeStruct((8, d), F32)] + [jax.ShapeDtypeStruct(g.shape, g.dtype) for g in ready],
        scratch_shapes=_exchange_sems(nr),
        compiler_params=_params(("arbitrary",), 48),
    )(dh2, dq, dz2, dkv, h1, ya, wbin_g, wkv, kvn, bpre, sm, *ready)
    return outs[:3], outs[3:]


def _layer_a_bwd(dya, proj, conv, dh1, x2, wout, win_g, sm, ts):
    seq, d = x2.shape
    width = wout.shape[0]
    half = win_g.shape[2]
    n_half = width // half
    nt = seq // ts

    def body(dya_ref, proj_ref, conv_ref, dh1_ref, x_ref, wout_ref, win_ref, sm_ref, dproj_ref, gx_ref, acc_ref,
             dnext_ref):
        @pl.when(pl.program_id(0) == 0)
        def _():
            acc_ref[...] = jnp.zeros_like(acc_ref)
            dnext_ref[...] = jnp.zeros_like(dnext_ref)

        dy = _dot_nt(dya_ref[...], wout_ref[...])
        row = lax.broadcasted_iota(jnp.int32, (ts, half), 0)
        dn1 = jnp.zeros((ts, d), F32)
        for hh in range(n_half):
            cols = slice(hh * half, (hh + 1) * half)
            b, c, u, z = [proj_ref[:, (part * n_half + hh) * half:(part * n_half + hh + 1) * half].astype(F32)
                          for part in range(4)]
            cv = conv_ref[:, cols].astype(F32)
            dyh = dy[:, cols]
            sz, dsz = _silu(z)
            dconv = dyh * b * sz
            grads = [dyh * cv * sz, None, None, dyh * b * cv * dsz]
            next0, next1 = dnext_ref[0:1, cols], dnext_ref[1:2, cols]
            dc1 = jnp.where(row == ts - 1, next0, pltpu.roll(dconv, ts - 1, 0))
            dc2 = jnp.where(row == ts - 1, next1, jnp.where(row == ts - 2, next0, pltpu.roll(dconv, ts - 2, 0)))
            dnext_ref[:, cols] = dconv[0:8, :]
            v = c * u
            acc_ref[1:2, cols] += jnp.sum(dc2 * v, axis=0, keepdims=True)
            acc_ref[2:3, cols] += jnp.sum(dc1 * v, axis=0, keepdims=True)
            acc_ref[3:4, cols] += jnp.sum(dconv * v, axis=0, keepdims=True)
            dv = sm_ref[3:4, cols] * dconv + sm_ref[2:3, cols] * dc1 + sm_ref[1:2, cols] * dc2
            grads[1] = dv * u
            grads[2] = dv * c
            for part in range(4):
                j = part * n_half + hh
                gj = grads[part].astype(BF16)
                dproj_ref[:, j * half:(j + 1) * half] = gj
                dn1 = dn1 + _dot_nt(gj, win_ref[j])
        xn, r = _rms(x_ref[...])
        acc_ref[0:1, :] += jnp.sum(dn1 * xn, axis=0, keepdims=True)
        gx_ref[...] = dh1_ref[...] + _rms_bwd(dn1 * sm_ref[0:1, :], xn, r)

    rev = lambda w: pl.BlockSpec((ts, w), lambda i: (nt - 1 - i, 0))
    return pl.pallas_call(
        body,
        name="layer_a_bwd",
        grid=(nt,),
        in_specs=[rev(d), rev(4 * width), rev(width), rev(d), rev(d), _full(wout.shape), _full(win_g.shape), _full(sm.shape)],
        out_specs=[rev(4 * width), rev(d), _resident((8, d))],
        out_shape=[jax.ShapeDtypeStruct((seq, 4 * width), BF16), jax.ShapeDtypeStruct((seq, d), F32),
                   jax.ShapeDtypeStruct((8, d), F32)],
        scratch_shapes=[pltpu.VMEM((8, width), F32)],
        compiler_params=_params(("arbitrary",), 56),
    )(dya, proj, conv, dh1, x2, wout, win_g, sm)


def _wgrad(a, bs, n_slots, ts, name, ready=(), block_cols=1024):
    nr = len(ready)
    seq, k = a.shape
    nb_in = len(bs)
    n_each = bs[0].shape[1]
    n = nb_in * n_each
    bn = min(n_each, block_cols)
    per_in = n_each // bn
    n_blocks = nb_in * per_in
    ns = seq // ts

    def b_spec(idx):
        def index(j, s):
            mine = j // per_in == idx
            row = jnp.where(mine, s, jnp.where(j // per_in > idx, ns - 1, 0))
            return (row, jnp.where(mine, j % per_in, jnp.where(j // per_in > idx, per_in - 1, 0)))
        return pl.BlockSpec((ts, bn), index)

    if n_slots:
        sw = n // n_slots
        spb = bn // sw
        out_shape = jax.ShapeDtypeStruct((n_slots, k, sw), BF16)
        out_spec = pl.BlockSpec((spb, k, sw), lambda j, s: (j, 0, 0))
    else:
        out_shape = jax.ShapeDtypeStruct((k, n), BF16)
        out_spec = pl.BlockSpec((k, bn), lambda j, s: (0, j))

    def body(a_ref, *refs):
        b_refs, ready_refs, o_ref = refs[:nb_in], refs[nb_in:nb_in + nr], refs[nb_in + nr]
        landed_refs, (acc_ref, *sems) = refs[nb_in + nr + 1:nb_in + 2 * nr + 1], refs[nb_in + 2 * nr + 1:]
        j, s = pl.program_id(0), pl.program_id(1)

        if nr:
            @pl.when(jnp.logical_and(j == 0, s == 0))
            def _():
                _exchange_start(ready_refs, landed_refs, *sems, True)

            @pl.when(jnp.logical_and(j == n_blocks - 1, s == ns - 1))
            def _():
                _exchange_wait(ready_refs, landed_refs, *sems, True)

        @pl.when(s == 0)
        def _():
            acc_ref[...] = jnp.zeros_like(acc_ref)

        for idx in range(nb_in):
            @pl.when(j // per_in == idx)
            def _(idx=idx):
                acc_ref[...] += _dot_tn(a_ref[...], b_refs[idx][...])

        @pl.when(s == ns - 1)
        def _():
            if n_slots:
                for e in range(spb):
                    o_ref[e] = acc_ref[:, e * sw:(e + 1) * sw].astype(BF16)
            else:
                o_ref[...] = acc_ref[...].astype(BF16)

    outs = pl.pallas_call(
        body,
        name=name,
        grid=(n_blocks, ns),
        in_specs=[pl.BlockSpec((ts, k), lambda j, s: (s, 0))] + [b_spec(idx) for idx in range(nb_in)] + [HBM_SPEC] * nr,
        out_specs=[out_spec] + [HBM_SPEC] * nr,
        out_shape=[out_shape] + [jax.ShapeDtypeStruct(g.shape, g.dtype) for g in ready],
        scratch_shapes=[pltpu.VMEM((k, bn), F32)] + (_exchange_sems(nr) if nr else []),
        compiler_params=_params(("arbitrary", "arbitrary"), 48),
    )(a, *bs, *ready)
    return (outs[0], outs[1:]) if nr else outs[0]


def _wgrad_tail(pairs, part, landed, ts):
    n_tasks = len(pairs)
    nl = len(landed)
    seq, k = pairs[0][0].shape
    n = pairs[0][1].shape[1]
    ns = seq // ts
    total = n_tasks * ns
    per = k // N_DEV

    def spec(t, width):
        return pl.BlockSpec((ts, width), lambda j, s: (jnp.where(j == t, s, jnp.where(j > t, ns - 1, 0)), 0))

    def body(*refs):
        ab_refs, part_ref = refs[:2 * n_tasks], refs[2 * n_tasks]
        landed_refs, refs = refs[2 * n_tasks + 1:2 * n_tasks + 1 + nl], refs[2 * n_tasks + 1 + nl:]
        o_ref, red_ref = refs[:2]
        summed_refs, (acc_ref, sib_ref, chip_ref, send_ref, *sems) = refs[2:2 + nl], refs[2 + nl:]
        j, s = pl.program_id(0), pl.program_id(1)
        flat = j * ns + s
        swap, send, forward, finish = _chip_reduce(part_ref, red_ref, sib_ref, chip_ref, send_ref, sems)

        @pl.when(flat == 0)
        def _():
            swap()

        @pl.when(flat == min(1, total - 1))
        def _():
            send()

        @pl.when(flat == min(total // 2 + 1, total - 1))
        def _():
            forward()
            for t in range(nl):
                _sum_slots(landed_refs[t], summed_refs[t])

        @pl.when(s == 0)
        def _():
            acc_ref[...] = jnp.zeros_like(acc_ref)

        for t in range(n_tasks):
            @pl.when(j == t)
            def _(t=t):
                acc_ref[...] += _dot_tn(ab_refs[2 * t][...], ab_refs[2 * t + 1][...])

        @pl.when(s == ns - 1)
        def _():
            for dev in range(N_DEV):
                o_ref[dev] = acc_ref[dev * per:(dev + 1) * per, :].astype(BF16)

        @pl.when(flat == total - 1)
        def _():
            finish()

    slot = part.shape[1:]
    outs = pl.pallas_call(
        body,
        name="wgrad_tail",
        grid=(n_tasks, ns),
        in_specs=[spec(t, w) for t in range(n_tasks) for w in (k, n)] + [_full(part.shape)]
        + [_full(g.shape) for g in landed],
        out_specs=[pl.BlockSpec((N_DEV, per, n), lambda j, s: (0, j, 0)), _resident(slot)]
        + [_resident(g.shape[1:]) for g in landed],
        out_shape=[jax.ShapeDtypeStruct((N_DEV, n_tasks * per, n), BF16), jax.ShapeDtypeStruct(slot, F32)]
        + [jax.ShapeDtypeStruct(g.shape[1:], F32) for g in landed],
        scratch_shapes=[pltpu.VMEM((k, n), F32)] + _chip_reduce_scratch(slot),
        compiler_params=_params(("arbitrary", "arbitrary"), 58),
    )(*[op for pair in pairs for op in pair], part, *landed)
    return outs[0], outs[1], outs[2:]


def _adamw(ws, gs, ms, vs):
    n = len(ws)

    def step(w, g, m, v):
        m = ADAM_B1 * m + (1.0 - ADAM_B1) * g
        v = ADAM_B2 * v + (1.0 - ADAM_B2) * jnp.square(g)
        m_hat = m / (1.0 - ADAM_B1 ** ADAM_STEP)
        v_hat = v / (1.0 - ADAM_B2 ** ADAM_STEP)
        return g, -ADAM_LR * (m_hat / (jnp.sqrt(v_hat) + ADAM_EPS) + ADAM_WD * w), m, v

    def body(*refs):
        w_refs, g_refs, m_refs, v_refs = (refs[k * n:(k + 1) * n] for k in range(4))
        go_refs, d_refs, nm_refs, nv_refs = (refs[(4 + k) * n:(5 + k) * n] for k in range(4))
        for t in range(n):
            rows = w_refs[t].shape[0]
            if rows <= 128:
                go_refs[t][...], d_refs[t][...], nm_refs[t][...], nv_refs[t][...] = step(
                    w_refs[t][...], g_refs[t][...], m_refs[t][...], v_refs[t][...])
                continue
            chunk = 128

            def one(i, carry, t=t):
                r = pl.ds(pl.multiple_of(i * chunk, chunk), chunk)
                go_refs[t][r, :], d_refs[t][r, :], nm_refs[t][r, :], nv_refs[t][r, :] = step(
                    w_refs[t][r, :], g_refs[t][r, :], m_refs[t][r, :], v_refs[t][r, :])
                return carry

            lax.fori_loop(0, rows // chunk, one, 0)

    vmem = pl.BlockSpec(memory_space=pltpu.VMEM)
    outs = pl.pallas_call(
        body,
        name="adamw",
        in_specs=[vmem] * (4 * n),
        out_specs=[vmem] * (4 * n),
        out_shape=[jax.ShapeDtypeStruct(w.shape, F32) for w in ws] * 4,
        compiler_params=_params(vmem_mib=56),
    )(*ws, *gs, *ms, *vs)
    return outs[:n], outs[n:2 * n], outs[2 * n:3 * n], outs[3 * n:]


def _band_structure():
    q_loc = jnp.arange(BLOCK, dtype=jnp.int32)[:, None]
    s_loc = jnp.arange(2 * BLOCK, dtype=jnp.int32)[None, :]
    dist = q_loc + BLOCK - s_loc
    in_window = (dist >= 0) & (dist < BLOCK)
    dd = jnp.maximum(dist, 0)
    max_exact = N_BUCKETS // 2
    large = max_exact + (jnp.log(jnp.maximum(dd, 1).astype(F32) / max_exact) / math.log(MAX_DISTANCE / max_exact)
                         * (N_BUCKETS - max_exact)).astype(jnp.int32)
    bucket = jnp.where(dd < max_exact, dd, jnp.minimum(large, N_BUCKETS - 1))
    return bucket, in_window.astype(jnp.int32)


def _place_rows(a, row, rows=8):
    return jnp.pad(a, ((row, rows - row - a.shape[0]), (0, 0)))


def kernel(x, a_pre_norm, a_w_in, a_conv_w, a_w_out, a_post_norm, kv_norm, w_kv, rel_bias, b_pre_norm, b_w_in, b_sinks, b_w_out, b_post_norm, loss_target, m_a_pre_norm, m_a_w_in, m_a_conv_w, m_a_w_out, m_a_post_norm, m_kv_norm, m_w_kv, m_rel_bias, m_b_pre_norm, m_b_w_in, m_b_sinks, m_b_w_out, m_b_post_norm, v_a_pre_norm, v_a_w_in, v_a_conv_w, v_a_w_out, v_a_post_norm, v_kv_norm, v_w_kv, v_rel_bias, v_b_pre_norm, v_b_w_in, v_b_sinks, v_b_w_out, v_b_post_norm):
    seq, d = x.shape[1], x.shape[2]
    x2 = x.reshape(seq, d)
    target = loss_target.reshape(seq, d)
    shard = a_pre_norm.shape[1]
    me = _my_index()
    ts_a = min(seq, 512)
    ts = min(seq, 512)
    ts_w = min(seq, 2048)

    small = _place_rows(a_pre_norm, 0) + _place_rows(a_conv_w[0], 1) + _place_rows(a_post_norm, 4)
    bucket, in_window = _band_structure()
    win_g, wout_g, small_g, biasm = _all_gather(
        [a_w_in[0], a_w_out[0], small], [BF16, BF16, F32], rel_bias.T, bucket.T, in_window.T)
    wout = wout_g.reshape(-1, wout_g.shape[2])
    sm = small_g.transpose(1, 0, 2).reshape(8, N_DEV * shard)
    kvn = kv_norm.reshape(1, d)

    (h1, n1, proj, conv, y, ya), (wkv_g, wbin_g, wbout_g) = _layer_a_fwd(
        x2, sm, win_g, wout, [w_kv.astype(BF16), b_w_in[0].astype(BF16), b_w_out[0].astype(BF16)], ts_a)
    wkv = wkv_g.reshape(-1, wkv_g.shape[2])
    wbout = wbout_g.reshape(-1, wbout_g.shape[2])
    n3, n4, kv, q, o, dh2, dyb, dattn, dz2, acc_c = _layer_b_fwd(
        h1, target, kvn, b_pre_norm, wkv, wbin_g, biasm, b_sinks, wbout, b_post_norm)

    (dq, dkv, dssum, dsink), _ = _attn_bwd(q, kv, dattn, biasm, b_sinks, [])
    by_head = dssum.reshape(N_PAIRS, BAND, 2, BLOCK).transpose(0, 2, 3, 1)
    relb = _relbias_grad(by_head.reshape(N_Q_HEADS, -1), bucket.reshape(1, -1), 4096)
    g_wkv = _wgrad(n3, [dkv], 0, ts_w, "wgrad_kv").reshape(wkv_g.shape)
    g_wbin = _wgrad(n4, [dq, dz2], N_DEV, ts_w, "wgrad_b_in")
    (dh1, dya, acc_b), (l_wkv, l_wbin) = _layer_b_in_bwd(
        dh2, dq, dz2, dkv, h1, ya, wbin_g, wkv, kvn, b_pre_norm, sm, [g_wkv, g_wbin], ts)
    dproj, gx, acc_a = _layer_a_bwd(dya, proj, conv, dh1, x2, wout, win_g, sm, ts_a)
    g_win = _wgrad(n1, [dproj], N_DEV, ts_w, "wgrad_a_in", block_cols=2048)
    g_outs, r_win, (r_wkv, r_wbin) = _wgrad_tail([(y, dya), (o, dyb)], g_win, [l_wkv, l_wbin], min(seq, 1024))

    r_outs, _, (s_a, s_b, s_c, s_relb, s_sink) = _reduce_exchange(g_outs, [], [acc_a, acc_b, acc_c, relb, dsink])
    rows_out = wout_g.shape[1]
    r_wout, r_wbout = r_outs[:rows_out], r_outs[rows_out:]
    mine = lambda rows: lax.dynamic_slice_in_dim(rows, me * shard, shard, axis=1)
    loss = s_c[1, 0]
    weights = [a_pre_norm, a_w_in[0], a_conv_w[0], a_w_out[0], a_post_norm, kvn, w_kv, rel_bias.T, b_pre_norm,
               b_w_in[0], b_sinks, b_w_out[0], b_post_norm]
    grads = [mine(s_a[0:1]), r_win, mine(s_a[1:4]), r_wout, mine(s_b[2:3]), s_b[1:2], r_wkv,
             s_relb[:, :N_BUCKETS], s_b[0:1], r_wbin, s_sink[0:1, :N_Q_HEADS], r_wbout, s_c[0:1]]
    first = [m_a_pre_norm, m_a_w_in[0], m_a_conv_w[0], m_a_w_out[0], m_a_post_norm, m_kv_norm.reshape(1, d), m_w_kv,
             m_rel_bias.T, m_b_pre_norm, m_b_w_in[0], m_b_sinks, m_b_w_out[0], m_b_post_norm]
    second = [v_a_pre_norm, v_a_w_in[0], v_a_conv_w[0], v_a_w_out[0], v_a_post_norm, v_kv_norm.reshape(1, d), v_w_kv,
              v_rel_bias.T, v_b_pre_norm, v_b_w_in[0], v_b_sinks, v_b_w_out[0], v_b_post_norm]
    grads, deltas, new_m, new_v = _adamw(weights, grads, first, second)

    shapes = [a_pre_norm.shape, a_w_in.shape, a_conv_w.shape, a_w_out.shape, a_post_norm.shape, kv_norm.shape,
              w_kv.shape, None, b_pre_norm.shape, b_w_in.shape, b_sinks.shape, b_w_out.shape, b_post_norm.shape]
    shaped = lambda arrays: [a.T if s is None else a.reshape(s) for a, s in zip(arrays, shapes)]
    return (loss, gx.reshape(x.shape), *shaped(grads), *shaped(deltas), *shaped(new_m), *shaped(new_v))
```

```python
import math

import jax
import jax.numpy as jnp
from jax import lax
from jax.experimental import pallas as pl
from jax.experimental.pallas import tpu as pltpu

HEAD_DIM = 64
N_Q_HEADS = 16
N_KV_HEADS = 2
GROUP = N_Q_HEADS // N_KV_HEADS
BLOCK = 128
N_BUCKETS = 32
MAX_DISTANCE = 128
EPS = 1e-6
NEG_INF = -1e30
SCALE = HEAD_DIM ** -0.5

ADAM_LR = 0.001
ADAM_B1 = 0.9
ADAM_B2 = 0.999
ADAM_EPS = 1e-08
ADAM_WD = 0.01
ADAM_STEP = 10

N_PAIRS = N_Q_HEADS // 2
BAND = 2 * BLOCK

N_DEV = 8
GATHER_PIECE_ROWS = 256
LANES = 128
F32 = jnp.float32
BF16 = jnp.bfloat16
MESH = pl.DeviceIdType.MESH
MIB = 1024 * 1024


def _params(semantics=None, vmem_mib=48):
    return pltpu.CompilerParams(dimension_semantics=semantics, vmem_limit_bytes=vmem_mib * MIB)


def _full(shape):
    zeros = (0,) * len(shape)
    return pl.BlockSpec(shape, lambda *_: zeros, pipeline_mode=pl.Buffered(1))


def _resident(shape):
    zeros = (0,) * len(shape)
    return pl.BlockSpec(shape, lambda *_: zeros)


def _rows(ts, cols):
    return pl.BlockSpec((ts, cols), lambda i: (i, 0))


def _dot(a, b):
    return jnp.dot(a, b, preferred_element_type=F32)


def _dot_nt(a, b):
    return lax.dot_general(a, b, (((1,), (1,)), ((), ())), preferred_element_type=F32)


def _dot_tn(a, b):
    return lax.dot_general(a, b, (((0,), (0,)), ((), ())), preferred_element_type=F32)


def _rms(xf):
    r = lax.rsqrt(jnp.mean(xf * xf, axis=-1, keepdims=True) + EPS)
    return xf * r, r


def _rms_bwd(dn, xn, r):
    return r * (dn - xn * jnp.mean(dn * xn, axis=-1, keepdims=True))


def _silu(z):
    s = jax.nn.sigmoid(z)
    return z * s, s * (1.0 + z * (1.0 - s))


def _my_index():
    return 4 * lax.axis_index("x") + 2 * lax.axis_index("y") + lax.axis_index("c")


def _bias_table(rb_ref, bucket_ref, win_ref, out_ref):
    bk = jnp.where(win_ref[...] != 0, bucket_ref[...], -1)
    has_prev = lax.broadcasted_iota(jnp.int32, bk.shape, 0) >= BLOCK
    for h in range(N_Q_HEADS):
        acc = jnp.full(bk.shape, NEG_INF, F32)
        for b in range(N_BUCKETS):
            acc = jnp.where(bk == b, rb_ref[h, b], acc)
        cols = slice((h % 2) * BLOCK, (h % 2 + 1) * BLOCK)
        out_ref[1, h // 2, :, cols] = acc
        out_ref[0, h // 2, :, cols] = jnp.where(has_prev, acc, NEG_INF)


def _all_gather(shards, out_dtypes, rel_bias_t, bucket_t, in_window_t):
    n = len(shards)
    pieces = [(t, r0, min(GATHER_PIECE_ROWS, s.shape[0] - r0))
              for t, s in enumerate(shards) for r0 in range(0, s.shape[0], GATHER_PIECE_ROWS)]

    def body(*refs):
        ins, (rb_ref, bucket_ref, win_ref) = refs[:n], refs[n:n + 3]
        outs, bias_ref = refs[n + 3:2 * n + 3], refs[2 * n + 3]
        send_sems, recv_sems = refs[2 * n + 4], refs[2 * n + 5]
        x, y, c = lax.axis_index("x"), lax.axis_index("y"), lax.axis_index("c")
        me, sibling = (x, y, c), (x, y, 1 - c)
        x_nbr, y_nbr, diagonal = (1 - x, y), (x, 1 - y), (1 - x, 1 - y)
        south = c == 0
        relayed = (jnp.where(south, 1 - x, x), jnp.where(south, y, 1 - y))
        relay_to = (jnp.where(south, x, 1 - x), jnp.where(south, 1 - y, y))

        def copy(u, k, block, to):
            t, r0, nrows = pieces[u]
            rows = outs[t].at[4 * block[0] + 2 * block[1] + block[2], pl.ds(r0, nrows)]
            return pltpu.make_async_remote_copy(
                src_ref=rows, dst_ref=rows, send_sem=send_sems.at[u, k], recv_sem=recv_sems.at[u, k],
                device_id=to, device_id_type=MESH)

        for t in range(n):
            outs[t][pl.ds(_my_index(), 1)] = ins[t][...].astype(outs[t].dtype)[None]
        started = []

        def start(cp):
            cp.start()
            started.append(cp)

        units = range(len(pieces))
        for u in units:
            start(copy(u, 0, me, sibling))
            start(copy(u, 1, me, (*x_nbr, c)))
            start(copy(u, 2, me, (*y_nbr, c)))
        _bias_table(rb_ref, bucket_ref, win_ref, bias_ref)
        for u in units:
            for k, chip in ((1, x_nbr), (2, y_nbr)):
                copy(u, k, (*chip, c), me).wait_recv()
                start(copy(u, 3 + k, (*chip, c), sibling))
            start(copy(u, 3, (*relayed, c), (*relay_to, c)))
        for u in units:
            copy(u, 3, (*diagonal, c), me).wait_recv()
            start(copy(u, 6, (*diagonal, c), sibling))
        for u in units:
            copy(u, 0, sibling, me).wait_recv()
        for k, chip in ((4, x_nbr), (5, y_nbr), (6, diagonal)):
            for u in units:
                copy(u, k, (*chip, 1 - c), me).wait_recv()
        for cp in started:
            cp.wait_send()

    vmem = pl.BlockSpec(memory_space=pltpu.VMEM)
    return pl.pallas_call(
        body,
        name="gather_weights",
        out_shape=[jax.ShapeDtypeStruct((N_DEV,) + s.shape, dt) for s, dt in zip(shards, out_dtypes)]
        + [jax.ShapeDtypeStruct((2, N_PAIRS, BAND, 2 * BLOCK), F32)],
        in_specs=[vmem] * n + [pl.BlockSpec(memory_space=pltpu.SMEM), vmem, vmem],
        out_specs=[vmem] * (n + 1),
        scratch_shapes=[pltpu.SemaphoreType.DMA((len(pieces), 7)), pltpu.SemaphoreType.DMA((len(pieces), 7))],
        compiler_params=_params(vmem_mib=48),
    )(*shards, rel_bias_t, bucket_t, in_window_t)


def _peer(k):
    x, y, c = lax.axis_index("x"), lax.axis_index("y"), lax.axis_index("c")
    px = 1 - x if k & 4 else x
    py = 1 - y if k & 2 else y
    pc = 1 - c if k & 1 else c
    return (px, py, pc), 4 * px + 2 * py + pc


def _exchange(srcs, dsts, send_sems, recv_sems, local_sems, scatter):
    me = _my_index()
    sends, arrivals = [], []
    for k in range(1, N_DEV):
        peer, pidx = _peer(k)
        for t, (src, dst) in enumerate(zip(srcs, dsts)):
            mine = src.at[pidx] if scatter else src
            sems = dict(send_sem=send_sems.at[t, k - 1], recv_sem=recv_sems.at[t, k - 1], device_id=peer, device_id_type=MESH)
            sends.append(pltpu.make_async_remote_copy(src_ref=mine, dst_ref=dst.at[me], **sems))
            arrivals.append(pltpu.make_async_remote_copy(src_ref=mine, dst_ref=dst.at[pidx], **sems))
    local = [pltpu.make_async_copy(src.at[me] if scatter else src, dst.at[me], local_sems.at[t])
             for t, (src, dst) in enumerate(zip(srcs, dsts))]
    return sends, arrivals, local


def _exchange_start(*args):
    sends, _, local = _exchange(*args)
    for cp in sends + local:
        cp.start()


def _exchange_wait(*args):
    sends, arrivals, local = _exchange(*args)
    for cp in arrivals:
        cp.wait_recv()
    for cp in sends:
        cp.wait_send()
    for cp in local:
        cp.wait()


def _exchange_sems(n):
    if not n:
        return []
    return [pltpu.SemaphoreType.DMA((n, N_DEV - 1)), pltpu.SemaphoreType.DMA((n, N_DEV - 1)), pltpu.SemaphoreType.DMA((n,))]


HBM_SPEC = pl.BlockSpec(memory_space=pl.ANY)


def _sum_slots(recv_ref, out_ref):
    rows = out_ref.shape[0]
    chunk = min(rows, 128)

    def add(i, carry):
        r0 = pl.multiple_of(i * chunk, chunk)
        acc = recv_ref[0, pl.ds(r0, chunk), :].astype(F32)
        for dev in range(1, N_DEV):
            acc = acc + recv_ref[dev, pl.ds(r0, chunk), :].astype(F32)
        out_ref[pl.ds(r0, chunk), :] = acc
        return carry

    lax.fori_loop(0, rows // chunk, add, 0)


N_CHIPS = N_DEV // 2


def _rows_loop(rows, fn):
    chunk = min(rows, 128)

    def step(i, carry):
        fn(pl.ds(pl.multiple_of(i * chunk, chunk), chunk))
        return carry

    lax.fori_loop(0, rows // chunk, step, 0)


def _chip_reduce(g_ref, out_ref, sib_ref, land_ref, send_ref, sems):
    sib_send, sib_recv, ici_send, ici_recv = sems
    x, y, c = lax.axis_index("x"), lax.axis_index("y"), lax.axis_index("c")
    south = c == 0
    near =(jnp.where(south, 1 - x, x), jnp.where(south, y, 1 - y))
    far = (jnp.where(south, x, 1 - x), jnp.where(south, 1 - y, y))
    diagonal = (1 - x, 1 - y)
    rows = out_ref.shape[0]
    direct, fold, folded = 0, 1, 2

    def to_sibling(t):
        return pltpu.make_async_remote_copy(
            src_ref=g_ref.at[2 * t + 1 - c], dst_ref=sib_ref.at[t], send_sem=sib_send.at[t], recv_sem=sib_recv.at[t],
            device_id=(x, y, 1 - c), device_id_type=MESH)

    def ici(role, chip):
        return pltpu.make_async_remote_copy(
            src_ref=send_ref.at[role], dst_ref=land_ref.at[role], send_sem=ici_send.at[role],
            recv_sem=ici_recv.at[role], device_id=(*chip, c), device_id_type=MESH)

    def pair_sum(chip, r):
        t = 2 * chip[0] + chip[1]
        return g_ref[2 * t + c, r, :].astype(F32) + sib_ref[t, r, :].astype(F32)

    def swap():
        for t in range(N_CHIPS):
            to_sibling(t).start()

    def send():
        for t in range(N_CHIPS):
            to_sibling(t).wait_recv()
        for role, chip in ((fold, diagonal), (direct, near)):
            def fill(r, role=role, chip=chip):
                send_ref[role, r, :] = pair_sum(chip, r).astype(BF16)

            _rows_loop(rows, fill)
            ici(role, near).start()

    def forward():
        ici(fold, near).wait_recv()

        def fill(r):
            send_ref[folded, r, :] = (pair_sum(far, r) + land_ref[fold, r, :].astype(F32)).astype(BF16)

        _rows_loop(rows, fill)
        ici(folded, far).start()

    def finish():
        ici(direct, near).wait_recv()
        ici(folded, far).wait_recv()

        def total(r):
            mine = pair_sum((x, y), r)
            out_ref[r, :] = mine + land_ref[direct, r, :].astype(F32) + land_ref[folded, r, :].astype(F32)

        _rows_loop(rows, total)
        for t in range(N_CHIPS):
            to_sibling(t).wait_send()
        for role, chip in ((direct, near), (fold, near), (folded, far)):
            ici(role, chip).wait_send()

    return swap, send, forward, finish


def _chip_reduce_scratch(slot):
    return [pltpu.VMEM((N_CHIPS,) + slot, BF16), pltpu.VMEM((3,) + slot, BF16), pltpu.VMEM((3,) + slot, BF16),
            pltpu.SemaphoreType.DMA((N_CHIPS,)), pltpu.SemaphoreType.DMA((N_CHIPS,)),
            pltpu.SemaphoreType.DMA((3,)), pltpu.SemaphoreType.DMA((3,))]


def _reduce_exchange(part, landed, smalls):
    nl, ng = len(landed), len(smalls)
    n_out = 1 + nl + ng

    def body(*refs):
        p_in, l_in, s_in = refs[0], refs[1:1 + nl], refs[1 + nl:n_out]
        p_out, l_out, s_out = refs[n_out], refs[n_out + 1:n_out + 1 + nl], refs[n_out + 1 + nl:2 * n_out]
        scratch = refs[2 * n_out:]
        s_recv, (sib_ref, chip_ref, send_ref), sems = scratch[:ng], scratch[ng:ng + 3], scratch[ng + 3:]
        swap, send, forward, finish = _chip_reduce(p_in, p_out, sib_ref, chip_ref, send_ref, sems[:4])
        swap()
        _exchange_start(s_in, s_recv, *sems[4:], False)
        send()
        for t in range(nl):
            _sum_slots(l_in[t], l_out[t])
        forward()
        finish()
        _exchange_wait(s_in, s_recv, *sems[4:], False)
        for t in range(ng):
            acc = s_recv[t][0]
            for dev in range(1, N_DEV):
                acc = acc + s_recv[t][dev]
            s_out[t][...] = acc

    vmem = pl.BlockSpec(memory_space=pltpu.VMEM)
    slot = part.shape[1:]
    outs = pl.pallas_call(
        body,
        name="reduce_grads",
        out_shape=[jax.ShapeDtypeStruct(p.shape[1:], F32) for p in [part] + landed]
        + [jax.ShapeDtypeStruct(s.shape, F32) for s in smalls],
        in_specs=[vmem] * n_out,
        out_specs=[vmem] * n_out,
        scratch_shapes=[pltpu.VMEM((N_DEV,) + s.shape, F32) for s in smalls] + _chip_reduce_scratch(slot)
        + _exchange_sems(ng),
        compiler_params=_params(vmem_mib=56),
    )(part, *landed, *smalls)
    return outs[0], outs[1:1 + nl], outs[1 + nl:]


def _layer_a_fwd(x2, sm, win_g, wout, later, ts):
    seq, d = x2.shape
    width = wout.shape[0]
    half = win_g.shape[2]
    n_half = width // half
    nl = len(later)
    nt = seq // ts

    def body(x_ref, sm_ref, win_ref, wout_ref, *refs):
        shard_refs, refs = refs[:nl], refs[nl:]
        h1_ref, n1_ref, proj_ref, conv_ref, y_ref, ya_ref = refs[:6]
        gathered_refs, (vprev_ref, *sems) = refs[6:6 + nl], refs[6 + nl:]

        @pl.when(pl.program_id(0) == 0)
        def _():
            vprev_ref[...] = jnp.zeros_like(vprev_ref)
            _exchange_start(shard_refs, gathered_refs, *sems, False)

        @pl.when(pl.program_id(0) == nt - 1)
        def _():
            _exchange_wait(shard_refs, gathered_refs, *sems, False)

        xf = x_ref[...]
        xn, _ = _rms(xf)
        n1 = (xn * sm_ref[0:1, :]).astype(BF16)
        n1_ref[...] = n1
        row = lax.broadcasted_iota(jnp.int32, (ts, half), 0)
        ya = jnp.zeros((ts, d), F32)
        for hh in range(n_half):
            cols = slice(hh * half, (hh + 1) * half)
            parts = []
            for part in range(4):
                j = part * n_half + hh
                pj = _dot(n1, win_ref[j])
                proj_ref[:, j * half:(j + 1) * half] = pj.astype(BF16)
                parts.append(pj)
            b, c, u, z = parts
            v = c * u
            last1, last2 = vprev_ref[7:8, cols], vprev_ref[6:7, cols]
            v1 = jnp.where(row == 0, last1, pltpu.roll(v, 1, 0))
            v2 = jnp.where(row == 0, last2, jnp.where(row == 1, last1, pltpu.roll(v, 2, 0)))
            vprev_ref[:, cols] = v[ts - 8:ts, :]
            conv = sm_ref[1:2, cols] * v2 + sm_ref[2:3, cols] * v1 + sm_ref[3:4, cols] * v
            conv_ref[:, cols] = conv.astype(BF16)
            yh = (b * conv * _silu(z)[0]).astype(BF16)
            y_ref[:, cols] = yh
            ya = ya + _dot(yh, wout_ref[cols, :])
        ya_ref[...] = ya.astype(BF16)
        h1_ref[...] = xf + _rms(ya)[0] * sm_ref[4:5, :]

    outs = pl.pallas_call(
        body,
        name="layer_a_fwd",
        grid=(nt,),
        in_specs=[_rows(ts, d), _full(sm.shape), _full(win_g.shape), _full(wout.shape)] + [HBM_SPEC] * nl,
        out_specs=[_rows(ts, d), _rows(ts, d), _rows(ts, 4 * width), _rows(ts, width), _rows(ts, width), _rows(ts, d)]
        + [HBM_SPEC] * nl,
        out_shape=[
            jax.ShapeDtypeStruct((seq, d), F32),
            jax.ShapeDtypeStruct((seq, d), BF16),
            jax.ShapeDtypeStruct((seq, 4 * width), BF16),
            jax.ShapeDtypeStruct((seq, width), BF16),
            jax.ShapeDtypeStruct((seq, width), BF16),
            jax.ShapeDtypeStruct((seq, d), BF16),
        ] + [jax.ShapeDtypeStruct((N_DEV,) + s.shape, s.dtype) for s in later],
        scratch_shapes=[pltpu.VMEM((8, width), F32)] + _exchange_sems(nl),
        compiler_params=_params(("arbitrary",), 56),
    )(x2, sm, win_g, wout, *later)
    return outs[:6], outs[6:]


Q_BLOCKS = 4


def _banded_tiles(kvp_ref, kvc_ref):
    tile = kvc_ref[...].astype(F32)
    blocks = [kvp_ref[...].astype(F32)] + [tile[u * BLOCK:(u + 1) * BLOCK] for u in range(Q_BLOCKS)]
    return [_banded_kv(blocks[u], blocks[u + 1]) for u in range(Q_BLOCKS)]


def _bias_of(bias_ref, i, u, m):
    return bias_ref[jnp.minimum(i, 1) if u == 0 else 1, m]


def _banded_kv(kvp, kvc):
    kw = N_KV_HEADS * HEAD_DIM
    out = []
    for full in (jnp.concatenate([kvp[:, :kw], kvc[:, :kw]], axis=0), jnp.concatenate([kvp[:, kw:], kvc[:, kw:]], axis=0)):
        lo = lax.broadcasted_iota(jnp.int32, full.shape, 1) < HEAD_DIM
        rolled = pltpu.roll(full, HEAD_DIM, 1)
        x2 = [jnp.where(lo, full, rolled).astype(BF16), jnp.where(lo, rolled, full).astype(BF16)]
        ft = full.T
        x2t = [jnp.concatenate([ft[kh * HEAD_DIM:(kh + 1) * HEAD_DIM]] * 2, axis=0).astype(BF16) for kh in range(N_KV_HEADS)]
        out += [x2, x2t]
    return out


def _pair_rows(ref, rows, m, scale=None):
    both = ref[rows, m * LANES:(m + 1) * LANES].astype(F32)
    if scale is not None:
        both = both * scale
    lo = lax.broadcasted_iota(jnp.int32, both.shape, 1) < HEAD_DIM
    zero = jnp.zeros_like(both)
    return jnp.concatenate([jnp.where(lo, both, zero), jnp.where(lo, zero, both)], axis=0).astype(BF16)


def _pair_cols(res_t):
    top = lax.broadcasted_iota(jnp.int32, (LANES, BLOCK), 0) < HEAD_DIM
    return jnp.where(top, res_t[:, :BLOCK], res_t[:, BLOCK:]).T


def _sink_row(sink_ref, m):
    first = lax.broadcasted_iota(jnp.int32, (1, 2 * BLOCK), 1) < BLOCK
    return jnp.where(first, sink_ref[0, 2 * m], sink_ref[0, 2 * m + 1])


def _softmax_t(logits, sink):
    mx =jnp.maximum(jnp.max(logits, axis=0, keepdims=True), sink)
    p = jnp.exp(logits - mx)
    sink_p = jnp.exp(sink - mx)
    inv = 1.0 / (jnp.sum(p, axis=0, keepdims=True) + sink_p)
    return p * inv, sink_p * inv


def _layer_b_fwd(h1, target, kvn, bpre, wkv, wbin_g, biasm, sinks, wbout, bpost):
    seq, d = h1.shape
    kvw = wkv.shape[1]
    cw = wbin_g.shape[2]
    aw = N_Q_HEADS * HEAD_DIM
    per = aw // cw
    tile = Q_BLOCKS * BLOCK

    def body(sink_ref, h1_ref, tgt_ref, kvn_ref, bpre_ref, wkv_ref, wbin_ref, bias_ref, w_ref, g_ref,
             n3_ref, n4_ref, kvc_ref, q_ref, o_ref, dh2_ref, dyb_ref, dattn_ref, dz2_ref, acc_ref,
             attn_ref, z2_ref, kvp_ref):
        i = pl.program_id(0)

        @pl.when(i == 0)
        def _():
            acc_ref[...] = jnp.zeros_like(acc_ref)
            kvp_ref[...] = jnp.zeros_like(kvp_ref)

        hn, _ = _rms(h1_ref[...])
        n3 = (hn * kvn_ref[...]).astype(BF16)
        n4 = (hn * bpre_ref[...]).astype(BF16)
        n3_ref[...] = n3
        n4_ref[...] = n4
        kvc_ref[...] = _dot(n3, wkv_ref[...]).astype(BF16)
        for j in range(N_DEV):
            pj = _dot(n4, wbin_ref[j])
            if j < per:
                q_ref[:, j * cw:(j + 1) * cw] = pj.astype(BF16)
            else:
                z2_ref[:, (j - per) * cw:(j - per + 1) * cw] = pj

        banded = _banded_tiles(kvp_ref, kvc_ref)
        kvp_ref[...] = kvc_ref[tile - BLOCK:tile, :]
        units = [(u, m) for u in range(Q_BLOCKS) for m in range(N_PAIRS)]
        kv_of = lambda m: (2 * m) // GROUP
        logits, probs = {}, {}
        for step in range(len(units) + 2):
            if step < len(units):
                u, m = units[step]
                qpair = _pair_rows(q_ref, slice(u * BLOCK, (u + 1) * BLOCK), m, SCALE)
                logits[step] = _dot_nt(banded[u][0][kv_of(m)], qpair) + _bias_of(bias_ref, i, u, m)
            if 0 <= step - 1 < len(units):
                u, m = units[step - 1]
                probs[step - 1] = _softmax_t(logits.pop(step - 1), _sink_row(sink_ref, m))[0].astype(BF16)
            if 0 <= step - 2 < len(units):
                u, m = units[step - 2]
                out_t = _dot(banded[u][3][kv_of(m)], probs.pop(step - 2))
                attn_ref[u * BLOCK:(u + 1) * BLOCK, m * LANES:(m + 1) * LANES] = _pair_cols(out_t)
        attn = attn_ref[...]
        sz, dsz = _silu(z2_ref[...])
        o = (attn * sz).astype(BF16)
        o_ref[...] = o

        w = w_ref[...]
        yb = _dot(o, w)
        ybn, r = _rms(yb)
        g = g_ref[...]
        diff = h1_ref[...] + ybn * g - tgt_ref[...]
        dh2 = diff * (1.0 / d)
        dh2_ref[...] = dh2.astype(BF16)
        acc_ref[0:1, :] += jnp.sum(dh2 * ybn, axis=0, keepdims=True)
        tok = jnp.mean(diff * diff, axis=-1, keepdims=True)
        acc_ref[1:2, :] += 0.5 * jnp.sum(tok, axis=0, keepdims=True)
        dyb = _rms_bwd(dh2 * g, ybn, r).astype(BF16)
        dyb_ref[...] = dyb
        do = _dot_nt(dyb, w)
        dattn_ref[...] = (do * sz).astype(BF16)
        dz2_ref[...] = (do * attn * dsz).astype(BF16)

    blk = lambda w: pl.BlockSpec((tile, w), lambda i: (i, 0))
    return pl.pallas_call(
        body,
        name="layer_b_fwd",
        grid=(seq // tile,),
        in_specs=[
            pl.BlockSpec(memory_space=pltpu.SMEM),
            blk(d),
            blk(d),
            _full(kvn.shape),
            _full(bpre.shape),
            _full(wkv.shape),
            _full(wbin_g.shape),
            _full(biasm.shape),
            _full(wbout.shape),
            _full(bpost.shape),
        ],
        out_specs=[blk(d), blk(d), blk(kvw), blk(aw), blk(aw), blk(d), blk(d), blk(aw), blk(aw), _resident((8, d))],
        out_shape=[
            jax.ShapeDtypeStruct((seq, d), BF16),
            jax.ShapeDtypeStruct((seq, d), BF16),
            jax.ShapeDtypeStruct((seq, kvw), BF16),
            jax.ShapeDtypeStruct((seq, aw), BF16),
            jax.ShapeDtypeStruct((seq, aw), BF16),
            jax.ShapeDtypeStruct((seq, d), BF16),
            jax.ShapeDtypeStruct((seq, d), BF16),
            jax.ShapeDtypeStruct((seq, aw), BF16),
            jax.ShapeDtypeStruct((seq, aw), BF16),
            jax.ShapeDtypeStruct((8, d), F32),
        ],
        scratch_shapes=[pltpu.VMEM((tile, aw), F32), pltpu.VMEM((tile, aw), F32), pltpu.VMEM((BLOCK, kvw), BF16)],
        compiler_params=_params(("arbitrary",), 56),
    )(sinks, h1, target, kvn, bpre, wkv, wbin_g, biasm, wbout, bpost)


def _attn_bwd(q, kv, dattn, biasm, sinks, ready):
    seq, aw = q.shape
    kvw = kv.shape[1]
    kw = N_KV_HEADS * HEAD_DIM
    nb = seq // BLOCK
    pairs_per_kv = N_PAIRS // N_KV_HEADS
    nr = len(ready)

    tile = Q_BLOCKS * BLOCK
    nsteps = seq // tile
    held = (Q_BLOCKS - 1) * BLOCK

    def body(sink_ref, q_ref, kvc_ref, kvp_ref, da_ref, bias_ref, *refs):
        ready_refs, (dq_ref, dkv_ref, dssum_ref, dsink_ref) = refs[:nr], refs[nr:nr + 4]
        landed_refs, scratch = refs[nr + 4:2 * nr + 4], refs[2 * nr + 4:]
        carry_ref, done_ref, qs_ref, dos_ref, dst_ref, pt_ref, *sems = scratch
        i = pl.program_id(0)

        @pl.when(i == 0)
        def _():
            dssum_ref[...] = jnp.zeros_like(dssum_ref)
            dsink_ref[...] = jnp.zeros_like(dsink_ref)
            carry_ref[...] = jnp.zeros_like(carry_ref)
            done_ref[...] = jnp.zeros_like(done_ref)
            if nr:
                _exchange_start(ready_refs, landed_refs, *sems, True)

        if nr:
            @pl.when(i == nsteps)
            def _():
                _exchange_wait(ready_refs, landed_refs, *sems, True)

        @pl.when(i < nsteps)
        def _():
            lo = lax.broadcasted_iota(jnp.int32, (BAND, LANES), 1) < HEAD_DIM
            head_lane = lax.broadcasted_iota(jnp.int32, (1, LANES), 1)
            banded = _banded_tiles(kvp_ref, kvc_ref)
            units = [(u, m) for u in range(Q_BLOCKS) for m in range(N_PAIRS)]
            dsink = jnp.zeros((1, LANES), F32)
            folded = {}
            logits, dps, dsbs = {}, {}, {}
            for step in range(len(units) + 2):
                if step < len(units):
                    u, m = units[step]
                    kh, rows = m // pairs_per_kv, slice((m % pairs_per_kv) * BAND, (m % pairs_per_kv + 1) * BAND)
                    qrows = slice(u * BLOCK, (u + 1) * BLOCK)
                    qpair = _pair_rows(q_ref, qrows, m, SCALE)
                    dopair = _pair_rows(da_ref, qrows, m)
                    qs_ref[u, kh, rows, :] = qpair
                    dos_ref[u, kh, rows, :] = dopair
                    logits[step] = _dot_nt(banded[u][0][kh], qpair) + _bias_of(bias_ref, i, u, m)
                    dps[step] = _dot_nt(banded[u][2][kh], dopair)
                if 0 <= step - 1 < len(units):
                    u, m = units[step - 1]
                    kh, rows = m // pairs_per_kv, slice((m % pairs_per_kv) * BAND, (m % pairs_per_kv + 1) * BAND)
                    pn, sink_p = _softmax_t(logits.pop(step - 1), _sink_row(sink_ref, m))
                    dp = dps.pop(step - 1)
                    delta = jnp.sum(pn * dp, axis=0, keepdims=True)
                    ds = pn * (dp - delta)
                    dssum_ref[m] += ds
                    sink_term = sink_p * delta
                    for e in range(2):
                        total = jnp.sum(sink_term[:, e * BLOCK:(e + 1) * BLOCK], axis=1, keepdims=True)
                        dsink = dsink - jnp.where(head_lane == 2 * m + e, total, 0.0)
                    dsbs[step - 1] = ds.astype(BF16)
                    dst_ref[u, kh, :, rows] = dsbs[step - 1]
                    pt_ref[u, kh, :, rows] = pn.astype(BF16)
                if 0 <= step - 2 < len(units):
                    u, m = units[step - 2]
                    kh = m // pairs_per_kv
                    dq_t = _dot(banded[u][1][kh], dsbs.pop(step - 2))
                    dq_ref[u * BLOCK:(u + 1) * BLOCK, m * LANES:(m + 1) * LANES] = (_pair_cols(dq_t) * SCALE).astype(BF16)
                    if m % pairs_per_kv == pairs_per_kv - 1:
                        for name, lhs_ref, rhs_ref in (("k", dst_ref, qs_ref), ("v", pt_ref, dos_ref)):
                            acc = _dot(lhs_ref[u, kh], rhs_ref[u, kh])
                            folded[u, kh, name] = acc + pltpu.roll(acc, HEAD_DIM, 1)
            dsink_ref[0:1, :] += dsink
            dkv = [jnp.concatenate([jnp.where(lo, folded[u, 0, n], folded[u, 1, n]) for n in ("k", "v")], axis=1)
                   for u in range(Q_BLOCKS)]

            @pl.when(i > 0)
            def _():
                if held:
                    dkv_ref[:held, :] = done_ref[...].astype(BF16)
                dkv_ref[held:, :] = (carry_ref[...] + dkv[0][:BLOCK]).astype(BF16)

            for u in range(Q_BLOCKS - 1):
                done_ref[u * BLOCK:(u + 1) * BLOCK, :] = dkv[u][BLOCK:] + dkv[u + 1][:BLOCK]
            carry_ref[...] = dkv[Q_BLOCKS - 1][BLOCK:]

        @pl.when(i == nsteps)
        def _():
            if held:
                dkv_ref[:held, :] = done_ref[...].astype(BF16)
            dkv_ref[held:, :] = carry_ref[...].astype(BF16)

    last = nsteps - 1
    blk = lambda w: pl.BlockSpec((tile, w), lambda i: (jnp.minimum(i, last), 0))
    outs = pl.pallas_call(
        body,
        name="attn_bwd",
        grid=(nsteps + 1,),
        in_specs=[
            pl.BlockSpec(memory_space=pltpu.SMEM),
            blk(aw),
            blk(kvw),
            pl.BlockSpec((BLOCK, kvw), lambda i: (jnp.clip(Q_BLOCKS * i - 1, 0, nb - 1), 0)),
            blk(aw),
            _full(biasm.shape),
        ] + [HBM_SPEC] * nr,
        out_specs=[
            blk(aw),
            pl.BlockSpec((tile, kvw), lambda i: (jnp.maximum(i - 1, 0), 0)),
            _resident(biasm.shape[1:]),
            _resident((8, LANES)),
        ] + [HBM_SPEC] * nr,
        out_shape=[
            jax.ShapeDtypeStruct((seq, aw), BF16),
            jax.ShapeDtypeStruct((seq, kvw), BF16),
            jax.ShapeDtypeStruct(biasm.shape[1:], F32),
            jax.ShapeDtypeStruct((8, LANES), F32),
        ] + [jax.ShapeDtypeStruct(g.shape, g.dtype) for g in ready],
        scratch_shapes=[
            pltpu.VMEM((BLOCK, kvw), F32),
            pltpu.VMEM((max(held, 8), kvw), F32),
            pltpu.VMEM((Q_BLOCKS, N_KV_HEADS, pairs_per_kv * BAND, LANES), BF16),
            pltpu.VMEM((Q_BLOCKS, N_KV_HEADS, pairs_per_kv * BAND, LANES), BF16),
            pltpu.VMEM((Q_BLOCKS, N_KV_HEADS, BAND, pairs_per_kv * BAND), BF16),
            pltpu.VMEM((Q_BLOCKS, N_KV_HEADS, BAND, pairs_per_kv * BAND), BF16),
        ] + _exchange_sems(nr),
        compiler_params=_params(("arbitrary",), 48),
    )(sinks, q, kv, kv, dattn, biasm, *ready)
    return outs[:4], outs[4:]


def _relbias_grad(dssum2, bucket_row, chunk):
    heads, n = dssum2.shape

    def body(a_ref, bucket_ref, out_ref):
        @pl.when(pl.program_id(0) == 0)
        def _():
            out_ref[...] = jnp.zeros_like(out_ref)

        a = a_ref[...]
        hi = a.astype(BF16)
        lo = (a - hi.astype(F32)).astype(BF16)
        onehot_t = (lax.broadcasted_iota(jnp.int32, (LANES, chunk), 0) == bucket_ref[...]).astype(F32).astype(BF16)
        out_ref[...] += _dot_nt(hi, onehot_t) + _dot_nt(lo, onehot_t)

    return pl.pallas_call(
        body,
        name="relbias_grad",
        grid=(n // chunk,),
        in_specs=[pl.BlockSpec((heads, chunk), lambda i: (0, i)), pl.BlockSpec((1, chunk), lambda i: (0, i))],
        out_specs=_resident((heads, LANES)),
        out_shape=jax.ShapeDtypeStruct((heads, LANES), F32),
        compiler_params=_params(("arbitrary",), 32),
    )(dssum2, bucket_row)


def _layer_b_in_bwd(dh2, dq, dz2, dkv, h1, ya, wbin_g, wkv, kvn, bpre, sm, ready, ts):
    seq, d = h1.shape
    aw = dq.shape[1]
    kvw = dkv.shape[1]
    cw = wbin_g.shape[2]
    per = aw // cw

    nr = len(ready)
    nt = seq // ts

    def body(dh2_ref, dq_ref, dz2_ref, dkv_ref, h1_ref, ya_ref, wbin_ref, wkv_ref, kvn_ref, bpre_ref, sm_ref, *refs):
        ready_refs, (dh1_ref, dya_ref, acc_ref) = refs[:nr], refs[nr:nr + 3]
        landed_refs, sems = refs[nr + 3:2 * nr + 3], refs[2 * nr + 3:]

        @pl.when(pl.program_id(0) == 0)
        def _():
            acc_ref[...] = jnp.zeros_like(acc_ref)
            _exchange_start(ready_refs, landed_refs, *sems, True)

        @pl.when(pl.program_id(0) == nt - 1)
        def _():
            _exchange_wait(ready_refs, landed_refs, *sems, True)

        dn4 = jnp.zeros((ts, d), F32)
        for j in range(N_DEV):
            src = dq_ref if j < per else dz2_ref
            jj = j % per
            dn4 = dn4 + _dot_nt(src[:, jj * cw:(jj + 1) * cw], wbin_ref[j])
        dn3 = _dot_nt(dkv_ref[...], wkv_ref[...])
        hn, r = _rms(h1_ref[...])
        acc_ref[0:1, :] += jnp.sum(dn4 * hn, axis=0, keepdims=True)
        acc_ref[1:2, :] += jnp.sum(dn3 * hn, axis=0, keepdims=True)
        dh1 = dh2_ref[...].astype(F32) + _rms_bwd(dn4 * bpre_ref[...] + dn3 * kvn_ref[...], hn, r)
        dh1_ref[...] = dh1.astype(BF16)
        yan, r2 = _rms(ya_ref[...].astype(F32))
        acc_ref[2:3, :] += jnp.sum(dh1 * yan, axis=0, keepdims=True)
        dya_ref[...] = _rms_bwd(dh1 * sm_ref[4:5, :], yan, r2).astype(BF16)

    outs = pl.pallas_call(
        body,
        name="layer_b_in_bwd",
        grid=(nt,),
        in_specs=[_rows(ts, d), _rows(ts, aw), _rows(ts, aw), _rows(ts, kvw), _rows(ts, d), _rows(ts, d),
                  _full(wbin_g.shape), _full(wkv.shape), _full(kvn.shape), _full(bpre.shape), _full(sm.shape)]
        + [HBM_SPEC] * nr,
        out_specs=[_rows(ts, d), _rows(ts, d), _resident((8, d))] + [HBM_SPEC] * nr,
        out_shape=[jax.ShapeDtypeStruct((seq, d), BF16), jax.ShapeDtypeStruct((seq, d), BF16),
                   jax.ShapeDtypeStruct((8, d), F32)] + [jax.ShapeDtypeStruct(g.shape, g.dtype) for g in ready],
        scratch_shapes=_exchange_sems(nr),
        compiler_params=_params(("arbitrary",), 48),
    )(dh2, dq, dz2, dkv, h1, ya, wbin_g, wkv, kvn, bpre, sm, *ready)
    return outs[:3], outs[3:]


def _layer_a_bwd(dya, proj, conv, dh1, x2, wout, win_g, sm, ts):
    seq, d = x2.shape
    width = wout.shape[0]
    half = win_g.shape[2]
    n_half = width // half
    nt = seq // ts

    def body(dya_ref, proj_ref, conv_ref, dh1_ref, x_ref, wout_ref, win_ref, sm_ref, dproj_ref, gx_ref, acc_ref,
             dnext_ref):
        @pl.when(pl.program_id(0) == 0)
        def _():
            acc_ref[...] = jnp.zeros_like(acc_ref)
            dnext_ref[...] = jnp.zeros_like(dnext_ref)

        dy = _dot_nt(dya_ref[...], wout_ref[...])
        row = lax.broadcasted_iota(jnp.int32, (ts, half), 0)
        dn1 = jnp.zeros((ts, d), F32)
        for hh in range(n_half):
            cols = slice(hh * half, (hh + 1) * half)
            b, c, u, z = [proj_ref[:, (part * n_half + hh) * half:(part * n_half + hh + 1) * half].astype(F32)
                          for part in range(4)]
            cv = conv_ref[:, cols].astype(F32)
            dyh = dy[:, cols]
            sz, dsz = _silu(z)
            dconv = dyh * b * sz
            grads = [dyh * cv * sz, None, None, dyh * b * cv * dsz]
            next0, next1 = dnext_ref[0:1, cols], dnext_ref[1:2, cols]
            dc1 = jnp.where(row == ts - 1, next0, pltpu.roll(dconv, ts - 1, 0))
            dc2 = jnp.where(row == ts - 1, next1, jnp.where(row == ts - 2, next0, pltpu.roll(dconv, ts - 2, 0)))
            dnext_ref[:, cols] = dconv[0:8, :]
            v = c * u
            acc_ref[1:2, cols] += jnp.sum(dc2 * v, axis=0, keepdims=True)
            acc_ref[2:3, cols] += jnp.sum(dc1 * v, axis=0, keepdims=True)
            acc_ref[3:4, cols] += jnp.sum(dconv * v, axis=0, keepdims=True)
            dv = sm_ref[3:4, cols] * dconv + sm_ref[2:3, cols] * dc1 + sm_ref[1:2, cols] * dc2
            grads[1] = dv * u
            grads[2] = dv * c
            for part in range(4):
                j = part * n_half + hh
                gj = grads[part].astype(BF16)
                dproj_ref[:, j * half:(j + 1) * half] = gj
                dn1 = dn1 + _dot_nt(gj, win_ref[j])
        xn, r = _rms(x_ref[...])
        acc_ref[0:1, :] += jnp.sum(dn1 * xn, axis=0, keepdims=True)
        gx_ref[...] = dh1_ref[...].astype(F32) + _rms_bwd(dn1 * sm_ref[0:1, :], xn, r)

    rev = lambda w: pl.BlockSpec((ts, w), lambda i: (nt - 1 - i, 0))
    return pl.pallas_call(
        body,
        name="layer_a_bwd",
        grid=(nt,),
        in_specs=[rev(d), rev(4 * width), rev(width), rev(d), rev(d), _full(wout.shape), _full(win_g.shape), _full(sm.shape)],
        out_specs=[rev(4 * width), rev(d), _resident((8, d))],
        out_shape=[jax.ShapeDtypeStruct((seq, 4 * width), BF16), jax.ShapeDtypeStruct((seq, d), F32),
                   jax.ShapeDtypeStruct((8, d), F32)],
        scratch_shapes=[pltpu.VMEM((8, width), F32)],
        compiler_params=_params(("arbitrary",), 56),
    )(dya, proj, conv, dh1, x2, wout, win_g, sm)


def _wgrad(a, bs, n_slots, ts, name, ready=(), block_cols=1024):
    nr = len(ready)
    seq, k = a.shape
    nb_in = len(bs)
    n_each = bs[0].shape[1]
    n = nb_in * n_each
    bn = min(n_each, block_cols)
    per_in = n_each // bn
    n_blocks = nb_in * per_in
    ns = seq // ts

    def b_spec(idx):
        def index(j, s):
            mine = j // per_in == idx
            row = jnp.where(mine, s, jnp.where(j // per_in > idx, ns - 1, 0))
            return (row, jnp.where(mine, j % per_in, jnp.where(j // per_in > idx, per_in - 1, 0)))
        return pl.BlockSpec((ts, bn), index)

    if n_slots:
        sw = n // n_slots
        spb = bn // sw
        out_shape = jax.ShapeDtypeStruct((n_slots, k, sw), BF16)
        out_spec = pl.BlockSpec((spb, k, sw), lambda j, s: (j, 0, 0))
    else:
        out_shape = jax.ShapeDtypeStruct((k, n), BF16)
        out_spec = pl.BlockSpec((k, bn), lambda j, s: (0, j))

    def body(a_ref, *refs):
        b_refs, ready_refs, o_ref = refs[:nb_in], refs[nb_in:nb_in + nr], refs[nb_in + nr]
        landed_refs, (acc_ref, *sems) = refs[nb_in + nr + 1:nb_in + 2 * nr + 1], refs[nb_in + 2 * nr + 1:]
        j, s = pl.program_id(0), pl.program_id(1)

        if nr:
            @pl.when(jnp.logical_and(j == 0, s == 0))
            def _():
                _exchange_start(ready_refs, landed_refs, *sems, True)

            @pl.when(jnp.logical_and(j == n_blocks - 1, s == ns - 1))
            def _():
                _exchange_wait(ready_refs, landed_refs, *sems, True)

        @pl.when(s == 0)
        def _():
            acc_ref[...] = jnp.zeros_like(acc_ref)

        for idx in range(nb_in):
            @pl.when(j // per_in == idx)
            def _(idx=idx):
                acc_ref[...] += _dot_tn(a_ref[...], b_refs[idx][...])

        @pl.when(s == ns - 1)
        def _():
            if n_slots:
                for e in range(spb):
                    o_ref[e] = acc_ref[:, e * sw:(e + 1) * sw].astype(BF16)
            else:
                o_ref[...] = acc_ref[...].astype(BF16)

    outs = pl.pallas_call(
        body,
        name=name,
        grid=(n_blocks, ns),
        in_specs=[pl.BlockSpec((ts, k), lambda j, s: (s, 0))] + [b_spec(idx) for idx in range(nb_in)] + [HBM_SPEC] * nr,
        out_specs=[out_spec] + [HBM_SPEC] * nr,
        out_shape=[out_shape] + [jax.ShapeDtypeStruct(g.shape, g.dtype) for g in ready],
        scratch_shapes=[pltpu.VMEM((k, bn), F32)] + (_exchange_sems(nr) if nr else []),
        compiler_params=_params(("arbitrary", "arbitrary"), 48),
    )(a, *bs, *ready)
    return (outs[0], outs[1:]) if nr else outs[0]


def _wgrad_tail(pairs, part, landed, ts):
    n_tasks = len(pairs)
    nl = len(landed)
    seq, k = pairs[0][0].shape
    n = pairs[0][1].shape[1]
    ns = seq // ts
    total = n_tasks * ns
    per = k // N_DEV

    def spec(t, width):
        return pl.BlockSpec((ts, width), lambda j, s: (jnp.where(j == t, s, jnp.where(j > t, ns - 1, 0)), 0))

    def body(*refs):
        ab_refs, part_ref = refs[:2 * n_tasks], refs[2 * n_tasks]
        landed_refs, refs = refs[2 * n_tasks + 1:2 * n_tasks + 1 + nl], refs[2 * n_tasks + 1 + nl:]
        o_ref, red_ref = refs[:2]
        summed_refs, (acc_ref, sib_ref, chip_ref, send_ref, *sems) = refs[2:2 + nl], refs[2 + nl:]
        j, s = pl.program_id(0), pl.program_id(1)
        flat = j * ns + s
        swap, send, forward, finish = _chip_reduce(part_ref, red_ref, sib_ref, chip_ref, send_ref, sems)

        @pl.when(flat == 0)
        def _():
            swap()

        @pl.when(flat == min(1, total - 1))
        def _():
            send()

        @pl.when(flat == min(total // 2 + 1, total - 1))
        def _():
            forward()
            for t in range(nl):
                _sum_slots(landed_refs[t], summed_refs[t])

        @pl.when(s == 0)
        def _():
            acc_ref[...] = jnp.zeros_like(acc_ref)

        for t in range(n_tasks):
            @pl.when(j == t)
            def _(t=t):
                acc_ref[...] += _dot_tn(ab_refs[2 * t][...], ab_refs[2 * t + 1][...])

        @pl.when(s == ns - 1)
        def _():
            for dev in range(N_DEV):
                o_ref[dev] = acc_ref[dev * per:(dev + 1) * per, :].astype(BF16)

        @pl.when(flat == total - 1)
        def _():
            finish()

    slot = part.shape[1:]
    outs = pl.pallas_call(
        body,
        name="wgrad_tail",
        grid=(n_tasks, ns),
        in_specs=[spec(t, w) for t in range(n_tasks) for w in (k, n)] + [_full(part.shape)]
        + [_full(g.shape) for g in landed],
        out_specs=[pl.BlockSpec((N_DEV, per, n), lambda j, s: (0, j, 0)), _resident(slot)]
        + [_resident(g.shape[1:]) for g in landed],
        out_shape=[jax.ShapeDtypeStruct((N_DEV, n_tasks * per, n), BF16), jax.ShapeDtypeStruct(slot, F32)]
        + [jax.ShapeDtypeStruct(g.shape[1:], F32) for g in landed],
        scratch_shapes=[pltpu.VMEM((k, n), F32)] + _chip_reduce_scratch(slot),
        compiler_params=_params(("arbitrary", "arbitrary"), 58),
    )(*[op for pair in pairs for op in pair], part, *landed)
    return outs[0], outs[1], outs[2:]


def _adamw(ws, gs, ms, vs):
    n = len(ws)

    def step(w, g, m, v):
        m = ADAM_B1 * m + (1.0 - ADAM_B1) * g
        v = ADAM_B2 * v + (1.0 - ADAM_B2) * jnp.square(g)
        m_hat = m / (1.0 - ADAM_B1 ** ADAM_STEP)
        v_hat = v / (1.0 - ADAM_B2 ** ADAM_STEP)
        return g, -ADAM_LR * (m_hat / (jnp.sqrt(v_hat) + ADAM_EPS) + ADAM_WD * w), m, v

    def body(*refs):
        w_refs, g_refs, m_refs, v_refs = (refs[k * n:(k + 1) * n] for k in range(4))
        go_refs, d_refs, nm_refs, nv_refs = (refs[(4 + k) * n:(5 + k) * n] for k in range(4))
        for t in range(n):
            rows = w_refs[t].shape[0]
            if rows <= 128:
                go_refs[t][...], d_refs[t][...], nm_refs[t][...], nv_refs[t][...] = step(
                    w_refs[t][...], g_refs[t][...], m_refs[t][...], v_refs[t][...])
                continue
            chunk = 128

            def one(i, carry, t=t):
                r = pl.ds(pl.multiple_of(i * chunk, chunk), chunk)
                go_refs[t][r, :], d_refs[t][r, :], nm_refs[t][r, :], nv_refs[t][r, :] = step(
                    w_refs[t][r, :], g_refs[t][r, :], m_refs[t][r, :], v_refs[t][r, :])
                return carry

            lax.fori_loop(0, rows // chunk, one, 0)

    vmem = pl.BlockSpec(memory_space=pltpu.VMEM)
    outs = pl.pallas_call(
        body,
        name="adamw",
        in_specs=[vmem] * (4 * n),
        out_specs=[vmem] * (4 * n),
        out_shape=[jax.ShapeDtypeStruct(w.shape, F32) for w in ws] * 4,
        compiler_params=_params(vmem_mib=56),
    )(*ws, *gs, *ms, *vs)
    return outs[:n], outs[n:2 * n], outs[2 * n:3 * n], outs[3 * n:]


def _band_structure():
    q_loc = jnp.arange(BLOCK, dtype=jnp.int32)[:, None]
    s_loc = jnp.arange(2 * BLOCK, dtype=jnp.int32)[None, :]
    dist = q_loc + BLOCK - s_loc
    in_window = (dist >= 0) & (dist < BLOCK)
    dd = jnp.maximum(dist, 0)
    max_exact = N_BUCKETS // 2
    large = max_exact + (jnp.log(jnp.maximum(dd, 1).astype(F32) / max_exact) / math.log(MAX_DISTANCE / max_exact)
                         * (N_BUCKETS - max_exact)).astype(jnp.int32)
    bucket = jnp.where(dd < max_exact, dd, jnp.minimum(large, N_BUCKETS - 1))
    return bucket, in_window.astype(jnp.int32)


def _place_rows(a, row, rows=8):
    return jnp.pad(a, ((row, rows - row - a.shape[0]), (0, 0)))


def kernel(x, a_pre_norm, a_w_in, a_conv_w, a_w_out, a_post_norm, kv_norm, w_kv, rel_bias, b_pre_norm, b_w_in, b_sinks, b_w_out, b_post_norm, loss_target, m_a_pre_norm, m_a_w_in, m_a_conv_w, m_a_w_out, m_a_post_norm, m_kv_norm, m_w_kv, m_rel_bias, m_b_pre_norm, m_b_w_in, m_b_sinks, m_b_w_out, m_b_post_norm, v_a_pre_norm, v_a_w_in, v_a_conv_w, v_a_w_out, v_a_post_norm, v_kv_norm, v_w_kv, v_rel_bias, v_b_pre_norm, v_b_w_in, v_b_sinks, v_b_w_out, v_b_post_norm):
    seq, d = x.shape[1], x.shape[2]
    x2 = x.reshape(seq, d)
    target = loss_target.reshape(seq, d)
    shard = a_pre_norm.shape[1]
    me = _my_index()
    ts_a = min(seq, 512)
    ts = min(seq, 512)
    ts_w = min(seq, 2048)

    small = _place_rows(a_pre_norm, 0) + _place_rows(a_conv_w[0], 1) + _place_rows(a_post_norm, 4)
    bucket, in_window = _band_structure()
    win_g, wout_g, small_g, biasm = _all_gather(
        [a_w_in[0], a_w_out[0], small], [BF16, BF16, F32], rel_bias.T, bucket.T, in_window.T)
    wout = wout_g.reshape(-1, wout_g.shape[2])
    sm = small_g.transpose(1, 0, 2).reshape(8, N_DEV * shard)
    kvn = kv_norm.reshape(1, d)

    (h1, n1, proj, conv, y, ya), (wkv_g, wbin_g, wbout_g) = _layer_a_fwd(
        x2, sm, win_g, wout, [w_kv.astype(BF16), b_w_in[0].astype(BF16), b_w_out[0].astype(BF16)], ts_a)
    wkv = wkv_g.reshape(-1, wkv_g.shape[2])
    wbout = wbout_g.reshape(-1, wbout_g.shape[2])
    n3, n4, kv, q, o, dh2, dyb, dattn, dz2, acc_c = _layer_b_fwd(
        h1, target, kvn, b_pre_norm, wkv, wbin_g, biasm, b_sinks, wbout, b_post_norm)

    (dq, dkv, dssum, dsink), _ = _attn_bwd(q, kv, dattn, biasm, b_sinks, [])
    by_head = dssum.reshape(N_PAIRS, BAND, 2, BLOCK).transpose(0, 2, 3, 1)
    relb = _relbias_grad(by_head.reshape(N_Q_HEADS, -1), bucket.reshape(1, -1), 4096)
    g_wkv = _wgrad(n3, [dkv], 0, ts_w, "wgrad_kv").reshape(wkv_g.shape)
    g_wbin = _wgrad(n4, [dq, dz2], N_DEV, ts_w, "wgrad_b_in")
    (dh1, dya, acc_b), (l_wkv, l_wbin) = _layer_b_in_bwd(
        dh2, dq, dz2, dkv, h1, ya, wbin_g, wkv, kvn, b_pre_norm, sm, [g_wkv, g_wbin], ts)
    dproj, gx, acc_a = _layer_a_bwd(dya, proj, conv, dh1, x2, wout, win_g, sm, ts_a)
    g_win = _wgrad(n1, [dproj], N_DEV, ts_w, "wgrad_a_in", block_cols=2048)
    g_outs, r_win, (r_wkv, r_wbin) = _wgrad_tail([(y, dya), (o, dyb)], g_win, [l_wkv, l_wbin], min(seq, 1024))

    r_outs, _, (s_a, s_b, s_c, s_relb, s_sink) = _reduce_exchange(g_outs, [], [acc_a, acc_b, acc_c, relb, dsink])
    rows_out = wout_g.shape[1]
    r_wout, r_wbout = r_outs[:rows_out], r_outs[rows_out:]
    mine = lambda rows: lax.dynamic_slice_in_dim(rows, me * shard, shard, axis=1)
    loss = s_c[1, 0]
    weights = [a_pre_norm, a_w_in[0], a_conv_w[0], a_w_out[0], a_post_norm, kvn, w_kv, rel_bias.T, b_pre_norm,
               b_w_in[0], b_sinks, b_w_out[0], b_post_norm]
    grads = [mine(s_a[0:1]), r_win, mine(s_a[1:4]), r_wout, mine(s_b[2:3]), s_b[1:2], r_wkv,
             s_relb[:, :N_BUCKETS], s_b[0:1], r_wbin, s_sink[0:1, :N_Q_HEADS], r_wbout, s_c[0:1]]
    first = [m_a_pre_norm, m_a_w_in[0], m_a_conv_w[0], m_a_w_out[0], m_a_post_norm, m_kv_norm.reshape(1, d), m_w_kv,
             m_rel_bias.T, m_b_pre_norm, m_b_w_in[0], m_b_sinks, m_b_w_out[0], m_b_post_norm]
    second = [v_a_pre_norm, v_a_w_in[0], v_a_conv_w[0], v_a_w_out[0], v_a_post_norm, v_kv_norm.reshape(1, d), v_w_kv,
              v_rel_bias.T, v_b_pre_norm, v_b_w_in[0], v_b_sinks, v_b_w_out[0], v_b_post_norm]
    grads, deltas, new_m, new_v = _adamw(weights, grads, first, second)

    shapes = [a_pre_norm.shape, a_w_in.shape, a_conv_w.shape, a_w_out.shape, a_post_norm.shape, kv_norm.shape,
              w_kv.shape, None, b_pre_norm.shape, b_w_in.shape, b_sinks.shape, b_w_out.shape, b_post_norm.shape]
    shaped = lambda arrays: [a.T if s is None else a.reshape(s) for a, s in zip(arrays, shapes)]
    return (loss, gx.reshape(x.shape), *shaped(grads), *shaped(deltas), *shaped(new_m), *shaped(new_v))
```

```python
import math

import jax
import jax.numpy as jnp
from jax import lax
from jax.experimental import pallas as pl
from jax.experimental.pallas import tpu as pltpu

HEAD_DIM = 64
N_Q_HEADS = 16
N_KV_HEADS = 2
GROUP = N_Q_HEADS // N_KV_HEADS
BLOCK = 128
N_BUCKETS = 32
MAX_DISTANCE = 128
EPS = 1e-6
NEG_INF = -1e30
SCALE = HEAD_DIM ** -0.5

ADAM_LR = 0.001
ADAM_B1 = 0.9
ADAM_B2 = 0.999
ADAM_EPS = 1e-08
ADAM_WD = 0.01
ADAM_STEP = 10

N_PAIRS = N_Q_HEADS // 2
BAND = 2 * BLOCK

N_DEV = 8
GATHER_PIECE_ROWS = 256
LANES = 128
F32 = jnp.float32
BF16 = jnp.bfloat16
MESH = pl.DeviceIdType.MESH
MIB = 1024 * 1024


def _params(semantics=None, vmem_mib=48):
    return pltpu.CompilerParams(dimension_semantics=semantics, vmem_limit_bytes=vmem_mib * MIB)


def _full(shape):
    zeros = (0,) * len(shape)
    return pl.BlockSpec(shape, lambda *_: zeros, pipeline_mode=pl.Buffered(1))


def _resident(shape):
    zeros = (0,) * len(shape)
    return pl.BlockSpec(shape, lambda *_: zeros)


def _rows(ts, cols):
    return pl.BlockSpec((ts, cols), lambda i: (i, 0))


def _dot(a, b):
    return jnp.dot(a, b, preferred_element_type=F32)


def _dot_nt(a, b):
    return lax.dot_general(a, b, (((1,), (1,)), ((), ())), preferred_element_type=F32)


def _dot_tn(a, b):
    return lax.dot_general(a, b, (((0,), (0,)), ((), ())), preferred_element_type=F32)


def _rms(xf):
    r = lax.rsqrt(jnp.mean(xf * xf, axis=-1, keepdims=True) + EPS)
    return xf * r, r


def _rms_bwd(dn, xn, r):
    return r * (dn - xn * jnp.mean(dn * xn, axis=-1, keepdims=True))


def _silu(z):
    s = jax.nn.sigmoid(z)
    return z * s, s * (1.0 + z * (1.0 - s))


def _my_index():
    return 4 * lax.axis_index("x") + 2 * lax.axis_index("y") + lax.axis_index("c")


def _bias_table(rb_ref, bucket_ref, win_ref, out_ref):
    bk = jnp.where(win_ref[...] != 0, bucket_ref[...], -1)
    has_prev = lax.broadcasted_iota(jnp.int32, bk.shape, 0) >= BLOCK
    for h in range(N_Q_HEADS):
        acc = jnp.full(bk.shape, NEG_INF, F32)
        for b in range(N_BUCKETS):
            acc = jnp.where(bk == b, rb_ref[h, b], acc)
        cols = slice((h % 2) * BLOCK, (h % 2 + 1) * BLOCK)
        out_ref[1, h // 2, :, cols] = acc
        out_ref[0, h // 2, :, cols] = jnp.where(has_prev, acc, NEG_INF)


def _all_gather(shards, out_dtypes, rel_bias_t, bucket_t, in_window_t):
    n = len(shards)
    pieces = [(t, r0, min(GATHER_PIECE_ROWS, s.shape[0] - r0))
              for t, s in enumerate(shards) for r0 in range(0, s.shape[0], GATHER_PIECE_ROWS)]

    def body(*refs):
        ins, (rb_ref, bucket_ref, win_ref) = refs[:n], refs[n:n + 3]
        outs, bias_ref = refs[n + 3:2 * n + 3], refs[2 * n + 3]
        send_sems, recv_sems = refs[2 * n + 4], refs[2 * n + 5]
        x, y, c = lax.axis_index("x"), lax.axis_index("y"), lax.axis_index("c")
        me, sibling = (x, y, c), (x, y, 1 - c)
        x_nbr, y_nbr, diagonal = (1 - x, y), (x, 1 - y), (1 - x, 1 - y)
        south = c == 0
        relayed = (jnp.where(south, 1 - x, x), jnp.where(south, y, 1 - y))
        relay_to = (jnp.where(south, x, 1 - x), jnp.where(south, 1 - y, y))

        def copy(u, k, block, to):
            t, r0, nrows = pieces[u]
            rows = outs[t].at[4 * block[0] + 2 * block[1] + block[2], pl.ds(r0, nrows)]
            return pltpu.make_async_remote_copy(
                src_ref=rows, dst_ref=rows, send_sem=send_sems.at[u, k], recv_sem=recv_sems.at[u, k],
                device_id=to, device_id_type=MESH)

        for t in range(n):
            outs[t][pl.ds(_my_index(), 1)] = ins[t][...].astype(outs[t].dtype)[None]
        started = []

        def start(cp):
            cp.start()
            started.append(cp)

        units = range(len(pieces))
        for u in units:
            start(copy(u, 0, me, sibling))
            start(copy(u, 1, me, (*x_nbr, c)))
            start(copy(u, 2, me, (*y_nbr, c)))
        _bias_table(rb_ref, bucket_ref, win_ref, bias_ref)
        for u in units:
            for k, chip in ((1, x_nbr), (2, y_nbr)):
                copy(u, k, (*chip, c), me).wait_recv()
                start(copy(u, 3 + k, (*chip, c), sibling))
            start(copy(u, 3, (*relayed, c), (*relay_to, c)))
        for u in units:
            copy(u, 3, (*diagonal, c), me).wait_recv()
            start(copy(u, 6, (*diagonal, c), sibling))
        for u in units:
            copy(u, 0, sibling, me).wait_recv()
        for k, chip in ((4, x_nbr), (5, y_nbr), (6, diagonal)):
            for u in units:
                copy(u, k, (*chip, 1 - c), me).wait_recv()
        for cp in started:
            cp.wait_send()

    vmem = pl.BlockSpec(memory_space=pltpu.VMEM)
    return pl.pallas_call(
        body,
        name="gather_weights",
        out_shape=[jax.ShapeDtypeStruct((N_DEV,) + s.shape, dt) for s, dt in zip(shards, out_dtypes)]
        + [jax.ShapeDtypeStruct((2, N_PAIRS, BAND, 2 * BLOCK), F32)],
        in_specs=[vmem] * n + [pl.BlockSpec(memory_space=pltpu.SMEM), vmem, vmem],
        out_specs=[vmem] * (n + 1),
        scratch_shapes=[pltpu.SemaphoreType.DMA((len(pieces), 7)), pltpu.SemaphoreType.DMA((len(pieces), 7))],
        compiler_params=_params(vmem_mib=48),
    )(*shards, rel_bias_t, bucket_t, in_window_t)


def _peer(k):
    x, y, c = lax.axis_index("x"), lax.axis_index("y"), lax.axis_index("c")
    px = 1 - x if k & 4 else x
    py = 1 - y if k & 2 else y
    pc = 1 - c if k & 1 else c
    return (px, py, pc), 4 * px + 2 * py + pc


def _exchange(srcs, dsts, send_sems, recv_sems, local_sems, scatter):
    me = _my_index()
    sends, arrivals = [], []
    for k in range(1, N_DEV):
        peer, pidx = _peer(k)
        for t, (src, dst) in enumerate(zip(srcs, dsts)):
            mine = src.at[pidx] if scatter else src
            sems = dict(send_sem=send_sems.at[t, k - 1], recv_sem=recv_sems.at[t, k - 1], device_id=peer, device_id_type=MESH)
            sends.append(pltpu.make_async_remote_copy(src_ref=mine, dst_ref=dst.at[me], **sems))
            arrivals.append(pltpu.make_async_remote_copy(src_ref=mine, dst_ref=dst.at[pidx], **sems))
    local = [pltpu.make_async_copy(src.at[me] if scatter else src, dst.at[me], local_sems.at[t])
             for t, (src, dst) in enumerate(zip(srcs, dsts))]
    return sends, arrivals, local


def _exchange_start(*args):
    sends, _, local = _exchange(*args)
    for cp in sends + local:
        cp.start()


def _exchange_wait(*args):
    sends, arrivals, local = _exchange(*args)
    for cp in arrivals:
        cp.wait_recv()
    for cp in sends:
        cp.wait_send()
    for cp in local:
        cp.wait()


def _exchange_sems(n):
    if not n:
        return []
    return [pltpu.SemaphoreType.DMA((n, N_DEV - 1)), pltpu.SemaphoreType.DMA((n, N_DEV - 1)), pltpu.SemaphoreType.DMA((n,))]


HBM_SPEC = pl.BlockSpec(memory_space=pl.ANY)


def _sum_slots(recv_ref, out_ref):
    rows = out_ref.shape[0]
    chunk = min(rows, 128)

    def add(i, carry):
        r0 = pl.multiple_of(i * chunk, chunk)
        acc = recv_ref[0, pl.ds(r0, chunk), :].astype(F32)
        for dev in range(1, N_DEV):
            acc = acc + recv_ref[dev, pl.ds(r0, chunk), :].astype(F32)
        out_ref[pl.ds(r0, chunk), :] = acc
        return carry

    lax.fori_loop(0, rows // chunk, add, 0)


N_CHIPS = N_DEV // 2


def _rows_loop(rows, fn):
    chunk = min(rows, 128)

    def step(i, carry):
        fn(pl.ds(pl.multiple_of(i * chunk, chunk), chunk))
        return carry

    lax.fori_loop(0, rows // chunk, step, 0)


def _chip_reduce(g_ref, out_ref, sib_ref, land_ref, send_ref, sems):
    sib_send, sib_recv, ici_send, ici_recv = sems
    x, y, c = lax.axis_index("x"), lax.axis_index("y"), lax.axis_index("c")
    south = c == 0
    near =(jnp.where(south, 1 - x, x), jnp.where(south, y, 1 - y))
    far = (jnp.where(south, x, 1 - x), jnp.where(south, 1 - y, y))
    diagonal = (1 - x, 1 - y)
    rows = out_ref.shape[0]
    direct, fold, folded = 0, 1, 2

    def to_sibling(t):
        return pltpu.make_async_remote_copy(
            src_ref=g_ref.at[2 * t + 1 - c], dst_ref=sib_ref.at[t], send_sem=sib_send.at[t], recv_sem=sib_recv.at[t],
            device_id=(x, y, 1 - c), device_id_type=MESH)

    def ici(role, chip):
        return pltpu.make_async_remote_copy(
            src_ref=send_ref.at[role], dst_ref=land_ref.at[role], send_sem=ici_send.at[role],
            recv_sem=ici_recv.at[role], device_id=(*chip, c), device_id_type=MESH)

    def pair_sum(chip, r):
        t = 2 * chip[0] + chip[1]
        return g_ref[2 * t + c, r, :].astype(F32) + sib_ref[t, r, :].astype(F32)

    def swap():
        for t in range(N_CHIPS):
            to_sibling(t).start()

    def send():
        for t in range(N_CHIPS):
            to_sibling(t).wait_recv()
        for role, chip in ((fold, diagonal), (direct, near)):
            def fill(r, role=role, chip=chip):
                send_ref[role, r, :] = pair_sum(chip, r).astype(BF16)

            _rows_loop(rows, fill)
            ici(role, near).start()

    def forward():
        ici(fold, near).wait_recv()

        def fill(r):
            send_ref[folded, r, :] = (pair_sum(far, r) + land_ref[fold, r, :].astype(F32)).astype(BF16)

        _rows_loop(rows, fill)
        ici(folded, far).start()

    def finish():
        ici(direct, near).wait_recv()
        ici(folded, far).wait_recv()

        def total(r):
            mine = pair_sum((x, y), r)
            out_ref[r, :] = mine + land_ref[direct, r, :].astype(F32) + land_ref[folded, r, :].astype(F32)

        _rows_loop(rows, total)
        for t in range(N_CHIPS):
            to_sibling(t).wait_send()
        for role, chip in ((direct, near), (fold, near), (folded, far)):
            ici(role, chip).wait_send()

    return swap, send, forward, finish


def _chip_reduce_scratch(slot):
    return [pltpu.VMEM((N_CHIPS,) + slot, BF16), pltpu.VMEM((3,) + slot, BF16), pltpu.VMEM((3,) + slot, BF16),
            pltpu.SemaphoreType.DMA((N_CHIPS,)), pltpu.SemaphoreType.DMA((N_CHIPS,)),
            pltpu.SemaphoreType.DMA((3,)), pltpu.SemaphoreType.DMA((3,))]


def _reduce_exchange(part, landed, smalls):
    nl, ng = len(landed), len(smalls)
    n_out = 1 + nl + ng

    def body(*refs):
        p_in, l_in, s_in = refs[0], refs[1:1 + nl], refs[1 + nl:n_out]
        p_out, l_out, s_out = refs[n_out], refs[n_out + 1:n_out + 1 + nl], refs[n_out + 1 + nl:2 * n_out]
        scratch = refs[2 * n_out:]
        s_recv, (sib_ref, chip_ref, send_ref), sems = scratch[:ng], scratch[ng:ng + 3], scratch[ng + 3:]
        swap, send, forward, finish = _chip_reduce(p_in, p_out, sib_ref, chip_ref, send_ref, sems[:4])
        swap()
        _exchange_start(s_in, s_recv, *sems[4:], False)
        send()
        for t in range(nl):
            _sum_slots(l_in[t], l_out[t])
        forward()
        finish()
        _exchange_wait(s_in, s_recv, *sems[4:], False)
        for t in range(ng):
            acc = s_recv[t][0]
            for dev in range(1, N_DEV):
                acc = acc + s_recv[t][dev]
            s_out[t][...] = acc

    vmem = pl.BlockSpec(memory_space=pltpu.VMEM)
    slot = part.shape[1:]
    outs = pl.pallas_call(
        body,
        name="reduce_grads",
        out_shape=[jax.ShapeDtypeStruct(p.shape[1:], F32) for p in [part] + landed]
        + [jax.ShapeDtypeStruct(s.shape, F32) for s in smalls],
        in_specs=[vmem] * n_out,
        out_specs=[vmem] * n_out,
        scratch_shapes=[pltpu.VMEM((N_DEV,) + s.shape, F32) for s in smalls] + _chip_reduce_scratch(slot)
        + _exchange_sems(ng),
        compiler_params=_params(vmem_mib=56),
    )(part, *landed, *smalls)
    return outs[0], outs[1:1 + nl], outs[1 + nl:]


def _layer_a_fwd(x2, sm, win_g, wout, later, ts):
    seq, d = x2.shape
    width = wout.shape[0]
    half = win_g.shape[2]
    n_half = width // half
    nl = len(later)
    nt = seq // ts

    def body(x_ref, sm_ref, win_ref, wout_ref, *refs):
        shard_refs, refs = refs[:nl], refs[nl:]
        h1_ref, n1_ref, proj_ref, conv_ref, y_ref, ya_ref = refs[:6]
        gathered_refs, (vprev_ref, *sems) = refs[6:6 + nl], refs[6 + nl:]

        @pl.when(pl.program_id(0) == 0)
        def _():
            vprev_ref[...] = jnp.zeros_like(vprev_ref)
            _exchange_start(shard_refs, gathered_refs, *sems, False)

        @pl.when(pl.program_id(0) == nt - 1)
        def _():
            _exchange_wait(shard_refs, gathered_refs, *sems, False)

        xf = x_ref[...]
        xn, _ = _rms(xf)
        n1 = (xn * sm_ref[0:1, :]).astype(BF16)
        n1_ref[...] = n1
        row = lax.broadcasted_iota(jnp.int32, (ts, half), 0)
        ya = jnp.zeros((ts, d), F32)
        for hh in range(n_half):
            cols = slice(hh * half, (hh + 1) * half)
            parts = []
            for part in range(4):
                j = part * n_half + hh
                pj = _dot(n1, win_ref[j])
                proj_ref[:, j * half:(j + 1) * half] = pj.astype(BF16)
                parts.append(pj)
            b, c, u, z = parts
            v = c * u
            last1, last2 = vprev_ref[7:8, cols], vprev_ref[6:7, cols]
            v1 = jnp.where(row == 0, last1, pltpu.roll(v, 1, 0))
            v2 = jnp.where(row == 0, last2, jnp.where(row == 1, last1, pltpu.roll(v, 2, 0)))
            vprev_ref[:, cols] = v[ts - 8:ts, :]
            conv = sm_ref[1:2, cols] * v2 + sm_ref[2:3, cols] * v1 + sm_ref[3:4, cols] * v
            conv_ref[:, cols] = conv.astype(BF16)
            yh = (b * conv * _silu(z)[0]).astype(BF16)
            y_ref[:, cols] = yh
            ya = ya + _dot(yh, wout_ref[cols, :])
        ya_ref[...] = ya
        h1_ref[...] = xf + _rms(ya)[0] * sm_ref[4:5, :]

    outs = pl.pallas_call(
        body,
        name="layer_a_fwd",
        grid=(nt,),
        in_specs=[_rows(ts, d), _full(sm.shape), _full(win_g.shape), _full(wout.shape)] + [HBM_SPEC] * nl,
        out_specs=[_rows(ts, d), _rows(ts, d), _rows(ts, 4 * width), _rows(ts, width), _rows(ts, width), _rows(ts, d)]
        + [HBM_SPEC] * nl,
        out_shape=[
            jax.ShapeDtypeStruct((seq, d), F32),
            jax.ShapeDtypeStruct((seq, d), BF16),
            jax.ShapeDtypeStruct((seq, 4 * width), BF16),
            jax.ShapeDtypeStruct((seq, width), BF16),
            jax.ShapeDtypeStruct((seq, width), BF16),
            jax.ShapeDtypeStruct((seq, d), F32),
        ] + [jax.ShapeDtypeStruct((N_DEV,) + s.shape, s.dtype) for s in later],
        scratch_shapes=[pltpu.VMEM((8, width), F32)] + _exchange_sems(nl),
        compiler_params=_params(("arbitrary",), 56),
    )(x2, sm, win_g, wout, *later)
    return outs[:6], outs[6:]


Q_BLOCKS = 4
ATTN_BWD_LAGS = (2, 4)
ATTN_FWD_LAGS = (2, 4)


def _banded_tiles(kvp_ref, kvc_ref):
    tile = kvc_ref[...].astype(F32)
    blocks = [kvp_ref[...].astype(F32)] + [tile[u * BLOCK:(u + 1) * BLOCK] for u in range(Q_BLOCKS)]
    return [_banded_kv(blocks[u], blocks[u + 1]) for u in range(Q_BLOCKS)]


def _bias_of(bias_ref, i, u, m):
    return bias_ref[jnp.minimum(i, 1) if u == 0 else 1, m]


def _banded_kv(kvp, kvc):
    kw = N_KV_HEADS * HEAD_DIM
    out = []
    for full in (jnp.concatenate([kvp[:, :kw], kvc[:, :kw]], axis=0), jnp.concatenate([kvp[:, kw:], kvc[:, kw:]], axis=0)):
        lo = lax.broadcasted_iota(jnp.int32, full.shape, 1) < HEAD_DIM
        rolled = pltpu.roll(full, HEAD_DIM, 1)
        x2 = [jnp.where(lo, full, rolled).astype(BF16), jnp.where(lo, rolled, full).astype(BF16)]
        ft = full.T
        x2t = [jnp.concatenate([ft[kh * HEAD_DIM:(kh + 1) * HEAD_DIM]] * 2, axis=0).astype(BF16) for kh in range(N_KV_HEADS)]
        out += [x2, x2t]
    return out


def _pair_rows(ref, rows, m, scale=None):
    both = ref[rows, m * LANES:(m + 1) * LANES].astype(F32)
    if scale is not None:
        both = both * scale
    lo = lax.broadcasted_iota(jnp.int32, both.shape, 1) < HEAD_DIM
    zero = jnp.zeros_like(both)
    return jnp.concatenate([jnp.where(lo, both, zero), jnp.where(lo, zero, both)], axis=0).astype(BF16)


def _pair_cols(res_t):
    top = lax.broadcasted_iota(jnp.int32, (LANES, BLOCK), 0) < HEAD_DIM
    return jnp.where(top, res_t[:, :BLOCK], res_t[:, BLOCK:]).T


def _sink_row(sink_ref, m):
    first = lax.broadcasted_iota(jnp.int32, (1, 2 * BLOCK), 1) < BLOCK
    return jnp.where(first, sink_ref[0, 2 * m], sink_ref[0, 2 * m + 1])


def _softmax_t(logits, sink):
    mx =jnp.maximum(jnp.max(logits, axis=0, keepdims=True), sink)
    p = jnp.exp(logits - mx)
    sink_p = jnp.exp(sink - mx)
    inv = 1.0 / (jnp.sum(p, axis=0, keepdims=True) + sink_p)
    return p * inv, sink_p * inv


def _layer_b_fwd(h1, target, kvn, bpre, wkv, wbin_g, biasm, sinks, wbout, bpost):
    seq, d = h1.shape
    kvw = wkv.shape[1]
    cw = wbin_g.shape[2]
    aw = N_Q_HEADS * HEAD_DIM
    per = aw // cw
    tile = Q_BLOCKS * BLOCK

    def body(sink_ref, h1_ref, tgt_ref, kvn_ref, bpre_ref, wkv_ref, wbin_ref, bias_ref, w_ref, g_ref,
             n3_ref, n4_ref, kvc_ref, q_ref, o_ref, dh2_ref, dyb_ref, dattn_ref, dz2_ref, acc_ref,
             attn_ref, z2_ref, kvp_ref):
        i = pl.program_id(0)

        @pl.when(i == 0)
        def _():
            acc_ref[...] = jnp.zeros_like(acc_ref)
            kvp_ref[...] = jnp.zeros_like(kvp_ref)

        hn, _ = _rms(h1_ref[...])
        n3 = (hn * kvn_ref[...]).astype(BF16)
        n4 = (hn * bpre_ref[...]).astype(BF16)
        n3_ref[...] = n3
        n4_ref[...] = n4
        kvc_ref[...] = _dot(n3, wkv_ref[...]).astype(BF16)
        for j in range(N_DEV):
            pj = _dot(n4, wbin_ref[j])
            if j < per:
                q_ref[:, j * cw:(j + 1) * cw] = pj.astype(BF16)
            else:
                z2_ref[:, (j - per) * cw:(j - per + 1) * cw] = pj

        banded = _banded_tiles(kvp_ref, kvc_ref)
        kvp_ref[...] = kvc_ref[tile - BLOCK:tile, :]
        units = [(u, m) for u in range(Q_BLOCKS) for m in range(N_PAIRS)]
        kv_of = lambda m: (2 * m) // GROUP
        logits, probs = {}, {}
        lag_b, lag_c = ATTN_FWD_LAGS
        for step in range(len(units) + lag_c):
            if step < len(units):
                u, m = units[step]
                qpair = _pair_rows(q_ref, slice(u * BLOCK, (u + 1) * BLOCK), m, SCALE)
                logits[step] = _dot_nt(banded[u][0][kv_of(m)], qpair) + _bias_of(bias_ref, i, u, m)
            if 0 <= step - lag_b < len(units):
                u, m = units[step - lag_b]
                probs[step - lag_b] = _softmax_t(logits.pop(step - lag_b), _sink_row(sink_ref, m))[0].astype(BF16)
            if 0 <= step - lag_c < len(units):
                u, m = units[step - lag_c]
                out_t = _dot(banded[u][3][kv_of(m)], probs.pop(step - lag_c))
                attn_ref[u * BLOCK:(u + 1) * BLOCK, m * LANES:(m + 1) * LANES] = _pair_cols(out_t)
        attn = attn_ref[...]
        sz, dsz = _silu(z2_ref[...])
        o = (attn * sz).astype(BF16)
        o_ref[...] = o

        w = w_ref[...]
        yb = _dot(o, w)
        ybn, r = _rms(yb)
        g = g_ref[...]
        diff = h1_ref[...] + ybn * g - tgt_ref[...]
        dh2 = diff * (1.0 / d)
        dh2_ref[...] = dh2
        acc_ref[0:1, :] += jnp.sum(dh2 * ybn, axis=0, keepdims=True)
        tok = jnp.mean(diff * diff, axis=-1, keepdims=True)
        acc_ref[1:2, :] += 0.5 * jnp.sum(tok, axis=0, keepdims=True)
        dyb = _rms_bwd(dh2 * g, ybn, r).astype(BF16)
        dyb_ref[...] = dyb
        do = _dot_nt(dyb, w)
        dattn_ref[...] = (do * sz).astype(BF16)
        dz2_ref[...] = (do * attn * dsz).astype(BF16)

    blk = lambda w: pl.BlockSpec((tile, w), lambda i: (i, 0))
    return pl.pallas_call(
        body,
        name="layer_b_fwd",
        grid=(seq // tile,),
        in_specs=[
            pl.BlockSpec(memory_space=pltpu.SMEM),
            blk(d),
            blk(d),
            _full(kvn.shape),
            _full(bpre.shape),
            _full(wkv.shape),
            _full(wbin_g.shape),
            _full(biasm.shape),
            _full(wbout.shape),
            _full(bpost.shape),
        ],
        out_specs=[blk(d), blk(d), blk(kvw), blk(aw), blk(aw), blk(d), blk(d), blk(aw), blk(aw), _resident((8, d))],
        out_shape=[
            jax.ShapeDtypeStruct((seq, d), BF16),
            jax.ShapeDtypeStruct((seq, d), BF16),
            jax.ShapeDtypeStruct((seq, kvw), BF16),
            jax.ShapeDtypeStruct((seq, aw), BF16),
            jax.ShapeDtypeStruct((seq, aw), BF16),
            jax.ShapeDtypeStruct((seq, d), F32),
            jax.ShapeDtypeStruct((seq, d), BF16),
            jax.ShapeDtypeStruct((seq, aw), BF16),
            jax.ShapeDtypeStruct((seq, aw), BF16),
            jax.ShapeDtypeStruct((8, d), F32),
        ],
        scratch_shapes=[pltpu.VMEM((tile, aw), F32), pltpu.VMEM((tile, aw), F32), pltpu.VMEM((BLOCK, kvw), BF16)],
        compiler_params=_params(("arbitrary",), 56),
    )(sinks, h1, target, kvn, bpre, wkv, wbin_g, biasm, wbout, bpost)


def _attn_bwd(q, kv, dattn, biasm, sinks, ready):
    seq, aw = q.shape
    kvw = kv.shape[1]
    kw = N_KV_HEADS * HEAD_DIM
    nb = seq // BLOCK
    pairs_per_kv = N_PAIRS // N_KV_HEADS
    nr = len(ready)

    tile = Q_BLOCKS * BLOCK
    nsteps = seq // tile
    held = (Q_BLOCKS - 1) * BLOCK

    def body(sink_ref, q_ref, kvc_ref, kvp_ref, da_ref, bias_ref, *refs):
        ready_refs, (dq_ref, dkv_ref, dssum_ref, dsink_ref) = refs[:nr], refs[nr:nr + 4]
        landed_refs, scratch = refs[nr + 4:2 * nr + 4], refs[2 * nr + 4:]
        carry_ref, done_ref, qs_ref, dos_ref, dst_ref, pt_ref, *sems = scratch
        i = pl.program_id(0)

        @pl.when(i == 0)
        def _():
            dssum_ref[...] = jnp.zeros_like(dssum_ref)
            dsink_ref[...] = jnp.zeros_like(dsink_ref)
            carry_ref[...] = jnp.zeros_like(carry_ref)
            done_ref[...] = jnp.zeros_like(done_ref)
            if nr:
                _exchange_start(ready_refs, landed_refs, *sems, True)

        if nr:
            @pl.when(i == nsteps)
            def _():
                _exchange_wait(ready_refs, landed_refs, *sems, True)

        @pl.when(i < nsteps)
        def _():
            lo = lax.broadcasted_iota(jnp.int32, (BAND, LANES), 1) < HEAD_DIM
            head_lane = lax.broadcasted_iota(jnp.int32, (1, LANES), 1)
            banded = _banded_tiles(kvp_ref, kvc_ref)
            units = [(u, m) for u in range(Q_BLOCKS) for m in range(N_PAIRS)]
            dsink = jnp.zeros((1, LANES), F32)
            folded = {}
            logits, dps = {}, {}
            lag_b, lag_c = ATTN_BWD_LAGS
            for step in range(len(units) + lag_c):
                if step < len(units):
                    u, m = units[step]
                    kh, rows = m // pairs_per_kv, slice((m % pairs_per_kv) * BAND, (m % pairs_per_kv + 1) * BAND)
                    qrows = slice(u * BLOCK, (u + 1) * BLOCK)
                    qpair = _pair_rows(q_ref, qrows, m, SCALE)
                    dopair = _pair_rows(da_ref, qrows, m)
                    qs_ref[u, kh, rows, :] = qpair
                    dos_ref[u, kh, rows, :] = dopair
                    logits[step] = _dot_nt(banded[u][0][kh], qpair) + _bias_of(bias_ref, i, u, m)
                    dps[step] = _dot_nt(banded[u][2][kh], dopair)
                if 0 <= step - lag_b < len(units):
                    u, m = units[step - lag_b]
                    kh, rows = m // pairs_per_kv, slice((m % pairs_per_kv) * BAND, (m % pairs_per_kv + 1) * BAND)
                    both_l, both_dp = logits.pop(step - lag_b), dps.pop(step - lag_b)
                    for e in range(2):
                        half = slice(e * BLOCK, (e + 1) * BLOCK)
                        cols = slice(rows.start + e * BLOCK, rows.start + (e + 1) * BLOCK)
                        pn, sink_p = _softmax_t(both_l[:, half], sink_ref[0, 2 * m + e])
                        dp = both_dp[:, half]
                        delta = jnp.sum(pn * dp, axis=0, keepdims=True)
                        ds = pn * (dp - delta)
                        dssum_ref[m, :, half] += ds
                        total = jnp.sum(sink_p * delta, axis=1, keepdims=True)
                        dsink = dsink - jnp.where(head_lane == 2 * m + e, total, 0.0)
                        dst_ref[u, kh, :, cols] = ds.astype(BF16)
                        pt_ref[u, kh, :, cols] = pn.astype(BF16)
                if 0 <= step - lag_c < len(units):
                    u, m = units[step - lag_c]
                    kh, rows = m // pairs_per_kv, slice((m % pairs_per_kv) * BAND, (m % pairs_per_kv + 1) * BAND)
                    dq_t = _dot(banded[u][1][kh], dst_ref[u, kh, :, rows])
                    dq_ref[u * BLOCK:(u + 1) * BLOCK, m * LANES:(m + 1) * LANES] = (_pair_cols(dq_t) * SCALE).astype(BF16)
                    if m % pairs_per_kv == pairs_per_kv - 1:
                        for name, lhs_ref, rhs_ref in (("k", dst_ref, qs_ref), ("v", pt_ref, dos_ref)):
                            acc = _dot(lhs_ref[u, kh], rhs_ref[u, kh])
                            folded[u, kh, name] = acc + pltpu.roll(acc, HEAD_DIM, 1)
            dsink_ref[0:1, :] += dsink
            dkv = [jnp.concatenate([jnp.where(lo, folded[u, 0, n], folded[u, 1, n]) for n in ("k", "v")], axis=1)
                   for u in range(Q_BLOCKS)]

            @pl.when(i > 0)
            def _():
                if held:
                    dkv_ref[:held, :] = done_ref[...].astype(BF16)
                dkv_ref[held:, :] = (carry_ref[...] + dkv[0][:BLOCK]).astype(BF16)

            for u in range(Q_BLOCKS - 1):
                done_ref[u * BLOCK:(u + 1) * BLOCK, :] = dkv[u][BLOCK:] + dkv[u + 1][:BLOCK]
            carry_ref[...] = dkv[Q_BLOCKS - 1][BLOCK:]

        @pl.when(i == nsteps)
        def _():
            if held:
                dkv_ref[:held, :] = done_ref[...].astype(BF16)
            dkv_ref[held:, :] = carry_ref[...].astype(BF16)

    last = nsteps - 1
    blk = lambda w: pl.BlockSpec((tile, w), lambda i: (jnp.minimum(i, last), 0))
    outs = pl.pallas_call(
        body,
        name="attn_bwd",
        grid=(nsteps + 1,),
        in_specs=[
            pl.BlockSpec(memory_space=pltpu.SMEM),
            blk(aw),
            blk(kvw),
            pl.BlockSpec((BLOCK, kvw), lambda i: (jnp.clip(Q_BLOCKS * i - 1, 0, nb - 1), 0)),
            blk(aw),
            _full(biasm.shape),
        ] + [HBM_SPEC] * nr,
        out_specs=[
            blk(aw),
            pl.BlockSpec((tile, kvw), lambda i: (jnp.maximum(i - 1, 0), 0)),
            _resident(biasm.shape[1:]),
            _resident((8, LANES)),
        ] + [HBM_SPEC] * nr,
        out_shape=[
            jax.ShapeDtypeStruct((seq, aw), BF16),
            jax.ShapeDtypeStruct((seq, kvw), BF16),
            jax.ShapeDtypeStruct(biasm.shape[1:], F32),
            jax.ShapeDtypeStruct((8, LANES), F32),
        ] + [jax.ShapeDtypeStruct(g.shape, g.dtype) for g in ready],
        scratch_shapes=[
            pltpu.VMEM((BLOCK, kvw), F32),
            pltpu.VMEM((max(held, 8), kvw), F32),
            pltpu.VMEM((Q_BLOCKS, N_KV_HEADS, pairs_per_kv * BAND, LANES), BF16),
            pltpu.VMEM((Q_BLOCKS, N_KV_HEADS, pairs_per_kv * BAND, LANES), BF16),
            pltpu.VMEM((Q_BLOCKS, N_KV_HEADS, BAND, pairs_per_kv * BAND), BF16),
            pltpu.VMEM((Q_BLOCKS, N_KV_HEADS, BAND, pairs_per_kv * BAND), BF16),
        ] + _exchange_sems(nr),
        compiler_params=_params(("arbitrary",), 48),
    )(sinks, q, kv, kv, dattn, biasm, *ready)
    return outs[:4], outs[4:]


def _relbias_grad(dssum2, bucket_row, chunk):
    heads, n = dssum2.shape

    def body(a_ref, bucket_ref, out_ref):
        @pl.when(pl.program_id(0) == 0)
        def _():
            out_ref[...] = jnp.zeros_like(out_ref)

        a = a_ref[...]
        hi = a.astype(BF16)
        lo = (a - hi.astype(F32)).astype(BF16)
        onehot_t = (lax.broadcasted_iota(jnp.int32, (LANES, chunk), 0) == bucket_ref[...]).astype(F32).astype(BF16)
        out_ref[...] += _dot_nt(hi, onehot_t) + _dot_nt(lo, onehot_t)

    return pl.pallas_call(
        body,
        name="relbias_grad",
        grid=(n // chunk,),
        in_specs=[pl.BlockSpec((heads, chunk), lambda i: (0, i)), pl.BlockSpec((1, chunk), lambda i: (0, i))],
        out_specs=_resident((heads, LANES)),
        out_shape=jax.ShapeDtypeStruct((heads, LANES), F32),
        compiler_params=_params(("arbitrary",), 32),
    )(dssum2, bucket_row)


def _layer_b_in_bwd(dh2, dq, dz2, dkv, h1, ya, wbin_g, wkv, kvn, bpre, sm, ready, ts):
    seq, d = h1.shape
    aw = dq.shape[1]
    kvw = dkv.shape[1]
    cw = wbin_g.shape[2]
    per = aw // cw

    nr = len(ready)
    nt = seq // ts

    def body(dh2_ref, dq_ref, dz2_ref, dkv_ref, h1_ref, ya_ref, wbin_ref, wkv_ref, kvn_ref, bpre_ref, sm_ref, *refs):
        ready_refs, (dh1_ref, dya_ref, acc_ref) = refs[:nr], refs[nr:nr + 3]
        landed_refs, sems = refs[nr + 3:2 * nr + 3], refs[2 * nr + 3:]

        @pl.when(pl.program_id(0) == 0)
        def _():
            acc_ref[...] = jnp.zeros_like(acc_ref)
            _exchange_start(ready_refs, landed_refs, *sems, True)

        @pl.when(pl.program_id(0) == nt - 1)
        def _():
            _exchange_wait(ready_refs, landed_refs, *sems, True)

        dn4 = jnp.zeros((ts, d), F32)
        for j in range(N_DEV):
            src = dq_ref if j < per else dz2_ref
            jj = j % per
            dn4 = dn4 + _dot_nt(src[:, jj * cw:(jj + 1) * cw], wbin_ref[j])
        dn3 = _dot_nt(dkv_ref[...], wkv_ref[...])
        hn, r = _rms(h1_ref[...])
        acc_ref[0:1, :] += jnp.sum(dn4 * hn, axis=0, keepdims=True)
        acc_ref[1:2, :] += jnp.sum(dn3 * hn, axis=0, keepdims=True)
        dh1 = dh2_ref[...] + _rms_bwd(dn4 * bpre_ref[...] + dn3 * kvn_ref[...], hn, r)
        dh1_ref[...] = dh1
        yan, r2 = _rms(ya_ref[...])
        acc_ref[2:3, :] += jnp.sum(dh1 * yan, axis=0, keepdims=True)
        dya_ref[...] = _rms_bwd(dh1 * sm_ref[4:5, :], yan, r2).astype(BF16)

    outs = pl.pallas_call(
        body,
        name="layer_b_in_bwd",
        grid=(nt,),
        in_specs=[_rows(ts, d), _rows(ts, aw), _rows(ts, aw), _rows(ts, kvw), _rows(ts, d), _rows(ts, d),
                  _full(wbin_g.shape), _full(wkv.shape), _full(kvn.shape), _full(bpre.shape), _full(sm.shape)]
        + [HBM_SPEC] * nr,
        out_specs=[_rows(ts, d), _rows(ts, d), _resident((8, d))] + [HBM_SPEC] * nr,
        out_shape=[jax.ShapeDtypeStruct((seq, d), F32), jax.ShapeDtypeStruct((seq, d), BF16),
                   jax.ShapeDtypeStruct((8, d), F32)] + [jax.ShapeDtypeStruct(g.shape, g.dtype) for g in ready],
        scratch_shapes=_exchange_sems(nr),
        compiler_params=_params(("arbitrary",), 48),
    )(dh2, dq, dz2, dkv, h1, ya, wbin_g, wkv, kvn, bpre, sm, *ready)
    return outs[:3], outs[3:]


def _layer_a_bwd(dya, proj, conv, dh1, x2, wout, win_g, sm, ts):
    seq, d = x2.shape
    width = wout.shape[0]
    half = win_g.shape[2]
    n_half = width // half
    nt = seq // ts

    def body(dya_ref, proj_ref, conv_ref, dh1_ref, x_ref, wout_ref, win_ref, sm_ref, dproj_ref, gx_ref, acc_ref,
             dnext_ref):
        @pl.when(pl.program_id(0) == 0)
        def _():
            acc_ref[...] = jnp.zeros_like(acc_ref)
            dnext_ref[...] = jnp.zeros_like(dnext_ref)

        dy = _dot_nt(dya_ref[...], wout_ref[...])
        row = lax.broadcasted_iota(jnp.int32, (ts, half), 0)
        dn1 = jnp.zeros((ts, d), F32)
        for hh in range(n_half):
            cols = slice(hh * half, (hh + 1) * half)
            b, c, u, z = [proj_ref[:, (part * n_half + hh) * half:(part * n_half + hh + 1) * half].astype(F32)
                          for part in range(4)]
            cv = conv_ref[:, cols].astype(F32)
            dyh = dy[:, cols]
            sz, dsz = _silu(z)
            dconv = dyh * b * sz
            grads = [dyh * cv * sz, None, None, dyh * b * cv * dsz]
            next0, next1 = dnext_ref[0:1, cols], dnext_ref[1:2, cols]
            dc1 = jnp.where(row == ts - 1, next0, pltpu.roll(dconv, ts - 1, 0))
            dc2 = jnp.where(row == ts - 1, next1, jnp.where(row == ts - 2, next0, pltpu.roll(dconv, ts - 2, 0)))
            dnext_ref[:, cols] = dconv[0:8, :]
            v = c * u
            acc_ref[1:2, cols] += jnp.sum(dc2 * v, axis=0, keepdims=True)
            acc_ref[2:3, cols] += jnp.sum(dc1 * v, axis=0, keepdims=True)
            acc_ref[3:4, cols] += jnp.sum(dconv * v, axis=0, keepdims=True)
            dv = sm_ref[3:4, cols] * dconv + sm_ref[2:3, cols] * dc1 + sm_ref[1:2, cols] * dc2
            grads[1] = dv * u
            grads[2] = dv * c
            for part in range(4):
                j = part * n_half + hh
                gj = grads[part].astype(BF16)
                dproj_ref[:, j * half:(j + 1) * half] = gj
                dn1 = dn1 + _dot_nt(gj, win_ref[j])
        xn, r = _rms(x_ref[...])
        acc_ref[0:1, :] += jnp.sum(dn1 * xn, axis=0, keepdims=True)
        gx_ref[...] = dh1_ref[...] + _rms_bwd(dn1 * sm_ref[0:1, :], xn, r)

    rev = lambda w: pl.BlockSpec((ts, w), lambda i: (nt - 1 - i, 0))
    return pl.pallas_call(
        body,
        name="layer_a_bwd",
        grid=(nt,),
        in_specs=[rev(d), rev(4 * width), rev(width), rev(d), rev(d), _full(wout.shape), _full(win_g.shape), _full(sm.shape)],
        out_specs=[rev(4 * width), rev(d), _resident((8, d))],
        out_shape=[jax.ShapeDtypeStruct((seq, 4 * width), BF16), jax.ShapeDtypeStruct((seq, d), F32),
                   jax.ShapeDtypeStruct((8, d), F32)],
        scratch_shapes=[pltpu.VMEM((8, width), F32)],
        compiler_params=_params(("arbitrary",), 56),
    )(dya, proj, conv, dh1, x2, wout, win_g, sm)


def _wgrad(a, bs, n_slots, ts, name, ready=(), block_cols=1024):
    nr = len(ready)
    seq, k = a.shape
    nb_in = len(bs)
    n_each = bs[0].shape[1]
    n = nb_in * n_each
    bn = min(n_each, block_cols)
    per_in = n_each // bn
    n_blocks = nb_in * per_in
    ns = seq // ts

    def b_spec(idx):
        def index(j, s):
            mine = j // per_in == idx
            row = jnp.where(mine, s, jnp.where(j // per_in > idx, ns - 1, 0))
            return (row, jnp.where(mine, j % per_in, jnp.where(j // per_in > idx, per_in - 1, 0)))
        return pl.BlockSpec((ts, bn), index)

    if n_slots:
        sw = n // n_slots
        spb = bn // sw
        out_shape = jax.ShapeDtypeStruct((n_slots, k, sw), BF16)
        out_spec = pl.BlockSpec((spb, k, sw), lambda j, s: (j, 0, 0))
    else:
        out_shape = jax.ShapeDtypeStruct((k, n), BF16)
        out_spec = pl.BlockSpec((k, bn), lambda j, s: (0, j))

    def body(a_ref, *refs):
        b_refs, ready_refs, o_ref = refs[:nb_in], refs[nb_in:nb_in + nr], refs[nb_in + nr]
        landed_refs, (acc_ref, *sems) = refs[nb_in + nr + 1:nb_in + 2 * nr + 1], refs[nb_in + 2 * nr + 1:]
        j, s = pl.program_id(0), pl.program_id(1)

        if nr:
            @pl.when(jnp.logical_and(j == 0, s == 0))
            def _():
                _exchange_start(ready_refs, landed_refs, *sems, True)

            @pl.when(jnp.logical_and(j == n_blocks - 1, s == ns - 1))
            def _():
                _exchange_wait(ready_refs, landed_refs, *sems, True)

        @pl.when(s == 0)
        def _():
            acc_ref[...] = jnp.zeros_like(acc_ref)

        for idx in range(nb_in):
            @pl.when(j // per_in == idx)
            def _(idx=idx):
                acc_ref[...] += _dot_tn(a_ref[...], b_refs[idx][...])

        @pl.when(s == ns - 1)
        def _():
            if n_slots:
                for e in range(spb):
                    o_ref[e] = acc_ref[:, e * sw:(e + 1) * sw].astype(BF16)
            else:
                o_ref[...] = acc_ref[...].astype(BF16)

    outs = pl.pallas_call(
        body,
        name=name,
        grid=(n_blocks, ns),
        in_specs=[pl.BlockSpec((ts, k), lambda j, s: (s, 0))] + [b_spec(idx) for idx in range(nb_in)] + [HBM_SPEC] * nr,
        out_specs=[out_spec] + [HBM_SPEC] * nr,
        out_shape=[out_shape] + [jax.ShapeDtypeStruct(g.shape, g.dtype) for g in ready],
        scratch_shapes=[pltpu.VMEM((k, bn), F32)] + (_exchange_sems(nr) if nr else []),
        compiler_params=_params(("arbitrary", "arbitrary"), 48),
    )(a, *bs, *ready)
    return (outs[0], outs[1:]) if nr else outs[0]


def _wgrad_tail(pairs, part, landed, ts):
    n_tasks = len(pairs)
    nl = len(landed)
    seq, k = pairs[0][0].shape
    n = pairs[0][1].shape[1]
    ns = seq // ts
    total = n_tasks * ns
    per = k // N_DEV

    def spec(t, width):
        return pl.BlockSpec((ts, width), lambda j, s: (jnp.where(j == t, s, jnp.where(j > t, ns - 1, 0)), 0))

    def body(*refs):
        ab_refs, part_ref = refs[:2 * n_tasks], refs[2 * n_tasks]
        landed_refs, refs = refs[2 * n_tasks + 1:2 * n_tasks + 1 + nl], refs[2 * n_tasks + 1 + nl:]
        o_ref, red_ref = refs[:2]
        summed_refs, (acc_ref, sib_ref, chip_ref, send_ref, *sems) = refs[2:2 + nl], refs[2 + nl:]
        j, s = pl.program_id(0), pl.program_id(1)
        flat = j * ns + s
        swap, send, forward, finish = _chip_reduce(part_ref, red_ref, sib_ref, chip_ref, send_ref, sems)

        @pl.when(flat == 0)
        def _():
            swap()

        @pl.when(flat == min(1, total - 1))
        def _():
            send()

        @pl.when(flat == min(total // 2 + 1, total - 1))
        def _():
            forward()
            for t in range(nl):
                _sum_slots(landed_refs[t], summed_refs[t])

        @pl.when(s == 0)
        def _():
            acc_ref[...] = jnp.zeros_like(acc_ref)

        for t in range(n_tasks):
            @pl.when(j == t)
            def _(t=t):
                acc_ref[...] += _dot_tn(ab_refs[2 * t][...], ab_refs[2 * t + 1][...])

        @pl.when(s == ns - 1)
        def _():
            for dev in range(N_DEV):
                o_ref[dev] = acc_ref[dev * per:(dev + 1) * per, :].astype(BF16)

        @pl.when(flat == total - 1)
        def _():
            finish()

    slot = part.shape[1:]
    outs = pl.pallas_call(
        body,
        name="wgrad_tail",
        grid=(n_tasks, ns),
        in_specs=[spec(t, w) for t in range(n_tasks) for w in (k, n)] + [_full(part.shape)]
        + [_full(g.shape) for g in landed],
        out_specs=[pl.BlockSpec((N_DEV, per, n), lambda j, s: (0, j, 0)), _resident(slot)]
        + [_resident(g.shape[1:]) for g in landed],
        out_shape=[jax.ShapeDtypeStruct((N_DEV, n_tasks * per, n), BF16), jax.ShapeDtypeStruct(slot, F32)]
        + [jax.ShapeDtypeStruct(g.shape[1:], F32) for g in landed],
        scratch_shapes=[pltpu.VMEM((k, n), F32)] + _chip_reduce_scratch(slot),
        compiler_params=_params(("arbitrary", "arbitrary"), 58),
    )(*[op for pair in pairs for op in pair], part, *landed)
    return outs[0], outs[1], outs[2:]


def _adamw(ws, gs, ms, vs):
    n = len(ws)

    def step(w, g, m, v):
        m = ADAM_B1 * m + (1.0 - ADAM_B1) * g
        v = ADAM_B2 * v + (1.0 - ADAM_B2) * jnp.square(g)
        m_hat = m / (1.0 - ADAM_B1 ** ADAM_STEP)
        v_hat = v / (1.0 - ADAM_B2 ** ADAM_STEP)
        return g, -ADAM_LR * (m_hat / (jnp.sqrt(v_hat) + ADAM_EPS) + ADAM_WD * w), m, v

    def body(*refs):
        w_refs, g_refs, m_refs, v_refs = (refs[k * n:(k + 1) * n] for k in range(4))
        go_refs, d_refs, nm_refs, nv_refs = (refs[(4 + k) * n:(5 + k) * n] for k in range(4))
        for t in range(n):
            rows = w_refs[t].shape[0]
            if rows <= 128:
                go_refs[t][...], d_refs[t][...], nm_refs[t][...], nv_refs[t][...] = step(
                    w_refs[t][...], g_refs[t][...], m_refs[t][...], v_refs[t][...])
                continue
            chunk = 128

            def one(i, carry, t=t):
                r = pl.ds(pl.multiple_of(i * chunk, chunk), chunk)
                go_refs[t][r, :], d_refs[t][r, :], nm_refs[t][r, :], nv_refs[t][r, :] = step(
                    w_refs[t][r, :], g_refs[t][r, :], m_refs[t][r, :], v_refs[t][r, :])
                return carry

            lax.fori_loop(0, rows // chunk, one, 0)

    vmem = pl.BlockSpec(memory_space=pltpu.VMEM)
    outs = pl.pallas_call(
        body,
        name="adamw",
        in_specs=[vmem] * (4 * n),
        out_specs=[vmem] * (4 * n),
        out_shape=[jax.ShapeDtypeStruct(w.shape, F32) for w in ws] * 4,
        compiler_params=_params(vmem_mib=56),
    )(*ws, *gs, *ms, *vs)
    return outs[:n], outs[n:2 * n], outs[2 * n:3 * n], outs[3 * n:]


def _band_structure():
    q_loc = jnp.arange(BLOCK, dtype=jnp.int32)[:, None]
    s_loc = jnp.arange(2 * BLOCK, dtype=jnp.int32)[None, :]
    dist = q_loc + BLOCK - s_loc
    in_window = (dist >= 0) & (dist < BLOCK)
    dd = jnp.maximum(dist, 0)
    max_exact = N_BUCKETS // 2
    large = max_exact + (jnp.log(jnp.maximum(dd, 1).astype(F32) / max_exact) / math.log(MAX_DISTANCE / max_exact)
                         * (N_BUCKETS - max_exact)).astype(jnp.int32)
    bucket = jnp.where(dd < max_exact, dd, jnp.minimum(large, N_BUCKETS - 1))
    return bucket, in_window.astype(jnp.int32)


def _place_rows(a, row, rows=8):
    return jnp.pad(a, ((row, rows - row - a.shape[0]), (0, 0)))


def kernel(x, a_pre_norm, a_w_in, a_conv_w, a_w_out, a_post_norm, kv_norm, w_kv, rel_bias, b_pre_norm, b_w_in, b_sinks, b_w_out, b_post_norm, loss_target, m_a_pre_norm, m_a_w_in, m_a_conv_w, m_a_w_out, m_a_post_norm, m_kv_norm, m_w_kv, m_rel_bias, m_b_pre_norm, m_b_w_in, m_b_sinks, m_b_w_out, m_b_post_norm, v_a_pre_norm, v_a_w_in, v_a_conv_w, v_a_w_out, v_a_post_norm, v_kv_norm, v_w_kv, v_rel_bias, v_b_pre_norm, v_b_w_in, v_b_sinks, v_b_w_out, v_b_post_norm):
    seq, d = x.shape[1], x.shape[2]
    x2 = x.reshape(seq, d)
    target = loss_target.reshape(seq, d)
    shard = a_pre_norm.shape[1]
    me = _my_index()
    ts_a = min(seq, 512)
    ts = min(seq, 512)
    ts_w = min(seq, 2048)

    small = _place_rows(a_pre_norm, 0) + _place_rows(a_conv_w[0], 1) + _place_rows(a_post_norm, 4)
    bucket, in_window = _band_structure()
    win_g, wout_g, small_g, biasm = _all_gather(
        [a_w_in[0], a_w_out[0], small], [BF16, BF16, F32], rel_bias.T, bucket.T, in_window.T)
    wout = wout_g.reshape(-1, wout_g.shape[2])
    sm = small_g.transpose(1, 0, 2).reshape(8, N_DEV * shard)
    kvn = kv_norm.reshape(1, d)

    (h1, n1, proj, conv, y, ya), (wkv_g, wbin_g, wbout_g) = _layer_a_fwd(
        x2, sm, win_g, wout, [w_kv.astype(BF16), b_w_in[0].astype(BF16), b_w_out[0].astype(BF16)], ts_a)
    wkv = wkv_g.reshape(-1, wkv_g.shape[2])
    wbout = wbout_g.reshape(-1, wbout_g.shape[2])
    n3, n4, kv, q, o, dh2, dyb, dattn, dz2, acc_c = _layer_b_fwd(
        h1, target, kvn, b_pre_norm, wkv, wbin_g, biasm, b_sinks, wbout, b_post_norm)

    (dq, dkv, dssum, dsink), _ = _attn_bwd(q, kv, dattn, biasm, b_sinks, [])
    by_head = dssum.reshape(N_PAIRS, BAND, 2, BLOCK).transpose(0, 2, 3, 1)
    relb = _relbias_grad(by_head.reshape(N_Q_HEADS, -1), bucket.reshape(1, -1), 4096)
    g_wkv = _wgrad(n3, [dkv], 0, ts_w, "wgrad_kv").reshape(wkv_g.shape)
    g_wbin = _wgrad(n4, [dq, dz2], N_DEV, ts_w, "wgrad_b_in")
    (dh1, dya, acc_b), (l_wkv, l_wbin) = _layer_b_in_bwd(
        dh2, dq, dz2, dkv, h1, ya, wbin_g, wkv, kvn, b_pre_norm, sm, [g_wkv, g_wbin], ts)
    dproj, gx, acc_a = _layer_a_bwd(dya, proj, conv, dh1, x2, wout, win_g, sm, ts_a)
    g_win = _wgrad(n1, [dproj], N_DEV, ts_w, "wgrad_a_in", block_cols=2048)
    g_outs, r_win, (r_wkv, r_wbin) = _wgrad_tail([(y, dya), (o, dyb)], g_win, [l_wkv, l_wbin], min(seq, 1024))

    r_outs, _, (s_a, s_b, s_c, s_relb, s_sink) = _reduce_exchange(g_outs, [], [acc_a, acc_b, acc_c, relb, dsink])
    rows_out = wout_g.shape[1]
    r_wout, r_wbout = r_outs[:rows_out], r_outs[rows_out:]
    mine = lambda rows: lax.dynamic_slice_in_dim(rows, me * shard, shard, axis=1)
    loss = s_c[1, 0]
    weights = [a_pre_norm, a_w_in[0], a_conv_w[0], a_w_out[0], a_post_norm, kvn, w_kv, rel_bias.T, b_pre_norm,
               b_w_in[0], b_sinks, b_w_out[0], b_post_norm]
    grads = [mine(s_a[0:1]), r_win, mine(s_a[1:4]), r_wout, mine(s_b[2:3]), s_b[1:2], r_wkv,
             s_relb[:, :N_BUCKETS], s_b[0:1], r_wbin, s_sink[0:1, :N_Q_HEADS], r_wbout, s_c[0:1]]
    first = [m_a_pre_norm, m_a_w_in[0], m_a_conv_w[0], m_a_w_out[0], m_a_post_norm, m_kv_norm.reshape(1, d), m_w_kv,
             m_rel_bias.T, m_b_pre_norm, m_b_w_in[0], m_b_sinks, m_b_w_out[0], m_b_post_norm]
    second = [v_a_pre_norm, v_a_w_in[0], v_a_conv_w[0], v_a_w_out[0], v_a_post_norm, v_kv_norm.reshape(1, d), v_w_kv,
              v_rel_bias.T, v_b_pre_norm, v_b_w_in[0], v_b_sinks, v_b_w_out[0], v_b_post_norm]
    grads, deltas, new_m, new_v = _adamw(weights, grads, first, second)

    shapes = [a_pre_norm.shape, a_w_in.shape, a_conv_w.shape, a_w_out.shape, a_post_norm.shape, kv_norm.shape,
              w_kv.shape, None, b_pre_norm.shape, b_w_in.shape, b_sinks.shape, b_w_out.shape, b_post_norm.shape]
    shaped = lambda arrays: [a.T if s is None else a.reshape(s) for a, s in zip(arrays, shapes)]
    return (loss, gx.reshape(x.shape), *shaped(grads), *shaped(deltas), *shaped(new_m), *shaped(new_v))
```

```python
import math

import jax
import jax.numpy as jnp
from jax import lax
from jax.experimental import pallas as pl
from jax.experimental.pallas import tpu as pltpu

HEAD_DIM = 64
N_Q_HEADS = 16
N_KV_HEADS = 2
GROUP = N_Q_HEADS // N_KV_HEADS
BLOCK = 128
N_BUCKETS = 32
MAX_DISTANCE = 128
EPS = 1e-6
NEG_INF = -1e30
SCALE = HEAD_DIM ** -0.5

ADAM_LR = 0.001
ADAM_B1 = 0.9
ADAM_B2 = 0.999
ADAM_EPS = 1e-08
ADAM_WD = 0.01
ADAM_STEP = 10

N_PAIRS = N_Q_HEADS // 2
BAND = 2 * BLOCK

N_DEV = 8
GATHER_PIECE_ROWS = 256
LANES = 128
F32 = jnp.float32
BF16 = jnp.bfloat16
MESH = pl.DeviceIdType.MESH
MIB = 1024 * 1024


def _params(semantics=None, vmem_mib=48):
    return pltpu.CompilerParams(dimension_semantics=semantics, vmem_limit_bytes=vmem_mib * MIB)


def _full(shape):
    zeros = (0,) * len(shape)
    return pl.BlockSpec(shape, lambda *_: zeros, pipeline_mode=pl.Buffered(1))


def _resident(shape):
    zeros = (0,) * len(shape)
    return pl.BlockSpec(shape, lambda *_: zeros)


def _rows(ts, cols):
    return pl.BlockSpec((ts, cols), lambda i: (i, 0))


def _dot(a, b):
    return jnp.dot(a, b, preferred_element_type=F32)


def _dot_nt(a, b):
    return lax.dot_general(a, b, (((1,), (1,)), ((), ())), preferred_element_type=F32)


def _dot_tn(a, b):
    return lax.dot_general(a, b, (((0,), (0,)), ((), ())), preferred_element_type=F32)


def _rms(xf):
    r = lax.rsqrt(jnp.mean(xf * xf, axis=-1, keepdims=True) + EPS)
    return xf * r, r


def _rms_bwd(dn, xn, r):
    return r * (dn - xn * jnp.mean(dn * xn, axis=-1, keepdims=True))


def _silu(z):
    s = jax.nn.sigmoid(z)
    return z * s, s * (1.0 + z * (1.0 - s))


def _my_index():
    return 4 * lax.axis_index("x") + 2 * lax.axis_index("y") + lax.axis_index("c")


def _bias_table(rb_ref, bucket_ref, win_ref, out_ref):
    bk = jnp.where(win_ref[...] != 0, bucket_ref[...], -1)
    has_prev = lax.broadcasted_iota(jnp.int32, bk.shape, 0) >= BLOCK
    for h in range(N_Q_HEADS):
        acc = jnp.full(bk.shape, NEG_INF, F32)
        for b in range(N_BUCKETS):
            acc = jnp.where(bk == b, rb_ref[h, b], acc)
        cols = slice((h % 2) * BLOCK, (h % 2 + 1) * BLOCK)
        out_ref[1, h // 2, :, cols] = acc
        out_ref[0, h // 2, :, cols] = jnp.where(has_prev, acc, NEG_INF)


def _all_gather(shards, out_dtypes, rel_bias_t, bucket_t, in_window_t):
    n = len(shards)
    pieces = [(t, r0, min(GATHER_PIECE_ROWS, s.shape[0] - r0))
              for t, s in enumerate(shards) for r0 in range(0, s.shape[0], GATHER_PIECE_ROWS)]

    def body(*refs):
        ins, (rb_ref, bucket_ref, win_ref) = refs[:n], refs[n:n + 3]
        outs, bias_ref = refs[n + 3:2 * n + 3], refs[2 * n + 3]
        send_sems, recv_sems = refs[2 * n + 4], refs[2 * n + 5]
        x, y, c = lax.axis_index("x"), lax.axis_index("y"), lax.axis_index("c")
        me, sibling = (x, y, c), (x, y, 1 - c)
        x_nbr, y_nbr, diagonal = (1 - x, y), (x, 1 - y), (1 - x, 1 - y)
        south = c == 0
        relayed = (jnp.where(south, 1 - x, x), jnp.where(south, y, 1 - y))
        relay_to = (jnp.where(south, x, 1 - x), jnp.where(south, 1 - y, y))

        def copy(u, k, block, to):
            t, r0, nrows = pieces[u]
            rows = outs[t].at[4 * block[0] + 2 * block[1] + block[2], pl.ds(r0, nrows)]
            return pltpu.make_async_remote_copy(
                src_ref=rows, dst_ref=rows, send_sem=send_sems.at[u, k], recv_sem=recv_sems.at[u, k],
                device_id=to, device_id_type=MESH)

        for t in range(n):
            outs[t][pl.ds(_my_index(), 1)] = ins[t][...].astype(outs[t].dtype)[None]
        started = []

        def start(cp):
            cp.start()
            started.append(cp)

        units = range(len(pieces))
        for u in units:
            start(copy(u, 0, me, sibling))
            start(copy(u, 1, me, (*x_nbr, c)))
            start(copy(u, 2, me, (*y_nbr, c)))
        _bias_table(rb_ref, bucket_ref, win_ref, bias_ref)
        for u in units:
            for k, chip in ((1, x_nbr), (2, y_nbr)):
                copy(u, k, (*chip, c), me).wait_recv()
                start(copy(u, 3 + k, (*chip, c), sibling))
            start(copy(u, 3, (*relayed, c), (*relay_to, c)))
        for u in units:
            copy(u, 3, (*diagonal, c), me).wait_recv()
            start(copy(u, 6, (*diagonal, c), sibling))
        for u in units:
            copy(u, 0, sibling, me).wait_recv()
        for k, chip in ((4, x_nbr), (5, y_nbr), (6, diagonal)):
            for u in units:
                copy(u, k, (*chip, 1 - c), me).wait_recv()
        for cp in started:
            cp.wait_send()

    vmem = pl.BlockSpec(memory_space=pltpu.VMEM)
    return pl.pallas_call(
        body,
        name="gather_weights",
        out_shape=[jax.ShapeDtypeStruct((N_DEV,) + s.shape, dt) for s, dt in zip(shards, out_dtypes)]
        + [jax.ShapeDtypeStruct((2, N_PAIRS, BAND, 2 * BLOCK), F32)],
        in_specs=[vmem] * n + [pl.BlockSpec(memory_space=pltpu.SMEM), vmem, vmem],
        out_specs=[vmem] * (n + 1),
        scratch_shapes=[pltpu.SemaphoreType.DMA((len(pieces), 7)), pltpu.SemaphoreType.DMA((len(pieces), 7))],
        compiler_params=_params(vmem_mib=48),
    )(*shards, rel_bias_t, bucket_t, in_window_t)


def _peer(k):
    x, y, c = lax.axis_index("x"), lax.axis_index("y"), lax.axis_index("c")
    px = 1 - x if k & 4 else x
    py = 1 - y if k & 2 else y
    pc = 1 - c if k & 1 else c
    return (px, py, pc), 4 * px + 2 * py + pc


def _exchange(srcs, dsts, send_sems, recv_sems, local_sems, scatter):
    me = _my_index()
    sends, arrivals = [], []
    for k in range(1, N_DEV):
        peer, pidx = _peer(k)
        for t, (src, dst) in enumerate(zip(srcs, dsts)):
            mine = src.at[pidx] if scatter else src
            sems = dict(send_sem=send_sems.at[t, k - 1], recv_sem=recv_sems.at[t, k - 1], device_id=peer, device_id_type=MESH)
            sends.append(pltpu.make_async_remote_copy(src_ref=mine, dst_ref=dst.at[me], **sems))
            arrivals.append(pltpu.make_async_remote_copy(src_ref=mine, dst_ref=dst.at[pidx], **sems))
    local = [pltpu.make_async_copy(src.at[me] if scatter else src, dst.at[me], local_sems.at[t])
             for t, (src, dst) in enumerate(zip(srcs, dsts))]
    return sends, arrivals, local


def _exchange_start(*args):
    sends, _, local = _exchange(*args)
    for cp in sends + local:
        cp.start()


def _exchange_wait(*args):
    sends, arrivals, local = _exchange(*args)
    for cp in arrivals:
        cp.wait_recv()
    for cp in sends:
        cp.wait_send()
    for cp in local:
        cp.wait()


def _exchange_sems(n):
    if not n:
        return []
    return [pltpu.SemaphoreType.DMA((n, N_DEV - 1)), pltpu.SemaphoreType.DMA((n, N_DEV - 1)), pltpu.SemaphoreType.DMA((n,))]


HBM_SPEC = pl.BlockSpec(memory_space=pl.ANY)


def _sum_slots(recv_ref, out_ref):
    rows = out_ref.shape[0]
    chunk = min(rows, 128)

    def add(i, carry):
        r0 = pl.multiple_of(i * chunk, chunk)
        acc = recv_ref[0, pl.ds(r0, chunk), :].astype(F32)
        for dev in range(1, N_DEV):
            acc = acc + recv_ref[dev, pl.ds(r0, chunk), :].astype(F32)
        out_ref[pl.ds(r0, chunk), :] = acc
        return carry

    lax.fori_loop(0, rows // chunk, add, 0)


N_CHIPS = N_DEV // 2


def _rows_loop(rows, fn):
    chunk = min(rows, 128)

    def step(i, carry):
        fn(pl.ds(pl.multiple_of(i * chunk, chunk), chunk))
        return carry

    lax.fori_loop(0, rows // chunk, step, 0)


def _chip_reduce(g_ref, out_ref, sib_ref, land_ref, send_ref, sems):
    sib_send, sib_recv, ici_send, ici_recv = sems
    x, y, c = lax.axis_index("x"), lax.axis_index("y"), lax.axis_index("c")
    south = c == 0
    near =(jnp.where(south, 1 - x, x), jnp.where(south, y, 1 - y))
    far = (jnp.where(south, x, 1 - x), jnp.where(south, 1 - y, y))
    diagonal = (1 - x, 1 - y)
    rows = out_ref.shape[0]
    direct, fold, folded = 0, 1, 2

    def to_sibling(t):
        return pltpu.make_async_remote_copy(
            src_ref=g_ref.at[2 * t + 1 - c], dst_ref=sib_ref.at[t], send_sem=sib_send.at[t], recv_sem=sib_recv.at[t],
            device_id=(x, y, 1 - c), device_id_type=MESH)

    def ici(role, chip):
        return pltpu.make_async_remote_copy(
            src_ref=send_ref.at[role], dst_ref=land_ref.at[role], send_sem=ici_send.at[role],
            recv_sem=ici_recv.at[role], device_id=(*chip, c), device_id_type=MESH)

    def pair_sum(chip, r):
        t = 2 * chip[0] + chip[1]
        return g_ref[2 * t + c, r, :].astype(F32) + sib_ref[t, r, :].astype(F32)

    def swap():
        for t in range(N_CHIPS):
            to_sibling(t).start()

    def send():
        for t in range(N_CHIPS):
            to_sibling(t).wait_recv()
        for role, chip in ((fold, diagonal), (direct, near)):
            def fill(r, role=role, chip=chip):
                send_ref[role, r, :] = pair_sum(chip, r).astype(BF16)

            _rows_loop(rows, fill)
            ici(role, near).start()

    def forward():
        ici(fold, near).wait_recv()

        def fill(r):
            send_ref[folded, r, :] = (pair_sum(far, r) + land_ref[fold, r, :].astype(F32)).astype(BF16)

        _rows_loop(rows, fill)
        ici(folded, far).start()

    def finish():
        ici(direct, near).wait_recv()
        ici(folded, far).wait_recv()

        def total(r):
            mine = pair_sum((x, y), r)
            out_ref[r, :] = mine + land_ref[direct, r, :].astype(F32) + land_ref[folded, r, :].astype(F32)

        _rows_loop(rows, total)
        for t in range(N_CHIPS):
            to_sibling(t).wait_send()
        for role, chip in ((direct, near), (fold, near), (folded, far)):
            ici(role, chip).wait_send()

    return swap, send, forward, finish


def _chip_reduce_scratch(slot):
    return [pltpu.VMEM((N_CHIPS,) + slot, BF16), pltpu.VMEM((3,) + slot, BF16), pltpu.VMEM((3,) + slot, BF16),
            pltpu.SemaphoreType.DMA((N_CHIPS,)), pltpu.SemaphoreType.DMA((N_CHIPS,)),
            pltpu.SemaphoreType.DMA((3,)), pltpu.SemaphoreType.DMA((3,))]


def _reduce_exchange(part, landed, smalls):
    nl, ng = len(landed), len(smalls)
    n_out = 1 + nl + ng

    def body(*refs):
        p_in, l_in, s_in = refs[0], refs[1:1 + nl], refs[1 + nl:n_out]
        p_out, l_out, s_out = refs[n_out], refs[n_out + 1:n_out + 1 + nl], refs[n_out + 1 + nl:2 * n_out]
        scratch = refs[2 * n_out:]
        s_recv, (sib_ref, chip_ref, send_ref), sems = scratch[:ng], scratch[ng:ng + 3], scratch[ng + 3:]
        swap, send, forward, finish = _chip_reduce(p_in, p_out, sib_ref, chip_ref, send_ref, sems[:4])
        swap()
        _exchange_start(s_in, s_recv, *sems[4:], False)
        send()
        for t in range(nl):
            _sum_slots(l_in[t], l_out[t])
        forward()
        finish()
        _exchange_wait(s_in, s_recv, *sems[4:], False)
        for t in range(ng):
            acc = s_recv[t][0]
            for dev in range(1, N_DEV):
                acc = acc + s_recv[t][dev]
            s_out[t][...] = acc

    vmem = pl.BlockSpec(memory_space=pltpu.VMEM)
    slot = part.shape[1:]
    outs = pl.pallas_call(
        body,
        name="reduce_grads",
        out_shape=[jax.ShapeDtypeStruct(p.shape[1:], F32) for p in [part] + landed]
        + [jax.ShapeDtypeStruct(s.shape, F32) for s in smalls],
        in_specs=[vmem] * n_out,
        out_specs=[vmem] * n_out,
        scratch_shapes=[pltpu.VMEM((N_DEV,) + s.shape, F32) for s in smalls] + _chip_reduce_scratch(slot)
        + _exchange_sems(ng),
        compiler_params=_params(vmem_mib=56),
    )(part, *landed, *smalls)
    return outs[0], outs[1:1 + nl], outs[1 + nl:]


def _layer_a_fwd(x2, sm, win_g, wout, later, ts):
    seq, d = x2.shape
    width = wout.shape[0]
    half = win_g.shape[2]
    n_half = width // half
    nl = len(later)
    nt = seq // ts

    def body(x_ref, sm_ref, win_ref, wout_ref, *refs):
        shard_refs, refs = refs[:nl], refs[nl:]
        h1_ref, n1_ref, proj_ref, conv_ref, y_ref, ya_ref = refs[:6]
        gathered_refs, (vprev_ref, *sems) = refs[6:6 + nl], refs[6 + nl:]

        @pl.when(pl.program_id(0) == 0)
        def _():
            vprev_ref[...] = jnp.zeros_like(vprev_ref)
            _exchange_start(shard_refs, gathered_refs, *sems, False)

        @pl.when(pl.program_id(0) == nt - 1)
        def _():
            _exchange_wait(shard_refs, gathered_refs, *sems, False)

        xf = x_ref[...]
        xn, _ = _rms(xf)
        n1 = (xn * sm_ref[0:1, :]).astype(BF16)
        n1_ref[...] = n1
        row = lax.broadcasted_iota(jnp.int32, (ts, half), 0)
        ya = jnp.zeros((ts, d), F32)
        for hh in range(n_half):
            cols = slice(hh * half, (hh + 1) * half)
            parts = []
            for part in range(4):
                j = part * n_half + hh
                pj = _dot(n1, win_ref[j])
                proj_ref[:, j * half:(j + 1) * half] = pj.astype(BF16)
                parts.append(pj)
            b, c, u, z = parts
            v = c * u
            last1, last2 = vprev_ref[7:8, cols], vprev_ref[6:7, cols]
            v1 = jnp.where(row == 0, last1, pltpu.roll(v, 1, 0))
            v2 = jnp.where(row == 0, last2, jnp.where(row == 1, last1, pltpu.roll(v, 2, 0)))
            vprev_ref[:, cols] = v[ts - 8:ts, :]
            conv = sm_ref[1:2, cols] * v2 + sm_ref[2:3, cols] * v1 + sm_ref[3:4, cols] * v
            conv_ref[:, cols] = conv.astype(BF16)
            yh = (b * conv * _silu(z)[0]).astype(BF16)
            y_ref[:, cols] = yh
            ya = ya + _dot(yh, wout_ref[cols, :])
        ya_ref[...] = ya
        h1_ref[...] = xf + _rms(ya)[0] * sm_ref[4:5, :]

    outs = pl.pallas_call(
        body,
        name="layer_a_fwd",
        grid=(nt,),
        in_specs=[_rows(ts, d), _full(sm.shape), _full(win_g.shape), _full(wout.shape)] + [HBM_SPEC] * nl,
        out_specs=[_rows(ts, d), _rows(ts, d), _rows(ts, 4 * width), _rows(ts, width), _rows(ts, width), _rows(ts, d)]
        + [HBM_SPEC] * nl,
        out_shape=[
            jax.ShapeDtypeStruct((seq, d), F32),
            jax.ShapeDtypeStruct((seq, d), BF16),
            jax.ShapeDtypeStruct((seq, 4 * width), BF16),
            jax.ShapeDtypeStruct((seq, width), BF16),
            jax.ShapeDtypeStruct((seq, width), BF16),
            jax.ShapeDtypeStruct((seq, d), F32),
        ] + [jax.ShapeDtypeStruct((N_DEV,) + s.shape, s.dtype) for s in later],
        scratch_shapes=[pltpu.VMEM((8, width), F32)] + _exchange_sems(nl),
        compiler_params=_params(("arbitrary",), 56),
    )(x2, sm, win_g, wout, *later)
    return outs[:6], outs[6:]


Q_BLOCKS = 4
ATTN_BWD_LAGS = (1, 2)
ATTN_FWD_LAGS = (2, 4)


def _banded_tiles(kvp_ref, kvc_ref):
    tile = kvc_ref[...].astype(F32)
    blocks = [kvp_ref[...].astype(F32)] + [tile[u * BLOCK:(u + 1) * BLOCK] for u in range(Q_BLOCKS)]
    return [_banded_kv(blocks[u], blocks[u + 1]) for u in range(Q_BLOCKS)]


def _bias_of(bias_ref, i, u, m):
    return bias_ref[jnp.minimum(i, 1) if u == 0 else 1, m]


def _banded_kv(kvp, kvc):
    kw = N_KV_HEADS * HEAD_DIM
    out = []
    for full in (jnp.concatenate([kvp[:, :kw], kvc[:, :kw]], axis=0), jnp.concatenate([kvp[:, kw:], kvc[:, kw:]], axis=0)):
        lo = lax.broadcasted_iota(jnp.int32, full.shape, 1) < HEAD_DIM
        rolled = pltpu.roll(full, HEAD_DIM, 1)
        x2 = [jnp.where(lo, full, rolled).astype(BF16), jnp.where(lo, rolled, full).astype(BF16)]
        ft = full.T
        x2t = [jnp.concatenate([ft[kh * HEAD_DIM:(kh + 1) * HEAD_DIM]] * 2, axis=0).astype(BF16) for kh in range(N_KV_HEADS)]
        out += [x2, x2t]
    return out


def _pair_rows(ref, rows, m, scale=None):
    both = ref[rows, m * LANES:(m + 1) * LANES].astype(F32)
    if scale is not None:
        both = both * scale
    lo = lax.broadcasted_iota(jnp.int32, both.shape, 1) < HEAD_DIM
    zero = jnp.zeros_like(both)
    return jnp.concatenate([jnp.where(lo, both, zero), jnp.where(lo, zero, both)], axis=0).astype(BF16)


def _pair_cols(res_t):
    top = lax.broadcasted_iota(jnp.int32, (LANES, BLOCK), 0) < HEAD_DIM
    return jnp.where(top, res_t[:, :BLOCK], res_t[:, BLOCK:]).T


def _sink_row(sink_ref, m):
    first = lax.broadcasted_iota(jnp.int32, (1, 2 * BLOCK), 1) < BLOCK
    return jnp.where(first, sink_ref[0, 2 * m], sink_ref[0, 2 * m + 1])


def _softmax_t(logits, sink):
    mx =jnp.maximum(jnp.max(logits, axis=0, keepdims=True), sink)
    p = jnp.exp(logits - mx)
    sink_p = jnp.exp(sink - mx)
    inv = 1.0 / (jnp.sum(p, axis=0, keepdims=True) + sink_p)
    return p * inv, sink_p * inv


def _layer_b_fwd(h1, target, kvn, bpre, wkv, wbin_g, biasm, sinks, wbout, bpost):
    seq, d = h1.shape
    kvw = wkv.shape[1]
    cw = wbin_g.shape[2]
    aw = N_Q_HEADS * HEAD_DIM
    per = aw // cw
    tile = Q_BLOCKS * BLOCK

    def body(sink_ref, h1_ref, tgt_ref, kvn_ref, bpre_ref, wkv_ref, wbin_ref, bias_ref, w_ref, g_ref,
             n3_ref, n4_ref, kvc_ref, q_ref, o_ref, dh2_ref, dyb_ref, dattn_ref, dz2_ref, acc_ref,
             attn_ref, z2_ref, kvp_ref):
        i = pl.program_id(0)

        @pl.when(i == 0)
        def _():
            acc_ref[...] = jnp.zeros_like(acc_ref)
            kvp_ref[...] = jnp.zeros_like(kvp_ref)

        hn, _ = _rms(h1_ref[...])
        n3 = (hn * kvn_ref[...]).astype(BF16)
        n4 = (hn * bpre_ref[...]).astype(BF16)
        n3_ref[...] = n3
        n4_ref[...] = n4
        kvc_ref[...] = _dot(n3, wkv_ref[...]).astype(BF16)
        for j in range(N_DEV):
            pj = _dot(n4, wbin_ref[j])
            if j < per:
                q_ref[:, j * cw:(j + 1) * cw] = pj.astype(BF16)
            else:
                z2_ref[:, (j - per) * cw:(j - per + 1) * cw] = pj

        banded = _banded_tiles(kvp_ref, kvc_ref)
        kvp_ref[...] = kvc_ref[tile - BLOCK:tile, :]
        units = [(u, m) for u in range(Q_BLOCKS) for m in range(N_PAIRS)]
        kv_of = lambda m: (2 * m) // GROUP
        logits, probs = {}, {}
        lag_b, lag_c = ATTN_FWD_LAGS
        for step in range(len(units) + lag_c):
            if step < len(units):
                u, m = units[step]
                qpair = _pair_rows(q_ref, slice(u * BLOCK, (u + 1) * BLOCK), m, SCALE)
                logits[step] = _dot_nt(banded[u][0][kv_of(m)], qpair) + _bias_of(bias_ref, i, u, m)
            if 0 <= step - lag_b < len(units):
                u, m = units[step - lag_b]
                probs[step - lag_b] = _softmax_t(logits.pop(step - lag_b), _sink_row(sink_ref, m))[0].astype(BF16)
            if 0 <= step - lag_c < len(units):
                u, m = units[step - lag_c]
                out_t = _dot(banded[u][3][kv_of(m)], probs.pop(step - lag_c))
                attn_ref[u * BLOCK:(u + 1) * BLOCK, m * LANES:(m + 1) * LANES] = _pair_cols(out_t)
        attn = attn_ref[...]
        sz, dsz = _silu(z2_ref[...])
        o = (attn * sz).astype(BF16)
        o_ref[...] = o

        w = w_ref[...]
        yb = _dot(o, w)
        ybn, r = _rms(yb)
        g = g_ref[...]
        diff = h1_ref[...] + ybn * g - tgt_ref[...]
        dh2 = diff * (1.0 / d)
        dh2_ref[...] = dh2
        acc_ref[0:1, :] += jnp.sum(dh2 * ybn, axis=0, keepdims=True)
        tok = jnp.mean(diff * diff, axis=-1, keepdims=True)
        acc_ref[1:2, :] += 0.5 * jnp.sum(tok, axis=0, keepdims=True)
        dyb = _rms_bwd(dh2 * g, ybn, r).astype(BF16)
        dyb_ref[...] = dyb
        do = _dot_nt(dyb, w)
        dattn_ref[...] = (do * sz).astype(BF16)
        dz2_ref[...] = (do * attn * dsz).astype(BF16)

    blk = lambda w: pl.BlockSpec((tile, w), lambda i: (i, 0))
    return pl.pallas_call(
        body,
        name="layer_b_fwd",
        grid=(seq // tile,),
        in_specs=[
            pl.BlockSpec(memory_space=pltpu.SMEM),
            blk(d),
            blk(d),
            _full(kvn.shape),
            _full(bpre.shape),
            _full(wkv.shape),
            _full(wbin_g.shape),
            _full(biasm.shape),
            _full(wbout.shape),
            _full(bpost.shape),
        ],
        out_specs=[blk(d), blk(d), blk(kvw), blk(aw), blk(aw), blk(d), blk(d), blk(aw), blk(aw), _resident((8, d))],
        out_shape=[
            jax.ShapeDtypeStruct((seq, d), BF16),
            jax.ShapeDtypeStruct((seq, d), BF16),
            jax.ShapeDtypeStruct((seq, kvw), BF16),
            jax.ShapeDtypeStruct((seq, aw), BF16),
            jax.ShapeDtypeStruct((seq, aw), BF16),
            jax.ShapeDtypeStruct((seq, d), F32),
            jax.ShapeDtypeStruct((seq, d), BF16),
            jax.ShapeDtypeStruct((seq, aw), BF16),
            jax.ShapeDtypeStruct((seq, aw), BF16),
            jax.ShapeDtypeStruct((8, d), F32),
        ],
        scratch_shapes=[pltpu.VMEM((tile, aw), F32), pltpu.VMEM((tile, aw), F32), pltpu.VMEM((BLOCK, kvw), BF16)],
        compiler_params=_params(("arbitrary",), 56),
    )(sinks, h1, target, kvn, bpre, wkv, wbin_g, biasm, wbout, bpost)


def _attn_bwd(q, kv, dattn, biasm, sinks, ready):
    seq, aw = q.shape
    kvw = kv.shape[1]
    kw = N_KV_HEADS * HEAD_DIM
    nb = seq // BLOCK
    pairs_per_kv = N_PAIRS // N_KV_HEADS
    nr = len(ready)

    tile = Q_BLOCKS * BLOCK
    nsteps = seq // tile
    held = (Q_BLOCKS - 1) * BLOCK

    def body(sink_ref, q_ref, kvc_ref, kvp_ref, da_ref, bias_ref, *refs):
        ready_refs, (dq_ref, dkv_ref, dssum_ref, dsink_ref) = refs[:nr], refs[nr:nr + 4]
        landed_refs, scratch = refs[nr + 4:2 * nr + 4], refs[2 * nr + 4:]
        carry_ref, done_ref, qs_ref, dos_ref, dst_ref, pt_ref, *sems = scratch
        i = pl.program_id(0)

        @pl.when(i == 0)
        def _():
            dssum_ref[...] = jnp.zeros_like(dssum_ref)
            dsink_ref[...] = jnp.zeros_like(dsink_ref)
            carry_ref[...] = jnp.zeros_like(carry_ref)
            done_ref[...] = jnp.zeros_like(done_ref)
            if nr:
                _exchange_start(ready_refs, landed_refs, *sems, True)

        if nr:
            @pl.when(i == nsteps)
            def _():
                _exchange_wait(ready_refs, landed_refs, *sems, True)

        @pl.when(i < nsteps)
        def _():
            lo = lax.broadcasted_iota(jnp.int32, (BAND, LANES), 1) < HEAD_DIM
            head_lane = lax.broadcasted_iota(jnp.int32, (1, LANES), 1)
            banded = _banded_tiles(kvp_ref, kvc_ref)
            units = [(u, m) for u in range(Q_BLOCKS) for m in range(N_PAIRS)]
            dsink = jnp.zeros((1, LANES), F32)
            folded = {}
            logits, dps, dsbs = {}, {}, {}
            lag_b, lag_c = ATTN_BWD_LAGS
            for step in range(len(units) + lag_c):
                if step < len(units):
                    u, m = units[step]
                    kh, rows = m // pairs_per_kv, slice((m % pairs_per_kv) * BAND, (m % pairs_per_kv + 1) * BAND)
                    qrows = slice(u * BLOCK, (u + 1) * BLOCK)
                    qpair = _pair_rows(q_ref, qrows, m, SCALE)
                    dopair = _pair_rows(da_ref, qrows, m)
                    qs_ref[u, kh, rows, :] = qpair
                    dos_ref[u, kh, rows, :] = dopair
                    logits[step] = _dot_nt(banded[u][0][kh], qpair) + _bias_of(bias_ref, i, u, m)
                    dps[step] = _dot_nt(banded[u][2][kh], dopair)
                if 0 <= step - lag_b < len(units):
                    u, m = units[step - lag_b]
                    kh, rows = m // pairs_per_kv, slice((m % pairs_per_kv) * BAND, (m % pairs_per_kv + 1) * BAND)
                    pn, sink_p = _softmax_t(logits.pop(step - lag_b), _sink_row(sink_ref, m))
                    dp = dps.pop(step - lag_b)
                    delta = jnp.sum(pn * dp, axis=0, keepdims=True)
                    ds = pn * (dp - delta)
                    dssum_ref[m] += ds
                    sink_term = sink_p * delta
                    for e in range(2):
                        total = jnp.sum(sink_term[:, e * BLOCK:(e + 1) * BLOCK], axis=1, keepdims=True)
                        dsink = dsink - jnp.where(head_lane == 2 * m + e, total, 0.0)
                    dsbs[step - lag_b] = ds.astype(BF16)
                    dst_ref[u, kh, :, rows] = dsbs[step - lag_b]
                    pt_ref[u, kh, :, rows] = pn.astype(BF16)
                if 0 <= step - lag_c < len(units):
                    u, m = units[step - lag_c]
                    kh = m // pairs_per_kv
                    dq_t = _dot(banded[u][1][kh], dsbs.pop(step - lag_c))
                    dq_ref[u * BLOCK:(u + 1) * BLOCK, m * LANES:(m + 1) * LANES] = (_pair_cols(dq_t) * SCALE).astype(BF16)
                    if m % pairs_per_kv == pairs_per_kv - 1:
                        for name, lhs_ref, rhs_ref in (("k", dst_ref, qs_ref), ("v", pt_ref, dos_ref)):
                            acc = _dot(lhs_ref[u, kh], rhs_ref[u, kh])
                            folded[u, kh, name] = acc + pltpu.roll(acc, HEAD_DIM, 1)
            dsink_ref[0:1, :] += dsink
            dkv = [jnp.concatenate([jnp.where(lo, folded[u, 0, n], folded[u, 1, n]) for n in ("k", "v")], axis=1)
                   for u in range(Q_BLOCKS)]

            @pl.when(i > 0)
            def _():
                if held:
                    dkv_ref[:held, :] = done_ref[...].astype(BF16)
                dkv_ref[held:, :] = (carry_ref[...] + dkv[0][:BLOCK]).astype(BF16)

            for u in range(Q_BLOCKS - 1):
                done_ref[u * BLOCK:(u + 1) * BLOCK, :] = dkv[u][BLOCK:] + dkv[u + 1][:BLOCK]
            carry_ref[...] = dkv[Q_BLOCKS - 1][BLOCK:]

        @pl.when(i == nsteps)
        def _():
            if held:
                dkv_ref[:held, :] = done_ref[...].astype(BF16)
            dkv_ref[held:, :] = carry_ref[...].astype(BF16)

    last = nsteps - 1
    blk = lambda w: pl.BlockSpec((tile, w), lambda i: (jnp.minimum(i, last), 0))
    outs = pl.pallas_call(
        body,
        name="attn_bwd",
        grid=(nsteps + 1,),
        in_specs=[
            pl.BlockSpec(memory_space=pltpu.SMEM),
            blk(aw),
            blk(kvw),
            pl.BlockSpec((BLOCK, kvw), lambda i: (jnp.clip(Q_BLOCKS * i - 1, 0, nb - 1), 0)),
            blk(aw),
            _full(biasm.shape),
        ] + [HBM_SPEC] * nr,
        out_specs=[
            blk(aw),
            pl.BlockSpec((tile, kvw), lambda i: (jnp.maximum(i - 1, 0), 0)),
            _resident(biasm.shape[1:]),
            _resident((8, LANES)),
        ] + [HBM_SPEC] * nr,
        out_shape=[
            jax.ShapeDtypeStruct((seq, aw), BF16),
            jax.ShapeDtypeStruct((seq, kvw), BF16),
            jax.ShapeDtypeStruct(biasm.shape[1:], F32),
            jax.ShapeDtypeStruct((8, LANES), F32),
        ] + [jax.ShapeDtypeStruct(g.shape, g.dtype) for g in ready],
        scratch_shapes=[
            pltpu.VMEM((BLOCK, kvw), F32),
            pltpu.VMEM((max(held, 8), kvw), F32),
            pltpu.VMEM((Q_BLOCKS, N_KV_HEADS, pairs_per_kv * BAND, LANES), BF16),
            pltpu.VMEM((Q_BLOCKS, N_KV_HEADS, pairs_per_kv * BAND, LANES), BF16),
            pltpu.VMEM((Q_BLOCKS, N_KV_HEADS, BAND, pairs_per_kv * BAND), BF16),
            pltpu.VMEM((Q_BLOCKS, N_KV_HEADS, BAND, pairs_per_kv * BAND), BF16),
        ] + _exchange_sems(nr),
        compiler_params=_params(("arbitrary",), 48),
    )(sinks, q, kv, kv, dattn, biasm, *ready)
    return outs[:4], outs[4:]


def _relbias_grad(dssum2, bucket_row, chunk):
    heads, n = dssum2.shape

    def body(a_ref, bucket_ref, out_ref):
        @pl.when(pl.program_id(0) == 0)
        def _():
            out_ref[...] = jnp.zeros_like(out_ref)

        a = a_ref[...]
        hi = a.astype(BF16)
        lo = (a - hi.astype(F32)).astype(BF16)
        onehot_t = (lax.broadcasted_iota(jnp.int32, (LANES, chunk), 0) == bucket_ref[...]).astype(F32).astype(BF16)
        out_ref[...] += _dot_nt(hi, onehot_t) + _dot_nt(lo, onehot_t)

    return pl.pallas_call(
        body,
        name="relbias_grad",
        grid=(n // chunk,),
        in_specs=[pl.BlockSpec((heads, chunk), lambda i: (0, i)), pl.BlockSpec((1, chunk), lambda i: (0, i))],
        out_specs=_resident((heads, LANES)),
        out_shape=jax.ShapeDtypeStruct((heads, LANES), F32),
        compiler_params=_params(("arbitrary",), 32),
    )(dssum2, bucket_row)


def _layer_b_in_bwd(dh2, dq, dz2, dkv, h1, ya, wbin_g, wkv, kvn, bpre, sm, ready, ts):
    seq, d = h1.shape
    aw = dq.shape[1]
    kvw = dkv.shape[1]
    cw = wbin_g.shape[2]
    per = aw // cw

    nr = len(ready)
    nt = seq // ts

    def body(dh2_ref, dq_ref, dz2_ref, dkv_ref, h1_ref, ya_ref, wbin_ref, wkv_ref, kvn_ref, bpre_ref, sm_ref, *refs):
        ready_refs, (dh1_ref, dya_ref, acc_ref) = refs[:nr], refs[nr:nr + 3]
        landed_refs, sems = refs[nr + 3:2 * nr + 3], refs[2 * nr + 3:]

        @pl.when(pl.program_id(0) == 0)
        def _():
            acc_ref[...] = jnp.zeros_like(acc_ref)
            _exchange_start(ready_refs, landed_refs, *sems, True)

        @pl.when(pl.program_id(0) == nt - 1)
        def _():
            _exchange_wait(ready_refs, landed_refs, *sems, True)

        dn4 = jnp.zeros((ts, d), F32)
        for j in range(N_DEV):
            src = dq_ref if j < per else dz2_ref
            jj = j % per
            dn4 = dn4 + _dot_nt(src[:, jj * cw:(jj + 1) * cw], wbin_ref[j])
        dn3 = _dot_nt(dkv_ref[...], wkv_ref[...])
        hn, r = _rms(h1_ref[...])
        acc_ref[0:1, :] += jnp.sum(dn4 * hn, axis=0, keepdims=True)
        acc_ref[1:2, :] += jnp.sum(dn3 * hn, axis=0, keepdims=True)
        dh1 = dh2_ref[...] + _rms_bwd(dn4 * bpre_ref[...] + dn3 * kvn_ref[...], hn, r)
        dh1_ref[...] = dh1
        yan, r2 = _rms(ya_ref[...])
        acc_ref[2:3, :] += jnp.sum(dh1 * yan, axis=0, keepdims=True)
        dya_ref[...] = _rms_bwd(dh1 * sm_ref[4:5, :], yan, r2).astype(BF16)

    outs = pl.pallas_call(
        body,
        name="layer_b_in_bwd",
        grid=(nt,),
        in_specs=[_rows(ts, d), _rows(ts, aw), _rows(ts, aw), _rows(ts, kvw), _rows(ts, d), _rows(ts, d),
                  _full(wbin_g.shape), _full(wkv.shape), _full(kvn.shape), _full(bpre.shape), _full(sm.shape)]
        + [HBM_SPEC] * nr,
        out_specs=[_rows(ts, d), _rows(ts, d), _resident((8, d))] + [HBM_SPEC] * nr,
        out_shape=[jax.ShapeDtypeStruct((seq, d), F32), jax.ShapeDtypeStruct((seq, d), BF16),
                   jax.ShapeDtypeStruct((8, d), F32)] + [jax.ShapeDtypeStruct(g.shape, g.dtype) for g in ready],
        scratch_shapes=_exchange_sems(nr),
        compiler_params=_params(("arbitrary",), 48),
    )(dh2, dq, dz2, dkv, h1, ya, wbin_g, wkv, kvn, bpre, sm, *ready)
    return outs[:3], outs[3:]


def _layer_a_bwd(dya, proj, conv, dh1, x2, wout, win_g, sm, ts):
    seq, d = x2.shape
    width = wout.shape[0]
    half = win_g.shape[2]
    n_half = width // half
    nt = seq // ts

    def body(dya_ref, proj_ref, conv_ref, dh1_ref, x_ref, wout_ref, win_ref, sm_ref, dproj_ref, gx_ref, acc_ref,
             dnext_ref):
        @pl.when(pl.program_id(0) == 0)
        def _():
            acc_ref[...] = jnp.zeros_like(acc_ref)
            dnext_ref[...] = jnp.zeros_like(dnext_ref)

        dy = _dot_nt(dya_ref[...], wout_ref[...])
        row = lax.broadcasted_iota(jnp.int32, (ts, half), 0)
        dn1 = jnp.zeros((ts, d), F32)
        for hh in range(n_half):
            cols = slice(hh * half, (hh + 1) * half)
            b, c, u, z = [proj_ref[:, (part * n_half + hh) * half:(part * n_half + hh + 1) * half].astype(F32)
                          for part in range(4)]
            cv = conv_ref[:, cols].astype(F32)
            dyh = dy[:, cols]
            sz, dsz = _silu(z)
            dconv = dyh * b * sz
            grads = [dyh * cv * sz, None, None, dyh * b * cv * dsz]
            next0, next1 = dnext_ref[0:1, cols], dnext_ref[1:2, cols]
            dc1 = jnp.where(row == ts - 1, next0, pltpu.roll(dconv, ts - 1, 0))
            dc2 = jnp.where(row == ts - 1, next1, jnp.where(row == ts - 2, next0, pltpu.roll(dconv, ts - 2, 0)))
            dnext_ref[:, cols] = dconv[0:8, :]
            v = c * u
            acc_ref[1:2, cols] += jnp.sum(dc2 * v, axis=0, keepdims=True)
            acc_ref[2:3, cols] += jnp.sum(dc1 * v, axis=0, keepdims=True)
            acc_ref[3:4, cols] += jnp.sum(dconv * v, axis=0, keepdims=True)
            dv = sm_ref[3:4, cols] * dconv + sm_ref[2:3, cols] * dc1 + sm_ref[1:2, cols] * dc2
            grads[1] = dv * u
            grads[2] = dv * c
            for part in range(4):
                j = part * n_half + hh
                gj = grads[part].astype(BF16)
                dproj_ref[:, j * half:(j + 1) * half] = gj
                dn1 = dn1 + _dot_nt(gj, win_ref[j])
        xn, r = _rms(x_ref[...])
        acc_ref[0:1, :] += jnp.sum(dn1 * xn, axis=0, keepdims=True)
        gx_ref[...] = dh1_ref[...] + _rms_bwd(dn1 * sm_ref[0:1, :], xn, r)

    rev = lambda w: pl.BlockSpec((ts, w), lambda i: (nt - 1 - i, 0))
    return pl.pallas_call(
        body,
        name="layer_a_bwd",
        grid=(nt,),
        in_specs=[rev(d), rev(4 * width), rev(width), rev(d), rev(d), _full(wout.shape), _full(win_g.shape), _full(sm.shape)],
        out_specs=[rev(4 * width), rev(d), _resident((8, d))],
        out_shape=[jax.ShapeDtypeStruct((seq, 4 * width), BF16), jax.ShapeDtypeStruct((seq, d), F32),
                   jax.ShapeDtypeStruct((8, d), F32)],
        scratch_shapes=[pltpu.VMEM((8, width), F32)],
        compiler_params=_params(("arbitrary",), 56),
    )(dya, proj, conv, dh1, x2, wout, win_g, sm)


def _wgrad(a, bs, n_slots, ts, name, ready=(), block_cols=1024):
    nr = len(ready)
    seq, k = a.shape
    nb_in = len(bs)
    n_each = bs[0].shape[1]
    n = nb_in * n_each
    bn = min(n_each, block_cols)
    per_in = n_each // bn
    n_blocks = nb_in * per_in
    ns = seq // ts

    def b_spec(idx):
        def index(j, s):
            mine = j // per_in == idx
            row = jnp.where(mine, s, jnp.where(j // per_in > idx, ns - 1, 0))
            return (row, jnp.where(mine, j % per_in, jnp.where(j // per_in > idx, per_in - 1, 0)))
        return pl.BlockSpec((ts, bn), index)

    if n_slots:
        sw = n // n_slots
        spb = bn // sw
        out_shape = jax.ShapeDtypeStruct((n_slots, k, sw), BF16)
        out_spec = pl.BlockSpec((spb, k, sw), lambda j, s: (j, 0, 0))
    else:
        out_shape = jax.ShapeDtypeStruct((k, n), BF16)
        out_spec = pl.BlockSpec((k, bn), lambda j, s: (0, j))

    def body(a_ref, *refs):
        b_refs, ready_refs, o_ref = refs[:nb_in], refs[nb_in:nb_in + nr], refs[nb_in + nr]
        landed_refs, (acc_ref, *sems) = refs[nb_in + nr + 1:nb_in + 2 * nr + 1], refs[nb_in + 2 * nr + 1:]
        j, s = pl.program_id(0), pl.program_id(1)

        if nr:
            @pl.when(jnp.logical_and(j == 0, s == 0))
            def _():
                _exchange_start(ready_refs, landed_refs, *sems, True)

            @pl.when(jnp.logical_and(j == n_blocks - 1, s == ns - 1))
            def _():
                _exchange_wait(ready_refs, landed_refs, *sems, True)

        @pl.when(s == 0)
        def _():
            acc_ref[...] = jnp.zeros_like(acc_ref)

        for idx in range(nb_in):
            @pl.when(j // per_in == idx)
            def _(idx=idx):
                acc_ref[...] += _dot_tn(a_ref[...], b_refs[idx][...])

        @pl.when(s == ns - 1)
        def _():
            if n_slots:
                for e in range(spb):
                    o_ref[e] = acc_ref[:, e * sw:(e + 1) * sw].astype(BF16)
            else:
                o_ref[...] = acc_ref[...].astype(BF16)

    outs = pl.pallas_call(
        body,
        name=name,
        grid=(n_blocks, ns),
        in_specs=[pl.BlockSpec((ts, k), lambda j, s: (s, 0))] + [b_spec(idx) for idx in range(nb_in)] + [HBM_SPEC] * nr,
        out_specs=[out_spec] + [HBM_SPEC] * nr,
        out_shape=[out_shape] + [jax.ShapeDtypeStruct(g.shape, g.dtype) for g in ready],
        scratch_shapes=[pltpu.VMEM((k, bn), F32)] + (_exchange_sems(nr) if nr else []),
        compiler_params=_params(("arbitrary", "arbitrary"), 48),
    )(a, *bs, *ready)
    return (outs[0], outs[1:]) if nr else outs[0]


def _wgrad_tail(pairs, part, landed, ts):
    n_tasks = len(pairs)
    nl = len(landed)
    seq, k = pairs[0][0].shape
    n = pairs[0][1].shape[1]
    ns = seq // ts
    total = n_tasks * ns
    per = k // N_DEV

    def spec(t, width):
        return pl.BlockSpec((ts, width), lambda j, s: (jnp.where(j == t, s, jnp.where(j > t, ns - 1, 0)), 0))

    def body(*refs):
        ab_refs, part_ref = refs[:2 * n_tasks], refs[2 * n_tasks]
        landed_refs, refs = refs[2 * n_tasks + 1:2 * n_tasks + 1 + nl], refs[2 * n_tasks + 1 + nl:]
        o_ref, red_ref = refs[:2]
        summed_refs, (acc_ref, sib_ref, chip_ref, send_ref, *sems) = refs[2:2 + nl], refs[2 + nl:]
        j, s = pl.program_id(0), pl.program_id(1)
        flat = j * ns + s
        swap, send, forward, finish = _chip_reduce(part_ref, red_ref, sib_ref, chip_ref, send_ref, sems)

        @pl.when(flat == 0)
        def _():
            swap()

        @pl.when(flat == min(1, total - 1))
        def _():
            send()

        @pl.when(flat == min(total // 2 + 1, total - 1))
        def _():
            forward()
            for t in range(nl):
                _sum_slots(landed_refs[t], summed_refs[t])

        @pl.when(s == 0)
        def _():
            acc_ref[...] = jnp.zeros_like(acc_ref)

        for t in range(n_tasks):
            @pl.when(j == t)
            def _(t=t):
                acc_ref[...] += _dot_tn(ab_refs[2 * t][...], ab_refs[2 * t + 1][...])

        @pl.when(s == ns - 1)
        def _():
            for dev in range(N_DEV):
                o_ref[dev] = acc_ref[dev * per:(dev + 1) * per, :].astype(BF16)

        @pl.when(flat == total - 1)
        def _():
            finish()

    slot = part.shape[1:]
    outs = pl.pallas_call(
        body,
        name="wgrad_tail",
        grid=(n_tasks, ns),
        in_specs=[spec(t, w) for t in range(n_tasks) for w in (k, n)] + [_full(part.shape)]
        + [_full(g.shape) for g in landed],
        out_specs=[pl.BlockSpec((N_DEV, per, n), lambda j, s: (0, j, 0)), _resident(slot)]
        + [_resident(g.shape[1:]) for g in landed],
        out_shape=[jax.ShapeDtypeStruct((N_DEV, n_tasks * per, n), BF16), jax.ShapeDtypeStruct(slot, F32)]
        + [jax.ShapeDtypeStruct(g.shape[1:], F32) for g in landed],
        scratch_shapes=[pltpu.VMEM((k, n), F32)] + _chip_reduce_scratch(slot),
        compiler_params=_params(("arbitrary", "arbitrary"), 58),
    )(*[op for pair in pairs for op in pair], part, *landed)
    return outs[0], outs[1], outs[2:]


def _adamw(ws, gs, ms, vs):
    n = len(ws)

    def step(w, g, m, v):
        m = ADAM_B1 * m + (1.0 - ADAM_B1) * g
        v = ADAM_B2 * v + (1.0 - ADAM_B2) * jnp.square(g)
        m_hat = m / (1.0 - ADAM_B1 ** ADAM_STEP)
        v_hat = v / (1.0 - ADAM_B2 ** ADAM_STEP)
        return g, -ADAM_LR * (m_hat / (jnp.sqrt(v_hat) + ADAM_EPS) + ADAM_WD * w), m, v

    def body(*refs):
        w_refs, g_refs, m_refs, v_refs = (refs[k * n:(k + 1) * n] for k in range(4))
        go_refs, d_refs, nm_refs, nv_refs = (refs[(4 + k) * n:(5 + k) * n] for k in range(4))
        for t in range(n):
            rows = w_refs[t].shape[0]
            if rows <= 128:
                go_refs[t][...], d_refs[t][...], nm_refs[t][...], nv_refs[t][...] = step(
                    w_refs[t][...], g_refs[t][...], m_refs[t][...], v_refs[t][...])
                continue
            chunk = 128

            def one(i, carry, t=t):
                r = pl.ds(pl.multiple_of(i * chunk, chunk), chunk)
                go_refs[t][r, :], d_refs[t][r, :], nm_refs[t][r, :], nv_refs[t][r, :] = step(
                    w_refs[t][r, :], g_refs[t][r, :], m_refs[t][r, :], v_refs[t][r, :])
                return carry

            lax.fori_loop(0, rows // chunk, one, 0)

    vmem = pl.BlockSpec(memory_space=pltpu.VMEM)
    outs = pl.pallas_call(
        body,
        name="adamw",
        in_specs=[vmem] * (4 * n),
        out_specs=[vmem] * (4 * n),
        out_shape=[jax.ShapeDtypeStruct(w.shape, F32) for w in ws] * 4,
        compiler_params=_params(vmem_mib=56),
    )(*ws, *gs, *ms, *vs)
    return outs[:n], outs[n:2 * n], outs[2 * n:3 * n], outs[3 * n:]


def _band_structure():
    q_loc = jnp.arange(BLOCK, dtype=jnp.int32)[:, None]
    s_loc = jnp.arange(2 * BLOCK, dtype=jnp.int32)[None, :]
    dist = q_loc + BLOCK - s_loc
    in_window = (dist >= 0) & (dist < BLOCK)
    dd = jnp.maximum(dist, 0)
    max_exact = N_BUCKETS // 2
    large = max_exact + (jnp.log(jnp.maximum(dd, 1).astype(F32) / max_exact) / math.log(MAX_DISTANCE / max_exact)
                         * (N_BUCKETS - max_exact)).astype(jnp.int32)
    bucket = jnp.where(dd < max_exact, dd, jnp.minimum(large, N_BUCKETS - 1))
    return bucket, in_window.astype(jnp.int32)


def _place_rows(a, row, rows=8):
    return jnp.pad(a, ((row, rows - row - a.shape[0]), (0, 0)))


def kernel(x, a_pre_norm, a_w_in, a_conv_w, a_w_out, a_post_norm, kv_norm, w_kv, rel_bias, b_pre_norm, b_w_in, b_sinks, b_w_out, b_post_norm, loss_target, m_a_pre_norm, m_a_w_in, m_a_conv_w, m_a_w_out, m_a_post_norm, m_kv_norm, m_w_kv, m_rel_bias, m_b_pre_norm, m_b_w_in, m_b_sinks, m_b_w_out, m_b_post_norm, v_a_pre_norm, v_a_w_in, v_a_conv_w, v_a_w_out, v_a_post_norm, v_kv_norm, v_w_kv, v_rel_bias, v_b_pre_norm, v_b_w_in, v_b_sinks, v_b_w_out, v_b_post_norm):
    seq, d = x.shape[1], x.shape[2]
    x2 = x.reshape(seq, d)
    target = loss_target.reshape(seq, d)
    shard = a_pre_norm.shape[1]
    me = _my_index()
    ts_a = min(seq, 512)
    ts = min(seq, 512)
    ts_w = min(seq, 2048)

    small = _place_rows(a_pre_norm, 0) + _place_rows(a_conv_w[0], 1) + _place_rows(a_post_norm, 4)
    bucket, in_window = _band_structure()
    win_g, wout_g, small_g, biasm = _all_gather(
        [a_w_in[0], a_w_out[0], small], [BF16, BF16, F32], rel_bias.T, bucket.T, in_window.T)
    wout = wout_g.reshape(-1, wout_g.shape[2])
    sm = small_g.transpose(1, 0, 2).reshape(8, N_DEV * shard)
    kvn = kv_norm.reshape(1, d)

    (h1, n1, proj, conv, y, ya), (wkv_g, wbin_g, wbout_g) = _layer_a_fwd(
        x2, sm, win_g, wout, [w_kv.astype(BF16), b_w_in[0].astype(BF16), b_w_out[0].astype(BF16)], ts_a)
    wkv = wkv_g.reshape(-1, wkv_g.shape[2])
    wbout = wbout_g.reshape(-1, wbout_g.shape[2])
    n3, n4, kv, q, o, dh2, dyb, dattn, dz2, acc_c = _layer_b_fwd(
        h1, target, kvn, b_pre_norm, wkv, wbin_g, biasm, b_sinks, wbout, b_post_norm)

    (dq, dkv, dssum, dsink), _ = _attn_bwd(q, kv, dattn, biasm, b_sinks, [])
    by_head = dssum.reshape(N_PAIRS, BAND, 2, BLOCK).transpose(0, 2, 3, 1)
    relb = _relbias_grad(by_head.reshape(N_Q_HEADS, -1), bucket.reshape(1, -1), 4096)
    g_wkv = _wgrad(n3, [dkv], 0, ts_w, "wgrad_kv").reshape(wkv_g.shape)
    g_wbin = _wgrad(n4, [dq, dz2], N_DEV, ts_w, "wgrad_b_in")
    (dh1, dya, acc_b), (l_wkv, l_wbin) = _layer_b_in_bwd(
        dh2, dq, dz2, dkv, h1, ya, wbin_g, wkv, kvn, b_pre_norm, sm, [g_wkv, g_wbin], ts)
    dproj, gx, acc_a = _layer_a_bwd(dya, proj, conv, dh1, x2, wout, win_g, sm, ts_a)
    g_win = _wgrad(n1, [dproj], N_DEV, ts_w, "wgrad_a_in", block_cols=2048)
    g_outs, r_win, (r_wkv, r_wbin) = _wgrad_tail([(y, dya), (o, dyb)], g_win, [l_wkv, l_wbin], min(seq, 1024))

    r_outs, _, (s_a, s_b, s_c, s_relb, s_sink) = _reduce_exchange(g_outs, [], [acc_a, acc_b, acc_c, relb, dsink])
    rows_out = wout_g.shape[1]
    r_wout, r_wbout = r_outs[:rows_out], r_outs[rows_out:]
    mine = lambda rows: lax.dynamic_slice_in_dim(rows, me * shard, shard, axis=1)
    loss = s_c[1, 0]
    weights = [a_pre_norm, a_w_in[0], a_conv_w[0], a_w_out[0], a_post_norm, kvn, w_kv, rel_bias.T, b_pre_norm,
               b_w_in[0], b_sinks, b_w_out[0], b_post_norm]
    grads = [mine(s_a[0:1]), r_win, mine(s_a[1:4]), r_wout, mine(s_b[2:3]), s_b[1:2], r_wkv,
             s_relb[:, :N_BUCKETS], s_b[0:1], r_wbin, s_sink[0:1, :N_Q_HEADS], r_wbout, s_c[0:1]]
    first = [m_a_pre_norm, m_a_w_in[0], m_a_conv_w[0], m_a_w_out[0], m_a_post_norm, m_kv_norm.reshape(1, d), m_w_kv,
             m_rel_bias.T, m_b_pre_norm, m_b_w_in[0], m_b_sinks, m_b_w_out[0], m_b_post_norm]
    second = [v_a_pre_norm, v_a_w_in[0], v_a_conv_w[0], v_a_w_out[0], v_a_post_norm, v_kv_norm.reshape(1, d), v_w_kv,
              v_rel_bias.T, v_b_pre_norm, v_b_w_in[0], v_b_sinks, v_b_w_out[0], v_b_post_norm]
    grads, deltas, new_m, new_v = _adamw(weights, grads, first, second)

    shapes = [a_pre_norm.shape, a_w_in.shape, a_conv_w.shape, a_w_out.shape, a_post_norm.shape, kv_norm.shape,
              w_kv.shape, None, b_pre_norm.shape, b_w_in.shape, b_sinks.shape, b_w_out.shape, b_post_norm.shape]
    shaped = lambda arrays: [a.T if s is None else a.reshape(s) for a, s in zip(arrays, shapes)]
    return (loss, gx.reshape(x.shape), *shaped(grads), *shaped(deltas), *shaped(new_m), *shaped(new_v))
```

```python
import math

import jax
import jax.numpy as jnp
from jax import lax
from jax.experimental import pallas as pl
from jax.experimental.pallas import tpu as pltpu

HEAD_DIM = 64
N_Q_HEADS = 16
N_KV_HEADS = 2
GROUP = N_Q_HEADS // N_KV_HEADS
BLOCK = 128
N_BUCKETS = 32
MAX_DISTANCE = 128
EPS = 1e-6
NEG_INF = -1e30
SCALE = HEAD_DIM ** -0.5

ADAM_LR = 0.001
ADAM_B1 = 0.9
ADAM_B2 = 0.999
ADAM_EPS = 1e-08
ADAM_WD = 0.01
ADAM_STEP = 10

N_PAIRS = N_Q_HEADS // 2
BAND = 2 * BLOCK

N_DEV = 8
GATHER_PIECE_ROWS = 256
LANES = 128
F32 = jnp.float32
BF16 = jnp.bfloat16
MESH = pl.DeviceIdType.MESH
MIB = 1024 * 1024


def _params(semantics=None, vmem_mib=48):
    return pltpu.CompilerParams(dimension_semantics=semantics, vmem_limit_bytes=vmem_mib * MIB)


def _full(shape):
    zeros = (0,) * len(shape)
    return pl.BlockSpec(shape, lambda *_: zeros, pipeline_mode=pl.Buffered(1))


def _resident(shape):
    zeros = (0,) * len(shape)
    return pl.BlockSpec(shape, lambda *_: zeros)


def _rows(ts, cols):
    return pl.BlockSpec((ts, cols), lambda i: (i, 0))


def _dot(a, b):
    return jnp.dot(a, b, preferred_element_type=F32)


def _dot_nt(a, b):
    return lax.dot_general(a, b, (((1,), (1,)), ((), ())), preferred_element_type=F32)


def _dot_tn(a, b):
    return lax.dot_general(a, b, (((0,), (0,)), ((), ())), preferred_element_type=F32)


def _rms(xf):
    r = lax.rsqrt(jnp.mean(xf * xf, axis=-1, keepdims=True) + EPS)
    return xf * r, r


def _rms_bwd(dn, xn, r):
    return r * (dn - xn * jnp.mean(dn * xn, axis=-1, keepdims=True))


def _silu(z):
    s = jax.nn.sigmoid(z)
    return z * s, s * (1.0 + z * (1.0 - s))


def _my_index():
    return 4 * lax.axis_index("x") + 2 * lax.axis_index("y") + lax.axis_index("c")


def _bias_table(rb_ref, bucket_ref, win_ref, out_ref):
    bk = jnp.where(win_ref[...] != 0, bucket_ref[...], -1)
    has_prev = lax.broadcasted_iota(jnp.int32, bk.shape, 0) >= BLOCK
    for h in range(N_Q_HEADS):
        acc = jnp.full(bk.shape, NEG_INF, F32)
        for b in range(N_BUCKETS):
            acc = jnp.where(bk == b, rb_ref[h, b], acc)
        cols = slice((h % 2) * BLOCK, (h % 2 + 1) * BLOCK)
        out_ref[1, h // 2, :, cols] = acc
        out_ref[0, h // 2, :, cols] = jnp.where(has_prev, acc, NEG_INF)


def _all_gather(shards, out_dtypes, rel_bias_t, bucket_t, in_window_t):
    n = len(shards)
    pieces = [(t, r0, min(GATHER_PIECE_ROWS, s.shape[0] - r0))
              for t, s in enumerate(shards) for r0 in range(0, s.shape[0], GATHER_PIECE_ROWS)]

    def body(*refs):
        ins, (rb_ref, bucket_ref, win_ref) = refs[:n], refs[n:n + 3]
        outs, bias_ref = refs[n + 3:2 * n + 3], refs[2 * n + 3]
        send_sems, recv_sems = refs[2 * n + 4], refs[2 * n + 5]
        x, y, c = lax.axis_index("x"), lax.axis_index("y"), lax.axis_index("c")
        me, sibling = (x, y, c), (x, y, 1 - c)
        x_nbr, y_nbr, diagonal = (1 - x, y), (x, 1 - y), (1 - x, 1 - y)
        south = c == 0
        relayed = (jnp.where(south, 1 - x, x), jnp.where(south, y, 1 - y))
        relay_to = (jnp.where(south, x, 1 - x), jnp.where(south, 1 - y, y))

        def copy(u, k, block, to):
            t, r0, nrows = pieces[u]
            rows = outs[t].at[4 * block[0] + 2 * block[1] + block[2], pl.ds(r0, nrows)]
            return pltpu.make_async_remote_copy(
                src_ref=rows, dst_ref=rows, send_sem=send_sems.at[u, k], recv_sem=recv_sems.at[u, k],
                device_id=to, device_id_type=MESH)

        for t in range(n):
            outs[t][pl.ds(_my_index(), 1)] = ins[t][...].astype(outs[t].dtype)[None]
        started = []

        def start(cp):
            cp.start()
            started.append(cp)

        units = range(len(pieces))
        for u in units:
            start(copy(u, 0, me, sibling))
            start(copy(u, 1, me, (*x_nbr, c)))
            start(copy(u, 2, me, (*y_nbr, c)))
        _bias_table(rb_ref, bucket_ref, win_ref, bias_ref)
        for u in units:
            for k, chip in ((1, x_nbr), (2, y_nbr)):
                copy(u, k, (*chip, c), me).wait_recv()
                start(copy(u, 3 + k, (*chip, c), sibling))
            start(copy(u, 3, (*relayed, c), (*relay_to, c)))
        for u in units:
            copy(u, 3, (*diagonal, c), me).wait_recv()
            start(copy(u, 6, (*diagonal, c), sibling))
        for u in units:
            copy(u, 0, sibling, me).wait_recv()
        for k, chip in ((4, x_nbr), (5, y_nbr), (6, diagonal)):
            for u in units:
                copy(u, k, (*chip, 1 - c), me).wait_recv()
        for cp in started:
            cp.wait_send()

    vmem = pl.BlockSpec(memory_space=pltpu.VMEM)
    return pl.pallas_call(
        body,
        name="gather_weights",
        out_shape=[jax.ShapeDtypeStruct((N_DEV,) + s.shape, dt) for s, dt in zip(shards, out_dtypes)]
        + [jax.ShapeDtypeStruct((2, N_PAIRS, BAND, 2 * BLOCK), F32)],
        in_specs=[vmem] * n + [pl.BlockSpec(memory_space=pltpu.SMEM), vmem, vmem],
        out_specs=[vmem] * (n + 1),
        scratch_shapes=[pltpu.SemaphoreType.DMA((len(pieces), 7)), pltpu.SemaphoreType.DMA((len(pieces), 7))],
        compiler_params=_params(vmem_mib=48),
    )(*shards, rel_bias_t, bucket_t, in_window_t)


def _peer(k):
    x, y, c = lax.axis_index("x"), lax.axis_index("y"), lax.axis_index("c")
    px = 1 - x if k & 4 else x
    py = 1 - y if k & 2 else y
    pc = 1 - c if k & 1 else c
    return (px, py, pc), 4 * px + 2 * py + pc


def _exchange(srcs, dsts, send_sems, recv_sems, local_sems, scatter):
    me = _my_index()
    sends, arrivals = [], []
    for k in range(1, N_DEV):
        peer, pidx = _peer(k)
        for t, (src, dst) in enumerate(zip(srcs, dsts)):
            mine = src.at[pidx] if scatter else src
            sems = dict(send_sem=send_sems.at[t, k - 1], recv_sem=recv_sems.at[t, k - 1], device_id=peer, device_id_type=MESH)
            sends.append(pltpu.make_async_remote_copy(src_ref=mine, dst_ref=dst.at[me], **sems))
            arrivals.append(pltpu.make_async_remote_copy(src_ref=mine, dst_ref=dst.at[pidx], **sems))
    local = [pltpu.make_async_copy(src.at[me] if scatter else src, dst.at[me], local_sems.at[t])
             for t, (src, dst) in enumerate(zip(srcs, dsts))]
    return sends, arrivals, local


def _exchange_start(*args):
    sends, _, local = _exchange(*args)
    for cp in sends + local:
        cp.start()


def _exchange_wait(*args):
    sends, arrivals, local = _exchange(*args)
    for cp in arrivals:
        cp.wait_recv()
    for cp in sends:
        cp.wait_send()
    for cp in local:
        cp.wait()


def _exchange_sems(n):
    if not n:
        return []
    return [pltpu.SemaphoreType.DMA((n, N_DEV - 1)), pltpu.SemaphoreType.DMA((n, N_DEV - 1)), pltpu.SemaphoreType.DMA((n,))]


HBM_SPEC = pl.BlockSpec(memory_space=pl.ANY)


def _sum_slots(recv_ref, out_ref):
    rows = out_ref.shape[0]
    chunk = min(rows, 128)

    def add(i, carry):
        r0 = pl.multiple_of(i * chunk, chunk)
        acc = recv_ref[0, pl.ds(r0, chunk), :].astype(F32)
        for dev in range(1, N_DEV):
            acc = acc + recv_ref[dev, pl.ds(r0, chunk), :].astype(F32)
        out_ref[pl.ds(r0, chunk), :] = acc
        return carry

    lax.fori_loop(0, rows // chunk, add, 0)


N_CHIPS = N_DEV // 2


def _rows_loop(rows, fn):
    chunk = min(rows, 128)

    def step(i, carry):
        fn(pl.ds(pl.multiple_of(i * chunk, chunk), chunk))
        return carry

    lax.fori_loop(0, rows // chunk, step, 0)


def _chip_reduce(g_ref, out_ref, sib_ref, land_ref, send_ref, sems):
    sib_send, sib_recv, ici_send, ici_recv = sems
    x, y, c = lax.axis_index("x"), lax.axis_index("y"), lax.axis_index("c")
    south = c == 0
    near =(jnp.where(south, 1 - x, x), jnp.where(south, y, 1 - y))
    far = (jnp.where(south, x, 1 - x), jnp.where(south, 1 - y, y))
    diagonal = (1 - x, 1 - y)
    rows = out_ref.shape[0]
    direct, fold, folded = 0, 1, 2

    def to_sibling(t):
        return pltpu.make_async_remote_copy(
            src_ref=g_ref.at[2 * t + 1 - c], dst_ref=sib_ref.at[t], send_sem=sib_send.at[t], recv_sem=sib_recv.at[t],
            device_id=(x, y, 1 - c), device_id_type=MESH)

    def ici(role, chip):
        return pltpu.make_async_remote_copy(
            src_ref=send_ref.at[role], dst_ref=land_ref.at[role], send_sem=ici_send.at[role],
            recv_sem=ici_recv.at[role], device_id=(*chip, c), device_id_type=MESH)

    def pair_sum(chip, r):
        t = 2 * chip[0] + chip[1]
        return g_ref[2 * t + c, r, :].astype(F32) + sib_ref[t, r, :].astype(F32)

    def swap():
        for t in range(N_CHIPS):
            to_sibling(t).start()

    def send():
        for t in range(N_CHIPS):
            to_sibling(t).wait_recv()
        for role, chip in ((fold, diagonal), (direct, near)):
            def fill(r, role=role, chip=chip):
                send_ref[role, r, :] = pair_sum(chip, r).astype(BF16)

            _rows_loop(rows, fill)
            ici(role, near).start()

    def forward():
        ici(fold, near).wait_recv()

        def fill(r):
            send_ref[folded, r, :] = (pair_sum(far, r) + land_ref[fold, r, :].astype(F32)).astype(BF16)

        _rows_loop(rows, fill)
        ici(folded, far).start()

    def finish():
        ici(direct, near).wait_recv()
        ici(folded, far).wait_recv()

        def total(r):
            mine = pair_sum((x, y), r)
            out_ref[r, :] = mine + land_ref[direct, r, :].astype(F32) + land_ref[folded, r, :].astype(F32)

        _rows_loop(rows, total)
        for t in range(N_CHIPS):
            to_sibling(t).wait_send()
        for role, chip in ((direct, near), (fold, near), (folded, far)):
            ici(role, chip).wait_send()

    return swap, send, forward, finish


def _chip_reduce_scratch(slot):
    return [pltpu.VMEM((N_CHIPS,) + slot, BF16), pltpu.VMEM((3,) + slot, BF16), pltpu.VMEM((3,) + slot, BF16),
            pltpu.SemaphoreType.DMA((N_CHIPS,)), pltpu.SemaphoreType.DMA((N_CHIPS,)),
            pltpu.SemaphoreType.DMA((3,)), pltpu.SemaphoreType.DMA((3,))]


def _reduce_exchange(part, landed, smalls):
    nl, ng = len(landed), len(smalls)
    n_out = 1 + nl + ng

    def body(*refs):
        p_in, l_in, s_in = refs[0], refs[1:1 + nl], refs[1 + nl:n_out]
        p_out, l_out, s_out = refs[n_out], refs[n_out + 1:n_out + 1 + nl], refs[n_out + 1 + nl:2 * n_out]
        scratch = refs[2 * n_out:]
        s_recv, (sib_ref, chip_ref, send_ref), sems = scratch[:ng], scratch[ng:ng + 3], scratch[ng + 3:]
        swap, send, forward, finish = _chip_reduce(p_in, p_out, sib_ref, chip_ref, send_ref, sems[:4])
        swap()
        _exchange_start(s_in, s_recv, *sems[4:], False)
        send()
        for t in range(nl):
            _sum_slots(l_in[t], l_out[t])
        forward()
        finish()
        _exchange_wait(s_in, s_recv, *sems[4:], False)
        for t in range(ng):
            acc = s_recv[t][0]
            for dev in range(1, N_DEV):
                acc = acc + s_recv[t][dev]
            s_out[t][...] = acc

    vmem = pl.BlockSpec(memory_space=pltpu.VMEM)
    slot = part.shape[1:]
    outs = pl.pallas_call(
        body,
        name="reduce_grads",
        out_shape=[jax.ShapeDtypeStruct(p.shape[1:], F32) for p in [part] + landed]
        + [jax.ShapeDtypeStruct(s.shape, F32) for s in smalls],
        in_specs=[vmem] * n_out,
        out_specs=[vmem] * n_out,
        scratch_shapes=[pltpu.VMEM((N_DEV,) + s.shape, F32) for s in smalls] + _chip_reduce_scratch(slot)
        + _exchange_sems(ng),
        compiler_params=_params(vmem_mib=56),
    )(part, *landed, *smalls)
    return outs[0], outs[1:1 + nl], outs[1 + nl:]


def _layer_a_fwd(x2, sm, win_g, wout, later, ts):
    seq, d = x2.shape
    width = wout.shape[0]
    half = win_g.shape[2]
    n_half = width // half
    nl = len(later)
    nt = seq // ts

    def body(x_ref, sm_ref, win_ref, wout_ref, *refs):
        shard_refs, refs = refs[:nl], refs[nl:]
        h1_ref, n1_ref, proj_ref, conv_ref, y_ref, ya_ref = refs[:6]
        gathered_refs, (vprev_ref, *sems) = refs[6:6 + nl], refs[6 + nl:]

        @pl.when(pl.program_id(0) == 0)
        def _():
            vprev_ref[...] = jnp.zeros_like(vprev_ref)
            _exchange_start(shard_refs, gathered_refs, *sems, False)

        @pl.when(pl.program_id(0) == nt - 1)
        def _():
            _exchange_wait(shard_refs, gathered_refs, *sems, False)

        xf = x_ref[...]
        xn, _ = _rms(xf)
        n1 = (xn * sm_ref[0:1, :]).astype(BF16)
        n1_ref[...] = n1
        row = lax.broadcasted_iota(jnp.int32, (ts, half), 0)
        ya = jnp.zeros((ts, d), F32)
        for hh in range(n_half):
            cols = slice(hh * half, (hh + 1) * half)
            parts = []
            for part in range(4):
                j = part * n_half + hh
                pj = _dot(n1, win_ref[j])
                proj_ref[:, j * half:(j + 1) * half] = pj.astype(BF16)
                parts.append(pj)
            b, c, u, z = parts
            v = c * u
            last1, last2 = vprev_ref[7:8, cols], vprev_ref[6:7, cols]
            v1 = jnp.where(row == 0, last1, pltpu.roll(v, 1, 0))
            v2 = jnp.where(row == 0, last2, jnp.where(row == 1, last1, pltpu.roll(v, 2, 0)))
            vprev_ref[:, cols] = v[ts - 8:ts, :]
            conv = sm_ref[1:2, cols] * v2 + sm_ref[2:3, cols] * v1 + sm_ref[3:4, cols] * v
            conv_ref[:, cols] = conv.astype(BF16)
            yh = (b * conv * _silu(z)[0]).astype(BF16)
            y_ref[:, cols] = yh
            ya = ya + _dot(yh, wout_ref[cols, :])
        ya_ref[...] = ya
        h1_ref[...] = xf + _rms(ya)[0] * sm_ref[4:5, :]

    outs = pl.pallas_call(
        body,
        name="layer_a_fwd",
        grid=(nt,),
        in_specs=[_rows(ts, d), _full(sm.shape), _full(win_g.shape), _full(wout.shape)] + [HBM_SPEC] * nl,
        out_specs=[_rows(ts, d), _rows(ts, d), _rows(ts, 4 * width), _rows(ts, width), _rows(ts, width), _rows(ts, d)]
        + [HBM_SPEC] * nl,
        out_shape=[
            jax.ShapeDtypeStruct((seq, d), F32),
            jax.ShapeDtypeStruct((seq, d), BF16),
            jax.ShapeDtypeStruct((seq, 4 * width), BF16),
            jax.ShapeDtypeStruct((seq, width), BF16),
            jax.ShapeDtypeStruct((seq, width), BF16),
            jax.ShapeDtypeStruct((seq, d), F32),
        ] + [jax.ShapeDtypeStruct((N_DEV,) + s.shape, s.dtype) for s in later],
        scratch_shapes=[pltpu.VMEM((8, width), F32)] + _exchange_sems(nl),
        compiler_params=_params(("arbitrary",), 56),
    )(x2, sm, win_g, wout, *later)
    return outs[:6], outs[6:]


Q_BLOCKS = 4
ATTN_BWD_LAGS = (2, 4)
ATTN_FWD_LAGS = (2, 4)


def _banded_tiles(kvp_ref, kvc_ref):
    tile = kvc_ref[...].astype(F32)
    blocks = [kvp_ref[...].astype(F32)] + [tile[u * BLOCK:(u + 1) * BLOCK] for u in range(Q_BLOCKS)]
    return [_banded_kv(blocks[u], blocks[u + 1]) for u in range(Q_BLOCKS)]


def _bias_of(bias_ref, i, u, m):
    return bias_ref[jnp.minimum(i, 1) if u == 0 else 1, m]


def _banded_kv(kvp, kvc):
    kw = N_KV_HEADS * HEAD_DIM
    out = []
    for full in (jnp.concatenate([kvp[:, :kw], kvc[:, :kw]], axis=0), jnp.concatenate([kvp[:, kw:], kvc[:, kw:]], axis=0)):
        lo = lax.broadcasted_iota(jnp.int32, full.shape, 1) < HEAD_DIM
        rolled = pltpu.roll(full, HEAD_DIM, 1)
        x2 = [jnp.where(lo, full, rolled).astype(BF16), jnp.where(lo, rolled, full).astype(BF16)]
        ft = full.T
        x2t = [jnp.concatenate([ft[kh * HEAD_DIM:(kh + 1) * HEAD_DIM]] * 2, axis=0).astype(BF16) for kh in range(N_KV_HEADS)]
        out += [x2, x2t]
    return out


def _pair_rows(ref, rows, m, scale=None):
    both = ref[rows, m * LANES:(m + 1) * LANES].astype(F32)
    if scale is not None:
        both = both * scale
    lo = lax.broadcasted_iota(jnp.int32, both.shape, 1) < HEAD_DIM
    zero = jnp.zeros_like(both)
    return jnp.concatenate([jnp.where(lo, both, zero), jnp.where(lo, zero, both)], axis=0).astype(BF16)


def _pair_cols(res_t):
    top = lax.broadcasted_iota(jnp.int32, (LANES, BLOCK), 0) < HEAD_DIM
    return jnp.where(top, res_t[:, :BLOCK], res_t[:, BLOCK:]).T


def _sink_row(sink_ref, m):
    first = lax.broadcasted_iota(jnp.int32, (1, 2 * BLOCK), 1) < BLOCK
    return jnp.where(first, sink_ref[0, 2 * m], sink_ref[0, 2 * m + 1])


def _softmax_t(logits, sink):
    mx =jnp.maximum(jnp.max(logits, axis=0, keepdims=True), sink)
    p = jnp.exp(logits - mx)
    sink_p = jnp.exp(sink - mx)
    inv = 1.0 / (jnp.sum(p, axis=0, keepdims=True) + sink_p)
    return p * inv, sink_p * inv


def _layer_b_fwd(h1, target, kvn, bpre, wkv, wbin_g, biasm, sinks, wbout, bpost):
    seq, d = h1.shape
    kvw = wkv.shape[1]
    cw = wbin_g.shape[2]
    aw = N_Q_HEADS * HEAD_DIM
    per = aw // cw
    tile = Q_BLOCKS * BLOCK

    def body(sink_ref, h1_ref, tgt_ref, kvn_ref, bpre_ref, wkv_ref, wbin_ref, bias_ref, w_ref, g_ref,
             n3_ref, n4_ref, kvc_ref, q_ref, o_ref, dh2_ref, dyb_ref, dattn_ref, dz2_ref, acc_ref,
             attn_ref, z2_ref, kvp_ref):
        i = pl.program_id(0)

        @pl.when(i == 0)
        def _():
            acc_ref[...] = jnp.zeros_like(acc_ref)
            kvp_ref[...] = jnp.zeros_like(kvp_ref)

        hn, _ = _rms(h1_ref[...])
        n3 = (hn * kvn_ref[...]).astype(BF16)
        n4 = (hn * bpre_ref[...]).astype(BF16)
        n3_ref[...] = n3
        n4_ref[...] = n4
        kvc_ref[...] = _dot(n3, wkv_ref[...]).astype(BF16)
        for j in range(N_DEV):
            pj = _dot(n4, wbin_ref[j])
            if j < per:
                q_ref[:, j * cw:(j + 1) * cw] = pj.astype(BF16)
            else:
                z2_ref[:, (j - per) * cw:(j - per + 1) * cw] = pj

        banded = _banded_tiles(kvp_ref, kvc_ref)
        kvp_ref[...] = kvc_ref[tile - BLOCK:tile, :]
        units = [(u, m) for u in range(Q_BLOCKS) for m in range(N_PAIRS)]
        kv_of = lambda m: (2 * m) // GROUP
        logits, probs = {}, {}
        lag_b, lag_c = ATTN_FWD_LAGS
        for step in range(len(units) + lag_c):
            if step < len(units):
                u, m = units[step]
                qpair = _pair_rows(q_ref, slice(u * BLOCK, (u + 1) * BLOCK), m, SCALE)
                logits[step] = _dot_nt(banded[u][0][kv_of(m)], qpair) + _bias_of(bias_ref, i, u, m)
            if 0 <= step - lag_b < len(units):
                u, m = units[step - lag_b]
                probs[step - lag_b] = _softmax_t(logits.pop(step - lag_b), _sink_row(sink_ref, m))[0].astype(BF16)
            if 0 <= step - lag_c < len(units):
                u, m = units[step - lag_c]
                out_t = _dot(banded[u][3][kv_of(m)], probs.pop(step - lag_c))
                attn_ref[u * BLOCK:(u + 1) * BLOCK, m * LANES:(m + 1) * LANES] = _pair_cols(out_t)
        attn = attn_ref[...]
        sz, dsz = _silu(z2_ref[...])
        o = (attn * sz).astype(BF16)
        o_ref[...] = o

        w = w_ref[...]
        yb = _dot(o, w)
        ybn, r = _rms(yb)
        g = g_ref[...]
        diff = h1_ref[...] + ybn * g - tgt_ref[...]
        dh2 = diff * (1.0 / d)
        dh2_ref[...] = dh2
        acc_ref[0:1, :] += jnp.sum(dh2 * ybn, axis=0, keepdims=True)
        tok = jnp.mean(diff * diff, axis=-1, keepdims=True)
        acc_ref[1:2, :] += 0.5 * jnp.sum(tok, axis=0, keepdims=True)
        dyb = _rms_bwd(dh2 * g, ybn, r).astype(BF16)
        dyb_ref[...] = dyb
        do = _dot_nt(dyb, w)
        dattn_ref[...] = (do * sz).astype(BF16)
        dz2_ref[...] = (do * attn * dsz).astype(BF16)

    blk = lambda w: pl.BlockSpec((tile, w), lambda i: (i, 0))
    return pl.pallas_call(
        body,
        name="layer_b_fwd",
        grid=(seq // tile,),
        in_specs=[
            pl.BlockSpec(memory_space=pltpu.SMEM),
            blk(d),
            blk(d),
            _full(kvn.shape),
            _full(bpre.shape),
            _full(wkv.shape),
            _full(wbin_g.shape),
            _full(biasm.shape),
            _full(wbout.shape),
            _full(bpost.shape),
        ],
        out_specs=[blk(d), blk(d), blk(kvw), blk(aw), blk(aw), blk(d), blk(d), blk(aw), blk(aw), _resident((8, d))],
        out_shape=[
            jax.ShapeDtypeStruct((seq, d), BF16),
            jax.ShapeDtypeStruct((seq, d), BF16),
            jax.ShapeDtypeStruct((seq, kvw), BF16),
            jax.ShapeDtypeStruct((seq, aw), BF16),
            jax.ShapeDtypeStruct((seq, aw), BF16),
            jax.ShapeDtypeStruct((seq, d), F32),
            jax.ShapeDtypeStruct((seq, d), BF16),
            jax.ShapeDtypeStruct((seq, aw), BF16),
            jax.ShapeDtypeStruct((seq, aw), BF16),
            jax.ShapeDtypeStruct((8, d), F32),
        ],
        scratch_shapes=[pltpu.VMEM((tile, aw), F32), pltpu.VMEM((tile, aw), F32), pltpu.VMEM((BLOCK, kvw), BF16)],
        compiler_params=_params(("arbitrary",), 56),
    )(sinks, h1, target, kvn, bpre, wkv, wbin_g, biasm, wbout, bpost)


def _attn_bwd(q, kv, dattn, biasm, sinks, ready):
    seq, aw = q.shape
    kvw = kv.shape[1]
    kw = N_KV_HEADS * HEAD_DIM
    nb = seq // BLOCK
    pairs_per_kv = N_PAIRS // N_KV_HEADS
    nr = len(ready)

    tile = Q_BLOCKS * BLOCK
    nsteps = seq // tile
    held = (Q_BLOCKS - 1) * BLOCK

    def body(sink_ref, q_ref, kvc_ref, kvp_ref, da_ref, bias_ref, *refs):
        ready_refs, (dq_ref, dkv_ref, dssum_ref, dsink_ref) = refs[:nr], refs[nr:nr + 4]
        landed_refs, scratch = refs[nr + 4:2 * nr + 4], refs[2 * nr + 4:]
        carry_ref, done_ref, qs_ref, dos_ref, dst_ref, pt_ref, *sems = scratch
        i = pl.program_id(0)

        @pl.when(i == 0)
        def _():
            dssum_ref[...] = jnp.zeros_like(dssum_ref)
            dsink_ref[...] = jnp.zeros_like(dsink_ref)
            carry_ref[...] = jnp.zeros_like(carry_ref)
            done_ref[...] = jnp.zeros_like(done_ref)
            if nr:
                _exchange_start(ready_refs, landed_refs, *sems, True)

        if nr:
            @pl.when(i == nsteps)
            def _():
                _exchange_wait(ready_refs, landed_refs, *sems, True)

        @pl.when(i < nsteps)
        def _():
            lo = lax.broadcasted_iota(jnp.int32, (BAND, LANES), 1) < HEAD_DIM
            head_lane = lax.broadcasted_iota(jnp.int32, (1, LANES), 1)
            banded = _banded_tiles(kvp_ref, kvc_ref)
            units = [(u, m) for u in range(Q_BLOCKS) for m in range(N_PAIRS)]
            dsink = jnp.zeros((1, LANES), F32)
            folded = {}
            logits, dps, dsbs = {}, {}, {}
            lag_b, lag_c = ATTN_BWD_LAGS
            for step in range(len(units) + lag_c):
                if step < len(units):
                    u, m = units[step]
                    kh, rows = m // pairs_per_kv, slice((m % pairs_per_kv) * BAND, (m % pairs_per_kv + 1) * BAND)
                    qrows = slice(u * BLOCK, (u + 1) * BLOCK)
                    qpair = _pair_rows(q_ref, qrows, m, SCALE)
                    dopair = _pair_rows(da_ref, qrows, m)
                    qs_ref[u, kh, rows, :] = qpair
                    dos_ref[u, kh, rows, :] = dopair
                    logits[step] = _dot_nt(banded[u][0][kh], qpair) + _bias_of(bias_ref, i, u, m)
                    dps[step] = _dot_nt(banded[u][2][kh], dopair)
                if 0 <= step - lag_b < len(units):
                    u, m = units[step - lag_b]
                    kh, rows = m // pairs_per_kv, slice((m % pairs_per_kv) * BAND, (m % pairs_per_kv + 1) * BAND)
                    pn, sink_p = _softmax_t(logits.pop(step - lag_b), _sink_row(sink_ref, m))
                    dp = dps.pop(step - lag_b)
                    delta = jnp.sum(pn * dp, axis=0, keepdims=True)
                    ds = pn * (dp - delta)
                    dssum_ref[m] += ds
                    sink_term = sink_p * delta
                    for e in range(2):
                        total = jnp.sum(sink_term[:, e * BLOCK:(e + 1) * BLOCK], axis=1, keepdims=True)
                        dsink = dsink - jnp.where(head_lane == 2 * m + e, total, 0.0)
                    dsbs[step - lag_b] = ds.astype(BF16)
                    dst_ref[u, kh, :, rows] = dsbs[step - lag_b]
                    pt_ref[u, kh, :, rows] = pn.astype(BF16)
                if 0 <= step - lag_c < len(units):
                    u, m = units[step - lag_c]
                    kh = m // pairs_per_kv
                    dq_t = _dot(banded[u][1][kh], dsbs.pop(step - lag_c))
                    dq_ref[u * BLOCK:(u + 1) * BLOCK, m * LANES:(m + 1) * LANES] = (_pair_cols(dq_t) * SCALE).astype(BF16)
                    if m % pairs_per_kv == pairs_per_kv - 1:
                        for name, lhs_ref, rhs_ref in (("k", dst_ref, qs_ref), ("v", pt_ref, dos_ref)):
                            acc = _dot(lhs_ref[u, kh], rhs_ref[u, kh])
                            folded[u, kh, name] = acc + pltpu.roll(acc, HEAD_DIM, 1)
            dsink_ref[0:1, :] += dsink
            dkv = [jnp.concatenate([jnp.where(lo, folded[u, 0, n], folded[u, 1, n]) for n in ("k", "v")], axis=1)
                   for u in range(Q_BLOCKS)]

            @pl.when(i > 0)
            def _():
                if held:
                    dkv_ref[:held, :] = done_ref[...].astype(BF16)
                dkv_ref[held:, :] = (carry_ref[...] + dkv[0][:BLOCK]).astype(BF16)

            for u in range(Q_BLOCKS - 1):
                done_ref[u * BLOCK:(u + 1) * BLOCK, :] = dkv[u][BLOCK:] + dkv[u + 1][:BLOCK]
            carry_ref[...] = dkv[Q_BLOCKS - 1][BLOCK:]

        @pl.when(i == nsteps)
        def _():
            if held:
                dkv_ref[:held, :] = done_ref[...].astype(BF16)
            dkv_ref[held:, :] = carry_ref[...].astype(BF16)

    last = nsteps - 1
    blk = lambda w: pl.BlockSpec((tile, w), lambda i: (jnp.minimum(i, last), 0))
    outs = pl.pallas_call(
        body,
        name="attn_bwd",
        grid=(nsteps + 1,),
        in_specs=[
            pl.BlockSpec(memory_space=pltpu.SMEM),
            blk(aw),
            blk(kvw),
            pl.BlockSpec((BLOCK, kvw), lambda i: (jnp.clip(Q_BLOCKS * i - 1, 0, nb - 1), 0)),
            blk(aw),
            _full(biasm.shape),
        ] + [HBM_SPEC] * nr,
        out_specs=[
            blk(aw),
            pl.BlockSpec((tile, kvw), lambda i: (jnp.maximum(i - 1, 0), 0)),
            _resident(biasm.shape[1:]),
            _resident((8, LANES)),
        ] + [HBM_SPEC] * nr,
        out_shape=[
            jax.ShapeDtypeStruct((seq, aw), BF16),
            jax.ShapeDtypeStruct((seq, kvw), BF16),
            jax.ShapeDtypeStruct(biasm.shape[1:], F32),
            jax.ShapeDtypeStruct((8, LANES), F32),
        ] + [jax.ShapeDtypeStruct(g.shape, g.dtype) for g in ready],
        scratch_shapes=[
            pltpu.VMEM((BLOCK, kvw), F32),
            pltpu.VMEM((max(held, 8), kvw), F32),
            pltpu.VMEM((Q_BLOCKS, N_KV_HEADS, pairs_per_kv * BAND, LANES), BF16),
            pltpu.VMEM((Q_BLOCKS, N_KV_HEADS, pairs_per_kv * BAND, LANES), BF16),
            pltpu.VMEM((Q_BLOCKS, N_KV_HEADS, BAND, pairs_per_kv * BAND), BF16),
            pltpu.VMEM((Q_BLOCKS, N_KV_HEADS, BAND, pairs_per_kv * BAND), BF16),
        ] + _exchange_sems(nr),
        compiler_params=_params(("arbitrary",), 48),
    )(sinks, q, kv, kv, dattn, biasm, *ready)
    return outs[:4], outs[4:]


def _relbias_grad(dssum2, bucket_row, chunk):
    heads, n = dssum2.shape

    def body(a_ref, bucket_ref, out_ref):
        @pl.when(pl.program_id(0) == 0)
        def _():
            out_ref[...] = jnp.zeros_like(out_ref)

        a = a_ref[...]
        hi = a.astype(BF16)
        lo = (a - hi.astype(F32)).astype(BF16)
        onehot_t = (lax.broadcasted_iota(jnp.int32, (LANES, chunk), 0) == bucket_ref[...]).astype(F32).astype(BF16)
        out_ref[...] += _dot_nt(hi, onehot_t) + _dot_nt(lo, onehot_t)

    return pl.pallas_call(
        body,
        name="relbias_grad",
        grid=(n // chunk,),
        in_specs=[pl.BlockSpec((heads, chunk), lambda i: (0, i)), pl.BlockSpec((1, chunk), lambda i: (0, i))],
        out_specs=_resident((heads, LANES)),
        out_shape=jax.ShapeDtypeStruct((heads, LANES), F32),
        compiler_params=_params(("arbitrary",), 32),
    )(dssum2, bucket_row)


def _layer_b_in_bwd(dh2, dq, dz2, dkv, h1, ya, wbin_g, wkv, kvn, bpre, sm, ready, ts):
    seq, d = h1.shape
    aw = dq.shape[1]
    kvw = dkv.shape[1]
    cw = wbin_g.shape[2]
    per = aw // cw

    nr = len(ready)
    nt = seq // ts

    def body(dh2_ref, dq_ref, dz2_ref, dkv_ref, h1_ref, ya_ref, wbin_ref, wkv_ref, kvn_ref, bpre_ref, sm_ref, *refs):
        ready_refs, (dh1_ref, dya_ref, acc_ref) = refs[:nr], refs[nr:nr + 3]
        landed_refs, sems = refs[nr + 3:2 * nr + 3], refs[2 * nr + 3:]

        @pl.when(pl.program_id(0) == 0)
        def _():
            acc_ref[...] = jnp.zeros_like(acc_ref)
            _exchange_start(ready_refs, landed_refs, *sems, True)

        @pl.when(pl.program_id(0) == nt - 1)
        def _():
            _exchange_wait(ready_refs, landed_refs, *sems, True)

        dn4 = jnp.zeros((ts, d), F32)
        for j in range(N_DEV):
            src = dq_ref if j < per else dz2_ref
            jj = j % per
            dn4 = dn4 + _dot_nt(src[:, jj * cw:(jj + 1) * cw], wbin_ref[j])
        dn3 = _dot_nt(dkv_ref[...], wkv_ref[...])
        hn, r = _rms(h1_ref[...])
        acc_ref[0:1, :] += jnp.sum(dn4 * hn, axis=0, keepdims=True)
        acc_ref[1:2, :] += jnp.sum(dn3 * hn, axis=0, keepdims=True)
        dh1 = dh2_ref[...] + _rms_bwd(dn4 * bpre_ref[...] + dn3 * kvn_ref[...], hn, r)
        dh1_ref[...] = dh1
        yan, r2 = _rms(ya_ref[...])
        acc_ref[2:3, :] += jnp.sum(dh1 * yan, axis=0, keepdims=True)
        dya_ref[...] = _rms_bwd(dh1 * sm_ref[4:5, :], yan, r2).astype(BF16)

    outs = pl.pallas_call(
        body,
        name="layer_b_in_bwd",
        grid=(nt,),
        in_specs=[_rows(ts, d), _rows(ts, aw), _rows(ts, aw), _rows(ts, kvw), _rows(ts, d), _rows(ts, d),
                  _full(wbin_g.shape), _full(wkv.shape), _full(kvn.shape), _full(bpre.shape), _full(sm.shape)]
        + [HBM_SPEC] * nr,
        out_specs=[_rows(ts, d), _rows(ts, d), _resident((8, d))] + [HBM_SPEC] * nr,
        out_shape=[jax.ShapeDtypeStruct((seq, d), F32), jax.ShapeDtypeStruct((seq, d), BF16),
                   jax.ShapeDtypeStruct((8, d), F32)] + [jax.ShapeDtypeStruct(g.shape, g.dtype) for g in ready],
        scratch_shapes=_exchange_sems(nr),
        compiler_params=_params(("arbitrary",), 48),
    )(dh2, dq, dz2, dkv, h1, ya, wbin_g, wkv, kvn, bpre, sm, *ready)
    return outs[:3], outs[3:]


def _layer_a_bwd(dya, proj, conv, dh1, x2, wout, win_g, sm, ts):
    seq, d = x2.shape
    width = wout.shape[0]
    half = win_g.shape[2]
    n_half = width // half
    nt = seq // ts

    def body(dya_ref, proj_ref, conv_ref, dh1_ref, x_ref, wout_ref, win_ref, sm_ref, dproj_ref, gx_ref, acc_ref,
             dnext_ref):
        @pl.when(pl.program_id(0) == 0)
        def _():
            acc_ref[...] = jnp.zeros_like(acc_ref)
            dnext_ref[...] = jnp.zeros_like(dnext_ref)

        dy = _dot_nt(dya_ref[...], wout_ref[...])
        row = lax.broadcasted_iota(jnp.int32, (ts, half), 0)
        dn1 = jnp.zeros((ts, d), F32)
        for hh in range(n_half):
            cols = slice(hh * half, (hh + 1) * half)
            b, c, u, z = [proj_ref[:, (part * n_half + hh) * half:(part * n_half + hh + 1) * half].astype(F32)
                          for part in range(4)]
            cv = conv_ref[:, cols].astype(F32)
            dyh = dy[:, cols]
            sz, dsz = _silu(z)
            dconv = dyh * b * sz
            grads = [dyh * cv * sz, None, None, dyh * b * cv * dsz]
            next0, next1 = dnext_ref[0:1, cols], dnext_ref[1:2, cols]
            dc1 = jnp.where(row == ts - 1, next0, pltpu.roll(dconv, ts - 1, 0))
            dc2 = jnp.where(row == ts - 1, next1, jnp.where(row == ts - 2, next0, pltpu.roll(dconv, ts - 2, 0)))
            dnext_ref[:, cols] = dconv[0:8, :]
            v = c * u
            acc_ref[1:2, cols] += jnp.sum(dc2 * v, axis=0, keepdims=True)
            acc_ref[2:3, cols] += jnp.sum(dc1 * v, axis=0, keepdims=True)
            acc_ref[3:4, cols] += jnp.sum(dconv * v, axis=0, keepdims=True)
            dv = sm_ref[3:4, cols] * dconv + sm_ref[2:3, cols] * dc1 + sm_ref[1:2, cols] * dc2
            grads[1] = dv * u
            grads[2] = dv * c
            for part in range(4):
                j = part * n_half + hh
                gj = grads[part].astype(BF16)
                dproj_ref[:, j * half:(j + 1) * half] = gj
                dn1 = dn1 + _dot_nt(gj, win_ref[j])
        xn, r = _rms(x_ref[...])
        acc_ref[0:1, :] += jnp.sum(dn1 * xn, axis=0, keepdims=True)
        gx_ref[...] = dh1_ref[...] + _rms_bwd(dn1 * sm_ref[0:1, :], xn, r)

    rev = lambda w: pl.BlockSpec((ts, w), lambda i: (nt - 1 - i, 0))
    return pl.pallas_call(
        body,
        name="layer_a_bwd",
        grid=(nt,),
        in_specs=[rev(d), rev(4 * width), rev(width), rev(d), rev(d), _full(wout.shape), _full(win_g.shape), _full(sm.shape)],
        out_specs=[rev(4 * width), rev(d), _resident((8, d))],
        out_shape=[jax.ShapeDtypeStruct((seq, 4 * width), BF16), jax.ShapeDtypeStruct((seq, d), F32),
                   jax.ShapeDtypeStruct((8, d), F32)],
        scratch_shapes=[pltpu.VMEM((8, width), F32)],
        compiler_params=_params(("arbitrary",), 56),
    )(dya, proj, conv, dh1, x2, wout, win_g, sm)


def _wgrad(a, bs, n_slots, ts, name, ready=(), block_cols=1024):
    nr = len(ready)
    seq, k = a.shape
    nb_in = len(bs)
    n_each = bs[0].shape[1]
    n = nb_in * n_each
    bn = min(n_each, block_cols)
    per_in = n_each // bn
    n_blocks = nb_in * per_in
    ns = seq // ts

    def b_spec(idx):
        def index(j, s):
            mine = j // per_in == idx
            row = jnp.where(mine, s, jnp.where(j // per_in > idx, ns - 1, 0))
            return (row, jnp.where(mine, j % per_in, jnp.where(j // per_in > idx, per_in - 1, 0)))
        return pl.BlockSpec((ts, bn), index)

    if n_slots:
        sw = n // n_slots
        spb = bn // sw
        out_shape = jax.ShapeDtypeStruct((n_slots, k, sw), BF16)
        out_spec = pl.BlockSpec((spb, k, sw), lambda j, s: (j, 0, 0))
    else:
        out_shape = jax.ShapeDtypeStruct((k, n), BF16)
        out_spec = pl.BlockSpec((k, bn), lambda j, s: (0, j))

    def body(a_ref, *refs):
        b_refs, ready_refs, o_ref = refs[:nb_in], refs[nb_in:nb_in + nr], refs[nb_in + nr]
        landed_refs, (acc_ref, *sems) = refs[nb_in + nr + 1:nb_in + 2 * nr + 1], refs[nb_in + 2 * nr + 1:]
        j, s = pl.program_id(0), pl.program_id(1)

        if nr:
            @pl.when(jnp.logical_and(j == 0, s == 0))
            def _():
                _exchange_start(ready_refs, landed_refs, *sems, True)

            @pl.when(jnp.logical_and(j == n_blocks - 1, s == ns - 1))
            def _():
                _exchange_wait(ready_refs, landed_refs, *sems, True)

        @pl.when(s == 0)
        def _():
            acc_ref[...] = jnp.zeros_like(acc_ref)

        for idx in range(nb_in):
            @pl.when(j // per_in == idx)
            def _(idx=idx):
                acc_ref[...] += _dot_tn(a_ref[...], b_refs[idx][...])

        @pl.when(s == ns - 1)
        def _():
            if n_slots:
                for e in range(spb):
                    o_ref[e] = acc_ref[:, e * sw:(e + 1) * sw].astype(BF16)
            else:
                o_ref[...] = acc_ref[...].astype(BF16)

    outs = pl.pallas_call(
        body,
        name=name,
        grid=(n_blocks, ns),
        in_specs=[pl.BlockSpec((ts, k), lambda j, s: (s, 0))] + [b_spec(idx) for idx in range(nb_in)] + [HBM_SPEC] * nr,
        out_specs=[out_spec] + [HBM_SPEC] * nr,
        out_shape=[out_shape] + [jax.ShapeDtypeStruct(g.shape, g.dtype) for g in ready],
        scratch_shapes=[pltpu.VMEM((k, bn), F32)] + (_exchange_sems(nr) if nr else []),
        compiler_params=_params(("arbitrary", "arbitrary"), 48),
    )(a, *bs, *ready)
    return (outs[0], outs[1:]) if nr else outs[0]


def _wgrad_tail(pairs, part, landed, ts):
    n_tasks = len(pairs)
    nl = len(landed)
    seq, k = pairs[0][0].shape
    n = pairs[0][1].shape[1]
    ns = seq // ts
    total = n_tasks * ns
    per = k // N_DEV

    def spec(t, width):
        return pl.BlockSpec((ts, width), lambda j, s: (jnp.where(j == t, s, jnp.where(j > t, ns - 1, 0)), 0))

    def body(*refs):
        ab_refs, part_ref = refs[:2 * n_tasks], refs[2 * n_tasks]
        landed_refs, refs = refs[2 * n_tasks + 1:2 * n_tasks + 1 + nl], refs[2 * n_tasks + 1 + nl:]
        o_ref, red_ref = refs[:2]
        summed_refs, (acc_ref, sib_ref, chip_ref, send_ref, *sems) = refs[2:2 + nl], refs[2 + nl:]
        j, s = pl.program_id(0), pl.program_id(1)
        flat = j * ns + s
        swap, send, forward, finish = _chip_reduce(part_ref, red_ref, sib_ref, chip_ref, send_ref, sems)

        @pl.when(flat == 0)
        def _():
            swap()

        @pl.when(flat == min(1, total - 1))
        def _():
            send()

        @pl.when(flat == min(total // 2 + 1, total - 1))
        def _():
            forward()
            for t in range(nl):
                _sum_slots(landed_refs[t], summed_refs[t])

        @pl.when(s == 0)
        def _():
            acc_ref[...] = jnp.zeros_like(acc_ref)

        for t in range(n_tasks):
            @pl.when(j == t)
            def _(t=t):
                acc_ref[...] += _dot_tn(ab_refs[2 * t][...], ab_refs[2 * t + 1][...])

        @pl.when(s == ns - 1)
        def _():
            for dev in range(N_DEV):
                o_ref[dev] = acc_ref[dev * per:(dev + 1) * per, :].astype(BF16)

        @pl.when(flat == total - 1)
        def _():
            finish()

    slot = part.shape[1:]
    outs = pl.pallas_call(
        body,
        name="wgrad_tail",
        grid=(n_tasks, ns),
        in_specs=[spec(t, w) for t in range(n_tasks) for w in (k, n)] + [_full(part.shape)]
        + [_full(g.shape) for g in landed],
        out_specs=[pl.BlockSpec((N_DEV, per, n), lambda j, s: (0, j, 0)), _resident(slot)]
        + [_resident(g.shape[1:]) for g in landed],
        out_shape=[jax.ShapeDtypeStruct((N_DEV, n_tasks * per, n), BF16), jax.ShapeDtypeStruct(slot, F32)]
        + [jax.ShapeDtypeStruct(g.shape[1:], F32) for g in landed],
        scratch_shapes=[pltpu.VMEM((k, n), F32)] + _chip_reduce_scratch(slot),
        compiler_params=_params(("arbitrary", "arbitrary"), 58),
    )(*[op for pair in pairs for op in pair], part, *landed)
    return outs[0], outs[1], outs[2:]


def _adamw(ws, gs, ms, vs):
    n = len(ws)

    def step(w, g, m, v):
        m = ADAM_B1 * m + (1.0 - ADAM_B1) * g
        v = ADAM_B2 * v + (1.0 - ADAM_B2) * jnp.square(g)
        m_hat = m / (1.0 - ADAM_B1 ** ADAM_STEP)
        v_hat = v / (1.0 - ADAM_B2 ** ADAM_STEP)
        return g, -ADAM_LR * (m_hat / (jnp.sqrt(v_hat) + ADAM_EPS) + ADAM_WD * w), m, v

    def body(*refs):
        w_refs, g_refs, m_refs, v_refs = (refs[k * n:(k + 1) * n] for k in range(4))
        go_refs, d_refs, nm_refs, nv_refs = (refs[(4 + k) * n:(5 + k) * n] for k in range(4))
        for t in range(n):
            rows = w_refs[t].shape[0]
            if rows <= 128:
                go_refs[t][...], d_refs[t][...], nm_refs[t][...], nv_refs[t][...] = step(
                    w_refs[t][...], g_refs[t][...], m_refs[t][...], v_refs[t][...])
                continue
            chunk = 128

            def one(i, carry, t=t):
                r = pl.ds(pl.multiple_of(i * chunk, chunk), chunk)
                go_refs[t][r, :], d_refs[t][r, :], nm_refs[t][r, :], nv_refs[t][r, :] = step(
                    w_refs[t][r, :], g_refs[t][r, :], m_refs[t][r, :], v_refs[t][r, :])
                return carry

            lax.fori_loop(0, rows // chunk, one, 0)

    vmem = pl.BlockSpec(memory_space=pltpu.VMEM)
    outs = pl.pallas_call(
        body,
        name="adamw",
        in_specs=[vmem] * (4 * n),
        out_specs=[vmem] * (4 * n),
        out_shape=[jax.ShapeDtypeStruct(w.shape, F32) for w in ws] * 4,
        compiler_params=_params(vmem_mib=56),
    )(*ws, *gs, *ms, *vs)
    return outs[:n], outs[n:2 * n], outs[2 * n:3 * n], outs[3 * n:]


def _band_structure():
    q_loc = jnp.arange(BLOCK, dtype=jnp.int32)[:, None]
    s_loc = jnp.arange(2 * BLOCK, dtype=jnp.int32)[None, :]
    dist = q_loc + BLOCK - s_loc
    in_window = (dist >= 0) & (dist < BLOCK)
    dd = jnp.maximum(dist, 0)
    max_exact = N_BUCKETS // 2
    large = max_exact + (jnp.log(jnp.maximum(dd, 1).astype(F32) / max_exact) / math.log(MAX_DISTANCE / max_exact)
                         * (N_BUCKETS - max_exact)).astype(jnp.int32)
    bucket = jnp.where(dd < max_exact, dd, jnp.minimum(large, N_BUCKETS - 1))
    return bucket, in_window.astype(jnp.int32)


def _place_rows(a, row, rows=8):
    return jnp.pad(a, ((row, rows - row - a.shape[0]), (0, 0)))


def kernel(x, a_pre_norm, a_w_in, a_conv_w, a_w_out, a_post_norm, kv_norm, w_kv, rel_bias, b_pre_norm, b_w_in, b_sinks, b_w_out, b_post_norm, loss_target, m_a_pre_norm, m_a_w_in, m_a_conv_w, m_a_w_out, m_a_post_norm, m_kv_norm, m_w_kv, m_rel_bias, m_b_pre_norm, m_b_w_in, m_b_sinks, m_b_w_out, m_b_post_norm, v_a_pre_norm, v_a_w_in, v_a_conv_w, v_a_w_out, v_a_post_norm, v_kv_norm, v_w_kv, v_rel_bias, v_b_pre_norm, v_b_w_in, v_b_sinks, v_b_w_out, v_b_post_norm):
    seq, d = x.shape[1], x.shape[2]
    x2 = x.reshape(seq, d)
    target = loss_target.reshape(seq, d)
    shard = a_pre_norm.shape[1]
    me = _my_index()
    ts_a = min(seq, 512)
    ts = min(seq, 512)
    ts_w = min(seq, 2048)

    small = _place_rows(a_pre_norm, 0) + _place_rows(a_conv_w[0], 1) + _place_rows(a_post_norm, 4)
    bucket, in_window = _band_structure()
    win_g, wout_g, small_g, biasm = _all_gather(
        [a_w_in[0], a_w_out[0], small], [BF16, BF16, F32], rel_bias.T, bucket.T, in_window.T)
    wout = wout_g.reshape(-1, wout_g.shape[2])
    sm = small_g.transpose(1, 0, 2).reshape(8, N_DEV * shard)
    kvn = kv_norm.reshape(1, d)

    (h1, n1, proj, conv, y, ya), (wkv_g, wbin_g, wbout_g) = _layer_a_fwd(
        x2, sm, win_g, wout, [w_kv.astype(BF16), b_w_in[0].astype(BF16), b_w_out[0].astype(BF16)], ts_a)
    wkv = wkv_g.reshape(-1, wkv_g.shape[2])
    wbout = wbout_g.reshape(-1, wbout_g.shape[2])
    n3, n4, kv, q, o, dh2, dyb, dattn, dz2, acc_c = _layer_b_fwd(
        h1, target, kvn, b_pre_norm, wkv, wbin_g, biasm, b_sinks, wbout, b_post_norm)

    (dq, dkv, dssum, dsink), _ = _attn_bwd(q, kv, dattn, biasm, b_sinks, [])
    by_head = dssum.reshape(N_PAIRS, BAND, 2, BLOCK).transpose(0, 2, 3, 1)
    relb = _relbias_grad(by_head.reshape(N_Q_HEADS, -1), bucket.reshape(1, -1), 4096)
    g_wkv = _wgrad(n3, [dkv], 0, ts_w, "wgrad_kv").reshape(wkv_g.shape)
    g_wbin = _wgrad(n4, [dq, dz2], N_DEV, ts_w, "wgrad_b_in")
    (dh1, dya, acc_b), (l_wkv, l_wbin) = _layer_b_in_bwd(
        dh2, dq, dz2, dkv, h1, ya, wbin_g, wkv, kvn, b_pre_norm, sm, [g_wkv, g_wbin], ts)
    dproj, gx, acc_a = _layer_a_bwd(dya, proj, conv, dh1, x2, wout, win_g, sm, ts_a)
    g_win = _wgrad(n1, [dproj], N_DEV, ts_w, "wgrad_a_in", block_cols=2048)
    g_outs, r_win, (r_wkv, r_wbin) = _wgrad_tail([(y, dya), (o, dyb)], g_win, [l_wkv, l_wbin], min(seq, 1024))

    r_outs, _, (s_a, s_b, s_c, s_relb, s_sink) = _reduce_exchange(g_outs, [], [acc_a, acc_b, acc_c, relb, dsink])
    rows_out = wout_g.shape[1]
    r_wout, r_wbout = r_outs[:rows_out], r_outs[rows_out:]
    mine = lambda rows: lax.dynamic_slice_in_dim(rows, me * shard, shard, axis=1)
    loss = s_c[1, 0]
    weights = [a_pre_norm, a_w_in[0], a_conv_w[0], a_w_out[0], a_post_norm, kvn, w_kv, rel_bias.T, b_pre_norm,
               b_w_in[0], b_sinks, b_w_out[0], b_post_norm]
    grads = [mine(s_a[0:1]), r_win, mine(s_a[1:4]), r_wout, mine(s_b[2:3]), s_b[1:2], r_wkv,
             s_relb[:, :N_BUCKETS], s_b[0:1], r_wbin, s_sink[0:1, :N_Q_HEADS], r_wbout, s_c[0:1]]
    first = [m_a_pre_norm, m_a_w_in[0], m_a_conv_w[0], m_a_w_out[0], m_a_post_norm, m_kv_norm.reshape(1, d), m_w_kv,
             m_rel_bias.T, m_b_pre_norm, m_b_w_in[0], m_b_sinks, m_b_w_out[0], m_b_post_norm]
    second = [v_a_pre_norm, v_a_w_in[0], v_a_conv_w[0], v_a_w_out[0], v_a_post_norm, v_kv_norm.reshape(1, d), v_w_kv,
              v_rel_bias.T, v_b_pre_norm, v_b_w_in[0], v_b_sinks, v_b_w_out[0], v_b_post_norm]
    grads, deltas, new_m, new_v = _adamw(weights, grads, first, second)

    shapes = [a_pre_norm.shape, a_w_in.shape, a_conv_w.shape, a_w_out.shape, a_post_norm.shape, kv_norm.shape,
              w_kv.shape, None, b_pre_norm.shape, b_w_in.shape, b_sinks.shape, b_w_out.shape, b_post_norm.shape]
    shaped = lambda arrays: [a.T if s is None else a.reshape(s) for a, s in zip(arrays, shapes)]
    return (loss, gx.reshape(x.shape), *shaped(grads), *shaped(deltas), *shaped(new_m), *shaped(new_v))
```

```python
import math

import jax
import jax.numpy as jnp
from jax import lax
from jax.experimental import pallas as pl
from jax.experimental.pallas import tpu as pltpu

HEAD_DIM = 64
N_Q_HEADS = 16
N_KV_HEADS = 2
GROUP = N_Q_HEADS // N_KV_HEADS
BLOCK = 128
N_BUCKETS = 32
MAX_DISTANCE = 128
EPS = 1e-6
NEG_INF = -1e30
SCALE = HEAD_DIM ** -0.5

ADAM_LR = 0.001
ADAM_B1 = 0.9
ADAM_B2 = 0.999
ADAM_EPS = 1e-08
ADAM_WD = 0.01
ADAM_STEP = 10

N_PAIRS = N_Q_HEADS // 2
BAND = 2 * BLOCK

N_DEV = 8
GATHER_PIECE_ROWS = 128
LANES = 128
F32 = jnp.float32
BF16 = jnp.bfloat16
MESH = pl.DeviceIdType.MESH
MIB = 1024 * 1024


def _params(semantics=None, vmem_mib=48):
    return pltpu.CompilerParams(dimension_semantics=semantics, vmem_limit_bytes=vmem_mib * MIB)


def _full(shape):
    zeros = (0,) * len(shape)
    return pl.BlockSpec(shape, lambda *_: zeros, pipeline_mode=pl.Buffered(1))


def _resident(shape):
    zeros = (0,) * len(shape)
    return pl.BlockSpec(shape, lambda *_: zeros)


def _rows(ts, cols):
    return pl.BlockSpec((ts, cols), lambda i: (i, 0))


def _dot(a, b):
    return jnp.dot(a, b, preferred_element_type=F32)


def _dot_nt(a, b):
    return lax.dot_general(a, b, (((1,), (1,)), ((), ())), preferred_element_type=F32)


def _dot_tn(a, b):
    return lax.dot_general(a, b, (((0,), (0,)), ((), ())), preferred_element_type=F32)


def _rms(xf):
    r = lax.rsqrt(jnp.mean(xf * xf, axis=-1, keepdims=True) + EPS)
    return xf * r, r


def _rms_bwd(dn, xn, r):
    return r * (dn - xn * jnp.mean(dn * xn, axis=-1, keepdims=True))


def _silu(z):
    s = jax.nn.sigmoid(z)
    return z * s, s * (1.0 + z * (1.0 - s))


def _my_index():
    return 4 * lax.axis_index("x") + 2 * lax.axis_index("y") + lax.axis_index("c")


def _bias_table(rb_ref, bucket_ref, win_ref, out_ref):
    bk = jnp.where(win_ref[...] != 0, bucket_ref[...], -1)
    has_prev = lax.broadcasted_iota(jnp.int32, bk.shape, 0) >= BLOCK
    for h in range(N_Q_HEADS):
        acc = jnp.full(bk.shape, NEG_INF, F32)
        for b in range(N_BUCKETS):
            acc = jnp.where(bk == b, rb_ref[h, b], acc)
        cols = slice((h % 2) * BLOCK, (h % 2 + 1) * BLOCK)
        out_ref[1, h // 2, :, cols] = acc
        out_ref[0, h // 2, :, cols] = jnp.where(has_prev, acc, NEG_INF)


def _all_gather(shards, out_dtypes, rel_bias_t, bucket_t, in_window_t):
    n = len(shards)
    pieces = [(t, r0, min(GATHER_PIECE_ROWS, s.shape[0] - r0))
              for t, s in enumerate(shards) for r0 in range(0, s.shape[0], GATHER_PIECE_ROWS)]

    def body(*refs):
        ins, (rb_ref, bucket_ref, win_ref) = refs[:n], refs[n:n + 3]
        outs, bias_ref = refs[n + 3:2 * n + 3], refs[2 * n + 3]
        send_sems, recv_sems = refs[2 * n + 4], refs[2 * n + 5]
        x, y, c = lax.axis_index("x"), lax.axis_index("y"), lax.axis_index("c")
        me, sibling = (x, y, c), (x, y, 1 - c)
        x_nbr, y_nbr, diagonal = (1 - x, y), (x, 1 - y), (1 - x, 1 - y)
        south = c == 0
        relayed = (jnp.where(south, 1 - x, x), jnp.where(south, y, 1 - y))
        relay_to = (jnp.where(south, x, 1 - x), jnp.where(south, 1 - y, y))

        def copy(u, k, block, to):
            t, r0, nrows = pieces[u]
            rows = outs[t].at[4 * block[0] + 2 * block[1] + block[2], pl.ds(r0, nrows)]
            return pltpu.make_async_remote_copy(
                src_ref=rows, dst_ref=rows, send_sem=send_sems.at[u, k], recv_sem=recv_sems.at[u, k],
                device_id=to, device_id_type=MESH)

        for t in range(n):
            outs[t][pl.ds(_my_index(), 1)] = ins[t][...].astype(outs[t].dtype)[None]
        started = []

        def start(cp):
            cp.start()
            started.append(cp)

        units = range(len(pieces))
        for u in units:
            start(copy(u, 0, me, sibling))
            start(copy(u, 1, me, (*x_nbr, c)))
            start(copy(u, 2, me, (*y_nbr, c)))
        _bias_table(rb_ref, bucket_ref, win_ref, bias_ref)
        for u in units:
            for k, chip in ((1, x_nbr), (2, y_nbr)):
                copy(u, k, (*chip, c), me).wait_recv()
                start(copy(u, 3 + k, (*chip, c), sibling))
            start(copy(u, 3, (*relayed, c), (*relay_to, c)))
        for u in units:
            copy(u, 3, (*diagonal, c), me).wait_recv()
            start(copy(u, 6, (*diagonal, c), sibling))
        for u in units:
            copy(u, 0, sibling, me).wait_recv()
        for k, chip in ((4, x_nbr), (5, y_nbr), (6, diagonal)):
            for u in units:
                copy(u, k, (*chip, 1 - c), me).wait_recv()
        for cp in started:
            cp.wait_send()

    vmem = pl.BlockSpec(memory_space=pltpu.VMEM)
    return pl.pallas_call(
        body,
        name="gather_weights",
        out_shape=[jax.ShapeDtypeStruct((N_DEV,) + s.shape, dt) for s, dt in zip(shards, out_dtypes)]
        + [jax.ShapeDtypeStruct((2, N_PAIRS, BAND, 2 * BLOCK), F32)],
        in_specs=[vmem] * n + [pl.BlockSpec(memory_space=pltpu.SMEM), vmem, vmem],
        out_specs=[vmem] * (n + 1),
        scratch_shapes=[pltpu.SemaphoreType.DMA((len(pieces), 7)), pltpu.SemaphoreType.DMA((len(pieces), 7))],
        compiler_params=_params(vmem_mib=48),
    )(*shards, rel_bias_t, bucket_t, in_window_t)


def _peer(k):
    x, y, c = lax.axis_index("x"), lax.axis_index("y"), lax.axis_index("c")
    px = 1 - x if k & 4 else x
    py = 1 - y if k & 2 else y
    pc = 1 - c if k & 1 else c
    return (px, py, pc), 4 * px + 2 * py + pc


def _exchange(srcs, dsts, send_sems, recv_sems, local_sems, scatter):
    me = _my_index()
    sends, arrivals = [], []
    for k in range(1, N_DEV):
        peer, pidx = _peer(k)
        for t, (src, dst) in enumerate(zip(srcs, dsts)):
            mine = src.at[pidx] if scatter else src
            sems = dict(send_sem=send_sems.at[t, k - 1], recv_sem=recv_sems.at[t, k - 1], device_id=peer, device_id_type=MESH)
            sends.append(pltpu.make_async_remote_copy(src_ref=mine, dst_ref=dst.at[me], **sems))
            arrivals.append(pltpu.make_async_remote_copy(src_ref=mine, dst_ref=dst.at[pidx], **sems))
    local = [pltpu.make_async_copy(src.at[me] if scatter else src, dst.at[me], local_sems.at[t])
             for t, (src, dst) in enumerate(zip(srcs, dsts))]
    return sends, arrivals, local


def _exchange_start(*args):
    sends, _, local = _exchange(*args)
    for cp in sends + local:
        cp.start()


def _exchange_wait(*args):
    sends, arrivals, local = _exchange(*args)
    for cp in arrivals:
        cp.wait_recv()
    for cp in sends:
        cp.wait_send()
    for cp in local:
        cp.wait()


def _exchange_sems(n):
    if not n:
        return []
    return [pltpu.SemaphoreType.DMA((n, N_DEV - 1)), pltpu.SemaphoreType.DMA((n, N_DEV - 1)), pltpu.SemaphoreType.DMA((n,))]


HBM_SPEC = pl.BlockSpec(memory_space=pl.ANY)


def _sum_slots(recv_ref, out_ref):
    rows = out_ref.shape[0]
    chunk = min(rows, 128)

    def add(i, carry):
        r0 = pl.multiple_of(i * chunk, chunk)
        acc = recv_ref[0, pl.ds(r0, chunk), :].astype(F32)
        for dev in range(1, N_DEV):
            acc = acc + recv_ref[dev, pl.ds(r0, chunk), :].astype(F32)
        out_ref[pl.ds(r0, chunk), :] = acc
        return carry

    lax.fori_loop(0, rows // chunk, add, 0)


N_CHIPS = N_DEV // 2


def _rows_loop(rows, fn):
    chunk = min(rows, 128)

    def step(i, carry):
        fn(pl.ds(pl.multiple_of(i * chunk, chunk), chunk))
        return carry

    lax.fori_loop(0, rows // chunk, step, 0)


def _chip_reduce(g_ref, out_ref, sib_ref, land_ref, send_ref, sems):
    sib_send, sib_recv, ici_send, ici_recv = sems
    x, y, c = lax.axis_index("x"), lax.axis_index("y"), lax.axis_index("c")
    south = c == 0
    near =(jnp.where(south, 1 - x, x), jnp.where(south, y, 1 - y))
    far = (jnp.where(south, x, 1 - x), jnp.where(south, 1 - y, y))
    diagonal = (1 - x, 1 - y)
    rows = out_ref.shape[0]
    direct, fold, folded = 0, 1, 2

    def to_sibling(t):
        return pltpu.make_async_remote_copy(
            src_ref=g_ref.at[2 * t + 1 - c], dst_ref=sib_ref.at[t], send_sem=sib_send.at[t], recv_sem=sib_recv.at[t],
            device_id=(x, y, 1 - c), device_id_type=MESH)

    def ici(role, chip):
        return pltpu.make_async_remote_copy(
            src_ref=send_ref.at[role], dst_ref=land_ref.at[role], send_sem=ici_send.at[role],
            recv_sem=ici_recv.at[role], device_id=(*chip, c), device_id_type=MESH)

    def pair_sum(chip, r):
        t = 2 * chip[0] + chip[1]
        return g_ref[2 * t + c, r, :].astype(F32) + sib_ref[t, r, :].astype(F32)

    def swap():
        for t in range(N_CHIPS):
            to_sibling(t).start()

    def send():
        for t in range(N_CHIPS):
            to_sibling(t).wait_recv()
        for role, chip in ((fold, diagonal), (direct, near)):
            def fill(r, role=role, chip=chip):
                send_ref[role, r, :] = pair_sum(chip, r).astype(BF16)

            _rows_loop(rows, fill)
            ici(role, near).start()

    def forward():
        ici(fold, near).wait_recv()

        def fill(r):
            send_ref[folded, r, :] = (pair_sum(far, r) + land_ref[fold, r, :].astype(F32)).astype(BF16)

        _rows_loop(rows, fill)
        ici(folded, far).start()

    def finish():
        ici(direct, near).wait_recv()
        ici(folded, far).wait_recv()

        def total(r):
            mine = pair_sum((x, y), r)
            out_ref[r, :] = mine + land_ref[direct, r, :].astype(F32) + land_ref[folded, r, :].astype(F32)

        _rows_loop(rows, total)
        for t in range(N_CHIPS):
            to_sibling(t).wait_send()
        for role, chip in ((direct, near), (fold, near), (folded, far)):
            ici(role, chip).wait_send()

    return swap, send, forward, finish


def _chip_reduce_scratch(slot):
    return [pltpu.VMEM((N_CHIPS,) + slot, BF16), pltpu.VMEM((3,) + slot, BF16), pltpu.VMEM((3,) + slot, BF16),
            pltpu.SemaphoreType.DMA((N_CHIPS,)), pltpu.SemaphoreType.DMA((N_CHIPS,)),
            pltpu.SemaphoreType.DMA((3,)), pltpu.SemaphoreType.DMA((3,))]


def _reduce_exchange(part, landed, smalls):
    nl, ng = len(landed), len(smalls)
    n_out = 1 + nl + ng

    def body(*refs):
        p_in, l_in, s_in = refs[0], refs[1:1 + nl], refs[1 + nl:n_out]
        p_out, l_out, s_out = refs[n_out], refs[n_out + 1:n_out + 1 + nl], refs[n_out + 1 + nl:2 * n_out]
        scratch = refs[2 * n_out:]
        s_recv, (sib_ref, chip_ref, send_ref), sems = scratch[:ng], scratch[ng:ng + 3], scratch[ng + 3:]
        swap, send, forward, finish = _chip_reduce(p_in, p_out, sib_ref, chip_ref, send_ref, sems[:4])
        swap()
        _exchange_start(s_in, s_recv, *sems[4:], False)
        send()
        for t in range(nl):
            _sum_slots(l_in[t], l_out[t])
        forward()
        finish()
        _exchange_wait(s_in, s_recv, *sems[4:], False)
        for t in range(ng):
            acc = s_recv[t][0]
            for dev in range(1, N_DEV):
                acc = acc + s_recv[t][dev]
            s_out[t][...] = acc

    vmem = pl.BlockSpec(memory_space=pltpu.VMEM)
    slot = part.shape[1:]
    outs = pl.pallas_call(
        body,
        name="reduce_grads",
        out_shape=[jax.ShapeDtypeStruct(p.shape[1:], F32) for p in [part] + landed]
        + [jax.ShapeDtypeStruct(s.shape, F32) for s in smalls],
        in_specs=[vmem] * n_out,
        out_specs=[vmem] * n_out,
        scratch_shapes=[pltpu.VMEM((N_DEV,) + s.shape, F32) for s in smalls] + _chip_reduce_scratch(slot)
        + _exchange_sems(ng),
        compiler_params=_params(vmem_mib=56),
    )(part, *landed, *smalls)
    return outs[0], outs[1:1 + nl], outs[1 + nl:]


def _layer_a_fwd(x2, sm, win_g, wout, later, ts):
    seq, d = x2.shape
    width = wout.shape[0]
    half = win_g.shape[2]
    n_half = width // half
    nl = len(later)
    nt = seq // ts

    def body(x_ref, sm_ref, win_ref, wout_ref, *refs):
        shard_refs, refs = refs[:nl], refs[nl:]
        h1_ref, n1_ref, proj_ref, conv_ref, y_ref, ya_ref = refs[:6]
        gathered_refs, (vprev_ref, *sems) = refs[6:6 + nl], refs[6 + nl:]

        @pl.when(pl.program_id(0) == 0)
        def _():
            vprev_ref[...] = jnp.zeros_like(vprev_ref)
            _exchange_start(shard_refs, gathered_refs, *sems, False)

        @pl.when(pl.program_id(0) == nt - 1)
        def _():
            _exchange_wait(shard_refs, gathered_refs, *sems, False)

        xf = x_ref[...]
        xn, _ = _rms(xf)
        n1 = (xn * sm_ref[0:1, :]).astype(BF16)
        n1_ref[...] = n1
        row = lax.broadcasted_iota(jnp.int32, (ts, half), 0)
        ya = jnp.zeros((ts, d), F32)
        for hh in range(n_half):
            cols = slice(hh * half, (hh + 1) * half)
            parts = []
            for part in range(4):
                j = part * n_half + hh
                pj = _dot(n1, win_ref[j])
                proj_ref[:, j * half:(j + 1) * half] = pj.astype(BF16)
                parts.append(pj)
            b, c, u, z = parts
            v = c * u
            last1, last2 = vprev_ref[7:8, cols], vprev_ref[6:7, cols]
            v1 = jnp.where(row == 0, last1, pltpu.roll(v, 1, 0))
            v2 = jnp.where(row == 0, last2, jnp.where(row == 1, last1, pltpu.roll(v, 2, 0)))
            vprev_ref[:, cols] = v[ts - 8:ts, :]
            conv = sm_ref[1:2, cols] * v2 + sm_ref[2:3, cols] * v1 + sm_ref[3:4, cols] * v
            conv_ref[:, cols] = conv.astype(BF16)
            yh = (b * conv * _silu(z)[0]).astype(BF16)
            y_ref[:, cols] = yh
            ya = ya + _dot(yh, wout_ref[cols, :])
        ya_ref[...] = ya
        h1_ref[...] = xf + _rms(ya)[0] * sm_ref[4:5, :]

    outs = pl.pallas_call(
        body,
        name="layer_a_fwd",
        grid=(nt,),
        in_specs=[_rows(ts, d), _full(sm.shape), _full(win_g.shape), _full(wout.shape)] + [HBM_SPEC] * nl,
        out_specs=[_rows(ts, d), _rows(ts, d), _rows(ts, 4 * width), _rows(ts, width), _rows(ts, width), _rows(ts, d)]
        + [HBM_SPEC] * nl,
        out_shape=[
            jax.ShapeDtypeStruct((seq, d), F32),
            jax.ShapeDtypeStruct((seq, d), BF16),
            jax.ShapeDtypeStruct((seq, 4 * width), BF16),
            jax.ShapeDtypeStruct((seq, width), BF16),
            jax.ShapeDtypeStruct((seq, width), BF16),
            jax.ShapeDtypeStruct((seq, d), F32),
        ] + [jax.ShapeDtypeStruct((N_DEV,) + s.shape, s.dtype) for s in later],
        scratch_shapes=[pltpu.VMEM((8, width), F32)] + _exchange_sems(nl),
        compiler_params=_params(("arbitrary",), 56),
    )(x2, sm, win_g, wout, *later)
    return outs[:6], outs[6:]


Q_BLOCKS = 4
ATTN_BWD_LAGS = (2, 4)
ATTN_FWD_LAGS = (2, 4)


def _banded_tiles(kvp_ref, kvc_ref):
    tile = kvc_ref[...].astype(F32)
    blocks = [kvp_ref[...].astype(F32)] + [tile[u * BLOCK:(u + 1) * BLOCK] for u in range(Q_BLOCKS)]
    return [_banded_kv(blocks[u], blocks[u + 1]) for u in range(Q_BLOCKS)]


def _bias_of(bias_ref, i, u, m):
    return bias_ref[jnp.minimum(i, 1) if u == 0 else 1, m]


def _banded_kv(kvp, kvc):
    kw = N_KV_HEADS * HEAD_DIM
    out = []
    for full in (jnp.concatenate([kvp[:, :kw], kvc[:, :kw]], axis=0), jnp.concatenate([kvp[:, kw:], kvc[:, kw:]], axis=0)):
        lo = lax.broadcasted_iota(jnp.int32, full.shape, 1) < HEAD_DIM
        rolled = pltpu.roll(full, HEAD_DIM, 1)
        x2 = [jnp.where(lo, full, rolled).astype(BF16), jnp.where(lo, rolled, full).astype(BF16)]
        ft = full.T
        x2t = [jnp.concatenate([ft[kh * HEAD_DIM:(kh + 1) * HEAD_DIM]] * 2, axis=0).astype(BF16) for kh in range(N_KV_HEADS)]
        out += [x2, x2t]
    return out


def _pair_rows(ref, rows, m, scale=None):
    both = ref[rows, m * LANES:(m + 1) * LANES].astype(F32)
    if scale is not None:
        both = both * scale
    lo = lax.broadcasted_iota(jnp.int32, both.shape, 1) < HEAD_DIM
    zero = jnp.zeros_like(both)
    return jnp.concatenate([jnp.where(lo, both, zero), jnp.where(lo, zero, both)], axis=0).astype(BF16)


def _pair_cols(res_t):
    top = lax.broadcasted_iota(jnp.int32, (LANES, BLOCK), 0) < HEAD_DIM
    return jnp.where(top, res_t[:, :BLOCK], res_t[:, BLOCK:]).T


def _sink_row(sink_ref, m):
    first = lax.broadcasted_iota(jnp.int32, (1, 2 * BLOCK), 1) < BLOCK
    return jnp.where(first, sink_ref[0, 2 * m], sink_ref[0, 2 * m + 1])


def _softmax_t(logits, sink):
    mx =jnp.maximum(jnp.max(logits, axis=0, keepdims=True), sink)
    p = jnp.exp(logits - mx)
    sink_p = jnp.exp(sink - mx)
    inv = 1.0 / (jnp.sum(p, axis=0, keepdims=True) + sink_p)
    return p * inv, sink_p * inv


def _layer_b_fwd(h1, target, kvn, bpre, wkv, wbin_g, biasm, sinks, wbout, bpost):
    seq, d = h1.shape
    kvw = wkv.shape[1]
    cw = wbin_g.shape[2]
    aw = N_Q_HEADS * HEAD_DIM
    per = aw // cw
    tile = Q_BLOCKS * BLOCK

    def body(sink_ref, h1_ref, tgt_ref, kvn_ref, bpre_ref, wkv_ref, wbin_ref, bias_ref, w_ref, g_ref,
             n3_ref, n4_ref, kvc_ref, q_ref, o_ref, dh2_ref, dyb_ref, dattn_ref, dz2_ref, acc_ref,
             attn_ref, z2_ref, kvp_ref):
        i = pl.program_id(0)

        @pl.when(i == 0)
        def _():
            acc_ref[...] = jnp.zeros_like(acc_ref)
            kvp_ref[...] = jnp.zeros_like(kvp_ref)

        hn, _ = _rms(h1_ref[...])
        n3 = (hn * kvn_ref[...]).astype(BF16)
        n4 = (hn * bpre_ref[...]).astype(BF16)
        n3_ref[...] = n3
        n4_ref[...] = n4
        kvc_ref[...] = _dot(n3, wkv_ref[...]).astype(BF16)

        def project(j):
            pj = _dot(n4, wbin_ref[j])
            if j < per:
                q_ref[:, j * cw:(j + 1) * cw] = pj.astype(BF16)
            else:
                z2_ref[:, (j - per) * cw:(j - per + 1) * cw] = pj

        banded = _banded_tiles(kvp_ref, kvc_ref)
        kvp_ref[...] = kvc_ref[tile - BLOCK:tile, :]
        pairs_per_block = cw // LANES
        units = [(u, m) for m in range(N_PAIRS) for u in range(Q_BLOCKS)]
        kv_of = lambda m: (2 * m) // GROUP
        logits, probs = {}, {}
        lag_b, lag_c = ATTN_FWD_LAGS
        for step in range(len(units) + lag_c):
            if step < len(units):
                u, m = units[step]
                if u == 0 and m % pairs_per_block == 0:
                    project(m // pairs_per_block)
                    project(per + m // pairs_per_block)
                qpair = _pair_rows(q_ref, slice(u * BLOCK, (u + 1) * BLOCK), m, SCALE)
                logits[step] = _dot_nt(banded[u][0][kv_of(m)], qpair) + _bias_of(bias_ref, i, u, m)
            if 0 <= step - lag_b < len(units):
                u, m = units[step - lag_b]
                probs[step - lag_b] = _softmax_t(logits.pop(step - lag_b), _sink_row(sink_ref, m))[0].astype(BF16)
            if 0 <= step - lag_c < len(units):
                u, m = units[step - lag_c]
                out_t = _dot(banded[u][3][kv_of(m)], probs.pop(step - lag_c))
                attn_ref[u * BLOCK:(u + 1) * BLOCK, m * LANES:(m + 1) * LANES] = _pair_cols(out_t)
        attn = attn_ref[...]
        sz, dsz = _silu(z2_ref[...])
        o = (attn * sz).astype(BF16)
        o_ref[...] = o

        w = w_ref[...]
        yb = _dot(o, w)
        ybn, r = _rms(yb)
        g = g_ref[...]
        diff = h1_ref[...] + ybn * g - tgt_ref[...]
        dh2 = diff * (1.0 / d)
        dh2_ref[...] = dh2
        acc_ref[0:1, :] += jnp.sum(dh2 * ybn, axis=0, keepdims=True)
        tok = jnp.mean(diff * diff, axis=-1, keepdims=True)
        acc_ref[1:2, :] += 0.5 * jnp.sum(tok, axis=0, keepdims=True)
        dyb = _rms_bwd(dh2 * g, ybn, r).astype(BF16)
        dyb_ref[...] = dyb
        do = _dot_nt(dyb, w)
        dattn_ref[...] = (do * sz).astype(BF16)
        dz2_ref[...] = (do * attn * dsz).astype(BF16)

    blk = lambda w: pl.BlockSpec((tile, w), lambda i: (i, 0))
    return pl.pallas_call(
        body,
        name="layer_b_fwd",
        grid=(seq // tile,),
        in_specs=[
            pl.BlockSpec(memory_space=pltpu.SMEM),
            blk(d),
            blk(d),
            _full(kvn.shape),
            _full(bpre.shape),
            _full(wkv.shape),
            _full(wbin_g.shape),
            _full(biasm.shape),
            _full(wbout.shape),
            _full(bpost.shape),
        ],
        out_specs=[blk(d), blk(d), blk(kvw), blk(aw), blk(aw), blk(d), blk(d), blk(aw), blk(aw), _resident((8, d))],
        out_shape=[
            jax.ShapeDtypeStruct((seq, d), BF16),
            jax.ShapeDtypeStruct((seq, d), BF16),
            jax.ShapeDtypeStruct((seq, kvw), BF16),
            jax.ShapeDtypeStruct((seq, aw), BF16),
            jax.ShapeDtypeStruct((seq, aw), BF16),
            jax.ShapeDtypeStruct((seq, d), F32),
            jax.ShapeDtypeStruct((seq, d), BF16),
            jax.ShapeDtypeStruct((seq, aw), BF16),
            jax.ShapeDtypeStruct((seq, aw), BF16),
            jax.ShapeDtypeStruct((8, d), F32),
        ],
        scratch_shapes=[pltpu.VMEM((tile, aw), F32), pltpu.VMEM((tile, aw), F32), pltpu.VMEM((BLOCK, kvw), BF16)],
        compiler_params=_params(("arbitrary",), 56),
    )(sinks, h1, target, kvn, bpre, wkv, wbin_g, biasm, wbout, bpost)


def _attn_bwd(q, kv, dattn, biasm, sinks, ready):
    seq, aw = q.shape
    kvw = kv.shape[1]
    kw = N_KV_HEADS * HEAD_DIM
    nb = seq // BLOCK
    pairs_per_kv = N_PAIRS // N_KV_HEADS
    nr = len(ready)

    tile = Q_BLOCKS * BLOCK
    nsteps = seq // tile
    held = (Q_BLOCKS - 1) * BLOCK

    def body(sink_ref, q_ref, kvc_ref, kvp_ref, da_ref, bias_ref, *refs):
        ready_refs, (dq_ref, dkv_ref, dssum_ref, dsink_ref) = refs[:nr], refs[nr:nr + 4]
        landed_refs, scratch = refs[nr + 4:2 * nr + 4], refs[2 * nr + 4:]
        carry_ref, done_ref, qs_ref, dos_ref, dst_ref, pt_ref, *sems = scratch
        i = pl.program_id(0)

        @pl.when(i == 0)
        def _():
            dssum_ref[...] = jnp.zeros_like(dssum_ref)
            dsink_ref[...] = jnp.zeros_like(dsink_ref)
            carry_ref[...] = jnp.zeros_like(carry_ref)
            done_ref[...] = jnp.zeros_like(done_ref)
            if nr:
                _exchange_start(ready_refs, landed_refs, *sems, True)

        if nr:
            @pl.when(i == nsteps)
            def _():
                _exchange_wait(ready_refs, landed_refs, *sems, True)

        @pl.when(i < nsteps)
        def _():
            lo = lax.broadcasted_iota(jnp.int32, (BAND, LANES), 1) < HEAD_DIM
            head_lane = lax.broadcasted_iota(jnp.int32, (1, LANES), 1)
            banded = _banded_tiles(kvp_ref, kvc_ref)
            units = [(u, m) for u in range(Q_BLOCKS) for m in range(N_PAIRS)]
            dsink = jnp.zeros((1, LANES), F32)
            folded = {}
            logits, dps, dsbs = {}, {}, {}
            lag_b, lag_c = ATTN_BWD_LAGS
            for step in range(len(units) + lag_c):
                if step < len(units):
                    u, m = units[step]
                    kh, rows = m // pairs_per_kv, slice((m % pairs_per_kv) * BAND, (m % pairs_per_kv + 1) * BAND)
                    qrows = slice(u * BLOCK, (u + 1) * BLOCK)
                    qpair = _pair_rows(q_ref, qrows, m, SCALE)
                    dopair = _pair_rows(da_ref, qrows, m)
                    qs_ref[u, kh, rows, :] = qpair
                    dos_ref[u, kh, rows, :] = dopair
                    logits[step] = _dot_nt(banded[u][0][kh], qpair) + _bias_of(bias_ref, i, u, m)
                    dps[step] = _dot_nt(banded[u][2][kh], dopair)
                if 0 <= step - lag_b < len(units):
                    u, m = units[step - lag_b]
                    kh, rows = m // pairs_per_kv, slice((m % pairs_per_kv) * BAND, (m % pairs_per_kv + 1) * BAND)
                    pn, sink_p = _softmax_t(logits.pop(step - lag_b), _sink_row(sink_ref, m))
                    dp = dps.pop(step - lag_b)
                    delta = jnp.sum(pn * dp, axis=0, keepdims=True)
                    ds = pn * (dp - delta)
                    dssum_ref[m] += ds
                    sink_term = sink_p * delta
                    for e in range(2):
                        total = jnp.sum(sink_term[:, e * BLOCK:(e + 1) * BLOCK], axis=1, keepdims=True)
                        dsink = dsink - jnp.where(head_lane == 2 * m + e, total, 0.0)
                    dsbs[step - lag_b] = ds.astype(BF16)
                    dst_ref[u, kh, :, rows] = dsbs[step - lag_b]
                    pt_ref[u, kh, :, rows] = pn.astype(BF16)
                if 0 <= step - lag_c < len(units):
                    u, m = units[step - lag_c]
                    kh = m // pairs_per_kv
                    dq_t = _dot(banded[u][1][kh], dsbs.pop(step - lag_c))
                    dq_ref[u * BLOCK:(u + 1) * BLOCK, m * LANES:(m + 1) * LANES] = (_pair_cols(dq_t) * SCALE).astype(BF16)
                    if m % pairs_per_kv == pairs_per_kv - 1:
                        for name, lhs_ref, rhs_ref in (("k", dst_ref, qs_ref), ("v", pt_ref, dos_ref)):
                            acc = _dot(lhs_ref[u, kh], rhs_ref[u, kh])
                            folded[u, kh, name] = acc + pltpu.roll(acc, HEAD_DIM, 1)
            dsink_ref[0:1, :] += dsink
            dkv = [jnp.concatenate([jnp.where(lo, folded[u, 0, n], folded[u, 1, n]) for n in ("k", "v")], axis=1)
                   for u in range(Q_BLOCKS)]

            @pl.when(i > 0)
            def _():
                if held:
                    dkv_ref[:held, :] = done_ref[...].astype(BF16)
                dkv_ref[held:, :] = (carry_ref[...] + dkv[0][:BLOCK]).astype(BF16)

            for u in range(Q_BLOCKS - 1):
                done_ref[u * BLOCK:(u + 1) * BLOCK, :] = dkv[u][BLOCK:] + dkv[u + 1][:BLOCK]
            carry_ref[...] = dkv[Q_BLOCKS - 1][BLOCK:]

        @pl.when(i == nsteps)
        def _():
            if held:
                dkv_ref[:held, :] = done_ref[...].astype(BF16)
            dkv_ref[held:, :] = carry_ref[...].astype(BF16)

    last = nsteps - 1
    blk = lambda w: pl.BlockSpec((tile, w), lambda i: (jnp.minimum(i, last), 0))
    outs = pl.pallas_call(
        body,
        name="attn_bwd",
        grid=(nsteps + 1,),
        in_specs=[
            pl.BlockSpec(memory_space=pltpu.SMEM),
            blk(aw),
            blk(kvw),
            pl.BlockSpec((BLOCK, kvw), lambda i: (jnp.clip(Q_BLOCKS * i - 1, 0, nb - 1), 0)),
            blk(aw),
            _full(biasm.shape),
        ] + [HBM_SPEC] * nr,
        out_specs=[
            blk(aw),
            pl.BlockSpec((tile, kvw), lambda i: (jnp.maximum(i - 1, 0), 0)),
            _resident(biasm.shape[1:]),
            _resident((8, LANES)),
        ] + [HBM_SPEC] * nr,
        out_shape=[
            jax.ShapeDtypeStruct((seq, aw), BF16),
            jax.ShapeDtypeStruct((seq, kvw), BF16),
            jax.ShapeDtypeStruct(biasm.shape[1:], F32),
            jax.ShapeDtypeStruct((8, LANES), F32),
        ] + [jax.ShapeDtypeStruct(g.shape, g.dtype) for g in ready],
        scratch_shapes=[
            pltpu.VMEM((BLOCK, kvw), F32),
            pltpu.VMEM((max(held, 8), kvw), F32),
            pltpu.VMEM((Q_BLOCKS, N_KV_HEADS, pairs_per_kv * BAND, LANES), BF16),
            pltpu.VMEM((Q_BLOCKS, N_KV_HEADS, pairs_per_kv * BAND, LANES), BF16),
            pltpu.VMEM((Q_BLOCKS, N_KV_HEADS, BAND, pairs_per_kv * BAND), BF16),
            pltpu.VMEM((Q_BLOCKS, N_KV_HEADS, BAND, pairs_per_kv * BAND), BF16),
        ] + _exchange_sems(nr),
        compiler_params=_params(("arbitrary",), 48),
    )(sinks, q, kv, kv, dattn, biasm, *ready)
    return outs[:4], outs[4:]


def _relbias_grad(dssum2, bucket_row, chunk):
    heads, n = dssum2.shape

    def body(a_ref, bucket_ref, out_ref):
        @pl.when(pl.program_id(0) == 0)
        def _():
            out_ref[...] = jnp.zeros_like(out_ref)

        a = a_ref[...]
        hi = a.astype(BF16)
        lo = (a - hi.astype(F32)).astype(BF16)
        onehot_t = (lax.broadcasted_iota(jnp.int32, (LANES, chunk), 0) == bucket_ref[...]).astype(F32).astype(BF16)
        out_ref[...] += _dot_nt(hi, onehot_t) + _dot_nt(lo, onehot_t)

    return pl.pallas_call(
        body,
        name="relbias_grad",
        grid=(n // chunk,),
        in_specs=[pl.BlockSpec((heads, chunk), lambda i: (0, i)), pl.BlockSpec((1, chunk), lambda i: (0, i))],
        out_specs=_resident((heads, LANES)),
        out_shape=jax.ShapeDtypeStruct((heads, LANES), F32),
        compiler_params=_params(("arbitrary",), 32),
    )(dssum2, bucket_row)


def _layer_b_in_bwd(dh2, dq, dz2, dkv, h1, ya, wbin_g, wkv, kvn, bpre, sm, ready, ts):
    seq, d = h1.shape
    aw = dq.shape[1]
    kvw = dkv.shape[1]
    cw = wbin_g.shape[2]
    per = aw // cw

    nr = len(ready)
    nt = seq // ts

    def body(dh2_ref, dq_ref, dz2_ref, dkv_ref, h1_ref, ya_ref, wbin_ref, wkv_ref, kvn_ref, bpre_ref, sm_ref, *refs):
        ready_refs, (dh1_ref, dya_ref, acc_ref) = refs[:nr], refs[nr:nr + 3]
        landed_refs, sems = refs[nr + 3:2 * nr + 3], refs[2 * nr + 3:]

        @pl.when(pl.program_id(0) == 0)
        def _():
            acc_ref[...] = jnp.zeros_like(acc_ref)
            _exchange_start(ready_refs, landed_refs, *sems, True)

        @pl.when(pl.program_id(0) == nt - 1)
        def _():
            _exchange_wait(ready_refs, landed_refs, *sems, True)

        dn4 = jnp.zeros((ts, d), F32)
        for j in range(N_DEV):
            src = dq_ref if j < per else dz2_ref
            jj = j % per
            dn4 = dn4 + _dot_nt(src[:, jj * cw:(jj + 1) * cw], wbin_ref[j])
        dn3 = _dot_nt(dkv_ref[...], wkv_ref[...])
        hn, r = _rms(h1_ref[...])
        acc_ref[0:1, :] += jnp.sum(dn4 * hn, axis=0, keepdims=True)
        acc_ref[1:2, :] += jnp.sum(dn3 * hn, axis=0, keepdims=True)
        dh1 = dh2_ref[...] + _rms_bwd(dn4 * bpre_ref[...] + dn3 * kvn_ref[...], hn, r)
        dh1_ref[...] = dh1
        yan, r2 = _rms(ya_ref[...])
        acc_ref[2:3, :] += jnp.sum(dh1 * yan, axis=0, keepdims=True)
        dya_ref[...] = _rms_bwd(dh1 * sm_ref[4:5, :], yan, r2).astype(BF16)

    outs = pl.pallas_call(
        body,
        name="layer_b_in_bwd",
        grid=(nt,),
        in_specs=[_rows(ts, d), _rows(ts, aw), _rows(ts, aw), _rows(ts, kvw), _rows(ts, d), _rows(ts, d),
                  _full(wbin_g.shape), _full(wkv.shape), _full(kvn.shape), _full(bpre.shape), _full(sm.shape)]
        + [HBM_SPEC] * nr,
        out_specs=[_rows(ts, d), _rows(ts, d), _resident((8, d))] + [HBM_SPEC] * nr,
        out_shape=[jax.ShapeDtypeStruct((seq, d), F32), jax.ShapeDtypeStruct((seq, d), BF16),
                   jax.ShapeDtypeStruct((8, d), F32)] + [jax.ShapeDtypeStruct(g.shape, g.dtype) for g in ready],
        scratch_shapes=_exchange_sems(nr),
        compiler_params=_params(("arbitrary",), 48),
    )(dh2, dq, dz2, dkv, h1, ya, wbin_g, wkv, kvn, bpre, sm, *ready)
    return outs[:3], outs[3:]


def _layer_a_bwd(dya, proj, conv, dh1, x2, wout, win_g, sm, ts):
    seq, d = x2.shape
    width = wout.shape[0]
    half = win_g.shape[2]
    n_half = width // half
    nt = seq // ts

    def body(dya_ref, proj_ref, conv_ref, dh1_ref, x_ref, wout_ref, win_ref, sm_ref, dproj_ref, gx_ref, acc_ref,
             dnext_ref):
        @pl.when(pl.program_id(0) == 0)
        def _():
            acc_ref[...] = jnp.zeros_like(acc_ref)
            dnext_ref[...] = jnp.zeros_like(dnext_ref)

        dy = _dot_nt(dya_ref[...], wout_ref[...])
        row = lax.broadcasted_iota(jnp.int32, (ts, half), 0)
        dn1 = jnp.zeros((ts, d), F32)
        for hh in range(n_half):
            cols = slice(hh * half, (hh + 1) * half)
            b, c, u, z = [proj_ref[:, (part * n_half + hh) * half:(part * n_half + hh + 1) * half].astype(F32)
                          for part in range(4)]
            cv = conv_ref[:, cols].astype(F32)
            dyh = dy[:, cols]
            sz, dsz = _silu(z)
            dconv = dyh * b * sz
            grads = [dyh * cv * sz, None, None, dyh * b * cv * dsz]
            next0, next1 = dnext_ref[0:1, cols], dnext_ref[1:2, cols]
            dc1 = jnp.where(row == ts - 1, next0, pltpu.roll(dconv, ts - 1, 0))
            dc2 = jnp.where(row == ts - 1, next1, jnp.where(row == ts - 2, next0, pltpu.roll(dconv, ts - 2, 0)))
            dnext_ref[:, cols] = dconv[0:8, :]
            v = c * u
            acc_ref[1:2, cols] += jnp.sum(dc2 * v, axis=0, keepdims=True)
            acc_ref[2:3, cols] += jnp.sum(dc1 * v, axis=0, keepdims=True)
            acc_ref[3:4, cols] += jnp.sum(dconv * v, axis=0, keepdims=True)
            dv = sm_ref[3:4, cols] * dconv + sm_ref[2:3, cols] * dc1 + sm_ref[1:2, cols] * dc2
            grads[1] = dv * u
            grads[2] = dv * c
            for part in range(4):
                j = part * n_half + hh
                gj = grads[part].astype(BF16)
                dproj_ref[:, j * half:(j + 1) * half] = gj
                dn1 = dn1 + _dot_nt(gj, win_ref[j])
        xn, r = _rms(x_ref[...])
        acc_ref[0:1, :] += jnp.sum(dn1 * xn, axis=0, keepdims=True)
        gx_ref[...] = dh1_ref[...] + _rms_bwd(dn1 * sm_ref[0:1, :], xn, r)

    rev = lambda w: pl.BlockSpec((ts, w), lambda i: (nt - 1 - i, 0))
    return pl.pallas_call(
        body,
        name="layer_a_bwd",
        grid=(nt,),
        in_specs=[rev(d), rev(4 * width), rev(width), rev(d), rev(d), _full(wout.shape), _full(win_g.shape), _full(sm.shape)],
        out_specs=[rev(4 * width), rev(d), _resident((8, d))],
        out_shape=[jax.ShapeDtypeStruct((seq, 4 * width), BF16), jax.ShapeDtypeStruct((seq, d), F32),
                   jax.ShapeDtypeStruct((8, d), F32)],
        scratch_shapes=[pltpu.VMEM((8, width), F32)],
        compiler_params=_params(("arbitrary",), 56),
    )(dya, proj, conv, dh1, x2, wout, win_g, sm)


def _wgrad(a, bs, n_slots, ts, name, ready=(), block_cols=1024):
    nr = len(ready)
    seq, k = a.shape
    nb_in = len(bs)
    n_each = bs[0].shape[1]
    n = nb_in * n_each
    bn = min(n_each, block_cols)
    per_in = n_each // bn
    n_blocks = nb_in * per_in
    ns = seq // ts

    def b_spec(idx):
        def index(j, s):
            mine = j // per_in == idx
            row = jnp.where(mine, s, jnp.where(j // per_in > idx, ns - 1, 0))
            return (row, jnp.where(mine, j % per_in, jnp.where(j // per_in > idx, per_in - 1, 0)))
        return pl.BlockSpec((ts, bn), index)

    if n_slots:
        sw = n // n_slots
        spb = bn // sw
        out_shape = jax.ShapeDtypeStruct((n_slots, k, sw), BF16)
        out_spec = pl.BlockSpec((spb, k, sw), lambda j, s: (j, 0, 0))
    else:
        out_shape = jax.ShapeDtypeStruct((k, n), BF16)
        out_spec = pl.BlockSpec((k, bn), lambda j, s: (0, j))

    def body(a_ref, *refs):
        b_refs, ready_refs, o_ref = refs[:nb_in], refs[nb_in:nb_in + nr], refs[nb_in + nr]
        landed_refs, (acc_ref, *sems) = refs[nb_in + nr + 1:nb_in + 2 * nr + 1], refs[nb_in + 2 * nr + 1:]
        j, s = pl.program_id(0), pl.program_id(1)

        if nr:
            @pl.when(jnp.logical_and(j == 0, s == 0))
            def _():
                _exchange_start(ready_refs, landed_refs, *sems, True)

            @pl.when(jnp.logical_and(j == n_blocks - 1, s == ns - 1))
            def _():
                _exchange_wait(ready_refs, landed_refs, *sems, True)

        @pl.when(s == 0)
        def _():
            acc_ref[...] = jnp.zeros_like(acc_ref)

        for idx in range(nb_in):
            @pl.when(j // per_in == idx)
            def _(idx=idx):
                acc_ref[...] += _dot_tn(a_ref[...], b_refs[idx][...])

        @pl.when(s == ns - 1)
        def _():
            if n_slots:
                for e in range(spb):
                    o_ref[e] = acc_ref[:, e * sw:(e + 1) * sw].astype(BF16)
            else:
                o_ref[...] = acc_ref[...].astype(BF16)

    outs = pl.pallas_call(
        body,
        name=name,
        grid=(n_blocks, ns),
        in_specs=[pl.BlockSpec((ts, k), lambda j, s: (s, 0))] + [b_spec(idx) for idx in range(nb_in)] + [HBM_SPEC] * nr,
        out_specs=[out_spec] + [HBM_SPEC] * nr,
        out_shape=[out_shape] + [jax.ShapeDtypeStruct(g.shape, g.dtype) for g in ready],
        scratch_shapes=[pltpu.VMEM((k, bn), F32)] + (_exchange_sems(nr) if nr else []),
        compiler_params=_params(("arbitrary", "arbitrary"), 48),
    )(a, *bs, *ready)
    return (outs[0], outs[1:]) if nr else outs[0]


def _wgrad_tail(pairs, part, landed, ts):
    n_tasks = len(pairs)
    nl = len(landed)
    seq, k = pairs[0][0].shape
    n = pairs[0][1].shape[1]
    ns = seq // ts
    total = n_tasks * ns
    per = k // N_DEV

    def spec(t, width):
        return pl.BlockSpec((ts, width), lambda j, s: (jnp.where(j == t, s, jnp.where(j > t, ns - 1, 0)), 0))

    def body(*refs):
        ab_refs, part_ref = refs[:2 * n_tasks], refs[2 * n_tasks]
        landed_refs, refs = refs[2 * n_tasks + 1:2 * n_tasks + 1 + nl], refs[2 * n_tasks + 1 + nl:]
        o_ref, red_ref = refs[:2]
        summed_refs, (acc_ref, sib_ref, chip_ref, send_ref, *sems) = refs[2:2 + nl], refs[2 + nl:]
        j, s = pl.program_id(0), pl.program_id(1)
        flat = j * ns + s
        swap, send, forward, finish = _chip_reduce(part_ref, red_ref, sib_ref, chip_ref, send_ref, sems)

        @pl.when(flat == 0)
        def _():
            swap()

        @pl.when(flat == min(1, total - 1))
        def _():
            send()

        @pl.when(flat == min(total // 2 + 1, total - 1))
        def _():
            forward()
            for t in range(nl):
                _sum_slots(landed_refs[t], summed_refs[t])

        @pl.when(s == 0)
        def _():
            acc_ref[...] = jnp.zeros_like(acc_ref)

        for t in range(n_tasks):
            @pl.when(j == t)
            def _(t=t):
                acc_ref[...] += _dot_tn(ab_refs[2 * t][...], ab_refs[2 * t + 1][...])

        @pl.when(s == ns - 1)
        def _():
            for dev in range(N_DEV):
                o_ref[dev] = acc_ref[dev * per:(dev + 1) * per, :].astype(BF16)

        @pl.when(flat == total - 1)
        def _():
            finish()

    slot = part.shape[1:]
    outs = pl.pallas_call(
        body,
        name="wgrad_tail",
        grid=(n_tasks, ns),
        in_specs=[spec(t, w) for t in range(n_tasks) for w in (k, n)] + [_full(part.shape)]
        + [_full(g.shape) for g in landed],
        out_specs=[pl.BlockSpec((N_DEV, per, n), lambda j, s: (0, j, 0)), _resident(slot)]
        + [_resident(g.shape[1:]) for g in landed],
        out_shape=[jax.ShapeDtypeStruct((N_DEV, n_tasks * per, n), BF16), jax.ShapeDtypeStruct(slot, F32)]
        + [jax.ShapeDtypeStruct(g.shape[1:], F32) for g in landed],
        scratch_shapes=[pltpu.VMEM((k, n), F32)] + _chip_reduce_scratch(slot),
        compiler_params=_params(("arbitrary", "arbitrary"), 58),
    )(*[op for pair in pairs for op in pair], part, *landed)
    return outs[0], outs[1], outs[2:]


def _adamw(ws, gs, ms, vs):
    n = len(ws)

    def step(w, g, m, v):
        m = ADAM_B1 * m + (1.0 - ADAM_B1) * g
        v = ADAM_B2 * v + (1.0 - ADAM_B2) * jnp.square(g)
        m_hat = m / (1.0 - ADAM_B1 ** ADAM_STEP)
        v_hat = v / (1.0 - ADAM_B2 ** ADAM_STEP)
        return g, -ADAM_LR * (m_hat / (jnp.sqrt(v_hat) + ADAM_EPS) + ADAM_WD * w), m, v

    def body(*refs):
        w_refs, g_refs, m_refs, v_refs = (refs[k * n:(k + 1) * n] for k in range(4))
        go_refs, d_refs, nm_refs, nv_refs = (refs[(4 + k) * n:(5 + k) * n] for k in range(4))
        for t in range(n):
            rows = w_refs[t].shape[0]
            if rows <= 128:
                go_refs[t][...], d_refs[t][...], nm_refs[t][...], nv_refs[t][...] = step(
                    w_refs[t][...], g_refs[t][...], m_refs[t][...], v_refs[t][...])
                continue
            chunk = 128

            def one(i, carry, t=t):
                r = pl.ds(pl.multiple_of(i * chunk, chunk), chunk)
                go_refs[t][r, :], d_refs[t][r, :], nm_refs[t][r, :], nv_refs[t][r, :] = step(
                    w_refs[t][r, :], g_refs[t][r, :], m_refs[t][r, :], v_refs[t][r, :])
                return carry

            lax.fori_loop(0, rows // chunk, one, 0)

    vmem = pl.BlockSpec(memory_space=pltpu.VMEM)
    outs = pl.pallas_call(
        body,
        name="adamw",
        in_specs=[vmem] * (4 * n),
        out_specs=[vmem] * (4 * n),
        out_shape=[jax.ShapeDtypeStruct(w.shape, F32) for w in ws] * 4,
        compiler_params=_params(vmem_mib=56),
    )(*ws, *gs, *ms, *vs)
    return outs[:n], outs[n:2 * n], outs[2 * n:3 * n], outs[3 * n:]


def _band_structure():
    q_loc = jnp.arange(BLOCK, dtype=jnp.int32)[:, None]
    s_loc = jnp.arange(2 * BLOCK, dtype=jnp.int32)[None, :]
    dist = q_loc + BLOCK - s_loc
    in_window = (dist >= 0) & (dist < BLOCK)
    dd = jnp.maximum(dist, 0)
    max_exact = N_BUCKETS // 2
    large = max_exact + (jnp.log(jnp.maximum(dd, 1).astype(F32) / max_exact) / math.log(MAX_DISTANCE / max_exact)
                         * (N_BUCKETS - max_exact)).astype(jnp.int32)
    bucket = jnp.where(dd < max_exact, dd, jnp.minimum(large, N_BUCKETS - 1))
    return bucket, in_window.astype(jnp.int32)


def _place_rows(a, row, rows=8):
    return jnp.pad(a, ((row, rows - row - a.shape[0]), (0, 0)))


def kernel(x, a_pre_norm, a_w_in, a_conv_w, a_w_out, a_post_norm, kv_norm, w_kv, rel_bias, b_pre_norm, b_w_in, b_sinks, b_w_out, b_post_norm, loss_target, m_a_pre_norm, m_a_w_in, m_a_conv_w, m_a_w_out, m_a_post_norm, m_kv_norm, m_w_kv, m_rel_bias, m_b_pre_norm, m_b_w_in, m_b_sinks, m_b_w_out, m_b_post_norm, v_a_pre_norm, v_a_w_in, v_a_conv_w, v_a_w_out, v_a_post_norm, v_kv_norm, v_w_kv, v_rel_bias, v_b_pre_norm, v_b_w_in, v_b_sinks, v_b_w_out, v_b_post_norm):
    seq, d = x.shape[1], x.shape[2]
    x2 = x.reshape(seq, d)
    target = loss_target.reshape(seq, d)
    shard = a_pre_norm.shape[1]
    me = _my_index()
    ts_a = min(seq, 512)
    ts = min(seq, 512)
    ts_w = min(seq, 2048)

    small = _place_rows(a_pre_norm, 0) + _place_rows(a_conv_w[0], 1) + _place_rows(a_post_norm, 4)
    bucket, in_window = _band_structure()
    win_g, wout_g, small_g, biasm = _all_gather(
        [a_w_in[0], a_w_out[0], small], [BF16, BF16, F32], rel_bias.T, bucket.T, in_window.T)
    wout = wout_g.reshape(-1, wout_g.shape[2])
    sm = small_g.transpose(1, 0, 2).reshape(8, N_DEV * shard)
    kvn = kv_norm.reshape(1, d)

    (h1, n1, proj, conv, y, ya), (wkv_g, wbin_g, wbout_g) = _layer_a_fwd(
        x2, sm, win_g, wout, [w_kv.astype(BF16), b_w_in[0].astype(BF16), b_w_out[0].astype(BF16)], ts_a)
    wkv = wkv_g.reshape(-1, wkv_g.shape[2])
    wbout = wbout_g.reshape(-1, wbout_g.shape[2])
    n3, n4, kv, q, o, dh2, dyb, dattn, dz2, acc_c = _layer_b_fwd(
        h1, target, kvn, b_pre_norm, wkv, wbin_g, biasm, b_sinks, wbout, b_post_norm)

    (dq, dkv, dssum, dsink), _ = _attn_bwd(q, kv, dattn, biasm, b_sinks, [])
    by_head = dssum.reshape(N_PAIRS, BAND, 2, BLOCK).transpose(0, 2, 3, 1)
    relb = _relbias_grad(by_head.reshape(N_Q_HEADS, -1), bucket.reshape(1, -1), 4096)
    g_wkv = _wgrad(n3, [dkv], 0, ts_w, "wgrad_kv").reshape(wkv_g.shape)
    g_wbin = _wgrad(n4, [dq, dz2], N_DEV, ts_w, "wgrad_b_in")
    (dh1, dya, acc_b), (l_wkv, l_wbin) = _layer_b_in_bwd(
        dh2, dq, dz2, dkv, h1, ya, wbin_g, wkv, kvn, b_pre_norm, sm, [g_wkv, g_wbin], ts)
    dproj, gx, acc_a = _layer_a_bwd(dya, proj, conv, dh1, x2, wout, win_g, sm, ts_a)
    g_win = _wgrad(n1, [dproj], N_DEV, ts_w, "wgrad_a_in", block_cols=2048)
    g_outs, r_win, (r_wkv, r_wbin) = _wgrad_tail([(y, dya), (o, dyb)], g_win, [l_wkv, l_wbin], min(seq, 1024))

    r_outs, _, (s_a, s_b, s_c, s_relb, s_sink) = _reduce_exchange(g_outs, [], [acc_a, acc_b, acc_c, relb, dsink])
    rows_out = wout_g.shape[1]
    r_wout, r_wbout = r_outs[:rows_out], r_outs[rows_out:]
    mine = lambda rows: lax.dynamic_slice_in_dim(rows, me * shard, shard, axis=1)
    loss = s_c[1, 0]
    weights = [a_pre_norm, a_w_in[0], a_conv_w[0], a_w_out[0], a_post_norm, kvn, w_kv, rel_bias.T, b_pre_norm,
               b_w_in[0], b_sinks, b_w_out[0], b_post_norm]
    grads = [mine(s_a[0:1]), r_win, mine(s_a[1:4]), r_wout, mine(s_b[2:3]), s_b[1:2], r_wkv,
             s_relb[:, :N_BUCKETS], s_b[0:1], r_wbin, s_sink[0:1, :N_Q_HEADS], r_wbout, s_c[0:1]]
    first = [m_a_pre_norm, m_a_w_in[0], m_a_conv_w[0], m_a_w_out[0], m_a_post_norm, m_kv_norm.reshape(1, d), m_w_kv,
             m_rel_bias.T, m_b_pre_norm, m_b_w_in[0], m_b_sinks, m_b_w_out[0], m_b_post_norm]
    second = [v_a_pre_norm, v_a_w_in[0], v_a_conv_w[0], v_a_w_out[0], v_a_post_norm, v_kv_norm.reshape(1, d), v_w_kv,
              v_rel_bias.T, v_b_pre_norm, v_b_w_in[0], v_b_sinks, v_b_w_out[0], v_b_post_norm]
    grads, deltas, new_m, new_v = _adamw(weights, grads, first, second)

    shapes = [a_pre_norm.shape, a_w_in.shape, a_conv_w.shape, a_w_out.shape, a_post_norm.shape, kv_norm.shape,
              w_kv.shape, None, b_pre_norm.shape, b_w_in.shape, b_sinks.shape, b_w_out.shape, b_post_norm.shape]
    shaped = lambda arrays: [a.T if s is None else a.reshape(s) for a, s in zip(arrays, shapes)]
    return (loss, gx.reshape(x.shape), *shaped(grads), *shaped(deltas), *shaped(new_m), *shaped(new_v))
```

```python
import math

import jax
import jax.numpy as jnp
import numpy as np
from jax import lax
from jax.experimental import pallas as pl
from jax.experimental.pallas import tpu as pltpu

HEAD_DIM = 64
N_Q_HEADS = 16
N_KV_HEADS = 2
GROUP = N_Q_HEADS // N_KV_HEADS
BLOCK = 128
N_BUCKETS = 32
MAX_DISTANCE = 128
EPS = 1e-6
NEG_INF = -1e30
SCALE = HEAD_DIM ** -0.5

ADAM_LR = 0.001
ADAM_B1 = 0.9
ADAM_B2 = 0.999
ADAM_EPS = 1e-08
ADAM_WD = 0.01
ADAM_STEP = 10

N_PAIRS = N_Q_HEADS // 2
BAND = 2 * BLOCK

N_DEV = 8
GATHER_PIECE_ROWS = 256
LANES = 128
F32 = jnp.float32
BF16 = jnp.bfloat16
MESH = pl.DeviceIdType.MESH
MIB = 1024 * 1024
VMEM_RESERVED_MIB = 63


def _params(semantics=None, vmem_mib=48):
    del vmem_mib
    return pltpu.CompilerParams(dimension_semantics=semantics, vmem_limit_bytes=VMEM_RESERVED_MIB * MIB)


def _full(shape):
    zeros = (0,) * len(shape)
    return pl.BlockSpec(shape, lambda *_: zeros, pipeline_mode=pl.Buffered(1))


def _resident(shape):
    zeros = (0,) * len(shape)
    return pl.BlockSpec(shape, lambda *_: zeros)


def _rows(ts, cols):
    return pl.BlockSpec((ts, cols), lambda i: (i, 0))


def _dot(a, b):
    return jnp.dot(a, b, preferred_element_type=F32)


def _dot_nt(a, b):
    return lax.dot_general(a, b, (((1,), (1,)), ((), ())), preferred_element_type=F32)


def _dot_tn(a, b):
    return lax.dot_general(a, b, (((0,), (0,)), ((), ())), preferred_element_type=F32)


def _rms(xf):
    r = lax.rsqrt(jnp.mean(xf * xf, axis=-1, keepdims=True) + EPS)
    return xf * r, r


def _rms_bwd(dn, xn, r):
    return r * (dn - xn * jnp.mean(dn * xn, axis=-1, keepdims=True))


def _silu(z):
    s = jax.nn.sigmoid(z)
    return z * s, s * (1.0 + z * (1.0 - s))


def _my_index():
    return 4 * lax.axis_index("x") + 2 * lax.axis_index("y") + lax.axis_index("c")


def _bias_table(rb_ref, bucket_ref, win_ref, out_ref):
    bk = jnp.where(win_ref[...] != 0, bucket_ref[...], -1)
    has_prev = lax.broadcasted_iota(jnp.int32, bk.shape, 0) >= BLOCK
    for h in range(N_Q_HEADS):
        acc = jnp.full(bk.shape, NEG_INF, F32)
        for b in range(N_BUCKETS):
            acc = jnp.where(bk == b, rb_ref[h, b], acc)
        cols = slice((h % 2) * BLOCK, (h % 2 + 1) * BLOCK)
        out_ref[1, h // 2, :, cols] = acc
        out_ref[0, h // 2, :, cols] = jnp.where(has_prev, acc, NEG_INF)


def _all_gather(shards, small_rows, casts, rel_bias_t, bucket_t, in_window_t):
    ns, nc, n = len(small_rows), len(casts), len(shards) + 1
    small_shape = (8, small_rows[0][1].shape[-1])
    shapes = [s.shape for s in shards] + [small_shape]
    pieces = [(t, r0, min(GATHER_PIECE_ROWS, shape[0] - r0))
              for t, shape in enumerate(shapes) for r0 in range(0, shape[0], GATHER_PIECE_ROWS)]

    def body(*refs):
        refs = list(refs)
        take = lambda k: [refs.pop(0) for _ in range(k)]
        ins, small_refs, cast_refs, (rb_ref, bucket_ref, win_ref) = take(n - 1), take(ns), take(nc), take(3)
        outs, cast_outs, (bias_ref, send_sems, recv_sems) = take(n), take(nc), take(3)
        x, y, c = lax.axis_index("x"), lax.axis_index("y"), lax.axis_index("c")
        me, sibling = (x, y, c), (x, y, 1 - c)
        x_nbr, y_nbr, diagonal = (1 - x, y), (x, 1 - y), (1 - x, 1 - y)
        south = c == 0
        relayed = (jnp.where(south, 1 - x, x), jnp.where(south, y, 1 - y))
        relay_to = (jnp.where(south, x, 1 - x), jnp.where(south, 1 - y, y))

        def copy(u, k, block, to):
            t, r0, nrows = pieces[u]
            rows = outs[t].at[4 * block[0] + 2 * block[1] + block[2], pl.ds(r0, nrows)]
            return pltpu.make_async_remote_copy(
                src_ref=rows, dst_ref=rows, send_sem=send_sems.at[u, k], recv_sem=recv_sems.at[u, k],
                device_id=to, device_id_type=MESH)

        mine = pl.ds(_my_index(), 1)
        for t in range(n - 1):
            outs[t][mine] = ins[t][...].astype(BF16)[None]
        outs[n - 1][mine] = jnp.zeros((1,) + small_shape, F32)
        for (row, _), ref in zip(small_rows, small_refs):
            if len(ref.shape) == 3:
                for j in range(ref.shape[0]):
                    outs[n - 1][mine, row + j:row + j + 1, :] = ref[j][None]
            else:
                outs[n - 1][mine, row:row + ref.shape[0], :] = ref[...][None]
        started = []

        def start(cp):
            cp.start()
            started.append(cp)

        units = range(len(pieces))
        for u in units:
            start(copy(u, 0, me, sibling))
            start(copy(u, 1, me, (*x_nbr, c)))
            start(copy(u, 2, me, (*y_nbr, c)))
        for src, dst in zip(cast_refs, cast_outs):
            dst[...] = src[...].astype(BF16)
        _bias_table(rb_ref, bucket_ref, win_ref, bias_ref)
        for u in units:
            for k, chip in ((1, x_nbr), (2, y_nbr)):
                copy(u, k, (*chip, c), me).wait_recv()
                start(copy(u, 3 + k, (*chip, c), sibling))
            start(copy(u, 3, (*relayed, c), (*relay_to, c)))
        for u in units:
            copy(u, 3, (*diagonal, c), me).wait_recv()
            start(copy(u, 6, (*diagonal, c), sibling))
        for u in units:
            copy(u, 0, sibling, me).wait_recv()
        for k, chip in ((4, x_nbr), (5, y_nbr), (6, diagonal)):
            for u in units:
                copy(u, k, (*chip, 1 - c), me).wait_recv()
        for cp in started:
            cp.wait_send()

    vmem = pl.BlockSpec(memory_space=pltpu.VMEM)
    outs = pl.pallas_call(
        body,
        name="gather_weights",
        out_shape=[jax.ShapeDtypeStruct((N_DEV,) + s.shape, BF16) for s in shards]
        + [jax.ShapeDtypeStruct((N_DEV,) + small_shape, F32)]
        + [jax.ShapeDtypeStruct(a.shape, BF16) for a in casts]
        + [jax.ShapeDtypeStruct((2, N_PAIRS, BAND, 2 * BLOCK), F32)],
        in_specs=[vmem] * (n - 1 + ns + nc) + [pl.BlockSpec(memory_space=pltpu.SMEM), vmem, vmem],
        out_specs=[vmem] * (n + nc + 1),
        scratch_shapes=[pltpu.SemaphoreType.DMA((len(pieces), 7)), pltpu.SemaphoreType.DMA((len(pieces), 7))],
        compiler_params=_params(),
    )(*shards, *[a for _, a in small_rows], *casts, rel_bias_t, bucket_t, in_window_t)
    return outs[:n - 1], outs[n - 1], outs[n:n + nc], outs[n + nc]


def _peer(k):
    x, y, c = lax.axis_index("x"), lax.axis_index("y"), lax.axis_index("c")
    px = 1 - x if k & 4 else x
    py = 1 - y if k & 2 else y
    pc = 1 - c if k & 1 else c
    return (px, py, pc), 4 * px + 2 * py + pc


def _exchange(srcs, dsts, send_sems, recv_sems, local_sems, scatter):
    me = _my_index()
    sends, arrivals = [], []
    for k in range(1, N_DEV):
        peer, pidx = _peer(k)
        for t, (src, dst) in enumerate(zip(srcs, dsts)):
            mine = src.at[pidx] if scatter else src
            sems = dict(send_sem=send_sems.at[t, k - 1], recv_sem=recv_sems.at[t, k - 1], device_id=peer, device_id_type=MESH)
            sends.append(pltpu.make_async_remote_copy(src_ref=mine, dst_ref=dst.at[me], **sems))
            arrivals.append(pltpu.make_async_remote_copy(src_ref=mine, dst_ref=dst.at[pidx], **sems))
    local = [pltpu.make_async_copy(src.at[me] if scatter else src, dst.at[me], local_sems.at[t])
             for t, (src, dst) in enumerate(zip(srcs, dsts))]
    return sends, arrivals, local


def _exchange_start(*args):
    sends, _, local = _exchange(*args)
    for cp in sends + local:
        cp.start()


def _exchange_wait(*args):
    sends, arrivals, local = _exchange(*args)
    for cp in arrivals:
        cp.wait_recv()
    for cp in sends:
        cp.wait_send()
    for cp in local:
        cp.wait()


def _exchange_sems(n):
    if not n:
        return []
    return [pltpu.SemaphoreType.DMA((n, N_DEV - 1)), pltpu.SemaphoreType.DMA((n, N_DEV - 1)), pltpu.SemaphoreType.DMA((n,))]


HBM_SPEC = pl.BlockSpec(memory_space=pl.ANY)


def _sum_slots(recv_ref, out_ref):
    rows = out_ref.shape[0]
    chunk = min(rows, 128)

    def add(i, carry):
        r0 = pl.multiple_of(i * chunk, chunk)
        acc = recv_ref[0, pl.ds(r0, chunk), :].astype(F32)
        for dev in range(1, N_DEV):
            acc = acc + recv_ref[dev, pl.ds(r0, chunk), :].astype(F32)
        out_ref[pl.ds(r0, chunk), :] = acc
        return carry

    lax.fori_loop(0, rows // chunk, add, 0)


N_CHIPS = N_DEV // 2


def _rows_loop(rows, fn):
    chunk = min(rows, 128)

    def step(i, carry):
        fn(pl.ds(pl.multiple_of(i * chunk, chunk), chunk))
        return carry

    lax.fori_loop(0, rows // chunk, step, 0)


def _chip_reduce(g_ref, out_ref, sib_ref, land_ref, send_ref, sems):
    sib_send, sib_recv, ici_send, ici_recv = sems
    x, y, c = lax.axis_index("x"), lax.axis_index("y"), lax.axis_index("c")
    south = c == 0
    near =(jnp.where(south, 1 - x, x), jnp.where(south, y, 1 - y))
    far = (jnp.where(south, x, 1 - x), jnp.where(south, 1 - y, y))
    diagonal = (1 - x, 1 - y)
    rows = out_ref.shape[0]
    direct, fold, folded = 0, 1, 2

    def to_sibling(t):
        return pltpu.make_async_remote_copy(
            src_ref=g_ref.at[2 * t + 1 - c], dst_ref=sib_ref.at[t], send_sem=sib_send.at[t], recv_sem=sib_recv.at[t],
            device_id=(x, y, 1 - c), device_id_type=MESH)

    def ici(role, chip):
        return pltpu.make_async_remote_copy(
            src_ref=send_ref.at[role], dst_ref=land_ref.at[role], send_sem=ici_send.at[role],
            recv_sem=ici_recv.at[role], device_id=(*chip, c), device_id_type=MESH)

    def pair_sum(chip, r):
        t = 2 * chip[0] + chip[1]
        return g_ref[2 * t + c, r, :].astype(F32) + sib_ref[t, r, :].astype(F32)

    def swap():
        for t in range(N_CHIPS):
            to_sibling(t).start()

    def send():
        for t in range(N_CHIPS):
            to_sibling(t).wait_recv()
        for role, chip in ((fold, diagonal), (direct, near)):
            def fill(r, role=role, chip=chip):
                send_ref[role, r, :] = pair_sum(chip, r).astype(BF16)

            _rows_loop(rows, fill)
            ici(role, near).start()

    def forward():
        ici(fold, near).wait_recv()

        def fill(r):
            send_ref[folded, r, :] = (pair_sum(far, r) + land_ref[fold, r, :].astype(F32)).astype(BF16)

        _rows_loop(rows, fill)
        ici(folded, far).start()

    def finish():
        ici(direct, near).wait_recv()
        ici(folded, far).wait_recv()

        def total(r):
            mine = pair_sum((x, y), r)
            out_ref[r, :] = mine + land_ref[direct, r, :].astype(F32) + land_ref[folded, r, :].astype(F32)

        _rows_loop(rows, total)
        for t in range(N_CHIPS):
            to_sibling(t).wait_send()
        for role, chip in ((direct, near), (fold, near), (folded, far)):
            ici(role, chip).wait_send()

    return swap, send, forward, finish


def _chip_reduce_scratch(slot):
    return [pltpu.VMEM((N_CHIPS,) + slot, BF16), pltpu.VMEM((3,) + slot, BF16), pltpu.VMEM((3,) + slot, BF16),
            pltpu.SemaphoreType.DMA((N_CHIPS,)), pltpu.SemaphoreType.DMA((N_CHIPS,)),
            pltpu.SemaphoreType.DMA((3,)), pltpu.SemaphoreType.DMA((3,))]


def _reduce_exchange(part, landed, smalls):
    nl, ng = len(landed), len(smalls)
    n_out = 1 + nl + ng

    def body(*refs):
        p_in, l_in, s_in = refs[0], refs[1:1 + nl], refs[1 + nl:n_out]
        p_out, l_out, s_out = refs[n_out], refs[n_out + 1:n_out + 1 + nl], refs[n_out + 1 + nl:2 * n_out]
        scratch = refs[2 * n_out:]
        s_recv, (sib_ref, chip_ref, send_ref), sems = scratch[:ng], scratch[ng:ng + 3], scratch[ng + 3:]
        swap, send, forward, finish = _chip_reduce(p_in, p_out, sib_ref, chip_ref, send_ref, sems[:4])
        swap()
        _exchange_start(s_in, s_recv, *sems[4:], False)
        send()
        for t in range(nl):
            _sum_slots(l_in[t], l_out[t])
        forward()
        finish()
        _exchange_wait(s_in, s_recv, *sems[4:], False)
        for t in range(ng):
            acc = s_recv[t][0]
            for dev in range(1, N_DEV):
                acc = acc + s_recv[t][dev]
            s_out[t][...] = acc

    vmem = pl.BlockSpec(memory_space=pltpu.VMEM)
    slot = part.shape[1:]
    outs = pl.pallas_call(
        body,
        name="reduce_grads",
        out_shape=[jax.ShapeDtypeStruct(p.shape[1:], F32) for p in [part] + landed]
        + [jax.ShapeDtypeStruct(s.shape, F32) for s in smalls],
        in_specs=[vmem] * n_out,
        out_specs=[vmem] * n_out,
        scratch_shapes=[pltpu.VMEM((N_DEV,) + s.shape, F32) for s in smalls] + _chip_reduce_scratch(slot)
        + _exchange_sems(ng),
        compiler_params=_params(vmem_mib=56),
    )(part, *landed, *smalls)
    return outs[0], outs[1:1 + nl], outs[1 + nl:]


def _layer_a_fwd(x2, sm, win_g, wout, later, ts):
    seq, d = x2.shape
    width = wout.shape[0]
    half = win_g.shape[2]
    n_half = width // half
    nl = len(later)
    nt = seq // ts

    def body(x_ref, sm_ref, win_ref, wout_ref, *refs):
        shard_refs, refs = refs[:nl], refs[nl:]
        h1_ref, n1_ref, proj_ref, conv_ref, y_ref, ya_ref = refs[:6]
        gathered_refs, (vprev_ref, *sems) = refs[6:6 + nl], refs[6 + nl:]

        @pl.when(pl.program_id(0) == 0)
        def _():
            vprev_ref[...] = jnp.zeros_like(vprev_ref)
            _exchange_start(shard_refs, gathered_refs, *sems, False)

        @pl.when(pl.program_id(0) == nt - 1)
        def _():
            _exchange_wait(shard_refs, gathered_refs, *sems, False)

        xf = x_ref[...]
        xn, _ = _rms(xf)
        n1 = (xn * sm_ref[0:1, :]).astype(BF16)
        n1_ref[...] = n1
        row = lax.broadcasted_iota(jnp.int32, (ts, half), 0)
        ya = jnp.zeros((ts, d), F32)
        for hh in range(n_half):
            cols = slice(hh * half, (hh + 1) * half)
            parts = []
            for part in range(4):
                j = part * n_half + hh
                pj = _dot(n1, win_ref[j])
                proj_ref[:, j * half:(j + 1) * half] = pj.astype(BF16)
                parts.append(pj)
            b, c, u, z = parts
            v = c * u
            last1, last2 = vprev_ref[7:8, cols], vprev_ref[6:7, cols]
            v1 = jnp.where(row == 0, last1, pltpu.roll(v, 1, 0))
            v2 = jnp.where(row == 0, last2, jnp.where(row == 1, last1, pltpu.roll(v, 2, 0)))
            vprev_ref[:, cols] = v[ts - 8:ts, :]
            conv = sm_ref[1:2, cols] * v2 + sm_ref[2:3, cols] * v1 + sm_ref[3:4, cols] * v
            conv_ref[:, cols] = conv.astype(BF16)
            yh = (b * conv * _silu(z)[0]).astype(BF16)
            y_ref[:, cols] = yh
            ya = ya + _dot(yh, wout_ref[cols, :])
        ya_ref[...] = ya
        h1_ref[...] = xf + _rms(ya)[0] * sm_ref[4:5, :]

    outs = pl.pallas_call(
        body,
        name="layer_a_fwd",
        grid=(nt,),
        in_specs=[_rows(ts, d), _full(sm.shape), _full(win_g.shape), _full(wout.shape)] + [HBM_SPEC] * nl,
        out_specs=[_rows(ts, d), _rows(ts, d), _rows(ts, 4 * width), _rows(ts, width), _rows(ts, width), _rows(ts, d)]
        + [HBM_SPEC] * nl,
        out_shape=[
            jax.ShapeDtypeStruct((seq, d), F32),
            jax.ShapeDtypeStruct((seq, d), BF16),
            jax.ShapeDtypeStruct((seq, 4 * width), BF16),
            jax.ShapeDtypeStruct((seq, width), BF16),
            jax.ShapeDtypeStruct((seq, width), BF16),
            jax.ShapeDtypeStruct((seq, d), F32),
        ] + [jax.ShapeDtypeStruct((N_DEV,) + s.shape, s.dtype) for s in later],
        scratch_shapes=[pltpu.VMEM((8, width), F32)] + _exchange_sems(nl),
        compiler_params=_params(("arbitrary",), 56),
    )(x2, sm, win_g, wout, *later)
    return outs[:6], outs[6:]


Q_BLOCKS = 4
ATTN_BWD_LAGS = (2, 4)
ATTN_FWD_LAGS = (2, 4)


def _banded_tiles(kvp_ref, kvc_ref):
    tile = kvc_ref[...].astype(F32)
    blocks = [kvp_ref[...].astype(F32)] + [tile[u * BLOCK:(u + 1) * BLOCK] for u in range(Q_BLOCKS)]
    return [_banded_kv(blocks[u], blocks[u + 1]) for u in range(Q_BLOCKS)]


def _bias_of(bias_ref, i, u, m):
    return bias_ref[jnp.minimum(i, 1) if u == 0 else 1, m]


def _banded_kv(kvp, kvc):
    kw = N_KV_HEADS * HEAD_DIM
    out = []
    for full in (jnp.concatenate([kvp[:, :kw], kvc[:, :kw]], axis=0), jnp.concatenate([kvp[:, kw:], kvc[:, kw:]], axis=0)):
        lo = lax.broadcasted_iota(jnp.int32, full.shape, 1) < HEAD_DIM
        rolled = pltpu.roll(full, HEAD_DIM, 1)
        x2 = [jnp.where(lo, full, rolled).astype(BF16), jnp.where(lo, rolled, full).astype(BF16)]
        ft = full.T
        x2t = [jnp.concatenate([ft[kh * HEAD_DIM:(kh + 1) * HEAD_DIM]] * 2, axis=0).astype(BF16) for kh in range(N_KV_HEADS)]
        out += [x2, x2t]
    return out


def _pair_rows(ref, rows, m, scale=None):
    both = ref[rows, m * LANES:(m + 1) * LANES].astype(F32)
    if scale is not None:
        both = both * scale
    lo = lax.broadcasted_iota(jnp.int32, both.shape, 1) < HEAD_DIM
    zero = jnp.zeros_like(both)
    return jnp.concatenate([jnp.where(lo, both, zero), jnp.where(lo, zero, both)], axis=0).astype(BF16)


def _pair_cols(res_t):
    top = lax.broadcasted_iota(jnp.int32, (LANES, BLOCK), 0) < HEAD_DIM
    return jnp.where(top, res_t[:, :BLOCK], res_t[:, BLOCK:]).T


def _sink_row(sink_ref, m):
    first = lax.broadcasted_iota(jnp.int32, (1, 2 * BLOCK), 1) < BLOCK
    return jnp.where(first, sink_ref[0, 2 * m], sink_ref[0, 2 * m + 1])


def _softmax_t(logits, sink):
    mx =jnp.maximum(jnp.max(logits, axis=0, keepdims=True), sink)
    p = jnp.exp(logits - mx)
    sink_p = jnp.exp(sink - mx)
    inv = 1.0 / (jnp.sum(p, axis=0, keepdims=True) + sink_p)
    return p * inv, sink_p * inv


def _layer_b_fwd(h1, target, kvn, bpre, wkv, wbin_g, biasm, sinks, wbout, bpost):
    seq, d = h1.shape
    kvw = wkv.shape[1]
    cw = wbin_g.shape[2]
    aw = N_Q_HEADS * HEAD_DIM
    per = aw // cw
    tile = Q_BLOCKS * BLOCK

    def body(sink_ref, h1_ref, tgt_ref, kvn_ref, bpre_ref, wkv_ref, wbin_ref, bias_ref, w_ref, g_ref,
             n3_ref, n4_ref, kvc_ref, q_ref, o_ref, dh2_ref, dyb_ref, dattn_ref, dz2_ref, acc_ref,
             attn_ref, z2_ref, kvp_ref):
        i = pl.program_id(0)

        @pl.when(i == 0)
        def _():
            acc_ref[...] = jnp.zeros_like(acc_ref)
            kvp_ref[...] = jnp.zeros_like(kvp_ref)

        hn, _ = _rms(h1_ref[...])
        n3 = (hn * kvn_ref[...]).astype(BF16)
        n4 = (hn * bpre_ref[...]).astype(BF16)
        n3_ref[...] = n3
        n4_ref[...] = n4
        kvc_ref[...] = _dot(n3, wkv_ref[...]).astype(BF16)
        for j in range(N_DEV):
            pj = _dot(n4, wbin_ref[j])
            if j < per:
                q_ref[:, j * cw:(j + 1) * cw] = pj.astype(BF16)
            else:
                z2_ref[:, (j - per) * cw:(j - per + 1) * cw] = pj

        banded = _banded_tiles(kvp_ref, kvc_ref)
        kvp_ref[...] = kvc_ref[tile - BLOCK:tile, :]
        units = [(u, m) for u in range(Q_BLOCKS) for m in range(N_PAIRS)]
        kv_of = lambda m: (2 * m) // GROUP
        logits, probs = {}, {}
        lag_b, lag_c = ATTN_FWD_LAGS
        for step in range(len(units) + lag_c):
            if step < len(units):
                u, m = units[step]
                qpair = _pair_rows(q_ref, slice(u * BLOCK, (u + 1) * BLOCK), m, SCALE)
                logits[step] = _dot_nt(banded[u][0][kv_of(m)], qpair) + _bias_of(bias_ref, i, u, m)
            if 0 <= step - lag_b < len(units):
                u, m = units[step - lag_b]
                probs[step - lag_b] = _softmax_t(logits.pop(step - lag_b), _sink_row(sink_ref, m))[0].astype(BF16)
            if 0 <= step - lag_c < len(units):
                u, m = units[step - lag_c]
                out_t = _dot(banded[u][3][kv_of(m)], probs.pop(step - lag_c))
                attn_ref[u * BLOCK:(u + 1) * BLOCK, m * LANES:(m + 1) * LANES] = _pair_cols(out_t)
        attn = attn_ref[...]
        sz, dsz = _silu(z2_ref[...])
        o = (attn * sz).astype(BF16)
        o_ref[...] = o

        w = w_ref[...]
        yb = _dot(o, w)
        ybn, r = _rms(yb)
        g = g_ref[...]
        diff = h1_ref[...] + ybn * g - tgt_ref[...]
        dh2 = diff * (1.0 / d)
        dh2_ref[...] = dh2
        acc_ref[0:1, :] += jnp.sum(dh2 * ybn, axis=0, keepdims=True)
        tok = jnp.mean(diff * diff, axis=-1, keepdims=True)
        acc_ref[1:2, :] += 0.5 * jnp.sum(tok, axis=0, keepdims=True)
        dyb = _rms_bwd(dh2 * g, ybn, r).astype(BF16)
        dyb_ref[...] = dyb
        do = _dot_nt(dyb, w)
        dattn_ref[...] = (do * sz).astype(BF16)
        dz2_ref[...] = (do * attn * dsz).astype(BF16)

    blk = lambda w: pl.BlockSpec((tile, w), lambda i: (i, 0))
    return pl.pallas_call(
        body,
        name="layer_b_fwd",
        grid=(seq // tile,),
        in_specs=[
            pl.BlockSpec(memory_space=pltpu.SMEM),
            blk(d),
            blk(d),
            _full(kvn.shape),
            _full(bpre.shape),
            _full(wkv.shape),
            _full(wbin_g.shape),
            _full(biasm.shape),
            _full(wbout.shape),
            _full(bpost.shape),
        ],
        out_specs=[blk(d), blk(d), blk(kvw), blk(aw), blk(aw), blk(d), blk(d), blk(aw), blk(aw), _resident((8, d))],
        out_shape=[
            jax.ShapeDtypeStruct((seq, d), BF16),
            jax.ShapeDtypeStruct((seq, d), BF16),
            jax.ShapeDtypeStruct((seq, kvw), BF16),
            jax.ShapeDtypeStruct((seq, aw), BF16),
            jax.ShapeDtypeStruct((seq, aw), BF16),
            jax.ShapeDtypeStruct((seq, d), F32),
            jax.ShapeDtypeStruct((seq, d), BF16),
            jax.ShapeDtypeStruct((seq, aw), BF16),
            jax.ShapeDtypeStruct((seq, aw), BF16),
            jax.ShapeDtypeStruct((8, d), F32),
        ],
        scratch_shapes=[pltpu.VMEM((tile, aw), F32), pltpu.VMEM((tile, aw), F32), pltpu.VMEM((BLOCK, kvw), BF16)],
        compiler_params=_params(("arbitrary",), 56),
    )(sinks, h1, target, kvn, bpre, wkv, wbin_g, biasm, wbout, bpost)


def _attn_bwd(q, kv, dattn, biasm, sinks, ready):
    seq, aw = q.shape
    kvw = kv.shape[1]
    kw = N_KV_HEADS * HEAD_DIM
    nb = seq // BLOCK
    pairs_per_kv = N_PAIRS // N_KV_HEADS
    nr = len(ready)

    tile = Q_BLOCKS * BLOCK
    nsteps = seq // tile
    held = (Q_BLOCKS - 1) * BLOCK

    def body(sink_ref, q_ref, kvc_ref, kvp_ref, da_ref, bias_ref, *refs):
        ready_refs, (dq_ref, dkv_ref, dssum_ref, dsink_ref) = refs[:nr], refs[nr:nr + 4]
        landed_refs, scratch = refs[nr + 4:2 * nr + 4], refs[2 * nr + 4:]
        carry_ref, done_ref, qs_ref, dos_ref, dst_ref, pt_ref, *sems = scratch
        i = pl.program_id(0)

        @pl.when(i == 0)
        def _():
            dssum_ref[...] = jnp.zeros_like(dssum_ref)
            dsink_ref[...] = jnp.zeros_like(dsink_ref)
            carry_ref[...] = jnp.zeros_like(carry_ref)
            done_ref[...] = jnp.zeros_like(done_ref)
            if nr:
                _exchange_start(ready_refs, landed_refs, *sems, True)

        if nr:
            @pl.when(i == nsteps)
            def _():
                _exchange_wait(ready_refs, landed_refs, *sems, True)

        @pl.when(i < nsteps)
        def _():
            lo = lax.broadcasted_iota(jnp.int32, (BAND, LANES), 1) < HEAD_DIM
            head_lane = lax.broadcasted_iota(jnp.int32, (1, LANES), 1)
            banded = _banded_tiles(kvp_ref, kvc_ref)
            units = [(u, m) for u in range(Q_BLOCKS) for m in range(N_PAIRS)]
            dsink = jnp.zeros((1, LANES), F32)
            folded = {}
            logits, dps, dsbs = {}, {}, {}
            lag_b, lag_c = ATTN_BWD_LAGS
            for step in range(len(units) + lag_c):
                if step < len(units):
                    u, m = units[step]
                    kh, rows = m // pairs_per_kv, slice((m % pairs_per_kv) * BAND, (m % pairs_per_kv + 1) * BAND)
                    qrows = slice(u * BLOCK, (u + 1) * BLOCK)
                    qpair = _pair_rows(q_ref, qrows, m, SCALE)
                    dopair = _pair_rows(da_ref, qrows, m)
                    qs_ref[u, kh, rows, :] = qpair
                    dos_ref[u, kh, rows, :] = dopair
                    logits[step] = _dot_nt(banded[u][0][kh], qpair) + _bias_of(bias_ref, i, u, m)
                    dps[step] = _dot_nt(banded[u][2][kh], dopair)
                if 0 <= step - lag_b < len(units):
                    u, m = units[step - lag_b]
                    kh, rows = m // pairs_per_kv, slice((m % pairs_per_kv) * BAND, (m % pairs_per_kv + 1) * BAND)
                    pn, sink_p = _softmax_t(logits.pop(step - lag_b), _sink_row(sink_ref, m))
                    dp = dps.pop(step - lag_b)
                    delta = jnp.sum(pn * dp, axis=0, keepdims=True)
                    ds = pn * (dp - delta)
                    dssum_ref[m] += ds
                    sink_term = sink_p * delta
                    for e in range(2):
                        total = jnp.sum(sink_term[:, e * BLOCK:(e + 1) * BLOCK], axis=1, keepdims=True)
                        dsink = dsink - jnp.where(head_lane == 2 * m + e, total, 0.0)
                    dsbs[step - lag_b] = ds.astype(BF16)
                    dst_ref[u, kh, :, rows] = dsbs[step - lag_b]
                    pt_ref[u, kh, :, rows] = pn.astype(BF16)
                if 0 <= step - lag_c < len(units):
                    u, m = units[step - lag_c]
                    kh = m // pairs_per_kv
                    dq_t = _dot(banded[u][1][kh], dsbs.pop(step - lag_c))
                    dq_ref[u * BLOCK:(u + 1) * BLOCK, m * LANES:(m + 1) * LANES] = (_pair_cols(dq_t) * SCALE).astype(BF16)
                    if m % pairs_per_kv == pairs_per_kv - 1:
                        for name, lhs_ref, rhs_ref in (("k", dst_ref, qs_ref), ("v", pt_ref, dos_ref)):
                            acc = _dot(lhs_ref[u, kh], rhs_ref[u, kh])
                            folded[u, kh, name] = acc + pltpu.roll(acc, HEAD_DIM, 1)
            dsink_ref[0:1, :] += dsink
            dkv = [jnp.concatenate([jnp.where(lo, folded[u, 0, n], folded[u, 1, n]) for n in ("k", "v")], axis=1)
                   for u in range(Q_BLOCKS)]

            @pl.when(i > 0)
            def _():
                if held:
                    dkv_ref[:held, :] = done_ref[...].astype(BF16)
                dkv_ref[held:, :] = (carry_ref[...] + dkv[0][:BLOCK]).astype(BF16)

            for u in range(Q_BLOCKS - 1):
                done_ref[u * BLOCK:(u + 1) * BLOCK, :] = dkv[u][BLOCK:] + dkv[u + 1][:BLOCK]
            carry_ref[...] = dkv[Q_BLOCKS - 1][BLOCK:]

        @pl.when(i == nsteps)
        def _():
            if held:
                dkv_ref[:held, :] = done_ref[...].astype(BF16)
            dkv_ref[held:, :] = carry_ref[...].astype(BF16)

    last = nsteps - 1
    blk = lambda w: pl.BlockSpec((tile, w), lambda i: (jnp.minimum(i, last), 0))
    outs = pl.pallas_call(
        body,
        name="attn_bwd",
        grid=(nsteps + 1,),
        in_specs=[
            pl.BlockSpec(memory_space=pltpu.SMEM),
            blk(aw),
            blk(kvw),
            pl.BlockSpec((BLOCK, kvw), lambda i: (jnp.clip(Q_BLOCKS * i - 1, 0, nb - 1), 0)),
            blk(aw),
            _full(biasm.shape),
        ] + [HBM_SPEC] * nr,
        out_specs=[
            blk(aw),
            pl.BlockSpec((tile, kvw), lambda i: (jnp.maximum(i - 1, 0), 0)),
            _resident(biasm.shape[1:]),
            _resident((8, LANES)),
        ] + [HBM_SPEC] * nr,
        out_shape=[
            jax.ShapeDtypeStruct((seq, aw), BF16),
            jax.ShapeDtypeStruct((seq, kvw), BF16),
            jax.ShapeDtypeStruct(biasm.shape[1:], F32),
            jax.ShapeDtypeStruct((8, LANES), F32),
        ] + [jax.ShapeDtypeStruct(g.shape, g.dtype) for g in ready],
        scratch_shapes=[
            pltpu.VMEM((BLOCK, kvw), F32),
            pltpu.VMEM((max(held, 8), kvw), F32),
            pltpu.VMEM((Q_BLOCKS, N_KV_HEADS, pairs_per_kv * BAND, LANES), BF16),
            pltpu.VMEM((Q_BLOCKS, N_KV_HEADS, pairs_per_kv * BAND, LANES), BF16),
            pltpu.VMEM((Q_BLOCKS, N_KV_HEADS, BAND, pairs_per_kv * BAND), BF16),
            pltpu.VMEM((Q_BLOCKS, N_KV_HEADS, BAND, pairs_per_kv * BAND), BF16),
        ] + _exchange_sems(nr),
        compiler_params=_params(("arbitrary",), 48),
    )(sinks, q, kv, kv, dattn, biasm, *ready)
    return outs[:4], outs[4:]


def _relbias_grad(dssum2, bucket_row, chunk):
    heads, n = dssum2.shape

    def body(a_ref, bucket_ref, out_ref):
        @pl.when(pl.program_id(0) == 0)
        def _():
            out_ref[...] = jnp.zeros_like(out_ref)

        a = a_ref[...]
        hi = a.astype(BF16)
        lo = (a - hi.astype(F32)).astype(BF16)
        onehot_t = (lax.broadcasted_iota(jnp.int32, (LANES, chunk), 0) == bucket_ref[...]).astype(F32).astype(BF16)
        out_ref[...] += _dot_nt(hi, onehot_t) + _dot_nt(lo, onehot_t)

    return pl.pallas_call(
        body,
        name="relbias_grad",
        grid=(n // chunk,),
        in_specs=[pl.BlockSpec((heads, chunk), lambda i: (0, i)), pl.BlockSpec((1, chunk), lambda i: (0, i))],
        out_specs=_resident((heads, LANES)),
        out_shape=jax.ShapeDtypeStruct((heads, LANES), F32),
        compiler_params=_params(("arbitrary",), 32),
    )(dssum2, bucket_row)


def _layer_b_in_bwd(dh2, dq, dz2, dkv, h1, ya, wbin_g, wkv, kvn, bpre, sm, ready, ts):
    seq, d = h1.shape
    aw = dq.shape[1]
    kvw = dkv.shape[1]
    cw = wbin_g.shape[2]
    per = aw // cw

    nr = len(ready)
    nt = seq // ts

    def body(dh2_ref, dq_ref, dz2_ref, dkv_ref, h1_ref, ya_ref, wbin_ref, wkv_ref, kvn_ref, bpre_ref, sm_ref, *refs):
        ready_refs, (dh1_ref, dya_ref, acc_ref) = refs[:nr], refs[nr:nr + 3]
        landed_refs, sems = refs[nr + 3:2 * nr + 3], refs[2 * nr + 3:]

        @pl.when(pl.program_id(0) == 0)
        def _():
            acc_ref[...] = jnp.zeros_like(acc_ref)
            _exchange_start(ready_refs, landed_refs, *sems, True)

        @pl.when(pl.program_id(0) == nt - 1)
        def _():
            _exchange_wait(ready_refs, landed_refs, *sems, True)

        dn4 = jnp.zeros((ts, d), F32)
        for j in range(N_DEV):
            src = dq_ref if j < per else dz2_ref
            jj = j % per
            dn4 = dn4 + _dot_nt(src[:, jj * cw:(jj + 1) * cw], wbin_ref[j])
        dn3 = _dot_nt(dkv_ref[...], wkv_ref[...])
        hn, r = _rms(h1_ref[...])
        acc_ref[0:1, :] += jnp.sum(dn4 * hn, axis=0, keepdims=True)
        acc_ref[1:2, :] += jnp.sum(dn3 * hn, axis=0, keepdims=True)
        dh1 = dh2_ref[...] + _rms_bwd(dn4 * bpre_ref[...] + dn3 * kvn_ref[...], hn, r)
        dh1_ref[...] = dh1
        yan, r2 = _rms(ya_ref[...])
        acc_ref[2:3, :] += jnp.sum(dh1 * yan, axis=0, keepdims=True)
        dya_ref[...] = _rms_bwd(dh1 * sm_ref[4:5, :], yan, r2).astype(BF16)

    outs = pl.pallas_call(
        body,
        name="layer_b_in_bwd",
        grid=(nt,),
        in_specs=[_rows(ts, d), _rows(ts, aw), _rows(ts, aw), _rows(ts, kvw), _rows(ts, d), _rows(ts, d),
                  _full(wbin_g.shape), _full(wkv.shape), _full(kvn.shape), _full(bpre.shape), _full(sm.shape)]
        + [HBM_SPEC] * nr,
        out_specs=[_rows(ts, d), _rows(ts, d), _resident((8, d))] + [HBM_SPEC] * nr,
        out_shape=[jax.ShapeDtypeStruct((seq, d), F32), jax.ShapeDtypeStruct((seq, d), BF16),
                   jax.ShapeDtypeStruct((8, d), F32)] + [jax.ShapeDtypeStruct(g.shape, g.dtype) for g in ready],
        scratch_shapes=_exchange_sems(nr),
        compiler_params=_params(("arbitrary",), 48),
    )(dh2, dq, dz2, dkv, h1, ya, wbin_g, wkv, kvn, bpre, sm, *ready)
    return outs[:3], outs[3:]


def _layer_a_bwd(dya, proj, conv, dh1, x2, wout, win_g, sm, ts):
    seq, d = x2.shape
    width = wout.shape[0]
    half = win_g.shape[2]
    n_half = width // half
    nt = seq // ts

    def body(dya_ref, proj_ref, conv_ref, dh1_ref, x_ref, wout_ref, win_ref, sm_ref, dproj_ref, gx_ref, acc_ref,
             dnext_ref):
        @pl.when(pl.program_id(0) == 0)
        def _():
            acc_ref[...] = jnp.zeros_like(acc_ref)
            dnext_ref[...] = jnp.zeros_like(dnext_ref)

        dy = _dot_nt(dya_ref[...], wout_ref[...])
        row = lax.broadcasted_iota(jnp.int32, (ts, half), 0)
        dn1 = jnp.zeros((ts, d), F32)
        for hh in range(n_half):
            cols = slice(hh * half, (hh + 1) * half)
            b, c, u, z = [proj_ref[:, (part * n_half + hh) * half:(part * n_half + hh + 1) * half].astype(F32)
                          for part in range(4)]
            cv = conv_ref[:, cols].astype(F32)
            dyh = dy[:, cols]
            sz, dsz = _silu(z)
            dconv = dyh * b * sz
            grads = [dyh * cv * sz, None, None, dyh * b * cv * dsz]
            next0, next1 = dnext_ref[0:1, cols], dnext_ref[1:2, cols]
            dc1 = jnp.where(row == ts - 1, next0, pltpu.roll(dconv, ts - 1, 0))
            dc2 = jnp.where(row == ts - 1, next1, jnp.where(row == ts - 2, next0, pltpu.roll(dconv, ts - 2, 0)))
            dnext_ref[:, cols] = dconv[0:8, :]
            v = c * u
            acc_ref[1:2, cols] += jnp.sum(dc2 * v, axis=0, keepdims=True)
            acc_ref[2:3, cols] += jnp.sum(dc1 * v, axis=0, keepdims=True)
            acc_ref[3:4, cols] += jnp.sum(dconv * v, axis=0, keepdims=True)
            dv = sm_ref[3:4, cols] * dconv + sm_ref[2:3, cols] * dc1 + sm_ref[1:2, cols] * dc2
            grads[1] = dv * u
            grads[2] = dv * c
            for part in range(4):
                j = part * n_half + hh
                gj = grads[part].astype(BF16)
                dproj_ref[:, j * half:(j + 1) * half] = gj
                dn1 = dn1 + _dot_nt(gj, win_ref[j])
        xn, r = _rms(x_ref[...])
        acc_ref[0:1, :] += jnp.sum(dn1 * xn, axis=0, keepdims=True)
        gx_ref[...] = dh1_ref[...] + _rms_bwd(dn1 * sm_ref[0:1, :], xn, r)

    rev = lambda w: pl.BlockSpec((ts, w), lambda i: (nt - 1 - i, 0))
    return pl.pallas_call(
        body,
        name="layer_a_bwd",
        grid=(nt,),
        in_specs=[rev(d), rev(4 * width), rev(width), rev(d), rev(d), _full(wout.shape), _full(win_g.shape), _full(sm.shape)],
        out_specs=[rev(4 * width), rev(d), _resident((8, d))],
        out_shape=[jax.ShapeDtypeStruct((seq, 4 * width), BF16), jax.ShapeDtypeStruct((seq, d), F32),
                   jax.ShapeDtypeStruct((8, d), F32)],
        scratch_shapes=[pltpu.VMEM((8, width), F32)],
        compiler_params=_params(("arbitrary",), 56),
    )(dya, proj, conv, dh1, x2, wout, win_g, sm)


def _wgrad(a, bs, n_slots, ts, name, ready=(), block_cols=1024):
    nr = len(ready)
    seq, k = a.shape
    nb_in = len(bs)
    n_each = bs[0].shape[1]
    n = nb_in * n_each
    bn = min(n_each, block_cols)
    per_in = n_each // bn
    n_blocks = nb_in * per_in
    ns = seq // ts

    def b_spec(idx):
        def index(j, s):
            mine = j // per_in == idx
            row = jnp.where(mine, s, jnp.where(j // per_in > idx, ns - 1, 0))
            return (row, jnp.where(mine, j % per_in, jnp.where(j // per_in > idx, per_in - 1, 0)))
        return pl.BlockSpec((ts, bn), index)

    if n_slots:
        sw = n // n_slots
        spb = bn // sw
        out_shape = jax.ShapeDtypeStruct((n_slots, k, sw), BF16)
        out_spec = pl.BlockSpec((spb, k, sw), lambda j, s: (j, 0, 0))
    else:
        out_shape = jax.ShapeDtypeStruct((k, n), BF16)
        out_spec = pl.BlockSpec((k, bn), lambda j, s: (0, j))

    def body(a_ref, *refs):
        b_refs, ready_refs, o_ref = refs[:nb_in], refs[nb_in:nb_in + nr], refs[nb_in + nr]
        landed_refs, (acc_ref, *sems) = refs[nb_in + nr + 1:nb_in + 2 * nr + 1], refs[nb_in + 2 * nr + 1:]
        j, s = pl.program_id(0), pl.program_id(1)

        if nr:
            @pl.when(jnp.logical_and(j == 0, s == 0))
            def _():
                _exchange_start(ready_refs, landed_refs, *sems, True)

            @pl.when(jnp.logical_and(j == n_blocks - 1, s == ns - 1))
            def _():
                _exchange_wait(ready_refs, landed_refs, *sems, True)

        @pl.when(s == 0)
        def _():
            acc_ref[...] = jnp.zeros_like(acc_ref)

        for idx in range(nb_in):
            @pl.when(j // per_in == idx)
            def _(idx=idx):
                acc_ref[...] += _dot_tn(a_ref[...], b_refs[idx][...])

        @pl.when(s == ns - 1)
        def _():
            if n_slots:
                for e in range(spb):
                    o_ref[e] = acc_ref[:, e * sw:(e + 1) * sw].astype(BF16)
            else:
                o_ref[...] = acc_ref[...].astype(BF16)

    outs = pl.pallas_call(
        body,
        name=name,
        grid=(n_blocks, ns),
        in_specs=[pl.BlockSpec((ts, k), lambda j, s: (s, 0))] + [b_spec(idx) for idx in range(nb_in)] + [HBM_SPEC] * nr,
        out_specs=[out_spec] + [HBM_SPEC] * nr,
        out_shape=[out_shape] + [jax.ShapeDtypeStruct(g.shape, g.dtype) for g in ready],
        scratch_shapes=[pltpu.VMEM((k, bn), F32)] + (_exchange_sems(nr) if nr else []),
        compiler_params=_params(("arbitrary", "arbitrary"), 48),
    )(a, *bs, *ready)
    return (outs[0], outs[1:]) if nr else outs[0]


def _wgrad_tail(pairs, part, landed, ts):
    n_tasks = len(pairs)
    nl = len(landed)
    seq, k = pairs[0][0].shape
    n = pairs[0][1].shape[1]
    ns = seq // ts
    total = n_tasks * ns
    per = k // N_DEV

    def spec(t, width):
        return pl.BlockSpec((ts, width), lambda j, s: (jnp.where(j == t, s, jnp.where(j > t, ns - 1, 0)), 0))

    def body(*refs):
        ab_refs, part_ref = refs[:2 * n_tasks], refs[2 * n_tasks]
        landed_refs, refs = refs[2 * n_tasks + 1:2 * n_tasks + 1 + nl], refs[2 * n_tasks + 1 + nl:]
        o_ref, red_ref = refs[:2]
        summed_refs, (acc_ref, sib_ref, chip_ref, send_ref, *sems) = refs[2:2 + nl], refs[2 + nl:]
        j, s = pl.program_id(0), pl.program_id(1)
        flat = j * ns + s
        swap, send, forward, finish = _chip_reduce(part_ref, red_ref, sib_ref, chip_ref, send_ref, sems)

        @pl.when(flat == 0)
        def _():
            swap()

        @pl.when(flat == min(1, total - 1))
        def _():
            send()

        @pl.when(flat == min(total // 2 + 1, total - 1))
        def _():
            forward()
            for t in range(nl):
                _sum_slots(landed_refs[t], summed_refs[t])

        @pl.when(s == 0)
        def _():
            acc_ref[...] = jnp.zeros_like(acc_ref)

        for t in range(n_tasks):
            @pl.when(j == t)
            def _(t=t):
                acc_ref[...] += _dot_tn(ab_refs[2 * t][...], ab_refs[2 * t + 1][...])

        @pl.when(s == ns - 1)
        def _():
            for dev in range(N_DEV):
                o_ref[dev] = acc_ref[dev * per:(dev + 1) * per, :].astype(BF16)

        @pl.when(flat == total - 1)
        def _():
            finish()

    slot = part.shape[1:]
    outs = pl.pallas_call(
        body,
        name="wgrad_tail",
        grid=(n_tasks, ns),
        in_specs=[spec(t, w) for t in range(n_tasks) for w in (k, n)] + [_full(part.shape)]
        + [_full(g.shape) for g in landed],
        out_specs=[pl.BlockSpec((N_DEV, per, n), lambda j, s: (0, j, 0)), _resident(slot)]
        + [_resident(g.shape[1:]) for g in landed],
        out_shape=[jax.ShapeDtypeStruct((N_DEV, n_tasks * per, n), BF16), jax.ShapeDtypeStruct(slot, F32)]
        + [jax.ShapeDtypeStruct(g.shape[1:], F32) for g in landed],
        scratch_shapes=[pltpu.VMEM((k, n), F32)] + _chip_reduce_scratch(slot),
        compiler_params=_params(("arbitrary", "arbitrary"), 58),
    )(*[op for pair in pairs for op in pair], part, *landed)
    return outs[0], outs[1], outs[2:]


MINE = "mine"


def _adamw(ws, sources, picks, loss_at, ms, vs):
    n, n_src = len(ws), len(sources)

    def step(w, g, m, v):
        m = ADAM_B1 * m + (1.0 - ADAM_B1) * g
        v = ADAM_B2 * v + (1.0 - ADAM_B2) * jnp.square(g)
        m_hat = m / (1.0 - ADAM_B1 ** ADAM_STEP)
        v_hat = v / (1.0 - ADAM_B2 ** ADAM_STEP)
        return g, -ADAM_LR * (m_hat / (jnp.sqrt(v_hat) + ADAM_EPS) + ADAM_WD * w), m, v

    def body(*refs):
        refs = list(refs)
        take = lambda k: [refs.pop(0) for _ in range(k)]
        w_refs, s_refs, m_refs, v_refs = take(n), take(n_src), take(n), take(n)
        (loss_ref,), go_refs, d_refs, nm_refs, nv_refs = take(1), take(n), take(n), take(n), take(n)
        me = _my_index()
        loss_ref[...] = s_refs[loss_at[0]][loss_at[1]:loss_at[1] + 1, 0:1]

        def grad(t, rows):
            k, first, cols = picks[t]
            if cols is None:
                return s_refs[k][rows, :]
            if cols is not MINE:
                return s_refs[k][rows, cols]
            width = w_refs[t].shape[-1]
            g = s_refs[k][rows, 0:width]
            for dev in range(1, N_DEV):
                g = jnp.where(me == dev, s_refs[k][rows, dev * width:(dev + 1) * width], g)
            return g

        for t in range(n):
            first = picks[t][1]
            rows = w_refs[t].shape[0]
            if len(w_refs[t].shape) == 3:
                for j in range(rows):
                    go_refs[t][j], d_refs[t][j], nm_refs[t][j], nv_refs[t][j] = step(
                        w_refs[t][j], grad(t, slice(first + j, first + j + 1)), m_refs[t][j], v_refs[t][j])
                continue
            if rows <= 128:
                go_refs[t][...], d_refs[t][...], nm_refs[t][...], nv_refs[t][...] = step(
                    w_refs[t][...], grad(t, slice(first, first + rows)), m_refs[t][...], v_refs[t][...])
                continue
            chunk = 128

            def one(i, carry, t=t, first=first):
                r = pl.ds(pl.multiple_of(i * chunk, chunk), chunk)
                go_refs[t][r, :], d_refs[t][r, :], nm_refs[t][r, :], nv_refs[t][r, :] = step(
                    w_refs[t][r, :], grad(t, pl.ds(pl.multiple_of(first + i * chunk, chunk), chunk)),
                    m_refs[t][r, :], v_refs[t][r, :])
                return carry

            lax.fori_loop(0, rows // chunk, one, 0)

    vmem = pl.BlockSpec(memory_space=pltpu.VMEM)
    outs = pl.pallas_call(
        body,
        name="adamw",
        in_specs=[vmem] * (3 * n + n_src),
        out_specs=[vmem] * (4 * n + 1),
        out_shape=[jax.ShapeDtypeStruct((1, 1), F32)] + [jax.ShapeDtypeStruct(w.shape, F32) for w in ws] * 4,
        compiler_params=_params(),
    )(*ws, *sources, *ms, *vs)
    return outs[0], outs[1:n + 1], outs[n + 1:2 * n + 1], outs[2 * n + 1:3 * n + 1], outs[3 * n + 1:]


def _band_structure():
    q_loc = np.arange(BLOCK, dtype=np.int32)[:, None]
    s_loc = np.arange(2 * BLOCK, dtype=np.int32)[None, :]
    dist = q_loc + BLOCK - s_loc
    in_window = (dist >= 0) & (dist < BLOCK)
    dd = np.maximum(dist, 0)
    max_exact = N_BUCKETS // 2
    large = max_exact + (np.log(np.maximum(dd, 1) / max_exact) / math.log(MAX_DISTANCE / max_exact)
                         * (N_BUCKETS - max_exact)).astype(np.int32)
    bucket = np.where(dd < max_exact, dd, np.minimum(large, N_BUCKETS - 1)).astype(np.int32)
    return bucket, in_window.astype(np.int32)


def kernel(x, a_pre_norm, a_w_in, a_conv_w, a_w_out, a_post_norm, kv_norm, w_kv, rel_bias, b_pre_norm, b_w_in, b_sinks, b_w_out, b_post_norm, loss_target, m_a_pre_norm, m_a_w_in, m_a_conv_w, m_a_w_out, m_a_post_norm, m_kv_norm, m_w_kv, m_rel_bias, m_b_pre_norm, m_b_w_in, m_b_sinks, m_b_w_out, m_b_post_norm, v_a_pre_norm, v_a_w_in, v_a_conv_w, v_a_w_out, v_a_post_norm, v_kv_norm, v_w_kv, v_rel_bias, v_b_pre_norm, v_b_w_in, v_b_sinks, v_b_w_out, v_b_post_norm):
    seq, d = x.shape[1], x.shape[2]
    x2 = x.reshape(seq, d)
    target = loss_target.reshape(seq, d)
    shard = a_pre_norm.shape[1]
    ts_a = min(seq, 512)
    ts = min(seq, 512)
    ts_w = min(seq, 2048)

    taps = lambda a: a.transpose(1, 0, 2)
    bucket, in_window = _band_structure()
    (win_g, wout_g), small_g, later, biasm = _all_gather(
        [a_w_in[0], a_w_out[0]], [(0, a_pre_norm), (1, taps(a_conv_w)), (4, a_post_norm)],
        [w_kv, b_w_in[0], b_w_out[0]], rel_bias.T, bucket.T, in_window.T)
    wout = wout_g.reshape(-1, wout_g.shape[2])
    sm = small_g.transpose(1, 0, 2).reshape(8, N_DEV * shard)
    kvn = kv_norm.reshape(1, d)

    (h1, n1, proj, conv, y, ya), (wkv_g, wbin_g, wbout_g) = _layer_a_fwd(x2, sm, win_g, wout, later, ts_a)
    wkv = wkv_g.reshape(-1, wkv_g.shape[2])
    wbout = wbout_g.reshape(-1, wbout_g.shape[2])
    n3, n4, kv, q, o, dh2, dyb, dattn, dz2, acc_c = _layer_b_fwd(
        h1, target, kvn, b_pre_norm, wkv, wbin_g, biasm, b_sinks, wbout, b_post_norm)

    (dq, dkv, dssum, dsink), _ = _attn_bwd(q, kv, dattn, biasm, b_sinks, [])
    by_head = dssum.reshape(N_PAIRS, BAND, 2, BLOCK).transpose(0, 2, 3, 1)
    relb = _relbias_grad(by_head.reshape(N_Q_HEADS, -1), bucket.reshape(1, -1), 4096)
    g_wkv = _wgrad(n3, [dkv], 0, ts_w, "wgrad_kv").reshape(wkv_g.shape)
    g_wbin = _wgrad(n4, [dq, dz2], N_DEV, ts_w, "wgrad_b_in")
    (dh1, dya, acc_b), (l_wkv, l_wbin) = _layer_b_in_bwd(
        dh2, dq, dz2, dkv, h1, ya, wbin_g, wkv, kvn, b_pre_norm, sm, [g_wkv, g_wbin], ts)
    dproj, gx, acc_a = _layer_a_bwd(dya, proj, conv, dh1, x2, wout, win_g, sm, ts_a)
    g_win = _wgrad(n1, [dproj], N_DEV, ts_w, "wgrad_a_in", block_cols=2048)
    g_outs, r_win, (r_wkv, r_wbin) = _wgrad_tail([(y, dya), (o, dyb)], g_win, [l_wkv, l_wbin], min(seq, 1024))

    r_outs, _, (s_a, s_b, s_c, s_relb, s_sink) = _reduce_exchange(g_outs, [], [acc_a, acc_b, acc_c, relb, dsink])
    rows_out = wout_g.shape[1]
    weights = [a_pre_norm, a_w_in[0], taps(a_conv_w), a_w_out[0], a_post_norm, kvn, w_kv, rel_bias.T, b_pre_norm,
               b_w_in[0], b_sinks, b_w_out[0], b_post_norm]
    sources = [s_a, s_b, s_c, s_relb, s_sink, r_win, r_outs, r_wkv, r_wbin]
    picks = [(0, 0, MINE), (5, 0, None), (0, 1, MINE), (6, 0, None), (1, 2, MINE), (1, 1, None), (7, 0, None),
             (3, 0, slice(0, N_BUCKETS)), (1, 0, None), (8, 0, None), (4, 0, slice(0, N_Q_HEADS)),
             (6, rows_out, None), (2, 0, None)]
    first = [m_a_pre_norm, m_a_w_in[0], taps(m_a_conv_w), m_a_w_out[0], m_a_post_norm, m_kv_norm.reshape(1, d),
             m_w_kv, m_rel_bias.T, m_b_pre_norm, m_b_w_in[0], m_b_sinks, m_b_w_out[0], m_b_post_norm]
    second = [v_a_pre_norm, v_a_w_in[0], taps(v_a_conv_w), v_a_w_out[0], v_a_post_norm, v_kv_norm.reshape(1, d),
              v_w_kv, v_rel_bias.T, v_b_pre_norm, v_b_w_in[0], v_b_sinks, v_b_w_out[0], v_b_post_norm]
    loss, grads, deltas, new_m, new_v = _adamw(weights, sources, picks, (2, 1), first, second)

    shapes = [a_pre_norm.shape, a_w_in.shape, taps, a_w_out.shape, a_post_norm.shape, kv_norm.shape,
              w_kv.shape, jnp.transpose, b_pre_norm.shape, b_w_in.shape, b_sinks.shape, b_w_out.shape, b_post_norm.shape]
    shaped = lambda arrays: [s(a) if callable(s) else a.reshape(s) for a, s in zip(arrays, shapes)]
    return (loss.reshape(()), gx.reshape(x.shape), *shaped(grads), *shaped(deltas), *shaped(new_m), *shaped(new_v))
```

```python
import math

import jax
import jax.numpy as jnp
import numpy as np
from jax import lax
from jax.experimental import pallas as pl
from jax.experimental.pallas import tpu as pltpu

HEAD_DIM = 64
N_Q_HEADS = 16
N_KV_HEADS = 2
GROUP = N_Q_HEADS // N_KV_HEADS
BLOCK = 128
N_BUCKETS = 32
MAX_DISTANCE = 128
EPS = 1e-6
NEG_INF = -1e30
SCALE = HEAD_DIM ** -0.5

ADAM_LR = 0.001
ADAM_B1 = 0.9
ADAM_B2 = 0.999
ADAM_EPS = 1e-08
ADAM_WD = 0.01
ADAM_STEP = 10

N_PAIRS = N_Q_HEADS // 2
BAND = 2 * BLOCK

N_DEV = 8
GATHER_PIECE_ROWS = 256
LANES = 128
F32 = jnp.float32
BF16 = jnp.bfloat16
MESH = pl.DeviceIdType.MESH
MIB = 1024 * 1024
VMEM_RESERVED_MIB = 63


def _params(semantics=None, vmem_mib=48):
    del vmem_mib
    return pltpu.CompilerParams(dimension_semantics=semantics, vmem_limit_bytes=VMEM_RESERVED_MIB * MIB)


def _full(shape):
    zeros = (0,) * len(shape)
    return pl.BlockSpec(shape, lambda *_: zeros, pipeline_mode=pl.Buffered(1))


def _resident(shape):
    zeros = (0,) * len(shape)
    return pl.BlockSpec(shape, lambda *_: zeros)


def _rows(ts, cols):
    return pl.BlockSpec((ts, cols), lambda i: (i, 0))


def _dot(a, b):
    return jnp.dot(a, b, preferred_element_type=F32)


def _dot_nt(a, b):
    return lax.dot_general(a, b, (((1,), (1,)), ((), ())), preferred_element_type=F32)


def _dot_tn(a, b):
    return lax.dot_general(a, b, (((0,), (0,)), ((), ())), preferred_element_type=F32)


def _rms(xf):
    r = lax.rsqrt(jnp.mean(xf * xf, axis=-1, keepdims=True) + EPS)
    return xf * r, r


def _rms_bwd(dn, xn, r):
    return r * (dn - xn * jnp.mean(dn * xn, axis=-1, keepdims=True))


def _silu(z):
    s = jax.nn.sigmoid(z)
    return z * s, s * (1.0 + z * (1.0 - s))


def _my_index():
    return 4 * lax.axis_index("x") + 2 * lax.axis_index("y") + lax.axis_index("c")


def _bias_table(rb_ref, bucket_ref, win_ref, out_ref):
    bk = jnp.where(win_ref[...] != 0, bucket_ref[...], -1)
    has_prev = lax.broadcasted_iota(jnp.int32, bk.shape, 0) >= BLOCK
    for h in range(N_Q_HEADS):
        acc = jnp.full(bk.shape, NEG_INF, F32)
        for b in range(N_BUCKETS):
            acc = jnp.where(bk == b, rb_ref[h, b], acc)
        cols = slice((h % 2) * BLOCK, (h % 2 + 1) * BLOCK)
        out_ref[1, h // 2, :, cols] = acc
        out_ref[0, h // 2, :, cols] = jnp.where(has_prev, acc, NEG_INF)


def _all_gather(shards, small_rows, casts, rel_bias_t, bucket_t, in_window_t):
    ns, nc, n = len(small_rows), len(casts), len(shards) + 1
    small_shape = (8, small_rows[0][1].shape[-1])
    shapes = [s.shape for s in shards] + [small_shape]
    pieces = [(t, r0, min(GATHER_PIECE_ROWS, shape[0] - r0))
              for t, shape in enumerate(shapes) for r0 in range(0, shape[0], GATHER_PIECE_ROWS)]

    def body(*refs):
        refs = list(refs)
        take = lambda k: [refs.pop(0) for _ in range(k)]
        ins, small_refs, cast_refs, (rb_ref, bucket_ref, win_ref) = take(n - 1), take(ns), take(nc), take(3)
        outs, cast_outs, (bias_ref, send_sems, recv_sems) = take(n), take(nc), take(3)
        x, y, c = lax.axis_index("x"), lax.axis_index("y"), lax.axis_index("c")
        me, sibling = (x, y, c), (x, y, 1 - c)
        x_nbr, y_nbr, diagonal = (1 - x, y), (x, 1 - y), (1 - x, 1 - y)
        south = c == 0
        relayed = (jnp.where(south, 1 - x, x), jnp.where(south, y, 1 - y))
        relay_to = (jnp.where(south, x, 1 - x), jnp.where(south, 1 - y, y))

        def copy(u, k, block, to):
            t, r0, nrows = pieces[u]
            rows = outs[t].at[4 * block[0] + 2 * block[1] + block[2], pl.ds(r0, nrows)]
            return pltpu.make_async_remote_copy(
                src_ref=rows, dst_ref=rows, send_sem=send_sems.at[u, k], recv_sem=recv_sems.at[u, k],
                device_id=to, device_id_type=MESH)

        mine = pl.ds(_my_index(), 1)
        for t in range(n - 1):
            outs[t][mine] = ins[t][...].astype(BF16)[None]
        outs[n - 1][mine] = jnp.zeros((1,) + small_shape, F32)
        for (row, _), ref in zip(small_rows, small_refs):
            if len(ref.shape) == 3:
                for j in range(ref.shape[0]):
                    outs[n - 1][mine, row + j:row + j + 1, :] = ref[j][None]
            else:
                outs[n - 1][mine, row:row + ref.shape[0], :] = ref[...][None]
        started = []

        def start(cp):
            cp.start()
            started.append(cp)

        units = range(len(pieces))
        for u in units:
            start(copy(u, 0, me, sibling))
            start(copy(u, 1, me, (*x_nbr, c)))
            start(copy(u, 2, me, (*y_nbr, c)))
        for src, dst in zip(cast_refs, cast_outs):
            dst[...] = src[...].astype(BF16)
        _bias_table(rb_ref, bucket_ref, win_ref, bias_ref)
        for u in units:
            for k, chip in ((1, x_nbr), (2, y_nbr)):
                copy(u, k, (*chip, c), me).wait_recv()
                start(copy(u, 3 + k, (*chip, c), sibling))
            start(copy(u, 3, (*relayed, c), (*relay_to, c)))
        for u in units:
            copy(u, 3, (*diagonal, c), me).wait_recv()
            start(copy(u, 6, (*diagonal, c), sibling))
        for u in units:
            copy(u, 0, sibling, me).wait_recv()
        for k, chip in ((4, x_nbr), (5, y_nbr), (6, diagonal)):
            for u in units:
                copy(u, k, (*chip, 1 - c), me).wait_recv()
        for cp in started:
            cp.wait_send()

    vmem = pl.BlockSpec(memory_space=pltpu.VMEM)
    outs = pl.pallas_call(
        body,
        name="gather_weights",
        out_shape=[jax.ShapeDtypeStruct((N_DEV,) + s.shape, BF16) for s in shards]
        + [jax.ShapeDtypeStruct((N_DEV,) + small_shape, F32)]
        + [jax.ShapeDtypeStruct(a.shape, BF16) for a in casts]
        + [jax.ShapeDtypeStruct((2, N_PAIRS, BAND, 2 * BLOCK), F32)],
        in_specs=[vmem] * (n - 1 + ns + nc) + [pl.BlockSpec(memory_space=pltpu.SMEM), vmem, vmem],
        out_specs=[vmem] * (n + nc + 1),
        scratch_shapes=[pltpu.SemaphoreType.DMA((len(pieces), 7)), pltpu.SemaphoreType.DMA((len(pieces), 7))],
        compiler_params=_params(),
    )(*shards, *[a for _, a in small_rows], *casts, rel_bias_t, bucket_t, in_window_t)
    return outs[:n - 1], outs[n - 1], outs[n:n + nc], outs[n + nc]


def _peer(k):
    x, y, c = lax.axis_index("x"), lax.axis_index("y"), lax.axis_index("c")
    px = 1 - x if k & 4 else x
    py = 1 - y if k & 2 else y
    pc = 1 - c if k & 1 else c
    return (px, py, pc), 4 * px + 2 * py + pc


def _exchange(srcs, dsts, send_sems, recv_sems, local_sems, scatter):
    me = _my_index()
    sends, arrivals = [], []
    for k in range(1, N_DEV):
        peer, pidx = _peer(k)
        for t, (src, dst) in enumerate(zip(srcs, dsts)):
            mine = src.at[pidx] if scatter else src
            sems = dict(send_sem=send_sems.at[t, k - 1], recv_sem=recv_sems.at[t, k - 1], device_id=peer, device_id_type=MESH)
            sends.append(pltpu.make_async_remote_copy(src_ref=mine, dst_ref=dst.at[me], **sems))
            arrivals.append(pltpu.make_async_remote_copy(src_ref=mine, dst_ref=dst.at[pidx], **sems))
    local = [pltpu.make_async_copy(src.at[me] if scatter else src, dst.at[me], local_sems.at[t])
             for t, (src, dst) in enumerate(zip(srcs, dsts))]
    return sends, arrivals, local


def _exchange_start(*args):
    sends, _, local = _exchange(*args)
    for cp in sends + local:
        cp.start()


def _exchange_wait(*args):
    sends, arrivals, local = _exchange(*args)
    for cp in arrivals:
        cp.wait_recv()
    for cp in sends:
        cp.wait_send()
    for cp in local:
        cp.wait()


def _exchange_sems(n):
    if not n:
        return []
    return [pltpu.SemaphoreType.DMA((n, N_DEV - 1)), pltpu.SemaphoreType.DMA((n, N_DEV - 1)), pltpu.SemaphoreType.DMA((n,))]


HBM_SPEC = pl.BlockSpec(memory_space=pl.ANY)


def _sum_slots(recv_ref, out_ref):
    rows = out_ref.shape[0]
    chunk = min(rows, 128)

    def add(i, carry):
        r0 = pl.multiple_of(i * chunk, chunk)
        acc = recv_ref[0, pl.ds(r0, chunk), :].astype(F32)
        for dev in range(1, N_DEV):
            acc = acc + recv_ref[dev, pl.ds(r0, chunk), :].astype(F32)
        out_ref[pl.ds(r0, chunk), :] = acc
        return carry

    lax.fori_loop(0, rows // chunk, add, 0)


N_CHIPS = N_DEV // 2


def _rows_loop(rows, fn):
    chunk = min(rows, 128)

    def step(i, carry):
        fn(pl.ds(pl.multiple_of(i * chunk, chunk), chunk))
        return carry

    lax.fori_loop(0, rows // chunk, step, 0)


def _chip_reduce(g_ref, out_ref, sib_ref, land_ref, send_ref, sems):
    sib_send, sib_recv, ici_send, ici_recv = sems
    x, y, c = lax.axis_index("x"), lax.axis_index("y"), lax.axis_index("c")
    south = c == 0
    near =(jnp.where(south, 1 - x, x), jnp.where(south, y, 1 - y))
    far = (jnp.where(south, x, 1 - x), jnp.where(south, 1 - y, y))
    diagonal = (1 - x, 1 - y)
    rows = out_ref.shape[0]
    direct, fold, folded = 0, 1, 2

    def to_sibling(t):
        return pltpu.make_async_remote_copy(
            src_ref=g_ref.at[2 * t + 1 - c], dst_ref=sib_ref.at[t], send_sem=sib_send.at[t], recv_sem=sib_recv.at[t],
            device_id=(x, y, 1 - c), device_id_type=MESH)

    def ici(role, chip):
        return pltpu.make_async_remote_copy(
            src_ref=send_ref.at[role], dst_ref=land_ref.at[role], send_sem=ici_send.at[role],
            recv_sem=ici_recv.at[role], device_id=(*chip, c), device_id_type=MESH)

    def pair_sum(chip, r):
        t = 2 * chip[0] + chip[1]
        return g_ref[2 * t + c, r, :].astype(F32) + sib_ref[t, r, :].astype(F32)

    def swap():
        for t in range(N_CHIPS):
            to_sibling(t).start()

    def send():
        for t in range(N_CHIPS):
            to_sibling(t).wait_recv()
        for role, chip in ((fold, diagonal), (direct, near)):
            def fill(r, role=role, chip=chip):
                send_ref[role, r, :] = pair_sum(chip, r).astype(BF16)

            _rows_loop(rows, fill)
            ici(role, near).start()

    def forward():
        ici(fold, near).wait_recv()

        def fill(r):
            send_ref[folded, r, :] = (pair_sum(far, r) + land_ref[fold, r, :].astype(F32)).astype(BF16)

        _rows_loop(rows, fill)
        ici(folded, far).start()

    def finish():
        ici(direct, near).wait_recv()
        ici(folded, far).wait_recv()

        def total(r):
            mine = pair_sum((x, y), r)
            out_ref[r, :] = mine + land_ref[direct, r, :].astype(F32) + land_ref[folded, r, :].astype(F32)

        _rows_loop(rows, total)
        for t in range(N_CHIPS):
            to_sibling(t).wait_send()
        for role, chip in ((direct, near), (fold, near), (folded, far)):
            ici(role, chip).wait_send()

    return swap, send, forward, finish


def _chip_reduce_scratch(slot):
    return [pltpu.VMEM((N_CHIPS,) + slot, BF16), pltpu.VMEM((3,) + slot, BF16), pltpu.VMEM((3,) + slot, BF16),
            pltpu.SemaphoreType.DMA((N_CHIPS,)), pltpu.SemaphoreType.DMA((N_CHIPS,)),
            pltpu.SemaphoreType.DMA((3,)), pltpu.SemaphoreType.DMA((3,))]


def _reduce_exchange(part, landed, smalls):
    nl, ng = len(landed), len(smalls)
    n_out = 1 + nl + ng

    def body(*refs):
        p_in, l_in, s_in = refs[0], refs[1:1 + nl], refs[1 + nl:n_out]
        p_out, l_out, s_out = refs[n_out], refs[n_out + 1:n_out + 1 + nl], refs[n_out + 1 + nl:2 * n_out]
        scratch = refs[2 * n_out:]
        s_recv, (sib_ref, chip_ref, send_ref), sems = scratch[:ng], scratch[ng:ng + 3], scratch[ng + 3:]
        swap, send, forward, finish = _chip_reduce(p_in, p_out, sib_ref, chip_ref, send_ref, sems[:4])
        swap()
        _exchange_start(s_in, s_recv, *sems[4:], False)
        send()
        for t in range(nl):
            _sum_slots(l_in[t], l_out[t])
        forward()
        finish()
        _exchange_wait(s_in, s_recv, *sems[4:], False)
        for t in range(ng):
            acc = s_recv[t][0]
            for dev in range(1, N_DEV):
                acc = acc + s_recv[t][dev]
            s_out[t][...] = acc

    vmem = pl.BlockSpec(memory_space=pltpu.VMEM)
    slot = part.shape[1:]
    outs = pl.pallas_call(
        body,
        name="reduce_grads",
        out_shape=[jax.ShapeDtypeStruct(p.shape[1:], F32) for p in [part] + landed]
        + [jax.ShapeDtypeStruct(s.shape, F32) for s in smalls],
        in_specs=[vmem] * n_out,
        out_specs=[vmem] * n_out,
        scratch_shapes=[pltpu.VMEM((N_DEV,) + s.shape, F32) for s in smalls] + _chip_reduce_scratch(slot)
        + _exchange_sems(ng),
        compiler_params=_params(vmem_mib=56),
    )(part, *landed, *smalls)
    return outs[0], outs[1:1 + nl], outs[1 + nl:]


def _layer_a_fwd(x2, sm, win_g, wout, later, ts):
    seq, d = x2.shape
    width = wout.shape[0]
    half = win_g.shape[2]
    n_half = width // half
    nl = len(later)
    nt = seq // ts

    def body(x_ref, sm_ref, win_ref, wout_ref, *refs):
        shard_refs, refs = refs[:nl], refs[nl:]
        h1_ref, n1_ref, proj_ref, conv_ref, y_ref, ya_ref = refs[:6]
        gathered_refs, (vprev_ref, *sems) = refs[6:6 + nl], refs[6 + nl:]

        @pl.when(pl.program_id(0) == 0)
        def _():
            vprev_ref[...] = jnp.zeros_like(vprev_ref)
            _exchange_start(shard_refs, gathered_refs, *sems, False)

        @pl.when(pl.program_id(0) == nt - 1)
        def _():
            _exchange_wait(shard_refs, gathered_refs, *sems, False)

        xf = x_ref[...]
        xn, _ = _rms(xf)
        n1 = (xn * sm_ref[0:1, :]).astype(BF16)
        n1_ref[...] = n1
        row = lax.broadcasted_iota(jnp.int32, (ts, half), 0)
        ya = jnp.zeros((ts, d), F32)
        for hh in range(n_half):
            cols = slice(hh * half, (hh + 1) * half)
            parts = []
            for part in range(4):
                j = part * n_half + hh
                pj = _dot(n1, win_ref[j])
                proj_ref[:, j * half:(j + 1) * half] = pj.astype(BF16)
                parts.append(pj)
            b, c, u, z = parts
            v = c * u
            last1, last2 = vprev_ref[7:8, cols], vprev_ref[6:7, cols]
            v1 = jnp.where(row == 0, last1, pltpu.roll(v, 1, 0))
            v2 = jnp.where(row == 0, last2, jnp.where(row == 1, last1, pltpu.roll(v, 2, 0)))
            vprev_ref[:, cols] = v[ts - 8:ts, :]
            conv = sm_ref[1:2, cols] * v2 + sm_ref[2:3, cols] * v1 + sm_ref[3:4, cols] * v
            conv_ref[:, cols] = conv.astype(BF16)
            yh = (b * conv * _silu(z)[0]).astype(BF16)
            y_ref[:, cols] = yh
            ya = ya + _dot(yh, wout_ref[cols, :])
        ya_ref[...] = ya
        h1_ref[...] = xf + _rms(ya)[0] * sm_ref[4:5, :]

    outs = pl.pallas_call(
        body,
        name="layer_a_fwd",
        grid=(nt,),
        in_specs=[_rows(ts, d), _full(sm.shape), _full(win_g.shape), _full(wout.shape)] + [HBM_SPEC] * nl,
        out_specs=[_rows(ts, d), _rows(ts, d), _rows(ts, 4 * width), _rows(ts, width), _rows(ts, width), _rows(ts, d)]
        + [HBM_SPEC] * nl,
        out_shape=[
            jax.ShapeDtypeStruct((seq, d), F32),
            jax.ShapeDtypeStruct((seq, d), BF16),
            jax.ShapeDtypeStruct((seq, 4 * width), BF16),
            jax.ShapeDtypeStruct((seq, width), BF16),
            jax.ShapeDtypeStruct((seq, width), BF16),
            jax.ShapeDtypeStruct((seq, d), F32),
        ] + [jax.ShapeDtypeStruct((N_DEV,) + s.shape, s.dtype) for s in later],
        scratch_shapes=[pltpu.VMEM((8, width), F32)] + _exchange_sems(nl),
        compiler_params=_params(("arbitrary",), 56),
    )(x2, sm, win_g, wout, *later)
    return outs[:6], outs[6:]


Q_BLOCKS = 4
ATTN_BWD_LAGS = (2, 4)
ATTN_FWD_LAGS = (2, 4)


def _banded_tiles(kvp_ref, kvc_ref):
    tile = kvc_ref[...].astype(F32)
    blocks = [kvp_ref[...].astype(F32)] + [tile[u * BLOCK:(u + 1) * BLOCK] for u in range(Q_BLOCKS)]
    return [_banded_kv(blocks[u], blocks[u + 1]) for u in range(Q_BLOCKS)]


def _bias_of(bias_ref, i, u, m):
    return bias_ref[jnp.minimum(i, 1) if u == 0 else 1, m]


def _banded_kv(kvp, kvc):
    kw = N_KV_HEADS * HEAD_DIM
    out = []
    for full in (jnp.concatenate([kvp[:, :kw], kvc[:, :kw]], axis=0), jnp.concatenate([kvp[:, kw:], kvc[:, kw:]], axis=0)):
        lo = lax.broadcasted_iota(jnp.int32, full.shape, 1) < HEAD_DIM
        rolled = pltpu.roll(full, HEAD_DIM, 1)
        x2 = [jnp.where(lo, full, rolled).astype(BF16), jnp.where(lo, rolled, full).astype(BF16)]
        ft = full.T
        x2t = [jnp.concatenate([ft[kh * HEAD_DIM:(kh + 1) * HEAD_DIM]] * 2, axis=0).astype(BF16) for kh in range(N_KV_HEADS)]
        out += [x2, x2t]
    return out


def _pair_rows(ref, rows, m, scale=None):
    both = ref[rows, m * LANES:(m + 1) * LANES].astype(F32)
    if scale is not None:
        both = both * scale
    lo = lax.broadcasted_iota(jnp.int32, both.shape, 1) < HEAD_DIM
    zero = jnp.zeros_like(both)
    return jnp.concatenate([jnp.where(lo, both, zero), jnp.where(lo, zero, both)], axis=0).astype(BF16)


def _pair_cols(res_t):
    top = lax.broadcasted_iota(jnp.int32, (LANES, BLOCK), 0) < HEAD_DIM
    return jnp.where(top, res_t[:, :BLOCK], res_t[:, BLOCK:]).T


def _sink_row(sink_ref, m):
    first = lax.broadcasted_iota(jnp.int32, (1, 2 * BLOCK), 1) < BLOCK
    return jnp.where(first, sink_ref[0, 2 * m], sink_ref[0, 2 * m + 1])


def _softmax_t(logits, sink):
    mx =jnp.maximum(jnp.max(logits, axis=0, keepdims=True), sink)
    p = jnp.exp(logits - mx)
    sink_p = jnp.exp(sink - mx)
    inv = 1.0 / (jnp.sum(p, axis=0, keepdims=True) + sink_p)
    return p * inv, sink_p * inv


def _layer_b_fwd(h1, target, kvn, bpre, wkv, wbin_g, biasm, sinks, wbout, bpost):
    seq, d = h1.shape
    kvw = wkv.shape[1]
    cw = wbin_g.shape[2]
    aw = N_Q_HEADS * HEAD_DIM
    per = aw // cw
    tile = Q_BLOCKS * BLOCK

    def body(sink_ref, h1_ref, tgt_ref, kvn_ref, bpre_ref, wkv_ref, wbin_ref, bias_ref, w_ref, g_ref,
             n3_ref, n4_ref, kvc_ref, q_ref, o_ref, dh2_ref, dyb_ref, dattn_ref, dz2_ref, acc_ref,
             attn_ref, z2_ref, kvp_ref):
        i = pl.program_id(0)

        @pl.when(i == 0)
        def _():
            acc_ref[...] = jnp.zeros_like(acc_ref)
            kvp_ref[...] = jnp.zeros_like(kvp_ref)

        hn, _ = _rms(h1_ref[...])
        n3 = (hn * kvn_ref[...]).astype(BF16)
        n4 = (hn * bpre_ref[...]).astype(BF16)
        n3_ref[...] = n3
        n4_ref[...] = n4
        kvc_ref[...] = _dot(n3, wkv_ref[...]).astype(BF16)
        for j in range(N_DEV):
            pj = _dot(n4, wbin_ref[j])
            if j < per:
                q_ref[:, j * cw:(j + 1) * cw] = pj.astype(BF16)
            else:
                z2_ref[:, (j - per) * cw:(j - per + 1) * cw] = pj

        banded = _banded_tiles(kvp_ref, kvc_ref)
        kvp_ref[...] = kvc_ref[tile - BLOCK:tile, :]
        units = [(u, m) for u in range(Q_BLOCKS) for m in range(N_PAIRS)]
        kv_of = lambda m: (2 * m) // GROUP
        logits, probs = {}, {}
        lag_b, lag_c = ATTN_FWD_LAGS
        for step in range(len(units) + lag_c):
            if step < len(units):
                u, m = units[step]
                qpair = _pair_rows(q_ref, slice(u * BLOCK, (u + 1) * BLOCK), m, SCALE)
                logits[step] = _dot_nt(banded[u][0][kv_of(m)], qpair) + _bias_of(bias_ref, i, u, m)
            if 0 <= step - lag_b < len(units):
                u, m = units[step - lag_b]
                probs[step - lag_b] = _softmax_t(logits.pop(step - lag_b), _sink_row(sink_ref, m))[0].astype(BF16)
            if 0 <= step - lag_c < len(units):
                u, m = units[step - lag_c]
                out_t = _dot(banded[u][3][kv_of(m)], probs.pop(step - lag_c))
                attn_ref[u * BLOCK:(u + 1) * BLOCK, m * LANES:(m + 1) * LANES] = _pair_cols(out_t)
        attn = attn_ref[...]
        sz, dsz = _silu(z2_ref[...])
        o = (attn * sz).astype(BF16)
        o_ref[...] = o

        w = w_ref[...]
        yb = _dot(o, w)
        ybn, r = _rms(yb)
        g = g_ref[...]
        diff = h1_ref[...] + ybn * g - tgt_ref[...]
        dh2 = diff * (1.0 / d)
        dh2_ref[...] = dh2
        acc_ref[0:1, :] += jnp.sum(dh2 * ybn, axis=0, keepdims=True)
        tok = jnp.mean(diff * diff, axis=-1, keepdims=True)
        acc_ref[1:2, :] += 0.5 * jnp.sum(tok, axis=0, keepdims=True)
        dyb = _rms_bwd(dh2 * g, ybn, r).astype(BF16)
        dyb_ref[...] = dyb
        do = _dot_nt(dyb, w)
        dattn_ref[...] = (do * sz).astype(BF16)
        dz2_ref[...] = (do * attn * dsz).astype(BF16)

    blk = lambda w: pl.BlockSpec((tile, w), lambda i: (i, 0))
    return pl.pallas_call(
        body,
        name="layer_b_fwd",
        grid=(seq // tile,),
        in_specs=[
            pl.BlockSpec(memory_space=pltpu.SMEM),
            blk(d),
            blk(d),
            _full(kvn.shape),
            _full(bpre.shape),
            _full(wkv.shape),
            _full(wbin_g.shape),
            _full(biasm.shape),
            _full(wbout.shape),
            _full(bpost.shape),
        ],
        out_specs=[blk(d), blk(d), blk(kvw), blk(aw), blk(aw), blk(d), blk(d), blk(aw), blk(aw), _resident((8, d))],
        out_shape=[
            jax.ShapeDtypeStruct((seq, d), BF16),
            jax.ShapeDtypeStruct((seq, d), BF16),
            jax.ShapeDtypeStruct((seq, kvw), BF16),
            jax.ShapeDtypeStruct((seq, aw), BF16),
            jax.ShapeDtypeStruct((seq, aw), BF16),
            jax.ShapeDtypeStruct((seq, d), F32),
            jax.ShapeDtypeStruct((seq, d), BF16),
            jax.ShapeDtypeStruct((seq, aw), BF16),
            jax.ShapeDtypeStruct((seq, aw), BF16),
            jax.ShapeDtypeStruct((8, d), F32),
        ],
        scratch_shapes=[pltpu.VMEM((tile, aw), F32), pltpu.VMEM((tile, aw), F32), pltpu.VMEM((BLOCK, kvw), BF16)],
        compiler_params=_params(("arbitrary",), 56),
    )(sinks, h1, target, kvn, bpre, wkv, wbin_g, biasm, wbout, bpost)


def _attn_bwd(q, kv, dattn, biasm, sinks, ready):
    seq, aw = q.shape
    kvw = kv.shape[1]
    kw = N_KV_HEADS * HEAD_DIM
    nb = seq // BLOCK
    pairs_per_kv = N_PAIRS // N_KV_HEADS
    nr = len(ready)

    tile = Q_BLOCKS * BLOCK
    nsteps = seq // tile
    held = (Q_BLOCKS - 1) * BLOCK

    def body(sink_ref, q_ref, kvc_ref, kvp_ref, da_ref, bias_ref, *refs):
        ready_refs, (dq_ref, dkv_ref, dssum_ref, dsink_ref) = refs[:nr], refs[nr:nr + 4]
        landed_refs, scratch = refs[nr + 4:2 * nr + 4], refs[2 * nr + 4:]
        carry_ref, done_ref, qs_ref, dos_ref, dst_ref, pt_ref, *sems = scratch
        i = pl.program_id(0)

        @pl.when(i == 0)
        def _():
            dssum_ref[...] = jnp.zeros_like(dssum_ref)
            dsink_ref[...] = jnp.zeros_like(dsink_ref)
            carry_ref[...] = jnp.zeros_like(carry_ref)
            done_ref[...] = jnp.zeros_like(done_ref)
            if nr:
                _exchange_start(ready_refs, landed_refs, *sems, True)

        if nr:
            @pl.when(i == nsteps)
            def _():
                _exchange_wait(ready_refs, landed_refs, *sems, True)

        @pl.when(i < nsteps)
        def _():
            lo = lax.broadcasted_iota(jnp.int32, (BAND, LANES), 1) < HEAD_DIM
            head_lane = lax.broadcasted_iota(jnp.int32, (1, LANES), 1)
            banded = _banded_tiles(kvp_ref, kvc_ref)
            units = [(u, m) for u in range(Q_BLOCKS) for m in range(N_PAIRS)]
            dsink = jnp.zeros((1, LANES), F32)
            folded = {}
            logits, dps, dsbs = {}, {}, {}
            lag_b, lag_c = ATTN_BWD_LAGS
            for step in range(len(units) + lag_c):
                if step < len(units):
                    u, m = units[step]
                    kh, rows = m // pairs_per_kv, slice((m % pairs_per_kv) * BAND, (m % pairs_per_kv + 1) * BAND)
                    qrows = slice(u * BLOCK, (u + 1) * BLOCK)
                    qpair = _pair_rows(q_ref, qrows, m, SCALE)
                    dopair = _pair_rows(da_ref, qrows, m)
                    qs_ref[u, kh, rows, :] = qpair
                    dos_ref[u, kh, rows, :] = dopair
                    logits[step] = _dot_nt(banded[u][0][kh], qpair) + _bias_of(bias_ref, i, u, m)
                    dps[step] = _dot_nt(banded[u][2][kh], dopair)
                if 0 <= step - lag_b < len(units):
                    u, m = units[step - lag_b]
                    kh, rows = m // pairs_per_kv, slice((m % pairs_per_kv) * BAND, (m % pairs_per_kv + 1) * BAND)
                    pn, sink_p = _softmax_t(logits.pop(step - lag_b), _sink_row(sink_ref, m))
                    dp = dps.pop(step - lag_b)
                    delta = jnp.sum(pn * dp, axis=0, keepdims=True)
                    ds = pn * (dp - delta)
                    dssum_ref[m] += ds
                    sink_term = sink_p * delta
                    for e in range(2):
                        total = jnp.sum(sink_term[:, e * BLOCK:(e + 1) * BLOCK], axis=1, keepdims=True)
                        dsink = dsink - jnp.where(head_lane == 2 * m + e, total, 0.0)
                    dsbs[step - lag_b] = ds.astype(BF16)
                    dst_ref[u, kh, :, rows] = dsbs[step - lag_b]
                    pt_ref[u, kh, :, rows] = pn.astype(BF16)
                if 0 <= step - lag_c < len(units):
                    u, m = units[step - lag_c]
                    kh = m // pairs_per_kv
                    dq_t = _dot(banded[u][1][kh], dsbs.pop(step - lag_c))
                    dq_ref[u * BLOCK:(u + 1) * BLOCK, m * LANES:(m + 1) * LANES] = (_pair_cols(dq_t) * SCALE).astype(BF16)
                    if m % pairs_per_kv == pairs_per_kv - 1:
                        for name, lhs_ref, rhs_ref in (("k", dst_ref, qs_ref), ("v", pt_ref, dos_ref)):
                            acc = _dot(lhs_ref[u, kh], rhs_ref[u, kh])
                            folded[u, kh, name] = acc + pltpu.roll(acc, HEAD_DIM, 1)
            dsink_ref[0:1, :] += dsink
            dkv = [jnp.concatenate([jnp.where(lo, folded[u, 0, n], folded[u, 1, n]) for n in ("k", "v")], axis=1)
                   for u in range(Q_BLOCKS)]

            @pl.when(i > 0)
            def _():
                if held:
                    dkv_ref[:held, :] = done_ref[...].astype(BF16)
                dkv_ref[held:, :] = (carry_ref[...] + dkv[0][:BLOCK]).astype(BF16)

            for u in range(Q_BLOCKS - 1):
                done_ref[u * BLOCK:(u + 1) * BLOCK, :] = dkv[u][BLOCK:] + dkv[u + 1][:BLOCK]
            carry_ref[...] = dkv[Q_BLOCKS - 1][BLOCK:]

        @pl.when(i == nsteps)
        def _():
            if held:
                dkv_ref[:held, :] = done_ref[...].astype(BF16)
            dkv_ref[held:, :] = carry_ref[...].astype(BF16)

    last = nsteps - 1
    blk = lambda w: pl.BlockSpec((tile, w), lambda i: (jnp.minimum(i, last), 0))
    outs = pl.pallas_call(
        body,
        name="attn_bwd",
        grid=(nsteps + 1,),
        in_specs=[
            pl.BlockSpec(memory_space=pltpu.SMEM),
            blk(aw),
            blk(kvw),
            pl.BlockSpec((BLOCK, kvw), lambda i: (jnp.clip(Q_BLOCKS * i - 1, 0, nb - 1), 0)),
            blk(aw),
            _full(biasm.shape),
        ] + [HBM_SPEC] * nr,
        out_specs=[
            blk(aw),
            pl.BlockSpec((tile, kvw), lambda i: (jnp.maximum(i - 1, 0), 0)),
            _resident(biasm.shape[1:]),
            _resident((8, LANES)),
        ] + [HBM_SPEC] * nr,
        out_shape=[
            jax.ShapeDtypeStruct((seq, aw), BF16),
            jax.ShapeDtypeStruct((seq, kvw), BF16),
            jax.ShapeDtypeStruct(biasm.shape[1:], F32),
            jax.ShapeDtypeStruct((8, LANES), F32),
        ] + [jax.ShapeDtypeStruct(g.shape, g.dtype) for g in ready],
        scratch_shapes=[
            pltpu.VMEM((BLOCK, kvw), F32),
            pltpu.VMEM((max(held, 8), kvw), F32),
            pltpu.VMEM((Q_BLOCKS, N_KV_HEADS, pairs_per_kv * BAND, LANES), BF16),
            pltpu.VMEM((Q_BLOCKS, N_KV_HEADS, pairs_per_kv * BAND, LANES), BF16),
            pltpu.VMEM((Q_BLOCKS, N_KV_HEADS, BAND, pairs_per_kv * BAND), BF16),
            pltpu.VMEM((Q_BLOCKS, N_KV_HEADS, BAND, pairs_per_kv * BAND), BF16),
        ] + _exchange_sems(nr),
        compiler_params=_params(("arbitrary",), 48),
    )(sinks, q, kv, kv, dattn, biasm, *ready)
    return outs[:4], outs[4:]


def _relbias_grad(dssum2, bucket_row, chunk):
    heads, n = dssum2.shape

    def body(a_ref, bucket_ref, out_ref):
        @pl.when(pl.program_id(0) == 0)
        def _():
            out_ref[...] = jnp.zeros_like(out_ref)

        a = a_ref[...]
        hi = a.astype(BF16)
        lo = (a - hi.astype(F32)).astype(BF16)
        onehot_t = (lax.broadcasted_iota(jnp.int32, (LANES, chunk), 0) == bucket_ref[...]).astype(F32).astype(BF16)
        out_ref[...] += _dot_nt(hi, onehot_t) + _dot_nt(lo, onehot_t)

    return pl.pallas_call(
        body,
        name="relbias_grad",
        grid=(n // chunk,),
        in_specs=[pl.BlockSpec((heads, chunk), lambda i: (0, i)), pl.BlockSpec((1, chunk), lambda i: (0, i))],
        out_specs=_resident((heads, LANES)),
        out_shape=jax.ShapeDtypeStruct((heads, LANES), F32),
        compiler_params=_params(("arbitrary",), 32),
    )(dssum2, bucket_row)


def _layer_b_in_bwd(dh2, dq, dz2, dkv, h1, ya, wbin_g, wkv, kvn, bpre, sm, ready, ts):
    seq, d = h1.shape
    aw = dq.shape[1]
    kvw = dkv.shape[1]
    cw = wbin_g.shape[2]
    per = aw // cw

    nr = len(ready)
    nt = seq // ts

    def body(dh2_ref, dq_ref, dz2_ref, dkv_ref, h1_ref, ya_ref, wbin_ref, wkv_ref, kvn_ref, bpre_ref, sm_ref, *refs):
        ready_refs, (dh1_ref, dya_ref, acc_ref) = refs[:nr], refs[nr:nr + 3]
        landed_refs, sems = refs[nr + 3:2 * nr + 3], refs[2 * nr + 3:]

        @pl.when(pl.program_id(0) == 0)
        def _():
            acc_ref[...] = jnp.zeros_like(acc_ref)
            _exchange_start(ready_refs, landed_refs, *sems, True)

        @pl.when(pl.program_id(0) == nt - 1)
        def _():
            _exchange_wait(ready_refs, landed_refs, *sems, True)

        dn4 = jnp.zeros((ts, d), F32)
        for j in range(N_DEV):
            src = dq_ref if j < per else dz2_ref
            jj = j % per
            dn4 = dn4 + _dot_nt(src[:, jj * cw:(jj + 1) * cw], wbin_ref[j])
        dn3 = _dot_nt(dkv_ref[...], wkv_ref[...])
        hn, r = _rms(h1_ref[...])
        acc_ref[0:1, :] += jnp.sum(dn4 * hn, axis=0, keepdims=True)
        acc_ref[1:2, :] += jnp.sum(dn3 * hn, axis=0, keepdims=True)
        dh1 = dh2_ref[...] + _rms_bwd(dn4 * bpre_ref[...] + dn3 * kvn_ref[...], hn, r)
        dh1_ref[...] = dh1
        yan, r2 = _rms(ya_ref[...])
        acc_ref[2:3, :] += jnp.sum(dh1 * yan, axis=0, keepdims=True)
        dya_ref[...] = _rms_bwd(dh1 * sm_ref[4:5, :], yan, r2).astype(BF16)

    outs = pl.pallas_call(
        body,
        name="layer_b_in_bwd",
        grid=(nt,),
        in_specs=[_rows(ts, d), _rows(ts, aw), _rows(ts, aw), _rows(ts, kvw), _rows(ts, d), _rows(ts, d),
                  _full(wbin_g.shape), _full(wkv.shape), _full(kvn.shape), _full(bpre.shape), _full(sm.shape)]
        + [HBM_SPEC] * nr,
        out_specs=[_rows(ts, d), _rows(ts, d), _resident((8, d))] + [HBM_SPEC] * nr,
        out_shape=[jax.ShapeDtypeStruct((seq, d), F32), jax.ShapeDtypeStruct((seq, d), BF16),
                   jax.ShapeDtypeStruct((8, d), F32)] + [jax.ShapeDtypeStruct(g.shape, g.dtype) for g in ready],
        scratch_shapes=_exchange_sems(nr),
        compiler_params=_params(("arbitrary",), 48),
    )(dh2, dq, dz2, dkv, h1, ya, wbin_g, wkv, kvn, bpre, sm, *ready)
    return outs[:3], outs[3:]


def _layer_a_bwd(dya, proj, conv, dh1, x2, wout, win_g, sm, ts):
    seq, d = x2.shape
    width = wout.shape[0]
    half = win_g.shape[2]
    n_half = width // half
    nt = seq // ts

    def body(dya_ref, proj_ref, conv_ref, dh1_ref, x_ref, wout_ref, win_ref, sm_ref, dproj_ref, gx_ref, acc_ref,
             dnext_ref):
        @pl.when(pl.program_id(0) == 0)
        def _():
            acc_ref[...] = jnp.zeros_like(acc_ref)
            dnext_ref[...] = jnp.zeros_like(dnext_ref)

        dy = _dot_nt(dya_ref[...], wout_ref[...])
        row = lax.broadcasted_iota(jnp.int32, (ts, half), 0)
        dn1 = jnp.zeros((ts, d), F32)
        for hh in range(n_half):
            cols = slice(hh * half, (hh + 1) * half)
            b, c, u, z = [proj_ref[:, (part * n_half + hh) * half:(part * n_half + hh + 1) * half].astype(F32)
                          for part in range(4)]
            cv = conv_ref[:, cols].astype(F32)
            dyh = dy[:, cols]
            sz, dsz = _silu(z)
            dconv = dyh * b * sz
            grads = [dyh * cv * sz, None, None, dyh * b * cv * dsz]
            next0, next1 = dnext_ref[0:1, cols], dnext_ref[1:2, cols]
            dc1 = jnp.where(row == ts - 1, next0, pltpu.roll(dconv, ts - 1, 0))
            dc2 = jnp.where(row == ts - 1, next1, jnp.where(row == ts - 2, next0, pltpu.roll(dconv, ts - 2, 0)))
            dnext_ref[:, cols] = dconv[0:8, :]
            v = c * u
            acc_ref[1:2, cols] += jnp.sum(dc2 * v, axis=0, keepdims=True)
            acc_ref[2:3, cols] += jnp.sum(dc1 * v, axis=0, keepdims=True)
            acc_ref[3:4, cols] += jnp.sum(dconv * v, axis=0, keepdims=True)
            dv = sm_ref[3:4, cols] * dconv + sm_ref[2:3, cols] * dc1 + sm_ref[1:2, cols] * dc2
            grads[1] = dv * u
            grads[2] = dv * c
            for part in range(4):
                j = part * n_half + hh
                gj = grads[part].astype(BF16)
                dproj_ref[:, j * half:(j + 1) * half] = gj
                dn1 = dn1 + _dot_nt(gj, win_ref[j])
        xn, r = _rms(x_ref[...])
        acc_ref[0:1, :] += jnp.sum(dn1 * xn, axis=0, keepdims=True)
        gx_ref[...] = dh1_ref[...] + _rms_bwd(dn1 * sm_ref[0:1, :], xn, r)

    rev = lambda w: pl.BlockSpec((ts, w), lambda i: (nt - 1 - i, 0))
    return pl.pallas_call(
        body,
        name="layer_a_bwd",
        grid=(nt,),
        in_specs=[rev(d), rev(4 * width), rev(width), rev(d), rev(d), _full(wout.shape), _full(win_g.shape), _full(sm.shape)],
        out_specs=[rev(4 * width), rev(d), _resident((8, d))],
        out_shape=[jax.ShapeDtypeStruct((seq, 4 * width), BF16), jax.ShapeDtypeStruct((seq, d), F32),
                   jax.ShapeDtypeStruct((8, d), F32)],
        scratch_shapes=[pltpu.VMEM((8, width), F32)],
        compiler_params=_params(("arbitrary",), 56),
    )(dya, proj, conv, dh1, x2, wout, win_g, sm)


def _wgrad(a, bs, n_slots, ts, name, ready=(), block_cols=1024):
    nr = len(ready)
    seq, k = a.shape
    nb_in = len(bs)
    n_each = bs[0].shape[1]
    n = nb_in * n_each
    bn = min(n_each, block_cols)
    per_in = n_each // bn
    n_blocks = nb_in * per_in
    ns = seq // ts

    def b_spec(idx):
        def index(j, s):
            mine = j // per_in == idx
            row = jnp.where(mine, s, jnp.where(j // per_in > idx, ns - 1, 0))
            return (row, jnp.where(mine, j % per_in, jnp.where(j // per_in > idx, per_in - 1, 0)))
        return pl.BlockSpec((ts, bn), index)

    if n_slots:
        sw = n // n_slots
        spb = bn // sw
        out_shape = jax.ShapeDtypeStruct((n_slots, k, sw), BF16)
        out_spec = pl.BlockSpec((spb, k, sw), lambda j, s: (j, 0, 0))
    else:
        out_shape = jax.ShapeDtypeStruct((k, n), BF16)
        out_spec = pl.BlockSpec((k, bn), lambda j, s: (0, j))

    def body(a_ref, *refs):
        b_refs, ready_refs, o_ref = refs[:nb_in], refs[nb_in:nb_in + nr], refs[nb_in + nr]
        landed_refs, (acc_ref, *sems) = refs[nb_in + nr + 1:nb_in + 2 * nr + 1], refs[nb_in + 2 * nr + 1:]
        j, s = pl.program_id(0), pl.program_id(1)

        if nr:
            @pl.when(jnp.logical_and(j == 0, s == 0))
            def _():
                _exchange_start(ready_refs, landed_refs, *sems, True)

            @pl.when(jnp.logical_and(j == n_blocks - 1, s == ns - 1))
            def _():
                _exchange_wait(ready_refs, landed_refs, *sems, True)

        @pl.when(s == 0)
        def _():
            acc_ref[...] = jnp.zeros_like(acc_ref)

        for idx in range(nb_in):
            @pl.when(j // per_in == idx)
            def _(idx=idx):
                acc_ref[...] += _dot_tn(a_ref[...], b_refs[idx][...])

        @pl.when(s == ns - 1)
        def _():
            if n_slots:
                for e in range(spb):
                    o_ref[e] = acc_ref[:, e * sw:(e + 1) * sw].astype(BF16)
            else:
                o_ref[...] = acc_ref[...].astype(BF16)

    outs = pl.pallas_call(
        body,
        name=name,
        grid=(n_blocks, ns),
        in_specs=[pl.BlockSpec((ts, k), lambda j, s: (s, 0))] + [b_spec(idx) for idx in range(nb_in)] + [HBM_SPEC] * nr,
        out_specs=[out_spec] + [HBM_SPEC] * nr,
        out_shape=[out_shape] + [jax.ShapeDtypeStruct(g.shape, g.dtype) for g in ready],
        scratch_shapes=[pltpu.VMEM((k, bn), F32)] + (_exchange_sems(nr) if nr else []),
        compiler_params=_params(("arbitrary", "arbitrary"), 48),
    )(a, *bs, *ready)
    return (outs[0], outs[1:]) if nr else outs[0]


def _wgrad_tail(pairs, part, landed, ts):
    n_tasks = len(pairs)
    assert n_tasks == 2
    nl = len(landed)
    seq, k = pairs[0][0].shape
    n = pairs[0][1].shape[1]
    ns = seq // ts
    total = n_tasks * ns
    per = k // N_DEV
    n_red = len(_chip_reduce_scratch((per, n)))

    def spec(t, width):
        return pl.BlockSpec((ts, width), lambda j, s: (jnp.where(j == t, s, jnp.where(j > t, ns - 1, 0)), 0))

    def body(*refs):
        ab_refs, part_ref = refs[:2 * n_tasks], refs[2 * n_tasks]
        landed_refs, refs = refs[2 * n_tasks + 1:2 * n_tasks + 1 + nl], refs[2 * n_tasks + 1 + nl:]
        o_ref, red_ref, early_ref = refs[:3]
        summed_refs, (acc_ref, first_ref, *scratch) = refs[3:3 + nl], refs[3 + nl:]
        j, s = pl.program_id(0), pl.program_id(1)
        flat = j * ns + s
        swap, send, forward, finish = _chip_reduce(part_ref, red_ref, *scratch[:3], scratch[3:n_red])
        swap_first, send_first, forward_first, finish_first = _chip_reduce(
            first_ref, early_ref, *scratch[n_red:n_red + 3], scratch[n_red + 3:])

        @pl.when(flat == 0)
        def _():
            swap()

        @pl.when(flat == min(1, total - 1))
        def _():
            send()

        @pl.when(flat == min(total // 2 + 1, total - 1))
        def _():
            forward()
            for t in range(nl):
                _sum_slots(landed_refs[t], summed_refs[t])

        @pl.when(flat == ns)
        def _():
            send_first()

        @pl.when(flat == min(ns + ns // 2, total - 1))
        def _():
            forward_first()

        @pl.when(s == 0)
        def _():
            acc_ref[...] = jnp.zeros_like(acc_ref)

        for t in range(n_tasks):
            @pl.when(j == t)
            def _(t=t):
                acc_ref[...] += _dot_tn(ab_refs[2 * t][...], ab_refs[2 * t + 1][...])

        @pl.when(flat == ns - 1)
        def _():
            for dev in range(N_DEV):
                first_ref[dev] = acc_ref[dev * per:(dev + 1) * per, :].astype(BF16)
            swap_first()

        @pl.when(flat == total - 1)
        def _():
            for dev in range(N_DEV):
                o_ref[dev] = acc_ref[dev * per:(dev + 1) * per, :].astype(BF16)
            finish()
            finish_first()

    slot = part.shape[1:]
    outs = pl.pallas_call(
        body,
        name="wgrad_tail",
        grid=(n_tasks, ns),
        in_specs=[spec(t, w) for t in range(n_tasks) for w in (k, n)] + [_full(part.shape)]
        + [_full(g.shape) for g in landed],
        out_specs=[_resident((N_DEV, per, n)), _resident(slot), _resident((per, n))]
        + [_resident(g.shape[1:]) for g in landed],
        out_shape=[jax.ShapeDtypeStruct((N_DEV, per, n), BF16), jax.ShapeDtypeStruct(slot, F32),
                   jax.ShapeDtypeStruct((per, n), F32)]
        + [jax.ShapeDtypeStruct(g.shape[1:], F32) for g in landed],
        scratch_shapes=[pltpu.VMEM((k, n), F32), pltpu.VMEM((N_DEV, per, n), BF16)] + _chip_reduce_scratch(slot)
        + _chip_reduce_scratch((per, n)),
        compiler_params=_params(("arbitrary", "arbitrary")),
    )(*[op for pair in pairs for op in pair], part, *landed)
    return outs[0], outs[1], outs[2], outs[3:]


MINE = "mine"


def _adamw(ws, sources, picks, loss_at, ms, vs):
    n, n_src = len(ws), len(sources)

    def step(w, g, m, v):
        m = ADAM_B1 * m + (1.0 - ADAM_B1) * g
        v = ADAM_B2 * v + (1.0 - ADAM_B2) * jnp.square(g)
        m_hat = m / (1.0 - ADAM_B1 ** ADAM_STEP)
        v_hat = v / (1.0 - ADAM_B2 ** ADAM_STEP)
        return g, -ADAM_LR * (m_hat / (jnp.sqrt(v_hat) + ADAM_EPS) + ADAM_WD * w), m, v

    def body(*refs):
        refs = list(refs)
        take = lambda k: [refs.pop(0) for _ in range(k)]
        w_refs, s_refs, m_refs, v_refs = take(n), take(n_src), take(n), take(n)
        (loss_ref,), go_refs, d_refs, nm_refs, nv_refs = take(1), take(n), take(n), take(n), take(n)
        me = _my_index()
        loss_ref[...] = s_refs[loss_at[0]][loss_at[1]:loss_at[1] + 1, 0:1]

        def grad(t, rows):
            k, first, cols = picks[t]
            if cols is None:
                return s_refs[k][rows, :]
            if cols is not MINE:
                return s_refs[k][rows, cols]
            width = w_refs[t].shape[-1]
            g = s_refs[k][rows, 0:width]
            for dev in range(1, N_DEV):
                g = jnp.where(me == dev, s_refs[k][rows, dev * width:(dev + 1) * width], g)
            return g

        for t in range(n):
            first = picks[t][1]
            rows = w_refs[t].shape[0]
            if len(w_refs[t].shape) == 3:
                for j in range(rows):
                    go_refs[t][j], d_refs[t][j], nm_refs[t][j], nv_refs[t][j] = step(
                        w_refs[t][j], grad(t, slice(first + j, first + j + 1)), m_refs[t][j], v_refs[t][j])
                continue
            if rows <= 128:
                go_refs[t][...], d_refs[t][...], nm_refs[t][...], nv_refs[t][...] = step(
                    w_refs[t][...], grad(t, slice(first, first + rows)), m_refs[t][...], v_refs[t][...])
                continue
            chunk = 128

            def one(i, carry, t=t, first=first):
                r = pl.ds(pl.multiple_of(i * chunk, chunk), chunk)
                go_refs[t][r, :], d_refs[t][r, :], nm_refs[t][r, :], nv_refs[t][r, :] = step(
                    w_refs[t][r, :], grad(t, pl.ds(pl.multiple_of(first + i * chunk, chunk), chunk)),
                    m_refs[t][r, :], v_refs[t][r, :])
                return carry

            lax.fori_loop(0, rows // chunk, one, 0)

    vmem = pl.BlockSpec(memory_space=pltpu.VMEM)
    outs = pl.pallas_call(
        body,
        name="adamw",
        in_specs=[vmem] * (3 * n + n_src),
        out_specs=[vmem] * (4 * n + 1),
        out_shape=[jax.ShapeDtypeStruct((1, 1), F32)] + [jax.ShapeDtypeStruct(w.shape, F32) for w in ws] * 4,
        compiler_params=_params(),
    )(*ws, *sources, *ms, *vs)
    return outs[0], outs[1:n + 1], outs[n + 1:2 * n + 1], outs[2 * n + 1:3 * n + 1], outs[3 * n + 1:]


def _band_structure():
    q_loc = np.arange(BLOCK, dtype=np.int32)[:, None]
    s_loc = np.arange(2 * BLOCK, dtype=np.int32)[None, :]
    dist = q_loc + BLOCK - s_loc
    in_window = (dist >= 0) & (dist < BLOCK)
    dd = np.maximum(dist, 0)
    max_exact = N_BUCKETS // 2
    large = max_exact + (np.log(np.maximum(dd, 1) / max_exact) / math.log(MAX_DISTANCE / max_exact)
                         * (N_BUCKETS - max_exact)).astype(np.int32)
    bucket = np.where(dd < max_exact, dd, np.minimum(large, N_BUCKETS - 1)).astype(np.int32)
    return bucket, in_window.astype(np.int32)


def kernel(x, a_pre_norm, a_w_in, a_conv_w, a_w_out, a_post_norm, kv_norm, w_kv, rel_bias, b_pre_norm, b_w_in, b_sinks, b_w_out, b_post_norm, loss_target, m_a_pre_norm, m_a_w_in, m_a_conv_w, m_a_w_out, m_a_post_norm, m_kv_norm, m_w_kv, m_rel_bias, m_b_pre_norm, m_b_w_in, m_b_sinks, m_b_w_out, m_b_post_norm, v_a_pre_norm, v_a_w_in, v_a_conv_w, v_a_w_out, v_a_post_norm, v_kv_norm, v_w_kv, v_rel_bias, v_b_pre_norm, v_b_w_in, v_b_sinks, v_b_w_out, v_b_post_norm):
    seq, d = x.shape[1], x.shape[2]
    x2 = x.reshape(seq, d)
    target = loss_target.reshape(seq, d)
    shard = a_pre_norm.shape[1]
    ts_a = min(seq, 512)
    ts = min(seq, 512)
    ts_w = min(seq, 2048)

    taps = lambda a: a.transpose(1, 0, 2)
    bucket, in_window = _band_structure()
    (win_g, wout_g), small_g, later, biasm = _all_gather(
        [a_w_in[0], a_w_out[0]], [(0, a_pre_norm), (1, taps(a_conv_w)), (4, a_post_norm)],
        [w_kv, b_w_in[0], b_w_out[0]], rel_bias.T, bucket.T, in_window.T)
    wout = wout_g.reshape(-1, wout_g.shape[2])
    sm = small_g.transpose(1, 0, 2).reshape(8, N_DEV * shard)
    kvn = kv_norm.reshape(1, d)

    (h1, n1, proj, conv, y, ya), (wkv_g, wbin_g, wbout_g) = _layer_a_fwd(x2, sm, win_g, wout, later, ts_a)
    wkv = wkv_g.reshape(-1, wkv_g.shape[2])
    wbout = wbout_g.reshape(-1, wbout_g.shape[2])
    n3, n4, kv, q, o, dh2, dyb, dattn, dz2, acc_c = _layer_b_fwd(
        h1, target, kvn, b_pre_norm, wkv, wbin_g, biasm, b_sinks, wbout, b_post_norm)

    (dq, dkv, dssum, dsink), _ = _attn_bwd(q, kv, dattn, biasm, b_sinks, [])
    by_head = dssum.reshape(N_PAIRS, BAND, 2, BLOCK).transpose(0, 2, 3, 1)
    relb = _relbias_grad(by_head.reshape(N_Q_HEADS, -1), bucket.reshape(1, -1), 4096)
    g_wkv = _wgrad(n3, [dkv], 0, ts_w, "wgrad_kv").reshape(wkv_g.shape)
    g_wbin = _wgrad(n4, [dq, dz2], N_DEV, ts_w, "wgrad_b_in")
    (dh1, dya, acc_b), (l_wkv, l_wbin) = _layer_b_in_bwd(
        dh2, dq, dz2, dkv, h1, ya, wbin_g, wkv, kvn, b_pre_norm, sm, [g_wkv, g_wbin], ts)
    dproj, gx, acc_a = _layer_a_bwd(dya, proj, conv, dh1, x2, wout, win_g, sm, ts_a)
    g_win = _wgrad(n1, [dproj], N_DEV, ts_w, "wgrad_a_in", block_cols=2048)
    g_wbout, r_win, r_wout, (r_wkv, r_wbin) = _wgrad_tail(
        [(y, dya), (o, dyb)], g_win, [l_wkv, l_wbin], min(seq, 1024))

    r_wbout, _, (s_a, s_b, s_c, s_relb, s_sink) = _reduce_exchange(g_wbout, [], [acc_a, acc_b, acc_c, relb, dsink])
    weights = [a_pre_norm, a_w_in[0], taps(a_conv_w), a_w_out[0], a_post_norm, kvn, w_kv, rel_bias.T, b_pre_norm,
               b_w_in[0], b_sinks, b_w_out[0], b_post_norm]
    sources = [s_a, s_b, s_c, s_relb, s_sink, r_win, r_wout, r_wkv, r_wbin, r_wbout]
    picks = [(0, 0, MINE), (5, 0, None), (0, 1, MINE), (6, 0, None), (1, 2, MINE), (1, 1, None), (7, 0, None),
             (3, 0, slice(0, N_BUCKETS)), (1, 0, None), (8, 0, None), (4, 0, slice(0, N_Q_HEADS)),
             (9, 0, None), (2, 0, None)]
    first = [m_a_pre_norm, m_a_w_in[0], taps(m_a_conv_w), m_a_w_out[0], m_a_post_norm, m_kv_norm.reshape(1, d),
             m_w_kv, m_rel_bias.T, m_b_pre_norm, m_b_w_in[0], m_b_sinks, m_b_w_out[0], m_b_post_norm]
    second = [v_a_pre_norm, v_a_w_in[0], taps(v_a_conv_w), v_a_w_out[0], v_a_post_norm, v_kv_norm.reshape(1, d),
              v_w_kv, v_rel_bias.T, v_b_pre_norm, v_b_w_in[0], v_b_sinks, v_b_w_out[0], v_b_post_norm]
    loss, grads, deltas, new_m, new_v = _adamw(weights, sources, picks, (2, 1), first, second)

    shapes = [a_pre_norm.shape, a_w_in.shape, taps, a_w_out.shape, a_post_norm.shape, kv_norm.shape,
              w_kv.shape, jnp.transpose, b_pre_norm.shape, b_w_in.shape, b_sinks.shape, b_w_out.shape, b_post_norm.shape]
    shaped = lambda arrays: [s(a) if callable(s) else a.reshape(s) for a, s in zip(arrays, shapes)]
    return (loss.reshape(()), gx.reshape(x.shape), *shaped(grads), *shaped(deltas), *shaped(new_m), *shaped(new_v))
```

```python
import math

import jax
import jax.numpy as jnp
import numpy as np
from jax import lax
from jax.experimental import pallas as pl
from jax.experimental.pallas import tpu as pltpu

HEAD_DIM = 64
N_Q_HEADS = 16
N_KV_HEADS = 2
GROUP = N_Q_HEADS // N_KV_HEADS
BLOCK = 128
N_BUCKETS = 32
MAX_DISTANCE = 128
EPS = 1e-6
NEG_INF = -1e30
SCALE = HEAD_DIM ** -0.5

ADAM_LR = 0.001
ADAM_B1 = 0.9
ADAM_B2 = 0.999
ADAM_EPS = 1e-08
ADAM_WD = 0.01
ADAM_STEP = 10

N_PAIRS = N_Q_HEADS // 2
BAND = 2 * BLOCK

N_DEV = 8
GATHER_PIECE_ROWS = 256
LANES = 128
F32 = jnp.float32
BF16 = jnp.bfloat16
MESH = pl.DeviceIdType.MESH
MIB = 1024 * 1024
VMEM_RESERVED_MIB = 63


def _params(semantics=None, vmem_mib=48):
    del vmem_mib
    return pltpu.CompilerParams(dimension_semantics=semantics, vmem_limit_bytes=VMEM_RESERVED_MIB * MIB)


def _full(shape):
    zeros = (0,) * len(shape)
    return pl.BlockSpec(shape, lambda *_: zeros, pipeline_mode=pl.Buffered(1))


def _resident(shape):
    zeros = (0,) * len(shape)
    return pl.BlockSpec(shape, lambda *_: zeros)


def _rows(ts, cols):
    return pl.BlockSpec((ts, cols), lambda i: (i, 0))


def _dot(a, b):
    return jnp.dot(a, b, preferred_element_type=F32)


def _dot_nt(a, b):
    return lax.dot_general(a, b, (((1,), (1,)), ((), ())), preferred_element_type=F32)


def _dot_tn(a, b):
    return lax.dot_general(a, b, (((0,), (0,)), ((), ())), preferred_element_type=F32)


def _rms(xf):
    r = lax.rsqrt(jnp.mean(xf * xf, axis=-1, keepdims=True) + EPS)
    return xf * r, r


def _rms_bwd(dn, xn, r):
    return r * (dn - xn * jnp.mean(dn * xn, axis=-1, keepdims=True))


def _silu(z):
    s = jax.nn.sigmoid(z)
    return z * s, s * (1.0 + z * (1.0 - s))


def _my_index():
    return 4 * lax.axis_index("x") + 2 * lax.axis_index("y") + lax.axis_index("c")


def _bias_table(rb_ref, bucket_ref, win_ref, out_ref):
    bk = jnp.where(win_ref[...] != 0, bucket_ref[...], -1)
    has_prev = lax.broadcasted_iota(jnp.int32, bk.shape, 0) >= BLOCK
    for h in range(N_Q_HEADS):
        acc = jnp.full(bk.shape, NEG_INF, F32)
        for b in range(N_BUCKETS):
            acc = jnp.where(bk == b, rb_ref[h, b], acc)
        cols = slice((h % 2) * BLOCK, (h % 2 + 1) * BLOCK)
        out_ref[1, h // 2, :, cols] = acc
        out_ref[0, h // 2, :, cols] = jnp.where(has_prev, acc, NEG_INF)


def _all_gather(shards, small_rows, casts, rel_bias_t, bucket_t, in_window_t):
    ns, nc, n = len(small_rows), len(casts), len(shards) + 1
    small_shape = (8, small_rows[0][1].shape[-1])
    shapes = [s.shape for s in shards] + [small_shape]
    pieces = [(t, r0, min(GATHER_PIECE_ROWS, shape[0] - r0))
              for t, shape in enumerate(shapes) for r0 in range(0, shape[0], GATHER_PIECE_ROWS)]

    def body(*refs):
        refs = list(refs)
        take = lambda k: [refs.pop(0) for _ in range(k)]
        ins, small_refs, cast_refs, (rb_ref, bucket_ref, win_ref) = take(n - 1), take(ns), take(nc), take(3)
        outs, cast_outs, (bias_ref, send_sems, recv_sems) = take(n), take(nc), take(3)
        x, y, c = lax.axis_index("x"), lax.axis_index("y"), lax.axis_index("c")
        me, sibling = (x, y, c), (x, y, 1 - c)
        x_nbr, y_nbr, diagonal = (1 - x, y), (x, 1 - y), (1 - x, 1 - y)
        south = c == 0
        relayed = (jnp.where(south, 1 - x, x), jnp.where(south, y, 1 - y))
        relay_to = (jnp.where(south, x, 1 - x), jnp.where(south, 1 - y, y))

        def copy(u, k, block, to):
            t, r0, nrows = pieces[u]
            rows = outs[t].at[4 * block[0] + 2 * block[1] + block[2], pl.ds(r0, nrows)]
            return pltpu.make_async_remote_copy(
                src_ref=rows, dst_ref=rows, send_sem=send_sems.at[u, k], recv_sem=recv_sems.at[u, k],
                device_id=to, device_id_type=MESH)

        mine = pl.ds(_my_index(), 1)
        for t in range(n - 1):
            outs[t][mine] = ins[t][...].astype(BF16)[None]
        outs[n - 1][mine] = jnp.zeros((1,) + small_shape, F32)
        for (row, _), ref in zip(small_rows, small_refs):
            if len(ref.shape) == 3:
                for j in range(ref.shape[0]):
                    outs[n - 1][mine, row + j:row + j + 1, :] = ref[j][None]
            else:
                outs[n - 1][mine, row:row + ref.shape[0], :] = ref[...][None]
        started = []

        def start(cp):
            cp.start()
            started.append(cp)

        units = range(len(pieces))
        for u in units:
            start(copy(u, 0, me, sibling))
            start(copy(u, 1, me, (*x_nbr, c)))
            start(copy(u, 2, me, (*y_nbr, c)))
        for src, dst in zip(cast_refs, cast_outs):
            dst[...] = src[...].astype(BF16)
        _bias_table(rb_ref, bucket_ref, win_ref, bias_ref)
        for u in units:
            for k, chip in ((1, x_nbr), (2, y_nbr)):
                copy(u, k, (*chip, c), me).wait_recv()
                start(copy(u, 3 + k, (*chip, c), sibling))
            start(copy(u, 3, (*relayed, c), (*relay_to, c)))
        for u in units:
            copy(u, 3, (*diagonal, c), me).wait_recv()
            start(copy(u, 6, (*diagonal, c), sibling))
        for u in units:
            copy(u, 0, sibling, me).wait_recv()
        for k, chip in ((4, x_nbr), (5, y_nbr), (6, diagonal)):
            for u in units:
                copy(u, k, (*chip, 1 - c), me).wait_recv()
        for cp in started:
            cp.wait_send()

    vmem = pl.BlockSpec(memory_space=pltpu.VMEM)
    outs = pl.pallas_call(
        body,
        name="gather_weights",
        out_shape=[jax.ShapeDtypeStruct((N_DEV,) + s.shape, BF16) for s in shards]
        + [jax.ShapeDtypeStruct((N_DEV,) + small_shape, F32)]
        + [jax.ShapeDtypeStruct(a.shape, BF16) for a in casts]
        + [jax.ShapeDtypeStruct((2, N_PAIRS, BAND, 2 * BLOCK), F32)],
        in_specs=[vmem] * (n - 1 + ns + nc) + [pl.BlockSpec(memory_space=pltpu.SMEM), vmem, vmem],
        out_specs=[vmem] * (n + nc + 1),
        scratch_shapes=[pltpu.SemaphoreType.DMA((len(pieces), 7)), pltpu.SemaphoreType.DMA((len(pieces), 7))],
        compiler_params=_params(),
    )(*shards, *[a for _, a in small_rows], *casts, rel_bias_t, bucket_t, in_window_t)
    return outs[:n - 1], outs[n - 1], outs[n:n + nc], outs[n + nc]


def _peer(k):
    x, y, c = lax.axis_index("x"), lax.axis_index("y"), lax.axis_index("c")
    px = 1 - x if k & 4 else x
    py = 1 - y if k & 2 else y
    pc = 1 - c if k & 1 else c
    return (px, py, pc), 4 * px + 2 * py + pc


def _exchange(srcs, dsts, send_sems, recv_sems, local_sems, scatter):
    me = _my_index()
    sends, arrivals = [], []
    for k in range(1, N_DEV):
        peer, pidx = _peer(k)
        for t, (src, dst) in enumerate(zip(srcs, dsts)):
            mine = src.at[pidx] if scatter else src
            sems = dict(send_sem=send_sems.at[t, k - 1], recv_sem=recv_sems.at[t, k - 1], device_id=peer, device_id_type=MESH)
            sends.append(pltpu.make_async_remote_copy(src_ref=mine, dst_ref=dst.at[me], **sems))
            arrivals.append(pltpu.make_async_remote_copy(src_ref=mine, dst_ref=dst.at[pidx], **sems))
    local = [pltpu.make_async_copy(src.at[me] if scatter else src, dst.at[me], local_sems.at[t])
             for t, (src, dst) in enumerate(zip(srcs, dsts))]
    return sends, arrivals, local


def _exchange_start(*args):
    sends, _, local = _exchange(*args)
    for cp in sends + local:
        cp.start()


def _exchange_wait(*args):
    sends, arrivals, local = _exchange(*args)
    for cp in arrivals:
        cp.wait_recv()
    for cp in sends:
        cp.wait_send()
    for cp in local:
        cp.wait()


def _exchange_sems(n):
    if not n:
        return []
    return [pltpu.SemaphoreType.DMA((n, N_DEV - 1)), pltpu.SemaphoreType.DMA((n, N_DEV - 1)), pltpu.SemaphoreType.DMA((n,))]


HBM_SPEC = pl.BlockSpec(memory_space=pl.ANY)


def _sum_slots(recv_ref, out_ref):
    rows = out_ref.shape[0]
    chunk = min(rows, 128)

    def add(i, carry):
        r0 = pl.multiple_of(i * chunk, chunk)
        acc = recv_ref[0, pl.ds(r0, chunk), :].astype(F32)
        for dev in range(1, N_DEV):
            acc = acc + recv_ref[dev, pl.ds(r0, chunk), :].astype(F32)
        out_ref[pl.ds(r0, chunk), :] = acc
        return carry

    lax.fori_loop(0, rows // chunk, add, 0)


N_CHIPS = N_DEV // 2


def _rows_loop(rows, fn):
    chunk = min(rows, 128)

    def step(i, carry):
        fn(pl.ds(pl.multiple_of(i * chunk, chunk), chunk))
        return carry

    lax.fori_loop(0, rows // chunk, step, 0)


def _chip_reduce(g_ref, out_ref, sib_ref, land_ref, send_ref, sems, swap_src=None):
    sib_send, sib_recv, ici_send, ici_recv = sems
    x, y, c = lax.axis_index("x"), lax.axis_index("y"), lax.axis_index("c")
    south = c == 0
    near =(jnp.where(south, 1 - x, x), jnp.where(south, y, 1 - y))
    far = (jnp.where(south, x, 1 - x), jnp.where(south, 1 - y, y))
    diagonal = (1 - x, 1 - y)
    rows = out_ref.shape[0]
    direct, fold, folded = 0, 1, 2

    def to_sibling(t):
        src = g_ref if swap_src is None else swap_src
        return pltpu.make_async_remote_copy(
            src_ref=src.at[2 * t + 1 - c], dst_ref=sib_ref.at[t], send_sem=sib_send.at[t], recv_sem=sib_recv.at[t],
            device_id=(x, y, 1 - c), device_id_type=MESH)

    def ici(role, chip):
        return pltpu.make_async_remote_copy(
            src_ref=send_ref.at[role], dst_ref=land_ref.at[role], send_sem=ici_send.at[role],
            recv_sem=ici_recv.at[role], device_id=(*chip, c), device_id_type=MESH)

    def pair_sum(chip, r):
        t = 2 * chip[0] + chip[1]
        return g_ref[2 * t + c, r, :].astype(F32) + sib_ref[t, r, :].astype(F32)

    def swap():
        for t in range(N_CHIPS):
            to_sibling(t).start()

    def send():
        for t in range(N_CHIPS):
            to_sibling(t).wait_recv()
        for role, chip in ((fold, diagonal), (direct, near)):
            def fill(r, role=role, chip=chip):
                send_ref[role, r, :] = pair_sum(chip, r).astype(BF16)

            _rows_loop(rows, fill)
            ici(role, near).start()

    def forward():
        ici(fold, near).wait_recv()

        def fill(r):
            send_ref[folded, r, :] = (pair_sum(far, r) + land_ref[fold, r, :].astype(F32)).astype(BF16)

        _rows_loop(rows, fill)
        ici(folded, far).start()

    def finish():
        ici(direct, near).wait_recv()
        ici(folded, far).wait_recv()

        def total(r):
            mine = pair_sum((x, y), r)
            out_ref[r, :] = mine + land_ref[direct, r, :].astype(F32) + land_ref[folded, r, :].astype(F32)

        _rows_loop(rows, total)
        for t in range(N_CHIPS):
            to_sibling(t).wait_send()
        for role, chip in ((direct, near), (fold, near), (folded, far)):
            ici(role, chip).wait_send()

    return swap, send, forward, finish


def _chip_reduce_scratch(slot):
    return [pltpu.VMEM((N_CHIPS,) + slot, BF16), pltpu.VMEM((3,) + slot, BF16), pltpu.VMEM((3,) + slot, BF16),
            pltpu.SemaphoreType.DMA((N_CHIPS,)), pltpu.SemaphoreType.DMA((N_CHIPS,)),
            pltpu.SemaphoreType.DMA((3,)), pltpu.SemaphoreType.DMA((3,))]


def _reduce_exchange(part, landed, smalls):
    nl, ng = len(landed), len(smalls)
    n_out = 1 + nl + ng

    def body(*refs):
        p_in, l_in, s_in = refs[0], refs[1:1 + nl], refs[1 + nl:n_out]
        p_out, l_out, s_out = refs[n_out], refs[n_out + 1:n_out + 1 + nl], refs[n_out + 1 + nl:2 * n_out]
        scratch = refs[2 * n_out:]
        s_recv, (sib_ref, chip_ref, send_ref), sems = scratch[:ng], scratch[ng:ng + 3], scratch[ng + 3:]
        swap, send, forward, finish = _chip_reduce(p_in, p_out, sib_ref, chip_ref, send_ref, sems[:4])
        swap()
        _exchange_start(s_in, s_recv, *sems[4:], False)
        send()
        for t in range(nl):
            _sum_slots(l_in[t], l_out[t])
        forward()
        finish()
        _exchange_wait(s_in, s_recv, *sems[4:], False)
        for t in range(ng):
            acc = s_recv[t][0]
            for dev in range(1, N_DEV):
                acc = acc + s_recv[t][dev]
            s_out[t][...] = acc

    vmem = pl.BlockSpec(memory_space=pltpu.VMEM)
    slot = part.shape[1:]
    outs = pl.pallas_call(
        body,
        name="reduce_grads",
        out_shape=[jax.ShapeDtypeStruct(p.shape[1:], F32) for p in [part] + landed]
        + [jax.ShapeDtypeStruct(s.shape, F32) for s in smalls],
        in_specs=[vmem] * n_out,
        out_specs=[vmem] * n_out,
        scratch_shapes=[pltpu.VMEM((N_DEV,) + s.shape, F32) for s in smalls] + _chip_reduce_scratch(slot)
        + _exchange_sems(ng),
        compiler_params=_params(vmem_mib=56),
    )(part, *landed, *smalls)
    return outs[0], outs[1:1 + nl], outs[1 + nl:]


def _layer_a_fwd(x2, sm, win_g, wout, later, ts):
    seq, d = x2.shape
    width = wout.shape[0]
    half = win_g.shape[2]
    n_half = width // half
    nl = len(later)
    nt = seq // ts

    def body(x_ref, sm_ref, win_ref, wout_ref, *refs):
        shard_refs, refs = refs[:nl], refs[nl:]
        h1_ref, n1_ref, proj_ref, conv_ref, y_ref, ya_ref = refs[:6]
        gathered_refs, (vprev_ref, *sems) = refs[6:6 + nl], refs[6 + nl:]

        @pl.when(pl.program_id(0) == 0)
        def _():
            vprev_ref[...] = jnp.zeros_like(vprev_ref)
            _exchange_start(shard_refs, gathered_refs, *sems, False)

        @pl.when(pl.program_id(0) == nt - 1)
        def _():
            _exchange_wait(shard_refs, gathered_refs, *sems, False)

        xf = x_ref[...]
        xn, _ = _rms(xf)
        n1 = (xn * sm_ref[0:1, :]).astype(BF16)
        n1_ref[...] = n1
        row = lax.broadcasted_iota(jnp.int32, (ts, half), 0)
        ya = jnp.zeros((ts, d), F32)
        for hh in range(n_half):
            cols = slice(hh * half, (hh + 1) * half)
            parts = []
            for part in range(4):
                j = part * n_half + hh
                pj = _dot(n1, win_ref[j])
                proj_ref[:, j * half:(j + 1) * half] = pj.astype(BF16)
                parts.append(pj)
            b, c, u, z = parts
            v = c * u
            last1, last2 = vprev_ref[7:8, cols], vprev_ref[6:7, cols]
            v1 = jnp.where(row == 0, last1, pltpu.roll(v, 1, 0))
            v2 = jnp.where(row == 0, last2, jnp.where(row == 1, last1, pltpu.roll(v, 2, 0)))
            vprev_ref[:, cols] = v[ts - 8:ts, :]
            conv = sm_ref[1:2, cols] * v2 + sm_ref[2:3, cols] * v1 + sm_ref[3:4, cols] * v
            conv_ref[:, cols] = conv.astype(BF16)
            yh = (b * conv * _silu(z)[0]).astype(BF16)
            y_ref[:, cols] = yh
            ya = ya + _dot(yh, wout_ref[cols, :])
        ya_ref[...] = ya
        h1_ref[...] = xf + _rms(ya)[0] * sm_ref[4:5, :]

    outs = pl.pallas_call(
        body,
        name="layer_a_fwd",
        grid=(nt,),
        in_specs=[_rows(ts, d), _full(sm.shape), _full(win_g.shape), _full(wout.shape)] + [HBM_SPEC] * nl,
        out_specs=[_rows(ts, d), _rows(ts, d), _rows(ts, 4 * width), _rows(ts, width), _rows(ts, width), _rows(ts, d)]
        + [HBM_SPEC] * nl,
        out_shape=[
            jax.ShapeDtypeStruct((seq, d), F32),
            jax.ShapeDtypeStruct((seq, d), BF16),
            jax.ShapeDtypeStruct((seq, 4 * width), BF16),
            jax.ShapeDtypeStruct((seq, width), BF16),
            jax.ShapeDtypeStruct((seq, width), BF16),
            jax.ShapeDtypeStruct((seq, d), F32),
        ] + [jax.ShapeDtypeStruct((N_DEV,) + s.shape, s.dtype) for s in later],
        scratch_shapes=[pltpu.VMEM((8, width), F32)] + _exchange_sems(nl),
        compiler_params=_params(("arbitrary",), 56),
    )(x2, sm, win_g, wout, *later)
    return outs[:6], outs[6:]


Q_BLOCKS = 4
ATTN_BWD_LAGS = (2, 4)
ATTN_FWD_LAGS = (2, 4)


def _banded_tiles(kvp_ref, kvc_ref):
    tile = kvc_ref[...].astype(F32)
    blocks = [kvp_ref[...].astype(F32)] + [tile[u * BLOCK:(u + 1) * BLOCK] for u in range(Q_BLOCKS)]
    return [_banded_kv(blocks[u], blocks[u + 1]) for u in range(Q_BLOCKS)]


def _bias_of(bias_ref, i, u, m):
    return bias_ref[jnp.minimum(i, 1) if u == 0 else 1, m]


def _banded_kv(kvp, kvc):
    kw = N_KV_HEADS * HEAD_DIM
    out = []
    for full in (jnp.concatenate([kvp[:, :kw], kvc[:, :kw]], axis=0), jnp.concatenate([kvp[:, kw:], kvc[:, kw:]], axis=0)):
        lo = lax.broadcasted_iota(jnp.int32, full.shape, 1) < HEAD_DIM
        rolled = pltpu.roll(full, HEAD_DIM, 1)
        x2 = [jnp.where(lo, full, rolled).astype(BF16), jnp.where(lo, rolled, full).astype(BF16)]
        ft = full.T
        x2t = [jnp.concatenate([ft[kh * HEAD_DIM:(kh + 1) * HEAD_DIM]] * 2, axis=0).astype(BF16) for kh in range(N_KV_HEADS)]
        out += [x2, x2t]
    return out


def _pair_rows(ref, rows, m, scale=None):
    both = ref[rows, m * LANES:(m + 1) * LANES].astype(F32)
    if scale is not None:
        both = both * scale
    lo = lax.broadcasted_iota(jnp.int32, both.shape, 1) < HEAD_DIM
    zero = jnp.zeros_like(both)
    return jnp.concatenate([jnp.where(lo, both, zero), jnp.where(lo, zero, both)], axis=0).astype(BF16)


def _pair_cols(res_t):
    top = lax.broadcasted_iota(jnp.int32, (LANES, BLOCK), 0) < HEAD_DIM
    return jnp.where(top, res_t[:, :BLOCK], res_t[:, BLOCK:]).T


def _sink_row(sink_ref, m):
    first = lax.broadcasted_iota(jnp.int32, (1, 2 * BLOCK), 1) < BLOCK
    return jnp.where(first, sink_ref[0, 2 * m], sink_ref[0, 2 * m + 1])


def _softmax_t(logits, sink):
    mx =jnp.maximum(jnp.max(logits, axis=0, keepdims=True), sink)
    p = jnp.exp(logits - mx)
    sink_p = jnp.exp(sink - mx)
    inv = 1.0 / (jnp.sum(p, axis=0, keepdims=True) + sink_p)
    return p * inv, sink_p * inv


def _layer_b_fwd(h1, target, kvn, bpre, wkv, wbin_g, biasm, sinks, wbout, bpost):
    seq, d = h1.shape
    kvw = wkv.shape[1]
    cw = wbin_g.shape[2]
    aw = N_Q_HEADS * HEAD_DIM
    per = aw // cw
    tile = Q_BLOCKS * BLOCK

    def body(sink_ref, h1_ref, tgt_ref, kvn_ref, bpre_ref, wkv_ref, wbin_ref, bias_ref, w_ref, g_ref,
             n3_ref, n4_ref, kvc_ref, q_ref, o_ref, dh2_ref, dyb_ref, dattn_ref, dz2_ref, acc_ref,
             attn_ref, z2_ref, kvp_ref):
        i = pl.program_id(0)

        @pl.when(i == 0)
        def _():
            acc_ref[...] = jnp.zeros_like(acc_ref)
            kvp_ref[...] = jnp.zeros_like(kvp_ref)

        hn, _ = _rms(h1_ref[...])
        n3 = (hn * kvn_ref[...]).astype(BF16)
        n4 = (hn * bpre_ref[...]).astype(BF16)
        n3_ref[...] = n3
        n4_ref[...] = n4
        kvc_ref[...] = _dot(n3, wkv_ref[...]).astype(BF16)
        for j in range(N_DEV):
            pj = _dot(n4, wbin_ref[j])
            if j < per:
                q_ref[:, j * cw:(j + 1) * cw] = pj.astype(BF16)
            else:
                z2_ref[:, (j - per) * cw:(j - per + 1) * cw] = pj

        banded = _banded_tiles(kvp_ref, kvc_ref)
        kvp_ref[...] = kvc_ref[tile - BLOCK:tile, :]
        units = [(u, m) for u in range(Q_BLOCKS) for m in range(N_PAIRS)]
        kv_of = lambda m: (2 * m) // GROUP
        logits, probs = {}, {}
        lag_b, lag_c = ATTN_FWD_LAGS
        for step in range(len(units) + lag_c):
            if step < len(units):
                u, m = units[step]
                qpair = _pair_rows(q_ref, slice(u * BLOCK, (u + 1) * BLOCK), m, SCALE)
                logits[step] = _dot_nt(banded[u][0][kv_of(m)], qpair) + _bias_of(bias_ref, i, u, m)
            if 0 <= step - lag_b < len(units):
                u, m = units[step - lag_b]
                probs[step - lag_b] = _softmax_t(logits.pop(step - lag_b), _sink_row(sink_ref, m))[0].astype(BF16)
            if 0 <= step - lag_c < len(units):
                u, m = units[step - lag_c]
                out_t = _dot(banded[u][3][kv_of(m)], probs.pop(step - lag_c))
                attn_ref[u * BLOCK:(u + 1) * BLOCK, m * LANES:(m + 1) * LANES] = _pair_cols(out_t)
        attn = attn_ref[...]
        sz, dsz = _silu(z2_ref[...])
        o = (attn * sz).astype(BF16)
        o_ref[...] = o

        w = w_ref[...]
        yb = _dot(o, w)
        ybn, r = _rms(yb)
        g = g_ref[...]
        diff = h1_ref[...] + ybn * g - tgt_ref[...]
        dh2 = diff * (1.0 / d)
        dh2_ref[...] = dh2
        acc_ref[0:1, :] += jnp.sum(dh2 * ybn, axis=0, keepdims=True)
        tok = jnp.mean(diff * diff, axis=-1, keepdims=True)
        acc_ref[1:2, :] += 0.5 * jnp.sum(tok, axis=0, keepdims=True)
        dyb = _rms_bwd(dh2 * g, ybn, r).astype(BF16)
        dyb_ref[...] = dyb
        do = _dot_nt(dyb, w)
        dattn_ref[...] = (do * sz).astype(BF16)
        dz2_ref[...] = (do * attn * dsz).astype(BF16)

    blk = lambda w: pl.BlockSpec((tile, w), lambda i: (i, 0))
    return pl.pallas_call(
        body,
        name="layer_b_fwd",
        grid=(seq // tile,),
        in_specs=[
            pl.BlockSpec(memory_space=pltpu.SMEM),
            blk(d),
            blk(d),
            _full(kvn.shape),
            _full(bpre.shape),
            _full(wkv.shape),
            _full(wbin_g.shape),
            _full(biasm.shape),
            _full(wbout.shape),
            _full(bpost.shape),
        ],
        out_specs=[blk(d), blk(d), blk(kvw), blk(aw), blk(aw), blk(d), blk(d), blk(aw), blk(aw), _resident((8, d))],
        out_shape=[
            jax.ShapeDtypeStruct((seq, d), BF16),
            jax.ShapeDtypeStruct((seq, d), BF16),
            jax.ShapeDtypeStruct((seq, kvw), BF16),
            jax.ShapeDtypeStruct((seq, aw), BF16),
            jax.ShapeDtypeStruct((seq, aw), BF16),
            jax.ShapeDtypeStruct((seq, d), F32),
            jax.ShapeDtypeStruct((seq, d), BF16),
            jax.ShapeDtypeStruct((seq, aw), BF16),
            jax.ShapeDtypeStruct((seq, aw), BF16),
            jax.ShapeDtypeStruct((8, d), F32),
        ],
        scratch_shapes=[pltpu.VMEM((tile, aw), F32), pltpu.VMEM((tile, aw), F32), pltpu.VMEM((BLOCK, kvw), BF16)],
        compiler_params=_params(("arbitrary",), 56),
    )(sinks, h1, target, kvn, bpre, wkv, wbin_g, biasm, wbout, bpost)


def _attn_bwd(q, kv, dattn, biasm, sinks, ready):
    seq, aw = q.shape
    kvw = kv.shape[1]
    kw = N_KV_HEADS * HEAD_DIM
    nb = seq // BLOCK
    pairs_per_kv = N_PAIRS // N_KV_HEADS
    nr = len(ready)

    tile = Q_BLOCKS * BLOCK
    nsteps = seq // tile
    held = (Q_BLOCKS - 1) * BLOCK

    def body(sink_ref, q_ref, kvc_ref, kvp_ref, da_ref, bias_ref, *refs):
        ready_refs, (dq_ref, dkv_ref, dssum_ref, dsink_ref) = refs[:nr], refs[nr:nr + 4]
        landed_refs, scratch = refs[nr + 4:2 * nr + 4], refs[2 * nr + 4:]
        carry_ref, done_ref, qs_ref, dos_ref, dst_ref, pt_ref, *sems = scratch
        i = pl.program_id(0)

        @pl.when(i == 0)
        def _():
            dssum_ref[...] = jnp.zeros_like(dssum_ref)
            dsink_ref[...] = jnp.zeros_like(dsink_ref)
            carry_ref[...] = jnp.zeros_like(carry_ref)
            done_ref[...] = jnp.zeros_like(done_ref)
            if nr:
                _exchange_start(ready_refs, landed_refs, *sems, True)

        if nr:
            @pl.when(i == nsteps)
            def _():
                _exchange_wait(ready_refs, landed_refs, *sems, True)

        @pl.when(i < nsteps)
        def _():
            lo = lax.broadcasted_iota(jnp.int32, (BAND, LANES), 1) < HEAD_DIM
            head_lane = lax.broadcasted_iota(jnp.int32, (1, LANES), 1)
            banded = _banded_tiles(kvp_ref, kvc_ref)
            units = [(u, m) for u in range(Q_BLOCKS) for m in range(N_PAIRS)]
            dsink = jnp.zeros((1, LANES), F32)
            folded = {}
            logits, dps, dsbs = {}, {}, {}
            lag_b, lag_c = ATTN_BWD_LAGS
            for step in range(len(units) + lag_c):
                if step < len(units):
                    u, m = units[step]
                    kh, rows = m // pairs_per_kv, slice((m % pairs_per_kv) * BAND, (m % pairs_per_kv + 1) * BAND)
                    qrows = slice(u * BLOCK, (u + 1) * BLOCK)
                    qpair = _pair_rows(q_ref, qrows, m, SCALE)
                    dopair = _pair_rows(da_ref, qrows, m)
                    qs_ref[u, kh, rows, :] = qpair
                    dos_ref[u, kh, rows, :] = dopair
                    logits[step] = _dot_nt(banded[u][0][kh], qpair) + _bias_of(bias_ref, i, u, m)
                    dps[step] = _dot_nt(banded[u][2][kh], dopair)
                if 0 <= step - lag_b < len(units):
                    u, m = units[step - lag_b]
                    kh, rows = m // pairs_per_kv, slice((m % pairs_per_kv) * BAND, (m % pairs_per_kv + 1) * BAND)
                    pn, sink_p = _softmax_t(logits.pop(step - lag_b), _sink_row(sink_ref, m))
                    dp = dps.pop(step - lag_b)
                    delta = jnp.sum(pn * dp, axis=0, keepdims=True)
                    ds = pn * (dp - delta)
                    dssum_ref[m] += ds
                    sink_term = sink_p * delta
                    for e in range(2):
                        total = jnp.sum(sink_term[:, e * BLOCK:(e + 1) * BLOCK], axis=1, keepdims=True)
                        dsink = dsink - jnp.where(head_lane == 2 * m + e, total, 0.0)
                    dsbs[step - lag_b] = ds.astype(BF16)
                    dst_ref[u, kh, :, rows] = dsbs[step - lag_b]
                    pt_ref[u, kh, :, rows] = pn.astype(BF16)
                if 0 <= step - lag_c < len(units):
                    u, m = units[step - lag_c]
                    kh = m // pairs_per_kv
                    dq_t = _dot(banded[u][1][kh], dsbs.pop(step - lag_c))
                    dq_ref[u * BLOCK:(u + 1) * BLOCK, m * LANES:(m + 1) * LANES] = (_pair_cols(dq_t) * SCALE).astype(BF16)
                    if m % pairs_per_kv == pairs_per_kv - 1:
                        for name, lhs_ref, rhs_ref in (("k", dst_ref, qs_ref), ("v", pt_ref, dos_ref)):
                            acc = _dot(lhs_ref[u, kh], rhs_ref[u, kh])
                            folded[u, kh, name] = acc + pltpu.roll(acc, HEAD_DIM, 1)
            dsink_ref[0:1, :] += dsink
            dkv = [jnp.concatenate([jnp.where(lo, folded[u, 0, n], folded[u, 1, n]) for n in ("k", "v")], axis=1)
                   for u in range(Q_BLOCKS)]

            @pl.when(i > 0)
            def _():
                if held:
                    dkv_ref[:held, :] = done_ref[...].astype(BF16)
                dkv_ref[held:, :] = (carry_ref[...] + dkv[0][:BLOCK]).astype(BF16)

            for u in range(Q_BLOCKS - 1):
                done_ref[u * BLOCK:(u + 1) * BLOCK, :] = dkv[u][BLOCK:] + dkv[u + 1][:BLOCK]
            carry_ref[...] = dkv[Q_BLOCKS - 1][BLOCK:]

        @pl.when(i == nsteps)
        def _():
            if held:
                dkv_ref[:held, :] = done_ref[...].astype(BF16)
            dkv_ref[held:, :] = carry_ref[...].astype(BF16)

    last = nsteps - 1
    blk = lambda w: pl.BlockSpec((tile, w), lambda i: (jnp.minimum(i, last), 0))
    outs = pl.pallas_call(
        body,
        name="attn_bwd",
        grid=(nsteps + 1,),
        in_specs=[
            pl.BlockSpec(memory_space=pltpu.SMEM),
            blk(aw),
            blk(kvw),
            pl.BlockSpec((BLOCK, kvw), lambda i: (jnp.clip(Q_BLOCKS * i - 1, 0, nb - 1), 0)),
            blk(aw),
            _full(biasm.shape),
        ] + [HBM_SPEC] * nr,
        out_specs=[
            blk(aw),
            pl.BlockSpec((tile, kvw), lambda i: (jnp.maximum(i - 1, 0), 0)),
            _resident(biasm.shape[1:]),
            _resident((8, LANES)),
        ] + [HBM_SPEC] * nr,
        out_shape=[
            jax.ShapeDtypeStruct((seq, aw), BF16),
            jax.ShapeDtypeStruct((seq, kvw), BF16),
            jax.ShapeDtypeStruct(biasm.shape[1:], F32),
            jax.ShapeDtypeStruct((8, LANES), F32),
        ] + [jax.ShapeDtypeStruct(g.shape, g.dtype) for g in ready],
        scratch_shapes=[
            pltpu.VMEM((BLOCK, kvw), F32),
            pltpu.VMEM((max(held, 8), kvw), F32),
            pltpu.VMEM((Q_BLOCKS, N_KV_HEADS, pairs_per_kv * BAND, LANES), BF16),
            pltpu.VMEM((Q_BLOCKS, N_KV_HEADS, pairs_per_kv * BAND, LANES), BF16),
            pltpu.VMEM((Q_BLOCKS, N_KV_HEADS, BAND, pairs_per_kv * BAND), BF16),
            pltpu.VMEM((Q_BLOCKS, N_KV_HEADS, BAND, pairs_per_kv * BAND), BF16),
        ] + _exchange_sems(nr),
        compiler_params=_params(("arbitrary",), 48),
    )(sinks, q, kv, kv, dattn, biasm, *ready)
    return outs[:4], outs[4:]


def _relbias_grad(dssum2, bucket_row, chunk):
    heads, n = dssum2.shape

    def body(a_ref, bucket_ref, out_ref):
        @pl.when(pl.program_id(0) == 0)
        def _():
            out_ref[...] = jnp.zeros_like(out_ref)

        a = a_ref[...]
        hi = a.astype(BF16)
        lo = (a - hi.astype(F32)).astype(BF16)
        onehot_t = (lax.broadcasted_iota(jnp.int32, (LANES, chunk), 0) == bucket_ref[...]).astype(F32).astype(BF16)
        out_ref[...] += _dot_nt(hi, onehot_t) + _dot_nt(lo, onehot_t)

    return pl.pallas_call(
        body,
        name="relbias_grad",
        grid=(n // chunk,),
        in_specs=[pl.BlockSpec((heads, chunk), lambda i: (0, i)), pl.BlockSpec((1, chunk), lambda i: (0, i))],
        out_specs=_resident((heads, LANES)),
        out_shape=jax.ShapeDtypeStruct((heads, LANES), F32),
        compiler_params=_params(("arbitrary",), 32),
    )(dssum2, bucket_row)


def _layer_b_in_bwd(dh2, dq, dz2, dkv, h1, ya, wbin_g, wkv, kvn, bpre, sm, ready, ts):
    seq, d = h1.shape
    aw = dq.shape[1]
    kvw = dkv.shape[1]
    cw = wbin_g.shape[2]
    per = aw // cw

    nr = len(ready)
    nt = seq // ts

    def body(dh2_ref, dq_ref, dz2_ref, dkv_ref, h1_ref, ya_ref, wbin_ref, wkv_ref, kvn_ref, bpre_ref, sm_ref, *refs):
        ready_refs, (dh1_ref, dya_ref, acc_ref) = refs[:nr], refs[nr:nr + 3]
        landed_refs, sems = refs[nr + 3:2 * nr + 3], refs[2 * nr + 3:]

        @pl.when(pl.program_id(0) == 0)
        def _():
            acc_ref[...] = jnp.zeros_like(acc_ref)
            _exchange_start(ready_refs, landed_refs, *sems, True)

        @pl.when(pl.program_id(0) == nt - 1)
        def _():
            _exchange_wait(ready_refs, landed_refs, *sems, True)

        dn4 = jnp.zeros((ts, d), F32)
        for j in range(N_DEV):
            src = dq_ref if j < per else dz2_ref
            jj = j % per
            dn4 = dn4 + _dot_nt(src[:, jj * cw:(jj + 1) * cw], wbin_ref[j])
        dn3 = _dot_nt(dkv_ref[...], wkv_ref[...])
        hn, r = _rms(h1_ref[...])
        acc_ref[0:1, :] += jnp.sum(dn4 * hn, axis=0, keepdims=True)
        acc_ref[1:2, :] += jnp.sum(dn3 * hn, axis=0, keepdims=True)
        dh1 = dh2_ref[...] + _rms_bwd(dn4 * bpre_ref[...] + dn3 * kvn_ref[...], hn, r)
        dh1_ref[...] = dh1
        yan, r2 = _rms(ya_ref[...])
        acc_ref[2:3, :] += jnp.sum(dh1 * yan, axis=0, keepdims=True)
        dya_ref[...] = _rms_bwd(dh1 * sm_ref[4:5, :], yan, r2).astype(BF16)

    outs = pl.pallas_call(
        body,
        name="layer_b_in_bwd",
        grid=(nt,),
        in_specs=[_rows(ts, d), _rows(ts, aw), _rows(ts, aw), _rows(ts, kvw), _rows(ts, d), _rows(ts, d),
                  _full(wbin_g.shape), _full(wkv.shape), _full(kvn.shape), _full(bpre.shape), _full(sm.shape)]
        + [HBM_SPEC] * nr,
        out_specs=[_rows(ts, d), _rows(ts, d), _resident((8, d))] + [HBM_SPEC] * nr,
        out_shape=[jax.ShapeDtypeStruct((seq, d), F32), jax.ShapeDtypeStruct((seq, d), BF16),
                   jax.ShapeDtypeStruct((8, d), F32)] + [jax.ShapeDtypeStruct(g.shape, g.dtype) for g in ready],
        scratch_shapes=_exchange_sems(nr),
        compiler_params=_params(("arbitrary",), 48),
    )(dh2, dq, dz2, dkv, h1, ya, wbin_g, wkv, kvn, bpre, sm, *ready)
    return outs[:3], outs[3:]


def _layer_a_bwd(dya, proj, conv, dh1, x2, wout, win_g, sm, ts):
    seq, d = x2.shape
    width = wout.shape[0]
    half = win_g.shape[2]
    n_half = width // half
    nt = seq // ts

    def body(dya_ref, proj_ref, conv_ref, dh1_ref, x_ref, wout_ref, win_ref, sm_ref, dproj_ref, gx_ref, acc_ref,
             dnext_ref):
        @pl.when(pl.program_id(0) == 0)
        def _():
            acc_ref[...] = jnp.zeros_like(acc_ref)
            dnext_ref[...] = jnp.zeros_like(dnext_ref)

        dy = _dot_nt(dya_ref[...], wout_ref[...])
        row = lax.broadcasted_iota(jnp.int32, (ts, half), 0)
        dn1 = jnp.zeros((ts, d), F32)
        for hh in range(n_half):
            cols = slice(hh * half, (hh + 1) * half)
            b, c, u, z = [proj_ref[:, (part * n_half + hh) * half:(part * n_half + hh + 1) * half].astype(F32)
                          for part in range(4)]
            cv = conv_ref[:, cols].astype(F32)
            dyh = dy[:, cols]
            sz, dsz = _silu(z)
            dconv = dyh * b * sz
            grads = [dyh * cv * sz, None, None, dyh * b * cv * dsz]
            next0, next1 = dnext_ref[0:1, cols], dnext_ref[1:2, cols]
            dc1 = jnp.where(row == ts - 1, next0, pltpu.roll(dconv, ts - 1, 0))
            dc2 = jnp.where(row == ts - 1, next1, jnp.where(row == ts - 2, next0, pltpu.roll(dconv, ts - 2, 0)))
            dnext_ref[:, cols] = dconv[0:8, :]
            v = c * u
            acc_ref[1:2, cols] += jnp.sum(dc2 * v, axis=0, keepdims=True)
            acc_ref[2:3, cols] += jnp.sum(dc1 * v, axis=0, keepdims=True)
            acc_ref[3:4, cols] += jnp.sum(dconv * v, axis=0, keepdims=True)
            dv = sm_ref[3:4, cols] * dconv + sm_ref[2:3, cols] * dc1 + sm_ref[1:2, cols] * dc2
            grads[1] = dv * u
            grads[2] = dv * c
            for part in range(4):
                j = part * n_half + hh
                gj = grads[part].astype(BF16)
                dproj_ref[:, j * half:(j + 1) * half] = gj
                dn1 = dn1 + _dot_nt(gj, win_ref[j])
        xn, r = _rms(x_ref[...])
        acc_ref[0:1, :] += jnp.sum(dn1 * xn, axis=0, keepdims=True)
        gx_ref[...] = dh1_ref[...] + _rms_bwd(dn1 * sm_ref[0:1, :], xn, r)

    rev = lambda w: pl.BlockSpec((ts, w), lambda i: (nt - 1 - i, 0))
    return pl.pallas_call(
        body,
        name="layer_a_bwd",
        grid=(nt,),
        in_specs=[rev(d), rev(4 * width), rev(width), rev(d), rev(d), _full(wout.shape), _full(win_g.shape), _full(sm.shape)],
        out_specs=[rev(4 * width), rev(d), _resident((8, d))],
        out_shape=[jax.ShapeDtypeStruct((seq, 4 * width), BF16), jax.ShapeDtypeStruct((seq, d), F32),
                   jax.ShapeDtypeStruct((8, d), F32)],
        scratch_shapes=[pltpu.VMEM((8, width), F32)],
        compiler_params=_params(("arbitrary",), 56),
    )(dya, proj, conv, dh1, x2, wout, win_g, sm)


def _wgrad(a, bs, n_slots, ts, name, ready=(), block_cols=1024):
    nr = len(ready)
    seq, k = a.shape
    nb_in = len(bs)
    n_each = bs[0].shape[1]
    n = nb_in * n_each
    bn = min(n_each, block_cols)
    per_in = n_each // bn
    n_blocks = nb_in * per_in
    ns = seq // ts

    def b_spec(idx):
        def index(j, s):
            mine = j // per_in == idx
            row = jnp.where(mine, s, jnp.where(j // per_in > idx, ns - 1, 0))
            return (row, jnp.where(mine, j % per_in, jnp.where(j // per_in > idx, per_in - 1, 0)))
        return pl.BlockSpec((ts, bn), index)

    if n_slots:
        sw = n // n_slots
        spb = bn // sw
        out_shape = jax.ShapeDtypeStruct((n_slots, k, sw), BF16)
        out_spec = pl.BlockSpec((spb, k, sw), lambda j, s: (j, 0, 0))
    else:
        out_shape = jax.ShapeDtypeStruct((k, n), BF16)
        out_spec = pl.BlockSpec((k, bn), lambda j, s: (0, j))

    def body(a_ref, *refs):
        b_refs, ready_refs, o_ref = refs[:nb_in], refs[nb_in:nb_in + nr], refs[nb_in + nr]
        landed_refs, (acc_ref, *sems) = refs[nb_in + nr + 1:nb_in + 2 * nr + 1], refs[nb_in + 2 * nr + 1:]
        j, s = pl.program_id(0), pl.program_id(1)

        if nr:
            @pl.when(jnp.logical_and(j == 0, s == 0))
            def _():
                _exchange_start(ready_refs, landed_refs, *sems, True)

            @pl.when(jnp.logical_and(j == n_blocks - 1, s == ns - 1))
            def _():
                _exchange_wait(ready_refs, landed_refs, *sems, True)

        @pl.when(s == 0)
        def _():
            acc_ref[...] = jnp.zeros_like(acc_ref)

        for idx in range(nb_in):
            @pl.when(j // per_in == idx)
            def _(idx=idx):
                acc_ref[...] += _dot_tn(a_ref[...], b_refs[idx][...])

        @pl.when(s == ns - 1)
        def _():
            if n_slots:
                for e in range(spb):
                    o_ref[e] = acc_ref[:, e * sw:(e + 1) * sw].astype(BF16)
            else:
                o_ref[...] = acc_ref[...].astype(BF16)

    outs = pl.pallas_call(
        body,
        name=name,
        grid=(n_blocks, ns),
        in_specs=[pl.BlockSpec((ts, k), lambda j, s: (s, 0))] + [b_spec(idx) for idx in range(nb_in)] + [HBM_SPEC] * nr,
        out_specs=[out_spec] + [HBM_SPEC] * nr,
        out_shape=[out_shape] + [jax.ShapeDtypeStruct(g.shape, g.dtype) for g in ready],
        scratch_shapes=[pltpu.VMEM((k, bn), F32)] + (_exchange_sems(nr) if nr else []),
        compiler_params=_params(("arbitrary", "arbitrary"), 48),
    )(a, *bs, *ready)
    return (outs[0], outs[1:]) if nr else outs[0]


def _wgrad_tail(pairs, part, landed, ts):
    n_tasks = len(pairs)
    assert n_tasks == 2
    nl = len(landed)
    seq, k = pairs[0][0].shape
    n = pairs[0][1].shape[1]
    ns = seq // ts
    total = n_tasks * ns
    per = k // N_DEV
    n_red = len(_chip_reduce_scratch((per, n)))

    def spec(t, width):
        return pl.BlockSpec((ts, width), lambda j, s: (jnp.where(j == t, s, jnp.where(j > t, ns - 1, 0)), 0))

    def body(*refs):
        ab_refs, part_hbm = refs[:2 * n_tasks], refs[2 * n_tasks]
        landed_hbm, refs = refs[2 * n_tasks + 1:2 * n_tasks + 1 + nl], refs[2 * n_tasks + 1 + nl:]
        o_ref, red_ref, early_ref = refs[:3]
        summed_refs, (acc_ref, first_ref, part_ref, *scratch) = refs[3:3 + nl], refs[3 + nl:]
        landed_refs, load_sems, scratch = scratch[:nl], scratch[nl], scratch[nl + 1:]
        j, s = pl.program_id(0), pl.program_id(1)
        flat = j * ns + s
        swap, send, forward, finish = _chip_reduce(part_ref, red_ref, *scratch[:3], scratch[3:n_red], part_hbm)
        swap_first, send_first, forward_first, finish_first = _chip_reduce(
            first_ref, early_ref, *scratch[n_red:n_red + 3], scratch[n_red + 3:])
        loads = [pltpu.make_async_copy(src, dst, load_sems.at[i])
                 for i, (src, dst) in enumerate(zip([part_hbm, *landed_hbm], [part_ref, *landed_refs]))]

        @pl.when(flat == 0)
        def _():
            swap()
            for load in loads:
                load.start()

        @pl.when(flat == min(1, total - 1))
        def _():
            loads[0].wait()
            send()

        @pl.when(flat == min(total // 2 + 1, total - 1))
        def _():
            forward()
            for t in range(nl):
                loads[1 + t].wait()
                _sum_slots(landed_refs[t], summed_refs[t])

        @pl.when(flat == ns)
        def _():
            send_first()

        @pl.when(flat == min(ns + ns // 2, total - 1))
        def _():
            forward_first()

        @pl.when(s == 0)
        def _():
            acc_ref[...] = jnp.zeros_like(acc_ref)

        for t in range(n_tasks):
            @pl.when(j == t)
            def _(t=t):
                acc_ref[...] += _dot_tn(ab_refs[2 * t][...], ab_refs[2 * t + 1][...])

        @pl.when(flat == ns - 1)
        def _():
            for dev in range(N_DEV):
                first_ref[dev] = acc_ref[dev * per:(dev + 1) * per, :].astype(BF16)
            swap_first()

        @pl.when(flat == total - 1)
        def _():
            for dev in range(N_DEV):
                o_ref[dev] = acc_ref[dev * per:(dev + 1) * per, :].astype(BF16)
            finish()
            finish_first()

    slot = part.shape[1:]
    outs = pl.pallas_call(
        body,
        name="wgrad_tail",
        grid=(n_tasks, ns),
        in_specs=[spec(t, w) for t in range(n_tasks) for w in (k, n)] + [HBM_SPEC] * (1 + nl),
        out_specs=[_resident((N_DEV, per, n)), _resident(slot), _resident((per, n))]
        + [_resident(g.shape[1:]) for g in landed],
        out_shape=[jax.ShapeDtypeStruct((N_DEV, per, n), BF16), jax.ShapeDtypeStruct(slot, F32),
                   jax.ShapeDtypeStruct((per, n), F32)]
        + [jax.ShapeDtypeStruct(g.shape[1:], F32) for g in landed],
        scratch_shapes=[pltpu.VMEM((k, n), F32), pltpu.VMEM((N_DEV, per, n), BF16), pltpu.VMEM(part.shape, part.dtype)]
        + [pltpu.VMEM(g.shape, g.dtype) for g in landed] + [pltpu.SemaphoreType.DMA((1 + nl,))]
        + _chip_reduce_scratch(slot) + _chip_reduce_scratch((per, n)),
        compiler_params=_params(("arbitrary", "arbitrary")),
    )(*[op for pair in pairs for op in pair], part, *landed)
    return outs[0], outs[1], outs[2], outs[3:]


MINE = "mine"
ADAMW_STEPS = 4


def _adamw(ws, sources, picks, loss_at, ms, vs):
    n, n_src = len(ws), len(sources)
    streamed = [len(w.shape) == 2 and w.shape[0] >= 128 and picks[t][1:] == (0, None)
                and sources[picks[t][0]].shape == w.shape for t, w in enumerate(ws)]
    streamed_sources = {picks[t][0] for t in range(n) if streamed[t]}

    def step(w, g, m, v):
        m = ADAM_B1 * m + (1.0 - ADAM_B1) * g
        v = ADAM_B2 * v + (1.0 - ADAM_B2) * jnp.square(g)
        m_hat = m / (1.0 - ADAM_B1 ** ADAM_STEP)
        v_hat = v / (1.0 - ADAM_B2 ** ADAM_STEP)
        return g, -ADAM_LR * (m_hat / (jnp.sqrt(v_hat) + ADAM_EPS) + ADAM_WD * w), m, v

    def body(*refs):
        refs = list(refs)
        take = lambda k: [refs.pop(0) for _ in range(k)]
        w_refs, s_refs, m_refs, v_refs = take(n), take(n_src), take(n), take(n)
        (loss_ref,), go_refs, d_refs, nm_refs, nv_refs = take(1), take(n), take(n), take(n), take(n)
        me = _my_index()

        def grad(t, rows):
            k, first, cols = picks[t]
            if cols is None:
                return s_refs[k][rows, :]
            if cols is not MINE:
                return s_refs[k][rows, cols]
            width = w_refs[t].shape[-1]
            g = s_refs[k][rows, 0:width]
            for dev in range(1, N_DEV):
                g = jnp.where(me == dev, s_refs[k][rows, dev * width:(dev + 1) * width], g)
            return g

        def whole(t):
            first = picks[t][1]
            rows = w_refs[t].shape[0]
            if len(w_refs[t].shape) == 3:
                for j in range(rows):
                    go_refs[t][j], d_refs[t][j], nm_refs[t][j], nv_refs[t][j] = step(
                        w_refs[t][j], grad(t, slice(first + j, first + j + 1)), m_refs[t][j], v_refs[t][j])
                return
            go_refs[t][...], d_refs[t][...], nm_refs[t][...], nv_refs[t][...] = step(
                w_refs[t][...], grad(t, slice(first, first + rows)), m_refs[t][...], v_refs[t][...])

        def block(t):
            rows = w_refs[t].shape[0]
            chunk = min(rows, 128)

            def one(i, carry):
                r = pl.ds(pl.multiple_of(i * chunk, chunk), chunk)
                go_refs[t][r, :], d_refs[t][r, :], nm_refs[t][r, :], nv_refs[t][r, :] = step(
                    w_refs[t][r, :], grad(t, r), m_refs[t][r, :], v_refs[t][r, :])
                return carry

            lax.fori_loop(0, rows // chunk, one, 0)

        @pl.when(pl.program_id(0) == 0)
        def _():
            loss_ref[...] = s_refs[loss_at[0]][loss_at[1]:loss_at[1] + 1, 0:1]
            for t in range(n):
                if not streamed[t]:
                    whole(t)

        for t in range(n):
            if streamed[t]:
                block(t)

    def rows_of(shape):
        return pl.BlockSpec((shape[0] // ADAMW_STEPS, shape[1]), lambda i: (i, 0))

    w_in = [rows_of(w.shape) if streamed[t] else _full(w.shape) for t, w in enumerate(ws)]
    w_out = [rows_of(w.shape) if streamed[t] else _resident(w.shape) for t, w in enumerate(ws)]
    s_in = [rows_of(s.shape) if k in streamed_sources else _full(s.shape) for k, s in enumerate(sources)]
    outs = pl.pallas_call(
        body,
        name="adamw",
        grid=(ADAMW_STEPS,),
        in_specs=w_in + s_in + w_in * 2,
        out_specs=[_resident((1, 1))] + w_out * 4,
        out_shape=[jax.ShapeDtypeStruct((1, 1), F32)] + [jax.ShapeDtypeStruct(w.shape, F32) for w in ws] * 4,
        compiler_params=_params(("arbitrary",)),
    )(*ws, *sources, *ms, *vs)
    return outs[0], outs[1:n + 1], outs[n + 1:2 * n + 1], outs[2 * n + 1:3 * n + 1], outs[3 * n + 1:]


def _band_structure():
    q_loc = np.arange(BLOCK, dtype=np.int32)[:, None]
    s_loc = np.arange(2 * BLOCK, dtype=np.int32)[None, :]
    dist = q_loc + BLOCK - s_loc
    in_window = (dist >= 0) & (dist < BLOCK)
    dd = np.maximum(dist, 0)
    max_exact = N_BUCKETS // 2
    large = max_exact + (np.log(np.maximum(dd, 1) / max_exact) / math.log(MAX_DISTANCE / max_exact)
                         * (N_BUCKETS - max_exact)).astype(np.int32)
    bucket = np.where(dd < max_exact, dd, np.minimum(large, N_BUCKETS - 1)).astype(np.int32)
    return bucket, in_window.astype(np.int32)


def kernel(x, a_pre_norm, a_w_in, a_conv_w, a_w_out, a_post_norm, kv_norm, w_kv, rel_bias, b_pre_norm, b_w_in, b_sinks, b_w_out, b_post_norm, loss_target, m_a_pre_norm, m_a_w_in, m_a_conv_w, m_a_w_out, m_a_post_norm, m_kv_norm, m_w_kv, m_rel_bias, m_b_pre_norm, m_b_w_in, m_b_sinks, m_b_w_out, m_b_post_norm, v_a_pre_norm, v_a_w_in, v_a_conv_w, v_a_w_out, v_a_post_norm, v_kv_norm, v_w_kv, v_rel_bias, v_b_pre_norm, v_b_w_in, v_b_sinks, v_b_w_out, v_b_post_norm):
    seq, d = x.shape[1], x.shape[2]
    x2 = x.reshape(seq, d)
    target = loss_target.reshape(seq, d)
    shard = a_pre_norm.shape[1]
    ts_a = min(seq, 512)
    ts = min(seq, 512)
    ts_w = min(seq, 2048)

    taps = lambda a: a.transpose(1, 0, 2)
    bucket, in_window = _band_structure()
    (win_g, wout_g), small_g, later, biasm = _all_gather(
        [a_w_in[0], a_w_out[0]], [(0, a_pre_norm), (1, taps(a_conv_w)), (4, a_post_norm)],
        [w_kv, b_w_in[0], b_w_out[0]], rel_bias.T, bucket.T, in_window.T)
    wout = wout_g.reshape(-1, wout_g.shape[2])
    sm = small_g.transpose(1, 0, 2).reshape(8, N_DEV * shard)
    kvn = kv_norm.reshape(1, d)

    (h1, n1, proj, conv, y, ya), (wkv_g, wbin_g, wbout_g) = _layer_a_fwd(x2, sm, win_g, wout, later, ts_a)
    wkv = wkv_g.reshape(-1, wkv_g.shape[2])
    wbout = wbout_g.reshape(-1, wbout_g.shape[2])
    n3, n4, kv, q, o, dh2, dyb, dattn, dz2, acc_c = _layer_b_fwd(
        h1, target, kvn, b_pre_norm, wkv, wbin_g, biasm, b_sinks, wbout, b_post_norm)

    (dq, dkv, dssum, dsink), _ = _attn_bwd(q, kv, dattn, biasm, b_sinks, [])
    by_head = dssum.reshape(N_PAIRS, BAND, 2, BLOCK).transpose(0, 2, 3, 1)
    relb = _relbias_grad(by_head.reshape(N_Q_HEADS, -1), bucket.reshape(1, -1), 4096)
    g_wkv = _wgrad(n3, [dkv], 0, ts_w, "wgrad_kv").reshape(wkv_g.shape)
    g_wbin = _wgrad(n4, [dq, dz2], N_DEV, ts_w, "wgrad_b_in")
    (dh1, dya, acc_b), (l_wkv, l_wbin) = _layer_b_in_bwd(
        dh2, dq, dz2, dkv, h1, ya, wbin_g, wkv, kvn, b_pre_norm, sm, [g_wkv, g_wbin], ts)
    dproj, gx, acc_a = _layer_a_bwd(dya, proj, conv, dh1, x2, wout, win_g, sm, ts_a)
    g_win = _wgrad(n1, [dproj], N_DEV, ts_w, "wgrad_a_in", block_cols=2048)
    g_wbout, r_win, r_wout, (r_wkv, r_wbin) = _wgrad_tail(
        [(y, dya), (o, dyb)], g_win, [l_wkv, l_wbin], min(seq, 1024))

    r_wbout, _, (s_a, s_b, s_c, s_relb, s_sink) = _reduce_exchange(g_wbout, [], [acc_a, acc_b, acc_c, relb, dsink])
    weights = [a_pre_norm, a_w_in[0], taps(a_conv_w), a_w_out[0], a_post_norm, kvn, w_kv, rel_bias.T, b_pre_norm,
               b_w_in[0], b_sinks, b_w_out[0], b_post_norm]
    sources = [s_a, s_b, s_c, s_relb, s_sink, r_win, r_wout, r_wkv, r_wbin, r_wbout]
    picks = [(0, 0, MINE), (5, 0, None), (0, 1, MINE), (6, 0, None), (1, 2, MINE), (1, 1, None), (7, 0, None),
             (3, 0, slice(0, N_BUCKETS)), (1, 0, None), (8, 0, None), (4, 0, slice(0, N_Q_HEADS)),
             (9, 0, None), (2, 0, None)]
    first = [m_a_pre_norm, m_a_w_in[0], taps(m_a_conv_w), m_a_w_out[0], m_a_post_norm, m_kv_norm.reshape(1, d),
             m_w_kv, m_rel_bias.T, m_b_pre_norm, m_b_w_in[0], m_b_sinks, m_b_w_out[0], m_b_post_norm]
    second = [v_a_pre_norm, v_a_w_in[0], taps(v_a_conv_w), v_a_w_out[0], v_a_post_norm, v_kv_norm.reshape(1, d),
              v_w_kv, v_rel_bias.T, v_b_pre_norm, v_b_w_in[0], v_b_sinks, v_b_w_out[0], v_b_post_norm]
    loss, grads, deltas, new_m, new_v = _adamw(weights, sources, picks, (2, 1), first, second)

    shapes = [a_pre_norm.shape, a_w_in.shape, taps, a_w_out.shape, a_post_norm.shape, kv_norm.shape,
              w_kv.shape, jnp.transpose, b_pre_norm.shape, b_w_in.shape, b_sinks.shape, b_w_out.shape, b_post_norm.shape]
    shaped = lambda arrays: [s(a) if callable(s) else a.reshape(s) for a, s in zip(arrays, shapes)]
    return (loss.reshape(()), gx.reshape(x.shape), *shaped(grads), *shaped(deltas), *shaped(new_m), *shaped(new_v))
```

```python
import math

import jax
import jax.numpy as jnp
import numpy as np
from jax import lax
from jax.experimental import pallas as pl
from jax.experimental.pallas import tpu as pltpu

HEAD_DIM = 64
N_Q_HEADS = 16
N_KV_HEADS = 2
GROUP = N_Q_HEADS // N_KV_HEADS
BLOCK = 128
N_BUCKETS = 32
MAX_DISTANCE = 128
EPS = 1e-6
NEG_INF = -1e30
SCALE = HEAD_DIM ** -0.5

ADAM_LR = 0.001
ADAM_B1 = 0.9
ADAM_B2 = 0.999
ADAM_EPS = 1e-08
ADAM_WD = 0.01
ADAM_STEP = 10

N_PAIRS = N_Q_HEADS // 2
BAND = 2 * BLOCK

N_DEV = 8
GATHER_PIECE_ROWS = 256
LANES = 128
F32 = jnp.float32
BF16 = jnp.bfloat16
MESH = pl.DeviceIdType.MESH
MIB = 1024 * 1024
VMEM_RESERVED_MIB = 63


def _params(semantics=None, vmem_mib=48):
    del vmem_mib
    return pltpu.CompilerParams(dimension_semantics=semantics, vmem_limit_bytes=VMEM_RESERVED_MIB * MIB)


def _full(shape):
    zeros = (0,) * len(shape)
    return pl.BlockSpec(shape, lambda *_: zeros, pipeline_mode=pl.Buffered(1))


def _resident(shape):
    zeros = (0,) * len(shape)
    return pl.BlockSpec(shape, lambda *_: zeros)


def _rows(ts, cols):
    return pl.BlockSpec((ts, cols), lambda i: (i, 0))


def _dot(a, b):
    return jnp.dot(a, b, preferred_element_type=F32)


def _dot_nt(a, b):
    return lax.dot_general(a, b, (((1,), (1,)), ((), ())), preferred_element_type=F32)


def _dot_tn(a, b):
    return lax.dot_general(a, b, (((0,), (0,)), ((), ())), preferred_element_type=F32)


def _rms(xf):
    r = lax.rsqrt(jnp.mean(xf * xf, axis=-1, keepdims=True) + EPS)
    return xf * r, r


def _rms_bwd(dn, xn, r):
    return r * (dn - xn * jnp.mean(dn * xn, axis=-1, keepdims=True))


def _silu(z):
    s = jax.nn.sigmoid(z)
    return z * s, s * (1.0 + z * (1.0 - s))


def _my_index():
    return 4 * lax.axis_index("x") + 2 * lax.axis_index("y") + lax.axis_index("c")


def _bias_table(rb_ref, bucket_ref, win_ref, out_ref):
    bk = jnp.where(win_ref[...] != 0, bucket_ref[...], -1)
    has_prev = lax.broadcasted_iota(jnp.int32, bk.shape, 0) >= BLOCK
    for h in range(N_Q_HEADS):
        acc = jnp.full(bk.shape, NEG_INF, F32)
        for b in range(N_BUCKETS):
            acc = jnp.where(bk == b, rb_ref[h, b], acc)
        cols = slice((h % 2) * BLOCK, (h % 2 + 1) * BLOCK)
        out_ref[1, h // 2, :, cols] = acc
        out_ref[0, h // 2, :, cols] = jnp.where(has_prev, acc, NEG_INF)


def _all_gather(shards, small_rows, casts, rel_bias_t, bucket_t, in_window_t):
    ns, nc, n = len(small_rows), len(casts), len(shards) + 1
    small_shape = (8, small_rows[0][1].shape[-1])
    shapes = [s.shape for s in shards] + [small_shape]
    pieces = [(t, r0, min(GATHER_PIECE_ROWS, shape[0] - r0))
              for t, shape in enumerate(shapes) for r0 in range(0, shape[0], GATHER_PIECE_ROWS)]

    def body(*refs):
        refs = list(refs)
        take = lambda k: [refs.pop(0) for _ in range(k)]
        ins, small_refs, cast_refs, (rb_ref, bucket_ref, win_ref) = take(n - 1), take(ns), take(nc), take(3)
        outs, cast_outs, (bias_ref, send_sems, recv_sems) = take(n), take(nc), take(3)
        x, y, c = lax.axis_index("x"), lax.axis_index("y"), lax.axis_index("c")
        me, sibling = (x, y, c), (x, y, 1 - c)
        x_nbr, y_nbr, diagonal = (1 - x, y), (x, 1 - y), (1 - x, 1 - y)
        south = c == 0
        relayed = (jnp.where(south, 1 - x, x), jnp.where(south, y, 1 - y))
        relay_to = (jnp.where(south, x, 1 - x), jnp.where(south, 1 - y, y))

        def copy(u, k, block, to):
            t, r0, nrows = pieces[u]
            rows = outs[t].at[4 * block[0] + 2 * block[1] + block[2], pl.ds(r0, nrows)]
            return pltpu.make_async_remote_copy(
                src_ref=rows, dst_ref=rows, send_sem=send_sems.at[u, k], recv_sem=recv_sems.at[u, k],
                device_id=to, device_id_type=MESH)

        mine = pl.ds(_my_index(), 1)
        for t in range(n - 1):
            outs[t][mine] = ins[t][...].astype(BF16)[None]
        outs[n - 1][mine] = jnp.zeros((1,) + small_shape, F32)
        for (row, _), ref in zip(small_rows, small_refs):
            if len(ref.shape) == 3:
                for j in range(ref.shape[0]):
                    outs[n - 1][mine, row + j:row + j + 1, :] = ref[j][None]
            else:
                outs[n - 1][mine, row:row + ref.shape[0], :] = ref[...][None]
        started = []

        def start(cp):
            cp.start()
            started.append(cp)

        units = range(len(pieces))
        for u in units:
            start(copy(u, 0, me, sibling))
            start(copy(u, 1, me, (*x_nbr, c)))
            start(copy(u, 2, me, (*y_nbr, c)))
        for src, dst in zip(cast_refs, cast_outs):
            dst[...] = src[...].astype(BF16)
        _bias_table(rb_ref, bucket_ref, win_ref, bias_ref)
        for u in units:
            for k, chip in ((1, x_nbr), (2, y_nbr)):
                copy(u, k, (*chip, c), me).wait_recv()
                start(copy(u, 3 + k, (*chip, c), sibling))
            start(copy(u, 3, (*relayed, c), (*relay_to, c)))
        for u in units:
            copy(u, 3, (*diagonal, c), me).wait_recv()
            start(copy(u, 6, (*diagonal, c), sibling))
        for u in units:
            copy(u, 0, sibling, me).wait_recv()
        for k, chip in ((4, x_nbr), (5, y_nbr), (6, diagonal)):
            for u in units:
                copy(u, k, (*chip, 1 - c), me).wait_recv()
        for cp in started:
            cp.wait_send()

    vmem = pl.BlockSpec(memory_space=pltpu.VMEM)
    outs = pl.pallas_call(
        body,
        name="gather_weights",
        out_shape=[jax.ShapeDtypeStruct((N_DEV,) + s.shape, BF16) for s in shards]
        + [jax.ShapeDtypeStruct((N_DEV,) + small_shape, F32)]
        + [jax.ShapeDtypeStruct(a.shape, BF16) for a in casts]
        + [jax.ShapeDtypeStruct((2, N_PAIRS, BAND, 2 * BLOCK), F32)],
        in_specs=[vmem] * (n - 1 + ns + nc) + [pl.BlockSpec(memory_space=pltpu.SMEM), vmem, vmem],
        out_specs=[vmem] * (n + nc + 1),
        scratch_shapes=[pltpu.SemaphoreType.DMA((len(pieces), 7)), pltpu.SemaphoreType.DMA((len(pieces), 7))],
        compiler_params=_params(),
    )(*shards, *[a for _, a in small_rows], *casts, rel_bias_t, bucket_t, in_window_t)
    return outs[:n - 1], outs[n - 1], outs[n:n + nc], outs[n + nc]


def _peer(k):
    x, y, c = lax.axis_index("x"), lax.axis_index("y"), lax.axis_index("c")
    px = 1 - x if k & 4 else x
    py = 1 - y if k & 2 else y
    pc = 1 - c if k & 1 else c
    return (px, py, pc), 4 * px + 2 * py + pc


def _exchange(srcs, dsts, send_sems, recv_sems, local_sems, scatter):
    me = _my_index()
    sends, arrivals = [], []
    for k in range(1, N_DEV):
        peer, pidx = _peer(k)
        for t, (src, dst) in enumerate(zip(srcs, dsts)):
            mine = src.at[pidx] if scatter else src
            sems = dict(send_sem=send_sems.at[t, k - 1], recv_sem=recv_sems.at[t, k - 1], device_id=peer, device_id_type=MESH)
            sends.append(pltpu.make_async_remote_copy(src_ref=mine, dst_ref=dst.at[me], **sems))
            arrivals.append(pltpu.make_async_remote_copy(src_ref=mine, dst_ref=dst.at[pidx], **sems))
    local = [pltpu.make_async_copy(src.at[me] if scatter else src, dst.at[me], local_sems.at[t])
             for t, (src, dst) in enumerate(zip(srcs, dsts))]
    return sends, arrivals, local


def _exchange_start(*args):
    sends, _, local = _exchange(*args)
    for cp in sends + local:
        cp.start()


def _exchange_wait(*args):
    sends, arrivals, local = _exchange(*args)
    for cp in arrivals:
        cp.wait_recv()
    for cp in sends:
        cp.wait_send()
    for cp in local:
        cp.wait()


def _exchange_sems(n):
    if not n:
        return []
    return [pltpu.SemaphoreType.DMA((n, N_DEV - 1)), pltpu.SemaphoreType.DMA((n, N_DEV - 1)), pltpu.SemaphoreType.DMA((n,))]


HBM_SPEC = pl.BlockSpec(memory_space=pl.ANY)


def _sum_slots(recv_ref, out_ref):
    rows = out_ref.shape[0]
    chunk = min(rows, 128)

    def add(i, carry):
        r0 = pl.multiple_of(i * chunk, chunk)
        acc = recv_ref[0, pl.ds(r0, chunk), :].astype(F32)
        for dev in range(1, N_DEV):
            acc = acc + recv_ref[dev, pl.ds(r0, chunk), :].astype(F32)
        out_ref[pl.ds(r0, chunk), :] = acc
        return carry

    lax.fori_loop(0, rows // chunk, add, 0)


N_CHIPS = N_DEV // 2


def _rows_loop(rows, fn):
    chunk = min(rows, 128)

    def step(i, carry):
        fn(pl.ds(pl.multiple_of(i * chunk, chunk), chunk))
        return carry

    lax.fori_loop(0, rows // chunk, step, 0)


def _chip_reduce(g_ref, out_ref, sib_ref, land_ref, send_ref, sems, swap_src=None):
    sib_send, sib_recv, ici_send, ici_recv = sems
    x, y, c = lax.axis_index("x"), lax.axis_index("y"), lax.axis_index("c")
    south = c == 0
    near =(jnp.where(south, 1 - x, x), jnp.where(south, y, 1 - y))
    far = (jnp.where(south, x, 1 - x), jnp.where(south, 1 - y, y))
    diagonal = (1 - x, 1 - y)
    rows = out_ref.shape[0]
    direct, fold, folded = 0, 1, 2

    def to_sibling(t):
        src = g_ref if swap_src is None else swap_src
        return pltpu.make_async_remote_copy(
            src_ref=src.at[2 * t + 1 - c], dst_ref=sib_ref.at[t], send_sem=sib_send.at[t], recv_sem=sib_recv.at[t],
            device_id=(x, y, 1 - c), device_id_type=MESH)

    def ici(role, chip):
        return pltpu.make_async_remote_copy(
            src_ref=send_ref.at[role], dst_ref=land_ref.at[role], send_sem=ici_send.at[role],
            recv_sem=ici_recv.at[role], device_id=(*chip, c), device_id_type=MESH)

    def pair_sum(chip, r):
        t = 2 * chip[0] + chip[1]
        return g_ref[2 * t + c, r, :].astype(F32) + sib_ref[t, r, :].astype(F32)

    def swap():
        for t in range(N_CHIPS):
            to_sibling(t).start()

    def send():
        for t in range(N_CHIPS):
            to_sibling(t).wait_recv()
        for role, chip in ((fold, diagonal), (direct, near)):
            def fill(r, role=role, chip=chip):
                send_ref[role, r, :] = pair_sum(chip, r).astype(BF16)

            _rows_loop(rows, fill)
            ici(role, near).start()

    def forward():
        ici(fold, near).wait_recv()

        def fill(r):
            send_ref[folded, r, :] = (pair_sum(far, r) + land_ref[fold, r, :].astype(F32)).astype(BF16)

        _rows_loop(rows, fill)
        ici(folded, far).start()

    def finish():
        ici(direct, near).wait_recv()
        ici(folded, far).wait_recv()

        def total(r):
            mine = pair_sum((x, y), r)
            out_ref[r, :] = mine + land_ref[direct, r, :].astype(F32) + land_ref[folded, r, :].astype(F32)

        _rows_loop(rows, total)
        for t in range(N_CHIPS):
            to_sibling(t).wait_send()
        for role, chip in ((direct, near), (fold, near), (folded, far)):
            ici(role, chip).wait_send()

    return swap, send, forward, finish


def _chip_reduce_scratch(slot):
    return [pltpu.VMEM((N_CHIPS,) + slot, BF16), pltpu.VMEM((3,) + slot, BF16), pltpu.VMEM((3,) + slot, BF16),
            pltpu.SemaphoreType.DMA((N_CHIPS,)), pltpu.SemaphoreType.DMA((N_CHIPS,)),
            pltpu.SemaphoreType.DMA((3,)), pltpu.SemaphoreType.DMA((3,))]


def _reduce_exchange(part, landed, smalls):
    nl, ng = len(landed), len(smalls)
    n_out = 1 + nl + ng

    def body(*refs):
        p_in, l_in, s_in = refs[0], refs[1:1 + nl], refs[1 + nl:n_out]
        p_out, l_out, s_out = refs[n_out], refs[n_out + 1:n_out + 1 + nl], refs[n_out + 1 + nl:2 * n_out]
        scratch = refs[2 * n_out:]
        s_recv, (sib_ref, chip_ref, send_ref), sems = scratch[:ng], scratch[ng:ng + 3], scratch[ng + 3:]
        swap, send, forward, finish = _chip_reduce(p_in, p_out, sib_ref, chip_ref, send_ref, sems[:4])
        swap()
        _exchange_start(s_in, s_recv, *sems[4:], False)
        send()
        for t in range(nl):
            _sum_slots(l_in[t], l_out[t])
        forward()
        finish()
        _exchange_wait(s_in, s_recv, *sems[4:], False)
        for t in range(ng):
            acc = s_recv[t][0]
            for dev in range(1, N_DEV):
                acc = acc + s_recv[t][dev]
            s_out[t][...] = acc

    vmem = pl.BlockSpec(memory_space=pltpu.VMEM)
    slot = part.shape[1:]
    outs = pl.pallas_call(
        body,
        name="reduce_grads",
        out_shape=[jax.ShapeDtypeStruct(p.shape[1:], F32) for p in [part] + landed]
        + [jax.ShapeDtypeStruct(s.shape, F32) for s in smalls],
        in_specs=[vmem] * n_out,
        out_specs=[vmem] * n_out,
        scratch_shapes=[pltpu.VMEM((N_DEV,) + s.shape, F32) for s in smalls] + _chip_reduce_scratch(slot)
        + _exchange_sems(ng),
        compiler_params=_params(vmem_mib=56),
    )(part, *landed, *smalls)
    return outs[0], outs[1:1 + nl], outs[1 + nl:]


def _layer_a_fwd(x2, sm, win_g, wout, later, ts):
    seq, d = x2.shape
    width = wout.shape[0]
    half = win_g.shape[2]
    n_half = width // half
    nl = len(later)
    nt = seq // ts

    def body(x_ref, sm_ref, win_ref, wout_ref, *refs):
        shard_refs, refs = refs[:nl], refs[nl:]
        h1_ref, n1_ref, proj_ref, conv_ref, y_ref, ya_ref = refs[:6]
        gathered_refs, (vprev_ref, *sems) = refs[6:6 + nl], refs[6 + nl:]

        @pl.when(pl.program_id(0) == 0)
        def _():
            vprev_ref[...] = jnp.zeros_like(vprev_ref)
            _exchange_start(shard_refs, gathered_refs, *sems, False)

        @pl.when(pl.program_id(0) == nt - 1)
        def _():
            _exchange_wait(shard_refs, gathered_refs, *sems, False)

        xf = x_ref[...]
        xn, _ = _rms(xf)
        n1 = (xn * sm_ref[0:1, :]).astype(BF16)
        n1_ref[...] = n1
        row = lax.broadcasted_iota(jnp.int32, (ts, half), 0)
        ya = jnp.zeros((ts, d), F32)
        for hh in range(n_half):
            cols = slice(hh * half, (hh + 1) * half)
            parts = []
            for part in range(4):
                j = part * n_half + hh
                pj = _dot(n1, win_ref[j])
                proj_ref[:, j * half:(j + 1) * half] = pj.astype(BF16)
                parts.append(pj)
            b, c, u, z = parts
            v = c * u
            last1, last2 = vprev_ref[7:8, cols], vprev_ref[6:7, cols]
            v1 = jnp.where(row == 0, last1, pltpu.roll(v, 1, 0))
            v2 = jnp.where(row == 0, last2, jnp.where(row == 1, last1, pltpu.roll(v, 2, 0)))
            vprev_ref[:, cols] = v[ts - 8:ts, :]
            conv = sm_ref[1:2, cols] * v2 + sm_ref[2:3, cols] * v1 + sm_ref[3:4, cols] * v
            conv_ref[:, cols] = conv.astype(BF16)
            yh = (b * conv * _silu(z)[0]).astype(BF16)
            y_ref[:, cols] = yh
            ya = ya + _dot(yh, wout_ref[cols, :])
        ya_ref[...] = ya
        h1_ref[...] = xf + _rms(ya)[0] * sm_ref[4:5, :]

    outs = pl.pallas_call(
        body,
        name="layer_a_fwd",
        grid=(nt,),
        in_specs=[_rows(ts, d), _full(sm.shape), _full(win_g.shape), _full(wout.shape)] + [HBM_SPEC] * nl,
        out_specs=[_rows(ts, d), _rows(ts, d), _rows(ts, 4 * width), _rows(ts, width), _rows(ts, width), _rows(ts, d)]
        + [HBM_SPEC] * nl,
        out_shape=[
            jax.ShapeDtypeStruct((seq, d), F32),
            jax.ShapeDtypeStruct((seq, d), BF16),
            jax.ShapeDtypeStruct((seq, 4 * width), BF16),
            jax.ShapeDtypeStruct((seq, width), BF16),
            jax.ShapeDtypeStruct((seq, width), BF16),
            jax.ShapeDtypeStruct((seq, d), F32),
        ] + [jax.ShapeDtypeStruct((N_DEV,) + s.shape, s.dtype) for s in later],
        scratch_shapes=[pltpu.VMEM((8, width), F32)] + _exchange_sems(nl),
        compiler_params=_params(("arbitrary",), 56),
    )(x2, sm, win_g, wout, *later)
    return outs[:6], outs[6:]


Q_BLOCKS = 4
ATTN_BWD_LAGS = (2, 4)
ATTN_FWD_LAGS = (2, 4)


def _banded_tiles(kvp_ref, kvc_ref):
    tile = kvc_ref[...].astype(F32)
    blocks = [kvp_ref[...].astype(F32)] + [tile[u * BLOCK:(u + 1) * BLOCK] for u in range(Q_BLOCKS)]
    return [_banded_kv(blocks[u], blocks[u + 1]) for u in range(Q_BLOCKS)]


def _bias_of(bias_ref, i, u, m):
    return bias_ref[jnp.minimum(i, 1) if u == 0 else 1, m]


def _banded_kv(kvp, kvc):
    kw = N_KV_HEADS * HEAD_DIM
    out = []
    for full in (jnp.concatenate([kvp[:, :kw], kvc[:, :kw]], axis=0), jnp.concatenate([kvp[:, kw:], kvc[:, kw:]], axis=0)):
        lo = lax.broadcasted_iota(jnp.int32, full.shape, 1) < HEAD_DIM
        rolled = pltpu.roll(full, HEAD_DIM, 1)
        x2 = [jnp.where(lo, full, rolled).astype(BF16), jnp.where(lo, rolled, full).astype(BF16)]
        ft = full.T
        x2t = [jnp.concatenate([ft[kh * HEAD_DIM:(kh + 1) * HEAD_DIM]] * 2, axis=0).astype(BF16) for kh in range(N_KV_HEADS)]
        out += [x2, x2t]
    return out


def _pair_rows(ref, rows, m, scale=None):
    both = ref[rows, m * LANES:(m + 1) * LANES].astype(F32)
    if scale is not None:
        both = both * scale
    lo = lax.broadcasted_iota(jnp.int32, both.shape, 1) < HEAD_DIM
    zero = jnp.zeros_like(both)
    return jnp.concatenate([jnp.where(lo, both, zero), jnp.where(lo, zero, both)], axis=0).astype(BF16)


def _pair_cols(res_t):
    top = lax.broadcasted_iota(jnp.int32, (LANES, BLOCK), 0) < HEAD_DIM
    return jnp.where(top, res_t[:, :BLOCK], res_t[:, BLOCK:]).T


def _sink_row(sink_ref, m):
    first = lax.broadcasted_iota(jnp.int32, (1, 2 * BLOCK), 1) < BLOCK
    return jnp.where(first, sink_ref[0, 2 * m], sink_ref[0, 2 * m + 1])


def _softmax_t(logits, sink):
    mx =jnp.maximum(jnp.max(logits, axis=0, keepdims=True), sink)
    p = jnp.exp(logits - mx)
    sink_p = jnp.exp(sink - mx)
    inv = 1.0 / (jnp.sum(p, axis=0, keepdims=True) + sink_p)
    return p * inv, sink_p * inv


def _layer_b_fwd(h1, target, kvn, bpre, wkv, wbin_g, biasm, sinks, wbout, bpost):
    seq, d = h1.shape
    kvw = wkv.shape[1]
    cw = wbin_g.shape[2]
    aw = N_Q_HEADS * HEAD_DIM
    per = aw // cw
    tile = Q_BLOCKS * BLOCK

    def body(sink_ref, h1_ref, tgt_ref, kvn_ref, bpre_ref, wkv_ref, wbin_ref, bias_ref, w_ref, g_ref,
             n3_ref, n4_ref, kvc_ref, q_ref, o_ref, dh2_ref, dyb_ref, dattn_ref, dz2_ref, acc_ref,
             attn_ref, z2_ref, kvp_ref):
        i = pl.program_id(0)

        @pl.when(i == 0)
        def _():
            acc_ref[...] = jnp.zeros_like(acc_ref)
            kvp_ref[...] = jnp.zeros_like(kvp_ref)

        hn, _ = _rms(h1_ref[...])
        n3 = (hn * kvn_ref[...]).astype(BF16)
        n4 = (hn * bpre_ref[...]).astype(BF16)
        n3_ref[...] = n3
        n4_ref[...] = n4
        kvc_ref[...] = _dot(n3, wkv_ref[...]).astype(BF16)
        for j in range(N_DEV):
            pj = _dot(n4, wbin_ref[j])
            if j < per:
                q_ref[:, j * cw:(j + 1) * cw] = pj.astype(BF16)
            else:
                z2_ref[:, (j - per) * cw:(j - per + 1) * cw] = pj

        banded = _banded_tiles(kvp_ref, kvc_ref)
        kvp_ref[...] = kvc_ref[tile - BLOCK:tile, :]
        units = [(u, m) for u in range(Q_BLOCKS) for m in range(N_PAIRS)]
        kv_of = lambda m: (2 * m) // GROUP
        logits, probs = {}, {}
        lag_b, lag_c = ATTN_FWD_LAGS
        for step in range(len(units) + lag_c):
            if step < len(units):
                u, m = units[step]
                qpair = _pair_rows(q_ref, slice(u * BLOCK, (u + 1) * BLOCK), m, SCALE)
                logits[step] = _dot_nt(banded[u][0][kv_of(m)], qpair) + _bias_of(bias_ref, i, u, m)
            if 0 <= step - lag_b < len(units):
                u, m = units[step - lag_b]
                probs[step - lag_b] = _softmax_t(logits.pop(step - lag_b), _sink_row(sink_ref, m))[0].astype(BF16)
            if 0 <= step - lag_c < len(units):
                u, m = units[step - lag_c]
                out_t = _dot(banded[u][3][kv_of(m)], probs.pop(step - lag_c))
                attn_ref[u * BLOCK:(u + 1) * BLOCK, m * LANES:(m + 1) * LANES] = _pair_cols(out_t)
        attn = attn_ref[...]
        sz, dsz = _silu(z2_ref[...])
        o = (attn * sz).astype(BF16)
        o_ref[...] = o

        w = w_ref[...]
        yb = _dot(o, w)
        ybn, r = _rms(yb)
        g = g_ref[...]
        diff = h1_ref[...] + ybn * g - tgt_ref[...]
        dh2 = diff * (1.0 / d)
        dh2_ref[...] = dh2
        acc_ref[0:1, :] += jnp.sum(dh2 * ybn, axis=0, keepdims=True)
        tok = jnp.mean(diff * diff, axis=-1, keepdims=True)
        acc_ref[1:2, :] += 0.5 * jnp.sum(tok, axis=0, keepdims=True)
        dyb = _rms_bwd(dh2 * g, ybn, r).astype(BF16)
        dyb_ref[...] = dyb
        do = _dot_nt(dyb, w)
        dattn_ref[...] = (do * sz).astype(BF16)
        dz2_ref[...] = (do * attn * dsz).astype(BF16)

    blk = lambda w: pl.BlockSpec((tile, w), lambda i: (i, 0))
    return pl.pallas_call(
        body,
        name="layer_b_fwd",
        grid=(seq // tile,),
        in_specs=[
            pl.BlockSpec(memory_space=pltpu.SMEM),
            blk(d),
            blk(d),
            _full(kvn.shape),
            _full(bpre.shape),
            _full(wkv.shape),
            _full(wbin_g.shape),
            _full(biasm.shape),
            _full(wbout.shape),
            _full(bpost.shape),
        ],
        out_specs=[blk(d), blk(d), blk(kvw), blk(aw), blk(aw), blk(d), blk(d), blk(aw), blk(aw), _resident((8, d))],
        out_shape=[
            jax.ShapeDtypeStruct((seq, d), BF16),
            jax.ShapeDtypeStruct((seq, d), BF16),
            jax.ShapeDtypeStruct((seq, kvw), BF16),
            jax.ShapeDtypeStruct((seq, aw), BF16),
            jax.ShapeDtypeStruct((seq, aw), BF16),
            jax.ShapeDtypeStruct((seq, d), F32),
            jax.ShapeDtypeStruct((seq, d), BF16),
            jax.ShapeDtypeStruct((seq, aw), BF16),
            jax.ShapeDtypeStruct((seq, aw), BF16),
            jax.ShapeDtypeStruct((8, d), F32),
        ],
        scratch_shapes=[pltpu.VMEM((tile, aw), F32), pltpu.VMEM((tile, aw), F32), pltpu.VMEM((BLOCK, kvw), BF16)],
        compiler_params=_params(("arbitrary",), 56),
    )(sinks, h1, target, kvn, bpre, wkv, wbin_g, biasm, wbout, bpost)


def _attn_bwd(q, kv, dattn, biasm, sinks, ready):
    seq, aw = q.shape
    kvw = kv.shape[1]
    kw = N_KV_HEADS * HEAD_DIM
    nb = seq // BLOCK
    pairs_per_kv = N_PAIRS // N_KV_HEADS
    nr = len(ready)

    tile = Q_BLOCKS * BLOCK
    nsteps = seq // tile
    held = (Q_BLOCKS - 1) * BLOCK

    def body(sink_ref, q_ref, kvc_ref, kvp_ref, da_ref, bias_ref, *refs):
        ready_refs, (dq_ref, dkv_ref, dssum_ref, dsink_ref) = refs[:nr], refs[nr:nr + 4]
        landed_refs, scratch = refs[nr + 4:2 * nr + 4], refs[2 * nr + 4:]
        carry_ref, done_ref, qs_ref, dos_ref, dst_ref, pt_ref, *sems = scratch
        i = pl.program_id(0)

        @pl.when(i == 0)
        def _():
            dssum_ref[...] = jnp.zeros_like(dssum_ref)
            dsink_ref[...] = jnp.zeros_like(dsink_ref)
            carry_ref[...] = jnp.zeros_like(carry_ref)
            done_ref[...] = jnp.zeros_like(done_ref)
            if nr:
                _exchange_start(ready_refs, landed_refs, *sems, True)

        if nr:
            @pl.when(i == nsteps)
            def _():
                _exchange_wait(ready_refs, landed_refs, *sems, True)

        @pl.when(i < nsteps)
        def _():
            lo = lax.broadcasted_iota(jnp.int32, (BAND, LANES), 1) < HEAD_DIM
            head_lane = lax.broadcasted_iota(jnp.int32, (1, LANES), 1)
            banded = _banded_tiles(kvp_ref, kvc_ref)
            units = [(u, m) for u in range(Q_BLOCKS) for m in range(N_PAIRS)]
            dsink = jnp.zeros((1, LANES), F32)
            folded = {}
            logits, dps, dsbs = {}, {}, {}
            lag_b, lag_c = ATTN_BWD_LAGS
            for step in range(len(units) + lag_c):
                if step < len(units):
                    u, m = units[step]
                    kh, rows = m // pairs_per_kv, slice((m % pairs_per_kv) * BAND, (m % pairs_per_kv + 1) * BAND)
                    qrows = slice(u * BLOCK, (u + 1) * BLOCK)
                    qpair = _pair_rows(q_ref, qrows, m, SCALE)
                    dopair = _pair_rows(da_ref, qrows, m)
                    qs_ref[u, kh, rows, :] = qpair
                    dos_ref[u, kh, rows, :] = dopair
                    logits[step] = _dot_nt(banded[u][0][kh], qpair) + _bias_of(bias_ref, i, u, m)
                    dps[step] = _dot_nt(banded[u][2][kh], dopair)
                if 0 <= step - lag_b < len(units):
                    u, m = units[step - lag_b]
                    kh, rows = m // pairs_per_kv, slice((m % pairs_per_kv) * BAND, (m % pairs_per_kv + 1) * BAND)
                    pn, sink_p = _softmax_t(logits.pop(step - lag_b), _sink_row(sink_ref, m))
                    dp = dps.pop(step - lag_b)
                    delta = jnp.sum(pn * dp, axis=0, keepdims=True)
                    ds = pn * (dp - delta)
                    dssum_ref[m] += ds
                    sink_term = sink_p * delta
                    for e in range(2):
                        total = jnp.sum(sink_term[:, e * BLOCK:(e + 1) * BLOCK], axis=1, keepdims=True)
                        dsink = dsink - jnp.where(head_lane == 2 * m + e, total, 0.0)
                    dsbs[step - lag_b] = ds.astype(BF16)
                    dst_ref[u, kh, :, rows] = dsbs[step - lag_b]
                    pt_ref[u, kh, :, rows] = pn.astype(BF16)
                if 0 <= step - lag_c < len(units):
                    u, m = units[step - lag_c]
                    kh = m // pairs_per_kv
                    dq_t = _dot(banded[u][1][kh], dsbs.pop(step - lag_c))
                    dq_ref[u * BLOCK:(u + 1) * BLOCK, m * LANES:(m + 1) * LANES] = (_pair_cols(dq_t) * SCALE).astype(BF16)
                    if m % pairs_per_kv == pairs_per_kv - 1:
                        for name, lhs_ref, rhs_ref in (("k", dst_ref, qs_ref), ("v", pt_ref, dos_ref)):
                            acc = _dot(lhs_ref[u, kh], rhs_ref[u, kh])
                            folded[u, kh, name] = acc + pltpu.roll(acc, HEAD_DIM, 1)
            dsink_ref[0:1, :] += dsink
            dkv = [jnp.concatenate([jnp.where(lo, folded[u, 0, n], folded[u, 1, n]) for n in ("k", "v")], axis=1)
                   for u in range(Q_BLOCKS)]

            @pl.when(i > 0)
            def _():
                if held:
                    dkv_ref[:held, :] = done_ref[...].astype(BF16)
                dkv_ref[held:, :] = (carry_ref[...] + dkv[0][:BLOCK]).astype(BF16)

            for u in range(Q_BLOCKS - 1):
                done_ref[u * BLOCK:(u + 1) * BLOCK, :] = dkv[u][BLOCK:] + dkv[u + 1][:BLOCK]
            carry_ref[...] = dkv[Q_BLOCKS - 1][BLOCK:]

        @pl.when(i == nsteps)
        def _():
            if held:
                dkv_ref[:held, :] = done_ref[...].astype(BF16)
            dkv_ref[held:, :] = carry_ref[...].astype(BF16)

    last = nsteps - 1
    blk = lambda w: pl.BlockSpec((tile, w), lambda i: (jnp.minimum(i, last), 0))
    outs = pl.pallas_call(
        body,
        name="attn_bwd",
        grid=(nsteps + 1,),
        in_specs=[
            pl.BlockSpec(memory_space=pltpu.SMEM),
            blk(aw),
            blk(kvw),
            pl.BlockSpec((BLOCK, kvw), lambda i: (jnp.clip(Q_BLOCKS * i - 1, 0, nb - 1), 0)),
            blk(aw),
            _full(biasm.shape),
        ] + [HBM_SPEC] * nr,
        out_specs=[
            blk(aw),
            pl.BlockSpec((tile, kvw), lambda i: (jnp.maximum(i - 1, 0), 0)),
            _resident(biasm.shape[1:]),
            _resident((8, LANES)),
        ] + [HBM_SPEC] * nr,
        out_shape=[
            jax.ShapeDtypeStruct((seq, aw), BF16),
            jax.ShapeDtypeStruct((seq, kvw), BF16),
            jax.ShapeDtypeStruct(biasm.shape[1:], F32),
            jax.ShapeDtypeStruct((8, LANES), F32),
        ] + [jax.ShapeDtypeStruct(g.shape, g.dtype) for g in ready],
        scratch_shapes=[
            pltpu.VMEM((BLOCK, kvw), F32),
            pltpu.VMEM((max(held, 8), kvw), F32),
            pltpu.VMEM((Q_BLOCKS, N_KV_HEADS, pairs_per_kv * BAND, LANES), BF16),
            pltpu.VMEM((Q_BLOCKS, N_KV_HEADS, pairs_per_kv * BAND, LANES), BF16),
            pltpu.VMEM((Q_BLOCKS, N_KV_HEADS, BAND, pairs_per_kv * BAND), BF16),
            pltpu.VMEM((Q_BLOCKS, N_KV_HEADS, BAND, pairs_per_kv * BAND), BF16),
        ] + _exchange_sems(nr),
        compiler_params=_params(("arbitrary",), 48),
    )(sinks, q, kv, kv, dattn, biasm, *ready)
    return outs[:4], outs[4:]


def _relbias_grad(dssum2, bucket_row, chunk):
    heads, n = dssum2.shape

    def body(a_ref, bucket_ref, out_ref):
        @pl.when(pl.program_id(0) == 0)
        def _():
            out_ref[...] = jnp.zeros_like(out_ref)

        a = a_ref[...]
        hi = a.astype(BF16)
        lo = (a - hi.astype(F32)).astype(BF16)
        onehot_t = (lax.broadcasted_iota(jnp.int32, (LANES, chunk), 0) == bucket_ref[...]).astype(F32).astype(BF16)
        out_ref[...] += _dot_nt(hi, onehot_t) + _dot_nt(lo, onehot_t)

    return pl.pallas_call(
        body,
        name="relbias_grad",
        grid=(n // chunk,),
        in_specs=[pl.BlockSpec((heads, chunk), lambda i: (0, i)), pl.BlockSpec((1, chunk), lambda i: (0, i))],
        out_specs=_resident((heads, LANES)),
        out_shape=jax.ShapeDtypeStruct((heads, LANES), F32),
        compiler_params=_params(("arbitrary",), 32),
    )(dssum2, bucket_row)


def _layer_b_in_bwd(dh2, dq, dz2, dkv, h1, ya, wbin_g, wkv, kvn, bpre, sm, ready, ts):
    seq, d = h1.shape
    aw = dq.shape[1]
    kvw = dkv.shape[1]
    cw = wbin_g.shape[2]
    per = aw // cw

    nr = len(ready)
    nt = seq // ts

    def body(dh2_ref, dq_ref, dz2_ref, dkv_ref, h1_ref, ya_ref, wbin_ref, wkv_ref, kvn_ref, bpre_ref, sm_ref, *refs):
        ready_refs, (dh1_ref, dya_ref, acc_ref) = refs[:nr], refs[nr:nr + 3]
        landed_refs, sems = refs[nr + 3:2 * nr + 3], refs[2 * nr + 3:]

        @pl.when(pl.program_id(0) == 0)
        def _():
            acc_ref[...] = jnp.zeros_like(acc_ref)
            if nr:
                _exchange_start(ready_refs, landed_refs, *sems, True)

        if nr:
            @pl.when(pl.program_id(0) == nt - 1)
            def _():
                _exchange_wait(ready_refs, landed_refs, *sems, True)

        dn4 = jnp.zeros((ts, d), F32)
        for j in range(N_DEV):
            src = dq_ref if j < per else dz2_ref
            jj = j % per
            dn4 = dn4 + _dot_nt(src[:, jj * cw:(jj + 1) * cw], wbin_ref[j])
        dn3 = _dot_nt(dkv_ref[...], wkv_ref[...])
        hn, r = _rms(h1_ref[...])
        acc_ref[0:1, :] += jnp.sum(dn4 * hn, axis=0, keepdims=True)
        acc_ref[1:2, :] += jnp.sum(dn3 * hn, axis=0, keepdims=True)
        dh1 = dh2_ref[...] + _rms_bwd(dn4 * bpre_ref[...] + dn3 * kvn_ref[...], hn, r)
        dh1_ref[...] = dh1
        yan, r2 = _rms(ya_ref[...])
        acc_ref[2:3, :] += jnp.sum(dh1 * yan, axis=0, keepdims=True)
        dya_ref[...] = _rms_bwd(dh1 * sm_ref[4:5, :], yan, r2).astype(BF16)

    outs = pl.pallas_call(
        body,
        name="layer_b_in_bwd",
        grid=(nt,),
        in_specs=[_rows(ts, d), _rows(ts, aw), _rows(ts, aw), _rows(ts, kvw), _rows(ts, d), _rows(ts, d),
                  _full(wbin_g.shape), _full(wkv.shape), _full(kvn.shape), _full(bpre.shape), _full(sm.shape)]
        + [HBM_SPEC] * nr,
        out_specs=[_rows(ts, d), _rows(ts, d), _resident((8, d))] + [HBM_SPEC] * nr,
        out_shape=[jax.ShapeDtypeStruct((seq, d), F32), jax.ShapeDtypeStruct((seq, d), BF16),
                   jax.ShapeDtypeStruct((8, d), F32)] + [jax.ShapeDtypeStruct(g.shape, g.dtype) for g in ready],
        scratch_shapes=_exchange_sems(nr),
        compiler_params=_params(("arbitrary",), 48),
    )(dh2, dq, dz2, dkv, h1, ya, wbin_g, wkv, kvn, bpre, sm, *ready)
    return outs[:3], outs[3:]


def _layer_a_bwd(dya, proj, conv, dh1, x2, wout, win_g, sm, ts):
    seq, d = x2.shape
    width = wout.shape[0]
    half = win_g.shape[2]
    n_half = width // half
    nt = seq // ts

    def body(dya_ref, proj_ref, conv_ref, dh1_ref, x_ref, wout_ref, win_ref, sm_ref, dproj_ref, gx_ref, acc_ref,
             dnext_ref):
        @pl.when(pl.program_id(0) == 0)
        def _():
            acc_ref[...] = jnp.zeros_like(acc_ref)
            dnext_ref[...] = jnp.zeros_like(dnext_ref)

        dy = _dot_nt(dya_ref[...], wout_ref[...])
        row = lax.broadcasted_iota(jnp.int32, (ts, half), 0)
        dn1 = jnp.zeros((ts, d), F32)
        for hh in range(n_half):
            cols = slice(hh * half, (hh + 1) * half)
            b, c, u, z = [proj_ref[:, (part * n_half + hh) * half:(part * n_half + hh + 1) * half].astype(F32)
                          for part in range(4)]
            cv = conv_ref[:, cols].astype(F32)
            dyh = dy[:, cols]
            sz, dsz = _silu(z)
            dconv = dyh * b * sz
            grads = [dyh * cv * sz, None, None, dyh * b * cv * dsz]
            next0, next1 = dnext_ref[0:1, cols], dnext_ref[1:2, cols]
            dc1 = jnp.where(row == ts - 1, next0, pltpu.roll(dconv, ts - 1, 0))
            dc2 = jnp.where(row == ts - 1, next1, jnp.where(row == ts - 2, next0, pltpu.roll(dconv, ts - 2, 0)))
            dnext_ref[:, cols] = dconv[0:8, :]
            v = c * u
            acc_ref[1:2, cols] += jnp.sum(dc2 * v, axis=0, keepdims=True)
            acc_ref[2:3, cols] += jnp.sum(dc1 * v, axis=0, keepdims=True)
            acc_ref[3:4, cols] += jnp.sum(dconv * v, axis=0, keepdims=True)
            dv = sm_ref[3:4, cols] * dconv + sm_ref[2:3, cols] * dc1 + sm_ref[1:2, cols] * dc2
            grads[1] = dv * u
            grads[2] = dv * c
            for part in range(4):
                j = part * n_half + hh
                gj = grads[part].astype(BF16)
                dproj_ref[:, j * half:(j + 1) * half] = gj
                dn1 = dn1 + _dot_nt(gj, win_ref[j])
        xn, r = _rms(x_ref[...])
        acc_ref[0:1, :] += jnp.sum(dn1 * xn, axis=0, keepdims=True)
        gx_ref[...] = dh1_ref[...] + _rms_bwd(dn1 * sm_ref[0:1, :], xn, r)

    rev = lambda w: pl.BlockSpec((ts, w), lambda i: (nt - 1 - i, 0))
    return pl.pallas_call(
        body,
        name="layer_a_bwd",
        grid=(nt,),
        in_specs=[rev(d), rev(4 * width), rev(width), rev(d), rev(d), _full(wout.shape), _full(win_g.shape), _full(sm.shape)],
        out_specs=[rev(4 * width), rev(d), _resident((8, d))],
        out_shape=[jax.ShapeDtypeStruct((seq, 4 * width), BF16), jax.ShapeDtypeStruct((seq, d), F32),
                   jax.ShapeDtypeStruct((8, d), F32)],
        scratch_shapes=[pltpu.VMEM((8, width), F32)],
        compiler_params=_params(("arbitrary",), 56),
    )(dya, proj, conv, dh1, x2, wout, win_g, sm)


def _wgrad(a, bs, n_slots, ts, name, ready=(), block_cols=1024):
    nr = len(ready)
    seq, k = a.shape
    nb_in = len(bs)
    n_each = bs[0].shape[1]
    n = nb_in * n_each
    bn = min(n_each, block_cols)
    per_in = n_each // bn
    n_blocks = nb_in * per_in
    ns = seq // ts

    def b_spec(idx):
        def index(j, s):
            mine = j // per_in == idx
            row = jnp.where(mine, s, jnp.where(j // per_in > idx, ns - 1, 0))
            return (row, jnp.where(mine, j % per_in, jnp.where(j // per_in > idx, per_in - 1, 0)))
        return pl.BlockSpec((ts, bn), index)

    if n_slots:
        sw = n // n_slots
        spb = bn // sw
        out_shape = jax.ShapeDtypeStruct((n_slots, k, sw), BF16)
        out_spec = pl.BlockSpec((spb, k, sw), lambda j, s: (j, 0, 0))
    else:
        out_shape = jax.ShapeDtypeStruct((k, n), BF16)
        out_spec = pl.BlockSpec((k, bn), lambda j, s: (0, j))

    def body(a_ref, *refs):
        b_refs, ready_refs, o_ref = refs[:nb_in], refs[nb_in:nb_in + nr], refs[nb_in + nr]
        landed_refs, (acc_ref, *sems) = refs[nb_in + nr + 1:nb_in + 2 * nr + 1], refs[nb_in + 2 * nr + 1:]
        j, s = pl.program_id(0), pl.program_id(1)

        if nr:
            @pl.when(jnp.logical_and(j == 0, s == 0))
            def _():
                _exchange_start(ready_refs, landed_refs, *sems, True)

            @pl.when(jnp.logical_and(j == n_blocks - 1, s == ns - 1))
            def _():
                _exchange_wait(ready_refs, landed_refs, *sems, True)

        @pl.when(s == 0)
        def _():
            acc_ref[...] = jnp.zeros_like(acc_ref)

        for idx in range(nb_in):
            @pl.when(j // per_in == idx)
            def _(idx=idx):
                acc_ref[...] += _dot_tn(a_ref[...], b_refs[idx][...])

        @pl.when(s == ns - 1)
        def _():
            if n_slots:
                for e in range(spb):
                    o_ref[e] = acc_ref[:, e * sw:(e + 1) * sw].astype(BF16)
            else:
                o_ref[...] = acc_ref[...].astype(BF16)

    outs = pl.pallas_call(
        body,
        name=name,
        grid=(n_blocks, ns),
        in_specs=[pl.BlockSpec((ts, k), lambda j, s: (s, 0))] + [b_spec(idx) for idx in range(nb_in)] + [HBM_SPEC] * nr,
        out_specs=[out_spec] + [HBM_SPEC] * nr,
        out_shape=[out_shape] + [jax.ShapeDtypeStruct(g.shape, g.dtype) for g in ready],
        scratch_shapes=[pltpu.VMEM((k, bn), F32)] + (_exchange_sems(nr) if nr else []),
        compiler_params=_params(("arbitrary", "arbitrary"), 48),
    )(a, *bs, *ready)
    return (outs[0], outs[1:]) if nr else outs[0]


def _wgrad_tail(pairs, part, landed, ts):
    n_tasks = len(pairs)
    assert n_tasks == 2
    nl = len(landed)
    seq, k = pairs[0][0].shape
    n = pairs[0][1].shape[1]
    ns = seq // ts
    total = n_tasks * ns
    per = k // N_DEV
    n_red = len(_chip_reduce_scratch((per, n)))

    def spec(t, width):
        return pl.BlockSpec((ts, width), lambda j, s: (jnp.where(j == t, s, jnp.where(j > t, ns - 1, 0)), 0))

    def body(*refs):
        ab_refs, part_hbm = refs[:2 * n_tasks], refs[2 * n_tasks]
        landed_hbm, refs = refs[2 * n_tasks + 1:2 * n_tasks + 1 + nl], refs[2 * n_tasks + 1 + nl:]
        o_ref, red_ref, early_ref = refs[:3]
        summed_refs, (acc_ref, first_ref, part_ref, *scratch) = refs[3:3 + nl], refs[3 + nl:]
        landed_refs, load_sems, scratch = scratch[:nl], scratch[nl], scratch[nl + 1:]
        j, s = pl.program_id(0), pl.program_id(1)
        flat = j * ns + s
        swap, send, forward, finish = _chip_reduce(part_ref, red_ref, *scratch[:3], scratch[3:n_red], part_hbm)
        swap_first, send_first, forward_first, finish_first = _chip_reduce(
            first_ref, early_ref, *scratch[n_red:n_red + 3], scratch[n_red + 3:])
        loads = [pltpu.make_async_copy(src, dst, load_sems.at[i])
                 for i, (src, dst) in enumerate(zip([part_hbm, *landed_hbm], [part_ref, *landed_refs]))]

        @pl.when(flat == 0)
        def _():
            swap()
            for load in loads:
                load.start()

        @pl.when(flat == min(1, total - 1))
        def _():
            loads[0].wait()
            send()

        @pl.when(flat == min(total // 2 + 1, total - 1))
        def _():
            forward()
            for t in range(nl):
                loads[1 + t].wait()
                _sum_slots(landed_refs[t], summed_refs[t])

        @pl.when(flat == ns)
        def _():
            send_first()

        @pl.when(flat == min(ns + ns // 2, total - 1))
        def _():
            forward_first()

        @pl.when(s == 0)
        def _():
            acc_ref[...] = jnp.zeros_like(acc_ref)

        for t in range(n_tasks):
            @pl.when(j == t)
            def _(t=t):
                acc_ref[...] += _dot_tn(ab_refs[2 * t][...], ab_refs[2 * t + 1][...])

        @pl.when(flat == ns - 1)
        def _():
            for dev in range(N_DEV):
                first_ref[dev] = acc_ref[dev * per:(dev + 1) * per, :].astype(BF16)
            swap_first()

        @pl.when(flat == total - 1)
        def _():
            for dev in range(N_DEV):
                o_ref[dev] = acc_ref[dev * per:(dev + 1) * per, :].astype(BF16)
            finish()
            finish_first()

    slot = part.shape[1:]
    outs = pl.pallas_call(
        body,
        name="wgrad_tail",
        grid=(n_tasks, ns),
        in_specs=[spec(t, w) for t in range(n_tasks) for w in (k, n)] + [HBM_SPEC] * (1 + nl),
        out_specs=[_resident((N_DEV, per, n)), _resident(slot), _resident((per, n))]
        + [_resident(g.shape[1:]) for g in landed],
        out_shape=[jax.ShapeDtypeStruct((N_DEV, per, n), BF16), jax.ShapeDtypeStruct(slot, F32),
                   jax.ShapeDtypeStruct((per, n), F32)]
        + [jax.ShapeDtypeStruct(g.shape[1:], F32) for g in landed],
        scratch_shapes=[pltpu.VMEM((k, n), F32), pltpu.VMEM((N_DEV, per, n), BF16), pltpu.VMEM(part.shape, part.dtype)]
        + [pltpu.VMEM(g.shape, g.dtype) for g in landed] + [pltpu.SemaphoreType.DMA((1 + nl,))]
        + _chip_reduce_scratch(slot) + _chip_reduce_scratch((per, n)),
        compiler_params=_params(("arbitrary", "arbitrary")),
    )(*[op for pair in pairs for op in pair], part, *landed)
    return outs[0], outs[1], outs[2], outs[3:]


MINE = "mine"
ADAMW_STEPS = 4


def _adamw(ws, sources, picks, loss_at, ms, vs):
    n, n_src = len(ws), len(sources)
    streamed = [len(w.shape) == 2 and w.shape[0] >= 128 and picks[t][1:] == (0, None)
                and sources[picks[t][0]].shape == w.shape for t, w in enumerate(ws)]
    streamed_sources = {picks[t][0] for t in range(n) if streamed[t]}

    def step(w, g, m, v):
        m = ADAM_B1 * m + (1.0 - ADAM_B1) * g
        v = ADAM_B2 * v + (1.0 - ADAM_B2) * jnp.square(g)
        m_hat = m / (1.0 - ADAM_B1 ** ADAM_STEP)
        v_hat = v / (1.0 - ADAM_B2 ** ADAM_STEP)
        return g, -ADAM_LR * (m_hat / (jnp.sqrt(v_hat) + ADAM_EPS) + ADAM_WD * w), m, v

    def body(*refs):
        refs = list(refs)
        take = lambda k: [refs.pop(0) for _ in range(k)]
        w_refs, s_refs, m_refs, v_refs = take(n), take(n_src), take(n), take(n)
        (loss_ref,), go_refs, d_refs, nm_refs, nv_refs = take(1), take(n), take(n), take(n), take(n)
        me = _my_index()

        def grad(t, rows):
            k, first, cols = picks[t]
            if cols is None:
                return s_refs[k][rows, :]
            if cols is not MINE:
                return s_refs[k][rows, cols]
            width = w_refs[t].shape[-1]
            g = s_refs[k][rows, 0:width]
            for dev in range(1, N_DEV):
                g = jnp.where(me == dev, s_refs[k][rows, dev * width:(dev + 1) * width], g)
            return g

        def whole(t):
            first = picks[t][1]
            rows = w_refs[t].shape[0]
            if len(w_refs[t].shape) == 3:
                for j in range(rows):
                    go_refs[t][j], d_refs[t][j], nm_refs[t][j], nv_refs[t][j] = step(
                        w_refs[t][j], grad(t, slice(first + j, first + j + 1)), m_refs[t][j], v_refs[t][j])
                return
            go_refs[t][...], d_refs[t][...], nm_refs[t][...], nv_refs[t][...] = step(
                w_refs[t][...], grad(t, slice(first, first + rows)), m_refs[t][...], v_refs[t][...])

        def block(t):
            rows = w_refs[t].shape[0]
            chunk = min(rows, 128)

            def one(i, carry):
                r = pl.ds(pl.multiple_of(i * chunk, chunk), chunk)
                go_refs[t][r, :], d_refs[t][r, :], nm_refs[t][r, :], nv_refs[t][r, :] = step(
                    w_refs[t][r, :], grad(t, r), m_refs[t][r, :], v_refs[t][r, :])
                return carry

            lax.fori_loop(0, rows // chunk, one, 0)

        @pl.when(pl.program_id(0) == 0)
        def _():
            loss_ref[...] = s_refs[loss_at[0]][loss_at[1]:loss_at[1] + 1, 0:1]
            for t in range(n):
                if not streamed[t]:
                    whole(t)

        for t in range(n):
            if streamed[t]:
                block(t)

    def rows_of(shape):
        return pl.BlockSpec((shape[0] // ADAMW_STEPS, shape[1]), lambda i: (i, 0))

    w_in = [rows_of(w.shape) if streamed[t] else _full(w.shape) for t, w in enumerate(ws)]
    w_out = [rows_of(w.shape) if streamed[t] else _resident(w.shape) for t, w in enumerate(ws)]
    s_in = [rows_of(s.shape) if k in streamed_sources else _full(s.shape) for k, s in enumerate(sources)]
    outs = pl.pallas_call(
        body,
        name="adamw",
        grid=(ADAMW_STEPS,),
        in_specs=w_in + s_in + w_in * 2,
        out_specs=[_resident((1, 1))] + w_out * 4,
        out_shape=[jax.ShapeDtypeStruct((1, 1), F32)] + [jax.ShapeDtypeStruct(w.shape, F32) for w in ws] * 4,
        compiler_params=_params(("arbitrary",)),
    )(*ws, *sources, *ms, *vs)
    return outs[0], outs[1:n + 1], outs[n + 1:2 * n + 1], outs[2 * n + 1:3 * n + 1], outs[3 * n + 1:]


def _band_structure():
    q_loc = np.arange(BLOCK, dtype=np.int32)[:, None]
    s_loc = np.arange(2 * BLOCK, dtype=np.int32)[None, :]
    dist = q_loc + BLOCK - s_loc
    in_window = (dist >= 0) & (dist < BLOCK)
    dd = np.maximum(dist, 0)
    max_exact = N_BUCKETS // 2
    large = max_exact + (np.log(np.maximum(dd, 1) / max_exact) / math.log(MAX_DISTANCE / max_exact)
                         * (N_BUCKETS - max_exact)).astype(np.int32)
    bucket = np.where(dd < max_exact, dd, np.minimum(large, N_BUCKETS - 1)).astype(np.int32)
    return bucket, in_window.astype(np.int32)


def kernel(x, a_pre_norm, a_w_in, a_conv_w, a_w_out, a_post_norm, kv_norm, w_kv, rel_bias, b_pre_norm, b_w_in, b_sinks, b_w_out, b_post_norm, loss_target, m_a_pre_norm, m_a_w_in, m_a_conv_w, m_a_w_out, m_a_post_norm, m_kv_norm, m_w_kv, m_rel_bias, m_b_pre_norm, m_b_w_in, m_b_sinks, m_b_w_out, m_b_post_norm, v_a_pre_norm, v_a_w_in, v_a_conv_w, v_a_w_out, v_a_post_norm, v_kv_norm, v_w_kv, v_rel_bias, v_b_pre_norm, v_b_w_in, v_b_sinks, v_b_w_out, v_b_post_norm):
    seq, d = x.shape[1], x.shape[2]
    x2 = x.reshape(seq, d)
    target = loss_target.reshape(seq, d)
    shard = a_pre_norm.shape[1]
    ts_a = min(seq, 512)
    ts = min(seq, 512)
    ts_w = min(seq, 2048)

    taps = lambda a: a.transpose(1, 0, 2)
    bucket, in_window = _band_structure()
    (win_g, wout_g), small_g, later, biasm = _all_gather(
        [a_w_in[0], a_w_out[0]], [(0, a_pre_norm), (1, taps(a_conv_w)), (4, a_post_norm)],
        [w_kv, b_w_in[0], b_w_out[0]], rel_bias.T, bucket.T, in_window.T)
    wout = wout_g.reshape(-1, wout_g.shape[2])
    sm = small_g.transpose(1, 0, 2).reshape(8, N_DEV * shard)
    kvn = kv_norm.reshape(1, d)

    (h1, n1, proj, conv, y, ya), (wkv_g, wbin_g, wbout_g) = _layer_a_fwd(x2, sm, win_g, wout, later, ts_a)
    wkv = wkv_g.reshape(-1, wkv_g.shape[2])
    wbout = wbout_g.reshape(-1, wbout_g.shape[2])
    n3, n4, kv, q, o, dh2, dyb, dattn, dz2, acc_c = _layer_b_fwd(
        h1, target, kvn, b_pre_norm, wkv, wbin_g, biasm, b_sinks, wbout, b_post_norm)

    (dq, dkv, dssum, dsink), _ = _attn_bwd(q, kv, dattn, biasm, b_sinks, [])
    by_head = dssum.reshape(N_PAIRS, BAND, 2, BLOCK).transpose(0, 2, 3, 1)
    relb = _relbias_grad(by_head.reshape(N_Q_HEADS, -1), bucket.reshape(1, -1), 4096)
    g_wkv = _wgrad(n3, [dkv], 0, ts_w, "wgrad_kv").reshape(wkv_g.shape)
    g_wbin = _wgrad(n4, [dq, dz2], N_DEV, ts_w, "wgrad_b_in")
    (dh1, dya, acc_b), _ = _layer_b_in_bwd(dh2, dq, dz2, dkv, h1, ya, wbin_g, wkv, kvn, b_pre_norm, sm, [], ts)
    dproj, gx, acc_a = _layer_a_bwd(dya, proj, conv, dh1, x2, wout, win_g, sm, ts_a)
    g_win, (l_wkv, l_wbin) = _wgrad(
        n1, [dproj], N_DEV, ts_w, "wgrad_a_in", ready=[g_wkv, g_wbin], block_cols=2048)
    g_wbout, r_win, r_wout, (r_wkv, r_wbin) = _wgrad_tail(
        [(y, dya), (o, dyb)], g_win, [l_wkv, l_wbin], min(seq, 1024))

    r_wbout, _, (s_a, s_b, s_c, s_relb, s_sink) = _reduce_exchange(g_wbout, [], [acc_a, acc_b, acc_c, relb, dsink])
    weights = [a_pre_norm, a_w_in[0], taps(a_conv_w), a_w_out[0], a_post_norm, kvn, w_kv, rel_bias.T, b_pre_norm,
               b_w_in[0], b_sinks, b_w_out[0], b_post_norm]
    sources = [s_a, s_b, s_c, s_relb, s_sink, r_win, r_wout, r_wkv, r_wbin, r_wbout]
    picks = [(0, 0, MINE), (5, 0, None), (0, 1, MINE), (6, 0, None), (1, 2, MINE), (1, 1, None), (7, 0, None),
             (3, 0, slice(0, N_BUCKETS)), (1, 0, None), (8, 0, None), (4, 0, slice(0, N_Q_HEADS)),
             (9, 0, None), (2, 0, None)]
    first = [m_a_pre_norm, m_a_w_in[0], taps(m_a_conv_w), m_a_w_out[0], m_a_post_norm, m_kv_norm.reshape(1, d),
             m_w_kv, m_rel_bias.T, m_b_pre_norm, m_b_w_in[0], m_b_sinks, m_b_w_out[0], m_b_post_norm]
    second = [v_a_pre_norm, v_a_w_in[0], taps(v_a_conv_w), v_a_w_out[0], v_a_post_norm, v_kv_norm.reshape(1, d),
              v_w_kv, v_rel_bias.T, v_b_pre_norm, v_b_w_in[0], v_b_sinks, v_b_w_out[0], v_b_post_norm]
    loss, grads, deltas, new_m, new_v = _adamw(weights, sources, picks, (2, 1), first, second)

    shapes = [a_pre_norm.shape, a_w_in.shape, taps, a_w_out.shape, a_post_norm.shape, kv_norm.shape,
              w_kv.shape, jnp.transpose, b_pre_norm.shape, b_w_in.shape, b_sinks.shape, b_w_out.shape, b_post_norm.shape]
    shaped = lambda arrays: [s(a) if callable(s) else a.reshape(s) for a, s in zip(arrays, shapes)]
    return (loss.reshape(()), gx.reshape(x.shape), *shaped(grads), *shaped(deltas), *shaped(new_m), *shaped(new_v))
```

```python
import math

import jax
import jax.numpy as jnp
import numpy as np
from jax import lax
from jax.experimental import pallas as pl
from jax.experimental.pallas import tpu as pltpu

HEAD_DIM = 64
N_Q_HEADS = 16
N_KV_HEADS = 2
GROUP = N_Q_HEADS // N_KV_HEADS
BLOCK = 128
N_BUCKETS = 32
MAX_DISTANCE = 128
EPS = 1e-6
NEG_INF = -1e30
SCALE = HEAD_DIM ** -0.5

ADAM_LR = 0.001
ADAM_B1 = 0.9
ADAM_B2 = 0.999
ADAM_EPS = 1e-08
ADAM_WD = 0.01
ADAM_STEP = 10

N_PAIRS = N_Q_HEADS // 2
BAND = 2 * BLOCK

N_DEV = 8
GATHER_PIECE_ROWS = 256
LANES = 128
F32 = jnp.float32
BF16 = jnp.bfloat16
MESH = pl.DeviceIdType.MESH
MIB = 1024 * 1024
VMEM_RESERVED_MIB = 63


def _params(semantics=None, vmem_mib=48):
    del vmem_mib
    return pltpu.CompilerParams(dimension_semantics=semantics, vmem_limit_bytes=VMEM_RESERVED_MIB * MIB)


def _full(shape):
    zeros = (0,) * len(shape)
    return pl.BlockSpec(shape, lambda *_: zeros, pipeline_mode=pl.Buffered(1))


def _resident(shape):
    zeros = (0,) * len(shape)
    return pl.BlockSpec(shape, lambda *_: zeros)


def _rows(ts, cols):
    return pl.BlockSpec((ts, cols), lambda i: (i, 0))


def _dot(a, b):
    return jnp.dot(a, b, preferred_element_type=F32)


def _dot_nt(a, b):
    return lax.dot_general(a, b, (((1,), (1,)), ((), ())), preferred_element_type=F32)


def _dot_tn(a, b):
    return lax.dot_general(a, b, (((0,), (0,)), ((), ())), preferred_element_type=F32)


def _rms(xf):
    r = lax.rsqrt(jnp.mean(xf * xf, axis=-1, keepdims=True) + EPS)
    return xf * r, r


def _rms_bwd(dn, xn, r):
    return r * (dn - xn * jnp.mean(dn * xn, axis=-1, keepdims=True))


def _silu(z):
    s = jax.nn.sigmoid(z)
    return z * s, s * (1.0 + z * (1.0 - s))


def _my_index():
    return 4 * lax.axis_index("x") + 2 * lax.axis_index("y") + lax.axis_index("c")


def _bias_table(rb_ref, bucket_ref, win_ref, out_ref):
    bk = jnp.where(win_ref[...] != 0, bucket_ref[...], -1)
    has_prev = lax.broadcasted_iota(jnp.int32, bk.shape, 0) >= BLOCK
    for h in range(N_Q_HEADS):
        acc = jnp.full(bk.shape, NEG_INF, F32)
        for b in range(N_BUCKETS):
            acc = jnp.where(bk == b, rb_ref[h, b], acc)
        cols = slice((h % 2) * BLOCK, (h % 2 + 1) * BLOCK)
        out_ref[1, h // 2, :, cols] = acc
        out_ref[0, h // 2, :, cols] = jnp.where(has_prev, acc, NEG_INF)


def _all_gather(shards, small_rows, casts, rel_bias_t, bucket_t, in_window_t):
    ns, nc, n = len(small_rows), len(casts), len(shards) + 1
    small_shape = (8, small_rows[0][1].shape[-1])
    shapes = [s.shape for s in shards] + [small_shape]
    pieces = [(t, r0, min(GATHER_PIECE_ROWS, shape[0] - r0))
              for t, shape in enumerate(shapes) for r0 in range(0, shape[0], GATHER_PIECE_ROWS)]

    def body(*refs):
        refs = list(refs)
        take = lambda k: [refs.pop(0) for _ in range(k)]
        ins, small_refs, cast_refs, (rb_ref, bucket_ref, win_ref) = take(n - 1), take(ns), take(nc), take(3)
        outs, cast_outs, (bias_ref,), own, (send_sems, recv_sems, own_sems) = take(n), take(nc), take(1), take(n), take(3)
        x, y, c = lax.axis_index("x"), lax.axis_index("y"), lax.axis_index("c")
        me, sibling = (x, y, c), (x, y, 1 - c)
        x_nbr, y_nbr, diagonal = (1 - x, y), (x, 1 - y), (1 - x, 1 - y)
        south = c == 0
        relayed = (jnp.where(south, 1 - x, x), jnp.where(south, y, 1 - y))
        relay_to = (jnp.where(south, x, 1 - x), jnp.where(south, 1 - y, y))

        def copy(u, k, block, to):
            t, r0, nrows = pieces[u]
            rows = outs[t].at[4 * block[0] + 2 * block[1] + block[2], pl.ds(r0, nrows)]
            return pltpu.make_async_remote_copy(
                src_ref=rows, dst_ref=rows, send_sem=send_sems.at[u, k], recv_sem=recv_sems.at[u, k],
                device_id=to, device_id_type=MESH)

        for t in range(n - 1):
            own[t][...] = ins[t][...].astype(BF16)
        own[n - 1][...] = jnp.zeros(small_shape, F32)
        for (row, _), ref in zip(small_rows, small_refs):
            if len(ref.shape) == 3:
                for j in range(ref.shape[0]):
                    own[n - 1][row + j:row + j + 1, :] = ref[j]
            else:
                own[n - 1][row:row + ref.shape[0], :] = ref[...]
        place = [pltpu.make_async_copy(own[t], outs[t].at[_my_index()], own_sems.at[t]) for t in range(n)]
        for cp in place:
            cp.start()
        for cp in place:
            cp.wait()
        started = []

        def start(cp):
            cp.start()
            started.append(cp)

        units = range(len(pieces))
        for u in units:
            start(copy(u, 0, me, sibling))
            start(copy(u, 1, me, (*x_nbr, c)))
            start(copy(u, 2, me, (*y_nbr, c)))
        for src, dst in zip(cast_refs, cast_outs):
            dst[...] = src[...].astype(BF16)
        _bias_table(rb_ref, bucket_ref, win_ref, bias_ref)
        for u in units:
            for k, chip in ((1, x_nbr), (2, y_nbr)):
                copy(u, k, (*chip, c), me).wait_recv()
                start(copy(u, 3 + k, (*chip, c), sibling))
            start(copy(u, 3, (*relayed, c), (*relay_to, c)))
        for u in units:
            copy(u, 3, (*diagonal, c), me).wait_recv()
            start(copy(u, 6, (*diagonal, c), sibling))
        for u in units:
            copy(u, 0, sibling, me).wait_recv()
        for k, chip in ((4, x_nbr), (5, y_nbr), (6, diagonal)):
            for u in units:
                copy(u, k, (*chip, 1 - c), me).wait_recv()
        for cp in started:
            cp.wait_send()

    vmem = pl.BlockSpec(memory_space=pltpu.VMEM)
    outs = pl.pallas_call(
        body,
        name="gather_weights",
        out_shape=[jax.ShapeDtypeStruct((N_DEV,) + s.shape, BF16) for s in shards]
        + [jax.ShapeDtypeStruct((N_DEV,) + small_shape, F32)]
        + [jax.ShapeDtypeStruct(a.shape, BF16) for a in casts]
        + [jax.ShapeDtypeStruct((2, N_PAIRS, BAND, 2 * BLOCK), F32)],
        in_specs=[vmem] * (n - 1 + ns + nc) + [pl.BlockSpec(memory_space=pltpu.SMEM), vmem, vmem],
        out_specs=[HBM_SPEC] * n + [vmem] * (nc + 1),
        scratch_shapes=[pltpu.VMEM(s.shape, BF16) for s in shards] + [pltpu.VMEM(small_shape, F32)]
        + [pltpu.SemaphoreType.DMA((len(pieces), 7)), pltpu.SemaphoreType.DMA((len(pieces), 7)),
           pltpu.SemaphoreType.DMA((n,))],
        compiler_params=_params(),
    )(*shards, *[a for _, a in small_rows], *casts, rel_bias_t, bucket_t, in_window_t)
    return outs[:n - 1], outs[n - 1], outs[n:n + nc], outs[n + nc]


def _peer(k):
    x, y, c = lax.axis_index("x"), lax.axis_index("y"), lax.axis_index("c")
    px = 1 - x if k & 4 else x
    py = 1 - y if k & 2 else y
    pc = 1 - c if k & 1 else c
    return (px, py, pc), 4 * px + 2 * py + pc


def _exchange(srcs, dsts, send_sems, recv_sems, local_sems, scatter):
    me = _my_index()
    sends, arrivals = [], []
    for k in range(1, N_DEV):
        peer, pidx = _peer(k)
        for t, (src, dst) in enumerate(zip(srcs, dsts)):
            mine = src.at[pidx] if scatter else src
            sems = dict(send_sem=send_sems.at[t, k - 1], recv_sem=recv_sems.at[t, k - 1], device_id=peer, device_id_type=MESH)
            sends.append(pltpu.make_async_remote_copy(src_ref=mine, dst_ref=dst.at[me], **sems))
            arrivals.append(pltpu.make_async_remote_copy(src_ref=mine, dst_ref=dst.at[pidx], **sems))
    local = [pltpu.make_async_copy(src.at[me] if scatter else src, dst.at[me], local_sems.at[t])
             for t, (src, dst) in enumerate(zip(srcs, dsts))]
    return sends, arrivals, local


def _exchange_start(*args):
    sends, _, local = _exchange(*args)
    for cp in sends + local:
        cp.start()


def _exchange_wait(*args):
    sends, arrivals, local = _exchange(*args)
    for cp in arrivals:
        cp.wait_recv()
    for cp in sends:
        cp.wait_send()
    for cp in local:
        cp.wait()


def _exchange_sems(n):
    if not n:
        return []
    return [pltpu.SemaphoreType.DMA((n, N_DEV - 1)), pltpu.SemaphoreType.DMA((n, N_DEV - 1)), pltpu.SemaphoreType.DMA((n,))]


HBM_SPEC = pl.BlockSpec(memory_space=pl.ANY)


def _sum_slots(recv_ref, out_ref):
    rows = out_ref.shape[0]
    chunk = min(rows, 128)

    def add(i, carry):
        r0 = pl.multiple_of(i * chunk, chunk)
        acc = recv_ref[0, pl.ds(r0, chunk), :].astype(F32)
        for dev in range(1, N_DEV):
            acc = acc + recv_ref[dev, pl.ds(r0, chunk), :].astype(F32)
        out_ref[pl.ds(r0, chunk), :] = acc
        return carry

    lax.fori_loop(0, rows // chunk, add, 0)


N_CHIPS = N_DEV // 2


def _rows_loop(rows, fn):
    chunk = min(rows, 128)

    def step(i, carry):
        fn(pl.ds(pl.multiple_of(i * chunk, chunk), chunk))
        return carry

    lax.fori_loop(0, rows // chunk, step, 0)


def _chip_reduce(g_ref, out_ref, sib_ref, land_ref, send_ref, sems, swap_src=None):
    sib_send, sib_recv, ici_send, ici_recv = sems
    x, y, c = lax.axis_index("x"), lax.axis_index("y"), lax.axis_index("c")
    south = c == 0
    near =(jnp.where(south, 1 - x, x), jnp.where(south, y, 1 - y))
    far = (jnp.where(south, x, 1 - x), jnp.where(south, 1 - y, y))
    diagonal = (1 - x, 1 - y)
    rows = out_ref.shape[0]
    direct, fold, folded = 0, 1, 2

    def to_sibling(t):
        src = g_ref if swap_src is None else swap_src
        return pltpu.make_async_remote_copy(
            src_ref=src.at[2 * t + 1 - c], dst_ref=sib_ref.at[t], send_sem=sib_send.at[t], recv_sem=sib_recv.at[t],
            device_id=(x, y, 1 - c), device_id_type=MESH)

    def ici(role, chip):
        return pltpu.make_async_remote_copy(
            src_ref=send_ref.at[role], dst_ref=land_ref.at[role], send_sem=ici_send.at[role],
            recv_sem=ici_recv.at[role], device_id=(*chip, c), device_id_type=MESH)

    def pair_sum(chip, r):
        t = 2 * chip[0] + chip[1]
        return g_ref[2 * t + c, r, :].astype(F32) + sib_ref[t, r, :].astype(F32)

    def swap():
        for t in range(N_CHIPS):
            to_sibling(t).start()

    def send():
        for t in range(N_CHIPS):
            to_sibling(t).wait_recv()
        for role, chip in ((fold, diagonal), (direct, near)):
            def fill(r, role=role, chip=chip):
                send_ref[role, r, :] = pair_sum(chip, r).astype(BF16)

            _rows_loop(rows, fill)
            ici(role, near).start()

    def forward():
        ici(fold, near).wait_recv()

        def fill(r):
            send_ref[folded, r, :] = (pair_sum(far, r) + land_ref[fold, r, :].astype(F32)).astype(BF16)

        _rows_loop(rows, fill)
        ici(folded, far).start()

    def finish():
        ici(direct, near).wait_recv()
        ici(folded, far).wait_recv()

        def total(r):
            mine = pair_sum((x, y), r)
            out_ref[r, :] = mine + land_ref[direct, r, :].astype(F32) + land_ref[folded, r, :].astype(F32)

        _rows_loop(rows, total)
        for t in range(N_CHIPS):
            to_sibling(t).wait_send()
        for role, chip in ((direct, near), (fold, near), (folded, far)):
            ici(role, chip).wait_send()

    return swap, send, forward, finish


def _chip_reduce_scratch(slot):
    return [pltpu.VMEM((N_CHIPS,) + slot, BF16), pltpu.VMEM((3,) + slot, BF16), pltpu.VMEM((3,) + slot, BF16),
            pltpu.SemaphoreType.DMA((N_CHIPS,)), pltpu.SemaphoreType.DMA((N_CHIPS,)),
            pltpu.SemaphoreType.DMA((3,)), pltpu.SemaphoreType.DMA((3,))]


def _reduce_exchange(part, landed, smalls):
    nl, ng = len(landed), len(smalls)
    n_out = 1 + nl + ng

    def body(*refs):
        p_in, l_in, s_in = refs[0], refs[1:1 + nl], refs[1 + nl:n_out]
        p_out, l_out, s_out = refs[n_out], refs[n_out + 1:n_out + 1 + nl], refs[n_out + 1 + nl:2 * n_out]
        scratch = refs[2 * n_out:]
        s_recv, (sib_ref, chip_ref, send_ref), sems = scratch[:ng], scratch[ng:ng + 3], scratch[ng + 3:]
        swap, send, forward, finish = _chip_reduce(p_in, p_out, sib_ref, chip_ref, send_ref, sems[:4])
        swap()
        _exchange_start(s_in, s_recv, *sems[4:], False)
        send()
        for t in range(nl):
            _sum_slots(l_in[t], l_out[t])
        forward()
        finish()
        _exchange_wait(s_in, s_recv, *sems[4:], False)
        for t in range(ng):
            acc = s_recv[t][0]
            for dev in range(1, N_DEV):
                acc = acc + s_recv[t][dev]
            s_out[t][...] = acc

    vmem = pl.BlockSpec(memory_space=pltpu.VMEM)
    slot = part.shape[1:]
    outs = pl.pallas_call(
        body,
        name="reduce_grads",
        out_shape=[jax.ShapeDtypeStruct(p.shape[1:], F32) for p in [part] + landed]
        + [jax.ShapeDtypeStruct(s.shape, F32) for s in smalls],
        in_specs=[vmem] * n_out,
        out_specs=[vmem] * n_out,
        scratch_shapes=[pltpu.VMEM((N_DEV,) + s.shape, F32) for s in smalls] + _chip_reduce_scratch(slot)
        + _exchange_sems(ng),
        compiler_params=_params(vmem_mib=56),
    )(part, *landed, *smalls)
    return outs[0], outs[1:1 + nl], outs[1 + nl:]


def _layer_a_fwd(x2, sm, win_g, wout, later, ts):
    seq, d = x2.shape
    width = wout.shape[0]
    half = win_g.shape[2]
    n_half = width // half
    nl = len(later)
    nt = seq // ts

    def body(x_ref, sm_ref, win_ref, wout_ref, *refs):
        shard_refs, refs = refs[:nl], refs[nl:]
        h1_ref, n1_ref, proj_ref, conv_ref, y_ref, ya_ref = refs[:6]
        gathered_refs, (vprev_ref, *sems) = refs[6:6 + nl], refs[6 + nl:]

        @pl.when(pl.program_id(0) == 0)
        def _():
            vprev_ref[...] = jnp.zeros_like(vprev_ref)
            _exchange_start(shard_refs, gathered_refs, *sems, False)

        @pl.when(pl.program_id(0) == nt - 1)
        def _():
            _exchange_wait(shard_refs, gathered_refs, *sems, False)

        xf = x_ref[...]
        xn, _ = _rms(xf)
        n1 = (xn * sm_ref[0:1, :]).astype(BF16)
        n1_ref[...] = n1
        row = lax.broadcasted_iota(jnp.int32, (ts, half), 0)
        ya = jnp.zeros((ts, d), F32)
        for hh in range(n_half):
            cols = slice(hh * half, (hh + 1) * half)
            parts = []
            for part in range(4):
                j = part * n_half + hh
                pj = _dot(n1, win_ref[j])
                proj_ref[:, j * half:(j + 1) * half] = pj.astype(BF16)
                parts.append(pj)
            b, c, u, z = parts
            v = c * u
            last1, last2 = vprev_ref[7:8, cols], vprev_ref[6:7, cols]
            v1 = jnp.where(row == 0, last1, pltpu.roll(v, 1, 0))
            v2 = jnp.where(row == 0, last2, jnp.where(row == 1, last1, pltpu.roll(v, 2, 0)))
            vprev_ref[:, cols] = v[ts - 8:ts, :]
            conv = sm_ref[1:2, cols] * v2 + sm_ref[2:3, cols] * v1 + sm_ref[3:4, cols] * v
            conv_ref[:, cols] = conv.astype(BF16)
            yh = (b * conv * _silu(z)[0]).astype(BF16)
            y_ref[:, cols] = yh
            ya = ya + _dot(yh, wout_ref[cols, :])
        ya_ref[...] = ya
        h1_ref[...] = xf + _rms(ya)[0] * sm_ref[4:5, :]

    outs = pl.pallas_call(
        body,
        name="layer_a_fwd",
        grid=(nt,),
        in_specs=[_rows(ts, d), _full(sm.shape), _full(win_g.shape), _full(wout.shape)] + [HBM_SPEC] * nl,
        out_specs=[_rows(ts, d), _rows(ts, d), _rows(ts, 4 * width), _rows(ts, width), _rows(ts, width), _rows(ts, d)]
        + [HBM_SPEC] * nl,
        out_shape=[
            jax.ShapeDtypeStruct((seq, d), F32),
            jax.ShapeDtypeStruct((seq, d), BF16),
            jax.ShapeDtypeStruct((seq, 4 * width), BF16),
            jax.ShapeDtypeStruct((seq, width), BF16),
            jax.ShapeDtypeStruct((seq, width), BF16),
            jax.ShapeDtypeStruct((seq, d), F32),
        ] + [jax.ShapeDtypeStruct((N_DEV,) + s.shape, s.dtype) for s in later],
        scratch_shapes=[pltpu.VMEM((8, width), F32)] + _exchange_sems(nl),
        compiler_params=_params(("arbitrary",), 56),
    )(x2, sm, win_g, wout, *later)
    return outs[:6], outs[6:]


Q_BLOCKS = 4
ATTN_BWD_LAGS = (2, 4)
ATTN_FWD_LAGS = (2, 4)


def _banded_tiles(kvp_ref, kvc_ref):
    tile = kvc_ref[...].astype(F32)
    blocks = [kvp_ref[...].astype(F32)] + [tile[u * BLOCK:(u + 1) * BLOCK] for u in range(Q_BLOCKS)]
    return [_banded_kv(blocks[u], blocks[u + 1]) for u in range(Q_BLOCKS)]


def _bias_of(bias_ref, i, u, m):
    return bias_ref[jnp.minimum(i, 1) if u == 0 else 1, m]


def _banded_kv(kvp, kvc):
    kw = N_KV_HEADS * HEAD_DIM
    out = []
    for full in (jnp.concatenate([kvp[:, :kw], kvc[:, :kw]], axis=0), jnp.concatenate([kvp[:, kw:], kvc[:, kw:]], axis=0)):
        lo = lax.broadcasted_iota(jnp.int32, full.shape, 1) < HEAD_DIM
        rolled = pltpu.roll(full, HEAD_DIM, 1)
        x2 = [jnp.where(lo, full, rolled).astype(BF16), jnp.where(lo, rolled, full).astype(BF16)]
        ft = full.T
        x2t = [jnp.concatenate([ft[kh * HEAD_DIM:(kh + 1) * HEAD_DIM]] * 2, axis=0).astype(BF16) for kh in range(N_KV_HEADS)]
        out += [x2, x2t]
    return out


def _pair_rows(ref, rows, m, scale=None):
    both = ref[rows, m * LANES:(m + 1) * LANES].astype(F32)
    if scale is not None:
        both = both * scale
    lo = lax.broadcasted_iota(jnp.int32, both.shape, 1) < HEAD_DIM
    zero = jnp.zeros_like(both)
    return jnp.concatenate([jnp.where(lo, both, zero), jnp.where(lo, zero, both)], axis=0).astype(BF16)


def _pair_cols(res_t):
    top = lax.broadcasted_iota(jnp.int32, (LANES, BLOCK), 0) < HEAD_DIM
    return jnp.where(top, res_t[:, :BLOCK], res_t[:, BLOCK:]).T


def _sink_row(sink_ref, m):
    first = lax.broadcasted_iota(jnp.int32, (1, 2 * BLOCK), 1) < BLOCK
    return jnp.where(first, sink_ref[0, 2 * m], sink_ref[0, 2 * m + 1])


def _softmax_t(logits, sink):
    mx =jnp.maximum(jnp.max(logits, axis=0, keepdims=True), sink)
    p = jnp.exp(logits - mx)
    sink_p = jnp.exp(sink - mx)
    inv = 1.0 / (jnp.sum(p, axis=0, keepdims=True) + sink_p)
    return p * inv, sink_p * inv


def _layer_b_fwd(h1, target, kvn, bpre, wkv, wbin_g, biasm, sinks, wbout, bpost):
    seq, d = h1.shape
    kvw = wkv.shape[1]
    cw = wbin_g.shape[2]
    aw = N_Q_HEADS * HEAD_DIM
    per = aw // cw
    tile = Q_BLOCKS * BLOCK

    def body(sink_ref, h1_ref, tgt_ref, kvn_ref, bpre_ref, wkv_ref, wbin_ref, bias_ref, w_ref, g_ref,
             n3_ref, n4_ref, kvc_ref, q_ref, o_ref, dh2_ref, dyb_ref, dattn_ref, dz2_ref, acc_ref,
             attn_ref, z2_ref, kvp_ref):
        i = pl.program_id(0)

        @pl.when(i == 0)
        def _():
            acc_ref[...] = jnp.zeros_like(acc_ref)
            kvp_ref[...] = jnp.zeros_like(kvp_ref)

        hn, _ = _rms(h1_ref[...])
        n3 = (hn * kvn_ref[...]).astype(BF16)
        n4 = (hn * bpre_ref[...]).astype(BF16)
        n3_ref[...] = n3
        n4_ref[...] = n4
        kvc_ref[...] = _dot(n3, wkv_ref[...]).astype(BF16)
        for j in range(N_DEV):
            pj = _dot(n4, wbin_ref[j])
            if j < per:
                q_ref[:, j * cw:(j + 1) * cw] = pj.astype(BF16)
            else:
                z2_ref[:, (j - per) * cw:(j - per + 1) * cw] = pj

        banded = _banded_tiles(kvp_ref, kvc_ref)
        kvp_ref[...] = kvc_ref[tile - BLOCK:tile, :]
        units = [(u, m) for u in range(Q_BLOCKS) for m in range(N_PAIRS)]
        kv_of = lambda m: (2 * m) // GROUP
        logits, probs = {}, {}
        lag_b, lag_c = ATTN_FWD_LAGS
        for step in range(len(units) + lag_c):
            if step < len(units):
                u, m = units[step]
                qpair = _pair_rows(q_ref, slice(u * BLOCK, (u + 1) * BLOCK), m, SCALE)
                logits[step] = _dot_nt(banded[u][0][kv_of(m)], qpair) + _bias_of(bias_ref, i, u, m)
            if 0 <= step - lag_b < len(units):
                u, m = units[step - lag_b]
                probs[step - lag_b] = _softmax_t(logits.pop(step - lag_b), _sink_row(sink_ref, m))[0].astype(BF16)
            if 0 <= step - lag_c < len(units):
                u, m = units[step - lag_c]
                out_t = _dot(banded[u][3][kv_of(m)], probs.pop(step - lag_c))
                attn_ref[u * BLOCK:(u + 1) * BLOCK, m * LANES:(m + 1) * LANES] = _pair_cols(out_t)
        attn = attn_ref[...]
        sz, dsz = _silu(z2_ref[...])
        o = (attn * sz).astype(BF16)
        o_ref[...] = o

        w = w_ref[...]
        yb = _dot(o, w)
        ybn, r = _rms(yb)
        g = g_ref[...]
        diff = h1_ref[...] + ybn * g - tgt_ref[...]
        dh2 = diff * (1.0 / d)
        dh2_ref[...] = dh2
        acc_ref[0:1, :] += jnp.sum(dh2 * ybn, axis=0, keepdims=True)
        tok = jnp.mean(diff * diff, axis=-1, keepdims=True)
        acc_ref[1:2, :] += 0.5 * jnp.sum(tok, axis=0, keepdims=True)
        dyb = _rms_bwd(dh2 * g, ybn, r).astype(BF16)
        dyb_ref[...] = dyb
        do = _dot_nt(dyb, w)
        dattn_ref[...] = (do * sz).astype(BF16)
        dz2_ref[...] = (do * attn * dsz).astype(BF16)

    blk = lambda w: pl.BlockSpec((tile, w), lambda i: (i, 0))
    return pl.pallas_call(
        body,
        name="layer_b_fwd",
        grid=(seq // tile,),
        in_specs=[
            pl.BlockSpec(memory_space=pltpu.SMEM),
            blk(d),
            blk(d),
            _full(kvn.shape),
            _full(bpre.shape),
            _full(wkv.shape),
            _full(wbin_g.shape),
            _full(biasm.shape),
            _full(wbout.shape),
            _full(bpost.shape),
        ],
        out_specs=[blk(d), blk(d), blk(kvw), blk(aw), blk(aw), blk(d), blk(d), blk(aw), blk(aw), _resident((8, d))],
        out_shape=[
            jax.ShapeDtypeStruct((seq, d), BF16),
            jax.ShapeDtypeStruct((seq, d), BF16),
            jax.ShapeDtypeStruct((seq, kvw), BF16),
            jax.ShapeDtypeStruct((seq, aw), BF16),
            jax.ShapeDtypeStruct((seq, aw), BF16),
            jax.ShapeDtypeStruct((seq, d), F32),
            jax.ShapeDtypeStruct((seq, d), BF16),
            jax.ShapeDtypeStruct((seq, aw), BF16),
            jax.ShapeDtypeStruct((seq, aw), BF16),
            jax.ShapeDtypeStruct((8, d), F32),
        ],
        scratch_shapes=[pltpu.VMEM((tile, aw), F32), pltpu.VMEM((tile, aw), F32), pltpu.VMEM((BLOCK, kvw), BF16)],
        compiler_params=_params(("arbitrary",), 56),
    )(sinks, h1, target, kvn, bpre, wkv, wbin_g, biasm, wbout, bpost)


def _attn_bwd(q, kv, dattn, biasm, sinks, ready):
    seq, aw = q.shape
    kvw = kv.shape[1]
    kw = N_KV_HEADS * HEAD_DIM
    nb = seq // BLOCK
    pairs_per_kv = N_PAIRS // N_KV_HEADS
    nr = len(ready)

    tile = Q_BLOCKS * BLOCK
    nsteps = seq // tile
    held = (Q_BLOCKS - 1) * BLOCK

    def body(sink_ref, q_ref, kvc_ref, kvp_ref, da_ref, bias_ref, *refs):
        ready_refs, (dq_ref, dkv_ref, dssum_ref, dsink_ref) = refs[:nr], refs[nr:nr + 4]
        landed_refs, scratch = refs[nr + 4:2 * nr + 4], refs[2 * nr + 4:]
        carry_ref, done_ref, qs_ref, dos_ref, dst_ref, pt_ref, *sems = scratch
        i = pl.program_id(0)

        @pl.when(i == 0)
        def _():
            dssum_ref[...] = jnp.zeros_like(dssum_ref)
            dsink_ref[...] = jnp.zeros_like(dsink_ref)
            carry_ref[...] = jnp.zeros_like(carry_ref)
            done_ref[...] = jnp.zeros_like(done_ref)
            if nr:
                _exchange_start(ready_refs, landed_refs, *sems, True)

        if nr:
            @pl.when(i == nsteps)
            def _():
                _exchange_wait(ready_refs, landed_refs, *sems, True)

        @pl.when(i < nsteps)
        def _():
            lo = lax.broadcasted_iota(jnp.int32, (BAND, LANES), 1) < HEAD_DIM
            head_lane = lax.broadcasted_iota(jnp.int32, (1, LANES), 1)
            banded = _banded_tiles(kvp_ref, kvc_ref)
            units = [(u, m) for u in range(Q_BLOCKS) for m in range(N_PAIRS)]
            dsink = jnp.zeros((1, LANES), F32)
            folded = {}
            logits, dps, dsbs = {}, {}, {}
            lag_b, lag_c = ATTN_BWD_LAGS
            for step in range(len(units) + lag_c):
                if step < len(units):
                    u, m = units[step]
                    kh, rows = m // pairs_per_kv, slice((m % pairs_per_kv) * BAND, (m % pairs_per_kv + 1) * BAND)
                    qrows = slice(u * BLOCK, (u + 1) * BLOCK)
                    qpair = _pair_rows(q_ref, qrows, m, SCALE)
                    dopair = _pair_rows(da_ref, qrows, m)
                    qs_ref[u, kh, rows, :] = qpair
                    dos_ref[u, kh, rows, :] = dopair
                    logits[step] = _dot_nt(banded[u][0][kh], qpair) + _bias_of(bias_ref, i, u, m)
                    dps[step] = _dot_nt(banded[u][2][kh], dopair)
                if 0 <= step - lag_b < len(units):
                    u, m = units[step - lag_b]
                    kh, rows = m // pairs_per_kv, slice((m % pairs_per_kv) * BAND, (m % pairs_per_kv + 1) * BAND)
                    pn, sink_p = _softmax_t(logits.pop(step - lag_b), _sink_row(sink_ref, m))
                    dp = dps.pop(step - lag_b)
                    delta = jnp.sum(pn * dp, axis=0, keepdims=True)
                    ds = pn * (dp - delta)
                    dssum_ref[m] += ds
                    sink_term = sink_p * delta
                    for e in range(2):
                        total = jnp.sum(sink_term[:, e * BLOCK:(e + 1) * BLOCK], axis=1, keepdims=True)
                        dsink = dsink - jnp.where(head_lane == 2 * m + e, total, 0.0)
                    dsbs[step - lag_b] = ds.astype(BF16)
                    dst_ref[u, kh, :, rows] = dsbs[step - lag_b]
                    pt_ref[u, kh, :, rows] = pn.astype(BF16)
                if 0 <= step - lag_c < len(units):
                    u, m = units[step - lag_c]
                    kh = m // pairs_per_kv
                    dq_t = _dot(banded[u][1][kh], dsbs.pop(step - lag_c))
                    dq_ref[u * BLOCK:(u + 1) * BLOCK, m * LANES:(m + 1) * LANES] = (_pair_cols(dq_t) * SCALE).astype(BF16)
                    if m % pairs_per_kv == pairs_per_kv - 1:
                        for name, lhs_ref, rhs_ref in (("k", dst_ref, qs_ref), ("v", pt_ref, dos_ref)):
                            acc = _dot(lhs_ref[u, kh], rhs_ref[u, kh])
                            folded[u, kh, name] = acc + pltpu.roll(acc, HEAD_DIM, 1)
            dsink_ref[0:1, :] += dsink
            dkv = [jnp.concatenate([jnp.where(lo, folded[u, 0, n], folded[u, 1, n]) for n in ("k", "v")], axis=1)
                   for u in range(Q_BLOCKS)]

            @pl.when(i > 0)
            def _():
                if held:
                    dkv_ref[:held, :] = done_ref[...].astype(BF16)
                dkv_ref[held:, :] = (carry_ref[...] + dkv[0][:BLOCK]).astype(BF16)

            for u in range(Q_BLOCKS - 1):
                done_ref[u * BLOCK:(u + 1) * BLOCK, :] = dkv[u][BLOCK:] + dkv[u + 1][:BLOCK]
            carry_ref[...] = dkv[Q_BLOCKS - 1][BLOCK:]

        @pl.when(i == nsteps)
        def _():
            if held:
                dkv_ref[:held, :] = done_ref[...].astype(BF16)
            dkv_ref[held:, :] = carry_ref[...].astype(BF16)

    last = nsteps - 1
    blk = lambda w: pl.BlockSpec((tile, w), lambda i: (jnp.minimum(i, last), 0))
    outs = pl.pallas_call(
        body,
        name="attn_bwd",
        grid=(nsteps + 1,),
        in_specs=[
            pl.BlockSpec(memory_space=pltpu.SMEM),
            blk(aw),
            blk(kvw),
            pl.BlockSpec((BLOCK, kvw), lambda i: (jnp.clip(Q_BLOCKS * i - 1, 0, nb - 1), 0)),
            blk(aw),
            _full(biasm.shape),
        ] + [HBM_SPEC] * nr,
        out_specs=[
            blk(aw),
            pl.BlockSpec((tile, kvw), lambda i: (jnp.maximum(i - 1, 0), 0)),
            _resident(biasm.shape[1:]),
            _resident((8, LANES)),
        ] + [HBM_SPEC] * nr,
        out_shape=[
            jax.ShapeDtypeStruct((seq, aw), BF16),
            jax.ShapeDtypeStruct((seq, kvw), BF16),
            jax.ShapeDtypeStruct(biasm.shape[1:], F32),
            jax.ShapeDtypeStruct((8, LANES), F32),
        ] + [jax.ShapeDtypeStruct(g.shape, g.dtype) for g in ready],
        scratch_shapes=[
            pltpu.VMEM((BLOCK, kvw), F32),
            pltpu.VMEM((max(held, 8), kvw), F32),
            pltpu.VMEM((Q_BLOCKS, N_KV_HEADS, pairs_per_kv * BAND, LANES), BF16),
            pltpu.VMEM((Q_BLOCKS, N_KV_HEADS, pairs_per_kv * BAND, LANES), BF16),
            pltpu.VMEM((Q_BLOCKS, N_KV_HEADS, BAND, pairs_per_kv * BAND), BF16),
            pltpu.VMEM((Q_BLOCKS, N_KV_HEADS, BAND, pairs_per_kv * BAND), BF16),
        ] + _exchange_sems(nr),
        compiler_params=_params(("arbitrary",), 48),
    )(sinks, q, kv, kv, dattn, biasm, *ready)
    return outs[:4], outs[4:]


def _relbias_grad(dssum2, bucket_row, chunk):
    heads, n = dssum2.shape

    def body(a_ref, bucket_ref, out_ref):
        @pl.when(pl.program_id(0) == 0)
        def _():
            out_ref[...] = jnp.zeros_like(out_ref)

        a = a_ref[...]
        hi = a.astype(BF16)
        lo = (a - hi.astype(F32)).astype(BF16)
        onehot_t = (lax.broadcasted_iota(jnp.int32, (LANES, chunk), 0) == bucket_ref[...]).astype(F32).astype(BF16)
        out_ref[...] += _dot_nt(hi, onehot_t) + _dot_nt(lo, onehot_t)

    return pl.pallas_call(
        body,
        name="relbias_grad",
        grid=(n // chunk,),
        in_specs=[pl.BlockSpec((heads, chunk), lambda i: (0, i)), pl.BlockSpec((1, chunk), lambda i: (0, i))],
        out_specs=_resident((heads, LANES)),
        out_shape=jax.ShapeDtypeStruct((heads, LANES), F32),
        compiler_params=_params(("arbitrary",), 32),
    )(dssum2, bucket_row)


def _layer_b_in_bwd(dh2, dq, dz2, dkv, h1, ya, wbin_g, wkv, kvn, bpre, sm, ready, ts):
    seq, d = h1.shape
    aw = dq.shape[1]
    kvw = dkv.shape[1]
    cw = wbin_g.shape[2]
    per = aw // cw

    nr = len(ready)
    nt = seq // ts

    def body(dh2_ref, dq_ref, dz2_ref, dkv_ref, h1_ref, ya_ref, wbin_ref, wkv_ref, kvn_ref, bpre_ref, sm_ref, *refs):
        ready_refs, (dh1_ref, dya_ref, acc_ref) = refs[:nr], refs[nr:nr + 3]
        landed_refs, sems = refs[nr + 3:2 * nr + 3], refs[2 * nr + 3:]

        @pl.when(pl.program_id(0) == 0)
        def _():
            acc_ref[...] = jnp.zeros_like(acc_ref)
            if nr:
                _exchange_start(ready_refs, landed_refs, *sems, True)

        if nr:
            @pl.when(pl.program_id(0) == nt - 1)
            def _():
                _exchange_wait(ready_refs, landed_refs, *sems, True)

        dn4 = jnp.zeros((ts, d), F32)
        for j in range(N_DEV):
            src = dq_ref if j < per else dz2_ref
            jj = j % per
            dn4 = dn4 + _dot_nt(src[:, jj * cw:(jj + 1) * cw], wbin_ref[j])
        dn3 = _dot_nt(dkv_ref[...], wkv_ref[...])
        hn, r = _rms(h1_ref[...])
        acc_ref[0:1, :] += jnp.sum(dn4 * hn, axis=0, keepdims=True)
        acc_ref[1:2, :] += jnp.sum(dn3 * hn, axis=0, keepdims=True)
        dh1 = dh2_ref[...] + _rms_bwd(dn4 * bpre_ref[...] + dn3 * kvn_ref[...], hn, r)
        dh1_ref[...] = dh1
        yan, r2 = _rms(ya_ref[...])
        acc_ref[2:3, :] += jnp.sum(dh1 * yan, axis=0, keepdims=True)
        dya_ref[...] = _rms_bwd(dh1 * sm_ref[4:5, :], yan, r2).astype(BF16)

    outs = pl.pallas_call(
        body,
        name="layer_b_in_bwd",
        grid=(nt,),
        in_specs=[_rows(ts, d), _rows(ts, aw), _rows(ts, aw), _rows(ts, kvw), _rows(ts, d), _rows(ts, d),
                  _full(wbin_g.shape), _full(wkv.shape), _full(kvn.shape), _full(bpre.shape), _full(sm.shape)]
        + [HBM_SPEC] * nr,
        out_specs=[_rows(ts, d), _rows(ts, d), _resident((8, d))] + [HBM_SPEC] * nr,
        out_shape=[jax.ShapeDtypeStruct((seq, d), F32), jax.ShapeDtypeStruct((seq, d), BF16),
                   jax.ShapeDtypeStruct((8, d), F32)] + [jax.ShapeDtypeStruct(g.shape, g.dtype) for g in ready],
        scratch_shapes=_exchange_sems(nr),
        compiler_params=_params(("arbitrary",), 48),
    )(dh2, dq, dz2, dkv, h1, ya, wbin_g, wkv, kvn, bpre, sm, *ready)
    return outs[:3], outs[3:]


def _layer_a_bwd(dya, proj, conv, dh1, x2, wout, win_g, sm, ts):
    seq, d = x2.shape
    width = wout.shape[0]
    half = win_g.shape[2]
    n_half = width // half
    nt = seq // ts

    def body(dya_ref, proj_ref, conv_ref, dh1_ref, x_ref, wout_ref, win_ref, sm_ref, dproj_ref, gx_ref, acc_ref,
             dnext_ref):
        @pl.when(pl.program_id(0) == 0)
        def _():
            acc_ref[...] = jnp.zeros_like(acc_ref)
            dnext_ref[...] = jnp.zeros_like(dnext_ref)

        dy = _dot_nt(dya_ref[...], wout_ref[...])
        row = lax.broadcasted_iota(jnp.int32, (ts, half), 0)
        dn1 = jnp.zeros((ts, d), F32)
        for hh in range(n_half):
            cols = slice(hh * half, (hh + 1) * half)
            b, c, u, z = [proj_ref[:, (part * n_half + hh) * half:(part * n_half + hh + 1) * half].astype(F32)
                          for part in range(4)]
            cv = conv_ref[:, cols].astype(F32)
            dyh = dy[:, cols]
            sz, dsz = _silu(z)
            dconv = dyh * b * sz
            grads = [dyh * cv * sz, None, None, dyh * b * cv * dsz]
            next0, next1 = dnext_ref[0:1, cols], dnext_ref[1:2, cols]
            dc1 = jnp.where(row == ts - 1, next0, pltpu.roll(dconv, ts - 1, 0))
            dc2 = jnp.where(row == ts - 1, next1, jnp.where(row == ts - 2, next0, pltpu.roll(dconv, ts - 2, 0)))
            dnext_ref[:, cols] = dconv[0:8, :]
            v = c * u
            acc_ref[1:2, cols] += jnp.sum(dc2 * v, axis=0, keepdims=True)
            acc_ref[2:3, cols] += jnp.sum(dc1 * v, axis=0, keepdims=True)
            acc_ref[3:4, cols] += jnp.sum(dconv * v, axis=0, keepdims=True)
            dv = sm_ref[3:4, cols] * dconv + sm_ref[2:3, cols] * dc1 + sm_ref[1:2, cols] * dc2
            grads[1] = dv * u
            grads[2] = dv * c
            for part in range(4):
                j = part * n_half + hh
                gj = grads[part].astype(BF16)
                dproj_ref[:, j * half:(j + 1) * half] = gj
                dn1 = dn1 + _dot_nt(gj, win_ref[j])
        xn, r = _rms(x_ref[...])
        acc_ref[0:1, :] += jnp.sum(dn1 * xn, axis=0, keepdims=True)
        gx_ref[...] = dh1_ref[...] + _rms_bwd(dn1 * sm_ref[0:1, :], xn, r)

    rev = lambda w: pl.BlockSpec((ts, w), lambda i: (nt - 1 - i, 0))
    return pl.pallas_call(
        body,
        name="layer_a_bwd",
        grid=(nt,),
        in_specs=[rev(d), rev(4 * width), rev(width), rev(d), rev(d), _full(wout.shape), _full(win_g.shape), _full(sm.shape)],
        out_specs=[rev(4 * width), rev(d), _resident((8, d))],
        out_shape=[jax.ShapeDtypeStruct((seq, 4 * width), BF16), jax.ShapeDtypeStruct((seq, d), F32),
                   jax.ShapeDtypeStruct((8, d), F32)],
        scratch_shapes=[pltpu.VMEM((8, width), F32)],
        compiler_params=_params(("arbitrary",), 56),
    )(dya, proj, conv, dh1, x2, wout, win_g, sm)


def _wgrad(a, bs, n_slots, ts, name, ready=(), block_cols=1024):
    nr = len(ready)
    seq, k = a.shape
    nb_in = len(bs)
    n_each = bs[0].shape[1]
    n = nb_in * n_each
    bn = min(n_each, block_cols)
    per_in = n_each // bn
    n_blocks = nb_in * per_in
    ns = seq // ts

    def b_spec(idx):
        def index(j, s):
            mine = j // per_in == idx
            row = jnp.where(mine, s, jnp.where(j // per_in > idx, ns - 1, 0))
            return (row, jnp.where(mine, j % per_in, jnp.where(j // per_in > idx, per_in - 1, 0)))
        return pl.BlockSpec((ts, bn), index)

    if n_slots:
        sw = n // n_slots
        spb = bn // sw
        out_shape = jax.ShapeDtypeStruct((n_slots, k, sw), BF16)
        out_spec = pl.BlockSpec((spb, k, sw), lambda j, s: (j, 0, 0))
    else:
        out_shape = jax.ShapeDtypeStruct((k, n), BF16)
        out_spec = pl.BlockSpec((k, bn), lambda j, s: (0, j))

    def body(a_ref, *refs):
        b_refs, ready_refs, o_ref = refs[:nb_in], refs[nb_in:nb_in + nr], refs[nb_in + nr]
        landed_refs, (acc_ref, *sems) = refs[nb_in + nr + 1:nb_in + 2 * nr + 1], refs[nb_in + 2 * nr + 1:]
        j, s = pl.program_id(0), pl.program_id(1)

        if nr:
            @pl.when(jnp.logical_and(j == 0, s == 0))
            def _():
                _exchange_start(ready_refs, landed_refs, *sems, True)

            @pl.when(jnp.logical_and(j == n_blocks - 1, s == ns - 1))
            def _():
                _exchange_wait(ready_refs, landed_refs, *sems, True)

        @pl.when(s == 0)
        def _():
            acc_ref[...] = jnp.zeros_like(acc_ref)

        for idx in range(nb_in):
            @pl.when(j // per_in == idx)
            def _(idx=idx):
                acc_ref[...] += _dot_tn(a_ref[...], b_refs[idx][...])

        @pl.when(s == ns - 1)
        def _():
            if n_slots:
                for e in range(spb):
                    o_ref[e] = acc_ref[:, e * sw:(e + 1) * sw].astype(BF16)
            else:
                o_ref[...] = acc_ref[...].astype(BF16)

    outs = pl.pallas_call(
        body,
        name=name,
        grid=(n_blocks, ns),
        in_specs=[pl.BlockSpec((ts, k), lambda j, s: (s, 0))] + [b_spec(idx) for idx in range(nb_in)] + [HBM_SPEC] * nr,
        out_specs=[out_spec] + [HBM_SPEC] * nr,
        out_shape=[out_shape] + [jax.ShapeDtypeStruct(g.shape, g.dtype) for g in ready],
        scratch_shapes=[pltpu.VMEM((k, bn), F32)] + (_exchange_sems(nr) if nr else []),
        compiler_params=_params(("arbitrary", "arbitrary"), 48),
    )(a, *bs, *ready)
    return (outs[0], outs[1:]) if nr else outs[0]


def _wgrad_tail(pairs, part, landed, ts):
    n_tasks = len(pairs)
    assert n_tasks == 2
    nl = len(landed)
    seq, k = pairs[0][0].shape
    n = pairs[0][1].shape[1]
    ns = seq // ts
    total = n_tasks * ns
    per = k // N_DEV
    n_red = len(_chip_reduce_scratch((per, n)))

    def spec(t, width):
        return pl.BlockSpec((ts, width), lambda j, s: (jnp.where(j == t, s, jnp.where(j > t, ns - 1, 0)), 0))

    def body(*refs):
        ab_refs, part_hbm = refs[:2 * n_tasks], refs[2 * n_tasks]
        landed_hbm, refs = refs[2 * n_tasks + 1:2 * n_tasks + 1 + nl], refs[2 * n_tasks + 1 + nl:]
        o_ref, red_ref, early_ref = refs[:3]
        summed_refs, (acc_ref, first_ref, part_ref, *scratch) = refs[3:3 + nl], refs[3 + nl:]
        landed_refs, load_sems, scratch = scratch[:nl], scratch[nl], scratch[nl + 1:]
        j, s = pl.program_id(0), pl.program_id(1)
        flat = j * ns + s
        swap, send, forward, finish = _chip_reduce(part_ref, red_ref, *scratch[:3], scratch[3:n_red], part_hbm)
        swap_first, send_first, forward_first, finish_first = _chip_reduce(
            first_ref, early_ref, *scratch[n_red:n_red + 3], scratch[n_red + 3:])
        loads = [pltpu.make_async_copy(src, dst, load_sems.at[i])
                 for i, (src, dst) in enumerate(zip([part_hbm, *landed_hbm], [part_ref, *landed_refs]))]

        @pl.when(flat == 0)
        def _():
            swap()
            for load in loads:
                load.start()

        @pl.when(flat == min(1, total - 1))
        def _():
            loads[0].wait()
            send()

        @pl.when(flat == min(total // 2 + 1, total - 1))
        def _():
            forward()
            for t in range(nl):
                loads[1 + t].wait()
                _sum_slots(landed_refs[t], summed_refs[t])

        @pl.when(flat == ns)
        def _():
            send_first()

        @pl.when(flat == min(ns + ns // 2, total - 1))
        def _():
            forward_first()

        @pl.when(s == 0)
        def _():
            acc_ref[...] = jnp.zeros_like(acc_ref)

        for t in range(n_tasks):
            @pl.when(j == t)
            def _(t=t):
                acc_ref[...] += _dot_tn(ab_refs[2 * t][...], ab_refs[2 * t + 1][...])

        @pl.when(flat == ns - 1)
        def _():
            for dev in range(N_DEV):
                first_ref[dev] = acc_ref[dev * per:(dev + 1) * per, :].astype(BF16)
            swap_first()

        @pl.when(flat == total - 1)
        def _():
            for dev in range(N_DEV):
                o_ref[dev] = acc_ref[dev * per:(dev + 1) * per, :].astype(BF16)
            finish()
            finish_first()

    slot = part.shape[1:]
    outs = pl.pallas_call(
        body,
        name="wgrad_tail",
        grid=(n_tasks, ns),
        in_specs=[spec(t, w) for t in range(n_tasks) for w in (k, n)] + [HBM_SPEC] * (1 + nl),
        out_specs=[_resident((N_DEV, per, n)), _resident(slot), _resident((per, n))]
        + [_resident(g.shape[1:]) for g in landed],
        out_shape=[jax.ShapeDtypeStruct((N_DEV, per, n), BF16), jax.ShapeDtypeStruct(slot, F32),
                   jax.ShapeDtypeStruct((per, n), F32)]
        + [jax.ShapeDtypeStruct(g.shape[1:], F32) for g in landed],
        scratch_shapes=[pltpu.VMEM((k, n), F32), pltpu.VMEM((N_DEV, per, n), BF16), pltpu.VMEM(part.shape, part.dtype)]
        + [pltpu.VMEM(g.shape, g.dtype) for g in landed] + [pltpu.SemaphoreType.DMA((1 + nl,))]
        + _chip_reduce_scratch(slot) + _chip_reduce_scratch((per, n)),
        compiler_params=_params(("arbitrary", "arbitrary")),
    )(*[op for pair in pairs for op in pair], part, *landed)
    return outs[0], outs[1], outs[2], outs[3:]


MINE = "mine"
ADAMW_STEPS = 4


def _adamw(ws, sources, picks, loss_at, ms, vs):
    n, n_src = len(ws), len(sources)
    streamed = [len(w.shape) == 2 and w.shape[0] >= 128 and picks[t][1:] == (0, None)
                and sources[picks[t][0]].shape == w.shape for t, w in enumerate(ws)]
    streamed_sources = {picks[t][0] for t in range(n) if streamed[t]}

    def step(w, g, m, v):
        m = ADAM_B1 * m + (1.0 - ADAM_B1) * g
        v = ADAM_B2 * v + (1.0 - ADAM_B2) * jnp.square(g)
        m_hat = m / (1.0 - ADAM_B1 ** ADAM_STEP)
        v_hat = v / (1.0 - ADAM_B2 ** ADAM_STEP)
        return g, -ADAM_LR * (m_hat / (jnp.sqrt(v_hat) + ADAM_EPS) + ADAM_WD * w), m, v

    def body(*refs):
        refs = list(refs)
        take = lambda k: [refs.pop(0) for _ in range(k)]
        w_refs, s_refs, m_refs, v_refs = take(n), take(n_src), take(n), take(n)
        (loss_ref,), go_refs, d_refs, nm_refs, nv_refs = take(1), take(n), take(n), take(n), take(n)
        me = _my_index()

        def grad(t, rows):
            k, first, cols = picks[t]
            if cols is None:
                return s_refs[k][rows, :]
            if cols is not MINE:
                return s_refs[k][rows, cols]
            width = w_refs[t].shape[-1]
            g = s_refs[k][rows, 0:width]
            for dev in range(1, N_DEV):
                g = jnp.where(me == dev, s_refs[k][rows, dev * width:(dev + 1) * width], g)
            return g

        def whole(t):
            first = picks[t][1]
            rows = w_refs[t].shape[0]
            if len(w_refs[t].shape) == 3:
                for j in range(rows):
                    go_refs[t][j], d_refs[t][j], nm_refs[t][j], nv_refs[t][j] = step(
                        w_refs[t][j], grad(t, slice(first + j, first + j + 1)), m_refs[t][j], v_refs[t][j])
                return
            go_refs[t][...], d_refs[t][...], nm_refs[t][...], nv_refs[t][...] = step(
                w_refs[t][...], grad(t, slice(first, first + rows)), m_refs[t][...], v_refs[t][...])

        def block(t):
            rows = w_refs[t].shape[0]
            chunk = min(rows, 128)

            def one(i, carry):
                r = pl.ds(pl.multiple_of(i * chunk, chunk), chunk)
                go_refs[t][r, :], d_refs[t][r, :], nm_refs[t][r, :], nv_refs[t][r, :] = step(
                    w_refs[t][r, :], grad(t, r), m_refs[t][r, :], v_refs[t][r, :])
                return carry

            lax.fori_loop(0, rows // chunk, one, 0)

        @pl.when(pl.program_id(0) == 0)
        def _():
            loss_ref[...] = s_refs[loss_at[0]][loss_at[1]:loss_at[1] + 1, 0:1]
            for t in range(n):
                if not streamed[t]:
                    whole(t)

        for t in range(n):
            if streamed[t]:
                block(t)

    def rows_of(shape):
        return pl.BlockSpec((shape[0] // ADAMW_STEPS, shape[1]), lambda i: (i, 0))

    w_in = [rows_of(w.shape) if streamed[t] else _full(w.shape) for t, w in enumerate(ws)]
    w_out = [rows_of(w.shape) if streamed[t] else _resident(w.shape) for t, w in enumerate(ws)]
    s_in = [rows_of(s.shape) if k in streamed_sources else _full(s.shape) for k, s in enumerate(sources)]
    outs = pl.pallas_call(
        body,
        name="adamw",
        grid=(ADAMW_STEPS,),
        in_specs=w_in + s_in + w_in * 2,
        out_specs=[_resident((1, 1))] + w_out * 4,
        out_shape=[jax.ShapeDtypeStruct((1, 1), F32)] + [jax.ShapeDtypeStruct(w.shape, F32) for w in ws] * 4,
        compiler_params=_params(("arbitrary",)),
    )(*ws, *sources, *ms, *vs)
    return outs[0], outs[1:n + 1], outs[n + 1:2 * n + 1], outs[2 * n + 1:3 * n + 1], outs[3 * n + 1:]


def _band_structure():
    q_loc = np.arange(BLOCK, dtype=np.int32)[:, None]
    s_loc = np.arange(2 * BLOCK, dtype=np.int32)[None, :]
    dist = q_loc + BLOCK - s_loc
    in_window = (dist >= 0) & (dist < BLOCK)
    dd = np.maximum(dist, 0)
    max_exact = N_BUCKETS // 2
    large = max_exact + (np.log(np.maximum(dd, 1) / max_exact) / math.log(MAX_DISTANCE / max_exact)
                         * (N_BUCKETS - max_exact)).astype(np.int32)
    bucket = np.where(dd < max_exact, dd, np.minimum(large, N_BUCKETS - 1)).astype(np.int32)
    return bucket, in_window.astype(np.int32)


def kernel(x, a_pre_norm, a_w_in, a_conv_w, a_w_out, a_post_norm, kv_norm, w_kv, rel_bias, b_pre_norm, b_w_in, b_sinks, b_w_out, b_post_norm, loss_target, m_a_pre_norm, m_a_w_in, m_a_conv_w, m_a_w_out, m_a_post_norm, m_kv_norm, m_w_kv, m_rel_bias, m_b_pre_norm, m_b_w_in, m_b_sinks, m_b_w_out, m_b_post_norm, v_a_pre_norm, v_a_w_in, v_a_conv_w, v_a_w_out, v_a_post_norm, v_kv_norm, v_w_kv, v_rel_bias, v_b_pre_norm, v_b_w_in, v_b_sinks, v_b_w_out, v_b_post_norm):
    seq, d = x.shape[1], x.shape[2]
    x2 = x.reshape(seq, d)
    target = loss_target.reshape(seq, d)
    shard = a_pre_norm.shape[1]
    ts_a = min(seq, 512)
    ts = min(seq, 512)
    ts_w = min(seq, 2048)

    taps = lambda a: a.transpose(1, 0, 2)
    bucket, in_window = _band_structure()
    (win_g, wout_g), small_g, later, biasm = _all_gather(
        [a_w_in[0], a_w_out[0]], [(0, a_pre_norm), (1, taps(a_conv_w)), (4, a_post_norm)],
        [w_kv, b_w_in[0], b_w_out[0]], rel_bias.T, bucket.T, in_window.T)
    wout = wout_g.reshape(-1, wout_g.shape[2])
    sm = small_g.transpose(1, 0, 2).reshape(8, N_DEV * shard)
    kvn = kv_norm.reshape(1, d)

    (h1, n1, proj, conv, y, ya), (wkv_g, wbin_g, wbout_g) = _layer_a_fwd(x2, sm, win_g, wout, later, ts_a)
    wkv = wkv_g.reshape(-1, wkv_g.shape[2])
    wbout = wbout_g.reshape(-1, wbout_g.shape[2])
    n3, n4, kv, q, o, dh2, dyb, dattn, dz2, acc_c = _layer_b_fwd(
        h1, target, kvn, b_pre_norm, wkv, wbin_g, biasm, b_sinks, wbout, b_post_norm)

    (dq, dkv, dssum, dsink), _ = _attn_bwd(q, kv, dattn, biasm, b_sinks, [])
    by_head = dssum.reshape(N_PAIRS, BAND, 2, BLOCK).transpose(0, 2, 3, 1)
    relb = _relbias_grad(by_head.reshape(N_Q_HEADS, -1), bucket.reshape(1, -1), 4096)
    g_wkv = _wgrad(n3, [dkv], 0, ts_w, "wgrad_kv").reshape(wkv_g.shape)
    g_wbin = _wgrad(n4, [dq, dz2], N_DEV, ts_w, "wgrad_b_in")
    (dh1, dya, acc_b), _ = _layer_b_in_bwd(dh2, dq, dz2, dkv, h1, ya, wbin_g, wkv, kvn, b_pre_norm, sm, [], ts)
    dproj, gx, acc_a = _layer_a_bwd(dya, proj, conv, dh1, x2, wout, win_g, sm, ts_a)
    g_win, (l_wkv, l_wbin) = _wgrad(
        n1, [dproj], N_DEV, ts_w, "wgrad_a_in", ready=[g_wkv, g_wbin], block_cols=2048)
    g_wbout, r_win, r_wout, (r_wkv, r_wbin) = _wgrad_tail(
        [(y, dya), (o, dyb)], g_win, [l_wkv, l_wbin], min(seq, 1024))

    r_wbout, _, (s_a, s_b, s_c, s_relb, s_sink) = _reduce_exchange(g_wbout, [], [acc_a, acc_b, acc_c, relb, dsink])
    weights = [a_pre_norm, a_w_in[0], taps(a_conv_w), a_w_out[0], a_post_norm, kvn, w_kv, rel_bias.T, b_pre_norm,
               b_w_in[0], b_sinks, b_w_out[0], b_post_norm]
    sources = [s_a, s_b, s_c, s_relb, s_sink, r_win, r_wout, r_wkv, r_wbin, r_wbout]
    picks = [(0, 0, MINE), (5, 0, None), (0, 1, MINE), (6, 0, None), (1, 2, MINE), (1, 1, None), (7, 0, None),
             (3, 0, slice(0, N_BUCKETS)), (1, 0, None), (8, 0, None), (4, 0, slice(0, N_Q_HEADS)),
             (9, 0, None), (2, 0, None)]
    first = [m_a_pre_norm, m_a_w_in[0], taps(m_a_conv_w), m_a_w_out[0], m_a_post_norm, m_kv_norm.reshape(1, d),
             m_w_kv, m_rel_bias.T, m_b_pre_norm, m_b_w_in[0], m_b_sinks, m_b_w_out[0], m_b_post_norm]
    second = [v_a_pre_norm, v_a_w_in[0], taps(v_a_conv_w), v_a_w_out[0], v_a_post_norm, v_kv_norm.reshape(1, d),
              v_w_kv, v_rel_bias.T, v_b_pre_norm, v_b_w_in[0], v_b_sinks, v_b_w_out[0], v_b_post_norm]
    loss, grads, deltas, new_m, new_v = _adamw(weights, sources, picks, (2, 1), first, second)

    shapes = [a_pre_norm.shape, a_w_in.shape, taps, a_w_out.shape, a_post_norm.shape, kv_norm.shape,
              w_kv.shape, jnp.transpose, b_pre_norm.shape, b_w_in.shape, b_sinks.shape, b_w_out.shape, b_post_norm.shape]
    shaped = lambda arrays: [s(a) if callable(s) else a.reshape(s) for a, s in zip(arrays, shapes)]
    return (loss.reshape(()), gx.reshape(x.shape), *shaped(grads), *shaped(deltas), *shaped(new_m), *shaped(new_v))
```

```python
import math

import jax
import jax.numpy as jnp
import numpy as np
from jax import lax
from jax.experimental import pallas as pl
from jax.experimental.pallas import tpu as pltpu

HEAD_DIM = 64
N_Q_HEADS = 16
N_KV_HEADS = 2
GROUP = N_Q_HEADS // N_KV_HEADS
BLOCK = 128
N_BUCKETS = 32
MAX_DISTANCE = 128
EPS = 1e-6
NEG_INF = -1e30
SCALE = HEAD_DIM ** -0.5

ADAM_LR = 0.001
ADAM_B1 = 0.9
ADAM_B2 = 0.999
ADAM_EPS = 1e-08
ADAM_WD = 0.01
ADAM_STEP = 10

N_PAIRS = N_Q_HEADS // 2
BAND = 2 * BLOCK

N_DEV = 8
GATHER_PIECE_ROWS = 256
LANES = 128
F32 = jnp.float32
BF16 = jnp.bfloat16
MESH = pl.DeviceIdType.MESH
MIB = 1024 * 1024
VMEM_RESERVED_MIB = 63


def _params(semantics=None, vmem_mib=48):
    del vmem_mib
    return pltpu.CompilerParams(dimension_semantics=semantics, vmem_limit_bytes=VMEM_RESERVED_MIB * MIB)


def _full(shape):
    zeros = (0,) * len(shape)
    return pl.BlockSpec(shape, lambda *_: zeros, pipeline_mode=pl.Buffered(1))


def _resident(shape):
    zeros = (0,) * len(shape)
    return pl.BlockSpec(shape, lambda *_: zeros)


def _rows(ts, cols):
    return pl.BlockSpec((ts, cols), lambda i: (i, 0))


def _dot(a, b):
    return jnp.dot(a, b, preferred_element_type=F32)


def _dot_nt(a, b):
    return lax.dot_general(a, b, (((1,), (1,)), ((), ())), preferred_element_type=F32)


def _dot_tn(a, b):
    return lax.dot_general(a, b, (((0,), (0,)), ((), ())), preferred_element_type=F32)


def _rms(xf):
    r = lax.rsqrt(jnp.mean(xf * xf, axis=-1, keepdims=True) + EPS)
    return xf * r, r


def _rms_bwd(dn, xn, r):
    return r * (dn - xn * jnp.mean(dn * xn, axis=-1, keepdims=True))


def _silu(z):
    s = jax.nn.sigmoid(z)
    return z * s, s * (1.0 + z * (1.0 - s))


def _my_index():
    return 4 * lax.axis_index("x") + 2 * lax.axis_index("y") + lax.axis_index("c")


def _bias_table(rb_ref, bucket_ref, win_ref, out_ref):
    bk = jnp.where(win_ref[...] != 0, bucket_ref[...], -1)
    has_prev = lax.broadcasted_iota(jnp.int32, bk.shape, 0) >= BLOCK
    for h in range(N_Q_HEADS):
        acc = jnp.full(bk.shape, NEG_INF, F32)
        for b in range(N_BUCKETS):
            acc = jnp.where(bk == b, rb_ref[h, b], acc)
        cols = slice((h % 2) * BLOCK, (h % 2 + 1) * BLOCK)
        out_ref[1, h // 2, :, cols] = acc
        out_ref[0, h // 2, :, cols] = jnp.where(has_prev, acc, NEG_INF)


def _all_gather(shards, small_rows, casts, rel_bias_t, bucket_t, in_window_t):
    ns, nc, n = len(small_rows), len(casts), len(shards) + 1
    small_shape = (8, small_rows[0][1].shape[-1])
    shapes = [s.shape for s in shards] + [small_shape]
    pieces = [(t, r0, min(GATHER_PIECE_ROWS, shape[0] - r0))
              for t, shape in enumerate(shapes) for r0 in range(0, shape[0], GATHER_PIECE_ROWS)]

    def body(*refs):
        refs = list(refs)
        take = lambda k: [refs.pop(0) for _ in range(k)]
        ins, small_refs, cast_refs, (rb_ref, bucket_ref, win_ref) = take(n - 1), take(ns), take(nc), take(3)
        outs, cast_outs, (bias_ref, send_sems, recv_sems) = take(n), take(nc), take(3)
        x, y, c = lax.axis_index("x"), lax.axis_index("y"), lax.axis_index("c")
        me, sibling = (x, y, c), (x, y, 1 - c)
        x_nbr, y_nbr, diagonal = (1 - x, y), (x, 1 - y), (1 - x, 1 - y)
        south = c == 0
        relayed = (jnp.where(south, 1 - x, x), jnp.where(south, y, 1 - y))
        relay_to = (jnp.where(south, x, 1 - x), jnp.where(south, 1 - y, y))

        def copy(u, k, block, to):
            t, r0, nrows = pieces[u]
            rows = outs[t].at[4 * block[0] + 2 * block[1] + block[2], pl.ds(r0, nrows)]
            return pltpu.make_async_remote_copy(
                src_ref=rows, dst_ref=rows, send_sem=send_sems.at[u, k], recv_sem=recv_sems.at[u, k],
                device_id=to, device_id_type=MESH)

        mine = pl.ds(_my_index(), 1)
        for t in range(n - 1):
            outs[t][mine] = ins[t][...].astype(BF16)[None]
        outs[n - 1][mine] = jnp.zeros((1,) + small_shape, F32)
        for (row, _), ref in zip(small_rows, small_refs):
            if len(ref.shape) == 3:
                for j in range(ref.shape[0]):
                    outs[n - 1][mine, row + j:row + j + 1, :] = ref[j][None]
            else:
                outs[n - 1][mine, row:row + ref.shape[0], :] = ref[...][None]
        started = []

        def start(cp):
            cp.start()
            started.append(cp)

        units = range(len(pieces))
        for u in units:
            start(copy(u, 0, me, sibling))
            start(copy(u, 1, me, (*x_nbr, c)))
            start(copy(u, 2, me, (*y_nbr, c)))
        for src, dst in zip(cast_refs, cast_outs):
            dst[...] = src[...].astype(BF16)
        _bias_table(rb_ref, bucket_ref, win_ref, bias_ref)
        for u in units:
            for k, chip in ((1, x_nbr), (2, y_nbr)):
                copy(u, k, (*chip, c), me).wait_recv()
                start(copy(u, 3 + k, (*chip, c), sibling))
            start(copy(u, 3, (*relayed, c), (*relay_to, c)))
        for u in units:
            copy(u, 3, (*diagonal, c), me).wait_recv()
            start(copy(u, 6, (*diagonal, c), sibling))
        for u in units:
            copy(u, 0, sibling, me).wait_recv()
        for k, chip in ((4, x_nbr), (5, y_nbr), (6, diagonal)):
            for u in units:
                copy(u, k, (*chip, 1 - c), me).wait_recv()
        for cp in started:
            cp.wait_send()

    vmem = pl.BlockSpec(memory_space=pltpu.VMEM)
    outs = pl.pallas_call(
        body,
        name="gather_weights",
        out_shape=[jax.ShapeDtypeStruct((N_DEV,) + s.shape, BF16) for s in shards]
        + [jax.ShapeDtypeStruct((N_DEV,) + small_shape, F32)]
        + [jax.ShapeDtypeStruct(a.shape, BF16) for a in casts]
        + [jax.ShapeDtypeStruct((2, N_PAIRS, BAND, 2 * BLOCK), F32)],
        in_specs=[vmem] * (n - 1 + ns + nc) + [pl.BlockSpec(memory_space=pltpu.SMEM), vmem, vmem],
        out_specs=[vmem] * (n + nc + 1),
        scratch_shapes=[pltpu.SemaphoreType.DMA((len(pieces), 7)), pltpu.SemaphoreType.DMA((len(pieces), 7))],
        compiler_params=_params(),
    )(*shards, *[a for _, a in small_rows], *casts, rel_bias_t, bucket_t, in_window_t)
    return outs[:n - 1], outs[n - 1], outs[n:n + nc], outs[n + nc]


def _peer(k):
    x, y, c = lax.axis_index("x"), lax.axis_index("y"), lax.axis_index("c")
    px = 1 - x if k & 4 else x
    py = 1 - y if k & 2 else y
    pc = 1 - c if k & 1 else c
    return (px, py, pc), 4 * px + 2 * py + pc


def _exchange(srcs, dsts, send_sems, recv_sems, local_sems, scatter):
    me = _my_index()
    sends, arrivals = [], []
    for k in range(1, N_DEV):
        peer, pidx = _peer(k)
        for t, (src, dst) in enumerate(zip(srcs, dsts)):
            mine = src.at[pidx] if scatter else src
            sems = dict(send_sem=send_sems.at[t, k - 1], recv_sem=recv_sems.at[t, k - 1], device_id=peer, device_id_type=MESH)
            sends.append(pltpu.make_async_remote_copy(src_ref=mine, dst_ref=dst.at[me], **sems))
            arrivals.append(pltpu.make_async_remote_copy(src_ref=mine, dst_ref=dst.at[pidx], **sems))
    local = [pltpu.make_async_copy(src.at[me] if scatter else src, dst.at[me], local_sems.at[t])
             for t, (src, dst) in enumerate(zip(srcs, dsts))]
    return sends, arrivals, local


def _exchange_start(*args):
    sends, _, local = _exchange(*args)
    for cp in sends + local:
        cp.start()


def _exchange_wait(*args):
    sends, arrivals, local = _exchange(*args)
    for cp in arrivals:
        cp.wait_recv()
    for cp in sends:
        cp.wait_send()
    for cp in local:
        cp.wait()


def _exchange_sems(n):
    if not n:
        return []
    return [pltpu.SemaphoreType.DMA((n, N_DEV - 1)), pltpu.SemaphoreType.DMA((n, N_DEV - 1)), pltpu.SemaphoreType.DMA((n,))]


HBM_SPEC = pl.BlockSpec(memory_space=pl.ANY)


def _sum_slots(recv_ref, out_ref):
    rows = out_ref.shape[0]
    chunk = min(rows, 128)

    def add(i, carry):
        r0 = pl.multiple_of(i * chunk, chunk)
        acc = recv_ref[0, pl.ds(r0, chunk), :].astype(F32)
        for dev in range(1, N_DEV):
            acc = acc + recv_ref[dev, pl.ds(r0, chunk), :].astype(F32)
        out_ref[pl.ds(r0, chunk), :] = acc
        return carry

    lax.fori_loop(0, rows // chunk, add, 0)


N_CHIPS = N_DEV // 2


def _rows_loop(rows, fn):
    chunk = min(rows, 128)

    def step(i, carry):
        fn(pl.ds(pl.multiple_of(i * chunk, chunk), chunk))
        return carry

    lax.fori_loop(0, rows // chunk, step, 0)


def _chip_reduce(g_ref, out_ref, sib_ref, land_ref, send_ref, sems, swap_src=None):
    sib_send, sib_recv, ici_send, ici_recv = sems
    x, y, c = lax.axis_index("x"), lax.axis_index("y"), lax.axis_index("c")
    south = c == 0
    near =(jnp.where(south, 1 - x, x), jnp.where(south, y, 1 - y))
    far = (jnp.where(south, x, 1 - x), jnp.where(south, 1 - y, y))
    diagonal = (1 - x, 1 - y)
    rows = out_ref.shape[0]
    direct, fold, folded = 0, 1, 2

    def to_sibling(t):
        src = g_ref if swap_src is None else swap_src
        return pltpu.make_async_remote_copy(
            src_ref=src.at[2 * t + 1 - c], dst_ref=sib_ref.at[t], send_sem=sib_send.at[t], recv_sem=sib_recv.at[t],
            device_id=(x, y, 1 - c), device_id_type=MESH)

    def ici(role, chip):
        return pltpu.make_async_remote_copy(
            src_ref=send_ref.at[role], dst_ref=land_ref.at[role], send_sem=ici_send.at[role],
            recv_sem=ici_recv.at[role], device_id=(*chip, c), device_id_type=MESH)

    def pair_sum(chip, r):
        t = 2 * chip[0] + chip[1]
        return g_ref[2 * t + c, r, :].astype(F32) + sib_ref[t, r, :].astype(F32)

    def swap():
        for t in range(N_CHIPS):
            to_sibling(t).start()

    def send():
        for t in range(N_CHIPS):
            to_sibling(t).wait_recv()
        for role, chip in ((fold, diagonal), (direct, near)):
            def fill(r, role=role, chip=chip):
                send_ref[role, r, :] = pair_sum(chip, r).astype(BF16)

            _rows_loop(rows, fill)
            ici(role, near).start()

    def forward():
        ici(fold, near).wait_recv()

        def fill(r):
            send_ref[folded, r, :] = (pair_sum(far, r) + land_ref[fold, r, :].astype(F32)).astype(BF16)

        _rows_loop(rows, fill)
        ici(folded, far).start()

    def finish():
        ici(direct, near).wait_recv()
        ici(folded, far).wait_recv()

        def total(r):
            mine = pair_sum((x, y), r)
            out_ref[r, :] = mine + land_ref[direct, r, :].astype(F32) + land_ref[folded, r, :].astype(F32)

        _rows_loop(rows, total)
        for t in range(N_CHIPS):
            to_sibling(t).wait_send()
        for role, chip in ((direct, near), (fold, near), (folded, far)):
            ici(role, chip).wait_send()

    return swap, send, forward, finish


def _chip_reduce_scratch(slot):
    return [pltpu.VMEM((N_CHIPS,) + slot, BF16), pltpu.VMEM((3,) + slot, BF16), pltpu.VMEM((3,) + slot, BF16),
            pltpu.SemaphoreType.DMA((N_CHIPS,)), pltpu.SemaphoreType.DMA((N_CHIPS,)),
            pltpu.SemaphoreType.DMA((3,)), pltpu.SemaphoreType.DMA((3,))]


def _reduce_exchange(part, landed, smalls):
    nl, ng = len(landed), len(smalls)
    n_out = 1 + nl + ng

    def body(*refs):
        p_in, l_in, s_in = refs[0], refs[1:1 + nl], refs[1 + nl:n_out]
        p_out, l_out, s_out = refs[n_out], refs[n_out + 1:n_out + 1 + nl], refs[n_out + 1 + nl:2 * n_out]
        scratch = refs[2 * n_out:]
        s_recv, (sib_ref, chip_ref, send_ref), sems = scratch[:ng], scratch[ng:ng + 3], scratch[ng + 3:]
        swap, send, forward, finish = _chip_reduce(p_in, p_out, sib_ref, chip_ref, send_ref, sems[:4])
        swap()
        _exchange_start(s_in, s_recv, *sems[4:], False)
        send()
        for t in range(nl):
            _sum_slots(l_in[t], l_out[t])
        forward()
        finish()
        _exchange_wait(s_in, s_recv, *sems[4:], False)
        for t in range(ng):
            acc = s_recv[t][0]
            for dev in range(1, N_DEV):
                acc = acc + s_recv[t][dev]
            s_out[t][...] = acc

    vmem = pl.BlockSpec(memory_space=pltpu.VMEM)
    slot = part.shape[1:]
    outs = pl.pallas_call(
        body,
        name="reduce_grads",
        out_shape=[jax.ShapeDtypeStruct(p.shape[1:], F32) for p in [part] + landed]
        + [jax.ShapeDtypeStruct(s.shape, F32) for s in smalls],
        in_specs=[vmem] * n_out,
        out_specs=[vmem] * n_out,
        scratch_shapes=[pltpu.VMEM((N_DEV,) + s.shape, F32) for s in smalls] + _chip_reduce_scratch(slot)
        + _exchange_sems(ng),
        compiler_params=_params(vmem_mib=56),
    )(part, *landed, *smalls)
    return outs[0], outs[1:1 + nl], outs[1 + nl:]


def _layer_a_fwd(x2, sm, win_g, wout, later, ts):
    seq, d = x2.shape
    width = wout.shape[0]
    half = win_g.shape[2]
    n_half = width // half
    nl = len(later)
    nt = seq // ts

    def body(x_ref, sm_ref, win_ref, wout_ref, *refs):
        shard_refs, refs = refs[:nl], refs[nl:]
        h1_ref, n1_ref, proj_ref, conv_ref, y_ref, ya_ref = refs[:6]
        gathered_refs, (vprev_ref, *sems) = refs[6:6 + nl], refs[6 + nl:]

        @pl.when(pl.program_id(0) == 0)
        def _():
            vprev_ref[...] = jnp.zeros_like(vprev_ref)
            _exchange_start(shard_refs, gathered_refs, *sems, False)

        @pl.when(pl.program_id(0) == nt - 1)
        def _():
            _exchange_wait(shard_refs, gathered_refs, *sems, False)

        xf = x_ref[...]
        xn, _ = _rms(xf)
        n1 = (xn * sm_ref[0:1, :]).astype(BF16)
        n1_ref[...] = n1
        row = lax.broadcasted_iota(jnp.int32, (ts, half), 0)
        ya = jnp.zeros((ts, d), F32)
        for hh in range(n_half):
            cols = slice(hh * half, (hh + 1) * half)
            parts = []
            for part in range(4):
                j = part * n_half + hh
                pj = _dot(n1, win_ref[j])
                proj_ref[:, j * half:(j + 1) * half] = pj.astype(BF16)
                parts.append(pj)
            b, c, u, z = parts
            v = c * u
            last1, last2 = vprev_ref[7:8, cols], vprev_ref[6:7, cols]
            v1 = jnp.where(row == 0, last1, pltpu.roll(v, 1, 0))
            v2 = jnp.where(row == 0, last2, jnp.where(row == 1, last1, pltpu.roll(v, 2, 0)))
            vprev_ref[:, cols] = v[ts - 8:ts, :]
            conv = sm_ref[1:2, cols] * v2 + sm_ref[2:3, cols] * v1 + sm_ref[3:4, cols] * v
            conv_ref[:, cols] = conv.astype(BF16)
            yh = (b * conv * _silu(z)[0]).astype(BF16)
            y_ref[:, cols] = yh
            ya = ya + _dot(yh, wout_ref[cols, :])
        ya_ref[...] = ya
        h1_ref[...] = xf + _rms(ya)[0] * sm_ref[4:5, :]

    outs = pl.pallas_call(
        body,
        name="layer_a_fwd",
        grid=(nt,),
        in_specs=[_rows(ts, d), _full(sm.shape), _full(win_g.shape), _full(wout.shape)] + [HBM_SPEC] * nl,
        out_specs=[_rows(ts, d), _rows(ts, d), _rows(ts, 4 * width), _rows(ts, width), _rows(ts, width), _rows(ts, d)]
        + [HBM_SPEC] * nl,
        out_shape=[
            jax.ShapeDtypeStruct((seq, d), F32),
            jax.ShapeDtypeStruct((seq, d), BF16),
            jax.ShapeDtypeStruct((seq, 4 * width), BF16),
            jax.ShapeDtypeStruct((seq, width), BF16),
            jax.ShapeDtypeStruct((seq, width), BF16),
            jax.ShapeDtypeStruct((seq, d), F32),
        ] + [jax.ShapeDtypeStruct((N_DEV,) + s.shape, s.dtype) for s in later],
        scratch_shapes=[pltpu.VMEM((8, width), F32)] + _exchange_sems(nl),
        compiler_params=_params(("arbitrary",), 56),
    )(x2, sm, win_g, wout, *later)
    return outs[:6], outs[6:]


Q_BLOCKS = 4
ATTN_BWD_Q_BLOCKS = 8
ATTN_BWD_LAGS = (2, 4)
ATTN_FWD_LAGS = (2, 4)


def _banded_tiles(kvp_ref, kvc_ref, n_blocks):
    tile = kvc_ref[...].astype(F32)
    blocks = [kvp_ref[...].astype(F32)] + [tile[u * BLOCK:(u + 1) * BLOCK] for u in range(n_blocks)]
    return [_banded_kv(blocks[u], blocks[u + 1]) for u in range(n_blocks)]


def _bias_of(bias_ref, i, u, m):
    return bias_ref[jnp.minimum(i, 1) if u == 0 else 1, m]


def _banded_kv(kvp, kvc):
    kw = N_KV_HEADS * HEAD_DIM
    out = []
    for full in (jnp.concatenate([kvp[:, :kw], kvc[:, :kw]], axis=0), jnp.concatenate([kvp[:, kw:], kvc[:, kw:]], axis=0)):
        lo = lax.broadcasted_iota(jnp.int32, full.shape, 1) < HEAD_DIM
        rolled = pltpu.roll(full, HEAD_DIM, 1)
        x2 = [jnp.where(lo, full, rolled).astype(BF16), jnp.where(lo, rolled, full).astype(BF16)]
        ft = full.T
        x2t = [jnp.concatenate([ft[kh * HEAD_DIM:(kh + 1) * HEAD_DIM]] * 2, axis=0).astype(BF16) for kh in range(N_KV_HEADS)]
        out += [x2, x2t]
    return out


def _pair_rows(ref, rows, m, scale=None):
    both = ref[rows, m * LANES:(m + 1) * LANES].astype(F32)
    if scale is not None:
        both = both * scale
    lo = lax.broadcasted_iota(jnp.int32, both.shape, 1) < HEAD_DIM
    zero = jnp.zeros_like(both)
    return jnp.concatenate([jnp.where(lo, both, zero), jnp.where(lo, zero, both)], axis=0).astype(BF16)


def _pair_cols(res_t):
    top = lax.broadcasted_iota(jnp.int32, (LANES, BLOCK), 0) < HEAD_DIM
    return jnp.where(top, res_t[:, :BLOCK], res_t[:, BLOCK:]).T


def _sink_row(sink_ref, m):
    first = lax.broadcasted_iota(jnp.int32, (1, 2 * BLOCK), 1) < BLOCK
    return jnp.where(first, sink_ref[0, 2 * m], sink_ref[0, 2 * m + 1])


def _softmax_t(logits, sink):
    mx =jnp.maximum(jnp.max(logits, axis=0, keepdims=True), sink)
    p = jnp.exp(logits - mx)
    sink_p = jnp.exp(sink - mx)
    inv = 1.0 / (jnp.sum(p, axis=0, keepdims=True) + sink_p)
    return p * inv, sink_p * inv


def _layer_b_fwd(h1, target, kvn, bpre, wkv, wbin_g, biasm, sinks, wbout, bpost):
    seq, d = h1.shape
    kvw = wkv.shape[1]
    cw = wbin_g.shape[2]
    aw = N_Q_HEADS * HEAD_DIM
    per = aw // cw
    tile = Q_BLOCKS * BLOCK

    def body(sink_ref, h1_ref, tgt_ref, kvn_ref, bpre_ref, wkv_ref, wbin_ref, bias_ref, w_ref, g_ref,
             n3_ref, n4_ref, kvc_ref, q_ref, o_ref, dh2_ref, dyb_ref, dattn_ref, dz2_ref, acc_ref,
             attn_ref, z2_ref, kvp_ref):
        i = pl.program_id(0)

        @pl.when(i == 0)
        def _():
            acc_ref[...] = jnp.zeros_like(acc_ref)
            kvp_ref[...] = jnp.zeros_like(kvp_ref)

        hn, _ = _rms(h1_ref[...])
        n3 = (hn * kvn_ref[...]).astype(BF16)
        n4 = (hn * bpre_ref[...]).astype(BF16)
        n3_ref[...] = n3
        n4_ref[...] = n4
        kvc_ref[...] = _dot(n3, wkv_ref[...]).astype(BF16)
        for j in range(N_DEV):
            pj = _dot(n4, wbin_ref[j])
            if j < per:
                q_ref[:, j * cw:(j + 1) * cw] = pj.astype(BF16)
            else:
                z2_ref[:, (j - per) * cw:(j - per + 1) * cw] = pj

        banded = _banded_tiles(kvp_ref, kvc_ref, Q_BLOCKS)
        kvp_ref[...] = kvc_ref[tile - BLOCK:tile, :]
        units = [(u, m) for u in range(Q_BLOCKS) for m in range(N_PAIRS)]
        kv_of = lambda m: (2 * m) // GROUP
        logits, probs = {}, {}
        lag_b, lag_c = ATTN_FWD_LAGS
        for step in range(len(units) + lag_c):
            if step < len(units):
                u, m = units[step]
                qpair = _pair_rows(q_ref, slice(u * BLOCK, (u + 1) * BLOCK), m, SCALE)
                logits[step] = _dot_nt(banded[u][0][kv_of(m)], qpair) + _bias_of(bias_ref, i, u, m)
            if 0 <= step - lag_b < len(units):
                u, m = units[step - lag_b]
                probs[step - lag_b] = _softmax_t(logits.pop(step - lag_b), _sink_row(sink_ref, m))[0].astype(BF16)
            if 0 <= step - lag_c < len(units):
                u, m = units[step - lag_c]
                out_t = _dot(banded[u][3][kv_of(m)], probs.pop(step - lag_c))
                attn_ref[u * BLOCK:(u + 1) * BLOCK, m * LANES:(m + 1) * LANES] = _pair_cols(out_t)
        attn = attn_ref[...]
        sz, dsz = _silu(z2_ref[...])
        o = (attn * sz).astype(BF16)
        o_ref[...] = o

        w = w_ref[...]
        yb = _dot(o, w)
        ybn, r = _rms(yb)
        g = g_ref[...]
        diff = h1_ref[...] + ybn * g - tgt_ref[...]
        dh2 = diff * (1.0 / d)
        dh2_ref[...] = dh2
        acc_ref[0:1, :] += jnp.sum(dh2 * ybn, axis=0, keepdims=True)
        tok = jnp.mean(diff * diff, axis=-1, keepdims=True)
        acc_ref[1:2, :] += 0.5 * jnp.sum(tok, axis=0, keepdims=True)
        dyb = _rms_bwd(dh2 * g, ybn, r).astype(BF16)
        dyb_ref[...] = dyb
        do = _dot_nt(dyb, w)
        dattn_ref[...] = (do * sz).astype(BF16)
        dz2_ref[...] = (do * attn * dsz).astype(BF16)

    blk = lambda w: pl.BlockSpec((tile, w), lambda i: (i, 0))
    return pl.pallas_call(
        body,
        name="layer_b_fwd",
        grid=(seq // tile,),
        in_specs=[
            pl.BlockSpec(memory_space=pltpu.SMEM),
            blk(d),
            blk(d),
            _full(kvn.shape),
            _full(bpre.shape),
            _full(wkv.shape),
            _full(wbin_g.shape),
            _full(biasm.shape),
            _full(wbout.shape),
            _full(bpost.shape),
        ],
        out_specs=[blk(d), blk(d), blk(kvw), blk(aw), blk(aw), blk(d), blk(d), blk(aw), blk(aw), _resident((8, d))],
        out_shape=[
            jax.ShapeDtypeStruct((seq, d), BF16),
            jax.ShapeDtypeStruct((seq, d), BF16),
            jax.ShapeDtypeStruct((seq, kvw), BF16),
            jax.ShapeDtypeStruct((seq, aw), BF16),
            jax.ShapeDtypeStruct((seq, aw), BF16),
            jax.ShapeDtypeStruct((seq, d), F32),
            jax.ShapeDtypeStruct((seq, d), BF16),
            jax.ShapeDtypeStruct((seq, aw), BF16),
            jax.ShapeDtypeStruct((seq, aw), BF16),
            jax.ShapeDtypeStruct((8, d), F32),
        ],
        scratch_shapes=[pltpu.VMEM((tile, aw), F32), pltpu.VMEM((tile, aw), F32), pltpu.VMEM((BLOCK, kvw), BF16)],
        compiler_params=_params(("arbitrary",), 56),
    )(sinks, h1, target, kvn, bpre, wkv, wbin_g, biasm, wbout, bpost)


def _attn_bwd(q, kv, dattn, biasm, sinks, ready):
    seq, aw = q.shape
    kvw = kv.shape[1]
    kw = N_KV_HEADS * HEAD_DIM
    nb = seq // BLOCK
    pairs_per_kv = N_PAIRS // N_KV_HEADS
    nr = len(ready)

    qb = min(ATTN_BWD_Q_BLOCKS, nb)
    tile = qb * BLOCK
    nsteps = seq // tile
    held = (qb - 1) * BLOCK

    def body(sink_ref, q_ref, kvc_ref, kvp_ref, da_ref, bias_ref, *refs):
        ready_refs, (dq_ref, dkv_ref, dssum_ref, dsink_ref) = refs[:nr], refs[nr:nr + 4]
        landed_refs, scratch = refs[nr + 4:2 * nr + 4], refs[2 * nr + 4:]
        carry_ref, done_ref, qs_ref, dos_ref, dst_ref, pt_ref, *sems = scratch
        i = pl.program_id(0)

        @pl.when(i == 0)
        def _():
            dssum_ref[...] = jnp.zeros_like(dssum_ref)
            dsink_ref[...] = jnp.zeros_like(dsink_ref)
            carry_ref[...] = jnp.zeros_like(carry_ref)
            done_ref[...] = jnp.zeros_like(done_ref)
            if nr:
                _exchange_start(ready_refs, landed_refs, *sems, True)

        if nr:
            @pl.when(i == nsteps)
            def _():
                _exchange_wait(ready_refs, landed_refs, *sems, True)

        @pl.when(i < nsteps)
        def _():
            lo = lax.broadcasted_iota(jnp.int32, (BAND, LANES), 1) < HEAD_DIM
            head_lane = lax.broadcasted_iota(jnp.int32, (1, LANES), 1)
            banded = _banded_tiles(kvp_ref, kvc_ref, qb)
            units = [(u, m) for u in range(qb) for m in range(N_PAIRS)]
            dsink = jnp.zeros((1, LANES), F32)
            folded = {}
            logits, dps, dsbs = {}, {}, {}
            lag_b, lag_c = ATTN_BWD_LAGS
            for step in range(len(units) + lag_c):
                if step < len(units):
                    u, m = units[step]
                    kh, rows = m // pairs_per_kv, slice((m % pairs_per_kv) * BAND, (m % pairs_per_kv + 1) * BAND)
                    qrows = slice(u * BLOCK, (u + 1) * BLOCK)
                    qpair = _pair_rows(q_ref, qrows, m, SCALE)
                    dopair = _pair_rows(da_ref, qrows, m)
                    qs_ref[u, kh, rows, :] = qpair
                    dos_ref[u, kh, rows, :] = dopair
                    logits[step] = _dot_nt(banded[u][0][kh], qpair) + _bias_of(bias_ref, i, u, m)
                    dps[step] = _dot_nt(banded[u][2][kh], dopair)
                if 0 <= step - lag_b < len(units):
                    u, m = units[step - lag_b]
                    kh, rows = m // pairs_per_kv, slice((m % pairs_per_kv) * BAND, (m % pairs_per_kv + 1) * BAND)
                    pn, sink_p = _softmax_t(logits.pop(step - lag_b), _sink_row(sink_ref, m))
                    dp = dps.pop(step - lag_b)
                    delta = jnp.sum(pn * dp, axis=0, keepdims=True)
                    ds = pn * (dp - delta)
                    dssum_ref[m] += ds
                    sink_term = sink_p * delta
                    for e in range(2):
                        total = jnp.sum(sink_term[:, e * BLOCK:(e + 1) * BLOCK], axis=1, keepdims=True)
                        dsink = dsink - jnp.where(head_lane == 2 * m + e, total, 0.0)
                    dsbs[step - lag_b] = ds.astype(BF16)
                    dst_ref[u, kh, :, rows] = dsbs[step - lag_b]
                    pt_ref[u, kh, :, rows] = pn.astype(BF16)
                if 0 <= step - lag_c < len(units):
                    u, m = units[step - lag_c]
                    kh = m // pairs_per_kv
                    dq_t = _dot(banded[u][1][kh], dsbs.pop(step - lag_c))
                    dq_ref[u * BLOCK:(u + 1) * BLOCK, m * LANES:(m + 1) * LANES] = (_pair_cols(dq_t) * SCALE).astype(BF16)
                    if m % pairs_per_kv == pairs_per_kv - 1:
                        for name, lhs_ref, rhs_ref in (("k", dst_ref, qs_ref), ("v", pt_ref, dos_ref)):
                            acc = _dot(lhs_ref[u, kh], rhs_ref[u, kh])
                            folded[u, kh, name] = acc + pltpu.roll(acc, HEAD_DIM, 1)
            dsink_ref[0:1, :] += dsink
            dkv = [jnp.concatenate([jnp.where(lo, folded[u, 0, n], folded[u, 1, n]) for n in ("k", "v")], axis=1)
                   for u in range(qb)]

            @pl.when(i > 0)
            def _():
                if held:
                    dkv_ref[:held, :] = done_ref[...].astype(BF16)
                dkv_ref[held:, :] = (carry_ref[...] + dkv[0][:BLOCK]).astype(BF16)

            for u in range(qb - 1):
                done_ref[u * BLOCK:(u + 1) * BLOCK, :] = dkv[u][BLOCK:] + dkv[u + 1][:BLOCK]
            carry_ref[...] = dkv[qb - 1][BLOCK:]

        @pl.when(i == nsteps)
        def _():
            if held:
                dkv_ref[:held, :] = done_ref[...].astype(BF16)
            dkv_ref[held:, :] = carry_ref[...].astype(BF16)

    last = nsteps - 1
    blk = lambda w: pl.BlockSpec((tile, w), lambda i: (jnp.minimum(i, last), 0))
    outs = pl.pallas_call(
        body,
        name="attn_bwd",
        grid=(nsteps + 1,),
        in_specs=[
            pl.BlockSpec(memory_space=pltpu.SMEM),
            blk(aw),
            blk(kvw),
            pl.BlockSpec((BLOCK, kvw), lambda i: (jnp.clip(qb * i - 1, 0, nb - 1), 0)),
            blk(aw),
            _full(biasm.shape),
        ] + [HBM_SPEC] * nr,
        out_specs=[
            blk(aw),
            pl.BlockSpec((tile, kvw), lambda i: (jnp.maximum(i - 1, 0), 0)),
            _resident(biasm.shape[1:]),
            _resident((8, LANES)),
        ] + [HBM_SPEC] * nr,
        out_shape=[
            jax.ShapeDtypeStruct((seq, aw), BF16),
            jax.ShapeDtypeStruct((seq, kvw), BF16),
            jax.ShapeDtypeStruct(biasm.shape[1:], F32),
            jax.ShapeDtypeStruct((8, LANES), F32),
        ] + [jax.ShapeDtypeStruct(g.shape, g.dtype) for g in ready],
        scratch_shapes=[
            pltpu.VMEM((BLOCK, kvw), F32),
            pltpu.VMEM((max(held, 8), kvw), F32),
            pltpu.VMEM((qb, N_KV_HEADS, pairs_per_kv * BAND, LANES), BF16),
            pltpu.VMEM((qb, N_KV_HEADS, pairs_per_kv * BAND, LANES), BF16),
            pltpu.VMEM((qb, N_KV_HEADS, BAND, pairs_per_kv * BAND), BF16),
            pltpu.VMEM((qb, N_KV_HEADS, BAND, pairs_per_kv * BAND), BF16),
        ] + _exchange_sems(nr),
        compiler_params=_params(("arbitrary",), 48),
    )(sinks, q, kv, kv, dattn, biasm, *ready)
    return outs[:4], outs[4:]


def _relbias_grad(dssum2, bucket_row, chunk):
    heads, n = dssum2.shape

    def body(a_ref, bucket_ref, out_ref):
        @pl.when(pl.program_id(0) == 0)
        def _():
            out_ref[...] = jnp.zeros_like(out_ref)

        a = a_ref[...]
        hi = a.astype(BF16)
        lo = (a - hi.astype(F32)).astype(BF16)
        onehot_t = (lax.broadcasted_iota(jnp.int32, (LANES, chunk), 0) == bucket_ref[...]).astype(F32).astype(BF16)
        out_ref[...] += _dot_nt(hi, onehot_t) + _dot_nt(lo, onehot_t)

    return pl.pallas_call(
        body,
        name="relbias_grad",
        grid=(n // chunk,),
        in_specs=[pl.BlockSpec((heads, chunk), lambda i: (0, i)), pl.BlockSpec((1, chunk), lambda i: (0, i))],
        out_specs=_resident((heads, LANES)),
        out_shape=jax.ShapeDtypeStruct((heads, LANES), F32),
        compiler_params=_params(("arbitrary",), 32),
    )(dssum2, bucket_row)


def _layer_b_in_bwd(dh2, dq, dz2, dkv, h1, ya, wbin_g, wkv, kvn, bpre, sm, ready, ts):
    seq, d = h1.shape
    aw = dq.shape[1]
    kvw = dkv.shape[1]
    cw = wbin_g.shape[2]
    per = aw // cw

    nr = len(ready)
    nt = seq // ts

    def body(dh2_ref, dq_ref, dz2_ref, dkv_ref, h1_ref, ya_ref, wbin_ref, wkv_ref, kvn_ref, bpre_ref, sm_ref, *refs):
        ready_refs, (dh1_ref, dya_ref, acc_ref) = refs[:nr], refs[nr:nr + 3]
        landed_refs, sems = refs[nr + 3:2 * nr + 3], refs[2 * nr + 3:]

        @pl.when(pl.program_id(0) == 0)
        def _():
            acc_ref[...] = jnp.zeros_like(acc_ref)
            if nr:
                _exchange_start(ready_refs, landed_refs, *sems, True)

        if nr:
            @pl.when(pl.program_id(0) == nt - 1)
            def _():
                _exchange_wait(ready_refs, landed_refs, *sems, True)

        dn4 = jnp.zeros((ts, d), F32)
        for j in range(N_DEV):
            src = dq_ref if j < per else dz2_ref
            jj = j % per
            dn4 = dn4 + _dot_nt(src[:, jj * cw:(jj + 1) * cw], wbin_ref[j])
        dn3 = _dot_nt(dkv_ref[...], wkv_ref[...])
        hn, r = _rms(h1_ref[...])
        acc_ref[0:1, :] += jnp.sum(dn4 * hn, axis=0, keepdims=True)
        acc_ref[1:2, :] += jnp.sum(dn3 * hn, axis=0, keepdims=True)
        dh1 = dh2_ref[...] + _rms_bwd(dn4 * bpre_ref[...] + dn3 * kvn_ref[...], hn, r)
        dh1_ref[...] = dh1
        yan, r2 = _rms(ya_ref[...])
        acc_ref[2:3, :] += jnp.sum(dh1 * yan, axis=0, keepdims=True)
        dya_ref[...] = _rms_bwd(dh1 * sm_ref[4:5, :], yan, r2).astype(BF16)

    outs = pl.pallas_call(
        body,
        name="layer_b_in_bwd",
        grid=(nt,),
        in_specs=[_rows(ts, d), _rows(ts, aw), _rows(ts, aw), _rows(ts, kvw), _rows(ts, d), _rows(ts, d),
                  _full(wbin_g.shape), _full(wkv.shape), _full(kvn.shape), _full(bpre.shape), _full(sm.shape)]
        + [HBM_SPEC] * nr,
        out_specs=[_rows(ts, d), _rows(ts, d), _resident((8, d))] + [HBM_SPEC] * nr,
        out_shape=[jax.ShapeDtypeStruct((seq, d), F32), jax.ShapeDtypeStruct((seq, d), BF16),
                   jax.ShapeDtypeStruct((8, d), F32)] + [jax.ShapeDtypeStruct(g.shape, g.dtype) for g in ready],
        scratch_shapes=_exchange_sems(nr),
        compiler_params=_params(("arbitrary",), 48),
    )(dh2, dq, dz2, dkv, h1, ya, wbin_g, wkv, kvn, bpre, sm, *ready)
    return outs[:3], outs[3:]


def _layer_a_bwd(dya, proj, conv, dh1, x2, wout, win_g, sm, ts):
    seq, d = x2.shape
    width = wout.shape[0]
    half = win_g.shape[2]
    n_half = width // half
    nt = seq // ts

    def body(dya_ref, proj_ref, conv_ref, dh1_ref, x_ref, wout_ref, win_ref, sm_ref, dproj_ref, gx_ref, acc_ref,
             dnext_ref):
        @pl.when(pl.program_id(0) == 0)
        def _():
            acc_ref[...] = jnp.zeros_like(acc_ref)
            dnext_ref[...] = jnp.zeros_like(dnext_ref)

        dy = _dot_nt(dya_ref[...], wout_ref[...])
        row = lax.broadcasted_iota(jnp.int32, (ts, half), 0)
        dn1 = jnp.zeros((ts, d), F32)
        for hh in range(n_half):
            cols = slice(hh * half, (hh + 1) * half)
            b, c, u, z = [proj_ref[:, (part * n_half + hh) * half:(part * n_half + hh + 1) * half].astype(F32)
                          for part in range(4)]
            cv = conv_ref[:, cols].astype(F32)
            dyh = dy[:, cols]
            sz, dsz = _silu(z)
            dconv = dyh * b * sz
            grads = [dyh * cv * sz, None, None, dyh * b * cv * dsz]
            next0, next1 = dnext_ref[0:1, cols], dnext_ref[1:2, cols]
            dc1 = jnp.where(row == ts - 1, next0, pltpu.roll(dconv, ts - 1, 0))
            dc2 = jnp.where(row == ts - 1, next1, jnp.where(row == ts - 2, next0, pltpu.roll(dconv, ts - 2, 0)))
            dnext_ref[:, cols] = dconv[0:8, :]
            v = c * u
            acc_ref[1:2, cols] += jnp.sum(dc2 * v, axis=0, keepdims=True)
            acc_ref[2:3, cols] += jnp.sum(dc1 * v, axis=0, keepdims=True)
            acc_ref[3:4, cols] += jnp.sum(dconv * v, axis=0, keepdims=True)
            dv = sm_ref[3:4, cols] * dconv + sm_ref[2:3, cols] * dc1 + sm_ref[1:2, cols] * dc2
            grads[1] = dv * u
            grads[2] = dv * c
            for part in range(4):
                j = part * n_half + hh
                gj = grads[part].astype(BF16)
                dproj_ref[:, j * half:(j + 1) * half] = gj
                dn1 = dn1 + _dot_nt(gj, win_ref[j])
        xn, r = _rms(x_ref[...])
        acc_ref[0:1, :] += jnp.sum(dn1 * xn, axis=0, keepdims=True)
        gx_ref[...] = dh1_ref[...] + _rms_bwd(dn1 * sm_ref[0:1, :], xn, r)

    rev = lambda w: pl.BlockSpec((ts, w), lambda i: (nt - 1 - i, 0))
    return pl.pallas_call(
        body,
        name="layer_a_bwd",
        grid=(nt,),
        in_specs=[rev(d), rev(4 * width), rev(width), rev(d), rev(d), _full(wout.shape), _full(win_g.shape), _full(sm.shape)],
        out_specs=[rev(4 * width), rev(d), _resident((8, d))],
        out_shape=[jax.ShapeDtypeStruct((seq, 4 * width), BF16), jax.ShapeDtypeStruct((seq, d), F32),
                   jax.ShapeDtypeStruct((8, d), F32)],
        scratch_shapes=[pltpu.VMEM((8, width), F32)],
        compiler_params=_params(("arbitrary",), 56),
    )(dya, proj, conv, dh1, x2, wout, win_g, sm)


def _wgrad(a, bs, n_slots, ts, name, ready=(), block_cols=1024):
    nr = len(ready)
    seq, k = a.shape
    nb_in = len(bs)
    n_each = bs[0].shape[1]
    n = nb_in * n_each
    bn = min(n_each, block_cols)
    per_in = n_each // bn
    n_blocks = nb_in * per_in
    ns = seq // ts

    def b_spec(idx):
        def index(j, s):
            mine = j // per_in == idx
            row = jnp.where(mine, s, jnp.where(j // per_in > idx, ns - 1, 0))
            return (row, jnp.where(mine, j % per_in, jnp.where(j // per_in > idx, per_in - 1, 0)))
        return pl.BlockSpec((ts, bn), index)

    if n_slots:
        sw = n // n_slots
        spb = bn // sw
        out_shape = jax.ShapeDtypeStruct((n_slots, k, sw), BF16)
        out_spec = pl.BlockSpec((spb, k, sw), lambda j, s: (j, 0, 0))
    else:
        out_shape = jax.ShapeDtypeStruct((k, n), BF16)
        out_spec = pl.BlockSpec((k, bn), lambda j, s: (0, j))

    def body(a_ref, *refs):
        b_refs, ready_refs, o_ref = refs[:nb_in], refs[nb_in:nb_in + nr], refs[nb_in + nr]
        landed_refs, (acc_ref, *sems) = refs[nb_in + nr + 1:nb_in + 2 * nr + 1], refs[nb_in + 2 * nr + 1:]
        j, s = pl.program_id(0), pl.program_id(1)

        if nr:
            @pl.when(jnp.logical_and(j == 0, s == 0))
            def _():
                _exchange_start(ready_refs, landed_refs, *sems, True)

            @pl.when(jnp.logical_and(j == n_blocks - 1, s == ns - 1))
            def _():
                _exchange_wait(ready_refs, landed_refs, *sems, True)

        @pl.when(s == 0)
        def _():
            acc_ref[...] = jnp.zeros_like(acc_ref)

        for idx in range(nb_in):
            @pl.when(j // per_in == idx)
            def _(idx=idx):
                acc_ref[...] += _dot_tn(a_ref[...], b_refs[idx][...])

        @pl.when(s == ns - 1)
        def _():
            if n_slots:
                for e in range(spb):
                    o_ref[e] = acc_ref[:, e * sw:(e + 1) * sw].astype(BF16)
            else:
                o_ref[...] = acc_ref[...].astype(BF16)

    outs = pl.pallas_call(
        body,
        name=name,
        grid=(n_blocks, ns),
        in_specs=[pl.BlockSpec((ts, k), lambda j, s: (s, 0))] + [b_spec(idx) for idx in range(nb_in)] + [HBM_SPEC] * nr,
        out_specs=[out_spec] + [HBM_SPEC] * nr,
        out_shape=[out_shape] + [jax.ShapeDtypeStruct(g.shape, g.dtype) for g in ready],
        scratch_shapes=[pltpu.VMEM((k, bn), F32)] + (_exchange_sems(nr) if nr else []),
        compiler_params=_params(("arbitrary", "arbitrary"), 48),
    )(a, *bs, *ready)
    return (outs[0], outs[1:]) if nr else outs[0]


def _wgrad_tail(pairs, part, landed, ts):
    n_tasks = len(pairs)
    assert n_tasks == 2
    nl = len(landed)
    seq, k = pairs[0][0].shape
    n = pairs[0][1].shape[1]
    ns = seq // ts
    total = n_tasks * ns
    per = k // N_DEV
    n_red = len(_chip_reduce_scratch((per, n)))

    def spec(t, width):
        return pl.BlockSpec((ts, width), lambda j, s: (jnp.where(j == t, s, jnp.where(j > t, ns - 1, 0)), 0))

    def body(*refs):
        ab_refs, part_hbm = refs[:2 * n_tasks], refs[2 * n_tasks]
        landed_hbm, refs = refs[2 * n_tasks + 1:2 * n_tasks + 1 + nl], refs[2 * n_tasks + 1 + nl:]
        o_ref, red_ref, early_ref = refs[:3]
        summed_refs, (acc_ref, first_ref, part_ref, *scratch) = refs[3:3 + nl], refs[3 + nl:]
        landed_refs, load_sems, scratch = scratch[:nl], scratch[nl], scratch[nl + 1:]
        j, s = pl.program_id(0), pl.program_id(1)
        flat = j * ns + s
        swap, send, forward, finish = _chip_reduce(part_ref, red_ref, *scratch[:3], scratch[3:n_red], part_hbm)
        swap_first, send_first, forward_first, finish_first = _chip_reduce(
            first_ref, early_ref, *scratch[n_red:n_red + 3], scratch[n_red + 3:])
        loads = [pltpu.make_async_copy(src, dst, load_sems.at[i])
                 for i, (src, dst) in enumerate(zip([part_hbm, *landed_hbm], [part_ref, *landed_refs]))]

        @pl.when(flat == 0)
        def _():
            swap()
            for load in loads:
                load.start()

        @pl.when(flat == min(1, total - 1))
        def _():
            loads[0].wait()
            send()

        @pl.when(flat == min(total // 2 + 1, total - 1))
        def _():
            forward()
            for t in range(nl):
                loads[1 + t].wait()
                _sum_slots(landed_refs[t], summed_refs[t])

        @pl.when(flat == ns)
        def _():
            send_first()

        @pl.when(flat == min(ns + ns // 2, total - 1))
        def _():
            forward_first()

        @pl.when(s == 0)
        def _():
            acc_ref[...] = jnp.zeros_like(acc_ref)

        for t in range(n_tasks):
            @pl.when(j == t)
            def _(t=t):
                acc_ref[...] += _dot_tn(ab_refs[2 * t][...], ab_refs[2 * t + 1][...])

        @pl.when(flat == ns - 1)
        def _():
            for dev in range(N_DEV):
                first_ref[dev] = acc_ref[dev * per:(dev + 1) * per, :].astype(BF16)
            swap_first()

        @pl.when(flat == total - 1)
        def _():
            for dev in range(N_DEV):
                o_ref[dev] = acc_ref[dev * per:(dev + 1) * per, :].astype(BF16)
            finish()
            finish_first()

    slot = part.shape[1:]
    outs = pl.pallas_call(
        body,
        name="wgrad_tail",
        grid=(n_tasks, ns),
        in_specs=[spec(t, w) for t in range(n_tasks) for w in (k, n)] + [HBM_SPEC] * (1 + nl),
        out_specs=[_resident((N_DEV, per, n)), _resident(slot), _resident((per, n))]
        + [_resident(g.shape[1:]) for g in landed],
        out_shape=[jax.ShapeDtypeStruct((N_DEV, per, n), BF16), jax.ShapeDtypeStruct(slot, F32),
                   jax.ShapeDtypeStruct((per, n), F32)]
        + [jax.ShapeDtypeStruct(g.shape[1:], F32) for g in landed],
        scratch_shapes=[pltpu.VMEM((k, n), F32), pltpu.VMEM((N_DEV, per, n), BF16), pltpu.VMEM(part.shape, part.dtype)]
        + [pltpu.VMEM(g.shape, g.dtype) for g in landed] + [pltpu.SemaphoreType.DMA((1 + nl,))]
        + _chip_reduce_scratch(slot) + _chip_reduce_scratch((per, n)),
        compiler_params=_params(("arbitrary", "arbitrary")),
    )(*[op for pair in pairs for op in pair], part, *landed)
    return outs[0], outs[1], outs[2], outs[3:]


MINE = "mine"
ADAMW_STEPS = 4


def _adamw(ws, sources, picks, loss_at, ms, vs):
    n, n_src = len(ws), len(sources)
    streamed = [len(w.shape) == 2 and w.shape[0] >= 128 and picks[t][1:] == (0, None)
                and sources[picks[t][0]].shape == w.shape for t, w in enumerate(ws)]
    streamed_sources = {picks[t][0] for t in range(n) if streamed[t]}

    def step(w, g, m, v):
        m = ADAM_B1 * m + (1.0 - ADAM_B1) * g
        v = ADAM_B2 * v + (1.0 - ADAM_B2) * jnp.square(g)
        m_hat = m / (1.0 - ADAM_B1 ** ADAM_STEP)
        v_hat = v / (1.0 - ADAM_B2 ** ADAM_STEP)
        return g, -ADAM_LR * (m_hat / (jnp.sqrt(v_hat) + ADAM_EPS) + ADAM_WD * w), m, v

    def body(*refs):
        refs = list(refs)
        take = lambda k: [refs.pop(0) for _ in range(k)]
        w_refs, s_refs, m_refs, v_refs = take(n), take(n_src), take(n), take(n)
        (loss_ref,), go_refs, d_refs, nm_refs, nv_refs = take(1), take(n), take(n), take(n), take(n)
        me = _my_index()

        def grad(t, rows):
            k, first, cols = picks[t]
            if cols is None:
                return s_refs[k][rows, :]
            if cols is not MINE:
                return s_refs[k][rows, cols]
            width = w_refs[t].shape[-1]
            g = s_refs[k][rows, 0:width]
            for dev in range(1, N_DEV):
                g = jnp.where(me == dev, s_refs[k][rows, dev * width:(dev + 1) * width], g)
            return g

        def whole(t):
            first = picks[t][1]
            rows = w_refs[t].shape[0]
            if len(w_refs[t].shape) == 3:
                for j in range(rows):
                    go_refs[t][j], d_refs[t][j], nm_refs[t][j], nv_refs[t][j] = step(
                        w_refs[t][j], grad(t, slice(first + j, first + j + 1)), m_refs[t][j], v_refs[t][j])
                return
            go_refs[t][...], d_refs[t][...], nm_refs[t][...], nv_refs[t][...] = step(
                w_refs[t][...], grad(t, slice(first, first + rows)), m_refs[t][...], v_refs[t][...])

        def block(t):
            rows = w_refs[t].shape[0]
            chunk = min(rows, 128)

            def one(i, carry):
                r = pl.ds(pl.multiple_of(i * chunk, chunk), chunk)
                go_refs[t][r, :], d_refs[t][r, :], nm_refs[t][r, :], nv_refs[t][r, :] = step(
                    w_refs[t][r, :], grad(t, r), m_refs[t][r, :], v_refs[t][r, :])
                return carry

            lax.fori_loop(0, rows // chunk, one, 0)

        @pl.when(pl.program_id(0) == 0)
        def _():
            loss_ref[...] = s_refs[loss_at[0]][loss_at[1]:loss_at[1] + 1, 0:1]
            for t in range(n):
                if not streamed[t]:
                    whole(t)

        for t in range(n):
            if streamed[t]:
                block(t)

    def rows_of(shape):
        return pl.BlockSpec((shape[0] // ADAMW_STEPS, shape[1]), lambda i: (i, 0))

    w_in = [rows_of(w.shape) if streamed[t] else _full(w.shape) for t, w in enumerate(ws)]
    w_out = [rows_of(w.shape) if streamed[t] else _resident(w.shape) for t, w in enumerate(ws)]
    s_in = [rows_of(s.shape) if k in streamed_sources else _full(s.shape) for k, s in enumerate(sources)]
    outs = pl.pallas_call(
        body,
        name="adamw",
        grid=(ADAMW_STEPS,),
        in_specs=w_in + s_in + w_in * 2,
        out_specs=[_resident((1, 1))] + w_out * 4,
        out_shape=[jax.ShapeDtypeStruct((1, 1), F32)] + [jax.ShapeDtypeStruct(w.shape, F32) for w in ws] * 4,
        compiler_params=_params(("arbitrary",)),
    )(*ws, *sources, *ms, *vs)
    return outs[0], outs[1:n + 1], outs[n + 1:2 * n + 1], outs[2 * n + 1:3 * n + 1], outs[3 * n + 1:]


def _band_structure():
    q_loc = np.arange(BLOCK, dtype=np.int32)[:, None]
    s_loc = np.arange(2 * BLOCK, dtype=np.int32)[None, :]
    dist = q_loc + BLOCK - s_loc
    in_window = (dist >= 0) & (dist < BLOCK)
    dd = np.maximum(dist, 0)
    max_exact = N_BUCKETS // 2
    large = max_exact + (np.log(np.maximum(dd, 1) / max_exact) / math.log(MAX_DISTANCE / max_exact)
                         * (N_BUCKETS - max_exact)).astype(np.int32)
    bucket = np.where(dd < max_exact, dd, np.minimum(large, N_BUCKETS - 1)).astype(np.int32)
    return bucket, in_window.astype(np.int32)


def kernel(x, a_pre_norm, a_w_in, a_conv_w, a_w_out, a_post_norm, kv_norm, w_kv, rel_bias, b_pre_norm, b_w_in, b_sinks, b_w_out, b_post_norm, loss_target, m_a_pre_norm, m_a_w_in, m_a_conv_w, m_a_w_out, m_a_post_norm, m_kv_norm, m_w_kv, m_rel_bias, m_b_pre_norm, m_b_w_in, m_b_sinks, m_b_w_out, m_b_post_norm, v_a_pre_norm, v_a_w_in, v_a_conv_w, v_a_w_out, v_a_post_norm, v_kv_norm, v_w_kv, v_rel_bias, v_b_pre_norm, v_b_w_in, v_b_sinks, v_b_w_out, v_b_post_norm):
    seq, d = x.shape[1], x.shape[2]
    x2 = x.reshape(seq, d)
    target = loss_target.reshape(seq, d)
    shard = a_pre_norm.shape[1]
    ts_a = min(seq, 512)
    ts = min(seq, 512)
    ts_w = min(seq, 2048)

    taps = lambda a: a.transpose(1, 0, 2)
    bucket, in_window = _band_structure()
    (win_g, wout_g), small_g, later, biasm = _all_gather(
        [a_w_in[0], a_w_out[0]], [(0, a_pre_norm), (1, taps(a_conv_w)), (4, a_post_norm)],
        [w_kv, b_w_in[0], b_w_out[0]], rel_bias.T, bucket.T, in_window.T)
    wout = wout_g.reshape(-1, wout_g.shape[2])
    sm = small_g.transpose(1, 0, 2).reshape(8, N_DEV * shard)
    kvn = kv_norm.reshape(1, d)

    (h1, n1, proj, conv, y, ya), (wkv_g, wbin_g, wbout_g) = _layer_a_fwd(x2, sm, win_g, wout, later, ts_a)
    wkv = wkv_g.reshape(-1, wkv_g.shape[2])
    wbout = wbout_g.reshape(-1, wbout_g.shape[2])
    n3, n4, kv, q, o, dh2, dyb, dattn, dz2, acc_c = _layer_b_fwd(
        h1, target, kvn, b_pre_norm, wkv, wbin_g, biasm, b_sinks, wbout, b_post_norm)

    (dq, dkv, dssum, dsink), _ = _attn_bwd(q, kv, dattn, biasm, b_sinks, [])
    by_head = dssum.reshape(N_PAIRS, BAND, 2, BLOCK).transpose(0, 2, 3, 1)
    relb = _relbias_grad(by_head.reshape(N_Q_HEADS, -1), bucket.reshape(1, -1), 4096)
    g_wkv = _wgrad(n3, [dkv], 0, ts_w, "wgrad_kv").reshape(wkv_g.shape)
    g_wbin = _wgrad(n4, [dq, dz2], N_DEV, ts_w, "wgrad_b_in")
    (dh1, dya, acc_b), _ = _layer_b_in_bwd(dh2, dq, dz2, dkv, h1, ya, wbin_g, wkv, kvn, b_pre_norm, sm, [], ts)
    dproj, gx, acc_a = _layer_a_bwd(dya, proj, conv, dh1, x2, wout, win_g, sm, ts_a)
    g_win, (l_wkv, l_wbin) = _wgrad(
        n1, [dproj], N_DEV, ts_w, "wgrad_a_in", ready=[g_wkv, g_wbin], block_cols=2048)
    g_wbout, r_win, r_wout, (r_wkv, r_wbin) = _wgrad_tail(
        [(y, dya), (o, dyb)], g_win, [l_wkv, l_wbin], min(seq, 1024))

    r_wbout, _, (s_a, s_b, s_c, s_relb, s_sink) = _reduce_exchange(g_wbout, [], [acc_a, acc_b, acc_c, relb, dsink])
    weights = [a_pre_norm, a_w_in[0], taps(a_conv_w), a_w_out[0], a_post_norm, kvn, w_kv, rel_bias.T, b_pre_norm,
               b_w_in[0], b_sinks, b_w_out[0], b_post_norm]
    sources = [s_a, s_b, s_c, s_relb, s_sink, r_win, r_wout, r_wkv, r_wbin, r_wbout]
    picks = [(0, 0, MINE), (5, 0, None), (0, 1, MINE), (6, 0, None), (1, 2, MINE), (1, 1, None), (7, 0, None),
             (3, 0, slice(0, N_BUCKETS)), (1, 0, None), (8, 0, None), (4, 0, slice(0, N_Q_HEADS)),
             (9, 0, None), (2, 0, None)]
    first = [m_a_pre_norm, m_a_w_in[0], taps(m_a_conv_w), m_a_w_out[0], m_a_post_norm, m_kv_norm.reshape(1, d),
             m_w_kv, m_rel_bias.T, m_b_pre_norm, m_b_w_in[0], m_b_sinks, m_b_w_out[0], m_b_post_norm]
    second = [v_a_pre_norm, v_a_w_in[0], taps(v_a_conv_w), v_a_w_out[0], v_a_post_norm, v_kv_norm.reshape(1, d),
              v_w_kv, v_rel_bias.T, v_b_pre_norm, v_b_w_in[0], v_b_sinks, v_b_w_out[0], v_b_post_norm]
    loss, grads, deltas, new_m, new_v = _adamw(weights, sources, picks, (2, 1), first, second)

    shapes = [a_pre_norm.shape, a_w_in.shape, taps, a_w_out.shape, a_post_norm.shape, kv_norm.shape,
              w_kv.shape, jnp.transpose, b_pre_norm.shape, b_w_in.shape, b_sinks.shape, b_w_out.shape, b_post_norm.shape]
    shaped = lambda arrays: [s(a) if callable(s) else a.reshape(s) for a, s in zip(arrays, shapes)]
    return (loss.reshape(()), gx.reshape(x.shape), *shaped(grads), *shaped(deltas), *shaped(new_m), *shaped(new_v))
```

```python
import math

import jax
import jax.numpy as jnp
import numpy as np
from jax import lax
from jax.experimental import pallas as pl
from jax.experimental.pallas import tpu as pltpu

HEAD_DIM = 64
N_Q_HEADS = 16
N_KV_HEADS = 2
GROUP = N_Q_HEADS // N_KV_HEADS
BLOCK = 128
N_BUCKETS = 32
MAX_DISTANCE = 128
EPS = 1e-6
NEG_INF = -1e30
SCALE = HEAD_DIM ** -0.5

ADAM_LR = 0.001
ADAM_B1 = 0.9
ADAM_B2 = 0.999
ADAM_EPS = 1e-08
ADAM_WD = 0.01
ADAM_STEP = 10

N_PAIRS = N_Q_HEADS // 2
BAND = 2 * BLOCK

N_DEV = 8
GATHER_PIECE_ROWS = 256
LANES = 128
F32 = jnp.float32
BF16 = jnp.bfloat16
MESH = pl.DeviceIdType.MESH
MIB = 1024 * 1024
VMEM_RESERVED_MIB = 63


def _params(semantics=None, vmem_mib=48):
    del vmem_mib
    return pltpu.CompilerParams(dimension_semantics=semantics, vmem_limit_bytes=VMEM_RESERVED_MIB * MIB)


def _full(shape):
    zeros = (0,) * len(shape)
    return pl.BlockSpec(shape, lambda *_: zeros, pipeline_mode=pl.Buffered(1))


def _resident(shape):
    zeros = (0,) * len(shape)
    return pl.BlockSpec(shape, lambda *_: zeros)


def _rows(ts, cols):
    return pl.BlockSpec((ts, cols), lambda i: (i, 0))


def _dot(a, b):
    return jnp.dot(a, b, preferred_element_type=F32)


def _dot_nt(a, b):
    return lax.dot_general(a, b, (((1,), (1,)), ((), ())), preferred_element_type=F32)


def _dot_tn(a, b):
    return lax.dot_general(a, b, (((0,), (0,)), ((), ())), preferred_element_type=F32)


def _rms(xf):
    r = lax.rsqrt(jnp.mean(xf * xf, axis=-1, keepdims=True) + EPS)
    return xf * r, r


def _rms_bwd(dn, xn, r):
    return r * (dn - xn * jnp.mean(dn * xn, axis=-1, keepdims=True))


def _silu(z):
    s = jax.nn.sigmoid(z)
    return z * s, s * (1.0 + z * (1.0 - s))


def _my_index():
    return 4 * lax.axis_index("x") + 2 * lax.axis_index("y") + lax.axis_index("c")


def _bias_table(rb_ref, bucket_ref, win_ref, out_ref):
    bk = jnp.where(win_ref[...] != 0, bucket_ref[...], -1)
    has_prev = lax.broadcasted_iota(jnp.int32, bk.shape, 0) >= BLOCK
    for h in range(N_Q_HEADS):
        acc = jnp.full(bk.shape, NEG_INF, F32)
        for b in range(N_BUCKETS):
            acc = jnp.where(bk == b, rb_ref[h, b], acc)
        cols = slice((h % 2) * BLOCK, (h % 2 + 1) * BLOCK)
        out_ref[1, h // 2, :, cols] = acc
        out_ref[0, h // 2, :, cols] = jnp.where(has_prev, acc, NEG_INF)


def _all_gather(shards, small_rows, casts, rel_bias_t, bucket_t, in_window_t):
    ns, nc, n = len(small_rows), len(casts), len(shards) + 1
    small_shape = (8, small_rows[0][1].shape[-1])
    shapes = [s.shape for s in shards] + [small_shape]
    pieces = [(t, r0, min(GATHER_PIECE_ROWS, shape[0] - r0))
              for t, shape in enumerate(shapes) for r0 in range(0, shape[0], GATHER_PIECE_ROWS)]

    def body(*refs):
        refs = list(refs)
        take = lambda k: [refs.pop(0) for _ in range(k)]
        ins, small_refs, cast_refs, (rb_ref, bucket_ref, win_ref) = take(n - 1), take(ns), take(nc), take(3)
        outs, cast_outs, (bias_ref, send_sems, recv_sems) = take(n), take(nc), take(3)
        x, y, c = lax.axis_index("x"), lax.axis_index("y"), lax.axis_index("c")
        me, sibling = (x, y, c), (x, y, 1 - c)
        x_nbr, y_nbr, diagonal = (1 - x, y), (x, 1 - y), (1 - x, 1 - y)
        south = c == 0
        relayed = (jnp.where(south, 1 - x, x), jnp.where(south, y, 1 - y))
        relay_to = (jnp.where(south, x, 1 - x), jnp.where(south, 1 - y, y))

        def copy(u, k, block, to):
            t, r0, nrows = pieces[u]
            rows = outs[t].at[4 * block[0] + 2 * block[1] + block[2], pl.ds(r0, nrows)]
            return pltpu.make_async_remote_copy(
                src_ref=rows, dst_ref=rows, send_sem=send_sems.at[u, k], recv_sem=recv_sems.at[u, k],
                device_id=to, device_id_type=MESH)

        mine = pl.ds(_my_index(), 1)
        for t in range(n - 1):
            outs[t][mine] = ins[t][...].astype(BF16)[None]
        outs[n - 1][mine] = jnp.zeros((1,) + small_shape, F32)
        for (row, _), ref in zip(small_rows, small_refs):
            if len(ref.shape) == 3:
                for j in range(ref.shape[0]):
                    outs[n - 1][mine, row + j:row + j + 1, :] = ref[j][None]
            else:
                outs[n - 1][mine, row:row + ref.shape[0], :] = ref[...][None]
        started = []

        def start(cp):
            cp.start()
            started.append(cp)

        units = range(len(pieces))
        for u in units:
            start(copy(u, 0, me, sibling))
            start(copy(u, 1, me, (*x_nbr, c)))
            start(copy(u, 2, me, (*y_nbr, c)))
        for src, dst in zip(cast_refs, cast_outs):
            dst[...] = src[...].astype(BF16)
        _bias_table(rb_ref, bucket_ref, win_ref, bias_ref)
        for u in units:
            for k, chip in ((1, x_nbr), (2, y_nbr)):
                copy(u, k, (*chip, c), me).wait_recv()
                start(copy(u, 3 + k, (*chip, c), sibling))
            start(copy(u, 3, (*relayed, c), (*relay_to, c)))
        for u in units:
            copy(u, 3, (*diagonal, c), me).wait_recv()
            start(copy(u, 6, (*diagonal, c), sibling))
        for u in units:
            copy(u, 0, sibling, me).wait_recv()
        for k, chip in ((4, x_nbr), (5, y_nbr), (6, diagonal)):
            for u in units:
                copy(u, k, (*chip, 1 - c), me).wait_recv()
        for cp in started:
            cp.wait_send()

    vmem = pl.BlockSpec(memory_space=pltpu.VMEM)
    outs = pl.pallas_call(
        body,
        name="gather_weights",
        out_shape=[jax.ShapeDtypeStruct((N_DEV,) + s.shape, BF16) for s in shards]
        + [jax.ShapeDtypeStruct((N_DEV,) + small_shape, F32)]
        + [jax.ShapeDtypeStruct(a.shape, BF16) for a in casts]
        + [jax.ShapeDtypeStruct((2, N_PAIRS, BAND, 2 * BLOCK), F32)],
        in_specs=[vmem] * (n - 1 + ns + nc) + [pl.BlockSpec(memory_space=pltpu.SMEM), vmem, vmem],
        out_specs=[vmem] * (n + nc + 1),
        scratch_shapes=[pltpu.SemaphoreType.DMA((len(pieces), 7)), pltpu.SemaphoreType.DMA((len(pieces), 7))],
        compiler_params=_params(),
    )(*shards, *[a for _, a in small_rows], *casts, rel_bias_t, bucket_t, in_window_t)
    return outs[:n - 1], outs[n - 1], outs[n:n + nc], outs[n + nc]


def _peer(k):
    x, y, c = lax.axis_index("x"), lax.axis_index("y"), lax.axis_index("c")
    px = 1 - x if k & 4 else x
    py = 1 - y if k & 2 else y
    pc = 1 - c if k & 1 else c
    return (px, py, pc), 4 * px + 2 * py + pc


def _exchange(srcs, dsts, send_sems, recv_sems, local_sems, scatter):
    me = _my_index()
    sends, arrivals = [], []
    for k in range(1, N_DEV):
        peer, pidx = _peer(k)
        for t, (src, dst) in enumerate(zip(srcs, dsts)):
            mine = src.at[pidx] if scatter else src
            sems = dict(send_sem=send_sems.at[t, k - 1], recv_sem=recv_sems.at[t, k - 1], device_id=peer, device_id_type=MESH)
            sends.append(pltpu.make_async_remote_copy(src_ref=mine, dst_ref=dst.at[me], **sems))
            arrivals.append(pltpu.make_async_remote_copy(src_ref=mine, dst_ref=dst.at[pidx], **sems))
    local = [pltpu.make_async_copy(src.at[me] if scatter else src, dst.at[me], local_sems.at[t])
             for t, (src, dst) in enumerate(zip(srcs, dsts))]
    return sends, arrivals, local


def _exchange_start(*args):
    sends, _, local = _exchange(*args)
    for cp in sends + local:
        cp.start()


def _exchange_wait(*args):
    sends, arrivals, local = _exchange(*args)
    for cp in arrivals:
        cp.wait_recv()
    for cp in sends:
        cp.wait_send()
    for cp in local:
        cp.wait()


def _exchange_sems(n):
    if not n:
        return []
    return [pltpu.SemaphoreType.DMA((n, N_DEV - 1)), pltpu.SemaphoreType.DMA((n, N_DEV - 1)), pltpu.SemaphoreType.DMA((n,))]


HBM_SPEC = pl.BlockSpec(memory_space=pl.ANY)


def _sum_slots(recv_ref, out_ref):
    rows = out_ref.shape[0]
    chunk = min(rows, 128)

    def add(i, carry):
        r0 = pl.multiple_of(i * chunk, chunk)
        acc = recv_ref[0, pl.ds(r0, chunk), :].astype(F32)
        for dev in range(1, N_DEV):
            acc = acc + recv_ref[dev, pl.ds(r0, chunk), :].astype(F32)
        out_ref[pl.ds(r0, chunk), :] = acc
        return carry

    lax.fori_loop(0, rows // chunk, add, 0)


N_CHIPS = N_DEV // 2


def _rows_loop(rows, fn):
    chunk = min(rows, 128)

    def step(i, carry):
        fn(pl.ds(pl.multiple_of(i * chunk, chunk), chunk))
        return carry

    lax.fori_loop(0, rows // chunk, step, 0)


def _chip_reduce(g_ref, out_ref, sib_ref, land_ref, send_ref, sems, swap_src=None):
    sib_send, sib_recv, ici_send, ici_recv = sems
    x, y, c = lax.axis_index("x"), lax.axis_index("y"), lax.axis_index("c")
    south = c == 0
    near =(jnp.where(south, 1 - x, x), jnp.where(south, y, 1 - y))
    far = (jnp.where(south, x, 1 - x), jnp.where(south, 1 - y, y))
    diagonal = (1 - x, 1 - y)
    rows = out_ref.shape[0]
    direct, fold, folded = 0, 1, 2

    def to_sibling(t):
        src = g_ref if swap_src is None else swap_src
        return pltpu.make_async_remote_copy(
            src_ref=src.at[2 * t + 1 - c], dst_ref=sib_ref.at[t], send_sem=sib_send.at[t], recv_sem=sib_recv.at[t],
            device_id=(x, y, 1 - c), device_id_type=MESH)

    def ici(role, chip):
        return pltpu.make_async_remote_copy(
            src_ref=send_ref.at[role], dst_ref=land_ref.at[role], send_sem=ici_send.at[role],
            recv_sem=ici_recv.at[role], device_id=(*chip, c), device_id_type=MESH)

    def pair_sum(chip, r):
        t = 2 * chip[0] + chip[1]
        return g_ref[2 * t + c, r, :].astype(F32) + sib_ref[t, r, :].astype(F32)

    def swap():
        for t in range(N_CHIPS):
            to_sibling(t).start()

    def send():
        for t in range(N_CHIPS):
            to_sibling(t).wait_recv()
        for role, chip in ((fold, diagonal), (direct, near)):
            def fill(r, role=role, chip=chip):
                send_ref[role, r, :] = pair_sum(chip, r).astype(BF16)

            _rows_loop(rows, fill)
            ici(role, near).start()

    def forward():
        ici(fold, near).wait_recv()

        def fill(r):
            send_ref[folded, r, :] = (pair_sum(far, r) + land_ref[fold, r, :].astype(F32)).astype(BF16)

        _rows_loop(rows, fill)
        ici(folded, far).start()

    def finish():
        ici(direct, near).wait_recv()
        ici(folded, far).wait_recv()

        def total(r):
            mine = pair_sum((x, y), r)
            out_ref[r, :] = mine + land_ref[direct, r, :].astype(F32) + land_ref[folded, r, :].astype(F32)

        _rows_loop(rows, total)
        for t in range(N_CHIPS):
            to_sibling(t).wait_send()
        for role, chip in ((direct, near), (fold, near), (folded, far)):
            ici(role, chip).wait_send()

    return swap, send, forward, finish


def _chip_reduce_scratch(slot):
    return [pltpu.VMEM((N_CHIPS,) + slot, BF16), pltpu.VMEM((3,) + slot, BF16), pltpu.VMEM((3,) + slot, BF16),
            pltpu.SemaphoreType.DMA((N_CHIPS,)), pltpu.SemaphoreType.DMA((N_CHIPS,)),
            pltpu.SemaphoreType.DMA((3,)), pltpu.SemaphoreType.DMA((3,))]


def _bucket_sums(a_ref, bucket_ref, cols):
    a = a_ref[:, cols]
    hi = a.astype(BF16)
    lo = (a - hi.astype(F32)).astype(BF16)
    rows = lax.broadcasted_iota(jnp.int32, (LANES, a.shape[1]), 0)
    onehot_t = (rows == bucket_ref[:, cols]).astype(F32).astype(BF16)
    return _dot_nt(hi, onehot_t) + _dot_nt(lo, onehot_t)


def _reduce_exchange(part, landed, smalls, by_bucket, bucket_row, chunk):
    nl, ng = len(landed), len(smalls)
    n_in = 1 + nl + ng + 2
    n_out = 1 + nl + ng + 1
    heads, positions = by_bucket.shape
    chunks = [slice(c0, c0 + chunk) for c0 in range(0, positions, chunk)]

    def body(*refs):
        p_in, l_in, s_in, (a_ref, bucket_ref) = refs[0], refs[1:1 + nl], refs[1 + nl:n_in - 2], refs[n_in - 2:n_in]
        refs = refs[n_in:]
        p_out, l_out, s_out, b_out = refs[0], refs[1:1 + nl], refs[1 + nl:n_out - 1], refs[n_out - 1]
        scratch = refs[n_out:]
        s_recv, (b_recv, b_ref, sib_ref, chip_ref, send_ref), sems = scratch[:ng], scratch[ng:ng + 5], scratch[ng + 5:]
        swap, send, forward, finish = _chip_reduce(p_in, p_out, sib_ref, chip_ref, send_ref, sems[:4])
        swap()
        _exchange_start(s_in, s_recv, *sems[4:7], False)
        send()
        for t in range(nl):
            _sum_slots(l_in[t], l_out[t])
        b_ref[...] = jnp.zeros_like(b_ref)
        for cols in chunks[:len(chunks) // 2]:
            b_ref[...] += _bucket_sums(a_ref, bucket_ref, cols)
        forward()
        for cols in chunks[len(chunks) // 2:]:
            b_ref[...] += _bucket_sums(a_ref, bucket_ref, cols)
        _exchange_start([b_ref], [b_recv], *sems[7:], False)
        finish()
        _exchange_wait(s_in, s_recv, *sems[4:7], False)
        _exchange_wait([b_ref], [b_recv], *sems[7:], False)
        for recv, out in zip([*s_recv, b_recv], [*s_out, b_out]):
            acc = recv[0]
            for dev in range(1, N_DEV):
                acc = acc + recv[dev]
            out[...] = acc

    vmem = pl.BlockSpec(memory_space=pltpu.VMEM)
    slot = part.shape[1:]
    outs = pl.pallas_call(
        body,
        name="reduce_grads",
        out_shape=[jax.ShapeDtypeStruct(p.shape[1:], F32) for p in [part] + landed]
        + [jax.ShapeDtypeStruct(s.shape, F32) for s in smalls] + [jax.ShapeDtypeStruct((heads, LANES), F32)],
        in_specs=[vmem] * n_in,
        out_specs=[vmem] * n_out,
        scratch_shapes=[pltpu.VMEM((N_DEV,) + s.shape, F32) for s in smalls]
        + [pltpu.VMEM((N_DEV, heads, LANES), F32), pltpu.VMEM((heads, LANES), F32)] + _chip_reduce_scratch(slot)
        + _exchange_sems(ng) + _exchange_sems(1),
        compiler_params=_params(),
    )(part, *landed, *smalls, by_bucket, bucket_row)
    return outs[0], outs[1:1 + nl], outs[1 + nl:n_out - 1], outs[n_out - 1]


def _layer_a_fwd(x2, sm, win_g, wout, later, ts):
    seq, d = x2.shape
    width = wout.shape[0]
    half = win_g.shape[2]
    n_half = width // half
    nl = len(later)
    nt = seq // ts

    def body(x_ref, sm_ref, win_ref, wout_ref, *refs):
        shard_refs, refs = refs[:nl], refs[nl:]
        h1_ref, n1_ref, proj_ref, conv_ref, y_ref, ya_ref = refs[:6]
        gathered_refs, (vprev_ref, *sems) = refs[6:6 + nl], refs[6 + nl:]

        @pl.when(pl.program_id(0) == 0)
        def _():
            vprev_ref[...] = jnp.zeros_like(vprev_ref)
            _exchange_start(shard_refs, gathered_refs, *sems, False)

        @pl.when(pl.program_id(0) == nt - 1)
        def _():
            _exchange_wait(shard_refs, gathered_refs, *sems, False)

        xf = x_ref[...]
        xn, _ = _rms(xf)
        n1 = (xn * sm_ref[0:1, :]).astype(BF16)
        n1_ref[...] = n1
        row = lax.broadcasted_iota(jnp.int32, (ts, half), 0)
        ya = jnp.zeros((ts, d), F32)
        for hh in range(n_half):
            cols = slice(hh * half, (hh + 1) * half)
            parts = []
            for part in range(4):
                j = part * n_half + hh
                pj = _dot(n1, win_ref[j])
                proj_ref[:, j * half:(j + 1) * half] = pj.astype(BF16)
                parts.append(pj)
            b, c, u, z = parts
            v = c * u
            last1, last2 = vprev_ref[7:8, cols], vprev_ref[6:7, cols]
            v1 = jnp.where(row == 0, last1, pltpu.roll(v, 1, 0))
            v2 = jnp.where(row == 0, last2, jnp.where(row == 1, last1, pltpu.roll(v, 2, 0)))
            vprev_ref[:, cols] = v[ts - 8:ts, :]
            conv = sm_ref[1:2, cols] * v2 + sm_ref[2:3, cols] * v1 + sm_ref[3:4, cols] * v
            conv_ref[:, cols] = conv.astype(BF16)
            yh = (b * conv * _silu(z)[0]).astype(BF16)
            y_ref[:, cols] = yh
            ya = ya + _dot(yh, wout_ref[cols, :])
        ya_ref[...] = ya
        h1_ref[...] = xf + _rms(ya)[0] * sm_ref[4:5, :]

    outs = pl.pallas_call(
        body,
        name="layer_a_fwd",
        grid=(nt,),
        in_specs=[_rows(ts, d), _full(sm.shape), _full(win_g.shape), _full(wout.shape)] + [HBM_SPEC] * nl,
        out_specs=[_rows(ts, d), _rows(ts, d), _rows(ts, 4 * width), _rows(ts, width), _rows(ts, width), _rows(ts, d)]
        + [HBM_SPEC] * nl,
        out_shape=[
            jax.ShapeDtypeStruct((seq, d), F32),
            jax.ShapeDtypeStruct((seq, d), BF16),
            jax.ShapeDtypeStruct((seq, 4 * width), BF16),
            jax.ShapeDtypeStruct((seq, width), BF16),
            jax.ShapeDtypeStruct((seq, width), BF16),
            jax.ShapeDtypeStruct((seq, d), F32),
        ] + [jax.ShapeDtypeStruct((N_DEV,) + s.shape, s.dtype) for s in later],
        scratch_shapes=[pltpu.VMEM((8, width), F32)] + _exchange_sems(nl),
        compiler_params=_params(("arbitrary",), 56),
    )(x2, sm, win_g, wout, *later)
    return outs[:6], outs[6:]


Q_BLOCKS = 4
ATTN_BWD_LAGS = (2, 4)
ATTN_FWD_LAGS = (2, 4)


def _banded_tiles(kvp_ref, kvc_ref):
    tile = kvc_ref[...].astype(F32)
    blocks = [kvp_ref[...].astype(F32)] + [tile[u * BLOCK:(u + 1) * BLOCK] for u in range(Q_BLOCKS)]
    return [_banded_kv(blocks[u], blocks[u + 1]) for u in range(Q_BLOCKS)]


def _bias_of(bias_ref, i, u, m):
    return bias_ref[jnp.minimum(i, 1) if u == 0 else 1, m]


def _banded_kv(kvp, kvc):
    kw = N_KV_HEADS * HEAD_DIM
    out = []
    for full in (jnp.concatenate([kvp[:, :kw], kvc[:, :kw]], axis=0), jnp.concatenate([kvp[:, kw:], kvc[:, kw:]], axis=0)):
        lo = lax.broadcasted_iota(jnp.int32, full.shape, 1) < HEAD_DIM
        rolled = pltpu.roll(full, HEAD_DIM, 1)
        x2 = [jnp.where(lo, full, rolled).astype(BF16), jnp.where(lo, rolled, full).astype(BF16)]
        ft = full.T
        x2t = [jnp.concatenate([ft[kh * HEAD_DIM:(kh + 1) * HEAD_DIM]] * 2, axis=0).astype(BF16) for kh in range(N_KV_HEADS)]
        out += [x2, x2t]
    return out


def _pair_rows(ref, rows, m, scale=None):
    both = ref[rows, m * LANES:(m + 1) * LANES].astype(F32)
    if scale is not None:
        both = both * scale
    lo = lax.broadcasted_iota(jnp.int32, both.shape, 1) < HEAD_DIM
    zero = jnp.zeros_like(both)
    return jnp.concatenate([jnp.where(lo, both, zero), jnp.where(lo, zero, both)], axis=0).astype(BF16)


def _pair_cols(res_t):
    top = lax.broadcasted_iota(jnp.int32, (LANES, BLOCK), 0) < HEAD_DIM
    return jnp.where(top, res_t[:, :BLOCK], res_t[:, BLOCK:]).T


def _sink_row(sink_ref, m):
    first = lax.broadcasted_iota(jnp.int32, (1, 2 * BLOCK), 1) < BLOCK
    return jnp.where(first, sink_ref[0, 2 * m], sink_ref[0, 2 * m + 1])


def _softmax_t(logits, sink):
    mx =jnp.maximum(jnp.max(logits, axis=0, keepdims=True), sink)
    p = jnp.exp(logits - mx)
    sink_p = jnp.exp(sink - mx)
    inv = 1.0 / (jnp.sum(p, axis=0, keepdims=True) + sink_p)
    return p * inv, sink_p * inv


def _layer_b_fwd(h1, target, kvn, bpre, wkv, wbin_g, biasm, sinks, wbout, bpost):
    seq, d = h1.shape
    kvw = wkv.shape[1]
    cw = wbin_g.shape[2]
    aw = N_Q_HEADS * HEAD_DIM
    per = aw // cw
    tile = Q_BLOCKS * BLOCK

    def body(sink_ref, h1_ref, tgt_ref, kvn_ref, bpre_ref, wkv_ref, wbin_ref, bias_ref, w_ref, g_ref,
             n3_ref, n4_ref, kvc_ref, q_ref, o_ref, dh2_ref, dyb_ref, dattn_ref, dz2_ref, acc_ref,
             attn_ref, z2_ref, kvp_ref):
        i = pl.program_id(0)

        @pl.when(i == 0)
        def _():
            acc_ref[...] = jnp.zeros_like(acc_ref)
            kvp_ref[...] = jnp.zeros_like(kvp_ref)

        hn, _ = _rms(h1_ref[...])
        n3 = (hn * kvn_ref[...]).astype(BF16)
        n4 = (hn * bpre_ref[...]).astype(BF16)
        n3_ref[...] = n3
        n4_ref[...] = n4
        kvc_ref[...] = _dot(n3, wkv_ref[...]).astype(BF16)
        for j in range(N_DEV):
            pj = _dot(n4, wbin_ref[j])
            if j < per:
                q_ref[:, j * cw:(j + 1) * cw] = pj.astype(BF16)
            else:
                z2_ref[:, (j - per) * cw:(j - per + 1) * cw] = pj

        banded = _banded_tiles(kvp_ref, kvc_ref)
        kvp_ref[...] = kvc_ref[tile - BLOCK:tile, :]
        units = [(u, m) for u in range(Q_BLOCKS) for m in range(N_PAIRS)]
        kv_of = lambda m: (2 * m) // GROUP
        logits, probs = {}, {}
        lag_b, lag_c = ATTN_FWD_LAGS
        for step in range(len(units) + lag_c):
            if step < len(units):
                u, m = units[step]
                qpair = _pair_rows(q_ref, slice(u * BLOCK, (u + 1) * BLOCK), m, SCALE)
                logits[step] = _dot_nt(banded[u][0][kv_of(m)], qpair) + _bias_of(bias_ref, i, u, m)
            if 0 <= step - lag_b < len(units):
                u, m = units[step - lag_b]
                probs[step - lag_b] = _softmax_t(logits.pop(step - lag_b), _sink_row(sink_ref, m))[0].astype(BF16)
            if 0 <= step - lag_c < len(units):
                u, m = units[step - lag_c]
                out_t = _dot(banded[u][3][kv_of(m)], probs.pop(step - lag_c))
                attn_ref[u * BLOCK:(u + 1) * BLOCK, m * LANES:(m + 1) * LANES] = _pair_cols(out_t)
        attn = attn_ref[...]
        sz, dsz = _silu(z2_ref[...])
        o = (attn * sz).astype(BF16)
        o_ref[...] = o

        w = w_ref[...]
        yb = _dot(o, w)
        ybn, r = _rms(yb)
        g = g_ref[...]
        diff = h1_ref[...] + ybn * g - tgt_ref[...]
        dh2 = diff * (1.0 / d)
        dh2_ref[...] = dh2
        acc_ref[0:1, :] += jnp.sum(dh2 * ybn, axis=0, keepdims=True)
        tok = jnp.mean(diff * diff, axis=-1, keepdims=True)
        acc_ref[1:2, :] += 0.5 * jnp.sum(tok, axis=0, keepdims=True)
        dyb = _rms_bwd(dh2 * g, ybn, r).astype(BF16)
        dyb_ref[...] = dyb
        do = _dot_nt(dyb, w)
        dattn_ref[...] = (do * sz).astype(BF16)
        dz2_ref[...] = (do * attn * dsz).astype(BF16)

    blk = lambda w: pl.BlockSpec((tile, w), lambda i: (i, 0))
    return pl.pallas_call(
        body,
        name="layer_b_fwd",
        grid=(seq // tile,),
        in_specs=[
            pl.BlockSpec(memory_space=pltpu.SMEM),
            blk(d),
            blk(d),
            _full(kvn.shape),
            _full(bpre.shape),
            _full(wkv.shape),
            _full(wbin_g.shape),
            _full(biasm.shape),
            _full(wbout.shape),
            _full(bpost.shape),
        ],
        out_specs=[blk(d), blk(d), blk(kvw), blk(aw), blk(aw), blk(d), blk(d), blk(aw), blk(aw), _resident((8, d))],
        out_shape=[
            jax.ShapeDtypeStruct((seq, d), BF16),
            jax.ShapeDtypeStruct((seq, d), BF16),
            jax.ShapeDtypeStruct((seq, kvw), BF16),
            jax.ShapeDtypeStruct((seq, aw), BF16),
            jax.ShapeDtypeStruct((seq, aw), BF16),
            jax.ShapeDtypeStruct((seq, d), F32),
            jax.ShapeDtypeStruct((seq, d), BF16),
            jax.ShapeDtypeStruct((seq, aw), BF16),
            jax.ShapeDtypeStruct((seq, aw), BF16),
            jax.ShapeDtypeStruct((8, d), F32),
        ],
        scratch_shapes=[pltpu.VMEM((tile, aw), F32), pltpu.VMEM((tile, aw), F32), pltpu.VMEM((BLOCK, kvw), BF16)],
        compiler_params=_params(("arbitrary",), 56),
    )(sinks, h1, target, kvn, bpre, wkv, wbin_g, biasm, wbout, bpost)


def _attn_bwd(q, kv, dattn, biasm, sinks, ready):
    seq, aw = q.shape
    kvw = kv.shape[1]
    kw = N_KV_HEADS * HEAD_DIM
    nb = seq // BLOCK
    pairs_per_kv = N_PAIRS // N_KV_HEADS
    nr = len(ready)

    tile = Q_BLOCKS * BLOCK
    nsteps = seq // tile
    held = (Q_BLOCKS - 1) * BLOCK

    def body(sink_ref, q_ref, kvc_ref, kvp_ref, da_ref, bias_ref, *refs):
        ready_refs, (dq_ref, dkv_ref, dssum_ref, dsink_ref) = refs[:nr], refs[nr:nr + 4]
        landed_refs, scratch = refs[nr + 4:2 * nr + 4], refs[2 * nr + 4:]
        carry_ref, done_ref, qs_ref, dos_ref, dst_ref, pt_ref, *sems = scratch
        i = pl.program_id(0)

        @pl.when(i == 0)
        def _():
            dssum_ref[...] = jnp.zeros_like(dssum_ref)
            dsink_ref[...] = jnp.zeros_like(dsink_ref)
            carry_ref[...] = jnp.zeros_like(carry_ref)
            done_ref[...] = jnp.zeros_like(done_ref)
            if nr:
                _exchange_start(ready_refs, landed_refs, *sems, True)

        if nr:
            @pl.when(i == nsteps)
            def _():
                _exchange_wait(ready_refs, landed_refs, *sems, True)

        @pl.when(i < nsteps)
        def _():
            lo = lax.broadcasted_iota(jnp.int32, (BAND, LANES), 1) < HEAD_DIM
            head_lane = lax.broadcasted_iota(jnp.int32, (1, LANES), 1)
            banded = _banded_tiles(kvp_ref, kvc_ref)
            units = [(u, m) for u in range(Q_BLOCKS) for m in range(N_PAIRS)]
            dsink = jnp.zeros((1, LANES), F32)
            folded = {}
            logits, dps, dsbs = {}, {}, {}
            lag_b, lag_c = ATTN_BWD_LAGS
            for step in range(len(units) + lag_c):
                if step < len(units):
                    u, m = units[step]
                    kh, rows = m // pairs_per_kv, slice((m % pairs_per_kv) * BAND, (m % pairs_per_kv + 1) * BAND)
                    qrows = slice(u * BLOCK, (u + 1) * BLOCK)
                    qpair = _pair_rows(q_ref, qrows, m, SCALE)
                    dopair = _pair_rows(da_ref, qrows, m)
                    qs_ref[u, kh, rows, :] = qpair
                    dos_ref[u, kh, rows, :] = dopair
                    logits[step] = _dot_nt(banded[u][0][kh], qpair) + _bias_of(bias_ref, i, u, m)
                    dps[step] = _dot_nt(banded[u][2][kh], dopair)
                if 0 <= step - lag_b < len(units):
                    u, m = units[step - lag_b]
                    kh, rows = m // pairs_per_kv, slice((m % pairs_per_kv) * BAND, (m % pairs_per_kv + 1) * BAND)
                    pn, sink_p = _softmax_t(logits.pop(step - lag_b), _sink_row(sink_ref, m))
                    dp = dps.pop(step - lag_b)
                    delta = jnp.sum(pn * dp, axis=0, keepdims=True)
                    ds = pn * (dp - delta)
                    dssum_ref[m] += ds
                    sink_term = sink_p * delta
                    for e in range(2):
                        total = jnp.sum(sink_term[:, e * BLOCK:(e + 1) * BLOCK], axis=1, keepdims=True)
                        dsink = dsink - jnp.where(head_lane == 2 * m + e, total, 0.0)
                    dsbs[step - lag_b] = ds.astype(BF16)
                    dst_ref[u, kh, :, rows] = dsbs[step - lag_b]
                    pt_ref[u, kh, :, rows] = pn.astype(BF16)
                if 0 <= step - lag_c < len(units):
                    u, m = units[step - lag_c]
                    kh = m // pairs_per_kv
                    dq_t = _dot(banded[u][1][kh], dsbs.pop(step - lag_c))
                    dq_ref[u * BLOCK:(u + 1) * BLOCK, m * LANES:(m + 1) * LANES] = (_pair_cols(dq_t) * SCALE).astype(BF16)
                    if m % pairs_per_kv == pairs_per_kv - 1:
                        for name, lhs_ref, rhs_ref in (("k", dst_ref, qs_ref), ("v", pt_ref, dos_ref)):
                            acc = _dot(lhs_ref[u, kh], rhs_ref[u, kh])
                            folded[u, kh, name] = acc + pltpu.roll(acc, HEAD_DIM, 1)
            dsink_ref[0:1, :] += dsink
            dkv = [jnp.concatenate([jnp.where(lo, folded[u, 0, n], folded[u, 1, n]) for n in ("k", "v")], axis=1)
                   for u in range(Q_BLOCKS)]

            @pl.when(i > 0)
            def _():
                if held:
                    dkv_ref[:held, :] = done_ref[...].astype(BF16)
                dkv_ref[held:, :] = (carry_ref[...] + dkv[0][:BLOCK]).astype(BF16)

            for u in range(Q_BLOCKS - 1):
                done_ref[u * BLOCK:(u + 1) * BLOCK, :] = dkv[u][BLOCK:] + dkv[u + 1][:BLOCK]
            carry_ref[...] = dkv[Q_BLOCKS - 1][BLOCK:]

        @pl.when(i == nsteps)
        def _():
            if held:
                dkv_ref[:held, :] = done_ref[...].astype(BF16)
            dkv_ref[held:, :] = carry_ref[...].astype(BF16)

    last = nsteps - 1
    blk = lambda w: pl.BlockSpec((tile, w), lambda i: (jnp.minimum(i, last), 0))
    outs = pl.pallas_call(
        body,
        name="attn_bwd",
        grid=(nsteps + 1,),
        in_specs=[
            pl.BlockSpec(memory_space=pltpu.SMEM),
            blk(aw),
            blk(kvw),
            pl.BlockSpec((BLOCK, kvw), lambda i: (jnp.clip(Q_BLOCKS * i - 1, 0, nb - 1), 0)),
            blk(aw),
            _full(biasm.shape),
        ] + [HBM_SPEC] * nr,
        out_specs=[
            blk(aw),
            pl.BlockSpec((tile, kvw), lambda i: (jnp.maximum(i - 1, 0), 0)),
            _resident(biasm.shape[1:]),
            _resident((8, LANES)),
        ] + [HBM_SPEC] * nr,
        out_shape=[
            jax.ShapeDtypeStruct((seq, aw), BF16),
            jax.ShapeDtypeStruct((seq, kvw), BF16),
            jax.ShapeDtypeStruct(biasm.shape[1:], F32),
            jax.ShapeDtypeStruct((8, LANES), F32),
        ] + [jax.ShapeDtypeStruct(g.shape, g.dtype) for g in ready],
        scratch_shapes=[
            pltpu.VMEM((BLOCK, kvw), F32),
            pltpu.VMEM((max(held, 8), kvw), F32),
            pltpu.VMEM((Q_BLOCKS, N_KV_HEADS, pairs_per_kv * BAND, LANES), BF16),
            pltpu.VMEM((Q_BLOCKS, N_KV_HEADS, pairs_per_kv * BAND, LANES), BF16),
            pltpu.VMEM((Q_BLOCKS, N_KV_HEADS, BAND, pairs_per_kv * BAND), BF16),
            pltpu.VMEM((Q_BLOCKS, N_KV_HEADS, BAND, pairs_per_kv * BAND), BF16),
        ] + _exchange_sems(nr),
        compiler_params=_params(("arbitrary",), 48),
    )(sinks, q, kv, kv, dattn, biasm, *ready)
    return outs[:4], outs[4:]


def _layer_b_in_bwd(dh2, dq, dz2, dkv, h1, ya, wbin_g, wkv, kvn, bpre, sm, ready, ts):
    seq, d = h1.shape
    aw = dq.shape[1]
    kvw = dkv.shape[1]
    cw = wbin_g.shape[2]
    per = aw // cw

    nr = len(ready)
    nt = seq // ts

    def body(dh2_ref, dq_ref, dz2_ref, dkv_ref, h1_ref, ya_ref, wbin_ref, wkv_ref, kvn_ref, bpre_ref, sm_ref, *refs):
        ready_refs, (dh1_ref, dya_ref, acc_ref) = refs[:nr], refs[nr:nr + 3]
        landed_refs, sems = refs[nr + 3:2 * nr + 3], refs[2 * nr + 3:]

        @pl.when(pl.program_id(0) == 0)
        def _():
            acc_ref[...] = jnp.zeros_like(acc_ref)
            if nr:
                _exchange_start(ready_refs, landed_refs, *sems, True)

        if nr:
            @pl.when(pl.program_id(0) == nt - 1)
            def _():
                _exchange_wait(ready_refs, landed_refs, *sems, True)

        dn4 = jnp.zeros((ts, d), F32)
        for j in range(N_DEV):
            src = dq_ref if j < per else dz2_ref
            jj = j % per
            dn4 = dn4 + _dot_nt(src[:, jj * cw:(jj + 1) * cw], wbin_ref[j])
        dn3 = _dot_nt(dkv_ref[...], wkv_ref[...])
        hn, r = _rms(h1_ref[...])
        acc_ref[0:1, :] += jnp.sum(dn4 * hn, axis=0, keepdims=True)
        acc_ref[1:2, :] += jnp.sum(dn3 * hn, axis=0, keepdims=True)
        dh1 = dh2_ref[...] + _rms_bwd(dn4 * bpre_ref[...] + dn3 * kvn_ref[...], hn, r)
        dh1_ref[...] = dh1
        yan, r2 = _rms(ya_ref[...])
        acc_ref[2:3, :] += jnp.sum(dh1 * yan, axis=0, keepdims=True)
        dya_ref[...] = _rms_bwd(dh1 * sm_ref[4:5, :], yan, r2).astype(BF16)

    outs = pl.pallas_call(
        body,
        name="layer_b_in_bwd",
        grid=(nt,),
        in_specs=[_rows(ts, d), _rows(ts, aw), _rows(ts, aw), _rows(ts, kvw), _rows(ts, d), _rows(ts, d),
                  _full(wbin_g.shape), _full(wkv.shape), _full(kvn.shape), _full(bpre.shape), _full(sm.shape)]
        + [HBM_SPEC] * nr,
        out_specs=[_rows(ts, d), _rows(ts, d), _resident((8, d))] + [HBM_SPEC] * nr,
        out_shape=[jax.ShapeDtypeStruct((seq, d), F32), jax.ShapeDtypeStruct((seq, d), BF16),
                   jax.ShapeDtypeStruct((8, d), F32)] + [jax.ShapeDtypeStruct(g.shape, g.dtype) for g in ready],
        scratch_shapes=_exchange_sems(nr),
        compiler_params=_params(("arbitrary",), 48),
    )(dh2, dq, dz2, dkv, h1, ya, wbin_g, wkv, kvn, bpre, sm, *ready)
    return outs[:3], outs[3:]


def _layer_a_bwd(dya, proj, conv, dh1, x2, wout, win_g, sm, ts):
    seq, d = x2.shape
    width = wout.shape[0]
    half = win_g.shape[2]
    n_half = width // half
    nt = seq // ts

    def body(dya_ref, proj_ref, conv_ref, dh1_ref, x_ref, wout_ref, win_ref, sm_ref, dproj_ref, gx_ref, acc_ref,
             dnext_ref):
        @pl.when(pl.program_id(0) == 0)
        def _():
            acc_ref[...] = jnp.zeros_like(acc_ref)
            dnext_ref[...] = jnp.zeros_like(dnext_ref)

        dy = _dot_nt(dya_ref[...], wout_ref[...])
        row = lax.broadcasted_iota(jnp.int32, (ts, half), 0)
        dn1 = jnp.zeros((ts, d), F32)
        for hh in range(n_half):
            cols = slice(hh * half, (hh + 1) * half)
            b, c, u, z = [proj_ref[:, (part * n_half + hh) * half:(part * n_half + hh + 1) * half].astype(F32)
                          for part in range(4)]
            cv = conv_ref[:, cols].astype(F32)
            dyh = dy[:, cols]
            sz, dsz = _silu(z)
            dconv = dyh * b * sz
            grads = [dyh * cv * sz, None, None, dyh * b * cv * dsz]
            next0, next1 = dnext_ref[0:1, cols], dnext_ref[1:2, cols]
            dc1 = jnp.where(row == ts - 1, next0, pltpu.roll(dconv, ts - 1, 0))
            dc2 = jnp.where(row == ts - 1, next1, jnp.where(row == ts - 2, next0, pltpu.roll(dconv, ts - 2, 0)))
            dnext_ref[:, cols] = dconv[0:8, :]
            v = c * u
            acc_ref[1:2, cols] += jnp.sum(dc2 * v, axis=0, keepdims=True)
            acc_ref[2:3, cols] += jnp.sum(dc1 * v, axis=0, keepdims=True)
            acc_ref[3:4, cols] += jnp.sum(dconv * v, axis=0, keepdims=True)
            dv = sm_ref[3:4, cols] * dconv + sm_ref[2:3, cols] * dc1 + sm_ref[1:2, cols] * dc2
            grads[1] = dv * u
            grads[2] = dv * c
            for part in range(4):
                j = part * n_half + hh
                gj = grads[part].astype(BF16)
                dproj_ref[:, j * half:(j + 1) * half] = gj
                dn1 = dn1 + _dot_nt(gj, win_ref[j])
        xn, r = _rms(x_ref[...])
        acc_ref[0:1, :] += jnp.sum(dn1 * xn, axis=0, keepdims=True)
        gx_ref[...] = dh1_ref[...] + _rms_bwd(dn1 * sm_ref[0:1, :], xn, r)

    rev = lambda w: pl.BlockSpec((ts, w), lambda i: (nt - 1 - i, 0))
    return pl.pallas_call(
        body,
        name="layer_a_bwd",
        grid=(nt,),
        in_specs=[rev(d), rev(4 * width), rev(width), rev(d), rev(d), _full(wout.shape), _full(win_g.shape), _full(sm.shape)],
        out_specs=[rev(4 * width), rev(d), _resident((8, d))],
        out_shape=[jax.ShapeDtypeStruct((seq, 4 * width), BF16), jax.ShapeDtypeStruct((seq, d), F32),
                   jax.ShapeDtypeStruct((8, d), F32)],
        scratch_shapes=[pltpu.VMEM((8, width), F32)],
        compiler_params=_params(("arbitrary",), 56),
    )(dya, proj, conv, dh1, x2, wout, win_g, sm)


def _wgrad(a, bs, n_slots, ts, name, ready=(), block_cols=1024):
    nr = len(ready)
    seq, k = a.shape
    nb_in = len(bs)
    n_each = bs[0].shape[1]
    n = nb_in * n_each
    bn = min(n_each, block_cols)
    per_in = n_each // bn
    n_blocks = nb_in * per_in
    ns = seq // ts

    def b_spec(idx):
        def index(j, s):
            mine = j // per_in == idx
            row = jnp.where(mine, s, jnp.where(j // per_in > idx, ns - 1, 0))
            return (row, jnp.where(mine, j % per_in, jnp.where(j // per_in > idx, per_in - 1, 0)))
        return pl.BlockSpec((ts, bn), index)

    if n_slots:
        sw = n // n_slots
        spb = bn // sw
        out_shape = jax.ShapeDtypeStruct((n_slots, k, sw), BF16)
        out_spec = pl.BlockSpec((spb, k, sw), lambda j, s: (j, 0, 0))
    else:
        out_shape = jax.ShapeDtypeStruct((k, n), BF16)
        out_spec = pl.BlockSpec((k, bn), lambda j, s: (0, j))

    def body(a_ref, *refs):
        b_refs, ready_refs, o_ref = refs[:nb_in], refs[nb_in:nb_in + nr], refs[nb_in + nr]
        landed_refs, (acc_ref, *sems) = refs[nb_in + nr + 1:nb_in + 2 * nr + 1], refs[nb_in + 2 * nr + 1:]
        j, s = pl.program_id(0), pl.program_id(1)

        if nr:
            @pl.when(jnp.logical_and(j == 0, s == 0))
            def _():
                _exchange_start(ready_refs, landed_refs, *sems, True)

            @pl.when(jnp.logical_and(j == n_blocks - 1, s == ns - 1))
            def _():
                _exchange_wait(ready_refs, landed_refs, *sems, True)

        @pl.when(s == 0)
        def _():
            acc_ref[...] = jnp.zeros_like(acc_ref)

        for idx in range(nb_in):
            @pl.when(j // per_in == idx)
            def _(idx=idx):
                acc_ref[...] += _dot_tn(a_ref[...], b_refs[idx][...])

        @pl.when(s == ns - 1)
        def _():
            if n_slots:
                for e in range(spb):
                    o_ref[e] = acc_ref[:, e * sw:(e + 1) * sw].astype(BF16)
            else:
                o_ref[...] = acc_ref[...].astype(BF16)

    outs = pl.pallas_call(
        body,
        name=name,
        grid=(n_blocks, ns),
        in_specs=[pl.BlockSpec((ts, k), lambda j, s: (s, 0))] + [b_spec(idx) for idx in range(nb_in)] + [HBM_SPEC] * nr,
        out_specs=[out_spec] + [HBM_SPEC] * nr,
        out_shape=[out_shape] + [jax.ShapeDtypeStruct(g.shape, g.dtype) for g in ready],
        scratch_shapes=[pltpu.VMEM((k, bn), F32)] + (_exchange_sems(nr) if nr else []),
        compiler_params=_params(("arbitrary", "arbitrary"), 48),
    )(a, *bs, *ready)
    return (outs[0], outs[1:]) if nr else outs[0]


def _wgrad_tail(pairs, part, landed, ts):
    n_tasks = len(pairs)
    assert n_tasks == 2
    nl = len(landed)
    seq, k = pairs[0][0].shape
    n = pairs[0][1].shape[1]
    ns = seq // ts
    total = n_tasks * ns
    per = k // N_DEV
    n_red = len(_chip_reduce_scratch((per, n)))

    def spec(t, width):
        return pl.BlockSpec((ts, width), lambda j, s: (jnp.where(j == t, s, jnp.where(j > t, ns - 1, 0)), 0))

    def body(*refs):
        ab_refs, part_hbm = refs[:2 * n_tasks], refs[2 * n_tasks]
        landed_hbm, refs = refs[2 * n_tasks + 1:2 * n_tasks + 1 + nl], refs[2 * n_tasks + 1 + nl:]
        o_ref, red_ref, early_ref = refs[:3]
        summed_refs, (acc_ref, first_ref, part_ref, *scratch) = refs[3:3 + nl], refs[3 + nl:]
        landed_refs, load_sems, scratch = scratch[:nl], scratch[nl], scratch[nl + 1:]
        j, s = pl.program_id(0), pl.program_id(1)
        flat = j * ns + s
        swap, send, forward, finish = _chip_reduce(part_ref, red_ref, *scratch[:3], scratch[3:n_red], part_hbm)
        swap_first, send_first, forward_first, finish_first = _chip_reduce(
            first_ref, early_ref, *scratch[n_red:n_red + 3], scratch[n_red + 3:])
        loads = [pltpu.make_async_copy(src, dst, load_sems.at[i])
                 for i, (src, dst) in enumerate(zip([part_hbm, *landed_hbm], [part_ref, *landed_refs]))]

        @pl.when(flat == 0)
        def _():
            swap()
            for load in loads:
                load.start()

        @pl.when(flat == min(1, total - 1))
        def _():
            loads[0].wait()
            send()

        @pl.when(flat == min(total // 2 + 1, total - 1))
        def _():
            forward()
            for t in range(nl):
                loads[1 + t].wait()
                _sum_slots(landed_refs[t], summed_refs[t])

        @pl.when(flat == ns)
        def _():
            send_first()

        @pl.when(flat == min(ns + ns // 2, total - 1))
        def _():
            forward_first()

        @pl.when(s == 0)
        def _():
            acc_ref[...] = jnp.zeros_like(acc_ref)

        for t in range(n_tasks):
            @pl.when(j == t)
            def _(t=t):
                acc_ref[...] += _dot_tn(ab_refs[2 * t][...], ab_refs[2 * t + 1][...])

        @pl.when(flat == ns - 1)
        def _():
            for dev in range(N_DEV):
                first_ref[dev] = acc_ref[dev * per:(dev + 1) * per, :].astype(BF16)
            swap_first()

        @pl.when(flat == total - 1)
        def _():
            for dev in range(N_DEV):
                o_ref[dev] = acc_ref[dev * per:(dev + 1) * per, :].astype(BF16)
            finish()
            finish_first()

    slot = part.shape[1:]
    outs = pl.pallas_call(
        body,
        name="wgrad_tail",
        grid=(n_tasks, ns),
        in_specs=[spec(t, w) for t in range(n_tasks) for w in (k, n)] + [HBM_SPEC] * (1 + nl),
        out_specs=[_resident((N_DEV, per, n)), _resident(slot), _resident((per, n))]
        + [_resident(g.shape[1:]) for g in landed],
        out_shape=[jax.ShapeDtypeStruct((N_DEV, per, n), BF16), jax.ShapeDtypeStruct(slot, F32),
                   jax.ShapeDtypeStruct((per, n), F32)]
        + [jax.ShapeDtypeStruct(g.shape[1:], F32) for g in landed],
        scratch_shapes=[pltpu.VMEM((k, n), F32), pltpu.VMEM((N_DEV, per, n), BF16), pltpu.VMEM(part.shape, part.dtype)]
        + [pltpu.VMEM(g.shape, g.dtype) for g in landed] + [pltpu.SemaphoreType.DMA((1 + nl,))]
        + _chip_reduce_scratch(slot) + _chip_reduce_scratch((per, n)),
        compiler_params=_params(("arbitrary", "arbitrary")),
    )(*[op for pair in pairs for op in pair], part, *landed)
    return outs[0], outs[1], outs[2], outs[3:]


MINE = "mine"
ADAMW_STEPS = 4


def _adamw(ws, sources, picks, loss_at, ms, vs):
    n, n_src = len(ws), len(sources)
    streamed = [len(w.shape) == 2 and w.shape[0] >= 128 and picks[t][1:] == (0, None)
                and sources[picks[t][0]].shape == w.shape for t, w in enumerate(ws)]
    streamed_sources = {picks[t][0] for t in range(n) if streamed[t]}

    def step(w, g, m, v):
        m = ADAM_B1 * m + (1.0 - ADAM_B1) * g
        v = ADAM_B2 * v + (1.0 - ADAM_B2) * jnp.square(g)
        m_hat = m / (1.0 - ADAM_B1 ** ADAM_STEP)
        v_hat = v / (1.0 - ADAM_B2 ** ADAM_STEP)
        return g, -ADAM_LR * (m_hat / (jnp.sqrt(v_hat) + ADAM_EPS) + ADAM_WD * w), m, v

    def body(*refs):
        refs = list(refs)
        take = lambda k: [refs.pop(0) for _ in range(k)]
        w_refs, s_refs, m_refs, v_refs = take(n), take(n_src), take(n), take(n)
        (loss_ref,), go_refs, d_refs, nm_refs, nv_refs = take(1), take(n), take(n), take(n), take(n)
        me = _my_index()

        def grad(t, rows):
            k, first, cols = picks[t]
            if cols is None:
                return s_refs[k][rows, :]
            if cols is not MINE:
                return s_refs[k][rows, cols]
            width = w_refs[t].shape[-1]
            g = s_refs[k][rows, 0:width]
            for dev in range(1, N_DEV):
                g = jnp.where(me == dev, s_refs[k][rows, dev * width:(dev + 1) * width], g)
            return g

        def whole(t):
            first = picks[t][1]
            rows = w_refs[t].shape[0]
            if len(w_refs[t].shape) == 3:
                for j in range(rows):
                    go_refs[t][j], d_refs[t][j], nm_refs[t][j], nv_refs[t][j] = step(
                        w_refs[t][j], grad(t, slice(first + j, first + j + 1)), m_refs[t][j], v_refs[t][j])
                return
            go_refs[t][...], d_refs[t][...], nm_refs[t][...], nv_refs[t][...] = step(
                w_refs[t][...], grad(t, slice(first, first + rows)), m_refs[t][...], v_refs[t][...])

        def block(t):
            rows = w_refs[t].shape[0]
            chunk = min(rows, 128)

            def one(i, carry):
                r = pl.ds(pl.multiple_of(i * chunk, chunk), chunk)
                go_refs[t][r, :], d_refs[t][r, :], nm_refs[t][r, :], nv_refs[t][r, :] = step(
                    w_refs[t][r, :], grad(t, r), m_refs[t][r, :], v_refs[t][r, :])
                return carry

            lax.fori_loop(0, rows // chunk, one, 0)

        @pl.when(pl.program_id(0) == 0)
        def _():
            loss_ref[...] = s_refs[loss_at[0]][loss_at[1]:loss_at[1] + 1, 0:1]
            for t in range(n):
                if not streamed[t]:
                    whole(t)

        for t in range(n):
            if streamed[t]:
                block(t)

    def rows_of(shape):
        return pl.BlockSpec((shape[0] // ADAMW_STEPS, shape[1]), lambda i: (i, 0))

    w_in = [rows_of(w.shape) if streamed[t] else _full(w.shape) for t, w in enumerate(ws)]
    w_out = [rows_of(w.shape) if streamed[t] else _resident(w.shape) for t, w in enumerate(ws)]
    s_in = [rows_of(s.shape) if k in streamed_sources else _full(s.shape) for k, s in enumerate(sources)]
    outs = pl.pallas_call(
        body,
        name="adamw",
        grid=(ADAMW_STEPS,),
        in_specs=w_in + s_in + w_in * 2,
        out_specs=[_resident((1, 1))] + w_out * 4,
        out_shape=[jax.ShapeDtypeStruct((1, 1), F32)] + [jax.ShapeDtypeStruct(w.shape, F32) for w in ws] * 4,
        compiler_params=_params(("arbitrary",)),
    )(*ws, *sources, *ms, *vs)
    return outs[0], outs[1:n + 1], outs[n + 1:2 * n + 1], outs[2 * n + 1:3 * n + 1], outs[3 * n + 1:]


def _band_structure():
    q_loc = np.arange(BLOCK, dtype=np.int32)[:, None]
    s_loc = np.arange(2 * BLOCK, dtype=np.int32)[None, :]
    dist = q_loc + BLOCK - s_loc
    in_window = (dist >= 0) & (dist < BLOCK)
    dd = np.maximum(dist, 0)
    max_exact = N_BUCKETS // 2
    large = max_exact + (np.log(np.maximum(dd, 1) / max_exact) / math.log(MAX_DISTANCE / max_exact)
                         * (N_BUCKETS - max_exact)).astype(np.int32)
    bucket = np.where(dd < max_exact, dd, np.minimum(large, N_BUCKETS - 1)).astype(np.int32)
    return bucket, in_window.astype(np.int32)


def kernel(x, a_pre_norm, a_w_in, a_conv_w, a_w_out, a_post_norm, kv_norm, w_kv, rel_bias, b_pre_norm, b_w_in, b_sinks, b_w_out, b_post_norm, loss_target, m_a_pre_norm, m_a_w_in, m_a_conv_w, m_a_w_out, m_a_post_norm, m_kv_norm, m_w_kv, m_rel_bias, m_b_pre_norm, m_b_w_in, m_b_sinks, m_b_w_out, m_b_post_norm, v_a_pre_norm, v_a_w_in, v_a_conv_w, v_a_w_out, v_a_post_norm, v_kv_norm, v_w_kv, v_rel_bias, v_b_pre_norm, v_b_w_in, v_b_sinks, v_b_w_out, v_b_post_norm):
    seq, d = x.shape[1], x.shape[2]
    x2 = x.reshape(seq, d)
    target = loss_target.reshape(seq, d)
    shard = a_pre_norm.shape[1]
    ts_a = min(seq, 512)
    ts = min(seq, 512)
    ts_w = min(seq, 2048)

    taps = lambda a: a.transpose(1, 0, 2)
    bucket, in_window = _band_structure()
    (win_g, wout_g), small_g, later, biasm = _all_gather(
        [a_w_in[0], a_w_out[0]], [(0, a_pre_norm), (1, taps(a_conv_w)), (4, a_post_norm)],
        [w_kv, b_w_in[0], b_w_out[0]], rel_bias.T, bucket.T, in_window.T)
    wout = wout_g.reshape(-1, wout_g.shape[2])
    sm = small_g.transpose(1, 0, 2).reshape(8, N_DEV * shard)
    kvn = kv_norm.reshape(1, d)

    (h1, n1, proj, conv, y, ya), (wkv_g, wbin_g, wbout_g) = _layer_a_fwd(x2, sm, win_g, wout, later, ts_a)
    wkv = wkv_g.reshape(-1, wkv_g.shape[2])
    wbout = wbout_g.reshape(-1, wbout_g.shape[2])
    n3, n4, kv, q, o, dh2, dyb, dattn, dz2, acc_c = _layer_b_fwd(
        h1, target, kvn, b_pre_norm, wkv, wbin_g, biasm, b_sinks, wbout, b_post_norm)

    (dq, dkv, dssum, dsink), _ = _attn_bwd(q, kv, dattn, biasm, b_sinks, [])
    by_head = dssum.reshape(N_PAIRS, BAND, 2, BLOCK).transpose(0, 2, 3, 1)
    g_wkv = _wgrad(n3, [dkv], 0, ts_w, "wgrad_kv").reshape(wkv_g.shape)
    g_wbin = _wgrad(n4, [dq, dz2], N_DEV, ts_w, "wgrad_b_in")
    (dh1, dya, acc_b), _ = _layer_b_in_bwd(dh2, dq, dz2, dkv, h1, ya, wbin_g, wkv, kvn, b_pre_norm, sm, [], ts)
    dproj, gx, acc_a = _layer_a_bwd(dya, proj, conv, dh1, x2, wout, win_g, sm, ts_a)
    g_win, (l_wkv, l_wbin) = _wgrad(
        n1, [dproj], N_DEV, ts_w, "wgrad_a_in", ready=[g_wkv, g_wbin], block_cols=2048)
    g_wbout, r_win, r_wout, (r_wkv, r_wbin) = _wgrad_tail(
        [(y, dya), (o, dyb)], g_win, [l_wkv, l_wbin], min(seq, 1024))

    r_wbout, _, (s_a, s_b, s_c, s_sink), s_relb = _reduce_exchange(
        g_wbout, [], [acc_a, acc_b, acc_c, dsink], by_head.reshape(N_Q_HEADS, -1), bucket.reshape(1, -1), 4096)
    weights = [a_pre_norm, a_w_in[0], taps(a_conv_w), a_w_out[0], a_post_norm, kvn, w_kv, rel_bias.T, b_pre_norm,
               b_w_in[0], b_sinks, b_w_out[0], b_post_norm]
    sources = [s_a, s_b, s_c, s_relb, s_sink, r_win, r_wout, r_wkv, r_wbin, r_wbout]
    picks = [(0, 0, MINE), (5, 0, None), (0, 1, MINE), (6, 0, None), (1, 2, MINE), (1, 1, None), (7, 0, None),
             (3, 0, slice(0, N_BUCKETS)), (1, 0, None), (8, 0, None), (4, 0, slice(0, N_Q_HEADS)),
             (9, 0, None), (2, 0, None)]
    first = [m_a_pre_norm, m_a_w_in[0], taps(m_a_conv_w), m_a_w_out[0], m_a_post_norm, m_kv_norm.reshape(1, d),
             m_w_kv, m_rel_bias.T, m_b_pre_norm, m_b_w_in[0], m_b_sinks, m_b_w_out[0], m_b_post_norm]
    second = [v_a_pre_norm, v_a_w_in[0], taps(v_a_conv_w), v_a_w_out[0], v_a_post_norm, v_kv_norm.reshape(1, d),
              v_w_kv, v_rel_bias.T, v_b_pre_norm, v_b_w_in[0], v_b_sinks, v_b_w_out[0], v_b_post_norm]
    loss, grads, deltas, new_m, new_v = _adamw(weights, sources, picks, (2, 1), first, second)

    shapes = [a_pre_norm.shape, a_w_in.shape, taps, a_w_out.shape, a_post_norm.shape, kv_norm.shape,
              w_kv.shape, jnp.transpose, b_pre_norm.shape, b_w_in.shape, b_sinks.shape, b_w_out.shape, b_post_norm.shape]
    shaped = lambda arrays: [s(a) if callable(s) else a.reshape(s) for a, s in zip(arrays, shapes)]
    return (loss.reshape(()), gx.reshape(x.shape), *shaped(grads), *shaped(deltas), *shaped(new_m), *shaped(new_v))
```

```python
import math

import jax
import jax.numpy as jnp
import numpy as np
from jax import lax
from jax.experimental import pallas as pl
from jax.experimental.pallas import tpu as pltpu

HEAD_DIM = 64
N_Q_HEADS = 16
N_KV_HEADS = 2
GROUP = N_Q_HEADS // N_KV_HEADS
BLOCK = 128
N_BUCKETS = 32
MAX_DISTANCE = 128
EPS = 1e-6
NEG_INF = -1e30
SCALE = HEAD_DIM ** -0.5

ADAM_LR = 0.001
ADAM_B1 = 0.9
ADAM_B2 = 0.999
ADAM_EPS = 1e-08
ADAM_WD = 0.01
ADAM_STEP = 10

N_PAIRS = N_Q_HEADS // 2
BAND = 2 * BLOCK

N_DEV = 8
GATHER_PIECE_ROWS = 256
LANES = 128
F32 = jnp.float32
BF16 = jnp.bfloat16
MESH = pl.DeviceIdType.MESH
MIB = 1024 * 1024
VMEM_RESERVED_MIB = 63


def _params(semantics=None):
    return pltpu.CompilerParams(dimension_semantics=semantics, vmem_limit_bytes=VMEM_RESERVED_MIB * MIB)


def _full(shape):
    zeros = (0,) * len(shape)
    return pl.BlockSpec(shape, lambda *_: zeros, pipeline_mode=pl.Buffered(1))


def _resident(shape):
    zeros = (0,) * len(shape)
    return pl.BlockSpec(shape, lambda *_: zeros)


def _rows(ts, cols):
    return pl.BlockSpec((ts, cols), lambda i: (i, 0))


def _dot(a, b):
    return jnp.dot(a, b, preferred_element_type=F32)


def _dot_nt(a, b):
    return lax.dot_general(a, b, (((1,), (1,)), ((), ())), preferred_element_type=F32)


def _dot_tn(a, b):
    return lax.dot_general(a, b, (((0,), (0,)), ((), ())), preferred_element_type=F32)


def _rms(xf):
    r = lax.rsqrt(jnp.mean(xf * xf, axis=-1, keepdims=True) + EPS)
    return xf * r, r


def _rms_bwd(dn, xn, r):
    return r * (dn - xn * jnp.mean(dn * xn, axis=-1, keepdims=True))


def _silu(z):
    s = jax.nn.sigmoid(z)
    return z * s, s * (1.0 + z * (1.0 - s))


def _my_index():
    return 4 * lax.axis_index("x") + 2 * lax.axis_index("y") + lax.axis_index("c")


def _bias_table(rb_ref, bucket_ref, win_ref, out_ref):
    bk = jnp.where(win_ref[...] != 0, bucket_ref[...], -1)
    has_prev = lax.broadcasted_iota(jnp.int32, bk.shape, 0) >= BLOCK
    for h in range(N_Q_HEADS):
        acc = jnp.full(bk.shape, NEG_INF, F32)
        for b in range(N_BUCKETS):
            acc = jnp.where(bk == b, rb_ref[h, b], acc)
        cols = slice((h % 2) * BLOCK, (h % 2 + 1) * BLOCK)
        out_ref[1, h // 2, :, cols] = acc
        out_ref[0, h // 2, :, cols] = jnp.where(has_prev, acc, NEG_INF)


def _all_gather(shards, small_rows, casts, rel_bias_t, bucket_t, in_window_t):
    ns, nc, n = len(small_rows), len(casts), len(shards) + 1
    small_shape = (8, small_rows[0][1].shape[-1])
    shapes = [s.shape for s in shards] + [small_shape]
    pieces = [(t, r0, min(GATHER_PIECE_ROWS, shape[0] - r0))
              for t, shape in enumerate(shapes) for r0 in range(0, shape[0], GATHER_PIECE_ROWS)]

    def body(*refs):
        refs = list(refs)
        take = lambda k: [refs.pop(0) for _ in range(k)]
        ins, small_refs, cast_refs, (rb_ref, bucket_ref, win_ref) = take(n - 1), take(ns), take(nc), take(3)
        outs, cast_outs, (bias_ref, send_sems, recv_sems) = take(n), take(nc), take(3)
        x, y, c = lax.axis_index("x"), lax.axis_index("y"), lax.axis_index("c")
        me, sibling = (x, y, c), (x, y, 1 - c)
        x_nbr, y_nbr, diagonal = (1 - x, y), (x, 1 - y), (1 - x, 1 - y)
        south = c == 0
        relayed = (jnp.where(south, 1 - x, x), jnp.where(south, y, 1 - y))
        relay_to = (jnp.where(south, x, 1 - x), jnp.where(south, 1 - y, y))

        def copy(u, k, block, to):
            t, r0, nrows = pieces[u]
            rows = outs[t].at[4 * block[0] + 2 * block[1] + block[2], pl.ds(r0, nrows)]
            return pltpu.make_async_remote_copy(
                src_ref=rows, dst_ref=rows, send_sem=send_sems.at[u, k], recv_sem=recv_sems.at[u, k],
                device_id=to, device_id_type=MESH)

        mine = pl.ds(_my_index(), 1)
        for t in range(n - 1):
            outs[t][mine] = ins[t][...].astype(BF16)[None]
        outs[n - 1][mine] = jnp.zeros((1,) + small_shape, F32)
        for (row, _), ref in zip(small_rows, small_refs):
            if len(ref.shape) == 3:
                for j in range(ref.shape[0]):
                    outs[n - 1][mine, row + j:row + j + 1, :] = ref[j][None]
            else:
                outs[n - 1][mine, row:row + ref.shape[0], :] = ref[...][None]
        started = []

        def start(cp):
            cp.start()
            started.append(cp)

        units = range(len(pieces))
        for u in units:
            start(copy(u, 0, me, sibling))
            start(copy(u, 1, me, (*x_nbr, c)))
            start(copy(u, 2, me, (*y_nbr, c)))
        for src, dst in zip(cast_refs, cast_outs):
            dst[...] = src[...].astype(BF16)
        _bias_table(rb_ref, bucket_ref, win_ref, bias_ref)
        for u in units:
            for k, chip in ((1, x_nbr), (2, y_nbr)):
                copy(u, k, (*chip, c), me).wait_recv()
                start(copy(u, 3 + k, (*chip, c), sibling))
            start(copy(u, 3, (*relayed, c), (*relay_to, c)))
        for u in units:
            copy(u, 3, (*diagonal, c), me).wait_recv()
            start(copy(u, 6, (*diagonal, c), sibling))
        for u in units:
            copy(u, 0, sibling, me).wait_recv()
        for k, chip in ((4, x_nbr), (5, y_nbr), (6, diagonal)):
            for u in units:
                copy(u, k, (*chip, 1 - c), me).wait_recv()
        for cp in started:
            cp.wait_send()

    vmem = pl.BlockSpec(memory_space=pltpu.VMEM)
    outs = pl.pallas_call(
        body,
        name="gather_weights",
        out_shape=[jax.ShapeDtypeStruct((N_DEV,) + s.shape, BF16) for s in shards]
        + [jax.ShapeDtypeStruct((N_DEV,) + small_shape, F32)]
        + [jax.ShapeDtypeStruct(a.shape, BF16) for a in casts]
        + [jax.ShapeDtypeStruct((2, N_PAIRS, BAND, 2 * BLOCK), F32)],
        in_specs=[vmem] * (n - 1 + ns + nc) + [pl.BlockSpec(memory_space=pltpu.SMEM), vmem, vmem],
        out_specs=[vmem] * (n + nc + 1),
        scratch_shapes=[pltpu.SemaphoreType.DMA((len(pieces), 7)), pltpu.SemaphoreType.DMA((len(pieces), 7))],
        compiler_params=_params(),
    )(*shards, *[a for _, a in small_rows], *casts, rel_bias_t, bucket_t, in_window_t)
    return outs[:n - 1], outs[n - 1], outs[n:n + nc], outs[n + nc]


def _peer(k):
    x, y, c = lax.axis_index("x"), lax.axis_index("y"), lax.axis_index("c")
    px = 1 - x if k & 4 else x
    py = 1 - y if k & 2 else y
    pc = 1 - c if k & 1 else c
    return (px, py, pc), 4 * px + 2 * py + pc


def _exchange(srcs, dsts, send_sems, recv_sems, local_sems, scatter):
    me = _my_index()
    sends, arrivals = [], []
    for k in range(1, N_DEV):
        peer, pidx = _peer(k)
        for t, (src, dst) in enumerate(zip(srcs, dsts)):
            mine = src.at[pidx] if scatter else src
            sems = dict(send_sem=send_sems.at[t, k - 1], recv_sem=recv_sems.at[t, k - 1], device_id=peer, device_id_type=MESH)
            sends.append(pltpu.make_async_remote_copy(src_ref=mine, dst_ref=dst.at[me], **sems))
            arrivals.append(pltpu.make_async_remote_copy(src_ref=mine, dst_ref=dst.at[pidx], **sems))
    local = [pltpu.make_async_copy(src.at[me] if scatter else src, dst.at[me], local_sems.at[t])
             for t, (src, dst) in enumerate(zip(srcs, dsts))]
    return sends, arrivals, local


def _exchange_start(*args):
    sends, _, local = _exchange(*args)
    for cp in sends + local:
        cp.start()


def _exchange_wait(*args):
    sends, arrivals, local = _exchange(*args)
    for cp in arrivals:
        cp.wait_recv()
    for cp in sends:
        cp.wait_send()
    for cp in local:
        cp.wait()


def _exchange_sems(n):
    if not n:
        return []
    return [pltpu.SemaphoreType.DMA((n, N_DEV - 1)), pltpu.SemaphoreType.DMA((n, N_DEV - 1)), pltpu.SemaphoreType.DMA((n,))]


HBM_SPEC = pl.BlockSpec(memory_space=pl.ANY)


def _sum_slots(recv_ref, out_ref):
    rows = out_ref.shape[0]
    chunk = min(rows, 128)

    def add(i, carry):
        r0 = pl.multiple_of(i * chunk, chunk)
        acc = recv_ref[0, pl.ds(r0, chunk), :].astype(F32)
        for dev in range(1, N_DEV):
            acc = acc + recv_ref[dev, pl.ds(r0, chunk), :].astype(F32)
        out_ref[pl.ds(r0, chunk), :] = acc
        return carry

    lax.fori_loop(0, rows // chunk, add, 0)


N_CHIPS = N_DEV // 2


def _rows_loop(rows, fn):
    chunk = min(rows, 128)

    def step(i, carry):
        fn(pl.ds(pl.multiple_of(i * chunk, chunk), chunk))
        return carry

    lax.fori_loop(0, rows // chunk, step, 0)


def _chip_reduce(g_ref, out_ref, sib_ref, land_ref, send_ref, sems, swap_src=None):
    sib_send, sib_recv, ici_send, ici_recv = sems
    x, y, c = lax.axis_index("x"), lax.axis_index("y"), lax.axis_index("c")
    south = c == 0
    near =(jnp.where(south, 1 - x, x), jnp.where(south, y, 1 - y))
    far = (jnp.where(south, x, 1 - x), jnp.where(south, 1 - y, y))
    diagonal = (1 - x, 1 - y)
    rows = out_ref.shape[0]
    direct, fold, folded = 0, 1, 2

    def to_sibling(t):
        src = g_ref if swap_src is None else swap_src
        return pltpu.make_async_remote_copy(
            src_ref=src.at[2 * t + 1 - c], dst_ref=sib_ref.at[t], send_sem=sib_send.at[t], recv_sem=sib_recv.at[t],
            device_id=(x, y, 1 - c), device_id_type=MESH)

    def ici(role, chip):
        return pltpu.make_async_remote_copy(
            src_ref=send_ref.at[role], dst_ref=land_ref.at[role], send_sem=ici_send.at[role],
            recv_sem=ici_recv.at[role], device_id=(*chip, c), device_id_type=MESH)

    def pair_sum(chip, r):
        t = 2 * chip[0] + chip[1]
        return g_ref[2 * t + c, r, :].astype(F32) + sib_ref[t, r, :].astype(F32)

    def swap():
        for t in range(N_CHIPS):
            to_sibling(t).start()

    def send():
        for t in range(N_CHIPS):
            to_sibling(t).wait_recv()
        for role, chip in ((fold, diagonal), (direct, near)):
            def fill(r, role=role, chip=chip):
                send_ref[role, r, :] = pair_sum(chip, r).astype(BF16)

            _rows_loop(rows, fill)
            ici(role, near).start()

    def forward():
        ici(fold, near).wait_recv()

        def fill(r):
            send_ref[folded, r, :] = (pair_sum(far, r) + land_ref[fold, r, :].astype(F32)).astype(BF16)

        _rows_loop(rows, fill)
        ici(folded, far).start()

    def finish():
        ici(direct, near).wait_recv()
        ici(folded, far).wait_recv()

        def total(r):
            mine = pair_sum((x, y), r)
            out_ref[r, :] = mine + land_ref[direct, r, :].astype(F32) + land_ref[folded, r, :].astype(F32)

        _rows_loop(rows, total)
        for t in range(N_CHIPS):
            to_sibling(t).wait_send()
        for role, chip in ((direct, near), (fold, near), (folded, far)):
            ici(role, chip).wait_send()

    return swap, send, forward, finish


def _chip_reduce_scratch(slot):
    return [pltpu.VMEM((N_CHIPS,) + slot, BF16), pltpu.VMEM((3,) + slot, BF16), pltpu.VMEM((3,) + slot, BF16),
            pltpu.SemaphoreType.DMA((N_CHIPS,)), pltpu.SemaphoreType.DMA((N_CHIPS,)),
            pltpu.SemaphoreType.DMA((3,)), pltpu.SemaphoreType.DMA((3,))]


def _bucket_sums(a_ref, bucket_ref, cols):
    a = a_ref[:, cols]
    hi = a.astype(BF16)
    lo = (a - hi.astype(F32)).astype(BF16)
    rows = lax.broadcasted_iota(jnp.int32, (LANES, a.shape[1]), 0)
    onehot_t = (rows == bucket_ref[:, cols]).astype(F32).astype(BF16)
    return _dot_nt(hi, onehot_t) + _dot_nt(lo, onehot_t)


def _reduce_exchange(part, landed, smalls, by_bucket, bucket_row, chunk):
    nl, ng = len(landed), len(smalls)
    n_in = 1 + nl + ng + 2
    n_out = 1 + nl + ng + 1
    heads, positions = by_bucket.shape
    chunks = [slice(c0, c0 + chunk) for c0 in range(0, positions, chunk)]

    def body(*refs):
        p_in, l_in, s_in, (a_ref, bucket_ref) = refs[0], refs[1:1 + nl], refs[1 + nl:n_in - 2], refs[n_in - 2:n_in]
        refs = refs[n_in:]
        p_out, l_out, s_out, b_out = refs[0], refs[1:1 + nl], refs[1 + nl:n_out - 1], refs[n_out - 1]
        scratch = refs[n_out:]
        s_recv, (b_recv, b_ref, sib_ref, chip_ref, send_ref), sems = scratch[:ng], scratch[ng:ng + 5], scratch[ng + 5:]
        swap, send, forward, finish = _chip_reduce(p_in, p_out, sib_ref, chip_ref, send_ref, sems[:4])
        swap()
        _exchange_start(s_in, s_recv, *sems[4:7], False)
        b_ref[...] = jnp.zeros_like(b_ref)
        for cols in chunks[:len(chunks) // 2]:
            b_ref[...] += _bucket_sums(a_ref, bucket_ref, cols)
        send()
        for t in range(nl):
            _sum_slots(l_in[t], l_out[t])
        for cols in chunks[len(chunks) // 2:]:
            b_ref[...] += _bucket_sums(a_ref, bucket_ref, cols)
        _exchange_start([b_ref], [b_recv], *sems[7:], False)
        forward()
        finish()
        _exchange_wait(s_in, s_recv, *sems[4:7], False)
        _exchange_wait([b_ref], [b_recv], *sems[7:], False)
        for recv, out in zip([*s_recv, b_recv], [*s_out, b_out]):
            acc = recv[0]
            for dev in range(1, N_DEV):
                acc = acc + recv[dev]
            out[...] = acc

    vmem = pl.BlockSpec(memory_space=pltpu.VMEM)
    slot = part.shape[1:]
    outs = pl.pallas_call(
        body,
        name="reduce_grads",
        out_shape=[jax.ShapeDtypeStruct(p.shape[1:], F32) for p in [part] + landed]
        + [jax.ShapeDtypeStruct(s.shape, F32) for s in smalls] + [jax.ShapeDtypeStruct((heads, LANES), F32)],
        in_specs=[vmem] * n_in,
        out_specs=[vmem] * n_out,
        scratch_shapes=[pltpu.VMEM((N_DEV,) + s.shape, F32) for s in smalls]
        + [pltpu.VMEM((N_DEV, heads, LANES), F32), pltpu.VMEM((heads, LANES), F32)] + _chip_reduce_scratch(slot)
        + _exchange_sems(ng) + _exchange_sems(1),
        compiler_params=_params(),
    )(part, *landed, *smalls, by_bucket, bucket_row)
    return outs[0], outs[1:1 + nl], outs[1 + nl:n_out - 1], outs[n_out - 1]


def _layer_a_fwd(x2, sm, win_g, wout, later, ts):
    seq, d = x2.shape
    width = wout.shape[0]
    half = win_g.shape[2]
    n_half = width // half
    nl = len(later)
    nt = seq // ts

    def body(x_ref, sm_ref, win_ref, wout_ref, *refs):
        shard_refs, refs = refs[:nl], refs[nl:]
        h1_ref, n1_ref, proj_ref, conv_ref, y_ref, ya_ref = refs[:6]
        gathered_refs, (vprev_ref, *sems) = refs[6:6 + nl], refs[6 + nl:]

        @pl.when(pl.program_id(0) == 0)
        def _():
            vprev_ref[...] = jnp.zeros_like(vprev_ref)
            _exchange_start(shard_refs, gathered_refs, *sems, False)

        @pl.when(pl.program_id(0) == nt - 1)
        def _():
            _exchange_wait(shard_refs, gathered_refs, *sems, False)

        xf = x_ref[...]
        xn, _ = _rms(xf)
        n1 = (xn * sm_ref[0:1, :]).astype(BF16)
        n1_ref[...] = n1
        row = lax.broadcasted_iota(jnp.int32, (ts, half), 0)
        ya = jnp.zeros((ts, d), F32)
        for hh in range(n_half):
            cols = slice(hh * half, (hh + 1) * half)
            parts = []
            for part in range(4):
                j = part * n_half + hh
                pj = _dot(n1, win_ref[j])
                proj_ref[:, j * half:(j + 1) * half] = pj.astype(BF16)
                parts.append(pj)
            b, c, u, z = parts
            v = c * u
            last1, last2 = vprev_ref[7:8, cols], vprev_ref[6:7, cols]
            v1 = jnp.where(row == 0, last1, pltpu.roll(v, 1, 0))
            v2 = jnp.where(row == 0, last2, jnp.where(row == 1, last1, pltpu.roll(v, 2, 0)))
            vprev_ref[:, cols] = v[ts - 8:ts, :]
            conv = sm_ref[1:2, cols] * v2 + sm_ref[2:3, cols] * v1 + sm_ref[3:4, cols] * v
            conv_ref[:, cols] = conv.astype(BF16)
            yh = (b * conv * _silu(z)[0]).astype(BF16)
            y_ref[:, cols] = yh
            ya = ya + _dot(yh, wout_ref[cols, :])
        ya_ref[...] = ya
        h1_ref[...] = xf + _rms(ya)[0] * sm_ref[4:5, :]

    outs = pl.pallas_call(
        body,
        name="layer_a_fwd",
        grid=(nt,),
        in_specs=[_rows(ts, d), _full(sm.shape), _full(win_g.shape), _full(wout.shape)] + [HBM_SPEC] * nl,
        out_specs=[_rows(ts, d), _rows(ts, d), _rows(ts, 4 * width), _rows(ts, width), _rows(ts, width), _rows(ts, d)]
        + [HBM_SPEC] * nl,
        out_shape=[
            jax.ShapeDtypeStruct((seq, d), F32),
            jax.ShapeDtypeStruct((seq, d), BF16),
            jax.ShapeDtypeStruct((seq, 4 * width), BF16),
            jax.ShapeDtypeStruct((seq, width), BF16),
            jax.ShapeDtypeStruct((seq, width), BF16),
            jax.ShapeDtypeStruct((seq, d), F32),
        ] + [jax.ShapeDtypeStruct((N_DEV,) + s.shape, s.dtype) for s in later],
        scratch_shapes=[pltpu.VMEM((8, width), F32)] + _exchange_sems(nl),
        compiler_params=_params(("arbitrary",)),
    )(x2, sm, win_g, wout, *later)
    return outs[:6], outs[6:]


Q_BLOCKS = 4
ATTN_BWD_LAGS = (2, 4)
ATTN_FWD_LAGS = (2, 4)


def _banded_tiles(kvp_ref, kvc_ref):
    tile = kvc_ref[...].astype(F32)
    blocks = [kvp_ref[...].astype(F32)] + [tile[u * BLOCK:(u + 1) * BLOCK] for u in range(Q_BLOCKS)]
    return [_banded_kv(blocks[u], blocks[u + 1]) for u in range(Q_BLOCKS)]


def _bias_of(bias_ref, i, u, m):
    return bias_ref[jnp.minimum(i, 1) if u == 0 else 1, m]


def _banded_kv(kvp, kvc):
    kw = N_KV_HEADS * HEAD_DIM
    out = []
    for full in (jnp.concatenate([kvp[:, :kw], kvc[:, :kw]], axis=0), jnp.concatenate([kvp[:, kw:], kvc[:, kw:]], axis=0)):
        lo = lax.broadcasted_iota(jnp.int32, full.shape, 1) < HEAD_DIM
        rolled = pltpu.roll(full, HEAD_DIM, 1)
        x2 = [jnp.where(lo, full, rolled).astype(BF16), jnp.where(lo, rolled, full).astype(BF16)]
        ft = full.T
        x2t = [jnp.concatenate([ft[kh * HEAD_DIM:(kh + 1) * HEAD_DIM]] * 2, axis=0).astype(BF16) for kh in range(N_KV_HEADS)]
        out += [x2, x2t]
    return out


def _pair_rows(ref, rows, m, scale=None):
    both = ref[rows, m * LANES:(m + 1) * LANES].astype(F32)
    if scale is not None:
        both = both * scale
    lo = lax.broadcasted_iota(jnp.int32, both.shape, 1) < HEAD_DIM
    zero = jnp.zeros_like(both)
    return jnp.concatenate([jnp.where(lo, both, zero), jnp.where(lo, zero, both)], axis=0).astype(BF16)


def _pair_cols(res_t):
    top = lax.broadcasted_iota(jnp.int32, (LANES, BLOCK), 0) < HEAD_DIM
    return jnp.where(top, res_t[:, :BLOCK], res_t[:, BLOCK:]).T


def _sink_row(sink_ref, m):
    first = lax.broadcasted_iota(jnp.int32, (1, 2 * BLOCK), 1) < BLOCK
    return jnp.where(first, sink_ref[0, 2 * m], sink_ref[0, 2 * m + 1])


def _softmax_t(logits, sink):
    mx =jnp.maximum(jnp.max(logits, axis=0, keepdims=True), sink)
    p = jnp.exp(logits - mx)
    sink_p = jnp.exp(sink - mx)
    inv = 1.0 / (jnp.sum(p, axis=0, keepdims=True) + sink_p)
    return p * inv, sink_p * inv


def _layer_b_fwd(h1, target, kvn, bpre, wkv, wbin_g, biasm, sinks, wbout, bpost):
    seq, d = h1.shape
    kvw = wkv.shape[1]
    cw = wbin_g.shape[2]
    aw = N_Q_HEADS * HEAD_DIM
    per = aw // cw
    tile = Q_BLOCKS * BLOCK

    def body(sink_ref, h1_ref, tgt_ref, kvn_ref, bpre_ref, wkv_ref, wbin_ref, bias_ref, w_ref, g_ref,
             n3_ref, n4_ref, kvc_ref, q_ref, o_ref, dh2_ref, dyb_ref, dattn_ref, dz2_ref, acc_ref,
             attn_ref, z2_ref, kvp_ref):
        i = pl.program_id(0)

        @pl.when(i == 0)
        def _():
            acc_ref[...] = jnp.zeros_like(acc_ref)
            kvp_ref[...] = jnp.zeros_like(kvp_ref)

        hn, _ = _rms(h1_ref[...])
        n3 = (hn * kvn_ref[...]).astype(BF16)
        n4 = (hn * bpre_ref[...]).astype(BF16)
        n3_ref[...] = n3
        n4_ref[...] = n4
        kvc_ref[...] = _dot(n3, wkv_ref[...]).astype(BF16)
        for j in range(N_DEV):
            pj = _dot(n4, wbin_ref[j])
            if j < per:
                q_ref[:, j * cw:(j + 1) * cw] = pj.astype(BF16)
            else:
                z2_ref[:, (j - per) * cw:(j - per + 1) * cw] = pj

        banded = _banded_tiles(kvp_ref, kvc_ref)
        kvp_ref[...] = kvc_ref[tile - BLOCK:tile, :]
        units = [(u, m) for u in range(Q_BLOCKS) for m in range(N_PAIRS)]
        kv_of = lambda m: (2 * m) // GROUP
        logits, probs = {}, {}
        lag_b, lag_c = ATTN_FWD_LAGS
        for step in range(len(units) + lag_c):
            if step < len(units):
                u, m = units[step]
                qpair = _pair_rows(q_ref, slice(u * BLOCK, (u + 1) * BLOCK), m, SCALE)
                logits[step] = _dot_nt(banded[u][0][kv_of(m)], qpair) + _bias_of(bias_ref, i, u, m)
            if 0 <= step - lag_b < len(units):
                u, m = units[step - lag_b]
                probs[step - lag_b] = _softmax_t(logits.pop(step - lag_b), _sink_row(sink_ref, m))[0].astype(BF16)
            if 0 <= step - lag_c < len(units):
                u, m = units[step - lag_c]
                out_t = _dot(banded[u][3][kv_of(m)], probs.pop(step - lag_c))
                attn_ref[u * BLOCK:(u + 1) * BLOCK, m * LANES:(m + 1) * LANES] = _pair_cols(out_t)
        attn = attn_ref[...]
        sz, dsz = _silu(z2_ref[...])
        o = (attn * sz).astype(BF16)
        o_ref[...] = o

        w = w_ref[...]
        yb = _dot(o, w)
        ybn, r = _rms(yb)
        g = g_ref[...]
        diff = h1_ref[...] + ybn * g - tgt_ref[...]
        dh2 = diff * (1.0 / d)
        dh2_ref[...] = dh2
        acc_ref[0:1, :] += jnp.sum(dh2 * ybn, axis=0, keepdims=True)
        tok = jnp.mean(diff * diff, axis=-1, keepdims=True)
        acc_ref[1:2, :] += 0.5 * jnp.sum(tok, axis=0, keepdims=True)
        dyb = _rms_bwd(dh2 * g, ybn, r).astype(BF16)
        dyb_ref[...] = dyb
        do = _dot_nt(dyb, w)
        dattn_ref[...] = (do * sz).astype(BF16)
        dz2_ref[...] = (do * attn * dsz).astype(BF16)

    blk = lambda w: pl.BlockSpec((tile, w), lambda i: (i, 0))
    return pl.pallas_call(
        body,
        name="layer_b_fwd",
        grid=(seq // tile,),
        in_specs=[
            pl.BlockSpec(memory_space=pltpu.SMEM),
            blk(d),
            blk(d),
            _full(kvn.shape),
            _full(bpre.shape),
            _full(wkv.shape),
            _full(wbin_g.shape),
            _full(biasm.shape),
            _full(wbout.shape),
            _full(bpost.shape),
        ],
        out_specs=[blk(d), blk(d), blk(kvw), blk(aw), blk(aw), blk(d), blk(d), blk(aw), blk(aw), _resident((8, d))],
        out_shape=[
            jax.ShapeDtypeStruct((seq, d), BF16),
            jax.ShapeDtypeStruct((seq, d), BF16),
            jax.ShapeDtypeStruct((seq, kvw), BF16),
            jax.ShapeDtypeStruct((seq, aw), BF16),
            jax.ShapeDtypeStruct((seq, aw), BF16),
            jax.ShapeDtypeStruct((seq, d), F32),
            jax.ShapeDtypeStruct((seq, d), BF16),
            jax.ShapeDtypeStruct((seq, aw), BF16),
            jax.ShapeDtypeStruct((seq, aw), BF16),
            jax.ShapeDtypeStruct((8, d), F32),
        ],
        scratch_shapes=[pltpu.VMEM((tile, aw), F32), pltpu.VMEM((tile, aw), F32), pltpu.VMEM((BLOCK, kvw), BF16)],
        compiler_params=_params(("arbitrary",)),
    )(sinks, h1, target, kvn, bpre, wkv, wbin_g, biasm, wbout, bpost)


def _attn_bwd(q, kv, dattn, biasm, sinks, ready):
    seq, aw = q.shape
    kvw = kv.shape[1]
    kw = N_KV_HEADS * HEAD_DIM
    nb = seq // BLOCK
    pairs_per_kv = N_PAIRS // N_KV_HEADS
    nr = len(ready)

    tile = Q_BLOCKS * BLOCK
    nsteps = seq // tile
    held = (Q_BLOCKS - 1) * BLOCK

    def body(sink_ref, q_ref, kvc_ref, kvp_ref, da_ref, bias_ref, *refs):
        ready_refs, (dq_ref, dkv_ref, dssum_ref, dsink_ref) = refs[:nr], refs[nr:nr + 4]
        landed_refs, scratch = refs[nr + 4:2 * nr + 4], refs[2 * nr + 4:]
        carry_ref, done_ref, qs_ref, dos_ref, dst_ref, pt_ref, *sems = scratch
        i = pl.program_id(0)

        @pl.when(i == 0)
        def _():
            dssum_ref[...] = jnp.zeros_like(dssum_ref)
            dsink_ref[...] = jnp.zeros_like(dsink_ref)
            carry_ref[...] = jnp.zeros_like(carry_ref)
            done_ref[...] = jnp.zeros_like(done_ref)
            if nr:
                _exchange_start(ready_refs, landed_refs, *sems, True)

        if nr:
            @pl.when(i == nsteps)
            def _():
                _exchange_wait(ready_refs, landed_refs, *sems, True)

        @pl.when(i < nsteps)
        def _():
            lo = lax.broadcasted_iota(jnp.int32, (BAND, LANES), 1) < HEAD_DIM
            head_lane = lax.broadcasted_iota(jnp.int32, (1, LANES), 1)
            banded = _banded_tiles(kvp_ref, kvc_ref)
            units = [(u, m) for u in range(Q_BLOCKS) for m in range(N_PAIRS)]
            dsink = jnp.zeros((1, LANES), F32)
            folded = {}
            logits, dps, dsbs = {}, {}, {}
            lag_b, lag_c = ATTN_BWD_LAGS
            for step in range(len(units) + lag_c):
                if step < len(units):
                    u, m = units[step]
                    kh, rows = m // pairs_per_kv, slice((m % pairs_per_kv) * BAND, (m % pairs_per_kv + 1) * BAND)
                    qrows = slice(u * BLOCK, (u + 1) * BLOCK)
                    qpair = _pair_rows(q_ref, qrows, m, SCALE)
                    dopair = _pair_rows(da_ref, qrows, m)
                    qs_ref[u, kh, rows, :] = qpair
                    dos_ref[u, kh, rows, :] = dopair
                    logits[step] = _dot_nt(banded[u][0][kh], qpair) + _bias_of(bias_ref, i, u, m)
                    dps[step] = _dot_nt(banded[u][2][kh], dopair)
                if 0 <= step - lag_b < len(units):
                    u, m = units[step - lag_b]
                    kh, rows = m // pairs_per_kv, slice((m % pairs_per_kv) * BAND, (m % pairs_per_kv + 1) * BAND)
                    pn, sink_p = _softmax_t(logits.pop(step - lag_b), _sink_row(sink_ref, m))
                    dp = dps.pop(step - lag_b)
                    delta = jnp.sum(pn * dp, axis=0, keepdims=True)
                    ds = pn * (dp - delta)
                    dssum_ref[m] += ds
                    sink_term = sink_p * delta
                    for e in range(2):
                        total = jnp.sum(sink_term[:, e * BLOCK:(e + 1) * BLOCK], axis=1, keepdims=True)
                        dsink = dsink - jnp.where(head_lane == 2 * m + e, total, 0.0)
                    dsbs[step - lag_b] = ds.astype(BF16)
                    dst_ref[u, kh, :, rows] = dsbs[step - lag_b]
                    pt_ref[u, kh, :, rows] = pn.astype(BF16)
                if 0 <= step - lag_c < len(units):
                    u, m = units[step - lag_c]
                    kh = m // pairs_per_kv
                    dq_t = _dot(banded[u][1][kh], dsbs.pop(step - lag_c))
                    dq_ref[u * BLOCK:(u + 1) * BLOCK, m * LANES:(m + 1) * LANES] = (_pair_cols(dq_t) * SCALE).astype(BF16)
                    if m % pairs_per_kv == pairs_per_kv - 1:
                        for name, lhs_ref, rhs_ref in (("k", dst_ref, qs_ref), ("v", pt_ref, dos_ref)):
                            acc = _dot(lhs_ref[u, kh], rhs_ref[u, kh])
                            folded[u, kh, name] = acc + pltpu.roll(acc, HEAD_DIM, 1)
            dsink_ref[0:1, :] += dsink
            dkv = [jnp.concatenate([jnp.where(lo, folded[u, 0, n], folded[u, 1, n]) for n in ("k", "v")], axis=1)
                   for u in range(Q_BLOCKS)]

            @pl.when(i > 0)
            def _():
                if held:
                    dkv_ref[:held, :] = done_ref[...].astype(BF16)
                dkv_ref[held:, :] = (carry_ref[...] + dkv[0][:BLOCK]).astype(BF16)

            for u in range(Q_BLOCKS - 1):
                done_ref[u * BLOCK:(u + 1) * BLOCK, :] = dkv[u][BLOCK:] + dkv[u + 1][:BLOCK]
            carry_ref[...] = dkv[Q_BLOCKS - 1][BLOCK:]

        @pl.when(i == nsteps)
        def _():
            if held:
                dkv_ref[:held, :] = done_ref[...].astype(BF16)
            dkv_ref[held:, :] = carry_ref[...].astype(BF16)

    last = nsteps - 1
    blk = lambda w: pl.BlockSpec((tile, w), lambda i: (jnp.minimum(i, last), 0))
    outs = pl.pallas_call(
        body,
        name="attn_bwd",
        grid=(nsteps + 1,),
        in_specs=[
            pl.BlockSpec(memory_space=pltpu.SMEM),
            blk(aw),
            blk(kvw),
            pl.BlockSpec((BLOCK, kvw), lambda i: (jnp.clip(Q_BLOCKS * i - 1, 0, nb - 1), 0)),
            blk(aw),
            _full(biasm.shape),
        ] + [HBM_SPEC] * nr,
        out_specs=[
            blk(aw),
            pl.BlockSpec((tile, kvw), lambda i: (jnp.maximum(i - 1, 0), 0)),
            _resident(biasm.shape[1:]),
            _resident((8, LANES)),
        ] + [HBM_SPEC] * nr,
        out_shape=[
            jax.ShapeDtypeStruct((seq, aw), BF16),
            jax.ShapeDtypeStruct((seq, kvw), BF16),
            jax.ShapeDtypeStruct(biasm.shape[1:], F32),
            jax.ShapeDtypeStruct((8, LANES), F32),
        ] + [jax.ShapeDtypeStruct(g.shape, g.dtype) for g in ready],
        scratch_shapes=[
            pltpu.VMEM((BLOCK, kvw), F32),
            pltpu.VMEM((max(held, 8), kvw), F32),
            pltpu.VMEM((Q_BLOCKS, N_KV_HEADS, pairs_per_kv * BAND, LANES), BF16),
            pltpu.VMEM((Q_BLOCKS, N_KV_HEADS, pairs_per_kv * BAND, LANES), BF16),
            pltpu.VMEM((Q_BLOCKS, N_KV_HEADS, BAND, pairs_per_kv * BAND), BF16),
            pltpu.VMEM((Q_BLOCKS, N_KV_HEADS, BAND, pairs_per_kv * BAND), BF16),
        ] + _exchange_sems(nr),
        compiler_params=_params(("arbitrary",)),
    )(sinks, q, kv, kv, dattn, biasm, *ready)
    return outs[:4], outs[4:]


def _layer_b_in_bwd(dh2, dq, dz2, dkv, h1, ya, wbin_g, wkv, kvn, bpre, sm, ready, ts):
    seq, d = h1.shape
    aw = dq.shape[1]
    kvw = dkv.shape[1]
    cw = wbin_g.shape[2]
    per = aw // cw

    nr = len(ready)
    nt = seq // ts

    def body(dh2_ref, dq_ref, dz2_ref, dkv_ref, h1_ref, ya_ref, wbin_ref, wkv_ref, kvn_ref, bpre_ref, sm_ref, *refs):
        ready_refs, (dh1_ref, dya_ref, acc_ref) = refs[:nr], refs[nr:nr + 3]
        landed_refs, sems = refs[nr + 3:2 * nr + 3], refs[2 * nr + 3:]

        @pl.when(pl.program_id(0) == 0)
        def _():
            acc_ref[...] = jnp.zeros_like(acc_ref)
            if nr:
                _exchange_start(ready_refs, landed_refs, *sems, True)

        if nr:
            @pl.when(pl.program_id(0) == nt - 1)
            def _():
                _exchange_wait(ready_refs, landed_refs, *sems, True)

        dn4 = jnp.zeros((ts, d), F32)
        for j in range(N_DEV):
            src = dq_ref if j < per else dz2_ref
            jj = j % per
            dn4 = dn4 + _dot_nt(src[:, jj * cw:(jj + 1) * cw], wbin_ref[j])
        dn3 = _dot_nt(dkv_ref[...], wkv_ref[...])
        hn, r = _rms(h1_ref[...])
        acc_ref[0:1, :] += jnp.sum(dn4 * hn, axis=0, keepdims=True)
        acc_ref[1:2, :] += jnp.sum(dn3 * hn, axis=0, keepdims=True)
        dh1 = dh2_ref[...] + _rms_bwd(dn4 * bpre_ref[...] + dn3 * kvn_ref[...], hn, r)
        dh1_ref[...] = dh1
        yan, r2 = _rms(ya_ref[...])
        acc_ref[2:3, :] += jnp.sum(dh1 * yan, axis=0, keepdims=True)
        dya_ref[...] = _rms_bwd(dh1 * sm_ref[4:5, :], yan, r2).astype(BF16)

    outs = pl.pallas_call(
        body,
        name="layer_b_in_bwd",
        grid=(nt,),
        in_specs=[_rows(ts, d), _rows(ts, aw), _rows(ts, aw), _rows(ts, kvw), _rows(ts, d), _rows(ts, d),
                  _full(wbin_g.shape), _full(wkv.shape), _full(kvn.shape), _full(bpre.shape), _full(sm.shape)]
        + [HBM_SPEC] * nr,
        out_specs=[_rows(ts, d), _rows(ts, d), _resident((8, d))] + [HBM_SPEC] * nr,
        out_shape=[jax.ShapeDtypeStruct((seq, d), F32), jax.ShapeDtypeStruct((seq, d), BF16),
                   jax.ShapeDtypeStruct((8, d), F32)] + [jax.ShapeDtypeStruct(g.shape, g.dtype) for g in ready],
        scratch_shapes=_exchange_sems(nr),
        compiler_params=_params(("arbitrary",)),
    )(dh2, dq, dz2, dkv, h1, ya, wbin_g, wkv, kvn, bpre, sm, *ready)
    return outs[:3], outs[3:]


def _layer_a_bwd(dya, proj, conv, dh1, x2, wout, win_g, sm, ts):
    seq, d = x2.shape
    width = wout.shape[0]
    half = win_g.shape[2]
    n_half = width // half
    nt = seq // ts

    def body(dya_ref, proj_ref, conv_ref, dh1_ref, x_ref, wout_ref, win_ref, sm_ref, dproj_ref, gx_ref, acc_ref,
             dnext_ref):
        @pl.when(pl.program_id(0) == 0)
        def _():
            acc_ref[...] = jnp.zeros_like(acc_ref)
            dnext_ref[...] = jnp.zeros_like(dnext_ref)

        dy = _dot_nt(dya_ref[...], wout_ref[...])
        row = lax.broadcasted_iota(jnp.int32, (ts, half), 0)
        dn1 = jnp.zeros((ts, d), F32)
        for hh in range(n_half):
            cols = slice(hh * half, (hh + 1) * half)
            b, c, u, z = [proj_ref[:, (part * n_half + hh) * half:(part * n_half + hh + 1) * half].astype(F32)
                          for part in range(4)]
            cv = conv_ref[:, cols].astype(F32)
            dyh = dy[:, cols]
            sz, dsz = _silu(z)
            dconv = dyh * b * sz
            grads = [dyh * cv * sz, None, None, dyh * b * cv * dsz]
            next0, next1 = dnext_ref[0:1, cols], dnext_ref[1:2, cols]
            dc1 = jnp.where(row == ts - 1, next0, pltpu.roll(dconv, ts - 1, 0))
            dc2 = jnp.where(row == ts - 1, next1, jnp.where(row == ts - 2, next0, pltpu.roll(dconv, ts - 2, 0)))
            dnext_ref[:, cols] = dconv[0:8, :]
            v = c * u
            acc_ref[1:2, cols] += jnp.sum(dc2 * v, axis=0, keepdims=True)
            acc_ref[2:3, cols] += jnp.sum(dc1 * v, axis=0, keepdims=True)
            acc_ref[3:4, cols] += jnp.sum(dconv * v, axis=0, keepdims=True)
            dv = sm_ref[3:4, cols] * dconv + sm_ref[2:3, cols] * dc1 + sm_ref[1:2, cols] * dc2
            grads[1] = dv * u
            grads[2] = dv * c
            for part in range(4):
                j = part * n_half + hh
                gj = grads[part].astype(BF16)
                dproj_ref[:, j * half:(j + 1) * half] = gj
                dn1 = dn1 + _dot_nt(gj, win_ref[j])
        xn, r = _rms(x_ref[...])
        acc_ref[0:1, :] += jnp.sum(dn1 * xn, axis=0, keepdims=True)
        gx_ref[...] = dh1_ref[...] + _rms_bwd(dn1 * sm_ref[0:1, :], xn, r)

    rev = lambda w: pl.BlockSpec((ts, w), lambda i: (nt - 1 - i, 0))
    return pl.pallas_call(
        body,
        name="layer_a_bwd",
        grid=(nt,),
        in_specs=[rev(d), rev(4 * width), rev(width), rev(d), rev(d), _full(wout.shape), _full(win_g.shape), _full(sm.shape)],
        out_specs=[rev(4 * width), rev(d), _resident((8, d))],
        out_shape=[jax.ShapeDtypeStruct((seq, 4 * width), BF16), jax.ShapeDtypeStruct((seq, d), F32),
                   jax.ShapeDtypeStruct((8, d), F32)],
        scratch_shapes=[pltpu.VMEM((8, width), F32)],
        compiler_params=_params(("arbitrary",)),
    )(dya, proj, conv, dh1, x2, wout, win_g, sm)


def _wgrad(a, bs, n_slots, ts, name, ready=(), block_cols=1024):
    nr = len(ready)
    seq, k = a.shape
    nb_in = len(bs)
    n_each = bs[0].shape[1]
    n = nb_in * n_each
    bn = min(n_each, block_cols)
    per_in = n_each // bn
    n_blocks = nb_in * per_in
    ns = seq // ts

    def b_spec(idx):
        def index(j, s):
            mine = j // per_in == idx
            row = jnp.where(mine, s, jnp.where(j // per_in > idx, ns - 1, 0))
            return (row, jnp.where(mine, j % per_in, jnp.where(j // per_in > idx, per_in - 1, 0)))
        return pl.BlockSpec((ts, bn), index)

    if n_slots:
        sw = n // n_slots
        spb = bn // sw
        out_shape = jax.ShapeDtypeStruct((n_slots, k, sw), BF16)
        out_spec = pl.BlockSpec((spb, k, sw), lambda j, s: (j, 0, 0))
    else:
        out_shape = jax.ShapeDtypeStruct((k, n), BF16)
        out_spec = pl.BlockSpec((k, bn), lambda j, s: (0, j))

    def body(a_ref, *refs):
        b_refs, ready_refs, o_ref = refs[:nb_in], refs[nb_in:nb_in + nr], refs[nb_in + nr]
        landed_refs, (acc_ref, *sems) = refs[nb_in + nr + 1:nb_in + 2 * nr + 1], refs[nb_in + 2 * nr + 1:]
        j, s = pl.program_id(0), pl.program_id(1)

        if nr:
            @pl.when(jnp.logical_and(j == 0, s == 0))
            def _():
                _exchange_start(ready_refs, landed_refs, *sems, True)

            @pl.when(jnp.logical_and(j == n_blocks - 1, s == ns - 1))
            def _():
                _exchange_wait(ready_refs, landed_refs, *sems, True)

        @pl.when(s == 0)
        def _():
            acc_ref[...] = jnp.zeros_like(acc_ref)

        for idx in range(nb_in):
            @pl.when(j // per_in == idx)
            def _(idx=idx):
                acc_ref[...] += _dot_tn(a_ref[...], b_refs[idx][...])

        @pl.when(s == ns - 1)
        def _():
            if n_slots:
                for e in range(spb):
                    o_ref[e] = acc_ref[:, e * sw:(e + 1) * sw].astype(BF16)
            else:
                o_ref[...] = acc_ref[...].astype(BF16)

    outs = pl.pallas_call(
        body,
        name=name,
        grid=(n_blocks, ns),
        in_specs=[pl.BlockSpec((ts, k), lambda j, s: (s, 0))] + [b_spec(idx) for idx in range(nb_in)] + [HBM_SPEC] * nr,
        out_specs=[out_spec] + [HBM_SPEC] * nr,
        out_shape=[out_shape] + [jax.ShapeDtypeStruct(g.shape, g.dtype) for g in ready],
        scratch_shapes=[pltpu.VMEM((k, bn), F32)] + (_exchange_sems(nr) if nr else []),
        compiler_params=_params(("arbitrary", "arbitrary")),
    )(a, *bs, *ready)
    return (outs[0], outs[1:]) if nr else outs[0]


def _wgrad_tail(pairs, part, landed, ts):
    n_tasks = len(pairs)
    assert n_tasks == 2
    nl = len(landed)
    seq, k = pairs[0][0].shape
    n = pairs[0][1].shape[1]
    ns = seq // ts
    total = n_tasks * ns
    per = k // N_DEV
    n_red = len(_chip_reduce_scratch((per, n)))

    def spec(t, width):
        return pl.BlockSpec((ts, width), lambda j, s: (jnp.where(j == t, s, jnp.where(j > t, ns - 1, 0)), 0))

    def body(*refs):
        ab_refs, part_hbm = refs[:2 * n_tasks], refs[2 * n_tasks]
        landed_hbm, refs = refs[2 * n_tasks + 1:2 * n_tasks + 1 + nl], refs[2 * n_tasks + 1 + nl:]
        o_ref, red_ref, early_ref = refs[:3]
        summed_refs, (acc_ref, first_ref, part_ref, *scratch) = refs[3:3 + nl], refs[3 + nl:]
        landed_refs, load_sems, scratch = scratch[:nl], scratch[nl], scratch[nl + 1:]
        j, s = pl.program_id(0), pl.program_id(1)
        flat = j * ns + s
        swap, send, forward, finish = _chip_reduce(part_ref, red_ref, *scratch[:3], scratch[3:n_red], part_hbm)
        swap_first, send_first, forward_first, finish_first = _chip_reduce(
            first_ref, early_ref, *scratch[n_red:n_red + 3], scratch[n_red + 3:])
        loads = [pltpu.make_async_copy(src, dst, load_sems.at[i])
                 for i, (src, dst) in enumerate(zip([part_hbm, *landed_hbm], [part_ref, *landed_refs]))]

        @pl.when(flat == 0)
        def _():
            swap()
            for load in loads:
                load.start()

        @pl.when(flat == min(1, total - 1))
        def _():
            loads[0].wait()
            send()

        @pl.when(flat == min(total // 2 + 1, total - 1))
        def _():
            forward()
            for t in range(nl):
                loads[1 + t].wait()
                _sum_slots(landed_refs[t], summed_refs[t])

        @pl.when(flat == ns)
        def _():
            send_first()

        @pl.when(flat == min(ns + ns // 2, total - 1))
        def _():
            forward_first()

        @pl.when(s == 0)
        def _():
            acc_ref[...] = jnp.zeros_like(acc_ref)

        for t in range(n_tasks):
            @pl.when(j == t)
            def _(t=t):
                acc_ref[...] += _dot_tn(ab_refs[2 * t][...], ab_refs[2 * t + 1][...])

        @pl.when(flat == ns - 1)
        def _():
            for dev in range(N_DEV):
                first_ref[dev] = acc_ref[dev * per:(dev + 1) * per, :].astype(BF16)
            swap_first()

        @pl.when(flat == total - 1)
        def _():
            for dev in range(N_DEV):
                o_ref[dev] = acc_ref[dev * per:(dev + 1) * per, :].astype(BF16)
            finish()
            finish_first()

    slot = part.shape[1:]
    outs = pl.pallas_call(
        body,
        name="wgrad_tail",
        grid=(n_tasks, ns),
        in_specs=[spec(t, w) for t in range(n_tasks) for w in (k, n)] + [HBM_SPEC] * (1 + nl),
        out_specs=[_resident((N_DEV, per, n)), _resident(slot), _resident((per, n))]
        + [_resident(g.shape[1:]) for g in landed],
        out_shape=[jax.ShapeDtypeStruct((N_DEV, per, n), BF16), jax.ShapeDtypeStruct(slot, F32),
                   jax.ShapeDtypeStruct((per, n), F32)]
        + [jax.ShapeDtypeStruct(g.shape[1:], F32) for g in landed],
        scratch_shapes=[pltpu.VMEM((k, n), F32), pltpu.VMEM((N_DEV, per, n), BF16), pltpu.VMEM(part.shape, part.dtype)]
        + [pltpu.VMEM(g.shape, g.dtype) for g in landed] + [pltpu.SemaphoreType.DMA((1 + nl,))]
        + _chip_reduce_scratch(slot) + _chip_reduce_scratch((per, n)),
        compiler_params=_params(("arbitrary", "arbitrary")),
    )(*[op for pair in pairs for op in pair], part, *landed)
    return outs[0], outs[1], outs[2], outs[3:]


MINE = "mine"
ADAMW_STEPS = 4


def _adamw(ws, sources, picks, loss_at, ms, vs):
    n, n_src = len(ws), len(sources)
    streamed = [len(w.shape) == 2 and w.shape[0] >= 128 and picks[t][1:] == (0, None)
                and sources[picks[t][0]].shape == w.shape for t, w in enumerate(ws)]
    streamed_sources = {picks[t][0] for t in range(n) if streamed[t]}

    def step(w, g, m, v):
        m = ADAM_B1 * m + (1.0 - ADAM_B1) * g
        v = ADAM_B2 * v + (1.0 - ADAM_B2) * jnp.square(g)
        m_hat = m / (1.0 - ADAM_B1 ** ADAM_STEP)
        v_hat = v / (1.0 - ADAM_B2 ** ADAM_STEP)
        return g, -ADAM_LR * (m_hat / (jnp.sqrt(v_hat) + ADAM_EPS) + ADAM_WD * w), m, v

    def body(*refs):
        refs = list(refs)
        take = lambda k: [refs.pop(0) for _ in range(k)]
        w_refs, s_refs, m_refs, v_refs = take(n), take(n_src), take(n), take(n)
        (loss_ref,), go_refs, d_refs, nm_refs, nv_refs = take(1), take(n), take(n), take(n), take(n)
        me = _my_index()

        def grad(t, rows):
            k, first, cols = picks[t]
            if cols is None:
                return s_refs[k][rows, :]
            if cols is not MINE:
                return s_refs[k][rows, cols]
            width = w_refs[t].shape[-1]
            g = s_refs[k][rows, 0:width]
            for dev in range(1, N_DEV):
                g = jnp.where(me == dev, s_refs[k][rows, dev * width:(dev + 1) * width], g)
            return g

        def whole(t):
            first = picks[t][1]
            rows = w_refs[t].shape[0]
            if len(w_refs[t].shape) == 3:
                for j in range(rows):
                    go_refs[t][j], d_refs[t][j], nm_refs[t][j], nv_refs[t][j] = step(
                        w_refs[t][j], grad(t, slice(first + j, first + j + 1)), m_refs[t][j], v_refs[t][j])
                return
            go_refs[t][...], d_refs[t][...], nm_refs[t][...], nv_refs[t][...] = step(
                w_refs[t][...], grad(t, slice(first, first + rows)), m_refs[t][...], v_refs[t][...])

        def block(t):
            rows = w_refs[t].shape[0]
            chunk = min(rows, 128)

            def one(i, carry):
                r = pl.ds(pl.multiple_of(i * chunk, chunk), chunk)
                go_refs[t][r, :], d_refs[t][r, :], nm_refs[t][r, :], nv_refs[t][r, :] = step(
                    w_refs[t][r, :], grad(t, r), m_refs[t][r, :], v_refs[t][r, :])
                return carry

            lax.fori_loop(0, rows // chunk, one, 0)

        @pl.when(pl.program_id(0) == 0)
        def _():
            loss_ref[...] = s_refs[loss_at[0]][loss_at[1]:loss_at[1] + 1, 0:1]
            for t in range(n):
                if not streamed[t]:
                    whole(t)

        for t in range(n):
            if streamed[t]:
                block(t)

    def rows_of(shape):
        return pl.BlockSpec((shape[0] // ADAMW_STEPS, shape[1]), lambda i: (i, 0))

    w_in = [rows_of(w.shape) if streamed[t] else _full(w.shape) for t, w in enumerate(ws)]
    w_out = [rows_of(w.shape) if streamed[t] else _resident(w.shape) for t, w in enumerate(ws)]
    s_in = [rows_of(s.shape) if k in streamed_sources else _full(s.shape) for k, s in enumerate(sources)]
    outs = pl.pallas_call(
        body,
        name="adamw",
        grid=(ADAMW_STEPS,),
        in_specs=w_in + s_in + w_in * 2,
        out_specs=[_resident((1, 1))] + w_out * 4,
        out_shape=[jax.ShapeDtypeStruct((1, 1), F32)] + [jax.ShapeDtypeStruct(w.shape, F32) for w in ws] * 4,
        compiler_params=_params(("arbitrary",)),
    )(*ws, *sources, *ms, *vs)
    return outs[0], outs[1:n + 1], outs[n + 1:2 * n + 1], outs[2 * n + 1:3 * n + 1], outs[3 * n + 1:]


def _band_structure():
    q_loc = np.arange(BLOCK, dtype=np.int32)[:, None]
    s_loc = np.arange(2 * BLOCK, dtype=np.int32)[None, :]
    dist = q_loc + BLOCK - s_loc
    in_window = (dist >= 0) & (dist < BLOCK)
    dd = np.maximum(dist, 0)
    max_exact = N_BUCKETS // 2
    large = max_exact + (np.log(np.maximum(dd, 1) / max_exact) / math.log(MAX_DISTANCE / max_exact)
                         * (N_BUCKETS - max_exact)).astype(np.int32)
    bucket = np.where(dd < max_exact, dd, np.minimum(large, N_BUCKETS - 1)).astype(np.int32)
    return bucket, in_window.astype(np.int32)


def kernel(x, a_pre_norm, a_w_in, a_conv_w, a_w_out, a_post_norm, kv_norm, w_kv, rel_bias, b_pre_norm, b_w_in, b_sinks, b_w_out, b_post_norm, loss_target, m_a_pre_norm, m_a_w_in, m_a_conv_w, m_a_w_out, m_a_post_norm, m_kv_norm, m_w_kv, m_rel_bias, m_b_pre_norm, m_b_w_in, m_b_sinks, m_b_w_out, m_b_post_norm, v_a_pre_norm, v_a_w_in, v_a_conv_w, v_a_w_out, v_a_post_norm, v_kv_norm, v_w_kv, v_rel_bias, v_b_pre_norm, v_b_w_in, v_b_sinks, v_b_w_out, v_b_post_norm):
    seq, d = x.shape[1], x.shape[2]
    x2 = x.reshape(seq, d)
    target = loss_target.reshape(seq, d)
    shard = a_pre_norm.shape[1]
    ts_a = min(seq, 512)
    ts = min(seq, 512)
    ts_w = min(seq, 2048)

    taps = lambda a: a.transpose(1, 0, 2)
    bucket, in_window = _band_structure()
    (win_g, wout_g), small_g, later, biasm = _all_gather(
        [a_w_in[0], a_w_out[0]], [(0, a_pre_norm), (1, taps(a_conv_w)), (4, a_post_norm)],
        [w_kv, b_w_in[0], b_w_out[0]], rel_bias.T, bucket.T, in_window.T)
    wout = wout_g.reshape(-1, wout_g.shape[2])
    sm = small_g.transpose(1, 0, 2).reshape(8, N_DEV * shard)
    kvn = kv_norm.reshape(1, d)

    (h1, n1, proj, conv, y, ya), (wkv_g, wbin_g, wbout_g) = _layer_a_fwd(x2, sm, win_g, wout, later, ts_a)
    wkv = wkv_g.reshape(-1, wkv_g.shape[2])
    wbout = wbout_g.reshape(-1, wbout_g.shape[2])
    n3, n4, kv, q, o, dh2, dyb, dattn, dz2, acc_c = _layer_b_fwd(
        h1, target, kvn, b_pre_norm, wkv, wbin_g, biasm, b_sinks, wbout, b_post_norm)

    (dq, dkv, dssum, dsink), _ = _attn_bwd(q, kv, dattn, biasm, b_sinks, [])
    by_head = dssum.reshape(N_PAIRS, BAND, 2, BLOCK).transpose(0, 2, 3, 1)
    g_wkv = _wgrad(n3, [dkv], 0, ts_w, "wgrad_kv").reshape(wkv_g.shape)
    g_wbin = _wgrad(n4, [dq, dz2], N_DEV, ts_w, "wgrad_b_in")
    (dh1, dya, acc_b), _ = _layer_b_in_bwd(dh2, dq, dz2, dkv, h1, ya, wbin_g, wkv, kvn, b_pre_norm, sm, [], ts)
    dproj, gx, acc_a = _layer_a_bwd(dya, proj, conv, dh1, x2, wout, win_g, sm, ts_a)
    g_win, (l_wkv, l_wbin) = _wgrad(
        n1, [dproj], N_DEV, ts_w, "wgrad_a_in", ready=[g_wkv, g_wbin], block_cols=2048)
    g_wbout, r_win, r_wout, (r_wkv, r_wbin) = _wgrad_tail(
        [(y, dya), (o, dyb)], g_win, [l_wkv, l_wbin], min(seq, 1024))

    r_wbout, _, (s_a, s_b, s_c, s_sink), s_relb = _reduce_exchange(
        g_wbout, [], [acc_a, acc_b, acc_c, dsink], by_head.reshape(N_Q_HEADS, -1), bucket.reshape(1, -1), 4096)
    weights = [a_pre_norm, a_w_in[0], taps(a_conv_w), a_w_out[0], a_post_norm, kvn, w_kv, rel_bias.T, b_pre_norm,
               b_w_in[0], b_sinks, b_w_out[0], b_post_norm]
    sources = [s_a, s_b, s_c, s_relb, s_sink, r_win, r_wout, r_wkv, r_wbin, r_wbout]
    picks = [(0, 0, MINE), (5, 0, None), (0, 1, MINE), (6, 0, None), (1, 2, MINE), (1, 1, None), (7, 0, None),
             (3, 0, slice(0, N_BUCKETS)), (1, 0, None), (8, 0, None), (4, 0, slice(0, N_Q_HEADS)),
             (9, 0, None), (2, 0, None)]
    first = [m_a_pre_norm, m_a_w_in[0], taps(m_a_conv_w), m_a_w_out[0], m_a_post_norm, m_kv_norm.reshape(1, d),
             m_w_kv, m_rel_bias.T, m_b_pre_norm, m_b_w_in[0], m_b_sinks, m_b_w_out[0], m_b_post_norm]
    second = [v_a_pre_norm, v_a_w_in[0], taps(v_a_conv_w), v_a_w_out[0], v_a_post_norm, v_kv_norm.reshape(1, d),
              v_w_kv, v_rel_bias.T, v_b_pre_norm, v_b_w_in[0], v_b_sinks, v_b_w_out[0], v_b_post_norm]
    loss, grads, deltas, new_m, new_v = _adamw(weights, sources, picks, (2, 1), first, second)

    shapes = [a_pre_norm.shape, a_w_in.shape, taps, a_w_out.shape, a_post_norm.shape, kv_norm.shape,
              w_kv.shape, jnp.transpose, b_pre_norm.shape, b_w_in.shape, b_sinks.shape, b_w_out.shape, b_post_norm.shape]
    shaped = lambda arrays: [s(a) if callable(s) else a.reshape(s) for a, s in zip(arrays, shapes)]
    return (loss.reshape(()), gx.reshape(x.shape), *shaped(grads), *shaped(deltas), *shaped(new_m), *shaped(new_v))
```

```python
import math

import jax
import jax.numpy as jnp
import numpy as np
from jax import lax
from jax.experimental import pallas as pl
from jax.experimental.pallas import tpu as pltpu

HEAD_DIM = 64
N_Q_HEADS = 16
N_KV_HEADS = 2
GROUP = N_Q_HEADS // N_KV_HEADS
BLOCK = 128
N_BUCKETS = 32
MAX_DISTANCE = 128
EPS = 1e-6
NEG_INF = -1e30
SCALE = HEAD_DIM ** -0.5

ADAM_LR = 0.001
ADAM_B1 = 0.9
ADAM_B2 = 0.999
ADAM_EPS = 1e-08
ADAM_WD = 0.01
ADAM_STEP = 10

N_PAIRS = N_Q_HEADS // 2
BAND = 2 * BLOCK

N_DEV = 8
GATHER_PIECE_ROWS = 256
LANES = 128
F32 = jnp.float32
BF16 = jnp.bfloat16
MESH = pl.DeviceIdType.MESH
MIB = 1024 * 1024
VMEM_RESERVED_MIB = 63


def _params(semantics=None):
    return pltpu.CompilerParams(dimension_semantics=semantics, vmem_limit_bytes=VMEM_RESERVED_MIB * MIB)


def _full(shape):
    zeros = (0,) * len(shape)
    return pl.BlockSpec(shape, lambda *_: zeros, pipeline_mode=pl.Buffered(1))


def _resident(shape):
    zeros = (0,) * len(shape)
    return pl.BlockSpec(shape, lambda *_: zeros)


def _rows(ts, cols):
    return pl.BlockSpec((ts, cols), lambda i: (i, 0))


def _dot(a, b):
    return jnp.dot(a, b, preferred_element_type=F32)


def _dot_nt(a, b):
    return lax.dot_general(a, b, (((1,), (1,)), ((), ())), preferred_element_type=F32)


def _dot_tn(a, b):
    return lax.dot_general(a, b, (((0,), (0,)), ((), ())), preferred_element_type=F32)


def _rms(xf):
    r = lax.rsqrt(jnp.mean(xf * xf, axis=-1, keepdims=True) + EPS)
    return xf * r, r


def _rms_bwd(dn, xn, r):
    return r * (dn - xn * jnp.mean(dn * xn, axis=-1, keepdims=True))


def _silu(z):
    s = jax.nn.sigmoid(z)
    return z * s, s * (1.0 + z * (1.0 - s))


def _my_index():
    return 4 * lax.axis_index("x") + 2 * lax.axis_index("y") + lax.axis_index("c")


def _bias_table(rb_ref, bucket_ref, win_ref, out_ref):
    bk = jnp.where(win_ref[...] != 0, bucket_ref[...], -1)
    has_prev = lax.broadcasted_iota(jnp.int32, bk.shape, 0) >= BLOCK
    for h in range(N_Q_HEADS):
        acc = jnp.full(bk.shape, NEG_INF, F32)
        for b in range(N_BUCKETS):
            acc = jnp.where(bk == b, rb_ref[h, b], acc)
        cols = slice((h % 2) * BLOCK, (h % 2 + 1) * BLOCK)
        out_ref[1, h // 2, :, cols] = acc
        out_ref[0, h // 2, :, cols] = jnp.where(has_prev, acc, NEG_INF)


def _all_gather(shards, small_rows, casts, rel_bias_t, bucket_t, in_window_t):
    ns, nc, n = len(small_rows), len(casts), len(shards) + 1
    small_shape = (8, small_rows[0][1].shape[-1])
    shapes = [s.shape for s in shards] + [small_shape]
    pieces = [(t, r0, min(GATHER_PIECE_ROWS, shape[0] - r0))
              for t, shape in enumerate(shapes) for r0 in range(0, shape[0], GATHER_PIECE_ROWS)]

    def body(*refs):
        refs = list(refs)
        take = lambda k: [refs.pop(0) for _ in range(k)]
        ins, small_refs, cast_refs, (rb_ref, bucket_ref, win_ref) = take(n - 1), take(ns), take(nc), take(3)
        outs, cast_outs, (bias_ref, send_sems, recv_sems) = take(n), take(nc), take(3)
        x, y, c = lax.axis_index("x"), lax.axis_index("y"), lax.axis_index("c")
        me, sibling = (x, y, c), (x, y, 1 - c)
        x_nbr, y_nbr, diagonal = (1 - x, y), (x, 1 - y), (1 - x, 1 - y)
        south = c == 0
        relayed = (jnp.where(south, 1 - x, x), jnp.where(south, y, 1 - y))
        relay_to = (jnp.where(south, x, 1 - x), jnp.where(south, 1 - y, y))

        def copy(u, k, block, to):
            t, r0, nrows = pieces[u]
            rows = outs[t].at[4 * block[0] + 2 * block[1] + block[2], pl.ds(r0, nrows)]
            return pltpu.make_async_remote_copy(
                src_ref=rows, dst_ref=rows, send_sem=send_sems.at[u, k], recv_sem=recv_sems.at[u, k],
                device_id=to, device_id_type=MESH)

        mine = pl.ds(_my_index(), 1)
        for t in range(n - 1):
            outs[t][mine] = ins[t][...].astype(BF16)[None]
        outs[n - 1][mine] = jnp.zeros((1,) + small_shape, F32)
        for (row, _), ref in zip(small_rows, small_refs):
            if len(ref.shape) == 3:
                for j in range(ref.shape[0]):
                    outs[n - 1][mine, row + j:row + j + 1, :] = ref[j][None]
            else:
                outs[n - 1][mine, row:row + ref.shape[0], :] = ref[...][None]
        started = []

        def start(cp):
            cp.start()
            started.append(cp)

        units = range(len(pieces))
        for u in units:
            start(copy(u, 0, me, sibling))
            start(copy(u, 1, me, (*x_nbr, c)))
            start(copy(u, 2, me, (*y_nbr, c)))
        for u in units:
            for k, chip in ((1, x_nbr), (2, y_nbr)):
                copy(u, k, (*chip, c), me).wait_recv()
                start(copy(u, 3 + k, (*chip, c), sibling))
            start(copy(u, 3, (*relayed, c), (*relay_to, c)))
        for src, dst in zip(cast_refs, cast_outs):
            dst[...] = src[...].astype(BF16)
        _bias_table(rb_ref, bucket_ref, win_ref, bias_ref)
        for u in units:
            copy(u, 3, (*diagonal, c), me).wait_recv()
            start(copy(u, 6, (*diagonal, c), sibling))
        for u in units:
            copy(u, 0, sibling, me).wait_recv()
        for k, chip in ((4, x_nbr), (5, y_nbr), (6, diagonal)):
            for u in units:
                copy(u, k, (*chip, 1 - c), me).wait_recv()
        for cp in started:
            cp.wait_send()

    vmem = pl.BlockSpec(memory_space=pltpu.VMEM)
    outs = pl.pallas_call(
        body,
        name="gather_weights",
        out_shape=[jax.ShapeDtypeStruct((N_DEV,) + s.shape, BF16) for s in shards]
        + [jax.ShapeDtypeStruct((N_DEV,) + small_shape, F32)]
        + [jax.ShapeDtypeStruct(a.shape, BF16) for a in casts]
        + [jax.ShapeDtypeStruct((2, N_PAIRS, BAND, 2 * BLOCK), F32)],
        in_specs=[vmem] * (n - 1 + ns + nc) + [pl.BlockSpec(memory_space=pltpu.SMEM), vmem, vmem],
        out_specs=[vmem] * (n + nc + 1),
        scratch_shapes=[pltpu.SemaphoreType.DMA((len(pieces), 7)), pltpu.SemaphoreType.DMA((len(pieces), 7))],
        compiler_params=_params(),
    )(*shards, *[a for _, a in small_rows], *casts, rel_bias_t, bucket_t, in_window_t)
    return outs[:n - 1], outs[n - 1], outs[n:n + nc], outs[n + nc]


def _peer(k):
    x, y, c = lax.axis_index("x"), lax.axis_index("y"), lax.axis_index("c")
    px = 1 - x if k & 4 else x
    py = 1 - y if k & 2 else y
    pc = 1 - c if k & 1 else c
    return (px, py, pc), 4 * px + 2 * py + pc


def _exchange(srcs, dsts, send_sems, recv_sems, local_sems, scatter):
    me = _my_index()
    sends, arrivals = [], []
    for k in range(1, N_DEV):
        peer, pidx = _peer(k)
        for t, (src, dst) in enumerate(zip(srcs, dsts)):
            mine = src.at[pidx] if scatter else src
            sems = dict(send_sem=send_sems.at[t, k - 1], recv_sem=recv_sems.at[t, k - 1], device_id=peer, device_id_type=MESH)
            sends.append(pltpu.make_async_remote_copy(src_ref=mine, dst_ref=dst.at[me], **sems))
            arrivals.append(pltpu.make_async_remote_copy(src_ref=mine, dst_ref=dst.at[pidx], **sems))
    local = [pltpu.make_async_copy(src.at[me] if scatter else src, dst.at[me], local_sems.at[t])
             for t, (src, dst) in enumerate(zip(srcs, dsts))]
    return sends, arrivals, local


def _exchange_start(*args):
    sends, _, local = _exchange(*args)
    for cp in sends + local:
        cp.start()


def _exchange_wait(*args):
    sends, arrivals, local = _exchange(*args)
    for cp in arrivals:
        cp.wait_recv()
    for cp in sends:
        cp.wait_send()
    for cp in local:
        cp.wait()


def _exchange_sems(n):
    if not n:
        return []
    return [pltpu.SemaphoreType.DMA((n, N_DEV - 1)), pltpu.SemaphoreType.DMA((n, N_DEV - 1)), pltpu.SemaphoreType.DMA((n,))]


HBM_SPEC = pl.BlockSpec(memory_space=pl.ANY)


def _sum_slots(recv_ref, out_ref):
    rows = out_ref.shape[0]
    chunk = min(rows, 128)

    def add(i, carry):
        r0 = pl.multiple_of(i * chunk, chunk)
        acc = recv_ref[0, pl.ds(r0, chunk), :].astype(F32)
        for dev in range(1, N_DEV):
            acc = acc + recv_ref[dev, pl.ds(r0, chunk), :].astype(F32)
        out_ref[pl.ds(r0, chunk), :] = acc
        return carry

    lax.fori_loop(0, rows // chunk, add, 0)


N_CHIPS = N_DEV // 2


def _rows_loop(rows, fn):
    chunk = min(rows, 128)

    def step(i, carry):
        fn(pl.ds(pl.multiple_of(i * chunk, chunk), chunk))
        return carry

    lax.fori_loop(0, rows // chunk, step, 0)


def _chip_reduce(g_ref, out_ref, sib_ref, land_ref, send_ref, sems, swap_src=None):
    sib_send, sib_recv, ici_send, ici_recv = sems
    x, y, c = lax.axis_index("x"), lax.axis_index("y"), lax.axis_index("c")
    south = c == 0
    near =(jnp.where(south, 1 - x, x), jnp.where(south, y, 1 - y))
    far = (jnp.where(south, x, 1 - x), jnp.where(south, 1 - y, y))
    diagonal = (1 - x, 1 - y)
    rows = out_ref.shape[0]
    direct, fold, folded = 0, 1, 2

    def to_sibling(t):
        src = g_ref if swap_src is None else swap_src
        return pltpu.make_async_remote_copy(
            src_ref=src.at[2 * t + 1 - c], dst_ref=sib_ref.at[t], send_sem=sib_send.at[t], recv_sem=sib_recv.at[t],
            device_id=(x, y, 1 - c), device_id_type=MESH)

    def ici(role, chip):
        return pltpu.make_async_remote_copy(
            src_ref=send_ref.at[role], dst_ref=land_ref.at[role], send_sem=ici_send.at[role],
            recv_sem=ici_recv.at[role], device_id=(*chip, c), device_id_type=MESH)

    def pair_sum(chip, r):
        t = 2 * chip[0] + chip[1]
        return g_ref[2 * t + c, r, :].astype(F32) + sib_ref[t, r, :].astype(F32)

    def swap():
        for t in range(N_CHIPS):
            to_sibling(t).start()

    def send():
        for t in range(N_CHIPS):
            to_sibling(t).wait_recv()
        for role, chip in ((fold, diagonal), (direct, near)):
            def fill(r, role=role, chip=chip):
                send_ref[role, r, :] = pair_sum(chip, r).astype(BF16)

            _rows_loop(rows, fill)
            ici(role, near).start()

    def forward():
        ici(fold, near).wait_recv()

        def fill(r):
            send_ref[folded, r, :] = (pair_sum(far, r) + land_ref[fold, r, :].astype(F32)).astype(BF16)

        _rows_loop(rows, fill)
        ici(folded, far).start()

    def finish():
        ici(direct, near).wait_recv()
        ici(folded, far).wait_recv()

        def total(r):
            mine = pair_sum((x, y), r)
            out_ref[r, :] = mine + land_ref[direct, r, :].astype(F32) + land_ref[folded, r, :].astype(F32)

        _rows_loop(rows, total)
        for t in range(N_CHIPS):
            to_sibling(t).wait_send()
        for role, chip in ((direct, near), (fold, near), (folded, far)):
            ici(role, chip).wait_send()

    return swap, send, forward, finish


def _chip_reduce_scratch(slot):
    return [pltpu.VMEM((N_CHIPS,) + slot, BF16), pltpu.VMEM((3,) + slot, BF16), pltpu.VMEM((3,) + slot, BF16),
            pltpu.SemaphoreType.DMA((N_CHIPS,)), pltpu.SemaphoreType.DMA((N_CHIPS,)),
            pltpu.SemaphoreType.DMA((3,)), pltpu.SemaphoreType.DMA((3,))]


def _bucket_sums(a_ref, bucket_ref, cols):
    a = a_ref[:, cols]
    hi = a.astype(BF16)
    lo = (a - hi.astype(F32)).astype(BF16)
    rows = lax.broadcasted_iota(jnp.int32, (LANES, a.shape[1]), 0)
    onehot_t = (rows == bucket_ref[:, cols]).astype(F32).astype(BF16)
    return _dot_nt(hi, onehot_t) + _dot_nt(lo, onehot_t)


def _reduce_exchange(part, landed, smalls, by_bucket, bucket_row, chunk):
    nl, ng = len(landed), len(smalls)
    n_in = 1 + nl + ng + 2
    n_out = 1 + nl + ng + 1
    heads, positions = by_bucket.shape
    chunks = [slice(c0, c0 + chunk) for c0 in range(0, positions, chunk)]

    def body(*refs):
        p_in, l_in, s_in, (a_ref, bucket_ref) = refs[0], refs[1:1 + nl], refs[1 + nl:n_in - 2], refs[n_in - 2:n_in]
        refs = refs[n_in:]
        p_out, l_out, s_out, b_out = refs[0], refs[1:1 + nl], refs[1 + nl:n_out - 1], refs[n_out - 1]
        scratch = refs[n_out:]
        s_recv, (b_recv, b_ref, sib_ref, chip_ref, send_ref), sems = scratch[:ng], scratch[ng:ng + 5], scratch[ng + 5:]
        swap, send, forward, finish = _chip_reduce(p_in, p_out, sib_ref, chip_ref, send_ref, sems[:4])
        swap()
        _exchange_start(s_in, s_recv, *sems[4:7], False)
        b_ref[...] = jnp.zeros_like(b_ref)
        for cols in chunks[:len(chunks) // 2]:
            b_ref[...] += _bucket_sums(a_ref, bucket_ref, cols)
        send()
        for t in range(nl):
            _sum_slots(l_in[t], l_out[t])
        for cols in chunks[len(chunks) // 2:]:
            b_ref[...] += _bucket_sums(a_ref, bucket_ref, cols)
        _exchange_start([b_ref], [b_recv], *sems[7:], False)
        forward()
        finish()
        _exchange_wait(s_in, s_recv, *sems[4:7], False)
        _exchange_wait([b_ref], [b_recv], *sems[7:], False)
        for recv, out in zip([*s_recv, b_recv], [*s_out, b_out]):
            acc = recv[0]
            for dev in range(1, N_DEV):
                acc = acc + recv[dev]
            out[...] = acc

    vmem = pl.BlockSpec(memory_space=pltpu.VMEM)
    slot = part.shape[1:]
    outs = pl.pallas_call(
        body,
        name="reduce_grads",
        out_shape=[jax.ShapeDtypeStruct(p.shape[1:], F32) for p in [part] + landed]
        + [jax.ShapeDtypeStruct(s.shape, F32) for s in smalls] + [jax.ShapeDtypeStruct((heads, LANES), F32)],
        in_specs=[vmem] * n_in,
        out_specs=[vmem] * n_out,
        scratch_shapes=[pltpu.VMEM((N_DEV,) + s.shape, F32) for s in smalls]
        + [pltpu.VMEM((N_DEV, heads, LANES), F32), pltpu.VMEM((heads, LANES), F32)] + _chip_reduce_scratch(slot)
        + _exchange_sems(ng) + _exchange_sems(1),
        compiler_params=_params(),
    )(part, *landed, *smalls, by_bucket, bucket_row)
    return outs[0], outs[1:1 + nl], outs[1 + nl:n_out - 1], outs[n_out - 1]


def _layer_a_fwd(x2, sm, win_g, wout, later, ts):
    seq, d = x2.shape
    width = wout.shape[0]
    half = win_g.shape[2]
    n_half = width // half
    nl = len(later)
    nt = seq // ts

    def body(x_ref, sm_ref, win_ref, wout_ref, *refs):
        shard_refs, refs = refs[:nl], refs[nl:]
        h1_ref, n1_ref, proj_ref, conv_ref, y_ref, ya_ref = refs[:6]
        gathered_refs, (vprev_ref, *sems) = refs[6:6 + nl], refs[6 + nl:]

        @pl.when(pl.program_id(0) == 0)
        def _():
            vprev_ref[...] = jnp.zeros_like(vprev_ref)
            _exchange_start(shard_refs, gathered_refs, *sems, False)

        @pl.when(pl.program_id(0) == nt - 1)
        def _():
            _exchange_wait(shard_refs, gathered_refs, *sems, False)

        xf = x_ref[...]
        xn, _ = _rms(xf)
        n1 = (xn * sm_ref[0:1, :]).astype(BF16)
        n1_ref[...] = n1
        row = lax.broadcasted_iota(jnp.int32, (ts, half), 0)
        ya = jnp.zeros((ts, d), F32)
        for hh in range(n_half):
            cols = slice(hh * half, (hh + 1) * half)
            parts = []
            for part in range(4):
                j = part * n_half + hh
                pj = _dot(n1, win_ref[j])
                proj_ref[:, j * half:(j + 1) * half] = pj.astype(BF16)
                parts.append(pj)
            b, c, u, z = parts
            v = c * u
            last1, last2 = vprev_ref[7:8, cols], vprev_ref[6:7, cols]
            v1 = jnp.where(row == 0, last1, pltpu.roll(v, 1, 0))
            v2 = jnp.where(row == 0, last2, jnp.where(row == 1, last1, pltpu.roll(v, 2, 0)))
            vprev_ref[:, cols] = v[ts - 8:ts, :]
            conv = sm_ref[1:2, cols] * v2 + sm_ref[2:3, cols] * v1 + sm_ref[3:4, cols] * v
            conv_ref[:, cols] = conv.astype(BF16)
            yh = (b * conv * _silu(z)[0]).astype(BF16)
            y_ref[:, cols] = yh
            ya = ya + _dot(yh, wout_ref[cols, :])
        ya_ref[...] = ya
        h1_ref[...] = xf + _rms(ya)[0] * sm_ref[4:5, :]

    outs = pl.pallas_call(
        body,
        name="layer_a_fwd",
        grid=(nt,),
        in_specs=[_rows(ts, d), _full(sm.shape), _full(win_g.shape), _full(wout.shape)] + [HBM_SPEC] * nl,
        out_specs=[_rows(ts, d), _rows(ts, d), _rows(ts, 4 * width), _rows(ts, width), _rows(ts, width), _rows(ts, d)]
        + [HBM_SPEC] * nl,
        out_shape=[
            jax.ShapeDtypeStruct((seq, d), F32),
            jax.ShapeDtypeStruct((seq, d), BF16),
            jax.ShapeDtypeStruct((seq, 4 * width), BF16),
            jax.ShapeDtypeStruct((seq, width), BF16),
            jax.ShapeDtypeStruct((seq, width), BF16),
            jax.ShapeDtypeStruct((seq, d), F32),
        ] + [jax.ShapeDtypeStruct((N_DEV,) + s.shape, s.dtype) for s in later],
        scratch_shapes=[pltpu.VMEM((8, width), F32)] + _exchange_sems(nl),
        compiler_params=_params(("arbitrary",)),
    )(x2, sm, win_g, wout, *later)
    return outs[:6], outs[6:]


Q_BLOCKS = 4
ATTN_BWD_LAGS = (2, 4)
ATTN_FWD_LAGS = (2, 4)


def _banded_tiles(kvp_ref, kvc_ref):
    tile = kvc_ref[...].astype(F32)
    blocks = [kvp_ref[...].astype(F32)] + [tile[u * BLOCK:(u + 1) * BLOCK] for u in range(Q_BLOCKS)]
    return [_banded_kv(blocks[u], blocks[u + 1]) for u in range(Q_BLOCKS)]


def _bias_of(bias_ref, i, u, m):
    return bias_ref[jnp.minimum(i, 1) if u == 0 else 1, m]


def _banded_kv(kvp, kvc):
    kw = N_KV_HEADS * HEAD_DIM
    out = []
    for full in (jnp.concatenate([kvp[:, :kw], kvc[:, :kw]], axis=0), jnp.concatenate([kvp[:, kw:], kvc[:, kw:]], axis=0)):
        lo = lax.broadcasted_iota(jnp.int32, full.shape, 1) < HEAD_DIM
        rolled = pltpu.roll(full, HEAD_DIM, 1)
        x2 = [jnp.where(lo, full, rolled).astype(BF16), jnp.where(lo, rolled, full).astype(BF16)]
        ft = full.T
        x2t = [jnp.concatenate([ft[kh * HEAD_DIM:(kh + 1) * HEAD_DIM]] * 2, axis=0).astype(BF16) for kh in range(N_KV_HEADS)]
        out += [x2, x2t]
    return out


def _pair_rows(ref, rows, m, scale=None):
    both = ref[rows, m * LANES:(m + 1) * LANES].astype(F32)
    if scale is not None:
        both = both * scale
    lo = lax.broadcasted_iota(jnp.int32, both.shape, 1) < HEAD_DIM
    zero = jnp.zeros_like(both)
    return jnp.concatenate([jnp.where(lo, both, zero), jnp.where(lo, zero, both)], axis=0).astype(BF16)


def _pair_cols(res_t):
    top = lax.broadcasted_iota(jnp.int32, (LANES, BLOCK), 0) < HEAD_DIM
    return jnp.where(top, res_t[:, :BLOCK], res_t[:, BLOCK:]).T


def _sink_row(sink_ref, m):
    first = lax.broadcasted_iota(jnp.int32, (1, 2 * BLOCK), 1) < BLOCK
    return jnp.where(first, sink_ref[0, 2 * m], sink_ref[0, 2 * m + 1])


def _softmax_t(logits, sink):
    mx =jnp.maximum(jnp.max(logits, axis=0, keepdims=True), sink)
    p = jnp.exp(logits - mx)
    sink_p = jnp.exp(sink - mx)
    inv = 1.0 / (jnp.sum(p, axis=0, keepdims=True) + sink_p)
    return p * inv, sink_p * inv


def _layer_b_fwd(h1, target, kvn, bpre, wkv, wbin_g, biasm, sinks, wbout, bpost):
    seq, d = h1.shape
    kvw = wkv.shape[1]
    cw = wbin_g.shape[2]
    aw = N_Q_HEADS * HEAD_DIM
    per = aw // cw
    tile = Q_BLOCKS * BLOCK

    def body(sink_ref, h1_ref, tgt_ref, kvn_ref, bpre_ref, wkv_ref, wbin_ref, bias_ref, w_ref, g_ref,
             n3_ref, n4_ref, kvc_ref, q_ref, o_ref, dh2_ref, dyb_ref, dattn_ref, dz2_ref, acc_ref,
             attn_ref, z2_ref, kvp_ref):
        i = pl.program_id(0)

        @pl.when(i == 0)
        def _():
            acc_ref[...] = jnp.zeros_like(acc_ref)
            kvp_ref[...] = jnp.zeros_like(kvp_ref)

        hn, _ = _rms(h1_ref[...])
        n3 = (hn * kvn_ref[...]).astype(BF16)
        n4 = (hn * bpre_ref[...]).astype(BF16)
        n3_ref[...] = n3
        n4_ref[...] = n4
        kvc_ref[...] = _dot(n3, wkv_ref[...]).astype(BF16)
        for j in range(N_DEV):
            pj = _dot(n4, wbin_ref[j])
            if j < per:
                q_ref[:, j * cw:(j + 1) * cw] = pj.astype(BF16)
            else:
                z2_ref[:, (j - per) * cw:(j - per + 1) * cw] = pj

        banded = _banded_tiles(kvp_ref, kvc_ref)
        kvp_ref[...] = kvc_ref[tile - BLOCK:tile, :]
        units = [(u, m) for u in range(Q_BLOCKS) for m in range(N_PAIRS)]
        kv_of = lambda m: (2 * m) // GROUP
        logits, probs = {}, {}
        lag_b, lag_c = ATTN_FWD_LAGS
        for step in range(len(units) + lag_c):
            if step < len(units):
                u, m = units[step]
                qpair = _pair_rows(q_ref, slice(u * BLOCK, (u + 1) * BLOCK), m, SCALE)
                logits[step] = _dot_nt(banded[u][0][kv_of(m)], qpair) + _bias_of(bias_ref, i, u, m)
            if 0 <= step - lag_b < len(units):
                u, m = units[step - lag_b]
                probs[step - lag_b] = _softmax_t(logits.pop(step - lag_b), _sink_row(sink_ref, m))[0].astype(BF16)
            if 0 <= step - lag_c < len(units):
                u, m = units[step - lag_c]
                out_t = _dot(banded[u][3][kv_of(m)], probs.pop(step - lag_c))
                attn_ref[u * BLOCK:(u + 1) * BLOCK, m * LANES:(m + 1) * LANES] = _pair_cols(out_t)
        attn = attn_ref[...]
        sz, dsz = _silu(z2_ref[...])
        o = (attn * sz).astype(BF16)
        o_ref[...] = o

        w = w_ref[...]
        yb = _dot(o, w)
        ybn, r = _rms(yb)
        g = g_ref[...]
        diff = h1_ref[...] + ybn * g - tgt_ref[...]
        dh2 = diff * (1.0 / d)
        dh2_ref[...] = dh2
        acc_ref[0:1, :] += jnp.sum(dh2 * ybn, axis=0, keepdims=True)
        tok = jnp.mean(diff * diff, axis=-1, keepdims=True)
        acc_ref[1:2, :] += 0.5 * jnp.sum(tok, axis=0, keepdims=True)
        dyb = _rms_bwd(dh2 * g, ybn, r).astype(BF16)
        dyb_ref[...] = dyb
        do = _dot_nt(dyb, w)
        dattn_ref[...] = (do * sz).astype(BF16)
        dz2_ref[...] = (do * attn * dsz).astype(BF16)

    blk = lambda w: pl.BlockSpec((tile, w), lambda i: (i, 0))
    return pl.pallas_call(
        body,
        name="layer_b_fwd",
        grid=(seq // tile,),
        in_specs=[
            pl.BlockSpec(memory_space=pltpu.SMEM),
            blk(d),
            blk(d),
            _full(kvn.shape),
            _full(bpre.shape),
            _full(wkv.shape),
            _full(wbin_g.shape),
            _full(biasm.shape),
            _full(wbout.shape),
            _full(bpost.shape),
        ],
        out_specs=[blk(d), blk(d), blk(kvw), blk(aw), blk(aw), blk(d), blk(d), blk(aw), blk(aw), _resident((8, d))],
        out_shape=[
            jax.ShapeDtypeStruct((seq, d), BF16),
            jax.ShapeDtypeStruct((seq, d), BF16),
            jax.ShapeDtypeStruct((seq, kvw), BF16),
            jax.ShapeDtypeStruct((seq, aw), BF16),
            jax.ShapeDtypeStruct((seq, aw), BF16),
            jax.ShapeDtypeStruct((seq, d), F32),
            jax.ShapeDtypeStruct((seq, d), BF16),
            jax.ShapeDtypeStruct((seq, aw), BF16),
            jax.ShapeDtypeStruct((seq, aw), BF16),
            jax.ShapeDtypeStruct((8, d), F32),
        ],
        scratch_shapes=[pltpu.VMEM((tile, aw), F32), pltpu.VMEM((tile, aw), F32), pltpu.VMEM((BLOCK, kvw), BF16)],
        compiler_params=_params(("arbitrary",)),
    )(sinks, h1, target, kvn, bpre, wkv, wbin_g, biasm, wbout, bpost)


def _attn_bwd(q, kv, dattn, biasm, sinks, ready):
    seq, aw = q.shape
    kvw = kv.shape[1]
    kw = N_KV_HEADS * HEAD_DIM
    nb = seq // BLOCK
    pairs_per_kv = N_PAIRS // N_KV_HEADS
    nr = len(ready)

    tile = Q_BLOCKS * BLOCK
    nsteps = seq // tile
    held = (Q_BLOCKS - 1) * BLOCK

    def body(sink_ref, q_ref, kvc_ref, kvp_ref, da_ref, bias_ref, *refs):
        ready_refs, (dq_ref, dkv_ref, dssum_ref, dsink_ref) = refs[:nr], refs[nr:nr + 4]
        landed_refs, scratch = refs[nr + 4:2 * nr + 4], refs[2 * nr + 4:]
        carry_ref, done_ref, qs_ref, dos_ref, dst_ref, pt_ref, *sems = scratch
        i = pl.program_id(0)

        @pl.when(i == 0)
        def _():
            dssum_ref[...] = jnp.zeros_like(dssum_ref)
            dsink_ref[...] = jnp.zeros_like(dsink_ref)
            carry_ref[...] = jnp.zeros_like(carry_ref)
            done_ref[...] = jnp.zeros_like(done_ref)
            if nr:
                _exchange_start(ready_refs, landed_refs, *sems, True)

        if nr:
            @pl.when(i == nsteps)
            def _():
                _exchange_wait(ready_refs, landed_refs, *sems, True)

        @pl.when(i < nsteps)
        def _():
            lo = lax.broadcasted_iota(jnp.int32, (BAND, LANES), 1) < HEAD_DIM
            head_lane = lax.broadcasted_iota(jnp.int32, (1, LANES), 1)
            banded = _banded_tiles(kvp_ref, kvc_ref)
            units = [(u, m) for u in range(Q_BLOCKS) for m in range(N_PAIRS)]
            dsink = jnp.zeros((1, LANES), F32)
            folded = {}
            logits, dps, dsbs = {}, {}, {}
            lag_b, lag_c = ATTN_BWD_LAGS
            for step in range(len(units) + lag_c):
                if step < len(units):
                    u, m = units[step]
                    kh, rows = m // pairs_per_kv, slice((m % pairs_per_kv) * BAND, (m % pairs_per_kv + 1) * BAND)
                    qrows = slice(u * BLOCK, (u + 1) * BLOCK)
                    qpair = _pair_rows(q_ref, qrows, m, SCALE)
                    dopair = _pair_rows(da_ref, qrows, m)
                    qs_ref[u, kh, rows, :] = qpair
                    dos_ref[u, kh, rows, :] = dopair
                    logits[step] = _dot_nt(banded[u][0][kh], qpair) + _bias_of(bias_ref, i, u, m)
                    dps[step] = _dot_nt(banded[u][2][kh], dopair)
                if 0 <= step - lag_b < len(units):
                    u, m = units[step - lag_b]
                    kh, rows = m // pairs_per_kv, slice((m % pairs_per_kv) * BAND, (m % pairs_per_kv + 1) * BAND)
                    pn, sink_p = _softmax_t(logits.pop(step - lag_b), _sink_row(sink_ref, m))
                    dp = dps.pop(step - lag_b)
                    delta = jnp.sum(pn * dp, axis=0, keepdims=True)
                    ds = pn * (dp - delta)
                    dssum_ref[m] += ds
                    sink_term = sink_p * delta
                    for e in range(2):
                        total = jnp.sum(sink_term[:, e * BLOCK:(e + 1) * BLOCK], axis=1, keepdims=True)
                        dsink = dsink - jnp.where(head_lane == 2 * m + e, total, 0.0)
                    dsbs[step - lag_b] = ds.astype(BF16)
                    dst_ref[u, kh, :, rows] = dsbs[step - lag_b]
                    pt_ref[u, kh, :, rows] = pn.astype(BF16)
                if 0 <= step - lag_c < len(units):
                    u, m = units[step - lag_c]
                    kh = m // pairs_per_kv
                    dq_t = _dot(banded[u][1][kh], dsbs.pop(step - lag_c))
                    dq_ref[u * BLOCK:(u + 1) * BLOCK, m * LANES:(m + 1) * LANES] = (_pair_cols(dq_t) * SCALE).astype(BF16)
                    if m % pairs_per_kv == pairs_per_kv - 1:
                        for name, lhs_ref, rhs_ref in (("k", dst_ref, qs_ref), ("v", pt_ref, dos_ref)):
                            acc = _dot(lhs_ref[u, kh], rhs_ref[u, kh])
                            folded[u, kh, name] = acc + pltpu.roll(acc, HEAD_DIM, 1)
            dsink_ref[0:1, :] += dsink
            dkv = [jnp.concatenate([jnp.where(lo, folded[u, 0, n], folded[u, 1, n]) for n in ("k", "v")], axis=1)
                   for u in range(Q_BLOCKS)]

            @pl.when(i > 0)
            def _():
                if held:
                    dkv_ref[:held, :] = done_ref[...].astype(BF16)
                dkv_ref[held:, :] = (carry_ref[...] + dkv[0][:BLOCK]).astype(BF16)

            for u in range(Q_BLOCKS - 1):
                done_ref[u * BLOCK:(u + 1) * BLOCK, :] = dkv[u][BLOCK:] + dkv[u + 1][:BLOCK]
            carry_ref[...] = dkv[Q_BLOCKS - 1][BLOCK:]

        @pl.when(i == nsteps)
        def _():
            if held:
                dkv_ref[:held, :] = done_ref[...].astype(BF16)
            dkv_ref[held:, :] = carry_ref[...].astype(BF16)

    last = nsteps - 1
    blk = lambda w: pl.BlockSpec((tile, w), lambda i: (jnp.minimum(i, last), 0))
    outs = pl.pallas_call(
        body,
        name="attn_bwd",
        grid=(nsteps + 1,),
        in_specs=[
            pl.BlockSpec(memory_space=pltpu.SMEM),
            blk(aw),
            blk(kvw),
            pl.BlockSpec((BLOCK, kvw), lambda i: (jnp.clip(Q_BLOCKS * i - 1, 0, nb - 1), 0)),
            blk(aw),
            _full(biasm.shape),
        ] + [HBM_SPEC] * nr,
        out_specs=[
            blk(aw),
            pl.BlockSpec((tile, kvw), lambda i: (jnp.maximum(i - 1, 0), 0)),
            _resident(biasm.shape[1:]),
            _resident((8, LANES)),
        ] + [HBM_SPEC] * nr,
        out_shape=[
            jax.ShapeDtypeStruct((seq, aw), BF16),
            jax.ShapeDtypeStruct((seq, kvw), BF16),
            jax.ShapeDtypeStruct(biasm.shape[1:], F32),
            jax.ShapeDtypeStruct((8, LANES), F32),
        ] + [jax.ShapeDtypeStruct(g.shape, g.dtype) for g in ready],
        scratch_shapes=[
            pltpu.VMEM((BLOCK, kvw), F32),
            pltpu.VMEM((max(held, 8), kvw), F32),
            pltpu.VMEM((Q_BLOCKS, N_KV_HEADS, pairs_per_kv * BAND, LANES), BF16),
            pltpu.VMEM((Q_BLOCKS, N_KV_HEADS, pairs_per_kv * BAND, LANES), BF16),
            pltpu.VMEM((Q_BLOCKS, N_KV_HEADS, BAND, pairs_per_kv * BAND), BF16),
            pltpu.VMEM((Q_BLOCKS, N_KV_HEADS, BAND, pairs_per_kv * BAND), BF16),
        ] + _exchange_sems(nr),
        compiler_params=_params(("arbitrary",)),
    )(sinks, q, kv, kv, dattn, biasm, *ready)
    return outs[:4], outs[4:]


def _layer_b_in_bwd(dh2, dq, dz2, dkv, h1, ya, wbin_g, wkv, kvn, bpre, sm, ready, ts):
    seq, d = h1.shape
    aw = dq.shape[1]
    kvw = dkv.shape[1]
    cw = wbin_g.shape[2]
    per = aw // cw

    nr = len(ready)
    nt = seq // ts

    def body(dh2_ref, dq_ref, dz2_ref, dkv_ref, h1_ref, ya_ref, wbin_ref, wkv_ref, kvn_ref, bpre_ref, sm_ref, *refs):
        ready_refs, (dh1_ref, dya_ref, acc_ref) = refs[:nr], refs[nr:nr + 3]
        landed_refs, sems = refs[nr + 3:2 * nr + 3], refs[2 * nr + 3:]

        @pl.when(pl.program_id(0) == 0)
        def _():
            acc_ref[...] = jnp.zeros_like(acc_ref)
            if nr:
                _exchange_start(ready_refs, landed_refs, *sems, True)

        if nr:
            @pl.when(pl.program_id(0) == nt - 1)
            def _():
                _exchange_wait(ready_refs, landed_refs, *sems, True)

        dn4 = jnp.zeros((ts, d), F32)
        for j in range(N_DEV):
            src = dq_ref if j < per else dz2_ref
            jj = j % per
            dn4 = dn4 + _dot_nt(src[:, jj * cw:(jj + 1) * cw], wbin_ref[j])
        dn3 = _dot_nt(dkv_ref[...], wkv_ref[...])
        hn, r = _rms(h1_ref[...])
        acc_ref[0:1, :] += jnp.sum(dn4 * hn, axis=0, keepdims=True)
        acc_ref[1:2, :] += jnp.sum(dn3 * hn, axis=0, keepdims=True)
        dh1 = dh2_ref[...] + _rms_bwd(dn4 * bpre_ref[...] + dn3 * kvn_ref[...], hn, r)
        dh1_ref[...] = dh1
        yan, r2 = _rms(ya_ref[...])
        acc_ref[2:3, :] += jnp.sum(dh1 * yan, axis=0, keepdims=True)
        dya_ref[...] = _rms_bwd(dh1 * sm_ref[4:5, :], yan, r2).astype(BF16)

    outs = pl.pallas_call(
        body,
        name="layer_b_in_bwd",
        grid=(nt,),
        in_specs=[_rows(ts, d), _rows(ts, aw), _rows(ts, aw), _rows(ts, kvw), _rows(ts, d), _rows(ts, d),
                  _full(wbin_g.shape), _full(wkv.shape), _full(kvn.shape), _full(bpre.shape), _full(sm.shape)]
        + [HBM_SPEC] * nr,
        out_specs=[_rows(ts, d), _rows(ts, d), _resident((8, d))] + [HBM_SPEC] * nr,
        out_shape=[jax.ShapeDtypeStruct((seq, d), F32), jax.ShapeDtypeStruct((seq, d), BF16),
                   jax.ShapeDtypeStruct((8, d), F32)] + [jax.ShapeDtypeStruct(g.shape, g.dtype) for g in ready],
        scratch_shapes=_exchange_sems(nr),
        compiler_params=_params(("arbitrary",)),
    )(dh2, dq, dz2, dkv, h1, ya, wbin_g, wkv, kvn, bpre, sm, *ready)
    return outs[:3], outs[3:]


def _layer_a_bwd(dya, proj, conv, dh1, x2, wout, win_g, sm, ts):
    seq, d = x2.shape
    width = wout.shape[0]
    half = win_g.shape[2]
    n_half = width // half
    nt = seq // ts

    def body(dya_ref, proj_ref, conv_ref, dh1_ref, x_ref, wout_ref, win_ref, sm_ref, dproj_ref, gx_ref, acc_ref,
             dnext_ref):
        @pl.when(pl.program_id(0) == 0)
        def _():
            acc_ref[...] = jnp.zeros_like(acc_ref)
            dnext_ref[...] = jnp.zeros_like(dnext_ref)

        dy = _dot_nt(dya_ref[...], wout_ref[...])
        row = lax.broadcasted_iota(jnp.int32, (ts, half), 0)
        dn1 = jnp.zeros((ts, d), F32)
        for hh in range(n_half):
            cols = slice(hh * half, (hh + 1) * half)
            b, c, u, z = [proj_ref[:, (part * n_half + hh) * half:(part * n_half + hh + 1) * half].astype(F32)
                          for part in range(4)]
            cv = conv_ref[:, cols].astype(F32)
            dyh = dy[:, cols]
            sz, dsz = _silu(z)
            dconv = dyh * b * sz
            grads = [dyh * cv * sz, None, None, dyh * b * cv * dsz]
            next0, next1 = dnext_ref[0:1, cols], dnext_ref[1:2, cols]
            dc1 = jnp.where(row == ts - 1, next0, pltpu.roll(dconv, ts - 1, 0))
            dc2 = jnp.where(row == ts - 1, next1, jnp.where(row == ts - 2, next0, pltpu.roll(dconv, ts - 2, 0)))
            dnext_ref[:, cols] = dconv[0:8, :]
            v = c * u
            acc_ref[1:2, cols] += jnp.sum(dc2 * v, axis=0, keepdims=True)
            acc_ref[2:3, cols] += jnp.sum(dc1 * v, axis=0, keepdims=True)
            acc_ref[3:4, cols] += jnp.sum(dconv * v, axis=0, keepdims=True)
            dv = sm_ref[3:4, cols] * dconv + sm_ref[2:3, cols] * dc1 + sm_ref[1:2, cols] * dc2
            grads[1] = dv * u
            grads[2] = dv * c
            for part in range(4):
                j = part * n_half + hh
                gj = grads[part].astype(BF16)
                dproj_ref[:, j * half:(j + 1) * half] = gj
                dn1 = dn1 + _dot_nt(gj, win_ref[j])
        xn, r = _rms(x_ref[...])
        acc_ref[0:1, :] += jnp.sum(dn1 * xn, axis=0, keepdims=True)
        gx_ref[...] = dh1_ref[...] + _rms_bwd(dn1 * sm_ref[0:1, :], xn, r)

    rev = lambda w: pl.BlockSpec((ts, w), lambda i: (nt - 1 - i, 0))
    return pl.pallas_call(
        body,
        name="layer_a_bwd",
        grid=(nt,),
        in_specs=[rev(d), rev(4 * width), rev(width), rev(d), rev(d), _full(wout.shape), _full(win_g.shape), _full(sm.shape)],
        out_specs=[rev(4 * width), rev(d), _resident((8, d))],
        out_shape=[jax.ShapeDtypeStruct((seq, 4 * width), BF16), jax.ShapeDtypeStruct((seq, d), F32),
                   jax.ShapeDtypeStruct((8, d), F32)],
        scratch_shapes=[pltpu.VMEM((8, width), F32)],
        compiler_params=_params(("arbitrary",)),
    )(dya, proj, conv, dh1, x2, wout, win_g, sm)


def _wgrad(a, bs, n_slots, ts, name, ready=(), block_cols=1024):
    nr = len(ready)
    seq, k = a.shape
    nb_in = len(bs)
    n_each = bs[0].shape[1]
    n = nb_in * n_each
    bn = min(n_each, block_cols)
    per_in = n_each // bn
    n_blocks = nb_in * per_in
    ns = seq // ts

    def b_spec(idx):
        def index(j, s):
            mine = j // per_in == idx
            row = jnp.where(mine, s, jnp.where(j // per_in > idx, ns - 1, 0))
            return (row, jnp.where(mine, j % per_in, jnp.where(j // per_in > idx, per_in - 1, 0)))
        return pl.BlockSpec((ts, bn), index)

    if n_slots:
        sw = n // n_slots
        spb = bn // sw
        out_shape = jax.ShapeDtypeStruct((n_slots, k, sw), BF16)
        out_spec = pl.BlockSpec((spb, k, sw), lambda j, s: (j, 0, 0))
    else:
        out_shape = jax.ShapeDtypeStruct((k, n), BF16)
        out_spec = pl.BlockSpec((k, bn), lambda j, s: (0, j))

    def body(a_ref, *refs):
        b_refs, ready_refs, o_ref = refs[:nb_in], refs[nb_in:nb_in + nr], refs[nb_in + nr]
        landed_refs, (acc_ref, *sems) = refs[nb_in + nr + 1:nb_in + 2 * nr + 1], refs[nb_in + 2 * nr + 1:]
        j, s = pl.program_id(0), pl.program_id(1)

        if nr:
            @pl.when(jnp.logical_and(j == 0, s == 0))
            def _():
                _exchange_start(ready_refs, landed_refs, *sems, True)

            @pl.when(jnp.logical_and(j == n_blocks - 1, s == ns - 1))
            def _():
                _exchange_wait(ready_refs, landed_refs, *sems, True)

        @pl.when(s == 0)
        def _():
            acc_ref[...] = jnp.zeros_like(acc_ref)

        for idx in range(nb_in):
            @pl.when(j // per_in == idx)
            def _(idx=idx):
                acc_ref[...] += _dot_tn(a_ref[...], b_refs[idx][...])

        @pl.when(s == ns - 1)
        def _():
            if n_slots:
                for e in range(spb):
                    o_ref[e] = acc_ref[:, e * sw:(e + 1) * sw].astype(BF16)
            else:
                o_ref[...] = acc_ref[...].astype(BF16)

    outs = pl.pallas_call(
        body,
        name=name,
        grid=(n_blocks, ns),
        in_specs=[pl.BlockSpec((ts, k), lambda j, s: (s, 0))] + [b_spec(idx) for idx in range(nb_in)] + [HBM_SPEC] * nr,
        out_specs=[out_spec] + [HBM_SPEC] * nr,
        out_shape=[out_shape] + [jax.ShapeDtypeStruct(g.shape, g.dtype) for g in ready],
        scratch_shapes=[pltpu.VMEM((k, bn), F32)] + (_exchange_sems(nr) if nr else []),
        compiler_params=_params(("arbitrary", "arbitrary")),
    )(a, *bs, *ready)
    return (outs[0], outs[1:]) if nr else outs[0]


def _wgrad_tail(pairs, part, landed, ts):
    n_tasks = len(pairs)
    assert n_tasks == 2
    nl = len(landed)
    seq, k = pairs[0][0].shape
    n = pairs[0][1].shape[1]
    ns = seq // ts
    total = n_tasks * ns
    per = k // N_DEV
    n_red = len(_chip_reduce_scratch((per, n)))

    def spec(t, width):
        return pl.BlockSpec((ts, width), lambda j, s: (jnp.where(j == t, s, jnp.where(j > t, ns - 1, 0)), 0))

    def body(*refs):
        ab_refs, part_hbm = refs[:2 * n_tasks], refs[2 * n_tasks]
        landed_hbm, refs = refs[2 * n_tasks + 1:2 * n_tasks + 1 + nl], refs[2 * n_tasks + 1 + nl:]
        o_ref, red_ref, early_ref = refs[:3]
        summed_refs, (acc_ref, first_ref, part_ref, *scratch) = refs[3:3 + nl], refs[3 + nl:]
        landed_refs, load_sems, scratch = scratch[:nl], scratch[nl], scratch[nl + 1:]
        j, s = pl.program_id(0), pl.program_id(1)
        flat = j * ns + s
        swap, send, forward, finish = _chip_reduce(part_ref, red_ref, *scratch[:3], scratch[3:n_red], part_hbm)
        swap_first, send_first, forward_first, finish_first = _chip_reduce(
            first_ref, early_ref, *scratch[n_red:n_red + 3], scratch[n_red + 3:])
        loads = [pltpu.make_async_copy(src, dst, load_sems.at[i])
                 for i, (src, dst) in enumerate(zip([part_hbm, *landed_hbm], [part_ref, *landed_refs]))]

        @pl.when(flat == 0)
        def _():
            swap()
            for load in loads:
                load.start()

        @pl.when(flat == min(1, total - 1))
        def _():
            loads[0].wait()
            send()

        @pl.when(flat == min(total // 2 + 1, total - 1))
        def _():
            forward()
            for t in range(nl):
                loads[1 + t].wait()
                _sum_slots(landed_refs[t], summed_refs[t])

        @pl.when(flat == ns)
        def _():
            send_first()

        @pl.when(flat == min(ns + ns // 2, total - 1))
        def _():
            forward_first()

        @pl.when(s == 0)
        def _():
            acc_ref[...] = jnp.zeros_like(acc_ref)

        for t in range(n_tasks):
            @pl.when(j == t)
            def _(t=t):
                acc_ref[...] += _dot_tn(ab_refs[2 * t][...], ab_refs[2 * t + 1][...])

        @pl.when(flat == ns - 1)
        def _():
            for dev in range(N_DEV):
                first_ref[dev] = acc_ref[dev * per:(dev + 1) * per, :].astype(BF16)
            swap_first()

        @pl.when(flat == total - 1)
        def _():
            for dev in range(N_DEV):
                o_ref[dev] = acc_ref[dev * per:(dev + 1) * per, :].astype(BF16)
            finish()
            finish_first()

    slot = part.shape[1:]
    outs = pl.pallas_call(
        body,
        name="wgrad_tail",
        grid=(n_tasks, ns),
        in_specs=[spec(t, w) for t in range(n_tasks) for w in (k, n)] + [HBM_SPEC] * (1 + nl),
        out_specs=[_resident((N_DEV, per, n)), _resident(slot), _resident((per, n))]
        + [_resident(g.shape[1:]) for g in landed],
        out_shape=[jax.ShapeDtypeStruct((N_DEV, per, n), BF16), jax.ShapeDtypeStruct(slot, F32),
                   jax.ShapeDtypeStruct((per, n), F32)]
        + [jax.ShapeDtypeStruct(g.shape[1:], F32) for g in landed],
        scratch_shapes=[pltpu.VMEM((k, n), F32), pltpu.VMEM((N_DEV, per, n), BF16), pltpu.VMEM(part.shape, part.dtype)]
        + [pltpu.VMEM(g.shape, g.dtype) for g in landed] + [pltpu.SemaphoreType.DMA((1 + nl,))]
        + _chip_reduce_scratch(slot) + _chip_reduce_scratch((per, n)),
        compiler_params=_params(("arbitrary", "arbitrary")),
    )(*[op for pair in pairs for op in pair], part, *landed)
    return outs[0], outs[1], outs[2], outs[3:]


MINE = "mine"
ADAMW_STEPS = 4


def _adamw(ws, sources, picks, loss_at, ms, vs):
    n, n_src = len(ws), len(sources)
    streamed = [len(w.shape) == 2 and w.shape[0] >= 128 and picks[t][1:] == (0, None)
                and sources[picks[t][0]].shape == w.shape for t, w in enumerate(ws)]
    streamed_sources = {picks[t][0] for t in range(n) if streamed[t]}

    def step(w, g, m, v):
        m = ADAM_B1 * m + (1.0 - ADAM_B1) * g
        v = ADAM_B2 * v + (1.0 - ADAM_B2) * jnp.square(g)
        m_hat = m / (1.0 - ADAM_B1 ** ADAM_STEP)
        v_hat = v / (1.0 - ADAM_B2 ** ADAM_STEP)
        return g, -ADAM_LR * (m_hat / (jnp.sqrt(v_hat) + ADAM_EPS) + ADAM_WD * w), m, v

    def body(*refs):
        refs = list(refs)
        take = lambda k: [refs.pop(0) for _ in range(k)]
        w_refs, s_refs, m_refs, v_refs = take(n), take(n_src), take(n), take(n)
        (loss_ref,), go_refs, d_refs, nm_refs, nv_refs = take(1), take(n), take(n), take(n), take(n)
        me = _my_index()

        def grad(t, rows):
            k, first, cols = picks[t]
            if cols is None:
                return s_refs[k][rows, :]
            if cols is not MINE:
                return s_refs[k][rows, cols]
            width = w_refs[t].shape[-1]
            g = s_refs[k][rows, 0:width]
            for dev in range(1, N_DEV):
                g = jnp.where(me == dev, s_refs[k][rows, dev * width:(dev + 1) * width], g)
            return g

        def whole(t):
            first = picks[t][1]
            rows = w_refs[t].shape[0]
            if len(w_refs[t].shape) == 3:
                for j in range(rows):
                    go_refs[t][j], d_refs[t][j], nm_refs[t][j], nv_refs[t][j] = step(
                        w_refs[t][j], grad(t, slice(first + j, first + j + 1)), m_refs[t][j], v_refs[t][j])
                return
            go_refs[t][...], d_refs[t][...], nm_refs[t][...], nv_refs[t][...] = step(
                w_refs[t][...], grad(t, slice(first, first + rows)), m_refs[t][...], v_refs[t][...])

        def block(t):
            rows = w_refs[t].shape[0]
            chunk = min(rows, 128)

            def one(i, carry):
                r = pl.ds(pl.multiple_of(i * chunk, chunk), chunk)
                go_refs[t][r, :], d_refs[t][r, :], nm_refs[t][r, :], nv_refs[t][r, :] = step(
                    w_refs[t][r, :], grad(t, r), m_refs[t][r, :], v_refs[t][r, :])
                return carry

            lax.fori_loop(0, rows // chunk, one, 0)

        @pl.when(pl.program_id(0) == 0)
        def _():
            loss_ref[...] = s_refs[loss_at[0]][loss_at[1]:loss_at[1] + 1, 0:1]
            for t in range(n):
                if not streamed[t]:
                    whole(t)

        for t in range(n):
            if streamed[t]:
                block(t)

    def rows_of(shape):
        return pl.BlockSpec((shape[0] // ADAMW_STEPS, shape[1]), lambda i: (i, 0))

    w_in = [rows_of(w.shape) if streamed[t] else _full(w.shape) for t, w in enumerate(ws)]
    w_out = [rows_of(w.shape) if streamed[t] else _resident(w.shape) for t, w in enumerate(ws)]
    s_in = [rows_of(s.shape) if k in streamed_sources else _full(s.shape) for k, s in enumerate(sources)]
    outs = pl.pallas_call(
        body,
        name="adamw",
        grid=(ADAMW_STEPS,),
        in_specs=w_in + s_in + w_in * 2,
        out_specs=[_resident((1, 1))] + w_out * 4,
        out_shape=[jax.ShapeDtypeStruct((1, 1), F32)] + [jax.ShapeDtypeStruct(w.shape, F32) for w in ws] * 4,
        compiler_params=_params(("arbitrary",)),
    )(*ws, *sources, *ms, *vs)
    return outs[0], outs[1:n + 1], outs[n + 1:2 * n + 1], outs[2 * n + 1:3 * n + 1], outs[3 * n + 1:]


def _band_structure():
    q_loc = np.arange(BLOCK, dtype=np.int32)[:, None]
    s_loc = np.arange(2 * BLOCK, dtype=np.int32)[None, :]
    dist = q_loc + BLOCK - s_loc
    in_window = (dist >= 0) & (dist < BLOCK)
    dd = np.maximum(dist, 0)
    max_exact = N_BUCKETS // 2
    large = max_exact + (np.log(np.maximum(dd, 1) / max_exact) / math.log(MAX_DISTANCE / max_exact)
                         * (N_BUCKETS - max_exact)).astype(np.int32)
    bucket = np.where(dd < max_exact, dd, np.minimum(large, N_BUCKETS - 1)).astype(np.int32)
    return bucket, in_window.astype(np.int32)


def kernel(x, a_pre_norm, a_w_in, a_conv_w, a_w_out, a_post_norm, kv_norm, w_kv, rel_bias, b_pre_norm, b_w_in, b_sinks, b_w_out, b_post_norm, loss_target, m_a_pre_norm, m_a_w_in, m_a_conv_w, m_a_w_out, m_a_post_norm, m_kv_norm, m_w_kv, m_rel_bias, m_b_pre_norm, m_b_w_in, m_b_sinks, m_b_w_out, m_b_post_norm, v_a_pre_norm, v_a_w_in, v_a_conv_w, v_a_w_out, v_a_post_norm, v_kv_norm, v_w_kv, v_rel_bias, v_b_pre_norm, v_b_w_in, v_b_sinks, v_b_w_out, v_b_post_norm):
    seq, d = x.shape[1], x.shape[2]
    x2 = x.reshape(seq, d)
    target = loss_target.reshape(seq, d)
    shard = a_pre_norm.shape[1]
    ts_a = min(seq, 512)
    ts = min(seq, 512)
    ts_w = min(seq, 2048)

    taps = lambda a: a.transpose(1, 0, 2)
    bucket, in_window = _band_structure()
    (win_g, wout_g), small_g, later, biasm = _all_gather(
        [a_w_in[0], a_w_out[0]], [(0, a_pre_norm), (1, taps(a_conv_w)), (4, a_post_norm)],
        [w_kv, b_w_in[0], b_w_out[0]], rel_bias.T, bucket.T, in_window.T)
    wout = wout_g.reshape(-1, wout_g.shape[2])
    sm = small_g.transpose(1, 0, 2).reshape(8, N_DEV * shard)
    kvn = kv_norm.reshape(1, d)

    (h1, n1, proj, conv, y, ya), (wkv_g, wbin_g, wbout_g) = _layer_a_fwd(x2, sm, win_g, wout, later, ts_a)
    wkv = wkv_g.reshape(-1, wkv_g.shape[2])
    wbout = wbout_g.reshape(-1, wbout_g.shape[2])
    n3, n4, kv, q, o, dh2, dyb, dattn, dz2, acc_c = _layer_b_fwd(
        h1, target, kvn, b_pre_norm, wkv, wbin_g, biasm, b_sinks, wbout, b_post_norm)

    (dq, dkv, dssum, dsink), _ = _attn_bwd(q, kv, dattn, biasm, b_sinks, [])
    by_head = dssum.reshape(N_PAIRS, BAND, 2, BLOCK).transpose(0, 2, 3, 1)
    g_wkv = _wgrad(n3, [dkv], 0, ts_w, "wgrad_kv").reshape(wkv_g.shape)
    g_wbin = _wgrad(n4, [dq, dz2], N_DEV, ts_w, "wgrad_b_in")
    (dh1, dya, acc_b), _ = _layer_b_in_bwd(dh2, dq, dz2, dkv, h1, ya, wbin_g, wkv, kvn, b_pre_norm, sm, [], ts)
    dproj, gx, acc_a = _layer_a_bwd(dya, proj, conv, dh1, x2, wout, win_g, sm, ts_a)
    g_win, (l_wkv, l_wbin) = _wgrad(
        n1, [dproj], N_DEV, ts_w, "wgrad_a_in", ready=[g_wkv, g_wbin], block_cols=2048)
    g_wbout, r_win, r_wout, (r_wkv, r_wbin) = _wgrad_tail(
        [(y, dya), (o, dyb)], g_win, [l_wkv, l_wbin], min(seq, 1024))

    r_wbout, _, (s_a, s_b, s_c, s_sink), s_relb = _reduce_exchange(
        g_wbout, [], [acc_a, acc_b, acc_c, dsink], by_head.reshape(N_Q_HEADS, -1), bucket.reshape(1, -1), 4096)
    weights = [a_pre_norm, a_w_in[0], taps(a_conv_w), a_w_out[0], a_post_norm, kvn, w_kv, rel_bias.T, b_pre_norm,
               b_w_in[0], b_sinks, b_w_out[0], b_post_norm]
    sources = [s_a, s_b, s_c, s_relb, s_sink, r_win, r_wout, r_wkv, r_wbin, r_wbout]
    picks = [(0, 0, MINE), (5, 0, None), (0, 1, MINE), (6, 0, None), (1, 2, MINE), (1, 1, None), (7, 0, None),
             (3, 0, slice(0, N_BUCKETS)), (1, 0, None), (8, 0, None), (4, 0, slice(0, N_Q_HEADS)),
             (9, 0, None), (2, 0, None)]
    first = [m_a_pre_norm, m_a_w_in[0], taps(m_a_conv_w), m_a_w_out[0], m_a_post_norm, m_kv_norm.reshape(1, d),
             m_w_kv, m_rel_bias.T, m_b_pre_norm, m_b_w_in[0], m_b_sinks, m_b_w_out[0], m_b_post_norm]
    second = [v_a_pre_norm, v_a_w_in[0], taps(v_a_conv_w), v_a_w_out[0], v_a_post_norm, v_kv_norm.reshape(1, d),
              v_w_kv, v_rel_bias.T, v_b_pre_norm, v_b_w_in[0], v_b_sinks, v_b_w_out[0], v_b_post_norm]
    loss, grads, deltas, new_m, new_v = _adamw(weights, sources, picks, (2, 1), first, second)

    shapes = [a_pre_norm.shape, a_w_in.shape, taps, a_w_out.shape, a_post_norm.shape, kv_norm.shape,
              w_kv.shape, jnp.transpose, b_pre_norm.shape, b_w_in.shape, b_sinks.shape, b_w_out.shape, b_post_norm.shape]
    shaped = lambda arrays: [s(a) if callable(s) else a.reshape(s) for a, s in zip(arrays, shapes)]
    return (loss.reshape(()), gx.reshape(x.shape), *shaped(grads), *shaped(deltas), *shaped(new_m), *shaped(new_v))
```

```python
import math

import jax
import jax.numpy as jnp
import numpy as np
from jax import lax
from jax.experimental import pallas as pl
from jax.experimental.pallas import tpu as pltpu

HEAD_DIM = 64
N_Q_HEADS = 16
N_KV_HEADS = 2
GROUP = N_Q_HEADS // N_KV_HEADS
BLOCK = 128
N_BUCKETS = 32
MAX_DISTANCE = 128
EPS = 1e-6
NEG_INF = -1e30
SCALE = HEAD_DIM ** -0.5

ADAM_LR = 0.001
ADAM_B1 = 0.9
ADAM_B2 = 0.999
ADAM_EPS = 1e-08
ADAM_WD = 0.01
ADAM_STEP = 10

N_PAIRS = N_Q_HEADS // 2
BAND = 2 * BLOCK

N_DEV = 8
GATHER_PIECE_ROWS = 256
LANES = 128
F32 = jnp.float32
BF16 = jnp.bfloat16
MESH = pl.DeviceIdType.MESH
MIB = 1024 * 1024
VMEM_RESERVED_MIB = 63


def _params(semantics=None):
    return pltpu.CompilerParams(dimension_semantics=semantics, vmem_limit_bytes=VMEM_RESERVED_MIB * MIB)


def _full(shape):
    zeros = (0,) * len(shape)
    return pl.BlockSpec(shape, lambda *_: zeros, pipeline_mode=pl.Buffered(1))


def _resident(shape):
    zeros = (0,) * len(shape)
    return pl.BlockSpec(shape, lambda *_: zeros)


def _rows(ts, cols):
    return pl.BlockSpec((ts, cols), lambda i: (i, 0))


def _dot(a, b):
    return jnp.dot(a, b, preferred_element_type=F32)


def _dot_nt(a, b):
    return lax.dot_general(a, b, (((1,), (1,)), ((), ())), preferred_element_type=F32)


def _dot_tn(a, b):
    return lax.dot_general(a, b, (((0,), (0,)), ((), ())), preferred_element_type=F32)


def _rms(xf):
    r = lax.rsqrt(jnp.mean(xf * xf, axis=-1, keepdims=True) + EPS)
    return xf * r, r


def _rms_bwd(dn, xn, r):
    return r * (dn - xn * jnp.mean(dn * xn, axis=-1, keepdims=True))


def _silu(z):
    s = jax.nn.sigmoid(z)
    return z * s, s * (1.0 + z * (1.0 - s))


def _my_index():
    return 4 * lax.axis_index("x") + 2 * lax.axis_index("y") + lax.axis_index("c")


def _bias_table(rb_ref, bucket_ref, win_ref, out_ref):
    bk = jnp.where(win_ref[...] != 0, bucket_ref[...], -1)
    has_prev = lax.broadcasted_iota(jnp.int32, bk.shape, 0) >= BLOCK
    for h in range(N_Q_HEADS):
        acc = jnp.full(bk.shape, NEG_INF, F32)
        for b in range(N_BUCKETS):
            acc = jnp.where(bk == b, rb_ref[h, b], acc)
        cols = slice((h % 2) * BLOCK, (h % 2 + 1) * BLOCK)
        out_ref[1, h // 2, :, cols] = acc
        out_ref[0, h // 2, :, cols] = jnp.where(has_prev, acc, NEG_INF)


def _all_gather(shards, small_rows, casts, rel_bias_t, bucket_t, in_window_t):
    ns, nc, n = len(small_rows), len(casts), len(shards) + 1
    small_shape = (8, small_rows[0][1].shape[-1])
    shapes = [s.shape for s in shards] + [small_shape]
    pieces = [(t, r0, min(GATHER_PIECE_ROWS, shape[0] - r0))
              for t, shape in enumerate(shapes) for r0 in range(0, shape[0], GATHER_PIECE_ROWS)]

    def body(*refs):
        refs = list(refs)
        take = lambda k: [refs.pop(0) for _ in range(k)]
        ins, small_refs, cast_refs, (rb_ref, bucket_ref, win_ref) = take(n - 1), take(ns), take(nc), take(3)
        outs, cast_outs, (bias_ref, send_sems, recv_sems) = take(n), take(nc), take(3)
        x, y, c = lax.axis_index("x"), lax.axis_index("y"), lax.axis_index("c")
        me, sibling = (x, y, c), (x, y, 1 - c)
        x_nbr, y_nbr, diagonal = (1 - x, y), (x, 1 - y), (1 - x, 1 - y)
        south = c == 0
        relayed = (jnp.where(south, 1 - x, x), jnp.where(south, y, 1 - y))
        relay_to = (jnp.where(south, x, 1 - x), jnp.where(south, 1 - y, y))

        def copy(u, k, block, to):
            t, r0, nrows = pieces[u]
            rows = outs[t].at[4 * block[0] + 2 * block[1] + block[2], pl.ds(r0, nrows)]
            return pltpu.make_async_remote_copy(
                src_ref=rows, dst_ref=rows, send_sem=send_sems.at[u, k], recv_sem=recv_sems.at[u, k],
                device_id=to, device_id_type=MESH)

        mine = pl.ds(_my_index(), 1)
        for t in range(n - 1):
            outs[t][mine] = ins[t][...].astype(BF16)[None]
        outs[n - 1][mine] = jnp.zeros((1,) + small_shape, F32)
        for (row, _), ref in zip(small_rows, small_refs):
            if len(ref.shape) == 3:
                for j in range(ref.shape[0]):
                    outs[n - 1][mine, row + j:row + j + 1, :] = ref[j][None]
            else:
                outs[n - 1][mine, row:row + ref.shape[0], :] = ref[...][None]
        started = []

        def start(cp):
            cp.start()
            started.append(cp)

        units = range(len(pieces))
        for u in units:
            start(copy(u, 0, me, sibling))
            start(copy(u, 1, me, (*x_nbr, c)))
            start(copy(u, 2, me, (*y_nbr, c)))
        for u in units:
            for k, chip in ((1, x_nbr), (2, y_nbr)):
                copy(u, k, (*chip, c), me).wait_recv()
                start(copy(u, 3 + k, (*chip, c), sibling))
            start(copy(u, 3, (*relayed, c), (*relay_to, c)))
        for src, dst in zip(cast_refs, cast_outs):
            dst[...] = src[...].astype(BF16)
        _bias_table(rb_ref, bucket_ref, win_ref, bias_ref)
        for u in units:
            copy(u, 3, (*diagonal, c), me).wait_recv()
            start(copy(u, 6, (*diagonal, c), sibling))
        for u in units:
            copy(u, 0, sibling, me).wait_recv()
        for k, chip in ((4, x_nbr), (5, y_nbr), (6, diagonal)):
            for u in units:
                copy(u, k, (*chip, 1 - c), me).wait_recv()
        for cp in started:
            cp.wait_send()

    vmem = pl.BlockSpec(memory_space=pltpu.VMEM)
    outs = pl.pallas_call(
        body,
        name="gather_weights",
        out_shape=[jax.ShapeDtypeStruct((N_DEV,) + s.shape, BF16) for s in shards]
        + [jax.ShapeDtypeStruct((N_DEV,) + small_shape, F32)]
        + [jax.ShapeDtypeStruct(a.shape, BF16) for a in casts]
        + [jax.ShapeDtypeStruct((2, N_PAIRS, BAND, 2 * BLOCK), F32)],
        in_specs=[vmem] * (n - 1 + ns + nc) + [pl.BlockSpec(memory_space=pltpu.SMEM), vmem, vmem],
        out_specs=[vmem] * (n + nc + 1),
        scratch_shapes=[pltpu.SemaphoreType.DMA((len(pieces), 7)), pltpu.SemaphoreType.DMA((len(pieces), 7))],
        compiler_params=_params(),
    )(*shards, *[a for _, a in small_rows], *casts, rel_bias_t, bucket_t, in_window_t)
    return outs[:n - 1], outs[n - 1], outs[n:n + nc], outs[n + nc]


def _peer(k):
    x, y, c = lax.axis_index("x"), lax.axis_index("y"), lax.axis_index("c")
    px = 1 - x if k & 4 else x
    py = 1 - y if k & 2 else y
    pc = 1 - c if k & 1 else c
    return (px, py, pc), 4 * px + 2 * py + pc


def _exchange(srcs, dsts, send_sems, recv_sems, local_sems, scatter):
    me = _my_index()
    sends, arrivals = [], []
    for k in range(1, N_DEV):
        peer, pidx = _peer(k)
        for t, (src, dst) in enumerate(zip(srcs, dsts)):
            mine = src.at[pidx] if scatter else src
            sems = dict(send_sem=send_sems.at[t, k - 1], recv_sem=recv_sems.at[t, k - 1], device_id=peer, device_id_type=MESH)
            sends.append(pltpu.make_async_remote_copy(src_ref=mine, dst_ref=dst.at[me], **sems))
            arrivals.append(pltpu.make_async_remote_copy(src_ref=mine, dst_ref=dst.at[pidx], **sems))
    local = [pltpu.make_async_copy(src.at[me] if scatter else src, dst.at[me], local_sems.at[t])
             for t, (src, dst) in enumerate(zip(srcs, dsts))]
    return sends, arrivals, local


def _exchange_start(*args):
    sends, _, local = _exchange(*args)
    for cp in sends + local:
        cp.start()


def _exchange_wait(*args):
    sends, arrivals, local = _exchange(*args)
    for cp in arrivals:
        cp.wait_recv()
    for cp in sends:
        cp.wait_send()
    for cp in local:
        cp.wait()


def _exchange_sems(n):
    if not n:
        return []
    return [pltpu.SemaphoreType.DMA((n, N_DEV - 1)), pltpu.SemaphoreType.DMA((n, N_DEV - 1)), pltpu.SemaphoreType.DMA((n,))]


HBM_SPEC = pl.BlockSpec(memory_space=pl.ANY)


def _sum_slots(recv_ref, out_ref):
    rows = out_ref.shape[0]
    chunk = min(rows, 128)

    def add(i, carry):
        r0 = pl.multiple_of(i * chunk, chunk)
        acc = recv_ref[0, pl.ds(r0, chunk), :].astype(F32)
        for dev in range(1, N_DEV):
            acc = acc + recv_ref[dev, pl.ds(r0, chunk), :].astype(F32)
        out_ref[pl.ds(r0, chunk), :] = acc
        return carry

    lax.fori_loop(0, rows // chunk, add, 0)


N_CHIPS = N_DEV // 2


def _rows_loop(rows, fn):
    chunk = min(rows, 128)

    def step(i, carry):
        fn(pl.ds(pl.multiple_of(i * chunk, chunk), chunk))
        return carry

    lax.fori_loop(0, rows // chunk, step, 0)


def _chip_reduce(g_ref, out_ref, sib_ref, land_ref, send_ref, sems, swap_src=None):
    sib_send, sib_recv, ici_send, ici_recv = sems
    x, y, c = lax.axis_index("x"), lax.axis_index("y"), lax.axis_index("c")
    south = c == 0
    near =(jnp.where(south, 1 - x, x), jnp.where(south, y, 1 - y))
    far = (jnp.where(south, x, 1 - x), jnp.where(south, 1 - y, y))
    diagonal = (1 - x, 1 - y)
    rows = out_ref.shape[0]
    direct, fold, folded = 0, 1, 2

    def to_sibling(t):
        src = g_ref if swap_src is None else swap_src
        return pltpu.make_async_remote_copy(
            src_ref=src.at[2 * t + 1 - c], dst_ref=sib_ref.at[t], send_sem=sib_send.at[t], recv_sem=sib_recv.at[t],
            device_id=(x, y, 1 - c), device_id_type=MESH)

    def ici(role, chip):
        return pltpu.make_async_remote_copy(
            src_ref=send_ref.at[role], dst_ref=land_ref.at[role], send_sem=ici_send.at[role],
            recv_sem=ici_recv.at[role], device_id=(*chip, c), device_id_type=MESH)

    def pair_sum(chip, r):
        t = 2 * chip[0] + chip[1]
        return g_ref[2 * t + c, r, :].astype(F32) + sib_ref[t, r, :].astype(F32)

    def swap():
        for t in range(N_CHIPS):
            to_sibling(t).start()

    def send():
        for t in range(N_CHIPS):
            to_sibling(t).wait_recv()
        for role, chip in ((fold, diagonal), (direct, near)):
            def fill(r, role=role, chip=chip):
                send_ref[role, r, :] = pair_sum(chip, r).astype(BF16)

            _rows_loop(rows, fill)
            ici(role, near).start()

    def forward():
        ici(fold, near).wait_recv()

        def fill(r):
            send_ref[folded, r, :] = (pair_sum(far, r) + land_ref[fold, r, :].astype(F32)).astype(BF16)

        _rows_loop(rows, fill)
        ici(folded, far).start()

    def finish():
        ici(direct, near).wait_recv()
        ici(folded, far).wait_recv()

        def total(r):
            mine = pair_sum((x, y), r)
            out_ref[r, :] = mine + land_ref[direct, r, :].astype(F32) + land_ref[folded, r, :].astype(F32)

        _rows_loop(rows, total)
        for t in range(N_CHIPS):
            to_sibling(t).wait_send()
        for role, chip in ((direct, near), (fold, near), (folded, far)):
            ici(role, chip).wait_send()

    return swap, send, forward, finish


def _chip_reduce_scratch(slot):
    return [pltpu.VMEM((N_CHIPS,) + slot, BF16), pltpu.VMEM((3,) + slot, BF16), pltpu.VMEM((3,) + slot, BF16),
            pltpu.SemaphoreType.DMA((N_CHIPS,)), pltpu.SemaphoreType.DMA((N_CHIPS,)),
            pltpu.SemaphoreType.DMA((3,)), pltpu.SemaphoreType.DMA((3,))]


def _bucket_sums(a_ref, bucket_ref, cols):
    a = a_ref[:, cols]
    hi = a.astype(BF16)
    lo = (a - hi.astype(F32)).astype(BF16)
    rows = lax.broadcasted_iota(jnp.int32, (LANES, a.shape[1]), 0)
    onehot_t = (rows == bucket_ref[:, cols]).astype(F32).astype(BF16)
    return _dot_nt(hi, onehot_t) + _dot_nt(lo, onehot_t)


def _reduce_exchange(part, landed, smalls, by_bucket, bucket_row, chunk):
    nl, ng = len(landed), len(smalls)
    n_in = 1 + nl + ng + 2
    n_out = 1 + nl + ng + 1
    heads, positions = by_bucket.shape
    chunks = [slice(c0, c0 + chunk) for c0 in range(0, positions, chunk)]

    def body(*refs):
        p_in, l_in, s_in, (a_ref, bucket_ref) = refs[0], refs[1:1 + nl], refs[1 + nl:n_in - 2], refs[n_in - 2:n_in]
        refs = refs[n_in:]
        p_out, l_out, s_out, b_out = refs[0], refs[1:1 + nl], refs[1 + nl:n_out - 1], refs[n_out - 1]
        scratch = refs[n_out:]
        s_recv, (b_recv, b_ref, sib_ref, chip_ref, send_ref), sems = scratch[:ng], scratch[ng:ng + 5], scratch[ng + 5:]
        swap, send, forward, finish = _chip_reduce(p_in, p_out, sib_ref, chip_ref, send_ref, sems[:4])
        swap()
        _exchange_start(s_in, s_recv, *sems[4:7], False)
        b_ref[...] = jnp.zeros_like(b_ref)
        for cols in chunks[:len(chunks) // 2]:
            b_ref[...] += _bucket_sums(a_ref, bucket_ref, cols)
        send()
        for t in range(nl):
            _sum_slots(l_in[t], l_out[t])
        for cols in chunks[len(chunks) // 2:]:
            b_ref[...] += _bucket_sums(a_ref, bucket_ref, cols)
        _exchange_start([b_ref], [b_recv], *sems[7:], False)
        forward()
        finish()
        _exchange_wait(s_in, s_recv, *sems[4:7], False)
        _exchange_wait([b_ref], [b_recv], *sems[7:], False)
        for recv, out in zip([*s_recv, b_recv], [*s_out, b_out]):
            acc = recv[0]
            for dev in range(1, N_DEV):
                acc = acc + recv[dev]
            out[...] = acc

    vmem = pl.BlockSpec(memory_space=pltpu.VMEM)
    slot = part.shape[1:]
    outs = pl.pallas_call(
        body,
        name="reduce_grads",
        out_shape=[jax.ShapeDtypeStruct(p.shape[1:], F32) for p in [part] + landed]
        + [jax.ShapeDtypeStruct(s.shape, F32) for s in smalls] + [jax.ShapeDtypeStruct((heads, LANES), F32)],
        in_specs=[vmem] * n_in,
        out_specs=[vmem] * n_out,
        scratch_shapes=[pltpu.VMEM((N_DEV,) + s.shape, F32) for s in smalls]
        + [pltpu.VMEM((N_DEV, heads, LANES), F32), pltpu.VMEM((heads, LANES), F32)] + _chip_reduce_scratch(slot)
        + _exchange_sems(ng) + _exchange_sems(1),
        compiler_params=_params(),
    )(part, *landed, *smalls, by_bucket, bucket_row)
    return outs[0], outs[1:1 + nl], outs[1 + nl:n_out - 1], outs[n_out - 1]


def _layer_a_fwd(x2, sm, win_g, wout, later, ts):
    seq, d = x2.shape
    width = wout.shape[0]
    half = win_g.shape[2]
    n_half = width // half
    nl = len(later)
    nt = seq // ts

    def body(x_ref, sm_ref, win_ref, wout_ref, *refs):
        shard_refs, refs = refs[:nl], refs[nl:]
        h1_ref, n1_ref, proj_ref, conv_ref, y_ref, ya_ref = refs[:6]
        gathered_refs, (vprev_ref, *sems) = refs[6:6 + nl], refs[6 + nl:]

        @pl.when(pl.program_id(0) == 0)
        def _():
            vprev_ref[...] = jnp.zeros_like(vprev_ref)
            _exchange_start(shard_refs, gathered_refs, *sems, False)

        @pl.when(pl.program_id(0) == nt - 1)
        def _():
            _exchange_wait(shard_refs, gathered_refs, *sems, False)

        xf = x_ref[...]
        xn, _ = _rms(xf)
        n1 = (xn * sm_ref[0:1, :]).astype(BF16)
        n1_ref[...] = n1
        row = lax.broadcasted_iota(jnp.int32, (ts, half), 0)
        ya = jnp.zeros((ts, d), F32)
        for hh in range(n_half):
            cols = slice(hh * half, (hh + 1) * half)
            parts = []
            for part in range(4):
                j = part * n_half + hh
                pj = _dot(n1, win_ref[j])
                proj_ref[:, j * half:(j + 1) * half] = pj.astype(BF16)
                parts.append(pj)
            b, c, u, z = parts
            v = c * u
            last1, last2 = vprev_ref[7:8, cols], vprev_ref[6:7, cols]
            v1 = jnp.where(row == 0, last1, pltpu.roll(v, 1, 0))
            v2 = jnp.where(row == 0, last2, jnp.where(row == 1, last1, pltpu.roll(v, 2, 0)))
            vprev_ref[:, cols] = v[ts - 8:ts, :]
            conv = sm_ref[1:2, cols] * v2 + sm_ref[2:3, cols] * v1 + sm_ref[3:4, cols] * v
            conv_ref[:, cols] = conv.astype(BF16)
            yh = (b * conv * _silu(z)[0]).astype(BF16)
            y_ref[:, cols] = yh
            ya = ya + _dot(yh, wout_ref[cols, :])
        ya_ref[...] = ya
        h1_ref[...] = xf + _rms(ya)[0] * sm_ref[4:5, :]

    outs = pl.pallas_call(
        body,
        name="layer_a_fwd",
        grid=(nt,),
        in_specs=[_rows(ts, d), _full(sm.shape), _full(win_g.shape), _full(wout.shape)] + [HBM_SPEC] * nl,
        out_specs=[_rows(ts, d), _rows(ts, d), _rows(ts, 4 * width), _rows(ts, width), _rows(ts, width), _rows(ts, d)]
        + [HBM_SPEC] * nl,
        out_shape=[
            jax.ShapeDtypeStruct((seq, d), F32),
            jax.ShapeDtypeStruct((seq, d), BF16),
            jax.ShapeDtypeStruct((seq, 4 * width), BF16),
            jax.ShapeDtypeStruct((seq, width), BF16),
            jax.ShapeDtypeStruct((seq, width), BF16),
            jax.ShapeDtypeStruct((seq, d), F32),
        ] + [jax.ShapeDtypeStruct((N_DEV,) + s.shape, s.dtype) for s in later],
        scratch_shapes=[pltpu.VMEM((8, width), F32)] + _exchange_sems(nl),
        compiler_params=_params(("arbitrary",)),
    )(x2, sm, win_g, wout, *later)
    return outs[:6], outs[6:]


Q_BLOCKS = 4
ATTN_BWD_LAGS = (2, 4)
ATTN_FWD_LAGS = (2, 4)


def _banded_tiles(kvp_ref, kvc_ref):
    tile = kvc_ref[...].astype(F32)
    blocks = [kvp_ref[...].astype(F32)] + [tile[u * BLOCK:(u + 1) * BLOCK] for u in range(Q_BLOCKS)]
    return [_banded_kv(blocks[u], blocks[u + 1]) for u in range(Q_BLOCKS)]


def _bias_of(bias_ref, i, u, m):
    return bias_ref[jnp.minimum(i, 1) if u == 0 else 1, m]


def _banded_kv(kvp, kvc):
    kw = N_KV_HEADS * HEAD_DIM
    out = []
    for full in (jnp.concatenate([kvp[:, :kw], kvc[:, :kw]], axis=0), jnp.concatenate([kvp[:, kw:], kvc[:, kw:]], axis=0)):
        lo = lax.broadcasted_iota(jnp.int32, full.shape, 1) < HEAD_DIM
        rolled = pltpu.roll(full, HEAD_DIM, 1)
        x2 = [jnp.where(lo, full, rolled).astype(BF16), jnp.where(lo, rolled, full).astype(BF16)]
        ft = full.T
        x2t = [jnp.concatenate([ft[kh * HEAD_DIM:(kh + 1) * HEAD_DIM]] * 2, axis=0).astype(BF16) for kh in range(N_KV_HEADS)]
        out += [x2, x2t]
    return out


def _pair_rows(ref, rows, m, scale=None):
    both = ref[rows, m * LANES:(m + 1) * LANES].astype(F32)
    if scale is not None:
        both = both * scale
    lo = lax.broadcasted_iota(jnp.int32, both.shape, 1) < HEAD_DIM
    zero = jnp.zeros_like(both)
    return jnp.concatenate([jnp.where(lo, both, zero), jnp.where(lo, zero, both)], axis=0).astype(BF16)


def _pair_cols(res_t):
    top = lax.broadcasted_iota(jnp.int32, (LANES, BLOCK), 0) < HEAD_DIM
    return jnp.where(top, res_t[:, :BLOCK], res_t[:, BLOCK:]).T


def _sink_row(sink_ref, m):
    first = lax.broadcasted_iota(jnp.int32, (1, 2 * BLOCK), 1) < BLOCK
    return jnp.where(first, sink_ref[0, 2 * m], sink_ref[0, 2 * m + 1])


def _softmax_t(logits, sink):
    mx =jnp.maximum(jnp.max(logits, axis=0, keepdims=True), sink)
    p = jnp.exp(logits - mx)
    sink_p = jnp.exp(sink - mx)
    inv = 1.0 / (jnp.sum(p, axis=0, keepdims=True) + sink_p)
    return p * inv, sink_p * inv


def _layer_b_fwd(h1, target, kvn, bpre, wkv, wbin_g, biasm, sinks, wbout, bpost):
    seq, d = h1.shape
    kvw = wkv.shape[1]
    cw = wbin_g.shape[2]
    aw = N_Q_HEADS * HEAD_DIM
    per = aw // cw
    tile = Q_BLOCKS * BLOCK

    def body(sink_ref, h1_ref, tgt_ref, kvn_ref, bpre_ref, wkv_ref, wbin_ref, bias_ref, w_ref, g_ref,
             n3_ref, n4_ref, kvc_ref, q_ref, o_ref, dh2_ref, dyb_ref, dattn_ref, dz2_ref, acc_ref,
             attn_ref, z2_ref, kvp_ref):
        i = pl.program_id(0)

        @pl.when(i == 0)
        def _():
            acc_ref[...] = jnp.zeros_like(acc_ref)
            kvp_ref[...] = jnp.zeros_like(kvp_ref)

        hn, _ = _rms(h1_ref[...])
        n3 = (hn * kvn_ref[...]).astype(BF16)
        n4 = (hn * bpre_ref[...]).astype(BF16)
        n3_ref[...] = n3
        n4_ref[...] = n4
        kvc_ref[...] = _dot(n3, wkv_ref[...]).astype(BF16)
        for j in range(N_DEV):
            pj = _dot(n4, wbin_ref[j])
            if j < per:
                q_ref[:, j * cw:(j + 1) * cw] = pj.astype(BF16)
            else:
                z2_ref[:, (j - per) * cw:(j - per + 1) * cw] = pj

        banded = _banded_tiles(kvp_ref, kvc_ref)
        kvp_ref[...] = kvc_ref[tile - BLOCK:tile, :]
        units = [(u, m) for u in range(Q_BLOCKS) for m in range(N_PAIRS)]
        kv_of = lambda m: (2 * m) // GROUP
        logits, probs = {}, {}
        lag_b, lag_c = ATTN_FWD_LAGS
        for step in range(len(units) + lag_c):
            if step < len(units):
                u, m = units[step]
                qpair = _pair_rows(q_ref, slice(u * BLOCK, (u + 1) * BLOCK), m, SCALE)
                logits[step] = _dot_nt(banded[u][0][kv_of(m)], qpair) + _bias_of(bias_ref, i, u, m)
            if 0 <= step - lag_b < len(units):
                u, m = units[step - lag_b]
                probs[step - lag_b] = _softmax_t(logits.pop(step - lag_b), _sink_row(sink_ref, m))[0].astype(BF16)
            if 0 <= step - lag_c < len(units):
                u, m = units[step - lag_c]
                out_t = _dot(banded[u][3][kv_of(m)], probs.pop(step - lag_c))
                attn_ref[u * BLOCK:(u + 1) * BLOCK, m * LANES:(m + 1) * LANES] = _pair_cols(out_t)
        attn = attn_ref[...]
        sz, dsz = _silu(z2_ref[...])
        o = (attn * sz).astype(BF16)
        o_ref[...] = o

        w = w_ref[...]
        yb = _dot(o, w)
        ybn, r = _rms(yb)
        g = g_ref[...]
        diff = h1_ref[...] + ybn * g - tgt_ref[...]
        dh2 = diff * (1.0 / d)
        dh2_ref[...] = dh2
        acc_ref[0:1, :] += jnp.sum(dh2 * ybn, axis=0, keepdims=True)
        tok = jnp.mean(diff * diff, axis=-1, keepdims=True)
        acc_ref[1:2, :] += 0.5 * jnp.sum(tok, axis=0, keepdims=True)
        dyb = _rms_bwd(dh2 * g, ybn, r).astype(BF16)
        dyb_ref[...] = dyb
        do = _dot_nt(dyb, w)
        dattn_ref[...] = (do * sz).astype(BF16)
        dz2_ref[...] = (do * attn * dsz).astype(BF16)

    blk = lambda w: pl.BlockSpec((tile, w), lambda i: (i, 0))
    return pl.pallas_call(
        body,
        name="layer_b_fwd",
        grid=(seq // tile,),
        in_specs=[
            pl.BlockSpec(memory_space=pltpu.SMEM),
            blk(d),
            blk(d),
            _full(kvn.shape),
            _full(bpre.shape),
            _full(wkv.shape),
            _full(wbin_g.shape),
            _full(biasm.shape),
            _full(wbout.shape),
            _full(bpost.shape),
        ],
        out_specs=[blk(d), blk(d), blk(kvw), blk(aw), blk(aw), blk(d), blk(d), blk(aw), blk(aw), _resident((8, d))],
        out_shape=[
            jax.ShapeDtypeStruct((seq, d), BF16),
            jax.ShapeDtypeStruct((seq, d), BF16),
            jax.ShapeDtypeStruct((seq, kvw), BF16),
            jax.ShapeDtypeStruct((seq, aw), BF16),
            jax.ShapeDtypeStruct((seq, aw), BF16),
            jax.ShapeDtypeStruct((seq, d), F32),
            jax.ShapeDtypeStruct((seq, d), BF16),
            jax.ShapeDtypeStruct((seq, aw), BF16),
            jax.ShapeDtypeStruct((seq, aw), BF16),
            jax.ShapeDtypeStruct((8, d), F32),
        ],
        scratch_shapes=[pltpu.VMEM((tile, aw), F32), pltpu.VMEM((tile, aw), F32), pltpu.VMEM((BLOCK, kvw), BF16)],
        compiler_params=_params(("arbitrary",)),
    )(sinks, h1, target, kvn, bpre, wkv, wbin_g, biasm, wbout, bpost)


def _attn_bwd(q, kv, dattn, biasm, sinks, ready):
    seq, aw = q.shape
    kvw = kv.shape[1]
    kw = N_KV_HEADS * HEAD_DIM
    nb = seq // BLOCK
    pairs_per_kv = N_PAIRS // N_KV_HEADS
    nr = len(ready)

    tile = Q_BLOCKS * BLOCK
    nsteps = seq // tile
    held = (Q_BLOCKS - 1) * BLOCK

    def body(sink_ref, q_ref, kvc_ref, kvp_ref, da_ref, bias_ref, *refs):
        ready_refs, (dq_ref, dkv_ref, dssum_ref, dsink_ref) = refs[:nr], refs[nr:nr + 4]
        landed_refs, scratch = refs[nr + 4:2 * nr + 4], refs[2 * nr + 4:]
        carry_ref, done_ref, qs_ref, dos_ref, dst_ref, pt_ref, *sems = scratch
        i = pl.program_id(0)

        @pl.when(i == 0)
        def _():
            dssum_ref[...] = jnp.zeros_like(dssum_ref)
            dsink_ref[...] = jnp.zeros_like(dsink_ref)
            carry_ref[...] = jnp.zeros_like(carry_ref)
            done_ref[...] = jnp.zeros_like(done_ref)
            if nr:
                _exchange_start(ready_refs, landed_refs, *sems, True)

        if nr:
            @pl.when(i == nsteps)
            def _():
                _exchange_wait(ready_refs, landed_refs, *sems, True)

        @pl.when(i < nsteps)
        def _():
            lo = lax.broadcasted_iota(jnp.int32, (BAND, LANES), 1) < HEAD_DIM
            head_lane = lax.broadcasted_iota(jnp.int32, (1, LANES), 1)
            banded = _banded_tiles(kvp_ref, kvc_ref)
            units = [(u, m) for u in range(Q_BLOCKS) for m in range(N_PAIRS)]
            dsink = jnp.zeros((1, LANES), F32)
            folded = {}
            logits, dps, dsbs = {}, {}, {}
            lag_b, lag_c = ATTN_BWD_LAGS
            for step in range(len(units) + lag_c):
                if step < len(units):
                    u, m = units[step]
                    kh, rows = m // pairs_per_kv, slice((m % pairs_per_kv) * BAND, (m % pairs_per_kv + 1) * BAND)
                    qrows = slice(u * BLOCK, (u + 1) * BLOCK)
                    qpair = _pair_rows(q_ref, qrows, m, SCALE)
                    dopair = _pair_rows(da_ref, qrows, m)
                    qs_ref[u, kh, rows, :] = qpair
                    dos_ref[u, kh, rows, :] = dopair
                    logits[step] = _dot_nt(banded[u][0][kh], qpair) + _bias_of(bias_ref, i, u, m)
                    dps[step] = _dot_nt(banded[u][2][kh], dopair)
                if 0 <= step - lag_b < len(units):
                    u, m = units[step - lag_b]
                    kh, rows = m // pairs_per_kv, slice((m % pairs_per_kv) * BAND, (m % pairs_per_kv + 1) * BAND)
                    pn, sink_p = _softmax_t(logits.pop(step - lag_b), _sink_row(sink_ref, m))
                    dp = dps.pop(step - lag_b)
                    delta = jnp.sum(pn * dp, axis=0, keepdims=True)
                    ds = pn * (dp - delta)
                    dssum_ref[m] += ds
                    sink_term = sink_p * delta
                    for e in range(2):
                        total = jnp.sum(sink_term[:, e * BLOCK:(e + 1) * BLOCK], axis=1, keepdims=True)
                        dsink = dsink - jnp.where(head_lane == 2 * m + e, total, 0.0)
                    dsbs[step - lag_b] = ds.astype(BF16)
                    dst_ref[u, kh, :, rows] = dsbs[step - lag_b]
                    pt_ref[u, kh, :, rows] = pn.astype(BF16)
                if 0 <= step - lag_c < len(units):
                    u, m = units[step - lag_c]
                    kh = m // pairs_per_kv
                    dq_t = _dot(banded[u][1][kh], dsbs.pop(step - lag_c))
                    dq_ref[u * BLOCK:(u + 1) * BLOCK, m * LANES:(m + 1) * LANES] = (_pair_cols(dq_t) * SCALE).astype(BF16)
                    if m % pairs_per_kv == pairs_per_kv - 1:
                        for name, lhs_ref, rhs_ref in (("k", dst_ref, qs_ref), ("v", pt_ref, dos_ref)):
                            acc = _dot(lhs_ref[u, kh], rhs_ref[u, kh])
                            folded[u, kh, name] = acc + pltpu.roll(acc, HEAD_DIM, 1)
            dsink_ref[0:1, :] += dsink
            dkv = [jnp.concatenate([jnp.where(lo, folded[u, 0, n], folded[u, 1, n]) for n in ("k", "v")], axis=1)
                   for u in range(Q_BLOCKS)]

            @pl.when(i > 0)
            def _():
                if held:
                    dkv_ref[:held, :] = done_ref[...].astype(BF16)
                dkv_ref[held:, :] = (carry_ref[...] + dkv[0][:BLOCK]).astype(BF16)

            for u in range(Q_BLOCKS - 1):
                done_ref[u * BLOCK:(u + 1) * BLOCK, :] = dkv[u][BLOCK:] + dkv[u + 1][:BLOCK]
            carry_ref[...] = dkv[Q_BLOCKS - 1][BLOCK:]

        @pl.when(i == nsteps)
        def _():
            if held:
                dkv_ref[:held, :] = done_ref[...].astype(BF16)
            dkv_ref[held:, :] = carry_ref[...].astype(BF16)

    last = nsteps - 1
    blk = lambda w: pl.BlockSpec((tile, w), lambda i: (jnp.minimum(i, last), 0))
    outs = pl.pallas_call(
        body,
        name="attn_bwd",
        grid=(nsteps + 1,),
        in_specs=[
            pl.BlockSpec(memory_space=pltpu.SMEM),
            blk(aw),
            blk(kvw),
            pl.BlockSpec((BLOCK, kvw), lambda i: (jnp.clip(Q_BLOCKS * i - 1, 0, nb - 1), 0)),
            blk(aw),
            _full(biasm.shape),
        ] + [HBM_SPEC] * nr,
        out_specs=[
            blk(aw),
            pl.BlockSpec((tile, kvw), lambda i: (jnp.maximum(i - 1, 0), 0)),
            _resident(biasm.shape[1:]),
            _resident((8, LANES)),
        ] + [HBM_SPEC] * nr,
        out_shape=[
            jax.ShapeDtypeStruct((seq, aw), BF16),
            jax.ShapeDtypeStruct((seq, kvw), BF16),
            jax.ShapeDtypeStruct(biasm.shape[1:], F32),
            jax.ShapeDtypeStruct((8, LANES), F32),
        ] + [jax.ShapeDtypeStruct(g.shape, g.dtype) for g in ready],
        scratch_shapes=[
            pltpu.VMEM((BLOCK, kvw), F32),
            pltpu.VMEM((max(held, 8), kvw), F32),
            pltpu.VMEM((Q_BLOCKS, N_KV_HEADS, pairs_per_kv * BAND, LANES), BF16),
            pltpu.VMEM((Q_BLOCKS, N_KV_HEADS, pairs_per_kv * BAND, LANES), BF16),
            pltpu.VMEM((Q_BLOCKS, N_KV_HEADS, BAND, pairs_per_kv * BAND), BF16),
            pltpu.VMEM((Q_BLOCKS, N_KV_HEADS, BAND, pairs_per_kv * BAND), BF16),
        ] + _exchange_sems(nr),
        compiler_params=_params(("arbitrary",)),
    )(sinks, q, kv, kv, dattn, biasm, *ready)
    return outs[:4], outs[4:]


def _layer_b_in_bwd(dh2, dq, dz2, dkv, h1, ya, wbin_g, wkv, kvn, bpre, sm, ready, ts):
    seq, d = h1.shape
    aw = dq.shape[1]
    kvw = dkv.shape[1]
    cw = wbin_g.shape[2]
    per = aw // cw

    nr = len(ready)
    nt = seq // ts

    def body(dh2_ref, dq_ref, dz2_ref, dkv_ref, h1_ref, ya_ref, wbin_ref, wkv_ref, kvn_ref, bpre_ref, sm_ref, *refs):
        ready_refs, (dh1_ref, dya_ref, acc_ref) = refs[:nr], refs[nr:nr + 3]
        landed_refs, sems = refs[nr + 3:2 * nr + 3], refs[2 * nr + 3:]

        @pl.when(pl.program_id(0) == 0)
        def _():
            acc_ref[...] = jnp.zeros_like(acc_ref)
            if nr:
                _exchange_start(ready_refs, landed_refs, *sems, True)

        if nr:
            @pl.when(pl.program_id(0) == nt - 1)
            def _():
                _exchange_wait(ready_refs, landed_refs, *sems, True)

        dn4 = jnp.zeros((ts, d), F32)
        for j in range(N_DEV):
            src = dq_ref if j < per else dz2_ref
            jj = j % per
            dn4 = dn4 + _dot_nt(src[:, jj * cw:(jj + 1) * cw], wbin_ref[j])
        dn3 = _dot_nt(dkv_ref[...], wkv_ref[...])
        hn, r = _rms(h1_ref[...])
        acc_ref[0:1, :] += jnp.sum(dn4 * hn, axis=0, keepdims=True)
        acc_ref[1:2, :] += jnp.sum(dn3 * hn, axis=0, keepdims=True)
        dh1 = dh2_ref[...] + _rms_bwd(dn4 * bpre_ref[...] + dn3 * kvn_ref[...], hn, r)
        dh1_ref[...] = dh1
        yan, r2 = _rms(ya_ref[...])
        acc_ref[2:3, :] += jnp.sum(dh1 * yan, axis=0, keepdims=True)
        dya_ref[...] = _rms_bwd(dh1 * sm_ref[4:5, :], yan, r2).astype(BF16)

    outs = pl.pallas_call(
        body,
        name="layer_b_in_bwd",
        grid=(nt,),
        in_specs=[_rows(ts, d), _rows(ts, aw), _rows(ts, aw), _rows(ts, kvw), _rows(ts, d), _rows(ts, d),
                  _full(wbin_g.shape), _full(wkv.shape), _full(kvn.shape), _full(bpre.shape), _full(sm.shape)]
        + [HBM_SPEC] * nr,
        out_specs=[_rows(ts, d), _rows(ts, d), _resident((8, d))] + [HBM_SPEC] * nr,
        out_shape=[jax.ShapeDtypeStruct((seq, d), F32), jax.ShapeDtypeStruct((seq, d), BF16),
                   jax.ShapeDtypeStruct((8, d), F32)] + [jax.ShapeDtypeStruct(g.shape, g.dtype) for g in ready],
        scratch_shapes=_exchange_sems(nr),
        compiler_params=_params(("arbitrary",)),
    )(dh2, dq, dz2, dkv, h1, ya, wbin_g, wkv, kvn, bpre, sm, *ready)
    return outs[:3], outs[3:]


def _layer_a_bwd(dya, proj, conv, dh1, x2, wout, win_g, sm, ts):
    seq, d = x2.shape
    width = wout.shape[0]
    half = win_g.shape[2]
    n_half = width // half
    nt = seq // ts

    def body(dya_ref, proj_ref, conv_ref, dh1_ref, x_ref, wout_ref, win_ref, sm_ref, dproj_ref, gx_ref, acc_ref,
             dnext_ref):
        @pl.when(pl.program_id(0) == 0)
        def _():
            acc_ref[...] = jnp.zeros_like(acc_ref)
            dnext_ref[...] = jnp.zeros_like(dnext_ref)

        dy = _dot_nt(dya_ref[...], wout_ref[...])
        row = lax.broadcasted_iota(jnp.int32, (ts, half), 0)
        dn1 = jnp.zeros((ts, d), F32)
        for hh in range(n_half):
            cols = slice(hh * half, (hh + 1) * half)
            b, c, u, z = [proj_ref[:, (part * n_half + hh) * half:(part * n_half + hh + 1) * half].astype(F32)
                          for part in range(4)]
            cv = conv_ref[:, cols].astype(F32)
            dyh = dy[:, cols]
            sz, dsz = _silu(z)
            dconv = dyh * b * sz
            grads = [dyh * cv * sz, None, None, dyh * b * cv * dsz]
            next0, next1 = dnext_ref[0:1, cols], dnext_ref[1:2, cols]
            dc1 = jnp.where(row == ts - 1, next0, pltpu.roll(dconv, ts - 1, 0))
            dc2 = jnp.where(row == ts - 1, next1, jnp.where(row == ts - 2, next0, pltpu.roll(dconv, ts - 2, 0)))
            dnext_ref[:, cols] = dconv[0:8, :]
            v = c * u
            acc_ref[1:2, cols] += jnp.sum(dc2 * v, axis=0, keepdims=True)
            acc_ref[2:3, cols] += jnp.sum(dc1 * v, axis=0, keepdims=True)
            acc_ref[3:4, cols] += jnp.sum(dconv * v, axis=0, keepdims=True)
            dv = sm_ref[3:4, cols] * dconv + sm_ref[2:3, cols] * dc1 + sm_ref[1:2, cols] * dc2
            grads[1] = dv * u
            grads[2] = dv * c
            for part in range(4):
                j = part * n_half + hh
                gj = grads[part].astype(BF16)
                dproj_ref[:, j * half:(j + 1) * half] = gj
                dn1 = dn1 + _dot_nt(gj, win_ref[j])
        xn, r = _rms(x_ref[...])
        acc_ref[0:1, :] += jnp.sum(dn1 * xn, axis=0, keepdims=True)
        gx_ref[...] = dh1_ref[...] + _rms_bwd(dn1 * sm_ref[0:1, :], xn, r)

    rev = lambda w: pl.BlockSpec((ts, w), lambda i: (nt - 1 - i, 0))
    return pl.pallas_call(
        body,
        name="layer_a_bwd",
        grid=(nt,),
        in_specs=[rev(d), rev(4 * width), rev(width), rev(d), rev(d), _full(wout.shape), _full(win_g.shape), _full(sm.shape)],
        out_specs=[rev(4 * width), rev(d), _resident((8, d))],
        out_shape=[jax.ShapeDtypeStruct((seq, 4 * width), BF16), jax.ShapeDtypeStruct((seq, d), F32),
                   jax.ShapeDtypeStruct((8, d), F32)],
        scratch_shapes=[pltpu.VMEM((8, width), F32)],
        compiler_params=_params(("arbitrary",)),
    )(dya, proj, conv, dh1, x2, wout, win_g, sm)


def _wgrad(a, bs, n_slots, ts, name, ready=(), block_cols=1024):
    nr = len(ready)
    seq, k = a.shape
    nb_in = len(bs)
    n_each = bs[0].shape[1]
    n = nb_in * n_each
    bn = min(n_each, block_cols)
    per_in = n_each // bn
    n_blocks = nb_in * per_in
    ns = seq // ts

    def b_spec(idx):
        def index(j, s):
            mine = j // per_in == idx
            row = jnp.where(mine, s, jnp.where(j // per_in > idx, ns - 1, 0))
            return (row, jnp.where(mine, j % per_in, jnp.where(j // per_in > idx, per_in - 1, 0)))
        return pl.BlockSpec((ts, bn), index)

    if n_slots:
        sw = n // n_slots
        spb = bn // sw
        out_shape = jax.ShapeDtypeStruct((n_slots, k, sw), BF16)
        out_spec = pl.BlockSpec((spb, k, sw), lambda j, s: (j, 0, 0))
    else:
        out_shape = jax.ShapeDtypeStruct((k, n), BF16)
        out_spec = pl.BlockSpec((k, bn), lambda j, s: (0, j))

    def body(a_ref, *refs):
        b_refs, ready_refs, o_ref = refs[:nb_in], refs[nb_in:nb_in + nr], refs[nb_in + nr]
        landed_refs, (acc_ref, *sems) = refs[nb_in + nr + 1:nb_in + 2 * nr + 1], refs[nb_in + 2 * nr + 1:]
        j, s = pl.program_id(0), pl.program_id(1)

        if nr:
            @pl.when(jnp.logical_and(j == 0, s == 0))
            def _():
                _exchange_start(ready_refs, landed_refs, *sems, True)

            @pl.when(jnp.logical_and(j == n_blocks - 1, s == ns - 1))
            def _():
                _exchange_wait(ready_refs, landed_refs, *sems, True)

        @pl.when(s == 0)
        def _():
            acc_ref[...] = jnp.zeros_like(acc_ref)

        for idx in range(nb_in):
            @pl.when(j // per_in == idx)
            def _(idx=idx):
                acc_ref[...] += _dot_tn(a_ref[...], b_refs[idx][...])

        @pl.when(s == ns - 1)
        def _():
            if n_slots:
                for e in range(spb):
                    o_ref[e] = acc_ref[:, e * sw:(e + 1) * sw].astype(BF16)
            else:
                o_ref[...] = acc_ref[...].astype(BF16)

    outs = pl.pallas_call(
        body,
        name=name,
        grid=(n_blocks, ns),
        in_specs=[pl.BlockSpec((ts, k), lambda j, s: (s, 0))] + [b_spec(idx) for idx in range(nb_in)] + [HBM_SPEC] * nr,
        out_specs=[out_spec] + [HBM_SPEC] * nr,
        out_shape=[out_shape] + [jax.ShapeDtypeStruct(g.shape, g.dtype) for g in ready],
        scratch_shapes=[pltpu.VMEM((k, bn), F32)] + (_exchange_sems(nr) if nr else []),
        compiler_params=_params(("arbitrary", "arbitrary")),
    )(a, *bs, *ready)
    return (outs[0], outs[1:]) if nr else outs[0]


def _wgrad_tail(pairs, part, landed, ts):
    n_tasks = len(pairs)
    assert n_tasks == 2
    nl = len(landed)
    seq, k = pairs[0][0].shape
    n = pairs[0][1].shape[1]
    ns = seq // ts
    total = n_tasks * ns
    per = k // N_DEV
    n_red = len(_chip_reduce_scratch((per, n)))

    def spec(t, width):
        return pl.BlockSpec((ts, width), lambda j, s: (jnp.where(j == t, s, jnp.where(j > t, ns - 1, 0)), 0))

    def body(*refs):
        ab_refs, part_hbm = refs[:2 * n_tasks], refs[2 * n_tasks]
        landed_hbm, refs = refs[2 * n_tasks + 1:2 * n_tasks + 1 + nl], refs[2 * n_tasks + 1 + nl:]
        o_ref, red_ref, early_ref = refs[:3]
        summed_refs, (acc_ref, first_ref, part_ref, *scratch) = refs[3:3 + nl], refs[3 + nl:]
        landed_refs, load_sems, scratch = scratch[:nl], scratch[nl], scratch[nl + 1:]
        j, s = pl.program_id(0), pl.program_id(1)
        flat = j * ns + s
        swap, send, forward, finish = _chip_reduce(part_ref, red_ref, *scratch[:3], scratch[3:n_red], part_hbm)
        swap_first, send_first, forward_first, finish_first = _chip_reduce(
            first_ref, early_ref, *scratch[n_red:n_red + 3], scratch[n_red + 3:])
        loads = [pltpu.make_async_copy(src, dst, load_sems.at[i])
                 for i, (src, dst) in enumerate(zip([part_hbm, *landed_hbm], [part_ref, *landed_refs]))]

        @pl.when(flat == 0)
        def _():
            swap()
            for load in loads:
                load.start()

        @pl.when(flat == min(2, total - 1))
        def _():
            loads[0].wait()
            send()

        @pl.when(flat == min(total // 2 + 1, total - 1))
        def _():
            forward()
            for t in range(nl):
                loads[1 + t].wait()
                _sum_slots(landed_refs[t], summed_refs[t])

        @pl.when(flat == min(ns + 1, total - 1))
        def _():
            send_first()

        @pl.when(flat == min(ns + ns // 2 + 1, total - 1))
        def _():
            forward_first()

        @pl.when(s == 0)
        def _():
            acc_ref[...] = jnp.zeros_like(acc_ref)

        for t in range(n_tasks):
            @pl.when(j == t)
            def _(t=t):
                acc_ref[...] += _dot_tn(ab_refs[2 * t][...], ab_refs[2 * t + 1][...])

        @pl.when(flat == ns - 1)
        def _():
            for dev in range(N_DEV):
                first_ref[dev] = acc_ref[dev * per:(dev + 1) * per, :].astype(BF16)
            swap_first()

        @pl.when(flat == total - 1)
        def _():
            for dev in range(N_DEV):
                o_ref[dev] = acc_ref[dev * per:(dev + 1) * per, :].astype(BF16)
            finish()
            finish_first()

    slot = part.shape[1:]
    outs = pl.pallas_call(
        body,
        name="wgrad_tail",
        grid=(n_tasks, ns),
        in_specs=[spec(t, w) for t in range(n_tasks) for w in (k, n)] + [HBM_SPEC] * (1 + nl),
        out_specs=[_resident((N_DEV, per, n)), _resident(slot), _resident((per, n))]
        + [_resident(g.shape[1:]) for g in landed],
        out_shape=[jax.ShapeDtypeStruct((N_DEV, per, n), BF16), jax.ShapeDtypeStruct(slot, F32),
                   jax.ShapeDtypeStruct((per, n), F32)]
        + [jax.ShapeDtypeStruct(g.shape[1:], F32) for g in landed],
        scratch_shapes=[pltpu.VMEM((k, n), F32), pltpu.VMEM((N_DEV, per, n), BF16), pltpu.VMEM(part.shape, part.dtype)]
        + [pltpu.VMEM(g.shape, g.dtype) for g in landed] + [pltpu.SemaphoreType.DMA((1 + nl,))]
        + _chip_reduce_scratch(slot) + _chip_reduce_scratch((per, n)),
        compiler_params=_params(("arbitrary", "arbitrary")),
    )(*[op for pair in pairs for op in pair], part, *landed)
    return outs[0], outs[1], outs[2], outs[3:]


MINE = "mine"
ADAMW_STEPS = 4


def _adamw(ws, sources, picks, loss_at, ms, vs):
    n, n_src = len(ws), len(sources)
    streamed = [len(w.shape) == 2 and w.shape[0] >= 128 and picks[t][1:] == (0, None)
                and sources[picks[t][0]].shape == w.shape for t, w in enumerate(ws)]
    streamed_sources = {picks[t][0] for t in range(n) if streamed[t]}

    def step(w, g, m, v):
        m = ADAM_B1 * m + (1.0 - ADAM_B1) * g
        v = ADAM_B2 * v + (1.0 - ADAM_B2) * jnp.square(g)
        m_hat = m / (1.0 - ADAM_B1 ** ADAM_STEP)
        v_hat = v / (1.0 - ADAM_B2 ** ADAM_STEP)
        return g, -ADAM_LR * (m_hat / (jnp.sqrt(v_hat) + ADAM_EPS) + ADAM_WD * w), m, v

    def body(*refs):
        refs = list(refs)
        take = lambda k: [refs.pop(0) for _ in range(k)]
        w_refs, s_refs, m_refs, v_refs = take(n), take(n_src), take(n), take(n)
        (loss_ref,), go_refs, d_refs, nm_refs, nv_refs = take(1), take(n), take(n), take(n), take(n)
        me = _my_index()

        def grad(t, rows):
            k, first, cols = picks[t]
            if cols is None:
                return s_refs[k][rows, :]
            if cols is not MINE:
                return s_refs[k][rows, cols]
            width = w_refs[t].shape[-1]
            g = s_refs[k][rows, 0:width]
            for dev in range(1, N_DEV):
                g = jnp.where(me == dev, s_refs[k][rows, dev * width:(dev + 1) * width], g)
            return g

        def whole(t):
            first = picks[t][1]
            rows = w_refs[t].shape[0]
            if len(w_refs[t].shape) == 3:
                for j in range(rows):
                    go_refs[t][j], d_refs[t][j], nm_refs[t][j], nv_refs[t][j] = step(
                        w_refs[t][j], grad(t, slice(first + j, first + j + 1)), m_refs[t][j], v_refs[t][j])
                return
            go_refs[t][...], d_refs[t][...], nm_refs[t][...], nv_refs[t][...] = step(
                w_refs[t][...], grad(t, slice(first, first + rows)), m_refs[t][...], v_refs[t][...])

        def block(t):
            rows = w_refs[t].shape[0]
            chunk = min(rows, 128)

            def one(i, carry):
                r = pl.ds(pl.multiple_of(i * chunk, chunk), chunk)
                go_refs[t][r, :], d_refs[t][r, :], nm_refs[t][r, :], nv_refs[t][r, :] = step(
                    w_refs[t][r, :], grad(t, r), m_refs[t][r, :], v_refs[t][r, :])
                return carry

            lax.fori_loop(0, rows // chunk, one, 0)

        @pl.when(pl.program_id(0) == 0)
        def _():
            loss_ref[...] = s_refs[loss_at[0]][loss_at[1]:loss_at[1] + 1, 0:1]
            for t in range(n):
                if not streamed[t]:
                    whole(t)

        for t in range(n):
            if streamed[t]:
                block(t)

    def rows_of(shape):
        return pl.BlockSpec((shape[0] // ADAMW_STEPS, shape[1]), lambda i: (i, 0))

    w_in = [rows_of(w.shape) if streamed[t] else _full(w.shape) for t, w in enumerate(ws)]
    w_out = [rows_of(w.shape) if streamed[t] else _resident(w.shape) for t, w in enumerate(ws)]
    s_in = [rows_of(s.shape) if k in streamed_sources else _full(s.shape) for k, s in enumerate(sources)]
    outs = pl.pallas_call(
        body,
        name="adamw",
        grid=(ADAMW_STEPS,),
        in_specs=w_in + s_in + w_in * 2,
        out_specs=[_resident((1, 1))] + w_out * 4,
        out_shape=[jax.ShapeDtypeStruct((1, 1), F32)] + [jax.ShapeDtypeStruct(w.shape, F32) for w in ws] * 4,
        compiler_params=_params(("arbitrary",)),
    )(*ws, *sources, *ms, *vs)
    return outs[0], outs[1:n + 1], outs[n + 1:2 * n + 1], outs[2 * n + 1:3 * n + 1], outs[3 * n + 1:]


def _band_structure():
    q_loc = np.arange(BLOCK, dtype=np.int32)[:, None]
    s_loc = np.arange(2 * BLOCK, dtype=np.int32)[None, :]
    dist = q_loc + BLOCK - s_loc
    in_window = (dist >= 0) & (dist < BLOCK)
    dd = np.maximum(dist, 0)
    max_exact = N_BUCKETS // 2
    large = max_exact + (np.log(np.maximum(dd, 1) / max_exact) / math.log(MAX_DISTANCE / max_exact)
                         * (N_BUCKETS - max_exact)).astype(np.int32)
    bucket = np.where(dd < max_exact, dd, np.minimum(large, N_BUCKETS - 1)).astype(np.int32)
    return bucket, in_window.astype(np.int32)


def kernel(x, a_pre_norm, a_w_in, a_conv_w, a_w_out, a_post_norm, kv_norm, w_kv, rel_bias, b_pre_norm, b_w_in, b_sinks, b_w_out, b_post_norm, loss_target, m_a_pre_norm, m_a_w_in, m_a_conv_w, m_a_w_out, m_a_post_norm, m_kv_norm, m_w_kv, m_rel_bias, m_b_pre_norm, m_b_w_in, m_b_sinks, m_b_w_out, m_b_post_norm, v_a_pre_norm, v_a_w_in, v_a_conv_w, v_a_w_out, v_a_post_norm, v_kv_norm, v_w_kv, v_rel_bias, v_b_pre_norm, v_b_w_in, v_b_sinks, v_b_w_out, v_b_post_norm):
    seq, d = x.shape[1], x.shape[2]
    x2 = x.reshape(seq, d)
    target = loss_target.reshape(seq, d)
    shard = a_pre_norm.shape[1]
    ts_a = min(seq, 512)
    ts = min(seq, 512)
    ts_w = min(seq, 2048)

    taps = lambda a: a.transpose(1, 0, 2)
    bucket, in_window = _band_structure()
    (win_g, wout_g), small_g, later, biasm = _all_gather(
        [a_w_in[0], a_w_out[0]], [(0, a_pre_norm), (1, taps(a_conv_w)), (4, a_post_norm)],
        [w_kv, b_w_in[0], b_w_out[0]], rel_bias.T, bucket.T, in_window.T)
    wout = wout_g.reshape(-1, wout_g.shape[2])
    sm = small_g.transpose(1, 0, 2).reshape(8, N_DEV * shard)
    kvn = kv_norm.reshape(1, d)

    (h1, n1, proj, conv, y, ya), (wkv_g, wbin_g, wbout_g) = _layer_a_fwd(x2, sm, win_g, wout, later, ts_a)
    wkv = wkv_g.reshape(-1, wkv_g.shape[2])
    wbout = wbout_g.reshape(-1, wbout_g.shape[2])
    n3, n4, kv, q, o, dh2, dyb, dattn, dz2, acc_c = _layer_b_fwd(
        h1, target, kvn, b_pre_norm, wkv, wbin_g, biasm, b_sinks, wbout, b_post_norm)

    (dq, dkv, dssum, dsink), _ = _attn_bwd(q, kv, dattn, biasm, b_sinks, [])
    by_head = dssum.reshape(N_PAIRS, BAND, 2, BLOCK).transpose(0, 2, 3, 1)
    g_wkv = _wgrad(n3, [dkv], 0, ts_w, "wgrad_kv").reshape(wkv_g.shape)
    g_wbin = _wgrad(n4, [dq, dz2], N_DEV, ts_w, "wgrad_b_in")
    (dh1, dya, acc_b), _ = _layer_b_in_bwd(dh2, dq, dz2, dkv, h1, ya, wbin_g, wkv, kvn, b_pre_norm, sm, [], ts)
    dproj, gx, acc_a = _layer_a_bwd(dya, proj, conv, dh1, x2, wout, win_g, sm, ts_a)
    g_win, (l_wkv, l_wbin) = _wgrad(
        n1, [dproj], N_DEV, ts_w, "wgrad_a_in", ready=[g_wkv, g_wbin], block_cols=2048)
    g_wbout, r_win, r_wout, (r_wkv, r_wbin) = _wgrad_tail(
        [(y, dya), (o, dyb)], g_win, [l_wkv, l_wbin], min(seq, 1024))

    r_wbout, _, (s_a, s_b, s_c, s_sink), s_relb = _reduce_exchange(
        g_wbout, [], [acc_a, acc_b, acc_c, dsink], by_head.reshape(N_Q_HEADS, -1), bucket.reshape(1, -1), 4096)
    weights = [a_pre_norm, a_w_in[0], taps(a_conv_w), a_w_out[0], a_post_norm, kvn, w_kv, rel_bias.T, b_pre_norm,
               b_w_in[0], b_sinks, b_w_out[0], b_post_norm]
    sources = [s_a, s_b, s_c, s_relb, s_sink, r_win, r_wout, r_wkv, r_wbin, r_wbout]
    picks = [(0, 0, MINE), (5, 0, None), (0, 1, MINE), (6, 0, None), (1, 2, MINE), (1, 1, None), (7, 0, None),
             (3, 0, slice(0, N_BUCKETS)), (1, 0, None), (8, 0, None), (4, 0, slice(0, N_Q_HEADS)),
             (9, 0, None), (2, 0, None)]
    first = [m_a_pre_norm, m_a_w_in[0], taps(m_a_conv_w), m_a_w_out[0], m_a_post_norm, m_kv_norm.reshape(1, d),
             m_w_kv, m_rel_bias.T, m_b_pre_norm, m_b_w_in[0], m_b_sinks, m_b_w_out[0], m_b_post_norm]
    second = [v_a_pre_norm, v_a_w_in[0], taps(v_a_conv_w), v_a_w_out[0], v_a_post_norm, v_kv_norm.reshape(1, d),
              v_w_kv, v_rel_bias.T, v_b_pre_norm, v_b_w_in[0], v_b_sinks, v_b_w_out[0], v_b_post_norm]
    loss, grads, deltas, new_m, new_v = _adamw(weights, sources, picks, (2, 1), first, second)

    shapes = [a_pre_norm.shape, a_w_in.shape, taps, a_w_out.shape, a_post_norm.shape, kv_norm.shape,
              w_kv.shape, jnp.transpose, b_pre_norm.shape, b_w_in.shape, b_sinks.shape, b_w_out.shape, b_post_norm.shape]
    shaped = lambda arrays: [s(a) if callable(s) else a.reshape(s) for a, s in zip(arrays, shapes)]
    return (loss.reshape(()), gx.reshape(x.shape), *shaped(grads), *shaped(deltas), *shaped(new_m), *shaped(new_v))
```

```python
import math

import jax
import jax.numpy as jnp
import numpy as np
from jax import lax
from jax.experimental import pallas as pl
from jax.experimental.pallas import tpu as pltpu

HEAD_DIM = 64
N_Q_HEADS = 16
N_KV_HEADS = 2
GROUP = N_Q_HEADS // N_KV_HEADS
BLOCK = 128
N_BUCKETS = 32
MAX_DISTANCE = 128
EPS = 1e-6
NEG_INF = -1e30
SCALE = HEAD_DIM ** -0.5

ADAM_LR = 0.001
ADAM_B1 = 0.9
ADAM_B2 = 0.999
ADAM_EPS = 1e-08
ADAM_WD = 0.01
ADAM_STEP = 10

N_PAIRS = N_Q_HEADS // 2
BAND = 2 * BLOCK

N_DEV = 8
GATHER_PIECE_ROWS = 256
LANES = 128
F32 = jnp.float32
BF16 = jnp.bfloat16
MESH = pl.DeviceIdType.MESH
MIB = 1024 * 1024
VMEM_RESERVED_MIB = 63


def _params(semantics=None):
    return pltpu.CompilerParams(dimension_semantics=semantics, vmem_limit_bytes=VMEM_RESERVED_MIB * MIB)


def _full(shape):
    zeros = (0,) * len(shape)
    return pl.BlockSpec(shape, lambda *_: zeros, pipeline_mode=pl.Buffered(1))


def _resident(shape):
    zeros = (0,) * len(shape)
    return pl.BlockSpec(shape, lambda *_: zeros)


def _rows(ts, cols):
    return pl.BlockSpec((ts, cols), lambda i: (i, 0))


RING = 3


def _row_ring(hbm_refs, buf_refs, sems, nt):
    i = pl.program_id(0)
    ts = buf_refs[0].shape[1]

    def fetch(k, step):
        slot = step % RING
        rows = pl.ds(pl.multiple_of(step * ts, ts), ts)
        return pltpu.make_async_copy(hbm_refs[k].at[rows], buf_refs[k].at[slot], sems.at[k, slot])

    @pl.when(i == 0)
    def _():
        for step in range(min(RING - 1, nt)):
            for k in range(len(hbm_refs)):
                fetch(k, step).start()

    @pl.when(i + RING - 1 < nt)
    def _():
        for k in range(len(hbm_refs)):
            fetch(k, i + RING - 1).start()

    for k in range(len(hbm_refs)):
        fetch(k, i).wait()
    return [buf.at[i % RING] for buf in buf_refs]


def _dot(a, b):
    return jnp.dot(a, b, preferred_element_type=F32)


def _dot_nt(a, b):
    return lax.dot_general(a, b, (((1,), (1,)), ((), ())), preferred_element_type=F32)


def _dot_tn(a, b):
    return lax.dot_general(a, b, (((0,), (0,)), ((), ())), preferred_element_type=F32)


def _rms(xf):
    r = lax.rsqrt(jnp.mean(xf * xf, axis=-1, keepdims=True) + EPS)
    return xf * r, r


def _rms_bwd(dn, xn, r):
    return r * (dn - xn * jnp.mean(dn * xn, axis=-1, keepdims=True))


def _silu(z):
    s = jax.nn.sigmoid(z)
    return z * s, s * (1.0 + z * (1.0 - s))


def _my_index():
    return 4 * lax.axis_index("x") + 2 * lax.axis_index("y") + lax.axis_index("c")


def _bias_table(rb_ref, bucket_ref, win_ref, out_ref):
    bk = jnp.where(win_ref[...] != 0, bucket_ref[...], -1)
    has_prev = lax.broadcasted_iota(jnp.int32, bk.shape, 0) >= BLOCK
    for h in range(N_Q_HEADS):
        acc = jnp.full(bk.shape, NEG_INF, F32)
        for b in range(N_BUCKETS):
            acc = jnp.where(bk == b, rb_ref[h, b], acc)
        cols = slice((h % 2) * BLOCK, (h % 2 + 1) * BLOCK)
        out_ref[1, h // 2, :, cols] = acc
        out_ref[0, h // 2, :, cols] = jnp.where(has_prev, acc, NEG_INF)


def _all_gather(shards, small_rows, casts, rel_bias_t, bucket_t, in_window_t):
    ns, nc, n = len(small_rows), len(casts), len(shards) + 1
    small_shape = (8, small_rows[0][1].shape[-1])
    shapes = [s.shape for s in shards] + [small_shape]
    pieces = [(t, r0, min(GATHER_PIECE_ROWS, shape[0] - r0))
              for t, shape in enumerate(shapes) for r0 in range(0, shape[0], GATHER_PIECE_ROWS)]

    def body(*refs):
        refs = list(refs)
        take = lambda k: [refs.pop(0) for _ in range(k)]
        ins, small_refs, cast_refs, (rb_ref, bucket_ref, win_ref) = take(n - 1), take(ns), take(nc), take(3)
        outs, cast_outs, (bias_ref, send_sems, recv_sems) = take(n), take(nc), take(3)
        x, y, c = lax.axis_index("x"), lax.axis_index("y"), lax.axis_index("c")
        me, sibling = (x, y, c), (x, y, 1 - c)
        x_nbr, y_nbr, diagonal = (1 - x, y), (x, 1 - y), (1 - x, 1 - y)
        south = c == 0
        relayed = (jnp.where(south, 1 - x, x), jnp.where(south, y, 1 - y))
        relay_to = (jnp.where(south, x, 1 - x), jnp.where(south, 1 - y, y))

        def copy(u, k, block, to):
            t, r0, nrows = pieces[u]
            rows = outs[t].at[4 * block[0] + 2 * block[1] + block[2], pl.ds(r0, nrows)]
            return pltpu.make_async_remote_copy(
                src_ref=rows, dst_ref=rows, send_sem=send_sems.at[u, k], recv_sem=recv_sems.at[u, k],
                device_id=to, device_id_type=MESH)

        mine = pl.ds(_my_index(), 1)
        for t in range(n - 1):
            outs[t][mine] = ins[t][...].astype(BF16)[None]
        outs[n - 1][mine] = jnp.zeros((1,) + small_shape, F32)
        for (row, _), ref in zip(small_rows, small_refs):
            if len(ref.shape) == 3:
                for j in range(ref.shape[0]):
                    outs[n - 1][mine, row + j:row + j + 1, :] = ref[j][None]
            else:
                outs[n - 1][mine, row:row + ref.shape[0], :] = ref[...][None]
        started = []

        def start(cp):
            cp.start()
            started.append(cp)

        units = range(len(pieces))
        for u in units:
            start(copy(u, 0, me, sibling))
            start(copy(u, 1, me, (*x_nbr, c)))
            start(copy(u, 2, me, (*y_nbr, c)))
        for u in units:
            for k, chip in ((1, x_nbr), (2, y_nbr)):
                copy(u, k, (*chip, c), me).wait_recv()
                start(copy(u, 3 + k, (*chip, c), sibling))
            start(copy(u, 3, (*relayed, c), (*relay_to, c)))
        for src, dst in zip(cast_refs, cast_outs):
            dst[...] = src[...].astype(BF16)
        _bias_table(rb_ref, bucket_ref, win_ref, bias_ref)
        for u in units:
            copy(u, 3, (*diagonal, c), me).wait_recv()
            start(copy(u, 6, (*diagonal, c), sibling))
        for u in units:
            copy(u, 0, sibling, me).wait_recv()
        for k, chip in ((4, x_nbr), (5, y_nbr), (6, diagonal)):
            for u in units:
                copy(u, k, (*chip, 1 - c), me).wait_recv()
        for cp in started:
            cp.wait_send()

    vmem = pl.BlockSpec(memory_space=pltpu.VMEM)
    outs = pl.pallas_call(
        body,
        name="gather_weights",
        out_shape=[jax.ShapeDtypeStruct((N_DEV,) + s.shape, BF16) for s in shards]
        + [jax.ShapeDtypeStruct((N_DEV,) + small_shape, F32)]
        + [jax.ShapeDtypeStruct(a.shape, BF16) for a in casts]
        + [jax.ShapeDtypeStruct((2, N_PAIRS, BAND, 2 * BLOCK), F32)],
        in_specs=[vmem] * (n - 1 + ns + nc) + [pl.BlockSpec(memory_space=pltpu.SMEM), vmem, vmem],
        out_specs=[vmem] * (n + nc + 1),
        scratch_shapes=[pltpu.SemaphoreType.DMA((len(pieces), 7)), pltpu.SemaphoreType.DMA((len(pieces), 7))],
        compiler_params=_params(),
    )(*shards, *[a for _, a in small_rows], *casts, rel_bias_t, bucket_t, in_window_t)
    return outs[:n - 1], outs[n - 1], outs[n:n + nc], outs[n + nc]


def _peer(k):
    x, y, c = lax.axis_index("x"), lax.axis_index("y"), lax.axis_index("c")
    px = 1 - x if k & 4 else x
    py = 1 - y if k & 2 else y
    pc = 1 - c if k & 1 else c
    return (px, py, pc), 4 * px + 2 * py + pc


def _exchange(srcs, dsts, send_sems, recv_sems, local_sems, scatter):
    me = _my_index()
    sends, arrivals = [], []
    for k in range(1, N_DEV):
        peer, pidx = _peer(k)
        for t, (src, dst) in enumerate(zip(srcs, dsts)):
            mine = src.at[pidx] if scatter else src
            sems = dict(send_sem=send_sems.at[t, k - 1], recv_sem=recv_sems.at[t, k - 1], device_id=peer, device_id_type=MESH)
            sends.append(pltpu.make_async_remote_copy(src_ref=mine, dst_ref=dst.at[me], **sems))
            arrivals.append(pltpu.make_async_remote_copy(src_ref=mine, dst_ref=dst.at[pidx], **sems))
    local = [pltpu.make_async_copy(src.at[me] if scatter else src, dst.at[me], local_sems.at[t])
             for t, (src, dst) in enumerate(zip(srcs, dsts))]
    return sends, arrivals, local


def _exchange_start(*args):
    sends, _, local = _exchange(*args)
    for cp in sends + local:
        cp.start()


def _exchange_wait(*args):
    sends, arrivals, local = _exchange(*args)
    for cp in arrivals:
        cp.wait_recv()
    for cp in sends:
        cp.wait_send()
    for cp in local:
        cp.wait()


def _exchange_sems(n):
    if not n:
        return []
    return [pltpu.SemaphoreType.DMA((n, N_DEV - 1)), pltpu.SemaphoreType.DMA((n, N_DEV - 1)), pltpu.SemaphoreType.DMA((n,))]


HBM_SPEC = pl.BlockSpec(memory_space=pl.ANY)


def _sum_slots(recv_ref, out_ref):
    rows = out_ref.shape[0]
    chunk = min(rows, 128)

    def add(i, carry):
        r0 = pl.multiple_of(i * chunk, chunk)
        acc = recv_ref[0, pl.ds(r0, chunk), :].astype(F32)
        for dev in range(1, N_DEV):
            acc = acc + recv_ref[dev, pl.ds(r0, chunk), :].astype(F32)
        out_ref[pl.ds(r0, chunk), :] = acc
        return carry

    lax.fori_loop(0, rows // chunk, add, 0)


N_CHIPS = N_DEV // 2


def _rows_loop(rows, fn):
    chunk = min(rows, 128)

    def step(i, carry):
        fn(pl.ds(pl.multiple_of(i * chunk, chunk), chunk))
        return carry

    lax.fori_loop(0, rows // chunk, step, 0)


def _chip_reduce(g_ref, out_ref, sib_ref, land_ref, send_ref, sems, swap_src=None):
    sib_send, sib_recv, ici_send, ici_recv = sems
    x, y, c = lax.axis_index("x"), lax.axis_index("y"), lax.axis_index("c")
    south = c == 0
    near =(jnp.where(south, 1 - x, x), jnp.where(south, y, 1 - y))
    far = (jnp.where(south, x, 1 - x), jnp.where(south, 1 - y, y))
    diagonal = (1 - x, 1 - y)
    rows = out_ref.shape[0]
    direct, fold, folded = 0, 1, 2

    def to_sibling(t):
        src = g_ref if swap_src is None else swap_src
        return pltpu.make_async_remote_copy(
            src_ref=src.at[2 * t + 1 - c], dst_ref=sib_ref.at[t], send_sem=sib_send.at[t], recv_sem=sib_recv.at[t],
            device_id=(x, y, 1 - c), device_id_type=MESH)

    def ici(role, chip):
        return pltpu.make_async_remote_copy(
            src_ref=send_ref.at[role], dst_ref=land_ref.at[role], send_sem=ici_send.at[role],
            recv_sem=ici_recv.at[role], device_id=(*chip, c), device_id_type=MESH)

    def pair_sum(chip, r):
        t = 2 * chip[0] + chip[1]
        return g_ref[2 * t + c, r, :].astype(F32) + sib_ref[t, r, :].astype(F32)

    def swap():
        for t in range(N_CHIPS):
            to_sibling(t).start()

    def send():
        for t in range(N_CHIPS):
            to_sibling(t).wait_recv()
        for role, chip in ((fold, diagonal), (direct, near)):
            def fill(r, role=role, chip=chip):
                send_ref[role, r, :] = pair_sum(chip, r).astype(BF16)

            _rows_loop(rows, fill)
            ici(role, near).start()

    def forward():
        ici(fold, near).wait_recv()

        def fill(r):
            send_ref[folded, r, :] = (pair_sum(far, r) + land_ref[fold, r, :].astype(F32)).astype(BF16)

        _rows_loop(rows, fill)
        ici(folded, far).start()

    def finish():
        ici(direct, near).wait_recv()
        ici(folded, far).wait_recv()

        def total(r):
            mine = pair_sum((x, y), r)
            out_ref[r, :] = mine + land_ref[direct, r, :].astype(F32) + land_ref[folded, r, :].astype(F32)

        _rows_loop(rows, total)
        for t in range(N_CHIPS):
            to_sibling(t).wait_send()
        for role, chip in ((direct, near), (fold, near), (folded, far)):
            ici(role, chip).wait_send()

    return swap, send, forward, finish


def _chip_reduce_scratch(slot):
    return [pltpu.VMEM((N_CHIPS,) + slot, BF16), pltpu.VMEM((3,) + slot, BF16), pltpu.VMEM((3,) + slot, BF16),
            pltpu.SemaphoreType.DMA((N_CHIPS,)), pltpu.SemaphoreType.DMA((N_CHIPS,)),
            pltpu.SemaphoreType.DMA((3,)), pltpu.SemaphoreType.DMA((3,))]


def _bucket_sums(a_ref, bucket_ref, cols):
    a = a_ref[:, cols]
    hi = a.astype(BF16)
    lo = (a - hi.astype(F32)).astype(BF16)
    rows = lax.broadcasted_iota(jnp.int32, (LANES, a.shape[1]), 0)
    onehot_t = (rows == bucket_ref[:, cols]).astype(F32).astype(BF16)
    return _dot_nt(hi, onehot_t) + _dot_nt(lo, onehot_t)


def _reduce_exchange(part, landed, smalls, by_bucket, bucket_row, chunk):
    nl, ng = len(landed), len(smalls)
    n_in = 1 + nl + ng + 2
    n_out = 1 + nl + ng + 1
    heads, positions = by_bucket.shape
    chunks = [slice(c0, c0 + chunk) for c0 in range(0, positions, chunk)]

    def body(*refs):
        p_in, l_in, s_in, (a_ref, bucket_ref) = refs[0], refs[1:1 + nl], refs[1 + nl:n_in - 2], refs[n_in - 2:n_in]
        refs = refs[n_in:]
        p_out, l_out, s_out, b_out = refs[0], refs[1:1 + nl], refs[1 + nl:n_out - 1], refs[n_out - 1]
        scratch = refs[n_out:]
        s_recv, (b_recv, b_ref, sib_ref, chip_ref, send_ref), sems = scratch[:ng], scratch[ng:ng + 5], scratch[ng + 5:]
        swap, send, forward, finish = _chip_reduce(p_in, p_out, sib_ref, chip_ref, send_ref, sems[:4])
        swap()
        _exchange_start(s_in, s_recv, *sems[4:7], False)
        b_ref[...] = jnp.zeros_like(b_ref)
        for cols in chunks[:len(chunks) // 2]:
            b_ref[...] += _bucket_sums(a_ref, bucket_ref, cols)
        send()
        for t in range(nl):
            _sum_slots(l_in[t], l_out[t])
        for cols in chunks[len(chunks) // 2:]:
            b_ref[...] += _bucket_sums(a_ref, bucket_ref, cols)
        _exchange_start([b_ref], [b_recv], *sems[7:], False)
        forward()
        finish()
        _exchange_wait(s_in, s_recv, *sems[4:7], False)
        _exchange_wait([b_ref], [b_recv], *sems[7:], False)
        for recv, out in zip([*s_recv, b_recv], [*s_out, b_out]):
            acc = recv[0]
            for dev in range(1, N_DEV):
                acc = acc + recv[dev]
            out[...] = acc

    vmem = pl.BlockSpec(memory_space=pltpu.VMEM)
    slot = part.shape[1:]
    outs = pl.pallas_call(
        body,
        name="reduce_grads",
        out_shape=[jax.ShapeDtypeStruct(p.shape[1:], F32) for p in [part] + landed]
        + [jax.ShapeDtypeStruct(s.shape, F32) for s in smalls] + [jax.ShapeDtypeStruct((heads, LANES), F32)],
        in_specs=[vmem] * n_in,
        out_specs=[vmem] * n_out,
        scratch_shapes=[pltpu.VMEM((N_DEV,) + s.shape, F32) for s in smalls]
        + [pltpu.VMEM((N_DEV, heads, LANES), F32), pltpu.VMEM((heads, LANES), F32)] + _chip_reduce_scratch(slot)
        + _exchange_sems(ng) + _exchange_sems(1),
        compiler_params=_params(),
    )(part, *landed, *smalls, by_bucket, bucket_row)
    return outs[0], outs[1:1 + nl], outs[1 + nl:n_out - 1], outs[n_out - 1]


def _layer_a_fwd(x2, sm, win_g, wout, later, ts):
    seq, d = x2.shape
    width = wout.shape[0]
    half = win_g.shape[2]
    n_half = width // half
    nl = len(later)
    nt = seq // ts

    def body(x_ref, sm_ref, win_ref, wout_ref, *refs):
        shard_refs, refs = refs[:nl], refs[nl:]
        h1_ref, n1_ref, proj_ref, conv_ref, y_ref, ya_ref = refs[:6]
        gathered_refs, (vprev_ref, *sems) = refs[6:6 + nl], refs[6 + nl:]

        @pl.when(pl.program_id(0) == 0)
        def _():
            vprev_ref[...] = jnp.zeros_like(vprev_ref)
            _exchange_start(shard_refs, gathered_refs, *sems, False)

        @pl.when(pl.program_id(0) == nt - 1)
        def _():
            _exchange_wait(shard_refs, gathered_refs, *sems, False)

        xf = x_ref[...]
        xn, _ = _rms(xf)
        n1 = (xn * sm_ref[0:1, :]).astype(BF16)
        n1_ref[...] = n1
        row = lax.broadcasted_iota(jnp.int32, (ts, half), 0)
        ya = jnp.zeros((ts, d), F32)
        for hh in range(n_half):
            cols = slice(hh * half, (hh + 1) * half)
            parts = []
            for part in range(4):
                j = part * n_half + hh
                pj = _dot(n1, win_ref[j])
                proj_ref[:, j * half:(j + 1) * half] = pj.astype(BF16)
                parts.append(pj)
            b, c, u, z = parts
            v = c * u
            last1, last2 = vprev_ref[7:8, cols], vprev_ref[6:7, cols]
            v1 = jnp.where(row == 0, last1, pltpu.roll(v, 1, 0))
            v2 = jnp.where(row == 0, last2, jnp.where(row == 1, last1, pltpu.roll(v, 2, 0)))
            vprev_ref[:, cols] = v[ts - 8:ts, :]
            conv = sm_ref[1:2, cols] * v2 + sm_ref[2:3, cols] * v1 + sm_ref[3:4, cols] * v
            conv_ref[:, cols] = conv.astype(BF16)
            yh = (b * conv * _silu(z)[0]).astype(BF16)
            y_ref[:, cols] = yh
            ya = ya + _dot(yh, wout_ref[cols, :])
        ya_ref[...] = ya
        h1_ref[...] = xf + _rms(ya)[0] * sm_ref[4:5, :]

    outs = pl.pallas_call(
        body,
        name="layer_a_fwd",
        grid=(nt,),
        in_specs=[_rows(ts, d), _full(sm.shape), _full(win_g.shape), _full(wout.shape)] + [HBM_SPEC] * nl,
        out_specs=[_rows(ts, d), _rows(ts, d), _rows(ts, 4 * width), _rows(ts, width), _rows(ts, width), _rows(ts, d)]
        + [HBM_SPEC] * nl,
        out_shape=[
            jax.ShapeDtypeStruct((seq, d), F32),
            jax.ShapeDtypeStruct((seq, d), BF16),
            jax.ShapeDtypeStruct((seq, 4 * width), BF16),
            jax.ShapeDtypeStruct((seq, width), BF16),
            jax.ShapeDtypeStruct((seq, width), BF16),
            jax.ShapeDtypeStruct((seq, d), F32),
        ] + [jax.ShapeDtypeStruct((N_DEV,) + s.shape, s.dtype) for s in later],
        scratch_shapes=[pltpu.VMEM((8, width), F32)] + _exchange_sems(nl),
        compiler_params=_params(("arbitrary",)),
    )(x2, sm, win_g, wout, *later)
    return outs[:6], outs[6:]


Q_BLOCKS = 4
ATTN_BWD_LAGS = (2, 4)
ATTN_FWD_LAGS = (2, 4)


def _banded_tiles(kvp_ref, kvc_ref):
    tile = kvc_ref[...].astype(F32)
    blocks = [kvp_ref[...].astype(F32)] + [tile[u * BLOCK:(u + 1) * BLOCK] for u in range(Q_BLOCKS)]
    return [_banded_kv(blocks[u], blocks[u + 1]) for u in range(Q_BLOCKS)]


def _bias_of(bias_ref, i, u, m):
    return bias_ref[jnp.minimum(i, 1) if u == 0 else 1, m]


def _banded_kv(kvp, kvc):
    kw = N_KV_HEADS * HEAD_DIM
    out = []
    for full in (jnp.concatenate([kvp[:, :kw], kvc[:, :kw]], axis=0), jnp.concatenate([kvp[:, kw:], kvc[:, kw:]], axis=0)):
        lo = lax.broadcasted_iota(jnp.int32, full.shape, 1) < HEAD_DIM
        rolled = pltpu.roll(full, HEAD_DIM, 1)
        x2 = [jnp.where(lo, full, rolled).astype(BF16), jnp.where(lo, rolled, full).astype(BF16)]
        ft = full.T
        x2t = [jnp.concatenate([ft[kh * HEAD_DIM:(kh + 1) * HEAD_DIM]] * 2, axis=0).astype(BF16) for kh in range(N_KV_HEADS)]
        out += [x2, x2t]
    return out


def _pair_rows(ref, rows, m, scale=None):
    both = ref[rows, m * LANES:(m + 1) * LANES].astype(F32)
    if scale is not None:
        both = both * scale
    lo = lax.broadcasted_iota(jnp.int32, both.shape, 1) < HEAD_DIM
    zero = jnp.zeros_like(both)
    return jnp.concatenate([jnp.where(lo, both, zero), jnp.where(lo, zero, both)], axis=0).astype(BF16)


def _pair_cols(res_t):
    top = lax.broadcasted_iota(jnp.int32, (LANES, BLOCK), 0) < HEAD_DIM
    return jnp.where(top, res_t[:, :BLOCK], res_t[:, BLOCK:]).T


def _sink_row(sink_ref, m):
    first = lax.broadcasted_iota(jnp.int32, (1, 2 * BLOCK), 1) < BLOCK
    return jnp.where(first, sink_ref[0, 2 * m], sink_ref[0, 2 * m + 1])


def _softmax_t(logits, sink):
    mx =jnp.maximum(jnp.max(logits, axis=0, keepdims=True), sink)
    p = jnp.exp(logits - mx)
    sink_p = jnp.exp(sink - mx)
    inv = 1.0 / (jnp.sum(p, axis=0, keepdims=True) + sink_p)
    return p * inv, sink_p * inv


def _layer_b_fwd(h1, target, kvn, bpre, wkv, wbin_g, biasm, sinks, wbout, bpost):
    seq, d = h1.shape
    kvw = wkv.shape[1]
    cw = wbin_g.shape[2]
    aw = N_Q_HEADS * HEAD_DIM
    per = aw // cw
    tile = Q_BLOCKS * BLOCK

    def body(sink_ref, h1_ref, tgt_ref, kvn_ref, bpre_ref, wkv_ref, wbin_ref, bias_ref, w_ref, g_ref,
             n3_ref, n4_ref, kvc_ref, q_ref, o_ref, dh2_ref, dyb_ref, dattn_ref, dz2_ref, acc_ref,
             attn_ref, z2_ref, kvp_ref):
        i = pl.program_id(0)

        @pl.when(i == 0)
        def _():
            acc_ref[...] = jnp.zeros_like(acc_ref)
            kvp_ref[...] = jnp.zeros_like(kvp_ref)

        hn, _ = _rms(h1_ref[...])
        n3 = (hn * kvn_ref[...]).astype(BF16)
        n4 = (hn * bpre_ref[...]).astype(BF16)
        n3_ref[...] = n3
        n4_ref[...] = n4
        kvc_ref[...] = _dot(n3, wkv_ref[...]).astype(BF16)
        for j in range(N_DEV):
            pj = _dot(n4, wbin_ref[j])
            if j < per:
                q_ref[:, j * cw:(j + 1) * cw] = pj.astype(BF16)
            else:
                z2_ref[:, (j - per) * cw:(j - per + 1) * cw] = pj

        banded = _banded_tiles(kvp_ref, kvc_ref)
        kvp_ref[...] = kvc_ref[tile - BLOCK:tile, :]
        units = [(u, m) for u in range(Q_BLOCKS) for m in range(N_PAIRS)]
        kv_of = lambda m: (2 * m) // GROUP
        logits, probs = {}, {}
        lag_b, lag_c = ATTN_FWD_LAGS
        for step in range(len(units) + lag_c):
            if step < len(units):
                u, m = units[step]
                qpair = _pair_rows(q_ref, slice(u * BLOCK, (u + 1) * BLOCK), m, SCALE)
                logits[step] = _dot_nt(banded[u][0][kv_of(m)], qpair) + _bias_of(bias_ref, i, u, m)
            if 0 <= step - lag_b < len(units):
                u, m = units[step - lag_b]
                probs[step - lag_b] = _softmax_t(logits.pop(step - lag_b), _sink_row(sink_ref, m))[0].astype(BF16)
            if 0 <= step - lag_c < len(units):
                u, m = units[step - lag_c]
                out_t = _dot(banded[u][3][kv_of(m)], probs.pop(step - lag_c))
                attn_ref[u * BLOCK:(u + 1) * BLOCK, m * LANES:(m + 1) * LANES] = _pair_cols(out_t)
        attn = attn_ref[...]
        sz, dsz = _silu(z2_ref[...])
        o = (attn * sz).astype(BF16)
        o_ref[...] = o

        w = w_ref[...]
        yb = _dot(o, w)
        ybn, r = _rms(yb)
        g = g_ref[...]
        diff = h1_ref[...] + ybn * g - tgt_ref[...]
        dh2 = diff * (1.0 / d)
        dh2_ref[...] = dh2
        acc_ref[0:1, :] += jnp.sum(dh2 * ybn, axis=0, keepdims=True)
        tok = jnp.mean(diff * diff, axis=-1, keepdims=True)
        acc_ref[1:2, :] += 0.5 * jnp.sum(tok, axis=0, keepdims=True)
        dyb = _rms_bwd(dh2 * g, ybn, r).astype(BF16)
        dyb_ref[...] = dyb
        do = _dot_nt(dyb, w)
        dattn_ref[...] = (do * sz).astype(BF16)
        dz2_ref[...] = (do * attn * dsz).astype(BF16)

    blk = lambda w: pl.BlockSpec((tile, w), lambda i: (i, 0))
    return pl.pallas_call(
        body,
        name="layer_b_fwd",
        grid=(seq // tile,),
        in_specs=[
            pl.BlockSpec(memory_space=pltpu.SMEM),
            blk(d),
            blk(d),
            _full(kvn.shape),
            _full(bpre.shape),
            _full(wkv.shape),
            _full(wbin_g.shape),
            _full(biasm.shape),
            _full(wbout.shape),
            _full(bpost.shape),
        ],
        out_specs=[blk(d), blk(d), blk(kvw), blk(aw), blk(aw), blk(d), blk(d), blk(aw), blk(aw), _resident((8, d))],
        out_shape=[
            jax.ShapeDtypeStruct((seq, d), BF16),
            jax.ShapeDtypeStruct((seq, d), BF16),
            jax.ShapeDtypeStruct((seq, kvw), BF16),
            jax.ShapeDtypeStruct((seq, aw), BF16),
            jax.ShapeDtypeStruct((seq, aw), BF16),
            jax.ShapeDtypeStruct((seq, d), F32),
            jax.ShapeDtypeStruct((seq, d), BF16),
            jax.ShapeDtypeStruct((seq, aw), BF16),
            jax.ShapeDtypeStruct((seq, aw), BF16),
            jax.ShapeDtypeStruct((8, d), F32),
        ],
        scratch_shapes=[pltpu.VMEM((tile, aw), F32), pltpu.VMEM((tile, aw), F32), pltpu.VMEM((BLOCK, kvw), BF16)],
        compiler_params=_params(("arbitrary",)),
    )(sinks, h1, target, kvn, bpre, wkv, wbin_g, biasm, wbout, bpost)


def _attn_bwd(q, kv, dattn, biasm, sinks, ready):
    seq, aw = q.shape
    kvw = kv.shape[1]
    kw = N_KV_HEADS * HEAD_DIM
    nb = seq // BLOCK
    pairs_per_kv = N_PAIRS // N_KV_HEADS
    nr = len(ready)

    tile = Q_BLOCKS * BLOCK
    nsteps = seq // tile
    held = (Q_BLOCKS - 1) * BLOCK

    def body(sink_ref, q_ref, kvc_ref, kvp_ref, da_ref, bias_ref, *refs):
        ready_refs, (dq_ref, dkv_ref, dssum_ref, dsink_ref) = refs[:nr], refs[nr:nr + 4]
        landed_refs, scratch = refs[nr + 4:2 * nr + 4], refs[2 * nr + 4:]
        carry_ref, done_ref, qs_ref, dos_ref, dst_ref, pt_ref, *sems = scratch
        i = pl.program_id(0)

        @pl.when(i == 0)
        def _():
            dssum_ref[...] = jnp.zeros_like(dssum_ref)
            dsink_ref[...] = jnp.zeros_like(dsink_ref)
            carry_ref[...] = jnp.zeros_like(carry_ref)
            done_ref[...] = jnp.zeros_like(done_ref)
            if nr:
                _exchange_start(ready_refs, landed_refs, *sems, True)

        if nr:
            @pl.when(i == nsteps)
            def _():
                _exchange_wait(ready_refs, landed_refs, *sems, True)

        @pl.when(i < nsteps)
        def _():
            lo = lax.broadcasted_iota(jnp.int32, (BAND, LANES), 1) < HEAD_DIM
            head_lane = lax.broadcasted_iota(jnp.int32, (1, LANES), 1)
            banded = _banded_tiles(kvp_ref, kvc_ref)
            units = [(u, m) for u in range(Q_BLOCKS) for m in range(N_PAIRS)]
            dsink = jnp.zeros((1, LANES), F32)
            folded = {}
            logits, dps, dsbs = {}, {}, {}
            lag_b, lag_c = ATTN_BWD_LAGS
            for step in range(len(units) + lag_c):
                if step < len(units):
                    u, m = units[step]
                    kh, rows = m // pairs_per_kv, slice((m % pairs_per_kv) * BAND, (m % pairs_per_kv + 1) * BAND)
                    qrows = slice(u * BLOCK, (u + 1) * BLOCK)
                    qpair = _pair_rows(q_ref, qrows, m, SCALE)
                    dopair = _pair_rows(da_ref, qrows, m)
                    qs_ref[u, kh, rows, :] = qpair
                    dos_ref[u, kh, rows, :] = dopair
                    logits[step] = _dot_nt(banded[u][0][kh], qpair) + _bias_of(bias_ref, i, u, m)
                    dps[step] = _dot_nt(banded[u][2][kh], dopair)
                if 0 <= step - lag_b < len(units):
                    u, m = units[step - lag_b]
                    kh, rows = m // pairs_per_kv, slice((m % pairs_per_kv) * BAND, (m % pairs_per_kv + 1) * BAND)
                    pn, sink_p = _softmax_t(logits.pop(step - lag_b), _sink_row(sink_ref, m))
                    dp = dps.pop(step - lag_b)
                    delta = jnp.sum(pn * dp, axis=0, keepdims=True)
                    ds = pn * (dp - delta)
                    dssum_ref[m] += ds
                    sink_term = sink_p * delta
                    for e in range(2):
                        total = jnp.sum(sink_term[:, e * BLOCK:(e + 1) * BLOCK], axis=1, keepdims=True)
                        dsink = dsink - jnp.where(head_lane == 2 * m + e, total, 0.0)
                    dsbs[step - lag_b] = ds.astype(BF16)
                    dst_ref[u, kh, :, rows] = dsbs[step - lag_b]
                    pt_ref[u, kh, :, rows] = pn.astype(BF16)
                if 0 <= step - lag_c < len(units):
                    u, m = units[step - lag_c]
                    kh = m // pairs_per_kv
                    dq_t = _dot(banded[u][1][kh], dsbs.pop(step - lag_c))
                    dq_ref[u * BLOCK:(u + 1) * BLOCK, m * LANES:(m + 1) * LANES] = (_pair_cols(dq_t) * SCALE).astype(BF16)
                    if m % pairs_per_kv == pairs_per_kv - 1:
                        for name, lhs_ref, rhs_ref in (("k", dst_ref, qs_ref), ("v", pt_ref, dos_ref)):
                            acc = _dot(lhs_ref[u, kh], rhs_ref[u, kh])
                            folded[u, kh, name] = acc + pltpu.roll(acc, HEAD_DIM, 1)
            dsink_ref[0:1, :] += dsink
            dkv = [jnp.concatenate([jnp.where(lo, folded[u, 0, n], folded[u, 1, n]) for n in ("k", "v")], axis=1)
                   for u in range(Q_BLOCKS)]

            @pl.when(i > 0)
            def _():
                if held:
                    dkv_ref[:held, :] = done_ref[...].astype(BF16)
                dkv_ref[held:, :] = (carry_ref[...] + dkv[0][:BLOCK]).astype(BF16)

            for u in range(Q_BLOCKS - 1):
                done_ref[u * BLOCK:(u + 1) * BLOCK, :] = dkv[u][BLOCK:] + dkv[u + 1][:BLOCK]
            carry_ref[...] = dkv[Q_BLOCKS - 1][BLOCK:]

        @pl.when(i == nsteps)
        def _():
            if held:
                dkv_ref[:held, :] = done_ref[...].astype(BF16)
            dkv_ref[held:, :] = carry_ref[...].astype(BF16)

    last = nsteps - 1
    blk = lambda w: pl.BlockSpec((tile, w), lambda i: (jnp.minimum(i, last), 0))
    outs = pl.pallas_call(
        body,
        name="attn_bwd",
        grid=(nsteps + 1,),
        in_specs=[
            pl.BlockSpec(memory_space=pltpu.SMEM),
            blk(aw),
            blk(kvw),
            pl.BlockSpec((BLOCK, kvw), lambda i: (jnp.clip(Q_BLOCKS * i - 1, 0, nb - 1), 0)),
            blk(aw),
            _full(biasm.shape),
        ] + [HBM_SPEC] * nr,
        out_specs=[
            blk(aw),
            pl.BlockSpec((tile, kvw), lambda i: (jnp.maximum(i - 1, 0), 0)),
            _resident(biasm.shape[1:]),
            _resident((8, LANES)),
        ] + [HBM_SPEC] * nr,
        out_shape=[
            jax.ShapeDtypeStruct((seq, aw), BF16),
            jax.ShapeDtypeStruct((seq, kvw), BF16),
            jax.ShapeDtypeStruct(biasm.shape[1:], F32),
            jax.ShapeDtypeStruct((8, LANES), F32),
        ] + [jax.ShapeDtypeStruct(g.shape, g.dtype) for g in ready],
        scratch_shapes=[
            pltpu.VMEM((BLOCK, kvw), F32),
            pltpu.VMEM((max(held, 8), kvw), F32),
            pltpu.VMEM((Q_BLOCKS, N_KV_HEADS, pairs_per_kv * BAND, LANES), BF16),
            pltpu.VMEM((Q_BLOCKS, N_KV_HEADS, pairs_per_kv * BAND, LANES), BF16),
            pltpu.VMEM((Q_BLOCKS, N_KV_HEADS, BAND, pairs_per_kv * BAND), BF16),
            pltpu.VMEM((Q_BLOCKS, N_KV_HEADS, BAND, pairs_per_kv * BAND), BF16),
        ] + _exchange_sems(nr),
        compiler_params=_params(("arbitrary",)),
    )(sinks, q, kv, kv, dattn, biasm, *ready)
    return outs[:4], outs[4:]


def _layer_b_in_bwd(dh2, dq, dz2, dkv, h1, ya, wbin_g, wkv, kvn, bpre, sm, ready, ts):
    seq, d = h1.shape
    aw = dq.shape[1]
    kvw = dkv.shape[1]
    cw = wbin_g.shape[2]
    per = aw // cw

    nr = len(ready)
    nt = seq // ts

    def body(*refs):
        streamed, (wbin_ref, wkv_ref, kvn_ref, bpre_ref, sm_ref), refs = refs[:6], refs[6:11], refs[11:]
        ready_refs, (dh1_ref, dya_ref, acc_ref) = refs[:nr], refs[nr:nr + 3]
        landed_refs, (*bufs, ring_sems), sems = refs[nr + 3:2 * nr + 3], refs[2 * nr + 3:2 * nr + 10], refs[2 * nr + 10:]
        dh2_ref, dq_ref, dz2_ref, dkv_ref, h1_ref, ya_ref = _row_ring(streamed, bufs, ring_sems, nt)

        @pl.when(pl.program_id(0) == 0)
        def _():
            acc_ref[...] = jnp.zeros_like(acc_ref)
            if nr:
                _exchange_start(ready_refs, landed_refs, *sems, True)

        if nr:
            @pl.when(pl.program_id(0) == nt - 1)
            def _():
                _exchange_wait(ready_refs, landed_refs, *sems, True)

        dn4 = jnp.zeros((ts, d), F32)
        for j in range(N_DEV):
            src = dq_ref if j < per else dz2_ref
            jj = j % per
            dn4 = dn4 + _dot_nt(src[:, jj * cw:(jj + 1) * cw], wbin_ref[j])
        dn3 = _dot_nt(dkv_ref[...], wkv_ref[...])
        hn, r = _rms(h1_ref[...])
        acc_ref[0:1, :] += jnp.sum(dn4 * hn, axis=0, keepdims=True)
        acc_ref[1:2, :] += jnp.sum(dn3 * hn, axis=0, keepdims=True)
        dh1 = dh2_ref[...] + _rms_bwd(dn4 * bpre_ref[...] + dn3 * kvn_ref[...], hn, r)
        dh1_ref[...] = dh1
        yan, r2 = _rms(ya_ref[...])
        acc_ref[2:3, :] += jnp.sum(dh1 * yan, axis=0, keepdims=True)
        dya_ref[...] = _rms_bwd(dh1 * sm_ref[4:5, :], yan, r2).astype(BF16)

    outs = pl.pallas_call(
        body,
        name="layer_b_in_bwd",
        grid=(nt,),
        in_specs=[HBM_SPEC] * 6
        + [_full(wbin_g.shape), _full(wkv.shape), _full(kvn.shape), _full(bpre.shape), _full(sm.shape)]
        + [HBM_SPEC] * nr,
        out_specs=[_rows(ts, d), _rows(ts, d), _resident((8, d))] + [HBM_SPEC] * nr,
        out_shape=[jax.ShapeDtypeStruct((seq, d), F32), jax.ShapeDtypeStruct((seq, d), BF16),
                   jax.ShapeDtypeStruct((8, d), F32)] + [jax.ShapeDtypeStruct(g.shape, g.dtype) for g in ready],
        scratch_shapes=[pltpu.VMEM((RING, ts, a.shape[1]), a.dtype) for a in (dh2, dq, dz2, dkv, h1, ya)]
        + [pltpu.SemaphoreType.DMA((6, RING))] + _exchange_sems(nr),
        compiler_params=_params(("arbitrary",)),
    )(dh2, dq, dz2, dkv, h1, ya, wbin_g, wkv, kvn, bpre, sm, *ready)
    return outs[:3], outs[3:]


def _layer_a_bwd(dya, proj, conv, dh1, x2, wout, win_g, sm, ts):
    seq, d = x2.shape
    width = wout.shape[0]
    half = win_g.shape[2]
    n_half = width // half
    nt = seq // ts

    def body(dya_ref, proj_ref, conv_ref, dh1_ref, x_ref, wout_ref, win_ref, sm_ref, dproj_ref, gx_ref, acc_ref,
             dnext_ref):
        @pl.when(pl.program_id(0) == 0)
        def _():
            acc_ref[...] = jnp.zeros_like(acc_ref)
            dnext_ref[...] = jnp.zeros_like(dnext_ref)

        dy = _dot_nt(dya_ref[...], wout_ref[...])
        row = lax.broadcasted_iota(jnp.int32, (ts, half), 0)
        dn1 = jnp.zeros((ts, d), F32)
        for hh in range(n_half):
            cols = slice(hh * half, (hh + 1) * half)
            b, c, u, z = [proj_ref[:, (part * n_half + hh) * half:(part * n_half + hh + 1) * half].astype(F32)
                          for part in range(4)]
            cv = conv_ref[:, cols].astype(F32)
            dyh = dy[:, cols]
            sz, dsz = _silu(z)
            dconv = dyh * b * sz
            grads = [dyh * cv * sz, None, None, dyh * b * cv * dsz]
            next0, next1 = dnext_ref[0:1, cols], dnext_ref[1:2, cols]
            dc1 = jnp.where(row == ts - 1, next0, pltpu.roll(dconv, ts - 1, 0))
            dc2 = jnp.where(row == ts - 1, next1, jnp.where(row == ts - 2, next0, pltpu.roll(dconv, ts - 2, 0)))
            dnext_ref[:, cols] = dconv[0:8, :]
            v = c * u
            acc_ref[1:2, cols] += jnp.sum(dc2 * v, axis=0, keepdims=True)
            acc_ref[2:3, cols] += jnp.sum(dc1 * v, axis=0, keepdims=True)
            acc_ref[3:4, cols] += jnp.sum(dconv * v, axis=0, keepdims=True)
            dv = sm_ref[3:4, cols] * dconv + sm_ref[2:3, cols] * dc1 + sm_ref[1:2, cols] * dc2
            grads[1] = dv * u
            grads[2] = dv * c
            for part in range(4):
                j = part * n_half + hh
                gj = grads[part].astype(BF16)
                dproj_ref[:, j * half:(j + 1) * half] = gj
                dn1 = dn1 + _dot_nt(gj, win_ref[j])
        xn, r = _rms(x_ref[...])
        acc_ref[0:1, :] += jnp.sum(dn1 * xn, axis=0, keepdims=True)
        gx_ref[...] = dh1_ref[...] + _rms_bwd(dn1 * sm_ref[0:1, :], xn, r)

    rev = lambda w: pl.BlockSpec((ts, w), lambda i: (nt - 1 - i, 0))
    return pl.pallas_call(
        body,
        name="layer_a_bwd",
        grid=(nt,),
        in_specs=[rev(d), rev(4 * width), rev(width), rev(d), rev(d), _full(wout.shape), _full(win_g.shape), _full(sm.shape)],
        out_specs=[rev(4 * width), rev(d), _resident((8, d))],
        out_shape=[jax.ShapeDtypeStruct((seq, 4 * width), BF16), jax.ShapeDtypeStruct((seq, d), F32),
                   jax.ShapeDtypeStruct((8, d), F32)],
        scratch_shapes=[pltpu.VMEM((8, width), F32)],
        compiler_params=_params(("arbitrary",)),
    )(dya, proj, conv, dh1, x2, wout, win_g, sm)


def _wgrad(a, bs, n_slots, ts, name, ready=(), block_cols=1024):
    nr = len(ready)
    seq, k = a.shape
    nb_in = len(bs)
    n_each = bs[0].shape[1]
    n = nb_in * n_each
    bn = min(n_each, block_cols)
    per_in = n_each // bn
    n_blocks = nb_in * per_in
    ns = seq // ts

    def b_spec(idx):
        def index(j, s):
            mine = j // per_in == idx
            row = jnp.where(mine, s, jnp.where(j // per_in > idx, ns - 1, 0))
            return (row, jnp.where(mine, j % per_in, jnp.where(j // per_in > idx, per_in - 1, 0)))
        return pl.BlockSpec((ts, bn), index)

    if n_slots:
        sw = n // n_slots
        spb = bn // sw
        out_shape = jax.ShapeDtypeStruct((n_slots, k, sw), BF16)
        out_spec = pl.BlockSpec((spb, k, sw), lambda j, s: (j, 0, 0))
    else:
        out_shape = jax.ShapeDtypeStruct((k, n), BF16)
        out_spec = pl.BlockSpec((k, bn), lambda j, s: (0, j))

    def body(a_ref, *refs):
        b_refs, ready_refs, o_ref = refs[:nb_in], refs[nb_in:nb_in + nr], refs[nb_in + nr]
        landed_refs, (acc_ref, *sems) = refs[nb_in + nr + 1:nb_in + 2 * nr + 1], refs[nb_in + 2 * nr + 1:]
        j, s = pl.program_id(0), pl.program_id(1)

        if nr:
            @pl.when(jnp.logical_and(j == 0, s == 0))
            def _():
                _exchange_start(ready_refs, landed_refs, *sems, True)

            @pl.when(jnp.logical_and(j == n_blocks - 1, s == ns - 1))
            def _():
                _exchange_wait(ready_refs, landed_refs, *sems, True)

        @pl.when(s == 0)
        def _():
            acc_ref[...] = jnp.zeros_like(acc_ref)

        for idx in range(nb_in):
            @pl.when(j // per_in == idx)
            def _(idx=idx):
                acc_ref[...] += _dot_tn(a_ref[...], b_refs[idx][...])

        @pl.when(s == ns - 1)
        def _():
            if n_slots:
                for e in range(spb):
                    o_ref[e] = acc_ref[:, e * sw:(e + 1) * sw].astype(BF16)
            else:
                o_ref[...] = acc_ref[...].astype(BF16)

    outs = pl.pallas_call(
        body,
        name=name,
        grid=(n_blocks, ns),
        in_specs=[pl.BlockSpec((ts, k), lambda j, s: (s, 0))] + [b_spec(idx) for idx in range(nb_in)] + [HBM_SPEC] * nr,
        out_specs=[out_spec] + [HBM_SPEC] * nr,
        out_shape=[out_shape] + [jax.ShapeDtypeStruct(g.shape, g.dtype) for g in ready],
        scratch_shapes=[pltpu.VMEM((k, bn), F32)] + (_exchange_sems(nr) if nr else []),
        compiler_params=_params(("arbitrary", "arbitrary")),
    )(a, *bs, *ready)
    return (outs[0], outs[1:]) if nr else outs[0]


def _wgrad_tail(pairs, part, landed, ts):
    n_tasks = len(pairs)
    assert n_tasks == 2
    nl = len(landed)
    seq, k = pairs[0][0].shape
    n = pairs[0][1].shape[1]
    ns = seq // ts
    total = n_tasks * ns
    per = k // N_DEV
    n_red = len(_chip_reduce_scratch((per, n)))

    def spec(t, width):
        return pl.BlockSpec((ts, width), lambda j, s: (jnp.where(j == t, s, jnp.where(j > t, ns - 1, 0)), 0))

    def body(*refs):
        ab_refs, part_hbm = refs[:2 * n_tasks], refs[2 * n_tasks]
        landed_hbm, refs = refs[2 * n_tasks + 1:2 * n_tasks + 1 + nl], refs[2 * n_tasks + 1 + nl:]
        o_ref, red_ref, early_ref = refs[:3]
        summed_refs, (acc_ref, first_ref, part_ref, *scratch) = refs[3:3 + nl], refs[3 + nl:]
        landed_refs, load_sems, scratch = scratch[:nl], scratch[nl], scratch[nl + 1:]
        j, s = pl.program_id(0), pl.program_id(1)
        flat = j * ns + s
        swap, send, forward, finish = _chip_reduce(part_ref, red_ref, *scratch[:3], scratch[3:n_red], part_hbm)
        swap_first, send_first, forward_first, finish_first = _chip_reduce(
            first_ref, early_ref, *scratch[n_red:n_red + 3], scratch[n_red + 3:])
        loads = [pltpu.make_async_copy(src, dst, load_sems.at[i])
                 for i, (src, dst) in enumerate(zip([part_hbm, *landed_hbm], [part_ref, *landed_refs]))]

        @pl.when(flat == 0)
        def _():
            swap()
            for load in loads:
                load.start()

        @pl.when(flat == min(2, total - 1))
        def _():
            loads[0].wait()
            send()

        @pl.when(flat == min(total // 2 + 1, total - 1))
        def _():
            forward()
            for t in range(nl):
                loads[1 + t].wait()
                _sum_slots(landed_refs[t], summed_refs[t])

        @pl.when(flat == min(ns + 1, total - 1))
        def _():
            send_first()

        @pl.when(flat == min(ns + ns // 2 + 1, total - 1))
        def _():
            forward_first()

        @pl.when(s == 0)
        def _():
            acc_ref[...] = jnp.zeros_like(acc_ref)

        for t in range(n_tasks):
            @pl.when(j == t)
            def _(t=t):
                acc_ref[...] += _dot_tn(ab_refs[2 * t][...], ab_refs[2 * t + 1][...])

        @pl.when(flat == ns - 1)
        def _():
            for dev in range(N_DEV):
                first_ref[dev] = acc_ref[dev * per:(dev + 1) * per, :].astype(BF16)
            swap_first()

        @pl.when(flat == total - 1)
        def _():
            for dev in range(N_DEV):
                o_ref[dev] = acc_ref[dev * per:(dev + 1) * per, :].astype(BF16)
            finish()
            finish_first()

    slot = part.shape[1:]
    outs = pl.pallas_call(
        body,
        name="wgrad_tail",
        grid=(n_tasks, ns),
        in_specs=[spec(t, w) for t in range(n_tasks) for w in (k, n)] + [HBM_SPEC] * (1 + nl),
        out_specs=[_resident((N_DEV, per, n)), _resident(slot), _resident((per, n))]
        + [_resident(g.shape[1:]) for g in landed],
        out_shape=[jax.ShapeDtypeStruct((N_DEV, per, n), BF16), jax.ShapeDtypeStruct(slot, F32),
                   jax.ShapeDtypeStruct((per, n), F32)]
        + [jax.ShapeDtypeStruct(g.shape[1:], F32) for g in landed],
        scratch_shapes=[pltpu.VMEM((k, n), F32), pltpu.VMEM((N_DEV, per, n), BF16), pltpu.VMEM(part.shape, part.dtype)]
        + [pltpu.VMEM(g.shape, g.dtype) for g in landed] + [pltpu.SemaphoreType.DMA((1 + nl,))]
        + _chip_reduce_scratch(slot) + _chip_reduce_scratch((per, n)),
        compiler_params=_params(("arbitrary", "arbitrary")),
    )(*[op for pair in pairs for op in pair], part, *landed)
    return outs[0], outs[1], outs[2], outs[3:]


MINE = "mine"
ADAMW_STEPS = 4


def _adamw(ws, sources, picks, loss_at, ms, vs):
    n, n_src = len(ws), len(sources)
    streamed = [len(w.shape) == 2 and w.shape[0] >= 128 and picks[t][1:] == (0, None)
                and sources[picks[t][0]].shape == w.shape for t, w in enumerate(ws)]
    streamed_sources = {picks[t][0] for t in range(n) if streamed[t]}

    def step(w, g, m, v):
        m = ADAM_B1 * m + (1.0 - ADAM_B1) * g
        v = ADAM_B2 * v + (1.0 - ADAM_B2) * jnp.square(g)
        m_hat = m / (1.0 - ADAM_B1 ** ADAM_STEP)
        v_hat = v / (1.0 - ADAM_B2 ** ADAM_STEP)
        return g, -ADAM_LR * (m_hat / (jnp.sqrt(v_hat) + ADAM_EPS) + ADAM_WD * w), m, v

    def body(*refs):
        refs = list(refs)
        take = lambda k: [refs.pop(0) for _ in range(k)]
        w_refs, s_refs, m_refs, v_refs = take(n), take(n_src), take(n), take(n)
        (loss_ref,), go_refs, d_refs, nm_refs, nv_refs = take(1), take(n), take(n), take(n), take(n)
        me = _my_index()

        def grad(t, rows):
            k, first, cols = picks[t]
            if cols is None:
                return s_refs[k][rows, :]
            if cols is not MINE:
                return s_refs[k][rows, cols]
            width = w_refs[t].shape[-1]
            g = s_refs[k][rows, 0:width]
            for dev in range(1, N_DEV):
                g = jnp.where(me == dev, s_refs[k][rows, dev * width:(dev + 1) * width], g)
            return g

        def whole(t):
            first = picks[t][1]
            rows = w_refs[t].shape[0]
            if len(w_refs[t].shape) == 3:
                for j in range(rows):
                    go_refs[t][j], d_refs[t][j], nm_refs[t][j], nv_refs[t][j] = step(
                        w_refs[t][j], grad(t, slice(first + j, first + j + 1)), m_refs[t][j], v_refs[t][j])
                return
            go_refs[t][...], d_refs[t][...], nm_refs[t][...], nv_refs[t][...] = step(
                w_refs[t][...], grad(t, slice(first, first + rows)), m_refs[t][...], v_refs[t][...])

        def block(t):
            rows = w_refs[t].shape[0]
            chunk = min(rows, 128)

            def one(i, carry):
                r = pl.ds(pl.multiple_of(i * chunk, chunk), chunk)
                go_refs[t][r, :], d_refs[t][r, :], nm_refs[t][r, :], nv_refs[t][r, :] = step(
                    w_refs[t][r, :], grad(t, r), m_refs[t][r, :], v_refs[t][r, :])
                return carry

            lax.fori_loop(0, rows // chunk, one, 0)

        @pl.when(pl.program_id(0) == 0)
        def _():
            loss_ref[...] = s_refs[loss_at[0]][loss_at[1]:loss_at[1] + 1, 0:1]
            for t in range(n):
                if not streamed[t]:
                    whole(t)

        for t in range(n):
            if streamed[t]:
                block(t)

    def rows_of(shape):
        return pl.BlockSpec((shape[0] // ADAMW_STEPS, shape[1]), lambda i: (i, 0))

    w_in = [rows_of(w.shape) if streamed[t] else _full(w.shape) for t, w in enumerate(ws)]
    w_out = [rows_of(w.shape) if streamed[t] else _resident(w.shape) for t, w in enumerate(ws)]
    s_in = [rows_of(s.shape) if k in streamed_sources else _full(s.shape) for k, s in enumerate(sources)]
    outs = pl.pallas_call(
        body,
        name="adamw",
        grid=(ADAMW_STEPS,),
        in_specs=w_in + s_in + w_in * 2,
        out_specs=[_resident((1, 1))] + w_out * 4,
        out_shape=[jax.ShapeDtypeStruct((1, 1), F32)] + [jax.ShapeDtypeStruct(w.shape, F32) for w in ws] * 4,
        compiler_params=_params(("arbitrary",)),
    )(*ws, *sources, *ms, *vs)
    return outs[0], outs[1:n + 1], outs[n + 1:2 * n + 1], outs[2 * n + 1:3 * n + 1], outs[3 * n + 1:]


def _band_structure():
    q_loc = np.arange(BLOCK, dtype=np.int32)[:, None]
    s_loc = np.arange(2 * BLOCK, dtype=np.int32)[None, :]
    dist = q_loc + BLOCK - s_loc
    in_window = (dist >= 0) & (dist < BLOCK)
    dd = np.maximum(dist, 0)
    max_exact = N_BUCKETS // 2
    large = max_exact + (np.log(np.maximum(dd, 1) / max_exact) / math.log(MAX_DISTANCE / max_exact)
                         * (N_BUCKETS - max_exact)).astype(np.int32)
    bucket = np.where(dd < max_exact, dd, np.minimum(large, N_BUCKETS - 1)).astype(np.int32)
    return bucket, in_window.astype(np.int32)


def kernel(x, a_pre_norm, a_w_in, a_conv_w, a_w_out, a_post_norm, kv_norm, w_kv, rel_bias, b_pre_norm, b_w_in, b_sinks, b_w_out, b_post_norm, loss_target, m_a_pre_norm, m_a_w_in, m_a_conv_w, m_a_w_out, m_a_post_norm, m_kv_norm, m_w_kv, m_rel_bias, m_b_pre_norm, m_b_w_in, m_b_sinks, m_b_w_out, m_b_post_norm, v_a_pre_norm, v_a_w_in, v_a_conv_w, v_a_w_out, v_a_post_norm, v_kv_norm, v_w_kv, v_rel_bias, v_b_pre_norm, v_b_w_in, v_b_sinks, v_b_w_out, v_b_post_norm):
    seq, d = x.shape[1], x.shape[2]
    x2 = x.reshape(seq, d)
    target = loss_target.reshape(seq, d)
    shard = a_pre_norm.shape[1]
    ts_a = min(seq, 512)
    ts = min(seq, 512)
    ts_w = min(seq, 2048)

    taps = lambda a: a.transpose(1, 0, 2)
    bucket, in_window = _band_structure()
    (win_g, wout_g), small_g, later, biasm = _all_gather(
        [a_w_in[0], a_w_out[0]], [(0, a_pre_norm), (1, taps(a_conv_w)), (4, a_post_norm)],
        [w_kv, b_w_in[0], b_w_out[0]], rel_bias.T, bucket.T, in_window.T)
    wout = wout_g.reshape(-1, wout_g.shape[2])
    sm = small_g.transpose(1, 0, 2).reshape(8, N_DEV * shard)
    kvn = kv_norm.reshape(1, d)

    (h1, n1, proj, conv, y, ya), (wkv_g, wbin_g, wbout_g) = _layer_a_fwd(x2, sm, win_g, wout, later, ts_a)
    wkv = wkv_g.reshape(-1, wkv_g.shape[2])
    wbout = wbout_g.reshape(-1, wbout_g.shape[2])
    n3, n4, kv, q, o, dh2, dyb, dattn, dz2, acc_c = _layer_b_fwd(
        h1, target, kvn, b_pre_norm, wkv, wbin_g, biasm, b_sinks, wbout, b_post_norm)

    (dq, dkv, dssum, dsink), _ = _attn_bwd(q, kv, dattn, biasm, b_sinks, [])
    by_head = dssum.reshape(N_PAIRS, BAND, 2, BLOCK).transpose(0, 2, 3, 1)
    g_wkv = _wgrad(n3, [dkv], 0, ts_w, "wgrad_kv").reshape(wkv_g.shape)
    g_wbin = _wgrad(n4, [dq, dz2], N_DEV, ts_w, "wgrad_b_in")
    (dh1, dya, acc_b), _ = _layer_b_in_bwd(dh2, dq, dz2, dkv, h1, ya, wbin_g, wkv, kvn, b_pre_norm, sm, [], ts)
    dproj, gx, acc_a = _layer_a_bwd(dya, proj, conv, dh1, x2, wout, win_g, sm, ts_a)
    g_win, (l_wkv, l_wbin) = _wgrad(
        n1, [dproj], N_DEV, ts_w, "wgrad_a_in", ready=[g_wkv, g_wbin], block_cols=2048)
    g_wbout, r_win, r_wout, (r_wkv, r_wbin) = _wgrad_tail(
        [(y, dya), (o, dyb)], g_win, [l_wkv, l_wbin], min(seq, 1024))

    r_wbout, _, (s_a, s_b, s_c, s_sink), s_relb = _reduce_exchange(
        g_wbout, [], [acc_a, acc_b, acc_c, dsink], by_head.reshape(N_Q_HEADS, -1), bucket.reshape(1, -1), 4096)
    weights = [a_pre_norm, a_w_in[0], taps(a_conv_w), a_w_out[0], a_post_norm, kvn, w_kv, rel_bias.T, b_pre_norm,
               b_w_in[0], b_sinks, b_w_out[0], b_post_norm]
    sources = [s_a, s_b, s_c, s_relb, s_sink, r_win, r_wout, r_wkv, r_wbin, r_wbout]
    picks = [(0, 0, MINE), (5, 0, None), (0, 1, MINE), (6, 0, None), (1, 2, MINE), (1, 1, None), (7, 0, None),
             (3, 0, slice(0, N_BUCKETS)), (1, 0, None), (8, 0, None), (4, 0, slice(0, N_Q_HEADS)),
             (9, 0, None), (2, 0, None)]
    first = [m_a_pre_norm, m_a_w_in[0], taps(m_a_conv_w), m_a_w_out[0], m_a_post_norm, m_kv_norm.reshape(1, d),
             m_w_kv, m_rel_bias.T, m_b_pre_norm, m_b_w_in[0], m_b_sinks, m_b_w_out[0], m_b_post_norm]
    second = [v_a_pre_norm, v_a_w_in[0], taps(v_a_conv_w), v_a_w_out[0], v_a_post_norm, v_kv_norm.reshape(1, d),
              v_w_kv, v_rel_bias.T, v_b_pre_norm, v_b_w_in[0], v_b_sinks, v_b_w_out[0], v_b_post_norm]
    loss, grads, deltas, new_m, new_v = _adamw(weights, sources, picks, (2, 1), first, second)

    shapes = [a_pre_norm.shape, a_w_in.shape, taps, a_w_out.shape, a_post_norm.shape, kv_norm.shape,
              w_kv.shape, jnp.transpose, b_pre_norm.shape, b_w_in.shape, b_sinks.shape, b_w_out.shape, b_post_norm.shape]
    shaped = lambda arrays: [s(a) if callable(s) else a.reshape(s) for a, s in zip(arrays, shapes)]
    return (loss.reshape(()), gx.reshape(x.shape), *shaped(grads), *shaped(deltas), *shaped(new_m), *shaped(new_v))
```

```python
import math

import jax
import jax.numpy as jnp
import numpy as np
from jax import lax
from jax.experimental import pallas as pl
from jax.experimental.pallas import tpu as pltpu

HEAD_DIM = 64
N_Q_HEADS = 16
N_KV_HEADS = 2
GROUP = N_Q_HEADS // N_KV_HEADS
BLOCK = 128
N_BUCKETS = 32
MAX_DISTANCE = 128
EPS = 1e-6
NEG_INF = -1e30
SCALE = HEAD_DIM ** -0.5

ADAM_LR = 0.001
ADAM_B1 = 0.9
ADAM_B2 = 0.999
ADAM_EPS = 1e-08
ADAM_WD = 0.01
ADAM_STEP = 10

N_PAIRS = N_Q_HEADS // 2
BAND = 2 * BLOCK

N_DEV = 8
GATHER_PIECE_ROWS = 256
LANES = 128
F32 = jnp.float32
BF16 = jnp.bfloat16
MESH = pl.DeviceIdType.MESH
MIB = 1024 * 1024
VMEM_RESERVED_MIB = 63


def _params(semantics=None, fusible_inputs=None):
    return pltpu.CompilerParams(dimension_semantics=semantics, vmem_limit_bytes=VMEM_RESERVED_MIB * MIB,
                                allow_input_fusion=fusible_inputs)


def _full(shape):
    zeros = (0,) * len(shape)
    return pl.BlockSpec(shape, lambda *_: zeros, pipeline_mode=pl.Buffered(1))


def _resident(shape):
    zeros = (0,) * len(shape)
    return pl.BlockSpec(shape, lambda *_: zeros)


def _rows(ts, cols):
    return pl.BlockSpec((ts, cols), lambda i: (i, 0))


def _dot(a, b):
    return jnp.dot(a, b, preferred_element_type=F32)


def _dot_nt(a, b):
    return lax.dot_general(a, b, (((1,), (1,)), ((), ())), preferred_element_type=F32)


def _dot_tn(a, b):
    return lax.dot_general(a, b, (((0,), (0,)), ((), ())), preferred_element_type=F32)


def _rms(xf):
    r = lax.rsqrt(jnp.mean(xf * xf, axis=-1, keepdims=True) + EPS)
    return xf * r, r


def _rms_bwd(dn, xn, r):
    return r * (dn - xn * jnp.mean(dn * xn, axis=-1, keepdims=True))


def _silu(z):
    s = jax.nn.sigmoid(z)
    return z * s, s * (1.0 + z * (1.0 - s))


def _my_index():
    return 4 * lax.axis_index("x") + 2 * lax.axis_index("y") + lax.axis_index("c")


def _bias_table(rb_ref, bucket_ref, win_ref, out_ref):
    bk = jnp.where(win_ref[...] != 0, bucket_ref[...], -1)
    has_prev = lax.broadcasted_iota(jnp.int32, bk.shape, 0) >= BLOCK
    for h in range(N_Q_HEADS):
        acc = jnp.full(bk.shape, NEG_INF, F32)
        for b in range(N_BUCKETS):
            acc = jnp.where(bk == b, rb_ref[h, b], acc)
        cols = slice((h % 2) * BLOCK, (h % 2 + 1) * BLOCK)
        out_ref[1, h // 2, :, cols] = acc
        out_ref[0, h // 2, :, cols] = jnp.where(has_prev, acc, NEG_INF)


def _all_gather(shards, small_rows, casts, rel_bias_t, bucket_t, in_window_t):
    ns, nc, n = len(small_rows), len(casts), len(shards) + 1
    small_shape = (8, small_rows[0][1].shape[-1])
    shapes = [s.shape for s in shards] + [small_shape]
    pieces = [(t, r0, min(GATHER_PIECE_ROWS, shape[0] - r0))
              for t, shape in enumerate(shapes) for r0 in range(0, shape[0], GATHER_PIECE_ROWS)]

    def body(*refs):
        refs = list(refs)
        take = lambda k: [refs.pop(0) for _ in range(k)]
        ins, small_refs, cast_refs, (rb_ref, bucket_ref, win_ref) = take(n - 1), take(ns), take(nc), take(3)
        outs, cast_outs, (bias_ref, send_sems, recv_sems) = take(n), take(nc), take(3)
        x, y, c = lax.axis_index("x"), lax.axis_index("y"), lax.axis_index("c")
        me, sibling = (x, y, c), (x, y, 1 - c)
        x_nbr, y_nbr, diagonal = (1 - x, y), (x, 1 - y), (1 - x, 1 - y)
        south = c == 0
        relayed = (jnp.where(south, 1 - x, x), jnp.where(south, y, 1 - y))
        relay_to = (jnp.where(south, x, 1 - x), jnp.where(south, 1 - y, y))

        def copy(u, k, block, to):
            t, r0, nrows = pieces[u]
            rows = outs[t].at[4 * block[0] + 2 * block[1] + block[2], pl.ds(r0, nrows)]
            return pltpu.make_async_remote_copy(
                src_ref=rows, dst_ref=rows, send_sem=send_sems.at[u, k], recv_sem=recv_sems.at[u, k],
                device_id=to, device_id_type=MESH)

        mine = pl.ds(_my_index(), 1)
        for t in range(n - 1):
            outs[t][mine] = ins[t][...].astype(BF16)[None]
        outs[n - 1][mine] = jnp.zeros((1,) + small_shape, F32)
        for (row, _), ref in zip(small_rows, small_refs):
            if len(ref.shape) == 3:
                for j in range(ref.shape[0]):
                    outs[n - 1][mine, row + j:row + j + 1, :] = ref[j][None]
            else:
                outs[n - 1][mine, row:row + ref.shape[0], :] = ref[...][None]
        started = []

        def start(cp):
            cp.start()
            started.append(cp)

        units = range(len(pieces))
        for u in units:
            start(copy(u, 0, me, sibling))
            start(copy(u, 1, me, (*x_nbr, c)))
            start(copy(u, 2, me, (*y_nbr, c)))
        for u in units:
            for k, chip in ((1, x_nbr), (2, y_nbr)):
                copy(u, k, (*chip, c), me).wait_recv()
                start(copy(u, 3 + k, (*chip, c), sibling))
            start(copy(u, 3, (*relayed, c), (*relay_to, c)))
        for src, dst in zip(cast_refs, cast_outs):
            dst[...] = src[...].astype(BF16)
        _bias_table(rb_ref, bucket_ref, win_ref, bias_ref)
        for u in units:
            copy(u, 3, (*diagonal, c), me).wait_recv()
            start(copy(u, 6, (*diagonal, c), sibling))
        for u in units:
            copy(u, 0, sibling, me).wait_recv()
        for k, chip in ((4, x_nbr), (5, y_nbr), (6, diagonal)):
            for u in units:
                copy(u, k, (*chip, 1 - c), me).wait_recv()
        for cp in started:
            cp.wait_send()

    vmem = pl.BlockSpec(memory_space=pltpu.VMEM)
    outs = pl.pallas_call(
        body,
        name="gather_weights",
        out_shape=[jax.ShapeDtypeStruct((N_DEV,) + s.shape, BF16) for s in shards]
        + [jax.ShapeDtypeStruct((N_DEV,) + small_shape, F32)]
        + [jax.ShapeDtypeStruct(a.shape, BF16) for a in casts]
        + [jax.ShapeDtypeStruct((2, N_PAIRS, BAND, 2 * BLOCK), F32)],
        in_specs=[vmem] * (n - 1 + ns + nc) + [pl.BlockSpec(memory_space=pltpu.SMEM), vmem, vmem],
        out_specs=[vmem] * (n + nc + 1),
        scratch_shapes=[pltpu.SemaphoreType.DMA((len(pieces), 7)), pltpu.SemaphoreType.DMA((len(pieces), 7))],
        compiler_params=_params(),
    )(*shards, *[a for _, a in small_rows], *casts, rel_bias_t, bucket_t, in_window_t)
    return outs[:n - 1], outs[n - 1], outs[n:n + nc], outs[n + nc]


def _peer(k):
    x, y, c = lax.axis_index("x"), lax.axis_index("y"), lax.axis_index("c")
    px = 1 - x if k & 4 else x
    py = 1 - y if k & 2 else y
    pc = 1 - c if k & 1 else c
    return (px, py, pc), 4 * px + 2 * py + pc


def _exchange(srcs, dsts, send_sems, recv_sems, local_sems, scatter):
    me = _my_index()
    sends, arrivals = [], []
    for k in range(1, N_DEV):
        peer, pidx = _peer(k)
        for t, (src, dst) in enumerate(zip(srcs, dsts)):
            mine = src.at[pidx] if scatter else src
            sems = dict(send_sem=send_sems.at[t, k - 1], recv_sem=recv_sems.at[t, k - 1], device_id=peer, device_id_type=MESH)
            sends.append(pltpu.make_async_remote_copy(src_ref=mine, dst_ref=dst.at[me], **sems))
            arrivals.append(pltpu.make_async_remote_copy(src_ref=mine, dst_ref=dst.at[pidx], **sems))
    local = [pltpu.make_async_copy(src.at[me] if scatter else src, dst.at[me], local_sems.at[t])
             for t, (src, dst) in enumerate(zip(srcs, dsts))]
    return sends, arrivals, local


def _exchange_start(*args):
    sends, _, local = _exchange(*args)
    for cp in sends + local:
        cp.start()


def _exchange_wait(*args):
    sends, arrivals, local = _exchange(*args)
    for cp in arrivals:
        cp.wait_recv()
    for cp in sends:
        cp.wait_send()
    for cp in local:
        cp.wait()


def _exchange_sems(n):
    if not n:
        return []
    return [pltpu.SemaphoreType.DMA((n, N_DEV - 1)), pltpu.SemaphoreType.DMA((n, N_DEV - 1)), pltpu.SemaphoreType.DMA((n,))]


HBM_SPEC = pl.BlockSpec(memory_space=pl.ANY)


def _sum_slots(recv_ref, out_ref):
    rows = out_ref.shape[0]
    chunk = min(rows, 128)

    def add(i, carry):
        r0 = pl.multiple_of(i * chunk, chunk)
        acc = recv_ref[0, pl.ds(r0, chunk), :].astype(F32)
        for dev in range(1, N_DEV):
            acc = acc + recv_ref[dev, pl.ds(r0, chunk), :].astype(F32)
        out_ref[pl.ds(r0, chunk), :] = acc
        return carry

    lax.fori_loop(0, rows // chunk, add, 0)


N_CHIPS = N_DEV // 2


def _rows_loop(rows, fn):
    chunk = min(rows, 128)

    def step(i, carry):
        fn(pl.ds(pl.multiple_of(i * chunk, chunk), chunk))
        return carry

    lax.fori_loop(0, rows // chunk, step, 0)


def _chip_reduce(g_ref, out_ref, sib_ref, land_ref, send_ref, sems, swap_src=None):
    sib_send, sib_recv, ici_send, ici_recv = sems
    x, y, c = lax.axis_index("x"), lax.axis_index("y"), lax.axis_index("c")
    south = c == 0
    near =(jnp.where(south, 1 - x, x), jnp.where(south, y, 1 - y))
    far = (jnp.where(south, x, 1 - x), jnp.where(south, 1 - y, y))
    diagonal = (1 - x, 1 - y)
    rows = out_ref.shape[0]
    direct, fold, folded = 0, 1, 2

    def to_sibling(t):
        src = g_ref if swap_src is None else swap_src
        return pltpu.make_async_remote_copy(
            src_ref=src.at[2 * t + 1 - c], dst_ref=sib_ref.at[t], send_sem=sib_send.at[t], recv_sem=sib_recv.at[t],
            device_id=(x, y, 1 - c), device_id_type=MESH)

    def ici(role, chip):
        return pltpu.make_async_remote_copy(
            src_ref=send_ref.at[role], dst_ref=land_ref.at[role], send_sem=ici_send.at[role],
            recv_sem=ici_recv.at[role], device_id=(*chip, c), device_id_type=MESH)

    def pair_sum(chip, r):
        t = 2 * chip[0] + chip[1]
        return g_ref[2 * t + c, r, :].astype(F32) + sib_ref[t, r, :].astype(F32)

    def swap():
        for t in range(N_CHIPS):
            to_sibling(t).start()

    def send():
        for t in range(N_CHIPS):
            to_sibling(t).wait_recv()
        for role, chip in ((fold, diagonal), (direct, near)):
            def fill(r, role=role, chip=chip):
                send_ref[role, r, :] = pair_sum(chip, r).astype(BF16)

            _rows_loop(rows, fill)
            ici(role, near).start()

    def forward():
        ici(fold, near).wait_recv()

        def fill(r):
            send_ref[folded, r, :] = (pair_sum(far, r) + land_ref[fold, r, :].astype(F32)).astype(BF16)

        _rows_loop(rows, fill)
        ici(folded, far).start()

    def finish():
        ici(direct, near).wait_recv()
        ici(folded, far).wait_recv()

        def total(r):
            mine = pair_sum((x, y), r)
            out_ref[r, :] = mine + land_ref[direct, r, :].astype(F32) + land_ref[folded, r, :].astype(F32)

        _rows_loop(rows, total)
        for t in range(N_CHIPS):
            to_sibling(t).wait_send()
        for role, chip in ((direct, near), (fold, near), (folded, far)):
            ici(role, chip).wait_send()

    return swap, send, forward, finish


def _chip_reduce_scratch(slot):
    return [pltpu.VMEM((N_CHIPS,) + slot, BF16), pltpu.VMEM((3,) + slot, BF16), pltpu.VMEM((3,) + slot, BF16),
            pltpu.SemaphoreType.DMA((N_CHIPS,)), pltpu.SemaphoreType.DMA((N_CHIPS,)),
            pltpu.SemaphoreType.DMA((3,)), pltpu.SemaphoreType.DMA((3,))]


def _bucket_sums(a_ref, bucket_ref, cols):
    a = a_ref[:, cols]
    hi = a.astype(BF16)
    lo = (a - hi.astype(F32)).astype(BF16)
    rows = lax.broadcasted_iota(jnp.int32, (LANES, a.shape[1]), 0)
    onehot_t = (rows == bucket_ref[:, cols]).astype(F32).astype(BF16)
    return _dot_nt(hi, onehot_t) + _dot_nt(lo, onehot_t)


def _reduce_exchange(part, landed, smalls, by_bucket, bucket_row, chunk):
    nl, ng = len(landed), len(smalls)
    n_in = 1 + nl + ng + 2
    n_out = 1 + nl + ng + 1
    heads, positions = by_bucket.shape
    chunks = [slice(c0, c0 + chunk) for c0 in range(0, positions, chunk)]

    def body(*refs):
        p_in, l_in, s_in, (a_ref, bucket_ref) = refs[0], refs[1:1 + nl], refs[1 + nl:n_in - 2], refs[n_in - 2:n_in]
        refs = refs[n_in:]
        p_out, l_out, s_out, b_out = refs[0], refs[1:1 + nl], refs[1 + nl:n_out - 1], refs[n_out - 1]
        scratch = refs[n_out:]
        s_recv, (b_recv, b_ref, sib_ref, chip_ref, send_ref), sems = scratch[:ng], scratch[ng:ng + 5], scratch[ng + 5:]
        swap, send, forward, finish = _chip_reduce(p_in, p_out, sib_ref, chip_ref, send_ref, sems[:4])
        swap()
        _exchange_start(s_in, s_recv, *sems[4:7], False)
        b_ref[...] = jnp.zeros_like(b_ref)
        for cols in chunks[:len(chunks) // 2]:
            b_ref[...] += _bucket_sums(a_ref, bucket_ref, cols)
        send()
        for t in range(nl):
            _sum_slots(l_in[t], l_out[t])
        for cols in chunks[len(chunks) // 2:]:
            b_ref[...] += _bucket_sums(a_ref, bucket_ref, cols)
        _exchange_start([b_ref], [b_recv], *sems[7:], False)
        forward()
        finish()
        _exchange_wait(s_in, s_recv, *sems[4:7], False)
        _exchange_wait([b_ref], [b_recv], *sems[7:], False)
        for recv, out in zip([*s_recv, b_recv], [*s_out, b_out]):
            acc = recv[0]
            for dev in range(1, N_DEV):
                acc = acc + recv[dev]
            out[...] = acc

    vmem = pl.BlockSpec(memory_space=pltpu.VMEM)
    slot = part.shape[1:]
    outs = pl.pallas_call(
        body,
        name="reduce_grads",
        out_shape=[jax.ShapeDtypeStruct(p.shape[1:], F32) for p in [part] + landed]
        + [jax.ShapeDtypeStruct(s.shape, F32) for s in smalls] + [jax.ShapeDtypeStruct((heads, LANES), F32)],
        in_specs=[vmem] * n_in,
        out_specs=[vmem] * n_out,
        scratch_shapes=[pltpu.VMEM((N_DEV,) + s.shape, F32) for s in smalls]
        + [pltpu.VMEM((N_DEV, heads, LANES), F32), pltpu.VMEM((heads, LANES), F32)] + _chip_reduce_scratch(slot)
        + _exchange_sems(ng) + _exchange_sems(1),
        compiler_params=_params(fusible_inputs=[False] * (n_in - 2) + [True, False]),
    )(part, *landed, *smalls, by_bucket, bucket_row)
    return outs[0], outs[1:1 + nl], outs[1 + nl:n_out - 1], outs[n_out - 1]


def _layer_a_fwd(x2, sm, win_g, wout, later, ts):
    seq, d = x2.shape
    width = wout.shape[0]
    half = win_g.shape[2]
    n_half = width // half
    nl = len(later)
    nt = seq // ts

    def body(x_ref, sm_ref, win_ref, wout_ref, *refs):
        shard_refs, refs = refs[:nl], refs[nl:]
        h1_ref, n1_ref, proj_ref, conv_ref, y_ref, ya_ref = refs[:6]
        gathered_refs, (vprev_ref, *sems) = refs[6:6 + nl], refs[6 + nl:]

        @pl.when(pl.program_id(0) == 0)
        def _():
            vprev_ref[...] = jnp.zeros_like(vprev_ref)
            _exchange_start(shard_refs, gathered_refs, *sems, False)

        @pl.when(pl.program_id(0) == nt - 1)
        def _():
            _exchange_wait(shard_refs, gathered_refs, *sems, False)

        xf = x_ref[...]
        xn, _ = _rms(xf)
        n1 = (xn * sm_ref[0:1, :]).astype(BF16)
        n1_ref[...] = n1
        row = lax.broadcasted_iota(jnp.int32, (ts, half), 0)
        ya = jnp.zeros((ts, d), F32)
        for hh in range(n_half):
            cols = slice(hh * half, (hh + 1) * half)
            parts = []
            for part in range(4):
                j = part * n_half + hh
                pj = _dot(n1, win_ref[j])
                proj_ref[:, j * half:(j + 1) * half] = pj.astype(BF16)
                parts.append(pj)
            b, c, u, z = parts
            v = c * u
            last1, last2 = vprev_ref[7:8, cols], vprev_ref[6:7, cols]
            v1 = jnp.where(row == 0, last1, pltpu.roll(v, 1, 0))
            v2 = jnp.where(row == 0, last2, jnp.where(row == 1, last1, pltpu.roll(v, 2, 0)))
            vprev_ref[:, cols] = v[ts - 8:ts, :]
            conv = sm_ref[1:2, cols] * v2 + sm_ref[2:3, cols] * v1 + sm_ref[3:4, cols] * v
            conv_ref[:, cols] = conv.astype(BF16)
            yh = (b * conv * _silu(z)[0]).astype(BF16)
            y_ref[:, cols] = yh
            ya = ya + _dot(yh, wout_ref[cols, :])
        ya_ref[...] = ya
        h1_ref[...] = xf + _rms(ya)[0] * sm_ref[4:5, :]

    outs = pl.pallas_call(
        body,
        name="layer_a_fwd",
        grid=(nt,),
        in_specs=[_rows(ts, d), _full(sm.shape), _full(win_g.shape), _full(wout.shape)] + [HBM_SPEC] * nl,
        out_specs=[_rows(ts, d), _rows(ts, d), _rows(ts, 4 * width), _rows(ts, width), _rows(ts, width), _rows(ts, d)]
        + [HBM_SPEC] * nl,
        out_shape=[
            jax.ShapeDtypeStruct((seq, d), F32),
            jax.ShapeDtypeStruct((seq, d), BF16),
            jax.ShapeDtypeStruct((seq, 4 * width), BF16),
            jax.ShapeDtypeStruct((seq, width), BF16),
            jax.ShapeDtypeStruct((seq, width), BF16),
            jax.ShapeDtypeStruct((seq, d), F32),
        ] + [jax.ShapeDtypeStruct((N_DEV,) + s.shape, s.dtype) for s in later],
        scratch_shapes=[pltpu.VMEM((8, width), F32)] + _exchange_sems(nl),
        compiler_params=_params(("arbitrary",)),
    )(x2, sm, win_g, wout, *later)
    return outs[:6], outs[6:]


Q_BLOCKS = 4
ATTN_BWD_LAGS = (2, 4)
ATTN_FWD_LAGS = (2, 4)


def _banded_tiles(kvp_ref, kvc_ref):
    tile = kvc_ref[...].astype(F32)
    blocks = [kvp_ref[...].astype(F32)] + [tile[u * BLOCK:(u + 1) * BLOCK] for u in range(Q_BLOCKS)]
    return [_banded_kv(blocks[u], blocks[u + 1]) for u in range(Q_BLOCKS)]


def _bias_of(bias_ref, i, u, m):
    return bias_ref[jnp.minimum(i, 1) if u == 0 else 1, m]


def _banded_kv(kvp, kvc):
    kw = N_KV_HEADS * HEAD_DIM
    out = []
    for full in (jnp.concatenate([kvp[:, :kw], kvc[:, :kw]], axis=0), jnp.concatenate([kvp[:, kw:], kvc[:, kw:]], axis=0)):
        lo = lax.broadcasted_iota(jnp.int32, full.shape, 1) < HEAD_DIM
        rolled = pltpu.roll(full, HEAD_DIM, 1)
        x2 = [jnp.where(lo, full, rolled).astype(BF16), jnp.where(lo, rolled, full).astype(BF16)]
        ft = full.T
        x2t = [jnp.concatenate([ft[kh * HEAD_DIM:(kh + 1) * HEAD_DIM]] * 2, axis=0).astype(BF16) for kh in range(N_KV_HEADS)]
        out += [x2, x2t]
    return out


def _pair_rows(ref, rows, m, scale=None):
    both = ref[rows, m * LANES:(m + 1) * LANES].astype(F32)
    if scale is not None:
        both = both * scale
    lo = lax.broadcasted_iota(jnp.int32, both.shape, 1) < HEAD_DIM
    zero = jnp.zeros_like(both)
    return jnp.concatenate([jnp.where(lo, both, zero), jnp.where(lo, zero, both)], axis=0).astype(BF16)


def _pair_cols(res_t):
    top = lax.broadcasted_iota(jnp.int32, (LANES, BLOCK), 0) < HEAD_DIM
    return jnp.where(top, res_t[:, :BLOCK], res_t[:, BLOCK:]).T


def _sink_row(sink_ref, m):
    first = lax.broadcasted_iota(jnp.int32, (1, 2 * BLOCK), 1) < BLOCK
    return jnp.where(first, sink_ref[0, 2 * m], sink_ref[0, 2 * m + 1])


def _softmax_t(logits, sink):
    mx =jnp.maximum(jnp.max(logits, axis=0, keepdims=True), sink)
    p = jnp.exp(logits - mx)
    sink_p = jnp.exp(sink - mx)
    inv = 1.0 / (jnp.sum(p, axis=0, keepdims=True) + sink_p)
    return p * inv, sink_p * inv


def _layer_b_fwd(h1, target, kvn, bpre, wkv, wbin_g, biasm, sinks, wbout, bpost):
    seq, d = h1.shape
    kvw = wkv.shape[1]
    cw = wbin_g.shape[2]
    aw = N_Q_HEADS * HEAD_DIM
    per = aw // cw
    tile = Q_BLOCKS * BLOCK

    def body(sink_ref, h1_ref, tgt_ref, kvn_ref, bpre_ref, wkv_ref, wbin_ref, bias_ref, w_ref, g_ref,
             n3_ref, n4_ref, kvc_ref, q_ref, o_ref, dh2_ref, dyb_ref, dattn_ref, dz2_ref, acc_ref,
             attn_ref, z2_ref, kvp_ref):
        i = pl.program_id(0)

        @pl.when(i == 0)
        def _():
            acc_ref[...] = jnp.zeros_like(acc_ref)
            kvp_ref[...] = jnp.zeros_like(kvp_ref)

        hn, _ = _rms(h1_ref[...])
        n3 = (hn * kvn_ref[...]).astype(BF16)
        n4 = (hn * bpre_ref[...]).astype(BF16)
        n3_ref[...] = n3
        n4_ref[...] = n4
        kvc_ref[...] = _dot(n3, wkv_ref[...]).astype(BF16)
        for j in range(N_DEV):
            pj = _dot(n4, wbin_ref[j])
            if j < per:
                q_ref[:, j * cw:(j + 1) * cw] = pj.astype(BF16)
            else:
                z2_ref[:, (j - per) * cw:(j - per + 1) * cw] = pj

        banded = _banded_tiles(kvp_ref, kvc_ref)
        kvp_ref[...] = kvc_ref[tile - BLOCK:tile, :]
        units = [(u, m) for u in range(Q_BLOCKS) for m in range(N_PAIRS)]
        kv_of = lambda m: (2 * m) // GROUP
        logits, probs = {}, {}
        lag_b, lag_c = ATTN_FWD_LAGS
        for step in range(len(units) + lag_c):
            if step < len(units):
                u, m = units[step]
                qpair = _pair_rows(q_ref, slice(u * BLOCK, (u + 1) * BLOCK), m, SCALE)
                logits[step] = _dot_nt(banded[u][0][kv_of(m)], qpair) + _bias_of(bias_ref, i, u, m)
            if 0 <= step - lag_b < len(units):
                u, m = units[step - lag_b]
                probs[step - lag_b] = _softmax_t(logits.pop(step - lag_b), _sink_row(sink_ref, m))[0].astype(BF16)
            if 0 <= step - lag_c < len(units):
                u, m = units[step - lag_c]
                out_t = _dot(banded[u][3][kv_of(m)], probs.pop(step - lag_c))
                attn_ref[u * BLOCK:(u + 1) * BLOCK, m * LANES:(m + 1) * LANES] = _pair_cols(out_t)
        attn = attn_ref[...]
        sz, dsz = _silu(z2_ref[...])
        o = (attn * sz).astype(BF16)
        o_ref[...] = o

        w = w_ref[...]
        yb = _dot(o, w)
        ybn, r = _rms(yb)
        g = g_ref[...]
        diff = h1_ref[...] + ybn * g - tgt_ref[...]
        dh2 = diff * (1.0 / d)
        dh2_ref[...] = dh2
        acc_ref[0:1, :] += jnp.sum(dh2 * ybn, axis=0, keepdims=True)
        tok = jnp.mean(diff * diff, axis=-1, keepdims=True)
        acc_ref[1:2, :] += 0.5 * jnp.sum(tok, axis=0, keepdims=True)
        dyb = _rms_bwd(dh2 * g, ybn, r).astype(BF16)
        dyb_ref[...] = dyb
        do = _dot_nt(dyb, w)
        dattn_ref[...] = (do * sz).astype(BF16)
        dz2_ref[...] = (do * attn * dsz).astype(BF16)

    blk = lambda w: pl.BlockSpec((tile, w), lambda i: (i, 0))
    return pl.pallas_call(
        body,
        name="layer_b_fwd",
        grid=(seq // tile,),
        in_specs=[
            pl.BlockSpec(memory_space=pltpu.SMEM),
            blk(d),
            blk(d),
            _full(kvn.shape),
            _full(bpre.shape),
            _full(wkv.shape),
            _full(wbin_g.shape),
            _full(biasm.shape),
            _full(wbout.shape),
            _full(bpost.shape),
        ],
        out_specs=[blk(d), blk(d), blk(kvw), blk(aw), blk(aw), blk(d), blk(d), blk(aw), blk(aw), _resident((8, d))],
        out_shape=[
            jax.ShapeDtypeStruct((seq, d), BF16),
            jax.ShapeDtypeStruct((seq, d), BF16),
            jax.ShapeDtypeStruct((seq, kvw), BF16),
            jax.ShapeDtypeStruct((seq, aw), BF16),
            jax.ShapeDtypeStruct((seq, aw), BF16),
            jax.ShapeDtypeStruct((seq, d), F32),
            jax.ShapeDtypeStruct((seq, d), BF16),
            jax.ShapeDtypeStruct((seq, aw), BF16),
            jax.ShapeDtypeStruct((seq, aw), BF16),
            jax.ShapeDtypeStruct((8, d), F32),
        ],
        scratch_shapes=[pltpu.VMEM((tile, aw), F32), pltpu.VMEM((tile, aw), F32), pltpu.VMEM((BLOCK, kvw), BF16)],
        compiler_params=_params(("arbitrary",)),
    )(sinks, h1, target, kvn, bpre, wkv, wbin_g, biasm, wbout, bpost)


def _attn_bwd(q, kv, dattn, biasm, sinks, ready):
    seq, aw = q.shape
    kvw = kv.shape[1]
    kw = N_KV_HEADS * HEAD_DIM
    nb = seq // BLOCK
    pairs_per_kv = N_PAIRS // N_KV_HEADS
    nr = len(ready)

    tile = Q_BLOCKS * BLOCK
    nsteps = seq // tile
    held = (Q_BLOCKS - 1) * BLOCK

    def body(sink_ref, q_ref, kvc_ref, kvp_ref, da_ref, bias_ref, *refs):
        ready_refs, (dq_ref, dkv_ref, dssum_ref, dsink_ref) = refs[:nr], refs[nr:nr + 4]
        landed_refs, scratch = refs[nr + 4:2 * nr + 4], refs[2 * nr + 4:]
        carry_ref, done_ref, qs_ref, dos_ref, dst_ref, pt_ref, *sems = scratch
        i = pl.program_id(0)

        @pl.when(i == 0)
        def _():
            dssum_ref[...] = jnp.zeros_like(dssum_ref)
            dsink_ref[...] = jnp.zeros_like(dsink_ref)
            carry_ref[...] = jnp.zeros_like(carry_ref)
            done_ref[...] = jnp.zeros_like(done_ref)
            if nr:
                _exchange_start(ready_refs, landed_refs, *sems, True)

        if nr:
            @pl.when(i == nsteps)
            def _():
                _exchange_wait(ready_refs, landed_refs, *sems, True)

        @pl.when(i < nsteps)
        def _():
            lo = lax.broadcasted_iota(jnp.int32, (BAND, LANES), 1) < HEAD_DIM
            head_lane = lax.broadcasted_iota(jnp.int32, (1, LANES), 1)
            banded = _banded_tiles(kvp_ref, kvc_ref)
            units = [(u, m) for u in range(Q_BLOCKS) for m in range(N_PAIRS)]
            dsink = jnp.zeros((1, LANES), F32)
            folded = {}
            logits, dps, dsbs = {}, {}, {}
            lag_b, lag_c = ATTN_BWD_LAGS
            for step in range(len(units) + lag_c):
                if step < len(units):
                    u, m = units[step]
                    kh, rows = m // pairs_per_kv, slice((m % pairs_per_kv) * BAND, (m % pairs_per_kv + 1) * BAND)
                    qrows = slice(u * BLOCK, (u + 1) * BLOCK)
                    qpair = _pair_rows(q_ref, qrows, m, SCALE)
                    dopair = _pair_rows(da_ref, qrows, m)
                    qs_ref[u, kh, rows, :] = qpair
                    dos_ref[u, kh, rows, :] = dopair
                    logits[step] = _dot_nt(banded[u][0][kh], qpair) + _bias_of(bias_ref, i, u, m)
                    dps[step] = _dot_nt(banded[u][2][kh], dopair)
                if 0 <= step - lag_b < len(units):
                    u, m = units[step - lag_b]
                    kh, rows = m // pairs_per_kv, slice((m % pairs_per_kv) * BAND, (m % pairs_per_kv + 1) * BAND)
                    pn, sink_p = _softmax_t(logits.pop(step - lag_b), _sink_row(sink_ref, m))
                    dp = dps.pop(step - lag_b)
                    delta = jnp.sum(pn * dp, axis=0, keepdims=True)
                    ds = pn * (dp - delta)
                    dssum_ref[m] += ds
                    sink_term = sink_p * delta
                    for e in range(2):
                        total = jnp.sum(sink_term[:, e * BLOCK:(e + 1) * BLOCK], axis=1, keepdims=True)
                        dsink = dsink - jnp.where(head_lane == 2 * m + e, total, 0.0)
                    dsbs[step - lag_b] = ds.astype(BF16)
                    dst_ref[u, kh, :, rows] = dsbs[step - lag_b]
                    pt_ref[u, kh, :, rows] = pn.astype(BF16)
                if 0 <= step - lag_c < len(units):
                    u, m = units[step - lag_c]
                    kh = m // pairs_per_kv
                    dq_t = _dot(banded[u][1][kh], dsbs.pop(step - lag_c))
                    dq_ref[u * BLOCK:(u + 1) * BLOCK, m * LANES:(m + 1) * LANES] = (_pair_cols(dq_t) * SCALE).astype(BF16)
                    if m % pairs_per_kv == pairs_per_kv - 1:
                        for name, lhs_ref, rhs_ref in (("k", dst_ref, qs_ref), ("v", pt_ref, dos_ref)):
                            acc = _dot(lhs_ref[u, kh], rhs_ref[u, kh])
                            folded[u, kh, name] = acc + pltpu.roll(acc, HEAD_DIM, 1)
            dsink_ref[0:1, :] += dsink
            dkv = [jnp.concatenate([jnp.where(lo, folded[u, 0, n], folded[u, 1, n]) for n in ("k", "v")], axis=1)
                   for u in range(Q_BLOCKS)]

            @pl.when(i > 0)
            def _():
                if held:
                    dkv_ref[:held, :] = done_ref[...].astype(BF16)
                dkv_ref[held:, :] = (carry_ref[...] + dkv[0][:BLOCK]).astype(BF16)

            for u in range(Q_BLOCKS - 1):
                done_ref[u * BLOCK:(u + 1) * BLOCK, :] = dkv[u][BLOCK:] + dkv[u + 1][:BLOCK]
            carry_ref[...] = dkv[Q_BLOCKS - 1][BLOCK:]

        @pl.when(i == nsteps)
        def _():
            if held:
                dkv_ref[:held, :] = done_ref[...].astype(BF16)
            dkv_ref[held:, :] = carry_ref[...].astype(BF16)

    last = nsteps - 1
    blk = lambda w: pl.BlockSpec((tile, w), lambda i: (jnp.minimum(i, last), 0))
    outs = pl.pallas_call(
        body,
        name="attn_bwd",
        grid=(nsteps + 1,),
        in_specs=[
            pl.BlockSpec(memory_space=pltpu.SMEM),
            blk(aw),
            blk(kvw),
            pl.BlockSpec((BLOCK, kvw), lambda i: (jnp.clip(Q_BLOCKS * i - 1, 0, nb - 1), 0)),
            blk(aw),
            _full(biasm.shape),
        ] + [HBM_SPEC] * nr,
        out_specs=[
            blk(aw),
            pl.BlockSpec((tile, kvw), lambda i: (jnp.maximum(i - 1, 0), 0)),
            _resident(biasm.shape[1:]),
            _resident((8, LANES)),
        ] + [HBM_SPEC] * nr,
        out_shape=[
            jax.ShapeDtypeStruct((seq, aw), BF16),
            jax.ShapeDtypeStruct((seq, kvw), BF16),
            jax.ShapeDtypeStruct(biasm.shape[1:], F32),
            jax.ShapeDtypeStruct((8, LANES), F32),
        ] + [jax.ShapeDtypeStruct(g.shape, g.dtype) for g in ready],
        scratch_shapes=[
            pltpu.VMEM((BLOCK, kvw), F32),
            pltpu.VMEM((max(held, 8), kvw), F32),
            pltpu.VMEM((Q_BLOCKS, N_KV_HEADS, pairs_per_kv * BAND, LANES), BF16),
            pltpu.VMEM((Q_BLOCKS, N_KV_HEADS, pairs_per_kv * BAND, LANES), BF16),
            pltpu.VMEM((Q_BLOCKS, N_KV_HEADS, BAND, pairs_per_kv * BAND), BF16),
            pltpu.VMEM((Q_BLOCKS, N_KV_HEADS, BAND, pairs_per_kv * BAND), BF16),
        ] + _exchange_sems(nr),
        compiler_params=_params(("arbitrary",)),
    )(sinks, q, kv, kv, dattn, biasm, *ready)
    return outs[:4], outs[4:]


def _layer_b_in_bwd(dh2, dq, dz2, dkv, h1, ya, wbin_g, wkv, kvn, bpre, sm, ready, ts):
    seq, d = h1.shape
    aw = dq.shape[1]
    kvw = dkv.shape[1]
    cw = wbin_g.shape[2]
    per = aw // cw

    nr = len(ready)
    nt = seq // ts

    def body(dh2_ref, dq_ref, dz2_ref, dkv_ref, h1_ref, ya_ref, wbin_ref, wkv_ref, kvn_ref, bpre_ref, sm_ref, *refs):
        ready_refs, (dh1_ref, dya_ref, acc_ref) = refs[:nr], refs[nr:nr + 3]
        landed_refs, sems = refs[nr + 3:2 * nr + 3], refs[2 * nr + 3:]

        @pl.when(pl.program_id(0) == 0)
        def _():
            acc_ref[...] = jnp.zeros_like(acc_ref)
            if nr:
                _exchange_start(ready_refs, landed_refs, *sems, True)

        if nr:
            @pl.when(pl.program_id(0) == nt - 1)
            def _():
                _exchange_wait(ready_refs, landed_refs, *sems, True)

        dn4 = jnp.zeros((ts, d), F32)
        for j in range(N_DEV):
            src = dq_ref if j < per else dz2_ref
            jj = j % per
            dn4 = dn4 + _dot_nt(src[:, jj * cw:(jj + 1) * cw], wbin_ref[j])
        dn3 = _dot_nt(dkv_ref[...], wkv_ref[...])
        hn, r = _rms(h1_ref[...])
        acc_ref[0:1, :] += jnp.sum(dn4 * hn, axis=0, keepdims=True)
        acc_ref[1:2, :] += jnp.sum(dn3 * hn, axis=0, keepdims=True)
        dh1 = dh2_ref[...] + _rms_bwd(dn4 * bpre_ref[...] + dn3 * kvn_ref[...], hn, r)
        dh1_ref[...] = dh1
        yan, r2 = _rms(ya_ref[...])
        acc_ref[2:3, :] += jnp.sum(dh1 * yan, axis=0, keepdims=True)
        dya_ref[...] = _rms_bwd(dh1 * sm_ref[4:5, :], yan, r2).astype(BF16)

    outs = pl.pallas_call(
        body,
        name="layer_b_in_bwd",
        grid=(nt,),
        in_specs=[_rows(ts, d), _rows(ts, aw), _rows(ts, aw), _rows(ts, kvw), _rows(ts, d), _rows(ts, d),
                  _full(wbin_g.shape), _full(wkv.shape), _full(kvn.shape), _full(bpre.shape), _full(sm.shape)]
        + [HBM_SPEC] * nr,
        out_specs=[_rows(ts, d), _rows(ts, d), _resident((8, d))] + [HBM_SPEC] * nr,
        out_shape=[jax.ShapeDtypeStruct((seq, d), F32), jax.ShapeDtypeStruct((seq, d), BF16),
                   jax.ShapeDtypeStruct((8, d), F32)] + [jax.ShapeDtypeStruct(g.shape, g.dtype) for g in ready],
        scratch_shapes=_exchange_sems(nr),
        compiler_params=_params(("arbitrary",)),
    )(dh2, dq, dz2, dkv, h1, ya, wbin_g, wkv, kvn, bpre, sm, *ready)
    return outs[:3], outs[3:]


def _layer_a_bwd(dya, proj, conv, dh1, x2, wout, win_g, sm, ts):
    seq, d = x2.shape
    width = wout.shape[0]
    half = win_g.shape[2]
    n_half = width // half
    nt = seq // ts

    def body(dya_ref, proj_ref, conv_ref, dh1_ref, x_ref, wout_ref, win_ref, sm_ref, dproj_ref, gx_ref, acc_ref,
             dnext_ref):
        @pl.when(pl.program_id(0) == 0)
        def _():
            acc_ref[...] = jnp.zeros_like(acc_ref)
            dnext_ref[...] = jnp.zeros_like(dnext_ref)

        dy = _dot_nt(dya_ref[...], wout_ref[...])
        row = lax.broadcasted_iota(jnp.int32, (ts, half), 0)
        dn1 = jnp.zeros((ts, d), F32)
        for hh in range(n_half):
            cols = slice(hh * half, (hh + 1) * half)
            b, c, u, z = [proj_ref[:, (part * n_half + hh) * half:(part * n_half + hh + 1) * half].astype(F32)
                          for part in range(4)]
            cv = conv_ref[:, cols].astype(F32)
            dyh = dy[:, cols]
            sz, dsz = _silu(z)
            dconv = dyh * b * sz
            grads = [dyh * cv * sz, None, None, dyh * b * cv * dsz]
            next0, next1 = dnext_ref[0:1, cols], dnext_ref[1:2, cols]
            dc1 = jnp.where(row == ts - 1, next0, pltpu.roll(dconv, ts - 1, 0))
            dc2 = jnp.where(row == ts - 1, next1, jnp.where(row == ts - 2, next0, pltpu.roll(dconv, ts - 2, 0)))
            dnext_ref[:, cols] = dconv[0:8, :]
            v = c * u
            acc_ref[1:2, cols] += jnp.sum(dc2 * v, axis=0, keepdims=True)
            acc_ref[2:3, cols] += jnp.sum(dc1 * v, axis=0, keepdims=True)
            acc_ref[3:4, cols] += jnp.sum(dconv * v, axis=0, keepdims=True)
            dv = sm_ref[3:4, cols] * dconv + sm_ref[2:3, cols] * dc1 + sm_ref[1:2, cols] * dc2
            grads[1] = dv * u
            grads[2] = dv * c
            for part in range(4):
                j = part * n_half + hh
                gj = grads[part].astype(BF16)
                dproj_ref[:, j * half:(j + 1) * half] = gj
                dn1 = dn1 + _dot_nt(gj, win_ref[j])
        xn, r = _rms(x_ref[...])
        acc_ref[0:1, :] += jnp.sum(dn1 * xn, axis=0, keepdims=True)
        gx_ref[...] = dh1_ref[...] + _rms_bwd(dn1 * sm_ref[0:1, :], xn, r)

    rev = lambda w: pl.BlockSpec((ts, w), lambda i: (nt - 1 - i, 0))
    return pl.pallas_call(
        body,
        name="layer_a_bwd",
        grid=(nt,),
        in_specs=[rev(d), rev(4 * width), rev(width), rev(d), rev(d), _full(wout.shape), _full(win_g.shape), _full(sm.shape)],
        out_specs=[rev(4 * width), rev(d), _resident((8, d))],
        out_shape=[jax.ShapeDtypeStruct((seq, 4 * width), BF16), jax.ShapeDtypeStruct((seq, d), F32),
                   jax.ShapeDtypeStruct((8, d), F32)],
        scratch_shapes=[pltpu.VMEM((8, width), F32)],
        compiler_params=_params(("arbitrary",)),
    )(dya, proj, conv, dh1, x2, wout, win_g, sm)


def _wgrad(a, bs, n_slots, ts, name, ready=(), block_cols=1024):
    nr = len(ready)
    seq, k = a.shape
    nb_in = len(bs)
    n_each = bs[0].shape[1]
    n = nb_in * n_each
    bn = min(n_each, block_cols)
    per_in = n_each // bn
    n_blocks = nb_in * per_in
    ns = seq // ts

    def b_spec(idx):
        def index(j, s):
            mine = j // per_in == idx
            row = jnp.where(mine, s, jnp.where(j // per_in > idx, ns - 1, 0))
            return (row, jnp.where(mine, j % per_in, jnp.where(j // per_in > idx, per_in - 1, 0)))
        return pl.BlockSpec((ts, bn), index)

    if n_slots:
        sw = n // n_slots
        spb = bn // sw
        out_shape = jax.ShapeDtypeStruct((n_slots, k, sw), BF16)
        out_spec = pl.BlockSpec((spb, k, sw), lambda j, s: (j, 0, 0))
    else:
        out_shape = jax.ShapeDtypeStruct((k, n), BF16)
        out_spec = pl.BlockSpec((k, bn), lambda j, s: (0, j))

    def body(a_ref, *refs):
        b_refs, ready_refs, o_ref = refs[:nb_in], refs[nb_in:nb_in + nr], refs[nb_in + nr]
        landed_refs, (acc_ref, *sems) = refs[nb_in + nr + 1:nb_in + 2 * nr + 1], refs[nb_in + 2 * nr + 1:]
        j, s = pl.program_id(0), pl.program_id(1)

        if nr:
            @pl.when(jnp.logical_and(j == 0, s == 0))
            def _():
                _exchange_start(ready_refs, landed_refs, *sems, True)

            @pl.when(jnp.logical_and(j == n_blocks - 1, s == ns - 1))
            def _():
                _exchange_wait(ready_refs, landed_refs, *sems, True)

        @pl.when(s == 0)
        def _():
            acc_ref[...] = jnp.zeros_like(acc_ref)

        for idx in range(nb_in):
            @pl.when(j // per_in == idx)
            def _(idx=idx):
                acc_ref[...] += _dot_tn(a_ref[...], b_refs[idx][...])

        @pl.when(s == ns - 1)
        def _():
            if n_slots:
                for e in range(spb):
                    o_ref[e] = acc_ref[:, e * sw:(e + 1) * sw].astype(BF16)
            else:
                o_ref[...] = acc_ref[...].astype(BF16)

    outs = pl.pallas_call(
        body,
        name=name,
        grid=(n_blocks, ns),
        in_specs=[pl.BlockSpec((ts, k), lambda j, s: (s, 0))] + [b_spec(idx) for idx in range(nb_in)] + [HBM_SPEC] * nr,
        out_specs=[out_spec] + [HBM_SPEC] * nr,
        out_shape=[out_shape] + [jax.ShapeDtypeStruct(g.shape, g.dtype) for g in ready],
        scratch_shapes=[pltpu.VMEM((k, bn), F32)] + (_exchange_sems(nr) if nr else []),
        compiler_params=_params(("arbitrary", "arbitrary")),
    )(a, *bs, *ready)
    return (outs[0], outs[1:]) if nr else outs[0]


def _wgrad_tail(pairs, part, landed, ts):
    n_tasks = len(pairs)
    assert n_tasks == 2
    nl = len(landed)
    seq, k = pairs[0][0].shape
    n = pairs[0][1].shape[1]
    ns = seq // ts
    total = n_tasks * ns
    per = k // N_DEV
    n_red = len(_chip_reduce_scratch((per, n)))

    def spec(t, width):
        return pl.BlockSpec((ts, width), lambda j, s: (jnp.where(j == t, s, jnp.where(j > t, ns - 1, 0)), 0))

    def body(*refs):
        ab_refs, part_hbm = refs[:2 * n_tasks], refs[2 * n_tasks]
        landed_hbm, refs = refs[2 * n_tasks + 1:2 * n_tasks + 1 + nl], refs[2 * n_tasks + 1 + nl:]
        o_ref, red_ref, early_ref = refs[:3]
        summed_refs, (acc_ref, first_ref, part_ref, *scratch) = refs[3:3 + nl], refs[3 + nl:]
        landed_refs, load_sems, scratch = scratch[:nl], scratch[nl], scratch[nl + 1:]
        j, s = pl.program_id(0), pl.program_id(1)
        flat = j * ns + s
        swap, send, forward, finish = _chip_reduce(part_ref, red_ref, *scratch[:3], scratch[3:n_red], part_hbm)
        swap_first, send_first, forward_first, finish_first = _chip_reduce(
            first_ref, early_ref, *scratch[n_red:n_red + 3], scratch[n_red + 3:])
        loads = [pltpu.make_async_copy(src, dst, load_sems.at[i])
                 for i, (src, dst) in enumerate(zip([part_hbm, *landed_hbm], [part_ref, *landed_refs]))]

        @pl.when(flat == 0)
        def _():
            swap()
            for load in loads:
                load.start()

        @pl.when(flat == min(2, total - 1))
        def _():
            loads[0].wait()
            send()

        @pl.when(flat == min(total // 2 + 1, total - 1))
        def _():
            forward()
            for t in range(nl):
                loads[1 + t].wait()
                _sum_slots(landed_refs[t], summed_refs[t])

        @pl.when(flat == min(ns + 1, total - 1))
        def _():
            send_first()

        @pl.when(flat == min(ns + ns // 2 + 1, total - 1))
        def _():
            forward_first()

        @pl.when(s == 0)
        def _():
            acc_ref[...] = jnp.zeros_like(acc_ref)

        for t in range(n_tasks):
            @pl.when(j == t)
            def _(t=t):
                acc_ref[...] += _dot_tn(ab_refs[2 * t][...], ab_refs[2 * t + 1][...])

        @pl.when(flat == ns - 1)
        def _():
            for dev in range(N_DEV):
                first_ref[dev] = acc_ref[dev * per:(dev + 1) * per, :].astype(BF16)
            swap_first()

        @pl.when(flat == total - 1)
        def _():
            for dev in range(N_DEV):
                o_ref[dev] = acc_ref[dev * per:(dev + 1) * per, :].astype(BF16)
            finish()
            finish_first()

    slot = part.shape[1:]
    outs = pl.pallas_call(
        body,
        name="wgrad_tail",
        grid=(n_tasks, ns),
        in_specs=[spec(t, w) for t in range(n_tasks) for w in (k, n)] + [HBM_SPEC] * (1 + nl),
        out_specs=[_resident((N_DEV, per, n)), _resident(slot), _resident((per, n))]
        + [_resident(g.shape[1:]) for g in landed],
        out_shape=[jax.ShapeDtypeStruct((N_DEV, per, n), BF16), jax.ShapeDtypeStruct(slot, F32),
                   jax.ShapeDtypeStruct((per, n), F32)]
        + [jax.ShapeDtypeStruct(g.shape[1:], F32) for g in landed],
        scratch_shapes=[pltpu.VMEM((k, n), F32), pltpu.VMEM((N_DEV, per, n), BF16), pltpu.VMEM(part.shape, part.dtype)]
        + [pltpu.VMEM(g.shape, g.dtype) for g in landed] + [pltpu.SemaphoreType.DMA((1 + nl,))]
        + _chip_reduce_scratch(slot) + _chip_reduce_scratch((per, n)),
        compiler_params=_params(("arbitrary", "arbitrary")),
    )(*[op for pair in pairs for op in pair], part, *landed)
    return outs[0], outs[1], outs[2], outs[3:]


MINE = "mine"
ADAMW_STEPS = 4


def _adamw(ws, sources, picks, loss_at, ms, vs):
    n, n_src = len(ws), len(sources)
    streamed = [len(w.shape) == 2 and w.shape[0] >= 128 and picks[t][1:] == (0, None)
                and sources[picks[t][0]].shape == w.shape for t, w in enumerate(ws)]
    streamed_sources = {picks[t][0] for t in range(n) if streamed[t]}

    def step(w, g, m, v):
        m = ADAM_B1 * m + (1.0 - ADAM_B1) * g
        v = ADAM_B2 * v + (1.0 - ADAM_B2) * jnp.square(g)
        m_hat = m / (1.0 - ADAM_B1 ** ADAM_STEP)
        v_hat = v / (1.0 - ADAM_B2 ** ADAM_STEP)
        return g, -ADAM_LR * (m_hat / (jnp.sqrt(v_hat) + ADAM_EPS) + ADAM_WD * w), m, v

    def body(*refs):
        refs = list(refs)
        take = lambda k: [refs.pop(0) for _ in range(k)]
        w_refs, s_refs, m_refs, v_refs = take(n), take(n_src), take(n), take(n)
        (loss_ref,), go_refs, d_refs, nm_refs, nv_refs = take(1), take(n), take(n), take(n), take(n)
        me = _my_index()

        def grad(t, rows):
            k, first, cols = picks[t]
            if cols is None:
                return s_refs[k][rows, :]
            if cols is not MINE:
                return s_refs[k][rows, cols]
            width = w_refs[t].shape[-1]
            g = s_refs[k][rows, 0:width]
            for dev in range(1, N_DEV):
                g = jnp.where(me == dev, s_refs[k][rows, dev * width:(dev + 1) * width], g)
            return g

        def whole(t):
            first = picks[t][1]
            rows = w_refs[t].shape[0]
            if len(w_refs[t].shape) == 3:
                for j in range(rows):
                    go_refs[t][j], d_refs[t][j], nm_refs[t][j], nv_refs[t][j] = step(
                        w_refs[t][j], grad(t, slice(first + j, first + j + 1)), m_refs[t][j], v_refs[t][j])
                return
            go_refs[t][...], d_refs[t][...], nm_refs[t][...], nv_refs[t][...] = step(
                w_refs[t][...], grad(t, slice(first, first + rows)), m_refs[t][...], v_refs[t][...])

        def block(t):
            rows = w_refs[t].shape[0]
            chunk = min(rows, 128)

            def one(i, carry):
                r = pl.ds(pl.multiple_of(i * chunk, chunk), chunk)
                go_refs[t][r, :], d_refs[t][r, :], nm_refs[t][r, :], nv_refs[t][r, :] = step(
                    w_refs[t][r, :], grad(t, r), m_refs[t][r, :], v_refs[t][r, :])
                return carry

            lax.fori_loop(0, rows // chunk, one, 0)

        @pl.when(pl.program_id(0) == 0)
        def _():
            loss_ref[...] = s_refs[loss_at[0]][loss_at[1]:loss_at[1] + 1, 0:1]
            for t in range(n):
                if not streamed[t]:
                    whole(t)

        for t in range(n):
            if streamed[t]:
                block(t)

    def rows_of(shape):
        return pl.BlockSpec((shape[0] // ADAMW_STEPS, shape[1]), lambda i: (i, 0))

    w_in = [rows_of(w.shape) if streamed[t] else _full(w.shape) for t, w in enumerate(ws)]
    w_out = [rows_of(w.shape) if streamed[t] else _resident(w.shape) for t, w in enumerate(ws)]
    s_in = [rows_of(s.shape) if k in streamed_sources else _full(s.shape) for k, s in enumerate(sources)]
    outs = pl.pallas_call(
        body,
        name="adamw",
        grid=(ADAMW_STEPS,),
        in_specs=w_in + s_in + w_in * 2,
        out_specs=[_resident((1, 1))] + w_out * 4,
        out_shape=[jax.ShapeDtypeStruct((1, 1), F32)] + [jax.ShapeDtypeStruct(w.shape, F32) for w in ws] * 4,
        compiler_params=_params(("arbitrary",)),
    )(*ws, *sources, *ms, *vs)
    return outs[0], outs[1:n + 1], outs[n + 1:2 * n + 1], outs[2 * n + 1:3 * n + 1], outs[3 * n + 1:]


def _band_structure():
    q_loc = np.arange(BLOCK, dtype=np.int32)[:, None]
    s_loc = np.arange(2 * BLOCK, dtype=np.int32)[None, :]
    dist = q_loc + BLOCK - s_loc
    in_window = (dist >= 0) & (dist < BLOCK)
    dd = np.maximum(dist, 0)
    max_exact = N_BUCKETS // 2
    large = max_exact + (np.log(np.maximum(dd, 1) / max_exact) / math.log(MAX_DISTANCE / max_exact)
                         * (N_BUCKETS - max_exact)).astype(np.int32)
    bucket = np.where(dd < max_exact, dd, np.minimum(large, N_BUCKETS - 1)).astype(np.int32)
    return bucket, in_window.astype(np.int32)


def kernel(x, a_pre_norm, a_w_in, a_conv_w, a_w_out, a_post_norm, kv_norm, w_kv, rel_bias, b_pre_norm, b_w_in, b_sinks, b_w_out, b_post_norm, loss_target, m_a_pre_norm, m_a_w_in, m_a_conv_w, m_a_w_out, m_a_post_norm, m_kv_norm, m_w_kv, m_rel_bias, m_b_pre_norm, m_b_w_in, m_b_sinks, m_b_w_out, m_b_post_norm, v_a_pre_norm, v_a_w_in, v_a_conv_w, v_a_w_out, v_a_post_norm, v_kv_norm, v_w_kv, v_rel_bias, v_b_pre_norm, v_b_w_in, v_b_sinks, v_b_w_out, v_b_post_norm):
    seq, d = x.shape[1], x.shape[2]
    x2 = x.reshape(seq, d)
    target = loss_target.reshape(seq, d)
    shard = a_pre_norm.shape[1]
    ts_a = min(seq, 512)
    ts = min(seq, 512)
    ts_w = min(seq, 2048)

    taps = lambda a: a.transpose(1, 0, 2)
    bucket, in_window = _band_structure()
    (win_g, wout_g), small_g, later, biasm = _all_gather(
        [a_w_in[0], a_w_out[0]], [(0, a_pre_norm), (1, taps(a_conv_w)), (4, a_post_norm)],
        [w_kv, b_w_in[0], b_w_out[0]], rel_bias.T, bucket.T, in_window.T)
    wout = wout_g.reshape(-1, wout_g.shape[2])
    sm = small_g.transpose(1, 0, 2).reshape(8, N_DEV * shard)
    kvn = kv_norm.reshape(1, d)

    (h1, n1, proj, conv, y, ya), (wkv_g, wbin_g, wbout_g) = _layer_a_fwd(x2, sm, win_g, wout, later, ts_a)
    wkv = wkv_g.reshape(-1, wkv_g.shape[2])
    wbout = wbout_g.reshape(-1, wbout_g.shape[2])
    n3, n4, kv, q, o, dh2, dyb, dattn, dz2, acc_c = _layer_b_fwd(
        h1, target, kvn, b_pre_norm, wkv, wbin_g, biasm, b_sinks, wbout, b_post_norm)

    (dq, dkv, dssum, dsink), _ = _attn_bwd(q, kv, dattn, biasm, b_sinks, [])
    by_head = dssum.reshape(N_PAIRS, BAND, 2, BLOCK).transpose(0, 2, 3, 1)
    g_wkv = _wgrad(n3, [dkv], 0, ts_w, "wgrad_kv").reshape(wkv_g.shape)
    g_wbin = _wgrad(n4, [dq, dz2], N_DEV, ts_w, "wgrad_b_in")
    (dh1, dya, acc_b), _ = _layer_b_in_bwd(dh2, dq, dz2, dkv, h1, ya, wbin_g, wkv, kvn, b_pre_norm, sm, [], ts)
    dproj, gx, acc_a = _layer_a_bwd(dya, proj, conv, dh1, x2, wout, win_g, sm, ts_a)
    g_win, (l_wkv, l_wbin) = _wgrad(
        n1, [dproj], N_DEV, ts_w, "wgrad_a_in", ready=[g_wkv, g_wbin], block_cols=2048)
    g_wbout, r_win, r_wout, (r_wkv, r_wbin) = _wgrad_tail(
        [(y, dya), (o, dyb)], g_win, [l_wkv, l_wbin], min(seq, 1024))

    r_wbout, _, (s_a, s_b, s_c, s_sink), s_relb = _reduce_exchange(
        g_wbout, [], [acc_a, acc_b, acc_c, dsink], by_head.reshape(N_Q_HEADS, -1), bucket.reshape(1, -1), 4096)
    weights = [a_pre_norm, a_w_in[0], taps(a_conv_w), a_w_out[0], a_post_norm, kvn, w_kv, rel_bias.T, b_pre_norm,
               b_w_in[0], b_sinks, b_w_out[0], b_post_norm]
    sources = [s_a, s_b, s_c, s_relb, s_sink, r_win, r_wout, r_wkv, r_wbin, r_wbout]
    picks = [(0, 0, MINE), (5, 0, None), (0, 1, MINE), (6, 0, None), (1, 2, MINE), (1, 1, None), (7, 0, None),
             (3, 0, slice(0, N_BUCKETS)), (1, 0, None), (8, 0, None), (4, 0, slice(0, N_Q_HEADS)),
             (9, 0, None), (2, 0, None)]
    first = [m_a_pre_norm, m_a_w_in[0], taps(m_a_conv_w), m_a_w_out[0], m_a_post_norm, m_kv_norm.reshape(1, d),
             m_w_kv, m_rel_bias.T, m_b_pre_norm, m_b_w_in[0], m_b_sinks, m_b_w_out[0], m_b_post_norm]
    second = [v_a_pre_norm, v_a_w_in[0], taps(v_a_conv_w), v_a_w_out[0], v_a_post_norm, v_kv_norm.reshape(1, d),
              v_w_kv, v_rel_bias.T, v_b_pre_norm, v_b_w_in[0], v_b_sinks, v_b_w_out[0], v_b_post_norm]
    loss, grads, deltas, new_m, new_v = _adamw(weights, sources, picks, (2, 1), first, second)

    shapes = [a_pre_norm.shape, a_w_in.shape, taps, a_w_out.shape, a_post_norm.shape, kv_norm.shape,
              w_kv.shape, jnp.transpose, b_pre_norm.shape, b_w_in.shape, b_sinks.shape, b_w_out.shape, b_post_norm.shape]
    shaped = lambda arrays: [s(a) if callable(s) else a.reshape(s) for a, s in zip(arrays, shapes)]
    return (loss.reshape(()), gx.reshape(x.shape), *shaped(grads), *shaped(deltas), *shaped(new_m), *shaped(new_v))
```

```python
import math

import jax
import jax.numpy as jnp
import numpy as np
from jax import lax
from jax.experimental import pallas as pl
from jax.experimental.pallas import tpu as pltpu

HEAD_DIM = 64
N_Q_HEADS = 16
N_KV_HEADS = 2
GROUP = N_Q_HEADS // N_KV_HEADS
BLOCK = 128
N_BUCKETS = 32
MAX_DISTANCE = 128
EPS = 1e-6
NEG_INF = -1e30
SCALE = HEAD_DIM ** -0.5

ADAM_LR = 0.001
ADAM_B1 = 0.9
ADAM_B2 = 0.999
ADAM_EPS = 1e-08
ADAM_WD = 0.01
ADAM_STEP = 10

N_PAIRS = N_Q_HEADS // 2
BAND = 2 * BLOCK

N_DEV = 8
GATHER_PIECE_ROWS = 256
LANES = 128
F32 = jnp.float32
BF16 = jnp.bfloat16
MESH = pl.DeviceIdType.MESH
MIB = 1024 * 1024
VMEM_RESERVED_MIB = 63


def _params(semantics=None):
    return pltpu.CompilerParams(dimension_semantics=semantics, vmem_limit_bytes=VMEM_RESERVED_MIB * MIB)


def _full(shape):
    zeros = (0,) * len(shape)
    return pl.BlockSpec(shape, lambda *_: zeros, pipeline_mode=pl.Buffered(1))


def _resident(shape):
    zeros = (0,) * len(shape)
    return pl.BlockSpec(shape, lambda *_: zeros)


def _rows(ts, cols):
    return pl.BlockSpec((ts, cols), lambda i: (i, 0))


def _dot(a, b):
    return jnp.dot(a, b, preferred_element_type=F32)


def _dot_nt(a, b):
    return lax.dot_general(a, b, (((1,), (1,)), ((), ())), preferred_element_type=F32)


def _dot_tn(a, b):
    return lax.dot_general(a, b, (((0,), (0,)), ((), ())), preferred_element_type=F32)


def _rms(xf):
    r = lax.rsqrt(jnp.mean(xf * xf, axis=-1, keepdims=True) + EPS)
    return xf * r, r


def _rms_bwd(dn, xn, r):
    return r * (dn - xn * jnp.mean(dn * xn, axis=-1, keepdims=True))


def _silu(z):
    s = jax.nn.sigmoid(z)
    return z * s, s * (1.0 + z * (1.0 - s))


def _my_index():
    return 4 * lax.axis_index("x") + 2 * lax.axis_index("y") + lax.axis_index("c")


def _bias_table(rb_ref, bucket_ref, win_ref, out_ref):
    bk = jnp.where(win_ref[...] != 0, bucket_ref[...], -1)
    has_prev = lax.broadcasted_iota(jnp.int32, bk.shape, 0) >= BLOCK
    for h in range(N_Q_HEADS):
        acc = jnp.full(bk.shape, NEG_INF, F32)
        for b in range(N_BUCKETS):
            acc = jnp.where(bk == b, rb_ref[h, b], acc)
        cols = slice((h % 2) * BLOCK, (h % 2 + 1) * BLOCK)
        out_ref[1, h // 2, :, cols] = acc
        out_ref[0, h // 2, :, cols] = jnp.where(has_prev, acc, NEG_INF)


def _all_gather(shards, small_rows, casts, rel_bias_t, bucket_t, in_window_t):
    ns, nc, n = len(small_rows), len(casts), len(shards) + 1
    small_shape = (8, small_rows[0][1].shape[-1])
    shapes = [s.shape for s in shards] + [small_shape]
    pieces = [(t, r0, min(GATHER_PIECE_ROWS, shape[0] - r0))
              for t, shape in enumerate(shapes) for r0 in range(0, shape[0], GATHER_PIECE_ROWS)]

    def body(*refs):
        refs = list(refs)
        take = lambda k: [refs.pop(0) for _ in range(k)]
        ins, small_refs, cast_refs, (rb_ref, bucket_ref, win_ref) = take(n - 1), take(ns), take(nc), take(3)
        outs, cast_outs, (bias_ref, send_sems, recv_sems) = take(n), take(nc), take(3)
        x, y, c = lax.axis_index("x"), lax.axis_index("y"), lax.axis_index("c")
        me, sibling = (x, y, c), (x, y, 1 - c)
        x_nbr, y_nbr, diagonal = (1 - x, y), (x, 1 - y), (1 - x, 1 - y)
        south = c == 0
        relayed = (jnp.where(south, 1 - x, x), jnp.where(south, y, 1 - y))
        relay_to = (jnp.where(south, x, 1 - x), jnp.where(south, 1 - y, y))

        def copy(u, k, block, to):
            t, r0, nrows = pieces[u]
            rows = outs[t].at[4 * block[0] + 2 * block[1] + block[2], pl.ds(r0, nrows)]
            return pltpu.make_async_remote_copy(
                src_ref=rows, dst_ref=rows, send_sem=send_sems.at[u, k], recv_sem=recv_sems.at[u, k],
                device_id=to, device_id_type=MESH)

        mine = pl.ds(_my_index(), 1)
        for t in range(n - 1):
            outs[t][mine] = ins[t][...].astype(BF16)[None]
        outs[n - 1][mine] = jnp.zeros((1,) + small_shape, F32)
        for (row, _), ref in zip(small_rows, small_refs):
            if len(ref.shape) == 3:
                for j in range(ref.shape[0]):
                    outs[n - 1][mine, row + j:row + j + 1, :] = ref[j][None]
            else:
                outs[n - 1][mine, row:row + ref.shape[0], :] = ref[...][None]
        started = []

        def start(cp):
            cp.start()
            started.append(cp)

        units = range(len(pieces))
        for u in units:
            start(copy(u, 0, me, sibling))
            start(copy(u, 1, me, (*x_nbr, c)))
            start(copy(u, 2, me, (*y_nbr, c)))
        for u in units:
            for k, chip in ((1, x_nbr), (2, y_nbr)):
                copy(u, k, (*chip, c), me).wait_recv()
                start(copy(u, 3 + k, (*chip, c), sibling))
            start(copy(u, 3, (*relayed, c), (*relay_to, c)))
        for src, dst in zip(cast_refs, cast_outs):
            dst[...] = src[...].astype(BF16)
        _bias_table(rb_ref, bucket_ref, win_ref, bias_ref)
        for u in units:
            copy(u, 3, (*diagonal, c), me).wait_recv()
            start(copy(u, 6, (*diagonal, c), sibling))
        for u in units:
            copy(u, 0, sibling, me).wait_recv()
        for k, chip in ((4, x_nbr), (5, y_nbr), (6, diagonal)):
            for u in units:
                copy(u, k, (*chip, 1 - c), me).wait_recv()
        for cp in started:
            cp.wait_send()

    vmem = pl.BlockSpec(memory_space=pltpu.VMEM)
    outs = pl.pallas_call(
        body,
        name="gather_weights",
        out_shape=[jax.ShapeDtypeStruct((N_DEV,) + s.shape, BF16) for s in shards]
        + [jax.ShapeDtypeStruct((N_DEV,) + small_shape, F32)]
        + [jax.ShapeDtypeStruct(a.shape, BF16) for a in casts]
        + [jax.ShapeDtypeStruct((2, N_PAIRS, BAND, 2 * BLOCK), F32)],
        in_specs=[vmem] * (n - 1 + ns + nc) + [pl.BlockSpec(memory_space=pltpu.SMEM), vmem, vmem],
        out_specs=[vmem] * (n + nc + 1),
        scratch_shapes=[pltpu.SemaphoreType.DMA((len(pieces), 7)), pltpu.SemaphoreType.DMA((len(pieces), 7))],
        compiler_params=_params(),
    )(*shards, *[a for _, a in small_rows], *casts, rel_bias_t, bucket_t, in_window_t)
    return outs[:n - 1], outs[n - 1], outs[n:n + nc], outs[n + nc]


def _peer(k):
    x, y, c = lax.axis_index("x"), lax.axis_index("y"), lax.axis_index("c")
    px = 1 - x if k & 4 else x
    py = 1 - y if k & 2 else y
    pc = 1 - c if k & 1 else c
    return (px, py, pc), 4 * px + 2 * py + pc


def _exchange(srcs, dsts, send_sems, recv_sems, local_sems, scatter):
    me = _my_index()
    sends, arrivals = [], []
    for k in range(1, N_DEV):
        peer, pidx = _peer(k)
        for t, (src, dst) in enumerate(zip(srcs, dsts)):
            mine = src.at[pidx] if scatter else src
            sems = dict(send_sem=send_sems.at[t, k - 1], recv_sem=recv_sems.at[t, k - 1], device_id=peer, device_id_type=MESH)
            sends.append(pltpu.make_async_remote_copy(src_ref=mine, dst_ref=dst.at[me], **sems))
            arrivals.append(pltpu.make_async_remote_copy(src_ref=mine, dst_ref=dst.at[pidx], **sems))
    local = [pltpu.make_async_copy(src.at[me] if scatter else src, dst.at[me], local_sems.at[t])
             for t, (src, dst) in enumerate(zip(srcs, dsts))]
    return sends, arrivals, local


def _exchange_start(*args):
    sends, _, local = _exchange(*args)
    for cp in sends + local:
        cp.start()


def _exchange_wait(*args):
    sends, arrivals, local = _exchange(*args)
    for cp in arrivals:
        cp.wait_recv()
    for cp in sends:
        cp.wait_send()
    for cp in local:
        cp.wait()


def _exchange_sems(n):
    if not n:
        return []
    return [pltpu.SemaphoreType.DMA((n, N_DEV - 1)), pltpu.SemaphoreType.DMA((n, N_DEV - 1)), pltpu.SemaphoreType.DMA((n,))]


HBM_SPEC = pl.BlockSpec(memory_space=pl.ANY)


def _sum_slots(recv_ref, out_ref):
    rows = out_ref.shape[0]
    chunk = min(rows, 128)

    def add(i, carry):
        r0 = pl.multiple_of(i * chunk, chunk)
        acc = recv_ref[0, pl.ds(r0, chunk), :].astype(F32)
        for dev in range(1, N_DEV):
            acc = acc + recv_ref[dev, pl.ds(r0, chunk), :].astype(F32)
        out_ref[pl.ds(r0, chunk), :] = acc
        return carry

    lax.fori_loop(0, rows // chunk, add, 0)


N_CHIPS = N_DEV // 2


def _rows_loop(rows, fn):
    chunk = min(rows, 128)

    def step(i, carry):
        fn(pl.ds(pl.multiple_of(i * chunk, chunk), chunk))
        return carry

    lax.fori_loop(0, rows // chunk, step, 0)


def _chip_reduce(g_ref, out_ref, sib_ref, land_ref, send_ref, sems, swap_src=None):
    sib_send, sib_recv, ici_send, ici_recv = sems
    x, y, c = lax.axis_index("x"), lax.axis_index("y"), lax.axis_index("c")
    south = c == 0
    near =(jnp.where(south, 1 - x, x), jnp.where(south, y, 1 - y))
    far = (jnp.where(south, x, 1 - x), jnp.where(south, 1 - y, y))
    diagonal = (1 - x, 1 - y)
    rows = out_ref.shape[0]
    direct, fold, folded = 0, 1, 2

    def to_sibling(t):
        src = g_ref if swap_src is None else swap_src
        return pltpu.make_async_remote_copy(
            src_ref=src.at[2 * t + 1 - c], dst_ref=sib_ref.at[t], send_sem=sib_send.at[t], recv_sem=sib_recv.at[t],
            device_id=(x, y, 1 - c), device_id_type=MESH)

    def ici(role, chip):
        return pltpu.make_async_remote_copy(
            src_ref=send_ref.at[role], dst_ref=land_ref.at[role], send_sem=ici_send.at[role],
            recv_sem=ici_recv.at[role], device_id=(*chip, c), device_id_type=MESH)

    def pair_sum(chip, r):
        t = 2 * chip[0] + chip[1]
        return g_ref[2 * t + c, r, :].astype(F32) + sib_ref[t, r, :].astype(F32)

    def swap():
        for t in range(N_CHIPS):
            to_sibling(t).start()

    def send():
        for t in range(N_CHIPS):
            to_sibling(t).wait_recv()
        for role, chip in ((fold, diagonal), (direct, near)):
            def fill(r, role=role, chip=chip):
                send_ref[role, r, :] = pair_sum(chip, r).astype(BF16)

            _rows_loop(rows, fill)
            ici(role, near).start()

    def forward():
        ici(fold, near).wait_recv()

        def fill(r):
            send_ref[folded, r, :] = (pair_sum(far, r) + land_ref[fold, r, :].astype(F32)).astype(BF16)

        _rows_loop(rows, fill)
        ici(folded, far).start()

    def finish():
        ici(direct, near).wait_recv()
        ici(folded, far).wait_recv()

        def total(r):
            mine = pair_sum((x, y), r)
            out_ref[r, :] = mine + land_ref[direct, r, :].astype(F32) + land_ref[folded, r, :].astype(F32)

        _rows_loop(rows, total)
        for t in range(N_CHIPS):
            to_sibling(t).wait_send()
        for role, chip in ((direct, near), (fold, near), (folded, far)):
            ici(role, chip).wait_send()

    return swap, send, forward, finish


def _chip_reduce_scratch(slot):
    return [pltpu.VMEM((N_CHIPS,) + slot, BF16), pltpu.VMEM((3,) + slot, BF16), pltpu.VMEM((3,) + slot, BF16),
            pltpu.SemaphoreType.DMA((N_CHIPS,)), pltpu.SemaphoreType.DMA((N_CHIPS,)),
            pltpu.SemaphoreType.DMA((3,)), pltpu.SemaphoreType.DMA((3,))]


def _bucket_sums(a_ref, bucket_ref, cols):
    a = a_ref[:, cols]
    hi = a.astype(BF16)
    lo = (a - hi.astype(F32)).astype(BF16)
    rows = lax.broadcasted_iota(jnp.int32, (LANES, a.shape[1]), 0)
    onehot_t = (rows == bucket_ref[:, cols]).astype(F32).astype(BF16)
    return _dot_nt(hi, onehot_t) + _dot_nt(lo, onehot_t)


def _reduce_exchange(part, landed, smalls, by_bucket, bucket_row, chunk):
    nl, ng = len(landed), len(smalls)
    n_in = 1 + nl + ng + 2
    n_out = 1 + nl + ng + 1
    heads, positions = by_bucket.shape
    chunks = [slice(c0, c0 + chunk) for c0 in range(0, positions, chunk)]

    def body(*refs):
        p_in, l_in, s_in, (a_ref, bucket_ref) = refs[0], refs[1:1 + nl], refs[1 + nl:n_in - 2], refs[n_in - 2:n_in]
        refs = refs[n_in:]
        p_out, l_out, s_out, b_out = refs[0], refs[1:1 + nl], refs[1 + nl:n_out - 1], refs[n_out - 1]
        scratch = refs[n_out:]
        s_recv, (b_recv, b_ref, sib_ref, chip_ref, send_ref), sems = scratch[:ng], scratch[ng:ng + 5], scratch[ng + 5:]
        swap, send, forward, finish = _chip_reduce(p_in, p_out, sib_ref, chip_ref, send_ref, sems[:4])
        swap()
        _exchange_start(s_in, s_recv, *sems[4:7], False)
        b_ref[...] = jnp.zeros_like(b_ref)
        for cols in chunks[:len(chunks) // 2]:
            b_ref[...] += _bucket_sums(a_ref, bucket_ref, cols)
        send()
        for t in range(nl):
            _sum_slots(l_in[t], l_out[t])
        for cols in chunks[len(chunks) // 2:]:
            b_ref[...] += _bucket_sums(a_ref, bucket_ref, cols)
        _exchange_start([b_ref], [b_recv], *sems[7:], False)
        forward()
        finish()
        _exchange_wait(s_in, s_recv, *sems[4:7], False)
        _exchange_wait([b_ref], [b_recv], *sems[7:], False)
        for recv, out in zip([*s_recv, b_recv], [*s_out, b_out]):
            acc = recv[0]
            for dev in range(1, N_DEV):
                acc = acc + recv[dev]
            out[...] = acc

    vmem = pl.BlockSpec(memory_space=pltpu.VMEM)
    slot = part.shape[1:]
    outs = pl.pallas_call(
        body,
        name="reduce_grads",
        out_shape=[jax.ShapeDtypeStruct(p.shape[1:], F32) for p in [part] + landed]
        + [jax.ShapeDtypeStruct(s.shape, F32) for s in smalls] + [jax.ShapeDtypeStruct((heads, LANES), F32)],
        in_specs=[vmem] * n_in,
        out_specs=[vmem] * n_out,
        scratch_shapes=[pltpu.VMEM((N_DEV,) + s.shape, F32) for s in smalls]
        + [pltpu.VMEM((N_DEV, heads, LANES), F32), pltpu.VMEM((heads, LANES), F32)] + _chip_reduce_scratch(slot)
        + _exchange_sems(ng) + _exchange_sems(1),
        compiler_params=_params(),
    )(part, *landed, *smalls, by_bucket, bucket_row)
    return outs[0], outs[1:1 + nl], outs[1 + nl:n_out - 1], outs[n_out - 1]


def _layer_a_fwd(x2, sm, win_g, wout, later, ts):
    seq, d = x2.shape
    width = wout.shape[0]
    half = win_g.shape[2]
    n_half = width // half
    nl = len(later)
    nt = seq // ts

    def body(x_ref, sm_ref, win_ref, wout_ref, *refs):
        shard_refs, refs = refs[:nl], refs[nl:]
        h1_ref, n1_ref, proj_ref, conv_ref, y_ref, ya_ref = refs[:6]
        gathered_refs, (vprev_ref, *sems) = refs[6:6 + nl], refs[6 + nl:]

        @pl.when(pl.program_id(0) == 0)
        def _():
            vprev_ref[...] = jnp.zeros_like(vprev_ref)
            _exchange_start(shard_refs, gathered_refs, *sems, False)

        @pl.when(pl.program_id(0) == nt - 1)
        def _():
            _exchange_wait(shard_refs, gathered_refs, *sems, False)

        xf = x_ref[...]
        xn, _ = _rms(xf)
        n1 = (xn * sm_ref[0:1, :]).astype(BF16)
        n1_ref[...] = n1
        row = lax.broadcasted_iota(jnp.int32, (ts, half), 0)
        ya = jnp.zeros((ts, d), F32)
        for hh in range(n_half):
            cols = slice(hh * half, (hh + 1) * half)
            parts = []
            for part in range(4):
                j = part * n_half + hh
                pj = _dot(n1, win_ref[j])
                proj_ref[:, j * half:(j + 1) * half] = pj.astype(BF16)
                parts.append(pj)
            b, c, u, z = parts
            v = c * u
            last1, last2 = vprev_ref[7:8, cols], vprev_ref[6:7, cols]
            v1 = jnp.where(row == 0, last1, pltpu.roll(v, 1, 0))
            v2 = jnp.where(row == 0, last2, jnp.where(row == 1, last1, pltpu.roll(v, 2, 0)))
            vprev_ref[:, cols] = v[ts - 8:ts, :]
            conv = sm_ref[1:2, cols] * v2 + sm_ref[2:3, cols] * v1 + sm_ref[3:4, cols] * v
            conv_ref[:, cols] = conv.astype(BF16)
            yh = (b * conv * _silu(z)[0]).astype(BF16)
            y_ref[:, cols] = yh
            ya = ya + _dot(yh, wout_ref[cols, :])
        ya_ref[...] = ya
        h1_ref[...] = xf + _rms(ya)[0] * sm_ref[4:5, :]

    outs = pl.pallas_call(
        body,
        name="layer_a_fwd",
        grid=(nt,),
        in_specs=[_rows(ts, d), _full(sm.shape), _full(win_g.shape), _full(wout.shape)] + [HBM_SPEC] * nl,
        out_specs=[_rows(ts, d), _rows(ts, d), _rows(ts, 4 * width), _rows(ts, width), _rows(ts, width), _rows(ts, d)]
        + [HBM_SPEC] * nl,
        out_shape=[
            jax.ShapeDtypeStruct((seq, d), F32),
            jax.ShapeDtypeStruct((seq, d), BF16),
            jax.ShapeDtypeStruct((seq, 4 * width), BF16),
            jax.ShapeDtypeStruct((seq, width), BF16),
            jax.ShapeDtypeStruct((seq, width), BF16),
            jax.ShapeDtypeStruct((seq, d), F32),
        ] + [jax.ShapeDtypeStruct((N_DEV,) + s.shape, s.dtype) for s in later],
        scratch_shapes=[pltpu.VMEM((8, width), F32)] + _exchange_sems(nl),
        compiler_params=_params(("arbitrary",)),
    )(x2, sm, win_g, wout, *later)
    return outs[:6], outs[6:]


Q_BLOCKS = 4
ATTN_BWD_LAGS = (2, 4)
ATTN_FWD_LAGS = (2, 4)


def _banded_tiles(kvp_ref, kvc_ref):
    tile = kvc_ref[...].astype(F32)
    blocks = [kvp_ref[...].astype(F32)] + [tile[u * BLOCK:(u + 1) * BLOCK] for u in range(Q_BLOCKS)]
    return [_banded_kv(blocks[u], blocks[u + 1]) for u in range(Q_BLOCKS)]


def _bias_of(bias_ref, i, u, m):
    return bias_ref[jnp.minimum(i, 1) if u == 0 else 1, m]


def _banded_kv(kvp, kvc):
    kw = N_KV_HEADS * HEAD_DIM
    out = []
    for full in (jnp.concatenate([kvp[:, :kw], kvc[:, :kw]], axis=0), jnp.concatenate([kvp[:, kw:], kvc[:, kw:]], axis=0)):
        lo = lax.broadcasted_iota(jnp.int32, full.shape, 1) < HEAD_DIM
        rolled = pltpu.roll(full, HEAD_DIM, 1)
        x2 = [jnp.where(lo, full, rolled).astype(BF16), jnp.where(lo, rolled, full).astype(BF16)]
        ft = full.T
        x2t = [jnp.concatenate([ft[kh * HEAD_DIM:(kh + 1) * HEAD_DIM]] * 2, axis=0).astype(BF16) for kh in range(N_KV_HEADS)]
        out += [x2, x2t]
    return out


def _pair_rows(ref, rows, m, scale=None):
    both = ref[rows, m * LANES:(m + 1) * LANES].astype(F32)
    if scale is not None:
        both = both * scale
    lo = lax.broadcasted_iota(jnp.int32, both.shape, 1) < HEAD_DIM
    zero = jnp.zeros_like(both)
    return jnp.concatenate([jnp.where(lo, both, zero), jnp.where(lo, zero, both)], axis=0).astype(BF16)


def _pair_cols(res_t):
    top = lax.broadcasted_iota(jnp.int32, (LANES, BLOCK), 0) < HEAD_DIM
    return jnp.where(top, res_t[:, :BLOCK], res_t[:, BLOCK:]).T


def _sink_row(sink_ref, m):
    first = lax.broadcasted_iota(jnp.int32, (1, 2 * BLOCK), 1) < BLOCK
    return jnp.where(first, sink_ref[0, 2 * m], sink_ref[0, 2 * m + 1])


def _softmax_t(logits, sink):
    mx =jnp.maximum(jnp.max(logits, axis=0, keepdims=True), sink)
    p = jnp.exp(logits - mx)
    sink_p = jnp.exp(sink - mx)
    inv = 1.0 / (jnp.sum(p, axis=0, keepdims=True) + sink_p)
    return p * inv, sink_p * inv


def _layer_b_fwd(h1, target, kvn, bpre, wkv, wbin_g, biasm, sinks, wbout, bpost):
    seq, d = h1.shape
    kvw = wkv.shape[1]
    cw = wbin_g.shape[2]
    aw = N_Q_HEADS * HEAD_DIM
    per = aw // cw
    tile = Q_BLOCKS * BLOCK

    def body(sink_ref, h1_ref, tgt_ref, kvn_ref, bpre_ref, wkv_ref, wbin_ref, bias_ref, w_ref, g_ref,
             n3_ref, n4_ref, kvc_ref, q_ref, o_ref, dh2_ref, dyb_ref, dattn_ref, dz2_ref, acc_ref,
             attn_ref, z2_ref, kvp_ref):
        i = pl.program_id(0)

        @pl.when(i == 0)
        def _():
            acc_ref[...] = jnp.zeros_like(acc_ref)
            kvp_ref[...] = jnp.zeros_like(kvp_ref)

        hn, _ = _rms(h1_ref[...])
        n3 = (hn * kvn_ref[...]).astype(BF16)
        n4 = (hn * bpre_ref[...]).astype(BF16)
        n3_ref[...] = n3
        n4_ref[...] = n4
        kvc_ref[...] = _dot(n3, wkv_ref[...]).astype(BF16)
        for j in range(N_DEV):
            pj = _dot(n4, wbin_ref[j])
            if j < per:
                q_ref[:, j * cw:(j + 1) * cw] = pj.astype(BF16)
            else:
                z2_ref[:, (j - per) * cw:(j - per + 1) * cw] = pj

        banded = _banded_tiles(kvp_ref, kvc_ref)
        kvp_ref[...] = kvc_ref[tile - BLOCK:tile, :]
        units = [(u, m) for u in range(Q_BLOCKS) for m in range(N_PAIRS)]
        kv_of = lambda m: (2 * m) // GROUP
        logits, probs = {}, {}
        lag_b, lag_c = ATTN_FWD_LAGS
        for step in range(len(units) + lag_c):
            if step < len(units):
                u, m = units[step]
                qpair = _pair_rows(q_ref, slice(u * BLOCK, (u + 1) * BLOCK), m, SCALE)
                logits[step] = _dot_nt(banded[u][0][kv_of(m)], qpair) + _bias_of(bias_ref, i, u, m)
            if 0 <= step - lag_b < len(units):
                u, m = units[step - lag_b]
                probs[step - lag_b] = _softmax_t(logits.pop(step - lag_b), _sink_row(sink_ref, m))[0].astype(BF16)
            if 0 <= step - lag_c < len(units):
                u, m = units[step - lag_c]
                out_t = _dot(banded[u][3][kv_of(m)], probs.pop(step - lag_c))
                attn_ref[u * BLOCK:(u + 1) * BLOCK, m * LANES:(m + 1) * LANES] = _pair_cols(out_t)
        attn = attn_ref[...]
        sz, dsz = _silu(z2_ref[...])
        o = (attn * sz).astype(BF16)
        o_ref[...] = o

        w = w_ref[...]
        yb = _dot(o, w)
        ybn, r = _rms(yb)
        g = g_ref[...]
        diff = h1_ref[...] + ybn * g - tgt_ref[...]
        dh2 = diff * (1.0 / d)
        dh2_ref[...] = dh2
        acc_ref[0:1, :] += jnp.sum(dh2 * ybn, axis=0, keepdims=True)
        tok = jnp.mean(diff * diff, axis=-1, keepdims=True)
        acc_ref[1:2, :] += 0.5 * jnp.sum(tok, axis=0, keepdims=True)
        dyb = _rms_bwd(dh2 * g, ybn, r).astype(BF16)
        dyb_ref[...] = dyb
        do = _dot_nt(dyb, w)
        dattn_ref[...] = (do * sz).astype(BF16)
        dz2_ref[...] = (do * attn * dsz).astype(BF16)

    blk = lambda w: pl.BlockSpec((tile, w), lambda i: (i, 0))
    return pl.pallas_call(
        body,
        name="layer_b_fwd",
        grid=(seq // tile,),
        in_specs=[
            pl.BlockSpec(memory_space=pltpu.SMEM),
            blk(d),
            blk(d),
            _full(kvn.shape),
            _full(bpre.shape),
            _full(wkv.shape),
            _full(wbin_g.shape),
            _full(biasm.shape),
            _full(wbout.shape),
            _full(bpost.shape),
        ],
        out_specs=[blk(d), blk(d), blk(kvw), blk(aw), blk(aw), blk(d), blk(d), blk(aw), blk(aw), _resident((8, d))],
        out_shape=[
            jax.ShapeDtypeStruct((seq, d), BF16),
            jax.ShapeDtypeStruct((seq, d), BF16),
            jax.ShapeDtypeStruct((seq, kvw), BF16),
            jax.ShapeDtypeStruct((seq, aw), BF16),
            jax.ShapeDtypeStruct((seq, aw), BF16),
            jax.ShapeDtypeStruct((seq, d), F32),
            jax.ShapeDtypeStruct((seq, d), BF16),
            jax.ShapeDtypeStruct((seq, aw), BF16),
            jax.ShapeDtypeStruct((seq, aw), BF16),
            jax.ShapeDtypeStruct((8, d), F32),
        ],
        scratch_shapes=[pltpu.VMEM((tile, aw), F32), pltpu.VMEM((tile, aw), F32), pltpu.VMEM((BLOCK, kvw), BF16)],
        compiler_params=_params(("arbitrary",)),
    )(sinks, h1, target, kvn, bpre, wkv, wbin_g, biasm, wbout, bpost)


def _attn_bwd(q, kv, dattn, biasm, sinks, ready):
    seq, aw = q.shape
    kvw = kv.shape[1]
    kw = N_KV_HEADS * HEAD_DIM
    nb = seq // BLOCK
    pairs_per_kv = N_PAIRS // N_KV_HEADS
    nr = len(ready)

    tile = Q_BLOCKS * BLOCK
    nsteps = seq // tile
    held = (Q_BLOCKS - 1) * BLOCK

    def body(sink_ref, q_ref, kvc_ref, kvp_ref, da_ref, bias_ref, *refs):
        ready_refs, (dq_ref, dkv_ref, dssum_ref, dsink_ref) = refs[:nr], refs[nr:nr + 4]
        landed_refs, scratch = refs[nr + 4:2 * nr + 4], refs[2 * nr + 4:]
        carry_ref, done_ref, qs_ref, dos_ref, dst_ref, pt_ref, *sems = scratch
        i = pl.program_id(0)

        @pl.when(i == 0)
        def _():
            dssum_ref[...] = jnp.zeros_like(dssum_ref)
            dsink_ref[...] = jnp.zeros_like(dsink_ref)
            carry_ref[...] = jnp.zeros_like(carry_ref)
            done_ref[...] = jnp.zeros_like(done_ref)
            if nr:
                _exchange_start(ready_refs, landed_refs, *sems, True)

        if nr:
            @pl.when(i == nsteps)
            def _():
                _exchange_wait(ready_refs, landed_refs, *sems, True)

        @pl.when(i < nsteps)
        def _():
            lo = lax.broadcasted_iota(jnp.int32, (BAND, LANES), 1) < HEAD_DIM
            head_lane = lax.broadcasted_iota(jnp.int32, (1, LANES), 1)
            banded = _banded_tiles(kvp_ref, kvc_ref)
            units = [(u, m) for u in range(Q_BLOCKS) for m in range(N_PAIRS)]
            dsink = jnp.zeros((1, LANES), F32)
            folded = {}
            logits, dps, dsbs = {}, {}, {}
            lag_b, lag_c = ATTN_BWD_LAGS
            for step in range(len(units) + lag_c):
                if step < len(units):
                    u, m = units[step]
                    kh, rows = m // pairs_per_kv, slice((m % pairs_per_kv) * BAND, (m % pairs_per_kv + 1) * BAND)
                    qrows = slice(u * BLOCK, (u + 1) * BLOCK)
                    qpair = _pair_rows(q_ref, qrows, m, SCALE)
                    dopair = _pair_rows(da_ref, qrows, m)
                    qs_ref[u, kh, rows, :] = qpair
                    dos_ref[u, kh, rows, :] = dopair
                    logits[step] = _dot_nt(banded[u][0][kh], qpair) + _bias_of(bias_ref, i, u, m)
                    dps[step] = _dot_nt(banded[u][2][kh], dopair)
                if 0 <= step - lag_b < len(units):
                    u, m = units[step - lag_b]
                    kh, rows = m // pairs_per_kv, slice((m % pairs_per_kv) * BAND, (m % pairs_per_kv + 1) * BAND)
                    pn, sink_p = _softmax_t(logits.pop(step - lag_b), _sink_row(sink_ref, m))
                    dp = dps.pop(step - lag_b)
                    delta = jnp.sum(pn * dp, axis=0, keepdims=True)
                    ds = pn * (dp - delta)
                    dssum_ref[m] += ds
                    sink_term = sink_p * delta
                    for e in range(2):
                        total = jnp.sum(sink_term[:, e * BLOCK:(e + 1) * BLOCK], axis=1, keepdims=True)
                        dsink = dsink - jnp.where(head_lane == 2 * m + e, total, 0.0)
                    dsbs[step - lag_b] = ds.astype(BF16)
                    dst_ref[u, kh, :, rows] = dsbs[step - lag_b]
                    pt_ref[u, kh, :, rows] = pn.astype(BF16)
                if 0 <= step - lag_c < len(units):
                    u, m = units[step - lag_c]
                    kh = m // pairs_per_kv
                    dq_t = _dot(banded[u][1][kh], dsbs.pop(step - lag_c))
                    dq_ref[u * BLOCK:(u + 1) * BLOCK, m * LANES:(m + 1) * LANES] = (_pair_cols(dq_t) * SCALE).astype(BF16)
                    if m % pairs_per_kv == pairs_per_kv - 1:
                        for name, lhs_ref, rhs_ref in (("k", dst_ref, qs_ref), ("v", pt_ref, dos_ref)):
                            acc = _dot(lhs_ref[u, kh], rhs_ref[u, kh])
                            folded[u, kh, name] = acc + pltpu.roll(acc, HEAD_DIM, 1)
            dsink_ref[0:1, :] += dsink
            dkv = [jnp.concatenate([jnp.where(lo, folded[u, 0, n], folded[u, 1, n]) for n in ("k", "v")], axis=1)
                   for u in range(Q_BLOCKS)]

            @pl.when(i > 0)
            def _():
                if held:
                    dkv_ref[:held, :] = done_ref[...].astype(BF16)
                dkv_ref[held:, :] = (carry_ref[...] + dkv[0][:BLOCK]).astype(BF16)

            for u in range(Q_BLOCKS - 1):
                done_ref[u * BLOCK:(u + 1) * BLOCK, :] = dkv[u][BLOCK:] + dkv[u + 1][:BLOCK]
            carry_ref[...] = dkv[Q_BLOCKS - 1][BLOCK:]

        @pl.when(i == nsteps)
        def _():
            if held:
                dkv_ref[:held, :] = done_ref[...].astype(BF16)
            dkv_ref[held:, :] = carry_ref[...].astype(BF16)

    last = nsteps - 1
    blk = lambda w: pl.BlockSpec((tile, w), lambda i: (jnp.minimum(i, last), 0))
    outs = pl.pallas_call(
        body,
        name="attn_bwd",
        grid=(nsteps + 1,),
        in_specs=[
            pl.BlockSpec(memory_space=pltpu.SMEM),
            blk(aw),
            blk(kvw),
            pl.BlockSpec((BLOCK, kvw), lambda i: (jnp.clip(Q_BLOCKS * i - 1, 0, nb - 1), 0)),
            blk(aw),
            _full(biasm.shape),
        ] + [HBM_SPEC] * nr,
        out_specs=[
            blk(aw),
            pl.BlockSpec((tile, kvw), lambda i: (jnp.maximum(i - 1, 0), 0)),
            _resident(biasm.shape[1:]),
            _resident((8, LANES)),
        ] + [HBM_SPEC] * nr,
        out_shape=[
            jax.ShapeDtypeStruct((seq, aw), BF16),
            jax.ShapeDtypeStruct((seq, kvw), BF16),
            jax.ShapeDtypeStruct(biasm.shape[1:], F32),
            jax.ShapeDtypeStruct((8, LANES), F32),
        ] + [jax.ShapeDtypeStruct(g.shape, g.dtype) for g in ready],
        scratch_shapes=[
            pltpu.VMEM((BLOCK, kvw), F32),
            pltpu.VMEM((max(held, 8), kvw), F32),
            pltpu.VMEM((Q_BLOCKS, N_KV_HEADS, pairs_per_kv * BAND, LANES), BF16),
            pltpu.VMEM((Q_BLOCKS, N_KV_HEADS, pairs_per_kv * BAND, LANES), BF16),
            pltpu.VMEM((Q_BLOCKS, N_KV_HEADS, BAND, pairs_per_kv * BAND), BF16),
            pltpu.VMEM((Q_BLOCKS, N_KV_HEADS, BAND, pairs_per_kv * BAND), BF16),
        ] + _exchange_sems(nr),
        compiler_params=_params(("arbitrary",)),
    )(sinks, q, kv, kv, dattn, biasm, *ready)
    return outs[:4], outs[4:]


def _layer_b_in_bwd(dh2, dq, dz2, dkv, h1, ya, wbin_g, wkv, kvn, bpre, sm, ready, ts):
    seq, d = h1.shape
    aw = dq.shape[1]
    kvw = dkv.shape[1]
    cw = wbin_g.shape[2]
    per = aw // cw

    nr = len(ready)
    nt = seq // ts

    def body(dh2_ref, dq_ref, dz2_ref, dkv_ref, h1_ref, ya_ref, wbin_ref, wkv_ref, kvn_ref, bpre_ref, sm_ref, *refs):
        ready_refs, (dh1_ref, dya_ref, acc_ref) = refs[:nr], refs[nr:nr + 3]
        landed_refs, sems = refs[nr + 3:2 * nr + 3], refs[2 * nr + 3:]

        @pl.when(pl.program_id(0) == 0)
        def _():
            acc_ref[...] = jnp.zeros_like(acc_ref)
            if nr:
                _exchange_start(ready_refs, landed_refs, *sems, True)

        if nr:
            @pl.when(pl.program_id(0) == nt - 1)
            def _():
                _exchange_wait(ready_refs, landed_refs, *sems, True)

        dn4 = jnp.zeros((ts, d), F32)
        for j in range(N_DEV):
            src = dq_ref if j < per else dz2_ref
            jj = j % per
            dn4 = dn4 + _dot_nt(src[:, jj * cw:(jj + 1) * cw], wbin_ref[j])
        dn3 = _dot_nt(dkv_ref[...], wkv_ref[...])
        hn, r = _rms(h1_ref[...])
        acc_ref[0:1, :] += jnp.sum(dn4 * hn, axis=0, keepdims=True)
        acc_ref[1:2, :] += jnp.sum(dn3 * hn, axis=0, keepdims=True)
        dh1 = dh2_ref[...] + _rms_bwd(dn4 * bpre_ref[...] + dn3 * kvn_ref[...], hn, r)
        dh1_ref[...] = dh1
        yan, r2 = _rms(ya_ref[...])
        acc_ref[2:3, :] += jnp.sum(dh1 * yan, axis=0, keepdims=True)
        dya_ref[...] = _rms_bwd(dh1 * sm_ref[4:5, :], yan, r2).astype(BF16)

    outs = pl.pallas_call(
        body,
        name="layer_b_in_bwd",
        grid=(nt,),
        in_specs=[_rows(ts, d), _rows(ts, aw), _rows(ts, aw), _rows(ts, kvw), _rows(ts, d), _rows(ts, d),
                  _full(wbin_g.shape), _full(wkv.shape), _full(kvn.shape), _full(bpre.shape), _full(sm.shape)]
        + [HBM_SPEC] * nr,
        out_specs=[_rows(ts, d), _rows(ts, d), _resident((8, d))] + [HBM_SPEC] * nr,
        out_shape=[jax.ShapeDtypeStruct((seq, d), F32), jax.ShapeDtypeStruct((seq, d), BF16),
                   jax.ShapeDtypeStruct((8, d), F32)] + [jax.ShapeDtypeStruct(g.shape, g.dtype) for g in ready],
        scratch_shapes=_exchange_sems(nr),
        compiler_params=_params(("arbitrary",)),
    )(dh2, dq, dz2, dkv, h1, ya, wbin_g, wkv, kvn, bpre, sm, *ready)
    return outs[:3], outs[3:]


def _layer_a_bwd(dya, proj, conv, dh1, x2, wout, win_g, sm, ts):
    seq, d = x2.shape
    width = wout.shape[0]
    half = win_g.shape[2]
    n_half = width // half
    nt = seq // ts

    def body(dya_ref, proj_ref, conv_ref, dh1_ref, x_ref, wout_ref, win_ref, sm_ref, dproj_ref, gx_ref, acc_ref,
             dnext_ref):
        @pl.when(pl.program_id(0) == 0)
        def _():
            acc_ref[...] = jnp.zeros_like(acc_ref)
            dnext_ref[...] = jnp.zeros_like(dnext_ref)

        dy = _dot_nt(dya_ref[...], wout_ref[...])
        row = lax.broadcasted_iota(jnp.int32, (ts, half), 0)
        dn1 = jnp.zeros((ts, d), F32)
        for hh in range(n_half):
            cols = slice(hh * half, (hh + 1) * half)
            b, c, u, z = [proj_ref[:, (part * n_half + hh) * half:(part * n_half + hh + 1) * half].astype(F32)
                          for part in range(4)]
            cv = conv_ref[:, cols].astype(F32)
            dyh = dy[:, cols]
            sz, dsz = _silu(z)
            dconv = dyh * b * sz
            grads = [dyh * cv * sz, None, None, dyh * b * cv * dsz]
            next0, next1 = dnext_ref[0:1, cols], dnext_ref[1:2, cols]
            dc1 = jnp.where(row == ts - 1, next0, pltpu.roll(dconv, ts - 1, 0))
            dc2 = jnp.where(row == ts - 1, next1, jnp.where(row == ts - 2, next0, pltpu.roll(dconv, ts - 2, 0)))
            dnext_ref[:, cols] = dconv[0:8, :]
            v = c * u
            acc_ref[1:2, cols] += jnp.sum(dc2 * v, axis=0, keepdims=True)
            acc_ref[2:3, cols] += jnp.sum(dc1 * v, axis=0, keepdims=True)
            acc_ref[3:4, cols] += jnp.sum(dconv * v, axis=0, keepdims=True)
            dv = sm_ref[3:4, cols] * dconv + sm_ref[2:3, cols] * dc1 + sm_ref[1:2, cols] * dc2
            grads[1] = dv * u
            grads[2] = dv * c
            for part in range(4):
                j = part * n_half + hh
                gj = grads[part].astype(BF16)
                dproj_ref[:, j * half:(j + 1) * half] = gj
                dn1 = dn1 + _dot_nt(gj, win_ref[j])
        xn, r = _rms(x_ref[...])
        acc_ref[0:1, :] += jnp.sum(dn1 * xn, axis=0, keepdims=True)
        gx_ref[...] = dh1_ref[...] + _rms_bwd(dn1 * sm_ref[0:1, :], xn, r)

    rev = lambda w: pl.BlockSpec((ts, w), lambda i: (nt - 1 - i, 0))
    return pl.pallas_call(
        body,
        name="layer_a_bwd",
        grid=(nt,),
        in_specs=[rev(d), rev(4 * width), rev(width), rev(d), rev(d), _full(wout.shape), _full(win_g.shape), _full(sm.shape)],
        out_specs=[rev(4 * width), rev(d), _resident((8, d))],
        out_shape=[jax.ShapeDtypeStruct((seq, 4 * width), BF16), jax.ShapeDtypeStruct((seq, d), F32),
                   jax.ShapeDtypeStruct((8, d), F32)],
        scratch_shapes=[pltpu.VMEM((8, width), F32)],
        compiler_params=_params(("arbitrary",)),
    )(dya, proj, conv, dh1, x2, wout, win_g, sm)


def _wgrad(a, bs, n_slots, ts, name, ready=(), block_cols=1024, together=False):
    nr = len(ready)
    seq, k = a.shape
    nb_in = len(bs)
    n_each = bs[0].shape[1]
    n = nb_in * n_each
    bn = min(n_each, block_cols)
    per_in = n_each // bn
    n_blocks = 1 if together else nb_in * per_in
    width = n if together else bn
    ns = seq // ts

    def b_spec(idx):
        if together:
            return pl.BlockSpec((ts, n_each), lambda j, s: (s, 0))

        def index(j, s):
            mine = j // per_in == idx
            row = jnp.where(mine, s, jnp.where(j // per_in > idx, ns - 1, 0))
            return (row, jnp.where(mine, j % per_in, jnp.where(j // per_in > idx, per_in - 1, 0)))
        return pl.BlockSpec((ts, bn), index)

    if n_slots:
        sw = n // n_slots
        spb = width // sw
        out_shape = jax.ShapeDtypeStruct((n_slots, k, sw), BF16)
        out_spec = pl.BlockSpec((spb, k, sw), lambda j, s: (j, 0, 0))
    else:
        out_shape = jax.ShapeDtypeStruct((k, n), BF16)
        out_spec = pl.BlockSpec((k, bn), lambda j, s: (0, j))

    def body(a_ref, *refs):
        b_refs, ready_refs, o_ref = refs[:nb_in], refs[nb_in:nb_in + nr], refs[nb_in + nr]
        landed_refs, (acc_ref, *sems) = refs[nb_in + nr + 1:nb_in + 2 * nr + 1], refs[nb_in + 2 * nr + 1:]
        j, s = pl.program_id(0), pl.program_id(1)

        if nr:
            @pl.when(jnp.logical_and(j == 0, s == 0))
            def _():
                _exchange_start(ready_refs, landed_refs, *sems, True)

            @pl.when(jnp.logical_and(j == n_blocks - 1, s == ns - 1))
            def _():
                _exchange_wait(ready_refs, landed_refs, *sems, True)

        @pl.when(s == 0)
        def _():
            acc_ref[...] = jnp.zeros_like(acc_ref)

        for idx in range(nb_in):
            if together:
                acc_ref[:, idx * n_each:(idx + 1) * n_each] += _dot_tn(a_ref[...], b_refs[idx][...])
                continue

            @pl.when(j // per_in == idx)
            def _(idx=idx):
                acc_ref[...] += _dot_tn(a_ref[...], b_refs[idx][...])

        @pl.when(s == ns - 1)
        def _():
            if n_slots:
                for e in range(spb):
                    o_ref[e] = acc_ref[:, e * sw:(e + 1) * sw].astype(BF16)
            else:
                o_ref[...] = acc_ref[...].astype(BF16)

    outs = pl.pallas_call(
        body,
        name=name,
        grid=(n_blocks, ns),
        in_specs=[pl.BlockSpec((ts, k), lambda j, s: (s, 0))] + [b_spec(idx) for idx in range(nb_in)] + [HBM_SPEC] * nr,
        out_specs=[out_spec] + [HBM_SPEC] * nr,
        out_shape=[out_shape] + [jax.ShapeDtypeStruct(g.shape, g.dtype) for g in ready],
        scratch_shapes=[pltpu.VMEM((k, width), F32)] + (_exchange_sems(nr) if nr else []),
        compiler_params=_params(("arbitrary", "arbitrary")),
    )(a, *bs, *ready)
    return (outs[0], outs[1:]) if nr else outs[0]


def _wgrad_tail(pairs, part, landed, ts):
    n_tasks = len(pairs)
    assert n_tasks == 2
    nl = len(landed)
    seq, k = pairs[0][0].shape
    n = pairs[0][1].shape[1]
    ns = seq // ts
    total = n_tasks * ns
    per = k // N_DEV
    n_red = len(_chip_reduce_scratch((per, n)))

    def spec(t, width):
        return pl.BlockSpec((ts, width), lambda j, s: (jnp.where(j == t, s, jnp.where(j > t, ns - 1, 0)), 0))

    def body(*refs):
        ab_refs, part_hbm = refs[:2 * n_tasks], refs[2 * n_tasks]
        landed_hbm, refs = refs[2 * n_tasks + 1:2 * n_tasks + 1 + nl], refs[2 * n_tasks + 1 + nl:]
        o_ref, red_ref, early_ref = refs[:3]
        summed_refs, (acc_ref, first_ref, part_ref, *scratch) = refs[3:3 + nl], refs[3 + nl:]
        landed_refs, load_sems, scratch = scratch[:nl], scratch[nl], scratch[nl + 1:]
        j, s = pl.program_id(0), pl.program_id(1)
        flat = j * ns + s
        swap, send, forward, finish = _chip_reduce(part_ref, red_ref, *scratch[:3], scratch[3:n_red], part_hbm)
        swap_first, send_first, forward_first, finish_first = _chip_reduce(
            first_ref, early_ref, *scratch[n_red:n_red + 3], scratch[n_red + 3:])
        loads = [pltpu.make_async_copy(src, dst, load_sems.at[i])
                 for i, (src, dst) in enumerate(zip([part_hbm, *landed_hbm], [part_ref, *landed_refs]))]

        @pl.when(flat == 0)
        def _():
            swap()
            for load in loads:
                load.start()

        @pl.when(flat == min(2, total - 1))
        def _():
            loads[0].wait()
            send()

        @pl.when(flat == min(total // 2 + 1, total - 1))
        def _():
            forward()
            for t in range(nl):
                loads[1 + t].wait()
                _sum_slots(landed_refs[t], summed_refs[t])

        @pl.when(flat == min(ns + 1, total - 1))
        def _():
            send_first()

        @pl.when(flat == min(ns + ns // 2 + 1, total - 1))
        def _():
            forward_first()

        @pl.when(s == 0)
        def _():
            acc_ref[...] = jnp.zeros_like(acc_ref)

        for t in range(n_tasks):
            @pl.when(j == t)
            def _(t=t):
                acc_ref[...] += _dot_tn(ab_refs[2 * t][...], ab_refs[2 * t + 1][...])

        @pl.when(flat == ns - 1)
        def _():
            for dev in range(N_DEV):
                first_ref[dev] = acc_ref[dev * per:(dev + 1) * per, :].astype(BF16)
            swap_first()

        @pl.when(flat == total - 1)
        def _():
            for dev in range(N_DEV):
                o_ref[dev] = acc_ref[dev * per:(dev + 1) * per, :].astype(BF16)
            finish()
            finish_first()

    slot = part.shape[1:]
    outs = pl.pallas_call(
        body,
        name="wgrad_tail",
        grid=(n_tasks, ns),
        in_specs=[spec(t, w) for t in range(n_tasks) for w in (k, n)] + [HBM_SPEC] * (1 + nl),
        out_specs=[_resident((N_DEV, per, n)), _resident(slot), _resident((per, n))]
        + [_resident(g.shape[1:]) for g in landed],
        out_shape=[jax.ShapeDtypeStruct((N_DEV, per, n), BF16), jax.ShapeDtypeStruct(slot, F32),
                   jax.ShapeDtypeStruct((per, n), F32)]
        + [jax.ShapeDtypeStruct(g.shape[1:], F32) for g in landed],
        scratch_shapes=[pltpu.VMEM((k, n), F32), pltpu.VMEM((N_DEV, per, n), BF16), pltpu.VMEM(part.shape, part.dtype)]
        + [pltpu.VMEM(g.shape, g.dtype) for g in landed] + [pltpu.SemaphoreType.DMA((1 + nl,))]
        + _chip_reduce_scratch(slot) + _chip_reduce_scratch((per, n)),
        compiler_params=_params(("arbitrary", "arbitrary")),
    )(*[op for pair in pairs for op in pair], part, *landed)
    return outs[0], outs[1], outs[2], outs[3:]


MINE = "mine"
ADAMW_STEPS = 4


def _adamw(ws, sources, picks, loss_at, ms, vs):
    n, n_src = len(ws), len(sources)
    streamed = [len(w.shape) == 2 and w.shape[0] >= 128 and picks[t][1:] == (0, None)
                and sources[picks[t][0]].shape == w.shape for t, w in enumerate(ws)]
    streamed_sources = {picks[t][0] for t in range(n) if streamed[t]}

    def step(w, g, m, v):
        m = ADAM_B1 * m + (1.0 - ADAM_B1) * g
        v = ADAM_B2 * v + (1.0 - ADAM_B2) * jnp.square(g)
        m_hat = m / (1.0 - ADAM_B1 ** ADAM_STEP)
        v_hat = v / (1.0 - ADAM_B2 ** ADAM_STEP)
        return g, -ADAM_LR * (m_hat / (jnp.sqrt(v_hat) + ADAM_EPS) + ADAM_WD * w), m, v

    def body(*refs):
        refs = list(refs)
        take = lambda k: [refs.pop(0) for _ in range(k)]
        w_refs, s_refs, m_refs, v_refs = take(n), take(n_src), take(n), take(n)
        (loss_ref,), go_refs, d_refs, nm_refs, nv_refs = take(1), take(n), take(n), take(n), take(n)
        me = _my_index()

        def grad(t, rows):
            k, first, cols = picks[t]
            if cols is None:
                return s_refs[k][rows, :]
            if cols is not MINE:
                return s_refs[k][rows, cols]
            width = w_refs[t].shape[-1]
            g = s_refs[k][rows, 0:width]
            for dev in range(1, N_DEV):
                g = jnp.where(me == dev, s_refs[k][rows, dev * width:(dev + 1) * width], g)
            return g

        def whole(t):
            first = picks[t][1]
            rows = w_refs[t].shape[0]
            if len(w_refs[t].shape) == 3:
                for j in range(rows):
                    go_refs[t][j], d_refs[t][j], nm_refs[t][j], nv_refs[t][j] = step(
                        w_refs[t][j], grad(t, slice(first + j, first + j + 1)), m_refs[t][j], v_refs[t][j])
                return
            go_refs[t][...], d_refs[t][...], nm_refs[t][...], nv_refs[t][...] = step(
                w_refs[t][...], grad(t, slice(first, first + rows)), m_refs[t][...], v_refs[t][...])

        def block(t):
            rows = w_refs[t].shape[0]
            chunk = min(rows, 128)

            def one(i, carry):
                r = pl.ds(pl.multiple_of(i * chunk, chunk), chunk)
                go_refs[t][r, :], d_refs[t][r, :], nm_refs[t][r, :], nv_refs[t][r, :] = step(
                    w_refs[t][r, :], grad(t, r), m_refs[t][r, :], v_refs[t][r, :])
                return carry

            lax.fori_loop(0, rows // chunk, one, 0)

        @pl.when(pl.program_id(0) == 0)
        def _():
            loss_ref[...] = s_refs[loss_at[0]][loss_at[1]:loss_at[1] + 1, 0:1]
            for t in range(n):
                if not streamed[t]:
                    whole(t)

        for t in range(n):
            if streamed[t]:
                block(t)

    def rows_of(shape):
        return pl.BlockSpec((shape[0] // ADAMW_STEPS, shape[1]), lambda i: (i, 0))

    w_in = [rows_of(w.shape) if streamed[t] else _full(w.shape) for t, w in enumerate(ws)]
    w_out = [rows_of(w.shape) if streamed[t] else _resident(w.shape) for t, w in enumerate(ws)]
    s_in = [rows_of(s.shape) if k in streamed_sources else _full(s.shape) for k, s in enumerate(sources)]
    outs = pl.pallas_call(
        body,
        name="adamw",
        grid=(ADAMW_STEPS,),
        in_specs=w_in + s_in + w_in * 2,
        out_specs=[_resident((1, 1))] + w_out * 4,
        out_shape=[jax.ShapeDtypeStruct((1, 1), F32)] + [jax.ShapeDtypeStruct(w.shape, F32) for w in ws] * 4,
        compiler_params=_params(("arbitrary",)),
    )(*ws, *sources, *ms, *vs)
    return outs[0], outs[1:n + 1], outs[n + 1:2 * n + 1], outs[2 * n + 1:3 * n + 1], outs[3 * n + 1:]


def _band_structure():
    q_loc = np.arange(BLOCK, dtype=np.int32)[:, None]
    s_loc = np.arange(2 * BLOCK, dtype=np.int32)[None, :]
    dist = q_loc + BLOCK - s_loc
    in_window = (dist >= 0) & (dist < BLOCK)
    dd = np.maximum(dist, 0)
    max_exact = N_BUCKETS // 2
    large = max_exact + (np.log(np.maximum(dd, 1) / max_exact) / math.log(MAX_DISTANCE / max_exact)
                         * (N_BUCKETS - max_exact)).astype(np.int32)
    bucket = np.where(dd < max_exact, dd, np.minimum(large, N_BUCKETS - 1)).astype(np.int32)
    return bucket, in_window.astype(np.int32)


def kernel(x, a_pre_norm, a_w_in, a_conv_w, a_w_out, a_post_norm, kv_norm, w_kv, rel_bias, b_pre_norm, b_w_in, b_sinks, b_w_out, b_post_norm, loss_target, m_a_pre_norm, m_a_w_in, m_a_conv_w, m_a_w_out, m_a_post_norm, m_kv_norm, m_w_kv, m_rel_bias, m_b_pre_norm, m_b_w_in, m_b_sinks, m_b_w_out, m_b_post_norm, v_a_pre_norm, v_a_w_in, v_a_conv_w, v_a_w_out, v_a_post_norm, v_kv_norm, v_w_kv, v_rel_bias, v_b_pre_norm, v_b_w_in, v_b_sinks, v_b_w_out, v_b_post_norm):
    seq, d = x.shape[1], x.shape[2]
    x2 = x.reshape(seq, d)
    target = loss_target.reshape(seq, d)
    shard = a_pre_norm.shape[1]
    ts_a = min(seq, 512)
    ts = min(seq, 512)
    ts_w = min(seq, 2048)

    taps = lambda a: a.transpose(1, 0, 2)
    bucket, in_window = _band_structure()
    (win_g, wout_g), small_g, later, biasm = _all_gather(
        [a_w_in[0], a_w_out[0]], [(0, a_pre_norm), (1, taps(a_conv_w)), (4, a_post_norm)],
        [w_kv, b_w_in[0], b_w_out[0]], rel_bias.T, bucket.T, in_window.T)
    wout = wout_g.reshape(-1, wout_g.shape[2])
    sm = small_g.transpose(1, 0, 2).reshape(8, N_DEV * shard)
    kvn = kv_norm.reshape(1, d)

    (h1, n1, proj, conv, y, ya), (wkv_g, wbin_g, wbout_g) = _layer_a_fwd(x2, sm, win_g, wout, later, ts_a)
    wkv = wkv_g.reshape(-1, wkv_g.shape[2])
    wbout = wbout_g.reshape(-1, wbout_g.shape[2])
    n3, n4, kv, q, o, dh2, dyb, dattn, dz2, acc_c = _layer_b_fwd(
        h1, target, kvn, b_pre_norm, wkv, wbin_g, biasm, b_sinks, wbout, b_post_norm)

    (dq, dkv, dssum, dsink), _ = _attn_bwd(q, kv, dattn, biasm, b_sinks, [])
    by_head = dssum.reshape(N_PAIRS, BAND, 2, BLOCK).transpose(0, 2, 3, 1)
    g_wkv = _wgrad(n3, [dkv], 0, ts_w, "wgrad_kv").reshape(wkv_g.shape)
    g_wbin = _wgrad(n4, [dq, dz2], N_DEV, ts_w, "wgrad_b_in", together=True)
    (dh1, dya, acc_b), _ = _layer_b_in_bwd(dh2, dq, dz2, dkv, h1, ya, wbin_g, wkv, kvn, b_pre_norm, sm, [], ts)
    dproj, gx, acc_a = _layer_a_bwd(dya, proj, conv, dh1, x2, wout, win_g, sm, ts_a)
    g_win, (l_wkv, l_wbin) = _wgrad(
        n1, [dproj], N_DEV, ts_w, "wgrad_a_in", ready=[g_wkv, g_wbin], block_cols=2048)
    g_wbout, r_win, r_wout, (r_wkv, r_wbin) = _wgrad_tail(
        [(y, dya), (o, dyb)], g_win, [l_wkv, l_wbin], min(seq, 1024))

    r_wbout, _, (s_a, s_b, s_c, s_sink), s_relb = _reduce_exchange(
        g_wbout, [], [acc_a, acc_b, acc_c, dsink], by_head.reshape(N_Q_HEADS, -1), bucket.reshape(1, -1), 4096)
    weights = [a_pre_norm, a_w_in[0], taps(a_conv_w), a_w_out[0], a_post_norm, kvn, w_kv, rel_bias.T, b_pre_norm,
               b_w_in[0], b_sinks, b_w_out[0], b_post_norm]
    sources = [s_a, s_b, s_c, s_relb, s_sink, r_win, r_wout, r_wkv, r_wbin, r_wbout]
    picks = [(0, 0, MINE), (5, 0, None), (0, 1, MINE), (6, 0, None), (1, 2, MINE), (1, 1, None), (7, 0, None),
             (3, 0, slice(0, N_BUCKETS)), (1, 0, None), (8, 0, None), (4, 0, slice(0, N_Q_HEADS)),
             (9, 0, None), (2, 0, None)]
    first = [m_a_pre_norm, m_a_w_in[0], taps(m_a_conv_w), m_a_w_out[0], m_a_post_norm, m_kv_norm.reshape(1, d),
             m_w_kv, m_rel_bias.T, m_b_pre_norm, m_b_w_in[0], m_b_sinks, m_b_w_out[0], m_b_post_norm]
    second = [v_a_pre_norm, v_a_w_in[0], taps(v_a_conv_w), v_a_w_out[0], v_a_post_norm, v_kv_norm.reshape(1, d),
              v_w_kv, v_rel_bias.T, v_b_pre_norm, v_b_w_in[0], v_b_sinks, v_b_w_out[0], v_b_post_norm]
    loss, grads, deltas, new_m, new_v = _adamw(weights, sources, picks, (2, 1), first, second)

    shapes = [a_pre_norm.shape, a_w_in.shape, taps, a_w_out.shape, a_post_norm.shape, kv_norm.shape,
              w_kv.shape, jnp.transpose, b_pre_norm.shape, b_w_in.shape, b_sinks.shape, b_w_out.shape, b_post_norm.shape]
    shaped = lambda arrays: [s(a) if callable(s) else a.reshape(s) for a, s in zip(arrays, shapes)]
    return (loss.reshape(()), gx.reshape(x.shape), *shaped(grads), *shaped(deltas), *shaped(new_m), *shaped(new_v))
```
